```python
import math
import jax, jax.numpy as jnp
from jax import lax
import numpy as np

D_MODEL = 1024
BATCH = 8
SEQ = 2048
DEPTH = 2

HEAD_DIM = 64
BLK = 128
A_GROUPS = ((128, 1), (512, 4), (2048, 16))
A_HEADS = 4
B_Q_HEADS = 8
B_KV_HEADS = 2
B_WINDOW = 128
C_HEADS = 4
N_BRANCH = 3
NUM_BUCKETS = 32
MAX_DISTANCE = max(w for w, _ in A_GROUPS)
D_FF = 4 * D_MODEL
CONV_WIDTH = 3
EPS = 1e-6

A_WIDTH = A_HEADS * HEAD_DIM
B_WIDTH = B_Q_HEADS * HEAD_DIM
C_WIDTH = C_HEADS * HEAD_DIM
B_GROUP = B_Q_HEADS // B_KV_HEADS
N_A_GROUP_HEADS = len(A_GROUPS) * A_HEADS
N_BIAS_HEADS = N_A_GROUP_HEADS + B_Q_HEADS
A_QKV_COLS = 3 * N_A_GROUP_HEADS * HEAD_DIM
B_Q_COLS = B_WIDTH
B_KV_COLS = 2 * B_KV_HEADS * HEAD_DIM
C_QKV_COLS = 3 * C_WIDTH
GATE_COLS = N_BRANCH * D_MODEL
OFF_B_Q = A_QKV_COLS
OFF_B_KV = OFF_B_Q + B_Q_COLS
OFF_C = OFF_B_KV + B_KV_COLS
OFF_GATE = OFF_C + C_QKV_COLS
IN_COLS = OFF_GATE + GATE_COLS
SCALE = HEAD_DIM ** -0.5

kernel_name = 'hybrid_dilated_swa_stickbreak_convffn'


def rms_norm(x, g):
    xf = x.astype(jnp.float32)
    y = xf * lax.rsqrt(jnp.mean(xf * xf, axis=-1, keepdims=True) + EPS)
    return (y * g.astype(jnp.float32)).astype(x.dtype)


def t5_bucket(dist):
    max_exact = NUM_BUCKETS // 2
    nf = jnp.maximum(dist, 1).astype(jnp.float32)
    large = max_exact + (jnp.log(nf / max_exact) / math.log(MAX_DISTANCE / max_exact)
                         * (NUM_BUCKETS - max_exact)).astype(jnp.int32)
    large = jnp.minimum(large, NUM_BUCKETS - 1)
    return jnp.where(dist < max_exact, dist, large)


def band_layout(n_blocks, max_dist):
    a = jnp.arange(BLK)[:, None]
    b = jnp.arange(2 * BLK)[None, :]
    dist = a + BLK - b
    in_band = (dist >= 0) & (dist <= max_dist)
    key_exists = (jnp.arange(n_blocks)[:, None] > 0) | (jnp.arange(2 * BLK)[None, :] >= BLK)
    mask = in_band[None] & key_exists[:, None, :]
    return jnp.maximum(dist, 0), mask


def band_bias(table, dist):
    return jnp.transpose(table[t5_bucket(dist)], (2, 0, 1)).astype(jnp.float32)


def banded_attention(q, k, v, bias, mask):
    n, hkv, g, L, dh = q.shape
    nb = L // BLK
    qb = q.reshape(n, hkv, g, nb, BLK, dh)

    def two_blocks(t):
        tb = t.reshape(n, hkv, nb, BLK, dh)
        prev = jnp.pad(tb, ((0, 0), (0, 0), (1, 0), (0, 0), (0, 0)))[:, :, :nb]
        return jnp.concatenate([prev, tb], axis=3)

    kw, vw = two_blocks(k), two_blocks(v)
    logits = jnp.einsum('nkgbqd,nkbsd->nkgbqs', qb, kw, preferred_element_type=jnp.float32) * SCALE
    logits = jnp.where(mask, logits + bias[:, :, None], -jnp.inf)
    m = jnp.max(logits, axis=-1)
    p = jnp.exp(logits - m[..., None])
    l = jnp.sum(p, axis=-1)
    num = jnp.einsum('nkgbqs,nkbsd->nkgbqd', p, vw.astype(jnp.float32))
    return num.reshape(n, hkv, g, L, dh), m.reshape(n, hkv, g, L), l.reshape(n, hkv, g, L)


def to_sub(t, d, lp):
    b, s, h, dh = t.shape
    L = s // d
    t = t.reshape(b, L, d, h, dh).transpose(0, 2, 3, 1, 4).reshape(b * d, h, L, dh)
    return jnp.pad(t, ((0, 0), (0, 0), (0, lp - L), (0, 0)))


def dilated_attention(q, k, v, table_a):
    b, s = q.shape[:2]
    nums, ms, ls = [], [], []
    for gi, (window, d) in enumerate(A_GROUPS):
        L = s // d
        lp = -(-L // BLK) * BLK
        dist, mask = band_layout(lp // BLK, window // d)
        bias = band_bias(table_a[:, gi], dist * d)[:, None]
        num, m, l = banded_attention(to_sub(q[:, :, gi], d, lp)[:, :, None],
                                     to_sub(k[:, :, gi], d, lp), to_sub(v[:, :, gi], d, lp), bias, mask)
        num = num[:, :, 0, :L].reshape(b, d, A_HEADS, L, HEAD_DIM).transpose(0, 3, 1, 2, 4)
        nums.append(num.reshape(b, s, A_HEADS, HEAD_DIM))
        ms.append(m[:, :, 0, :L].reshape(b, d, A_HEADS, L).transpose(0, 3, 1, 2).reshape(b, s, A_HEADS))
        ls.append(l[:, :, 0, :L].reshape(b, d, A_HEADS, L).transpose(0, 3, 1, 2).reshape(b, s, A_HEADS))
    nums = jnp.stack(nums, axis=2)
    ms = jnp.stack(ms, axis=2)
    ls = jnp.stack(ls, axis=2)
    c = jnp.exp(ms - jnp.max(ms, axis=2, keepdims=True))
    out = jnp.sum(c[..., None] * nums, axis=2) / jnp.sum(c * ls, axis=2)[..., None]
    return out.reshape(b, s, A_WIDTH).astype(q.dtype)


def sliding_window_gqa(q, k, v, sinks, table_b):
    b, s = q.shape[:2]
    dist, mask = band_layout(s // BLK, B_WINDOW - 1)
    bias = band_bias(table_b, dist).reshape(B_KV_HEADS, B_GROUP, BLK, 2 * BLK)
    qh = q.reshape(b, s, B_KV_HEADS, B_GROUP, HEAD_DIM).transpose(0, 2, 3, 1, 4)
    num, m, l = banded_attention(qh, k.transpose(0, 2, 1, 3), v.transpose(0, 2, 1, 3), bias, mask)
    sink = sinks.reshape(B_KV_HEADS, B_GROUP)[None, :, :, None].astype(jnp.float32)
    mx = jnp.maximum(m, sink)
    c = jnp.exp(m - mx)
    out = num * (c / (l * c + jnp.exp(sink - mx)))[..., None]
    return out.transpose(0, 3, 1, 2, 4).reshape(b, s, B_WIDTH).astype(q.dtype)


def stick_breaking_attention(q, k, v):
    b, s = q.shape[:2]
    qh, kh, vh = (t.transpose(0, 2, 1, 3) for t in (q, k, v))
    outs = []
    for i in range(s // BLK):
        lo, hi = i * BLK, (i + 1) * BLK
        z = jnp.einsum('bhqd,bhsd->bhqs', qh[:, :, lo:hi], kh[:, :, :hi],
                       preferred_element_type=jnp.float32) * SCALE
        before = jnp.arange(hi)[None, :] < jnp.arange(lo, hi)[:, None]
        log_keep = jnp.where(before, jax.nn.log_sigmoid(-z), 0.0)
        log_rest = lax.cumsum(log_keep, axis=3, reverse=True) - log_keep
        wts = jnp.where(before, jnp.exp(jax.nn.log_sigmoid(z) + log_rest), 0.0)
        outs.append(jnp.einsum('bhqs,bhsd->bhqd', wts, vh[:, :, :hi].astype(jnp.float32)))
    out = jnp.concatenate(outs, axis=2).transpose(0, 2, 1, 3)
    return out.reshape(b, s, C_WIDTH).astype(q.dtype)


def hybrid_mixer(h, rel_bias, w_in, b_gate, sinks, w_br_a, w_br_b, w_br_c, w_out):
    b, s, _ = h.shape
    proj = h @ w_in
    a_qkv, b_q, b_kv, c_qkv, gates = jnp.split(proj, [OFF_B_Q, OFF_B_KV, OFF_C, OFF_GATE], axis=-1)
    a_qkv = a_qkv.reshape(b, s, 3, len(A_GROUPS), A_HEADS, HEAD_DIM)
    table_a = rel_bias[:, :N_A_GROUP_HEADS].reshape(NUM_BUCKETS, len(A_GROUPS), A_HEADS)
    o_a = dilated_attention(a_qkv[:, :, 0], a_qkv[:, :, 1], a_qkv[:, :, 2], table_a)
    b_kv = b_kv.reshape(b, s, 2, B_KV_HEADS, HEAD_DIM)
    o_b = sliding_window_gqa(b_q.reshape(b, s, B_Q_HEADS, HEAD_DIM), b_kv[:, :, 0], b_kv[:, :, 1],
                             sinks, rel_bias[:, N_A_GROUP_HEADS:])
    c_qkv = c_qkv.reshape(b, s, 3, C_HEADS, HEAD_DIM)
    o_c = stick_breaking_attention(c_qkv[:, :, 0], c_qkv[:, :, 1], c_qkv[:, :, 2])
    g = jax.nn.sigmoid(gates.reshape(b, s, N_BRANCH, D_MODEL) + b_gate)
    merged = g[:, :, 0] * (o_a @ w_br_a) + g[:, :, 1] * (o_b @ w_br_b) + g[:, :, 2] * (o_c @ w_br_c)
    return merged @ w_out


def conv_ffn(h, w_up, conv_w, conv_b, w_down):
    u = h @ w_up
    u = lax.conv_general_dilated(u, conv_w[:, None, :], window_strides=(1,),
                                 padding=[(CONV_WIDTH - 1, 0)],
                                 dimension_numbers=('NWC', 'WIO', 'NWC'),
                                 feature_group_count=u.shape[-1]) + conv_b
    gate, val = jnp.split(u, 2, axis=-1)
    return (jax.nn.gelu(gate, approximate=True) * val) @ w_down


def _fwd_setup_inputs(seed: int = 0) -> dict:
    key = jax.random.key(seed)
    ks = jax.random.split(key, 17)

    def nrm(k, shape, scale):
        return jax.random.normal(k, shape, jnp.float32) * scale

    def gain(k):
        return 1.0 + nrm(k, (DEPTH, D_MODEL), 0.1)

    return {
        'x': nrm(ks[0], (BATCH, SEQ, D_MODEL), 1.0),
        'rel_bias': nrm(ks[1], (NUM_BUCKETS, N_BIAS_HEADS), 0.5),
        'attn_pre_norm': gain(ks[2]),
        'w_in': nrm(ks[3], (DEPTH, D_MODEL, IN_COLS), D_MODEL ** -0.5),
        'b_gate': nrm(ks[4], (DEPTH, N_BRANCH, D_MODEL), 0.1),
        'sinks': nrm(ks[5], (DEPTH, B_Q_HEADS), 0.5),
        'w_br_a': nrm(ks[6], (DEPTH, A_WIDTH, D_MODEL), A_WIDTH ** -0.5),
        'w_br_b': nrm(ks[7], (DEPTH, B_WIDTH, D_MODEL), B_WIDTH ** -0.5),
        'w_br_c': nrm(ks[8], (DEPTH, C_WIDTH, D_MODEL), C_WIDTH ** -0.5),
        'w_out': nrm(ks[9], (DEPTH, D_MODEL, D_MODEL), D_MODEL ** -0.5),
        'attn_post_norm': gain(ks[10]),
        'ffn_pre_norm': gain(ks[11]),
        'w_up': nrm(ks[12], (DEPTH, D_MODEL, 2 * D_FF), D_MODEL ** -0.5),
        'conv_w': nrm(ks[13], (DEPTH, CONV_WIDTH, 2 * D_FF), CONV_WIDTH ** -0.5),
        'conv_b': nrm(ks[14], (DEPTH, 2 * D_FF), 0.02),
        'w_down': nrm(ks[15], (DEPTH, D_FF, D_MODEL), D_FF ** -0.5),
        'ffn_post_norm': gain(ks[16]),
    }


def _fwd_reference(x, rel_bias, attn_pre_norm, w_in, b_gate, sinks, w_br_a, w_br_b, w_br_c, w_out,
              attn_post_norm, ffn_pre_norm, w_up, conv_w, conv_b, w_down, ffn_post_norm):
    for layer in range(DEPTH):
        h = rms_norm(x, attn_pre_norm[layer])
        h = hybrid_mixer(h, rel_bias, w_in[layer], b_gate[layer], sinks[layer],
                         w_br_a[layer], w_br_b[layer], w_br_c[layer], w_out[layer])
        x = x + rms_norm(h, attn_post_norm[layer])
        h = rms_norm(x, ffn_pre_norm[layer])
        h = conv_ffn(h, w_up[layer], conv_w[layer], conv_b[layer], w_down[layer])
        x = x + rms_norm(h, ffn_post_norm[layer])
    return x


import jax as _jax
import jax.numpy as _jnp

TWIN_FORMAT = 'train_step'
FWD_PARAMS = ['x', 'rel_bias', 'attn_pre_norm', 'w_in', 'b_gate', 'sinks', 'w_br_a', 'w_br_b', 'w_br_c', 'w_out', 'attn_post_norm', 'ffn_pre_norm', 'w_up', 'conv_w', 'conv_b', 'w_down', 'ffn_post_norm']
TWIN_WEIGHTS = ['rel_bias', 'attn_pre_norm', 'w_in', 'b_gate', 'sinks', 'w_br_a', 'w_br_b', 'w_br_c', 'w_out', 'attn_post_norm', 'ffn_pre_norm', 'w_up', 'conv_w', 'conv_b', 'w_down', 'ffn_post_norm']
TWIN_DIFF_INPUT = 'x'
TWIN_INPUTS = ['x', 'rel_bias', 'attn_pre_norm', 'w_in', 'b_gate', 'sinks', 'w_br_a', 'w_br_b', 'w_br_c', 'w_out', 'attn_post_norm', 'ffn_pre_norm', 'w_up', 'conv_w', 'conv_b', 'w_down', 'ffn_post_norm', 'loss_target', 'm_rel_bias', 'm_attn_pre_norm', 'm_w_in', 'm_b_gate', 'm_sinks', 'm_w_br_a', 'm_w_br_b', 'm_w_br_c', 'm_w_out', 'm_attn_post_norm', 'm_ffn_pre_norm', 'm_w_up', 'm_conv_w', 'm_conv_b', 'm_w_down', 'm_ffn_post_norm', 'v_rel_bias', 'v_attn_pre_norm', 'v_w_in', 'v_b_gate', 'v_sinks', 'v_w_br_a', 'v_w_br_b', 'v_w_br_c', 'v_w_out', 'v_attn_post_norm', 'v_ffn_pre_norm', 'v_w_up', 'v_conv_w', 'v_conv_b', 'v_w_down', 'v_ffn_post_norm']
TWIN_OUTPUTS = ['loss', 'grad_x', 'grad_rel_bias', 'grad_attn_pre_norm', 'grad_w_in', 'grad_b_gate', 'grad_sinks', 'grad_w_br_a', 'grad_w_br_b', 'grad_w_br_c', 'grad_w_out', 'grad_attn_post_norm', 'grad_ffn_pre_norm', 'grad_w_up', 'grad_conv_w', 'grad_conv_b', 'grad_w_down', 'grad_ffn_post_norm', 'delta_rel_bias', 'delta_attn_pre_norm', 'delta_w_in', 'delta_b_gate', 'delta_sinks', 'delta_w_br_a', 'delta_w_br_b', 'delta_w_br_c', 'delta_w_out', 'delta_attn_post_norm', 'delta_ffn_pre_norm', 'delta_w_up', 'delta_conv_w', 'delta_conv_b', 'delta_w_down', 'delta_ffn_post_norm', 'new_m_rel_bias', 'new_m_attn_pre_norm', 'new_m_w_in', 'new_m_b_gate', 'new_m_sinks', 'new_m_w_br_a', 'new_m_w_br_b', 'new_m_w_br_c', 'new_m_w_out', 'new_m_attn_post_norm', 'new_m_ffn_pre_norm', 'new_m_w_up', 'new_m_conv_w', 'new_m_conv_b', 'new_m_w_down', 'new_m_ffn_post_norm', 'new_v_rel_bias', 'new_v_attn_pre_norm', 'new_v_w_in', 'new_v_b_gate', 'new_v_sinks', 'new_v_w_br_a', 'new_v_w_br_b', 'new_v_w_br_c', 'new_v_w_out', 'new_v_attn_post_norm', 'new_v_ffn_pre_norm', 'new_v_w_up', 'new_v_conv_w', 'new_v_conv_b', 'new_v_w_down', 'new_v_ffn_post_norm']
TWIN_LEAF_KINDS = {'loss': 'loss', 'grad_x': 'grad_x', 'grad_rel_bias': 'grad_w', 'grad_attn_pre_norm': 'grad_w', 'grad_w_in': 'grad_w', 'grad_b_gate': 'grad_w', 'grad_sinks': 'grad_w', 'grad_w_br_a': 'grad_w', 'grad_w_br_b': 'grad_w', 'grad_w_br_c': 'grad_w', 'grad_w_out': 'grad_w', 'grad_attn_post_norm': 'grad_w', 'grad_ffn_pre_norm': 'grad_w', 'grad_w_up': 'grad_w', 'grad_conv_w': 'grad_w', 'grad_conv_b': 'grad_w', 'grad_w_down': 'grad_w', 'grad_ffn_post_norm': 'grad_w', 'delta_rel_bias': 'delta_w', 'delta_attn_pre_norm': 'delta_w', 'delta_w_in': 'delta_w', 'delta_b_gate': 'delta_w', 'delta_sinks': 'delta_w', 'delta_w_br_a': 'delta_w', 'delta_w_br_b': 'delta_w', 'delta_w_br_c': 'delta_w', 'delta_w_out': 'delta_w', 'delta_attn_post_norm': 'delta_w', 'delta_ffn_pre_norm': 'delta_w', 'delta_w_up': 'delta_w', 'delta_conv_w': 'delta_w', 'delta_conv_b': 'delta_w', 'delta_w_down': 'delta_w', 'delta_ffn_post_norm': 'delta_w', 'new_m_rel_bias': 'new_m', 'new_m_attn_pre_norm': 'new_m', 'new_m_w_in': 'new_m', 'new_m_b_gate': 'new_m', 'new_m_sinks': 'new_m', 'new_m_w_br_a': 'new_m', 'new_m_w_br_b': 'new_m', 'new_m_w_br_c': 'new_m', 'new_m_w_out': 'new_m', 'new_m_attn_post_norm': 'new_m', 'new_m_ffn_pre_norm': 'new_m', 'new_m_w_up': 'new_m', 'new_m_conv_w': 'new_m', 'new_m_conv_b': 'new_m', 'new_m_w_down': 'new_m', 'new_m_ffn_post_norm': 'new_m', 'new_v_rel_bias': 'new_v', 'new_v_attn_pre_norm': 'new_v', 'new_v_w_in': 'new_v', 'new_v_b_gate': 'new_v', 'new_v_sinks': 'new_v', 'new_v_w_br_a': 'new_v', 'new_v_w_br_b': 'new_v', 'new_v_w_br_c': 'new_v', 'new_v_w_out': 'new_v', 'new_v_attn_post_norm': 'new_v', 'new_v_ffn_pre_norm': 'new_v', 'new_v_w_up': 'new_v', 'new_v_conv_w': 'new_v', 'new_v_conv_b': 'new_v', 'new_v_w_down': 'new_v', 'new_v_ffn_post_norm': 'new_v'}


def _forward(args):
    return _fwd_reference(*[args[k] for k in FWD_PARAMS])


def _output_shape():
    out = _jax.eval_shape(lambda: _forward(_fwd_setup_inputs(0)))
    return out.shape, out.dtype

N_MICROBATCH = 1
ADAM_LR = 0.001
ADAM_B1 = 0.9
ADAM_B2 = 0.999
ADAM_EPS = 1e-08
ADAM_WD = 0.01
ADAM_STEP = 10
PER_EXAMPLE_BATCH_AXIS = {'x': 0, 'loss_target': 0}
SHARED_INPUTS = []
_WEIGHT_DTYPES = {'rel_bias': _jnp.float32, 'attn_pre_norm': _jnp.float32, 'w_in': _jnp.float32, 'b_gate': _jnp.float32, 'sinks': _jnp.float32, 'w_br_a': _jnp.float32, 'w_br_b': _jnp.float32, 'w_br_c': _jnp.float32, 'w_out': _jnp.float32, 'attn_post_norm': _jnp.float32, 'ffn_pre_norm': _jnp.float32, 'w_up': _jnp.float32, 'conv_w': _jnp.float32, 'conv_b': _jnp.float32, 'w_down': _jnp.float32, 'ffn_post_norm': _jnp.float32}
MOMENT_SCALE = {'rel_bias': 2.947129e-01, 'attn_pre_norm': 7.768819e-01, 'w_in': 2.706531e-01, 'b_gate': 1.481802e-01, 'sinks': 2.242074e-01, 'w_br_a': 1.660472e-01, 'w_br_b': 1.895409e-01, 'w_br_c': 6.210564e-01, 'w_out': 6.397578e-01, 'attn_post_norm': 1.614083e+01, 'ffn_pre_norm': 5.842037e-01, 'w_up': 2.096110e-01, 'conv_w': 2.231845e-01, 'conv_b': 3.246133e-01, 'w_down': 4.625616e-01, 'ffn_post_norm': 1.607547e+01}


def _to_microbatches(a, axis):
    t = _jnp.moveaxis(a, axis, 0)
    t = t.reshape((N_MICROBATCH, t.shape[0] // N_MICROBATCH) + t.shape[1:])
    return _jnp.moveaxis(t, 1, axis + 1)


def setup_inputs(seed: int = 0) -> dict:
    inp = _fwd_setup_inputs(seed)
    key = _jax.random.fold_in(_jax.random.key(seed), 7919)
    shape, _ = _output_shape()
    out = dict(inp)
    out["loss_target"] = _jax.random.normal(_jax.random.fold_in(key, 0), shape, _jnp.float32)
    for i, name in enumerate(TWIN_WEIGHTS):
        w = inp[name].astype(_jnp.float32)
        if MOMENT_SCALE is None:
            s = _jnp.sqrt(_jnp.mean(_jnp.square(w)) + 1e-30)
        else:
            s = MOMENT_SCALE[name]
        km, kv = _jax.random.split(_jax.random.fold_in(key, i + 1))
        out[name] = w
        out["m_" + name] = s * _jax.random.normal(km, w.shape, _jnp.float32)
        out["v_" + name] = (s * s) * _jax.random.uniform(kv, w.shape, _jnp.float32, 0.5, 1.5)
    if N_MICROBATCH > 1:
        for name, axis in PER_EXAMPLE_BATCH_AXIS.items():
            out[name] = _to_microbatches(out[name], axis)
    return {'x': out['x'], 'rel_bias': out['rel_bias'], 'attn_pre_norm': out['attn_pre_norm'], 'w_in': out['w_in'], 'b_gate': out['b_gate'], 'sinks': out['sinks'], 'w_br_a': out['w_br_a'], 'w_br_b': out['w_br_b'], 'w_br_c': out['w_br_c'], 'w_out': out['w_out'], 'attn_post_norm': out['attn_post_norm'], 'ffn_pre_norm': out['ffn_pre_norm'], 'w_up': out['w_up'], 'conv_w': out['conv_w'], 'conv_b': out['conv_b'], 'w_down': out['w_down'], 'ffn_post_norm': out['ffn_post_norm'], 'loss_target': out['loss_target'], 'm_rel_bias': out['m_rel_bias'], 'm_attn_pre_norm': out['m_attn_pre_norm'], 'm_w_in': out['m_w_in'], 'm_b_gate': out['m_b_gate'], 'm_sinks': out['m_sinks'], 'm_w_br_a': out['m_w_br_a'], 'm_w_br_b': out['m_w_br_b'], 'm_w_br_c': out['m_w_br_c'], 'm_w_out': out['m_w_out'], 'm_attn_post_norm': out['m_attn_post_norm'], 'm_ffn_pre_norm': out['m_ffn_pre_norm'], 'm_w_up': out['m_w_up'], 'm_conv_w': out['m_conv_w'], 'm_conv_b': out['m_conv_b'], 'm_w_down': out['m_w_down'], 'm_ffn_post_norm': out['m_ffn_post_norm'], 'v_rel_bias': out['v_rel_bias'], 'v_attn_pre_norm': out['v_attn_pre_norm'], 'v_w_in': out['v_w_in'], 'v_b_gate': out['v_b_gate'], 'v_sinks': out['v_sinks'], 'v_w_br_a': out['v_w_br_a'], 'v_w_br_b': out['v_w_br_b'], 'v_w_br_c': out['v_w_br_c'], 'v_w_out': out['v_w_out'], 'v_attn_post_norm': out['v_attn_post_norm'], 'v_ffn_pre_norm': out['v_ffn_pre_norm'], 'v_w_up': out['v_w_up'], 'v_conv_w': out['v_conv_w'], 'v_conv_b': out['v_conv_b'], 'v_w_down': out['v_w_down'], 'v_ffn_post_norm': out['v_ffn_post_norm']}


def _loss(weights, diff, rest, loss_target):
    with _jax.named_scope("forward"):
        args = {**rest, TWIN_DIFF_INPUT: diff, **{k: w.astype(_WEIGHT_DTYPES[k]) for k, w in weights.items()}}
        y = _forward(args)
    with _jax.named_scope("loss_head"):
        err = _jnp.square(y.astype(_jnp.float32) - loss_target)
        return 0.5 * _jnp.sum(_jnp.mean(err, axis=-1)) if err.ndim else 0.5 * err


def _adamw(w, g, m, v):
    m = ADAM_B1 * m + (1.0 - ADAM_B1) * g
    v = ADAM_B2 * v + (1.0 - ADAM_B2) * _jnp.square(g)
    m_hat = m / (1.0 - ADAM_B1 ** ADAM_STEP)
    v_hat = v / (1.0 - ADAM_B2 ** ADAM_STEP)
    delta = -ADAM_LR * (m_hat / (_jnp.sqrt(v_hat) + ADAM_EPS) + ADAM_WD * w)
    return delta, m, v


def reference(x, rel_bias, attn_pre_norm, w_in, b_gate, sinks, w_br_a, w_br_b, w_br_c, w_out, attn_post_norm, ffn_pre_norm, w_up, conv_w, conv_b, w_down, ffn_post_norm, loss_target, m_rel_bias, m_attn_pre_norm, m_w_in, m_b_gate, m_sinks, m_w_br_a, m_w_br_b, m_w_br_c, m_w_out, m_attn_post_norm, m_ffn_pre_norm, m_w_up, m_conv_w, m_conv_b, m_w_down, m_ffn_post_norm, v_rel_bias, v_attn_pre_norm, v_w_in, v_b_gate, v_sinks, v_w_br_a, v_w_br_b, v_w_br_c, v_w_out, v_attn_post_norm, v_ffn_pre_norm, v_w_up, v_conv_w, v_conv_b, v_w_down, v_ffn_post_norm):
    given = dict(x=x, rel_bias=rel_bias, attn_pre_norm=attn_pre_norm, w_in=w_in, b_gate=b_gate, sinks=sinks, w_br_a=w_br_a, w_br_b=w_br_b, w_br_c=w_br_c, w_out=w_out, attn_post_norm=attn_post_norm, ffn_pre_norm=ffn_pre_norm, w_up=w_up, conv_w=conv_w, conv_b=conv_b, w_down=w_down, ffn_post_norm=ffn_post_norm, loss_target=loss_target, m_rel_bias=m_rel_bias, m_attn_pre_norm=m_attn_pre_norm, m_w_in=m_w_in, m_b_gate=m_b_gate, m_sinks=m_sinks, m_w_br_a=m_w_br_a, m_w_br_b=m_w_br_b, m_w_br_c=m_w_br_c, m_w_out=m_w_out, m_attn_post_norm=m_attn_post_norm, m_ffn_pre_norm=m_ffn_pre_norm, m_w_up=m_w_up, m_conv_w=m_conv_w, m_conv_b=m_conv_b, m_w_down=m_w_down, m_ffn_post_norm=m_ffn_post_norm, v_rel_bias=v_rel_bias, v_attn_pre_norm=v_attn_pre_norm, v_w_in=v_w_in, v_b_gate=v_b_gate, v_sinks=v_sinks, v_w_br_a=v_w_br_a, v_w_br_b=v_w_br_b, v_w_br_c=v_w_br_c, v_w_out=v_w_out, v_attn_post_norm=v_attn_post_norm, v_ffn_pre_norm=v_ffn_pre_norm, v_w_up=v_w_up, v_conv_w=v_conv_w, v_conv_b=v_conv_b, v_w_down=v_w_down, v_ffn_post_norm=v_ffn_post_norm)
    weights = {n: given[n] for n in TWIN_WEIGHTS}
    shared = {n: given[n] for n in SHARED_INPUTS}
    per_example = {n: given[n] for n in ['x']}
    grad_fn = _jax.value_and_grad(_loss, argnums=(0, 1))

    def one_microbatch(ex, loss_target):
        ex = dict(ex)
        diff = ex.pop(TWIN_DIFF_INPUT)
        return grad_fn(weights, diff, {**shared, **ex}, loss_target)

    if N_MICROBATCH == 1:
        loss, (grad_w, grad_x) = one_microbatch(per_example, given["loss_target"])
    else:
        def body(carry, xs):
            loss_sum, grad_sum = carry
            l_k, (gw_k, gx_k) = one_microbatch(xs[0], xs[1])
            with _jax.named_scope("update"):
                return (loss_sum + l_k, _jax.tree.map(_jnp.add, grad_sum, gw_k)), gx_k

        init = (_jnp.zeros((), _jnp.float32), _jax.tree.map(_jnp.zeros_like, weights))
        (loss, grad_w), grad_x = _jax.lax.scan(body, init, (per_example, given["loss_target"]))
    with _jax.named_scope("update"):
        delta_w, new_m, new_v = {}, {}, {}
        for n in TWIN_WEIGHTS:
            delta_w[n], new_m[n], new_v[n] = _adamw(weights[n], grad_w[n], given["m_" + n], given["v_" + n])
    return (loss, grad_x, *[grad_w[n] for n in TWIN_WEIGHTS], *[delta_w[n] for n in TWIN_WEIGHTS],
            *[new_m[n] for n in TWIN_WEIGHTS], *[new_v[n] for n in TWIN_WEIGHTS])
```

```python
import functools
import math

import numpy as np
import jax
import jax.numpy as jnp
from jax import lax
from jax.experimental import pallas as pl
from jax.experimental.pallas import tpu as pltpu

F32 = jnp.float32
BF16 = jnp.bfloat16

S = 2048
D = 1024
DEPTH = 2
HD = 64
BLK = 128
NQB = S // BLK
A_GROUPS = ((128, 1), (512, 4), (2048, 16))
N_BAND_Q = 20
N_BAND_KV = 14
N_A = 12
C_HEADS = 4
NUM_BUCKETS = 32
MAX_DISTANCE = 2048
D_FF = 4096
IN_COLS = 6912
IN_SHARD = IN_COLS // 4
OFF_GATE = 3840
EPS = 1e-6
SCALE = HD ** -0.5
NEG = -1e30
N_CHIPS = 4
N_DEV = 8

ADAM_LR = 0.001
ADAM_B1 = 0.9
ADAM_B2 = 0.999
ADAM_EPS = 1e-08
ADAM_WD = 0.01
ADAM_STEP = 10

VMEM_LIMIT = 56 * 1024 * 1024

NN = (((1,), (0,)), ((), ()))
NT = (((1,), (1,)), ((), ()))
TN = (((0,), (0,)), ((), ()))


def _dot(a, b, dims):
    return lax.dot_general(a, b, dims, preferred_element_type=F32)


def _params(sem):
    return pltpu.CompilerParams(dimension_semantics=sem, vmem_limit_bytes=VMEM_LIMIT)


def _matmul(name, a, b, out_shape, out_dtype, grid, a_spec, b_spec, o_spec, dims, acc_shape):
    nk = grid[-1]

    def body(a_ref, b_ref, o_ref, *scratch):
        part = _dot(a_ref[...].astype(BF16), b_ref[...].astype(BF16), dims)
        if nk == 1:
            o_ref[...] = part.astype(o_ref.dtype)
            return
        acc_ref, = scratch
        k = pl.program_id(len(grid) - 1)

        @pl.when(k == 0)
        def _():
            acc_ref[...] = part

        @pl.when(k > 0)
        def _():
            acc_ref[...] += part

        @pl.when(k == nk - 1)
        def _():
            o_ref[...] = acc_ref[...].astype(o_ref.dtype)

    scratch = [] if nk == 1 else [pltpu.VMEM(acc_shape, F32)]
    sem = ("parallel",) * (len(grid) - 1) + ("arbitrary",)
    return pl.pallas_call(
        body, name=name, grid=grid, in_specs=[a_spec, b_spec], out_specs=o_spec,
        out_shape=jax.ShapeDtypeStruct(out_shape, out_dtype), scratch_shapes=scratch,
        compiler_params=_params(sem))(a, b)


def _mm_nn(name, a, b, out_dtype, tm=512, tn=512, tk=1024):
    m, k = a.shape
    n = b.shape[1]
    tm, tn, tk = min(tm, m), min(tn, n), min(tk, k)
    return _matmul(name, a, b, (m, n), out_dtype, (m // tm, n // tn, k // tk),
                   pl.BlockSpec((tm, tk), lambda i, j, l: (i, l)),
                   pl.BlockSpec((tk, tn), lambda i, j, l: (l, j)),
                   pl.BlockSpec((tm, tn), lambda i, j, l: (i, j)), NN, (tm, tn))


def _mm_nt(name, a, b, out_dtype, tm=512, tn=512, tk=1024):
    m, k = a.shape
    n = b.shape[0]
    tm, tn, tk = min(tm, m), min(tn, n), min(tk, k)
    return _matmul(name, a, b, (m, n), out_dtype, (m // tm, n // tn, k // tk),
                   pl.BlockSpec((tm, tk), lambda i, j, l: (i, l)),
                   pl.BlockSpec((tn, tk), lambda i, j, l: (j, l)),
                   pl.BlockSpec((tm, tn), lambda i, j, l: (i, j)), NT, (tm, tn))


def _mm_tn(name, a, b, out_dtype, tm=512, tn=512, tk=1024):
    k, m = a.shape
    n = b.shape[1]
    tm, tn, tk = min(tm, m), min(tn, n), min(tk, k)
    return _matmul(name, a, b, (m, n), out_dtype, (m // tm, n // tn, k // tk),
                   pl.BlockSpec((tk, tm), lambda i, j, l: (l, i)),
                   pl.BlockSpec((tk, tn), lambda i, j, l: (l, j)),
                   pl.BlockSpec((tm, tn), lambda i, j, l: (i, j)), TN, (tm, tn))


def _proj_in(h, w_in_g):
    tm = 512
    return _matmul("proj_in", h, w_in_g, (N_CHIPS, S, IN_SHARD), F32, (N_CHIPS, S // tm, 1),
                   pl.BlockSpec((tm, D), lambda j, i, l: (i, 0)),
                   pl.BlockSpec((None, D, IN_SHARD), lambda j, i, l: (j, 0, 0)),
                   pl.BlockSpec((None, tm, IN_SHARD), lambda j, i, l: (j, i, 0)), NN, (tm, IN_SHARD))


def _proj_in_dx(dproj_s, w_in_g):
    tm = 512
    return _matmul("proj_in_dx", dproj_s, w_in_g, (S, D), F32, (S // tm, N_CHIPS),
                   pl.BlockSpec((None, tm, IN_SHARD), lambda i, j: (j, i, 0)),
                   pl.BlockSpec((None, D, IN_SHARD), lambda i, j: (j, 0, 0)),
                   pl.BlockSpec((tm, D), lambda i, j: (i, 0)), NT, (tm, D))


def _proj_in_dw(h, dproj_s):
    tm = 512
    return _matmul("proj_in_dw", h, dproj_s, (N_CHIPS, D, IN_SHARD), BF16, (N_CHIPS, D // tm, 1),
                   pl.BlockSpec((S, tm), lambda j, i, l: (0, i)),
                   pl.BlockSpec((None, S, IN_SHARD), lambda j, i, l: (j, 0, 0)),
                   pl.BlockSpec((None, tm, IN_SHARD), lambda j, i, l: (j, i, 0)), TN, (tm, IN_SHARD))


TR = 256


def _row_spec(width=D):
    return pl.BlockSpec((TR, width), lambda i: (i, 0))


def _vec_spec(width=D):
    return pl.BlockSpec((1, width), lambda i: (0, 0))


def _rms(x, g):
    r = lax.rsqrt(jnp.mean(x * x, axis=-1, keepdims=True) + EPS)
    return x * r * g


def _rms_fwd(name, x, g):
    def body(x_ref, g_ref, h_ref):
        h_ref[...] = _rms(x_ref[...], g_ref[...]).astype(BF16)

    return pl.pallas_call(
        body, name=name, grid=(S // TR,), in_specs=[_row_spec(), _vec_spec()], out_specs=_row_spec(),
        out_shape=jax.ShapeDtypeStruct((S, D), BF16), compiler_params=_params(("parallel",)))(x, g)


def _post_pre_fwd(name, x, y, g_post, g_pre):
    has_pre = g_pre is not None

    def body(*refs):
        if has_pre:
            x_ref, y_ref, gp_ref, gn_ref, xn_ref, h_ref = refs
        else:
            x_ref, y_ref, gp_ref, xn_ref = refs
        xn = x_ref[...] + _rms(y_ref[...], gp_ref[...])
        xn_ref[...] = xn
        if has_pre:
            h_ref[...] = _rms(xn, gn_ref[...]).astype(BF16)

    ins = [x, y, g_post] + ([g_pre] if has_pre else [])
    in_specs = [_row_spec(), _row_spec(), _vec_spec()] + ([_vec_spec()] if has_pre else [])
    out_shape = [jax.ShapeDtypeStruct((S, D), F32)] + ([jax.ShapeDtypeStruct((S, D), BF16)] if has_pre else [])
    out_specs = [_row_spec()] + ([_row_spec()] if has_pre else [])
    out = pl.pallas_call(
        body, name=name, grid=(S // TR,), in_specs=in_specs, out_specs=out_specs, out_shape=out_shape,
        compiler_params=_params(("parallel",)))(*ins)
    return out if has_pre else (out[0], None)


def _rms_bwd_math(x, g, dy):
    r = lax.rsqrt(jnp.mean(x * x, axis=-1, keepdims=True) + EPS)
    n = x * r
    dn = dy * g
    dx = r * (dn - n * jnp.mean(dn * n, axis=-1, keepdims=True))
    return dx, jnp.sum(dy * n, axis=0, keepdims=True)


def _norm_bwd(name, dres, pre=None, post=None):
    has_pre, has_post = pre is not None, post is not None

    def body(*refs):
        refs = list(refs)
        dres_ref = refs.pop(0)
        if has_pre:
            xn_ref, gn_ref, dh_ref = refs[:3]
            refs = refs[3:]
        if has_post:
            y_ref, gp_ref = refs[:2]
            refs = refs[2:]
        dxn_ref = refs.pop(0)
        dy_ref = refs.pop(0) if has_post else None
        dgn_ref = refs.pop(0) if has_pre else None
        dgp_ref = refs.pop(0) if has_post else None
        first = pl.program_id(0) == 0
        dxn = dres_ref[...]
        if has_pre:
            dx, dg = _rms_bwd_math(xn_ref[...], gn_ref[...], dh_ref[...])
            dxn = dxn + dx

            @pl.when(first)
            def _():
                dgn_ref[...] = dg

            @pl.when(jnp.logical_not(first))
            def _():
                dgn_ref[...] += dg
        dxn_ref[...] = dxn
        if has_post:
            dy, dg = _rms_bwd_math(y_ref[...], gp_ref[...], dxn)
            dy_ref[...] = dy.astype(BF16)

            @pl.when(first)
            def _():
                dgp_ref[...] = dg

            @pl.when(jnp.logical_not(first))
            def _():
                dgp_ref[...] += dg

    ins, in_specs = [dres], [_row_spec()]
    if has_pre:
        ins += list(pre)
        in_specs += [_row_spec(), _vec_spec(), _row_spec()]
    if has_post:
        ins += list(post)
        in_specs += [_row_spec(), _vec_spec()]
    out_shape, out_specs = [jax.ShapeDtypeStruct((S, D), F32)], [_row_spec()]
    if has_post:
        out_shape.append(jax.ShapeDtypeStruct((S, D), BF16))
        out_specs.append(_row_spec())
    for _ in range(int(has_pre) + int(has_post)):
        out_shape.append(jax.ShapeDtypeStruct((1, D), F32))
        out_specs.append(_vec_spec())
    out = list(pl.pallas_call(
        body, name=name, grid=(S // TR,), in_specs=in_specs, out_specs=out_specs, out_shape=out_shape,
        compiler_params=_params(("arbitrary",)))(*ins))
    dxn = out.pop(0)
    dy = out.pop(0) if has_post else None
    dgn = out.pop(0) if has_pre else None
    dgp = out.pop(0) if has_post else None
    return dxn, dy, dgn, dgp


def _loss_kernel(y, target):
    def body(y_ref, t_ref, loss_ref, dy_ref):
        e = y_ref[...] - t_ref[...]
        dy_ref[...] = e * (1.0 / D)
        part = jnp.zeros((1, 128), F32) + 0.5 * jnp.sum(jnp.mean(e * e, axis=-1, keepdims=True))

        @pl.when(pl.program_id(0) == 0)
        def _():
            loss_ref[...] = part

        @pl.when(pl.program_id(0) > 0)
        def _():
            loss_ref[...] += part

    return pl.pallas_call(
        body, name="loss", grid=(S // TR,), in_specs=[_row_spec(), _row_spec()],
        out_specs=[_vec_spec(128), _row_spec()],
        out_shape=[jax.ShapeDtypeStruct((1, 128), F32), jax.ShapeDtypeStruct((S, D), F32)],
        compiler_params=_params(("arbitrary",)))(y, target)


def _t5_bucket_np(dist):
    max_exact = NUM_BUCKETS // 2
    nf = np.maximum(dist, 1).astype(np.float32)
    large = max_exact + (np.log(nf / max_exact) / np.float32(math.log(MAX_DISTANCE / max_exact))
                         * (NUM_BUCKETS - max_exact)).astype(np.int32)
    large = np.minimum(large, NUM_BUCKETS - 1)
    return np.where(dist < max_exact, dist, large).astype(np.int32)


def _bucket_maps():
    a = np.arange(BLK)[:, None]
    b = np.arange(2 * BLK)[None, :]
    dist = np.maximum(a + BLK - b, 0)
    maps = [_t5_bucket_np(dist * d) for _, d in A_GROUPS] + [_t5_bucket_np(dist)]
    return np.stack(maps).astype(np.int32)


def _band_head_scalars(h):
    is_b = h >= N_A
    bpc = jnp.where(h < 4, 16, jnp.where(h < 8, 4, jnp.where(h < N_A, 1, 16)))
    maxd = jnp.where(is_b, BLK - 1, BLK)
    return bpc, maxd


def _kv_head(h):
    return jnp.where(h < N_A, h, N_A + (h - N_A) // 4)


def _bias_map(h):
    return jnp.minimum(h // 4, 3)


def _band_specs():
    q_spec = pl.BlockSpec((None, BLK, HD), lambda h, i: (h, i, 0))
    kp_spec = pl.BlockSpec((None, BLK, HD), lambda h, i: (_kv_head(h), jnp.maximum(i - 1, 0), 0))
    kc_spec = pl.BlockSpec((None, BLK, HD), lambda h, i: (_kv_head(h), i, 0))
    col_spec = pl.BlockSpec((None, BLK, 1), lambda h, i: (h, i, 0))
    bidx_spec = pl.BlockSpec((None, BLK, 2 * BLK), lambda h, i: (_bias_map(h), 0, 0))
    smem = pl.BlockSpec(memory_space=pltpu.SMEM)
    return q_spec, kp_spec, kc_spec, col_spec, bidx_spec, smem


def _band_bias(tab_ref, bidx_ref, h):
    bi = bidx_ref[...]
    bias = jnp.zeros((BLK, 2 * BLK), F32)
    for kk in range(NUM_BUCKETS):
        bias = jnp.where(bi == kk, tab_ref[kk, h], bias)
    return bias


def _band_scores(q_ref, kp_ref, kc_ref, bias_ref, h, i):
    bpc, maxd = _band_head_scalars(h)
    q = (q_ref[...] * SCALE).astype(BF16)
    sp = _dot(q, kp_ref[...].astype(BF16), NT) + bias_ref[:, :BLK]
    sc = _dot(q, kc_ref[...].astype(BF16), NT) + bias_ref[:, BLK:]
    a = lax.broadcasted_iota(jnp.int32, (BLK, BLK), 0)
    b = lax.broadcasted_iota(jnp.int32, (BLK, BLK), 1)
    has_prev = (i % bpc) != 0
    mask_p = jnp.logical_and(a + BLK - b <= maxd, has_prev)
    mask_c = a >= b
    return jnp.where(mask_p, sp, NEG), jnp.where(mask_c, sc, NEG), mask_p, mask_c


def _band_fwd(q, k, v, bidx, tab, sinks):
    q_spec, kp_spec, kc_spec, col_spec, bidx_spec, smem = _band_specs()

    def body(tab_ref, sink_ref, q_ref, kp_ref, kc_ref, vp_ref, vc_ref, bidx_ref, o_ref, lse_ref, bias_ref):
        h, i = pl.program_id(0), pl.program_id(1)

        @pl.when(i == 0)
        def _():
            bias_ref[...] = _band_bias(tab_ref, bidx_ref, h)

        sp, sc, _, _ = _band_scores(q_ref, kp_ref, kc_ref, bias_ref, h, i)
        m = jnp.maximum(jnp.max(sp, axis=1, keepdims=True), jnp.max(sc, axis=1, keepdims=True))
        pp, pc = jnp.exp(sp - m), jnp.exp(sc - m)
        l = jnp.sum(pp, axis=1, keepdims=True) + jnp.sum(pc, axis=1, keepdims=True)
        num = _dot(pp.astype(BF16), vp_ref[...].astype(BF16), NN) + _dot(pc.astype(BF16), vc_ref[...].astype(BF16), NN)
        lse = m + jnp.log(l)
        sig = 1.0 / (1.0 + jnp.exp(sink_ref[h] - lse))
        o_ref[...] = num * (sig / l)
        lse_ref[...] = lse

    return pl.pallas_call(
        body, name="band_fwd", grid=(N_BAND_Q, NQB),
        in_specs=[smem, smem, q_spec, kp_spec, kc_spec, kp_spec, kc_spec, bidx_spec],
        out_specs=[q_spec, col_spec],
        out_shape=[jax.ShapeDtypeStruct((N_BAND_Q, S, HD), F32), jax.ShapeDtypeStruct((N_BAND_Q, S, 1), F32)],
        scratch_shapes=[pltpu.VMEM((BLK, 2 * BLK), F32)],
        compiler_params=_params(("parallel", "arbitrary")))(tab, sinks, q, k, k, v, v, bidx)


def _band_bwd(q, k, v, bidx, tab, sinks, o, lse, do, dlse, stats_in):
    q_spec, kp_spec, kc_spec, col_spec, bidx_spec, smem = _band_specs()
    kv_full = pl.BlockSpec((None, S, HD), lambda h, i: (_kv_head(h), 0, 0))
    stat_spec = pl.BlockSpec((None, 8, 128), lambda h, i: (h, 0, 0))

    def body(tab_ref, sink_ref, q_ref, kp_ref, kc_ref, vp_ref, vc_ref, bidx_ref, o_ref, lse_ref, do_ref, dlse_ref,
             sin_ref, dq_ref, dk_ref, dv_ref, stat_ref, bias_ref, dsacc_ref, sk_ref):
        h, i = pl.program_id(0), pl.program_id(1)

        @pl.when(i == 0)
        def _():
            bias_ref[...] = _band_bias(tab_ref, bidx_ref, h)
            dsacc_ref[...] = jnp.zeros_like(dsacc_ref)
            sk_ref[...] = jnp.zeros_like(sk_ref)

        first_of_kv = jnp.logical_or(h < N_A, (h - N_A) % 4 == 0)

        @pl.when(jnp.logical_and(i == 0, first_of_kv))
        def _():
            dk_ref[...] = jnp.zeros_like(dk_ref)
            dv_ref[...] = jnp.zeros_like(dv_ref)

        sp, sc, mask_p, mask_c = _band_scores(q_ref, kp_ref, kc_ref, bias_ref, h, i)
        lse = lse_ref[...]
        pp = jnp.where(mask_p, jnp.exp(sp - lse), 0.0)
        pc = jnp.where(mask_c, jnp.exp(sc - lse), 0.0)
        sig = 1.0 / (1.0 + jnp.exp(sink_ref[h] - lse))
        do = do_ref[...]
        delta = jnp.sum(do * o_ref[...], axis=1, keepdims=True)
        dob = do.astype(BF16)
        dpp = _dot(dob, vp_ref[...].astype(BF16), NT)
        dpc = _dot(dob, vc_ref[...].astype(BF16), NT)
        dlse = dlse_ref[...]
        dsp = pp * (sig * (dpp - delta) + dlse)
        dsc = pc * (sig * (dpc - delta) + dlse)
        dspb, dscb = dsp.astype(BF16), dsc.astype(BF16)
        dq_ref[...] = SCALE * (_dot(dspb, kp_ref[...].astype(BF16), NN) + _dot(dscb, kc_ref[...].astype(BF16), NN))
        qb = q_ref[...].astype(BF16)
        sdo = (sig * do).astype(BF16)
        cur = pl.ds(pl.multiple_of(i * BLK, BLK), BLK)
        prev = pl.ds(pl.multiple_of(jnp.maximum(i - 1, 0) * BLK, BLK), BLK)
        dk_ref[cur, :] += SCALE * _dot(dscb, qb, TN)
        dk_ref[prev, :] += SCALE * _dot(dspb, qb, TN)
        dv_ref[cur, :] += _dot(pc.astype(BF16), sdo, TN)
        dv_ref[prev, :] += _dot(pp.astype(BF16), sdo, TN)
        dsacc_ref[:, :BLK] += dsp
        dsacc_ref[:, BLK:] += dsc
        sk_ref[...] += jnp.zeros((8, 128), F32) + jnp.sum(-delta * (1.0 - sig))

        @pl.when(i == NQB - 1)
        def _():
            acc = dsacc_ref[...]
            bi = bidx_ref[...]
            lane = lax.broadcasted_iota(jnp.int32, (8, 128), 1)
            sub = lax.broadcasted_iota(jnp.int32, (8, 128), 0)
            row = jnp.where(jnp.logical_and(sub == 1, lane == 0), sk_ref[...], 0.0)
            for kk in range(NUM_BUCKETS):
                tot = jnp.sum(jnp.where(bi == kk, acc, 0.0))
                row = jnp.where(jnp.logical_and(sub == 0, lane == kk), tot, row)
            stat_ref[...] = row + jnp.where(sub == 0, sin_ref[...], 0.0)

    return pl.pallas_call(
        body, name="band_bwd", grid=(N_BAND_Q, NQB),
        in_specs=[smem, smem, q_spec, kp_spec, kc_spec, kp_spec, kc_spec, bidx_spec, q_spec, col_spec, q_spec, col_spec,
                  stat_spec],
        out_specs=[q_spec, kv_full, kv_full, stat_spec],
        out_shape=[jax.ShapeDtypeStruct((N_BAND_Q, S, HD), F32), jax.ShapeDtypeStruct((N_BAND_KV, S, HD), F32),
                   jax.ShapeDtypeStruct((N_BAND_KV, S, HD), F32), jax.ShapeDtypeStruct((N_BAND_Q, 8, 128), F32)],
        scratch_shapes=[pltpu.VMEM((BLK, 2 * BLK), F32), pltpu.VMEM((BLK, 2 * BLK), F32), pltpu.VMEM((8, 128), F32)],
        compiler_params=_params(("arbitrary", "arbitrary")))(tab, sinks, q, k, k, v, v, bidx, o, lse, do, dlse, stats_in)


def _comb_specs():
    o_spec = pl.BlockSpec((3, None, S, HD), lambda h: (0, h, 0, 0))
    l_spec = pl.BlockSpec((3, None, S, 1), lambda h: (0, h, 0, 0))
    h_spec = pl.BlockSpec((None, S, HD), lambda h: (h, 0, 0))
    return o_spec, l_spec, h_spec


def _comb_weights(l_ref):
    l0, l1, l2 = l_ref[0], l_ref[1], l_ref[2]
    m = jnp.maximum(jnp.maximum(l0, l1), l2)
    e0, e1, e2 = jnp.exp(l0 - m), jnp.exp(l1 - m), jnp.exp(l2 - m)
    inv = 1.0 / (e0 + e1 + e2)
    return e0 * inv, e1 * inv, e2 * inv


def _comb_fwd(o_g, lse_g):
    o_spec, l_spec, h_spec = _comb_specs()

    def body(o_ref, l_ref, out_ref):
        w0, w1, w2 = _comb_weights(l_ref)
        out_ref[...] = w0 * o_ref[0] + w1 * o_ref[1] + w2 * o_ref[2]

    return pl.pallas_call(
        body, name="comb_fwd", grid=(4,), in_specs=[o_spec, l_spec], out_specs=h_spec,
        out_shape=jax.ShapeDtypeStruct((4, S, HD), F32), compiler_params=_params(("parallel",)))(o_g, lse_g)


def _comb_bwd(o_g, lse_g, dout):
    o_spec, l_spec, h_spec = _comb_specs()

    def body(o_ref, l_ref, d_ref, do_ref, dl_ref):
        ws = _comb_weights(l_ref)
        d = d_ref[...]
        dots = [jnp.sum(d * o_ref[g], axis=1, keepdims=True) for g in range(3)]
        tot = ws[0] * dots[0] + ws[1] * dots[1] + ws[2] * dots[2]
        for g in range(3):
            do_ref[g] = ws[g] * d
            dl_ref[g] = ws[g] * (dots[g] - tot)

    return pl.pallas_call(
        body, name="comb_bwd", grid=(4,), in_specs=[o_spec, l_spec, h_spec], out_specs=[o_spec, l_spec],
        out_shape=[jax.ShapeDtypeStruct((3, 4, S, HD), F32), jax.ShapeDtypeStruct((3, 4, S, 1), F32)],
        compiler_params=_params(("parallel",)))(o_g, lse_g, dout)


def _split3(x):
    hi = x.astype(BF16)
    r = x - hi.astype(F32)
    mid = r.astype(BF16)
    lo = (r - mid.astype(F32)).astype(BF16)
    return hi, mid, lo


def _tri_sum(x, tri):
    hi, mid, lo = _split3(x)
    return _dot(hi, tri, NN) + _dot(mid, tri, NN) + _dot(lo, tri, NN)


def _tri(strict_upper):
    r = lax.broadcasted_iota(jnp.int32, (BLK, BLK), 0)
    c = lax.broadcasted_iota(jnp.int32, (BLK, BLK), 1)
    return jnp.where(r > c if strict_upper else r < c, 1.0, 0.0).astype(BF16)


def _sb_block(q, k_ref, j, i):
    kj = k_ref[pl.ds(pl.multiple_of(j * BLK, BLK), BLK), :].astype(BF16)
    z = _dot(q, kj, NT)
    t = lax.broadcasted_iota(jnp.int32, (BLK, BLK), 0) + i * BLK
    s = lax.broadcasted_iota(jnp.int32, (BLK, BLK), 1) + j * BLK
    before = s < t
    lsp = jnp.minimum(z, 0.0) - jnp.log(1.0 + jnp.exp(-jnp.abs(z)))
    lk = jnp.where(before, lsp - z, 0.0)
    return before, lsp, lk


def _sb_specs():
    q_spec = pl.BlockSpec((None, BLK, HD), lambda h, i: (h, i, 0))
    full = pl.BlockSpec((None, S, HD), lambda h, i: (h, 0, 0))
    return q_spec, full


def _sb_fwd(q, k, v):
    q_spec, full = _sb_specs()

    def body(q_ref, k_ref, v_ref, o_ref):
        i = pl.program_id(1)
        q = (q_ref[...] * SCALE).astype(BF16)
        suffix = _tri(True)

        def step(n, carry):
            acc, rest = carry
            j = i - n
            before, lsp, lk = _sb_block(q, k_ref, j, i)
            log_rest = _tri_sum(lk, suffix) + rest
            w = jnp.where(before, jnp.exp(lsp + log_rest), 0.0)
            vj = v_ref[pl.ds(pl.multiple_of(j * BLK, BLK), BLK), :].astype(BF16)
            return acc + _dot(w.astype(BF16), vj, NN), rest + jnp.sum(lk, axis=1, keepdims=True)

        acc, _ = lax.fori_loop(0, i + 1, step, (jnp.zeros((BLK, HD), F32), jnp.zeros((BLK, 1), F32)))
        o_ref[...] = acc

    return pl.pallas_call(
        body, name="sb_fwd", grid=(C_HEADS, NQB), in_specs=[q_spec, full, full], out_specs=q_spec,
        out_shape=jax.ShapeDtypeStruct((C_HEADS, S, HD), F32),
        compiler_params=_params(("parallel", "arbitrary")))(q, k, v)


def _sb_bwd(q, k, v, do):
    q_spec, full = _sb_specs()

    def body(q_ref, k_ref, v_ref, do_ref, dq_ref, dk_ref, dv_ref, rest_ref):
        i = pl.program_id(1)

        @pl.when(i == 0)
        def _():
            dk_ref[...] = jnp.zeros_like(dk_ref)
            dv_ref[...] = jnp.zeros_like(dv_ref)

        qf = q_ref[...]
        q = (qf * SCALE).astype(BF16)
        qb = qf.astype(BF16)
        dob = do_ref[...].astype(BF16)
        suffix = _tri(True)
        prefix = _tri(False)

        def sweep_rest(n, rest):
            j = i - n
            _, _, lk = _sb_block(q, k_ref, j, i)
            rest_ref[j] = rest
            return rest + jnp.sum(lk, axis=1, keepdims=True)

        lax.fori_loop(0, i + 1, sweep_rest, jnp.zeros((BLK, 1), F32))

        def step(j, carry):
            dq, left = carry
            before, lsp, lk = _sb_block(q, k_ref, j, i)
            w = jnp.where(before, jnp.exp(lsp + _tri_sum(lk, suffix) + rest_ref[j]), 0.0)
            rows = pl.ds(pl.multiple_of(j * BLK, BLK), BLK)
            vj = v_ref[rows, :].astype(BF16)
            kj = k_ref[rows, :].astype(BF16)
            g = w * _dot(dob, vj, NT)
            h_in = _tri_sum(g, prefix) + left
            beta = jnp.exp(lsp)
            dz = jnp.where(before, g * (1.0 - beta) - h_in * beta, 0.0)
            dzb = dz.astype(BF16)
            dk_ref[rows, :] += SCALE * _dot(dzb, qb, TN)
            dv_ref[rows, :] += _dot(w.astype(BF16), dob, TN)
            return dq + _dot(dzb, kj, NN), left + jnp.sum(g, axis=1, keepdims=True)

        dq, _ = lax.fori_loop(0, i + 1, step, (jnp.zeros((BLK, HD), F32), jnp.zeros((BLK, 1), F32)))
        dq_ref[...] = SCALE * dq

    shape = jax.ShapeDtypeStruct((C_HEADS, S, HD), F32)
    return pl.pallas_call(
        body, name="sb_bwd", grid=(C_HEADS, NQB), in_specs=[q_spec, full, full, q_spec],
        out_specs=[q_spec, full, full], out_shape=[shape, shape, shape],
        scratch_shapes=[pltpu.VMEM((NQB, BLK, 1), F32)],
        compiler_params=_params(("arbitrary", "arbitrary")))(q, k, v, do)


TG = 256
GATE_BLK0 = OFF_GATE // TG


def _gate_specs():
    grid = (D // TG, S // TG)
    p_specs = [pl.BlockSpec((TG, TG), functools.partial(lambda c, r, br: (r, GATE_BLK0 + br * (D // TG) + c), br=br))
               for br in range(3)]
    b_spec = pl.BlockSpec((3, TG), lambda c, r: (0, c))
    t_spec = pl.BlockSpec((TG, TG), lambda c, r: (r, c))
    e_spec = pl.BlockSpec((3, TG, TG), lambda c, r: (0, r, c))
    return grid, p_specs, b_spec, t_spec, e_spec


def _sigmoid(x):
    return 1.0 / (1.0 + jnp.exp(-x))


def _three_rows(rows):
    sub = lax.broadcasted_iota(jnp.int32, (3, rows[0].shape[1]), 0)
    return jnp.where(sub == 0, rows[0], jnp.where(sub == 1, rows[1], rows[2]))


def _gate_fwd(proj, b_gate, br):
    grid, p_specs, b_spec, t_spec, e_spec = _gate_specs()

    def body(p0, p1, p2, b_ref, br_ref, out_ref):
        acc = jnp.zeros((TG, TG), F32)
        for n, p in enumerate((p0, p1, p2)):
            acc += _sigmoid(p[...] + b_ref[n:n + 1, :]) * br_ref[n]
        out_ref[...] = acc.astype(BF16)

    return pl.pallas_call(
        body, name="gate_fwd", grid=grid, in_specs=p_specs + [b_spec, e_spec], out_specs=t_spec,
        out_shape=jax.ShapeDtypeStruct((S, D), BF16),
        compiler_params=_params(("parallel", "parallel")))(proj, proj, proj, b_gate, br)


def _gate_bwd(proj, b_gate, br, dmerged):
    grid, p_specs, b_spec, t_spec, e_spec = _gate_specs()

    def body(p0, p1, p2, b_ref, br_ref, dm_ref, e_ref, dg_ref, db_ref):
        dm = dm_ref[...]
        rows = []
        for n, p in enumerate((p0, p1, p2)):
            g = _sigmoid(p[...] + b_ref[n:n + 1, :])
            e_ref[n] = (dm * g).astype(BF16)
            dpre = dm * br_ref[n] * g * (1.0 - g)
            dg_ref[n] = dpre.astype(BF16)
            rows.append(jnp.sum(dpre, axis=0, keepdims=True))
        db = _three_rows(rows)

        @pl.when(pl.program_id(1) == 0)
        def _():
            db_ref[...] = db

        @pl.when(pl.program_id(1) > 0)
        def _():
            db_ref[...] += db

    return pl.pallas_call(
        body, name="gate_bwd", grid=grid, in_specs=p_specs + [b_spec, e_spec, t_spec],
        out_specs=[e_spec, e_spec, b_spec],
        out_shape=[jax.ShapeDtypeStruct((3, S, D), BF16), jax.ShapeDtypeStruct((3, S, D), BF16),
                   jax.ShapeDtypeStruct((3, D), F32)],
        compiler_params=_params(("parallel", "arbitrary")))(proj, proj, proj, b_gate, br, dmerged)


TC = 256
N_FF_BLK = D_FF // TC
GELU_C = math.sqrt(2.0 / math.pi)


def _shift_down(x, n):
    rows = lax.broadcasted_iota(jnp.int32, x.shape, 0)
    return jnp.where(rows >= n, pltpu.roll(x, n, axis=0), 0.0)


def _shift_up(x, n):
    rows = lax.broadcasted_iota(jnp.int32, x.shape, 0)
    return jnp.where(rows < x.shape[0] - n, pltpu.roll(x, x.shape[0] - n, axis=0), 0.0)


def _conv(u, w, b):
    return w[2:3, :] * u + w[1:2, :] * _shift_down(u, 1) + w[0:1, :] * _shift_down(u, 2) + b


def _gelu_parts(x):
    inner = GELU_C * (x + 0.044715 * x * x * x)
    t = jnp.tanh(inner)
    gelu = 0.5 * x * (1.0 + t)
    dgelu = 0.5 * (1.0 + t) + 0.5 * x * (1.0 - t * t) * GELU_C * (1.0 + 3 * 0.044715 * x * x)
    return gelu, dgelu


def _conv_specs():
    ug = pl.BlockSpec((S, TC), lambda c: (0, c))
    uv = pl.BlockSpec((S, TC), lambda c: (0, N_FF_BLK + c))
    wg = pl.BlockSpec((3, TC), lambda c: (0, c))
    wv = pl.BlockSpec((3, TC), lambda c: (0, N_FF_BLK + c))
    bg = pl.BlockSpec((1, TC), lambda c: (0, c))
    bv = pl.BlockSpec((1, TC), lambda c: (0, N_FF_BLK + c))
    return ug, uv, wg, wv, bg, bv


def _conv_fwd(u, conv_w, conv_b):
    ug, uv, wg, wv, bg, bv = _conv_specs()

    def body(ug_ref, uv_ref, wg_ref, wv_ref, bg_ref, bv_ref, a_ref):
        gc = _conv(ug_ref[...], wg_ref[...], bg_ref[...])
        vc = _conv(uv_ref[...], wv_ref[...], bv_ref[...])
        a_ref[...] = (_gelu_parts(gc)[0] * vc).astype(BF16)

    return pl.pallas_call(
        body, name="conv_fwd", grid=(N_FF_BLK,), in_specs=[ug, uv, wg, wv, bg, bv], out_specs=ug,
        out_shape=jax.ShapeDtypeStruct((S, D_FF), BF16),
        compiler_params=_params(("parallel",)))(u, u, conv_w, conv_w, conv_b, conv_b)


def _conv_bwd(u, conv_w, conv_b, da):
    ug, uv, wg, wv, bg, bv = _conv_specs()

    def back(duc, u, w):
        du = w[2:3, :] * duc + w[1:2, :] * _shift_up(duc, 1) + w[0:1, :] * _shift_up(duc, 2)
        dw = _three_rows([jnp.sum(duc * _shift_down(u, 2), axis=0, keepdims=True),
                          jnp.sum(duc * _shift_down(u, 1), axis=0, keepdims=True),
                          jnp.sum(duc * u, axis=0, keepdims=True)])
        return du, dw, jnp.sum(duc, axis=0, keepdims=True)

    def body(ug_ref, uv_ref, wg_ref, wv_ref, bg_ref, bv_ref, da_ref, dug_ref, duv_ref, dwg_ref, dwv_ref, dbg_ref, dbv_ref):
        u_g, u_v = ug_ref[...], uv_ref[...]
        gc = _conv(u_g, wg_ref[...], bg_ref[...])
        vc = _conv(u_v, wv_ref[...], bv_ref[...])
        gelu, dgelu = _gelu_parts(gc)
        da = da_ref[...]
        du, dw, db = back(da * vc * dgelu, u_g, wg_ref[...])
        dug_ref[...] = du.astype(BF16)
        dwg_ref[...] = dw
        dbg_ref[...] = db
        du, dw, db = back(da * gelu, u_v, wv_ref[...])
        duv_ref[...] = du.astype(BF16)
        dwv_ref[...] = dw
        dbv_ref[...] = db

    outs = pl.pallas_call(
        body, name="conv_bwd", grid=(N_FF_BLK,), in_specs=[ug, uv, wg, wv, bg, bv, ug],
        out_specs=[ug, ug, wg, wg, bg, bg],
        out_shape=[jax.ShapeDtypeStruct((S, D_FF), BF16), jax.ShapeDtypeStruct((S, D_FF), BF16),
                   jax.ShapeDtypeStruct((3, D_FF), F32), jax.ShapeDtypeStruct((3, D_FF), F32),
                   jax.ShapeDtypeStruct((1, D_FF), F32), jax.ShapeDtypeStruct((1, D_FF), F32)],
        compiler_params=_params(("parallel",)))(u, u, conv_w, conv_w, conv_b, conv_b, da)
    return outs


def _rowwise(name, fn, ins, out_dtypes, rows_per_block):
    r, c = ins[0].shape
    tr = min(rows_per_block, r)
    n_in = len(ins)

    def body(*refs):
        outs = fn([ref[...] for ref in refs[:n_in]])
        for ref, val in zip(refs[n_in:], outs):
            ref[...] = val.astype(ref.dtype)

    spec = pl.BlockSpec((tr, c), lambda i: (i, 0))
    return pl.pallas_call(
        body, name=name, grid=(pl.cdiv(r, tr),), in_specs=[spec] * n_in, out_specs=[spec] * len(out_dtypes),
        out_shape=[jax.ShapeDtypeStruct((r, c), dt) for dt in out_dtypes],
        compiler_params=_params(("parallel",)))(*ins)


def _adamw_math(vals):
    w, g, m, v = vals
    m = ADAM_B1 * m + (1.0 - ADAM_B1) * g
    v = ADAM_B2 * v + (1.0 - ADAM_B2) * (g * g)
    m_hat = m / (1.0 - ADAM_B1 ** ADAM_STEP)
    v_hat = v / (1.0 - ADAM_B2 ** ADAM_STEP)
    delta = -ADAM_LR * (m_hat / (jnp.sqrt(v_hat) + ADAM_EPS) + ADAM_WD * w)
    return [delta, m, v]


def _adamw(name, w, g, m, v):
    shape = w.shape
    cols = shape[-1]
    flat = [t.reshape(-1, cols) for t in (w, g, m, v)]
    outs = _rowwise(name, _adamw_math, flat, [F32, F32, F32], 128)
    return [t.reshape(shape) for t in outs]


MESH = pl.DeviceIdType.MESH
ANY = pl.BlockSpec(memory_space=pl.ANY)


def _place():
    x, y, c = lax.axis_index("x"), lax.axis_index("y"), lax.axis_index("c")
    chips = [(1 - x, y), (x, 1 - y), (1 - x, 1 - y)]
    return x, y, c, chips


def _gather_weights(shards):
    n = len(shards)

    def body(*refs):
        ins, outs = refs[:n], refs[n:2 * n]
        local_sem, send_sem, recv_sem = refs[2 * n:]
        x, y, c, chips = _place()
        me = 2 * x + y
        sibling = (x, y, 1 - c)
        local = [pltpu.make_async_copy(ins[a], outs[a].at[me], local_sem.at[a]) for a in range(n)]
        for cp in local:
            cp.start()

        def over_ici(a, k, from_chip):
            return pltpu.make_async_remote_copy(
                src_ref=ins[a].at[c], dst_ref=outs[a].at[from_chip, c],
                send_sem=send_sem.at[a, k], recv_sem=recv_sem.at[a, k],
                device_id=(*chips[k], c), device_id_type=MESH)

        def over_d2d(a, k, layer):
            rows = outs[a].at[2 * chips[k][0] + chips[k][1], layer]
            return pltpu.make_async_remote_copy(
                src_ref=rows, dst_ref=rows, send_sem=send_sem.at[a, 3 + k], recv_sem=recv_sem.at[a, 3 + k],
                device_id=sibling, device_id_type=MESH)

        sends = [over_ici(a, k, me) for a in range(n) for k in range(3)]
        for cp in sends:
            cp.start()
        passed = []
        for a in range(n):
            for k in range(3):
                over_ici(a, k, 2 * chips[k][0] + chips[k][1]).wait_recv()
                cp = over_d2d(a, k, c)
                cp.start()
                passed.append(cp)
        for a in range(n):
            for k in range(3):
                over_d2d(a, k, 1 - c).wait_recv()
        for cp in sends + passed:
            cp.wait_send()
        for cp in local:
            cp.wait()

    return pl.pallas_call(
        body, name="gather_weights", in_specs=[ANY] * n, out_specs=[ANY] * n,
        out_shape=[jax.ShapeDtypeStruct((N_CHIPS,) + s.shape, s.dtype) for s in shards],
        scratch_shapes=[pltpu.SemaphoreType.DMA((n,)), pltpu.SemaphoreType.DMA((n, 6)), pltpu.SemaphoreType.DMA((n, 6))],
    )(*shards)


def _swap_layers(grads):
    n = len(grads)

    def body(*refs):
        ins, own, got = refs[:n], refs[n:2 * n], refs[2 * n:3 * n]
        local_sem, send_sem, recv_sem = refs[3 * n:]
        x, y, c, _ = _place()
        local = [pltpu.make_async_copy(ins[a].at[c], own[a], local_sem.at[a]) for a in range(n)]
        sends = [pltpu.make_async_remote_copy(
            src_ref=ins[a].at[1 - c], dst_ref=got[a], send_sem=send_sem.at[a], recv_sem=recv_sem.at[a],
            device_id=(x, y, 1 - c), device_id_type=MESH) for a in range(n)]
        for cp in local + sends:
            cp.start()
        for cp in sends:
            cp.wait()
        for cp in local:
            cp.wait()

    shapes = [jax.ShapeDtypeStruct(g.shape[1:], g.dtype) for g in grads]
    out = pl.pallas_call(
        body, name="swap_layers", in_specs=[ANY] * n, out_specs=[ANY] * (2 * n), out_shape=shapes + shapes,
        scratch_shapes=[pltpu.SemaphoreType.DMA((n,)), pltpu.SemaphoreType.DMA((n,)), pltpu.SemaphoreType.DMA((n,))],
    )(*grads)
    return out[:n], out[n:]


def _scatter_shards(parts):
    n = len(parts)

    def body(*refs):
        ins, outs = refs[:n], refs[n:2 * n]
        local_sem, send_sem, recv_sem = refs[2 * n:]
        x, y, c, chips = _place()
        me = 2 * x + y
        local = [pltpu.make_async_copy(ins[a].at[me], outs[a].at[me], local_sem.at[a]) for a in range(n)]
        sends = [pltpu.make_async_remote_copy(
            src_ref=ins[a].at[2 * chips[k][0] + chips[k][1]], dst_ref=outs[a].at[me],
            send_sem=send_sem.at[a, k], recv_sem=recv_sem.at[a, k],
            device_id=(*chips[k], c), device_id_type=MESH) for a in range(n) for k in range(3)]
        for cp in local + sends:
            cp.start()
        for a in range(n):
            for k in range(3):
                src_chip = 2 * chips[k][0] + chips[k][1]
                pltpu.make_async_remote_copy(
                    src_ref=ins[a].at[me], dst_ref=outs[a].at[src_chip],
                    send_sem=send_sem.at[a, k], recv_sem=recv_sem.at[a, k],
                    device_id=(x, y, c), device_id_type=MESH).wait_recv()
        for cp in sends:
            cp.wait_send()
        for cp in local:
            cp.wait()

    return pl.pallas_call(
        body, name="scatter_shards", in_specs=[ANY] * n, out_specs=[ANY] * n,
        out_shape=[jax.ShapeDtypeStruct(p.shape, p.dtype) for p in parts],
        scratch_shapes=[pltpu.SemaphoreType.DMA((n,)), pltpu.SemaphoreType.DMA((n, 3)), pltpu.SemaphoreType.DMA((n, 3))],
    )(*parts)


def _join_layers(halves):
    n = len(halves)

    def body(*refs):
        ins, outs = refs[:n], refs[n:2 * n]
        local_sem, send_sem, recv_sem = refs[2 * n:]
        x, y, c, _ = _place()
        local = [pltpu.make_async_copy(ins[a], outs[a].at[c], local_sem.at[a]) for a in range(n)]
        sends = [pltpu.make_async_remote_copy(
            src_ref=ins[a], dst_ref=outs[a].at[c], send_sem=send_sem.at[a], recv_sem=recv_sem.at[a],
            device_id=(x, y, 1 - c), device_id_type=MESH) for a in range(n)]
        for cp in local + sends:
            cp.start()
        for a in range(n):
            sends[a].wait_send()
            pltpu.make_async_remote_copy(
                src_ref=ins[a], dst_ref=outs[a].at[1 - c], send_sem=send_sem.at[a], recv_sem=recv_sem.at[a],
                device_id=(x, y, 1 - c), device_id_type=MESH).wait_recv()
        for cp in local:
            cp.wait()

    return pl.pallas_call(
        body, name="join_layers", in_specs=[ANY] * n, out_specs=[ANY] * n,
        out_shape=[jax.ShapeDtypeStruct((2,) + h.shape, h.dtype) for h in halves],
        scratch_shapes=[pltpu.SemaphoreType.DMA((n,)), pltpu.SemaphoreType.DMA((n,)), pltpu.SemaphoreType.DMA((n,))],
    )(*halves)


def _all_reduce_small(block):
    r = block.shape[0]

    def body(x_ref, out_ref, slots, send_sem, recv_sem):
        x, y, c, _ = _place()
        me = 4 * x + 2 * y + c
        slots[me] = x_ref[...]
        sends = []
        for mask in range(1, N_DEV):
            fx, fy, fc = (mask >> 2) & 1, (mask >> 1) & 1, mask & 1
            peer = (x ^ fx, y ^ fy, c ^ fc)
            cp = pltpu.make_async_remote_copy(
                src_ref=x_ref, dst_ref=slots.at[me], send_sem=send_sem.at[mask - 1], recv_sem=recv_sem.at[mask - 1],
                device_id=peer, device_id_type=MESH)
            cp.start()
            sends.append(cp)
        for mask in range(1, N_DEV):
            src = me ^ mask
            pltpu.make_async_remote_copy(
                src_ref=x_ref, dst_ref=slots.at[src], send_sem=send_sem.at[mask - 1], recv_sem=recv_sem.at[mask - 1],
                device_id=(x, y, c), device_id_type=MESH).wait_recv()
        for cp in sends:
            cp.wait_send()
        total = slots[0]
        for d in range(1, N_DEV):
            total = total + slots[d]
        out_ref[...] = total

    vmem = pl.BlockSpec(memory_space=pltpu.VMEM)
    return pl.pallas_call(
        body, name="all_reduce_small", in_specs=[vmem], out_specs=vmem,
        out_shape=jax.ShapeDtypeStruct((r, 128), F32),
        scratch_shapes=[pltpu.VMEM((N_DEV, r, 128), F32), pltpu.SemaphoreType.DMA((N_DEV - 1,)),
                        pltpu.SemaphoreType.DMA((N_DEV - 1,))],
        compiler_params=pltpu.CompilerParams(vmem_limit_bytes=VMEM_LIMIT))(block)


def _heads(t):
    return t.reshape(S, -1, HD).transpose(1, 0, 2)


def _unheads(t):
    return t.transpose(1, 0, 2).reshape(S, -1)


def _to_classes(t, d):
    h, _, e = t.shape
    return t.reshape(h, S // d, d, e).transpose(0, 2, 1, 3).reshape(h, S, e)


def _from_classes(t, d):
    h, _, e = t.shape
    return t.reshape(h, d, S // d, e).transpose(0, 2, 1, 3).reshape(h, S, e)


def _full_cols(w_g):
    return w_g.transpose(1, 0, 2).reshape(w_g.shape[1], -1)


def _shard_cols(dw):
    k = dw.shape[0]
    return dw.reshape(k, N_CHIPS, -1).transpose(1, 0, 2)


def _split_proj(proj):
    a = proj[:, :2304].reshape(S, 3, 3, 4 * HD)
    bq = _heads(proj[:, 2304:2816])
    bkv = proj[:, 2816:3072].reshape(S, 2, 2 * HD)
    cq = proj[:, 3072:3840].reshape(S, 3, 4 * HD)
    qa, ka, va = ([_to_classes(_heads(a[:, t, g]), d) for g, (_, d) in enumerate(A_GROUPS)] for t in range(3))
    q = jnp.concatenate(qa + [bq], axis=0)
    k = jnp.concatenate(ka + [_heads(bkv[:, 0])], axis=0)
    v = jnp.concatenate(va + [_heads(bkv[:, 1])], axis=0)
    return q, k, v, _heads(cq[:, 0]), _heads(cq[:, 1]), _heads(cq[:, 2])


def _join_dproj(dq, dk, dv, dcq, dck, dcv, dgate):
    def groups(t):
        return [_unheads(_from_classes(t[4 * g:4 * g + 4], d)) for g, (_, d) in enumerate(A_GROUPS)]

    cols = groups(dq) + groups(dk) + groups(dv)
    cols += [_unheads(dq[N_A:]), _unheads(dk[N_A:]), _unheads(dv[N_A:]), _unheads(dcq), _unheads(dck), _unheads(dcv)]
    cols = [t.astype(BF16) for t in cols] + [dgate[0], dgate[1], dgate[2]]
    return jnp.concatenate(cols, axis=1).reshape(S, N_CHIPS, IN_SHARD).transpose(1, 0, 2)


def _mixer_fwd(h1, w, rel_bias, sinks20, bidx):
    proj = _proj_in(h1, w["w_in"]).transpose(1, 0, 2).reshape(S, IN_COLS)
    q, k, v, cq, ck, cv = _split_proj(proj)
    o_band, lse = _band_fwd(q, k, v, bidx, rel_bias, sinks20)
    o_g = jnp.stack([_from_classes(o_band[4 * g:4 * g + 4], d) for g, (_, d) in enumerate(A_GROUPS)])
    lse_g = jnp.stack([_from_classes(lse[4 * g:4 * g + 4], d) for g, (_, d) in enumerate(A_GROUPS)])
    o_a = _unheads(_comb_fwd(o_g, lse_g)).astype(BF16)
    o_b = _unheads(o_band[N_A:]).astype(BF16)
    o_c = _unheads(_sb_fwd(cq, ck, cv)).astype(BF16)
    br = jnp.stack([_mm_nn("branch_a", o_a, w["w_br_a"], F32), _mm_nn("branch_b", o_b, w["w_br_b"], F32),
                    _mm_nn("branch_c", o_c, w["w_br_c"], F32)])
    merged = _gate_fwd(proj, w["b_gate"], br)
    mo = _mm_nn("out_proj", merged, w["w_out"], F32)
    saved = dict(proj=proj, q=q, k=k, v=v, cq=cq, ck=ck, cv=cv, o_band=o_band, lse=lse, o_g=o_g, lse_g=lse_g,
                 o_a=o_a, o_b=o_b, o_c=o_c, br=br, merged=merged)
    return mo, saved


def _mixer_bwd(d_mo, h1, w, sv, rel_bias, sinks20, bidx, stats_in):
    grads = {}
    dmerged = _mm_nt("out_proj_dx", d_mo, w["w_out"], F32)
    grads["w_out"] = _mm_tn("out_proj_dw", sv["merged"], d_mo, BF16).reshape(N_CHIPS, D // N_CHIPS, D)
    e, dgate, db_gate = _gate_bwd(sv["proj"], w["b_gate"], sv["br"], dmerged)
    grads["b_gate"] = db_gate
    d_o = {}
    for n, name in enumerate("abc"):
        d_o[name] = _mm_nt("branch_%s_dx" % name, e[n], w["w_br_" + name], F32)
        grads["w_br_" + name] = _shard_cols(_mm_tn("branch_%s_dw" % name, sv["o_" + name], e[n], BF16))
    do_g, dlse_g = _comb_bwd(sv["o_g"], sv["lse_g"], _heads(d_o["a"]))
    do_band = jnp.concatenate([_to_classes(do_g[g], d) for g, (_, d) in enumerate(A_GROUPS)] + [_heads(d_o["b"])], axis=0)
    dlse = jnp.concatenate([_to_classes(dlse_g[g], d) for g, (_, d) in enumerate(A_GROUPS)]
                           + [jnp.zeros((N_BAND_Q - N_A, S, 1), F32)], axis=0)
    dq, dk, dv, stats = _band_bwd(sv["q"], sv["k"], sv["v"], bidx, rel_bias, sinks20, sv["o_band"], sv["lse"],
                                  do_band, dlse, stats_in)
    dcq, dck, dcv = _sb_bwd(sv["cq"], sv["ck"], sv["cv"], _heads(d_o["c"]))
    dproj_s = _join_dproj(dq, dk, dv, dcq, dck, dcv, dgate)
    dh1 = _proj_in_dx(dproj_s, w["w_in"])
    grads["w_in"] = _proj_in_dw(h1, dproj_s)
    return dh1, grads, stats


def _ffn_fwd(h2, w):
    u = _mm_nn("ffn_up", h2, w["w_up"], F32, tn=1024)
    a = _conv_fwd(u, w["conv_w"], w["conv_b"])
    dn = _mm_nn("ffn_down", a, w["w_down"], F32)
    return dn, dict(u=u, a=a)


def _ffn_bwd(d_dn, h2, w, sv):
    grads = {}
    da = _mm_nt("ffn_down_dx", d_dn, w["w_down"], F32, tn=1024)
    grads["w_down"] = _mm_tn("ffn_down_dw", sv["a"], d_dn, BF16).reshape(N_CHIPS, D_FF // N_CHIPS, D)
    dug, duv, dwg, dwv, dbg, dbv = _conv_bwd(sv["u"], w["conv_w"], w["conv_b"], da)
    du = jnp.concatenate([dug, duv], axis=1)
    grads["conv_w"] = jnp.concatenate([dwg, dwv], axis=1)
    grads["conv_b"] = jnp.concatenate([dbg, dbv], axis=1)
    dh2 = _mm_nt("ffn_up_dx", du, w["w_up"], F32)
    grads["w_up"] = _shard_cols(_mm_tn("ffn_up_dw", h2, du, BF16, tn=1024))
    return dh2, grads


BIG = ("w_in", "w_br_a", "w_br_b", "w_br_c", "w_out", "w_up", "w_down")
SMALL_ROWS = (("rel_bias", 5), ("attn_pre_norm", 16), ("attn_post_norm", 16), ("ffn_pre_norm", 16), ("ffn_post_norm", 16),
              ("sinks", 1), ("conv_b", 128), ("b_gate", 48), ("conv_w", 384), ("loss", 1))
SMALL_TOTAL = 632


def _pack_small(vals):
    rows = []
    for name, n in SMALL_ROWS:
        flat = vals[name].reshape(-1).astype(F32)
        rows.append(jnp.pad(flat, (0, n * 128 - flat.shape[0])).reshape(n, 128))
    used = sum(n for _, n in SMALL_ROWS)
    rows.append(jnp.zeros((SMALL_TOTAL - used, 128), F32))
    return jnp.concatenate(rows, axis=0)


def _unpack_small(block, shapes):
    out, row = {}, 0
    for name, n in SMALL_ROWS:
        size = int(np.prod(shapes[name]))
        out[name] = block[row:row + n].reshape(-1)[:size].reshape(shapes[name])
        row += n
    return out


def kernel(x, rel_bias, attn_pre_norm, w_in, b_gate, sinks, w_br_a, w_br_b, w_br_c, w_out, attn_post_norm, ffn_pre_norm, w_up, conv_w, conv_b, w_down, ffn_post_norm, loss_target, m_rel_bias, m_attn_pre_norm, m_w_in, m_b_gate, m_sinks, m_w_br_a, m_w_br_b, m_w_br_c, m_w_out, m_attn_post_norm, m_ffn_pre_norm, m_w_up, m_conv_w, m_conv_b, m_w_down, m_ffn_post_norm, v_rel_bias, v_attn_pre_norm, v_w_in, v_b_gate, v_sinks, v_w_br_a, v_w_br_b, v_w_br_c, v_w_out, v_attn_post_norm, v_ffn_pre_norm, v_w_up, v_conv_w, v_conv_b, v_w_down, v_ffn_post_norm):
    names = ("rel_bias", "attn_pre_norm", "w_in", "b_gate", "sinks", "w_br_a", "w_br_b", "w_br_c", "w_out",
             "attn_post_norm", "ffn_pre_norm", "w_up", "conv_w", "conv_b", "w_down", "ffn_post_norm")
    weights = dict(zip(names, (rel_bias, attn_pre_norm, w_in, b_gate, sinks, w_br_a, w_br_b, w_br_c, w_out,
                               attn_post_norm, ffn_pre_norm, w_up, conv_w, conv_b, w_down, ffn_post_norm)))
    mom1 = dict(zip(names, (m_rel_bias, m_attn_pre_norm, m_w_in, m_b_gate, m_sinks, m_w_br_a, m_w_br_b, m_w_br_c,
                            m_w_out, m_attn_post_norm, m_ffn_pre_norm, m_w_up, m_conv_w, m_conv_b, m_w_down,
                            m_ffn_post_norm)))
    mom2 = dict(zip(names, (v_rel_bias, v_attn_pre_norm, v_w_in, v_b_gate, v_sinks, v_w_br_a, v_w_br_b, v_w_br_c,
                            v_w_out, v_attn_post_norm, v_ffn_pre_norm, v_w_up, v_conv_w, v_conv_b, v_w_down,
                            v_ffn_post_norm)))

    gathered = _gather_weights([weights[n].astype(BF16) for n in BIG] + [b_gate, conv_w])
    gathered = dict(zip(BIG + ("b_gate", "conv_w"), gathered))
    layers = []
    for l in range(DEPTH):
        w = {"w_in": gathered["w_in"][:, l]}
        for n in ("w_br_a", "w_br_b", "w_br_c", "w_up", "b_gate", "conv_w"):
            w[n] = _full_cols(gathered[n][:, l])
        w["w_out"] = gathered["w_out"][:, l].reshape(D, D)
        w["w_down"] = gathered["w_down"][:, l].reshape(D_FF, D)
        w["conv_b"] = conv_b[l:l + 1]
        layers.append(w)

    local = _local_step(x.reshape(S, D), loss_target.reshape(S, D), layers, rel_bias, sinks, attn_pre_norm,
                        attn_post_norm, ffn_pre_norm, ffn_post_norm)
    return _reduce_and_update(x.shape, names, weights, mom1, mom2, *local)


def _local_step(xs, target, layers, rel_bias, sinks, attn_pre_norm, attn_post_norm, ffn_pre_norm, ffn_post_norm):
    bidx = jnp.asarray(_bucket_maps())

    saved = []
    h1 = _rms_fwd("pre_norm_first", xs, attn_pre_norm[0:1])
    x_in = xs
    for l in range(DEPTH):
        w = layers[l]
        sinks20 = jnp.concatenate([jnp.full((N_A,), NEG, F32), sinks[l]])
        mo, sv_mix = _mixer_fwd(h1, w, rel_bias, sinks20, bidx)
        x_mid, h2 = _post_pre_fwd("post_attn_norm", x_in, mo, attn_post_norm[l:l + 1], ffn_pre_norm[l:l + 1])
        dn, sv_ffn = _ffn_fwd(h2, w)
        g_next = attn_pre_norm[l + 1:l + 2] if l + 1 < DEPTH else None
        x_out, h1_next = _post_pre_fwd("post_ffn_norm" if l + 1 < DEPTH else "post_ffn_norm_last", x_mid, dn,
                                       ffn_post_norm[l:l + 1], g_next)
        saved.append(dict(x_in=x_in, h1=h1, mo=mo, x_mid=x_mid, h2=h2, dn=dn, sinks20=sinks20, mix=sv_mix, ffn=sv_ffn))
        x_in, h1 = x_out, h1_next

    loss_row, dres = _loss_kernel(x_in, target)

    big_grads = [None] * DEPTH
    small = [None] * DEPTH
    stats = jnp.zeros((N_BAND_Q, 8, 128), F32)
    dh_next = None
    for l in reversed(range(DEPTH)):
        w, sv = layers[l], saved[l]
        if l + 1 < DEPTH:
            pre = (saved[l + 1]["x_in"], attn_pre_norm[l + 1:l + 2], dh_next)
            dres, d_dn, dg_pre_next, dg_fpost = _norm_bwd("post_ffn_norm_bwd", dres, pre,
                                                          (sv["dn"], ffn_post_norm[l:l + 1]))
            small[l + 1]["attn_pre_norm"] = dg_pre_next
        else:
            dres, d_dn, _, dg_fpost = _norm_bwd("post_ffn_norm_last_bwd", dres, None, (sv["dn"], ffn_post_norm[l:l + 1]))
        dh2, g_ffn = _ffn_bwd(d_dn, sv["h2"], w, sv["ffn"])
        dres, d_mo, dg_fpre, dg_apost = _norm_bwd("post_attn_norm_bwd", dres, (sv["x_mid"], ffn_pre_norm[l:l + 1], dh2),
                                                  (sv["mo"], attn_post_norm[l:l + 1]))
        dh_next, g_mix, stats = _mixer_bwd(d_mo, sv["h1"], w, sv["mix"], rel_bias, sv["sinks20"], bidx, stats)
        big_grads[l] = {**g_ffn, **g_mix}
        small[l] = dict(ffn_post_norm=dg_fpost, ffn_pre_norm=dg_fpre, attn_post_norm=dg_apost,
                        sinks=stats[N_A:, 1, 0], conv_b=g_ffn["conv_b"], b_gate=g_mix["b_gate"], conv_w=g_ffn["conv_w"])
    grad_x, _, dg_pre0, _ = _norm_bwd("pre_norm_first_bwd", dres, (saved[0]["x_in"], attn_pre_norm[0:1], dh_next), None)
    small[0]["attn_pre_norm"] = dg_pre0
    return loss_row, grad_x, big_grads, small, stats


def _reduce_and_update(x_shape, names, weights, mom1, mom2, loss_row, grad_x, big_grads, small, stats):
    small_vals = {n: jnp.stack([small[l][n].reshape(weights[n].shape[1:]) for l in range(DEPTH)])
                  for n in ("attn_pre_norm", "attn_post_norm", "ffn_pre_norm", "ffn_post_norm", "conv_b", "sinks")}
    small_vals["b_gate"] = jnp.stack([small[l]["b_gate"] for l in range(DEPTH)])
    small_vals["conv_w"] = jnp.stack([small[l]["conv_w"] for l in range(DEPTH)])
    small_vals["rel_bias"] = stats[:, 0, :NUM_BUCKETS].T
    small_vals["loss"] = loss_row[0, :1]
    shapes = {n: v.shape for n, v in small_vals.items()}
    reduced = _unpack_small(_all_reduce_small(_pack_small(small_vals)), shapes)
    chip = 2 * lax.axis_index("x") + lax.axis_index("y")
    reduced["b_gate"] = lax.dynamic_slice_in_dim(reduced["b_gate"], chip * (D // N_CHIPS), D // N_CHIPS, axis=2)
    reduced["conv_w"] = lax.dynamic_slice_in_dim(reduced["conv_w"], chip * (2 * D_FF // N_CHIPS), 2 * D_FF // N_CHIPS, axis=2)

    stacked = [jnp.stack([big_grads[l][n] for l in range(DEPTH)]) for n in BIG]
    own, got = _swap_layers(stacked)
    parts = [_rowwise("add_sibling_" + n, lambda v: [v[0].astype(F32) + v[1].astype(F32)],
                      [o.reshape(-1, o.shape[-1]), g.reshape(-1, g.shape[-1])], [BF16], 512)[0].reshape(o.shape)
             for n, o, g in zip(BIG, own, got)]
    arrived = _scatter_shards(parts)
    halves = []
    for n, t in zip(BIG, arrived):
        flat = [t[i].reshape(-1, t.shape[-1]) for i in range(N_CHIPS)]
        tot = _rowwise("add_chips_" + n, lambda v: [(v[0].astype(F32) + v[1].astype(F32)) + v[2].astype(F32) + v[3].astype(F32)],
                       flat, [F32], 512)[0]
        halves.append(tot.reshape(t.shape[1:]))
    full = _join_layers(halves)
    grads = dict(zip(BIG, [f.reshape(weights[n].shape) for n, f in zip(BIG, full)]))
    for n in names:
        if n not in grads:
            grads[n] = reduced[n].reshape(weights[n].shape)

    delta, new_m, new_v = {}, {}, {}
    for n in names:
        w2 = weights[n] if weights[n].ndim > 1 else weights[n].reshape(1, -1)
        shape2 = w2.shape
        d_, m_, v_ = _adamw("adamw_" + n, w2, grads[n].reshape(shape2), mom1[n].reshape(shape2), mom2[n].reshape(shape2))
        delta[n], new_m[n], new_v[n] = (t.reshape(weights[n].shape) for t in (d_, m_, v_))

    loss = reduced["loss"].reshape(())
    return (loss, grad_x.reshape(x_shape), *[grads[n] for n in names], *[delta[n] for n in names],
            *[new_m[n] for n in names], *[new_v[n] for n in names])
```

```python
import functools
import math

import numpy as np
import jax
import jax.numpy as jnp
from jax import lax
from jax.experimental import pallas as pl
from jax.experimental.pallas import tpu as pltpu

F32 = jnp.float32
BF16 = jnp.bfloat16

S = 2048
D = 1024
DEPTH = 2
HD = 64
BLK = 128
NQB = S // BLK
A_GROUPS = ((128, 1), (512, 4), (2048, 16))
N_BAND_Q = 20
N_BAND_KV = 14
N_A = 12
C_HEADS = 4
NUM_BUCKETS = 32
MAX_DISTANCE = 2048
D_FF = 4096
IN_COLS = 6912
IN_SHARD = IN_COLS // 4
OFF_GATE = 3840
EPS = 1e-6
SCALE = HD ** -0.5
NEG = -1e30
N_CHIPS = 4
N_DEV = 8

ADAM_LR = 0.001
ADAM_B1 = 0.9
ADAM_B2 = 0.999
ADAM_EPS = 1e-08
ADAM_WD = 0.01
ADAM_STEP = 10

VMEM_LIMIT = 56 * 1024 * 1024

NN = (((1,), (0,)), ((), ()))
NT = (((1,), (1,)), ((), ()))
TN = (((0,), (0,)), ((), ()))


def _dot(a, b, dims):
    return lax.dot_general(a, b, dims, preferred_element_type=F32)


def _params(sem):
    return pltpu.CompilerParams(dimension_semantics=sem, vmem_limit_bytes=VMEM_LIMIT)


def _matmul(name, a, b, out_shape, out_dtype, grid, a_spec, b_spec, o_spec, dims, acc_shape, into=None):
    nk = grid[-1]

    def body(a_ref, b_ref, *rest):
        o_ref, scratch = (rest[1], rest[2:]) if into is not None else (rest[0], rest[1:])
        part = _dot(a_ref[...].astype(BF16), b_ref[...].astype(BF16), dims)
        if nk == 1:
            o_ref[...] = part.astype(o_ref.dtype)
            return
        acc_ref, = scratch
        k = pl.program_id(len(grid) - 1)

        @pl.when(k == 0)
        def _():
            acc_ref[...] = part

        @pl.when(k > 0)
        def _():
            acc_ref[...] += part

        @pl.when(k == nk - 1)
        def _():
            o_ref[...] = acc_ref[...].astype(o_ref.dtype)

    scratch = [] if nk == 1 else [pltpu.VMEM(acc_shape, F32)]
    sem = ("parallel",) * (len(grid) - 1) + ("arbitrary",)
    ins, in_specs, aliases = [a, b], [a_spec, b_spec], {}
    if into is not None:
        ins, in_specs, aliases = ins + [into], in_specs + [ANY], {2: 0}
    return pl.pallas_call(
        body, name=name, grid=grid, in_specs=in_specs, out_specs=o_spec,
        out_shape=jax.ShapeDtypeStruct(out_shape, out_dtype), scratch_shapes=scratch,
        input_output_aliases=aliases, compiler_params=_params(sem))(*ins)


def _mm_tn_stacked(name, a, b, layer, into, row_sharded, tm=512, tn=512, tk=1024):
    k, m = a.shape
    n = b.shape[1]
    m4, n4 = (m // N_CHIPS, n) if row_sharded else (m, n // N_CHIPS)
    tm, tn, tk = min(tm, m4), min(tn, n4), min(tk, k)
    per_m, per_n = m4 // tm, n4 // tn
    if row_sharded:
        o_map = lambda i, j, l: (layer, i // per_m, i % per_m, j)
    else:
        o_map = lambda i, j, l: (layer, j // per_n, i, j % per_n)
    return _matmul(name, a, b, (DEPTH, N_CHIPS, m4, n4), BF16, (m // tm, n // tn, k // tk),
                   pl.BlockSpec((tk, tm), lambda i, j, l: (l, i)),
                   pl.BlockSpec((tk, tn), lambda i, j, l: (l, j)),
                   pl.BlockSpec((None, None, tm, tn), o_map), TN, (tm, tn), into=into)


def _mm_nn(name, a, b, out_dtype, tm=512, tn=512, tk=1024):
    m, k = a.shape
    n = b.shape[1]
    tm, tn, tk = min(tm, m), min(tn, n), min(tk, k)
    return _matmul(name, a, b, (m, n), out_dtype, (m // tm, n // tn, k // tk),
                   pl.BlockSpec((tm, tk), lambda i, j, l: (i, l)),
                   pl.BlockSpec((tk, tn), lambda i, j, l: (l, j)),
                   pl.BlockSpec((tm, tn), lambda i, j, l: (i, j)), NN, (tm, tn))


def _mm_nt(name, a, b, out_dtype, tm=512, tn=512, tk=1024):
    m, k = a.shape
    n = b.shape[0]
    tm, tn, tk = min(tm, m), min(tn, n), min(tk, k)
    return _matmul(name, a, b, (m, n), out_dtype, (m // tm, n // tn, k // tk),
                   pl.BlockSpec((tm, tk), lambda i, j, l: (i, l)),
                   pl.BlockSpec((tn, tk), lambda i, j, l: (j, l)),
                   pl.BlockSpec((tm, tn), lambda i, j, l: (i, j)), NT, (tm, tn))


def _mm_tn(name, a, b, out_dtype, tm=512, tn=512, tk=1024):
    k, m = a.shape
    n = b.shape[1]
    tm, tn, tk = min(tm, m), min(tn, n), min(tk, k)
    return _matmul(name, a, b, (m, n), out_dtype, (m // tm, n // tn, k // tk),
                   pl.BlockSpec((tk, tm), lambda i, j, l: (l, i)),
                   pl.BlockSpec((tk, tn), lambda i, j, l: (l, j)),
                   pl.BlockSpec((tm, tn), lambda i, j, l: (i, j)), TN, (tm, tn))


def _proj_in(h, w_in_g, layer):
    tm = 512
    return _matmul("proj_in", h, w_in_g, (N_CHIPS, S, IN_SHARD), F32, (N_CHIPS, S // tm, 1),
                   pl.BlockSpec((tm, D), lambda j, i, l: (i, 0)),
                   pl.BlockSpec((None, None, D, IN_SHARD), lambda j, i, l: (j, layer, 0, 0)),
                   pl.BlockSpec((None, tm, IN_SHARD), lambda j, i, l: (j, i, 0)), NN, (tm, IN_SHARD))


def _proj_in_dx(dproj_s, w_in_g, layer):
    tm = 512
    return _matmul("proj_in_dx", dproj_s, w_in_g, (S, D), F32, (S // tm, N_CHIPS),
                   pl.BlockSpec((None, tm, IN_SHARD), lambda i, j: (j, i, 0)),
                   pl.BlockSpec((None, None, D, IN_SHARD), lambda i, j: (j, layer, 0, 0)),
                   pl.BlockSpec((tm, D), lambda i, j: (i, 0)), NT, (tm, D))


def _proj_in_dw(h, dproj_s, layer, into):
    tm = 512
    return _matmul("proj_in_dw", h, dproj_s, (DEPTH, N_CHIPS, D, IN_SHARD), BF16, (N_CHIPS, D // tm, 1),
                   pl.BlockSpec((S, tm), lambda j, i, l: (0, i)),
                   pl.BlockSpec((None, S, IN_SHARD), lambda j, i, l: (j, 0, 0)),
                   pl.BlockSpec((None, None, tm, IN_SHARD), lambda j, i, l: (layer, j, i, 0)), TN, (tm, IN_SHARD),
                   into=into)


TR = 256


def _row_spec(width=D):
    return pl.BlockSpec((TR, width), lambda i: (i, 0))


def _vec_spec(width=D):
    return pl.BlockSpec((1, width), lambda i: (0, 0))


def _rms(x, g):
    r = lax.rsqrt(jnp.mean(x * x, axis=-1, keepdims=True) + EPS)
    return x * r * g


def _rms_fwd(name, x, g):
    def body(x_ref, g_ref, h_ref):
        h_ref[...] = _rms(x_ref[...], g_ref[...]).astype(BF16)

    return pl.pallas_call(
        body, name=name, grid=(S // TR,), in_specs=[_row_spec(), _vec_spec()], out_specs=_row_spec(),
        out_shape=jax.ShapeDtypeStruct((S, D), BF16), compiler_params=_params(("parallel",)))(x, g)


def _post_pre_fwd(name, x, y, g_post, g_pre):
    has_pre = g_pre is not None

    def body(*refs):
        if has_pre:
            x_ref, y_ref, gp_ref, gn_ref, xn_ref, h_ref = refs
        else:
            x_ref, y_ref, gp_ref, xn_ref = refs
        xn = x_ref[...] + _rms(y_ref[...], gp_ref[...])
        xn_ref[...] = xn
        if has_pre:
            h_ref[...] = _rms(xn, gn_ref[...]).astype(BF16)

    ins = [x, y, g_post] + ([g_pre] if has_pre else [])
    in_specs = [_row_spec(), _row_spec(), _vec_spec()] + ([_vec_spec()] if has_pre else [])
    out_shape = [jax.ShapeDtypeStruct((S, D), F32)] + ([jax.ShapeDtypeStruct((S, D), BF16)] if has_pre else [])
    out_specs = [_row_spec()] + ([_row_spec()] if has_pre else [])
    out = pl.pallas_call(
        body, name=name, grid=(S // TR,), in_specs=in_specs, out_specs=out_specs, out_shape=out_shape,
        compiler_params=_params(("parallel",)))(*ins)
    return out if has_pre else (out[0], None)


def _rms_bwd_math(x, g, dy):
    r = lax.rsqrt(jnp.mean(x * x, axis=-1, keepdims=True) + EPS)
    n = x * r
    dn = dy * g
    dx = r * (dn - n * jnp.mean(dn * n, axis=-1, keepdims=True))
    return dx, jnp.sum(dy * n, axis=0, keepdims=True)


def _norm_bwd(name, dres, pre=None, post=None):
    has_pre, has_post = pre is not None, post is not None

    def body(*refs):
        refs = list(refs)
        dres_ref = refs.pop(0)
        if has_pre:
            xn_ref, gn_ref, dh_ref = refs[:3]
            refs = refs[3:]
        if has_post:
            y_ref, gp_ref = refs[:2]
            refs = refs[2:]
        dxn_ref = refs.pop(0)
        dy_ref = refs.pop(0) if has_post else None
        dgn_ref = refs.pop(0) if has_pre else None
        dgp_ref = refs.pop(0) if has_post else None
        first = pl.program_id(0) == 0
        dxn = dres_ref[...]
        if has_pre:
            dx, dg = _rms_bwd_math(xn_ref[...], gn_ref[...], dh_ref[...])
            dxn = dxn + dx

            @pl.when(first)
            def _():
                dgn_ref[...] = dg

            @pl.when(jnp.logical_not(first))
            def _():
                dgn_ref[...] += dg
        dxn_ref[...] = dxn
        if has_post:
            dy, dg = _rms_bwd_math(y_ref[...], gp_ref[...], dxn)
            dy_ref[...] = dy.astype(BF16)

            @pl.when(first)
            def _():
                dgp_ref[...] = dg

            @pl.when(jnp.logical_not(first))
            def _():
                dgp_ref[...] += dg

    ins, in_specs = [dres], [_row_spec()]
    if has_pre:
        ins += list(pre)
        in_specs += [_row_spec(), _vec_spec(), _row_spec()]
    if has_post:
        ins += list(post)
        in_specs += [_row_spec(), _vec_spec()]
    out_shape, out_specs = [jax.ShapeDtypeStruct((S, D), F32)], [_row_spec()]
    if has_post:
        out_shape.append(jax.ShapeDtypeStruct((S, D), BF16))
        out_specs.append(_row_spec())
    for _ in range(int(has_pre) + int(has_post)):
        out_shape.append(jax.ShapeDtypeStruct((1, D), F32))
        out_specs.append(_vec_spec())
    out = list(pl.pallas_call(
        body, name=name, grid=(S // TR,), in_specs=in_specs, out_specs=out_specs, out_shape=out_shape,
        compiler_params=_params(("arbitrary",)))(*ins))
    dxn = out.pop(0)
    dy = out.pop(0) if has_post else None
    dgn = out.pop(0) if has_pre else None
    dgp = out.pop(0) if has_post else None
    return dxn, dy, dgn, dgp


def _loss_kernel(y, target):
    def body(y_ref, t_ref, loss_ref, dy_ref):
        e = y_ref[...] - t_ref[...]
        dy_ref[...] = e * (1.0 / D)
        part = jnp.zeros((1, 128), F32) + 0.5 * jnp.sum(jnp.mean(e * e, axis=-1, keepdims=True))

        @pl.when(pl.program_id(0) == 0)
        def _():
            loss_ref[...] = part

        @pl.when(pl.program_id(0) > 0)
        def _():
            loss_ref[...] += part

    return pl.pallas_call(
        body, name="loss", grid=(S // TR,), in_specs=[_row_spec(), _row_spec()],
        out_specs=[_vec_spec(128), _row_spec()],
        out_shape=[jax.ShapeDtypeStruct((1, 128), F32), jax.ShapeDtypeStruct((S, D), F32)],
        compiler_params=_params(("arbitrary",)))(y, target)


def _t5_bucket_np(dist):
    max_exact = NUM_BUCKETS // 2
    nf = np.maximum(dist, 1).astype(np.float32)
    large = max_exact + (np.log(nf / max_exact) / np.float32(math.log(MAX_DISTANCE / max_exact))
                         * (NUM_BUCKETS - max_exact)).astype(np.int32)
    large = np.minimum(large, NUM_BUCKETS - 1)
    return np.where(dist < max_exact, dist, large).astype(np.int32)


def _bucket_maps():
    a = np.arange(BLK)[:, None]
    b = np.arange(2 * BLK)[None, :]
    dist = np.maximum(a + BLK - b, 0)
    maps = [_t5_bucket_np(dist * d) for _, d in A_GROUPS] + [_t5_bucket_np(dist)]
    return np.stack(maps).astype(np.int32)


def _band_head_scalars(h):
    is_b = h >= N_A
    bpc = jnp.where(h < 4, 16, jnp.where(h < 8, 4, jnp.where(h < N_A, 1, 16)))
    maxd = jnp.where(is_b, BLK - 1, BLK)
    return bpc, maxd


def _kv_head(h):
    return jnp.where(h < N_A, h, N_A + (h - N_A) // 4)


def _bias_map(h):
    return jnp.minimum(h // 4, 3)


def _band_specs():
    q_spec = pl.BlockSpec((None, BLK, HD), lambda h, i: (h, i, 0))
    kp_spec = pl.BlockSpec((None, BLK, HD), lambda h, i: (_kv_head(h), jnp.maximum(i - 1, 0), 0))
    kc_spec = pl.BlockSpec((None, BLK, HD), lambda h, i: (_kv_head(h), i, 0))
    col_spec = pl.BlockSpec((None, BLK, 1), lambda h, i: (h, i, 0))
    bidx_spec = pl.BlockSpec((None, BLK, 2 * BLK), lambda h, i: (_bias_map(h), 0, 0))
    smem = pl.BlockSpec(memory_space=pltpu.SMEM)
    return q_spec, kp_spec, kc_spec, col_spec, bidx_spec, smem


def _band_bias(tab_ref, bidx_ref, h):
    bi = bidx_ref[...]
    bias = jnp.zeros((BLK, 2 * BLK), F32)
    for kk in range(NUM_BUCKETS):
        bias = jnp.where(bi == kk, tab_ref[kk, h], bias)
    return bias


def _band_scores(q_ref, kp_ref, kc_ref, bias_ref, h, i):
    bpc, maxd = _band_head_scalars(h)
    q = (q_ref[...] * SCALE).astype(BF16)
    sp = _dot(q, kp_ref[...].astype(BF16), NT) + bias_ref[:, :BLK]
    sc = _dot(q, kc_ref[...].astype(BF16), NT) + bias_ref[:, BLK:]
    a = lax.broadcasted_iota(jnp.int32, (BLK, BLK), 0)
    b = lax.broadcasted_iota(jnp.int32, (BLK, BLK), 1)
    has_prev = (i % bpc) != 0
    mask_p = jnp.logical_and(a + BLK - b <= maxd, has_prev)
    mask_c = a >= b
    return jnp.where(mask_p, sp, NEG), jnp.where(mask_c, sc, NEG), mask_p, mask_c


def _band_fwd(q, k, v, bidx, tab, sinks):
    q_spec, kp_spec, kc_spec, col_spec, bidx_spec, smem = _band_specs()

    def body(tab_ref, sink_ref, q_ref, kp_ref, kc_ref, vp_ref, vc_ref, bidx_ref, o_ref, lse_ref, bias_ref):
        h, i = pl.program_id(0), pl.program_id(1)

        @pl.when(i == 0)
        def _():
            bias_ref[...] = _band_bias(tab_ref, bidx_ref, h)

        sp, sc, _, _ = _band_scores(q_ref, kp_ref, kc_ref, bias_ref, h, i)
        m = jnp.maximum(jnp.max(sp, axis=1, keepdims=True), jnp.max(sc, axis=1, keepdims=True))
        pp, pc = jnp.exp(sp - m), jnp.exp(sc - m)
        l = jnp.sum(pp, axis=1, keepdims=True) + jnp.sum(pc, axis=1, keepdims=True)
        num = _dot(pp.astype(BF16), vp_ref[...].astype(BF16), NN) + _dot(pc.astype(BF16), vc_ref[...].astype(BF16), NN)
        lse = m + jnp.log(l)
        sig = 1.0 / (1.0 + jnp.exp(sink_ref[h] - lse))
        o_ref[...] = num * (sig / l)
        lse_ref[...] = lse

    return pl.pallas_call(
        body, name="band_fwd", grid=(N_BAND_Q, NQB),
        in_specs=[smem, smem, q_spec, kp_spec, kc_spec, kp_spec, kc_spec, bidx_spec],
        out_specs=[q_spec, col_spec],
        out_shape=[jax.ShapeDtypeStruct((N_BAND_Q, S, HD), F32), jax.ShapeDtypeStruct((N_BAND_Q, S, 1), F32)],
        scratch_shapes=[pltpu.VMEM((BLK, 2 * BLK), F32)],
        compiler_params=_params(("parallel", "arbitrary")))(tab, sinks, q, k, k, v, v, bidx)


def _band_bwd(q, k, v, bidx, tab, sinks, o, lse, do, dlse, stats_in):
    q_spec, kp_spec, kc_spec, col_spec, bidx_spec, smem = _band_specs()
    kv_full = pl.BlockSpec((None, S, HD), lambda h, i: (_kv_head(h), 0, 0))
    stat_spec = pl.BlockSpec((None, 8, 128), lambda h, i: (h, 0, 0))

    def body(tab_ref, sink_ref, q_ref, kp_ref, kc_ref, vp_ref, vc_ref, bidx_ref, o_ref, lse_ref, do_ref, dlse_ref,
             sin_ref, dq_ref, dk_ref, dv_ref, stat_ref, bias_ref, dsacc_ref, sk_ref):
        h, i = pl.program_id(0), pl.program_id(1)

        @pl.when(i == 0)
        def _():
            bias_ref[...] = _band_bias(tab_ref, bidx_ref, h)
            dsacc_ref[...] = jnp.zeros_like(dsacc_ref)
            sk_ref[...] = jnp.zeros_like(sk_ref)

        first_of_kv = jnp.logical_or(h < N_A, (h - N_A) % 4 == 0)

        @pl.when(jnp.logical_and(i == 0, first_of_kv))
        def _():
            dk_ref[...] = jnp.zeros_like(dk_ref)
            dv_ref[...] = jnp.zeros_like(dv_ref)

        sp, sc, mask_p, mask_c = _band_scores(q_ref, kp_ref, kc_ref, bias_ref, h, i)
        lse = lse_ref[...]
        pp = jnp.where(mask_p, jnp.exp(sp - lse), 0.0)
        pc = jnp.where(mask_c, jnp.exp(sc - lse), 0.0)
        sig = 1.0 / (1.0 + jnp.exp(sink_ref[h] - lse))
        do = do_ref[...]
        delta = jnp.sum(do * o_ref[...], axis=1, keepdims=True)
        dob = do.astype(BF16)
        dpp = _dot(dob, vp_ref[...].astype(BF16), NT)
        dpc = _dot(dob, vc_ref[...].astype(BF16), NT)
        dlse = dlse_ref[...]
        dsp = pp * (sig * (dpp - delta) + dlse)
        dsc = pc * (sig * (dpc - delta) + dlse)
        dspb, dscb = dsp.astype(BF16), dsc.astype(BF16)
        dq_ref[...] = SCALE * (_dot(dspb, kp_ref[...].astype(BF16), NN) + _dot(dscb, kc_ref[...].astype(BF16), NN))
        qb = q_ref[...].astype(BF16)
        sdo = (sig * do).astype(BF16)
        cur = pl.ds(pl.multiple_of(i * BLK, BLK), BLK)
        prev = pl.ds(pl.multiple_of(jnp.maximum(i - 1, 0) * BLK, BLK), BLK)
        dk_ref[cur, :] += SCALE * _dot(dscb, qb, TN)
        dk_ref[prev, :] += SCALE * _dot(dspb, qb, TN)
        dv_ref[cur, :] += _dot(pc.astype(BF16), sdo, TN)
        dv_ref[prev, :] += _dot(pp.astype(BF16), sdo, TN)
        dsacc_ref[:, :BLK] += dsp
        dsacc_ref[:, BLK:] += dsc
        sk_ref[...] += jnp.zeros((8, 128), F32) + jnp.sum(-delta * (1.0 - sig))

        @pl.when(i == NQB - 1)
        def _():
            acc = dsacc_ref[...]
            bi = bidx_ref[...]
            lane = lax.broadcasted_iota(jnp.int32, (8, 128), 1)
            sub = lax.broadcasted_iota(jnp.int32, (8, 128), 0)
            row = jnp.where(jnp.logical_and(sub == 1, lane == 0), sk_ref[...], 0.0)
            for kk in range(NUM_BUCKETS):
                tot = jnp.sum(jnp.where(bi == kk, acc, 0.0))
                row = jnp.where(jnp.logical_and(sub == 0, lane == kk), tot, row)
            stat_ref[...] = row + jnp.where(sub == 0, sin_ref[...], 0.0)

    return pl.pallas_call(
        body, name="band_bwd", grid=(N_BAND_Q, NQB),
        in_specs=[smem, smem, q_spec, kp_spec, kc_spec, kp_spec, kc_spec, bidx_spec, q_spec, col_spec, q_spec, col_spec,
                  stat_spec],
        out_specs=[q_spec, kv_full, kv_full, stat_spec],
        out_shape=[jax.ShapeDtypeStruct((N_BAND_Q, S, HD), F32), jax.ShapeDtypeStruct((N_BAND_KV, S, HD), F32),
                   jax.ShapeDtypeStruct((N_BAND_KV, S, HD), F32), jax.ShapeDtypeStruct((N_BAND_Q, 8, 128), F32)],
        scratch_shapes=[pltpu.VMEM((BLK, 2 * BLK), F32), pltpu.VMEM((BLK, 2 * BLK), F32), pltpu.VMEM((8, 128), F32)],
        compiler_params=_params(("arbitrary", "arbitrary")))(tab, sinks, q, k, k, v, v, bidx, o, lse, do, dlse, stats_in)


def _comb_specs():
    o_spec = pl.BlockSpec((3, None, S, HD), lambda h: (0, h, 0, 0))
    l_spec = pl.BlockSpec((3, None, S, 1), lambda h: (0, h, 0, 0))
    h_spec = pl.BlockSpec((None, S, HD), lambda h: (h, 0, 0))
    return o_spec, l_spec, h_spec


def _comb_weights(l_ref):
    l0, l1, l2 = l_ref[0], l_ref[1], l_ref[2]
    m = jnp.maximum(jnp.maximum(l0, l1), l2)
    e0, e1, e2 = jnp.exp(l0 - m), jnp.exp(l1 - m), jnp.exp(l2 - m)
    inv = 1.0 / (e0 + e1 + e2)
    return e0 * inv, e1 * inv, e2 * inv


def _comb_fwd(o_g, lse_g):
    o_spec, l_spec, h_spec = _comb_specs()

    def body(o_ref, l_ref, out_ref):
        w0, w1, w2 = _comb_weights(l_ref)
        out_ref[...] = w0 * o_ref[0] + w1 * o_ref[1] + w2 * o_ref[2]

    return pl.pallas_call(
        body, name="comb_fwd", grid=(4,), in_specs=[o_spec, l_spec], out_specs=h_spec,
        out_shape=jax.ShapeDtypeStruct((4, S, HD), F32), compiler_params=_params(("parallel",)))(o_g, lse_g)


def _comb_bwd(o_g, lse_g, dout):
    o_spec, l_spec, h_spec = _comb_specs()

    def body(o_ref, l_ref, d_ref, do_ref, dl_ref):
        ws = _comb_weights(l_ref)
        d = d_ref[...]
        dots = [jnp.sum(d * o_ref[g], axis=1, keepdims=True) for g in range(3)]
        tot = ws[0] * dots[0] + ws[1] * dots[1] + ws[2] * dots[2]
        for g in range(3):
            do_ref[g] = ws[g] * d
            dl_ref[g] = ws[g] * (dots[g] - tot)

    return pl.pallas_call(
        body, name="comb_bwd", grid=(4,), in_specs=[o_spec, l_spec, h_spec], out_specs=[o_spec, l_spec],
        out_shape=[jax.ShapeDtypeStruct((3, 4, S, HD), F32), jax.ShapeDtypeStruct((3, 4, S, 1), F32)],
        compiler_params=_params(("parallel",)))(o_g, lse_g, dout)


def _split3(x):
    hi = x.astype(BF16)
    r = x - hi.astype(F32)
    mid = r.astype(BF16)
    lo = (r - mid.astype(F32)).astype(BF16)
    return hi, mid, lo


def _tri_sum(x, tri):
    hi, mid, lo = _split3(x)
    return _dot(hi, tri, NN) + _dot(mid, tri, NN) + _dot(lo, tri, NN)


def _tri(strict_upper):
    r = lax.broadcasted_iota(jnp.int32, (BLK, BLK), 0)
    c = lax.broadcasted_iota(jnp.int32, (BLK, BLK), 1)
    return jnp.where(r > c if strict_upper else r < c, 1.0, 0.0).astype(BF16)


def _sb_block(q, k_ref, j, i):
    kj = k_ref[pl.ds(pl.multiple_of(j * BLK, BLK), BLK), :].astype(BF16)
    z = _dot(q, kj, NT)
    t = lax.broadcasted_iota(jnp.int32, (BLK, BLK), 0) + i * BLK
    s = lax.broadcasted_iota(jnp.int32, (BLK, BLK), 1) + j * BLK
    before = s < t
    lsp = jnp.minimum(z, 0.0) - jnp.log(1.0 + jnp.exp(-jnp.abs(z)))
    lk = jnp.where(before, lsp - z, 0.0)
    return before, lsp, lk


def _sb_specs():
    q_spec = pl.BlockSpec((None, BLK, HD), lambda h, i: (h, i, 0))
    full = pl.BlockSpec((None, S, HD), lambda h, i: (h, 0, 0))
    return q_spec, full


def _sb_fwd(q, k, v):
    q_spec, full = _sb_specs()

    def body(q_ref, k_ref, v_ref, o_ref):
        i = pl.program_id(1)
        q = (q_ref[...] * SCALE).astype(BF16)
        suffix = _tri(True)

        def step(n, carry):
            acc, rest = carry
            j = i - n
            before, lsp, lk = _sb_block(q, k_ref, j, i)
            log_rest = _tri_sum(lk, suffix) + rest
            w = jnp.where(before, jnp.exp(lsp + log_rest), 0.0)
            vj = v_ref[pl.ds(pl.multiple_of(j * BLK, BLK), BLK), :].astype(BF16)
            return acc + _dot(w.astype(BF16), vj, NN), rest + jnp.sum(lk, axis=1, keepdims=True)

        acc, _ = lax.fori_loop(0, i + 1, step, (jnp.zeros((BLK, HD), F32), jnp.zeros((BLK, 1), F32)))
        o_ref[...] = acc

    return pl.pallas_call(
        body, name="sb_fwd", grid=(C_HEADS, NQB), in_specs=[q_spec, full, full], out_specs=q_spec,
        out_shape=jax.ShapeDtypeStruct((C_HEADS, S, HD), F32),
        compiler_params=_params(("parallel", "arbitrary")))(q, k, v)


def _sb_bwd(q, k, v, do):
    q_spec, full = _sb_specs()

    def body(q_ref, k_ref, v_ref, do_ref, dq_ref, dk_ref, dv_ref, rest_ref):
        i = pl.program_id(1)

        @pl.when(i == 0)
        def _():
            dk_ref[...] = jnp.zeros_like(dk_ref)
            dv_ref[...] = jnp.zeros_like(dv_ref)

        qf = q_ref[...]
        q = (qf * SCALE).astype(BF16)
        qb = qf.astype(BF16)
        dob = do_ref[...].astype(BF16)
        suffix = _tri(True)
        prefix = _tri(False)

        def sweep_rest(n, rest):
            j = i - n
            _, _, lk = _sb_block(q, k_ref, j, i)
            rest_ref[j] = rest
            return rest + jnp.sum(lk, axis=1, keepdims=True)

        lax.fori_loop(0, i + 1, sweep_rest, jnp.zeros((BLK, 1), F32))

        def step(j, carry):
            dq, left = carry
            before, lsp, lk = _sb_block(q, k_ref, j, i)
            w = jnp.where(before, jnp.exp(lsp + _tri_sum(lk, suffix) + rest_ref[j]), 0.0)
            rows = pl.ds(pl.multiple_of(j * BLK, BLK), BLK)
            vj = v_ref[rows, :].astype(BF16)
            kj = k_ref[rows, :].astype(BF16)
            g = w * _dot(dob, vj, NT)
            h_in = _tri_sum(g, prefix) + left
            beta = jnp.exp(lsp)
            dz = jnp.where(before, g * (1.0 - beta) - h_in * beta, 0.0)
            dzb = dz.astype(BF16)
            dk_ref[rows, :] += SCALE * _dot(dzb, qb, TN)
            dv_ref[rows, :] += _dot(w.astype(BF16), dob, TN)
            return dq + _dot(dzb, kj, NN), left + jnp.sum(g, axis=1, keepdims=True)

        dq, _ = lax.fori_loop(0, i + 1, step, (jnp.zeros((BLK, HD), F32), jnp.zeros((BLK, 1), F32)))
        dq_ref[...] = SCALE * dq

    shape = jax.ShapeDtypeStruct((C_HEADS, S, HD), F32)
    return pl.pallas_call(
        body, name="sb_bwd", grid=(C_HEADS, NQB), in_specs=[q_spec, full, full, q_spec],
        out_specs=[q_spec, full, full], out_shape=[shape, shape, shape],
        scratch_shapes=[pltpu.VMEM((NQB, BLK, 1), F32)],
        compiler_params=_params(("arbitrary", "arbitrary")))(q, k, v, do)


TG = 256
GATE_BLK0 = OFF_GATE // TG


def _gate_specs():
    grid = (D // TG, S // TG)
    p_specs = [pl.BlockSpec((TG, TG), functools.partial(lambda c, r, br: (r, GATE_BLK0 + br * (D // TG) + c), br=br))
               for br in range(3)]
    b_spec = pl.BlockSpec((3, TG), lambda c, r: (0, c))
    t_spec = pl.BlockSpec((TG, TG), lambda c, r: (r, c))
    e_spec = pl.BlockSpec((3, TG, TG), lambda c, r: (0, r, c))
    return grid, p_specs, b_spec, t_spec, e_spec


def _sigmoid(x):
    return 1.0 / (1.0 + jnp.exp(-x))


def _three_rows(rows):
    sub = lax.broadcasted_iota(jnp.int32, (3, rows[0].shape[1]), 0)
    return jnp.where(sub == 0, rows[0], jnp.where(sub == 1, rows[1], rows[2]))


def _gate_fwd(proj, b_gate, br):
    grid, p_specs, b_spec, t_spec, e_spec = _gate_specs()

    def body(p0, p1, p2, b_ref, br_ref, out_ref):
        acc = jnp.zeros((TG, TG), F32)
        for n, p in enumerate((p0, p1, p2)):
            acc += _sigmoid(p[...] + b_ref[n:n + 1, :]) * br_ref[n]
        out_ref[...] = acc.astype(BF16)

    return pl.pallas_call(
        body, name="gate_fwd", grid=grid, in_specs=p_specs + [b_spec, e_spec], out_specs=t_spec,
        out_shape=jax.ShapeDtypeStruct((S, D), BF16),
        compiler_params=_params(("parallel", "parallel")))(proj, proj, proj, b_gate, br)


def _gate_bwd(proj, b_gate, br, dmerged):
    grid, p_specs, b_spec, t_spec, e_spec = _gate_specs()

    def body(p0, p1, p2, b_ref, br_ref, dm_ref, e_ref, dg_ref, db_ref):
        dm = dm_ref[...]
        rows = []
        for n, p in enumerate((p0, p1, p2)):
            g = _sigmoid(p[...] + b_ref[n:n + 1, :])
            e_ref[n] = (dm * g).astype(BF16)
            dpre = dm * br_ref[n] * g * (1.0 - g)
            dg_ref[n] = dpre.astype(BF16)
            rows.append(jnp.sum(dpre, axis=0, keepdims=True))
        db = _three_rows(rows)

        @pl.when(pl.program_id(1) == 0)
        def _():
            db_ref[...] = db

        @pl.when(pl.program_id(1) > 0)
        def _():
            db_ref[...] += db

    return pl.pallas_call(
        body, name="gate_bwd", grid=grid, in_specs=p_specs + [b_spec, e_spec, t_spec],
        out_specs=[e_spec, e_spec, b_spec],
        out_shape=[jax.ShapeDtypeStruct((3, S, D), BF16), jax.ShapeDtypeStruct((3, S, D), BF16),
                   jax.ShapeDtypeStruct((3, D), F32)],
        compiler_params=_params(("parallel", "arbitrary")))(proj, proj, proj, b_gate, br, dmerged)


TC = 256
N_FF_BLK = D_FF // TC
GELU_C = math.sqrt(2.0 / math.pi)


def _shift_down(x, n):
    rows = lax.broadcasted_iota(jnp.int32, x.shape, 0)
    return jnp.where(rows >= n, pltpu.roll(x, n, axis=0), 0.0)


def _shift_up(x, n):
    rows = lax.broadcasted_iota(jnp.int32, x.shape, 0)
    return jnp.where(rows < x.shape[0] - n, pltpu.roll(x, x.shape[0] - n, axis=0), 0.0)


def _conv(u, w, b):
    return w[2:3, :] * u + w[1:2, :] * _shift_down(u, 1) + w[0:1, :] * _shift_down(u, 2) + b


def _gelu_parts(x):
    inner = GELU_C * (x + 0.044715 * x * x * x)
    t = jnp.tanh(inner)
    gelu = 0.5 * x * (1.0 + t)
    dgelu = 0.5 * (1.0 + t) + 0.5 * x * (1.0 - t * t) * GELU_C * (1.0 + 3 * 0.044715 * x * x)
    return gelu, dgelu


def _conv_specs():
    ug = pl.BlockSpec((S, TC), lambda c: (0, c))
    uv = pl.BlockSpec((S, TC), lambda c: (0, N_FF_BLK + c))
    wg = pl.BlockSpec((3, TC), lambda c: (0, c))
    wv = pl.BlockSpec((3, TC), lambda c: (0, N_FF_BLK + c))
    bg = pl.BlockSpec((1, TC), lambda c: (0, c))
    bv = pl.BlockSpec((1, TC), lambda c: (0, N_FF_BLK + c))
    return ug, uv, wg, wv, bg, bv


def _conv_fwd(u, conv_w, conv_b):
    ug, uv, wg, wv, bg, bv = _conv_specs()

    def body(ug_ref, uv_ref, wg_ref, wv_ref, bg_ref, bv_ref, a_ref):
        gc = _conv(ug_ref[...], wg_ref[...], bg_ref[...])
        vc = _conv(uv_ref[...], wv_ref[...], bv_ref[...])
        a_ref[...] = (_gelu_parts(gc)[0] * vc).astype(BF16)

    return pl.pallas_call(
        body, name="conv_fwd", grid=(N_FF_BLK,), in_specs=[ug, uv, wg, wv, bg, bv], out_specs=ug,
        out_shape=jax.ShapeDtypeStruct((S, D_FF), BF16),
        compiler_params=_params(("parallel",)))(u, u, conv_w, conv_w, conv_b, conv_b)


def _conv_bwd(u, conv_w, conv_b, da):
    ug, uv, wg, wv, bg, bv = _conv_specs()

    def back(duc, u, w):
        du = w[2:3, :] * duc + w[1:2, :] * _shift_up(duc, 1) + w[0:1, :] * _shift_up(duc, 2)
        dw = _three_rows([jnp.sum(duc * _shift_down(u, 2), axis=0, keepdims=True),
                          jnp.sum(duc * _shift_down(u, 1), axis=0, keepdims=True),
                          jnp.sum(duc * u, axis=0, keepdims=True)])
        return du, dw, jnp.sum(duc, axis=0, keepdims=True)

    def body(ug_ref, uv_ref, wg_ref, wv_ref, bg_ref, bv_ref, da_ref, dug_ref, duv_ref, dwg_ref, dwv_ref, dbg_ref, dbv_ref):
        u_g, u_v = ug_ref[...], uv_ref[...]
        gc = _conv(u_g, wg_ref[...], bg_ref[...])
        vc = _conv(u_v, wv_ref[...], bv_ref[...])
        gelu, dgelu = _gelu_parts(gc)
        da = da_ref[...]
        du, dw, db = back(da * vc * dgelu, u_g, wg_ref[...])
        dug_ref[...] = du.astype(BF16)
        dwg_ref[...] = dw
        dbg_ref[...] = db
        du, dw, db = back(da * gelu, u_v, wv_ref[...])
        duv_ref[...] = du.astype(BF16)
        dwv_ref[...] = dw
        dbv_ref[...] = db

    outs = pl.pallas_call(
        body, name="conv_bwd", grid=(N_FF_BLK,), in_specs=[ug, uv, wg, wv, bg, bv, ug],
        out_specs=[ug, ug, wg, wg, bg, bg],
        out_shape=[jax.ShapeDtypeStruct((S, D_FF), BF16), jax.ShapeDtypeStruct((S, D_FF), BF16),
                   jax.ShapeDtypeStruct((3, D_FF), F32), jax.ShapeDtypeStruct((3, D_FF), F32),
                   jax.ShapeDtypeStruct((1, D_FF), F32), jax.ShapeDtypeStruct((1, D_FF), F32)],
        compiler_params=_params(("parallel",)))(u, u, conv_w, conv_w, conv_b, conv_b, da)
    return outs


def _rowwise(name, fn, ins, out_dtypes, rows_per_block):
    r, c = ins[0].shape
    tr = min(rows_per_block, r)
    n_in = len(ins)

    def body(*refs):
        outs = fn([ref[...] for ref in refs[:n_in]])
        for ref, val in zip(refs[n_in:], outs):
            ref[...] = val.astype(ref.dtype)

    spec = pl.BlockSpec((tr, c), lambda i: (i, 0))
    return pl.pallas_call(
        body, name=name, grid=(pl.cdiv(r, tr),), in_specs=[spec] * n_in, out_specs=[spec] * len(out_dtypes),
        out_shape=[jax.ShapeDtypeStruct((r, c), dt) for dt in out_dtypes],
        compiler_params=_params(("parallel",)))(*ins)


def _adamw_math(vals):
    w, g, m, v = vals
    m = ADAM_B1 * m + (1.0 - ADAM_B1) * g
    v = ADAM_B2 * v + (1.0 - ADAM_B2) * (g * g)
    m_hat = m / (1.0 - ADAM_B1 ** ADAM_STEP)
    v_hat = v / (1.0 - ADAM_B2 ** ADAM_STEP)
    delta = -ADAM_LR * (m_hat / (jnp.sqrt(v_hat) + ADAM_EPS) + ADAM_WD * w)
    return [delta, m, v]


def _adamw(name, w, g, m, v):
    shape = w.shape
    cols = shape[-1]
    flat = [t.reshape(-1, cols) for t in (w, g, m, v)]
    outs = _rowwise(name, _adamw_math, flat, [F32, F32, F32], 128)
    return [t.reshape(shape) for t in outs]


MESH = pl.DeviceIdType.MESH
ANY = pl.BlockSpec(memory_space=pl.ANY)


def _place():
    x, y, c = lax.axis_index("x"), lax.axis_index("y"), lax.axis_index("c")
    chips = [(1 - x, y), (x, 1 - y), (1 - x, 1 - y)]
    return x, y, c, chips


def _scalars(*vals):
    return jnp.stack([jnp.asarray(v, jnp.int32) for v in vals])


def _cast_into_slot(name, w, chip):
    _, k, n4 = w.shape
    tr = min(256, k)

    def body(chip_ref, w_ref, o_ref):
        o_ref[...] = w_ref[...].astype(BF16)

    return pl.pallas_call(
        body, name=name,
        grid_spec=pltpu.PrefetchScalarGridSpec(
            num_scalar_prefetch=1, grid=(DEPTH, k // tr),
            in_specs=[pl.BlockSpec((None, tr, n4), lambda l, i, s: (l, i, 0))],
            out_specs=pl.BlockSpec((None, None, tr, n4), lambda l, i, s: (s[0], l, i, 0))),
        out_shape=jax.ShapeDtypeStruct((N_CHIPS, DEPTH, k, n4), BF16),
        compiler_params=_params(("parallel", "parallel")))(_scalars(chip), w)


def _gather_weights(bufs, smalls):
    nb, n = len(bufs), len(bufs) + len(smalls)

    def body(*refs):
        ins, outs = refs[:n], refs[n:2 * n]
        local_sem, send_sem, recv_sem = refs[2 * n:]
        x, y, c, chips = _place()
        me = 2 * x + y
        sibling = (x, y, 1 - c)
        local = [pltpu.make_async_copy(ins[a], outs[a].at[me], local_sem.at[a - nb]) for a in range(nb, n)]
        for cp in local:
            cp.start()

        def over_ici(a, k, from_chip):
            return pltpu.make_async_remote_copy(
                src_ref=ins[a].at[me, c] if a < nb else ins[a].at[c], dst_ref=outs[a].at[from_chip, c],
                send_sem=send_sem.at[a, k], recv_sem=recv_sem.at[a, k],
                device_id=(*chips[k], c), device_id_type=MESH)

        def over_d2d(a, k, layer):
            rows = outs[a].at[2 * chips[k][0] + chips[k][1], layer]
            return pltpu.make_async_remote_copy(
                src_ref=rows, dst_ref=rows, send_sem=send_sem.at[a, 3 + k], recv_sem=recv_sem.at[a, 3 + k],
                device_id=sibling, device_id_type=MESH)

        sends = [over_ici(a, k, me) for a in range(n) for k in range(3)]
        for cp in sends:
            cp.start()
        passed = []
        for a in range(n):
            for k in range(3):
                over_ici(a, k, 2 * chips[k][0] + chips[k][1]).wait_recv()
                cp = over_d2d(a, k, c)
                cp.start()
                passed.append(cp)
        for a in range(n):
            for k in range(3):
                over_d2d(a, k, 1 - c).wait_recv()
        for cp in sends + passed:
            cp.wait_send()
        for cp in local:
            cp.wait()

    out_shape = [jax.ShapeDtypeStruct(b.shape, b.dtype) for b in bufs]
    out_shape += [jax.ShapeDtypeStruct((N_CHIPS,) + s.shape, s.dtype) for s in smalls]
    return pl.pallas_call(
        body, name="gather_weights", in_specs=[ANY] * n, out_specs=[ANY] * n, out_shape=out_shape,
        input_output_aliases={a: a for a in range(nb)},
        scratch_shapes=[pltpu.SemaphoreType.DMA((n - nb,)), pltpu.SemaphoreType.DMA((n, 6)),
                        pltpu.SemaphoreType.DMA((n, 6))],
    )(*bufs, *smalls)


def _swap_layers(grads):
    n = len(grads)

    def body(*refs):
        ins, got = refs[:n], refs[n:2 * n]
        send_sem, recv_sem = refs[2 * n:]
        x, y, c, _ = _place()
        sends = [pltpu.make_async_remote_copy(
            src_ref=ins[a].at[1 - c], dst_ref=got[a], send_sem=send_sem.at[a], recv_sem=recv_sem.at[a],
            device_id=(x, y, 1 - c), device_id_type=MESH) for a in range(n)]
        for cp in sends:
            cp.start()
        for cp in sends:
            cp.wait()

    return pl.pallas_call(
        body, name="swap_layers", in_specs=[ANY] * n, out_specs=[ANY] * n,
        out_shape=[jax.ShapeDtypeStruct(g.shape[1:], g.dtype) for g in grads],
        scratch_shapes=[pltpu.SemaphoreType.DMA((n,)), pltpu.SemaphoreType.DMA((n,))],
    )(*grads)


def _add_sibling(name, g, got, c):
    _, _, k, n4 = g.shape
    rows = N_CHIPS * k
    tr = min(512, rows)

    def body(c_ref, g_ref, got_ref, o_ref):
        o_ref[...] = (g_ref[...].astype(F32) + got_ref[...].astype(F32)).astype(BF16)

    out = pl.pallas_call(
        body, name=name,
        grid_spec=pltpu.PrefetchScalarGridSpec(
            num_scalar_prefetch=1, grid=(rows // tr,),
            in_specs=[pl.BlockSpec((None, tr, n4), lambda i, s: (s[0], i, 0)),
                      pl.BlockSpec((tr, n4), lambda i, s: (i, 0))],
            out_specs=pl.BlockSpec((tr, n4), lambda i, s: (i, 0))),
        out_shape=jax.ShapeDtypeStruct((rows, n4), BF16),
        compiler_params=_params(("parallel",)))(_scalars(c), g.reshape(DEPTH, rows, n4), got.reshape(rows, n4))
    return out.reshape(N_CHIPS, k, n4)


def _scatter_shards(parts):
    n = len(parts)

    def body(*refs):
        ins, outs = refs[:n], refs[n:2 * n]
        send_sem, recv_sem = refs[2 * n:]
        x, y, c, chips = _place()
        me = 2 * x + y
        sends = [pltpu.make_async_remote_copy(
            src_ref=ins[a].at[2 * chips[k][0] + chips[k][1]], dst_ref=outs[a].at[me],
            send_sem=send_sem.at[a, k], recv_sem=recv_sem.at[a, k],
            device_id=(*chips[k], c), device_id_type=MESH) for a in range(n) for k in range(3)]
        for cp in sends:
            cp.start()
        for a in range(n):
            for k in range(3):
                src_chip = 2 * chips[k][0] + chips[k][1]
                pltpu.make_async_remote_copy(
                    src_ref=ins[a].at[me], dst_ref=outs[a].at[src_chip],
                    send_sem=send_sem.at[a, k], recv_sem=recv_sem.at[a, k],
                    device_id=(x, y, c), device_id_type=MESH).wait_recv()
        for cp in sends:
            cp.wait_send()

    return pl.pallas_call(
        body, name="scatter_shards", in_specs=[ANY] * n, out_specs=[ANY] * n,
        out_shape=[jax.ShapeDtypeStruct(p.shape, p.dtype) for p in parts],
        scratch_shapes=[pltpu.SemaphoreType.DMA((n, 3)), pltpu.SemaphoreType.DMA((n, 3))],
    )(*parts)


def _add_chips(name, part, arrived, chip, c):
    _, k, n4 = part.shape
    tr = min(512, k)

    def body(s_ref, own_ref, a1_ref, a2_ref, a3_ref, o_ref):
        o_ref[...] = ((own_ref[...].astype(F32) + a1_ref[...].astype(F32)) + a2_ref[...].astype(F32)) + a3_ref[...].astype(F32)

    def pick(j):
        return pl.BlockSpec((None, tr, n4), lambda i, s: (s[j], i, 0))

    return pl.pallas_call(
        body, name=name,
        grid_spec=pltpu.PrefetchScalarGridSpec(
            num_scalar_prefetch=1, grid=(k // tr,), in_specs=[pick(0), pick(2), pick(3), pick(4)], out_specs=pick(1)),
        out_shape=jax.ShapeDtypeStruct((DEPTH, k, n4), F32),
        compiler_params=_params(("parallel",)))(
            _scalars(chip, c, (chip + 1) % N_CHIPS, (chip + 2) % N_CHIPS, (chip + 3) % N_CHIPS),
            part, arrived, arrived, arrived)


def _join_layers(bufs):
    n = len(bufs)

    def body(*refs):
        ins, outs = refs[:n], refs[n:2 * n]
        send_sem, recv_sem = refs[2 * n:]
        x, y, c, _ = _place()
        sends = [pltpu.make_async_remote_copy(
            src_ref=ins[a].at[c], dst_ref=outs[a].at[c], send_sem=send_sem.at[a], recv_sem=recv_sem.at[a],
            device_id=(x, y, 1 - c), device_id_type=MESH) for a in range(n)]
        for cp in sends:
            cp.start()
        for a in range(n):
            sends[a].wait_send()
            pltpu.make_async_remote_copy(
                src_ref=ins[a].at[c], dst_ref=outs[a].at[1 - c], send_sem=send_sem.at[a], recv_sem=recv_sem.at[a],
                device_id=(x, y, 1 - c), device_id_type=MESH).wait_recv()

    return pl.pallas_call(
        body, name="join_layers", in_specs=[ANY] * n, out_specs=[ANY] * n,
        out_shape=[jax.ShapeDtypeStruct(b.shape, b.dtype) for b in bufs],
        input_output_aliases={a: a for a in range(n)},
        scratch_shapes=[pltpu.SemaphoreType.DMA((n,)), pltpu.SemaphoreType.DMA((n,))],
    )(*bufs)


def _old_gather_weights(shards):
    n = len(shards)

    def body(*refs):
        ins, outs = refs[:n], refs[n:2 * n]
        local_sem, send_sem, recv_sem = refs[2 * n:]
        x, y, c, chips = _place()
        me = 2 * x + y
        sibling = (x, y, 1 - c)
        local = [pltpu.make_async_copy(ins[a], outs[a].at[me], local_sem.at[a]) for a in range(n)]
        for cp in local:
            cp.start()

        def over_ici(a, k, from_chip):
            return pltpu.make_async_remote_copy(
                src_ref=ins[a].at[c], dst_ref=outs[a].at[from_chip, c],
                send_sem=send_sem.at[a, k], recv_sem=recv_sem.at[a, k],
                device_id=(*chips[k], c), device_id_type=MESH)

        def over_d2d(a, k, layer):
            rows = outs[a].at[2 * chips[k][0] + chips[k][1], layer]
            return pltpu.make_async_remote_copy(
                src_ref=rows, dst_ref=rows, send_sem=send_sem.at[a, 3 + k], recv_sem=recv_sem.at[a, 3 + k],
                device_id=sibling, device_id_type=MESH)

        sends = [over_ici(a, k, me) for a in range(n) for k in range(3)]
        for cp in sends:
            cp.start()
        passed = []
        for a in range(n):
            for k in range(3):
                over_ici(a, k, 2 * chips[k][0] + chips[k][1]).wait_recv()
                cp = over_d2d(a, k, c)
                cp.start()
                passed.append(cp)
        for a in range(n):
            for k in range(3):
                over_d2d(a, k, 1 - c).wait_recv()
        for cp in sends + passed:
            cp.wait_send()
        for cp in local:
            cp.wait()

    return pl.pallas_call(
        body, name="gather_weights", in_specs=[ANY] * n, out_specs=[ANY] * n,
        out_shape=[jax.ShapeDtypeStruct((N_CHIPS,) + s.shape, s.dtype) for s in shards],
        scratch_shapes=[pltpu.SemaphoreType.DMA((n,)), pltpu.SemaphoreType.DMA((n, 6)), pltpu.SemaphoreType.DMA((n, 6))],
    )(*shards)


def _old_swap_layers(grads):
    n = len(grads)

    def body(*refs):
        ins, own, got = refs[:n], refs[n:2 * n], refs[2 * n:3 * n]
        local_sem, send_sem, recv_sem = refs[3 * n:]
        x, y, c, _ = _place()
        local = [pltpu.make_async_copy(ins[a].at[c], own[a], local_sem.at[a]) for a in range(n)]
        sends = [pltpu.make_async_remote_copy(
            src_ref=ins[a].at[1 - c], dst_ref=got[a], send_sem=send_sem.at[a], recv_sem=recv_sem.at[a],
            device_id=(x, y, 1 - c), device_id_type=MESH) for a in range(n)]
        for cp in local + sends:
            cp.start()
        for cp in sends:
            cp.wait()
        for cp in local:
            cp.wait()

    shapes = [jax.ShapeDtypeStruct(g.shape[1:], g.dtype) for g in grads]
    out = pl.pallas_call(
        body, name="swap_layers", in_specs=[ANY] * n, out_specs=[ANY] * (2 * n), out_shape=shapes + shapes,
        scratch_shapes=[pltpu.SemaphoreType.DMA((n,)), pltpu.SemaphoreType.DMA((n,)), pltpu.SemaphoreType.DMA((n,))],
    )(*grads)
    return out[:n], out[n:]


def _old_scatter_shards(parts):
    n = len(parts)

    def body(*refs):
        ins, outs = refs[:n], refs[n:2 * n]
        local_sem, send_sem, recv_sem = refs[2 * n:]
        x, y, c, chips = _place()
        me = 2 * x + y
        local = [pltpu.make_async_copy(ins[a].at[me], outs[a].at[me], local_sem.at[a]) for a in range(n)]
        sends = [pltpu.make_async_remote_copy(
            src_ref=ins[a].at[2 * chips[k][0] + chips[k][1]], dst_ref=outs[a].at[me],
            send_sem=send_sem.at[a, k], recv_sem=recv_sem.at[a, k],
            device_id=(*chips[k], c), device_id_type=MESH) for a in range(n) for k in range(3)]
        for cp in local + sends:
            cp.start()
        for a in range(n):
            for k in range(3):
                src_chip = 2 * chips[k][0] + chips[k][1]
                pltpu.make_async_remote_copy(
                    src_ref=ins[a].at[me], dst_ref=outs[a].at[src_chip],
                    send_sem=send_sem.at[a, k], recv_sem=recv_sem.at[a, k],
                    device_id=(x, y, c), device_id_type=MESH).wait_recv()
        for cp in sends:
            cp.wait_send()
        for cp in local:
            cp.wait()

    return pl.pallas_call(
        body, name="scatter_shards", in_specs=[ANY] * n, out_specs=[ANY] * n,
        out_shape=[jax.ShapeDtypeStruct(p.shape, p.dtype) for p in parts],
        scratch_shapes=[pltpu.SemaphoreType.DMA((n,)), pltpu.SemaphoreType.DMA((n, 3)), pltpu.SemaphoreType.DMA((n, 3))],
    )(*parts)


def _old_join_layers(halves):
    n = len(halves)

    def body(*refs):
        ins, outs = refs[:n], refs[n:2 * n]
        local_sem, send_sem, recv_sem = refs[2 * n:]
        x, y, c, _ = _place()
        local = [pltpu.make_async_copy(ins[a], outs[a].at[c], local_sem.at[a]) for a in range(n)]
        sends = [pltpu.make_async_remote_copy(
            src_ref=ins[a], dst_ref=outs[a].at[c], send_sem=send_sem.at[a], recv_sem=recv_sem.at[a],
            device_id=(x, y, 1 - c), device_id_type=MESH) for a in range(n)]
        for cp in local + sends:
            cp.start()
        for a in range(n):
            sends[a].wait_send()
            pltpu.make_async_remote_copy(
                src_ref=ins[a], dst_ref=outs[a].at[1 - c], send_sem=send_sem.at[a], recv_sem=recv_sem.at[a],
                device_id=(x, y, 1 - c), device_id_type=MESH).wait_recv()
        for cp in local:
            cp.wait()

    return pl.pallas_call(
        body, name="join_layers", in_specs=[ANY] * n, out_specs=[ANY] * n,
        out_shape=[jax.ShapeDtypeStruct((2,) + h.shape, h.dtype) for h in halves],
        scratch_shapes=[pltpu.SemaphoreType.DMA((n,)), pltpu.SemaphoreType.DMA((n,)), pltpu.SemaphoreType.DMA((n,))],
    )(*halves)


def _all_reduce_small(block):
    r = block.shape[0]

    def body(x_ref, out_ref, slots, send_sem, recv_sem):
        x, y, c, _ = _place()
        me = 4 * x + 2 * y + c
        slots[me] = x_ref[...]
        sends = []
        for mask in range(1, N_DEV):
            fx, fy, fc = (mask >> 2) & 1, (mask >> 1) & 1, mask & 1
            peer = (x ^ fx, y ^ fy, c ^ fc)
            cp = pltpu.make_async_remote_copy(
                src_ref=x_ref, dst_ref=slots.at[me], send_sem=send_sem.at[mask - 1], recv_sem=recv_sem.at[mask - 1],
                device_id=peer, device_id_type=MESH)
            cp.start()
            sends.append(cp)
        for mask in range(1, N_DEV):
            src = me ^ mask
            pltpu.make_async_remote_copy(
                src_ref=x_ref, dst_ref=slots.at[src], send_sem=send_sem.at[mask - 1], recv_sem=recv_sem.at[mask - 1],
                device_id=(x, y, c), device_id_type=MESH).wait_recv()
        for cp in sends:
            cp.wait_send()
        total = slots[0]
        for d in range(1, N_DEV):
            total = total + slots[d]
        out_ref[...] = total

    vmem = pl.BlockSpec(memory_space=pltpu.VMEM)
    return pl.pallas_call(
        body, name="all_reduce_small", in_specs=[vmem], out_specs=vmem,
        out_shape=jax.ShapeDtypeStruct((r, 128), F32),
        scratch_shapes=[pltpu.VMEM((N_DEV, r, 128), F32), pltpu.SemaphoreType.DMA((N_DEV - 1,)),
                        pltpu.SemaphoreType.DMA((N_DEV - 1,))],
        compiler_params=pltpu.CompilerParams(vmem_limit_bytes=VMEM_LIMIT))(block)


def _heads(t):
    return t.reshape(S, -1, HD).transpose(1, 0, 2)


def _unheads(t):
    return t.transpose(1, 0, 2).reshape(S, -1)


def _to_classes(t, d):
    h, _, e = t.shape
    return t.reshape(h, S // d, d, e).transpose(0, 2, 1, 3).reshape(h, S, e)


def _from_classes(t, d):
    h, _, e = t.shape
    return t.reshape(h, d, S // d, e).transpose(0, 2, 1, 3).reshape(h, S, e)


def _full_cols(w_g):
    return w_g.transpose(1, 0, 2).reshape(w_g.shape[1], -1)


def _shard_cols(dw):
    k = dw.shape[0]
    return dw.reshape(k, N_CHIPS, -1).transpose(1, 0, 2)


def _split_proj(proj):
    a = proj[:, :2304].reshape(S, 3, 3, 4 * HD)
    bq = _heads(proj[:, 2304:2816])
    bkv = proj[:, 2816:3072].reshape(S, 2, 2 * HD)
    cq = proj[:, 3072:3840].reshape(S, 3, 4 * HD)
    qa, ka, va = ([_to_classes(_heads(a[:, t, g]), d) for g, (_, d) in enumerate(A_GROUPS)] for t in range(3))
    q = jnp.concatenate(qa + [bq], axis=0)
    k = jnp.concatenate(ka + [_heads(bkv[:, 0])], axis=0)
    v = jnp.concatenate(va + [_heads(bkv[:, 1])], axis=0)
    return q, k, v, _heads(cq[:, 0]), _heads(cq[:, 1]), _heads(cq[:, 2])


def _join_dproj(dq, dk, dv, dcq, dck, dcv, dgate):
    def groups(t):
        return [_unheads(_from_classes(t[4 * g:4 * g + 4], d)) for g, (_, d) in enumerate(A_GROUPS)]

    cols = groups(dq) + groups(dk) + groups(dv)
    cols += [_unheads(dq[N_A:]), _unheads(dk[N_A:]), _unheads(dv[N_A:]), _unheads(dcq), _unheads(dck), _unheads(dcv)]
    cols = [t.astype(BF16) for t in cols] + [dgate[0], dgate[1], dgate[2]]
    return jnp.concatenate(cols, axis=1).reshape(S, N_CHIPS, IN_SHARD).transpose(1, 0, 2)


def _mixer_fwd(h1, w, rel_bias, sinks20, bidx):
    proj = _proj_in(h1, w["w_in"], w["layer"]).transpose(1, 0, 2).reshape(S, IN_COLS)
    q, k, v, cq, ck, cv = _split_proj(proj)
    o_band, lse = _band_fwd(q, k, v, bidx, rel_bias, sinks20)
    o_g = jnp.stack([_from_classes(o_band[4 * g:4 * g + 4], d) for g, (_, d) in enumerate(A_GROUPS)])
    lse_g = jnp.stack([_from_classes(lse[4 * g:4 * g + 4], d) for g, (_, d) in enumerate(A_GROUPS)])
    o_a = _unheads(_comb_fwd(o_g, lse_g)).astype(BF16)
    o_b = _unheads(o_band[N_A:]).astype(BF16)
    o_c = _unheads(_sb_fwd(cq, ck, cv)).astype(BF16)
    br = jnp.stack([_mm_nn("branch_a", o_a, w["w_br_a"], F32), _mm_nn("branch_b", o_b, w["w_br_b"], F32),
                    _mm_nn("branch_c", o_c, w["w_br_c"], F32)])
    merged = _gate_fwd(proj, w["b_gate"], br)
    mo = _mm_nn("out_proj", merged, w["w_out"], F32)
    saved = dict(proj=proj, q=q, k=k, v=v, cq=cq, ck=ck, cv=cv, o_band=o_band, lse=lse, o_g=o_g, lse_g=lse_g,
                 o_a=o_a, o_b=o_b, o_c=o_c, br=br, merged=merged)
    return mo, saved


def _mixer_bwd(d_mo, h1, w, sv, rel_bias, sinks20, bidx, stats_in, into):
    grads, layer = {}, w["layer"]
    dmerged = _mm_nt("out_proj_dx", d_mo, w["w_out"], F32)
    grads["w_out"] = _mm_tn_stacked("out_proj_dw", sv["merged"], d_mo, layer, into.get("w_out"), True)
    e, dgate, db_gate = _gate_bwd(sv["proj"], w["b_gate"], sv["br"], dmerged)
    grads["b_gate"] = db_gate
    d_o = {}
    for n, name in enumerate("abc"):
        d_o[name] = _mm_nt("branch_%s_dx" % name, e[n], w["w_br_" + name], F32)
        grads["w_br_" + name] = _mm_tn_stacked("branch_%s_dw" % name, sv["o_" + name], e[n], layer,
                                               into.get("w_br_" + name), False)
    do_g, dlse_g = _comb_bwd(sv["o_g"], sv["lse_g"], _heads(d_o["a"]))
    do_band = jnp.concatenate([_to_classes(do_g[g], d) for g, (_, d) in enumerate(A_GROUPS)] + [_heads(d_o["b"])], axis=0)
    dlse = jnp.concatenate([_to_classes(dlse_g[g], d) for g, (_, d) in enumerate(A_GROUPS)]
                           + [jnp.zeros((N_BAND_Q - N_A, S, 1), F32)], axis=0)
    dq, dk, dv, stats = _band_bwd(sv["q"], sv["k"], sv["v"], bidx, rel_bias, sinks20, sv["o_band"], sv["lse"],
                                  do_band, dlse, stats_in)
    dcq, dck, dcv = _sb_bwd(sv["cq"], sv["ck"], sv["cv"], _heads(d_o["c"]))
    dproj_s = _join_dproj(dq, dk, dv, dcq, dck, dcv, dgate)
    dh1 = _proj_in_dx(dproj_s, w["w_in"], layer)
    grads["w_in"] = _proj_in_dw(h1, dproj_s, layer, into.get("w_in"))
    return dh1, grads, stats


def _ffn_fwd(h2, w):
    u = _mm_nn("ffn_up", h2, w["w_up"], F32, tn=1024)
    a = _conv_fwd(u, w["conv_w"], w["conv_b"])
    dn = _mm_nn("ffn_down", a, w["w_down"], F32)
    return dn, dict(u=u, a=a)


def _ffn_bwd(d_dn, h2, w, sv, into):
    grads, layer = {}, w["layer"]
    da = _mm_nt("ffn_down_dx", d_dn, w["w_down"], F32, tn=1024)
    grads["w_down"] = _mm_tn_stacked("ffn_down_dw", sv["a"], d_dn, layer, into.get("w_down"), True)
    dug, duv, dwg, dwv, dbg, dbv = _conv_bwd(sv["u"], w["conv_w"], w["conv_b"], da)
    du = jnp.concatenate([dug, duv], axis=1)
    grads["conv_w"] = jnp.concatenate([dwg, dwv], axis=1)
    grads["conv_b"] = jnp.concatenate([dbg, dbv], axis=1)
    dh2 = _mm_nt("ffn_up_dx", du, w["w_up"], F32)
    grads["w_up"] = _mm_tn_stacked("ffn_up_dw", h2, du, layer, into.get("w_up"), False, tn=1024)
    return dh2, grads


BIG = ("w_in", "w_br_a", "w_br_b", "w_br_c", "w_out", "w_up", "w_down")
SMALL_ROWS = (("rel_bias", 5), ("attn_pre_norm", 16), ("attn_post_norm", 16), ("ffn_pre_norm", 16), ("ffn_post_norm", 16),
              ("sinks", 1), ("conv_b", 128), ("b_gate", 48), ("conv_w", 384), ("loss", 1))
SMALL_TOTAL = 632


def _pack_small(vals):
    rows = []
    for name, n in SMALL_ROWS:
        flat = vals[name].reshape(-1).astype(F32)
        rows.append(jnp.pad(flat, (0, n * 128 - flat.shape[0])).reshape(n, 128))
    used = sum(n for _, n in SMALL_ROWS)
    rows.append(jnp.zeros((SMALL_TOTAL - used, 128), F32))
    return jnp.concatenate(rows, axis=0)


def _unpack_small(block, shapes):
    out, row = {}, 0
    for name, n in SMALL_ROWS:
        size = int(np.prod(shapes[name]))
        out[name] = block[row:row + n].reshape(-1)[:size].reshape(shapes[name])
        row += n
    return out


def kernel(x, rel_bias, attn_pre_norm, w_in, b_gate, sinks, w_br_a, w_br_b, w_br_c, w_out, attn_post_norm, ffn_pre_norm, w_up, conv_w, conv_b, w_down, ffn_post_norm, loss_target, m_rel_bias, m_attn_pre_norm, m_w_in, m_b_gate, m_sinks, m_w_br_a, m_w_br_b, m_w_br_c, m_w_out, m_attn_post_norm, m_ffn_pre_norm, m_w_up, m_conv_w, m_conv_b, m_w_down, m_ffn_post_norm, v_rel_bias, v_attn_pre_norm, v_w_in, v_b_gate, v_sinks, v_w_br_a, v_w_br_b, v_w_br_c, v_w_out, v_attn_post_norm, v_ffn_pre_norm, v_w_up, v_conv_w, v_conv_b, v_w_down, v_ffn_post_norm):
    names = ("rel_bias", "attn_pre_norm", "w_in", "b_gate", "sinks", "w_br_a", "w_br_b", "w_br_c", "w_out",
             "attn_post_norm", "ffn_pre_norm", "w_up", "conv_w", "conv_b", "w_down", "ffn_post_norm")
    weights = dict(zip(names, (rel_bias, attn_pre_norm, w_in, b_gate, sinks, w_br_a, w_br_b, w_br_c, w_out,
                               attn_post_norm, ffn_pre_norm, w_up, conv_w, conv_b, w_down, ffn_post_norm)))
    mom1 = dict(zip(names, (m_rel_bias, m_attn_pre_norm, m_w_in, m_b_gate, m_sinks, m_w_br_a, m_w_br_b, m_w_br_c,
                            m_w_out, m_attn_post_norm, m_ffn_pre_norm, m_w_up, m_conv_w, m_conv_b, m_w_down,
                            m_ffn_post_norm)))
    mom2 = dict(zip(names, (v_rel_bias, v_attn_pre_norm, v_w_in, v_b_gate, v_sinks, v_w_br_a, v_w_br_b, v_w_br_c,
                            v_w_out, v_attn_post_norm, v_ffn_pre_norm, v_w_up, v_conv_w, v_conv_b, v_w_down,
                            v_ffn_post_norm)))

    chip = 2 * lax.axis_index("x") + lax.axis_index("y")
    core = lax.axis_index("c")
    gathered = _gather_weights([_cast_into_slot("cast_" + n, weights[n], chip) for n in BIG], [b_gate, conv_w])
    gathered = dict(zip(BIG + ("b_gate", "conv_w"), gathered))
    layers = []
    for l in range(DEPTH):
        w = {"w_in": gathered["w_in"], "layer": l}
        for n in ("w_br_a", "w_br_b", "w_br_c", "w_up", "b_gate", "conv_w"):
            w[n] = _full_cols(gathered[n][:, l])
        w["w_out"] = gathered["w_out"][:, l].reshape(D, D)
        w["w_down"] = gathered["w_down"][:, l].reshape(D_FF, D)
        w["conv_b"] = conv_b[l:l + 1]
        layers.append(w)

    local = _local_step(x.reshape(S, D), loss_target.reshape(S, D), layers, rel_bias, sinks, attn_pre_norm,
                        attn_post_norm, ffn_pre_norm, ffn_post_norm)
    return _reduce_and_update(x.shape, names, weights, mom1, mom2, chip, core, *local)


def _local_step(xs, target, layers, rel_bias, sinks, attn_pre_norm, attn_post_norm, ffn_pre_norm, ffn_post_norm):
    bidx = jnp.asarray(_bucket_maps())

    saved = []
    h1 = _rms_fwd("pre_norm_first", xs, attn_pre_norm[0:1])
    x_in = xs
    for l in range(DEPTH):
        w = layers[l]
        sinks20 = jnp.concatenate([jnp.full((N_A,), NEG, F32), sinks[l]])
        mo, sv_mix = _mixer_fwd(h1, w, rel_bias, sinks20, bidx)
        x_mid, h2 = _post_pre_fwd("post_attn_norm", x_in, mo, attn_post_norm[l:l + 1], ffn_pre_norm[l:l + 1])
        dn, sv_ffn = _ffn_fwd(h2, w)
        g_next = attn_pre_norm[l + 1:l + 2] if l + 1 < DEPTH else None
        x_out, h1_next = _post_pre_fwd("post_ffn_norm" if l + 1 < DEPTH else "post_ffn_norm_last", x_mid, dn,
                                       ffn_post_norm[l:l + 1], g_next)
        saved.append(dict(x_in=x_in, h1=h1, mo=mo, x_mid=x_mid, h2=h2, dn=dn, sinks20=sinks20, mix=sv_mix, ffn=sv_ffn))
        x_in, h1 = x_out, h1_next

    loss_row, dres = _loss_kernel(x_in, target)

    big_grads = {}
    small = [None] * DEPTH
    stats = jnp.zeros((N_BAND_Q, 8, 128), F32)
    dh_next = None
    for l in reversed(range(DEPTH)):
        w, sv = layers[l], saved[l]
        if l + 1 < DEPTH:
            pre = (saved[l + 1]["x_in"], attn_pre_norm[l + 1:l + 2], dh_next)
            dres, d_dn, dg_pre_next, dg_fpost = _norm_bwd("post_ffn_norm_bwd", dres, pre,
                                                          (sv["dn"], ffn_post_norm[l:l + 1]))
            small[l + 1]["attn_pre_norm"] = dg_pre_next
        else:
            dres, d_dn, _, dg_fpost = _norm_bwd("post_ffn_norm_last_bwd", dres, None, (sv["dn"], ffn_post_norm[l:l + 1]))
        dh2, g_ffn = _ffn_bwd(d_dn, sv["h2"], w, sv["ffn"], big_grads)
        dres, d_mo, dg_fpre, dg_apost = _norm_bwd("post_attn_norm_bwd", dres, (sv["x_mid"], ffn_pre_norm[l:l + 1], dh2),
                                                  (sv["mo"], attn_post_norm[l:l + 1]))
        dh_next, g_mix, stats = _mixer_bwd(d_mo, sv["h1"], w, sv["mix"], rel_bias, sv["sinks20"], bidx, stats,
                                           big_grads)
        big_grads = {n: {**g_ffn, **g_mix}[n] for n in BIG}
        small[l] = dict(ffn_post_norm=dg_fpost, ffn_pre_norm=dg_fpre, attn_post_norm=dg_apost,
                        sinks=stats[N_A:, 1, 0], conv_b=g_ffn["conv_b"], b_gate=g_mix["b_gate"], conv_w=g_ffn["conv_w"])
    grad_x, _, dg_pre0, _ = _norm_bwd("pre_norm_first_bwd", dres, (saved[0]["x_in"], attn_pre_norm[0:1], dh_next), None)
    small[0]["attn_pre_norm"] = dg_pre0
    return loss_row, grad_x, big_grads, small, stats


def _reduce_and_update(x_shape, names, weights, mom1, mom2, chip, core, loss_row, grad_x, big_grads, small, stats):
    small_vals = {n: jnp.stack([small[l][n].reshape(weights[n].shape[1:]) for l in range(DEPTH)])
                  for n in ("attn_pre_norm", "attn_post_norm", "ffn_pre_norm", "ffn_post_norm", "conv_b", "sinks")}
    small_vals["b_gate"] = jnp.stack([small[l]["b_gate"] for l in range(DEPTH)])
    small_vals["conv_w"] = jnp.stack([small[l]["conv_w"] for l in range(DEPTH)])
    small_vals["rel_bias"] = stats[:, 0, :NUM_BUCKETS].T
    small_vals["loss"] = loss_row[0, :1]
    shapes = {n: v.shape for n, v in small_vals.items()}
    reduced = _unpack_small(_all_reduce_small(_pack_small(small_vals)), shapes)
    chip = 2 * lax.axis_index("x") + lax.axis_index("y")
    reduced["b_gate"] = lax.dynamic_slice_in_dim(reduced["b_gate"], chip * (D // N_CHIPS), D // N_CHIPS, axis=2)
    reduced["conv_w"] = lax.dynamic_slice_in_dim(reduced["conv_w"], chip * (2 * D_FF // N_CHIPS), 2 * D_FF // N_CHIPS, axis=2)

    stacked = [big_grads[n] for n in BIG]
    got = _swap_layers(stacked)
    parts = [_add_sibling("add_sibling_" + n, g, t, core) for n, g, t in zip(BIG, stacked, got)]
    arrived = _scatter_shards(parts)
    halves = [_add_chips("add_chips_" + n, p, t, chip, core) for n, p, t in zip(BIG, parts, arrived)]
    full = _join_layers(halves)
    grads = dict(zip(BIG, [f.reshape(weights[n].shape) for n, f in zip(BIG, full)]))
    for n in names:
        if n not in grads:
            grads[n] = reduced[n].reshape(weights[n].shape)

    delta, new_m, new_v = {}, {}, {}
    for n in names:
        w2 = weights[n] if weights[n].ndim > 1 else weights[n].reshape(1, -1)
        shape2 = w2.shape
        d_, m_, v_ = _adamw("adamw_" + n, w2, grads[n].reshape(shape2), mom1[n].reshape(shape2), mom2[n].reshape(shape2))
        delta[n], new_m[n], new_v[n] = (t.reshape(weights[n].shape) for t in (d_, m_, v_))

    loss = reduced["loss"].reshape(())
    return (loss, grad_x.reshape(x_shape), *[grads[n] for n in names], *[delta[n] for n in names],
            *[new_m[n] for n in names], *[new_v[n] for n in names])
```

```python
import functools
import math

import numpy as np
import jax
import jax.numpy as jnp
from jax import lax
from jax.experimental import pallas as pl
from jax.experimental.pallas import tpu as pltpu

F32 = jnp.float32
BF16 = jnp.bfloat16

S = 2048
D = 1024
DEPTH = 2
HD = 64
BLK = 128
NQB = S // BLK
A_GROUPS = ((128, 1), (512, 4), (2048, 16))
N_BAND_Q = 20
N_A = 12
NUM_BUCKETS = 32
MAX_DISTANCE = 2048
D_FF = 4096
IN_COLS = 6912
IN_SHARD = IN_COLS // 4
OFF_GATE = 3840
EPS = 1e-6
SCALE = HD ** -0.5
NEG = -1e30
N_CHIPS = 4
N_DEV = 8

ADAM_LR = 0.001
ADAM_B1 = 0.9
ADAM_B2 = 0.999
ADAM_EPS = 1e-08
ADAM_WD = 0.01
ADAM_STEP = 10

VMEM_LIMIT = 56 * 1024 * 1024

NN = (((1,), (0,)), ((), ()))
NT = (((1,), (1,)), ((), ()))
TN = (((0,), (0,)), ((), ()))

MESH = pl.DeviceIdType.MESH
ANY = pl.BlockSpec(memory_space=pl.ANY)


def _dot(a, b, dims):
    return lax.dot_general(a, b, dims, preferred_element_type=F32)


def _params(sem):
    return pltpu.CompilerParams(dimension_semantics=sem, vmem_limit_bytes=VMEM_LIMIT)


def _matmul(name, a, b, out_shape, out_dtype, grid, a_spec, b_spec, o_spec, dims, acc_shape, into=None):
    nk = grid[-1]

    def body(a_ref, b_ref, *rest):
        o_ref, scratch = (rest[1], rest[2:]) if into is not None else (rest[0], rest[1:])
        part = _dot(a_ref[...].astype(BF16), b_ref[...].astype(BF16), dims)
        if nk == 1:
            o_ref[...] = part.astype(o_ref.dtype)
            return
        acc_ref, = scratch
        k = pl.program_id(len(grid) - 1)

        @pl.when(k == 0)
        def _():
            acc_ref[...] = part

        @pl.when(k > 0)
        def _():
            acc_ref[...] += part

        @pl.when(k == nk - 1)
        def _():
            o_ref[...] = acc_ref[...].astype(o_ref.dtype)

    scratch = [] if nk == 1 else [pltpu.VMEM(acc_shape, F32)]
    sem = ("parallel",) * (len(grid) - 1) + ("arbitrary",)
    ins, in_specs, aliases = [a, b], [a_spec, b_spec], {}
    if into is not None:
        ins, in_specs, aliases = ins + [into], in_specs + [ANY], {2: 0}
    return pl.pallas_call(
        body, name=name, grid=grid, in_specs=in_specs, out_specs=o_spec,
        out_shape=jax.ShapeDtypeStruct(out_shape, out_dtype), scratch_shapes=scratch,
        input_output_aliases=aliases, compiler_params=_params(sem))(*ins)


def _mm_tn_stacked(name, a, b, layer, into, row_sharded, tm=512, tn=512, tk=1024):
    k, m = a.shape
    n = b.shape[1]
    m4, n4 = (m // N_CHIPS, n) if row_sharded else (m, n // N_CHIPS)
    tm, tn, tk = min(tm, m4), min(tn, n4), min(tk, k)
    per_m, per_n = m4 // tm, n4 // tn
    if row_sharded:
        o_map = lambda i, j, l: (layer, i // per_m, i % per_m, j)
    else:
        o_map = lambda i, j, l: (layer, j // per_n, i, j % per_n)
    return _matmul(name, a, b, (DEPTH, N_CHIPS, m4, n4), BF16, (m // tm, n // tn, k // tk),
                   pl.BlockSpec((tk, tm), lambda i, j, l: (l, i)),
                   pl.BlockSpec((tk, tn), lambda i, j, l: (l, j)),
                   pl.BlockSpec((None, None, tm, tn), o_map), TN, (tm, tn), into=into)


def _mm_nn(name, a, b, out_dtype, tm=512, tn=512, tk=1024):
    m, k = a.shape
    n = b.shape[1]
    tm, tn, tk = min(tm, m), min(tn, n), min(tk, k)
    return _matmul(name, a, b, (m, n), out_dtype, (m // tm, n // tn, k // tk),
                   pl.BlockSpec((tm, tk), lambda i, j, l: (i, l)),
                   pl.BlockSpec((tk, tn), lambda i, j, l: (l, j)),
                   pl.BlockSpec((tm, tn), lambda i, j, l: (i, j)), NN, (tm, tn))


def _mm_nt(name, a, b, out_dtype, tm=512, tn=512, tk=1024):
    m, k = a.shape
    n = b.shape[0]
    tm, tn, tk = min(tm, m), min(tn, n), min(tk, k)
    return _matmul(name, a, b, (m, n), out_dtype, (m // tm, n // tn, k // tk),
                   pl.BlockSpec((tm, tk), lambda i, j, l: (i, l)),
                   pl.BlockSpec((tn, tk), lambda i, j, l: (j, l)),
                   pl.BlockSpec((tm, tn), lambda i, j, l: (i, j)), NT, (tm, tn))


def _proj_in_dw(h, dproj_s, layer, into):
    tm = 512
    return _matmul("proj_in_dw", h, dproj_s, (DEPTH, N_CHIPS, D, IN_SHARD), BF16, (N_CHIPS, D // tm, 1),
                   pl.BlockSpec((S, tm), lambda j, i, l: (0, i)),
                   pl.BlockSpec((None, S, IN_SHARD), lambda j, i, l: (j, 0, 0)),
                   pl.BlockSpec((None, None, tm, IN_SHARD), lambda j, i, l: (layer, j, i, 0)), TN, (tm, IN_SHARD),
                   into=into)


TR = 256


def _row_spec(width=D):
    return pl.BlockSpec((TR, width), lambda i: (i, 0))


def _vec_spec(width=D):
    return pl.BlockSpec((1, width), lambda i: (0, 0))


def _rms(x, g):
    r = lax.rsqrt(jnp.mean(x * x, axis=-1, keepdims=True) + EPS)
    return x * r * g


def _rms_fwd(name, x, g):
    def body(x_ref, g_ref, h_ref):
        h_ref[...] = _rms(x_ref[...], g_ref[...]).astype(BF16)

    return pl.pallas_call(
        body, name=name, grid=(S // TR,), in_specs=[_row_spec(), _vec_spec()], out_specs=_row_spec(),
        out_shape=jax.ShapeDtypeStruct((S, D), BF16), compiler_params=_params(("parallel",)))(x, g)


def _post_pre_fwd(name, x, y, g_post, g_pre):
    has_pre = g_pre is not None

    def body(*refs):
        if has_pre:
            x_ref, y_ref, gp_ref, gn_ref, xn_ref, h_ref = refs
        else:
            x_ref, y_ref, gp_ref, xn_ref = refs
        xn = x_ref[...] + _rms(y_ref[...], gp_ref[...])
        xn_ref[...] = xn
        if has_pre:
            h_ref[...] = _rms(xn, gn_ref[...]).astype(BF16)

    ins = [x, y, g_post] + ([g_pre] if has_pre else [])
    in_specs = [_row_spec(), _row_spec(), _vec_spec()] + ([_vec_spec()] if has_pre else [])
    out_shape = [jax.ShapeDtypeStruct((S, D), F32)] + ([jax.ShapeDtypeStruct((S, D), BF16)] if has_pre else [])
    out_specs = [_row_spec()] + ([_row_spec()] if has_pre else [])
    out = pl.pallas_call(
        body, name=name, grid=(S // TR,), in_specs=in_specs, out_specs=out_specs, out_shape=out_shape,
        compiler_params=_params(("parallel",)))(*ins)
    return out if has_pre else (out[0], None)


def _rms_bwd_math(x, g, dy):
    r = lax.rsqrt(jnp.mean(x * x, axis=-1, keepdims=True) + EPS)
    n = x * r
    dn = dy * g
    dx = r * (dn - n * jnp.mean(dn * n, axis=-1, keepdims=True))
    return dx, jnp.sum(dy * n, axis=0, keepdims=True)


def _norm_bwd(name, dres, pre=None, post=None):
    has_pre, has_post = pre is not None, post is not None

    def body(*refs):
        refs = list(refs)
        dres_ref = refs.pop(0)
        if has_pre:
            xn_ref, gn_ref, dh_ref = refs[:3]
            refs = refs[3:]
        if has_post:
            y_ref, gp_ref = refs[:2]
            refs = refs[2:]
        dxn_ref = refs.pop(0)
        dy_ref = refs.pop(0) if has_post else None
        dgn_ref = refs.pop(0) if has_pre else None
        dgp_ref = refs.pop(0) if has_post else None
        first = pl.program_id(0) == 0
        dxn = dres_ref[...]
        if has_pre:
            dx, dg = _rms_bwd_math(xn_ref[...], gn_ref[...], dh_ref[...])
            dxn = dxn + dx

            @pl.when(first)
            def _():
                dgn_ref[...] = dg

            @pl.when(jnp.logical_not(first))
            def _():
                dgn_ref[...] += dg
        dxn_ref[...] = dxn
        if has_post:
            dy, dg = _rms_bwd_math(y_ref[...], gp_ref[...], dxn)
            dy_ref[...] = dy.astype(BF16)

            @pl.when(first)
            def _():
                dgp_ref[...] = dg

            @pl.when(jnp.logical_not(first))
            def _():
                dgp_ref[...] += dg

    ins, in_specs = [dres], [_row_spec()]
    if has_pre:
        ins += list(pre)
        in_specs += [_row_spec(), _vec_spec(), _row_spec()]
    if has_post:
        ins += list(post)
        in_specs += [_row_spec(), _vec_spec()]
    out_shape, out_specs = [jax.ShapeDtypeStruct((S, D), F32)], [_row_spec()]
    if has_post:
        out_shape.append(jax.ShapeDtypeStruct((S, D), BF16))
        out_specs.append(_row_spec())
    for _ in range(int(has_pre) + int(has_post)):
        out_shape.append(jax.ShapeDtypeStruct((1, D), F32))
        out_specs.append(_vec_spec())
    out = list(pl.pallas_call(
        body, name=name, grid=(S // TR,), in_specs=in_specs, out_specs=out_specs, out_shape=out_shape,
        compiler_params=_params(("arbitrary",)))(*ins))
    dxn = out.pop(0)
    dy = out.pop(0) if has_post else None
    dgn = out.pop(0) if has_pre else None
    dgp = out.pop(0) if has_post else None
    return dxn, dy, dgn, dgp


def _loss_kernel(y, target):
    def body(y_ref, t_ref, loss_ref, dy_ref):
        e = y_ref[...] - t_ref[...]
        dy_ref[...] = e * (1.0 / D)
        part = jnp.zeros((1, 128), F32) + 0.5 * jnp.sum(jnp.mean(e * e, axis=-1, keepdims=True))

        @pl.when(pl.program_id(0) == 0)
        def _():
            loss_ref[...] = part

        @pl.when(pl.program_id(0) > 0)
        def _():
            loss_ref[...] += part

    return pl.pallas_call(
        body, name="loss", grid=(S // TR,), in_specs=[_row_spec(), _row_spec()],
        out_specs=[_vec_spec(128), _row_spec()],
        out_shape=[jax.ShapeDtypeStruct((1, 128), F32), jax.ShapeDtypeStruct((S, D), F32)],
        compiler_params=_params(("arbitrary",)))(y, target)


def _t5_bucket_np(dist):
    max_exact = NUM_BUCKETS // 2
    nf = np.maximum(dist, 1).astype(np.float32)
    large = max_exact + (np.log(nf / max_exact) / np.float32(math.log(MAX_DISTANCE / max_exact))
                         * (NUM_BUCKETS - max_exact)).astype(np.int32)
    large = np.minimum(large, NUM_BUCKETS - 1)
    return np.where(dist < max_exact, dist, large).astype(np.int32)


def _bucket_maps():
    a = np.arange(BLK)[:, None]
    b = np.arange(2 * BLK)[None, :]
    dist = np.maximum(a + BLK - b, 0)
    maps = [_t5_bucket_np(dist * d) for _, d in A_GROUPS] + [_t5_bucket_np(dist)]
    return np.stack(maps).astype(np.int32)


def _classes(arr, d):
    return arr.reshape(S // d, d * arr.shape[1])


def _band_spec(arr, col0, prev):
    ncol = arr.shape[1] // 128
    if prev:
        return pl.BlockSpec((BLK, 128), lambda p, r, b: (jnp.maximum(b - 1, 0), r * ncol + col0 + p))
    return pl.BlockSpec((BLK, 128), lambda p, r, b: (b, r * ncol + col0 + p))


def _band_bias(tab_ref, bidx_ref, h):
    bi = bidx_ref[...]
    bias = jnp.zeros((BLK, 2 * BLK), F32)
    for kk in range(NUM_BUCKETS):
        bias = jnp.where(bi == kk, tab_ref[kk, h], bias)
    return bias


def _lane_lo():
    return lax.broadcasted_iota(jnp.int32, (BLK, 128), 1) < HD


def _two_heads(x, lo, dtype=BF16):
    return jnp.where(lo, x, 0.0).astype(dtype), jnp.where(lo, 0.0, x).astype(dtype)


def _per_head(x, lo):
    return (jnp.sum(jnp.where(lo, x, 0.0), axis=1, keepdims=True) * (1.0 / HD),
            jnp.sum(jnp.where(lo, 0.0, x), axis=1, keepdims=True) * (1.0 / HD))


def _band_masks(b, maxd):
    a = lax.broadcasted_iota(jnp.int32, (BLK, BLK), 0)
    c = lax.broadcasted_iota(jnp.int32, (BLK, BLK), 1)
    return jnp.logical_and(a + BLK - c <= maxd, b > 0), a >= c


def _band_fwd(name, d, n_pairs, maxd, head0, srcs, bidx_g, tab, sinks):
    nb = S // d // BLK
    (qa, qc), (ka, kc), (va, vc) = srcs
    out_spec = pl.BlockSpec((BLK, 128), lambda p, r, b: (b, r * n_pairs + p))
    smem = pl.BlockSpec(memory_space=pltpu.SMEM)
    full = pl.BlockSpec((BLK, 2 * BLK), lambda p, r, b: (0, 0))

    def body(tab_ref, sink_ref, q_ref, kp_ref, kc_ref, vp_ref, vc_ref, bidx_ref, o_ref, lse_ref, bias_ref):
        p, r, b = pl.program_id(0), pl.program_id(1), pl.program_id(2)

        @pl.when(jnp.logical_and(r == 0, b == 0))
        def _():
            for h in range(2):
                bias_ref[h] = _band_bias(tab_ref, bidx_ref, head0 + 2 * p + h)

        lo = _lane_lo()
        qs = _two_heads(q_ref[...] * SCALE, lo)
        kp, kc_, vp, vc_ = (ref[...].astype(BF16) for ref in (kp_ref, kc_ref, vp_ref, vc_ref))
        mask_p, mask_c = _band_masks(b, maxd)
        outs, lses = [], []
        for h in range(2):
            sp = jnp.where(mask_p, _dot(qs[h], kp, NT) + bias_ref[h, :, :BLK], NEG)
            sc = jnp.where(mask_c, _dot(qs[h], kc_, NT) + bias_ref[h, :, BLK:], NEG)
            m = jnp.maximum(jnp.max(sp, axis=1, keepdims=True), jnp.max(sc, axis=1, keepdims=True))
            pp, pc = jnp.exp(sp - m), jnp.exp(sc - m)
            l = jnp.sum(pp, axis=1, keepdims=True) + jnp.sum(pc, axis=1, keepdims=True)
            num = _dot(pp.astype(BF16), vp, NN) + _dot(pc.astype(BF16), vc_, NN)
            lse = m + jnp.log(l)
            sig = 1.0 / (1.0 + jnp.exp(sink_ref[2 * p + h] - lse))
            outs.append(num * (sig / l))
            lses.append(lse + jnp.zeros((BLK, 128), F32))
        o_ref[...] = jnp.where(lo, outs[0], outs[1])
        lse_ref[...] = jnp.where(lo, lses[0], lses[1])

    shape = jax.ShapeDtypeStruct((S // d, d * n_pairs * 128), F32)
    o, lse = pl.pallas_call(
        body, name=name, grid=(n_pairs, d, nb),
        in_specs=[smem, smem, _band_spec(qa, qc, False), _band_spec(ka, kc, True), _band_spec(ka, kc, False),
                  _band_spec(va, vc, True), _band_spec(va, vc, False), full],
        out_specs=[out_spec, out_spec], out_shape=[shape, shape],
        scratch_shapes=[pltpu.VMEM((2, BLK, 2 * BLK), F32)],
        compiler_params=_params(("parallel", "arbitrary", "arbitrary")))(
            tab, sinks, _classes(qa, d), _classes(ka, d), _classes(ka, d), _classes(va, d), _classes(va, d), bidx_g)
    return o.reshape(S, n_pairs * 128), lse.reshape(S, n_pairs * 128)


def _band_bwd(name, d, n_pairs, maxd, head0, srcs, bidx_g, tab, sinks, o, lse, do, stats_in):
    nb = S // d // BLK
    rows = S // d
    (qa, qc), (ka, kc), (va, vc) = srcs
    blk_spec = pl.BlockSpec((BLK, 128), lambda p, r, b: (b, r * n_pairs + p))
    cls_spec = pl.BlockSpec((rows, 128), lambda p, r, b: (0, r * n_pairs + p))
    smem = pl.BlockSpec(memory_space=pltpu.SMEM)
    full = pl.BlockSpec((BLK, 2 * BLK), lambda p, r, b: (0, 0))
    stat_spec = pl.BlockSpec((2, 8, 128), lambda p, r, b: (p, 0, 0))

    def body(tab_ref, sink_ref, q_ref, kp_ref, kc_ref, vp_ref, vc_ref, bidx_ref, o_ref, lse_ref, do_ref, sin_ref,
             dq_ref, dk_ref, dv_ref, stat_ref, bias_ref, dsacc_ref, sk_ref):
        p, r, b = pl.program_id(0), pl.program_id(1), pl.program_id(2)

        @pl.when(jnp.logical_and(r == 0, b == 0))
        def _():
            for h in range(2):
                bias_ref[h] = _band_bias(tab_ref, bidx_ref, head0 + 2 * p + h)
            dsacc_ref[...] = jnp.zeros_like(dsacc_ref)
            sk_ref[...] = jnp.zeros_like(sk_ref)

        @pl.when(b == 0)
        def _():
            dk_ref[...] = jnp.zeros_like(dk_ref)
            dv_ref[...] = jnp.zeros_like(dv_ref)

        lo = _lane_lo()
        qs = _two_heads(q_ref[...] * SCALE, lo)
        kp, kc_, vp, vc_ = (ref[...].astype(BF16) for ref in (kp_ref, kc_ref, vp_ref, vc_ref))
        mask_p, mask_c = _band_masks(b, maxd)
        do = do_ref[...]
        dos = _two_heads(do, lo, F32)
        lses = _per_head(lse_ref[...], lo)
        prod = do * o_ref[...]
        deltas = (jnp.sum(jnp.where(lo, prod, 0.0), axis=1, keepdims=True),
                  jnp.sum(jnp.where(lo, 0.0, prod), axis=1, keepdims=True))
        dqs = []
        dk_c = jnp.zeros((BLK, 128), F32)
        dk_p = jnp.zeros((BLK, 128), F32)
        dv_c = jnp.zeros((BLK, 128), F32)
        dv_p = jnp.zeros((BLK, 128), F32)
        for h in range(2):
            sp = _dot(qs[h], kp, NT) + bias_ref[h, :, :BLK]
            sc = _dot(qs[h], kc_, NT) + bias_ref[h, :, BLK:]
            pp = jnp.where(mask_p, jnp.exp(sp - lses[h]), 0.0)
            pc = jnp.where(mask_c, jnp.exp(sc - lses[h]), 0.0)
            sig = 1.0 / (1.0 + jnp.exp(sink_ref[2 * p + h] - lses[h]))
            dob = dos[h].astype(BF16)
            dsp = pp * (sig * (_dot(dob, vp, NT) - deltas[h]))
            dsc = pc * (sig * (_dot(dob, vc_, NT) - deltas[h]))
            dspb, dscb = dsp.astype(BF16), dsc.astype(BF16)
            dqs.append(_dot(dspb, kp, NN) + _dot(dscb, kc_, NN))
            sdo = (sig * dos[h]).astype(BF16)
            dk_c += _dot(dscb, qs[h], TN)
            dk_p += _dot(dspb, qs[h], TN)
            dv_c += _dot(pc.astype(BF16), sdo, TN)
            dv_p += _dot(pp.astype(BF16), sdo, TN)
            dsacc_ref[h, :, :BLK] += dsp
            dsacc_ref[h, :, BLK:] += dsc
            sk_ref[h] += jnp.zeros((8, 128), F32) + jnp.sum(-deltas[h] * (1.0 - sig))
        dq_ref[...] = SCALE * jnp.where(lo, dqs[0], dqs[1])
        cur = pl.ds(pl.multiple_of(b * BLK, BLK), BLK)
        prev = pl.ds(pl.multiple_of(jnp.maximum(b - 1, 0) * BLK, BLK), BLK)
        dk_ref[cur, :] += dk_c
        dk_ref[prev, :] += dk_p
        dv_ref[cur, :] += dv_c
        dv_ref[prev, :] += dv_p

        @pl.when(jnp.logical_and(r == d - 1, b == nb - 1))
        def _():
            bi = bidx_ref[...]
            lane = lax.broadcasted_iota(jnp.int32, (8, 128), 1)
            sub = lax.broadcasted_iota(jnp.int32, (8, 128), 0)
            for h in range(2):
                acc = dsacc_ref[h]
                row = jnp.where(jnp.logical_and(sub == 1, lane == 0), sk_ref[h], 0.0)
                for kk in range(NUM_BUCKETS):
                    tot = jnp.sum(jnp.where(bi == kk, acc, 0.0))
                    row = jnp.where(jnp.logical_and(sub == 0, lane == kk), tot, row)
                stat_ref[h] = row + jnp.where(sub == 0, sin_ref[h], 0.0)

    shape = jax.ShapeDtypeStruct((rows, d * n_pairs * 128), F32)
    dq, dk, dv, stats = pl.pallas_call(
        body, name=name, grid=(n_pairs, d, nb),
        in_specs=[smem, smem, _band_spec(qa, qc, False), _band_spec(ka, kc, True), _band_spec(ka, kc, False),
                  _band_spec(va, vc, True), _band_spec(va, vc, False), full, blk_spec, blk_spec, blk_spec, stat_spec],
        out_specs=[blk_spec, cls_spec, cls_spec, stat_spec],
        out_shape=[shape, shape, shape, jax.ShapeDtypeStruct((2 * n_pairs, 8, 128), F32)],
        scratch_shapes=[pltpu.VMEM((2, BLK, 2 * BLK), F32), pltpu.VMEM((2, BLK, 2 * BLK), F32),
                        pltpu.VMEM((2, 8, 128), F32)],
        compiler_params=_params(("arbitrary", "arbitrary", "arbitrary")))(
            tab, sinks, _classes(qa, d), _classes(ka, d), _classes(ka, d), _classes(va, d), _classes(va, d), bidx_g,
            _classes(o, d), _classes(lse, d), _classes(do, d), stats_in)
    width = n_pairs * 128
    return dq.reshape(S, width), dk.reshape(S, width), dv.reshape(S, width), stats


def _comb_fwd(o_g, lse_g):
    def body(o0, o1, o2, l0, l1, l2, out_ref, outb_ref, lse_ref):
        a0, a1, a2 = l0[...], l1[...], l2[...]
        m = jnp.maximum(jnp.maximum(a0, a1), a2)
        e0, e1, e2 = jnp.exp(a0 - m), jnp.exp(a1 - m), jnp.exp(a2 - m)
        tot = e0 + e1 + e2
        out = (e0 * o0[...] + e1 * o1[...] + e2 * o2[...]) / tot
        out_ref[...] = out
        outb_ref[...] = out.astype(BF16)
        lse_ref[...] = m + jnp.log(tot)

    spec = _row_spec(4 * HD)
    f32 = jax.ShapeDtypeStruct((S, 4 * HD), F32)
    return pl.pallas_call(
        body, name="comb_fwd", grid=(S // TR,), in_specs=[spec] * 6, out_specs=[spec] * 3,
        out_shape=[f32, jax.ShapeDtypeStruct((S, 4 * HD), BF16), f32],
        compiler_params=_params(("parallel",)))(*o_g, *lse_g)


def _split2(x):
    hi = x.astype(BF16)
    return hi, (x - hi.astype(F32)).astype(BF16)


def _tri_sum(x, tri):
    hi, lo = _split2(x)
    return _dot(hi, tri, NN) + _dot(lo, tri, NN)


def _tri(strict_upper):
    r = lax.broadcasted_iota(jnp.int32, (BLK, BLK), 0)
    c = lax.broadcasted_iota(jnp.int32, (BLK, BLK), 1)
    return jnp.where(r > c if strict_upper else r < c, 1.0, 0.0).astype(BF16)


def _sb_terms(qh, kj, before):
    z = _dot(qh, kj, NT)
    lsp = jnp.minimum(z, 0.0) - jnp.log(1.0 + jnp.exp(-jnp.abs(z)))
    return lsp, jnp.where(before, lsp - z, 0.0)


def _sb_before(i, j):
    t = lax.broadcasted_iota(jnp.int32, (BLK, BLK), 0) + i * BLK
    s = lax.broadcasted_iota(jnp.int32, (BLK, BLK), 1) + j * BLK
    return s < t


C_COL = 3072 // 128


def _sb_fwd(proj):
    blk = lambda off: pl.BlockSpec((BLK, 128), lambda p, i: (i, off + p))
    col = lambda off: pl.BlockSpec((S, 128), lambda p, i: (0, off + p))
    out = pl.BlockSpec((BLK, 128), lambda p, i: (i, p))

    def body(q_ref, k_ref, v_ref, o_ref, ob_ref, tot_ref):
        i = pl.program_id(1)
        lo = _lane_lo()
        qs = _two_heads(q_ref[...] * SCALE, lo)
        suffix = _tri(True)

        def step(n, carry):
            j = i - n
            rows = pl.ds(pl.multiple_of(j * BLK, BLK), BLK)
            kj, vj = k_ref[rows, :].astype(BF16), v_ref[rows, :].astype(BF16)
            before = _sb_before(i, j)
            new = []
            for h in range(2):
                acc, rest = carry[2 * h], carry[2 * h + 1]
                lsp, lk = _sb_terms(qs[h], kj, before)
                w = jnp.where(before, jnp.exp(lsp + _tri_sum(lk, suffix) + rest), 0.0)
                new += [acc + _dot(w.astype(BF16), vj, NN), rest + jnp.sum(lk, axis=1, keepdims=True)]
            return tuple(new)

        zero = (jnp.zeros((BLK, 128), F32), jnp.zeros((BLK, 1), F32))
        acc0, rest0, acc1, rest1 = lax.fori_loop(0, i + 1, step, zero + zero)
        o = jnp.where(lo, acc0, acc1)
        o_ref[...] = o
        ob_ref[...] = o.astype(BF16)
        tot_ref[...] = jnp.where(lo, rest0, rest1) + jnp.zeros((BLK, 128), F32)

    f32 = jax.ShapeDtypeStruct((S, 4 * HD), F32)
    return pl.pallas_call(
        body, name="sb_fwd", grid=(2, NQB), in_specs=[blk(C_COL), col(C_COL + 2), col(C_COL + 4)],
        out_specs=[out, out, out], out_shape=[f32, jax.ShapeDtypeStruct((S, 4 * HD), BF16), f32],
        compiler_params=_params(("parallel", "arbitrary")))(proj, proj, proj)


def _sb_bwd(proj, tot, do):
    blk = lambda off: pl.BlockSpec((BLK, 128), lambda p, i: (i, off + p))
    col = lambda off: pl.BlockSpec((S, 128), lambda p, i: (0, off + p))

    def body(q_ref, k_ref, v_ref, tot_ref, do_ref, dq_ref, dk_ref, dv_ref):
        i = pl.program_id(1)

        @pl.when(i == 0)
        def _():
            dk_ref[...] = jnp.zeros_like(dk_ref)
            dv_ref[...] = jnp.zeros_like(dv_ref)

        lo = _lane_lo()
        qs = _two_heads(q_ref[...] * SCALE, lo)
        dos = _two_heads(do_ref[...], lo)
        tots = _per_head(tot_ref[...], lo)
        prefix = _tri(False)

        def step(j, carry):
            rows = pl.ds(pl.multiple_of(j * BLK, BLK), BLK)
            kj, vj = k_ref[rows, :].astype(BF16), v_ref[rows, :].astype(BF16)
            before = _sb_before(i, j)
            new = []
            dk_j = jnp.zeros((BLK, 128), F32)
            dv_j = jnp.zeros((BLK, 128), F32)
            for h in range(2):
                dq, keep_left, g_left = carry[3 * h], carry[3 * h + 1], carry[3 * h + 2]
                lsp, lk = _sb_terms(qs[h], kj, before)
                log_rest = tots[h] - keep_left - lk - _tri_sum(lk, prefix)
                w = jnp.where(before, jnp.exp(lsp + log_rest), 0.0)
                g = w * _dot(dos[h], vj, NT)
                g_before = g_left + _dot(g.astype(BF16), prefix, NN)
                beta = jnp.exp(lsp)
                dz = jnp.where(before, g * (1.0 - beta) - g_before * beta, 0.0).astype(BF16)
                dk_j += _dot(dz, qs[h], TN)
                dv_j += _dot(w.astype(BF16), dos[h], TN)
                new += [dq + _dot(dz, kj, NN), keep_left + jnp.sum(lk, axis=1, keepdims=True),
                        g_left + jnp.sum(g, axis=1, keepdims=True)]
            dk_ref[rows, :] += dk_j
            dv_ref[rows, :] += dv_j
            return tuple(new)

        zero = (jnp.zeros((BLK, 128), F32), jnp.zeros((BLK, 1), F32), jnp.zeros((BLK, 1), F32))
        out = lax.fori_loop(0, i + 1, step, zero + zero)
        dq_ref[...] = SCALE * jnp.where(lo, out[0], out[3])

    out_blk = pl.BlockSpec((BLK, 128), lambda p, i: (i, p))
    out_col = pl.BlockSpec((S, 128), lambda p, i: (0, p))
    f32 = jax.ShapeDtypeStruct((S, 4 * HD), F32)
    return pl.pallas_call(
        body, name="sb_bwd", grid=(2, NQB),
        in_specs=[blk(C_COL), col(C_COL + 2), col(C_COL + 4), out_blk, out_blk],
        out_specs=[out_blk, out_col, out_col], out_shape=[f32, f32, f32],
        compiler_params=_params(("arbitrary", "arbitrary")))(proj, proj, proj, tot, do)


TG = 256
GATE_BLK0 = OFF_GATE // TG


def _gate_specs():
    grid = (D // TG, S // TG)
    p_specs = [pl.BlockSpec((TG, TG), functools.partial(lambda c, r, br: (r, GATE_BLK0 + br * (D // TG) + c), br=br))
               for br in range(3)]
    b_spec = pl.BlockSpec((3, TG), lambda c, r: (0, c))
    t_spec = pl.BlockSpec((TG, TG), lambda c, r: (r, c))
    return grid, p_specs, b_spec, t_spec


def _sigmoid(x):
    return 1.0 / (1.0 + jnp.exp(-x))


def _three_rows(rows):
    sub = lax.broadcasted_iota(jnp.int32, (3, rows[0].shape[1]), 0)
    return jnp.where(sub == 0, rows[0], jnp.where(sub == 1, rows[1], rows[2]))


def _gate_fwd(proj, b_gate, br):
    grid, p_specs, b_spec, t_spec = _gate_specs()

    def body(p0, p1, p2, b_ref, r0, r1, r2, out_ref):
        acc = jnp.zeros((TG, TG), F32)
        for n, (p, r) in enumerate(((p0, r0), (p1, r1), (p2, r2))):
            acc += _sigmoid(p[...] + b_ref[n:n + 1, :]) * r[...]
        out_ref[...] = acc.astype(BF16)

    return pl.pallas_call(
        body, name="gate_fwd", grid=grid, in_specs=p_specs + [b_spec] + [t_spec] * 3, out_specs=t_spec,
        out_shape=jax.ShapeDtypeStruct((S, D), BF16),
        compiler_params=_params(("parallel", "parallel")))(proj, proj, proj, b_gate, *br)


def _gate_bwd(proj, b_gate, br, dmerged):
    grid, p_specs, b_spec, t_spec = _gate_specs()

    def body(p0, p1, p2, b_ref, r0, r1, r2, dm_ref, e0, e1, e2, g0, g1, g2, db_ref):
        dm = dm_ref[...]
        rows = []
        for n, (p, r, e_ref, dg_ref) in enumerate(((p0, r0, e0, g0), (p1, r1, e1, g1), (p2, r2, e2, g2))):
            g = _sigmoid(p[...] + b_ref[n:n + 1, :])
            e_ref[...] = (dm * g).astype(BF16)
            dpre = dm * r[...] * g * (1.0 - g)
            dg_ref[...] = dpre.astype(BF16)
            rows.append(jnp.sum(dpre, axis=0, keepdims=True))
        db = _three_rows(rows)

        @pl.when(pl.program_id(1) == 0)
        def _():
            db_ref[...] = db

        @pl.when(pl.program_id(1) > 0)
        def _():
            db_ref[...] += db

    bf = jax.ShapeDtypeStruct((S, D), BF16)
    out = pl.pallas_call(
        body, name="gate_bwd", grid=grid, in_specs=p_specs + [b_spec] + [t_spec] * 4,
        out_specs=[t_spec] * 6 + [b_spec], out_shape=[bf] * 6 + [jax.ShapeDtypeStruct((3, D), F32)],
        compiler_params=_params(("parallel", "arbitrary")))(proj, proj, proj, b_gate, *br, dmerged)
    return out[:3], out[3:6], out[6]


TC = 256
N_FF_BLK = D_FF // TC
GELU_C = math.sqrt(2.0 / math.pi)


def _shift_down(x, n):
    rows = lax.broadcasted_iota(jnp.int32, x.shape, 0)
    return jnp.where(rows >= n, pltpu.roll(x, n, axis=0), 0.0)


def _shift_up(x, n):
    rows = lax.broadcasted_iota(jnp.int32, x.shape, 0)
    return jnp.where(rows < x.shape[0] - n, pltpu.roll(x, x.shape[0] - n, axis=0), 0.0)


def _conv(u, w, b):
    return w[2:3, :] * u + w[1:2, :] * _shift_down(u, 1) + w[0:1, :] * _shift_down(u, 2) + b


def _gelu_parts(x):
    inner = GELU_C * (x + 0.044715 * x * x * x)
    t = jnp.tanh(inner)
    gelu = 0.5 * x * (1.0 + t)
    dgelu = 0.5 * (1.0 + t) + 0.5 * x * (1.0 - t * t) * GELU_C * (1.0 + 3 * 0.044715 * x * x)
    return gelu, dgelu


def _conv_specs():
    ug = pl.BlockSpec((S, TC), lambda c: (0, c))
    uv = pl.BlockSpec((S, TC), lambda c: (0, N_FF_BLK + c))
    wg = pl.BlockSpec((3, TC), lambda c: (0, c))
    wv = pl.BlockSpec((3, TC), lambda c: (0, N_FF_BLK + c))
    bg = pl.BlockSpec((1, TC), lambda c: (0, c))
    bv = pl.BlockSpec((1, TC), lambda c: (0, N_FF_BLK + c))
    return ug, uv, wg, wv, bg, bv


def _conv_fwd(u, conv_w, conv_b):
    ug, uv, wg, wv, bg, bv = _conv_specs()

    def body(ug_ref, uv_ref, wg_ref, wv_ref, bg_ref, bv_ref, a_ref):
        gc = _conv(ug_ref[...], wg_ref[...], bg_ref[...])
        vc = _conv(uv_ref[...], wv_ref[...], bv_ref[...])
        a_ref[...] = (_gelu_parts(gc)[0] * vc).astype(BF16)

    return pl.pallas_call(
        body, name="conv_fwd", grid=(N_FF_BLK,), in_specs=[ug, uv, wg, wv, bg, bv], out_specs=ug,
        out_shape=jax.ShapeDtypeStruct((S, D_FF), BF16),
        compiler_params=_params(("parallel",)))(u, u, conv_w, conv_w, conv_b, conv_b)


def _conv_bwd(u, conv_w, conv_b, da):
    ug, uv, wg, wv, bg, bv = _conv_specs()

    def back(duc, u, w):
        du = w[2:3, :] * duc + w[1:2, :] * _shift_up(duc, 1) + w[0:1, :] * _shift_up(duc, 2)
        dw = _three_rows([jnp.sum(duc * _shift_down(u, 2), axis=0, keepdims=True),
                          jnp.sum(duc * _shift_down(u, 1), axis=0, keepdims=True),
                          jnp.sum(duc * u, axis=0, keepdims=True)])
        return du, dw, jnp.sum(duc, axis=0, keepdims=True)

    def body(ug_ref, uv_ref, wg_ref, wv_ref, bg_ref, bv_ref, da_ref, dug_ref, duv_ref, dwg_ref, dwv_ref, dbg_ref, dbv_ref):
        u_g, u_v = ug_ref[...], uv_ref[...]
        gc = _conv(u_g, wg_ref[...], bg_ref[...])
        vc = _conv(u_v, wv_ref[...], bv_ref[...])
        gelu, dgelu = _gelu_parts(gc)
        da = da_ref[...]
        du, dw, db = back(da * vc * dgelu, u_g, wg_ref[...])
        dug_ref[...] = du.astype(BF16)
        dwg_ref[...] = dw
        dbg_ref[...] = db
        du, dw, db = back(da * gelu, u_v, wv_ref[...])
        duv_ref[...] = du.astype(BF16)
        dwv_ref[...] = dw
        dbv_ref[...] = db

    return pl.pallas_call(
        body, name="conv_bwd", grid=(N_FF_BLK,), in_specs=[ug, uv, wg, wv, bg, bv, ug],
        out_specs=[ug, ug, wg, wg, bg, bg],
        out_shape=[jax.ShapeDtypeStruct((S, D_FF), BF16), jax.ShapeDtypeStruct((S, D_FF), BF16),
                   jax.ShapeDtypeStruct((3, D_FF), F32), jax.ShapeDtypeStruct((3, D_FF), F32),
                   jax.ShapeDtypeStruct((1, D_FF), F32), jax.ShapeDtypeStruct((1, D_FF), F32)],
        compiler_params=_params(("parallel",)))(u, u, conv_w, conv_w, conv_b, conv_b, da)


def _adamw(name, w, g, m, v):
    shape = w.shape
    cols = shape[-1]
    flat = [t.reshape(-1, cols) for t in (w, g, m, v)]
    r = flat[0].shape[0]
    tr = min(128, r)

    def body(w_ref, g_ref, m_ref, v_ref, d_ref, mo_ref, vo_ref):
        g = g_ref[...]
        m = ADAM_B1 * m_ref[...] + (1.0 - ADAM_B1) * g
        v = ADAM_B2 * v_ref[...] + (1.0 - ADAM_B2) * (g * g)
        m_hat = m / (1.0 - ADAM_B1 ** ADAM_STEP)
        v_hat = v / (1.0 - ADAM_B2 ** ADAM_STEP)
        d_ref[...] = -ADAM_LR * (m_hat / (jnp.sqrt(v_hat) + ADAM_EPS) + ADAM_WD * w_ref[...])
        mo_ref[...] = m
        vo_ref[...] = v

    spec = pl.BlockSpec((tr, cols), lambda i: (i, 0))
    outs = pl.pallas_call(
        body, name=name, grid=(pl.cdiv(r, tr),), in_specs=[spec] * 4, out_specs=[spec] * 3,
        out_shape=[jax.ShapeDtypeStruct((r, cols), F32)] * 3, compiler_params=_params(("parallel",)))(*flat)
    return [t.reshape(shape) for t in outs]


def _place():
    x, y, c = lax.axis_index("x"), lax.axis_index("y"), lax.axis_index("c")
    chips = [(1 - x, y), (x, 1 - y), (1 - x, 1 - y)]
    return x, y, c, chips


def _scalars(*vals):
    return jnp.stack([jnp.asarray(v, jnp.int32) for v in vals])


def _cast_into_slot(name, w, chip):
    _, k, n4 = w.shape
    tr = min(256, k)

    def body(chip_ref, w_ref, o_ref):
        o_ref[...] = w_ref[...].astype(BF16)

    return pl.pallas_call(
        body, name=name,
        grid_spec=pltpu.PrefetchScalarGridSpec(
            num_scalar_prefetch=1, grid=(DEPTH, k // tr),
            in_specs=[pl.BlockSpec((None, tr, n4), lambda l, i, s: (l, i, 0))],
            out_specs=pl.BlockSpec((None, None, tr, n4), lambda l, i, s: (s[0], l, i, 0))),
        out_shape=jax.ShapeDtypeStruct((N_CHIPS, DEPTH, k, n4), BF16),
        compiler_params=_params(("parallel", "parallel")))(_scalars(chip), w)


def _gather_weights(bufs, smalls):
    nb, n = len(bufs), len(bufs) + len(smalls)

    def body(*refs):
        ins, outs = refs[:n], refs[n:2 * n]
        local_sem, send_sem, recv_sem = refs[2 * n:]
        x, y, c, chips = _place()
        me = 2 * x + y
        sibling = (x, y, 1 - c)
        local = [pltpu.make_async_copy(ins[a], outs[a].at[me], local_sem.at[a - nb]) for a in range(nb, n)]
        for cp in local:
            cp.start()

        def over_ici(a, k, from_chip):
            return pltpu.make_async_remote_copy(
                src_ref=ins[a].at[me, c] if a < nb else ins[a].at[c], dst_ref=outs[a].at[from_chip, c],
                send_sem=send_sem.at[a, k], recv_sem=recv_sem.at[a, k],
                device_id=(*chips[k], c), device_id_type=MESH)

        def over_d2d(a, k, layer):
            rows = outs[a].at[2 * chips[k][0] + chips[k][1], layer]
            return pltpu.make_async_remote_copy(
                src_ref=rows, dst_ref=rows, send_sem=send_sem.at[a, 3 + k], recv_sem=recv_sem.at[a, 3 + k],
                device_id=sibling, device_id_type=MESH)

        sends = [over_ici(a, k, me) for a in range(n) for k in range(3)]
        for cp in sends:
            cp.start()
        passed = []
        for a in range(n):
            for k in range(3):
                over_ici(a, k, 2 * chips[k][0] + chips[k][1]).wait_recv()
                cp = over_d2d(a, k, c)
                cp.start()
                passed.append(cp)
        for a in range(n):
            for k in range(3):
                over_d2d(a, k, 1 - c).wait_recv()
        for cp in sends + passed:
            cp.wait_send()
        for cp in local:
            cp.wait()

    out_shape = [jax.ShapeDtypeStruct(b.shape, b.dtype) for b in bufs]
    out_shape += [jax.ShapeDtypeStruct((N_CHIPS,) + s.shape, s.dtype) for s in smalls]
    return pl.pallas_call(
        body, name="gather_weights", in_specs=[ANY] * n, out_specs=[ANY] * n, out_shape=out_shape,
        input_output_aliases={a: a for a in range(nb)},
        scratch_shapes=[pltpu.SemaphoreType.DMA((n - nb,)), pltpu.SemaphoreType.DMA((n, 6)),
                        pltpu.SemaphoreType.DMA((n, 6))],
    )(*bufs, *smalls)


def _swap_layers(grads):
    n = len(grads)

    def body(*refs):
        ins, got = refs[:n], refs[n:2 * n]
        send_sem, recv_sem = refs[2 * n:]
        x, y, c, _ = _place()
        sends = [pltpu.make_async_remote_copy(
            src_ref=ins[a].at[1 - c], dst_ref=got[a], send_sem=send_sem.at[a], recv_sem=recv_sem.at[a],
            device_id=(x, y, 1 - c), device_id_type=MESH) for a in range(n)]
        for cp in sends:
            cp.start()
        for cp in sends:
            cp.wait()

    return pl.pallas_call(
        body, name="swap_layers", in_specs=[ANY] * n, out_specs=[ANY] * n,
        out_shape=[jax.ShapeDtypeStruct(g.shape[1:], g.dtype) for g in grads],
        scratch_shapes=[pltpu.SemaphoreType.DMA((n,)), pltpu.SemaphoreType.DMA((n,))],
    )(*grads)


def _add_sibling(name, g, got, c):
    _, _, k, n4 = g.shape
    rows = N_CHIPS * k
    tr = min(512, rows)

    def body(c_ref, g_ref, got_ref, o_ref):
        o_ref[...] = (g_ref[...].astype(F32) + got_ref[...].astype(F32)).astype(BF16)

    out = pl.pallas_call(
        body, name=name,
        grid_spec=pltpu.PrefetchScalarGridSpec(
            num_scalar_prefetch=1, grid=(rows // tr,),
            in_specs=[pl.BlockSpec((None, tr, n4), lambda i, s: (s[0], i, 0)),
                      pl.BlockSpec((tr, n4), lambda i, s: (i, 0))],
            out_specs=pl.BlockSpec((tr, n4), lambda i, s: (i, 0))),
        out_shape=jax.ShapeDtypeStruct((rows, n4), BF16),
        compiler_params=_params(("parallel",)))(_scalars(c), g.reshape(DEPTH, rows, n4), got.reshape(rows, n4))
    return out.reshape(N_CHIPS, k, n4)


def _scatter_shards(parts):
    n = len(parts)

    def body(*refs):
        ins, outs = refs[:n], refs[n:2 * n]
        send_sem, recv_sem = refs[2 * n:]
        x, y, c, chips = _place()
        me = 2 * x + y
        sends = [pltpu.make_async_remote_copy(
            src_ref=ins[a].at[2 * chips[k][0] + chips[k][1]], dst_ref=outs[a].at[me],
            send_sem=send_sem.at[a, k], recv_sem=recv_sem.at[a, k],
            device_id=(*chips[k], c), device_id_type=MESH) for a in range(n) for k in range(3)]
        for cp in sends:
            cp.start()
        for a in range(n):
            for k in range(3):
                src_chip = 2 * chips[k][0] + chips[k][1]
                pltpu.make_async_remote_copy(
                    src_ref=ins[a].at[me], dst_ref=outs[a].at[src_chip],
                    send_sem=send_sem.at[a, k], recv_sem=recv_sem.at[a, k],
                    device_id=(x, y, c), device_id_type=MESH).wait_recv()
        for cp in sends:
            cp.wait_send()

    return pl.pallas_call(
        body, name="scatter_shards", in_specs=[ANY] * n, out_specs=[ANY] * n,
        out_shape=[jax.ShapeDtypeStruct(p.shape, p.dtype) for p in parts],
        scratch_shapes=[pltpu.SemaphoreType.DMA((n, 3)), pltpu.SemaphoreType.DMA((n, 3))],
    )(*parts)


def _add_chips(name, part, arrived, chip, c):
    _, k, n4 = part.shape
    tr = min(512, k)

    def body(s_ref, own_ref, a1_ref, a2_ref, a3_ref, o_ref):
        o_ref[...] = ((own_ref[...].astype(F32) + a1_ref[...].astype(F32)) + a2_ref[...].astype(F32)) + a3_ref[...].astype(F32)

    def pick(j):
        return pl.BlockSpec((None, tr, n4), lambda i, s: (s[j], i, 0))

    return pl.pallas_call(
        body, name=name,
        grid_spec=pltpu.PrefetchScalarGridSpec(
            num_scalar_prefetch=1, grid=(k // tr,), in_specs=[pick(0), pick(2), pick(3), pick(4)], out_specs=pick(1)),
        out_shape=jax.ShapeDtypeStruct((DEPTH, k, n4), F32),
        compiler_params=_params(("parallel",)))(
            _scalars(chip, c, (chip + 1) % N_CHIPS, (chip + 2) % N_CHIPS, (chip + 3) % N_CHIPS),
            part, arrived, arrived, arrived)


def _join_layers(bufs):
    n = len(bufs)

    def body(*refs):
        ins, outs = refs[:n], refs[n:2 * n]
        send_sem, recv_sem = refs[2 * n:]
        x, y, c, _ = _place()
        sends = [pltpu.make_async_remote_copy(
            src_ref=ins[a].at[c], dst_ref=outs[a].at[c], send_sem=send_sem.at[a], recv_sem=recv_sem.at[a],
            device_id=(x, y, 1 - c), device_id_type=MESH) for a in range(n)]
        for cp in sends:
            cp.start()
        for a in range(n):
            sends[a].wait_send()
            pltpu.make_async_remote_copy(
                src_ref=ins[a].at[c], dst_ref=outs[a].at[1 - c], send_sem=send_sem.at[a], recv_sem=recv_sem.at[a],
                device_id=(x, y, 1 - c), device_id_type=MESH).wait_recv()

    return pl.pallas_call(
        body, name="join_layers", in_specs=[ANY] * n, out_specs=[ANY] * n,
        out_shape=[jax.ShapeDtypeStruct(b.shape, b.dtype) for b in bufs],
        input_output_aliases={a: a for a in range(n)},
        scratch_shapes=[pltpu.SemaphoreType.DMA((n,)), pltpu.SemaphoreType.DMA((n,))],
    )(*bufs)


def _all_reduce_small(block):
    r = block.shape[0]

    def body(x_ref, out_ref, slots, send_sem, recv_sem):
        x, y, c, _ = _place()
        me = 4 * x + 2 * y + c
        slots[me] = x_ref[...]
        sends = []
        for mask in range(1, N_DEV):
            fx, fy, fc = (mask >> 2) & 1, (mask >> 1) & 1, mask & 1
            peer = (x ^ fx, y ^ fy, c ^ fc)
            cp = pltpu.make_async_remote_copy(
                src_ref=x_ref, dst_ref=slots.at[me], send_sem=send_sem.at[mask - 1], recv_sem=recv_sem.at[mask - 1],
                device_id=peer, device_id_type=MESH)
            cp.start()
            sends.append(cp)
        for mask in range(1, N_DEV):
            src = me ^ mask
            pltpu.make_async_remote_copy(
                src_ref=x_ref, dst_ref=slots.at[src], send_sem=send_sem.at[mask - 1], recv_sem=recv_sem.at[mask - 1],
                device_id=(x, y, c), device_id_type=MESH).wait_recv()
        for cp in sends:
            cp.wait_send()
        total = slots[0]
        for d in range(1, N_DEV):
            total = total + slots[d]
        out_ref[...] = total

    vmem = pl.BlockSpec(memory_space=pltpu.VMEM)
    return pl.pallas_call(
        body, name="all_reduce_small", in_specs=[vmem], out_specs=vmem,
        out_shape=jax.ShapeDtypeStruct((r, 128), F32),
        scratch_shapes=[pltpu.VMEM((N_DEV, r, 128), F32), pltpu.SemaphoreType.DMA((N_DEV - 1,)),
                        pltpu.SemaphoreType.DMA((N_DEV - 1,))],
        compiler_params=pltpu.CompilerParams(vmem_limit_bytes=VMEM_LIMIT))(block)


B_Q_COL = 2304 // 128
B_K0, B_V0, B_END = 2816, 2944, 3072


def _full_cols(w_g):
    return w_g.transpose(1, 0, 2).reshape(w_g.shape[1], -1)


def _group_src(proj, g):
    if A_GROUPS[g][1] == 1:
        return ((proj, 2 * g), (proj, 6 + 2 * g), (proj, 12 + 2 * g))
    packed = jnp.concatenate([proj[:, t * 768 + g * 256:t * 768 + (g + 1) * 256] for t in range(3)], axis=1)
    return ((packed, 0), (packed, 2), (packed, 4))


def _kv_expand(kv):
    return jnp.broadcast_to(kv.reshape(S, 2, 1, HD), (S, 2, 4, HD)).reshape(S, 8 * HD)


def _kv_reduce(dkv):
    return dkv.reshape(S, 2, 4, HD).sum(axis=2).reshape(S, 2 * HD)


def _mixer_fwd(h1, w, rel_bias, sinks_l, bidx):
    proj = _mm_nn("proj_in", h1, w["w_in"], F32, tn=1152)
    no_sinks = jnp.full((4,), NEG, F32)
    srcs = [_group_src(proj, g) for g in range(3)]
    o_g, lse_g = [], []
    for g, (_, d) in enumerate(A_GROUPS):
        o, lse = _band_fwd("band_fwd_g%d" % g, d, 2, BLK, 4 * g, srcs[g], bidx[g], rel_bias, no_sinks)
        o_g.append(o)
        lse_g.append(lse)
    o_a32, o_a, lse_a = _comb_fwd(o_g, lse_g)
    src_b = ((proj, B_Q_COL), (_kv_expand(proj[:, B_K0:B_V0]), 0), (_kv_expand(proj[:, B_V0:B_END]), 0))
    o_b32, lse_b = _band_fwd("band_fwd_b", 1, 4, BLK - 1, N_A, src_b, bidx[3], rel_bias, sinks_l)
    o_b = o_b32.astype(BF16)
    o_c32, o_c, tot_c = _sb_fwd(proj)
    br = [_mm_nn("branch_a", o_a, w["w_br_a"], F32), _mm_nn("branch_b", o_b, w["w_br_b"], F32),
          _mm_nn("branch_c", o_c, w["w_br_c"], F32)]
    merged = _gate_fwd(proj, w["b_gate"], br)
    mo = _mm_nn("out_proj", merged, w["w_out"], F32)
    saved = dict(proj=proj, srcs=srcs, src_b=src_b, o_a32=o_a32, lse_a=lse_a, o_b32=o_b32, lse_b=lse_b, tot_c=tot_c,
                 o_a=o_a, o_b=o_b, o_c=o_c, br=br, merged=merged)
    return mo, saved


def _mixer_bwd(d_mo, h1, w, sv, rel_bias, sinks_l, bidx, stats_in, into):
    grads, layer = {}, w["layer"]
    dmerged = _mm_nt("out_proj_dx", d_mo, w["w_out"], F32)
    grads["w_out"] = _mm_tn_stacked("out_proj_dw", sv["merged"], d_mo, layer, into.get("w_out"), True)
    e, dgate, db_gate = _gate_bwd(sv["proj"], w["b_gate"], sv["br"], dmerged)
    grads["b_gate"] = db_gate
    d_o = {}
    for n, name in enumerate("abc"):
        d_o[name] = _mm_nt("branch_%s_dx" % name, e[n], w["w_br_" + name], F32)
        grads["w_br_" + name] = _mm_tn_stacked("branch_%s_dw" % name, sv["o_" + name], e[n], layer,
                                               into.get("w_br_" + name), False)
    no_sinks = jnp.full((4,), NEG, F32)
    dqs, dks, dvs, stats = [], [], [], []
    for g, (_, d) in enumerate(A_GROUPS):
        dq, dk, dv, st = _band_bwd("band_bwd_g%d" % g, d, 2, BLK, 4 * g, sv["srcs"][g], bidx[g], rel_bias, no_sinks,
                                   sv["o_a32"], sv["lse_a"], d_o["a"], stats_in[4 * g:4 * g + 4])
        dqs.append(dq)
        dks.append(dk)
        dvs.append(dv)
        stats.append(st)
    dq_b, dk_x, dv_x, st = _band_bwd("band_bwd_b", 1, 4, BLK - 1, N_A, sv["src_b"], bidx[3], rel_bias, sinks_l,
                                     sv["o_b32"], sv["lse_b"], d_o["b"], stats_in[N_A:])
    stats = jnp.concatenate(stats + [st], axis=0)
    dcq, dck, dcv = _sb_bwd(sv["proj"], sv["tot_c"], d_o["c"])
    cols = dqs + dks + dvs + [dq_b, _kv_reduce(dk_x), _kv_reduce(dv_x), dcq, dck, dcv]
    dproj = jnp.concatenate([t.astype(BF16) for t in cols] + list(dgate), axis=1)
    dh1 = _mm_nt("proj_in_dx", dproj, w["w_in"], F32, tk=1152)
    dproj_s = dproj.reshape(S, N_CHIPS, IN_SHARD).transpose(1, 0, 2)
    grads["w_in"] = _proj_in_dw(h1, dproj_s, layer, into.get("w_in"))
    return dh1, grads, stats


def _ffn_fwd(h2, w):
    u = _mm_nn("ffn_up", h2, w["w_up"], F32, tn=1024)
    a = _conv_fwd(u, w["conv_w"], w["conv_b"])
    dn = _mm_nn("ffn_down", a, w["w_down"], F32)
    return dn, dict(u=u, a=a)


def _ffn_bwd(d_dn, h2, w, sv, into):
    grads, layer = {}, w["layer"]
    da = _mm_nt("ffn_down_dx", d_dn, w["w_down"], F32, tn=1024)
    grads["w_down"] = _mm_tn_stacked("ffn_down_dw", sv["a"], d_dn, layer, into.get("w_down"), True)
    dug, duv, dwg, dwv, dbg, dbv = _conv_bwd(sv["u"], w["conv_w"], w["conv_b"], da)
    du = jnp.concatenate([dug, duv], axis=1)
    grads["conv_w"] = jnp.concatenate([dwg, dwv], axis=1)
    grads["conv_b"] = jnp.concatenate([dbg, dbv], axis=1)
    dh2 = _mm_nt("ffn_up_dx", du, w["w_up"], F32)
    grads["w_up"] = _mm_tn_stacked("ffn_up_dw", h2, du, layer, into.get("w_up"), False, tn=1024)
    return dh2, grads


BIG = ("w_in", "w_br_a", "w_br_b", "w_br_c", "w_out", "w_up", "w_down")
SMALL_ROWS = (("rel_bias", 5), ("attn_pre_norm", 16), ("attn_post_norm", 16), ("ffn_pre_norm", 16), ("ffn_post_norm", 16),
              ("sinks", 1), ("conv_b", 128), ("b_gate", 48), ("conv_w", 384), ("loss", 1))
SMALL_TOTAL = 632


def _pack_small(vals):
    rows = []
    for name, n in SMALL_ROWS:
        flat = vals[name].reshape(-1).astype(F32)
        rows.append(jnp.pad(flat, (0, n * 128 - flat.shape[0])).reshape(n, 128))
    used = sum(n for _, n in SMALL_ROWS)
    rows.append(jnp.zeros((SMALL_TOTAL - used, 128), F32))
    return jnp.concatenate(rows, axis=0)


def _unpack_small(block, shapes):
    out, row = {}, 0
    for name, n in SMALL_ROWS:
        size = int(np.prod(shapes[name]))
        out[name] = block[row:row + n].reshape(-1)[:size].reshape(shapes[name])
        row += n
    return out


def kernel(x, rel_bias, attn_pre_norm, w_in, b_gate, sinks, w_br_a, w_br_b, w_br_c, w_out, attn_post_norm, ffn_pre_norm, w_up, conv_w, conv_b, w_down, ffn_post_norm, loss_target, m_rel_bias, m_attn_pre_norm, m_w_in, m_b_gate, m_sinks, m_w_br_a, m_w_br_b, m_w_br_c, m_w_out, m_attn_post_norm, m_ffn_pre_norm, m_w_up, m_conv_w, m_conv_b, m_w_down, m_ffn_post_norm, v_rel_bias, v_attn_pre_norm, v_w_in, v_b_gate, v_sinks, v_w_br_a, v_w_br_b, v_w_br_c, v_w_out, v_attn_post_norm, v_ffn_pre_norm, v_w_up, v_conv_w, v_conv_b, v_w_down, v_ffn_post_norm):
    names = ("rel_bias", "attn_pre_norm", "w_in", "b_gate", "sinks", "w_br_a", "w_br_b", "w_br_c", "w_out",
             "attn_post_norm", "ffn_pre_norm", "w_up", "conv_w", "conv_b", "w_down", "ffn_post_norm")
    weights = dict(zip(names, (rel_bias, attn_pre_norm, w_in, b_gate, sinks, w_br_a, w_br_b, w_br_c, w_out,
                               attn_post_norm, ffn_pre_norm, w_up, conv_w, conv_b, w_down, ffn_post_norm)))
    mom1 = dict(zip(names, (m_rel_bias, m_attn_pre_norm, m_w_in, m_b_gate, m_sinks, m_w_br_a, m_w_br_b, m_w_br_c,
                            m_w_out, m_attn_post_norm, m_ffn_pre_norm, m_w_up, m_conv_w, m_conv_b, m_w_down,
                            m_ffn_post_norm)))
    mom2 = dict(zip(names, (v_rel_bias, v_attn_pre_norm, v_w_in, v_b_gate, v_sinks, v_w_br_a, v_w_br_b, v_w_br_c,
                            v_w_out, v_attn_post_norm, v_ffn_pre_norm, v_w_up, v_conv_w, v_conv_b, v_w_down,
                            v_ffn_post_norm)))

    chip = 2 * lax.axis_index("x") + lax.axis_index("y")
    core = lax.axis_index("c")
    gathered = _gather_weights([_cast_into_slot("cast_" + n, weights[n], chip) for n in BIG], [b_gate, conv_w])
    gathered = dict(zip(BIG + ("b_gate", "conv_w"), gathered))
    layers = []
    for l in range(DEPTH):
        w = {"layer": l}
        for n in ("w_in", "w_br_a", "w_br_b", "w_br_c", "w_up", "b_gate", "conv_w"):
            w[n] = _full_cols(gathered[n][:, l])
        w["w_out"] = gathered["w_out"][:, l].reshape(D, D)
        w["w_down"] = gathered["w_down"][:, l].reshape(D_FF, D)
        w["conv_b"] = conv_b[l:l + 1]
        layers.append(w)

    local = _local_step(x.reshape(S, D), loss_target.reshape(S, D), layers, rel_bias, sinks, attn_pre_norm,
                        attn_post_norm, ffn_pre_norm, ffn_post_norm)
    return _reduce_and_update(x.shape, names, weights, mom1, mom2, chip, core, *local)


def _local_step(xs, target, layers, rel_bias, sinks, attn_pre_norm, attn_post_norm, ffn_pre_norm, ffn_post_norm):
    bidx = jnp.asarray(_bucket_maps())

    saved = []
    h1 = _rms_fwd("pre_norm_first", xs, attn_pre_norm[0:1])
    x_in = xs
    for l in range(DEPTH):
        w = layers[l]
        mo, sv_mix = _mixer_fwd(h1, w, rel_bias, sinks[l], bidx)
        x_mid, h2 = _post_pre_fwd("post_attn_norm", x_in, mo, attn_post_norm[l:l + 1], ffn_pre_norm[l:l + 1])
        dn, sv_ffn = _ffn_fwd(h2, w)
        g_next = attn_pre_norm[l + 1:l + 2] if l + 1 < DEPTH else None
        x_out, h1_next = _post_pre_fwd("post_ffn_norm" if l + 1 < DEPTH else "post_ffn_norm_last", x_mid, dn,
                                       ffn_post_norm[l:l + 1], g_next)
        saved.append(dict(x_in=x_in, h1=h1, mo=mo, x_mid=x_mid, h2=h2, dn=dn, mix=sv_mix, ffn=sv_ffn))
        x_in, h1 = x_out, h1_next

    loss_row, dres = _loss_kernel(x_in, target)

    big_grads = {}
    small = [None] * DEPTH
    stats = jnp.zeros((N_BAND_Q, 8, 128), F32)
    dh_next = None
    for l in reversed(range(DEPTH)):
        w, sv = layers[l], saved[l]
        if l + 1 < DEPTH:
            pre = (saved[l + 1]["x_in"], attn_pre_norm[l + 1:l + 2], dh_next)
            dres, d_dn, dg_pre_next, dg_fpost = _norm_bwd("post_ffn_norm_bwd", dres, pre,
                                                          (sv["dn"], ffn_post_norm[l:l + 1]))
            small[l + 1]["attn_pre_norm"] = dg_pre_next
        else:
            dres, d_dn, _, dg_fpost = _norm_bwd("post_ffn_norm_last_bwd", dres, None, (sv["dn"], ffn_post_norm[l:l + 1]))
        dh2, g_ffn = _ffn_bwd(d_dn, sv["h2"], w, sv["ffn"], big_grads)
        dres, d_mo, dg_fpre, dg_apost = _norm_bwd("post_attn_norm_bwd", dres, (sv["x_mid"], ffn_pre_norm[l:l + 1], dh2),
                                                  (sv["mo"], attn_post_norm[l:l + 1]))
        dh_next, g_mix, stats = _mixer_bwd(d_mo, sv["h1"], w, sv["mix"], rel_bias, sinks[l], bidx, stats,
                                           big_grads)
        big_grads = {n: {**g_ffn, **g_mix}[n] for n in BIG}
        small[l] = dict(ffn_post_norm=dg_fpost, ffn_pre_norm=dg_fpre, attn_post_norm=dg_apost,
                        sinks=stats[N_A:, 1, 0], conv_b=g_ffn["conv_b"], b_gate=g_mix["b_gate"], conv_w=g_ffn["conv_w"])
    grad_x, _, dg_pre0, _ = _norm_bwd("pre_norm_first_bwd", dres, (saved[0]["x_in"], attn_pre_norm[0:1], dh_next), None)
    small[0]["attn_pre_norm"] = dg_pre0
    return loss_row, grad_x, big_grads, small, stats


def _reduce_and_update(x_shape, names, weights, mom1, mom2, chip, core, loss_row, grad_x, big_grads, small, stats):
    small_vals = {n: jnp.stack([small[l][n].reshape(weights[n].shape[1:]) for l in range(DEPTH)])
                  for n in ("attn_pre_norm", "attn_post_norm", "ffn_pre_norm", "ffn_post_norm", "conv_b", "sinks")}
    small_vals["b_gate"] = jnp.stack([small[l]["b_gate"] for l in range(DEPTH)])
    small_vals["conv_w"] = jnp.stack([small[l]["conv_w"] for l in range(DEPTH)])
    small_vals["rel_bias"] = stats[:, 0, :NUM_BUCKETS].T
    small_vals["loss"] = loss_row[0, :1]
    shapes = {n: v.shape for n, v in small_vals.items()}
    reduced = _unpack_small(_all_reduce_small(_pack_small(small_vals)), shapes)
    reduced["b_gate"] = lax.dynamic_slice_in_dim(reduced["b_gate"], chip * (D // N_CHIPS), D // N_CHIPS, axis=2)
    reduced["conv_w"] = lax.dynamic_slice_in_dim(reduced["conv_w"], chip * (2 * D_FF // N_CHIPS), 2 * D_FF // N_CHIPS, axis=2)

    stacked = [big_grads[n] for n in BIG]
    got = _swap_layers(stacked)
    parts = [_add_sibling("add_sibling_" + n, g, t, core) for n, g, t in zip(BIG, stacked, got)]
    arrived = _scatter_shards(parts)
    halves = [_add_chips("add_chips_" + n, p, t, chip, core) for n, p, t in zip(BIG, parts, arrived)]
    full = _join_layers(halves)
    grads = dict(zip(BIG, [f.reshape(weights[n].shape) for n, f in zip(BIG, full)]))
    for n in names:
        if n not in grads:
            grads[n] = reduced[n].reshape(weights[n].shape)

    delta, new_m, new_v = {}, {}, {}
    for n in names:
        delta[n], new_m[n], new_v[n] = _adamw("adamw_" + n, weights[n], grads[n], mom1[n], mom2[n])

    loss = reduced["loss"].reshape(())
    return (loss, grad_x.reshape(x_shape), *[grads[n] for n in names], *[delta[n] for n in names],
            *[new_m[n] for n in names], *[new_v[n] for n in names])
```

```python
import functools
import math

import numpy as np
import jax
import jax.numpy as jnp
from jax import lax
from jax.experimental import pallas as pl
from jax.experimental.pallas import tpu as pltpu

F32 = jnp.float32
BF16 = jnp.bfloat16

S = 2048
D = 1024
DEPTH = 2
HD = 64
BLK = 128
NQB = S // BLK
A_GROUPS = ((128, 1), (512, 4), (2048, 16))
N_BAND_Q = 20
N_A = 12
NUM_BUCKETS = 32
MAX_DISTANCE = 2048
D_FF = 4096
IN_COLS = 6912
IN_SHARD = IN_COLS // 4
OFF_GATE = 3840
EPS = 1e-6
SCALE = HD ** -0.5
NEG = -1e30
N_CHIPS = 4
N_DEV = 8

ADAM_LR = 0.001
ADAM_B1 = 0.9
ADAM_B2 = 0.999
ADAM_EPS = 1e-08
ADAM_WD = 0.01
ADAM_STEP = 10

VMEM_LIMIT = 56 * 1024 * 1024

NN = (((1,), (0,)), ((), ()))
NT = (((1,), (1,)), ((), ()))
TN = (((0,), (0,)), ((), ()))

MESH = pl.DeviceIdType.MESH
ANY = pl.BlockSpec(memory_space=pl.ANY)


def _dot(a, b, dims):
    return lax.dot_general(a, b, dims, preferred_element_type=F32)


def _params(sem):
    return pltpu.CompilerParams(dimension_semantics=sem, vmem_limit_bytes=VMEM_LIMIT)


def _matmul(name, a, b, out_shape, out_dtype, grid, a_spec, b_spec, o_spec, dims, acc_shape, into=None):
    nk = grid[-1]

    def body(a_ref, b_ref, *rest):
        o_ref, scratch = (rest[1], rest[2:]) if into is not None else (rest[0], rest[1:])
        part = _dot(a_ref[...].astype(BF16), b_ref[...].astype(BF16), dims)
        if nk == 1:
            o_ref[...] = part.astype(o_ref.dtype)
            return
        acc_ref, = scratch
        k = pl.program_id(len(grid) - 1)

        @pl.when(k == 0)
        def _():
            acc_ref[...] = part

        @pl.when(k > 0)
        def _():
            acc_ref[...] += part

        @pl.when(k == nk - 1)
        def _():
            o_ref[...] = acc_ref[...].astype(o_ref.dtype)

    scratch = [] if nk == 1 else [pltpu.VMEM(acc_shape, F32)]
    sem = ("parallel",) * (len(grid) - 1) + ("arbitrary",)
    ins, in_specs, aliases = [a, b], [a_spec, b_spec], {}
    if into is not None:
        ins, in_specs, aliases = ins + [into], in_specs + [ANY], {2: 0}
    return pl.pallas_call(
        body, name=name, grid=grid, in_specs=in_specs, out_specs=o_spec,
        out_shape=jax.ShapeDtypeStruct(out_shape, out_dtype), scratch_shapes=scratch,
        input_output_aliases=aliases, compiler_params=_params(sem))(*ins)


FULL_K = 8192


def _mm_tn_stacked(name, a, b, layer, into, row_sharded, tm=512, tn=512, tk=FULL_K):
    k, m = a.shape
    n = b.shape[1]
    m4, n4 = (m // N_CHIPS, n) if row_sharded else (m, n // N_CHIPS)
    tm, tn, tk = min(tm, m4), min(tn, n4), min(tk, k)
    per_m, per_n = m4 // tm, n4 // tn
    if row_sharded:
        o_map = lambda i, j, l: (layer, i // per_m, i % per_m, j)
    else:
        o_map = lambda i, j, l: (layer, j // per_n, i, j % per_n)
    return _matmul(name, a, b, (DEPTH, N_CHIPS, m4, n4), BF16, (m // tm, n // tn, k // tk),
                   pl.BlockSpec((tk, tm), lambda i, j, l: (l, i)),
                   pl.BlockSpec((tk, tn), lambda i, j, l: (l, j)),
                   pl.BlockSpec((None, None, tm, tn), o_map), TN, (tm, tn), into=into)


def _mm_nn(name, a, b, out_dtype, tm=512, tn=512, tk=FULL_K):
    m, k = a.shape
    n = b.shape[1]
    tm, tn, tk = min(tm, m), min(tn, n), min(tk, k)
    return _matmul(name, a, b, (m, n), out_dtype, (m // tm, n // tn, k // tk),
                   pl.BlockSpec((tm, tk), lambda i, j, l: (i, l)),
                   pl.BlockSpec((tk, tn), lambda i, j, l: (l, j)),
                   pl.BlockSpec((tm, tn), lambda i, j, l: (i, j)), NN, (tm, tn))


def _mm_nt(name, a, b, out_dtype, tm=512, tn=512, tk=FULL_K):
    m, k = a.shape
    n = b.shape[0]
    tm, tn, tk = min(tm, m), min(tn, n), min(tk, k)
    return _matmul(name, a, b, (m, n), out_dtype, (m // tm, n // tn, k // tk),
                   pl.BlockSpec((tm, tk), lambda i, j, l: (i, l)),
                   pl.BlockSpec((tn, tk), lambda i, j, l: (j, l)),
                   pl.BlockSpec((tm, tn), lambda i, j, l: (i, j)), NT, (tm, tn))


def _proj_in_dw(h, dproj_s, layer, into):
    tm = 512
    return _matmul("proj_in_dw", h, dproj_s, (DEPTH, N_CHIPS, D, IN_SHARD), BF16, (N_CHIPS, D // tm, 1),
                   pl.BlockSpec((S, tm), lambda j, i, l: (0, i)),
                   pl.BlockSpec((None, S, IN_SHARD), lambda j, i, l: (j, 0, 0)),
                   pl.BlockSpec((None, None, tm, IN_SHARD), lambda j, i, l: (layer, j, i, 0)), TN, (tm, IN_SHARD),
                   into=into)


TR = 256


def _row_spec(width=D):
    return pl.BlockSpec((TR, width), lambda i: (i, 0))


def _vec_spec(width=D):
    return pl.BlockSpec((1, width), lambda i: (0, 0))


def _rms(x, g):
    r = lax.rsqrt(jnp.mean(x * x, axis=-1, keepdims=True) + EPS)
    return x * r * g


def _rms_fwd(name, x, g):
    def body(x_ref, g_ref, h_ref):
        h_ref[...] = _rms(x_ref[...], g_ref[...]).astype(BF16)

    return pl.pallas_call(
        body, name=name, grid=(S // TR,), in_specs=[_row_spec(), _vec_spec()], out_specs=_row_spec(),
        out_shape=jax.ShapeDtypeStruct((S, D), BF16), compiler_params=_params(("parallel",)))(x, g)


def _post_pre_fwd(name, x, y, g_post, g_pre):
    has_pre = g_pre is not None

    def body(*refs):
        if has_pre:
            x_ref, y_ref, gp_ref, gn_ref, xn_ref, h_ref = refs
        else:
            x_ref, y_ref, gp_ref, xn_ref = refs
        xn = x_ref[...] + _rms(y_ref[...], gp_ref[...])
        xn_ref[...] = xn
        if has_pre:
            h_ref[...] = _rms(xn, gn_ref[...]).astype(BF16)

    ins = [x, y, g_post] + ([g_pre] if has_pre else [])
    in_specs = [_row_spec(), _row_spec(), _vec_spec()] + ([_vec_spec()] if has_pre else [])
    out_shape = [jax.ShapeDtypeStruct((S, D), F32)] + ([jax.ShapeDtypeStruct((S, D), BF16)] if has_pre else [])
    out_specs = [_row_spec()] + ([_row_spec()] if has_pre else [])
    out = pl.pallas_call(
        body, name=name, grid=(S // TR,), in_specs=in_specs, out_specs=out_specs, out_shape=out_shape,
        compiler_params=_params(("parallel",)))(*ins)
    return out if has_pre else (out[0], None)


def _rms_bwd_math(x, g, dy):
    r = lax.rsqrt(jnp.mean(x * x, axis=-1, keepdims=True) + EPS)
    n = x * r
    dn = dy * g
    dx = r * (dn - n * jnp.mean(dn * n, axis=-1, keepdims=True))
    return dx, jnp.sum(dy * n, axis=0, keepdims=True)


def _norm_bwd(name, dres, pre=None, post=None):
    has_pre, has_post = pre is not None, post is not None

    def body(*refs):
        refs = list(refs)
        dres_ref = refs.pop(0)
        if has_pre:
            xn_ref, gn_ref, dh_ref = refs[:3]
            refs = refs[3:]
        if has_post:
            y_ref, gp_ref = refs[:2]
            refs = refs[2:]
        dxn_ref = refs.pop(0)
        dy_ref = refs.pop(0) if has_post else None
        dgn_ref = refs.pop(0) if has_pre else None
        dgp_ref = refs.pop(0) if has_post else None
        first = pl.program_id(0) == 0
        dxn = dres_ref[...]
        if has_pre:
            dx, dg = _rms_bwd_math(xn_ref[...], gn_ref[...], dh_ref[...])
            dxn = dxn + dx

            @pl.when(first)
            def _():
                dgn_ref[...] = dg

            @pl.when(jnp.logical_not(first))
            def _():
                dgn_ref[...] += dg
        dxn_ref[...] = dxn
        if has_post:
            dy, dg = _rms_bwd_math(y_ref[...], gp_ref[...], dxn)
            dy_ref[...] = dy.astype(BF16)

            @pl.when(first)
            def _():
                dgp_ref[...] = dg

            @pl.when(jnp.logical_not(first))
            def _():
                dgp_ref[...] += dg

    ins, in_specs = [dres], [_row_spec()]
    if has_pre:
        ins += list(pre)
        in_specs += [_row_spec(), _vec_spec(), _row_spec()]
    if has_post:
        ins += list(post)
        in_specs += [_row_spec(), _vec_spec()]
    out_shape, out_specs = [jax.ShapeDtypeStruct((S, D), F32)], [_row_spec()]
    if has_post:
        out_shape.append(jax.ShapeDtypeStruct((S, D), BF16))
        out_specs.append(_row_spec())
    for _ in range(int(has_pre) + int(has_post)):
        out_shape.append(jax.ShapeDtypeStruct((1, D), F32))
        out_specs.append(_vec_spec())
    out = list(pl.pallas_call(
        body, name=name, grid=(S // TR,), in_specs=in_specs, out_specs=out_specs, out_shape=out_shape,
        compiler_params=_params(("arbitrary",)))(*ins))
    dxn = out.pop(0)
    dy = out.pop(0) if has_post else None
    dgn = out.pop(0) if has_pre else None
    dgp = out.pop(0) if has_post else None
    return dxn, dy, dgn, dgp


def _loss_kernel(y, target):
    def body(y_ref, t_ref, loss_ref, dy_ref):
        e = y_ref[...] - t_ref[...]
        dy_ref[...] = e * (1.0 / D)
        part = jnp.zeros((1, 128), F32) + 0.5 * jnp.sum(jnp.mean(e * e, axis=-1, keepdims=True))

        @pl.when(pl.program_id(0) == 0)
        def _():
            loss_ref[...] = part

        @pl.when(pl.program_id(0) > 0)
        def _():
            loss_ref[...] += part

    return pl.pallas_call(
        body, name="loss", grid=(S // TR,), in_specs=[_row_spec(), _row_spec()],
        out_specs=[_vec_spec(128), _row_spec()],
        out_shape=[jax.ShapeDtypeStruct((1, 128), F32), jax.ShapeDtypeStruct((S, D), F32)],
        compiler_params=_params(("arbitrary",)))(y, target)


def _t5_bucket_np(dist):
    max_exact = NUM_BUCKETS // 2
    nf = np.maximum(dist, 1).astype(np.float32)
    large = max_exact + (np.log(nf / max_exact) / np.float32(math.log(MAX_DISTANCE / max_exact))
                         * (NUM_BUCKETS - max_exact)).astype(np.int32)
    large = np.minimum(large, NUM_BUCKETS - 1)
    return np.where(dist < max_exact, dist, large).astype(np.int32)


def _bucket_maps():
    a = np.arange(BLK)[:, None]
    b = np.arange(2 * BLK)[None, :]
    dist = np.maximum(a + BLK - b, 0)
    maps = [_t5_bucket_np(dist * d) for _, d in A_GROUPS] + [_t5_bucket_np(dist)]
    return np.stack(maps).astype(np.int32)


def _classes(arr, d):
    return arr.reshape(S // d, d * arr.shape[1])


def _band_spec(arr, col0, prev):
    ncol = arr.shape[1] // 128
    if prev:
        return pl.BlockSpec((BLK, 128), lambda p, r, b: (jnp.maximum(b - 1, 0), r * ncol + col0 + p))
    return pl.BlockSpec((BLK, 128), lambda p, r, b: (b, r * ncol + col0 + p))


def _band_bias(tab_ref, bidx_ref, h):
    bi = bidx_ref[...]
    bias = jnp.zeros((BLK, 2 * BLK), F32)
    for kk in range(NUM_BUCKETS):
        bias = jnp.where(bi == kk, tab_ref[kk, h], bias)
    return bias


def _lane_lo():
    return lax.broadcasted_iota(jnp.int32, (BLK, 128), 1) < HD


def _per_head(x, lo):
    return (jnp.sum(jnp.where(lo, x, 0.0), axis=1, keepdims=True) * (1.0 / HD),
            jnp.sum(jnp.where(lo, 0.0, x), axis=1, keepdims=True) * (1.0 / HD))


def _band_mask(b, maxd):
    a = lax.broadcasted_iota(jnp.int32, (2 * BLK, 2 * BLK), 0) & (BLK - 1)
    c = lax.broadcasted_iota(jnp.int32, (2 * BLK, 2 * BLK), 1)
    dist = a + BLK - c
    return jnp.logical_and(jnp.logical_and(dist >= 0, dist <= maxd), jnp.logical_or(c >= BLK, b > 0))


def _stack_heads(x, lo, dtype=BF16):
    return jnp.concatenate([jnp.where(lo, x, 0.0), jnp.where(lo, 0.0, x)], axis=0).astype(dtype)


def _unstack_heads(x, lo):
    return jnp.where(lo, x[:BLK], x[BLK:])


def _stack_rows(prev_ref, cur_ref):
    return jnp.concatenate([prev_ref[...], cur_ref[...]], axis=0).astype(BF16)


def _band_fwd(name, d, n_pairs, maxd, head0, srcs, bidx_g, tab, sinks):
    nb = S // d // BLK
    (qa, qc), (ka, kc), (va, vc) = srcs
    out_spec = pl.BlockSpec((BLK, 128), lambda p, r, b: (b, r * n_pairs + p))
    smem = pl.BlockSpec(memory_space=pltpu.SMEM)
    full = pl.BlockSpec((BLK, 2 * BLK), lambda p, r, b: (0, 0))

    def body(tab_ref, sink_ref, q_ref, kp_ref, kc_ref, vp_ref, vc_ref, bidx_ref, o_ref, lse_ref, bias_ref):
        p, r, b = pl.program_id(0), pl.program_id(1), pl.program_id(2)

        @pl.when(jnp.logical_and(r == 0, b == 0))
        def _():
            for h in range(2):
                bias_ref[h * BLK:(h + 1) * BLK, :] = _band_bias(tab_ref, bidx_ref, head0 + 2 * p + h)

        lo = _lane_lo()
        qs = _stack_heads(q_ref[...] * SCALE, lo)
        ks, vs = _stack_rows(kp_ref, kc_ref), _stack_rows(vp_ref, vc_ref)
        s = jnp.where(_band_mask(b, maxd), _dot(qs, ks, NT) + bias_ref[...], NEG)
        m = jnp.max(s, axis=1, keepdims=True)
        pr = jnp.exp(s - m)
        l = jnp.sum(pr, axis=1, keepdims=True)
        num = _dot(pr.astype(BF16), vs, NN)
        lse = m + jnp.log(l)
        sink = jnp.where(lax.broadcasted_iota(jnp.int32, (2 * BLK, 1), 0) < BLK, sink_ref[2 * p], sink_ref[2 * p + 1])
        sig = 1.0 / (1.0 + jnp.exp(sink - lse))
        o_ref[...] = _unstack_heads(num * (sig / l), lo)
        lse_ref[...] = _unstack_heads(lse + jnp.zeros((2 * BLK, 128), F32), lo)

    shape = jax.ShapeDtypeStruct((S // d, d * n_pairs * 128), F32)
    o, lse = pl.pallas_call(
        body, name=name, grid=(n_pairs, d, nb),
        in_specs=[smem, smem, _band_spec(qa, qc, False), _band_spec(ka, kc, True), _band_spec(ka, kc, False),
                  _band_spec(va, vc, True), _band_spec(va, vc, False), full],
        out_specs=[out_spec, out_spec], out_shape=[shape, shape],
        scratch_shapes=[pltpu.VMEM((2 * BLK, 2 * BLK), F32)],
        compiler_params=_params(("parallel", "arbitrary", "arbitrary")))(
            tab, sinks, _classes(qa, d), _classes(ka, d), _classes(ka, d), _classes(va, d), _classes(va, d), bidx_g)
    return o.reshape(S, n_pairs * 128), lse.reshape(S, n_pairs * 128)


def _band_bwd(name, d, n_pairs, maxd, head0, srcs, bidx_g, tab, sinks, o, lse, do, stats_in):
    nb = S // d // BLK
    rows = S // d
    (qa, qc), (ka, kc), (va, vc) = srcs
    blk_spec = pl.BlockSpec((BLK, 128), lambda p, r, b: (b, r * n_pairs + p))
    cls_spec = pl.BlockSpec((rows, 128), lambda p, r, b: (0, r * n_pairs + p))
    smem = pl.BlockSpec(memory_space=pltpu.SMEM)
    full = pl.BlockSpec((BLK, 2 * BLK), lambda p, r, b: (0, 0))
    stat_spec = pl.BlockSpec((2, 8, 128), lambda p, r, b: (p, 0, 0))

    def body(tab_ref, sink_ref, q_ref, kp_ref, kc_ref, vp_ref, vc_ref, bidx_ref, o_ref, lse_ref, do_ref, sin_ref,
             dq_ref, dk_ref, dv_ref, stat_ref, bias_ref, dsacc_ref, sk_ref):
        p, r, b = pl.program_id(0), pl.program_id(1), pl.program_id(2)

        @pl.when(jnp.logical_and(r == 0, b == 0))
        def _():
            for h in range(2):
                bias_ref[h * BLK:(h + 1) * BLK, :] = _band_bias(tab_ref, bidx_ref, head0 + 2 * p + h)
            dsacc_ref[...] = jnp.zeros_like(dsacc_ref)
            sk_ref[...] = jnp.zeros_like(sk_ref)

        @pl.when(b == 0)
        def _():
            dk_ref[...] = jnp.zeros_like(dk_ref)
            dv_ref[...] = jnp.zeros_like(dv_ref)

        lo = _lane_lo()
        qs = _stack_heads(q_ref[...] * SCALE, lo)
        ks, vs = _stack_rows(kp_ref, kc_ref), _stack_rows(vp_ref, vc_ref)
        do = do_ref[...]
        dos = _stack_heads(do, lo, F32)
        lse = jnp.concatenate(_per_head(lse_ref[...], lo), axis=0)
        prod = do * o_ref[...]
        delta = jnp.concatenate([jnp.sum(jnp.where(lo, prod, 0.0), axis=1, keepdims=True),
                                 jnp.sum(jnp.where(lo, 0.0, prod), axis=1, keepdims=True)], axis=0)
        head1 = lax.broadcasted_iota(jnp.int32, (2 * BLK, 1), 0) >= BLK
        sig = 1.0 / (1.0 + jnp.exp(jnp.where(head1, sink_ref[2 * p + 1], sink_ref[2 * p]) - lse))
        s = _dot(qs, ks, NT) + bias_ref[...]
        pr = jnp.where(_band_mask(b, maxd), jnp.exp(s - lse), 0.0)
        ds = pr * (sig * (_dot(dos.astype(BF16), vs, NT) - delta))
        dsb = ds.astype(BF16)
        dq_ref[...] = SCALE * _unstack_heads(_dot(dsb, ks, NN), lo)
        dk = _dot(dsb, qs, TN)
        dv = _dot(pr.astype(BF16), (sig * dos).astype(BF16), TN)
        cur = pl.ds(pl.multiple_of(b * BLK, BLK), BLK)
        prev = pl.ds(pl.multiple_of(jnp.maximum(b - 1, 0) * BLK, BLK), BLK)
        dk_ref[prev, :] += dk[:BLK]
        dk_ref[cur, :] += dk[BLK:]
        dv_ref[prev, :] += dv[:BLK]
        dv_ref[cur, :] += dv[BLK:]
        dsacc_ref[...] += ds
        sink_grad = -delta * (1.0 - sig)
        for h in range(2):
            sk_ref[h] += jnp.zeros((8, 128), F32) + jnp.sum(sink_grad[h * BLK:(h + 1) * BLK])

        @pl.when(jnp.logical_and(r == d - 1, b == nb - 1))
        def _():
            bi = bidx_ref[...]
            lane = lax.broadcasted_iota(jnp.int32, (8, 128), 1)
            sub = lax.broadcasted_iota(jnp.int32, (8, 128), 0)
            for h in range(2):
                acc = dsacc_ref[h * BLK:(h + 1) * BLK, :]
                row = jnp.where(jnp.logical_and(sub == 1, lane == 0), sk_ref[h], 0.0)
                for kk in range(NUM_BUCKETS):
                    tot = jnp.sum(jnp.where(bi == kk, acc, 0.0))
                    row = jnp.where(jnp.logical_and(sub == 0, lane == kk), tot, row)
                stat_ref[h] = row + jnp.where(sub == 0, sin_ref[h], 0.0)

    shape = jax.ShapeDtypeStruct((rows, d * n_pairs * 128), F32)
    dq, dk, dv, stats = pl.pallas_call(
        body, name=name, grid=(n_pairs, d, nb),
        in_specs=[smem, smem, _band_spec(qa, qc, False), _band_spec(ka, kc, True), _band_spec(ka, kc, False),
                  _band_spec(va, vc, True), _band_spec(va, vc, False), full, blk_spec, blk_spec, blk_spec, stat_spec],
        out_specs=[blk_spec, cls_spec, cls_spec, stat_spec],
        out_shape=[shape, shape, shape, jax.ShapeDtypeStruct((2 * n_pairs, 8, 128), F32)],
        scratch_shapes=[pltpu.VMEM((2 * BLK, 2 * BLK), F32), pltpu.VMEM((2 * BLK, 2 * BLK), F32),
                        pltpu.VMEM((2, 8, 128), F32)],
        compiler_params=_params(("arbitrary", "arbitrary", "arbitrary")))(
            tab, sinks, _classes(qa, d), _classes(ka, d), _classes(ka, d), _classes(va, d), _classes(va, d), bidx_g,
            _classes(o, d), _classes(lse, d), _classes(do, d), stats_in)
    width = n_pairs * 128
    return dq.reshape(S, width), dk.reshape(S, width), dv.reshape(S, width), stats


def _comb_fwd(o_g, lse_g):
    def body(o0, o1, o2, l0, l1, l2, out_ref, outb_ref, lse_ref):
        a0, a1, a2 = l0[...], l1[...], l2[...]
        m = jnp.maximum(jnp.maximum(a0, a1), a2)
        e0, e1, e2 = jnp.exp(a0 - m), jnp.exp(a1 - m), jnp.exp(a2 - m)
        tot = e0 + e1 + e2
        out = (e0 * o0[...] + e1 * o1[...] + e2 * o2[...]) / tot
        out_ref[...] = out
        outb_ref[...] = out.astype(BF16)
        lse_ref[...] = m + jnp.log(tot)

    spec = _row_spec(4 * HD)
    f32 = jax.ShapeDtypeStruct((S, 4 * HD), F32)
    return pl.pallas_call(
        body, name="comb_fwd", grid=(S // TR,), in_specs=[spec] * 6, out_specs=[spec] * 3,
        out_shape=[f32, jax.ShapeDtypeStruct((S, 4 * HD), BF16), f32],
        compiler_params=_params(("parallel",)))(*o_g, *lse_g)


def _split2(x):
    hi = x.astype(BF16)
    return hi, (x - hi.astype(F32)).astype(BF16)


KB = 2 * BLK


def _tri_sum(x, tri):
    hi, lo = _split2(x)
    both = _dot(jnp.concatenate([hi, lo], axis=0), tri, NN)
    return both[:x.shape[0]] + both[x.shape[0]:]


def _tri(strict_upper):
    r = lax.broadcasted_iota(jnp.int32, (KB, KB), 0)
    c = lax.broadcasted_iota(jnp.int32, (KB, KB), 1)
    return jnp.where(r > c if strict_upper else r < c, 1.0, 0.0).astype(BF16)


def _sb_terms(qs, kj, before):
    z = _dot(qs, kj, NT)
    lsp = jnp.minimum(z, 0.0) - jnp.log(1.0 + jnp.exp(-jnp.abs(z)))
    return lsp, jnp.where(before, lsp - z, 0.0)


def _sb_before(i, m):
    t = (lax.broadcasted_iota(jnp.int32, (2 * BLK, KB), 0) & (BLK - 1)) + i * BLK
    s = lax.broadcasted_iota(jnp.int32, (2 * BLK, KB), 1) + m * KB
    return s < t


C_COL = 3072 // 128


def _sb_fwd(proj):
    blk = lambda off: pl.BlockSpec((BLK, 128), lambda p, i: (i, off + p))
    col = lambda off: pl.BlockSpec((S, 128), lambda p, i: (0, off + p))
    out = pl.BlockSpec((BLK, 128), lambda p, i: (i, p))

    def body(q_ref, k_ref, v_ref, o_ref, ob_ref, tot_ref):
        i = pl.program_id(1)
        lo = _lane_lo()
        qs = _stack_heads(q_ref[...] * SCALE, lo)
        suffix = _tri(True)

        def step(n, carry):
            acc, rest = carry
            m = i // 2 - n
            rows = pl.ds(pl.multiple_of(m * KB, KB), KB)
            kj, vj = k_ref[rows, :].astype(BF16), v_ref[rows, :].astype(BF16)
            before = _sb_before(i, m)
            lsp, lk = _sb_terms(qs, kj, before)
            w = jnp.where(before, jnp.exp(lsp + _tri_sum(lk, suffix) + rest), 0.0)
            return acc + _dot(w.astype(BF16), vj, NN), rest + jnp.sum(lk, axis=1, keepdims=True)

        acc, rest = lax.fori_loop(0, i // 2 + 1, step, (jnp.zeros((2 * BLK, 128), F32), jnp.zeros((2 * BLK, 1), F32)))
        o = _unstack_heads(acc, lo)
        o_ref[...] = o
        ob_ref[...] = o.astype(BF16)
        tot_ref[...] = _unstack_heads(rest + jnp.zeros((2 * BLK, 128), F32), lo)

    f32 = jax.ShapeDtypeStruct((S, 4 * HD), F32)
    return pl.pallas_call(
        body, name="sb_fwd", grid=(2, NQB), in_specs=[blk(C_COL), col(C_COL + 2), col(C_COL + 4)],
        out_specs=[out, out, out], out_shape=[f32, jax.ShapeDtypeStruct((S, 4 * HD), BF16), f32],
        compiler_params=_params(("parallel", "arbitrary")))(proj, proj, proj)


def _sb_bwd(proj, tot, do):
    blk = lambda off: pl.BlockSpec((BLK, 128), lambda p, i: (i, off + p))
    col = lambda off: pl.BlockSpec((S, 128), lambda p, i: (0, off + p))

    def body(q_ref, k_ref, v_ref, tot_ref, do_ref, dq_ref, dk_ref, dv_ref):
        i = pl.program_id(1)

        @pl.when(i == 0)
        def _():
            dk_ref[...] = jnp.zeros_like(dk_ref)
            dv_ref[...] = jnp.zeros_like(dv_ref)

        lo = _lane_lo()
        qs = _stack_heads(q_ref[...] * SCALE, lo)
        dos = _stack_heads(do_ref[...], lo)
        tots = jnp.concatenate(_per_head(tot_ref[...], lo), axis=0)
        prefix = _tri(False)

        def step(m, carry):
            dq, keep_left, g_left = carry
            rows = pl.ds(pl.multiple_of(m * KB, KB), KB)
            kj, vj = k_ref[rows, :].astype(BF16), v_ref[rows, :].astype(BF16)
            before = _sb_before(i, m)
            lsp, lk = _sb_terms(qs, kj, before)
            log_rest = tots - keep_left - lk - _tri_sum(lk, prefix)
            w = jnp.where(before, jnp.exp(lsp + log_rest), 0.0)
            g = w * _dot(dos, vj, NT)
            g_before = g_left + _dot(g.astype(BF16), prefix, NN)
            beta = jnp.exp(lsp)
            dz = jnp.where(before, g * (1.0 - beta) - g_before * beta, 0.0).astype(BF16)
            dk_ref[rows, :] += _dot(dz, qs, TN)
            dv_ref[rows, :] += _dot(w.astype(BF16), dos, TN)
            return (dq + _dot(dz, kj, NN), keep_left + jnp.sum(lk, axis=1, keepdims=True),
                    g_left + jnp.sum(g, axis=1, keepdims=True))

        zero = (jnp.zeros((2 * BLK, 128), F32), jnp.zeros((2 * BLK, 1), F32), jnp.zeros((2 * BLK, 1), F32))
        dq, _, _ = lax.fori_loop(0, i // 2 + 1, step, zero)
        dq_ref[...] = SCALE * _unstack_heads(dq, lo)

    out_blk = pl.BlockSpec((BLK, 128), lambda p, i: (i, p))
    out_col = pl.BlockSpec((S, 128), lambda p, i: (0, p))
    f32 = jax.ShapeDtypeStruct((S, 4 * HD), F32)
    return pl.pallas_call(
        body, name="sb_bwd", grid=(2, NQB),
        in_specs=[blk(C_COL), col(C_COL + 2), col(C_COL + 4), out_blk, out_blk],
        out_specs=[out_blk, out_col, out_col], out_shape=[f32, f32, f32],
        compiler_params=_params(("arbitrary", "arbitrary")))(proj, proj, proj, tot, do)


TG = 256
GATE_BLK0 = OFF_GATE // TG


def _gate_specs():
    grid = (D // TG, S // TG)
    p_specs = [pl.BlockSpec((TG, TG), functools.partial(lambda c, r, br: (r, GATE_BLK0 + br * (D // TG) + c), br=br))
               for br in range(3)]
    b_spec = pl.BlockSpec((3, TG), lambda c, r: (0, c))
    t_spec = pl.BlockSpec((TG, TG), lambda c, r: (r, c))
    return grid, p_specs, b_spec, t_spec


def _sigmoid(x):
    return 1.0 / (1.0 + jnp.exp(-x))


def _three_rows(rows):
    sub = lax.broadcasted_iota(jnp.int32, (3, rows[0].shape[1]), 0)
    return jnp.where(sub == 0, rows[0], jnp.where(sub == 1, rows[1], rows[2]))


def _gate_fwd(proj, b_gate, br):
    grid, p_specs, b_spec, t_spec = _gate_specs()

    def body(p0, p1, p2, b_ref, r0, r1, r2, out_ref):
        acc = jnp.zeros((TG, TG), F32)
        for n, (p, r) in enumerate(((p0, r0), (p1, r1), (p2, r2))):
            acc += _sigmoid(p[...] + b_ref[n:n + 1, :]) * r[...]
        out_ref[...] = acc.astype(BF16)

    return pl.pallas_call(
        body, name="gate_fwd", grid=grid, in_specs=p_specs + [b_spec] + [t_spec] * 3, out_specs=t_spec,
        out_shape=jax.ShapeDtypeStruct((S, D), BF16),
        compiler_params=_params(("parallel", "parallel")))(proj, proj, proj, b_gate, *br)


def _gate_bwd(proj, b_gate, br, dmerged):
    grid, p_specs, b_spec, t_spec = _gate_specs()

    def body(p0, p1, p2, b_ref, r0, r1, r2, dm_ref, e0, e1, e2, g0, g1, g2, db_ref):
        dm = dm_ref[...]
        rows = []
        for n, (p, r, e_ref, dg_ref) in enumerate(((p0, r0, e0, g0), (p1, r1, e1, g1), (p2, r2, e2, g2))):
            g = _sigmoid(p[...] + b_ref[n:n + 1, :])
            e_ref[...] = (dm * g).astype(BF16)
            dpre = dm * r[...] * g * (1.0 - g)
            dg_ref[...] = dpre.astype(BF16)
            rows.append(jnp.sum(dpre, axis=0, keepdims=True))
        db = _three_rows(rows)

        @pl.when(pl.program_id(1) == 0)
        def _():
            db_ref[...] = db

        @pl.when(pl.program_id(1) > 0)
        def _():
            db_ref[...] += db

    bf = jax.ShapeDtypeStruct((S, D), BF16)
    out = pl.pallas_call(
        body, name="gate_bwd", grid=grid, in_specs=p_specs + [b_spec] + [t_spec] * 4,
        out_specs=[t_spec] * 6 + [b_spec], out_shape=[bf] * 6 + [jax.ShapeDtypeStruct((3, D), F32)],
        compiler_params=_params(("parallel", "arbitrary")))(proj, proj, proj, b_gate, *br, dmerged)
    return out[:3], out[3:6], out[6]


TC = 256
N_FF_BLK = D_FF // TC
GELU_C = math.sqrt(2.0 / math.pi)


def _shift_down(x, n):
    rows = lax.broadcasted_iota(jnp.int32, x.shape, 0)
    return jnp.where(rows >= n, pltpu.roll(x, n, axis=0), 0.0)


def _shift_up(x, n):
    rows = lax.broadcasted_iota(jnp.int32, x.shape, 0)
    return jnp.where(rows < x.shape[0] - n, pltpu.roll(x, x.shape[0] - n, axis=0), 0.0)


def _conv(u, w, b):
    return w[2:3, :] * u + w[1:2, :] * _shift_down(u, 1) + w[0:1, :] * _shift_down(u, 2) + b


def _gelu_parts(x):
    inner = GELU_C * (x + 0.044715 * x * x * x)
    t = jnp.tanh(inner)
    gelu = 0.5 * x * (1.0 + t)
    dgelu = 0.5 * (1.0 + t) + 0.5 * x * (1.0 - t * t) * GELU_C * (1.0 + 3 * 0.044715 * x * x)
    return gelu, dgelu


def _conv_specs():
    ug = pl.BlockSpec((S, TC), lambda c: (0, c))
    uv = pl.BlockSpec((S, TC), lambda c: (0, N_FF_BLK + c))
    wg = pl.BlockSpec((3, TC), lambda c: (0, c))
    wv = pl.BlockSpec((3, TC), lambda c: (0, N_FF_BLK + c))
    bg = pl.BlockSpec((1, TC), lambda c: (0, c))
    bv = pl.BlockSpec((1, TC), lambda c: (0, N_FF_BLK + c))
    return ug, uv, wg, wv, bg, bv


def _conv_fwd(u, conv_w, conv_b):
    ug, uv, wg, wv, bg, bv = _conv_specs()

    def body(ug_ref, uv_ref, wg_ref, wv_ref, bg_ref, bv_ref, a_ref):
        gc = _conv(ug_ref[...], wg_ref[...], bg_ref[...])
        vc = _conv(uv_ref[...], wv_ref[...], bv_ref[...])
        a_ref[...] = (_gelu_parts(gc)[0] * vc).astype(BF16)

    return pl.pallas_call(
        body, name="conv_fwd", grid=(N_FF_BLK,), in_specs=[ug, uv, wg, wv, bg, bv], out_specs=ug,
        out_shape=jax.ShapeDtypeStruct((S, D_FF), BF16),
        compiler_params=_params(("parallel",)))(u, u, conv_w, conv_w, conv_b, conv_b)


def _conv_bwd(u, conv_w, conv_b, da):
    ug, uv, wg, wv, bg, bv = _conv_specs()

    def back(duc, u, w):
        du = w[2:3, :] * duc + w[1:2, :] * _shift_up(duc, 1) + w[0:1, :] * _shift_up(duc, 2)
        dw = _three_rows([jnp.sum(duc * _shift_down(u, 2), axis=0, keepdims=True),
                          jnp.sum(duc * _shift_down(u, 1), axis=0, keepdims=True),
                          jnp.sum(duc * u, axis=0, keepdims=True)])
        return du, dw, jnp.sum(duc, axis=0, keepdims=True)

    def body(ug_ref, uv_ref, wg_ref, wv_ref, bg_ref, bv_ref, da_ref, dug_ref, duv_ref, dwg_ref, dwv_ref, dbg_ref, dbv_ref):
        u_g, u_v = ug_ref[...], uv_ref[...]
        gc = _conv(u_g, wg_ref[...], bg_ref[...])
        vc = _conv(u_v, wv_ref[...], bv_ref[...])
        gelu, dgelu = _gelu_parts(gc)
        da = da_ref[...]
        du, dw, db = back(da * vc * dgelu, u_g, wg_ref[...])
        dug_ref[...] = du.astype(BF16)
        dwg_ref[...] = dw
        dbg_ref[...] = db
        du, dw, db = back(da * gelu, u_v, wv_ref[...])
        duv_ref[...] = du.astype(BF16)
        dwv_ref[...] = dw
        dbv_ref[...] = db

    return pl.pallas_call(
        body, name="conv_bwd", grid=(N_FF_BLK,), in_specs=[ug, uv, wg, wv, bg, bv, ug],
        out_specs=[ug, ug, wg, wg, bg, bg],
        out_shape=[jax.ShapeDtypeStruct((S, D_FF), BF16), jax.ShapeDtypeStruct((S, D_FF), BF16),
                   jax.ShapeDtypeStruct((3, D_FF), F32), jax.ShapeDtypeStruct((3, D_FF), F32),
                   jax.ShapeDtypeStruct((1, D_FF), F32), jax.ShapeDtypeStruct((1, D_FF), F32)],
        compiler_params=_params(("parallel",)))(u, u, conv_w, conv_w, conv_b, conv_b, da)


def _adamw(name, w, g, m, v):
    shape = w.shape
    cols = shape[-1]
    flat = [t.reshape(-1, cols) for t in (w, g, m, v)]
    r = flat[0].shape[0]
    tr = min(128, r)

    def body(w_ref, g_ref, m_ref, v_ref, d_ref, mo_ref, vo_ref):
        g = g_ref[...]
        m = ADAM_B1 * m_ref[...] + (1.0 - ADAM_B1) * g
        v = ADAM_B2 * v_ref[...] + (1.0 - ADAM_B2) * (g * g)
        m_hat = m / (1.0 - ADAM_B1 ** ADAM_STEP)
        v_hat = v / (1.0 - ADAM_B2 ** ADAM_STEP)
        d_ref[...] = -ADAM_LR * (m_hat / (jnp.sqrt(v_hat) + ADAM_EPS) + ADAM_WD * w_ref[...])
        mo_ref[...] = m
        vo_ref[...] = v

    spec = pl.BlockSpec((tr, cols), lambda i: (i, 0))
    outs = pl.pallas_call(
        body, name=name, grid=(pl.cdiv(r, tr),), in_specs=[spec] * 4, out_specs=[spec] * 3,
        out_shape=[jax.ShapeDtypeStruct((r, cols), F32)] * 3, compiler_params=_params(("parallel",)))(*flat)
    return [t.reshape(shape) for t in outs]


def _place():
    x, y, c = lax.axis_index("x"), lax.axis_index("y"), lax.axis_index("c")
    chips = [(1 - x, y), (x, 1 - y), (1 - x, 1 - y)]
    return x, y, c, chips


def _scalars(*vals):
    return jnp.stack([jnp.asarray(v, jnp.int32) for v in vals])


def _cast_into_slot(name, w, chip):
    _, k, n4 = w.shape
    tr = min(256, k)

    def body(chip_ref, w_ref, o_ref):
        o_ref[...] = w_ref[...].astype(BF16)

    return pl.pallas_call(
        body, name=name,
        grid_spec=pltpu.PrefetchScalarGridSpec(
            num_scalar_prefetch=1, grid=(DEPTH, k // tr),
            in_specs=[pl.BlockSpec((None, tr, n4), lambda l, i, s: (l, i, 0))],
            out_specs=pl.BlockSpec((None, None, tr, n4), lambda l, i, s: (s[0], l, i, 0))),
        out_shape=jax.ShapeDtypeStruct((N_CHIPS, DEPTH, k, n4), BF16),
        compiler_params=_params(("parallel", "parallel")))(_scalars(chip), w)


def _gather_weights(bufs, smalls):
    nb, n = len(bufs), len(bufs) + len(smalls)

    def body(*refs):
        ins, outs = refs[:n], refs[n:2 * n]
        local_sem, send_sem, recv_sem = refs[2 * n:]
        x, y, c, chips = _place()
        me = 2 * x + y
        sibling = (x, y, 1 - c)
        local = [pltpu.make_async_copy(ins[a], outs[a].at[me], local_sem.at[a - nb]) for a in range(nb, n)]
        for cp in local:
            cp.start()

        def over_ici(a, k, from_chip):
            return pltpu.make_async_remote_copy(
                src_ref=ins[a].at[me, c] if a < nb else ins[a].at[c], dst_ref=outs[a].at[from_chip, c],
                send_sem=send_sem.at[a, k], recv_sem=recv_sem.at[a, k],
                device_id=(*chips[k], c), device_id_type=MESH)

        def over_d2d(a, k, layer):
            rows = outs[a].at[2 * chips[k][0] + chips[k][1], layer]
            return pltpu.make_async_remote_copy(
                src_ref=rows, dst_ref=rows, send_sem=send_sem.at[a, 3 + k], recv_sem=recv_sem.at[a, 3 + k],
                device_id=sibling, device_id_type=MESH)

        sends = [over_ici(a, k, me) for a in range(n) for k in range(3)]
        for cp in sends:
            cp.start()
        passed = []
        for a in range(n):
            for k in range(3):
                over_ici(a, k, 2 * chips[k][0] + chips[k][1]).wait_recv()
                cp = over_d2d(a, k, c)
                cp.start()
                passed.append(cp)
        for a in range(n):
            for k in range(3):
                over_d2d(a, k, 1 - c).wait_recv()
        for cp in sends + passed:
            cp.wait_send()
        for cp in local:
            cp.wait()

    out_shape = [jax.ShapeDtypeStruct(b.shape, b.dtype) for b in bufs]
    out_shape += [jax.ShapeDtypeStruct((N_CHIPS,) + s.shape, s.dtype) for s in smalls]
    return pl.pallas_call(
        body, name="gather_weights", in_specs=[ANY] * n, out_specs=[ANY] * n, out_shape=out_shape,
        input_output_aliases={a: a for a in range(nb)},
        scratch_shapes=[pltpu.SemaphoreType.DMA((n - nb,)), pltpu.SemaphoreType.DMA((n, 6)),
                        pltpu.SemaphoreType.DMA((n, 6))],
    )(*bufs, *smalls)


def _swap_layers(grads):
    n = len(grads)

    def body(*refs):
        ins, got = refs[:n], refs[n:2 * n]
        send_sem, recv_sem = refs[2 * n:]
        x, y, c, _ = _place()
        sends = [pltpu.make_async_remote_copy(
            src_ref=ins[a].at[1 - c], dst_ref=got[a], send_sem=send_sem.at[a], recv_sem=recv_sem.at[a],
            device_id=(x, y, 1 - c), device_id_type=MESH) for a in range(n)]
        for cp in sends:
            cp.start()
        for cp in sends:
            cp.wait()

    return pl.pallas_call(
        body, name="swap_layers", in_specs=[ANY] * n, out_specs=[ANY] * n,
        out_shape=[jax.ShapeDtypeStruct(g.shape[1:], g.dtype) for g in grads],
        scratch_shapes=[pltpu.SemaphoreType.DMA((n,)), pltpu.SemaphoreType.DMA((n,))],
    )(*grads)


def _add_sibling(name, g, got, c):
    _, _, k, n4 = g.shape
    rows = N_CHIPS * k
    tr = min(512, rows)

    def body(c_ref, g_ref, got_ref, o_ref):
        o_ref[...] = (g_ref[...].astype(F32) + got_ref[...].astype(F32)).astype(BF16)

    out = pl.pallas_call(
        body, name=name,
        grid_spec=pltpu.PrefetchScalarGridSpec(
            num_scalar_prefetch=1, grid=(rows // tr,),
            in_specs=[pl.BlockSpec((None, tr, n4), lambda i, s: (s[0], i, 0)),
                      pl.BlockSpec((tr, n4), lambda i, s: (i, 0))],
            out_specs=pl.BlockSpec((tr, n4), lambda i, s: (i, 0))),
        out_shape=jax.ShapeDtypeStruct((rows, n4), BF16),
        compiler_params=_params(("parallel",)))(_scalars(c), g.reshape(DEPTH, rows, n4), got.reshape(rows, n4))
    return out.reshape(N_CHIPS, k, n4)


def _scatter_shards(parts):
    n = len(parts)

    def body(*refs):
        ins, outs = refs[:n], refs[n:2 * n]
        send_sem, recv_sem = refs[2 * n:]
        x, y, c, chips = _place()
        me = 2 * x + y
        sends = [pltpu.make_async_remote_copy(
            src_ref=ins[a].at[2 * chips[k][0] + chips[k][1]], dst_ref=outs[a].at[me],
            send_sem=send_sem.at[a, k], recv_sem=recv_sem.at[a, k],
            device_id=(*chips[k], c), device_id_type=MESH) for a in range(n) for k in range(3)]
        for cp in sends:
            cp.start()
        for a in range(n):
            for k in range(3):
                src_chip = 2 * chips[k][0] + chips[k][1]
                pltpu.make_async_remote_copy(
                    src_ref=ins[a].at[me], dst_ref=outs[a].at[src_chip],
                    send_sem=send_sem.at[a, k], recv_sem=recv_sem.at[a, k],
                    device_id=(x, y, c), device_id_type=MESH).wait_recv()
        for cp in sends:
            cp.wait_send()

    return pl.pallas_call(
        body, name="scatter_shards", in_specs=[ANY] * n, out_specs=[ANY] * n,
        out_shape=[jax.ShapeDtypeStruct(p.shape, p.dtype) for p in parts],
        scratch_shapes=[pltpu.SemaphoreType.DMA((n, 3)), pltpu.SemaphoreType.DMA((n, 3))],
    )(*parts)


def _add_chips(name, part, arrived, chip, c):
    _, k, n4 = part.shape
    tr = min(512, k)

    def body(s_ref, own_ref, a1_ref, a2_ref, a3_ref, o_ref):
        o_ref[...] = ((own_ref[...].astype(F32) + a1_ref[...].astype(F32)) + a2_ref[...].astype(F32)) + a3_ref[...].astype(F32)

    def pick(j):
        return pl.BlockSpec((None, tr, n4), lambda i, s: (s[j], i, 0))

    return pl.pallas_call(
        body, name=name,
        grid_spec=pltpu.PrefetchScalarGridSpec(
            num_scalar_prefetch=1, grid=(k // tr,), in_specs=[pick(0), pick(2), pick(3), pick(4)], out_specs=pick(1)),
        out_shape=jax.ShapeDtypeStruct((DEPTH, k, n4), F32),
        compiler_params=_params(("parallel",)))(
            _scalars(chip, c, (chip + 1) % N_CHIPS, (chip + 2) % N_CHIPS, (chip + 3) % N_CHIPS),
            part, arrived, arrived, arrived)


def _join_layers(bufs):
    n = len(bufs)

    def body(*refs):
        ins, outs = refs[:n], refs[n:2 * n]
        send_sem, recv_sem = refs[2 * n:]
        x, y, c, _ = _place()
        sends = [pltpu.make_async_remote_copy(
            src_ref=ins[a].at[c], dst_ref=outs[a].at[c], send_sem=send_sem.at[a], recv_sem=recv_sem.at[a],
            device_id=(x, y, 1 - c), device_id_type=MESH) for a in range(n)]
        for cp in sends:
            cp.start()
        for a in range(n):
            sends[a].wait_send()
            pltpu.make_async_remote_copy(
                src_ref=ins[a].at[c], dst_ref=outs[a].at[1 - c], send_sem=send_sem.at[a], recv_sem=recv_sem.at[a],
                device_id=(x, y, 1 - c), device_id_type=MESH).wait_recv()

    return pl.pallas_call(
        body, name="join_layers", in_specs=[ANY] * n, out_specs=[ANY] * n,
        out_shape=[jax.ShapeDtypeStruct(b.shape, b.dtype) for b in bufs],
        input_output_aliases={a: a for a in range(n)},
        scratch_shapes=[pltpu.SemaphoreType.DMA((n,)), pltpu.SemaphoreType.DMA((n,))],
    )(*bufs)


def _all_reduce_small(block):
    r = block.shape[0]

    def body(x_ref, out_ref, slots, send_sem, recv_sem):
        x, y, c, _ = _place()
        me = 4 * x + 2 * y + c
        slots[me] = x_ref[...]
        sends = []
        for mask in range(1, N_DEV):
            fx, fy, fc = (mask >> 2) & 1, (mask >> 1) & 1, mask & 1
            peer = (x ^ fx, y ^ fy, c ^ fc)
            cp = pltpu.make_async_remote_copy(
                src_ref=x_ref, dst_ref=slots.at[me], send_sem=send_sem.at[mask - 1], recv_sem=recv_sem.at[mask - 1],
                device_id=peer, device_id_type=MESH)
            cp.start()
            sends.append(cp)
        for mask in range(1, N_DEV):
            src = me ^ mask
            pltpu.make_async_remote_copy(
                src_ref=x_ref, dst_ref=slots.at[src], send_sem=send_sem.at[mask - 1], recv_sem=recv_sem.at[mask - 1],
                device_id=(x, y, c), device_id_type=MESH).wait_recv()
        for cp in sends:
            cp.wait_send()
        total = slots[0]
        for d in range(1, N_DEV):
            total = total + slots[d]
        out_ref[...] = total

    vmem = pl.BlockSpec(memory_space=pltpu.VMEM)
    return pl.pallas_call(
        body, name="all_reduce_small", in_specs=[vmem], out_specs=vmem,
        out_shape=jax.ShapeDtypeStruct((r, 128), F32),
        scratch_shapes=[pltpu.VMEM((N_DEV, r, 128), F32), pltpu.SemaphoreType.DMA((N_DEV - 1,)),
                        pltpu.SemaphoreType.DMA((N_DEV - 1,))],
        compiler_params=pltpu.CompilerParams(vmem_limit_bytes=VMEM_LIMIT))(block)


B_Q_COL = 2304 // 128
B_K0, B_V0, B_END = 2816, 2944, 3072


def _full_cols(w_g):
    return w_g.transpose(1, 0, 2).reshape(w_g.shape[1], -1)


def _group_src(proj, g):
    if A_GROUPS[g][1] == 1:
        return ((proj, 2 * g), (proj, 6 + 2 * g), (proj, 12 + 2 * g))
    packed = jnp.concatenate([proj[:, t * 768 + g * 256:t * 768 + (g + 1) * 256] for t in range(3)], axis=1)
    return ((packed, 0), (packed, 2), (packed, 4))


def _kv_expand(kv):
    return jnp.broadcast_to(kv.reshape(S, 2, 1, HD), (S, 2, 4, HD)).reshape(S, 8 * HD)


def _kv_reduce(dkv):
    return dkv.reshape(S, 2, 4, HD).sum(axis=2).reshape(S, 2 * HD)


def _mixer_fwd(h1, w, rel_bias, sinks_l, bidx):
    proj = _mm_nn("proj_in", h1, w["w_in"], F32, tn=1152)
    no_sinks = jnp.full((4,), NEG, F32)
    srcs = [_group_src(proj, g) for g in range(3)]
    o_g, lse_g = [], []
    for g, (_, d) in enumerate(A_GROUPS):
        o, lse = _band_fwd("band_fwd_g%d" % g, d, 2, BLK, 4 * g, srcs[g], bidx[g], rel_bias, no_sinks)
        o_g.append(o)
        lse_g.append(lse)
    o_a32, o_a, lse_a = _comb_fwd(o_g, lse_g)
    src_b = ((proj, B_Q_COL), (_kv_expand(proj[:, B_K0:B_V0]), 0), (_kv_expand(proj[:, B_V0:B_END]), 0))
    o_b32, lse_b = _band_fwd("band_fwd_b", 1, 4, BLK - 1, N_A, src_b, bidx[3], rel_bias, sinks_l)
    o_b = o_b32.astype(BF16)
    o_c32, o_c, tot_c = _sb_fwd(proj)
    br = [_mm_nn("branch_a", o_a, w["w_br_a"], F32), _mm_nn("branch_b", o_b, w["w_br_b"], F32),
          _mm_nn("branch_c", o_c, w["w_br_c"], F32)]
    merged = _gate_fwd(proj, w["b_gate"], br)
    mo = _mm_nn("out_proj", merged, w["w_out"], F32)
    saved = dict(proj=proj, srcs=srcs, src_b=src_b, o_a32=o_a32, lse_a=lse_a, o_b32=o_b32, lse_b=lse_b, tot_c=tot_c,
                 o_a=o_a, o_b=o_b, o_c=o_c, br=br, merged=merged)
    return mo, saved


def _mixer_bwd(d_mo, h1, w, sv, rel_bias, sinks_l, bidx, stats_in, into):
    grads, layer = {}, w["layer"]
    dmerged = _mm_nt("out_proj_dx", d_mo, w["w_out"], F32)
    grads["w_out"] = _mm_tn_stacked("out_proj_dw", sv["merged"], d_mo, layer, into.get("w_out"), True)
    e, dgate, db_gate = _gate_bwd(sv["proj"], w["b_gate"], sv["br"], dmerged)
    grads["b_gate"] = db_gate
    d_o = {}
    for n, name in enumerate("abc"):
        d_o[name] = _mm_nt("branch_%s_dx" % name, e[n], w["w_br_" + name], F32)
        grads["w_br_" + name] = _mm_tn_stacked("branch_%s_dw" % name, sv["o_" + name], e[n], layer,
                                               into.get("w_br_" + name), False)
    no_sinks = jnp.full((4,), NEG, F32)
    dqs, dks, dvs, stats = [], [], [], []
    for g, (_, d) in enumerate(A_GROUPS):
        dq, dk, dv, st = _band_bwd("band_bwd_g%d" % g, d, 2, BLK, 4 * g, sv["srcs"][g], bidx[g], rel_bias, no_sinks,
                                   sv["o_a32"], sv["lse_a"], d_o["a"], stats_in[4 * g:4 * g + 4])
        dqs.append(dq)
        dks.append(dk)
        dvs.append(dv)
        stats.append(st)
    dq_b, dk_x, dv_x, st = _band_bwd("band_bwd_b", 1, 4, BLK - 1, N_A, sv["src_b"], bidx[3], rel_bias, sinks_l,
                                     sv["o_b32"], sv["lse_b"], d_o["b"], stats_in[N_A:])
    stats = jnp.concatenate(stats + [st], axis=0)
    dcq, dck, dcv = _sb_bwd(sv["proj"], sv["tot_c"], d_o["c"])
    cols = dqs + dks + dvs + [dq_b, _kv_reduce(dk_x), _kv_reduce(dv_x), dcq, dck, dcv]
    dproj = jnp.concatenate([t.astype(BF16) for t in cols] + list(dgate), axis=1)
    dh1 = _mm_nt("proj_in_dx", dproj, w["w_in"], F32)
    dproj_s = dproj.reshape(S, N_CHIPS, IN_SHARD).transpose(1, 0, 2)
    grads["w_in"] = _proj_in_dw(h1, dproj_s, layer, into.get("w_in"))
    return dh1, grads, stats


def _ffn_fwd(h2, w):
    u = _mm_nn("ffn_up", h2, w["w_up"], F32, tn=1024)
    a = _conv_fwd(u, w["conv_w"], w["conv_b"])
    dn = _mm_nn("ffn_down", a, w["w_down"], F32)
    return dn, dict(u=u, a=a)


def _ffn_bwd(d_dn, h2, w, sv, into):
    grads, layer = {}, w["layer"]
    da = _mm_nt("ffn_down_dx", d_dn, w["w_down"], F32, tn=1024)
    grads["w_down"] = _mm_tn_stacked("ffn_down_dw", sv["a"], d_dn, layer, into.get("w_down"), True)
    dug, duv, dwg, dwv, dbg, dbv = _conv_bwd(sv["u"], w["conv_w"], w["conv_b"], da)
    du = jnp.concatenate([dug, duv], axis=1)
    grads["conv_w"] = jnp.concatenate([dwg, dwv], axis=1)
    grads["conv_b"] = jnp.concatenate([dbg, dbv], axis=1)
    dh2 = _mm_nt("ffn_up_dx", du, w["w_up"], F32)
    grads["w_up"] = _mm_tn_stacked("ffn_up_dw", h2, du, layer, into.get("w_up"), False, tn=1024)
    return dh2, grads


BIG = ("w_in", "w_br_a", "w_br_b", "w_br_c", "w_out", "w_up", "w_down")
SMALL_ROWS = (("rel_bias", 8), ("attn_pre_norm", 16), ("attn_post_norm", 16), ("ffn_pre_norm", 16), ("ffn_post_norm", 16),
              ("sinks", 8), ("conv_b", 128), ("b_gate", 48), ("conv_w", 384), ("loss", 8))


def _pack_small(vals):
    rows = []
    for name, n in SMALL_ROWS:
        flat = vals[name].reshape(-1).astype(F32)
        rows.append(jnp.pad(flat, (0, n * 128 - flat.shape[0])).reshape(n, 128))
    return jnp.concatenate(rows, axis=0)


def _unpack_small(block, shapes):
    out, row = {}, 0
    for name, n in SMALL_ROWS:
        size = int(np.prod(shapes[name]))
        out[name] = block[row:row + n].reshape(-1)[:size].reshape(shapes[name])
        row += n
    return out


def kernel(x, rel_bias, attn_pre_norm, w_in, b_gate, sinks, w_br_a, w_br_b, w_br_c, w_out, attn_post_norm, ffn_pre_norm, w_up, conv_w, conv_b, w_down, ffn_post_norm, loss_target, m_rel_bias, m_attn_pre_norm, m_w_in, m_b_gate, m_sinks, m_w_br_a, m_w_br_b, m_w_br_c, m_w_out, m_attn_post_norm, m_ffn_pre_norm, m_w_up, m_conv_w, m_conv_b, m_w_down, m_ffn_post_norm, v_rel_bias, v_attn_pre_norm, v_w_in, v_b_gate, v_sinks, v_w_br_a, v_w_br_b, v_w_br_c, v_w_out, v_attn_post_norm, v_ffn_pre_norm, v_w_up, v_conv_w, v_conv_b, v_w_down, v_ffn_post_norm):
    names = ("rel_bias", "attn_pre_norm", "w_in", "b_gate", "sinks", "w_br_a", "w_br_b", "w_br_c", "w_out",
             "attn_post_norm", "ffn_pre_norm", "w_up", "conv_w", "conv_b", "w_down", "ffn_post_norm")
    weights = dict(zip(names, (rel_bias, attn_pre_norm, w_in, b_gate, sinks, w_br_a, w_br_b, w_br_c, w_out,
                               attn_post_norm, ffn_pre_norm, w_up, conv_w, conv_b, w_down, ffn_post_norm)))
    mom1 = dict(zip(names, (m_rel_bias, m_attn_pre_norm, m_w_in, m_b_gate, m_sinks, m_w_br_a, m_w_br_b, m_w_br_c,
                            m_w_out, m_attn_post_norm, m_ffn_pre_norm, m_w_up, m_conv_w, m_conv_b, m_w_down,
                            m_ffn_post_norm)))
    mom2 = dict(zip(names, (v_rel_bias, v_attn_pre_norm, v_w_in, v_b_gate, v_sinks, v_w_br_a, v_w_br_b, v_w_br_c,
                            v_w_out, v_attn_post_norm, v_ffn_pre_norm, v_w_up, v_conv_w, v_conv_b, v_w_down,
                            v_ffn_post_norm)))

    chip = 2 * lax.axis_index("x") + lax.axis_index("y")
    core = lax.axis_index("c")
    gathered = _gather_weights([_cast_into_slot("cast_" + n, weights[n], chip) for n in BIG], [b_gate, conv_w])
    gathered = dict(zip(BIG + ("b_gate", "conv_w"), gathered))
    layers = []
    for l in range(DEPTH):
        w = {"layer": l}
        for n in ("w_in", "w_br_a", "w_br_b", "w_br_c", "w_up", "b_gate", "conv_w"):
            w[n] = _full_cols(gathered[n][:, l])
        w["w_out"] = gathered["w_out"][:, l].reshape(D, D)
        w["w_down"] = gathered["w_down"][:, l].reshape(D_FF, D)
        w["conv_b"] = conv_b[l:l + 1]
        layers.append(w)

    local = _local_step(x.reshape(S, D), loss_target.reshape(S, D), layers, rel_bias, sinks, attn_pre_norm,
                        attn_post_norm, ffn_pre_norm, ffn_post_norm)
    return _reduce_and_update(x.shape, names, weights, mom1, mom2, chip, core, *local)


def _local_step(xs, target, layers, rel_bias, sinks, attn_pre_norm, attn_post_norm, ffn_pre_norm, ffn_post_norm):
    bidx = jnp.asarray(_bucket_maps())

    saved = []
    h1 = _rms_fwd("pre_norm_first", xs, attn_pre_norm[0:1])
    x_in = xs
    for l in range(DEPTH):
        w = layers[l]
        mo, sv_mix = _mixer_fwd(h1, w, rel_bias, sinks[l], bidx)
        x_mid, h2 = _post_pre_fwd("post_attn_norm", x_in, mo, attn_post_norm[l:l + 1], ffn_pre_norm[l:l + 1])
        dn, sv_ffn = _ffn_fwd(h2, w)
        g_next = attn_pre_norm[l + 1:l + 2] if l + 1 < DEPTH else None
        x_out, h1_next = _post_pre_fwd("post_ffn_norm" if l + 1 < DEPTH else "post_ffn_norm_last", x_mid, dn,
                                       ffn_post_norm[l:l + 1], g_next)
        saved.append(dict(x_in=x_in, h1=h1, mo=mo, x_mid=x_mid, h2=h2, dn=dn, mix=sv_mix, ffn=sv_ffn))
        x_in, h1 = x_out, h1_next

    loss_row, dres = _loss_kernel(x_in, target)

    big_grads = {}
    small = [None] * DEPTH
    stats = jnp.zeros((N_BAND_Q, 8, 128), F32)
    dh_next = None
    for l in reversed(range(DEPTH)):
        w, sv = layers[l], saved[l]
        if l + 1 < DEPTH:
            pre = (saved[l + 1]["x_in"], attn_pre_norm[l + 1:l + 2], dh_next)
            dres, d_dn, dg_pre_next, dg_fpost = _norm_bwd("post_ffn_norm_bwd", dres, pre,
                                                          (sv["dn"], ffn_post_norm[l:l + 1]))
            small[l + 1]["attn_pre_norm"] = dg_pre_next
        else:
            dres, d_dn, _, dg_fpost = _norm_bwd("post_ffn_norm_last_bwd", dres, None, (sv["dn"], ffn_post_norm[l:l + 1]))
        dh2, g_ffn = _ffn_bwd(d_dn, sv["h2"], w, sv["ffn"], big_grads)
        dres, d_mo, dg_fpre, dg_apost = _norm_bwd("post_attn_norm_bwd", dres, (sv["x_mid"], ffn_pre_norm[l:l + 1], dh2),
                                                  (sv["mo"], attn_post_norm[l:l + 1]))
        dh_next, g_mix, stats = _mixer_bwd(d_mo, sv["h1"], w, sv["mix"], rel_bias, sinks[l], bidx, stats,
                                           big_grads)
        big_grads = {n: {**g_ffn, **g_mix}[n] for n in BIG}
        small[l] = dict(ffn_post_norm=dg_fpost, ffn_pre_norm=dg_fpre, attn_post_norm=dg_apost,
                        sinks=stats[N_A:, 1, 0], conv_b=g_ffn["conv_b"], b_gate=g_mix["b_gate"], conv_w=g_ffn["conv_w"])
    grad_x, _, dg_pre0, _ = _norm_bwd("pre_norm_first_bwd", dres, (saved[0]["x_in"], attn_pre_norm[0:1], dh_next), None)
    small[0]["attn_pre_norm"] = dg_pre0
    return loss_row, grad_x, big_grads, small, stats


def _reduce_and_update(x_shape, names, weights, mom1, mom2, chip, core, loss_row, grad_x, big_grads, small, stats):
    small_vals = {n: jnp.stack([small[l][n].reshape(weights[n].shape[1:]) for l in range(DEPTH)])
                  for n in ("attn_pre_norm", "attn_post_norm", "ffn_pre_norm", "ffn_post_norm", "conv_b", "sinks")}
    small_vals["b_gate"] = jnp.stack([small[l]["b_gate"] for l in range(DEPTH)])
    small_vals["conv_w"] = jnp.stack([small[l]["conv_w"] for l in range(DEPTH)])
    small_vals["rel_bias"] = stats[:, 0, :NUM_BUCKETS].T
    small_vals["loss"] = loss_row[0, :1]
    shapes = {n: v.shape for n, v in small_vals.items()}
    reduced = _unpack_small(_all_reduce_small(_pack_small(small_vals)), shapes)
    reduced["b_gate"] = lax.dynamic_slice_in_dim(reduced["b_gate"], chip * (D // N_CHIPS), D // N_CHIPS, axis=2)
    reduced["conv_w"] = lax.dynamic_slice_in_dim(reduced["conv_w"], chip * (2 * D_FF // N_CHIPS), 2 * D_FF // N_CHIPS, axis=2)

    stacked = [big_grads[n] for n in BIG]
    got = _swap_layers(stacked)
    parts = [_add_sibling("add_sibling_" + n, g, t, core) for n, g, t in zip(BIG, stacked, got)]
    arrived = _scatter_shards(parts)
    halves = [_add_chips("add_chips_" + n, p, t, chip, core) for n, p, t in zip(BIG, parts, arrived)]
    full = _join_layers(halves)
    grads = dict(zip(BIG, [f.reshape(weights[n].shape) for n, f in zip(BIG, full)]))
    for n in names:
        if n not in grads:
            grads[n] = reduced[n].reshape(weights[n].shape)

    delta, new_m, new_v = {}, {}, {}
    for n in names:
        delta[n], new_m[n], new_v[n] = _adamw("adamw_" + n, weights[n], grads[n], mom1[n], mom2[n])

    loss = reduced["loss"].reshape(())
    return (loss, grad_x.reshape(x_shape), *[grads[n] for n in names], *[delta[n] for n in names],
            *[new_m[n] for n in names], *[new_v[n] for n in names])
```

```python
import functools
import math

import numpy as np
import jax
import jax.numpy as jnp
from jax import lax
from jax.experimental import pallas as pl
from jax.experimental.pallas import tpu as pltpu

F32 = jnp.float32
BF16 = jnp.bfloat16

S = 2048
D = 1024
DEPTH = 2
HD = 64
BLK = 128
NQB = S // BLK
A_GROUPS = ((128, 1), (512, 4), (2048, 16))
N_BAND_Q = 20
N_A = 12
NUM_BUCKETS = 32
MAX_DISTANCE = 2048
D_FF = 4096
IN_COLS = 6912
IN_SHARD = IN_COLS // 4
OFF_GATE = 3840
EPS = 1e-6
SCALE = HD ** -0.5
NEG = -1e30
N_CHIPS = 4
N_DEV = 8

ADAM_LR = 0.001
ADAM_B1 = 0.9
ADAM_B2 = 0.999
ADAM_EPS = 1e-08
ADAM_WD = 0.01
ADAM_STEP = 10

VMEM_LIMIT = 56 * 1024 * 1024

NN = (((1,), (0,)), ((), ()))
NT = (((1,), (1,)), ((), ()))
TN = (((0,), (0,)), ((), ()))

MESH = pl.DeviceIdType.MESH
ANY = pl.BlockSpec(memory_space=pl.ANY)


def _dot(a, b, dims):
    return lax.dot_general(a, b, dims, preferred_element_type=F32)


def _params(sem):
    return pltpu.CompilerParams(dimension_semantics=sem, vmem_limit_bytes=VMEM_LIMIT)


def _matmul(name, a, b, out_shape, out_dtype, grid, a_spec, b_spec, o_spec, dims, acc_shape):
    nk = grid[-1]

    def body(a_ref, b_ref, o_ref, *scratch):
        part = _dot(a_ref[...].astype(BF16), b_ref[...].astype(BF16), dims)
        if nk == 1:
            o_ref[...] = part.astype(o_ref.dtype)
            return
        acc_ref, = scratch
        k = pl.program_id(len(grid) - 1)

        @pl.when(k == 0)
        def _():
            acc_ref[...] = part

        @pl.when(k > 0)
        def _():
            acc_ref[...] += part

        @pl.when(k == nk - 1)
        def _():
            o_ref[...] = acc_ref[...].astype(o_ref.dtype)

    scratch = [] if nk == 1 else [pltpu.VMEM(acc_shape, F32)]
    sem = ("parallel",) * (len(grid) - 1) + ("arbitrary",)
    return pl.pallas_call(
        body, name=name, grid=grid, in_specs=[a_spec, b_spec], out_specs=o_spec,
        out_shape=jax.ShapeDtypeStruct(out_shape, out_dtype), scratch_shapes=scratch,
        compiler_params=_params(sem))(a, b)


FULL_K = 8192


def _mm_tn_sharded(name, a, b, row_sharded, tm=512, tn=512, tk=FULL_K):
    k, m = a.shape
    n = b.shape[1]
    m4, n4 = (m // N_CHIPS, n) if row_sharded else (m, n // N_CHIPS)
    tm, tn, tk = min(tm, m4), min(tn, n4), min(tk, k)
    per_m, per_n = m4 // tm, n4 // tn
    if row_sharded:
        o_map = lambda i, j, l: (i // per_m, i % per_m, j)
    else:
        o_map = lambda i, j, l: (j // per_n, i, j % per_n)
    return _matmul(name, a, b, (N_CHIPS, m4, n4), BF16, (m // tm, n // tn, k // tk),
                   pl.BlockSpec((tk, tm), lambda i, j, l: (l, i)),
                   pl.BlockSpec((tk, tn), lambda i, j, l: (l, j)),
                   pl.BlockSpec((None, tm, tn), o_map), TN, (tm, tn))


def _mm_nn(name, a, b, out_dtype, tm=512, tn=512, tk=FULL_K):
    m, k = a.shape
    n = b.shape[1]
    tm, tn, tk = min(tm, m), min(tn, n), min(tk, k)
    return _matmul(name, a, b, (m, n), out_dtype, (m // tm, n // tn, k // tk),
                   pl.BlockSpec((tm, tk), lambda i, j, l: (i, l)),
                   pl.BlockSpec((tk, tn), lambda i, j, l: (l, j)),
                   pl.BlockSpec((tm, tn), lambda i, j, l: (i, j)), NN, (tm, tn))


def _mm_nt(name, a, b, out_dtype, tm=512, tn=512, tk=FULL_K):
    m, k = a.shape
    n = b.shape[0]
    tm, tn, tk = min(tm, m), min(tn, n), min(tk, k)
    return _matmul(name, a, b, (m, n), out_dtype, (m // tm, n // tn, k // tk),
                   pl.BlockSpec((tm, tk), lambda i, j, l: (i, l)),
                   pl.BlockSpec((tn, tk), lambda i, j, l: (j, l)),
                   pl.BlockSpec((tm, tn), lambda i, j, l: (i, j)), NT, (tm, tn))


def _proj_in_dw(h, dproj_s):
    tm = 512
    return _matmul("proj_in_dw", h, dproj_s, (N_CHIPS, D, IN_SHARD), BF16, (N_CHIPS, D // tm, 1),
                   pl.BlockSpec((S, tm), lambda j, i, l: (0, i)),
                   pl.BlockSpec((None, S, IN_SHARD), lambda j, i, l: (j, 0, 0)),
                   pl.BlockSpec((None, tm, IN_SHARD), lambda j, i, l: (j, i, 0)), TN, (tm, IN_SHARD))


TR = 256


def _row_spec(width=D):
    return pl.BlockSpec((TR, width), lambda i: (i, 0))


def _vec_spec(width=D):
    return pl.BlockSpec((1, width), lambda i: (0, 0))


def _rms(x, g):
    r = lax.rsqrt(jnp.mean(x * x, axis=-1, keepdims=True) + EPS)
    return x * r * g


def _rms_fwd(name, x, g):
    def body(x_ref, g_ref, h_ref):
        h_ref[...] = _rms(x_ref[...], g_ref[...]).astype(BF16)

    return pl.pallas_call(
        body, name=name, grid=(S // TR,), in_specs=[_row_spec(), _vec_spec()], out_specs=_row_spec(),
        out_shape=jax.ShapeDtypeStruct((S, D), BF16), compiler_params=_params(("parallel",)))(x, g)


def _post_pre_fwd(name, x, y, g_post, g_pre):
    has_pre = g_pre is not None

    def body(*refs):
        if has_pre:
            x_ref, y_ref, gp_ref, gn_ref, xn_ref, h_ref = refs
        else:
            x_ref, y_ref, gp_ref, xn_ref = refs
        xn = x_ref[...] + _rms(y_ref[...], gp_ref[...])
        xn_ref[...] = xn
        if has_pre:
            h_ref[...] = _rms(xn, gn_ref[...]).astype(BF16)

    ins = [x, y, g_post] + ([g_pre] if has_pre else [])
    in_specs = [_row_spec(), _row_spec(), _vec_spec()] + ([_vec_spec()] if has_pre else [])
    out_shape = [jax.ShapeDtypeStruct((S, D), F32)] + ([jax.ShapeDtypeStruct((S, D), BF16)] if has_pre else [])
    out_specs = [_row_spec()] + ([_row_spec()] if has_pre else [])
    out = pl.pallas_call(
        body, name=name, grid=(S // TR,), in_specs=in_specs, out_specs=out_specs, out_shape=out_shape,
        compiler_params=_params(("parallel",)))(*ins)
    return out if has_pre else (out[0], None)


def _rms_bwd_math(x, g, dy):
    r = lax.rsqrt(jnp.mean(x * x, axis=-1, keepdims=True) + EPS)
    n = x * r
    dn = dy * g
    dx = r * (dn - n * jnp.mean(dn * n, axis=-1, keepdims=True))
    return dx, jnp.sum(dy * n, axis=0, keepdims=True)


def _norm_bwd(name, dres, pre=None, post=None):
    has_pre, has_post = pre is not None, post is not None

    def body(*refs):
        refs = list(refs)
        dres_ref = refs.pop(0)
        if has_pre:
            xn_ref, gn_ref, dh_ref = refs[:3]
            refs = refs[3:]
        if has_post:
            y_ref, gp_ref = refs[:2]
            refs = refs[2:]
        dxn_ref = refs.pop(0)
        dy_ref = refs.pop(0) if has_post else None
        dgn_ref = refs.pop(0) if has_pre else None
        dgp_ref = refs.pop(0) if has_post else None
        first = pl.program_id(0) == 0
        dxn = dres_ref[...]
        if has_pre:
            dx, dg = _rms_bwd_math(xn_ref[...], gn_ref[...], dh_ref[...])
            dxn = dxn + dx

            @pl.when(first)
            def _():
                dgn_ref[...] = dg

            @pl.when(jnp.logical_not(first))
            def _():
                dgn_ref[...] += dg
        dxn_ref[...] = dxn
        if has_post:
            dy, dg = _rms_bwd_math(y_ref[...], gp_ref[...], dxn)
            dy_ref[...] = dy.astype(BF16)

            @pl.when(first)
            def _():
                dgp_ref[...] = dg

            @pl.when(jnp.logical_not(first))
            def _():
                dgp_ref[...] += dg

    ins, in_specs = [dres], [_row_spec()]
    if has_pre:
        ins += list(pre)
        in_specs += [_row_spec(), _vec_spec(), _row_spec()]
    if has_post:
        ins += list(post)
        in_specs += [_row_spec(), _vec_spec()]
    out_shape, out_specs = [jax.ShapeDtypeStruct((S, D), F32)], [_row_spec()]
    if has_post:
        out_shape.append(jax.ShapeDtypeStruct((S, D), BF16))
        out_specs.append(_row_spec())
    for _ in range(int(has_pre) + int(has_post)):
        out_shape.append(jax.ShapeDtypeStruct((1, D), F32))
        out_specs.append(_vec_spec())
    out = list(pl.pallas_call(
        body, name=name, grid=(S // TR,), in_specs=in_specs, out_specs=out_specs, out_shape=out_shape,
        compiler_params=_params(("arbitrary",)))(*ins))
    dxn = out.pop(0)
    dy = out.pop(0) if has_post else None
    dgn = out.pop(0) if has_pre else None
    dgp = out.pop(0) if has_post else None
    return dxn, dy, dgn, dgp


def _loss_kernel(y, target):
    def body(y_ref, t_ref, loss_ref, dy_ref):
        e = y_ref[...] - t_ref[...]
        dy_ref[...] = e * (1.0 / D)
        part = jnp.zeros((1, 128), F32) + 0.5 * jnp.sum(jnp.mean(e * e, axis=-1, keepdims=True))

        @pl.when(pl.program_id(0) == 0)
        def _():
            loss_ref[...] = part

        @pl.when(pl.program_id(0) > 0)
        def _():
            loss_ref[...] += part

    return pl.pallas_call(
        body, name="loss", grid=(S // TR,), in_specs=[_row_spec(), _row_spec()],
        out_specs=[_vec_spec(128), _row_spec()],
        out_shape=[jax.ShapeDtypeStruct((1, 128), F32), jax.ShapeDtypeStruct((S, D), F32)],
        compiler_params=_params(("arbitrary",)))(y, target)


def _t5_bucket_np(dist):
    max_exact = NUM_BUCKETS // 2
    nf = np.maximum(dist, 1).astype(np.float32)
    large = max_exact + (np.log(nf / max_exact) / np.float32(math.log(MAX_DISTANCE / max_exact))
                         * (NUM_BUCKETS - max_exact)).astype(np.int32)
    large = np.minimum(large, NUM_BUCKETS - 1)
    return np.where(dist < max_exact, dist, large).astype(np.int32)


def _bucket_maps():
    a = np.arange(BLK)[:, None]
    b = np.arange(2 * BLK)[None, :]
    dist = np.maximum(a + BLK - b, 0)
    maps = [_t5_bucket_np(dist * d) for _, d in A_GROUPS] + [_t5_bucket_np(dist)]
    return np.stack(maps).astype(np.int32)


def _classes(arr, d):
    return arr.reshape(S // d, d * arr.shape[1])


def _band_spec(arr, col0, prev):
    ncol = arr.shape[1] // 128
    if prev:
        return pl.BlockSpec((BLK, 128), lambda p, r, b: (jnp.maximum(b - 1, 0), r * ncol + col0 + p))
    return pl.BlockSpec((BLK, 128), lambda p, r, b: (b, r * ncol + col0 + p))


def _band_bias(tab_ref, bidx_ref, h):
    bi = bidx_ref[...]
    bias = jnp.zeros((BLK, 2 * BLK), F32)
    for kk in range(NUM_BUCKETS):
        bias = jnp.where(bi == kk, tab_ref[kk, h], bias)
    return bias


def _lane_lo():
    return lax.broadcasted_iota(jnp.int32, (BLK, 128), 1) < HD


def _per_head(x, lo):
    return (jnp.sum(jnp.where(lo, x, 0.0), axis=1, keepdims=True) * (1.0 / HD),
            jnp.sum(jnp.where(lo, 0.0, x), axis=1, keepdims=True) * (1.0 / HD))


def _band_mask(b, maxd):
    a = lax.broadcasted_iota(jnp.int32, (2 * BLK, 2 * BLK), 0) & (BLK - 1)
    c = lax.broadcasted_iota(jnp.int32, (2 * BLK, 2 * BLK), 1)
    dist = a + BLK - c
    return jnp.logical_and(jnp.logical_and(dist >= 0, dist <= maxd), jnp.logical_or(c >= BLK, b > 0))


def _stack_heads(x, lo, dtype=BF16):
    return jnp.concatenate([jnp.where(lo, x, 0.0), jnp.where(lo, 0.0, x)], axis=0).astype(dtype)


def _unstack_heads(x, lo):
    return jnp.where(lo, x[:BLK], x[BLK:])


def _stack_rows(prev_ref, cur_ref):
    return jnp.concatenate([prev_ref[...], cur_ref[...]], axis=0).astype(BF16)


def _band_fwd(name, d, n_pairs, maxd, head0, srcs, bidx_g, tab, sinks):
    nb = S // d // BLK
    (qa, qc), (ka, kc), (va, vc) = srcs
    out_spec = pl.BlockSpec((BLK, 128), lambda p, r, b: (b, r * n_pairs + p))
    smem = pl.BlockSpec(memory_space=pltpu.SMEM)
    full = pl.BlockSpec((BLK, 2 * BLK), lambda p, r, b: (0, 0))

    def body(tab_ref, sink_ref, q_ref, kp_ref, kc_ref, vp_ref, vc_ref, bidx_ref, o_ref, lse_ref, bias_ref):
        p, r, b = pl.program_id(0), pl.program_id(1), pl.program_id(2)

        @pl.when(jnp.logical_and(r == 0, b == 0))
        def _():
            for h in range(2):
                bias_ref[h * BLK:(h + 1) * BLK, :] = _band_bias(tab_ref, bidx_ref, head0 + 2 * p + h)

        lo = _lane_lo()
        qs = _stack_heads(q_ref[...] * SCALE, lo)
        ks, vs = _stack_rows(kp_ref, kc_ref), _stack_rows(vp_ref, vc_ref)
        s = jnp.where(_band_mask(b, maxd), _dot(qs, ks, NT) + bias_ref[...], NEG)
        m = jnp.max(s, axis=1, keepdims=True)
        pr = jnp.exp(s - m)
        l = jnp.sum(pr, axis=1, keepdims=True)
        num = _dot(pr.astype(BF16), vs, NN)
        lse = m + jnp.log(l)
        sink = jnp.where(lax.broadcasted_iota(jnp.int32, (2 * BLK, 1), 0) < BLK, sink_ref[2 * p], sink_ref[2 * p + 1])
        sig = 1.0 / (1.0 + jnp.exp(sink - lse))
        o_ref[...] = _unstack_heads(num * (sig / l), lo)
        lse_ref[...] = _unstack_heads(lse + jnp.zeros((2 * BLK, 128), F32), lo)

    shape = jax.ShapeDtypeStruct((S // d, d * n_pairs * 128), F32)
    o, lse = pl.pallas_call(
        body, name=name, grid=(n_pairs, d, nb),
        in_specs=[smem, smem, _band_spec(qa, qc, False), _band_spec(ka, kc, True), _band_spec(ka, kc, False),
                  _band_spec(va, vc, True), _band_spec(va, vc, False), full],
        out_specs=[out_spec, out_spec], out_shape=[shape, shape],
        scratch_shapes=[pltpu.VMEM((2 * BLK, 2 * BLK), F32)],
        compiler_params=_params(("parallel", "arbitrary", "arbitrary")))(
            tab, sinks, _classes(qa, d), _classes(ka, d), _classes(ka, d), _classes(va, d), _classes(va, d), bidx_g)
    return o.reshape(S, n_pairs * 128), lse.reshape(S, n_pairs * 128)


def _band_bwd(name, d, n_pairs, maxd, head0, srcs, bidx_g, tab, sinks, o, lse, do, stats_in):
    nb = S // d // BLK
    rows = S // d
    (qa, qc), (ka, kc), (va, vc) = srcs
    blk_spec = pl.BlockSpec((BLK, 128), lambda p, r, b: (b, r * n_pairs + p))
    cls_spec = pl.BlockSpec((rows, 128), lambda p, r, b: (0, r * n_pairs + p))
    smem = pl.BlockSpec(memory_space=pltpu.SMEM)
    full = pl.BlockSpec((BLK, 2 * BLK), lambda p, r, b: (0, 0))
    stat_spec = pl.BlockSpec((2, 8, 128), lambda p, r, b: (p, 0, 0))

    def body(tab_ref, sink_ref, q_ref, kp_ref, kc_ref, vp_ref, vc_ref, bidx_ref, o_ref, lse_ref, do_ref, sin_ref,
             dq_ref, dk_ref, dv_ref, stat_ref, bias_ref, dsacc_ref, sk_ref):
        p, r, b = pl.program_id(0), pl.program_id(1), pl.program_id(2)

        @pl.when(jnp.logical_and(r == 0, b == 0))
        def _():
            for h in range(2):
                bias_ref[h * BLK:(h + 1) * BLK, :] = _band_bias(tab_ref, bidx_ref, head0 + 2 * p + h)
            dsacc_ref[...] = jnp.zeros_like(dsacc_ref)
            sk_ref[...] = jnp.zeros_like(sk_ref)

        @pl.when(b == 0)
        def _():
            dk_ref[...] = jnp.zeros_like(dk_ref)
            dv_ref[...] = jnp.zeros_like(dv_ref)

        lo = _lane_lo()
        qs = _stack_heads(q_ref[...] * SCALE, lo)
        ks, vs = _stack_rows(kp_ref, kc_ref), _stack_rows(vp_ref, vc_ref)
        do = do_ref[...]
        dos = _stack_heads(do, lo, F32)
        lse = jnp.concatenate(_per_head(lse_ref[...], lo), axis=0)
        prod = do * o_ref[...]
        delta = jnp.concatenate([jnp.sum(jnp.where(lo, prod, 0.0), axis=1, keepdims=True),
                                 jnp.sum(jnp.where(lo, 0.0, prod), axis=1, keepdims=True)], axis=0)
        head1 = lax.broadcasted_iota(jnp.int32, (2 * BLK, 1), 0) >= BLK
        sig = 1.0 / (1.0 + jnp.exp(jnp.where(head1, sink_ref[2 * p + 1], sink_ref[2 * p]) - lse))
        s = _dot(qs, ks, NT) + bias_ref[...]
        pr = jnp.where(_band_mask(b, maxd), jnp.exp(s - lse), 0.0)
        ds = pr * (sig * (_dot(dos.astype(BF16), vs, NT) - delta))
        dsb = ds.astype(BF16)
        dq_ref[...] = SCALE * _unstack_heads(_dot(dsb, ks, NN), lo)
        dk = _dot(dsb, qs, TN)
        dv = _dot(pr.astype(BF16), (sig * dos).astype(BF16), TN)
        cur = pl.ds(pl.multiple_of(b * BLK, BLK), BLK)
        prev = pl.ds(pl.multiple_of(jnp.maximum(b - 1, 0) * BLK, BLK), BLK)
        dk_ref[prev, :] += dk[:BLK]
        dk_ref[cur, :] += dk[BLK:]
        dv_ref[prev, :] += dv[:BLK]
        dv_ref[cur, :] += dv[BLK:]
        dsacc_ref[...] += ds
        sink_grad = -delta * (1.0 - sig)
        for h in range(2):
            sk_ref[h] += jnp.zeros((8, 128), F32) + jnp.sum(sink_grad[h * BLK:(h + 1) * BLK])

        @pl.when(jnp.logical_and(r == d - 1, b == nb - 1))
        def _():
            bi = bidx_ref[...]
            lane = lax.broadcasted_iota(jnp.int32, (8, 128), 1)
            sub = lax.broadcasted_iota(jnp.int32, (8, 128), 0)
            for h in range(2):
                acc = dsacc_ref[h * BLK:(h + 1) * BLK, :]
                row = jnp.where(jnp.logical_and(sub == 1, lane == 0), sk_ref[h], 0.0)
                for kk in range(NUM_BUCKETS):
                    tot = jnp.sum(jnp.where(bi == kk, acc, 0.0))
                    row = jnp.where(jnp.logical_and(sub == 0, lane == kk), tot, row)
                stat_ref[h] = row + jnp.where(sub == 0, sin_ref[h], 0.0)

    shape = jax.ShapeDtypeStruct((rows, d * n_pairs * 128), F32)
    dq, dk, dv, stats = pl.pallas_call(
        body, name=name, grid=(n_pairs, d, nb),
        in_specs=[smem, smem, _band_spec(qa, qc, False), _band_spec(ka, kc, True), _band_spec(ka, kc, False),
                  _band_spec(va, vc, True), _band_spec(va, vc, False), full, blk_spec, blk_spec, blk_spec, stat_spec],
        out_specs=[blk_spec, cls_spec, cls_spec, stat_spec],
        out_shape=[shape, shape, shape, jax.ShapeDtypeStruct((2 * n_pairs, 8, 128), F32)],
        scratch_shapes=[pltpu.VMEM((2 * BLK, 2 * BLK), F32), pltpu.VMEM((2 * BLK, 2 * BLK), F32),
                        pltpu.VMEM((2, 8, 128), F32)],
        compiler_params=_params(("arbitrary", "arbitrary", "arbitrary")))(
            tab, sinks, _classes(qa, d), _classes(ka, d), _classes(ka, d), _classes(va, d), _classes(va, d), bidx_g,
            _classes(o, d), _classes(lse, d), _classes(do, d), stats_in)
    width = n_pairs * 128
    return dq.reshape(S, width), dk.reshape(S, width), dv.reshape(S, width), stats


def _comb_fwd(o_g, lse_g):
    def body(o0, o1, o2, l0, l1, l2, out_ref, outb_ref, lse_ref):
        a0, a1, a2 = l0[...], l1[...], l2[...]
        m = jnp.maximum(jnp.maximum(a0, a1), a2)
        e0, e1, e2 = jnp.exp(a0 - m), jnp.exp(a1 - m), jnp.exp(a2 - m)
        tot = e0 + e1 + e2
        out = (e0 * o0[...] + e1 * o1[...] + e2 * o2[...]) / tot
        out_ref[...] = out
        outb_ref[...] = out.astype(BF16)
        lse_ref[...] = m + jnp.log(tot)

    spec = _row_spec(4 * HD)
    f32 = jax.ShapeDtypeStruct((S, 4 * HD), F32)
    return pl.pallas_call(
        body, name="comb_fwd", grid=(S // TR,), in_specs=[spec] * 6, out_specs=[spec] * 3,
        out_shape=[f32, jax.ShapeDtypeStruct((S, 4 * HD), BF16), f32],
        compiler_params=_params(("parallel",)))(*o_g, *lse_g)


def _split2(x):
    hi = x.astype(BF16)
    return hi, (x - hi.astype(F32)).astype(BF16)


KB = 2 * BLK


def _tri_sum(x, tri):
    hi, lo = _split2(x)
    both = _dot(jnp.concatenate([hi, lo], axis=0), tri, NN)
    return both[:x.shape[0]] + both[x.shape[0]:]


def _tri(strict_upper):
    r = lax.broadcasted_iota(jnp.int32, (KB, KB), 0)
    c = lax.broadcasted_iota(jnp.int32, (KB, KB), 1)
    return jnp.where(r > c if strict_upper else r < c, 1.0, 0.0).astype(BF16)


def _sb_terms(qs, kj, before):
    z = _dot(qs, kj, NT)
    lsp = jnp.minimum(z, 0.0) - jnp.log(1.0 + jnp.exp(-jnp.abs(z)))
    return lsp, jnp.where(before, lsp - z, 0.0)


def _sb_before(i, m):
    t = (lax.broadcasted_iota(jnp.int32, (2 * BLK, KB), 0) & (BLK - 1)) + i * BLK
    s = lax.broadcasted_iota(jnp.int32, (2 * BLK, KB), 1) + m * KB
    return s < t


C_COL = 3072 // 128


def _sb_fwd(proj):
    blk = lambda off: pl.BlockSpec((BLK, 128), lambda p, i: (i, off + p))
    col = lambda off: pl.BlockSpec((S, 128), lambda p, i: (0, off + p))
    out = pl.BlockSpec((BLK, 128), lambda p, i: (i, p))

    def body(q_ref, k_ref, v_ref, o_ref, ob_ref, tot_ref):
        i = pl.program_id(1)
        lo = _lane_lo()
        qs = _stack_heads(q_ref[...] * SCALE, lo)
        suffix = _tri(True)

        def step(n, carry):
            acc, rest = carry
            m = i // 2 - n
            rows = pl.ds(pl.multiple_of(m * KB, KB), KB)
            kj, vj = k_ref[rows, :].astype(BF16), v_ref[rows, :].astype(BF16)
            before = _sb_before(i, m)
            lsp, lk = _sb_terms(qs, kj, before)
            w = jnp.where(before, jnp.exp(lsp + _tri_sum(lk, suffix) + rest), 0.0)
            return acc + _dot(w.astype(BF16), vj, NN), rest + jnp.sum(lk, axis=1, keepdims=True)

        acc, rest = lax.fori_loop(0, i // 2 + 1, step, (jnp.zeros((2 * BLK, 128), F32), jnp.zeros((2 * BLK, 1), F32)))
        o = _unstack_heads(acc, lo)
        o_ref[...] = o
        ob_ref[...] = o.astype(BF16)
        tot_ref[...] = _unstack_heads(rest + jnp.zeros((2 * BLK, 128), F32), lo)

    f32 = jax.ShapeDtypeStruct((S, 4 * HD), F32)
    return pl.pallas_call(
        body, name="sb_fwd", grid=(2, NQB), in_specs=[blk(C_COL), col(C_COL + 2), col(C_COL + 4)],
        out_specs=[out, out, out], out_shape=[f32, jax.ShapeDtypeStruct((S, 4 * HD), BF16), f32],
        compiler_params=_params(("parallel", "arbitrary")))(proj, proj, proj)


def _sb_bwd(proj, tot, do):
    blk = lambda off: pl.BlockSpec((BLK, 128), lambda p, i: (i, off + p))
    col = lambda off: pl.BlockSpec((S, 128), lambda p, i: (0, off + p))

    def body(q_ref, k_ref, v_ref, tot_ref, do_ref, dq_ref, dk_ref, dv_ref):
        i = pl.program_id(1)

        @pl.when(i == 0)
        def _():
            dk_ref[...] = jnp.zeros_like(dk_ref)
            dv_ref[...] = jnp.zeros_like(dv_ref)

        lo = _lane_lo()
        qs = _stack_heads(q_ref[...] * SCALE, lo)
        dos = _stack_heads(do_ref[...], lo)
        tots = jnp.concatenate(_per_head(tot_ref[...], lo), axis=0)
        prefix = _tri(False)

        def step(m, carry):
            dq, keep_left, g_left = carry
            rows = pl.ds(pl.multiple_of(m * KB, KB), KB)
            kj, vj = k_ref[rows, :].astype(BF16), v_ref[rows, :].astype(BF16)
            before = _sb_before(i, m)
            lsp, lk = _sb_terms(qs, kj, before)
            log_rest = tots - keep_left - lk - _tri_sum(lk, prefix)
            w = jnp.where(before, jnp.exp(lsp + log_rest), 0.0)
            g = w * _dot(dos, vj, NT)
            g_before = g_left + _dot(g.astype(BF16), prefix, NN)
            beta = jnp.exp(lsp)
            dz = jnp.where(before, g * (1.0 - beta) - g_before * beta, 0.0).astype(BF16)
            dk_ref[rows, :] += _dot(dz, qs, TN)
            dv_ref[rows, :] += _dot(w.astype(BF16), dos, TN)
            return (dq + _dot(dz, kj, NN), keep_left + jnp.sum(lk, axis=1, keepdims=True),
                    g_left + jnp.sum(g, axis=1, keepdims=True))

        zero = (jnp.zeros((2 * BLK, 128), F32), jnp.zeros((2 * BLK, 1), F32), jnp.zeros((2 * BLK, 1), F32))
        dq, _, _ = lax.fori_loop(0, i // 2 + 1, step, zero)
        dq_ref[...] = SCALE * _unstack_heads(dq, lo)

    out_blk = pl.BlockSpec((BLK, 128), lambda p, i: (i, p))
    out_col = pl.BlockSpec((S, 128), lambda p, i: (0, p))
    f32 = jax.ShapeDtypeStruct((S, 4 * HD), F32)
    return pl.pallas_call(
        body, name="sb_bwd", grid=(2, NQB),
        in_specs=[blk(C_COL), col(C_COL + 2), col(C_COL + 4), out_blk, out_blk],
        out_specs=[out_blk, out_col, out_col], out_shape=[f32, f32, f32],
        compiler_params=_params(("arbitrary", "arbitrary")))(proj, proj, proj, tot, do)


TG = 256
GATE_BLK0 = OFF_GATE // TG


def _gate_specs():
    grid = (D // TG, S // TG)
    p_specs = [pl.BlockSpec((TG, TG), functools.partial(lambda c, r, br: (r, GATE_BLK0 + br * (D // TG) + c), br=br))
               for br in range(3)]
    b_spec = pl.BlockSpec((3, TG), lambda c, r: (0, c))
    t_spec = pl.BlockSpec((TG, TG), lambda c, r: (r, c))
    return grid, p_specs, b_spec, t_spec


def _sigmoid(x):
    return 1.0 / (1.0 + jnp.exp(-x))


def _three_rows(rows):
    sub = lax.broadcasted_iota(jnp.int32, (3, rows[0].shape[1]), 0)
    return jnp.where(sub == 0, rows[0], jnp.where(sub == 1, rows[1], rows[2]))


def _gate_fwd(proj, b_gate, br):
    grid, p_specs, b_spec, t_spec = _gate_specs()

    def body(p0, p1, p2, b_ref, r0, r1, r2, out_ref):
        acc = jnp.zeros((TG, TG), F32)
        for n, (p, r) in enumerate(((p0, r0), (p1, r1), (p2, r2))):
            acc += _sigmoid(p[...] + b_ref[n:n + 1, :]) * r[...]
        out_ref[...] = acc.astype(BF16)

    return pl.pallas_call(
        body, name="gate_fwd", grid=grid, in_specs=p_specs + [b_spec] + [t_spec] * 3, out_specs=t_spec,
        out_shape=jax.ShapeDtypeStruct((S, D), BF16),
        compiler_params=_params(("parallel", "parallel")))(proj, proj, proj, b_gate, *br)


def _gate_bwd(proj, b_gate, br, dmerged):
    grid, p_specs, b_spec, t_spec = _gate_specs()

    def body(p0, p1, p2, b_ref, r0, r1, r2, dm_ref, e0, e1, e2, g0, g1, g2, db_ref):
        dm = dm_ref[...]
        rows = []
        for n, (p, r, e_ref, dg_ref) in enumerate(((p0, r0, e0, g0), (p1, r1, e1, g1), (p2, r2, e2, g2))):
            g = _sigmoid(p[...] + b_ref[n:n + 1, :])
            e_ref[...] = (dm * g).astype(BF16)
            dpre = dm * r[...] * g * (1.0 - g)
            dg_ref[...] = dpre.astype(BF16)
            rows.append(jnp.sum(dpre, axis=0, keepdims=True))
        db = _three_rows(rows)

        @pl.when(pl.program_id(1) == 0)
        def _():
            db_ref[...] = db

        @pl.when(pl.program_id(1) > 0)
        def _():
            db_ref[...] += db

    bf = jax.ShapeDtypeStruct((S, D), BF16)
    out = pl.pallas_call(
        body, name="gate_bwd", grid=grid, in_specs=p_specs + [b_spec] + [t_spec] * 4,
        out_specs=[t_spec] * 6 + [b_spec], out_shape=[bf] * 6 + [jax.ShapeDtypeStruct((3, D), F32)],
        compiler_params=_params(("parallel", "arbitrary")))(proj, proj, proj, b_gate, *br, dmerged)
    return out[:3], out[3:6], out[6]


TC = 256
N_FF_BLK = D_FF // TC
GELU_C = math.sqrt(2.0 / math.pi)


def _shift_down(x, n):
    rows = lax.broadcasted_iota(jnp.int32, x.shape, 0)
    return jnp.where(rows >= n, pltpu.roll(x, n, axis=0), 0.0)


def _shift_up(x, n):
    rows = lax.broadcasted_iota(jnp.int32, x.shape, 0)
    return jnp.where(rows < x.shape[0] - n, pltpu.roll(x, x.shape[0] - n, axis=0), 0.0)


def _conv(u, w, b):
    return w[2:3, :] * u + w[1:2, :] * _shift_down(u, 1) + w[0:1, :] * _shift_down(u, 2) + b


def _gelu_parts(x):
    inner = GELU_C * (x + 0.044715 * x * x * x)
    t = jnp.tanh(inner)
    gelu = 0.5 * x * (1.0 + t)
    dgelu = 0.5 * (1.0 + t) + 0.5 * x * (1.0 - t * t) * GELU_C * (1.0 + 3 * 0.044715 * x * x)
    return gelu, dgelu


def _conv_specs():
    ug = pl.BlockSpec((S, TC), lambda c: (0, c))
    uv = pl.BlockSpec((S, TC), lambda c: (0, N_FF_BLK + c))
    wg = pl.BlockSpec((3, TC), lambda c: (0, c))
    wv = pl.BlockSpec((3, TC), lambda c: (0, N_FF_BLK + c))
    bg = pl.BlockSpec((1, TC), lambda c: (0, c))
    bv = pl.BlockSpec((1, TC), lambda c: (0, N_FF_BLK + c))
    return ug, uv, wg, wv, bg, bv


def _conv_fwd(u, conv_w, conv_b):
    ug, uv, wg, wv, bg, bv = _conv_specs()

    def body(ug_ref, uv_ref, wg_ref, wv_ref, bg_ref, bv_ref, a_ref):
        gc = _conv(ug_ref[...], wg_ref[...], bg_ref[...])
        vc = _conv(uv_ref[...], wv_ref[...], bv_ref[...])
        a_ref[...] = (_gelu_parts(gc)[0] * vc).astype(BF16)

    return pl.pallas_call(
        body, name="conv_fwd", grid=(N_FF_BLK,), in_specs=[ug, uv, wg, wv, bg, bv], out_specs=ug,
        out_shape=jax.ShapeDtypeStruct((S, D_FF), BF16),
        compiler_params=_params(("parallel",)))(u, u, conv_w, conv_w, conv_b, conv_b)


def _conv_bwd(u, conv_w, conv_b, da):
    ug, uv, wg, wv, bg, bv = _conv_specs()

    def back(duc, u, w):
        du = w[2:3, :] * duc + w[1:2, :] * _shift_up(duc, 1) + w[0:1, :] * _shift_up(duc, 2)
        dw = _three_rows([jnp.sum(duc * _shift_down(u, 2), axis=0, keepdims=True),
                          jnp.sum(duc * _shift_down(u, 1), axis=0, keepdims=True),
                          jnp.sum(duc * u, axis=0, keepdims=True)])
        return du, dw, jnp.sum(duc, axis=0, keepdims=True)

    def body(ug_ref, uv_ref, wg_ref, wv_ref, bg_ref, bv_ref, da_ref, dug_ref, duv_ref, dwg_ref, dwv_ref, dbg_ref, dbv_ref):
        u_g, u_v = ug_ref[...], uv_ref[...]
        gc = _conv(u_g, wg_ref[...], bg_ref[...])
        vc = _conv(u_v, wv_ref[...], bv_ref[...])
        gelu, dgelu = _gelu_parts(gc)
        da = da_ref[...]
        du, dw, db = back(da * vc * dgelu, u_g, wg_ref[...])
        dug_ref[...] = du.astype(BF16)
        dwg_ref[...] = dw
        dbg_ref[...] = db
        du, dw, db = back(da * gelu, u_v, wv_ref[...])
        duv_ref[...] = du.astype(BF16)
        dwv_ref[...] = dw
        dbv_ref[...] = db

    return pl.pallas_call(
        body, name="conv_bwd", grid=(N_FF_BLK,), in_specs=[ug, uv, wg, wv, bg, bv, ug],
        out_specs=[ug, ug, wg, wg, bg, bg],
        out_shape=[jax.ShapeDtypeStruct((S, D_FF), BF16), jax.ShapeDtypeStruct((S, D_FF), BF16),
                   jax.ShapeDtypeStruct((3, D_FF), F32), jax.ShapeDtypeStruct((3, D_FF), F32),
                   jax.ShapeDtypeStruct((1, D_FF), F32), jax.ShapeDtypeStruct((1, D_FF), F32)],
        compiler_params=_params(("parallel",)))(u, u, conv_w, conv_w, conv_b, conv_b, da)


def _adamw(name, w, g, m, v):
    shape = w.shape
    cols = shape[-1]
    flat = [t.reshape(-1, cols) for t in (w, g, m, v)]
    r = flat[0].shape[0]
    tr = min(128, r)

    def body(w_ref, g_ref, m_ref, v_ref, d_ref, mo_ref, vo_ref):
        g = g_ref[...]
        m = ADAM_B1 * m_ref[...] + (1.0 - ADAM_B1) * g
        v = ADAM_B2 * v_ref[...] + (1.0 - ADAM_B2) * (g * g)
        m_hat = m / (1.0 - ADAM_B1 ** ADAM_STEP)
        v_hat = v / (1.0 - ADAM_B2 ** ADAM_STEP)
        d_ref[...] = -ADAM_LR * (m_hat / (jnp.sqrt(v_hat) + ADAM_EPS) + ADAM_WD * w_ref[...])
        mo_ref[...] = m
        vo_ref[...] = v

    spec = pl.BlockSpec((tr, cols), lambda i: (i, 0))
    outs = pl.pallas_call(
        body, name=name, grid=(pl.cdiv(r, tr),), in_specs=[spec] * 4, out_specs=[spec] * 3,
        out_shape=[jax.ShapeDtypeStruct((r, cols), F32)] * 3, compiler_params=_params(("parallel",)))(*flat)
    return [t.reshape(shape) for t in outs]


def _place():
    x, y, c = lax.axis_index("x"), lax.axis_index("y"), lax.axis_index("c")
    chips = [(1 - x, y), (x, 1 - y), (1 - x, 1 - y)]
    return x, y, c, chips


def _scalars(*vals):
    return jnp.stack([jnp.asarray(v, jnp.int32) for v in vals])


HBM = pl.BlockSpec(memory_space=pltpu.HBM)
SEM = pl.BlockSpec(memory_space=pltpu.SEMAPHORE)
SPLIT_COPY = pltpu.CompilerParams(has_side_effects=pltpu.SideEffectType.DATAFLOW_SIDE_EFFECTING)


def _in_hbm(x):
    return pltpu.with_memory_space_constraint(x, pltpu.HBM)


def _cast_into_slot(name, w, layer, chip):
    _, k, n4 = w.shape
    tr = min(256, k)

    def body(chip_ref, w_ref, o_ref):
        o_ref[...] = w_ref[...].astype(BF16)

    return pl.pallas_call(
        body, name=name,
        grid_spec=pltpu.PrefetchScalarGridSpec(
            num_scalar_prefetch=1, grid=(k // tr,),
            in_specs=[pl.BlockSpec((None, tr, n4), lambda i, s: (layer, i, 0))],
            out_specs=pl.BlockSpec((None, tr, n4), lambda i, s: (s[0], i, 0))),
        out_shape=jax.ShapeDtypeStruct((N_CHIPS, k, n4), BF16),
        compiler_params=_params(("parallel",)))(_scalars(chip), w)


def _gather_copy(buf_ref, k, from_chip, send_sem, recv_sem, chips, c):
    rows = buf_ref.at[from_chip]
    return pltpu.make_async_remote_copy(src_ref=rows, dst_ref=rows, send_sem=send_sem, recv_sem=recv_sem,
                                        device_id=(*chips[k], c), device_id_type=MESH)


def _gather_start(bufs, groups):
    n, ng = len(bufs), len(groups)
    where = {a: (gi, e) for gi, g in enumerate(groups) for e, a in enumerate(g)}

    def body(*refs):
        ins, sems, token = refs[:n], refs[n:n + 2 * ng], refs[-1]
        x, y, c, chips = _place()
        for a in range(n):
            gi, e = where[a]
            for k in range(3):
                _gather_copy(ins[a], k, 2 * x + y, sems[2 * gi].at[3 * e + k], sems[2 * gi + 1].at[3 * e + k],
                             chips, c).start()
        token[...] = jnp.zeros_like(token)

    out_shape = [pltpu.SemaphoreType.DMA((3 * len(g),)) for g in groups for _ in range(2)]
    out_shape += [pltpu.HBM(b.shape, b.dtype) for b in bufs] + [jax.ShapeDtypeStruct((8, 128), F32)]
    out = pl.pallas_call(
        body, name="gather_start", in_specs=[HBM] * n,
        out_specs=[SEM] * (2 * ng) + [HBM] * n + [pl.BlockSpec(memory_space=pltpu.VMEM)], out_shape=out_shape,
        input_output_aliases={a: 2 * ng + a for a in range(n)}, compiler_params=SPLIT_COPY)(*[_in_hbm(b) for b in bufs])
    sems = [(out[2 * gi], out[2 * gi + 1]) for gi in range(ng)]
    return sems, list(out[2 * ng:2 * ng + n]), out[-1]


def _gather_wait(name, bufs, send, recv, after):
    n = len(bufs)

    def body(*refs):
        ins, send_sem, recv_sem = refs[:n], refs[n], refs[n + 1]
        x, y, c, chips = _place()
        for e in range(n):
            for k in range(3):
                sems = (send_sem.at[3 * e + k], recv_sem.at[3 * e + k])
                _gather_copy(ins[e], k, 2 * x + y, *sems, chips, c).wait_send()
                _gather_copy(ins[e], k, 2 * chips[k][0] + chips[k][1], *sems, chips, c).wait_recv()

    return pl.pallas_call(
        body, name=name, in_specs=[HBM] * n + [SEM, SEM, ANY], out_specs=[HBM] * n,
        out_shape=[pltpu.HBM(b.shape, b.dtype) for b in bufs],
        input_output_aliases={a: a for a in range(n)}, compiler_params=SPLIT_COPY)(*bufs, send, recv, after)


def _reduce_copy(g_ref, land_ref, mask, send_sem, recv_sem, x, y, c, sending):
    px, py, pc = x ^ ((mask >> 2) & 1), y ^ ((mask >> 1) & 1), c ^ (mask & 1)
    half = g_ref.shape[1] // 2
    src = g_ref.at[2 * px + py, pl.ds(pl.multiple_of(pc * half, half), half)]
    dst = land_ref.at[4 * x + 2 * y + c] if sending else land_ref.at[4 * px + 2 * py + pc]
    return pltpu.make_async_remote_copy(src_ref=src, dst_ref=dst, send_sem=send_sem, recv_sem=recv_sem,
                                        device_id=(px, py, pc), device_id_type=MESH)


def _reduce_start(name, grads):
    n = len(grads)
    lands = [lax.empty((N_DEV, g.shape[1] // 2, g.shape[2]), g.dtype) for g in grads]

    def body(*refs):
        gs, ls, send_sem, recv_sem = refs[:n], refs[n:2 * n], refs[2 * n], refs[2 * n + 1]
        x, y, c, _ = _place()
        for a in range(n):
            for mask in range(1, N_DEV):
                s = (N_DEV - 1) * a + mask - 1
                _reduce_copy(gs[a], ls[a], mask, send_sem.at[s], recv_sem.at[s], x, y, c, True).start()

    sem = pltpu.SemaphoreType.DMA((n * (N_DEV - 1),))
    out = pl.pallas_call(
        body, name=name, in_specs=[HBM] * (2 * n), out_specs=[SEM, SEM] + [HBM] * (2 * n),
        out_shape=[sem, sem] + [pltpu.HBM(t.shape, t.dtype) for t in grads + lands],
        input_output_aliases={a: 2 + a for a in range(2 * n)}, compiler_params=SPLIT_COPY)(
            *[_in_hbm(t) for t in grads + lands])
    return out[0], out[1], list(out[2:2 + n]), list(out[2 + n:])


def _reduce_wait(name, send, recv, grads, lands, after):
    n = len(grads)

    def body(*refs):
        gs, ls, send_sem, recv_sem = refs[:n], refs[n:2 * n], refs[2 * n], refs[2 * n + 1]
        x, y, c, _ = _place()
        for a in range(n):
            for mask in range(1, N_DEV):
                s = (N_DEV - 1) * a + mask - 1
                sems = (send_sem.at[s], recv_sem.at[s])
                _reduce_copy(gs[a], ls[a], mask, *sems, x, y, c, True).wait_send()
                _reduce_copy(gs[a], ls[a], mask, *sems, x, y, c, False).wait_recv()

    out = pl.pallas_call(
        body, name=name, in_specs=[HBM] * (2 * n) + [SEM, SEM, ANY], out_specs=[HBM] * (2 * n),
        out_shape=[pltpu.HBM(t.shape, t.dtype) for t in grads + lands],
        input_output_aliases={a: a for a in range(2 * n)}, compiler_params=SPLIT_COPY)(*grads, *lands, send, recv, after)
    return list(out[:n]), list(out[n:])


def _reduce_sum(name, g, land, layer, into, chip, c):
    _, k4, n4 = g.shape
    half = k4 // 2
    tr = min(256, half)
    per = half // tr
    me = 2 * chip + c

    def body(s_ref, own_ref, *refs):
        total = own_ref[...].astype(F32)
        for ref in refs[:N_DEV - 1]:
            total = total + ref[...].astype(F32)
        refs[-1][...] = total

    in_specs = [pl.BlockSpec((None, tr, n4), lambda i, s: (s[0], s[1] * per + i, 0))]
    in_specs += [pl.BlockSpec((None, tr, n4), functools.partial(lambda i, s, m: (s[1 + m], i, 0), m=m))
                 for m in range(1, N_DEV)]
    ins = [g] + [land] * (N_DEV - 1)
    aliases = {}
    if into is not None:
        in_specs, ins, aliases = in_specs + [ANY], ins + [into], {1 + N_DEV: 0}
    return pl.pallas_call(
        body, name=name,
        grid_spec=pltpu.PrefetchScalarGridSpec(
            num_scalar_prefetch=1, grid=(per,), in_specs=in_specs,
            out_specs=pl.BlockSpec((None, tr, n4), lambda i, s: (layer, s[1] * per + i, 0))),
        out_shape=jax.ShapeDtypeStruct((DEPTH, k4, n4), F32), input_output_aliases=aliases,
        compiler_params=_params(("parallel",)))(_scalars(chip, c, *[me ^ m for m in range(1, N_DEV)]), *ins)


def _join_halves(bufs):
    n = len(bufs)

    def body(*refs):
        ins, outs = refs[:n], refs[n:2 * n]
        send_sem, recv_sem = refs[2 * n:]
        x, y, c, _ = _place()

        def rows(ref, which):
            half = ref.shape[1] // 2
            return ref.at[:, pl.ds(pl.multiple_of(which * half, half), half)]

        sends = [pltpu.make_async_remote_copy(
            src_ref=rows(ins[a], c), dst_ref=rows(outs[a], c), send_sem=send_sem.at[a], recv_sem=recv_sem.at[a],
            device_id=(x, y, 1 - c), device_id_type=MESH) for a in range(n)]
        for cp in sends:
            cp.start()
        for a in range(n):
            sends[a].wait_send()
            pltpu.make_async_remote_copy(
                src_ref=rows(ins[a], c), dst_ref=rows(outs[a], 1 - c), send_sem=send_sem.at[a], recv_sem=recv_sem.at[a],
                device_id=(x, y, 1 - c), device_id_type=MESH).wait_recv()

    return pl.pallas_call(
        body, name="join_halves", in_specs=[ANY] * n, out_specs=[ANY] * n,
        out_shape=[jax.ShapeDtypeStruct(b.shape, b.dtype) for b in bufs],
        input_output_aliases={a: a for a in range(n)},
        scratch_shapes=[pltpu.SemaphoreType.DMA((n,)), pltpu.SemaphoreType.DMA((n,))],
    )(*bufs)


def _all_reduce_small(block):
    r = block.shape[0]

    def body(x_ref, out_ref, slots, send_sem, recv_sem):
        x, y, c, _ = _place()
        me = 4 * x + 2 * y + c
        slots[me] = x_ref[...]
        sends = []
        for mask in range(1, N_DEV):
            fx, fy, fc = (mask >> 2) & 1, (mask >> 1) & 1, mask & 1
            peer = (x ^ fx, y ^ fy, c ^ fc)
            cp = pltpu.make_async_remote_copy(
                src_ref=x_ref, dst_ref=slots.at[me], send_sem=send_sem.at[mask - 1], recv_sem=recv_sem.at[mask - 1],
                device_id=peer, device_id_type=MESH)
            cp.start()
            sends.append(cp)
        for mask in range(1, N_DEV):
            src = me ^ mask
            pltpu.make_async_remote_copy(
                src_ref=x_ref, dst_ref=slots.at[src], send_sem=send_sem.at[mask - 1], recv_sem=recv_sem.at[mask - 1],
                device_id=(x, y, c), device_id_type=MESH).wait_recv()
        for cp in sends:
            cp.wait_send()
        total = slots[0]
        for d in range(1, N_DEV):
            total = total + slots[d]
        out_ref[...] = total

    vmem = pl.BlockSpec(memory_space=pltpu.VMEM)
    return pl.pallas_call(
        body, name="all_reduce_small", in_specs=[vmem], out_specs=vmem,
        out_shape=jax.ShapeDtypeStruct((r, 128), F32),
        scratch_shapes=[pltpu.VMEM((N_DEV, r, 128), F32), pltpu.SemaphoreType.DMA((N_DEV - 1,)),
                        pltpu.SemaphoreType.DMA((N_DEV - 1,))],
        compiler_params=pltpu.CompilerParams(vmem_limit_bytes=VMEM_LIMIT))(block)


B_Q_COL = 2304 // 128
B_K0, B_V0, B_END = 2816, 2944, 3072


def _full_cols(w_g):
    return w_g.transpose(1, 0, 2).reshape(w_g.shape[1], -1)


def _group_src(proj, g):
    if A_GROUPS[g][1] == 1:
        return ((proj, 2 * g), (proj, 6 + 2 * g), (proj, 12 + 2 * g))
    packed = jnp.concatenate([proj[:, t * 768 + g * 256:t * 768 + (g + 1) * 256] for t in range(3)], axis=1)
    return ((packed, 0), (packed, 2), (packed, 4))


def _kv_expand(kv):
    return jnp.broadcast_to(kv.reshape(S, 2, 1, HD), (S, 2, 4, HD)).reshape(S, 8 * HD)


def _kv_reduce(dkv):
    return dkv.reshape(S, 2, 4, HD).sum(axis=2).reshape(S, 2 * HD)


def _mixer_fwd(h1, wget, rel_bias, sinks_l, bidx):
    w = dict(wget(0, h1))
    proj = _mm_nn("proj_in", h1, w["w_in"], F32, tn=1152)
    no_sinks = jnp.full((4,), NEG, F32)
    srcs = [_group_src(proj, g) for g in range(3)]
    o_g, lse_g = [], []
    for g, (_, d) in enumerate(A_GROUPS):
        o, lse = _band_fwd("band_fwd_g%d" % g, d, 2, BLK, 4 * g, srcs[g], bidx[g], rel_bias, no_sinks)
        o_g.append(o)
        lse_g.append(lse)
    o_a32, o_a, lse_a = _comb_fwd(o_g, lse_g)
    src_b = ((proj, B_Q_COL), (_kv_expand(proj[:, B_K0:B_V0]), 0), (_kv_expand(proj[:, B_V0:B_END]), 0))
    o_b32, lse_b = _band_fwd("band_fwd_b", 1, 4, BLK - 1, N_A, src_b, bidx[3], rel_bias, sinks_l)
    o_b = o_b32.astype(BF16)
    o_c32, o_c, tot_c = _sb_fwd(proj)
    w.update(wget(1, o_c32))
    br = [_mm_nn("branch_a", o_a, w["w_br_a"], F32), _mm_nn("branch_b", o_b, w["w_br_b"], F32),
          _mm_nn("branch_c", o_c, w["w_br_c"], F32)]
    merged = _gate_fwd(proj, w["b_gate"], br)
    mo = _mm_nn("out_proj", merged, w["w_out"], F32)
    saved = dict(proj=proj, srcs=srcs, src_b=src_b, o_a32=o_a32, lse_a=lse_a, o_b32=o_b32, lse_b=lse_b, tot_c=tot_c,
                 o_a=o_a, o_b=o_b, o_c=o_c, br=br, merged=merged)
    return mo, saved, w


def _mixer_bwd(d_mo, h1, w, sv, rel_bias, sinks_l, bidx, stats_in):
    grads = {}
    dmerged = _mm_nt("out_proj_dx", d_mo, w["w_out"], F32)
    grads["w_out"] = _mm_tn_sharded("out_proj_dw", sv["merged"], d_mo, True)
    e, dgate, db_gate = _gate_bwd(sv["proj"], w["b_gate"], sv["br"], dmerged)
    grads["b_gate"] = db_gate
    d_o = {}
    for n, name in enumerate("abc"):
        d_o[name] = _mm_nt("branch_%s_dx" % name, e[n], w["w_br_" + name], F32)
        grads["w_br_" + name] = _mm_tn_sharded("branch_%s_dw" % name, sv["o_" + name], e[n], False)
    no_sinks = jnp.full((4,), NEG, F32)
    dqs, dks, dvs, stats = [], [], [], []
    for g, (_, d) in enumerate(A_GROUPS):
        dq, dk, dv, st = _band_bwd("band_bwd_g%d" % g, d, 2, BLK, 4 * g, sv["srcs"][g], bidx[g], rel_bias, no_sinks,
                                   sv["o_a32"], sv["lse_a"], d_o["a"], stats_in[4 * g:4 * g + 4])
        dqs.append(dq)
        dks.append(dk)
        dvs.append(dv)
        stats.append(st)
    dq_b, dk_x, dv_x, st = _band_bwd("band_bwd_b", 1, 4, BLK - 1, N_A, sv["src_b"], bidx[3], rel_bias, sinks_l,
                                     sv["o_b32"], sv["lse_b"], d_o["b"], stats_in[N_A:])
    stats = jnp.concatenate(stats + [st], axis=0)
    dcq, dck, dcv = _sb_bwd(sv["proj"], sv["tot_c"], d_o["c"])
    cols = dqs + dks + dvs + [dq_b, _kv_reduce(dk_x), _kv_reduce(dv_x), dcq, dck, dcv]
    dproj = jnp.concatenate([t.astype(BF16) for t in cols] + list(dgate), axis=1)
    dh1 = _mm_nt("proj_in_dx", dproj, w["w_in"], F32)
    dproj_s = dproj.reshape(S, N_CHIPS, IN_SHARD).transpose(1, 0, 2)
    grads["w_in"] = _proj_in_dw(h1, dproj_s)
    return dh1, grads, stats


def _ffn_fwd(h2, w):
    u = _mm_nn("ffn_up", h2, w["w_up"], F32, tn=1024)
    a = _conv_fwd(u, w["conv_w"], w["conv_b"])
    dn = _mm_nn("ffn_down", a, w["w_down"], F32)
    return dn, dict(u=u, a=a)


def _ffn_bwd(d_dn, h2, w, sv):
    grads = {}
    da = _mm_nt("ffn_down_dx", d_dn, w["w_down"], F32, tn=1024)
    grads["w_down"] = _mm_tn_sharded("ffn_down_dw", sv["a"], d_dn, True)
    dug, duv, dwg, dwv, dbg, dbv = _conv_bwd(sv["u"], w["conv_w"], w["conv_b"], da)
    du = jnp.concatenate([dug, duv], axis=1)
    grads["conv_w"] = jnp.concatenate([dwg, dwv], axis=1)
    grads["conv_b"] = jnp.concatenate([dbg, dbv], axis=1)
    dh2 = _mm_nt("ffn_up_dx", du, w["w_up"], F32)
    grads["w_up"] = _mm_tn_sharded("ffn_up_dw", h2, du, False, tn=1024)
    return dh2, grads


BIG = ("w_in", "w_br_a", "w_br_b", "w_br_c", "w_out", "w_up", "w_down")
WEIGHT_GROUPS = (("w_in", "b_gate"), ("w_br_a", "w_br_b", "w_br_c", "w_out"), ("w_up", "conv_w", "w_down"))
GRAD_GROUPS = (("w_down", "w_up"), ("w_out", "w_br_a", "w_br_b", "w_br_c", "w_in"))
SMALL_ROWS = (("rel_bias", 8), ("attn_pre_norm", 16), ("attn_post_norm", 16), ("ffn_pre_norm", 16), ("ffn_post_norm", 16),
              ("sinks", 8), ("conv_b", 128), ("b_gate", 48), ("conv_w", 384), ("loss", 8))


def _pack_small(vals):
    rows = []
    for name, n in SMALL_ROWS:
        flat = vals[name].reshape(-1).astype(F32)
        rows.append(jnp.pad(flat, (0, n * 128 - flat.shape[0])).reshape(n, 128))
    return jnp.concatenate(rows, axis=0)


def _unpack_small(block, shapes):
    out, row = {}, 0
    for name, n in SMALL_ROWS:
        size = int(np.prod(shapes[name]))
        out[name] = block[row:row + n].reshape(-1)[:size].reshape(shapes[name])
        row += n
    return out


def kernel(x, rel_bias, attn_pre_norm, w_in, b_gate, sinks, w_br_a, w_br_b, w_br_c, w_out, attn_post_norm, ffn_pre_norm, w_up, conv_w, conv_b, w_down, ffn_post_norm, loss_target, m_rel_bias, m_attn_pre_norm, m_w_in, m_b_gate, m_sinks, m_w_br_a, m_w_br_b, m_w_br_c, m_w_out, m_attn_post_norm, m_ffn_pre_norm, m_w_up, m_conv_w, m_conv_b, m_w_down, m_ffn_post_norm, v_rel_bias, v_attn_pre_norm, v_w_in, v_b_gate, v_sinks, v_w_br_a, v_w_br_b, v_w_br_c, v_w_out, v_attn_post_norm, v_ffn_pre_norm, v_w_up, v_conv_w, v_conv_b, v_w_down, v_ffn_post_norm):
    names = ("rel_bias", "attn_pre_norm", "w_in", "b_gate", "sinks", "w_br_a", "w_br_b", "w_br_c", "w_out",
             "attn_post_norm", "ffn_pre_norm", "w_up", "conv_w", "conv_b", "w_down", "ffn_post_norm")
    weights = dict(zip(names, (rel_bias, attn_pre_norm, w_in, b_gate, sinks, w_br_a, w_br_b, w_br_c, w_out,
                               attn_post_norm, ffn_pre_norm, w_up, conv_w, conv_b, w_down, ffn_post_norm)))
    mom1 = dict(zip(names, (m_rel_bias, m_attn_pre_norm, m_w_in, m_b_gate, m_sinks, m_w_br_a, m_w_br_b, m_w_br_c,
                            m_w_out, m_attn_post_norm, m_ffn_pre_norm, m_w_up, m_conv_w, m_conv_b, m_w_down,
                            m_ffn_post_norm)))
    mom2 = dict(zip(names, (v_rel_bias, v_attn_pre_norm, v_w_in, v_b_gate, v_sinks, v_w_br_a, v_w_br_b, v_w_br_c,
                            v_w_out, v_attn_post_norm, v_ffn_pre_norm, v_w_up, v_conv_w, v_conv_b, v_w_down,
                            v_ffn_post_norm)))

    chip = 2 * lax.axis_index("x") + lax.axis_index("y")
    core = lax.axis_index("c")

    keys = [(n, l) for l in range(DEPTH) for group in WEIGHT_GROUPS for n in group]
    bufs = []
    for n, l in keys:
        if n in BIG:
            bufs.append(_cast_into_slot("cast_" + n, weights[n], l, chip))
        else:
            shard = weights[n][l]
            bufs.append(lax.dynamic_update_slice(jnp.zeros((N_CHIPS,) + shard.shape, F32), shard[None],
                                                 (chip, jnp.int32(0), jnp.int32(0))))
    groups = [[keys.index((n, l)) for n in group] for l in range(DEPTH) for group in WEIGHT_GROUPS]
    sems, in_flight, _ = _gather_start(bufs, groups)

    def wget(l, gi, after):
        g = l * len(WEIGHT_GROUPS) + gi
        got = _gather_wait("gather_wait_%d_%d" % (l, gi), [in_flight[a] for a in groups[g]], *sems[g], after)
        out = {}
        for n, buf in zip(WEIGHT_GROUPS[gi], got):
            out[n] = buf.reshape(-1, buf.shape[-1]) if n in ("w_out", "w_down") else _full_cols(buf)
        if gi == len(WEIGHT_GROUPS) - 1:
            out["conv_b"] = conv_b[l:l + 1]
        return out

    pending = []

    def emit(l, gi, grads):
        group = GRAD_GROUPS[gi]
        pending.append((l, group) + _reduce_start("reduce_start_%d_%d" % (l, gi), [grads[n] for n in group]))

    local = _local_step(x.reshape(S, D), loss_target.reshape(S, D), wget, emit, rel_bias, sinks, attn_pre_norm,
                        attn_post_norm, ffn_pre_norm, ffn_post_norm)
    return _reduce_and_update(x.shape, names, weights, mom1, mom2, chip, core, pending, *local)


def _local_step(xs, target, wget, emit, rel_bias, sinks, attn_pre_norm, attn_post_norm, ffn_pre_norm, ffn_post_norm):
    bidx = jnp.asarray(_bucket_maps())

    saved, layers = [], []
    h1 = _rms_fwd("pre_norm_first", xs, attn_pre_norm[0:1])
    x_in = xs
    for l in range(DEPTH):
        mo, sv_mix, w = _mixer_fwd(h1, functools.partial(wget, l), rel_bias, sinks[l], bidx)
        x_mid, h2 = _post_pre_fwd("post_attn_norm", x_in, mo, attn_post_norm[l:l + 1], ffn_pre_norm[l:l + 1])
        w.update(wget(l, 2, h2))
        dn, sv_ffn = _ffn_fwd(h2, w)
        g_next = attn_pre_norm[l + 1:l + 2] if l + 1 < DEPTH else None
        x_out, h1_next = _post_pre_fwd("post_ffn_norm" if l + 1 < DEPTH else "post_ffn_norm_last", x_mid, dn,
                                       ffn_post_norm[l:l + 1], g_next)
        saved.append(dict(x_in=x_in, h1=h1, mo=mo, x_mid=x_mid, h2=h2, dn=dn, mix=sv_mix, ffn=sv_ffn))
        layers.append(w)
        x_in, h1 = x_out, h1_next

    loss_row, dres = _loss_kernel(x_in, target)

    small = [None] * DEPTH
    stats = jnp.zeros((N_BAND_Q, 8, 128), F32)
    dh_next = None
    for l in reversed(range(DEPTH)):
        w, sv = layers[l], saved[l]
        if l + 1 < DEPTH:
            pre = (saved[l + 1]["x_in"], attn_pre_norm[l + 1:l + 2], dh_next)
            dres, d_dn, dg_pre_next, dg_fpost = _norm_bwd("post_ffn_norm_bwd", dres, pre,
                                                          (sv["dn"], ffn_post_norm[l:l + 1]))
            small[l + 1]["attn_pre_norm"] = dg_pre_next
        else:
            dres, d_dn, _, dg_fpost = _norm_bwd("post_ffn_norm_last_bwd", dres, None, (sv["dn"], ffn_post_norm[l:l + 1]))
        dh2, g_ffn = _ffn_bwd(d_dn, sv["h2"], w, sv["ffn"])
        emit(l, 0, g_ffn)
        dres, d_mo, dg_fpre, dg_apost = _norm_bwd("post_attn_norm_bwd", dres, (sv["x_mid"], ffn_pre_norm[l:l + 1], dh2),
                                                  (sv["mo"], attn_post_norm[l:l + 1]))
        dh_next, g_mix, stats = _mixer_bwd(d_mo, sv["h1"], w, sv["mix"], rel_bias, sinks[l], bidx, stats)
        emit(l, 1, g_mix)
        small[l] = dict(ffn_post_norm=dg_fpost, ffn_pre_norm=dg_fpre, attn_post_norm=dg_apost,
                        sinks=stats[N_A:, 1, 0], conv_b=g_ffn["conv_b"], b_gate=g_mix["b_gate"], conv_w=g_ffn["conv_w"])
    grad_x, _, dg_pre0, _ = _norm_bwd("pre_norm_first_bwd", dres, (saved[0]["x_in"], attn_pre_norm[0:1], dh_next), None)
    small[0]["attn_pre_norm"] = dg_pre0
    return loss_row, grad_x, small, stats


def _reduce_and_update(x_shape, names, weights, mom1, mom2, chip, core, pending, loss_row, grad_x, small, stats):
    small_vals = {n: jnp.stack([small[l][n].reshape(weights[n].shape[1:]) for l in range(DEPTH)])
                  for n in ("attn_pre_norm", "attn_post_norm", "ffn_pre_norm", "ffn_post_norm", "conv_b", "sinks")}
    small_vals["b_gate"] = jnp.stack([small[l]["b_gate"] for l in range(DEPTH)])
    small_vals["conv_w"] = jnp.stack([small[l]["conv_w"] for l in range(DEPTH)])
    small_vals["rel_bias"] = stats[:, 0, :NUM_BUCKETS].T
    small_vals["loss"] = loss_row[0, :1]
    shapes = {n: v.shape for n, v in small_vals.items()}
    reduced = _unpack_small(_all_reduce_small(_pack_small(small_vals)), shapes)
    reduced["b_gate"] = lax.dynamic_slice_in_dim(reduced["b_gate"], chip * (D // N_CHIPS), D // N_CHIPS, axis=2)
    reduced["conv_w"] = lax.dynamic_slice_in_dim(reduced["conv_w"], chip * (2 * D_FF // N_CHIPS), 2 * D_FF // N_CHIPS, axis=2)

    summed = {}
    for l, group, send, recv, gs, lands in pending:
        gs, lands = _reduce_wait("reduce_wait_%d_%s" % (l, group[0]), send, recv, gs, lands, grad_x)
        for n, g, land in zip(group, gs, lands):
            summed[n] = _reduce_sum("reduce_sum_%d_%s" % (l, n), g, land, l, summed.get(n), chip, core)
    full = _join_halves([summed[n] for n in BIG])
    grads = dict(zip(BIG, [f.reshape(weights[n].shape) for n, f in zip(BIG, full)]))
    for n in names:
        if n not in grads:
            grads[n] = reduced[n].reshape(weights[n].shape)

    delta, new_m, new_v = {}, {}, {}
    for n in names:
        delta[n], new_m[n], new_v[n] = _adamw("adamw_" + n, weights[n], grads[n], mom1[n], mom2[n])

    loss = reduced["loss"].reshape(())
    return (loss, grad_x.reshape(x_shape), *[grads[n] for n in names], *[delta[n] for n in names],
            *[new_m[n] for n in names], *[new_v[n] for n in names])
```

```python
import functools
import math

import numpy as np
import jax
import jax.numpy as jnp
from jax import lax
from jax.experimental import pallas as pl
from jax.experimental.pallas import tpu as pltpu

F32 = jnp.float32
BF16 = jnp.bfloat16

S = 2048
D = 1024
DEPTH = 2
HD = 64
BLK = 128
NQB = S // BLK
A_GROUPS = ((128, 1), (512, 4), (2048, 16))
N_BAND_Q = 20
N_A = 12
NUM_BUCKETS = 32
MAX_DISTANCE = 2048
D_FF = 4096
IN_COLS = 6912
IN_SHARD = IN_COLS // 4
OFF_GATE = 3840
EPS = 1e-6
SCALE = HD ** -0.5
NEG = -1e30
N_CHIPS = 4
N_DEV = 8

ADAM_LR = 0.001
ADAM_B1 = 0.9
ADAM_B2 = 0.999
ADAM_EPS = 1e-08
ADAM_WD = 0.01
ADAM_STEP = 10

VMEM_LIMIT = 56 * 1024 * 1024

NN = (((1,), (0,)), ((), ()))
NT = (((1,), (1,)), ((), ()))
TN = (((0,), (0,)), ((), ()))

MESH = pl.DeviceIdType.MESH
ANY = pl.BlockSpec(memory_space=pl.ANY)


def _dot(a, b, dims):
    return lax.dot_general(a, b, dims, preferred_element_type=F32)


def _params(sem):
    return pltpu.CompilerParams(dimension_semantics=sem, vmem_limit_bytes=VMEM_LIMIT)


def _matmul(name, a, b, out_shape, out_dtype, grid, a_spec, b_spec, o_spec, dims, acc_shape):
    nk = grid[-1]

    def body(a_ref, b_ref, o_ref, *scratch):
        part = _dot(a_ref[...].astype(BF16), b_ref[...].astype(BF16), dims)
        if nk == 1:
            o_ref[...] = part.astype(o_ref.dtype)
            return
        acc_ref, = scratch
        k = pl.program_id(len(grid) - 1)

        @pl.when(k == 0)
        def _():
            acc_ref[...] = part

        @pl.when(k > 0)
        def _():
            acc_ref[...] += part

        @pl.when(k == nk - 1)
        def _():
            o_ref[...] = acc_ref[...].astype(o_ref.dtype)

    scratch = [] if nk == 1 else [pltpu.VMEM(acc_shape, F32)]
    sem = ("parallel",) * (len(grid) - 1) + ("arbitrary",)
    return pl.pallas_call(
        body, name=name, grid=grid, in_specs=[a_spec, b_spec], out_specs=o_spec,
        out_shape=jax.ShapeDtypeStruct(out_shape, out_dtype), scratch_shapes=scratch,
        compiler_params=_params(sem))(a, b)


FULL_K = 8192


def _mm_tn_sharded(name, a, b, row_sharded, tm=512, tn=512, tk=FULL_K):
    k, m = a.shape
    n = b.shape[1]
    m4, n4 = (m // N_CHIPS, n) if row_sharded else (m, n // N_CHIPS)
    tm, tn, tk = min(tm, m4), min(tn, n4), min(tk, k)
    per_m, per_n = m4 // tm, n4 // tn
    if row_sharded:
        o_map = lambda i, j, l: (i // per_m, i % per_m, j)
    else:
        o_map = lambda i, j, l: (j // per_n, i, j % per_n)
    return _matmul(name, a, b, (N_CHIPS, m4, n4), BF16, (m // tm, n // tn, k // tk),
                   pl.BlockSpec((tk, tm), lambda i, j, l: (l, i)),
                   pl.BlockSpec((tk, tn), lambda i, j, l: (l, j)),
                   pl.BlockSpec((None, tm, tn), o_map), TN, (tm, tn))


def _mm_nn(name, a, b, out_dtype, tm=512, tn=512, tk=FULL_K):
    m, k = a.shape
    n = b.shape[1]
    tm, tn, tk = min(tm, m), min(tn, n), min(tk, k)
    return _matmul(name, a, b, (m, n), out_dtype, (m // tm, n // tn, k // tk),
                   pl.BlockSpec((tm, tk), lambda i, j, l: (i, l)),
                   pl.BlockSpec((tk, tn), lambda i, j, l: (l, j)),
                   pl.BlockSpec((tm, tn), lambda i, j, l: (i, j)), NN, (tm, tn))


def _mm_nt(name, a, b, out_dtype, tm=512, tn=512, tk=FULL_K):
    m, k = a.shape
    n = b.shape[0]
    tm, tn, tk = min(tm, m), min(tn, n), min(tk, k)
    return _matmul(name, a, b, (m, n), out_dtype, (m // tm, n // tn, k // tk),
                   pl.BlockSpec((tm, tk), lambda i, j, l: (i, l)),
                   pl.BlockSpec((tn, tk), lambda i, j, l: (j, l)),
                   pl.BlockSpec((tm, tn), lambda i, j, l: (i, j)), NT, (tm, tn))


def _proj_in_dw(h, dproj_s):
    tm = 512
    return _matmul("proj_in_dw", h, dproj_s, (N_CHIPS, D, IN_SHARD), BF16, (N_CHIPS, D // tm, 1),
                   pl.BlockSpec((S, tm), lambda j, i, l: (0, i)),
                   pl.BlockSpec((None, S, IN_SHARD), lambda j, i, l: (j, 0, 0)),
                   pl.BlockSpec((None, tm, IN_SHARD), lambda j, i, l: (j, i, 0)), TN, (tm, IN_SHARD))


TR = 256


def _row_spec(width=D):
    return pl.BlockSpec((TR, width), lambda i: (i, 0))


def _vec_spec(width=D):
    return pl.BlockSpec((1, width), lambda i: (0, 0))


def _rms(x, g):
    r = lax.rsqrt(jnp.mean(x * x, axis=-1, keepdims=True) + EPS)
    return x * r * g


def _rms_fwd(name, x, g):
    def body(x_ref, g_ref, h_ref):
        h_ref[...] = _rms(x_ref[...], g_ref[...]).astype(BF16)

    return pl.pallas_call(
        body, name=name, grid=(S // TR,), in_specs=[_row_spec(), _vec_spec()], out_specs=_row_spec(),
        out_shape=jax.ShapeDtypeStruct((S, D), BF16), compiler_params=_params(("parallel",)))(x, g)


def _post_pre_fwd(name, x, y, g_post, g_pre):
    has_pre = g_pre is not None

    def body(*refs):
        if has_pre:
            x_ref, y_ref, gp_ref, gn_ref, xn_ref, h_ref = refs
        else:
            x_ref, y_ref, gp_ref, xn_ref = refs
        xn = x_ref[...] + _rms(y_ref[...], gp_ref[...])
        xn_ref[...] = xn
        if has_pre:
            h_ref[...] = _rms(xn, gn_ref[...]).astype(BF16)

    ins = [x, y, g_post] + ([g_pre] if has_pre else [])
    in_specs = [_row_spec(), _row_spec(), _vec_spec()] + ([_vec_spec()] if has_pre else [])
    out_shape = [jax.ShapeDtypeStruct((S, D), F32)] + ([jax.ShapeDtypeStruct((S, D), BF16)] if has_pre else [])
    out_specs = [_row_spec()] + ([_row_spec()] if has_pre else [])
    out = pl.pallas_call(
        body, name=name, grid=(S // TR,), in_specs=in_specs, out_specs=out_specs, out_shape=out_shape,
        compiler_params=_params(("parallel",)))(*ins)
    return out if has_pre else (out[0], None)


def _rms_bwd_math(x, g, dy):
    r = lax.rsqrt(jnp.mean(x * x, axis=-1, keepdims=True) + EPS)
    n = x * r
    dn = dy * g
    dx = r * (dn - n * jnp.mean(dn * n, axis=-1, keepdims=True))
    return dx, jnp.sum(dy * n, axis=0, keepdims=True)


def _norm_bwd(name, dres, pre=None, post=None):
    has_pre, has_post = pre is not None, post is not None

    def body(*refs):
        refs = list(refs)
        dres_ref = refs.pop(0)
        if has_pre:
            xn_ref, gn_ref, dh_ref = refs[:3]
            refs = refs[3:]
        if has_post:
            y_ref, gp_ref = refs[:2]
            refs = refs[2:]
        dxn_ref = refs.pop(0)
        dy_ref = refs.pop(0) if has_post else None
        dgn_ref = refs.pop(0) if has_pre else None
        dgp_ref = refs.pop(0) if has_post else None
        first = pl.program_id(0) == 0
        dxn = dres_ref[...]
        if has_pre:
            dx, dg = _rms_bwd_math(xn_ref[...], gn_ref[...], dh_ref[...])
            dxn = dxn + dx

            @pl.when(first)
            def _():
                dgn_ref[...] = dg

            @pl.when(jnp.logical_not(first))
            def _():
                dgn_ref[...] += dg
        dxn_ref[...] = dxn
        if has_post:
            dy, dg = _rms_bwd_math(y_ref[...], gp_ref[...], dxn)
            dy_ref[...] = dy.astype(BF16)

            @pl.when(first)
            def _():
                dgp_ref[...] = dg

            @pl.when(jnp.logical_not(first))
            def _():
                dgp_ref[...] += dg

    ins, in_specs = [dres], [_row_spec()]
    if has_pre:
        ins += list(pre)
        in_specs += [_row_spec(), _vec_spec(), _row_spec()]
    if has_post:
        ins += list(post)
        in_specs += [_row_spec(), _vec_spec()]
    out_shape, out_specs = [jax.ShapeDtypeStruct((S, D), F32)], [_row_spec()]
    if has_post:
        out_shape.append(jax.ShapeDtypeStruct((S, D), BF16))
        out_specs.append(_row_spec())
    for _ in range(int(has_pre) + int(has_post)):
        out_shape.append(jax.ShapeDtypeStruct((1, D), F32))
        out_specs.append(_vec_spec())
    out = list(pl.pallas_call(
        body, name=name, grid=(S // TR,), in_specs=in_specs, out_specs=out_specs, out_shape=out_shape,
        compiler_params=_params(("arbitrary",)))(*ins))
    dxn = out.pop(0)
    dy = out.pop(0) if has_post else None
    dgn = out.pop(0) if has_pre else None
    dgp = out.pop(0) if has_post else None
    return dxn, dy, dgn, dgp


def _loss_kernel(y, target):
    def body(y_ref, t_ref, loss_ref, dy_ref):
        e = y_ref[...] - t_ref[...]
        dy_ref[...] = e * (1.0 / D)
        part = jnp.zeros((1, 128), F32) + 0.5 * jnp.sum(jnp.mean(e * e, axis=-1, keepdims=True))

        @pl.when(pl.program_id(0) == 0)
        def _():
            loss_ref[...] = part

        @pl.when(pl.program_id(0) > 0)
        def _():
            loss_ref[...] += part

    return pl.pallas_call(
        body, name="loss", grid=(S // TR,), in_specs=[_row_spec(), _row_spec()],
        out_specs=[_vec_spec(128), _row_spec()],
        out_shape=[jax.ShapeDtypeStruct((1, 128), F32), jax.ShapeDtypeStruct((S, D), F32)],
        compiler_params=_params(("arbitrary",)))(y, target)


def _t5_bucket_np(dist):
    max_exact = NUM_BUCKETS // 2
    nf = np.maximum(dist, 1).astype(np.float32)
    large = max_exact + (np.log(nf / max_exact) / np.float32(math.log(MAX_DISTANCE / max_exact))
                         * (NUM_BUCKETS - max_exact)).astype(np.int32)
    large = np.minimum(large, NUM_BUCKETS - 1)
    return np.where(dist < max_exact, dist, large).astype(np.int32)


def _bucket_maps():
    a = np.arange(BLK)[:, None]
    b = np.arange(2 * BLK)[None, :]
    dist = np.maximum(a + BLK - b, 0)
    maps = [_t5_bucket_np(dist * d) for _, d in A_GROUPS] + [_t5_bucket_np(dist)]
    return np.stack(maps).astype(np.int32)


def _classes(arr, d):
    return arr.reshape(S // d, d * arr.shape[1])


def _band_spec(arr, col0, prev):
    ncol = arr.shape[1] // 128
    if prev:
        return pl.BlockSpec((BLK, 128), lambda p, r, b: (jnp.maximum(b - 1, 0), r * ncol + col0 + p))
    return pl.BlockSpec((BLK, 128), lambda p, r, b: (b, r * ncol + col0 + p))


def _band_bias(tab_ref, bidx_ref, h):
    bi = bidx_ref[...]
    bias = jnp.zeros((BLK, 2 * BLK), F32)
    for kk in range(NUM_BUCKETS):
        bias = jnp.where(bi == kk, tab_ref[kk, h], bias)
    return bias


def _lane_lo():
    return lax.broadcasted_iota(jnp.int32, (BLK, 128), 1) < HD


def _per_head(x, lo):
    return (jnp.sum(jnp.where(lo, x, 0.0), axis=1, keepdims=True) * (1.0 / HD),
            jnp.sum(jnp.where(lo, 0.0, x), axis=1, keepdims=True) * (1.0 / HD))


def _band_mask(b, maxd):
    a = lax.broadcasted_iota(jnp.int32, (2 * BLK, 2 * BLK), 0) & (BLK - 1)
    c = lax.broadcasted_iota(jnp.int32, (2 * BLK, 2 * BLK), 1)
    dist = a + BLK - c
    return jnp.logical_and(jnp.logical_and(dist >= 0, dist <= maxd), jnp.logical_or(c >= BLK, b > 0))


def _stack_heads(x, lo, dtype=BF16):
    return jnp.concatenate([jnp.where(lo, x, 0.0), jnp.where(lo, 0.0, x)], axis=0).astype(dtype)


def _unstack_heads(x, lo):
    return jnp.where(lo, x[:BLK], x[BLK:])


def _stack_rows(prev_ref, cur_ref):
    return jnp.concatenate([prev_ref[...], cur_ref[...]], axis=0).astype(BF16)


def _band_fwd(name, d, n_pairs, maxd, head0, srcs, bidx_g, tab, sinks):
    nb = S // d // BLK
    (qa, qc), (ka, kc), (va, vc) = srcs
    out_spec = pl.BlockSpec((BLK, 128), lambda p, r, b: (b, r * n_pairs + p))
    smem = pl.BlockSpec(memory_space=pltpu.SMEM)
    full = pl.BlockSpec((BLK, 2 * BLK), lambda p, r, b: (0, 0))

    def body(tab_ref, sink_ref, q_ref, kp_ref, kc_ref, vp_ref, vc_ref, bidx_ref, o_ref, lse_ref, bias_ref):
        p, r, b = pl.program_id(0), pl.program_id(1), pl.program_id(2)

        @pl.when(jnp.logical_and(r == 0, b == 0))
        def _():
            for h in range(2):
                bias_ref[h * BLK:(h + 1) * BLK, :] = _band_bias(tab_ref, bidx_ref, head0 + 2 * p + h)

        lo = _lane_lo()
        qs = _stack_heads(q_ref[...] * SCALE, lo)
        ks, vs = _stack_rows(kp_ref, kc_ref), _stack_rows(vp_ref, vc_ref)
        s = jnp.where(_band_mask(b, maxd), _dot(qs, ks, NT) + bias_ref[...], NEG)
        m = jnp.max(s, axis=1, keepdims=True)
        pr = jnp.exp(s - m)
        l = jnp.sum(pr, axis=1, keepdims=True)
        num = _dot(pr.astype(BF16), vs, NN)
        lse = m + jnp.log(l)
        sink = jnp.where(lax.broadcasted_iota(jnp.int32, (2 * BLK, 1), 0) < BLK, sink_ref[2 * p], sink_ref[2 * p + 1])
        sig = 1.0 / (1.0 + jnp.exp(sink - lse))
        o_ref[...] = _unstack_heads(num * (sig / l), lo)
        lse_ref[...] = _unstack_heads(lse + jnp.zeros((2 * BLK, 128), F32), lo)

    shape = jax.ShapeDtypeStruct((S // d, d * n_pairs * 128), F32)
    o, lse = pl.pallas_call(
        body, name=name, grid=(n_pairs, d, nb),
        in_specs=[smem, smem, _band_spec(qa, qc, False), _band_spec(ka, kc, True), _band_spec(ka, kc, False),
                  _band_spec(va, vc, True), _band_spec(va, vc, False), full],
        out_specs=[out_spec, out_spec], out_shape=[shape, shape],
        scratch_shapes=[pltpu.VMEM((2 * BLK, 2 * BLK), F32)],
        compiler_params=_params(("parallel", "arbitrary", "arbitrary")))(
            tab, sinks, _classes(qa, d), _classes(ka, d), _classes(ka, d), _classes(va, d), _classes(va, d), bidx_g)
    return o.reshape(S, n_pairs * 128), lse.reshape(S, n_pairs * 128)


def _band_bwd(name, d, n_pairs, maxd, head0, srcs, bidx_g, tab, sinks, o, lse, do, stats_in):
    nb = S // d // BLK
    rows = S // d
    (qa, qc), (ka, kc), (va, vc) = srcs
    blk_spec = pl.BlockSpec((BLK, 128), lambda p, r, b: (b, r * n_pairs + p))
    cls_spec = pl.BlockSpec((rows, 128), lambda p, r, b: (0, r * n_pairs + p))
    smem = pl.BlockSpec(memory_space=pltpu.SMEM)
    full = pl.BlockSpec((BLK, 2 * BLK), lambda p, r, b: (0, 0))
    stat_spec = pl.BlockSpec((2, 8, 128), lambda p, r, b: (p, 0, 0))

    def body(tab_ref, sink_ref, q_ref, kp_ref, kc_ref, vp_ref, vc_ref, bidx_ref, o_ref, lse_ref, do_ref, sin_ref,
             dq_ref, dk_ref, dv_ref, stat_ref, bias_ref, dsacc_ref, sk_ref):
        p, r, b = pl.program_id(0), pl.program_id(1), pl.program_id(2)

        @pl.when(jnp.logical_and(r == 0, b == 0))
        def _():
            for h in range(2):
                bias_ref[h * BLK:(h + 1) * BLK, :] = _band_bias(tab_ref, bidx_ref, head0 + 2 * p + h)
            dsacc_ref[...] = jnp.zeros_like(dsacc_ref)
            sk_ref[...] = jnp.zeros_like(sk_ref)

        @pl.when(b == 0)
        def _():
            dk_ref[...] = jnp.zeros_like(dk_ref)
            dv_ref[...] = jnp.zeros_like(dv_ref)

        lo = _lane_lo()
        qs = _stack_heads(q_ref[...] * SCALE, lo)
        ks, vs = _stack_rows(kp_ref, kc_ref), _stack_rows(vp_ref, vc_ref)
        do = do_ref[...]
        dos = _stack_heads(do, lo, F32)
        lse = jnp.concatenate(_per_head(lse_ref[...], lo), axis=0)
        prod = do * o_ref[...]
        delta = jnp.concatenate([jnp.sum(jnp.where(lo, prod, 0.0), axis=1, keepdims=True),
                                 jnp.sum(jnp.where(lo, 0.0, prod), axis=1, keepdims=True)], axis=0)
        head1 = lax.broadcasted_iota(jnp.int32, (2 * BLK, 1), 0) >= BLK
        sig = 1.0 / (1.0 + jnp.exp(jnp.where(head1, sink_ref[2 * p + 1], sink_ref[2 * p]) - lse))
        s = _dot(qs, ks, NT) + bias_ref[...]
        pr = jnp.where(_band_mask(b, maxd), jnp.exp(s - lse), 0.0)
        ds = pr * (sig * (_dot(dos.astype(BF16), vs, NT) - delta))
        dsb = ds.astype(BF16)
        dq_ref[...] = SCALE * _unstack_heads(_dot(dsb, ks, NN), lo)
        dk = _dot(dsb, qs, TN)
        dv = _dot(pr.astype(BF16), (sig * dos).astype(BF16), TN)
        cur = pl.ds(pl.multiple_of(b * BLK, BLK), BLK)
        prev = pl.ds(pl.multiple_of(jnp.maximum(b - 1, 0) * BLK, BLK), BLK)
        dk_ref[prev, :] += dk[:BLK]
        dk_ref[cur, :] += dk[BLK:]
        dv_ref[prev, :] += dv[:BLK]
        dv_ref[cur, :] += dv[BLK:]
        dsacc_ref[...] += ds
        sink_grad = -delta * (1.0 - sig)
        for h in range(2):
            sk_ref[h] += jnp.zeros((8, 128), F32) + jnp.sum(sink_grad[h * BLK:(h + 1) * BLK])

        @pl.when(jnp.logical_and(r == d - 1, b == nb - 1))
        def _():
            bi = bidx_ref[...]
            lane = lax.broadcasted_iota(jnp.int32, (8, 128), 1)
            sub = lax.broadcasted_iota(jnp.int32, (8, 128), 0)
            for h in range(2):
                acc = dsacc_ref[h * BLK:(h + 1) * BLK, :]
                row = jnp.where(jnp.logical_and(sub == 1, lane == 0), sk_ref[h], 0.0)
                for kk in range(NUM_BUCKETS):
                    tot = jnp.sum(jnp.where(bi == kk, acc, 0.0))
                    row = jnp.where(jnp.logical_and(sub == 0, lane == kk), tot, row)
                stat_ref[h] = row + jnp.where(sub == 0, sin_ref[h], 0.0)

    shape = jax.ShapeDtypeStruct((rows, d * n_pairs * 128), F32)
    dq, dk, dv, stats = pl.pallas_call(
        body, name=name, grid=(n_pairs, d, nb),
        in_specs=[smem, smem, _band_spec(qa, qc, False), _band_spec(ka, kc, True), _band_spec(ka, kc, False),
                  _band_spec(va, vc, True), _band_spec(va, vc, False), full, blk_spec, blk_spec, blk_spec, stat_spec],
        out_specs=[blk_spec, cls_spec, cls_spec, stat_spec],
        out_shape=[shape, shape, shape, jax.ShapeDtypeStruct((2 * n_pairs, 8, 128), F32)],
        scratch_shapes=[pltpu.VMEM((2 * BLK, 2 * BLK), F32), pltpu.VMEM((2 * BLK, 2 * BLK), F32),
                        pltpu.VMEM((2, 8, 128), F32)],
        compiler_params=_params(("arbitrary", "arbitrary", "arbitrary")))(
            tab, sinks, _classes(qa, d), _classes(ka, d), _classes(ka, d), _classes(va, d), _classes(va, d), bidx_g,
            _classes(o, d), _classes(lse, d), _classes(do, d), stats_in)
    width = n_pairs * 128
    return dq.reshape(S, width), dk.reshape(S, width), dv.reshape(S, width), stats


def _comb_fwd(o_g, lse_g):
    def body(o0, o1, o2, l0, l1, l2, out_ref, outb_ref, lse_ref):
        a0, a1, a2 = l0[...], l1[...], l2[...]
        m = jnp.maximum(jnp.maximum(a0, a1), a2)
        e0, e1, e2 = jnp.exp(a0 - m), jnp.exp(a1 - m), jnp.exp(a2 - m)
        tot = e0 + e1 + e2
        out = (e0 * o0[...] + e1 * o1[...] + e2 * o2[...]) / tot
        out_ref[...] = out
        outb_ref[...] = out.astype(BF16)
        lse_ref[...] = m + jnp.log(tot)

    spec = _row_spec(4 * HD)
    f32 = jax.ShapeDtypeStruct((S, 4 * HD), F32)
    return pl.pallas_call(
        body, name="comb_fwd", grid=(S // TR,), in_specs=[spec] * 6, out_specs=[spec] * 3,
        out_shape=[f32, jax.ShapeDtypeStruct((S, 4 * HD), BF16), f32],
        compiler_params=_params(("parallel",)))(*o_g, *lse_g)


def _split2(x):
    hi = x.astype(BF16)
    return hi, (x - hi.astype(F32)).astype(BF16)


KB = 2 * BLK


def _tri_sum(x, tri):
    hi, lo = _split2(x)
    both = _dot(jnp.concatenate([hi, lo], axis=0), tri, NN)
    return both[:x.shape[0]] + both[x.shape[0]:]


def _tri(strict_upper):
    r = lax.broadcasted_iota(jnp.int32, (KB, KB), 0)
    c = lax.broadcasted_iota(jnp.int32, (KB, KB), 1)
    return jnp.where(r > c if strict_upper else r < c, 1.0, 0.0).astype(BF16)


def _sb_terms(qs, kj, before):
    z = _dot(qs, kj, NT)
    lsp = jnp.minimum(z, 0.0) - jnp.log(1.0 + jnp.exp(-jnp.abs(z)))
    return lsp, jnp.where(before, lsp - z, 0.0)


def _sb_before(i, m):
    t = (lax.broadcasted_iota(jnp.int32, (2 * BLK, KB), 0) & (BLK - 1)) + i * BLK
    s = lax.broadcasted_iota(jnp.int32, (2 * BLK, KB), 1) + m * KB
    return s < t


C_COL = 3072 // 128


def _sb_fwd(proj):
    blk = lambda off: pl.BlockSpec((BLK, 128), lambda p, i: (i, off + p))
    col = lambda off: pl.BlockSpec((S, 128), lambda p, i: (0, off + p))
    out = pl.BlockSpec((BLK, 128), lambda p, i: (i, p))

    def body(q_ref, k_ref, v_ref, o_ref, ob_ref, tot_ref):
        i = pl.program_id(1)
        lo = _lane_lo()
        qs = _stack_heads(q_ref[...] * SCALE, lo)
        suffix = _tri(True)

        def step(n, carry):
            acc, rest = carry
            m = i // 2 - n
            rows = pl.ds(pl.multiple_of(m * KB, KB), KB)
            kj, vj = k_ref[rows, :].astype(BF16), v_ref[rows, :].astype(BF16)
            before = _sb_before(i, m)
            lsp, lk = _sb_terms(qs, kj, before)
            w = jnp.where(before, jnp.exp(lsp + _tri_sum(lk, suffix) + rest), 0.0)
            return acc + _dot(w.astype(BF16), vj, NN), rest + jnp.sum(lk, axis=1, keepdims=True)

        acc, rest = lax.fori_loop(0, i // 2 + 1, step, (jnp.zeros((2 * BLK, 128), F32), jnp.zeros((2 * BLK, 1), F32)))
        o = _unstack_heads(acc, lo)
        o_ref[...] = o
        ob_ref[...] = o.astype(BF16)
        tot_ref[...] = _unstack_heads(rest + jnp.zeros((2 * BLK, 128), F32), lo)

    f32 = jax.ShapeDtypeStruct((S, 4 * HD), F32)
    return pl.pallas_call(
        body, name="sb_fwd", grid=(2, NQB), in_specs=[blk(C_COL), col(C_COL + 2), col(C_COL + 4)],
        out_specs=[out, out, out], out_shape=[f32, jax.ShapeDtypeStruct((S, 4 * HD), BF16), f32],
        compiler_params=_params(("parallel", "arbitrary")))(proj, proj, proj)


def _sb_bwd(proj, tot, do):
    blk = lambda off: pl.BlockSpec((BLK, 128), lambda p, i: (i, off + p))
    col = lambda off: pl.BlockSpec((S, 128), lambda p, i: (0, off + p))

    def body(q_ref, k_ref, v_ref, tot_ref, do_ref, dq_ref, dk_ref, dv_ref):
        i = pl.program_id(1)

        @pl.when(i == 0)
        def _():
            dk_ref[...] = jnp.zeros_like(dk_ref)
            dv_ref[...] = jnp.zeros_like(dv_ref)

        lo = _lane_lo()
        qs = _stack_heads(q_ref[...] * SCALE, lo)
        dos = _stack_heads(do_ref[...], lo)
        tots = jnp.concatenate(_per_head(tot_ref[...], lo), axis=0)
        prefix = _tri(False)

        def step(m, carry):
            dq, keep_left, g_left = carry
            rows = pl.ds(pl.multiple_of(m * KB, KB), KB)
            kj, vj = k_ref[rows, :].astype(BF16), v_ref[rows, :].astype(BF16)
            before = _sb_before(i, m)
            lsp, lk = _sb_terms(qs, kj, before)
            log_rest = tots - keep_left - lk - _tri_sum(lk, prefix)
            w = jnp.where(before, jnp.exp(lsp + log_rest), 0.0)
            g = w * _dot(dos, vj, NT)
            g_before = g_left + _dot(g.astype(BF16), prefix, NN)
            beta = jnp.exp(lsp)
            dz = jnp.where(before, g * (1.0 - beta) - g_before * beta, 0.0).astype(BF16)
            dk_ref[rows, :] += _dot(dz, qs, TN)
            dv_ref[rows, :] += _dot(w.astype(BF16), dos, TN)
            return (dq + _dot(dz, kj, NN), keep_left + jnp.sum(lk, axis=1, keepdims=True),
                    g_left + jnp.sum(g, axis=1, keepdims=True))

        zero = (jnp.zeros((2 * BLK, 128), F32), jnp.zeros((2 * BLK, 1), F32), jnp.zeros((2 * BLK, 1), F32))
        dq, _, _ = lax.fori_loop(0, i // 2 + 1, step, zero)
        dq_ref[...] = SCALE * _unstack_heads(dq, lo)

    out_blk = pl.BlockSpec((BLK, 128), lambda p, i: (i, p))
    out_col = pl.BlockSpec((S, 128), lambda p, i: (0, p))
    f32 = jax.ShapeDtypeStruct((S, 4 * HD), F32)
    return pl.pallas_call(
        body, name="sb_bwd", grid=(2, NQB),
        in_specs=[blk(C_COL), col(C_COL + 2), col(C_COL + 4), out_blk, out_blk],
        out_specs=[out_blk, out_col, out_col], out_shape=[f32, f32, f32],
        compiler_params=_params(("arbitrary", "arbitrary")))(proj, proj, proj, tot, do)


TG = 256
GATE_BLK0 = OFF_GATE // TG


def _gate_specs():
    grid = (D // TG, S // TG)
    p_specs = [pl.BlockSpec((TG, TG), functools.partial(lambda c, r, br: (r, GATE_BLK0 + br * (D // TG) + c), br=br))
               for br in range(3)]
    b_spec = pl.BlockSpec((3, TG), lambda c, r: (0, c))
    t_spec = pl.BlockSpec((TG, TG), lambda c, r: (r, c))
    return grid, p_specs, b_spec, t_spec


def _sigmoid(x):
    return 1.0 / (1.0 + jnp.exp(-x))


def _three_rows(rows):
    sub = lax.broadcasted_iota(jnp.int32, (3, rows[0].shape[1]), 0)
    return jnp.where(sub == 0, rows[0], jnp.where(sub == 1, rows[1], rows[2]))


def _gate_fwd(proj, b_gate, br):
    grid, p_specs, b_spec, t_spec = _gate_specs()

    def body(p0, p1, p2, b_ref, r0, r1, r2, out_ref):
        acc = jnp.zeros((TG, TG), F32)
        for n, (p, r) in enumerate(((p0, r0), (p1, r1), (p2, r2))):
            acc += _sigmoid(p[...] + b_ref[n:n + 1, :]) * r[...]
        out_ref[...] = acc.astype(BF16)

    return pl.pallas_call(
        body, name="gate_fwd", grid=grid, in_specs=p_specs + [b_spec] + [t_spec] * 3, out_specs=t_spec,
        out_shape=jax.ShapeDtypeStruct((S, D), BF16),
        compiler_params=_params(("parallel", "parallel")))(proj, proj, proj, b_gate, *br)


def _gate_bwd(proj, b_gate, br, dmerged):
    grid, p_specs, b_spec, t_spec = _gate_specs()

    def body(p0, p1, p2, b_ref, r0, r1, r2, dm_ref, e0, e1, e2, g0, g1, g2, db_ref):
        dm = dm_ref[...]
        rows = []
        for n, (p, r, e_ref, dg_ref) in enumerate(((p0, r0, e0, g0), (p1, r1, e1, g1), (p2, r2, e2, g2))):
            g = _sigmoid(p[...] + b_ref[n:n + 1, :])
            e_ref[...] = (dm * g).astype(BF16)
            dpre = dm * r[...] * g * (1.0 - g)
            dg_ref[...] = dpre.astype(BF16)
            rows.append(jnp.sum(dpre, axis=0, keepdims=True))
        db = _three_rows(rows)

        @pl.when(pl.program_id(1) == 0)
        def _():
            db_ref[...] = db

        @pl.when(pl.program_id(1) > 0)
        def _():
            db_ref[...] += db

    bf = jax.ShapeDtypeStruct((S, D), BF16)
    out = pl.pallas_call(
        body, name="gate_bwd", grid=grid, in_specs=p_specs + [b_spec] + [t_spec] * 4,
        out_specs=[t_spec] * 6 + [b_spec], out_shape=[bf] * 6 + [jax.ShapeDtypeStruct((3, D), F32)],
        compiler_params=_params(("parallel", "arbitrary")))(proj, proj, proj, b_gate, *br, dmerged)
    return out[:3], out[3:6], out[6]


TC = 256
N_FF_BLK = D_FF // TC
GELU_C = math.sqrt(2.0 / math.pi)


def _shift_down(x, n):
    rows = lax.broadcasted_iota(jnp.int32, x.shape, 0)
    return jnp.where(rows >= n, pltpu.roll(x, n, axis=0), 0.0)


def _shift_up(x, n):
    rows = lax.broadcasted_iota(jnp.int32, x.shape, 0)
    return jnp.where(rows < x.shape[0] - n, pltpu.roll(x, x.shape[0] - n, axis=0), 0.0)


def _conv(u, w, b):
    return w[2:3, :] * u + w[1:2, :] * _shift_down(u, 1) + w[0:1, :] * _shift_down(u, 2) + b


def _gelu_parts(x):
    inner = GELU_C * (x + 0.044715 * x * x * x)
    t = jnp.tanh(inner)
    gelu = 0.5 * x * (1.0 + t)
    dgelu = 0.5 * (1.0 + t) + 0.5 * x * (1.0 - t * t) * GELU_C * (1.0 + 3 * 0.044715 * x * x)
    return gelu, dgelu


def _conv_specs():
    ug = pl.BlockSpec((S, TC), lambda c: (0, c))
    uv = pl.BlockSpec((S, TC), lambda c: (0, N_FF_BLK + c))
    wg = pl.BlockSpec((3, TC), lambda c: (0, c))
    wv = pl.BlockSpec((3, TC), lambda c: (0, N_FF_BLK + c))
    bg = pl.BlockSpec((1, TC), lambda c: (0, c))
    bv = pl.BlockSpec((1, TC), lambda c: (0, N_FF_BLK + c))
    return ug, uv, wg, wv, bg, bv


def _conv_fwd(u, conv_w, conv_b):
    ug, uv, wg, wv, bg, bv = _conv_specs()

    def body(ug_ref, uv_ref, wg_ref, wv_ref, bg_ref, bv_ref, a_ref):
        gc = _conv(ug_ref[...], wg_ref[...], bg_ref[...])
        vc = _conv(uv_ref[...], wv_ref[...], bv_ref[...])
        a_ref[...] = (_gelu_parts(gc)[0] * vc).astype(BF16)

    return pl.pallas_call(
        body, name="conv_fwd", grid=(N_FF_BLK,), in_specs=[ug, uv, wg, wv, bg, bv], out_specs=ug,
        out_shape=jax.ShapeDtypeStruct((S, D_FF), BF16),
        compiler_params=_params(("parallel",)))(u, u, conv_w, conv_w, conv_b, conv_b)


def _conv_bwd(u, conv_w, conv_b, da):
    ug, uv, wg, wv, bg, bv = _conv_specs()

    def back(duc, u, w):
        du = w[2:3, :] * duc + w[1:2, :] * _shift_up(duc, 1) + w[0:1, :] * _shift_up(duc, 2)
        dw = _three_rows([jnp.sum(duc * _shift_down(u, 2), axis=0, keepdims=True),
                          jnp.sum(duc * _shift_down(u, 1), axis=0, keepdims=True),
                          jnp.sum(duc * u, axis=0, keepdims=True)])
        return du, dw, jnp.sum(duc, axis=0, keepdims=True)

    def body(ug_ref, uv_ref, wg_ref, wv_ref, bg_ref, bv_ref, da_ref, dug_ref, duv_ref, dwg_ref, dwv_ref, dbg_ref, dbv_ref):
        u_g, u_v = ug_ref[...], uv_ref[...]
        gc = _conv(u_g, wg_ref[...], bg_ref[...])
        vc = _conv(u_v, wv_ref[...], bv_ref[...])
        gelu, dgelu = _gelu_parts(gc)
        da = da_ref[...]
        du, dw, db = back(da * vc * dgelu, u_g, wg_ref[...])
        dug_ref[...] = du.astype(BF16)
        dwg_ref[...] = dw
        dbg_ref[...] = db
        du, dw, db = back(da * gelu, u_v, wv_ref[...])
        duv_ref[...] = du.astype(BF16)
        dwv_ref[...] = dw
        dbv_ref[...] = db

    return pl.pallas_call(
        body, name="conv_bwd", grid=(N_FF_BLK,), in_specs=[ug, uv, wg, wv, bg, bv, ug],
        out_specs=[ug, ug, wg, wg, bg, bg],
        out_shape=[jax.ShapeDtypeStruct((S, D_FF), BF16), jax.ShapeDtypeStruct((S, D_FF), BF16),
                   jax.ShapeDtypeStruct((3, D_FF), F32), jax.ShapeDtypeStruct((3, D_FF), F32),
                   jax.ShapeDtypeStruct((1, D_FF), F32), jax.ShapeDtypeStruct((1, D_FF), F32)],
        compiler_params=_params(("parallel",)))(u, u, conv_w, conv_w, conv_b, conv_b, da)


def _adamw(name, w, g, m, v):
    shape = w.shape
    cols = shape[-1]
    flat = [t.reshape(-1, cols) for t in (w, g, m, v)]
    r = flat[0].shape[0]
    tr = min(128, r)

    def body(w_ref, g_ref, m_ref, v_ref, d_ref, mo_ref, vo_ref):
        g = g_ref[...]
        m = ADAM_B1 * m_ref[...] + (1.0 - ADAM_B1) * g
        v = ADAM_B2 * v_ref[...] + (1.0 - ADAM_B2) * (g * g)
        m_hat = m / (1.0 - ADAM_B1 ** ADAM_STEP)
        v_hat = v / (1.0 - ADAM_B2 ** ADAM_STEP)
        d_ref[...] = -ADAM_LR * (m_hat / (jnp.sqrt(v_hat) + ADAM_EPS) + ADAM_WD * w_ref[...])
        mo_ref[...] = m
        vo_ref[...] = v

    spec = pl.BlockSpec((tr, cols), lambda i: (i, 0))
    outs = pl.pallas_call(
        body, name=name, grid=(pl.cdiv(r, tr),), in_specs=[spec] * 4, out_specs=[spec] * 3,
        out_shape=[jax.ShapeDtypeStruct((r, cols), F32)] * 3, compiler_params=_params(("parallel",)))(*flat)
    return [t.reshape(shape) for t in outs]


def _place():
    x, y, c = lax.axis_index("x"), lax.axis_index("y"), lax.axis_index("c")
    chips = [(1 - x, y), (x, 1 - y), (1 - x, 1 - y)]
    return x, y, c, chips


def _scalars(*vals):
    return jnp.stack([jnp.asarray(v, jnp.int32) for v in vals])


HBM = pl.BlockSpec(memory_space=pltpu.HBM)
SEM = pl.BlockSpec(memory_space=pltpu.SEMAPHORE)
SPLIT_COPY = pltpu.CompilerParams(has_side_effects=pltpu.SideEffectType.DATAFLOW_SIDE_EFFECTING)


def _in_hbm(x):
    return pltpu.with_memory_space_constraint(x, pltpu.HBM)


def _cast_into_slot(name, w, layer, chip):
    _, k, n4 = w.shape
    tr = min(256, k)

    def body(chip_ref, w_ref, o_ref):
        o_ref[...] = w_ref[...].astype(BF16)

    return pl.pallas_call(
        body, name=name,
        grid_spec=pltpu.PrefetchScalarGridSpec(
            num_scalar_prefetch=1, grid=(k // tr,),
            in_specs=[pl.BlockSpec((None, tr, n4), lambda i, s: (layer, i, 0))],
            out_specs=pl.BlockSpec((None, tr, n4), lambda i, s: (s[0], i, 0))),
        out_shape=jax.ShapeDtypeStruct((N_CHIPS, k, n4), BF16),
        compiler_params=_params(("parallel",)))(_scalars(chip), w)


def _gather_copy(buf_ref, k, from_chip, send_sem, recv_sem, chips, c):
    rows = buf_ref.at[from_chip]
    return pltpu.make_async_remote_copy(src_ref=rows, dst_ref=rows, send_sem=send_sem, recv_sem=recv_sem,
                                        device_id=(*chips[k], c), device_id_type=MESH)


def _gather_start(bufs, groups):
    n, ng = len(bufs), len(groups)
    where = {a: (gi, e) for gi, g in enumerate(groups) for e, a in enumerate(g)}

    def body(*refs):
        ins, sems, token = refs[:n], refs[n:n + 2 * ng], refs[-1]
        x, y, c, chips = _place()
        for a in range(n):
            gi, e = where[a]
            for k in range(3):
                _gather_copy(ins[a], k, 2 * x + y, sems[2 * gi].at[3 * e + k], sems[2 * gi + 1].at[3 * e + k],
                             chips, c).start()
        token[...] = jnp.zeros_like(token)

    out_shape = [pltpu.SemaphoreType.DMA((3 * len(g),)) for g in groups for _ in range(2)]
    out_shape += [pltpu.HBM(b.shape, b.dtype) for b in bufs] + [jax.ShapeDtypeStruct((8, 128), F32)]
    out = pl.pallas_call(
        body, name="gather_start", in_specs=[HBM] * n,
        out_specs=[SEM] * (2 * ng) + [HBM] * n + [pl.BlockSpec(memory_space=pltpu.VMEM)], out_shape=out_shape,
        input_output_aliases={a: 2 * ng + a for a in range(n)}, compiler_params=SPLIT_COPY)(*[_in_hbm(b) for b in bufs])
    sems = [(out[2 * gi], out[2 * gi + 1]) for gi in range(ng)]
    return sems, list(out[2 * ng:2 * ng + n]), out[-1]


def _gather_wait(name, bufs, send, recv, after):
    n = len(bufs)

    def body(*refs):
        ins, send_sem, recv_sem = refs[:n], refs[n], refs[n + 1]
        x, y, c, chips = _place()
        for e in range(n):
            for k in range(3):
                sems = (send_sem.at[3 * e + k], recv_sem.at[3 * e + k])
                _gather_copy(ins[e], k, 2 * x + y, *sems, chips, c).wait_send()
                _gather_copy(ins[e], k, 2 * chips[k][0] + chips[k][1], *sems, chips, c).wait_recv()

    return pl.pallas_call(
        body, name=name, in_specs=[HBM] * n + [SEM, SEM, ANY], out_specs=[HBM] * n,
        out_shape=[pltpu.HBM(b.shape, b.dtype) for b in bufs],
        input_output_aliases={a: a for a in range(n)}, compiler_params=SPLIT_COPY)(*bufs, send, recv, after)


def _reduce_copy(g_ref, land_ref, mask, send_sem, recv_sem, x, y, c, sending):
    px, py, pc = x ^ ((mask >> 2) & 1), y ^ ((mask >> 1) & 1), c ^ (mask & 1)
    half = g_ref.shape[1] // 2
    src = g_ref.at[2 * px + py, pl.ds(pl.multiple_of(pc * half, half), half)]
    dst = land_ref.at[4 * x + 2 * y + c] if sending else land_ref.at[4 * px + 2 * py + pc]
    return pltpu.make_async_remote_copy(src_ref=src, dst_ref=dst, send_sem=send_sem, recv_sem=recv_sem,
                                        device_id=(px, py, pc), device_id_type=MESH)


def _reduce_start(name, grads):
    n = len(grads)
    lands = [lax.empty((N_DEV, g.shape[1] // 2, g.shape[2]), g.dtype) for g in grads]

    def body(*refs):
        gs, ls, send_sem, recv_sem = refs[:n], refs[n:2 * n], refs[2 * n], refs[2 * n + 1]
        x, y, c, _ = _place()
        for a in range(n):
            for mask in range(1, N_DEV):
                s = (N_DEV - 1) * a + mask - 1
                _reduce_copy(gs[a], ls[a], mask, send_sem.at[s], recv_sem.at[s], x, y, c, True).start()
        refs[-1][...] = jnp.zeros_like(refs[-1])

    sem = pltpu.SemaphoreType.DMA((n * (N_DEV - 1),))
    out = pl.pallas_call(
        body, name=name, in_specs=[HBM] * (2 * n),
        out_specs=[SEM, SEM] + [HBM] * (2 * n) + [pl.BlockSpec(memory_space=pltpu.VMEM)],
        out_shape=[sem, sem] + [pltpu.HBM(t.shape, t.dtype) for t in grads + lands] + [jax.ShapeDtypeStruct((8, 128), F32)],
        input_output_aliases={a: 2 + a for a in range(2 * n)}, compiler_params=SPLIT_COPY)(
            *[_in_hbm(t) for t in grads + lands])
    return out[0], out[1], list(out[2:2 + n]), list(out[2 + n:2 + 2 * n]), out[-1]


def _reduce_wait(name, send, recv, grads, lands, after):
    n = len(grads)

    def body(*refs):
        gs, ls, send_sem, recv_sem = refs[:n], refs[n:2 * n], refs[2 * n], refs[2 * n + 1]
        x, y, c, _ = _place()
        for a in range(n):
            for mask in range(1, N_DEV):
                s = (N_DEV - 1) * a + mask - 1
                sems = (send_sem.at[s], recv_sem.at[s])
                _reduce_copy(gs[a], ls[a], mask, *sems, x, y, c, True).wait_send()
                _reduce_copy(gs[a], ls[a], mask, *sems, x, y, c, False).wait_recv()

    out = pl.pallas_call(
        body, name=name, in_specs=[HBM] * (2 * n) + [SEM, SEM, ANY], out_specs=[HBM] * (2 * n),
        out_shape=[pltpu.HBM(t.shape, t.dtype) for t in grads + lands],
        input_output_aliases={a: a for a in range(2 * n)}, compiler_params=SPLIT_COPY)(*grads, *lands, send, recv, after)
    return list(out[:n]), list(out[n:])


def _reduce_sum(name, g, land, layer, into, chip, c):
    _, k4, n4 = g.shape
    half = k4 // 2
    tr = min(256, half)
    per = half // tr
    me = 2 * chip + c

    def body(s_ref, own_ref, *refs):
        total = own_ref[...].astype(F32)
        for ref in refs[:N_DEV - 1]:
            total = total + ref[...].astype(F32)
        refs[-1][...] = total

    in_specs = [pl.BlockSpec((None, tr, n4), lambda i, s: (s[0], s[1] * per + i, 0))]
    in_specs += [pl.BlockSpec((None, tr, n4), functools.partial(lambda i, s, m: (s[1 + m], i, 0), m=m))
                 for m in range(1, N_DEV)]
    ins = [g] + [land] * (N_DEV - 1)
    aliases = {}
    if into is not None:
        in_specs, ins, aliases = in_specs + [ANY], ins + [into], {1 + N_DEV: 0}
    return pl.pallas_call(
        body, name=name,
        grid_spec=pltpu.PrefetchScalarGridSpec(
            num_scalar_prefetch=1, grid=(per,), in_specs=in_specs,
            out_specs=pl.BlockSpec((None, tr, n4), lambda i, s: (layer, s[1] * per + i, 0))),
        out_shape=jax.ShapeDtypeStruct((DEPTH, k4, n4), F32), input_output_aliases=aliases,
        compiler_params=_params(("parallel",)))(_scalars(chip, c, *[me ^ m for m in range(1, N_DEV)]), *ins)


def _join_halves(bufs):
    n = len(bufs)

    def body(*refs):
        ins, outs = refs[:n], refs[n:2 * n]
        send_sem, recv_sem = refs[2 * n:]
        x, y, c, _ = _place()

        def rows(ref, which):
            half = ref.shape[1] // 2
            return ref.at[:, pl.ds(pl.multiple_of(which * half, half), half)]

        sends = [pltpu.make_async_remote_copy(
            src_ref=rows(ins[a], c), dst_ref=rows(outs[a], c), send_sem=send_sem.at[a], recv_sem=recv_sem.at[a],
            device_id=(x, y, 1 - c), device_id_type=MESH) for a in range(n)]
        for cp in sends:
            cp.start()
        for a in range(n):
            sends[a].wait_send()
            pltpu.make_async_remote_copy(
                src_ref=rows(ins[a], c), dst_ref=rows(outs[a], 1 - c), send_sem=send_sem.at[a], recv_sem=recv_sem.at[a],
                device_id=(x, y, 1 - c), device_id_type=MESH).wait_recv()

    return pl.pallas_call(
        body, name="join_halves", in_specs=[ANY] * n, out_specs=[ANY] * n,
        out_shape=[jax.ShapeDtypeStruct(b.shape, b.dtype) for b in bufs],
        input_output_aliases={a: a for a in range(n)},
        scratch_shapes=[pltpu.SemaphoreType.DMA((n,)), pltpu.SemaphoreType.DMA((n,))],
    )(*bufs)


def _all_reduce_small(block):
    r = block.shape[0]

    def body(x_ref, out_ref, slots, send_sem, recv_sem):
        x, y, c, _ = _place()
        me = 4 * x + 2 * y + c
        slots[me] = x_ref[...]
        sends = []
        for mask in range(1, N_DEV):
            fx, fy, fc = (mask >> 2) & 1, (mask >> 1) & 1, mask & 1
            peer = (x ^ fx, y ^ fy, c ^ fc)
            cp = pltpu.make_async_remote_copy(
                src_ref=x_ref, dst_ref=slots.at[me], send_sem=send_sem.at[mask - 1], recv_sem=recv_sem.at[mask - 1],
                device_id=peer, device_id_type=MESH)
            cp.start()
            sends.append(cp)
        for mask in range(1, N_DEV):
            src = me ^ mask
            pltpu.make_async_remote_copy(
                src_ref=x_ref, dst_ref=slots.at[src], send_sem=send_sem.at[mask - 1], recv_sem=recv_sem.at[mask - 1],
                device_id=(x, y, c), device_id_type=MESH).wait_recv()
        for cp in sends:
            cp.wait_send()
        total = slots[0]
        for d in range(1, N_DEV):
            total = total + slots[d]
        out_ref[...] = total

    vmem = pl.BlockSpec(memory_space=pltpu.VMEM)
    return pl.pallas_call(
        body, name="all_reduce_small", in_specs=[vmem], out_specs=vmem,
        out_shape=jax.ShapeDtypeStruct((r, 128), F32),
        scratch_shapes=[pltpu.VMEM((N_DEV, r, 128), F32), pltpu.SemaphoreType.DMA((N_DEV - 1,)),
                        pltpu.SemaphoreType.DMA((N_DEV - 1,))],
        compiler_params=pltpu.CompilerParams(vmem_limit_bytes=VMEM_LIMIT))(block)


B_Q_COL = 2304 // 128
B_K0, B_V0, B_END = 2816, 2944, 3072


def _full_cols(w_g):
    return w_g.transpose(1, 0, 2).reshape(w_g.shape[1], -1)


def _group_src(proj, g):
    if A_GROUPS[g][1] == 1:
        return ((proj, 2 * g), (proj, 6 + 2 * g), (proj, 12 + 2 * g))
    packed = jnp.concatenate([proj[:, t * 768 + g * 256:t * 768 + (g + 1) * 256] for t in range(3)], axis=1)
    return ((packed, 0), (packed, 2), (packed, 4))


def _kv_expand(kv):
    return jnp.broadcast_to(kv.reshape(S, 2, 1, HD), (S, 2, 4, HD)).reshape(S, 8 * HD)


def _kv_reduce(dkv):
    return dkv.reshape(S, 2, 4, HD).sum(axis=2).reshape(S, 2 * HD)


def _mixer_fwd(h1, wget, rel_bias, sinks_l, bidx):
    w = dict(wget(0, h1))
    proj = _mm_nn("proj_in", h1, w["w_in"], F32, tn=1152)
    no_sinks = jnp.full((4,), NEG, F32)
    srcs = [_group_src(proj, g) for g in range(3)]
    o_g, lse_g = [], []
    for g, (_, d) in enumerate(A_GROUPS):
        o, lse = _band_fwd("band_fwd_g%d" % g, d, 2, BLK, 4 * g, srcs[g], bidx[g], rel_bias, no_sinks)
        o_g.append(o)
        lse_g.append(lse)
    o_a32, o_a, lse_a = _comb_fwd(o_g, lse_g)
    src_b = ((proj, B_Q_COL), (_kv_expand(proj[:, B_K0:B_V0]), 0), (_kv_expand(proj[:, B_V0:B_END]), 0))
    o_b32, lse_b = _band_fwd("band_fwd_b", 1, 4, BLK - 1, N_A, src_b, bidx[3], rel_bias, sinks_l)
    o_b = o_b32.astype(BF16)
    o_c32, o_c, tot_c = _sb_fwd(proj)
    w.update(wget(1, o_c32))
    br = [_mm_nn("branch_a", o_a, w["w_br_a"], F32), _mm_nn("branch_b", o_b, w["w_br_b"], F32),
          _mm_nn("branch_c", o_c, w["w_br_c"], F32)]
    merged = _gate_fwd(proj, w["b_gate"], br)
    mo = _mm_nn("out_proj", merged, w["w_out"], F32)
    saved = dict(proj=proj, srcs=srcs, src_b=src_b, o_a32=o_a32, lse_a=lse_a, o_b32=o_b32, lse_b=lse_b, tot_c=tot_c,
                 o_a=o_a, o_b=o_b, o_c=o_c, br=br, merged=merged)
    return mo, saved, w


def _mixer_bwd(d_mo, h1, w, sv, rel_bias, sinks_l, bidx, stats_in):
    grads = {}
    dmerged = _mm_nt("out_proj_dx", d_mo, w["w_out"], F32)
    grads["w_out"] = _mm_tn_sharded("out_proj_dw", sv["merged"], d_mo, True)
    e, dgate, db_gate = _gate_bwd(sv["proj"], w["b_gate"], sv["br"], dmerged)
    grads["b_gate"] = db_gate
    d_o = {}
    for n, name in enumerate("abc"):
        d_o[name] = _mm_nt("branch_%s_dx" % name, e[n], w["w_br_" + name], F32)
        grads["w_br_" + name] = _mm_tn_sharded("branch_%s_dw" % name, sv["o_" + name], e[n], False)
    no_sinks = jnp.full((4,), NEG, F32)
    dqs, dks, dvs, stats = [], [], [], []
    for g, (_, d) in enumerate(A_GROUPS):
        dq, dk, dv, st = _band_bwd("band_bwd_g%d" % g, d, 2, BLK, 4 * g, sv["srcs"][g], bidx[g], rel_bias, no_sinks,
                                   sv["o_a32"], sv["lse_a"], d_o["a"], stats_in[4 * g:4 * g + 4])
        dqs.append(dq)
        dks.append(dk)
        dvs.append(dv)
        stats.append(st)
    dq_b, dk_x, dv_x, st = _band_bwd("band_bwd_b", 1, 4, BLK - 1, N_A, sv["src_b"], bidx[3], rel_bias, sinks_l,
                                     sv["o_b32"], sv["lse_b"], d_o["b"], stats_in[N_A:])
    stats = jnp.concatenate(stats + [st], axis=0)
    dcq, dck, dcv = _sb_bwd(sv["proj"], sv["tot_c"], d_o["c"])
    cols = dqs + dks + dvs + [dq_b, _kv_reduce(dk_x), _kv_reduce(dv_x), dcq, dck, dcv]
    dproj = jnp.concatenate([t.astype(BF16) for t in cols] + list(dgate), axis=1)
    dh1 = _mm_nt("proj_in_dx", dproj, w["w_in"], F32)
    dproj_s = dproj.reshape(S, N_CHIPS, IN_SHARD).transpose(1, 0, 2)
    grads["w_in"] = _proj_in_dw(h1, dproj_s)
    return dh1, grads, stats


def _ffn_fwd(h2, w):
    u = _mm_nn("ffn_up", h2, w["w_up"], F32, tn=1024)
    a = _conv_fwd(u, w["conv_w"], w["conv_b"])
    dn = _mm_nn("ffn_down", a, w["w_down"], F32)
    return dn, dict(u=u, a=a)


def _ffn_bwd(d_dn, h2, w, sv):
    grads = {}
    da = _mm_nt("ffn_down_dx", d_dn, w["w_down"], F32, tn=1024)
    grads["w_down"] = _mm_tn_sharded("ffn_down_dw", sv["a"], d_dn, True)
    dug, duv, dwg, dwv, dbg, dbv = _conv_bwd(sv["u"], w["conv_w"], w["conv_b"], da)
    du = jnp.concatenate([dug, duv], axis=1)
    grads["conv_w"] = jnp.concatenate([dwg, dwv], axis=1)
    grads["conv_b"] = jnp.concatenate([dbg, dbv], axis=1)
    dh2 = _mm_nt("ffn_up_dx", du, w["w_up"], F32)
    grads["w_up"] = _mm_tn_sharded("ffn_up_dw", h2, du, False, tn=1024)
    return dh2, grads


BIG = ("w_in", "w_br_a", "w_br_b", "w_br_c", "w_out", "w_up", "w_down")
WEIGHT_GROUPS = (("w_in", "b_gate"), ("w_br_a", "w_br_b", "w_br_c", "w_out"), ("w_up", "conv_w", "w_down"))
GRAD_GROUPS = (("w_down", "w_up"), ("w_out", "w_br_a", "w_br_b", "w_br_c", "w_in"))
SMALL_ROWS = (("rel_bias", 8), ("attn_pre_norm", 16), ("attn_post_norm", 16), ("ffn_pre_norm", 16), ("ffn_post_norm", 16),
              ("sinks", 8), ("conv_b", 128), ("b_gate", 48), ("conv_w", 384), ("loss", 8))


def _pack_small(vals):
    rows = []
    for name, n in SMALL_ROWS:
        flat = vals[name].reshape(-1).astype(F32)
        rows.append(jnp.pad(flat, (0, n * 128 - flat.shape[0])).reshape(n, 128))
    return jnp.concatenate(rows, axis=0)


def _unpack_small(block, shapes):
    out, row = {}, 0
    for name, n in SMALL_ROWS:
        size = int(np.prod(shapes[name]))
        out[name] = block[row:row + n].reshape(-1)[:size].reshape(shapes[name])
        row += n
    return out


def kernel(x, rel_bias, attn_pre_norm, w_in, b_gate, sinks, w_br_a, w_br_b, w_br_c, w_out, attn_post_norm, ffn_pre_norm, w_up, conv_w, conv_b, w_down, ffn_post_norm, loss_target, m_rel_bias, m_attn_pre_norm, m_w_in, m_b_gate, m_sinks, m_w_br_a, m_w_br_b, m_w_br_c, m_w_out, m_attn_post_norm, m_ffn_pre_norm, m_w_up, m_conv_w, m_conv_b, m_w_down, m_ffn_post_norm, v_rel_bias, v_attn_pre_norm, v_w_in, v_b_gate, v_sinks, v_w_br_a, v_w_br_b, v_w_br_c, v_w_out, v_attn_post_norm, v_ffn_pre_norm, v_w_up, v_conv_w, v_conv_b, v_w_down, v_ffn_post_norm):
    names = ("rel_bias", "attn_pre_norm", "w_in", "b_gate", "sinks", "w_br_a", "w_br_b", "w_br_c", "w_out",
             "attn_post_norm", "ffn_pre_norm", "w_up", "conv_w", "conv_b", "w_down", "ffn_post_norm")
    weights = dict(zip(names, (rel_bias, attn_pre_norm, w_in, b_gate, sinks, w_br_a, w_br_b, w_br_c, w_out,
                               attn_post_norm, ffn_pre_norm, w_up, conv_w, conv_b, w_down, ffn_post_norm)))
    mom1 = dict(zip(names, (m_rel_bias, m_attn_pre_norm, m_w_in, m_b_gate, m_sinks, m_w_br_a, m_w_br_b, m_w_br_c,
                            m_w_out, m_attn_post_norm, m_ffn_pre_norm, m_w_up, m_conv_w, m_conv_b, m_w_down,
                            m_ffn_post_norm)))
    mom2 = dict(zip(names, (v_rel_bias, v_attn_pre_norm, v_w_in, v_b_gate, v_sinks, v_w_br_a, v_w_br_b, v_w_br_c,
                            v_w_out, v_attn_post_norm, v_ffn_pre_norm, v_w_up, v_conv_w, v_conv_b, v_w_down,
                            v_ffn_post_norm)))

    chip = 2 * lax.axis_index("x") + lax.axis_index("y")
    core = lax.axis_index("c")

    keys = [(n, l) for l in range(DEPTH) for group in WEIGHT_GROUPS for n in group]
    bufs = []
    for n, l in keys:
        if n in BIG:
            bufs.append(_cast_into_slot("cast_" + n, weights[n], l, chip))
        else:
            shard = weights[n][l]
            bufs.append(lax.dynamic_update_slice(jnp.zeros((N_CHIPS,) + shard.shape, F32), shard[None],
                                                 (chip, jnp.int32(0), jnp.int32(0))))
    groups = [[keys.index((n, l)) for n in group] for l in range(DEPTH) for group in WEIGHT_GROUPS]
    sems, in_flight, _ = _gather_start(bufs, groups)

    def wget(l, gi, after):
        g = l * len(WEIGHT_GROUPS) + gi
        got = _gather_wait("gather_wait_%d_%d" % (l, gi), [in_flight[a] for a in groups[g]], *sems[g], after)
        out = {}
        for n, buf in zip(WEIGHT_GROUPS[gi], got):
            out[n] = buf.reshape(-1, buf.shape[-1]) if n in ("w_out", "w_down") else _full_cols(buf)
        if gi == len(WEIGHT_GROUPS) - 1:
            out["conv_b"] = conv_b[l:l + 1]
        return out

    pending = []

    def emit(l, gi, grads):
        group = GRAD_GROUPS[gi]
        *started, token = _reduce_start("reduce_start_%d_%d" % (l, gi), [grads[n] for n in group])
        pending.append((l, group) + tuple(started))
        return token[:1, :1]

    local = _local_step(x.reshape(S, D), loss_target.reshape(S, D), wget, emit, rel_bias, sinks, attn_pre_norm,
                        attn_post_norm, ffn_pre_norm, ffn_post_norm)
    return _reduce_and_update(x.shape, names, weights, mom1, mom2, chip, core, pending, *local)


def _local_step(xs, target, wget, emit, rel_bias, sinks, attn_pre_norm, attn_post_norm, ffn_pre_norm, ffn_post_norm):
    bidx = jnp.asarray(_bucket_maps())

    saved, layers = [], []
    h1 = _rms_fwd("pre_norm_first", xs, attn_pre_norm[0:1])
    x_in = xs
    for l in range(DEPTH):
        mo, sv_mix, w = _mixer_fwd(h1, functools.partial(wget, l), rel_bias, sinks[l], bidx)
        x_mid, h2 = _post_pre_fwd("post_attn_norm", x_in, mo, attn_post_norm[l:l + 1], ffn_pre_norm[l:l + 1])
        w.update(wget(l, 2, h2))
        dn, sv_ffn = _ffn_fwd(h2, w)
        g_next = attn_pre_norm[l + 1:l + 2] if l + 1 < DEPTH else None
        x_out, h1_next = _post_pre_fwd("post_ffn_norm" if l + 1 < DEPTH else "post_ffn_norm_last", x_mid, dn,
                                       ffn_post_norm[l:l + 1], g_next)
        saved.append(dict(x_in=x_in, h1=h1, mo=mo, x_mid=x_mid, h2=h2, dn=dn, mix=sv_mix, ffn=sv_ffn))
        layers.append(w)
        x_in, h1 = x_out, h1_next

    loss_row, dres = _loss_kernel(x_in, target)

    small = [None] * DEPTH
    stats = jnp.zeros((N_BAND_Q, 8, 128), F32)
    dh_next = None
    for l in reversed(range(DEPTH)):
        w, sv = layers[l], saved[l]
        if l + 1 < DEPTH:
            pre = (saved[l + 1]["x_in"], attn_pre_norm[l + 1:l + 2] + zero, dh_next)
            dres, d_dn, dg_pre_next, dg_fpost = _norm_bwd("post_ffn_norm_bwd", dres, pre,
                                                          (sv["dn"], ffn_post_norm[l:l + 1]))
            small[l + 1]["attn_pre_norm"] = dg_pre_next
        else:
            dres, d_dn, _, dg_fpost = _norm_bwd("post_ffn_norm_last_bwd", dres, None, (sv["dn"], ffn_post_norm[l:l + 1]))
        dh2, g_ffn = _ffn_bwd(d_dn, sv["h2"], w, sv["ffn"])
        zero = emit(l, 0, g_ffn)
        dres, d_mo, dg_fpre, dg_apost = _norm_bwd("post_attn_norm_bwd", dres,
                                                  (sv["x_mid"], ffn_pre_norm[l:l + 1] + zero, dh2),
                                                  (sv["mo"], attn_post_norm[l:l + 1]))
        dh_next, g_mix, stats = _mixer_bwd(d_mo, sv["h1"], w, sv["mix"], rel_bias, sinks[l], bidx, stats)
        zero = emit(l, 1, g_mix)
        small[l] = dict(ffn_post_norm=dg_fpost, ffn_pre_norm=dg_fpre, attn_post_norm=dg_apost,
                        sinks=stats[N_A:, 1, 0], conv_b=g_ffn["conv_b"], b_gate=g_mix["b_gate"], conv_w=g_ffn["conv_w"])
    grad_x, _, dg_pre0, _ = _norm_bwd("pre_norm_first_bwd", dres, (saved[0]["x_in"], attn_pre_norm[0:1] + zero, dh_next),
                                      None)
    small[0]["attn_pre_norm"] = dg_pre0
    return loss_row, grad_x, small, stats


def _reduce_and_update(x_shape, names, weights, mom1, mom2, chip, core, pending, loss_row, grad_x, small, stats):
    small_vals = {n: jnp.stack([small[l][n].reshape(weights[n].shape[1:]) for l in range(DEPTH)])
                  for n in ("attn_pre_norm", "attn_post_norm", "ffn_pre_norm", "ffn_post_norm", "conv_b", "sinks")}
    small_vals["b_gate"] = jnp.stack([small[l]["b_gate"] for l in range(DEPTH)])
    small_vals["conv_w"] = jnp.stack([small[l]["conv_w"] for l in range(DEPTH)])
    small_vals["rel_bias"] = stats[:, 0, :NUM_BUCKETS].T
    small_vals["loss"] = loss_row[0, :1]
    shapes = {n: v.shape for n, v in small_vals.items()}
    reduced = _unpack_small(_all_reduce_small(_pack_small(small_vals)), shapes)
    reduced["b_gate"] = lax.dynamic_slice_in_dim(reduced["b_gate"], chip * (D // N_CHIPS), D // N_CHIPS, axis=2)
    reduced["conv_w"] = lax.dynamic_slice_in_dim(reduced["conv_w"], chip * (2 * D_FF // N_CHIPS), 2 * D_FF // N_CHIPS, axis=2)

    summed = {}
    for l, group, send, recv, gs, lands in pending:
        gs, lands = _reduce_wait("reduce_wait_%d_%s" % (l, group[0]), send, recv, gs, lands, grad_x)
        for n, g, land in zip(group, gs, lands):
            summed[n] = _reduce_sum("reduce_sum_%d_%s" % (l, n), g, land, l, summed.get(n), chip, core)
    full = _join_halves([summed[n] for n in BIG])
    grads = dict(zip(BIG, [f.reshape(weights[n].shape) for n, f in zip(BIG, full)]))
    for n in names:
        if n not in grads:
            grads[n] = reduced[n].reshape(weights[n].shape)

    delta, new_m, new_v = {}, {}, {}
    for n in names:
        delta[n], new_m[n], new_v[n] = _adamw("adamw_" + n, weights[n], grads[n], mom1[n], mom2[n])

    loss = reduced["loss"].reshape(())
    return (loss, grad_x.reshape(x_shape), *[grads[n] for n in names], *[delta[n] for n in names],
            *[new_m[n] for n in names], *[new_v[n] for n in names])
```

```python
import functools
import math

import numpy as np
import jax
import jax.numpy as jnp
from jax import lax
from jax.experimental import pallas as pl
from jax.experimental.pallas import tpu as pltpu

F32 = jnp.float32
BF16 = jnp.bfloat16

S = 2048
D = 1024
DEPTH = 2
HD = 64
BLK = 128
NQB = S // BLK
A_GROUPS = ((128, 1), (512, 4), (2048, 16))
N_BAND_Q = 20
N_A = 12
NUM_BUCKETS = 32
MAX_DISTANCE = 2048
D_FF = 4096
IN_COLS = 6912
IN_SHARD = IN_COLS // 4
OFF_GATE = 3840
EPS = 1e-6
SCALE = HD ** -0.5
NEG = -1e30
N_CHIPS = 4
N_DEV = 8

ADAM_LR = 0.001
ADAM_B1 = 0.9
ADAM_B2 = 0.999
ADAM_EPS = 1e-08
ADAM_WD = 0.01
ADAM_STEP = 10

VMEM_LIMIT = 56 * 1024 * 1024

NN = (((1,), (0,)), ((), ()))
NT = (((1,), (1,)), ((), ()))
TN = (((0,), (0,)), ((), ()))

MESH = pl.DeviceIdType.MESH
ANY = pl.BlockSpec(memory_space=pl.ANY)


def _dot(a, b, dims):
    return lax.dot_general(a, b, dims, preferred_element_type=F32)


def _params(sem):
    return pltpu.CompilerParams(dimension_semantics=sem, vmem_limit_bytes=VMEM_LIMIT)


def _matmul(name, a, b, out_shape, out_dtype, grid, a_spec, b_spec, o_spec, dims, acc_shape):
    nk = grid[-1]

    def body(a_ref, b_ref, o_ref, *scratch):
        part = _dot(a_ref[...].astype(BF16), b_ref[...].astype(BF16), dims)
        if nk == 1:
            o_ref[...] = part.astype(o_ref.dtype)
            return
        acc_ref, = scratch
        k = pl.program_id(len(grid) - 1)

        @pl.when(k == 0)
        def _():
            acc_ref[...] = part

        @pl.when(k > 0)
        def _():
            acc_ref[...] += part

        @pl.when(k == nk - 1)
        def _():
            o_ref[...] = acc_ref[...].astype(o_ref.dtype)

    scratch = [] if nk == 1 else [pltpu.VMEM(acc_shape, F32)]
    sem = ("parallel",) * (len(grid) - 1) + ("arbitrary",)
    return pl.pallas_call(
        body, name=name, grid=grid, in_specs=[a_spec, b_spec], out_specs=o_spec,
        out_shape=jax.ShapeDtypeStruct(out_shape, out_dtype), scratch_shapes=scratch,
        compiler_params=_params(sem))(a, b)


FULL_K = 8192


def _mm_tn_sharded(name, a, b, row_sharded, tm=512, tn=512, tk=FULL_K):
    k, m = a.shape
    n = b.shape[1]
    m4, n4 = (m // N_CHIPS, n) if row_sharded else (m, n // N_CHIPS)
    tm, tn, tk = min(tm, m4), min(tn, n4), min(tk, k)
    per_m, per_n = m4 // tm, n4 // tn
    if row_sharded:
        o_map = lambda i, j, l: (i // per_m, i % per_m, j)
    else:
        o_map = lambda i, j, l: (j // per_n, i, j % per_n)
    return _matmul(name, a, b, (N_CHIPS, m4, n4), BF16, (m // tm, n // tn, k // tk),
                   pl.BlockSpec((tk, tm), lambda i, j, l: (l, i)),
                   pl.BlockSpec((tk, tn), lambda i, j, l: (l, j)),
                   pl.BlockSpec((None, tm, tn), o_map), TN, (tm, tn))


def _mm_nn(name, a, b, out_dtype, tm=512, tn=512, tk=FULL_K):
    m, k = a.shape
    n = b.shape[1]
    tm, tn, tk = min(tm, m), min(tn, n), min(tk, k)
    return _matmul(name, a, b, (m, n), out_dtype, (m // tm, n // tn, k // tk),
                   pl.BlockSpec((tm, tk), lambda i, j, l: (i, l)),
                   pl.BlockSpec((tk, tn), lambda i, j, l: (l, j)),
                   pl.BlockSpec((tm, tn), lambda i, j, l: (i, j)), NN, (tm, tn))


def _mm_nt(name, a, b, out_dtype, tm=512, tn=512, tk=FULL_K):
    m, k = a.shape
    n = b.shape[0]
    tm, tn, tk = min(tm, m), min(tn, n), min(tk, k)
    return _matmul(name, a, b, (m, n), out_dtype, (m // tm, n // tn, k // tk),
                   pl.BlockSpec((tm, tk), lambda i, j, l: (i, l)),
                   pl.BlockSpec((tn, tk), lambda i, j, l: (j, l)),
                   pl.BlockSpec((tm, tn), lambda i, j, l: (i, j)), NT, (tm, tn))


def _mm_tn(name, a, b, out_dtype, tm=512, tn=512, tk=FULL_K):
    k, m = a.shape
    n = b.shape[1]
    tm, tn, tk = min(tm, m), min(tn, n), min(tk, k)
    return _matmul(name, a, b, (m, n), out_dtype, (m // tm, n // tn, k // tk),
                   pl.BlockSpec((tk, tm), lambda i, j, l: (l, i)),
                   pl.BlockSpec((tk, tn), lambda i, j, l: (l, j)),
                   pl.BlockSpec((tm, tn), lambda i, j, l: (i, j)), TN, (tm, tn))


TR = 256


def _row_spec(width=D):
    return pl.BlockSpec((TR, width), lambda i: (i, 0))


def _vec_spec(width=D):
    return pl.BlockSpec((1, width), lambda i: (0, 0))


def _rms(x, g):
    r = lax.rsqrt(jnp.mean(x * x, axis=-1, keepdims=True) + EPS)
    return x * r * g


def _rms_fwd(name, x, g):
    def body(x_ref, g_ref, h_ref):
        h_ref[...] = _rms(x_ref[...], g_ref[...]).astype(BF16)

    return pl.pallas_call(
        body, name=name, grid=(S // TR,), in_specs=[_row_spec(), _vec_spec()], out_specs=_row_spec(),
        out_shape=jax.ShapeDtypeStruct((S, D), BF16), compiler_params=_params(("parallel",)))(x, g)


def _post_pre_fwd(name, x, y, g_post, g_pre):
    has_pre = g_pre is not None

    def body(*refs):
        if has_pre:
            x_ref, y_ref, gp_ref, gn_ref, xn_ref, h_ref = refs
        else:
            x_ref, y_ref, gp_ref, xn_ref = refs
        xn = x_ref[...] + _rms(y_ref[...], gp_ref[...])
        xn_ref[...] = xn
        if has_pre:
            h_ref[...] = _rms(xn, gn_ref[...]).astype(BF16)

    ins = [x, y, g_post] + ([g_pre] if has_pre else [])
    in_specs = [_row_spec(), _row_spec(), _vec_spec()] + ([_vec_spec()] if has_pre else [])
    out_shape = [jax.ShapeDtypeStruct((S, D), F32)] + ([jax.ShapeDtypeStruct((S, D), BF16)] if has_pre else [])
    out_specs = [_row_spec()] + ([_row_spec()] if has_pre else [])
    out = pl.pallas_call(
        body, name=name, grid=(S // TR,), in_specs=in_specs, out_specs=out_specs, out_shape=out_shape,
        compiler_params=_params(("parallel",)))(*ins)
    return out if has_pre else (out[0], None)


def _rms_bwd_math(x, g, dy):
    r = lax.rsqrt(jnp.mean(x * x, axis=-1, keepdims=True) + EPS)
    n = x * r
    dn = dy * g
    dx = r * (dn - n * jnp.mean(dn * n, axis=-1, keepdims=True))
    return dx, jnp.sum(dy * n, axis=0, keepdims=True)


def _norm_bwd(name, dres, pre=None, post=None):
    has_pre, has_post = pre is not None, post is not None

    def body(*refs):
        refs = list(refs)
        dres_ref = refs.pop(0)
        if has_pre:
            xn_ref, gn_ref, dh_ref = refs[:3]
            refs = refs[3:]
        if has_post:
            y_ref, gp_ref = refs[:2]
            refs = refs[2:]
        dxn_ref = refs.pop(0)
        dy_ref = refs.pop(0) if has_post else None
        dgn_ref = refs.pop(0) if has_pre else None
        dgp_ref = refs.pop(0) if has_post else None
        first = pl.program_id(0) == 0
        dxn = dres_ref[...]
        if has_pre:
            dx, dg = _rms_bwd_math(xn_ref[...], gn_ref[...], dh_ref[...])
            dxn = dxn + dx

            @pl.when(first)
            def _():
                dgn_ref[...] = dg

            @pl.when(jnp.logical_not(first))
            def _():
                dgn_ref[...] += dg
        dxn_ref[...] = dxn
        if has_post:
            dy, dg = _rms_bwd_math(y_ref[...], gp_ref[...], dxn)
            dy_ref[...] = dy.astype(BF16)

            @pl.when(first)
            def _():
                dgp_ref[...] = dg

            @pl.when(jnp.logical_not(first))
            def _():
                dgp_ref[...] += dg

    ins, in_specs = [dres], [_row_spec()]
    if has_pre:
        ins += list(pre)
        in_specs += [_row_spec(), _vec_spec(), _row_spec()]
    if has_post:
        ins += list(post)
        in_specs += [_row_spec(), _vec_spec()]
    out_shape, out_specs = [jax.ShapeDtypeStruct((S, D), F32)], [_row_spec()]
    if has_post:
        out_shape.append(jax.ShapeDtypeStruct((S, D), BF16))
        out_specs.append(_row_spec())
    for _ in range(int(has_pre) + int(has_post)):
        out_shape.append(jax.ShapeDtypeStruct((1, D), F32))
        out_specs.append(_vec_spec())
    out = list(pl.pallas_call(
        body, name=name, grid=(S // TR,), in_specs=in_specs, out_specs=out_specs, out_shape=out_shape,
        compiler_params=_params(("arbitrary",)))(*ins))
    dxn = out.pop(0)
    dy = out.pop(0) if has_post else None
    dgn = out.pop(0) if has_pre else None
    dgp = out.pop(0) if has_post else None
    return dxn, dy, dgn, dgp


def _loss_kernel(y, target):
    def body(y_ref, t_ref, loss_ref, dy_ref):
        e = y_ref[...] - t_ref[...]
        dy_ref[...] = e * (1.0 / D)
        part = jnp.zeros((1, 128), F32) + 0.5 * jnp.sum(jnp.mean(e * e, axis=-1, keepdims=True))

        @pl.when(pl.program_id(0) == 0)
        def _():
            loss_ref[...] = part

        @pl.when(pl.program_id(0) > 0)
        def _():
            loss_ref[...] += part

    return pl.pallas_call(
        body, name="loss", grid=(S // TR,), in_specs=[_row_spec(), _row_spec()],
        out_specs=[_vec_spec(128), _row_spec()],
        out_shape=[jax.ShapeDtypeStruct((1, 128), F32), jax.ShapeDtypeStruct((S, D), F32)],
        compiler_params=_params(("arbitrary",)))(y, target)


def _t5_bucket_np(dist):
    max_exact = NUM_BUCKETS // 2
    nf = np.maximum(dist, 1).astype(np.float32)
    large = max_exact + (np.log(nf / max_exact) / np.float32(math.log(MAX_DISTANCE / max_exact))
                         * (NUM_BUCKETS - max_exact)).astype(np.int32)
    large = np.minimum(large, NUM_BUCKETS - 1)
    return np.where(dist < max_exact, dist, large).astype(np.int32)


def _bucket_maps():
    a = np.arange(BLK)[:, None]
    b = np.arange(2 * BLK)[None, :]
    dist = np.maximum(a + BLK - b, 0)
    maps = [_t5_bucket_np(dist * d) for _, d in A_GROUPS] + [_t5_bucket_np(dist)]
    return np.stack(maps).astype(np.int32)


def _classes(arr, d):
    return arr.reshape(S // d, d * arr.shape[1])


def _band_spec(arr, col0, prev):
    ncol = arr.shape[1] // 128
    if prev:
        return pl.BlockSpec((BLK, 128), lambda p, r, b: (jnp.maximum(b - 1, 0), r * ncol + col0 + p))
    return pl.BlockSpec((BLK, 128), lambda p, r, b: (b, r * ncol + col0 + p))


def _band_bias(tab_ref, bidx_ref, h):
    bi = bidx_ref[...]
    bias = jnp.zeros((BLK, 2 * BLK), F32)
    for kk in range(NUM_BUCKETS):
        bias = jnp.where(bi == kk, tab_ref[kk, h], bias)
    return bias


def _lane_lo():
    return lax.broadcasted_iota(jnp.int32, (BLK, 128), 1) < HD


def _per_head(x, lo):
    return (jnp.sum(jnp.where(lo, x, 0.0), axis=1, keepdims=True) * (1.0 / HD),
            jnp.sum(jnp.where(lo, 0.0, x), axis=1, keepdims=True) * (1.0 / HD))


def _band_mask(b, maxd):
    a = lax.broadcasted_iota(jnp.int32, (2 * BLK, 2 * BLK), 0) & (BLK - 1)
    c = lax.broadcasted_iota(jnp.int32, (2 * BLK, 2 * BLK), 1)
    dist = a + BLK - c
    return jnp.logical_and(jnp.logical_and(dist >= 0, dist <= maxd), jnp.logical_or(c >= BLK, b > 0))


def _stack_heads(x, lo, dtype=BF16):
    return jnp.concatenate([jnp.where(lo, x, 0.0), jnp.where(lo, 0.0, x)], axis=0).astype(dtype)


def _unstack_heads(x, lo):
    return jnp.where(lo, x[:BLK], x[BLK:])


def _stack_rows(prev_ref, cur_ref):
    return jnp.concatenate([prev_ref[...], cur_ref[...]], axis=0).astype(BF16)


def _band_fwd(name, d, n_pairs, maxd, head0, srcs, bidx_g, tab, sinks):
    nb = S // d // BLK
    (qa, qc), (ka, kc), (va, vc) = srcs
    out_spec = pl.BlockSpec((BLK, 128), lambda p, r, b: (b, r * n_pairs + p))
    smem = pl.BlockSpec(memory_space=pltpu.SMEM)
    full = pl.BlockSpec((BLK, 2 * BLK), lambda p, r, b: (0, 0))

    def body(tab_ref, sink_ref, q_ref, kp_ref, kc_ref, vp_ref, vc_ref, bidx_ref, o_ref, lse_ref, bias_ref):
        p, r, b = pl.program_id(0), pl.program_id(1), pl.program_id(2)

        @pl.when(jnp.logical_and(r == 0, b == 0))
        def _():
            for h in range(2):
                bias_ref[h * BLK:(h + 1) * BLK, :] = _band_bias(tab_ref, bidx_ref, head0 + 2 * p + h)

        lo = _lane_lo()
        qs = _stack_heads(q_ref[...] * SCALE, lo)
        ks, vs = _stack_rows(kp_ref, kc_ref), _stack_rows(vp_ref, vc_ref)
        s = jnp.where(_band_mask(b, maxd), _dot(qs, ks, NT) + bias_ref[...], NEG)
        m = jnp.max(s, axis=1, keepdims=True)
        pr = jnp.exp(s - m)
        l = jnp.sum(pr, axis=1, keepdims=True)
        num = _dot(pr.astype(BF16), vs, NN)
        lse = m + jnp.log(l)
        sink = jnp.where(lax.broadcasted_iota(jnp.int32, (2 * BLK, 1), 0) < BLK, sink_ref[2 * p], sink_ref[2 * p + 1])
        sig = 1.0 / (1.0 + jnp.exp(sink - lse))
        o_ref[...] = _unstack_heads(num * (sig / l), lo)
        lse_ref[...] = _unstack_heads(lse + jnp.zeros((2 * BLK, 128), F32), lo)

    shape = jax.ShapeDtypeStruct((S // d, d * n_pairs * 128), F32)
    o, lse = pl.pallas_call(
        body, name=name, grid=(n_pairs, d, nb),
        in_specs=[smem, smem, _band_spec(qa, qc, False), _band_spec(ka, kc, True), _band_spec(ka, kc, False),
                  _band_spec(va, vc, True), _band_spec(va, vc, False), full],
        out_specs=[out_spec, out_spec], out_shape=[shape, shape],
        scratch_shapes=[pltpu.VMEM((2 * BLK, 2 * BLK), F32)],
        compiler_params=_params(("parallel", "arbitrary", "arbitrary")))(
            tab, sinks, _classes(qa, d), _classes(ka, d), _classes(ka, d), _classes(va, d), _classes(va, d), bidx_g)
    return o.reshape(S, n_pairs * 128), lse.reshape(S, n_pairs * 128)


def _band_bwd(name, d, n_pairs, maxd, head0, srcs, bidx_g, tab, sinks, o, lse, do, stats_in):
    nb = S // d // BLK
    rows = S // d
    (qa, qc), (ka, kc), (va, vc) = srcs
    blk_spec = pl.BlockSpec((BLK, 128), lambda p, r, b: (b, r * n_pairs + p))
    cls_spec = pl.BlockSpec((rows, 128), lambda p, r, b: (0, r * n_pairs + p))
    smem = pl.BlockSpec(memory_space=pltpu.SMEM)
    full = pl.BlockSpec((BLK, 2 * BLK), lambda p, r, b: (0, 0))
    stat_spec = pl.BlockSpec((2, 8, 128), lambda p, r, b: (p, 0, 0))

    def body(tab_ref, sink_ref, q_ref, kp_ref, kc_ref, vp_ref, vc_ref, bidx_ref, o_ref, lse_ref, do_ref, sin_ref,
             dq_ref, dk_ref, dv_ref, stat_ref, bias_ref, dsacc_ref, sk_ref):
        p, r, b = pl.program_id(0), pl.program_id(1), pl.program_id(2)

        @pl.when(jnp.logical_and(r == 0, b == 0))
        def _():
            for h in range(2):
                bias_ref[h * BLK:(h + 1) * BLK, :] = _band_bias(tab_ref, bidx_ref, head0 + 2 * p + h)
            dsacc_ref[...] = jnp.zeros_like(dsacc_ref)
            sk_ref[...] = jnp.zeros_like(sk_ref)

        @pl.when(b == 0)
        def _():
            dk_ref[...] = jnp.zeros_like(dk_ref)
            dv_ref[...] = jnp.zeros_like(dv_ref)

        lo = _lane_lo()
        qs = _stack_heads(q_ref[...] * SCALE, lo)
        ks, vs = _stack_rows(kp_ref, kc_ref), _stack_rows(vp_ref, vc_ref)
        do = do_ref[...]
        dos = _stack_heads(do, lo, F32)
        lse = jnp.concatenate(_per_head(lse_ref[...], lo), axis=0)
        prod = do * o_ref[...]
        delta = jnp.concatenate([jnp.sum(jnp.where(lo, prod, 0.0), axis=1, keepdims=True),
                                 jnp.sum(jnp.where(lo, 0.0, prod), axis=1, keepdims=True)], axis=0)
        head1 = lax.broadcasted_iota(jnp.int32, (2 * BLK, 1), 0) >= BLK
        sig = 1.0 / (1.0 + jnp.exp(jnp.where(head1, sink_ref[2 * p + 1], sink_ref[2 * p]) - lse))
        s = _dot(qs, ks, NT) + bias_ref[...]
        pr = jnp.where(_band_mask(b, maxd), jnp.exp(s - lse), 0.0)
        ds = pr * (sig * (_dot(dos.astype(BF16), vs, NT) - delta))
        dsb = ds.astype(BF16)
        dq_ref[...] = SCALE * _unstack_heads(_dot(dsb, ks, NN), lo)
        dk = _dot(dsb, qs, TN)
        dv = _dot(pr.astype(BF16), (sig * dos).astype(BF16), TN)
        cur = pl.ds(pl.multiple_of(b * BLK, BLK), BLK)
        prev = pl.ds(pl.multiple_of(jnp.maximum(b - 1, 0) * BLK, BLK), BLK)
        dk_ref[prev, :] += dk[:BLK]
        dk_ref[cur, :] += dk[BLK:]
        dv_ref[prev, :] += dv[:BLK]
        dv_ref[cur, :] += dv[BLK:]
        dsacc_ref[...] += ds
        sink_grad = -delta * (1.0 - sig)
        for h in range(2):
            sk_ref[h] += jnp.zeros((8, 128), F32) + jnp.sum(sink_grad[h * BLK:(h + 1) * BLK])

        @pl.when(jnp.logical_and(r == d - 1, b == nb - 1))
        def _():
            bi = bidx_ref[...]
            lane = lax.broadcasted_iota(jnp.int32, (8, 128), 1)
            sub = lax.broadcasted_iota(jnp.int32, (8, 128), 0)
            for h in range(2):
                acc = dsacc_ref[h * BLK:(h + 1) * BLK, :]
                row = jnp.where(jnp.logical_and(sub == 1, lane == 0), sk_ref[h], 0.0)
                for kk in range(NUM_BUCKETS):
                    tot = jnp.sum(jnp.where(bi == kk, acc, 0.0))
                    row = jnp.where(jnp.logical_and(sub == 0, lane == kk), tot, row)
                stat_ref[h] = row + jnp.where(sub == 0, sin_ref[h], 0.0)

    shape = jax.ShapeDtypeStruct((rows, d * n_pairs * 128), F32)
    dq, dk, dv, stats = pl.pallas_call(
        body, name=name, grid=(n_pairs, d, nb),
        in_specs=[smem, smem, _band_spec(qa, qc, False), _band_spec(ka, kc, True), _band_spec(ka, kc, False),
                  _band_spec(va, vc, True), _band_spec(va, vc, False), full, blk_spec, blk_spec, blk_spec, stat_spec],
        out_specs=[blk_spec, cls_spec, cls_spec, stat_spec],
        out_shape=[shape, shape, shape, jax.ShapeDtypeStruct((2 * n_pairs, 8, 128), F32)],
        scratch_shapes=[pltpu.VMEM((2 * BLK, 2 * BLK), F32), pltpu.VMEM((2 * BLK, 2 * BLK), F32),
                        pltpu.VMEM((2, 8, 128), F32)],
        compiler_params=_params(("arbitrary", "arbitrary", "arbitrary")))(
            tab, sinks, _classes(qa, d), _classes(ka, d), _classes(ka, d), _classes(va, d), _classes(va, d), bidx_g,
            _classes(o, d), _classes(lse, d), _classes(do, d), stats_in)
    width = n_pairs * 128
    return dq.reshape(S, width), dk.reshape(S, width), dv.reshape(S, width), stats


def _comb_fwd(o_g, lse_g):
    def body(o0, o1, o2, l0, l1, l2, out_ref, outb_ref, lse_ref):
        a0, a1, a2 = l0[...], l1[...], l2[...]
        m = jnp.maximum(jnp.maximum(a0, a1), a2)
        e0, e1, e2 = jnp.exp(a0 - m), jnp.exp(a1 - m), jnp.exp(a2 - m)
        tot = e0 + e1 + e2
        out = (e0 * o0[...] + e1 * o1[...] + e2 * o2[...]) / tot
        out_ref[...] = out
        outb_ref[...] = out.astype(BF16)
        lse_ref[...] = m + jnp.log(tot)

    spec = _row_spec(4 * HD)
    f32 = jax.ShapeDtypeStruct((S, 4 * HD), F32)
    return pl.pallas_call(
        body, name="comb_fwd", grid=(S // TR,), in_specs=[spec] * 6, out_specs=[spec] * 3,
        out_shape=[f32, jax.ShapeDtypeStruct((S, 4 * HD), BF16), f32],
        compiler_params=_params(("parallel",)))(*o_g, *lse_g)


def _split2(x):
    hi = x.astype(BF16)
    return hi, (x - hi.astype(F32)).astype(BF16)


KB = 2 * BLK


def _tri_sum(x, tri):
    hi, lo = _split2(x)
    both = _dot(jnp.concatenate([hi, lo], axis=0), tri, NN)
    return both[:x.shape[0]] + both[x.shape[0]:]


def _tri(strict_upper):
    r = lax.broadcasted_iota(jnp.int32, (KB, KB), 0)
    c = lax.broadcasted_iota(jnp.int32, (KB, KB), 1)
    return jnp.where(r > c if strict_upper else r < c, 1.0, 0.0).astype(BF16)


def _sb_terms(qs, kj, before):
    z = _dot(qs, kj, NT)
    lsp = jnp.minimum(z, 0.0) - jnp.log(1.0 + jnp.exp(-jnp.abs(z)))
    return lsp, jnp.where(before, lsp - z, 0.0)


def _sb_before(i, m):
    t = (lax.broadcasted_iota(jnp.int32, (2 * BLK, KB), 0) & (BLK - 1)) + i * BLK
    s = lax.broadcasted_iota(jnp.int32, (2 * BLK, KB), 1) + m * KB
    return s < t


C_COL = 3072 // 128


def _sb_fwd(proj):
    blk = lambda off: pl.BlockSpec((BLK, 128), lambda p, i: (i, off + p))
    col = lambda off: pl.BlockSpec((S, 128), lambda p, i: (0, off + p))
    out = pl.BlockSpec((BLK, 128), lambda p, i: (i, p))

    def body(q_ref, k_ref, v_ref, o_ref, ob_ref, tot_ref):
        i = pl.program_id(1)
        lo = _lane_lo()
        qs = _stack_heads(q_ref[...] * SCALE, lo)
        suffix = _tri(True)

        def step(n, carry):
            acc, rest = carry
            m = i // 2 - n
            rows = pl.ds(pl.multiple_of(m * KB, KB), KB)
            kj, vj = k_ref[rows, :].astype(BF16), v_ref[rows, :].astype(BF16)
            before = _sb_before(i, m)
            lsp, lk = _sb_terms(qs, kj, before)
            w = jnp.where(before, jnp.exp(lsp + _tri_sum(lk, suffix) + rest), 0.0)
            return acc + _dot(w.astype(BF16), vj, NN), rest + jnp.sum(lk, axis=1, keepdims=True)

        acc, rest = lax.fori_loop(0, i // 2 + 1, step, (jnp.zeros((2 * BLK, 128), F32), jnp.zeros((2 * BLK, 1), F32)))
        o = _unstack_heads(acc, lo)
        o_ref[...] = o
        ob_ref[...] = o.astype(BF16)
        tot_ref[...] = _unstack_heads(rest + jnp.zeros((2 * BLK, 128), F32), lo)

    f32 = jax.ShapeDtypeStruct((S, 4 * HD), F32)
    return pl.pallas_call(
        body, name="sb_fwd", grid=(2, NQB), in_specs=[blk(C_COL), col(C_COL + 2), col(C_COL + 4)],
        out_specs=[out, out, out], out_shape=[f32, jax.ShapeDtypeStruct((S, 4 * HD), BF16), f32],
        compiler_params=_params(("parallel", "arbitrary")))(proj, proj, proj)


def _sb_bwd(proj, tot, do):
    blk = lambda off: pl.BlockSpec((BLK, 128), lambda p, i: (i, off + p))
    col = lambda off: pl.BlockSpec((S, 128), lambda p, i: (0, off + p))

    def body(q_ref, k_ref, v_ref, tot_ref, do_ref, dq_ref, dk_ref, dv_ref):
        i = pl.program_id(1)

        @pl.when(i == 0)
        def _():
            dk_ref[...] = jnp.zeros_like(dk_ref)
            dv_ref[...] = jnp.zeros_like(dv_ref)

        lo = _lane_lo()
        qs = _stack_heads(q_ref[...] * SCALE, lo)
        dos = _stack_heads(do_ref[...], lo)
        tots = jnp.concatenate(_per_head(tot_ref[...], lo), axis=0)
        prefix = _tri(False)

        def step(m, carry):
            dq, keep_left, g_left = carry
            rows = pl.ds(pl.multiple_of(m * KB, KB), KB)
            kj, vj = k_ref[rows, :].astype(BF16), v_ref[rows, :].astype(BF16)
            before = _sb_before(i, m)
            lsp, lk = _sb_terms(qs, kj, before)
            log_rest = tots - keep_left - lk - _tri_sum(lk, prefix)
            w = jnp.where(before, jnp.exp(lsp + log_rest), 0.0)
            g = w * _dot(dos, vj, NT)
            g_before = g_left + _dot(g.astype(BF16), prefix, NN)
            beta = jnp.exp(lsp)
            dz = jnp.where(before, g * (1.0 - beta) - g_before * beta, 0.0).astype(BF16)
            dk_ref[rows, :] += _dot(dz, qs, TN)
            dv_ref[rows, :] += _dot(w.astype(BF16), dos, TN)
            return (dq + _dot(dz, kj, NN), keep_left + jnp.sum(lk, axis=1, keepdims=True),
                    g_left + jnp.sum(g, axis=1, keepdims=True))

        zero = (jnp.zeros((2 * BLK, 128), F32), jnp.zeros((2 * BLK, 1), F32), jnp.zeros((2 * BLK, 1), F32))
        dq, _, _ = lax.fori_loop(0, i // 2 + 1, step, zero)
        dq_ref[...] = SCALE * _unstack_heads(dq, lo)

    out_blk = pl.BlockSpec((BLK, 128), lambda p, i: (i, p))
    out_col = pl.BlockSpec((S, 128), lambda p, i: (0, p))
    f32 = jax.ShapeDtypeStruct((S, 4 * HD), F32)
    return pl.pallas_call(
        body, name="sb_bwd", grid=(2, NQB),
        in_specs=[blk(C_COL), col(C_COL + 2), col(C_COL + 4), out_blk, out_blk],
        out_specs=[out_blk, out_col, out_col], out_shape=[f32, f32, f32],
        compiler_params=_params(("arbitrary", "arbitrary")))(proj, proj, proj, tot, do)


TG = 256
GATE_BLK0 = OFF_GATE // TG


def _gate_specs():
    grid = (D // TG, S // TG)
    p_specs = [pl.BlockSpec((TG, TG), functools.partial(lambda c, r, br: (r, GATE_BLK0 + br * (D // TG) + c), br=br))
               for br in range(3)]
    b_spec = pl.BlockSpec((3, TG), lambda c, r: (0, c))
    t_spec = pl.BlockSpec((TG, TG), lambda c, r: (r, c))
    return grid, p_specs, b_spec, t_spec


def _sigmoid(x):
    return 1.0 / (1.0 + jnp.exp(-x))


def _three_rows(rows):
    sub = lax.broadcasted_iota(jnp.int32, (3, rows[0].shape[1]), 0)
    return jnp.where(sub == 0, rows[0], jnp.where(sub == 1, rows[1], rows[2]))


def _gate_fwd(proj, b_gate, br):
    grid, p_specs, b_spec, t_spec = _gate_specs()

    def body(p0, p1, p2, b_ref, r0, r1, r2, out_ref):
        acc = jnp.zeros((TG, TG), F32)
        for n, (p, r) in enumerate(((p0, r0), (p1, r1), (p2, r2))):
            acc += _sigmoid(p[...] + b_ref[n:n + 1, :]) * r[...]
        out_ref[...] = acc.astype(BF16)

    return pl.pallas_call(
        body, name="gate_fwd", grid=grid, in_specs=p_specs + [b_spec] + [t_spec] * 3, out_specs=t_spec,
        out_shape=jax.ShapeDtypeStruct((S, D), BF16),
        compiler_params=_params(("parallel", "parallel")))(proj, proj, proj, b_gate, *br)


def _gate_bwd(proj, b_gate, br, dmerged):
    grid, p_specs, b_spec, t_spec = _gate_specs()

    def body(p0, p1, p2, b_ref, r0, r1, r2, dm_ref, e0, e1, e2, g0, g1, g2, db_ref):
        dm = dm_ref[...]
        rows = []
        for n, (p, r, e_ref, dg_ref) in enumerate(((p0, r0, e0, g0), (p1, r1, e1, g1), (p2, r2, e2, g2))):
            g = _sigmoid(p[...] + b_ref[n:n + 1, :])
            e_ref[...] = (dm * g).astype(BF16)
            dpre = dm * r[...] * g * (1.0 - g)
            dg_ref[...] = dpre.astype(BF16)
            rows.append(jnp.sum(dpre, axis=0, keepdims=True))
        db = _three_rows(rows)

        @pl.when(pl.program_id(1) == 0)
        def _():
            db_ref[...] = db

        @pl.when(pl.program_id(1) > 0)
        def _():
            db_ref[...] += db

    bf = jax.ShapeDtypeStruct((S, D), BF16)
    out = pl.pallas_call(
        body, name="gate_bwd", grid=grid, in_specs=p_specs + [b_spec] + [t_spec] * 4,
        out_specs=[t_spec] * 6 + [b_spec], out_shape=[bf] * 6 + [jax.ShapeDtypeStruct((3, D), F32)],
        compiler_params=_params(("parallel", "arbitrary")))(proj, proj, proj, b_gate, *br, dmerged)
    return out[:3], out[3:6], out[6]


TC = 256
N_FF_BLK = D_FF // TC
GELU_C = math.sqrt(2.0 / math.pi)


def _shift_down(x, n):
    rows = lax.broadcasted_iota(jnp.int32, x.shape, 0)
    return jnp.where(rows >= n, pltpu.roll(x, n, axis=0), 0.0)


def _shift_up(x, n):
    rows = lax.broadcasted_iota(jnp.int32, x.shape, 0)
    return jnp.where(rows < x.shape[0] - n, pltpu.roll(x, x.shape[0] - n, axis=0), 0.0)


def _conv(u, w, b):
    return w[2:3, :] * u + w[1:2, :] * _shift_down(u, 1) + w[0:1, :] * _shift_down(u, 2) + b


def _gelu_parts(x):
    inner = GELU_C * (x + 0.044715 * x * x * x)
    t = jnp.tanh(inner)
    gelu = 0.5 * x * (1.0 + t)
    dgelu = 0.5 * (1.0 + t) + 0.5 * x * (1.0 - t * t) * GELU_C * (1.0 + 3 * 0.044715 * x * x)
    return gelu, dgelu


def _conv_specs():
    ug = pl.BlockSpec((S, TC), lambda c: (0, c))
    uv = pl.BlockSpec((S, TC), lambda c: (0, N_FF_BLK + c))
    wg = pl.BlockSpec((3, TC), lambda c: (0, c))
    wv = pl.BlockSpec((3, TC), lambda c: (0, N_FF_BLK + c))
    bg = pl.BlockSpec((1, TC), lambda c: (0, c))
    bv = pl.BlockSpec((1, TC), lambda c: (0, N_FF_BLK + c))
    return ug, uv, wg, wv, bg, bv


def _conv_fwd(u, conv_w, conv_b):
    ug, uv, wg, wv, bg, bv = _conv_specs()

    def body(ug_ref, uv_ref, wg_ref, wv_ref, bg_ref, bv_ref, a_ref):
        gc = _conv(ug_ref[...], wg_ref[...], bg_ref[...])
        vc = _conv(uv_ref[...], wv_ref[...], bv_ref[...])
        a_ref[...] = (_gelu_parts(gc)[0] * vc).astype(BF16)

    return pl.pallas_call(
        body, name="conv_fwd", grid=(N_FF_BLK,), in_specs=[ug, uv, wg, wv, bg, bv], out_specs=ug,
        out_shape=jax.ShapeDtypeStruct((S, D_FF), BF16),
        compiler_params=_params(("parallel",)))(u, u, conv_w, conv_w, conv_b, conv_b)


def _conv_bwd(u, conv_w, conv_b, da):
    ug, uv, wg, wv, bg, bv = _conv_specs()

    def back(duc, u, w):
        du = w[2:3, :] * duc + w[1:2, :] * _shift_up(duc, 1) + w[0:1, :] * _shift_up(duc, 2)
        dw = _three_rows([jnp.sum(duc * _shift_down(u, 2), axis=0, keepdims=True),
                          jnp.sum(duc * _shift_down(u, 1), axis=0, keepdims=True),
                          jnp.sum(duc * u, axis=0, keepdims=True)])
        return du, dw, jnp.sum(duc, axis=0, keepdims=True)

    def body(ug_ref, uv_ref, wg_ref, wv_ref, bg_ref, bv_ref, da_ref, dug_ref, duv_ref, dwg_ref, dwv_ref, dbg_ref, dbv_ref):
        u_g, u_v = ug_ref[...], uv_ref[...]
        gc = _conv(u_g, wg_ref[...], bg_ref[...])
        vc = _conv(u_v, wv_ref[...], bv_ref[...])
        gelu, dgelu = _gelu_parts(gc)
        da = da_ref[...]
        du, dw, db = back(da * vc * dgelu, u_g, wg_ref[...])
        dug_ref[...] = du.astype(BF16)
        dwg_ref[...] = dw
        dbg_ref[...] = db
        du, dw, db = back(da * gelu, u_v, wv_ref[...])
        duv_ref[...] = du.astype(BF16)
        dwv_ref[...] = dw
        dbv_ref[...] = db

    return pl.pallas_call(
        body, name="conv_bwd", grid=(N_FF_BLK,), in_specs=[ug, uv, wg, wv, bg, bv, ug],
        out_specs=[ug, ug, wg, wg, bg, bg],
        out_shape=[jax.ShapeDtypeStruct((S, D_FF), BF16), jax.ShapeDtypeStruct((S, D_FF), BF16),
                   jax.ShapeDtypeStruct((3, D_FF), F32), jax.ShapeDtypeStruct((3, D_FF), F32),
                   jax.ShapeDtypeStruct((1, D_FF), F32), jax.ShapeDtypeStruct((1, D_FF), F32)],
        compiler_params=_params(("parallel",)))(u, u, conv_w, conv_w, conv_b, conv_b, da)


def _adamw(name, w, g, m, v):
    shape = w.shape
    cols = shape[-1]
    flat = [t.reshape(-1, cols) for t in (w, g, m, v)]
    r = flat[0].shape[0]
    tr = min(128, r)

    def body(w_ref, g_ref, m_ref, v_ref, d_ref, mo_ref, vo_ref):
        g = g_ref[...]
        m = ADAM_B1 * m_ref[...] + (1.0 - ADAM_B1) * g
        v = ADAM_B2 * v_ref[...] + (1.0 - ADAM_B2) * (g * g)
        m_hat = m / (1.0 - ADAM_B1 ** ADAM_STEP)
        v_hat = v / (1.0 - ADAM_B2 ** ADAM_STEP)
        d_ref[...] = -ADAM_LR * (m_hat / (jnp.sqrt(v_hat) + ADAM_EPS) + ADAM_WD * w_ref[...])
        mo_ref[...] = m
        vo_ref[...] = v

    spec = pl.BlockSpec((tr, cols), lambda i: (i, 0))
    outs = pl.pallas_call(
        body, name=name, grid=(pl.cdiv(r, tr),), in_specs=[spec] * 4, out_specs=[spec] * 3,
        out_shape=[jax.ShapeDtypeStruct((r, cols), F32)] * 3, compiler_params=_params(("parallel",)))(*flat)
    return [t.reshape(shape) for t in outs]


def _place():
    x, y, c = lax.axis_index("x"), lax.axis_index("y"), lax.axis_index("c")
    chips = [(1 - x, y), (x, 1 - y), (1 - x, 1 - y)]
    return x, y, c, chips


def _scalars(*vals):
    return jnp.stack([jnp.asarray(v, jnp.int32) for v in vals])


HBM = pl.BlockSpec(memory_space=pltpu.HBM)
SEM = pl.BlockSpec(memory_space=pltpu.SEMAPHORE)
SPLIT_COPY = pltpu.CompilerParams(has_side_effects=pltpu.SideEffectType.DATAFLOW_SIDE_EFFECTING)


def _in_hbm(x):
    return pltpu.with_memory_space_constraint(x, pltpu.HBM)


def _cast_into_slot(name, w, layer, chip):
    _, k, n4 = w.shape
    tr = max(t for t in range(16, 257, 16) if k % t == 0)

    def body(chip_ref, w_ref, o_ref):
        o_ref[...] = w_ref[...].astype(BF16)

    return pl.pallas_call(
        body, name=name,
        grid_spec=pltpu.PrefetchScalarGridSpec(
            num_scalar_prefetch=1, grid=(k // tr,),
            in_specs=[pl.BlockSpec((None, tr, n4), lambda i, s: (layer, i, 0))],
            out_specs=pl.BlockSpec((None, tr, n4), lambda i, s: (s[0], i, 0))),
        out_shape=jax.ShapeDtypeStruct((N_CHIPS, k, n4), BF16),
        compiler_params=_params(("parallel",)))(_scalars(chip), w)


def _gather_copy(buf_ref, k, from_chip, send_sem, recv_sem, chips, c):
    rows = buf_ref.at[from_chip]
    return pltpu.make_async_remote_copy(src_ref=rows, dst_ref=rows, send_sem=send_sem, recv_sem=recv_sem,
                                        device_id=(*chips[k], c), device_id_type=MESH)


def _gather_start(bufs, groups):
    n, ng = len(bufs), len(groups)
    where = {a: (gi, e) for gi, g in enumerate(groups) for e, a in enumerate(g)}

    def body(*refs):
        ins, sems, token = refs[:n], refs[n:n + 2 * ng], refs[-1]
        x, y, c, chips = _place()
        for a in range(n):
            gi, e = where[a]
            for k in range(3):
                _gather_copy(ins[a], k, 2 * x + y, sems[2 * gi].at[3 * e + k], sems[2 * gi + 1].at[3 * e + k],
                             chips, c).start()
        token[...] = jnp.zeros_like(token)

    out_shape = [pltpu.SemaphoreType.DMA((3 * len(g),)) for g in groups for _ in range(2)]
    out_shape += [pltpu.HBM(b.shape, b.dtype) for b in bufs] + [jax.ShapeDtypeStruct((8, 128), F32)]
    out = pl.pallas_call(
        body, name="gather_start", in_specs=[HBM] * n,
        out_specs=[SEM] * (2 * ng) + [HBM] * n + [pl.BlockSpec(memory_space=pltpu.VMEM)], out_shape=out_shape,
        input_output_aliases={a: 2 * ng + a for a in range(n)}, compiler_params=SPLIT_COPY)(*[_in_hbm(b) for b in bufs])
    sems = [(out[2 * gi], out[2 * gi + 1]) for gi in range(ng)]
    return sems, list(out[2 * ng:2 * ng + n]), out[-1]


def _gather_wait(name, bufs, send, recv, after):
    n = len(bufs)

    def body(*refs):
        ins, send_sem, recv_sem = refs[:n], refs[n], refs[n + 1]
        x, y, c, chips = _place()
        for e in range(n):
            for k in range(3):
                sems = (send_sem.at[3 * e + k], recv_sem.at[3 * e + k])
                _gather_copy(ins[e], k, 2 * x + y, *sems, chips, c).wait_send()
                _gather_copy(ins[e], k, 2 * chips[k][0] + chips[k][1], *sems, chips, c).wait_recv()

    return pl.pallas_call(
        body, name=name, in_specs=[HBM] * n + [SEM, SEM, ANY], out_specs=[HBM] * n,
        out_shape=[pltpu.HBM(b.shape, b.dtype) for b in bufs],
        input_output_aliases={a: a for a in range(n)}, compiler_params=SPLIT_COPY)(*bufs, send, recv, after)


def _reduce_copy(g_ref, land_ref, mask, send_sem, recv_sem, x, y, c, sending):
    px, py, pc = x ^ ((mask >> 2) & 1), y ^ ((mask >> 1) & 1), c ^ (mask & 1)
    half = g_ref.shape[1] // 2
    src = g_ref.at[2 * px + py, pl.ds(pl.multiple_of(pc * half, half), half)]
    dst = land_ref.at[4 * x + 2 * y + c] if sending else land_ref.at[4 * px + 2 * py + pc]
    return pltpu.make_async_remote_copy(src_ref=src, dst_ref=dst, send_sem=send_sem, recv_sem=recv_sem,
                                        device_id=(px, py, pc), device_id_type=MESH)


def _reduce_start(name, grads):
    n = len(grads)
    lands = [lax.empty((N_DEV, g.shape[1] // 2, g.shape[2]), g.dtype) for g in grads]

    def body(*refs):
        gs, ls, send_sem, recv_sem = refs[:n], refs[n:2 * n], refs[2 * n], refs[2 * n + 1]
        x, y, c, _ = _place()
        for a in range(n):
            for mask in range(1, N_DEV):
                s = (N_DEV - 1) * a + mask - 1
                _reduce_copy(gs[a], ls[a], mask, send_sem.at[s], recv_sem.at[s], x, y, c, True).start()
        refs[-1][...] = jnp.zeros_like(refs[-1])

    sem = pltpu.SemaphoreType.DMA((n * (N_DEV - 1),))
    out = pl.pallas_call(
        body, name=name, in_specs=[HBM] * (2 * n),
        out_specs=[SEM, SEM] + [HBM] * (2 * n) + [pl.BlockSpec(memory_space=pltpu.VMEM)],
        out_shape=[sem, sem] + [pltpu.HBM(t.shape, t.dtype) for t in grads + lands] + [jax.ShapeDtypeStruct((8, 128), F32)],
        input_output_aliases={a: 2 + a for a in range(2 * n)}, compiler_params=SPLIT_COPY)(
            *[_in_hbm(t) for t in grads + lands])
    return out[0], out[1], list(out[2:2 + n]), list(out[2 + n:2 + 2 * n]), out[-1]


def _reduce_wait(name, send, recv, grads, lands, after):
    n = len(grads)

    def body(*refs):
        gs, ls, send_sem, recv_sem = refs[:n], refs[n:2 * n], refs[2 * n], refs[2 * n + 1]
        x, y, c, _ = _place()
        for a in range(n):
            for mask in range(1, N_DEV):
                s = (N_DEV - 1) * a + mask - 1
                sems = (send_sem.at[s], recv_sem.at[s])
                _reduce_copy(gs[a], ls[a], mask, *sems, x, y, c, True).wait_send()
                _reduce_copy(gs[a], ls[a], mask, *sems, x, y, c, False).wait_recv()

    out = pl.pallas_call(
        body, name=name, in_specs=[HBM] * (2 * n) + [SEM, SEM, ANY], out_specs=[HBM] * (2 * n),
        out_shape=[pltpu.HBM(t.shape, t.dtype) for t in grads + lands],
        input_output_aliases={a: a for a in range(2 * n)}, compiler_params=SPLIT_COPY)(*grads, *lands, send, recv, after)
    return list(out[:n]), list(out[n:])


def _reduce_sum(name, g, land, layer, into, chip, c):
    _, k4, n4 = g.shape
    half = k4 // 2
    tr = max(t for t in range(16, 513, 16) if half % t == 0)
    per = half // tr
    me = 2 * chip + c

    def body(s_ref, own_ref, *refs):
        total = own_ref[...].astype(F32)
        for ref in refs[:N_DEV - 1]:
            total = total + ref[...].astype(F32)
        refs[-1][...] = total

    in_specs = [pl.BlockSpec((None, tr, n4), lambda i, s: (s[0], s[1] * per + i, 0))]
    in_specs += [pl.BlockSpec((None, tr, n4), functools.partial(lambda i, s, m: (s[1 + m], i, 0), m=m))
                 for m in range(1, N_DEV)]
    ins = [g] + [land] * (N_DEV - 1)
    aliases = {}
    if into is not None:
        in_specs, ins, aliases = in_specs + [ANY], ins + [into], {1 + N_DEV: 0}
    return pl.pallas_call(
        body, name=name,
        grid_spec=pltpu.PrefetchScalarGridSpec(
            num_scalar_prefetch=1, grid=(per,), in_specs=in_specs,
            out_specs=pl.BlockSpec((None, tr, n4), lambda i, s: (layer, s[1] * per + i, 0))),
        out_shape=jax.ShapeDtypeStruct((DEPTH, k4, n4), F32), input_output_aliases=aliases,
        compiler_params=_params(("parallel",)))(_scalars(chip, c, *[me ^ m for m in range(1, N_DEV)]), *ins)


def _join_halves(bufs):
    n = len(bufs)

    def body(*refs):
        ins, outs = refs[:n], refs[n:2 * n]
        send_sem, recv_sem = refs[2 * n:]
        x, y, c, _ = _place()

        def rows(ref, which):
            half = ref.shape[1] // 2
            return ref.at[:, pl.ds(pl.multiple_of(which * half, half), half)]

        sends = [pltpu.make_async_remote_copy(
            src_ref=rows(ins[a], c), dst_ref=rows(outs[a], c), send_sem=send_sem.at[a], recv_sem=recv_sem.at[a],
            device_id=(x, y, 1 - c), device_id_type=MESH) for a in range(n)]
        for cp in sends:
            cp.start()
        for a in range(n):
            sends[a].wait_send()
            pltpu.make_async_remote_copy(
                src_ref=rows(ins[a], c), dst_ref=rows(outs[a], 1 - c), send_sem=send_sem.at[a], recv_sem=recv_sem.at[a],
                device_id=(x, y, 1 - c), device_id_type=MESH).wait_recv()

    return pl.pallas_call(
        body, name="join_halves", in_specs=[ANY] * n, out_specs=[ANY] * n,
        out_shape=[jax.ShapeDtypeStruct(b.shape, b.dtype) for b in bufs],
        input_output_aliases={a: a for a in range(n)},
        scratch_shapes=[pltpu.SemaphoreType.DMA((n,)), pltpu.SemaphoreType.DMA((n,))],
    )(*bufs)


def _all_reduce_small(block):
    r = block.shape[0]

    def body(x_ref, out_ref, slots, send_sem, recv_sem):
        x, y, c, _ = _place()
        me = 4 * x + 2 * y + c
        slots[me] = x_ref[...]
        sends = []
        for mask in range(1, N_DEV):
            fx, fy, fc = (mask >> 2) & 1, (mask >> 1) & 1, mask & 1
            peer = (x ^ fx, y ^ fy, c ^ fc)
            cp = pltpu.make_async_remote_copy(
                src_ref=x_ref, dst_ref=slots.at[me], send_sem=send_sem.at[mask - 1], recv_sem=recv_sem.at[mask - 1],
                device_id=peer, device_id_type=MESH)
            cp.start()
            sends.append(cp)
        for mask in range(1, N_DEV):
            src = me ^ mask
            pltpu.make_async_remote_copy(
                src_ref=x_ref, dst_ref=slots.at[src], send_sem=send_sem.at[mask - 1], recv_sem=recv_sem.at[mask - 1],
                device_id=(x, y, c), device_id_type=MESH).wait_recv()
        for cp in sends:
            cp.wait_send()
        total = slots[0]
        for d in range(1, N_DEV):
            total = total + slots[d]
        out_ref[...] = total

    vmem = pl.BlockSpec(memory_space=pltpu.VMEM)
    return pl.pallas_call(
        body, name="all_reduce_small", in_specs=[vmem], out_specs=vmem,
        out_shape=jax.ShapeDtypeStruct((r, 128), F32),
        scratch_shapes=[pltpu.VMEM((N_DEV, r, 128), F32), pltpu.SemaphoreType.DMA((N_DEV - 1,)),
                        pltpu.SemaphoreType.DMA((N_DEV - 1,))],
        compiler_params=pltpu.CompilerParams(vmem_limit_bytes=VMEM_LIMIT))(block)


B_Q_COL = 2304 // 128
B_K0, B_V0, B_END = 2816, 2944, 3072


def _full_cols(w_g):
    return w_g.transpose(1, 0, 2).reshape(w_g.shape[1], -1)


def _group_src(proj, g):
    if A_GROUPS[g][1] == 1:
        return ((proj, 2 * g), (proj, 6 + 2 * g), (proj, 12 + 2 * g))
    packed = jnp.concatenate([proj[:, t * 768 + g * 256:t * 768 + (g + 1) * 256] for t in range(3)], axis=1)
    return ((packed, 0), (packed, 2), (packed, 4))


def _kv_expand(kv):
    return jnp.broadcast_to(kv.reshape(S, 2, 1, HD), (S, 2, 4, HD)).reshape(S, 8 * HD)


def _kv_reduce(dkv):
    return dkv.reshape(S, 2, 4, HD).sum(axis=2).reshape(S, 2 * HD)


def _mixer_fwd(h1, wget, rel_bias, sinks_l, bidx):
    w = dict(wget(0, h1))
    proj = _mm_nt("proj_in", h1, w["w_in"], F32, tn=1152)
    no_sinks = jnp.full((4,), NEG, F32)
    srcs = [_group_src(proj, g) for g in range(3)]
    o_g, lse_g = [], []
    for g, (_, d) in enumerate(A_GROUPS):
        o, lse = _band_fwd("band_fwd_g%d" % g, d, 2, BLK, 4 * g, srcs[g], bidx[g], rel_bias, no_sinks)
        o_g.append(o)
        lse_g.append(lse)
    o_a32, o_a, lse_a = _comb_fwd(o_g, lse_g)
    src_b = ((proj, B_Q_COL), (_kv_expand(proj[:, B_K0:B_V0]), 0), (_kv_expand(proj[:, B_V0:B_END]), 0))
    o_b32, lse_b = _band_fwd("band_fwd_b", 1, 4, BLK - 1, N_A, src_b, bidx[3], rel_bias, sinks_l)
    o_b = o_b32.astype(BF16)
    o_c32, o_c, tot_c = _sb_fwd(proj)
    w.update(wget(1, o_c32))
    br = [_mm_nn("branch_a", o_a, w["w_br_a"], F32), _mm_nn("branch_b", o_b, w["w_br_b"], F32),
          _mm_nn("branch_c", o_c, w["w_br_c"], F32)]
    merged = _gate_fwd(proj, w["b_gate"], br)
    mo = _mm_nn("out_proj", merged, w["w_out"], F32)
    saved = dict(proj=proj, srcs=srcs, src_b=src_b, o_a32=o_a32, lse_a=lse_a, o_b32=o_b32, lse_b=lse_b, tot_c=tot_c,
                 o_a=o_a, o_b=o_b, o_c=o_c, br=br, merged=merged)
    return mo, saved, w


def _mixer_bwd(d_mo, h1, w, sv, rel_bias, sinks_l, bidx, stats_in):
    grads = {}
    dmerged = _mm_nt("out_proj_dx", d_mo, w["w_out"], F32)
    grads["w_out"] = _mm_tn_sharded("out_proj_dw", sv["merged"], d_mo, True)
    e, dgate, db_gate = _gate_bwd(sv["proj"], w["b_gate"], sv["br"], dmerged)
    grads["b_gate"] = db_gate
    d_o = {}
    for n, name in enumerate("abc"):
        d_o[name] = _mm_nt("branch_%s_dx" % name, e[n], w["w_br_" + name], F32)
        grads["w_br_" + name] = _mm_tn_sharded("branch_%s_dw" % name, sv["o_" + name], e[n], False)
    no_sinks = jnp.full((4,), NEG, F32)
    dqs, dks, dvs, stats = [], [], [], []
    for g, (_, d) in enumerate(A_GROUPS):
        dq, dk, dv, st = _band_bwd("band_bwd_g%d" % g, d, 2, BLK, 4 * g, sv["srcs"][g], bidx[g], rel_bias, no_sinks,
                                   sv["o_a32"], sv["lse_a"], d_o["a"], stats_in[4 * g:4 * g + 4])
        dqs.append(dq)
        dks.append(dk)
        dvs.append(dv)
        stats.append(st)
    dq_b, dk_x, dv_x, st = _band_bwd("band_bwd_b", 1, 4, BLK - 1, N_A, sv["src_b"], bidx[3], rel_bias, sinks_l,
                                     sv["o_b32"], sv["lse_b"], d_o["b"], stats_in[N_A:])
    stats = jnp.concatenate(stats + [st], axis=0)
    dcq, dck, dcv = _sb_bwd(sv["proj"], sv["tot_c"], d_o["c"])
    cols = dqs + dks + dvs + [dq_b, _kv_reduce(dk_x), _kv_reduce(dv_x), dcq, dck, dcv]
    dproj = jnp.concatenate([t.astype(BF16) for t in cols] + list(dgate), axis=1)
    dh1 = _mm_nn("proj_in_dx", dproj, w["w_in"], F32)
    grads["w_in"] = _mm_tn("proj_in_dw", dproj, h1, BF16, tm=768).reshape(N_CHIPS, IN_SHARD, D)
    return dh1, grads, stats


def _ffn_fwd(h2, w):
    u = _mm_nn("ffn_up", h2, w["w_up"], F32, tn=1024)
    a = _conv_fwd(u, w["conv_w"], w["conv_b"])
    dn = _mm_nn("ffn_down", a, w["w_down"], F32)
    return dn, dict(u=u, a=a)


def _ffn_bwd(d_dn, h2, w, sv):
    grads = {}
    da = _mm_nt("ffn_down_dx", d_dn, w["w_down"], F32, tn=1024)
    grads["w_down"] = _mm_tn_sharded("ffn_down_dw", sv["a"], d_dn, True)
    dug, duv, dwg, dwv, dbg, dbv = _conv_bwd(sv["u"], w["conv_w"], w["conv_b"], da)
    du = jnp.concatenate([dug, duv], axis=1)
    grads["conv_w"] = jnp.concatenate([dwg, dwv], axis=1)
    grads["conv_b"] = jnp.concatenate([dbg, dbv], axis=1)
    dh2 = _mm_nt("ffn_up_dx", du, w["w_up"], F32)
    grads["w_up"] = _mm_tn_sharded("ffn_up_dw", h2, du, False, tn=1024)
    return dh2, grads


BIG = ("w_in", "w_br_a", "w_br_b", "w_br_c", "w_out", "w_up", "w_down")


def _shard_view(name, w):
    return jnp.swapaxes(w, 1, 2) if name == "w_in" else w
WEIGHT_GROUPS = (("w_in", "b_gate"), ("w_br_a", "w_br_b", "w_br_c", "w_out"), ("w_up", "conv_w", "w_down"))
GRAD_GROUPS = (("w_down", "w_up"), ("w_out", "w_br_a", "w_br_b", "w_br_c", "w_in"))
SMALL_ROWS = (("rel_bias", 8), ("attn_pre_norm", 16), ("attn_post_norm", 16), ("ffn_pre_norm", 16), ("ffn_post_norm", 16),
              ("sinks", 8), ("conv_b", 128), ("b_gate", 48), ("conv_w", 384), ("loss", 8))


def _pack_small(vals):
    rows = []
    for name, n in SMALL_ROWS:
        flat = vals[name].reshape(-1).astype(F32)
        rows.append(jnp.pad(flat, (0, n * 128 - flat.shape[0])).reshape(n, 128))
    return jnp.concatenate(rows, axis=0)


def _unpack_small(block, shapes):
    out, row = {}, 0
    for name, n in SMALL_ROWS:
        size = int(np.prod(shapes[name]))
        out[name] = block[row:row + n].reshape(-1)[:size].reshape(shapes[name])
        row += n
    return out


def kernel(x, rel_bias, attn_pre_norm, w_in, b_gate, sinks, w_br_a, w_br_b, w_br_c, w_out, attn_post_norm, ffn_pre_norm, w_up, conv_w, conv_b, w_down, ffn_post_norm, loss_target, m_rel_bias, m_attn_pre_norm, m_w_in, m_b_gate, m_sinks, m_w_br_a, m_w_br_b, m_w_br_c, m_w_out, m_attn_post_norm, m_ffn_pre_norm, m_w_up, m_conv_w, m_conv_b, m_w_down, m_ffn_post_norm, v_rel_bias, v_attn_pre_norm, v_w_in, v_b_gate, v_sinks, v_w_br_a, v_w_br_b, v_w_br_c, v_w_out, v_attn_post_norm, v_ffn_pre_norm, v_w_up, v_conv_w, v_conv_b, v_w_down, v_ffn_post_norm):
    names = ("rel_bias", "attn_pre_norm", "w_in", "b_gate", "sinks", "w_br_a", "w_br_b", "w_br_c", "w_out",
             "attn_post_norm", "ffn_pre_norm", "w_up", "conv_w", "conv_b", "w_down", "ffn_post_norm")
    weights = dict(zip(names, (rel_bias, attn_pre_norm, w_in, b_gate, sinks, w_br_a, w_br_b, w_br_c, w_out,
                               attn_post_norm, ffn_pre_norm, w_up, conv_w, conv_b, w_down, ffn_post_norm)))
    mom1 = dict(zip(names, (m_rel_bias, m_attn_pre_norm, m_w_in, m_b_gate, m_sinks, m_w_br_a, m_w_br_b, m_w_br_c,
                            m_w_out, m_attn_post_norm, m_ffn_pre_norm, m_w_up, m_conv_w, m_conv_b, m_w_down,
                            m_ffn_post_norm)))
    mom2 = dict(zip(names, (v_rel_bias, v_attn_pre_norm, v_w_in, v_b_gate, v_sinks, v_w_br_a, v_w_br_b, v_w_br_c,
                            v_w_out, v_attn_post_norm, v_ffn_pre_norm, v_w_up, v_conv_w, v_conv_b, v_w_down,
                            v_ffn_post_norm)))

    chip = 2 * lax.axis_index("x") + lax.axis_index("y")
    core = lax.axis_index("c")

    keys = [(n, l) for l in range(DEPTH) for group in WEIGHT_GROUPS for n in group]
    bufs = []
    for n, l in keys:
        if n in BIG:
            bufs.append(_cast_into_slot("cast_" + n, _shard_view(n, weights[n]), l, chip))
        else:
            shard = weights[n][l]
            bufs.append(lax.dynamic_update_slice(jnp.zeros((N_CHIPS,) + shard.shape, F32), shard[None],
                                                 (chip, jnp.int32(0), jnp.int32(0))))
    groups = [[keys.index((n, l)) for n in group] for l in range(DEPTH) for group in WEIGHT_GROUPS]
    sems, in_flight, _ = _gather_start(bufs, groups)

    def wget(l, gi, after):
        g = l * len(WEIGHT_GROUPS) + gi
        got = _gather_wait("gather_wait_%d_%d" % (l, gi), [in_flight[a] for a in groups[g]], *sems[g], after)
        out = {}
        for n, buf in zip(WEIGHT_GROUPS[gi], got):
            out[n] = buf.reshape(-1, buf.shape[-1]) if n in ("w_in", "w_out", "w_down") else _full_cols(buf)
        if gi == len(WEIGHT_GROUPS) - 1:
            out["conv_b"] = conv_b[l:l + 1]
        return out

    pending = []

    def emit(l, gi, grads):
        group = GRAD_GROUPS[gi]
        *started, token = _reduce_start("reduce_start_%d_%d" % (l, gi), [grads[n] for n in group])
        pending.append((l, group) + tuple(started))
        return token[:1, :1]

    local = _local_step(x.reshape(S, D), loss_target.reshape(S, D), wget, emit, rel_bias, sinks, attn_pre_norm,
                        attn_post_norm, ffn_pre_norm, ffn_post_norm)
    return _reduce_and_update(x.shape, names, weights, mom1, mom2, chip, core, pending, *local)


def _local_step(xs, target, wget, emit, rel_bias, sinks, attn_pre_norm, attn_post_norm, ffn_pre_norm, ffn_post_norm):
    bidx = jnp.asarray(_bucket_maps())

    saved, layers = [], []
    h1 = _rms_fwd("pre_norm_first", xs, attn_pre_norm[0:1])
    x_in = xs
    for l in range(DEPTH):
        mo, sv_mix, w = _mixer_fwd(h1, functools.partial(wget, l), rel_bias, sinks[l], bidx)
        x_mid, h2 = _post_pre_fwd("post_attn_norm", x_in, mo, attn_post_norm[l:l + 1], ffn_pre_norm[l:l + 1])
        w.update(wget(l, 2, h2))
        dn, sv_ffn = _ffn_fwd(h2, w)
        g_next = attn_pre_norm[l + 1:l + 2] if l + 1 < DEPTH else None
        x_out, h1_next = _post_pre_fwd("post_ffn_norm" if l + 1 < DEPTH else "post_ffn_norm_last", x_mid, dn,
                                       ffn_post_norm[l:l + 1], g_next)
        saved.append(dict(x_in=x_in, h1=h1, mo=mo, x_mid=x_mid, h2=h2, dn=dn, mix=sv_mix, ffn=sv_ffn))
        layers.append(w)
        x_in, h1 = x_out, h1_next

    loss_row, dres = _loss_kernel(x_in, target)

    small = [None] * DEPTH
    stats = jnp.zeros((N_BAND_Q, 8, 128), F32)
    dh_next = None
    for l in reversed(range(DEPTH)):
        w, sv = layers[l], saved[l]
        if l + 1 < DEPTH:
            pre = (saved[l + 1]["x_in"], attn_pre_norm[l + 1:l + 2] + zero, dh_next)
            dres, d_dn, dg_pre_next, dg_fpost = _norm_bwd("post_ffn_norm_bwd", dres, pre,
                                                          (sv["dn"], ffn_post_norm[l:l + 1]))
            small[l + 1]["attn_pre_norm"] = dg_pre_next
        else:
            dres, d_dn, _, dg_fpost = _norm_bwd("post_ffn_norm_last_bwd", dres, None, (sv["dn"], ffn_post_norm[l:l + 1]))
        dh2, g_ffn = _ffn_bwd(d_dn, sv["h2"], w, sv["ffn"])
        zero = emit(l, 0, g_ffn)
        dres, d_mo, dg_fpre, dg_apost = _norm_bwd("post_attn_norm_bwd", dres,
                                                  (sv["x_mid"], ffn_pre_norm[l:l + 1] + zero, dh2),
                                                  (sv["mo"], attn_post_norm[l:l + 1]))
        dh_next, g_mix, stats = _mixer_bwd(d_mo, sv["h1"], w, sv["mix"], rel_bias, sinks[l], bidx, stats)
        zero = emit(l, 1, g_mix)
        small[l] = dict(ffn_post_norm=dg_fpost, ffn_pre_norm=dg_fpre, attn_post_norm=dg_apost,
                        sinks=stats[N_A:, 1, 0], conv_b=g_ffn["conv_b"], b_gate=g_mix["b_gate"], conv_w=g_ffn["conv_w"])
    grad_x, _, dg_pre0, _ = _norm_bwd("pre_norm_first_bwd", dres, (saved[0]["x_in"], attn_pre_norm[0:1] + zero, dh_next),
                                      None)
    small[0]["attn_pre_norm"] = dg_pre0
    return loss_row, grad_x, small, stats


def _reduce_and_update(x_shape, names, weights, mom1, mom2, chip, core, pending, loss_row, grad_x, small, stats):
    small_vals = {n: jnp.stack([small[l][n].reshape(weights[n].shape[1:]) for l in range(DEPTH)])
                  for n in ("attn_pre_norm", "attn_post_norm", "ffn_pre_norm", "ffn_post_norm", "conv_b", "sinks")}
    small_vals["b_gate"] = jnp.stack([small[l]["b_gate"] for l in range(DEPTH)])
    small_vals["conv_w"] = jnp.stack([small[l]["conv_w"] for l in range(DEPTH)])
    small_vals["rel_bias"] = stats[:, 0, :NUM_BUCKETS].T
    small_vals["loss"] = loss_row[0, :1]
    shapes = {n: v.shape for n, v in small_vals.items()}
    reduced = _unpack_small(_all_reduce_small(_pack_small(small_vals)), shapes)
    reduced["b_gate"] = lax.dynamic_slice_in_dim(reduced["b_gate"], chip * (D // N_CHIPS), D // N_CHIPS, axis=2)
    reduced["conv_w"] = lax.dynamic_slice_in_dim(reduced["conv_w"], chip * (2 * D_FF // N_CHIPS), 2 * D_FF // N_CHIPS, axis=2)

    summed = {}
    for l, group, send, recv, gs, lands in pending:
        gs, lands = _reduce_wait("reduce_wait_%d_%s" % (l, group[0]), send, recv, gs, lands, grad_x)
        for n, g, land in zip(group, gs, lands):
            summed[n] = _reduce_sum("reduce_sum_%d_%s" % (l, n), g, land, l, summed.get(n), chip, core)
    full = _join_halves([summed[n] for n in BIG])
    grads = dict(zip(BIG, full))
    for n in names:
        if n not in grads:
            grads[n] = reduced[n].reshape(weights[n].shape)

    delta, new_m, new_v = {}, {}, {}
    for n in names:
        delta[n], new_m[n], new_v[n] = _adamw("adamw_" + n, _shard_view(n, weights[n]), grads[n],
                                              _shard_view(n, mom1[n]), _shard_view(n, mom2[n]))
    for out in (grads, delta, new_m, new_v):
        out["w_in"] = _shard_view("w_in", out["w_in"])

    loss = reduced["loss"].reshape(())
    return (loss, grad_x.reshape(x_shape), *[grads[n] for n in names], *[delta[n] for n in names],
            *[new_m[n] for n in names], *[new_v[n] for n in names])
```

```python
import functools
import math

import numpy as np
import jax
import jax.numpy as jnp
from jax import lax
from jax.experimental import pallas as pl
from jax.experimental.pallas import tpu as pltpu

F32 = jnp.float32
BF16 = jnp.bfloat16

S = 2048
D = 1024
DEPTH = 2
HD = 64
BLK = 128
NQB = S // BLK
A_GROUPS = ((128, 1), (512, 4), (2048, 16))
N_BAND_Q = 20
N_A = 12
NUM_BUCKETS = 32
MAX_DISTANCE = 2048
D_FF = 4096
IN_COLS = 6912
IN_SHARD = IN_COLS // 4
OFF_GATE = 3840
EPS = 1e-6
SCALE = HD ** -0.5
NEG = -1e30
N_CHIPS = 4
N_DEV = 8

ADAM_LR = 0.001
ADAM_B1 = 0.9
ADAM_B2 = 0.999
ADAM_EPS = 1e-08
ADAM_WD = 0.01
ADAM_STEP = 10

VMEM_LIMIT = 56 * 1024 * 1024

NN = (((1,), (0,)), ((), ()))
NT = (((1,), (1,)), ((), ()))
TN = (((0,), (0,)), ((), ()))

MESH = pl.DeviceIdType.MESH
ANY = pl.BlockSpec(memory_space=pl.ANY)


def _dot(a, b, dims):
    return lax.dot_general(a, b, dims, preferred_element_type=F32)


def _params(sem):
    return pltpu.CompilerParams(dimension_semantics=sem, vmem_limit_bytes=VMEM_LIMIT)


def _matmul(name, a, b, out_shape, out_dtype, grid, a_spec, b_spec, o_spec, dims, acc_shape):
    nk = grid[-1]

    def body(a_ref, b_ref, o_ref, *scratch):
        part = _dot(a_ref[...].astype(BF16), b_ref[...].astype(BF16), dims)
        if nk == 1:
            o_ref[...] = part.astype(o_ref.dtype)
            return
        acc_ref, = scratch
        k = pl.program_id(len(grid) - 1)

        @pl.when(k == 0)
        def _():
            acc_ref[...] = part

        @pl.when(k > 0)
        def _():
            acc_ref[...] += part

        @pl.when(k == nk - 1)
        def _():
            o_ref[...] = acc_ref[...].astype(o_ref.dtype)

    scratch = [] if nk == 1 else [pltpu.VMEM(acc_shape, F32)]
    sem = ("parallel",) * (len(grid) - 1) + ("arbitrary",)
    return pl.pallas_call(
        body, name=name, grid=grid, in_specs=[a_spec, b_spec], out_specs=o_spec,
        out_shape=jax.ShapeDtypeStruct(out_shape, out_dtype), scratch_shapes=scratch,
        compiler_params=_params(sem))(a, b)


FULL_K = 8192


def _mm_tn_sharded(name, a, b, row_sharded, tm=512, tn=512, tk=FULL_K):
    k, m = a.shape
    n = b.shape[1]
    m4, n4 = (m // N_CHIPS, n) if row_sharded else (m, n // N_CHIPS)
    tm, tn, tk = min(tm, m4), min(tn, n4), min(tk, k)
    per_m, per_n = m4 // tm, n4 // tn
    if row_sharded:
        o_map = lambda i, j, l: (i // per_m, i % per_m, j)
    else:
        o_map = lambda i, j, l: (j // per_n, i, j % per_n)
    return _matmul(name, a, b, (N_CHIPS, m4, n4), BF16, (m // tm, n // tn, k // tk),
                   pl.BlockSpec((tk, tm), lambda i, j, l: (l, i)),
                   pl.BlockSpec((tk, tn), lambda i, j, l: (l, j)),
                   pl.BlockSpec((None, tm, tn), o_map), TN, (tm, tn))


def _mm_nn(name, a, b, out_dtype, tm=512, tn=512, tk=FULL_K):
    m, k = a.shape
    n = b.shape[1]
    tm, tn, tk = min(tm, m), min(tn, n), min(tk, k)
    return _matmul(name, a, b, (m, n), out_dtype, (m // tm, n // tn, k // tk),
                   pl.BlockSpec((tm, tk), lambda i, j, l: (i, l)),
                   pl.BlockSpec((tk, tn), lambda i, j, l: (l, j)),
                   pl.BlockSpec((tm, tn), lambda i, j, l: (i, j)), NN, (tm, tn))


def _mm_nt(name, a, b, out_dtype, tm=512, tn=512, tk=FULL_K):
    m, k = a.shape
    n = b.shape[0]
    tm, tn, tk = min(tm, m), min(tn, n), min(tk, k)
    return _matmul(name, a, b, (m, n), out_dtype, (m // tm, n // tn, k // tk),
                   pl.BlockSpec((tm, tk), lambda i, j, l: (i, l)),
                   pl.BlockSpec((tn, tk), lambda i, j, l: (j, l)),
                   pl.BlockSpec((tm, tn), lambda i, j, l: (i, j)), NT, (tm, tn))


def _mm_tn(name, a, b, out_dtype, tm=512, tn=512, tk=FULL_K):
    k, m = a.shape
    n = b.shape[1]
    tm, tn, tk = min(tm, m), min(tn, n), min(tk, k)
    return _matmul(name, a, b, (m, n), out_dtype, (m // tm, n // tn, k // tk),
                   pl.BlockSpec((tk, tm), lambda i, j, l: (l, i)),
                   pl.BlockSpec((tk, tn), lambda i, j, l: (l, j)),
                   pl.BlockSpec((tm, tn), lambda i, j, l: (i, j)), TN, (tm, tn))


TR = 256


def _row_spec(width=D):
    return pl.BlockSpec((TR, width), lambda i: (i, 0))


def _vec_spec(width=D):
    return pl.BlockSpec((1, width), lambda i: (0, 0))


def _rms(x, g):
    r = lax.rsqrt(jnp.mean(x * x, axis=-1, keepdims=True) + EPS)
    return x * r * g


def _rms_fwd(name, x, g):
    def body(x_ref, g_ref, h_ref):
        h_ref[...] = _rms(x_ref[...], g_ref[...]).astype(BF16)

    return pl.pallas_call(
        body, name=name, grid=(S // TR,), in_specs=[_row_spec(), _vec_spec()], out_specs=_row_spec(),
        out_shape=jax.ShapeDtypeStruct((S, D), BF16), compiler_params=_params(("parallel",)))(x, g)


def _post_pre_fwd(name, x, y, g_post, g_pre):
    has_pre = g_pre is not None

    def body(*refs):
        if has_pre:
            x_ref, y_ref, gp_ref, gn_ref, xn_ref, h_ref = refs
        else:
            x_ref, y_ref, gp_ref, xn_ref = refs
        xn = x_ref[...] + _rms(y_ref[...], gp_ref[...])
        xn_ref[...] = xn
        if has_pre:
            h_ref[...] = _rms(xn, gn_ref[...]).astype(BF16)

    ins = [x, y, g_post] + ([g_pre] if has_pre else [])
    in_specs = [_row_spec(), _row_spec(), _vec_spec()] + ([_vec_spec()] if has_pre else [])
    out_shape = [jax.ShapeDtypeStruct((S, D), F32)] + ([jax.ShapeDtypeStruct((S, D), BF16)] if has_pre else [])
    out_specs = [_row_spec()] + ([_row_spec()] if has_pre else [])
    out = pl.pallas_call(
        body, name=name, grid=(S // TR,), in_specs=in_specs, out_specs=out_specs, out_shape=out_shape,
        compiler_params=_params(("parallel",)))(*ins)
    return out if has_pre else (out[0], None)


def _rms_bwd_math(x, g, dy):
    r = lax.rsqrt(jnp.mean(x * x, axis=-1, keepdims=True) + EPS)
    n = x * r
    dn = dy * g
    dx = r * (dn - n * jnp.mean(dn * n, axis=-1, keepdims=True))
    return dx, jnp.sum(dy * n, axis=0, keepdims=True)


def _norm_bwd(name, dres, pre=None, post=None):
    has_pre, has_post = pre is not None, post is not None

    def body(*refs):
        refs = list(refs)
        dres_ref = refs.pop(0)
        if has_pre:
            xn_ref, gn_ref, dh_ref = refs[:3]
            refs = refs[3:]
        if has_post:
            y_ref, gp_ref = refs[:2]
            refs = refs[2:]
        dxn_ref = refs.pop(0)
        dy_ref = refs.pop(0) if has_post else None
        dgn_ref = refs.pop(0) if has_pre else None
        dgp_ref = refs.pop(0) if has_post else None
        first = pl.program_id(0) == 0
        dxn = dres_ref[...]
        if has_pre:
            dx, dg = _rms_bwd_math(xn_ref[...], gn_ref[...], dh_ref[...])
            dxn = dxn + dx

            @pl.when(first)
            def _():
                dgn_ref[...] = dg

            @pl.when(jnp.logical_not(first))
            def _():
                dgn_ref[...] += dg
        dxn_ref[...] = dxn
        if has_post:
            dy, dg = _rms_bwd_math(y_ref[...], gp_ref[...], dxn)
            dy_ref[...] = dy.astype(BF16)

            @pl.when(first)
            def _():
                dgp_ref[...] = dg

            @pl.when(jnp.logical_not(first))
            def _():
                dgp_ref[...] += dg

    ins, in_specs = [dres], [_row_spec()]
    if has_pre:
        ins += list(pre)
        in_specs += [_row_spec(), _vec_spec(), _row_spec()]
    if has_post:
        ins += list(post)
        in_specs += [_row_spec(), _vec_spec()]
    out_shape, out_specs = [jax.ShapeDtypeStruct((S, D), F32)], [_row_spec()]
    if has_post:
        out_shape.append(jax.ShapeDtypeStruct((S, D), BF16))
        out_specs.append(_row_spec())
    for _ in range(int(has_pre) + int(has_post)):
        out_shape.append(jax.ShapeDtypeStruct((1, D), F32))
        out_specs.append(_vec_spec())
    out = list(pl.pallas_call(
        body, name=name, grid=(S // TR,), in_specs=in_specs, out_specs=out_specs, out_shape=out_shape,
        compiler_params=_params(("arbitrary",)))(*ins))
    dxn = out.pop(0)
    dy = out.pop(0) if has_post else None
    dgn = out.pop(0) if has_pre else None
    dgp = out.pop(0) if has_post else None
    return dxn, dy, dgn, dgp


def _loss_kernel(y, target):
    def body(y_ref, t_ref, loss_ref, dy_ref):
        e = y_ref[...] - t_ref[...]
        dy_ref[...] = e * (1.0 / D)
        part = jnp.zeros((1, 128), F32) + 0.5 * jnp.sum(jnp.mean(e * e, axis=-1, keepdims=True))

        @pl.when(pl.program_id(0) == 0)
        def _():
            loss_ref[...] = part

        @pl.when(pl.program_id(0) > 0)
        def _():
            loss_ref[...] += part

    return pl.pallas_call(
        body, name="loss", grid=(S // TR,), in_specs=[_row_spec(), _row_spec()],
        out_specs=[_vec_spec(128), _row_spec()],
        out_shape=[jax.ShapeDtypeStruct((1, 128), F32), jax.ShapeDtypeStruct((S, D), F32)],
        compiler_params=_params(("arbitrary",)))(y, target)


def _t5_bucket_np(dist):
    max_exact = NUM_BUCKETS // 2
    nf = np.maximum(dist, 1).astype(np.float32)
    large = max_exact + (np.log(nf / max_exact) / np.float32(math.log(MAX_DISTANCE / max_exact))
                         * (NUM_BUCKETS - max_exact)).astype(np.int32)
    large = np.minimum(large, NUM_BUCKETS - 1)
    return np.where(dist < max_exact, dist, large).astype(np.int32)


def _bucket_maps():
    a = np.arange(BLK)[:, None]
    b = np.arange(2 * BLK)[None, :]
    dist = np.maximum(a + BLK - b, 0)
    maps = [_t5_bucket_np(dist * d) for _, d in A_GROUPS] + [_t5_bucket_np(dist)]
    return np.stack(maps).astype(np.int32)


def _classes(arr, d):
    return arr.reshape(S // d, d * arr.shape[1])


def _band_spec(arr, col0, prev):
    ncol = arr.shape[1] // 128
    if prev:
        return pl.BlockSpec((BLK, 128), lambda p, r, b: (jnp.maximum(b - 1, 0), r * ncol + col0 + p))
    return pl.BlockSpec((BLK, 128), lambda p, r, b: (b, r * ncol + col0 + p))


def _band_bias(tab_ref, bidx_ref, h):
    bi = bidx_ref[...]
    bias = jnp.zeros((BLK, 2 * BLK), F32)
    for kk in range(NUM_BUCKETS):
        bias = jnp.where(bi == kk, tab_ref[kk, h], bias)
    return bias


def _lane_lo(rows=BLK):
    return lax.broadcasted_iota(jnp.int32, (rows, 128), 1) < HD


def _per_head(x, lo):
    return (jnp.sum(jnp.where(lo, x, 0.0), axis=1, keepdims=True) * (1.0 / HD),
            jnp.sum(jnp.where(lo, 0.0, x), axis=1, keepdims=True) * (1.0 / HD))


def _band_mask(b, maxd):
    a = lax.broadcasted_iota(jnp.int32, (2 * BLK, 2 * BLK), 0) & (BLK - 1)
    c = lax.broadcasted_iota(jnp.int32, (2 * BLK, 2 * BLK), 1)
    dist = a + BLK - c
    return jnp.logical_and(jnp.logical_and(dist >= 0, dist <= maxd), jnp.logical_or(c >= BLK, b > 0))


def _stack_heads(x, lo, dtype=BF16):
    return jnp.concatenate([jnp.where(lo, x, 0.0), jnp.where(lo, 0.0, x)], axis=0).astype(dtype)


def _unstack_heads(x, lo):
    n = x.shape[0] // 2
    return jnp.where(lo, x[:n], x[n:])


def _stack_rows(prev_ref, cur_ref):
    return jnp.concatenate([prev_ref[...], cur_ref[...]], axis=0).astype(BF16)


def _band_fwd(name, d, n_pairs, maxd, head0, srcs, bidx_g, tab, sinks):
    nb = S // d // BLK
    (qa, qc), (ka, kc), (va, vc) = srcs
    out_spec = pl.BlockSpec((BLK, 128), lambda p, r, b: (b, r * n_pairs + p))
    smem = pl.BlockSpec(memory_space=pltpu.SMEM)
    full = pl.BlockSpec((BLK, 2 * BLK), lambda p, r, b: (0, 0))

    def body(tab_ref, sink_ref, q_ref, kp_ref, kc_ref, vp_ref, vc_ref, bidx_ref, o_ref, lse_ref, bias_ref):
        p, r, b = pl.program_id(0), pl.program_id(1), pl.program_id(2)

        @pl.when(jnp.logical_and(r == 0, b == 0))
        def _():
            for h in range(2):
                bias_ref[h * BLK:(h + 1) * BLK, :] = _band_bias(tab_ref, bidx_ref, head0 + 2 * p + h)

        lo = _lane_lo()
        qs = _stack_heads(q_ref[...] * SCALE, lo)
        ks, vs = _stack_rows(kp_ref, kc_ref), _stack_rows(vp_ref, vc_ref)
        s = jnp.where(_band_mask(b, maxd), _dot(qs, ks, NT) + bias_ref[...], NEG)
        m = jnp.max(s, axis=1, keepdims=True)
        pr = jnp.exp(s - m)
        l = jnp.sum(pr, axis=1, keepdims=True)
        num = _dot(pr.astype(BF16), vs, NN)
        lse = m + jnp.log(l)
        sink = jnp.where(lax.broadcasted_iota(jnp.int32, (2 * BLK, 1), 0) < BLK, sink_ref[2 * p], sink_ref[2 * p + 1])
        sig = 1.0 / (1.0 + jnp.exp(sink - lse))
        o_ref[...] = _unstack_heads(num * (sig / l), lo)
        lse_ref[...] = _unstack_heads(lse + jnp.zeros((2 * BLK, 128), F32), lo)

    shape = jax.ShapeDtypeStruct((S // d, d * n_pairs * 128), F32)
    o, lse = pl.pallas_call(
        body, name=name, grid=(n_pairs, d, nb),
        in_specs=[smem, smem, _band_spec(qa, qc, False), _band_spec(ka, kc, True), _band_spec(ka, kc, False),
                  _band_spec(va, vc, True), _band_spec(va, vc, False), full],
        out_specs=[out_spec, out_spec], out_shape=[shape, shape],
        scratch_shapes=[pltpu.VMEM((2 * BLK, 2 * BLK), F32)],
        compiler_params=_params(("parallel", "arbitrary", "arbitrary")))(
            tab, sinks, _classes(qa, d), _classes(ka, d), _classes(ka, d), _classes(va, d), _classes(va, d), bidx_g)
    return o.reshape(S, n_pairs * 128), lse.reshape(S, n_pairs * 128)


def _band_bwd(name, d, n_pairs, maxd, head0, srcs, bidx_g, tab, sinks, o, lse, do, stats_in):
    nb = S // d // BLK
    rows = S // d
    (qa, qc), (ka, kc), (va, vc) = srcs
    blk_spec = pl.BlockSpec((BLK, 128), lambda p, r, b: (b, r * n_pairs + p))
    cls_spec = pl.BlockSpec((rows, 128), lambda p, r, b: (0, r * n_pairs + p))
    smem = pl.BlockSpec(memory_space=pltpu.SMEM)
    full = pl.BlockSpec((BLK, 2 * BLK), lambda p, r, b: (0, 0))
    stat_spec = pl.BlockSpec((2, 8, 128), lambda p, r, b: (p, 0, 0))

    def body(tab_ref, sink_ref, q_ref, kp_ref, kc_ref, vp_ref, vc_ref, bidx_ref, o_ref, lse_ref, do_ref, sin_ref,
             dq_ref, dk_ref, dv_ref, stat_ref, bias_ref, dsacc_ref, sk_ref):
        p, r, b = pl.program_id(0), pl.program_id(1), pl.program_id(2)

        @pl.when(jnp.logical_and(r == 0, b == 0))
        def _():
            for h in range(2):
                bias_ref[h * BLK:(h + 1) * BLK, :] = _band_bias(tab_ref, bidx_ref, head0 + 2 * p + h)
            dsacc_ref[...] = jnp.zeros_like(dsacc_ref)
            sk_ref[...] = jnp.zeros_like(sk_ref)

        @pl.when(b == 0)
        def _():
            dk_ref[...] = jnp.zeros_like(dk_ref)
            dv_ref[...] = jnp.zeros_like(dv_ref)

        lo = _lane_lo()
        qs = _stack_heads(q_ref[...] * SCALE, lo)
        ks, vs = _stack_rows(kp_ref, kc_ref), _stack_rows(vp_ref, vc_ref)
        do = do_ref[...]
        dos = _stack_heads(do, lo, F32)
        lse = jnp.concatenate(_per_head(lse_ref[...], lo), axis=0)
        prod = do * o_ref[...]
        delta = jnp.concatenate([jnp.sum(jnp.where(lo, prod, 0.0), axis=1, keepdims=True),
                                 jnp.sum(jnp.where(lo, 0.0, prod), axis=1, keepdims=True)], axis=0)
        head1 = lax.broadcasted_iota(jnp.int32, (2 * BLK, 1), 0) >= BLK
        sig = 1.0 / (1.0 + jnp.exp(jnp.where(head1, sink_ref[2 * p + 1], sink_ref[2 * p]) - lse))
        s = _dot(qs, ks, NT) + bias_ref[...]
        pr = jnp.where(_band_mask(b, maxd), jnp.exp(s - lse), 0.0)
        ds = pr * (sig * (_dot(dos.astype(BF16), vs, NT) - delta))
        dsb = ds.astype(BF16)
        dq_ref[...] = SCALE * _unstack_heads(_dot(dsb, ks, NN), lo)
        dk = _dot(dsb, qs, TN)
        dv = _dot(pr.astype(BF16), (sig * dos).astype(BF16), TN)
        cur = pl.ds(pl.multiple_of(b * BLK, BLK), BLK)
        prev = pl.ds(pl.multiple_of(jnp.maximum(b - 1, 0) * BLK, BLK), BLK)
        dk_ref[prev, :] += dk[:BLK]
        dk_ref[cur, :] += dk[BLK:]
        dv_ref[prev, :] += dv[:BLK]
        dv_ref[cur, :] += dv[BLK:]
        dsacc_ref[...] += ds
        sink_grad = -delta * (1.0 - sig)
        for h in range(2):
            sk_ref[h] += jnp.zeros((8, 128), F32) + jnp.sum(sink_grad[h * BLK:(h + 1) * BLK])

        @pl.when(jnp.logical_and(r == d - 1, b == nb - 1))
        def _():
            bi = bidx_ref[...]
            lane = lax.broadcasted_iota(jnp.int32, (8, 128), 1)
            sub = lax.broadcasted_iota(jnp.int32, (8, 128), 0)
            for h in range(2):
                acc = dsacc_ref[h * BLK:(h + 1) * BLK, :]
                row = jnp.where(jnp.logical_and(sub == 1, lane == 0), sk_ref[h], 0.0)
                for kk in range(NUM_BUCKETS):
                    tot = jnp.sum(jnp.where(bi == kk, acc, 0.0))
                    row = jnp.where(jnp.logical_and(sub == 0, lane == kk), tot, row)
                stat_ref[h] = row + jnp.where(sub == 0, sin_ref[h], 0.0)

    shape = jax.ShapeDtypeStruct((rows, d * n_pairs * 128), F32)
    dq, dk, dv, stats = pl.pallas_call(
        body, name=name, grid=(n_pairs, d, nb),
        in_specs=[smem, smem, _band_spec(qa, qc, False), _band_spec(ka, kc, True), _band_spec(ka, kc, False),
                  _band_spec(va, vc, True), _band_spec(va, vc, False), full, blk_spec, blk_spec, blk_spec, stat_spec],
        out_specs=[blk_spec, cls_spec, cls_spec, stat_spec],
        out_shape=[shape, shape, shape, jax.ShapeDtypeStruct((2 * n_pairs, 8, 128), F32)],
        scratch_shapes=[pltpu.VMEM((2 * BLK, 2 * BLK), F32), pltpu.VMEM((2 * BLK, 2 * BLK), F32),
                        pltpu.VMEM((2, 8, 128), F32)],
        compiler_params=_params(("arbitrary", "arbitrary", "arbitrary")))(
            tab, sinks, _classes(qa, d), _classes(ka, d), _classes(ka, d), _classes(va, d), _classes(va, d), bidx_g,
            _classes(o, d), _classes(lse, d), _classes(do, d), stats_in)
    width = n_pairs * 128
    return dq.reshape(S, width), dk.reshape(S, width), dv.reshape(S, width), stats


def _comb_fwd(o_g, lse_g):
    def body(o0, o1, o2, l0, l1, l2, out_ref, outb_ref, lse_ref):
        a0, a1, a2 = l0[...], l1[...], l2[...]
        m = jnp.maximum(jnp.maximum(a0, a1), a2)
        e0, e1, e2 = jnp.exp(a0 - m), jnp.exp(a1 - m), jnp.exp(a2 - m)
        tot = e0 + e1 + e2
        out = (e0 * o0[...] + e1 * o1[...] + e2 * o2[...]) / tot
        out_ref[...] = out
        outb_ref[...] = out.astype(BF16)
        lse_ref[...] = m + jnp.log(tot)

    spec = _row_spec(4 * HD)
    f32 = jax.ShapeDtypeStruct((S, 4 * HD), F32)
    return pl.pallas_call(
        body, name="comb_fwd", grid=(S // TR,), in_specs=[spec] * 6, out_specs=[spec] * 3,
        out_shape=[f32, jax.ShapeDtypeStruct((S, 4 * HD), BF16), f32],
        compiler_params=_params(("parallel",)))(*o_g, *lse_g)


def _split2(x):
    hi = x.astype(BF16)
    return hi, (x - hi.astype(F32)).astype(BF16)


KB = 2 * BLK
SBQ = 2 * BLK


def _tri_sum(x, tri):
    hi, lo = _split2(x)
    both = _dot(jnp.concatenate([hi, lo], axis=0), tri, NN)
    return both[:x.shape[0]] + both[x.shape[0]:]


def _tri(strict_upper):
    r = lax.broadcasted_iota(jnp.int32, (KB, KB), 0)
    c = lax.broadcasted_iota(jnp.int32, (KB, KB), 1)
    return jnp.where(r > c if strict_upper else r < c, 1.0, 0.0).astype(BF16)


def _sb_terms(qs, kj, before):
    z = _dot(qs, kj, NT)
    lsp = jnp.minimum(z, 0.0) - jnp.log(1.0 + jnp.exp(-jnp.abs(z)))
    return lsp, jnp.where(before, lsp - z, 0.0)


def _sb_before(i, m):
    t = (lax.broadcasted_iota(jnp.int32, (2 * SBQ, KB), 0) & (SBQ - 1)) + i * SBQ
    s = lax.broadcasted_iota(jnp.int32, (2 * SBQ, KB), 1) + m * KB
    return s < t


C_COL = 3072 // 128


def _sb_fwd(proj):
    blk = lambda off: pl.BlockSpec((SBQ, 128), lambda p, i: (i, off + p))
    col = lambda off: pl.BlockSpec((S, 128), lambda p, i: (0, off + p))
    out = pl.BlockSpec((SBQ, 128), lambda p, i: (i, p))

    def body(q_ref, k_ref, v_ref, o_ref, ob_ref, tot_ref):
        i = pl.program_id(1)
        lo = _lane_lo(SBQ)
        qs = _stack_heads(q_ref[...] * SCALE, lo)
        suffix = _tri(True)

        def step(n, carry):
            acc, rest = carry
            m = i - n
            rows = pl.ds(pl.multiple_of(m * KB, KB), KB)
            kj, vj = k_ref[rows, :].astype(BF16), v_ref[rows, :].astype(BF16)
            before = _sb_before(i, m)
            lsp, lk = _sb_terms(qs, kj, before)
            w = jnp.where(before, jnp.exp(lsp + _tri_sum(lk, suffix) + rest), 0.0)
            return acc + _dot(w.astype(BF16), vj, NN), rest + jnp.sum(lk, axis=1, keepdims=True)

        acc, rest = lax.fori_loop(0, i + 1, step, (jnp.zeros((2 * SBQ, 128), F32), jnp.zeros((2 * SBQ, 1), F32)))
        o = _unstack_heads(acc, lo)
        o_ref[...] = o
        ob_ref[...] = o.astype(BF16)
        tot_ref[...] = _unstack_heads(rest + jnp.zeros((2 * SBQ, 128), F32), lo)

    f32 = jax.ShapeDtypeStruct((S, 4 * HD), F32)
    return pl.pallas_call(
        body, name="sb_fwd", grid=(2, S // SBQ), in_specs=[blk(C_COL), col(C_COL + 2), col(C_COL + 4)],
        out_specs=[out, out, out], out_shape=[f32, jax.ShapeDtypeStruct((S, 4 * HD), BF16), f32],
        compiler_params=_params(("parallel", "arbitrary")))(proj, proj, proj)


def _sb_bwd(proj, tot, do):
    blk = lambda off: pl.BlockSpec((SBQ, 128), lambda p, i: (i, off + p))
    col = lambda off: pl.BlockSpec((S, 128), lambda p, i: (0, off + p))

    def body(q_ref, k_ref, v_ref, tot_ref, do_ref, dq_ref, dk_ref, dv_ref):
        i = pl.program_id(1)

        @pl.when(i == 0)
        def _():
            dk_ref[...] = jnp.zeros_like(dk_ref)
            dv_ref[...] = jnp.zeros_like(dv_ref)

        lo = _lane_lo(SBQ)
        qs = _stack_heads(q_ref[...] * SCALE, lo)
        dos = _stack_heads(do_ref[...], lo)
        tots = jnp.concatenate(_per_head(tot_ref[...], lo), axis=0)
        prefix = _tri(False)

        def step(m, carry):
            dq, keep_left, g_left = carry
            rows = pl.ds(pl.multiple_of(m * KB, KB), KB)
            kj, vj = k_ref[rows, :].astype(BF16), v_ref[rows, :].astype(BF16)
            before = _sb_before(i, m)
            lsp, lk = _sb_terms(qs, kj, before)
            log_rest = tots - keep_left - lk - _tri_sum(lk, prefix)
            w = jnp.where(before, jnp.exp(lsp + log_rest), 0.0)
            g = w * _dot(dos, vj, NT)
            g_before = g_left + _dot(g.astype(BF16), prefix, NN)
            beta = jnp.exp(lsp)
            dz = jnp.where(before, g * (1.0 - beta) - g_before * beta, 0.0).astype(BF16)
            dk_ref[rows, :] += _dot(dz, qs, TN)
            dv_ref[rows, :] += _dot(w.astype(BF16), dos, TN)
            return (dq + _dot(dz, kj, NN), keep_left + jnp.sum(lk, axis=1, keepdims=True),
                    g_left + jnp.sum(g, axis=1, keepdims=True))

        zero = (jnp.zeros((2 * SBQ, 128), F32), jnp.zeros((2 * SBQ, 1), F32), jnp.zeros((2 * SBQ, 1), F32))
        dq, _, _ = lax.fori_loop(0, i + 1, step, zero)
        dq_ref[...] = SCALE * _unstack_heads(dq, lo)

    out_blk = pl.BlockSpec((SBQ, 128), lambda p, i: (i, p))
    out_col = pl.BlockSpec((S, 128), lambda p, i: (0, p))
    f32 = jax.ShapeDtypeStruct((S, 4 * HD), F32)
    return pl.pallas_call(
        body, name="sb_bwd", grid=(2, S // SBQ),
        in_specs=[blk(C_COL), col(C_COL + 2), col(C_COL + 4), out_blk, out_blk],
        out_specs=[out_blk, out_col, out_col], out_shape=[f32, f32, f32],
        compiler_params=_params(("arbitrary", "arbitrary")))(proj, proj, proj, tot, do)


TG = 256
GATE_BLK0 = OFF_GATE // TG


def _gate_specs():
    grid = (D // TG, S // TG)
    p_specs = [pl.BlockSpec((TG, TG), functools.partial(lambda c, r, br: (r, GATE_BLK0 + br * (D // TG) + c), br=br))
               for br in range(3)]
    b_spec = pl.BlockSpec((3, TG), lambda c, r: (0, c))
    t_spec = pl.BlockSpec((TG, TG), lambda c, r: (r, c))
    return grid, p_specs, b_spec, t_spec


def _sigmoid(x):
    return 1.0 / (1.0 + jnp.exp(-x))


def _three_rows(rows):
    sub = lax.broadcasted_iota(jnp.int32, (3, rows[0].shape[1]), 0)
    return jnp.where(sub == 0, rows[0], jnp.where(sub == 1, rows[1], rows[2]))


def _gate_fwd(proj, b_gate, br):
    grid, p_specs, b_spec, t_spec = _gate_specs()

    def body(p0, p1, p2, b_ref, r0, r1, r2, out_ref):
        acc = jnp.zeros((TG, TG), F32)
        for n, (p, r) in enumerate(((p0, r0), (p1, r1), (p2, r2))):
            acc += _sigmoid(p[...] + b_ref[n:n + 1, :]) * r[...]
        out_ref[...] = acc.astype(BF16)

    return pl.pallas_call(
        body, name="gate_fwd", grid=grid, in_specs=p_specs + [b_spec] + [t_spec] * 3, out_specs=t_spec,
        out_shape=jax.ShapeDtypeStruct((S, D), BF16),
        compiler_params=_params(("parallel", "parallel")))(proj, proj, proj, b_gate, *br)


def _gate_bwd(proj, b_gate, br, dmerged):
    grid, p_specs, b_spec, t_spec = _gate_specs()

    def body(p0, p1, p2, b_ref, r0, r1, r2, dm_ref, e0, e1, e2, g0, g1, g2, db_ref):
        dm = dm_ref[...]
        rows = []
        for n, (p, r, e_ref, dg_ref) in enumerate(((p0, r0, e0, g0), (p1, r1, e1, g1), (p2, r2, e2, g2))):
            g = _sigmoid(p[...] + b_ref[n:n + 1, :])
            e_ref[...] = (dm * g).astype(BF16)
            dpre = dm * r[...] * g * (1.0 - g)
            dg_ref[...] = dpre.astype(BF16)
            rows.append(jnp.sum(dpre, axis=0, keepdims=True))
        db = _three_rows(rows)

        @pl.when(pl.program_id(1) == 0)
        def _():
            db_ref[...] = db

        @pl.when(pl.program_id(1) > 0)
        def _():
            db_ref[...] += db

    bf = jax.ShapeDtypeStruct((S, D), BF16)
    out = pl.pallas_call(
        body, name="gate_bwd", grid=grid, in_specs=p_specs + [b_spec] + [t_spec] * 4,
        out_specs=[t_spec] * 6 + [b_spec], out_shape=[bf] * 6 + [jax.ShapeDtypeStruct((3, D), F32)],
        compiler_params=_params(("parallel", "arbitrary")))(proj, proj, proj, b_gate, *br, dmerged)
    return out[:3], out[3:6], out[6]


TC = 256
N_FF_BLK = D_FF // TC
GELU_C = math.sqrt(2.0 / math.pi)


def _shift_down(x, n):
    rows = lax.broadcasted_iota(jnp.int32, x.shape, 0)
    return jnp.where(rows >= n, pltpu.roll(x, n, axis=0), 0.0)


def _shift_up(x, n):
    rows = lax.broadcasted_iota(jnp.int32, x.shape, 0)
    return jnp.where(rows < x.shape[0] - n, pltpu.roll(x, x.shape[0] - n, axis=0), 0.0)


def _conv(u, w, b):
    return w[2:3, :] * u + w[1:2, :] * _shift_down(u, 1) + w[0:1, :] * _shift_down(u, 2) + b


def _gelu_parts(x):
    inner = GELU_C * (x + 0.044715 * x * x * x)
    t = jnp.tanh(inner)
    gelu = 0.5 * x * (1.0 + t)
    dgelu = 0.5 * (1.0 + t) + 0.5 * x * (1.0 - t * t) * GELU_C * (1.0 + 3 * 0.044715 * x * x)
    return gelu, dgelu


def _conv_specs():
    ug = pl.BlockSpec((S, TC), lambda c: (0, c))
    uv = pl.BlockSpec((S, TC), lambda c: (0, N_FF_BLK + c))
    wg = pl.BlockSpec((3, TC), lambda c: (0, c))
    wv = pl.BlockSpec((3, TC), lambda c: (0, N_FF_BLK + c))
    bg = pl.BlockSpec((1, TC), lambda c: (0, c))
    bv = pl.BlockSpec((1, TC), lambda c: (0, N_FF_BLK + c))
    return ug, uv, wg, wv, bg, bv


def _conv_fwd(u, conv_w, conv_b):
    ug, uv, wg, wv, bg, bv = _conv_specs()

    def body(ug_ref, uv_ref, wg_ref, wv_ref, bg_ref, bv_ref, a_ref):
        gc = _conv(ug_ref[...], wg_ref[...], bg_ref[...])
        vc = _conv(uv_ref[...], wv_ref[...], bv_ref[...])
        a_ref[...] = (_gelu_parts(gc)[0] * vc).astype(BF16)

    return pl.pallas_call(
        body, name="conv_fwd", grid=(N_FF_BLK,), in_specs=[ug, uv, wg, wv, bg, bv], out_specs=ug,
        out_shape=jax.ShapeDtypeStruct((S, D_FF), BF16),
        compiler_params=_params(("parallel",)))(u, u, conv_w, conv_w, conv_b, conv_b)


def _conv_bwd(u, conv_w, conv_b, da):
    ug, uv, wg, wv, bg, bv = _conv_specs()

    def back(duc, u, w):
        du = w[2:3, :] * duc + w[1:2, :] * _shift_up(duc, 1) + w[0:1, :] * _shift_up(duc, 2)
        dw = _three_rows([jnp.sum(duc * _shift_down(u, 2), axis=0, keepdims=True),
                          jnp.sum(duc * _shift_down(u, 1), axis=0, keepdims=True),
                          jnp.sum(duc * u, axis=0, keepdims=True)])
        return du, dw, jnp.sum(duc, axis=0, keepdims=True)

    def body(ug_ref, uv_ref, wg_ref, wv_ref, bg_ref, bv_ref, da_ref, dug_ref, duv_ref, dwg_ref, dwv_ref, dbg_ref, dbv_ref):
        u_g, u_v = ug_ref[...], uv_ref[...]
        gc = _conv(u_g, wg_ref[...], bg_ref[...])
        vc = _conv(u_v, wv_ref[...], bv_ref[...])
        gelu, dgelu = _gelu_parts(gc)
        da = da_ref[...]
        du, dw, db = back(da * vc * dgelu, u_g, wg_ref[...])
        dug_ref[...] = du.astype(BF16)
        dwg_ref[...] = dw
        dbg_ref[...] = db
        du, dw, db = back(da * gelu, u_v, wv_ref[...])
        duv_ref[...] = du.astype(BF16)
        dwv_ref[...] = dw
        dbv_ref[...] = db

    return pl.pallas_call(
        body, name="conv_bwd", grid=(N_FF_BLK,), in_specs=[ug, uv, wg, wv, bg, bv, ug],
        out_specs=[ug, ug, wg, wg, bg, bg],
        out_shape=[jax.ShapeDtypeStruct((S, D_FF), BF16), jax.ShapeDtypeStruct((S, D_FF), BF16),
                   jax.ShapeDtypeStruct((3, D_FF), F32), jax.ShapeDtypeStruct((3, D_FF), F32),
                   jax.ShapeDtypeStruct((1, D_FF), F32), jax.ShapeDtypeStruct((1, D_FF), F32)],
        compiler_params=_params(("parallel",)))(u, u, conv_w, conv_w, conv_b, conv_b, da)


def _adamw(name, w, g, m, v):
    shape = w.shape
    cols = shape[-1]
    flat = [t.reshape(-1, cols) for t in (w, g, m, v)]
    r = flat[0].shape[0]
    tr = min(128, r)

    def body(w_ref, g_ref, m_ref, v_ref, d_ref, mo_ref, vo_ref):
        g = g_ref[...]
        m = ADAM_B1 * m_ref[...] + (1.0 - ADAM_B1) * g
        v = ADAM_B2 * v_ref[...] + (1.0 - ADAM_B2) * (g * g)
        m_hat = m / (1.0 - ADAM_B1 ** ADAM_STEP)
        v_hat = v / (1.0 - ADAM_B2 ** ADAM_STEP)
        d_ref[...] = -ADAM_LR * (m_hat / (jnp.sqrt(v_hat) + ADAM_EPS) + ADAM_WD * w_ref[...])
        mo_ref[...] = m
        vo_ref[...] = v

    spec = pl.BlockSpec((tr, cols), lambda i: (i, 0))
    outs = pl.pallas_call(
        body, name=name, grid=(pl.cdiv(r, tr),), in_specs=[spec] * 4, out_specs=[spec] * 3,
        out_shape=[jax.ShapeDtypeStruct((r, cols), F32)] * 3, compiler_params=_params(("parallel",)))(*flat)
    return [t.reshape(shape) for t in outs]


def _place():
    x, y, c = lax.axis_index("x"), lax.axis_index("y"), lax.axis_index("c")
    chips = [(1 - x, y), (x, 1 - y), (1 - x, 1 - y)]
    return x, y, c, chips


def _scalars(*vals):
    return jnp.stack([jnp.asarray(v, jnp.int32) for v in vals])


HBM = pl.BlockSpec(memory_space=pltpu.HBM)
SEM = pl.BlockSpec(memory_space=pltpu.SEMAPHORE)
SPLIT_COPY = pltpu.CompilerParams(has_side_effects=pltpu.SideEffectType.DATAFLOW_SIDE_EFFECTING)


def _in_hbm(x):
    return pltpu.with_memory_space_constraint(x, pltpu.HBM)


def _cast_into_slot(name, w, layer, chip):
    _, k, n4 = w.shape
    tr = max(t for t in range(16, 257, 16) if k % t == 0)

    def body(chip_ref, w_ref, o_ref):
        o_ref[...] = w_ref[...].astype(BF16)

    return pl.pallas_call(
        body, name=name,
        grid_spec=pltpu.PrefetchScalarGridSpec(
            num_scalar_prefetch=1, grid=(k // tr,),
            in_specs=[pl.BlockSpec((None, tr, n4), lambda i, s: (layer, i, 0))],
            out_specs=pl.BlockSpec((None, tr, n4), lambda i, s: (s[0], i, 0))),
        out_shape=jax.ShapeDtypeStruct((N_CHIPS, k, n4), BF16),
        compiler_params=_params(("parallel",)))(_scalars(chip), w)


def _gather_copy(buf_ref, k, from_chip, send_sem, recv_sem, chips, c):
    rows = buf_ref.at[from_chip]
    return pltpu.make_async_remote_copy(src_ref=rows, dst_ref=rows, send_sem=send_sem, recv_sem=recv_sem,
                                        device_id=(*chips[k], c), device_id_type=MESH)


def _gather_start(bufs, groups):
    n, ng = len(bufs), len(groups)
    where = {a: (gi, e) for gi, g in enumerate(groups) for e, a in enumerate(g)}

    def body(*refs):
        ins, sems, token = refs[:n], refs[n:n + 2 * ng], refs[-1]
        x, y, c, chips = _place()
        for a in range(n):
            gi, e = where[a]
            for k in range(3):
                _gather_copy(ins[a], k, 2 * x + y, sems[2 * gi].at[3 * e + k], sems[2 * gi + 1].at[3 * e + k],
                             chips, c).start()
        token[...] = jnp.zeros_like(token)

    out_shape = [pltpu.SemaphoreType.DMA((3 * len(g),)) for g in groups for _ in range(2)]
    out_shape += [pltpu.HBM(b.shape, b.dtype) for b in bufs] + [jax.ShapeDtypeStruct((8, 128), F32)]
    out = pl.pallas_call(
        body, name="gather_start", in_specs=[HBM] * n,
        out_specs=[SEM] * (2 * ng) + [HBM] * n + [pl.BlockSpec(memory_space=pltpu.VMEM)], out_shape=out_shape,
        input_output_aliases={a: 2 * ng + a for a in range(n)}, compiler_params=SPLIT_COPY)(*[_in_hbm(b) for b in bufs])
    sems = [(out[2 * gi], out[2 * gi + 1]) for gi in range(ng)]
    return sems, list(out[2 * ng:2 * ng + n]), out[-1]


def _gather_wait(name, bufs, send, recv, after):
    n = len(bufs)

    def body(*refs):
        ins, send_sem, recv_sem = refs[:n], refs[n], refs[n + 1]
        x, y, c, chips = _place()
        for e in range(n):
            for k in range(3):
                sems = (send_sem.at[3 * e + k], recv_sem.at[3 * e + k])
                _gather_copy(ins[e], k, 2 * x + y, *sems, chips, c).wait_send()
                _gather_copy(ins[e], k, 2 * chips[k][0] + chips[k][1], *sems, chips, c).wait_recv()

    return pl.pallas_call(
        body, name=name, in_specs=[HBM] * n + [SEM, SEM, ANY], out_specs=[HBM] * n,
        out_shape=[pltpu.HBM(b.shape, b.dtype) for b in bufs],
        input_output_aliases={a: a for a in range(n)}, compiler_params=SPLIT_COPY)(*bufs, send, recv, after)


def _reduce_copy(g_ref, land_ref, mask, send_sem, recv_sem, x, y, c, sending):
    px, py, pc = x ^ ((mask >> 2) & 1), y ^ ((mask >> 1) & 1), c ^ (mask & 1)
    half = g_ref.shape[1] // 2
    src = g_ref.at[2 * px + py, pl.ds(pl.multiple_of(pc * half, half), half)]
    dst = land_ref.at[4 * x + 2 * y + c] if sending else land_ref.at[4 * px + 2 * py + pc]
    return pltpu.make_async_remote_copy(src_ref=src, dst_ref=dst, send_sem=send_sem, recv_sem=recv_sem,
                                        device_id=(px, py, pc), device_id_type=MESH)


def _reduce_start(name, grads):
    n = len(grads)
    lands = [lax.empty((N_DEV, g.shape[1] // 2, g.shape[2]), g.dtype) for g in grads]

    def body(*refs):
        gs, ls, send_sem, recv_sem = refs[:n], refs[n:2 * n], refs[2 * n], refs[2 * n + 1]
        x, y, c, _ = _place()
        for a in range(n):
            for mask in range(1, N_DEV):
                s = (N_DEV - 1) * a + mask - 1
                _reduce_copy(gs[a], ls[a], mask, send_sem.at[s], recv_sem.at[s], x, y, c, True).start()
        refs[-1][...] = jnp.zeros_like(refs[-1])

    sem = pltpu.SemaphoreType.DMA((n * (N_DEV - 1),))
    out = pl.pallas_call(
        body, name=name, in_specs=[HBM] * (2 * n),
        out_specs=[SEM, SEM] + [HBM] * (2 * n) + [pl.BlockSpec(memory_space=pltpu.VMEM)],
        out_shape=[sem, sem] + [pltpu.HBM(t.shape, t.dtype) for t in grads + lands] + [jax.ShapeDtypeStruct((8, 128), F32)],
        input_output_aliases={a: 2 + a for a in range(2 * n)}, compiler_params=SPLIT_COPY)(
            *[_in_hbm(t) for t in grads + lands])
    return out[0], out[1], list(out[2:2 + n]), list(out[2 + n:2 + 2 * n]), out[-1]


def _reduce_wait(name, send, recv, grads, lands, after):
    n = len(grads)

    def body(*refs):
        gs, ls, send_sem, recv_sem = refs[:n], refs[n:2 * n], refs[2 * n], refs[2 * n + 1]
        x, y, c, _ = _place()
        for a in range(n):
            for mask in range(1, N_DEV):
                s = (N_DEV - 1) * a + mask - 1
                sems = (send_sem.at[s], recv_sem.at[s])
                _reduce_copy(gs[a], ls[a], mask, *sems, x, y, c, True).wait_send()
                _reduce_copy(gs[a], ls[a], mask, *sems, x, y, c, False).wait_recv()

    out = pl.pallas_call(
        body, name=name, in_specs=[HBM] * (2 * n) + [SEM, SEM, ANY], out_specs=[HBM] * (2 * n),
        out_shape=[pltpu.HBM(t.shape, t.dtype) for t in grads + lands],
        input_output_aliases={a: a for a in range(2 * n)}, compiler_params=SPLIT_COPY)(*grads, *lands, send, recv, after)
    return list(out[:n]), list(out[n:])


def _reduce_sum(name, g, land, layer, into, chip, c):
    _, k4, n4 = g.shape
    half = k4 // 2
    tr = max(t for t in range(16, 513, 16) if half % t == 0)
    per = half // tr
    me = 2 * chip + c

    def body(s_ref, own_ref, *refs):
        total = own_ref[...].astype(F32)
        for ref in refs[:N_DEV - 1]:
            total = total + ref[...].astype(F32)
        refs[-1][...] = total

    in_specs = [pl.BlockSpec((None, tr, n4), lambda i, s: (s[0], s[1] * per + i, 0))]
    in_specs += [pl.BlockSpec((None, tr, n4), functools.partial(lambda i, s, m: (s[1 + m], i, 0), m=m))
                 for m in range(1, N_DEV)]
    ins = [g] + [land] * (N_DEV - 1)
    aliases = {}
    if into is not None:
        in_specs, ins, aliases = in_specs + [ANY], ins + [into], {1 + N_DEV: 0}
    return pl.pallas_call(
        body, name=name,
        grid_spec=pltpu.PrefetchScalarGridSpec(
            num_scalar_prefetch=1, grid=(per,), in_specs=in_specs,
            out_specs=pl.BlockSpec((None, tr, n4), lambda i, s: (layer, s[1] * per + i, 0))),
        out_shape=jax.ShapeDtypeStruct((DEPTH, k4, n4), F32), input_output_aliases=aliases,
        compiler_params=_params(("parallel",)))(_scalars(chip, c, *[me ^ m for m in range(1, N_DEV)]), *ins)


def _join_halves(bufs):
    n = len(bufs)

    def body(*refs):
        ins, outs = refs[:n], refs[n:2 * n]
        send_sem, recv_sem = refs[2 * n:]
        x, y, c, _ = _place()

        def rows(ref, which):
            half = ref.shape[1] // 2
            return ref.at[:, pl.ds(pl.multiple_of(which * half, half), half)]

        sends = [pltpu.make_async_remote_copy(
            src_ref=rows(ins[a], c), dst_ref=rows(outs[a], c), send_sem=send_sem.at[a], recv_sem=recv_sem.at[a],
            device_id=(x, y, 1 - c), device_id_type=MESH) for a in range(n)]
        for cp in sends:
            cp.start()
        for a in range(n):
            sends[a].wait_send()
            pltpu.make_async_remote_copy(
                src_ref=rows(ins[a], c), dst_ref=rows(outs[a], 1 - c), send_sem=send_sem.at[a], recv_sem=recv_sem.at[a],
                device_id=(x, y, 1 - c), device_id_type=MESH).wait_recv()

    return pl.pallas_call(
        body, name="join_halves", in_specs=[ANY] * n, out_specs=[ANY] * n,
        out_shape=[jax.ShapeDtypeStruct(b.shape, b.dtype) for b in bufs],
        input_output_aliases={a: a for a in range(n)},
        scratch_shapes=[pltpu.SemaphoreType.DMA((n,)), pltpu.SemaphoreType.DMA((n,))],
    )(*bufs)


def _all_reduce_small(block):
    r = block.shape[0]

    def body(x_ref, out_ref, slots, send_sem, recv_sem):
        x, y, c, _ = _place()
        me = 4 * x + 2 * y + c
        slots[me] = x_ref[...]
        sends = []
        for mask in range(1, N_DEV):
            fx, fy, fc = (mask >> 2) & 1, (mask >> 1) & 1, mask & 1
            peer = (x ^ fx, y ^ fy, c ^ fc)
            cp = pltpu.make_async_remote_copy(
                src_ref=x_ref, dst_ref=slots.at[me], send_sem=send_sem.at[mask - 1], recv_sem=recv_sem.at[mask - 1],
                device_id=peer, device_id_type=MESH)
            cp.start()
            sends.append(cp)
        for mask in range(1, N_DEV):
            src = me ^ mask
            pltpu.make_async_remote_copy(
                src_ref=x_ref, dst_ref=slots.at[src], send_sem=send_sem.at[mask - 1], recv_sem=recv_sem.at[mask - 1],
                device_id=(x, y, c), device_id_type=MESH).wait_recv()
        for cp in sends:
            cp.wait_send()
        total = slots[0]
        for d in range(1, N_DEV):
            total = total + slots[d]
        out_ref[...] = total

    vmem = pl.BlockSpec(memory_space=pltpu.VMEM)
    return pl.pallas_call(
        body, name="all_reduce_small", in_specs=[vmem], out_specs=vmem,
        out_shape=jax.ShapeDtypeStruct((r, 128), F32),
        scratch_shapes=[pltpu.VMEM((N_DEV, r, 128), F32), pltpu.SemaphoreType.DMA((N_DEV - 1,)),
                        pltpu.SemaphoreType.DMA((N_DEV - 1,))],
        compiler_params=pltpu.CompilerParams(vmem_limit_bytes=VMEM_LIMIT))(block)


B_Q_COL = 2304 // 128
B_K0, B_V0, B_END = 2816, 2944, 3072


def _full_cols(w_g):
    return w_g.transpose(1, 0, 2).reshape(w_g.shape[1], -1)


def _group_src(proj, g):
    if A_GROUPS[g][1] == 1:
        return ((proj, 2 * g), (proj, 6 + 2 * g), (proj, 12 + 2 * g))
    packed = jnp.concatenate([proj[:, t * 768 + g * 256:t * 768 + (g + 1) * 256] for t in range(3)], axis=1)
    return ((packed, 0), (packed, 2), (packed, 4))


def _kv_expand(kv):
    return jnp.broadcast_to(kv.reshape(S, 2, 1, HD), (S, 2, 4, HD)).reshape(S, 8 * HD)


def _kv_reduce(dkv):
    return dkv.reshape(S, 2, 4, HD).sum(axis=2).reshape(S, 2 * HD)


def _mixer_fwd(h1, wget, rel_bias, sinks_l, bidx):
    w = dict(wget(0, h1))
    proj = _mm_nt("proj_in", h1, w["w_in"], F32, tn=1152)
    no_sinks = jnp.full((4,), NEG, F32)
    srcs = [_group_src(proj, g) for g in range(3)]
    o_g, lse_g = [], []
    for g, (_, d) in enumerate(A_GROUPS):
        o, lse = _band_fwd("band_fwd_g%d" % g, d, 2, BLK, 4 * g, srcs[g], bidx[g], rel_bias, no_sinks)
        o_g.append(o)
        lse_g.append(lse)
    o_a32, o_a, lse_a = _comb_fwd(o_g, lse_g)
    src_b = ((proj, B_Q_COL), (_kv_expand(proj[:, B_K0:B_V0]), 0), (_kv_expand(proj[:, B_V0:B_END]), 0))
    o_b32, lse_b = _band_fwd("band_fwd_b", 1, 4, BLK - 1, N_A, src_b, bidx[3], rel_bias, sinks_l)
    o_b = o_b32.astype(BF16)
    o_c32, o_c, tot_c = _sb_fwd(proj)
    w.update(wget(1, o_c32))
    br = [_mm_nn("branch_a", o_a, w["w_br_a"], F32), _mm_nn("branch_b", o_b, w["w_br_b"], F32),
          _mm_nn("branch_c", o_c, w["w_br_c"], F32)]
    merged = _gate_fwd(proj, w["b_gate"], br)
    mo = _mm_nn("out_proj", merged, w["w_out"], F32)
    saved = dict(proj=proj, srcs=srcs, src_b=src_b, o_a32=o_a32, lse_a=lse_a, o_b32=o_b32, lse_b=lse_b, tot_c=tot_c,
                 o_a=o_a, o_b=o_b, o_c=o_c, br=br, merged=merged)
    return mo, saved, w


def _mixer_bwd(d_mo, h1, w, sv, rel_bias, sinks_l, bidx, stats_in, emit):
    grads = {}
    dmerged = _mm_nt("out_proj_dx", d_mo, w["w_out"], F32)
    grads["w_out"] = _mm_tn_sharded("out_proj_dw", sv["merged"], d_mo, True)
    e, dgate, db_gate = _gate_bwd(sv["proj"], w["b_gate"], sv["br"], dmerged)
    grads["b_gate"] = db_gate
    d_o = {}
    for n, name in enumerate("abc"):
        d_o[name] = _mm_nt("branch_%s_dx" % name, e[n], w["w_br_" + name], F32)
        grads["w_br_" + name] = _mm_tn_sharded("branch_%s_dw" % name, sv["o_" + name], e[n], False)
    zero = emit(1, grads)
    no_sinks = jnp.full((4,), NEG, F32) + zero[0]
    dqs, dks, dvs, stats = [], [], [], []
    for g, (_, d) in enumerate(A_GROUPS):
        dq, dk, dv, st = _band_bwd("band_bwd_g%d" % g, d, 2, BLK, 4 * g, sv["srcs"][g], bidx[g], rel_bias, no_sinks,
                                   sv["o_a32"], sv["lse_a"], d_o["a"], stats_in[4 * g:4 * g + 4])
        dqs.append(dq)
        dks.append(dk)
        dvs.append(dv)
        stats.append(st)
    dq_b, dk_x, dv_x, st = _band_bwd("band_bwd_b", 1, 4, BLK - 1, N_A, sv["src_b"], bidx[3], rel_bias, sinks_l,
                                     sv["o_b32"], sv["lse_b"], d_o["b"], stats_in[N_A:])
    stats = jnp.concatenate(stats + [st], axis=0)
    dcq, dck, dcv = _sb_bwd(sv["proj"], sv["tot_c"], d_o["c"])
    cols = dqs + dks + dvs + [dq_b, _kv_reduce(dk_x), _kv_reduce(dv_x), dcq, dck, dcv]
    dproj = jnp.concatenate([t.astype(BF16) for t in cols] + list(dgate), axis=1)
    grads["w_in"] = _mm_tn("proj_in_dw", dproj, h1, BF16, tm=768).reshape(N_CHIPS, IN_SHARD, D)
    zero = emit(2, grads)
    dh1 = _mm_nn("proj_in_dx", dproj, w["w_in"], F32)
    return dh1, grads, stats, zero


def _ffn_fwd(h2, w):
    u = _mm_nn("ffn_up", h2, w["w_up"], F32, tn=1024)
    a = _conv_fwd(u, w["conv_w"], w["conv_b"])
    dn = _mm_nn("ffn_down", a, w["w_down"], F32)
    return dn, dict(u=u, a=a)


def _ffn_bwd(d_dn, h2, w, sv):
    grads = {}
    da = _mm_nt("ffn_down_dx", d_dn, w["w_down"], F32, tn=1024)
    grads["w_down"] = _mm_tn_sharded("ffn_down_dw", sv["a"], d_dn, True)
    dug, duv, dwg, dwv, dbg, dbv = _conv_bwd(sv["u"], w["conv_w"], w["conv_b"], da)
    du = jnp.concatenate([dug, duv], axis=1)
    grads["conv_w"] = jnp.concatenate([dwg, dwv], axis=1)
    grads["conv_b"] = jnp.concatenate([dbg, dbv], axis=1)
    dh2 = _mm_nt("ffn_up_dx", du, w["w_up"], F32)
    grads["w_up"] = _mm_tn_sharded("ffn_up_dw", h2, du, False, tn=1024)
    return dh2, grads


BIG = ("w_in", "w_br_a", "w_br_b", "w_br_c", "w_out", "w_up", "w_down")


def _shard_view(name, w):
    return jnp.swapaxes(w, 1, 2) if name == "w_in" else w
WEIGHT_GROUPS = (("w_in", "b_gate"), ("w_br_a", "w_br_b", "w_br_c", "w_out"), ("w_up", "conv_w", "w_down"))
GRAD_GROUPS = (("w_down", "w_up"), ("w_out", "w_br_a", "w_br_b", "w_br_c"), ("w_in",))
SMALL_ROWS = (("rel_bias", 8), ("attn_pre_norm", 16), ("attn_post_norm", 16), ("ffn_pre_norm", 16), ("ffn_post_norm", 16),
              ("sinks", 8), ("conv_b", 128), ("b_gate", 48), ("conv_w", 384), ("loss", 8))


def _pack_small(vals):
    rows = []
    for name, n in SMALL_ROWS:
        flat = vals[name].reshape(-1).astype(F32)
        rows.append(jnp.pad(flat, (0, n * 128 - flat.shape[0])).reshape(n, 128))
    return jnp.concatenate(rows, axis=0)


def _unpack_small(block, shapes):
    out, row = {}, 0
    for name, n in SMALL_ROWS:
        size = int(np.prod(shapes[name]))
        out[name] = block[row:row + n].reshape(-1)[:size].reshape(shapes[name])
        row += n
    return out


def kernel(x, rel_bias, attn_pre_norm, w_in, b_gate, sinks, w_br_a, w_br_b, w_br_c, w_out, attn_post_norm, ffn_pre_norm, w_up, conv_w, conv_b, w_down, ffn_post_norm, loss_target, m_rel_bias, m_attn_pre_norm, m_w_in, m_b_gate, m_sinks, m_w_br_a, m_w_br_b, m_w_br_c, m_w_out, m_attn_post_norm, m_ffn_pre_norm, m_w_up, m_conv_w, m_conv_b, m_w_down, m_ffn_post_norm, v_rel_bias, v_attn_pre_norm, v_w_in, v_b_gate, v_sinks, v_w_br_a, v_w_br_b, v_w_br_c, v_w_out, v_attn_post_norm, v_ffn_pre_norm, v_w_up, v_conv_w, v_conv_b, v_w_down, v_ffn_post_norm):
    names = ("rel_bias", "attn_pre_norm", "w_in", "b_gate", "sinks", "w_br_a", "w_br_b", "w_br_c", "w_out",
             "attn_post_norm", "ffn_pre_norm", "w_up", "conv_w", "conv_b", "w_down", "ffn_post_norm")
    weights = dict(zip(names, (rel_bias, attn_pre_norm, w_in, b_gate, sinks, w_br_a, w_br_b, w_br_c, w_out,
                               attn_post_norm, ffn_pre_norm, w_up, conv_w, conv_b, w_down, ffn_post_norm)))
    mom1 = dict(zip(names, (m_rel_bias, m_attn_pre_norm, m_w_in, m_b_gate, m_sinks, m_w_br_a, m_w_br_b, m_w_br_c,
                            m_w_out, m_attn_post_norm, m_ffn_pre_norm, m_w_up, m_conv_w, m_conv_b, m_w_down,
                            m_ffn_post_norm)))
    mom2 = dict(zip(names, (v_rel_bias, v_attn_pre_norm, v_w_in, v_b_gate, v_sinks, v_w_br_a, v_w_br_b, v_w_br_c,
                            v_w_out, v_attn_post_norm, v_ffn_pre_norm, v_w_up, v_conv_w, v_conv_b, v_w_down,
                            v_ffn_post_norm)))

    chip = 2 * lax.axis_index("x") + lax.axis_index("y")
    core = lax.axis_index("c")

    keys = [(n, l) for l in range(DEPTH) for group in WEIGHT_GROUPS for n in group]
    bufs = []
    for n, l in keys:
        if n in BIG:
            bufs.append(_cast_into_slot("cast_" + n, _shard_view(n, weights[n]), l, chip))
        else:
            shard = weights[n][l]
            bufs.append(lax.dynamic_update_slice(jnp.zeros((N_CHIPS,) + shard.shape, F32), shard[None],
                                                 (chip, jnp.int32(0), jnp.int32(0))))
    groups = [[keys.index((n, l)) for n in group] for l in range(DEPTH) for group in WEIGHT_GROUPS]
    sems, in_flight, _ = _gather_start(bufs, groups)

    def wget(l, gi, after):
        g = l * len(WEIGHT_GROUPS) + gi
        got = _gather_wait("gather_wait_%d_%d" % (l, gi), [in_flight[a] for a in groups[g]], *sems[g], after)
        out = {}
        for n, buf in zip(WEIGHT_GROUPS[gi], got):
            out[n] = buf.reshape(-1, buf.shape[-1]) if n in ("w_in", "w_out", "w_down") else _full_cols(buf)
        if gi == len(WEIGHT_GROUPS) - 1:
            out["conv_b"] = conv_b[l:l + 1]
        return out

    pending = []

    def emit(l, gi, grads):
        group = GRAD_GROUPS[gi]
        *started, token = _reduce_start("reduce_start_%d_%d" % (l, gi), [grads[n] for n in group])
        pending.append((l, group) + tuple(started))
        return token[:1, :1]

    local = _local_step(x.reshape(S, D), loss_target.reshape(S, D), wget, emit, rel_bias, sinks, attn_pre_norm,
                        attn_post_norm, ffn_pre_norm, ffn_post_norm)
    return _reduce_and_update(x.shape, names, weights, mom1, mom2, chip, core, pending, *local)


def _local_step(xs, target, wget, emit, rel_bias, sinks, attn_pre_norm, attn_post_norm, ffn_pre_norm, ffn_post_norm):
    bidx = jnp.asarray(_bucket_maps())

    saved, layers = [], []
    h1 = _rms_fwd("pre_norm_first", xs, attn_pre_norm[0:1])
    x_in = xs
    for l in range(DEPTH):
        mo, sv_mix, w = _mixer_fwd(h1, functools.partial(wget, l), rel_bias, sinks[l], bidx)
        x_mid, h2 = _post_pre_fwd("post_attn_norm", x_in, mo, attn_post_norm[l:l + 1], ffn_pre_norm[l:l + 1])
        w.update(wget(l, 2, h2))
        dn, sv_ffn = _ffn_fwd(h2, w)
        g_next = attn_pre_norm[l + 1:l + 2] if l + 1 < DEPTH else None
        x_out, h1_next = _post_pre_fwd("post_ffn_norm" if l + 1 < DEPTH else "post_ffn_norm_last", x_mid, dn,
                                       ffn_post_norm[l:l + 1], g_next)
        saved.append(dict(x_in=x_in, h1=h1, mo=mo, x_mid=x_mid, h2=h2, dn=dn, mix=sv_mix, ffn=sv_ffn))
        layers.append(w)
        x_in, h1 = x_out, h1_next

    loss_row, dres = _loss_kernel(x_in, target)

    small = [None] * DEPTH
    stats = jnp.zeros((N_BAND_Q, 8, 128), F32)
    dh_next = None
    for l in reversed(range(DEPTH)):
        w, sv = layers[l], saved[l]
        if l + 1 < DEPTH:
            pre = (saved[l + 1]["x_in"], attn_pre_norm[l + 1:l + 2] + zero, dh_next)
            dres, d_dn, dg_pre_next, dg_fpost = _norm_bwd("post_ffn_norm_bwd", dres, pre,
                                                          (sv["dn"], ffn_post_norm[l:l + 1]))
            small[l + 1]["attn_pre_norm"] = dg_pre_next
        else:
            dres, d_dn, _, dg_fpost = _norm_bwd("post_ffn_norm_last_bwd", dres, None, (sv["dn"], ffn_post_norm[l:l + 1]))
        dh2, g_ffn = _ffn_bwd(d_dn, sv["h2"], w, sv["ffn"])
        zero = emit(l, 0, g_ffn)
        dres, d_mo, dg_fpre, dg_apost = _norm_bwd("post_attn_norm_bwd", dres,
                                                  (sv["x_mid"], ffn_pre_norm[l:l + 1] + zero, dh2),
                                                  (sv["mo"], attn_post_norm[l:l + 1]))
        dh_next, g_mix, stats, zero = _mixer_bwd(d_mo, sv["h1"], w, sv["mix"], rel_bias, sinks[l], bidx, stats,
                                                 functools.partial(emit, l))
        small[l] = dict(ffn_post_norm=dg_fpost, ffn_pre_norm=dg_fpre, attn_post_norm=dg_apost,
                        sinks=stats[N_A:, 1, 0], conv_b=g_ffn["conv_b"], b_gate=g_mix["b_gate"], conv_w=g_ffn["conv_w"])
    grad_x, _, dg_pre0, _ = _norm_bwd("pre_norm_first_bwd", dres, (saved[0]["x_in"], attn_pre_norm[0:1] + zero, dh_next),
                                      None)
    small[0]["attn_pre_norm"] = dg_pre0
    return loss_row, grad_x, small, stats


def _reduce_and_update(x_shape, names, weights, mom1, mom2, chip, core, pending, loss_row, grad_x, small, stats):
    small_vals = {n: jnp.stack([small[l][n].reshape(weights[n].shape[1:]) for l in range(DEPTH)])
                  for n in ("attn_pre_norm", "attn_post_norm", "ffn_pre_norm", "ffn_post_norm", "conv_b", "sinks")}
    small_vals["b_gate"] = jnp.stack([small[l]["b_gate"] for l in range(DEPTH)])
    small_vals["conv_w"] = jnp.stack([small[l]["conv_w"] for l in range(DEPTH)])
    small_vals["rel_bias"] = stats[:, 0, :NUM_BUCKETS].T
    small_vals["loss"] = loss_row[0, :1]
    shapes = {n: v.shape for n, v in small_vals.items()}
    reduced = _unpack_small(_all_reduce_small(_pack_small(small_vals)), shapes)
    reduced["b_gate"] = lax.dynamic_slice_in_dim(reduced["b_gate"], chip * (D // N_CHIPS), D // N_CHIPS, axis=2)
    reduced["conv_w"] = lax.dynamic_slice_in_dim(reduced["conv_w"], chip * (2 * D_FF // N_CHIPS), 2 * D_FF // N_CHIPS, axis=2)

    summed = {}
    for l, group, send, recv, gs, lands in pending:
        gs, lands = _reduce_wait("reduce_wait_%d_%s" % (l, group[0]), send, recv, gs, lands, grad_x)
        for n, g, land in zip(group, gs, lands):
            summed[n] = _reduce_sum("reduce_sum_%d_%s" % (l, n), g, land, l, summed.get(n), chip, core)
    full = _join_halves([summed[n] for n in BIG])
    grads = dict(zip(BIG, full))
    for n in names:
        if n not in grads:
            grads[n] = reduced[n].reshape(weights[n].shape)

    delta, new_m, new_v = {}, {}, {}
    for n in names:
        delta[n], new_m[n], new_v[n] = _adamw("adamw_" + n, _shard_view(n, weights[n]), grads[n],
                                              _shard_view(n, mom1[n]), _shard_view(n, mom2[n]))
    for out in (grads, delta, new_m, new_v):
        out["w_in"] = _shard_view("w_in", out["w_in"])

    loss = reduced["loss"].reshape(())
    return (loss, grad_x.reshape(x_shape), *[grads[n] for n in names], *[delta[n] for n in names],
            *[new_m[n] for n in names], *[new_v[n] for n in names])
```

```python
import functools
import math

import numpy as np
import jax
import jax.numpy as jnp
from jax import lax
from jax.experimental import pallas as pl
from jax.experimental.pallas import tpu as pltpu

F32 = jnp.float32
BF16 = jnp.bfloat16

S = 2048
D = 1024
DEPTH = 2
HD = 64
BLK = 128
NQB = S // BLK
A_GROUPS = ((128, 1), (512, 4), (2048, 16))
N_BAND_Q = 20
N_A = 12
NUM_BUCKETS = 32
MAX_DISTANCE = 2048
D_FF = 4096
IN_COLS = 6912
IN_SHARD = IN_COLS // 4
OFF_GATE = 3840
EPS = 1e-6
SCALE = HD ** -0.5
NEG = -1e30
N_CHIPS = 4
N_DEV = 8

ADAM_LR = 0.001
ADAM_B1 = 0.9
ADAM_B2 = 0.999
ADAM_EPS = 1e-08
ADAM_WD = 0.01
ADAM_STEP = 10

VMEM_LIMIT = 56 * 1024 * 1024

NN = (((1,), (0,)), ((), ()))
NT = (((1,), (1,)), ((), ()))
TN = (((0,), (0,)), ((), ()))

MESH = pl.DeviceIdType.MESH
ANY = pl.BlockSpec(memory_space=pl.ANY)


def _dot(a, b, dims):
    return lax.dot_general(a, b, dims, preferred_element_type=F32)


def _params(sem):
    return pltpu.CompilerParams(dimension_semantics=sem, vmem_limit_bytes=VMEM_LIMIT)


def _matmul(name, a, b, out_shape, out_dtype, grid, a_spec, b_spec, o_spec, dims, acc_shape):
    nk = grid[-1]

    def body(a_ref, b_ref, o_ref, *scratch):
        part = _dot(a_ref[...].astype(BF16), b_ref[...].astype(BF16), dims)
        if nk == 1:
            o_ref[...] = part.astype(o_ref.dtype)
            return
        acc_ref, = scratch
        k = pl.program_id(len(grid) - 1)

        @pl.when(k == 0)
        def _():
            acc_ref[...] = part

        @pl.when(k > 0)
        def _():
            acc_ref[...] += part

        @pl.when(k == nk - 1)
        def _():
            o_ref[...] = acc_ref[...].astype(o_ref.dtype)

    scratch = [] if nk == 1 else [pltpu.VMEM(acc_shape, F32)]
    sem = ("parallel",) * (len(grid) - 1) + ("arbitrary",)
    return pl.pallas_call(
        body, name=name, grid=grid, in_specs=[a_spec, b_spec], out_specs=o_spec,
        out_shape=jax.ShapeDtypeStruct(out_shape, out_dtype), scratch_shapes=scratch,
        compiler_params=_params(sem))(a, b)


FULL_K = 8192


def _mm_tn_sharded(name, a, b, row_sharded, tm=512, tn=512, tk=FULL_K):
    k, m = a.shape
    n = b.shape[1]
    m4, n4 = (m // N_CHIPS, n) if row_sharded else (m, n // N_CHIPS)
    tm, tn, tk = min(tm, m4), min(tn, n4), min(tk, k)
    per_m, per_n = m4 // tm, n4 // tn
    if row_sharded:
        o_map = lambda i, j, l: (i // per_m, i % per_m, j)
    else:
        o_map = lambda i, j, l: (j // per_n, i, j % per_n)
    return _matmul(name, a, b, (N_CHIPS, m4, n4), BF16, (m // tm, n // tn, k // tk),
                   pl.BlockSpec((tk, tm), lambda i, j, l: (l, i)),
                   pl.BlockSpec((tk, tn), lambda i, j, l: (l, j)),
                   pl.BlockSpec((None, tm, tn), o_map), TN, (tm, tn))


def _mm_nn(name, a, b, out_dtype, tm=512, tn=512, tk=FULL_K):
    m, k = a.shape
    n = b.shape[1]
    tm, tn, tk = min(tm, m), min(tn, n), min(tk, k)
    return _matmul(name, a, b, (m, n), out_dtype, (m // tm, n // tn, k // tk),
                   pl.BlockSpec((tm, tk), lambda i, j, l: (i, l)),
                   pl.BlockSpec((tk, tn), lambda i, j, l: (l, j)),
                   pl.BlockSpec((tm, tn), lambda i, j, l: (i, j)), NN, (tm, tn))


def _mm_nt(name, a, b, out_dtype, tm=512, tn=512, tk=FULL_K):
    m, k = a.shape
    n = b.shape[0]
    tm, tn, tk = min(tm, m), min(tn, n), min(tk, k)
    return _matmul(name, a, b, (m, n), out_dtype, (m // tm, n // tn, k // tk),
                   pl.BlockSpec((tm, tk), lambda i, j, l: (i, l)),
                   pl.BlockSpec((tn, tk), lambda i, j, l: (j, l)),
                   pl.BlockSpec((tm, tn), lambda i, j, l: (i, j)), NT, (tm, tn))


def _mm_tn(name, a, b, out_dtype, tm=512, tn=512, tk=FULL_K):
    k, m = a.shape
    n = b.shape[1]
    tm, tn, tk = min(tm, m), min(tn, n), min(tk, k)
    return _matmul(name, a, b, (m, n), out_dtype, (m // tm, n // tn, k // tk),
                   pl.BlockSpec((tk, tm), lambda i, j, l: (l, i)),
                   pl.BlockSpec((tk, tn), lambda i, j, l: (l, j)),
                   pl.BlockSpec((tm, tn), lambda i, j, l: (i, j)), TN, (tm, tn))


TR = 256


def _row_spec(width=D):
    return pl.BlockSpec((TR, width), lambda i: (i, 0))


def _vec_spec(width=D):
    return pl.BlockSpec((1, width), lambda i: (0, 0))


def _rms(x, g):
    r = lax.rsqrt(jnp.mean(x * x, axis=-1, keepdims=True) + EPS)
    return x * r * g


def _rms_fwd(name, x, g):
    def body(x_ref, g_ref, h_ref):
        h_ref[...] = _rms(x_ref[...], g_ref[...]).astype(BF16)

    return pl.pallas_call(
        body, name=name, grid=(S // TR,), in_specs=[_row_spec(), _vec_spec()], out_specs=_row_spec(),
        out_shape=jax.ShapeDtypeStruct((S, D), BF16), compiler_params=_params(("parallel",)))(x, g)


def _post_pre_fwd(name, x, y, g_post, g_pre):
    has_pre = g_pre is not None

    def body(*refs):
        if has_pre:
            x_ref, y_ref, gp_ref, gn_ref, xn_ref, h_ref = refs
        else:
            x_ref, y_ref, gp_ref, xn_ref = refs
        xn = x_ref[...] + _rms(y_ref[...], gp_ref[...])
        xn_ref[...] = xn
        if has_pre:
            h_ref[...] = _rms(xn, gn_ref[...]).astype(BF16)

    ins = [x, y, g_post] + ([g_pre] if has_pre else [])
    in_specs = [_row_spec(), _row_spec(), _vec_spec()] + ([_vec_spec()] if has_pre else [])
    out_shape = [jax.ShapeDtypeStruct((S, D), F32)] + ([jax.ShapeDtypeStruct((S, D), BF16)] if has_pre else [])
    out_specs = [_row_spec()] + ([_row_spec()] if has_pre else [])
    out = pl.pallas_call(
        body, name=name, grid=(S // TR,), in_specs=in_specs, out_specs=out_specs, out_shape=out_shape,
        compiler_params=_params(("parallel",)))(*ins)
    return out if has_pre else (out[0], None)


def _rms_bwd_math(x, g, dy):
    r = lax.rsqrt(jnp.mean(x * x, axis=-1, keepdims=True) + EPS)
    n = x * r
    dn = dy * g
    dx = r * (dn - n * jnp.mean(dn * n, axis=-1, keepdims=True))
    return dx, jnp.sum(dy * n, axis=0, keepdims=True)


def _norm_bwd(name, dres, pre=None, post=None):
    has_pre, has_post = pre is not None, post is not None

    def body(*refs):
        refs = list(refs)
        dres_ref = refs.pop(0)
        if has_pre:
            xn_ref, gn_ref, dh_ref = refs[:3]
            refs = refs[3:]
        if has_post:
            y_ref, gp_ref = refs[:2]
            refs = refs[2:]
        dxn_ref = refs.pop(0)
        dy_ref = refs.pop(0) if has_post else None
        dgn_ref = refs.pop(0) if has_pre else None
        dgp_ref = refs.pop(0) if has_post else None
        first = pl.program_id(0) == 0
        dxn = dres_ref[...]
        if has_pre:
            dx, dg = _rms_bwd_math(xn_ref[...], gn_ref[...], dh_ref[...])
            dxn = dxn + dx

            @pl.when(first)
            def _():
                dgn_ref[...] = dg

            @pl.when(jnp.logical_not(first))
            def _():
                dgn_ref[...] += dg
        dxn_ref[...] = dxn
        if has_post:
            dy, dg = _rms_bwd_math(y_ref[...], gp_ref[...], dxn)
            dy_ref[...] = dy.astype(BF16)

            @pl.when(first)
            def _():
                dgp_ref[...] = dg

            @pl.when(jnp.logical_not(first))
            def _():
                dgp_ref[...] += dg

    ins, in_specs = [dres], [_row_spec()]
    if has_pre:
        ins += list(pre)
        in_specs += [_row_spec(), _vec_spec(), _row_spec()]
    if has_post:
        ins += list(post)
        in_specs += [_row_spec(), _vec_spec()]
    out_shape, out_specs = [jax.ShapeDtypeStruct((S, D), F32)], [_row_spec()]
    if has_post:
        out_shape.append(jax.ShapeDtypeStruct((S, D), BF16))
        out_specs.append(_row_spec())
    for _ in range(int(has_pre) + int(has_post)):
        out_shape.append(jax.ShapeDtypeStruct((1, D), F32))
        out_specs.append(_vec_spec())
    out = list(pl.pallas_call(
        body, name=name, grid=(S // TR,), in_specs=in_specs, out_specs=out_specs, out_shape=out_shape,
        compiler_params=_params(("arbitrary",)))(*ins))
    dxn = out.pop(0)
    dy = out.pop(0) if has_post else None
    dgn = out.pop(0) if has_pre else None
    dgp = out.pop(0) if has_post else None
    return dxn, dy, dgn, dgp


def _loss_kernel(y, target):
    def body(y_ref, t_ref, loss_ref, dy_ref):
        e = y_ref[...] - t_ref[...]
        dy_ref[...] = e * (1.0 / D)
        part = jnp.zeros((1, 128), F32) + 0.5 * jnp.sum(jnp.mean(e * e, axis=-1, keepdims=True))

        @pl.when(pl.program_id(0) == 0)
        def _():
            loss_ref[...] = part

        @pl.when(pl.program_id(0) > 0)
        def _():
            loss_ref[...] += part

    return pl.pallas_call(
        body, name="loss", grid=(S // TR,), in_specs=[_row_spec(), _row_spec()],
        out_specs=[_vec_spec(128), _row_spec()],
        out_shape=[jax.ShapeDtypeStruct((1, 128), F32), jax.ShapeDtypeStruct((S, D), F32)],
        compiler_params=_params(("arbitrary",)))(y, target)


def _t5_bucket_np(dist):
    max_exact = NUM_BUCKETS // 2
    nf = np.maximum(dist, 1).astype(np.float32)
    large = max_exact + (np.log(nf / max_exact) / np.float32(math.log(MAX_DISTANCE / max_exact))
                         * (NUM_BUCKETS - max_exact)).astype(np.int32)
    large = np.minimum(large, NUM_BUCKETS - 1)
    return np.where(dist < max_exact, dist, large).astype(np.int32)


def _bucket_maps():
    a = np.arange(BLK)[:, None]
    b = np.arange(2 * BLK)[None, :]
    dist = np.maximum(a + BLK - b, 0)
    maps = [_t5_bucket_np(dist * d) for _, d in A_GROUPS] + [_t5_bucket_np(dist)]
    return np.stack(maps).astype(np.int32)


def _classes(arr, d):
    return arr.reshape(S // d, d * arr.shape[1])


def _band_spec(arr, col0, prev):
    ncol = arr.shape[1] // 128
    if prev:
        return pl.BlockSpec((BLK, 128), lambda p, r, b: (jnp.maximum(b - 1, 0), r * ncol + col0 + p))
    return pl.BlockSpec((BLK, 128), lambda p, r, b: (b, r * ncol + col0 + p))


def _band_bias(tab_ref, bidx_ref, h):
    bi = bidx_ref[...]
    bias = jnp.zeros((BLK, 2 * BLK), F32)
    for kk in range(NUM_BUCKETS):
        bias = jnp.where(bi == kk, tab_ref[kk, h], bias)
    return bias


def _lane_lo(rows=BLK):
    return lax.broadcasted_iota(jnp.int32, (rows, 128), 1) < HD


def _per_head(x, lo):
    return (jnp.sum(jnp.where(lo, x, 0.0), axis=1, keepdims=True) * (1.0 / HD),
            jnp.sum(jnp.where(lo, 0.0, x), axis=1, keepdims=True) * (1.0 / HD))


def _band_fill(bias_ref, tab_ref, bidx_ref, head, maxd):
    a = lax.broadcasted_iota(jnp.int32, (BLK, 2 * BLK), 0)
    c = lax.broadcasted_iota(jnp.int32, (BLK, 2 * BLK), 1)
    dist = a + BLK - c
    in_band = jnp.logical_and(dist >= 0, dist <= maxd)
    for h in range(2):
        bias = jnp.where(in_band, _band_bias(tab_ref, bidx_ref, head + h), NEG)
        bias_ref[1, h * BLK:(h + 1) * BLK, :] = bias
        bias_ref[0, h * BLK:(h + 1) * BLK, :] = jnp.where(c >= BLK, bias, NEG)


def _stack_heads(x, lo, dtype=BF16):
    return jnp.concatenate([jnp.where(lo, x, 0.0), jnp.where(lo, 0.0, x)], axis=0).astype(dtype)


def _unstack_heads(x, lo):
    n = x.shape[0] // 2
    return jnp.where(lo, x[:n], x[n:])


def _stack_rows(prev_ref, cur_ref):
    return jnp.concatenate([prev_ref[...], cur_ref[...]], axis=0).astype(BF16)


def _band_fwd(name, d, n_pairs, maxd, head0, srcs, bidx_g, tab, sinks):
    nb = S // d // BLK
    (qa, qc), (ka, kc), (va, vc) = srcs
    out_spec = pl.BlockSpec((BLK, 128), lambda p, r, b: (b, r * n_pairs + p))
    smem = pl.BlockSpec(memory_space=pltpu.SMEM)
    full = pl.BlockSpec((BLK, 2 * BLK), lambda p, r, b: (0, 0))

    def body(tab_ref, sink_ref, q_ref, kp_ref, kc_ref, vp_ref, vc_ref, bidx_ref, o_ref, lse_ref, bias_ref):
        p, r, b = pl.program_id(0), pl.program_id(1), pl.program_id(2)

        @pl.when(jnp.logical_and(r == 0, b == 0))
        def _():
            _band_fill(bias_ref, tab_ref, bidx_ref, head0 + 2 * p, maxd)

        lo = _lane_lo()
        qs = _stack_heads(q_ref[...] * SCALE, lo)
        ks, vs = _stack_rows(kp_ref, kc_ref), _stack_rows(vp_ref, vc_ref)
        s = _dot(qs, ks, NT) + bias_ref[jnp.minimum(b, 1)]
        m = jnp.max(s, axis=1, keepdims=True)
        pr = jnp.exp(s - m)
        l = jnp.sum(pr, axis=1, keepdims=True)
        num = _dot(pr.astype(BF16), vs, NN)
        lse = m + jnp.log(l)
        sink = jnp.where(lax.broadcasted_iota(jnp.int32, (2 * BLK, 1), 0) < BLK, sink_ref[2 * p], sink_ref[2 * p + 1])
        sig = 1.0 / (1.0 + jnp.exp(sink - lse))
        o_ref[...] = _unstack_heads(num * (sig / l), lo)
        lse_ref[...] = _unstack_heads(lse + jnp.zeros((2 * BLK, 128), F32), lo)

    shape = jax.ShapeDtypeStruct((S // d, d * n_pairs * 128), F32)
    o, lse = pl.pallas_call(
        body, name=name, grid=(n_pairs, d, nb),
        in_specs=[smem, smem, _band_spec(qa, qc, False), _band_spec(ka, kc, True), _band_spec(ka, kc, False),
                  _band_spec(va, vc, True), _band_spec(va, vc, False), full],
        out_specs=[out_spec, out_spec], out_shape=[shape, shape],
        scratch_shapes=[pltpu.VMEM((2, 2 * BLK, 2 * BLK), F32)],
        compiler_params=_params(("parallel", "arbitrary", "arbitrary")))(
            tab, sinks, _classes(qa, d), _classes(ka, d), _classes(ka, d), _classes(va, d), _classes(va, d), bidx_g)
    return o.reshape(S, n_pairs * 128), lse.reshape(S, n_pairs * 128)


def _band_bwd(name, d, n_pairs, maxd, head0, srcs, bidx_g, tab, sinks, o, lse, do, stats_in):
    nb = S // d // BLK
    rows = S // d
    (qa, qc), (ka, kc), (va, vc) = srcs
    blk_spec = pl.BlockSpec((BLK, 128), lambda p, r, b: (b, r * n_pairs + p))
    cls_spec = pl.BlockSpec((rows, 128), lambda p, r, b: (0, r * n_pairs + p))
    smem = pl.BlockSpec(memory_space=pltpu.SMEM)
    full = pl.BlockSpec((BLK, 2 * BLK), lambda p, r, b: (0, 0))
    stat_spec = pl.BlockSpec((2, 8, 128), lambda p, r, b: (p, 0, 0))

    def body(tab_ref, sink_ref, q_ref, kp_ref, kc_ref, vp_ref, vc_ref, bidx_ref, o_ref, lse_ref, do_ref, sin_ref,
             dq_ref, dk_ref, dv_ref, stat_ref, bias_ref, dsacc_ref, sk_ref):
        p, r, b = pl.program_id(0), pl.program_id(1), pl.program_id(2)

        @pl.when(jnp.logical_and(r == 0, b == 0))
        def _():
            _band_fill(bias_ref, tab_ref, bidx_ref, head0 + 2 * p, maxd)
            dsacc_ref[...] = jnp.zeros_like(dsacc_ref)
            sk_ref[...] = jnp.zeros_like(sk_ref)

        @pl.when(b == 0)
        def _():
            dk_ref[...] = jnp.zeros_like(dk_ref)
            dv_ref[...] = jnp.zeros_like(dv_ref)

        lo = _lane_lo()
        qs = _stack_heads(q_ref[...] * SCALE, lo)
        ks, vs = _stack_rows(kp_ref, kc_ref), _stack_rows(vp_ref, vc_ref)
        do = do_ref[...]
        dos = _stack_heads(do, lo, F32)
        lse = jnp.concatenate(_per_head(lse_ref[...], lo), axis=0)
        prod = do * o_ref[...]
        delta = jnp.concatenate([jnp.sum(jnp.where(lo, prod, 0.0), axis=1, keepdims=True),
                                 jnp.sum(jnp.where(lo, 0.0, prod), axis=1, keepdims=True)], axis=0)
        head1 = lax.broadcasted_iota(jnp.int32, (2 * BLK, 1), 0) >= BLK
        sig = 1.0 / (1.0 + jnp.exp(jnp.where(head1, sink_ref[2 * p + 1], sink_ref[2 * p]) - lse))
        pr = jnp.exp(_dot(qs, ks, NT) + bias_ref[jnp.minimum(b, 1)] - lse)
        ds = pr * (sig * (_dot(dos.astype(BF16), vs, NT) - delta))
        dsb = ds.astype(BF16)
        dq_ref[...] = SCALE * _unstack_heads(_dot(dsb, ks, NN), lo)
        dk = _dot(dsb, qs, TN)
        dv = _dot(pr.astype(BF16), (sig * dos).astype(BF16), TN)
        cur = pl.ds(pl.multiple_of(b * BLK, BLK), BLK)
        prev = pl.ds(pl.multiple_of(jnp.maximum(b - 1, 0) * BLK, BLK), BLK)
        dk_ref[prev, :] += dk[:BLK]
        dk_ref[cur, :] += dk[BLK:]
        dv_ref[prev, :] += dv[:BLK]
        dv_ref[cur, :] += dv[BLK:]
        dsacc_ref[...] += ds
        sink_grad = -delta * (1.0 - sig)
        for h in range(2):
            sk_ref[h] += jnp.zeros((8, 128), F32) + jnp.sum(sink_grad[h * BLK:(h + 1) * BLK])

        @pl.when(jnp.logical_and(r == d - 1, b == nb - 1))
        def _():
            bi = bidx_ref[...]
            lane = lax.broadcasted_iota(jnp.int32, (8, 128), 1)
            sub = lax.broadcasted_iota(jnp.int32, (8, 128), 0)
            for h in range(2):
                acc = dsacc_ref[h * BLK:(h + 1) * BLK, :]
                row = jnp.where(jnp.logical_and(sub == 1, lane == 0), sk_ref[h], 0.0)
                for kk in range(NUM_BUCKETS):
                    tot = jnp.sum(jnp.where(bi == kk, acc, 0.0))
                    row = jnp.where(jnp.logical_and(sub == 0, lane == kk), tot, row)
                stat_ref[h] = row + jnp.where(sub == 0, sin_ref[h], 0.0)

    shape = jax.ShapeDtypeStruct((rows, d * n_pairs * 128), F32)
    dq, dk, dv, stats = pl.pallas_call(
        body, name=name, grid=(n_pairs, d, nb),
        in_specs=[smem, smem, _band_spec(qa, qc, False), _band_spec(ka, kc, True), _band_spec(ka, kc, False),
                  _band_spec(va, vc, True), _band_spec(va, vc, False), full, blk_spec, blk_spec, blk_spec, stat_spec],
        out_specs=[blk_spec, cls_spec, cls_spec, stat_spec],
        out_shape=[shape, shape, shape, jax.ShapeDtypeStruct((2 * n_pairs, 8, 128), F32)],
        scratch_shapes=[pltpu.VMEM((2, 2 * BLK, 2 * BLK), F32), pltpu.VMEM((2 * BLK, 2 * BLK), F32),
                        pltpu.VMEM((2, 8, 128), F32)],
        compiler_params=_params(("arbitrary", "arbitrary", "arbitrary")))(
            tab, sinks, _classes(qa, d), _classes(ka, d), _classes(ka, d), _classes(va, d), _classes(va, d), bidx_g,
            _classes(o, d), _classes(lse, d), _classes(do, d), stats_in)
    width = n_pairs * 128
    return dq.reshape(S, width), dk.reshape(S, width), dv.reshape(S, width), stats


def _comb_fwd(o_g, lse_g):
    def body(o0, o1, o2, l0, l1, l2, out_ref, outb_ref, lse_ref):
        a0, a1, a2 = l0[...], l1[...], l2[...]
        m = jnp.maximum(jnp.maximum(a0, a1), a2)
        e0, e1, e2 = jnp.exp(a0 - m), jnp.exp(a1 - m), jnp.exp(a2 - m)
        tot = e0 + e1 + e2
        out = (e0 * o0[...] + e1 * o1[...] + e2 * o2[...]) / tot
        out_ref[...] = out
        outb_ref[...] = out.astype(BF16)
        lse_ref[...] = m + jnp.log(tot)

    spec = _row_spec(4 * HD)
    f32 = jax.ShapeDtypeStruct((S, 4 * HD), F32)
    return pl.pallas_call(
        body, name="comb_fwd", grid=(S // TR,), in_specs=[spec] * 6, out_specs=[spec] * 3,
        out_shape=[f32, jax.ShapeDtypeStruct((S, 4 * HD), BF16), f32],
        compiler_params=_params(("parallel",)))(*o_g, *lse_g)


def _split2(x):
    hi = x.astype(BF16)
    return hi, (x - hi.astype(F32)).astype(BF16)


KB = 2 * BLK
SBQ = 2 * BLK


def _tri_sum(x, tri):
    hi, lo = _split2(x)
    both = _dot(jnp.concatenate([hi, lo], axis=0), tri, NN)
    return both[:x.shape[0]] + both[x.shape[0]:]


def _tri(strict_upper):
    r = lax.broadcasted_iota(jnp.int32, (KB, KB), 0)
    c = lax.broadcasted_iota(jnp.int32, (KB, KB), 1)
    return jnp.where(r > c if strict_upper else r < c, 1.0, 0.0).astype(BF16)


def _sb_terms(qs, kj, before):
    z = _dot(qs, kj, NT)
    lsp = jnp.minimum(z, 0.0) - jnp.log(1.0 + jnp.exp(-jnp.abs(z)))
    return lsp, jnp.where(before, lsp - z, 0.0)


def _sb_before(i, m):
    t = (lax.broadcasted_iota(jnp.int32, (2 * SBQ, KB), 0) & (SBQ - 1)) + i * SBQ
    s = lax.broadcasted_iota(jnp.int32, (2 * SBQ, KB), 1) + m * KB
    return s < t


C_COL = 3072 // 128


def _sb_fwd(proj):
    blk = lambda off: pl.BlockSpec((SBQ, 128), lambda p, i: (i, off + p))
    col = lambda off: pl.BlockSpec((S, 128), lambda p, i: (0, off + p))
    out = pl.BlockSpec((SBQ, 128), lambda p, i: (i, p))

    def body(q_ref, k_ref, v_ref, o_ref, ob_ref, tot_ref):
        i = pl.program_id(1)
        lo = _lane_lo(SBQ)
        qs = _stack_heads(q_ref[...] * SCALE, lo)
        suffix = _tri(True)

        def step(n, carry):
            acc, rest = carry
            m = i - n
            rows = pl.ds(pl.multiple_of(m * KB, KB), KB)
            kj, vj = k_ref[rows, :].astype(BF16), v_ref[rows, :].astype(BF16)
            before = _sb_before(i, m)
            lsp, lk = _sb_terms(qs, kj, before)
            w = jnp.where(before, jnp.exp(lsp + _tri_sum(lk, suffix) + rest), 0.0)
            return acc + _dot(w.astype(BF16), vj, NN), rest + jnp.sum(lk, axis=1, keepdims=True)

        acc, rest = lax.fori_loop(0, i + 1, step, (jnp.zeros((2 * SBQ, 128), F32), jnp.zeros((2 * SBQ, 1), F32)))
        o = _unstack_heads(acc, lo)
        o_ref[...] = o
        ob_ref[...] = o.astype(BF16)
        tot_ref[...] = _unstack_heads(rest + jnp.zeros((2 * SBQ, 128), F32), lo)

    f32 = jax.ShapeDtypeStruct((S, 4 * HD), F32)
    return pl.pallas_call(
        body, name="sb_fwd", grid=(2, S // SBQ), in_specs=[blk(C_COL), col(C_COL + 2), col(C_COL + 4)],
        out_specs=[out, out, out], out_shape=[f32, jax.ShapeDtypeStruct((S, 4 * HD), BF16), f32],
        compiler_params=_params(("parallel", "arbitrary")))(proj, proj, proj)


def _sb_bwd(proj, tot, do):
    blk = lambda off: pl.BlockSpec((SBQ, 128), lambda p, i: (i, off + p))
    col = lambda off: pl.BlockSpec((S, 128), lambda p, i: (0, off + p))

    def body(q_ref, k_ref, v_ref, tot_ref, do_ref, dq_ref, dk_ref, dv_ref):
        i = pl.program_id(1)

        @pl.when(i == 0)
        def _():
            dk_ref[...] = jnp.zeros_like(dk_ref)
            dv_ref[...] = jnp.zeros_like(dv_ref)

        lo = _lane_lo(SBQ)
        qs = _stack_heads(q_ref[...] * SCALE, lo)
        dos = _stack_heads(do_ref[...], lo)
        tots = jnp.concatenate(_per_head(tot_ref[...], lo), axis=0)
        prefix = _tri(False)

        def step(m, carry):
            dq, keep_left, g_left = carry
            rows = pl.ds(pl.multiple_of(m * KB, KB), KB)
            kj, vj = k_ref[rows, :].astype(BF16), v_ref[rows, :].astype(BF16)
            before = _sb_before(i, m)
            lsp, lk = _sb_terms(qs, kj, before)
            log_rest = tots - keep_left - lk - _tri_sum(lk, prefix)
            w = jnp.where(before, jnp.exp(lsp + log_rest), 0.0)
            g = w * _dot(dos, vj, NT)
            g_before = g_left + _dot(g.astype(BF16), prefix, NN)
            beta = jnp.exp(lsp)
            dz = jnp.where(before, g * (1.0 - beta) - g_before * beta, 0.0).astype(BF16)
            dk_ref[rows, :] += _dot(dz, qs, TN)
            dv_ref[rows, :] += _dot(w.astype(BF16), dos, TN)
            return (dq + _dot(dz, kj, NN), keep_left + jnp.sum(lk, axis=1, keepdims=True),
                    g_left + jnp.sum(g, axis=1, keepdims=True))

        zero = (jnp.zeros((2 * SBQ, 128), F32), jnp.zeros((2 * SBQ, 1), F32), jnp.zeros((2 * SBQ, 1), F32))
        dq, _, _ = lax.fori_loop(0, i + 1, step, zero)
        dq_ref[...] = SCALE * _unstack_heads(dq, lo)

    out_blk = pl.BlockSpec((SBQ, 128), lambda p, i: (i, p))
    out_col = pl.BlockSpec((S, 128), lambda p, i: (0, p))
    f32 = jax.ShapeDtypeStruct((S, 4 * HD), F32)
    return pl.pallas_call(
        body, name="sb_bwd", grid=(2, S // SBQ),
        in_specs=[blk(C_COL), col(C_COL + 2), col(C_COL + 4), out_blk, out_blk],
        out_specs=[out_blk, out_col, out_col], out_shape=[f32, f32, f32],
        compiler_params=_params(("arbitrary", "arbitrary")))(proj, proj, proj, tot, do)


TG = 256
GATE_BLK0 = OFF_GATE // TG


def _gate_specs():
    grid = (D // TG, S // TG)
    p_specs = [pl.BlockSpec((TG, TG), functools.partial(lambda c, r, br: (r, GATE_BLK0 + br * (D // TG) + c), br=br))
               for br in range(3)]
    b_spec = pl.BlockSpec((3, TG), lambda c, r: (0, c))
    t_spec = pl.BlockSpec((TG, TG), lambda c, r: (r, c))
    return grid, p_specs, b_spec, t_spec


def _sigmoid(x):
    return 1.0 / (1.0 + jnp.exp(-x))


def _three_rows(rows):
    sub = lax.broadcasted_iota(jnp.int32, (3, rows[0].shape[1]), 0)
    return jnp.where(sub == 0, rows[0], jnp.where(sub == 1, rows[1], rows[2]))


def _gate_fwd(proj, b_gate, br):
    grid, p_specs, b_spec, t_spec = _gate_specs()

    def body(p0, p1, p2, b_ref, r0, r1, r2, out_ref):
        acc = jnp.zeros((TG, TG), F32)
        for n, (p, r) in enumerate(((p0, r0), (p1, r1), (p2, r2))):
            acc += _sigmoid(p[...] + b_ref[n:n + 1, :]) * r[...]
        out_ref[...] = acc.astype(BF16)

    return pl.pallas_call(
        body, name="gate_fwd", grid=grid, in_specs=p_specs + [b_spec] + [t_spec] * 3, out_specs=t_spec,
        out_shape=jax.ShapeDtypeStruct((S, D), BF16),
        compiler_params=_params(("parallel", "parallel")))(proj, proj, proj, b_gate, *br)


def _gate_bwd(proj, b_gate, br, dmerged):
    grid, p_specs, b_spec, t_spec = _gate_specs()

    def body(p0, p1, p2, b_ref, r0, r1, r2, dm_ref, e0, e1, e2, g0, g1, g2, db_ref):
        dm = dm_ref[...]
        rows = []
        for n, (p, r, e_ref, dg_ref) in enumerate(((p0, r0, e0, g0), (p1, r1, e1, g1), (p2, r2, e2, g2))):
            g = _sigmoid(p[...] + b_ref[n:n + 1, :])
            e_ref[...] = (dm * g).astype(BF16)
            dpre = dm * r[...] * g * (1.0 - g)
            dg_ref[...] = dpre.astype(BF16)
            rows.append(jnp.sum(dpre, axis=0, keepdims=True))
        db = _three_rows(rows)

        @pl.when(pl.program_id(1) == 0)
        def _():
            db_ref[...] = db

        @pl.when(pl.program_id(1) > 0)
        def _():
            db_ref[...] += db

    bf = jax.ShapeDtypeStruct((S, D), BF16)
    out = pl.pallas_call(
        body, name="gate_bwd", grid=grid, in_specs=p_specs + [b_spec] + [t_spec] * 4,
        out_specs=[t_spec] * 6 + [b_spec], out_shape=[bf] * 6 + [jax.ShapeDtypeStruct((3, D), F32)],
        compiler_params=_params(("parallel", "arbitrary")))(proj, proj, proj, b_gate, *br, dmerged)
    return out[:3], out[3:6], out[6]


TC = 256
N_FF_BLK = D_FF // TC
GELU_C = math.sqrt(2.0 / math.pi)


def _shift_down(x, n):
    rows = lax.broadcasted_iota(jnp.int32, x.shape, 0)
    return jnp.where(rows >= n, pltpu.roll(x, n, axis=0), 0.0)


def _shift_up(x, n):
    rows = lax.broadcasted_iota(jnp.int32, x.shape, 0)
    return jnp.where(rows < x.shape[0] - n, pltpu.roll(x, x.shape[0] - n, axis=0), 0.0)


def _conv(u, w, b):
    return w[2:3, :] * u + w[1:2, :] * _shift_down(u, 1) + w[0:1, :] * _shift_down(u, 2) + b


def _gelu_parts(x):
    inner = GELU_C * (x + 0.044715 * x * x * x)
    t = jnp.tanh(inner)
    gelu = 0.5 * x * (1.0 + t)
    dgelu = 0.5 * (1.0 + t) + 0.5 * x * (1.0 - t * t) * GELU_C * (1.0 + 3 * 0.044715 * x * x)
    return gelu, dgelu


def _conv_specs():
    ug = pl.BlockSpec((S, TC), lambda c: (0, c))
    uv = pl.BlockSpec((S, TC), lambda c: (0, N_FF_BLK + c))
    wg = pl.BlockSpec((3, TC), lambda c: (0, c))
    wv = pl.BlockSpec((3, TC), lambda c: (0, N_FF_BLK + c))
    bg = pl.BlockSpec((1, TC), lambda c: (0, c))
    bv = pl.BlockSpec((1, TC), lambda c: (0, N_FF_BLK + c))
    return ug, uv, wg, wv, bg, bv


def _conv_fwd(u, conv_w, conv_b):
    ug, uv, wg, wv, bg, bv = _conv_specs()

    def body(ug_ref, uv_ref, wg_ref, wv_ref, bg_ref, bv_ref, a_ref):
        gc = _conv(ug_ref[...], wg_ref[...], bg_ref[...])
        vc = _conv(uv_ref[...], wv_ref[...], bv_ref[...])
        a_ref[...] = (_gelu_parts(gc)[0] * vc).astype(BF16)

    return pl.pallas_call(
        body, name="conv_fwd", grid=(N_FF_BLK,), in_specs=[ug, uv, wg, wv, bg, bv], out_specs=ug,
        out_shape=jax.ShapeDtypeStruct((S, D_FF), BF16),
        compiler_params=_params(("parallel",)))(u, u, conv_w, conv_w, conv_b, conv_b)


def _conv_bwd(u, conv_w, conv_b, da):
    ug, uv, wg, wv, bg, bv = _conv_specs()

    def back(duc, u, w):
        du = w[2:3, :] * duc + w[1:2, :] * _shift_up(duc, 1) + w[0:1, :] * _shift_up(duc, 2)
        dw = _three_rows([jnp.sum(duc * _shift_down(u, 2), axis=0, keepdims=True),
                          jnp.sum(duc * _shift_down(u, 1), axis=0, keepdims=True),
                          jnp.sum(duc * u, axis=0, keepdims=True)])
        return du, dw, jnp.sum(duc, axis=0, keepdims=True)

    def body(ug_ref, uv_ref, wg_ref, wv_ref, bg_ref, bv_ref, da_ref, dug_ref, duv_ref, dwg_ref, dwv_ref, dbg_ref, dbv_ref):
        u_g, u_v = ug_ref[...], uv_ref[...]
        gc = _conv(u_g, wg_ref[...], bg_ref[...])
        vc = _conv(u_v, wv_ref[...], bv_ref[...])
        gelu, dgelu = _gelu_parts(gc)
        da = da_ref[...]
        du, dw, db = back(da * vc * dgelu, u_g, wg_ref[...])
        dug_ref[...] = du.astype(BF16)
        dwg_ref[...] = dw
        dbg_ref[...] = db
        du, dw, db = back(da * gelu, u_v, wv_ref[...])
        duv_ref[...] = du.astype(BF16)
        dwv_ref[...] = dw
        dbv_ref[...] = db

    return pl.pallas_call(
        body, name="conv_bwd", grid=(N_FF_BLK,), in_specs=[ug, uv, wg, wv, bg, bv, ug],
        out_specs=[ug, ug, wg, wg, bg, bg],
        out_shape=[jax.ShapeDtypeStruct((S, D_FF), BF16), jax.ShapeDtypeStruct((S, D_FF), BF16),
                   jax.ShapeDtypeStruct((3, D_FF), F32), jax.ShapeDtypeStruct((3, D_FF), F32),
                   jax.ShapeDtypeStruct((1, D_FF), F32), jax.ShapeDtypeStruct((1, D_FF), F32)],
        compiler_params=_params(("parallel",)))(u, u, conv_w, conv_w, conv_b, conv_b, da)


def _adamw(name, w, g, m, v):
    shape = w.shape
    cols = shape[-1]
    flat = [t.reshape(-1, cols) for t in (w, g, m, v)]
    r = flat[0].shape[0]
    tr = min(128, r)

    def body(w_ref, g_ref, m_ref, v_ref, d_ref, mo_ref, vo_ref):
        g = g_ref[...]
        m = ADAM_B1 * m_ref[...] + (1.0 - ADAM_B1) * g
        v = ADAM_B2 * v_ref[...] + (1.0 - ADAM_B2) * (g * g)
        m_hat = m / (1.0 - ADAM_B1 ** ADAM_STEP)
        v_hat = v / (1.0 - ADAM_B2 ** ADAM_STEP)
        d_ref[...] = -ADAM_LR * (m_hat / (jnp.sqrt(v_hat) + ADAM_EPS) + ADAM_WD * w_ref[...])
        mo_ref[...] = m
        vo_ref[...] = v

    spec = pl.BlockSpec((tr, cols), lambda i: (i, 0))
    outs = pl.pallas_call(
        body, name=name, grid=(pl.cdiv(r, tr),), in_specs=[spec] * 4, out_specs=[spec] * 3,
        out_shape=[jax.ShapeDtypeStruct((r, cols), F32)] * 3, compiler_params=_params(("parallel",)))(*flat)
    return [t.reshape(shape) for t in outs]


def _place():
    x, y, c = lax.axis_index("x"), lax.axis_index("y"), lax.axis_index("c")
    chips = [(1 - x, y), (x, 1 - y), (1 - x, 1 - y)]
    return x, y, c, chips


def _scalars(*vals):
    return jnp.stack([jnp.asarray(v, jnp.int32) for v in vals])


HBM = pl.BlockSpec(memory_space=pltpu.HBM)
SEM = pl.BlockSpec(memory_space=pltpu.SEMAPHORE)
SPLIT_COPY = pltpu.CompilerParams(has_side_effects=pltpu.SideEffectType.DATAFLOW_SIDE_EFFECTING)


def _in_hbm(x):
    return pltpu.with_memory_space_constraint(x, pltpu.HBM)


def _cast_into_slot(name, w, layer, chip):
    _, k, n4 = w.shape
    tr = max(t for t in range(16, 257, 16) if k % t == 0)

    def body(chip_ref, w_ref, o_ref):
        o_ref[...] = w_ref[...].astype(BF16)

    return pl.pallas_call(
        body, name=name,
        grid_spec=pltpu.PrefetchScalarGridSpec(
            num_scalar_prefetch=1, grid=(k // tr,),
            in_specs=[pl.BlockSpec((None, tr, n4), lambda i, s: (layer, i, 0))],
            out_specs=pl.BlockSpec((None, tr, n4), lambda i, s: (s[0], i, 0))),
        out_shape=jax.ShapeDtypeStruct((N_CHIPS, k, n4), BF16),
        compiler_params=_params(("parallel",)))(_scalars(chip), w)


def _gather_copy(buf_ref, k, from_chip, send_sem, recv_sem, chips, c):
    rows = buf_ref.at[from_chip]
    return pltpu.make_async_remote_copy(src_ref=rows, dst_ref=rows, send_sem=send_sem, recv_sem=recv_sem,
                                        device_id=(*chips[k], c), device_id_type=MESH)


def _gather_start(name, bufs, groups):
    n, ng = len(bufs), len(groups)
    where = {a: (gi, e) for gi, g in enumerate(groups) for e, a in enumerate(g)}

    def body(*refs):
        ins, sems, token = refs[:n], refs[n:n + 2 * ng], refs[-1]
        x, y, c, chips = _place()
        for a in range(n):
            gi, e = where[a]
            for k in range(3):
                _gather_copy(ins[a], k, 2 * x + y, sems[2 * gi].at[3 * e + k], sems[2 * gi + 1].at[3 * e + k],
                             chips, c).start()
        token[...] = jnp.zeros_like(token)

    out_shape = [pltpu.SemaphoreType.DMA((3 * len(g),)) for g in groups for _ in range(2)]
    out_shape += [pltpu.HBM(b.shape, b.dtype) for b in bufs] + [jax.ShapeDtypeStruct((8, 128), F32)]
    out = pl.pallas_call(
        body, name=name, in_specs=[HBM] * n,
        out_specs=[SEM] * (2 * ng) + [HBM] * n + [pl.BlockSpec(memory_space=pltpu.VMEM)], out_shape=out_shape,
        input_output_aliases={a: 2 * ng + a for a in range(n)}, compiler_params=SPLIT_COPY)(*[_in_hbm(b) for b in bufs])
    sems = [(out[2 * gi], out[2 * gi + 1]) for gi in range(ng)]
    return sems, list(out[2 * ng:2 * ng + n]), out[-1]


def _gather_wait(name, bufs, send, recv, after):
    n = len(bufs)

    def body(*refs):
        ins, send_sem, recv_sem = refs[:n], refs[n], refs[n + 1]
        x, y, c, chips = _place()
        for e in range(n):
            for k in range(3):
                sems = (send_sem.at[3 * e + k], recv_sem.at[3 * e + k])
                _gather_copy(ins[e], k, 2 * x + y, *sems, chips, c).wait_send()
                _gather_copy(ins[e], k, 2 * chips[k][0] + chips[k][1], *sems, chips, c).wait_recv()

    return pl.pallas_call(
        body, name=name, in_specs=[HBM] * n + [SEM, SEM, ANY], out_specs=[HBM] * n,
        out_shape=[pltpu.HBM(b.shape, b.dtype) for b in bufs],
        input_output_aliases={a: a for a in range(n)}, compiler_params=SPLIT_COPY)(*bufs, send, recv, after)


def _reduce_copy(g_ref, land_ref, mask, send_sem, recv_sem, x, y, c, sending):
    px, py, pc = x ^ ((mask >> 2) & 1), y ^ ((mask >> 1) & 1), c ^ (mask & 1)
    half = g_ref.shape[1] // 2
    src = g_ref.at[2 * px + py, pl.ds(pl.multiple_of(pc * half, half), half)]
    dst = land_ref.at[4 * x + 2 * y + c] if sending else land_ref.at[4 * px + 2 * py + pc]
    return pltpu.make_async_remote_copy(src_ref=src, dst_ref=dst, send_sem=send_sem, recv_sem=recv_sem,
                                        device_id=(px, py, pc), device_id_type=MESH)


def _reduce_start(name, grads):
    n = len(grads)
    lands = [lax.empty((N_DEV, g.shape[1] // 2, g.shape[2]), g.dtype) for g in grads]

    def body(*refs):
        gs, ls, send_sem, recv_sem = refs[:n], refs[n:2 * n], refs[2 * n], refs[2 * n + 1]
        x, y, c, _ = _place()
        for a in range(n):
            for mask in range(1, N_DEV):
                s = (N_DEV - 1) * a + mask - 1
                _reduce_copy(gs[a], ls[a], mask, send_sem.at[s], recv_sem.at[s], x, y, c, True).start()
        refs[-1][...] = jnp.zeros_like(refs[-1])

    sem = pltpu.SemaphoreType.DMA((n * (N_DEV - 1),))
    out = pl.pallas_call(
        body, name=name, in_specs=[HBM] * (2 * n),
        out_specs=[SEM, SEM] + [HBM] * (2 * n) + [pl.BlockSpec(memory_space=pltpu.VMEM)],
        out_shape=[sem, sem] + [pltpu.HBM(t.shape, t.dtype) for t in grads + lands] + [jax.ShapeDtypeStruct((8, 128), F32)],
        input_output_aliases={a: 2 + a for a in range(2 * n)}, compiler_params=SPLIT_COPY)(
            *[_in_hbm(t) for t in grads + lands])
    return out[0], out[1], list(out[2:2 + n]), list(out[2 + n:2 + 2 * n]), out[-1]


def _reduce_wait(name, send, recv, grads, lands, after):
    n = len(grads)

    def body(*refs):
        gs, ls, send_sem, recv_sem = refs[:n], refs[n:2 * n], refs[2 * n], refs[2 * n + 1]
        x, y, c, _ = _place()
        for a in range(n):
            for mask in range(1, N_DEV):
                s = (N_DEV - 1) * a + mask - 1
                sems = (send_sem.at[s], recv_sem.at[s])
                _reduce_copy(gs[a], ls[a], mask, *sems, x, y, c, True).wait_send()
                _reduce_copy(gs[a], ls[a], mask, *sems, x, y, c, False).wait_recv()

    out = pl.pallas_call(
        body, name=name, in_specs=[HBM] * (2 * n) + [SEM, SEM, ANY], out_specs=[HBM] * (2 * n),
        out_shape=[pltpu.HBM(t.shape, t.dtype) for t in grads + lands],
        input_output_aliases={a: a for a in range(2 * n)}, compiler_params=SPLIT_COPY)(*grads, *lands, send, recv, after)
    return list(out[:n]), list(out[n:])


def _reduce_sum(name, g, land, layer, into, chip, c):
    _, k4, n4 = g.shape
    half = k4 // 2
    tr = max(t for t in range(16, 513, 16) if half % t == 0)
    per = half // tr
    me = 2 * chip + c

    def body(s_ref, own_ref, *refs):
        total = own_ref[...].astype(F32)
        for ref in refs[:N_DEV - 1]:
            total = total + ref[...].astype(F32)
        refs[-1][...] = total

    in_specs = [pl.BlockSpec((None, tr, n4), lambda i, s: (s[0], s[1] * per + i, 0))]
    in_specs += [pl.BlockSpec((None, tr, n4), functools.partial(lambda i, s, m: (s[1 + m], i, 0), m=m))
                 for m in range(1, N_DEV)]
    ins = [g] + [land] * (N_DEV - 1)
    aliases = {}
    if into is not None:
        in_specs, ins, aliases = in_specs + [ANY], ins + [into], {1 + N_DEV: 0}
    return pl.pallas_call(
        body, name=name,
        grid_spec=pltpu.PrefetchScalarGridSpec(
            num_scalar_prefetch=1, grid=(per,), in_specs=in_specs,
            out_specs=pl.BlockSpec((None, tr, n4), lambda i, s: (layer, s[1] * per + i, 0))),
        out_shape=jax.ShapeDtypeStruct((DEPTH, k4, n4), F32), input_output_aliases=aliases,
        compiler_params=_params(("parallel",)))(_scalars(chip, c, *[me ^ m for m in range(1, N_DEV)]), *ins)


def _join_halves(name, bufs):
    n = len(bufs)

    def body(*refs):
        ins, outs = refs[:n], refs[n:2 * n]
        send_sem, recv_sem = refs[2 * n:]
        x, y, c, _ = _place()

        def rows(ref, which):
            half = ref.shape[1] // 2
            return ref.at[:, pl.ds(pl.multiple_of(which * half, half), half)]

        sends = [pltpu.make_async_remote_copy(
            src_ref=rows(ins[a], c), dst_ref=rows(outs[a], c), send_sem=send_sem.at[a], recv_sem=recv_sem.at[a],
            device_id=(x, y, 1 - c), device_id_type=MESH) for a in range(n)]
        for cp in sends:
            cp.start()
        for a in range(n):
            sends[a].wait_send()
            pltpu.make_async_remote_copy(
                src_ref=rows(ins[a], c), dst_ref=rows(outs[a], 1 - c), send_sem=send_sem.at[a], recv_sem=recv_sem.at[a],
                device_id=(x, y, 1 - c), device_id_type=MESH).wait_recv()

    return pl.pallas_call(
        body, name=name, in_specs=[ANY] * n, out_specs=[ANY] * n,
        out_shape=[jax.ShapeDtypeStruct(b.shape, b.dtype) for b in bufs],
        input_output_aliases={a: a for a in range(n)},
        scratch_shapes=[pltpu.SemaphoreType.DMA((n,)), pltpu.SemaphoreType.DMA((n,))],
    )(*bufs)


def _all_reduce_small(block):
    r = block.shape[0]

    def body(x_ref, out_ref, slots, send_sem, recv_sem):
        x, y, c, _ = _place()
        me = 4 * x + 2 * y + c
        slots[me] = x_ref[...]
        sends = []
        for mask in range(1, N_DEV):
            fx, fy, fc = (mask >> 2) & 1, (mask >> 1) & 1, mask & 1
            peer = (x ^ fx, y ^ fy, c ^ fc)
            cp = pltpu.make_async_remote_copy(
                src_ref=x_ref, dst_ref=slots.at[me], send_sem=send_sem.at[mask - 1], recv_sem=recv_sem.at[mask - 1],
                device_id=peer, device_id_type=MESH)
            cp.start()
            sends.append(cp)
        for mask in range(1, N_DEV):
            src = me ^ mask
            pltpu.make_async_remote_copy(
                src_ref=x_ref, dst_ref=slots.at[src], send_sem=send_sem.at[mask - 1], recv_sem=recv_sem.at[mask - 1],
                device_id=(x, y, c), device_id_type=MESH).wait_recv()
        for cp in sends:
            cp.wait_send()
        total = slots[0]
        for d in range(1, N_DEV):
            total = total + slots[d]
        out_ref[...] = total

    vmem = pl.BlockSpec(memory_space=pltpu.VMEM)
    return pl.pallas_call(
        body, name="all_reduce_small", in_specs=[vmem], out_specs=vmem,
        out_shape=jax.ShapeDtypeStruct((r, 128), F32),
        scratch_shapes=[pltpu.VMEM((N_DEV, r, 128), F32), pltpu.SemaphoreType.DMA((N_DEV - 1,)),
                        pltpu.SemaphoreType.DMA((N_DEV - 1,))],
        compiler_params=pltpu.CompilerParams(vmem_limit_bytes=VMEM_LIMIT))(block)


B_Q_COL = 2304 // 128
B_K0, B_V0, B_END = 2816, 2944, 3072


def _full_cols(w_g):
    return w_g.transpose(1, 0, 2).reshape(w_g.shape[1], -1)


def _group_src(proj, g):
    if A_GROUPS[g][1] == 1:
        return ((proj, 2 * g), (proj, 6 + 2 * g), (proj, 12 + 2 * g))
    packed = jnp.concatenate([proj[:, t * 768 + g * 256:t * 768 + (g + 1) * 256] for t in range(3)], axis=1)
    return ((packed, 0), (packed, 2), (packed, 4))


def _kv_expand(kv):
    return jnp.broadcast_to(kv.reshape(S, 2, 1, HD), (S, 2, 4, HD)).reshape(S, 8 * HD)


def _kv_reduce(dkv):
    return dkv.reshape(S, 2, 4, HD).sum(axis=2).reshape(S, 2 * HD)


def _mixer_fwd(h1, wget, rel_bias, sinks_l, bidx):
    w = dict(wget(0, h1))
    proj = _mm_nt("proj_in", h1, w["w_in"], F32, tn=1152)
    no_sinks = jnp.full((4,), NEG, F32)
    srcs = [_group_src(proj, g) for g in range(3)]
    o_g, lse_g = [], []
    for g, (_, d) in enumerate(A_GROUPS):
        o, lse = _band_fwd("band_fwd_g%d" % g, d, 2, BLK, 4 * g, srcs[g], bidx[g], rel_bias, no_sinks)
        o_g.append(o)
        lse_g.append(lse)
    o_a32, o_a, lse_a = _comb_fwd(o_g, lse_g)
    src_b = ((proj, B_Q_COL), (_kv_expand(proj[:, B_K0:B_V0]), 0), (_kv_expand(proj[:, B_V0:B_END]), 0))
    o_b32, lse_b = _band_fwd("band_fwd_b", 1, 4, BLK - 1, N_A, src_b, bidx[3], rel_bias, sinks_l)
    o_b = o_b32.astype(BF16)
    o_c32, o_c, tot_c = _sb_fwd(proj)
    w.update(wget(1, o_c32))
    br = [_mm_nn("branch_a", o_a, w["w_br_a"], F32), _mm_nn("branch_b", o_b, w["w_br_b"], F32),
          _mm_nn("branch_c", o_c, w["w_br_c"], F32)]
    merged = _gate_fwd(proj, w["b_gate"], br)
    mo = _mm_nn("out_proj", merged, w["w_out"], F32)
    saved = dict(proj=proj, srcs=srcs, src_b=src_b, o_a32=o_a32, lse_a=lse_a, o_b32=o_b32, lse_b=lse_b, tot_c=tot_c,
                 o_a=o_a, o_b=o_b, o_c=o_c, br=br, merged=merged)
    return mo, saved, w


def _mixer_bwd(d_mo, h1, w, sv, rel_bias, sinks_l, bidx, stats_in, emit):
    grads = {}
    dmerged = _mm_nt("out_proj_dx", d_mo, w["w_out"], F32)
    grads["w_out"] = _mm_tn_sharded("out_proj_dw", sv["merged"], d_mo, True)
    e, dgate, db_gate = _gate_bwd(sv["proj"], w["b_gate"], sv["br"], dmerged)
    grads["b_gate"] = db_gate
    d_o = {}
    for n, name in enumerate("abc"):
        d_o[name] = _mm_nt("branch_%s_dx" % name, e[n], w["w_br_" + name], F32)
        grads["w_br_" + name] = _mm_tn_sharded("branch_%s_dw" % name, sv["o_" + name], e[n], False)
    zero = emit(1, grads)
    no_sinks = jnp.full((4,), NEG, F32) + zero[0]
    dqs, dks, dvs, stats = [], [], [], []
    for g, (_, d) in enumerate(A_GROUPS):
        dq, dk, dv, st = _band_bwd("band_bwd_g%d" % g, d, 2, BLK, 4 * g, sv["srcs"][g], bidx[g], rel_bias, no_sinks,
                                   sv["o_a32"], sv["lse_a"], d_o["a"], stats_in[4 * g:4 * g + 4])
        dqs.append(dq)
        dks.append(dk)
        dvs.append(dv)
        stats.append(st)
    dq_b, dk_x, dv_x, st = _band_bwd("band_bwd_b", 1, 4, BLK - 1, N_A, sv["src_b"], bidx[3], rel_bias, sinks_l,
                                     sv["o_b32"], sv["lse_b"], d_o["b"], stats_in[N_A:])
    stats = jnp.concatenate(stats + [st], axis=0)
    dcq, dck, dcv = _sb_bwd(sv["proj"], sv["tot_c"], d_o["c"])
    cols = dqs + dks + dvs + [dq_b, _kv_reduce(dk_x), _kv_reduce(dv_x), dcq, dck, dcv]
    dproj = jnp.concatenate([t.astype(BF16) for t in cols] + list(dgate), axis=1)
    grads["w_in"] = _mm_tn("proj_in_dw", dproj, h1, BF16, tm=768).reshape(N_CHIPS, IN_SHARD, D)
    zero = emit(2, grads)
    dh1 = _mm_nn("proj_in_dx", dproj, w["w_in"], F32)
    return dh1, grads, stats, zero


def _ffn_fwd(h2, w):
    u = _mm_nn("ffn_up", h2, w["w_up"], F32, tn=1024)
    a = _conv_fwd(u, w["conv_w"], w["conv_b"])
    dn = _mm_nn("ffn_down", a, w["w_down"], F32)
    return dn, dict(u=u, a=a)


def _ffn_bwd(d_dn, h2, w, sv):
    grads = {}
    da = _mm_nt("ffn_down_dx", d_dn, w["w_down"], F32, tn=1024)
    grads["w_down"] = _mm_tn_sharded("ffn_down_dw", sv["a"], d_dn, True)
    dug, duv, dwg, dwv, dbg, dbv = _conv_bwd(sv["u"], w["conv_w"], w["conv_b"], da)
    du = jnp.concatenate([dug, duv], axis=1)
    grads["conv_w"] = jnp.concatenate([dwg, dwv], axis=1)
    grads["conv_b"] = jnp.concatenate([dbg, dbv], axis=1)
    dh2 = _mm_nt("ffn_up_dx", du, w["w_up"], F32)
    grads["w_up"] = _mm_tn_sharded("ffn_up_dw", h2, du, False, tn=1024)
    return dh2, grads


BIG = ("w_in", "w_br_a", "w_br_b", "w_br_c", "w_out", "w_up", "w_down")


def _shard_view(name, w):
    return jnp.swapaxes(w, 1, 2) if name == "w_in" else w
WEIGHT_GROUPS = (("w_in", "b_gate"), ("w_br_a", "w_br_b", "w_br_c", "w_out"), ("w_up", "conv_w", "w_down"))
GRAD_GROUPS = (("w_down", "w_up"), ("w_out", "w_br_a", "w_br_b", "w_br_c"), ("w_in",))
SMALL_ROWS = (("rel_bias", 8), ("attn_pre_norm", 16), ("attn_post_norm", 16), ("ffn_pre_norm", 16), ("ffn_post_norm", 16),
              ("sinks", 8), ("conv_b", 128), ("b_gate", 48), ("conv_w", 384), ("loss", 8))


def _pack_small(vals):
    rows = []
    for name, n in SMALL_ROWS:
        flat = vals[name].reshape(-1).astype(F32)
        rows.append(jnp.pad(flat, (0, n * 128 - flat.shape[0])).reshape(n, 128))
    return jnp.concatenate(rows, axis=0)


def _unpack_small(block, shapes):
    out, row = {}, 0
    for name, n in SMALL_ROWS:
        size = int(np.prod(shapes[name]))
        out[name] = block[row:row + n].reshape(-1)[:size].reshape(shapes[name])
        row += n
    return out


def kernel(x, rel_bias, attn_pre_norm, w_in, b_gate, sinks, w_br_a, w_br_b, w_br_c, w_out, attn_post_norm, ffn_pre_norm, w_up, conv_w, conv_b, w_down, ffn_post_norm, loss_target, m_rel_bias, m_attn_pre_norm, m_w_in, m_b_gate, m_sinks, m_w_br_a, m_w_br_b, m_w_br_c, m_w_out, m_attn_post_norm, m_ffn_pre_norm, m_w_up, m_conv_w, m_conv_b, m_w_down, m_ffn_post_norm, v_rel_bias, v_attn_pre_norm, v_w_in, v_b_gate, v_sinks, v_w_br_a, v_w_br_b, v_w_br_c, v_w_out, v_attn_post_norm, v_ffn_pre_norm, v_w_up, v_conv_w, v_conv_b, v_w_down, v_ffn_post_norm):
    names = ("rel_bias", "attn_pre_norm", "w_in", "b_gate", "sinks", "w_br_a", "w_br_b", "w_br_c", "w_out",
             "attn_post_norm", "ffn_pre_norm", "w_up", "conv_w", "conv_b", "w_down", "ffn_post_norm")
    weights = dict(zip(names, (rel_bias, attn_pre_norm, w_in, b_gate, sinks, w_br_a, w_br_b, w_br_c, w_out,
                               attn_post_norm, ffn_pre_norm, w_up, conv_w, conv_b, w_down, ffn_post_norm)))
    mom1 = dict(zip(names, (m_rel_bias, m_attn_pre_norm, m_w_in, m_b_gate, m_sinks, m_w_br_a, m_w_br_b, m_w_br_c,
                            m_w_out, m_attn_post_norm, m_ffn_pre_norm, m_w_up, m_conv_w, m_conv_b, m_w_down,
                            m_ffn_post_norm)))
    mom2 = dict(zip(names, (v_rel_bias, v_attn_pre_norm, v_w_in, v_b_gate, v_sinks, v_w_br_a, v_w_br_b, v_w_br_c,
                            v_w_out, v_attn_post_norm, v_ffn_pre_norm, v_w_up, v_conv_w, v_conv_b, v_w_down,
                            v_ffn_post_norm)))

    chip = 2 * lax.axis_index("x") + lax.axis_index("y")
    core = lax.axis_index("c")

    keys = [(n, l) for l in range(DEPTH) for group in WEIGHT_GROUPS for n in group]
    groups = [[keys.index((n, l)) for n in group] for l in range(DEPTH) for group in WEIGHT_GROUPS]

    def slot_buffer(n, l):
        if n in BIG:
            return _cast_into_slot("cast_" + n, _shard_view(n, weights[n]), l, chip)
        shard = weights[n][l]
        return lax.dynamic_update_slice(jnp.zeros((N_CHIPS,) + shard.shape, F32), shard[None],
                                        (chip, jnp.int32(0), jnp.int32(0)))

    n_first = len(groups[0])
    sems, in_flight, _ = _gather_start("gather_start_first", [slot_buffer(*k) for k in keys[:n_first]], groups[:1])
    more = _gather_start("gather_start", [slot_buffer(*k) for k in keys[n_first:]],
                         [[a - n_first for a in g] for g in groups[1:]])
    sems, in_flight = sems + more[0], in_flight + more[1]

    def wget(l, gi, after):
        g = l * len(WEIGHT_GROUPS) + gi
        got = _gather_wait("gather_wait_%d_%d" % (l, gi), [in_flight[a] for a in groups[g]], *sems[g], after)
        out = {}
        for n, buf in zip(WEIGHT_GROUPS[gi], got):
            out[n] = buf.reshape(-1, buf.shape[-1]) if n in ("w_in", "w_out", "w_down") else _full_cols(buf)
        if gi == len(WEIGHT_GROUPS) - 1:
            out["conv_b"] = conv_b[l:l + 1]
        return out

    pending = []

    def emit(l, gi, grads):
        group = GRAD_GROUPS[gi]
        *started, token = _reduce_start("reduce_start_%d_%d" % (l, gi), [grads[n] for n in group])
        pending.append((l, group) + tuple(started))
        return token[:1, :1]

    local = _local_step(x.reshape(S, D), loss_target.reshape(S, D), wget, emit, rel_bias, sinks, attn_pre_norm,
                        attn_post_norm, ffn_pre_norm, ffn_post_norm)
    return _reduce_and_update(x.shape, names, weights, mom1, mom2, chip, core, pending, *local)


def _local_step(xs, target, wget, emit, rel_bias, sinks, attn_pre_norm, attn_post_norm, ffn_pre_norm, ffn_post_norm):
    bidx = jnp.asarray(_bucket_maps())

    saved, layers = [], []
    h1 = _rms_fwd("pre_norm_first", xs, attn_pre_norm[0:1])
    x_in = xs
    for l in range(DEPTH):
        mo, sv_mix, w = _mixer_fwd(h1, functools.partial(wget, l), rel_bias, sinks[l], bidx)
        x_mid, h2 = _post_pre_fwd("post_attn_norm", x_in, mo, attn_post_norm[l:l + 1], ffn_pre_norm[l:l + 1])
        w.update(wget(l, 2, h2))
        dn, sv_ffn = _ffn_fwd(h2, w)
        g_next = attn_pre_norm[l + 1:l + 2] if l + 1 < DEPTH else None
        x_out, h1_next = _post_pre_fwd("post_ffn_norm" if l + 1 < DEPTH else "post_ffn_norm_last", x_mid, dn,
                                       ffn_post_norm[l:l + 1], g_next)
        saved.append(dict(x_in=x_in, h1=h1, mo=mo, x_mid=x_mid, h2=h2, dn=dn, mix=sv_mix, ffn=sv_ffn))
        layers.append(w)
        x_in, h1 = x_out, h1_next

    loss_row, dres = _loss_kernel(x_in, target)

    small = [None] * DEPTH
    stats = jnp.zeros((N_BAND_Q, 8, 128), F32)
    dh_next = None
    for l in reversed(range(DEPTH)):
        w, sv = layers[l], saved[l]
        if l + 1 < DEPTH:
            pre = (saved[l + 1]["x_in"], attn_pre_norm[l + 1:l + 2] + zero, dh_next)
            dres, d_dn, dg_pre_next, dg_fpost = _norm_bwd("post_ffn_norm_bwd", dres, pre,
                                                          (sv["dn"], ffn_post_norm[l:l + 1]))
            small[l + 1]["attn_pre_norm"] = dg_pre_next
        else:
            dres, d_dn, _, dg_fpost = _norm_bwd("post_ffn_norm_last_bwd", dres, None, (sv["dn"], ffn_post_norm[l:l + 1]))
        dh2, g_ffn = _ffn_bwd(d_dn, sv["h2"], w, sv["ffn"])
        zero = emit(l, 0, g_ffn)
        dres, d_mo, dg_fpre, dg_apost = _norm_bwd("post_attn_norm_bwd", dres,
                                                  (sv["x_mid"], ffn_pre_norm[l:l + 1] + zero, dh2),
                                                  (sv["mo"], attn_post_norm[l:l + 1]))
        dh_next, g_mix, stats, zero = _mixer_bwd(d_mo, sv["h1"], w, sv["mix"], rel_bias, sinks[l], bidx, stats,
                                                 functools.partial(emit, l))
        small[l] = dict(ffn_post_norm=dg_fpost, ffn_pre_norm=dg_fpre, attn_post_norm=dg_apost,
                        sinks=stats[N_A:, 1, 0], conv_b=g_ffn["conv_b"], b_gate=g_mix["b_gate"], conv_w=g_ffn["conv_w"])
    grad_x, _, dg_pre0, _ = _norm_bwd("pre_norm_first_bwd", dres, (saved[0]["x_in"], attn_pre_norm[0:1] + zero, dh_next),
                                      None)
    small[0]["attn_pre_norm"] = dg_pre0
    return loss_row, grad_x, small, stats


def _reduce_and_update(x_shape, names, weights, mom1, mom2, chip, core, pending, loss_row, grad_x, small, stats):
    delta, new_m, new_v, grads = {}, {}, {}, {}

    def update(n, g):
        grads[n] = g
        delta[n], new_m[n], new_v[n] = _adamw("adamw_" + n, _shard_view(n, weights[n]), g,
                                              _shard_view(n, mom1[n]), _shard_view(n, mom2[n]))

    summed = {}

    def finish(which, after):
        for l, group, send, recv, gs, lands in pending:
            if (group == ("w_in",)) == which:
                gs, lands = _reduce_wait("reduce_wait_%d_%s" % (l, group[0]), send, recv, gs, lands, after)
                for n, g, land in zip(group, gs, lands):
                    summed[n] = _reduce_sum("reduce_sum_%d_%s" % (l, n), g, land, l, summed.get(n), chip, core)

    finish(False, grad_x)
    early = [n for n in BIG if n != "w_in"]
    for n, g in zip(early, _join_halves("join_halves", [summed[n] for n in early])):
        update(n, g)
    finish(True, delta[early[-1]])
    update("w_in", _join_halves("join_halves_w_in", [summed["w_in"]])[0])
    for out in (grads, delta, new_m, new_v):
        out["w_in"] = _shard_view("w_in", out["w_in"])

    small_vals = {n: jnp.stack([small[l][n].reshape(weights[n].shape[1:]) for l in range(DEPTH)])
                  for n in ("attn_pre_norm", "attn_post_norm", "ffn_pre_norm", "ffn_post_norm", "conv_b", "sinks")}
    small_vals["b_gate"] = jnp.stack([small[l]["b_gate"] for l in range(DEPTH)])
    small_vals["conv_w"] = jnp.stack([small[l]["conv_w"] for l in range(DEPTH)])
    small_vals["rel_bias"] = stats[:, 0, :NUM_BUCKETS].T
    small_vals["loss"] = loss_row[0, :1]
    shapes = {n: v.shape for n, v in small_vals.items()}
    reduced = _unpack_small(_all_reduce_small(_pack_small(small_vals)), shapes)
    reduced["b_gate"] = lax.dynamic_slice_in_dim(reduced["b_gate"], chip * (D // N_CHIPS), D // N_CHIPS, axis=2)
    reduced["conv_w"] = lax.dynamic_slice_in_dim(reduced["conv_w"], chip * (2 * D_FF // N_CHIPS), 2 * D_FF // N_CHIPS, axis=2)
    for n in names:
        if n not in grads:
            update(n, reduced[n].reshape(weights[n].shape))

    loss = reduced["loss"].reshape(())
    return (loss, grad_x.reshape(x_shape), *[grads[n] for n in names], *[delta[n] for n in names],
            *[new_m[n] for n in names], *[new_v[n] for n in names])
```

```python
import functools
import math

import numpy as np
import jax
import jax.numpy as jnp
from jax import lax
from jax.experimental import pallas as pl
from jax.experimental.pallas import tpu as pltpu

F32 = jnp.float32
BF16 = jnp.bfloat16

S = 2048
D = 1024
DEPTH = 2
HD = 64
BLK = 128
NQB = S // BLK
A_GROUPS = ((128, 1), (512, 4), (2048, 16))
N_BAND_Q = 20
N_A = 12
NUM_BUCKETS = 32
MAX_DISTANCE = 2048
D_FF = 4096
IN_COLS = 6912
IN_SHARD = IN_COLS // 4
OFF_GATE = 3840
EPS = 1e-6
SCALE = HD ** -0.5
NEG = -1e30
N_CHIPS = 4
N_DEV = 8

ADAM_LR = 0.001
ADAM_B1 = 0.9
ADAM_B2 = 0.999
ADAM_EPS = 1e-08
ADAM_WD = 0.01
ADAM_STEP = 10

VMEM_LIMIT = 56 * 1024 * 1024

NN = (((1,), (0,)), ((), ()))
NT = (((1,), (1,)), ((), ()))
TN = (((0,), (0,)), ((), ()))

MESH = pl.DeviceIdType.MESH
ANY = pl.BlockSpec(memory_space=pl.ANY)


def _dot(a, b, dims):
    return lax.dot_general(a, b, dims, preferred_element_type=F32)


def _params(sem):
    return pltpu.CompilerParams(dimension_semantics=sem, vmem_limit_bytes=VMEM_LIMIT)


def _matmul(name, a, b, out_shape, out_dtype, grid, a_spec, b_spec, o_spec, dims, acc_shape):
    nk = grid[-1]

    def body(a_ref, b_ref, o_ref, *scratch):
        part = _dot(a_ref[...].astype(BF16), b_ref[...].astype(BF16), dims)
        if nk == 1:
            o_ref[...] = part.astype(o_ref.dtype)
            return
        acc_ref, = scratch
        k = pl.program_id(len(grid) - 1)

        @pl.when(k == 0)
        def _():
            acc_ref[...] = part

        @pl.when(k > 0)
        def _():
            acc_ref[...] += part

        @pl.when(k == nk - 1)
        def _():
            o_ref[...] = acc_ref[...].astype(o_ref.dtype)

    scratch = [] if nk == 1 else [pltpu.VMEM(acc_shape, F32)]
    sem = ("parallel",) * (len(grid) - 1) + ("arbitrary",)
    return pl.pallas_call(
        body, name=name, grid=grid, in_specs=[a_spec, b_spec], out_specs=o_spec,
        out_shape=jax.ShapeDtypeStruct(out_shape, out_dtype), scratch_shapes=scratch,
        compiler_params=_params(sem))(a, b)


FULL_K = 8192


def _mm_tn_sharded(name, a, b, row_sharded, tm=512, tn=512, tk=FULL_K):
    k, m = a.shape
    n = b.shape[1]
    m4, n4 = (m // N_CHIPS, n) if row_sharded else (m, n // N_CHIPS)
    tm, tn, tk = min(tm, m4), min(tn, n4), min(tk, k)
    per_m, per_n = m4 // tm, n4 // tn
    if row_sharded:
        o_map = lambda i, j, l: (i // per_m, i % per_m, j)
    else:
        o_map = lambda i, j, l: (j // per_n, i, j % per_n)
    return _matmul(name, a, b, (N_CHIPS, m4, n4), BF16, (m // tm, n // tn, k // tk),
                   pl.BlockSpec((tk, tm), lambda i, j, l: (l, i)),
                   pl.BlockSpec((tk, tn), lambda i, j, l: (l, j)),
                   pl.BlockSpec((None, tm, tn), o_map), TN, (tm, tn))


def _mm_nn(name, a, b, out_dtype, tm=512, tn=512, tk=FULL_K):
    m, k = a.shape
    n = b.shape[1]
    tm, tn, tk = min(tm, m), min(tn, n), min(tk, k)
    return _matmul(name, a, b, (m, n), out_dtype, (m // tm, n // tn, k // tk),
                   pl.BlockSpec((tm, tk), lambda i, j, l: (i, l)),
                   pl.BlockSpec((tk, tn), lambda i, j, l: (l, j)),
                   pl.BlockSpec((tm, tn), lambda i, j, l: (i, j)), NN, (tm, tn))


def _mm_nt(name, a, b, out_dtype, tm=512, tn=512, tk=FULL_K):
    m, k = a.shape
    n = b.shape[0]
    tm, tn, tk = min(tm, m), min(tn, n), min(tk, k)
    return _matmul(name, a, b, (m, n), out_dtype, (m // tm, n // tn, k // tk),
                   pl.BlockSpec((tm, tk), lambda i, j, l: (i, l)),
                   pl.BlockSpec((tn, tk), lambda i, j, l: (j, l)),
                   pl.BlockSpec((tm, tn), lambda i, j, l: (i, j)), NT, (tm, tn))


def _mm_tn(name, a, b, out_dtype, tm=512, tn=512, tk=FULL_K):
    k, m = a.shape
    n = b.shape[1]
    tm, tn, tk = min(tm, m), min(tn, n), min(tk, k)
    return _matmul(name, a, b, (m, n), out_dtype, (m // tm, n // tn, k // tk),
                   pl.BlockSpec((tk, tm), lambda i, j, l: (l, i)),
                   pl.BlockSpec((tk, tn), lambda i, j, l: (l, j)),
                   pl.BlockSpec((tm, tn), lambda i, j, l: (i, j)), TN, (tm, tn))


TR = 256


def _row_spec(width=D):
    return pl.BlockSpec((TR, width), lambda i: (i, 0))


def _vec_spec(width=D):
    return pl.BlockSpec((1, width), lambda i: (0, 0))


def _rms(x, g):
    r = lax.rsqrt(jnp.mean(x * x, axis=-1, keepdims=True) + EPS)
    return x * r * g


def _rms_fwd(name, x, g):
    def body(x_ref, g_ref, h_ref):
        h_ref[...] = _rms(x_ref[...], g_ref[...]).astype(BF16)

    return pl.pallas_call(
        body, name=name, grid=(S // TR,), in_specs=[_row_spec(), _vec_spec()], out_specs=_row_spec(),
        out_shape=jax.ShapeDtypeStruct((S, D), BF16), compiler_params=_params(("parallel",)))(x, g)


def _post_pre_fwd(name, x, y, g_post, g_pre):
    has_pre = g_pre is not None

    def body(*refs):
        if has_pre:
            x_ref, y_ref, gp_ref, gn_ref, xn_ref, h_ref = refs
        else:
            x_ref, y_ref, gp_ref, xn_ref = refs
        xn = x_ref[...] + _rms(y_ref[...], gp_ref[...])
        xn_ref[...] = xn
        if has_pre:
            h_ref[...] = _rms(xn, gn_ref[...]).astype(BF16)

    ins = [x, y, g_post] + ([g_pre] if has_pre else [])
    in_specs = [_row_spec(), _row_spec(), _vec_spec()] + ([_vec_spec()] if has_pre else [])
    out_shape = [jax.ShapeDtypeStruct((S, D), F32)] + ([jax.ShapeDtypeStruct((S, D), BF16)] if has_pre else [])
    out_specs = [_row_spec()] + ([_row_spec()] if has_pre else [])
    out = pl.pallas_call(
        body, name=name, grid=(S // TR,), in_specs=in_specs, out_specs=out_specs, out_shape=out_shape,
        compiler_params=_params(("parallel",)))(*ins)
    return out if has_pre else (out[0], None)


def _rms_bwd_math(x, g, dy):
    r = lax.rsqrt(jnp.mean(x * x, axis=-1, keepdims=True) + EPS)
    n = x * r
    dn = dy * g
    dx = r * (dn - n * jnp.mean(dn * n, axis=-1, keepdims=True))
    return dx, jnp.sum(dy * n, axis=0, keepdims=True)


def _norm_bwd(name, dres, pre=None, post=None):
    has_pre, has_post = pre is not None, post is not None

    def body(*refs):
        refs = list(refs)
        dres_ref = refs.pop(0)
        if has_pre:
            xn_ref, gn_ref, dh_ref = refs[:3]
            refs = refs[3:]
        if has_post:
            y_ref, gp_ref = refs[:2]
            refs = refs[2:]
        dxn_ref = refs.pop(0)
        dy_ref = refs.pop(0) if has_post else None
        dgn_ref = refs.pop(0) if has_pre else None
        dgp_ref = refs.pop(0) if has_post else None
        first = pl.program_id(0) == 0
        dxn = dres_ref[...]
        if has_pre:
            dx, dg = _rms_bwd_math(xn_ref[...], gn_ref[...], dh_ref[...])
            dxn = dxn + dx

            @pl.when(first)
            def _():
                dgn_ref[...] = dg

            @pl.when(jnp.logical_not(first))
            def _():
                dgn_ref[...] += dg
        dxn_ref[...] = dxn
        if has_post:
            dy, dg = _rms_bwd_math(y_ref[...], gp_ref[...], dxn)
            dy_ref[...] = dy.astype(BF16)

            @pl.when(first)
            def _():
                dgp_ref[...] = dg

            @pl.when(jnp.logical_not(first))
            def _():
                dgp_ref[...] += dg

    ins, in_specs = [dres], [_row_spec()]
    if has_pre:
        ins += list(pre)
        in_specs += [_row_spec(), _vec_spec(), _row_spec()]
    if has_post:
        ins += list(post)
        in_specs += [_row_spec(), _vec_spec()]
    out_shape, out_specs = [jax.ShapeDtypeStruct((S, D), F32)], [_row_spec()]
    if has_post:
        out_shape.append(jax.ShapeDtypeStruct((S, D), BF16))
        out_specs.append(_row_spec())
    for _ in range(int(has_pre) + int(has_post)):
        out_shape.append(jax.ShapeDtypeStruct((1, D), F32))
        out_specs.append(_vec_spec())
    out = list(pl.pallas_call(
        body, name=name, grid=(S // TR,), in_specs=in_specs, out_specs=out_specs, out_shape=out_shape,
        compiler_params=_params(("arbitrary",)))(*ins))
    dxn = out.pop(0)
    dy = out.pop(0) if has_post else None
    dgn = out.pop(0) if has_pre else None
    dgp = out.pop(0) if has_post else None
    return dxn, dy, dgn, dgp


def _loss_kernel(y, target):
    def body(y_ref, t_ref, loss_ref, dy_ref):
        e = y_ref[...] - t_ref[...]
        dy_ref[...] = e * (1.0 / D)
        part = jnp.zeros((1, 128), F32) + 0.5 * jnp.sum(jnp.mean(e * e, axis=-1, keepdims=True))

        @pl.when(pl.program_id(0) == 0)
        def _():
            loss_ref[...] = part

        @pl.when(pl.program_id(0) > 0)
        def _():
            loss_ref[...] += part

    return pl.pallas_call(
        body, name="loss", grid=(S // TR,), in_specs=[_row_spec(), _row_spec()],
        out_specs=[_vec_spec(128), _row_spec()],
        out_shape=[jax.ShapeDtypeStruct((1, 128), F32), jax.ShapeDtypeStruct((S, D), F32)],
        compiler_params=_params(("arbitrary",)))(y, target)


def _t5_bucket_np(dist):
    max_exact = NUM_BUCKETS // 2
    nf = np.maximum(dist, 1).astype(np.float32)
    large = max_exact + (np.log(nf / max_exact) / np.float32(math.log(MAX_DISTANCE / max_exact))
                         * (NUM_BUCKETS - max_exact)).astype(np.int32)
    large = np.minimum(large, NUM_BUCKETS - 1)
    return np.where(dist < max_exact, dist, large).astype(np.int32)


def _bucket_maps():
    a = np.arange(BLK)[:, None]
    b = np.arange(2 * BLK)[None, :]
    dist = np.maximum(a + BLK - b, 0)
    maps = [_t5_bucket_np(dist * d) for _, d in A_GROUPS] + [_t5_bucket_np(dist)]
    return np.stack(maps).astype(np.int32)


def _classes(arr, d):
    return arr.reshape(S // d, d * arr.shape[1])


def _band_spec(arr, col0, prev):
    ncol = arr.shape[1] // 128
    if prev:
        return pl.BlockSpec((BLK, 128), lambda p, r, b: (jnp.maximum(b - 1, 0), r * ncol + col0 + p))
    return pl.BlockSpec((BLK, 128), lambda p, r, b: (b, r * ncol + col0 + p))


def _band_bias(tab_ref, bidx_ref, h):
    bi = bidx_ref[...]
    bias = jnp.zeros((BLK, 2 * BLK), F32)
    for kk in range(NUM_BUCKETS):
        bias = jnp.where(bi == kk, tab_ref[kk, h], bias)
    return bias


def _lane_lo(rows=BLK):
    return lax.broadcasted_iota(jnp.int32, (rows, 128), 1) < HD


def _per_head(x, lo):
    return (jnp.sum(jnp.where(lo, x, 0.0), axis=1, keepdims=True) * (1.0 / HD),
            jnp.sum(jnp.where(lo, 0.0, x), axis=1, keepdims=True) * (1.0 / HD))


def _band_fill(bias_ref, tab_ref, bidx_ref, head, maxd):
    a = lax.broadcasted_iota(jnp.int32, (BLK, 2 * BLK), 0)
    c = lax.broadcasted_iota(jnp.int32, (BLK, 2 * BLK), 1)
    dist = a + BLK - c
    in_band = jnp.logical_and(dist >= 0, dist <= maxd)
    for h in range(2):
        bias = jnp.where(in_band, _band_bias(tab_ref, bidx_ref, head + h), NEG)
        bias_ref[1, h * BLK:(h + 1) * BLK, :] = bias
        bias_ref[0, h * BLK:(h + 1) * BLK, :] = jnp.where(c >= BLK, bias, NEG)


def _stack_heads(x, lo, dtype=BF16):
    return jnp.concatenate([jnp.where(lo, x, 0.0), jnp.where(lo, 0.0, x)], axis=0).astype(dtype)


def _unstack_heads(x, lo):
    n = x.shape[0] // 2
    return jnp.where(lo, x[:n], x[n:])


def _stack_rows(prev_ref, cur_ref):
    return jnp.concatenate([prev_ref[...], cur_ref[...]], axis=0).astype(BF16)


def _band_fwd(name, d, n_pairs, maxd, head0, srcs, bidx_g, tab, sinks):
    nb = S // d // BLK
    (qa, qc), (ka, kc), (va, vc) = srcs
    out_spec = pl.BlockSpec((BLK, 128), lambda p, r, b: (b, r * n_pairs + p))
    smem = pl.BlockSpec(memory_space=pltpu.SMEM)
    full = pl.BlockSpec((BLK, 2 * BLK), lambda p, r, b: (0, 0))

    def body(tab_ref, sink_ref, q_ref, kp_ref, kc_ref, vp_ref, vc_ref, bidx_ref, o_ref, lse_ref, bias_ref):
        p, r, b = pl.program_id(0), pl.program_id(1), pl.program_id(2)

        @pl.when(jnp.logical_and(r == 0, b == 0))
        def _():
            _band_fill(bias_ref, tab_ref, bidx_ref, head0 + 2 * p, maxd)

        lo = _lane_lo()
        qs = _stack_heads(q_ref[...] * SCALE, lo)
        ks, vs = _stack_rows(kp_ref, kc_ref), _stack_rows(vp_ref, vc_ref)
        s = _dot(qs, ks, NT) + bias_ref[jnp.minimum(b, 1)]
        m = jnp.max(s, axis=1, keepdims=True)
        pr = jnp.exp(s - m)
        l = jnp.sum(pr, axis=1, keepdims=True)
        num = _dot(pr.astype(BF16), vs, NN)
        lse = m + jnp.log(l)
        sink = jnp.where(lax.broadcasted_iota(jnp.int32, (2 * BLK, 1), 0) < BLK, sink_ref[2 * p], sink_ref[2 * p + 1])
        sig = 1.0 / (1.0 + jnp.exp(sink - lse))
        o_ref[...] = _unstack_heads(num * (sig / l), lo)
        lse_ref[...] = _unstack_heads(lse + jnp.zeros((2 * BLK, 128), F32), lo)

    shape = jax.ShapeDtypeStruct((S // d, d * n_pairs * 128), F32)
    o, lse = pl.pallas_call(
        body, name=name, grid=(n_pairs, d, nb),
        in_specs=[smem, smem, _band_spec(qa, qc, False), _band_spec(ka, kc, True), _band_spec(ka, kc, False),
                  _band_spec(va, vc, True), _band_spec(va, vc, False), full],
        out_specs=[out_spec, out_spec], out_shape=[shape, shape],
        scratch_shapes=[pltpu.VMEM((2, 2 * BLK, 2 * BLK), F32)],
        compiler_params=_params(("parallel", "arbitrary", "arbitrary")))(
            tab, sinks, _classes(qa, d), _classes(ka, d), _classes(ka, d), _classes(va, d), _classes(va, d), bidx_g)
    return o.reshape(S, n_pairs * 128), lse.reshape(S, n_pairs * 128)


def _band_bwd(name, d, n_pairs, maxd, head0, srcs, bidx_g, tab, sinks, o, lse, do, stats_in):
    nb = S // d // BLK
    rows = S // d
    (qa, qc), (ka, kc), (va, vc) = srcs
    blk_spec = pl.BlockSpec((BLK, 128), lambda p, r, b: (b, r * n_pairs + p))
    cls_spec = pl.BlockSpec((rows, 128), lambda p, r, b: (0, r * n_pairs + p))
    smem = pl.BlockSpec(memory_space=pltpu.SMEM)
    full = pl.BlockSpec((BLK, 2 * BLK), lambda p, r, b: (0, 0))
    stat_spec = pl.BlockSpec((2, 8, 128), lambda p, r, b: (p, 0, 0))

    def body(tab_ref, sink_ref, q_ref, kp_ref, kc_ref, vp_ref, vc_ref, bidx_ref, o_ref, lse_ref, do_ref, sin_ref,
             dq_ref, dk_ref, dv_ref, stat_ref, bias_ref, dsacc_ref, sk_ref):
        p, r, b = pl.program_id(0), pl.program_id(1), pl.program_id(2)

        @pl.when(jnp.logical_and(r == 0, b == 0))
        def _():
            _band_fill(bias_ref, tab_ref, bidx_ref, head0 + 2 * p, maxd)
            dsacc_ref[...] = jnp.zeros_like(dsacc_ref)
            sk_ref[...] = jnp.zeros_like(sk_ref)

        @pl.when(b == 0)
        def _():
            dk_ref[...] = jnp.zeros_like(dk_ref)
            dv_ref[...] = jnp.zeros_like(dv_ref)

        lo = _lane_lo()
        qs = _stack_heads(q_ref[...] * SCALE, lo)
        ks, vs = _stack_rows(kp_ref, kc_ref), _stack_rows(vp_ref, vc_ref)
        do = do_ref[...]
        dos = _stack_heads(do, lo, F32)
        lse = jnp.concatenate(_per_head(lse_ref[...], lo), axis=0)
        prod = do * o_ref[...]
        delta = jnp.concatenate([jnp.sum(jnp.where(lo, prod, 0.0), axis=1, keepdims=True),
                                 jnp.sum(jnp.where(lo, 0.0, prod), axis=1, keepdims=True)], axis=0)
        head1 = lax.broadcasted_iota(jnp.int32, (2 * BLK, 1), 0) >= BLK
        sig = 1.0 / (1.0 + jnp.exp(jnp.where(head1, sink_ref[2 * p + 1], sink_ref[2 * p]) - lse))
        pr = jnp.exp(_dot(qs, ks, NT) + bias_ref[jnp.minimum(b, 1)] - lse)
        ds = pr * (sig * (_dot(dos.astype(BF16), vs, NT) - delta))
        dsb = ds.astype(BF16)
        dq_ref[...] = SCALE * _unstack_heads(_dot(dsb, ks, NN), lo)
        dk = _dot(dsb, qs, TN)
        dv = _dot(pr.astype(BF16), (sig * dos).astype(BF16), TN)
        cur = pl.ds(pl.multiple_of(b * BLK, BLK), BLK)
        prev = pl.ds(pl.multiple_of(jnp.maximum(b - 1, 0) * BLK, BLK), BLK)
        dk_ref[prev, :] += dk[:BLK]
        dk_ref[cur, :] += dk[BLK:]
        dv_ref[prev, :] += dv[:BLK]
        dv_ref[cur, :] += dv[BLK:]
        dsacc_ref[...] += ds
        sink_grad = -delta * (1.0 - sig)
        for h in range(2):
            sk_ref[h] += jnp.zeros((8, 128), F32) + jnp.sum(sink_grad[h * BLK:(h + 1) * BLK])

        @pl.when(jnp.logical_and(r == d - 1, b == nb - 1))
        def _():
            bi = bidx_ref[...]
            lane = lax.broadcasted_iota(jnp.int32, (8, 128), 1)
            sub = lax.broadcasted_iota(jnp.int32, (8, 128), 0)
            for h in range(2):
                acc = dsacc_ref[h * BLK:(h + 1) * BLK, :]
                row = jnp.where(jnp.logical_and(sub == 1, lane == 0), sk_ref[h], 0.0)
                for kk in range(NUM_BUCKETS):
                    tot = jnp.sum(jnp.where(bi == kk, acc, 0.0))
                    row = jnp.where(jnp.logical_and(sub == 0, lane == kk), tot, row)
                stat_ref[h] = row + jnp.where(sub == 0, sin_ref[h], 0.0)

    shape = jax.ShapeDtypeStruct((rows, d * n_pairs * 128), F32)
    dq, dk, dv, stats = pl.pallas_call(
        body, name=name, grid=(n_pairs, d, nb),
        in_specs=[smem, smem, _band_spec(qa, qc, False), _band_spec(ka, kc, True), _band_spec(ka, kc, False),
                  _band_spec(va, vc, True), _band_spec(va, vc, False), full, blk_spec, blk_spec, blk_spec, stat_spec],
        out_specs=[blk_spec, cls_spec, cls_spec, stat_spec],
        out_shape=[shape, shape, shape, jax.ShapeDtypeStruct((2 * n_pairs, 8, 128), F32)],
        scratch_shapes=[pltpu.VMEM((2, 2 * BLK, 2 * BLK), F32), pltpu.VMEM((2 * BLK, 2 * BLK), F32),
                        pltpu.VMEM((2, 8, 128), F32)],
        compiler_params=_params(("arbitrary", "arbitrary", "arbitrary")))(
            tab, sinks, _classes(qa, d), _classes(ka, d), _classes(ka, d), _classes(va, d), _classes(va, d), bidx_g,
            _classes(o, d), _classes(lse, d), _classes(do, d), stats_in)
    width = n_pairs * 128
    return dq.reshape(S, width), dk.reshape(S, width), dv.reshape(S, width), stats


def _comb_fwd(o_g, lse_g):
    def body(o0, o1, o2, l0, l1, l2, out_ref, outb_ref, lse_ref):
        a0, a1, a2 = l0[...], l1[...], l2[...]
        m = jnp.maximum(jnp.maximum(a0, a1), a2)
        e0, e1, e2 = jnp.exp(a0 - m), jnp.exp(a1 - m), jnp.exp(a2 - m)
        tot = e0 + e1 + e2
        out = (e0 * o0[...] + e1 * o1[...] + e2 * o2[...]) / tot
        out_ref[...] = out
        outb_ref[...] = out.astype(BF16)
        lse_ref[...] = m + jnp.log(tot)

    spec = _row_spec(4 * HD)
    f32 = jax.ShapeDtypeStruct((S, 4 * HD), F32)
    return pl.pallas_call(
        body, name="comb_fwd", grid=(S // TR,), in_specs=[spec] * 6, out_specs=[spec] * 3,
        out_shape=[f32, jax.ShapeDtypeStruct((S, 4 * HD), BF16), f32],
        compiler_params=_params(("parallel",)))(*o_g, *lse_g)


def _split2(x):
    hi = x.astype(BF16)
    return hi, (x - hi.astype(F32)).astype(BF16)


KB = 2 * BLK
SBQ = 2 * BLK


def _tri_sum(x, tri):
    hi, lo = _split2(x)
    both = _dot(jnp.concatenate([hi, lo], axis=0), tri, NN)
    return both[:x.shape[0]] + both[x.shape[0]:]


def _tri(strict_upper):
    r = lax.broadcasted_iota(jnp.int32, (KB, KB), 0)
    c = lax.broadcasted_iota(jnp.int32, (KB, KB), 1)
    return jnp.where(r > c if strict_upper else r < c, 1.0, 0.0).astype(BF16)


def _sb_terms(qs, kj, before):
    z = _dot(qs, kj, NT)
    lsp = jnp.minimum(z, 0.0) - jnp.log(1.0 + jnp.exp(-jnp.abs(z)))
    return lsp, jnp.where(before, lsp - z, 0.0)


def _sb_before(i, m):
    t = (lax.broadcasted_iota(jnp.int32, (2 * SBQ, KB), 0) & (SBQ - 1)) + i * SBQ
    s = lax.broadcasted_iota(jnp.int32, (2 * SBQ, KB), 1) + m * KB
    return s < t


C_COL = 3072 // 128


def _sb_fwd(proj):
    blk = lambda off: pl.BlockSpec((SBQ, 128), lambda p, i: (i, off + p))
    col = lambda off: pl.BlockSpec((S, 128), lambda p, i: (0, off + p))
    out = pl.BlockSpec((SBQ, 128), lambda p, i: (i, p))

    def body(q_ref, k_ref, v_ref, o_ref, ob_ref, tot_ref):
        i = pl.program_id(1)
        lo = _lane_lo(SBQ)
        qs = _stack_heads(q_ref[...] * SCALE, lo)
        suffix = _tri(True)

        def step(n, carry):
            acc, rest = carry
            m = i - n
            rows = pl.ds(pl.multiple_of(m * KB, KB), KB)
            kj, vj = k_ref[rows, :].astype(BF16), v_ref[rows, :].astype(BF16)
            before = _sb_before(i, m)
            lsp, lk = _sb_terms(qs, kj, before)
            w = jnp.where(before, jnp.exp(lsp + _tri_sum(lk, suffix) + rest), 0.0)
            return acc + _dot(w.astype(BF16), vj, NN), rest + jnp.sum(lk, axis=1, keepdims=True)

        acc, rest = lax.fori_loop(0, i + 1, step, (jnp.zeros((2 * SBQ, 128), F32), jnp.zeros((2 * SBQ, 1), F32)))
        o = _unstack_heads(acc, lo)
        o_ref[...] = o
        ob_ref[...] = o.astype(BF16)
        tot_ref[...] = _unstack_heads(rest + jnp.zeros((2 * SBQ, 128), F32), lo)

    f32 = jax.ShapeDtypeStruct((S, 4 * HD), F32)
    return pl.pallas_call(
        body, name="sb_fwd", grid=(2, S // SBQ), in_specs=[blk(C_COL), col(C_COL + 2), col(C_COL + 4)],
        out_specs=[out, out, out], out_shape=[f32, jax.ShapeDtypeStruct((S, 4 * HD), BF16), f32],
        compiler_params=_params(("parallel", "arbitrary")))(proj, proj, proj)


def _sb_bwd(proj, tot, do):
    blk = lambda off: pl.BlockSpec((SBQ, 128), lambda p, i: (i, off + p))
    col = lambda off: pl.BlockSpec((S, 128), lambda p, i: (0, off + p))

    def body(q_ref, k_ref, v_ref, tot_ref, do_ref, dq_ref, dk_ref, dv_ref):
        i = pl.program_id(1)

        @pl.when(i == 0)
        def _():
            dk_ref[...] = jnp.zeros_like(dk_ref)
            dv_ref[...] = jnp.zeros_like(dv_ref)

        lo = _lane_lo(SBQ)
        qs = _stack_heads(q_ref[...] * SCALE, lo)
        dos = _stack_heads(do_ref[...], lo)
        tots = jnp.concatenate(_per_head(tot_ref[...], lo), axis=0)
        prefix = _tri(False)

        def step(m, carry):
            dq, keep_left, g_left = carry
            rows = pl.ds(pl.multiple_of(m * KB, KB), KB)
            kj, vj = k_ref[rows, :].astype(BF16), v_ref[rows, :].astype(BF16)
            before = _sb_before(i, m)
            lsp, lk = _sb_terms(qs, kj, before)
            log_rest = tots - keep_left - lk - _tri_sum(lk, prefix)
            w = jnp.where(before, jnp.exp(lsp + log_rest), 0.0)
            g = w * _dot(dos, vj, NT)
            g_before = g_left + _dot(g.astype(BF16), prefix, NN)
            beta = jnp.exp(lsp)
            dz = jnp.where(before, g * (1.0 - beta) - g_before * beta, 0.0).astype(BF16)
            dk_ref[rows, :] += _dot(dz, qs, TN)
            dv_ref[rows, :] += _dot(w.astype(BF16), dos, TN)
            return (dq + _dot(dz, kj, NN), keep_left + jnp.sum(lk, axis=1, keepdims=True),
                    g_left + jnp.sum(g, axis=1, keepdims=True))

        zero = (jnp.zeros((2 * SBQ, 128), F32), jnp.zeros((2 * SBQ, 1), F32), jnp.zeros((2 * SBQ, 1), F32))
        dq, _, _ = lax.fori_loop(0, i + 1, step, zero)
        dq_ref[...] = SCALE * _unstack_heads(dq, lo)

    out_blk = pl.BlockSpec((SBQ, 128), lambda p, i: (i, p))
    out_col = pl.BlockSpec((S, 128), lambda p, i: (0, p))
    f32 = jax.ShapeDtypeStruct((S, 4 * HD), F32)
    return pl.pallas_call(
        body, name="sb_bwd", grid=(2, S // SBQ),
        in_specs=[blk(C_COL), col(C_COL + 2), col(C_COL + 4), out_blk, out_blk],
        out_specs=[out_blk, out_col, out_col], out_shape=[f32, f32, f32],
        compiler_params=_params(("arbitrary", "arbitrary")))(proj, proj, proj, tot, do)


TG = 256
GATE_BLK0 = OFF_GATE // TG


def _gate_specs():
    grid = (D // TG, S // TG)
    p_specs = [pl.BlockSpec((TG, TG), functools.partial(lambda c, r, br: (r, GATE_BLK0 + br * (D // TG) + c), br=br))
               for br in range(3)]
    b_spec = pl.BlockSpec((3, TG), lambda c, r: (0, c))
    t_spec = pl.BlockSpec((TG, TG), lambda c, r: (r, c))
    return grid, p_specs, b_spec, t_spec


def _sigmoid(x):
    return 1.0 / (1.0 + jnp.exp(-x))


def _three_rows(rows):
    sub = lax.broadcasted_iota(jnp.int32, (3, rows[0].shape[1]), 0)
    return jnp.where(sub == 0, rows[0], jnp.where(sub == 1, rows[1], rows[2]))


def _gate_fwd(proj, b_gate, br):
    grid, p_specs, b_spec, t_spec = _gate_specs()

    def body(p0, p1, p2, b_ref, r0, r1, r2, out_ref):
        acc = jnp.zeros((TG, TG), F32)
        for n, (p, r) in enumerate(((p0, r0), (p1, r1), (p2, r2))):
            acc += _sigmoid(p[...] + b_ref[n:n + 1, :]) * r[...]
        out_ref[...] = acc.astype(BF16)

    return pl.pallas_call(
        body, name="gate_fwd", grid=grid, in_specs=p_specs + [b_spec] + [t_spec] * 3, out_specs=t_spec,
        out_shape=jax.ShapeDtypeStruct((S, D), BF16),
        compiler_params=_params(("parallel", "parallel")))(proj, proj, proj, b_gate, *br)


def _gate_bwd(proj, b_gate, br, dmerged):
    grid, p_specs, b_spec, t_spec = _gate_specs()

    def body(p0, p1, p2, b_ref, r0, r1, r2, dm_ref, e0, e1, e2, g0, g1, g2, db_ref):
        dm = dm_ref[...]
        rows = []
        for n, (p, r, e_ref, dg_ref) in enumerate(((p0, r0, e0, g0), (p1, r1, e1, g1), (p2, r2, e2, g2))):
            g = _sigmoid(p[...] + b_ref[n:n + 1, :])
            e_ref[...] = (dm * g).astype(BF16)
            dpre = dm * r[...] * g * (1.0 - g)
            dg_ref[...] = dpre.astype(BF16)
            rows.append(jnp.sum(dpre, axis=0, keepdims=True))
        db = _three_rows(rows)

        @pl.when(pl.program_id(1) == 0)
        def _():
            db_ref[...] = db

        @pl.when(pl.program_id(1) > 0)
        def _():
            db_ref[...] += db

    bf = jax.ShapeDtypeStruct((S, D), BF16)
    out = pl.pallas_call(
        body, name="gate_bwd", grid=grid, in_specs=p_specs + [b_spec] + [t_spec] * 4,
        out_specs=[t_spec] * 6 + [b_spec], out_shape=[bf] * 6 + [jax.ShapeDtypeStruct((3, D), F32)],
        compiler_params=_params(("parallel", "arbitrary")))(proj, proj, proj, b_gate, *br, dmerged)
    return out[:3], out[3:6], out[6]


TC = 256
N_FF_BLK = D_FF // TC
GELU_C = math.sqrt(2.0 / math.pi)


def _shift_down(x, n):
    rows = lax.broadcasted_iota(jnp.int32, x.shape, 0)
    return jnp.where(rows >= n, pltpu.roll(x, n, axis=0), 0.0)


def _shift_up(x, n):
    rows = lax.broadcasted_iota(jnp.int32, x.shape, 0)
    return jnp.where(rows < x.shape[0] - n, pltpu.roll(x, x.shape[0] - n, axis=0), 0.0)


def _conv(u, w, b):
    s1, s2 = _shift_down(u, 1), _shift_down(u, 2)
    return w[2:3, :] * u + w[1:2, :] * s1 + w[0:1, :] * s2 + b, s1, s2


def _gelu_parts(x):
    inner = GELU_C * (x + 0.044715 * x * x * x)
    t = jnp.tanh(inner)
    gelu = 0.5 * x * (1.0 + t)
    dgelu = 0.5 * (1.0 + t) + 0.5 * x * (1.0 - t * t) * GELU_C * (1.0 + 3 * 0.044715 * x * x)
    return gelu, dgelu


def _conv_specs():
    ug = pl.BlockSpec((S, TC), lambda c: (0, c))
    uv = pl.BlockSpec((S, TC), lambda c: (0, N_FF_BLK + c))
    wg = pl.BlockSpec((3, TC), lambda c: (0, c))
    wv = pl.BlockSpec((3, TC), lambda c: (0, N_FF_BLK + c))
    bg = pl.BlockSpec((1, TC), lambda c: (0, c))
    bv = pl.BlockSpec((1, TC), lambda c: (0, N_FF_BLK + c))
    return ug, uv, wg, wv, bg, bv


def _conv_fwd(u, conv_w, conv_b):
    ug, uv, wg, wv, bg, bv = _conv_specs()

    def body(ug_ref, uv_ref, wg_ref, wv_ref, bg_ref, bv_ref, a_ref):
        gc = _conv(ug_ref[...], wg_ref[...], bg_ref[...])[0]
        vc = _conv(uv_ref[...], wv_ref[...], bv_ref[...])[0]
        a_ref[...] = (_gelu_parts(gc)[0] * vc).astype(BF16)

    return pl.pallas_call(
        body, name="conv_fwd", grid=(N_FF_BLK,), in_specs=[ug, uv, wg, wv, bg, bv], out_specs=ug,
        out_shape=jax.ShapeDtypeStruct((S, D_FF), BF16),
        compiler_params=_params(("parallel",)))(u, u, conv_w, conv_w, conv_b, conv_b)


def _conv_bwd(u, conv_w, conv_b, da):
    ug, uv, wg, wv, bg, bv = _conv_specs()

    def back(duc, u, s1, s2, w):
        du = w[2:3, :] * duc + w[1:2, :] * _shift_up(duc, 1) + w[0:1, :] * _shift_up(duc, 2)
        dw = _three_rows([jnp.sum(duc * s2, axis=0, keepdims=True), jnp.sum(duc * s1, axis=0, keepdims=True),
                          jnp.sum(duc * u, axis=0, keepdims=True)])
        return du, dw, jnp.sum(duc, axis=0, keepdims=True)

    def body(ug_ref, uv_ref, wg_ref, wv_ref, bg_ref, bv_ref, da_ref, dug_ref, duv_ref, dwg_ref, dwv_ref, dbg_ref, dbv_ref):
        u_g, u_v = ug_ref[...], uv_ref[...]
        gc, g1, g2 = _conv(u_g, wg_ref[...], bg_ref[...])
        vc, v1, v2 = _conv(u_v, wv_ref[...], bv_ref[...])
        gelu, dgelu = _gelu_parts(gc)
        da = da_ref[...]
        du, dw, db = back(da * vc * dgelu, u_g, g1, g2, wg_ref[...])
        dug_ref[...] = du.astype(BF16)
        dwg_ref[...] = dw
        dbg_ref[...] = db
        du, dw, db = back(da * gelu, u_v, v1, v2, wv_ref[...])
        duv_ref[...] = du.astype(BF16)
        dwv_ref[...] = dw
        dbv_ref[...] = db

    return pl.pallas_call(
        body, name="conv_bwd", grid=(N_FF_BLK,), in_specs=[ug, uv, wg, wv, bg, bv, ug],
        out_specs=[ug, ug, wg, wg, bg, bg],
        out_shape=[jax.ShapeDtypeStruct((S, D_FF), BF16), jax.ShapeDtypeStruct((S, D_FF), BF16),
                   jax.ShapeDtypeStruct((3, D_FF), F32), jax.ShapeDtypeStruct((3, D_FF), F32),
                   jax.ShapeDtypeStruct((1, D_FF), F32), jax.ShapeDtypeStruct((1, D_FF), F32)],
        compiler_params=_params(("parallel",)))(u, u, conv_w, conv_w, conv_b, conv_b, da)


def _adamw(name, w, g, m, v):
    shape = w.shape
    cols = shape[-1]
    flat = [t.reshape(-1, cols) for t in (w, g, m, v)]
    r = flat[0].shape[0]
    tr = min(128, r)

    def body(w_ref, g_ref, m_ref, v_ref, d_ref, mo_ref, vo_ref):
        g = g_ref[...]
        m = ADAM_B1 * m_ref[...] + (1.0 - ADAM_B1) * g
        v = ADAM_B2 * v_ref[...] + (1.0 - ADAM_B2) * (g * g)
        m_hat = m / (1.0 - ADAM_B1 ** ADAM_STEP)
        v_hat = v / (1.0 - ADAM_B2 ** ADAM_STEP)
        d_ref[...] = -ADAM_LR * (m_hat / (jnp.sqrt(v_hat) + ADAM_EPS) + ADAM_WD * w_ref[...])
        mo_ref[...] = m
        vo_ref[...] = v

    spec = pl.BlockSpec((tr, cols), lambda i: (i, 0))
    outs = pl.pallas_call(
        body, name=name, grid=(pl.cdiv(r, tr),), in_specs=[spec] * 4, out_specs=[spec] * 3,
        out_shape=[jax.ShapeDtypeStruct((r, cols), F32)] * 3, compiler_params=_params(("parallel",)))(*flat)
    return [t.reshape(shape) for t in outs]


def _place():
    x, y, c = lax.axis_index("x"), lax.axis_index("y"), lax.axis_index("c")
    chips = [(1 - x, y), (x, 1 - y), (1 - x, 1 - y)]
    return x, y, c, chips


def _scalars(*vals):
    return jnp.stack([jnp.asarray(v, jnp.int32) for v in vals])


HBM = pl.BlockSpec(memory_space=pltpu.HBM)
SEM = pl.BlockSpec(memory_space=pltpu.SEMAPHORE)
SPLIT_COPY = pltpu.CompilerParams(has_side_effects=pltpu.SideEffectType.DATAFLOW_SIDE_EFFECTING)


def _in_hbm(x):
    return pltpu.with_memory_space_constraint(x, pltpu.HBM)


def _cast_into_slot(name, w, layer, chip):
    _, k, n4 = w.shape
    tr = max(t for t in range(16, 257, 16) if k % t == 0)

    def body(chip_ref, w_ref, o_ref):
        o_ref[...] = w_ref[...].astype(BF16)

    return pl.pallas_call(
        body, name=name,
        grid_spec=pltpu.PrefetchScalarGridSpec(
            num_scalar_prefetch=1, grid=(k // tr,),
            in_specs=[pl.BlockSpec((None, tr, n4), lambda i, s: (layer, i, 0))],
            out_specs=pl.BlockSpec((None, tr, n4), lambda i, s: (s[0], i, 0))),
        out_shape=jax.ShapeDtypeStruct((N_CHIPS, k, n4), BF16),
        compiler_params=_params(("parallel",)))(_scalars(chip), w)


def _gather_copy(buf_ref, k, from_chip, send_sem, recv_sem, chips, c):
    rows = buf_ref.at[from_chip]
    return pltpu.make_async_remote_copy(src_ref=rows, dst_ref=rows, send_sem=send_sem, recv_sem=recv_sem,
                                        device_id=(*chips[k], c), device_id_type=MESH)


def _gather_start(name, bufs, groups):
    n, ng = len(bufs), len(groups)
    where = {a: (gi, e) for gi, g in enumerate(groups) for e, a in enumerate(g)}

    def body(*refs):
        ins, sems, token = refs[:n], refs[n:n + 2 * ng], refs[-1]
        x, y, c, chips = _place()
        for a in range(n):
            gi, e = where[a]
            for k in range(3):
                _gather_copy(ins[a], k, 2 * x + y, sems[2 * gi].at[3 * e + k], sems[2 * gi + 1].at[3 * e + k],
                             chips, c).start()
        token[...] = jnp.zeros_like(token)

    out_shape = [pltpu.SemaphoreType.DMA((3 * len(g),)) for g in groups for _ in range(2)]
    out_shape += [pltpu.HBM(b.shape, b.dtype) for b in bufs] + [jax.ShapeDtypeStruct((8, 128), F32)]
    out = pl.pallas_call(
        body, name=name, in_specs=[HBM] * n,
        out_specs=[SEM] * (2 * ng) + [HBM] * n + [pl.BlockSpec(memory_space=pltpu.VMEM)], out_shape=out_shape,
        input_output_aliases={a: 2 * ng + a for a in range(n)}, compiler_params=SPLIT_COPY)(*[_in_hbm(b) for b in bufs])
    sems = [(out[2 * gi], out[2 * gi + 1]) for gi in range(ng)]
    return sems, list(out[2 * ng:2 * ng + n]), out[-1]


def _gather_wait(name, bufs, send, recv, after):
    n = len(bufs)

    def body(*refs):
        ins, send_sem, recv_sem = refs[:n], refs[n], refs[n + 1]
        x, y, c, chips = _place()
        for e in range(n):
            for k in range(3):
                sems = (send_sem.at[3 * e + k], recv_sem.at[3 * e + k])
                _gather_copy(ins[e], k, 2 * x + y, *sems, chips, c).wait_send()
                _gather_copy(ins[e], k, 2 * chips[k][0] + chips[k][1], *sems, chips, c).wait_recv()

    return pl.pallas_call(
        body, name=name, in_specs=[HBM] * n + [SEM, SEM, ANY], out_specs=[HBM] * n,
        out_shape=[pltpu.HBM(b.shape, b.dtype) for b in bufs],
        input_output_aliases={a: a for a in range(n)}, compiler_params=SPLIT_COPY)(*bufs, send, recv, after)


def _reduce_copy(g_ref, land_ref, mask, send_sem, recv_sem, x, y, c, sending):
    px, py, pc = x ^ ((mask >> 2) & 1), y ^ ((mask >> 1) & 1), c ^ (mask & 1)
    half = g_ref.shape[1] // 2
    src = g_ref.at[2 * px + py, pl.ds(pl.multiple_of(pc * half, half), half)]
    dst = land_ref.at[4 * x + 2 * y + c] if sending else land_ref.at[4 * px + 2 * py + pc]
    return pltpu.make_async_remote_copy(src_ref=src, dst_ref=dst, send_sem=send_sem, recv_sem=recv_sem,
                                        device_id=(px, py, pc), device_id_type=MESH)


def _reduce_start(name, grads):
    n = len(grads)
    lands = [lax.empty((N_DEV, g.shape[1] // 2, g.shape[2]), g.dtype) for g in grads]

    def body(*refs):
        gs, ls, send_sem, recv_sem = refs[:n], refs[n:2 * n], refs[2 * n], refs[2 * n + 1]
        x, y, c, _ = _place()
        for a in range(n):
            for mask in range(1, N_DEV):
                s = (N_DEV - 1) * a + mask - 1
                _reduce_copy(gs[a], ls[a], mask, send_sem.at[s], recv_sem.at[s], x, y, c, True).start()
        refs[-1][...] = jnp.zeros_like(refs[-1])

    sem = pltpu.SemaphoreType.DMA((n * (N_DEV - 1),))
    out = pl.pallas_call(
        body, name=name, in_specs=[HBM] * (2 * n),
        out_specs=[SEM, SEM] + [HBM] * (2 * n) + [pl.BlockSpec(memory_space=pltpu.VMEM)],
        out_shape=[sem, sem] + [pltpu.HBM(t.shape, t.dtype) for t in grads + lands] + [jax.ShapeDtypeStruct((8, 128), F32)],
        input_output_aliases={a: 2 + a for a in range(2 * n)}, compiler_params=SPLIT_COPY)(
            *[_in_hbm(t) for t in grads + lands])
    return out[0], out[1], list(out[2:2 + n]), list(out[2 + n:2 + 2 * n]), out[-1]


def _reduce_wait(name, send, recv, grads, lands, after):
    n = len(grads)

    def body(*refs):
        gs, ls, send_sem, recv_sem = refs[:n], refs[n:2 * n], refs[2 * n], refs[2 * n + 1]
        x, y, c, _ = _place()
        for a in range(n):
            for mask in range(1, N_DEV):
                s = (N_DEV - 1) * a + mask - 1
                sems = (send_sem.at[s], recv_sem.at[s])
                _reduce_copy(gs[a], ls[a], mask, *sems, x, y, c, True).wait_send()
                _reduce_copy(gs[a], ls[a], mask, *sems, x, y, c, False).wait_recv()

    out = pl.pallas_call(
        body, name=name, in_specs=[HBM] * (2 * n) + [SEM, SEM, ANY], out_specs=[HBM] * (2 * n),
        out_shape=[pltpu.HBM(t.shape, t.dtype) for t in grads + lands],
        input_output_aliases={a: a for a in range(2 * n)}, compiler_params=SPLIT_COPY)(*grads, *lands, send, recv, after)
    return list(out[:n]), list(out[n:])


def _reduce_sum(name, g, land, layer, into, chip, c):
    _, k4, n4 = g.shape
    half = k4 // 2
    tr = max(t for t in range(16, 513, 16) if half % t == 0)
    per = half // tr
    me = 2 * chip + c

    def body(s_ref, own_ref, *refs):
        total = own_ref[...].astype(F32)
        for ref in refs[:N_DEV - 1]:
            total = total + ref[...].astype(F32)
        refs[-1][...] = total

    in_specs = [pl.BlockSpec((None, tr, n4), lambda i, s: (s[0], s[1] * per + i, 0))]
    in_specs += [pl.BlockSpec((None, tr, n4), functools.partial(lambda i, s, m: (s[1 + m], i, 0), m=m))
                 for m in range(1, N_DEV)]
    ins = [g] + [land] * (N_DEV - 1)
    aliases = {}
    if into is not None:
        in_specs, ins, aliases = in_specs + [ANY], ins + [into], {1 + N_DEV: 0}
    return pl.pallas_call(
        body, name=name,
        grid_spec=pltpu.PrefetchScalarGridSpec(
            num_scalar_prefetch=1, grid=(per,), in_specs=in_specs,
            out_specs=pl.BlockSpec((None, tr, n4), lambda i, s: (layer, s[1] * per + i, 0))),
        out_shape=jax.ShapeDtypeStruct((DEPTH, k4, n4), F32), input_output_aliases=aliases,
        compiler_params=_params(("parallel",)))(_scalars(chip, c, *[me ^ m for m in range(1, N_DEV)]), *ins)


def _join_halves(name, bufs):
    n = len(bufs)

    def body(*refs):
        ins, outs = refs[:n], refs[n:2 * n]
        send_sem, recv_sem = refs[2 * n:]
        x, y, c, _ = _place()

        def rows(ref, which):
            half = ref.shape[1] // 2
            return ref.at[:, pl.ds(pl.multiple_of(which * half, half), half)]

        sends = [pltpu.make_async_remote_copy(
            src_ref=rows(ins[a], c), dst_ref=rows(outs[a], c), send_sem=send_sem.at[a], recv_sem=recv_sem.at[a],
            device_id=(x, y, 1 - c), device_id_type=MESH) for a in range(n)]
        for cp in sends:
            cp.start()
        for a in range(n):
            sends[a].wait_send()
            pltpu.make_async_remote_copy(
                src_ref=rows(ins[a], c), dst_ref=rows(outs[a], 1 - c), send_sem=send_sem.at[a], recv_sem=recv_sem.at[a],
                device_id=(x, y, 1 - c), device_id_type=MESH).wait_recv()

    return pl.pallas_call(
        body, name=name, in_specs=[ANY] * n, out_specs=[ANY] * n,
        out_shape=[jax.ShapeDtypeStruct(b.shape, b.dtype) for b in bufs],
        input_output_aliases={a: a for a in range(n)},
        scratch_shapes=[pltpu.SemaphoreType.DMA((n,)), pltpu.SemaphoreType.DMA((n,))],
    )(*bufs)


def _all_reduce_small(block):
    r = block.shape[0]

    def body(x_ref, out_ref, slots, send_sem, recv_sem):
        x, y, c, _ = _place()
        me = 4 * x + 2 * y + c
        slots[me] = x_ref[...]
        sends = []
        for mask in range(1, N_DEV):
            fx, fy, fc = (mask >> 2) & 1, (mask >> 1) & 1, mask & 1
            peer = (x ^ fx, y ^ fy, c ^ fc)
            cp = pltpu.make_async_remote_copy(
                src_ref=x_ref, dst_ref=slots.at[me], send_sem=send_sem.at[mask - 1], recv_sem=recv_sem.at[mask - 1],
                device_id=peer, device_id_type=MESH)
            cp.start()
            sends.append(cp)
        for mask in range(1, N_DEV):
            src = me ^ mask
            pltpu.make_async_remote_copy(
                src_ref=x_ref, dst_ref=slots.at[src], send_sem=send_sem.at[mask - 1], recv_sem=recv_sem.at[mask - 1],
                device_id=(x, y, c), device_id_type=MESH).wait_recv()
        for cp in sends:
            cp.wait_send()
        total = slots[0]
        for d in range(1, N_DEV):
            total = total + slots[d]
        out_ref[...] = total

    vmem = pl.BlockSpec(memory_space=pltpu.VMEM)
    return pl.pallas_call(
        body, name="all_reduce_small", in_specs=[vmem], out_specs=vmem,
        out_shape=jax.ShapeDtypeStruct((r, 128), F32),
        scratch_shapes=[pltpu.VMEM((N_DEV, r, 128), F32), pltpu.SemaphoreType.DMA((N_DEV - 1,)),
                        pltpu.SemaphoreType.DMA((N_DEV - 1,))],
        compiler_params=pltpu.CompilerParams(vmem_limit_bytes=VMEM_LIMIT))(block)


B_Q_COL = 2304 // 128
B_K0, B_V0, B_END = 2816, 2944, 3072


def _full_cols(w_g):
    return w_g.transpose(1, 0, 2).reshape(w_g.shape[1], -1)


def _group_src(proj, g):
    if A_GROUPS[g][1] == 1:
        return ((proj, 2 * g), (proj, 6 + 2 * g), (proj, 12 + 2 * g))
    packed = jnp.concatenate([proj[:, t * 768 + g * 256:t * 768 + (g + 1) * 256] for t in range(3)], axis=1)
    return ((packed, 0), (packed, 2), (packed, 4))


def _kv_expand(kv):
    return jnp.broadcast_to(kv.reshape(S, 2, 1, HD), (S, 2, 4, HD)).reshape(S, 8 * HD)


def _kv_reduce(dkv):
    return dkv.reshape(S, 2, 4, HD).sum(axis=2).reshape(S, 2 * HD)


def _mixer_fwd(h1, wget, rel_bias, sinks_l, bidx):
    w = dict(wget(0, h1))
    proj = _mm_nt("proj_in", h1, w["w_in"], F32, tn=1152)
    no_sinks = jnp.full((4,), NEG, F32)
    srcs = [_group_src(proj, g) for g in range(3)]
    o_g, lse_g = [], []
    for g, (_, d) in enumerate(A_GROUPS):
        o, lse = _band_fwd("band_fwd_g%d" % g, d, 2, BLK, 4 * g, srcs[g], bidx[g], rel_bias, no_sinks)
        o_g.append(o)
        lse_g.append(lse)
    o_a32, o_a, lse_a = _comb_fwd(o_g, lse_g)
    src_b = ((proj, B_Q_COL), (_kv_expand(proj[:, B_K0:B_V0]), 0), (_kv_expand(proj[:, B_V0:B_END]), 0))
    o_b32, lse_b = _band_fwd("band_fwd_b", 1, 4, BLK - 1, N_A, src_b, bidx[3], rel_bias, sinks_l)
    o_b = o_b32.astype(BF16)
    o_c32, o_c, tot_c = _sb_fwd(proj)
    w.update(wget(1, o_c32))
    br = [_mm_nn("branch_a", o_a, w["w_br_a"], F32), _mm_nn("branch_b", o_b, w["w_br_b"], F32),
          _mm_nn("branch_c", o_c, w["w_br_c"], F32)]
    merged = _gate_fwd(proj, w["b_gate"], br)
    mo = _mm_nn("out_proj", merged, w["w_out"], F32)
    saved = dict(proj=proj, srcs=srcs, src_b=src_b, o_a32=o_a32, lse_a=lse_a, o_b32=o_b32, lse_b=lse_b, tot_c=tot_c,
                 o_a=o_a, o_b=o_b, o_c=o_c, br=br, merged=merged)
    return mo, saved, w


def _mixer_bwd(d_mo, h1, w, sv, rel_bias, sinks_l, bidx, stats_in, emit):
    grads = {}
    dmerged = _mm_nt("out_proj_dx", d_mo, w["w_out"], F32)
    grads["w_out"] = _mm_tn_sharded("out_proj_dw", sv["merged"], d_mo, True)
    e, dgate, db_gate = _gate_bwd(sv["proj"], w["b_gate"], sv["br"], dmerged)
    grads["b_gate"] = db_gate
    d_o = {}
    for n, name in enumerate("abc"):
        d_o[name] = _mm_nt("branch_%s_dx" % name, e[n], w["w_br_" + name], F32)
        grads["w_br_" + name] = _mm_tn_sharded("branch_%s_dw" % name, sv["o_" + name], e[n], False)
    zero = emit(1, grads)
    no_sinks = jnp.full((4,), NEG, F32) + zero[0]
    dqs, dks, dvs, stats = [], [], [], []
    for g, (_, d) in enumerate(A_GROUPS):
        dq, dk, dv, st = _band_bwd("band_bwd_g%d" % g, d, 2, BLK, 4 * g, sv["srcs"][g], bidx[g], rel_bias, no_sinks,
                                   sv["o_a32"], sv["lse_a"], d_o["a"], stats_in[4 * g:4 * g + 4])
        dqs.append(dq)
        dks.append(dk)
        dvs.append(dv)
        stats.append(st)
    dq_b, dk_x, dv_x, st = _band_bwd("band_bwd_b", 1, 4, BLK - 1, N_A, sv["src_b"], bidx[3], rel_bias, sinks_l,
                                     sv["o_b32"], sv["lse_b"], d_o["b"], stats_in[N_A:])
    stats = jnp.concatenate(stats + [st], axis=0)
    dcq, dck, dcv = _sb_bwd(sv["proj"], sv["tot_c"], d_o["c"])
    cols = dqs + dks + dvs + [dq_b, _kv_reduce(dk_x), _kv_reduce(dv_x), dcq, dck, dcv]
    dproj = jnp.concatenate([t.astype(BF16) for t in cols] + list(dgate), axis=1)
    grads["w_in"] = _mm_tn("proj_in_dw", dproj, h1, BF16, tm=768).reshape(N_CHIPS, IN_SHARD, D)
    zero = emit(2, grads)
    dh1 = _mm_nn("proj_in_dx", dproj, w["w_in"], F32)
    return dh1, grads, stats, zero


def _ffn_fwd(h2, w):
    u = _mm_nn("ffn_up", h2, w["w_up"], F32, tn=1024)
    a = _conv_fwd(u, w["conv_w"], w["conv_b"])
    dn = _mm_nn("ffn_down", a, w["w_down"], F32)
    return dn, dict(u=u, a=a)


def _ffn_bwd(d_dn, h2, w, sv):
    grads = {}
    da = _mm_nt("ffn_down_dx", d_dn, w["w_down"], F32, tn=1024)
    grads["w_down"] = _mm_tn_sharded("ffn_down_dw", sv["a"], d_dn, True)
    dug, duv, dwg, dwv, dbg, dbv = _conv_bwd(sv["u"], w["conv_w"], w["conv_b"], da)
    du = jnp.concatenate([dug, duv], axis=1)
    grads["conv_w"] = jnp.concatenate([dwg, dwv], axis=1)
    grads["conv_b"] = jnp.concatenate([dbg, dbv], axis=1)
    dh2 = _mm_nt("ffn_up_dx", du, w["w_up"], F32)
    grads["w_up"] = _mm_tn_sharded("ffn_up_dw", h2, du, False, tn=1024)
    return dh2, grads


BIG = ("w_in", "w_br_a", "w_br_b", "w_br_c", "w_out", "w_up", "w_down")


def _shard_view(name, w):
    return jnp.swapaxes(w, 1, 2) if name == "w_in" else w
WEIGHT_GROUPS = (("w_in", "b_gate"), ("w_br_a", "w_br_b", "w_br_c", "w_out"), ("w_up", "conv_w", "w_down"))
GRAD_GROUPS = (("w_down", "w_up"), ("w_out", "w_br_a", "w_br_b", "w_br_c"), ("w_in",))
SMALL_ROWS = (("rel_bias", 8), ("attn_pre_norm", 16), ("attn_post_norm", 16), ("ffn_pre_norm", 16), ("ffn_post_norm", 16),
              ("sinks", 8), ("conv_b", 128), ("b_gate", 48), ("conv_w", 384), ("loss", 8))


def _pack_small(vals):
    rows = []
    for name, n in SMALL_ROWS:
        flat = vals[name].reshape(-1).astype(F32)
        rows.append(jnp.pad(flat, (0, n * 128 - flat.shape[0])).reshape(n, 128))
    return jnp.concatenate(rows, axis=0)


def _unpack_small(block, shapes):
    out, row = {}, 0
    for name, n in SMALL_ROWS:
        size = int(np.prod(shapes[name]))
        out[name] = block[row:row + n].reshape(-1)[:size].reshape(shapes[name])
        row += n
    return out


def kernel(x, rel_bias, attn_pre_norm, w_in, b_gate, sinks, w_br_a, w_br_b, w_br_c, w_out, attn_post_norm, ffn_pre_norm, w_up, conv_w, conv_b, w_down, ffn_post_norm, loss_target, m_rel_bias, m_attn_pre_norm, m_w_in, m_b_gate, m_sinks, m_w_br_a, m_w_br_b, m_w_br_c, m_w_out, m_attn_post_norm, m_ffn_pre_norm, m_w_up, m_conv_w, m_conv_b, m_w_down, m_ffn_post_norm, v_rel_bias, v_attn_pre_norm, v_w_in, v_b_gate, v_sinks, v_w_br_a, v_w_br_b, v_w_br_c, v_w_out, v_attn_post_norm, v_ffn_pre_norm, v_w_up, v_conv_w, v_conv_b, v_w_down, v_ffn_post_norm):
    names = ("rel_bias", "attn_pre_norm", "w_in", "b_gate", "sinks", "w_br_a", "w_br_b", "w_br_c", "w_out",
             "attn_post_norm", "ffn_pre_norm", "w_up", "conv_w", "conv_b", "w_down", "ffn_post_norm")
    weights = dict(zip(names, (rel_bias, attn_pre_norm, w_in, b_gate, sinks, w_br_a, w_br_b, w_br_c, w_out,
                               attn_post_norm, ffn_pre_norm, w_up, conv_w, conv_b, w_down, ffn_post_norm)))
    mom1 = dict(zip(names, (m_rel_bias, m_attn_pre_norm, m_w_in, m_b_gate, m_sinks, m_w_br_a, m_w_br_b, m_w_br_c,
                            m_w_out, m_attn_post_norm, m_ffn_pre_norm, m_w_up, m_conv_w, m_conv_b, m_w_down,
                            m_ffn_post_norm)))
    mom2 = dict(zip(names, (v_rel_bias, v_attn_pre_norm, v_w_in, v_b_gate, v_sinks, v_w_br_a, v_w_br_b, v_w_br_c,
                            v_w_out, v_attn_post_norm, v_ffn_pre_norm, v_w_up, v_conv_w, v_conv_b, v_w_down,
                            v_ffn_post_norm)))

    chip = 2 * lax.axis_index("x") + lax.axis_index("y")
    core = lax.axis_index("c")

    keys = [(n, l) for l in range(DEPTH) for group in WEIGHT_GROUPS for n in group]
    groups = [[keys.index((n, l)) for n in group] for l in range(DEPTH) for group in WEIGHT_GROUPS]

    def slot_buffer(n, l):
        if n in BIG:
            return _cast_into_slot("cast_" + n, _shard_view(n, weights[n]), l, chip)
        shard = weights[n][l]
        return lax.dynamic_update_slice(jnp.zeros((N_CHIPS,) + shard.shape, F32), shard[None],
                                        (chip, jnp.int32(0), jnp.int32(0)))

    sems, in_flight, _ = _gather_start("gather_start", [slot_buffer(*k) for k in keys], groups)

    def wget(l, gi, after):
        g = l * len(WEIGHT_GROUPS) + gi
        got = _gather_wait("gather_wait_%d_%d" % (l, gi), [in_flight[a] for a in groups[g]], *sems[g], after)
        out = {}
        for n, buf in zip(WEIGHT_GROUPS[gi], got):
            out[n] = buf.reshape(-1, buf.shape[-1]) if n in ("w_in", "w_out", "w_down") else _full_cols(buf)
        if gi == len(WEIGHT_GROUPS) - 1:
            out["conv_b"] = conv_b[l:l + 1]
        return out

    pending = []

    def emit(l, gi, grads):
        group = GRAD_GROUPS[gi]
        *started, token = _reduce_start("reduce_start_%d_%d" % (l, gi), [grads[n] for n in group])
        pending.append((l, group) + tuple(started))
        return token[:1, :1]

    local = _local_step(x.reshape(S, D), loss_target.reshape(S, D), wget, emit, rel_bias, sinks, attn_pre_norm,
                        attn_post_norm, ffn_pre_norm, ffn_post_norm)
    return _reduce_and_update(x.shape, names, weights, mom1, mom2, chip, core, pending, *local)


def _local_step(xs, target, wget, emit, rel_bias, sinks, attn_pre_norm, attn_post_norm, ffn_pre_norm, ffn_post_norm):
    bidx = jnp.asarray(_bucket_maps())

    saved, layers = [], []
    h1 = _rms_fwd("pre_norm_first", xs, attn_pre_norm[0:1])
    x_in = xs
    for l in range(DEPTH):
        mo, sv_mix, w = _mixer_fwd(h1, functools.partial(wget, l), rel_bias, sinks[l], bidx)
        x_mid, h2 = _post_pre_fwd("post_attn_norm", x_in, mo, attn_post_norm[l:l + 1], ffn_pre_norm[l:l + 1])
        w.update(wget(l, 2, h2))
        dn, sv_ffn = _ffn_fwd(h2, w)
        g_next = attn_pre_norm[l + 1:l + 2] if l + 1 < DEPTH else None
        x_out, h1_next = _post_pre_fwd("post_ffn_norm" if l + 1 < DEPTH else "post_ffn_norm_last", x_mid, dn,
                                       ffn_post_norm[l:l + 1], g_next)
        saved.append(dict(x_in=x_in, h1=h1, mo=mo, x_mid=x_mid, h2=h2, dn=dn, mix=sv_mix, ffn=sv_ffn))
        layers.append(w)
        x_in, h1 = x_out, h1_next

    loss_row, dres = _loss_kernel(x_in, target)

    small = [None] * DEPTH
    stats = jnp.zeros((N_BAND_Q, 8, 128), F32)
    dh_next = None
    for l in reversed(range(DEPTH)):
        w, sv = layers[l], saved[l]
        if l + 1 < DEPTH:
            pre = (saved[l + 1]["x_in"], attn_pre_norm[l + 1:l + 2] + zero, dh_next)
            dres, d_dn, dg_pre_next, dg_fpost = _norm_bwd("post_ffn_norm_bwd", dres, pre,
                                                          (sv["dn"], ffn_post_norm[l:l + 1]))
            small[l + 1]["attn_pre_norm"] = dg_pre_next
        else:
            dres, d_dn, _, dg_fpost = _norm_bwd("post_ffn_norm_last_bwd", dres, None, (sv["dn"], ffn_post_norm[l:l + 1]))
        dh2, g_ffn = _ffn_bwd(d_dn, sv["h2"], w, sv["ffn"])
        zero = emit(l, 0, g_ffn)
        dres, d_mo, dg_fpre, dg_apost = _norm_bwd("post_attn_norm_bwd", dres,
                                                  (sv["x_mid"], ffn_pre_norm[l:l + 1] + zero, dh2),
                                                  (sv["mo"], attn_post_norm[l:l + 1]))
        dh_next, g_mix, stats, zero = _mixer_bwd(d_mo, sv["h1"], w, sv["mix"], rel_bias, sinks[l], bidx, stats,
                                                 functools.partial(emit, l))
        small[l] = dict(ffn_post_norm=dg_fpost, ffn_pre_norm=dg_fpre, attn_post_norm=dg_apost,
                        sinks=stats[N_A:, 1, 0], conv_b=g_ffn["conv_b"], b_gate=g_mix["b_gate"], conv_w=g_ffn["conv_w"])
    grad_x, _, dg_pre0, _ = _norm_bwd("pre_norm_first_bwd", dres, (saved[0]["x_in"], attn_pre_norm[0:1] + zero, dh_next),
                                      None)
    small[0]["attn_pre_norm"] = dg_pre0
    return loss_row, grad_x, small, stats


def _reduce_and_update(x_shape, names, weights, mom1, mom2, chip, core, pending, loss_row, grad_x, small, stats):
    delta, new_m, new_v, grads = {}, {}, {}, {}

    def update(n, g):
        grads[n] = g
        delta[n], new_m[n], new_v[n] = _adamw("adamw_" + n, _shard_view(n, weights[n]), g,
                                              _shard_view(n, mom1[n]), _shard_view(n, mom2[n]))

    summed = {}

    def finish(which, after):
        for l, group, send, recv, gs, lands in pending:
            if (group == ("w_in",)) == which:
                gs, lands = _reduce_wait("reduce_wait_%d_%s" % (l, group[0]), send, recv, gs, lands, after)
                for n, g, land in zip(group, gs, lands):
                    summed[n] = _reduce_sum("reduce_sum_%d_%s" % (l, n), g, land, l, summed.get(n), chip, core)

    finish(False, grad_x)
    early = [n for n in BIG if n != "w_in"]
    for n, g in zip(early, _join_halves("join_halves", [summed[n] for n in early])):
        update(n, g)
    finish(True, delta[early[-1]])
    update("w_in", _join_halves("join_halves_w_in", [summed["w_in"]])[0])
    for out in (grads, delta, new_m, new_v):
        out["w_in"] = _shard_view("w_in", out["w_in"])

    small_vals = {n: jnp.stack([small[l][n].reshape(weights[n].shape[1:]) for l in range(DEPTH)])
                  for n in ("attn_pre_norm", "attn_post_norm", "ffn_pre_norm", "ffn_post_norm", "conv_b", "sinks")}
    small_vals["b_gate"] = jnp.stack([small[l]["b_gate"] for l in range(DEPTH)])
    small_vals["conv_w"] = jnp.stack([small[l]["conv_w"] for l in range(DEPTH)])
    small_vals["rel_bias"] = stats[:, 0, :NUM_BUCKETS].T
    small_vals["loss"] = loss_row[0, :1]
    shapes = {n: v.shape for n, v in small_vals.items()}
    packed, delta["w_in"] = lax.optimization_barrier((_pack_small(small_vals), delta["w_in"]))
    reduced = _unpack_small(_all_reduce_small(packed), shapes)
    reduced["b_gate"] = lax.dynamic_slice_in_dim(reduced["b_gate"], chip * (D // N_CHIPS), D // N_CHIPS, axis=2)
    reduced["conv_w"] = lax.dynamic_slice_in_dim(reduced["conv_w"], chip * (2 * D_FF // N_CHIPS), 2 * D_FF // N_CHIPS, axis=2)
    for n in names:
        if n not in grads:
            update(n, reduced[n].reshape(weights[n].shape))

    loss = reduced["loss"].reshape(())
    return (loss, grad_x.reshape(x_shape), *[grads[n] for n in names], *[delta[n] for n in names],
            *[new_m[n] for n in names], *[new_v[n] for n in names])
```

```python
import functools
import math

import numpy as np
import jax
import jax.numpy as jnp
from jax import lax
from jax.experimental import pallas as pl
from jax.experimental.pallas import tpu as pltpu

F32 = jnp.float32
BF16 = jnp.bfloat16

S = 2048
D = 1024
DEPTH = 2
HD = 64
BLK = 128
NQB = S // BLK
A_GROUPS = ((128, 1), (512, 4), (2048, 16))
N_BAND_Q = 20
N_A = 12
NUM_BUCKETS = 32
MAX_DISTANCE = 2048
D_FF = 4096
IN_COLS = 6912
IN_SHARD = IN_COLS // 4
OFF_GATE = 3840
EPS = 1e-6
SCALE = HD ** -0.5
NEG = -1e30
N_CHIPS = 4
N_DEV = 8

ADAM_LR = 0.001
ADAM_B1 = 0.9
ADAM_B2 = 0.999
ADAM_EPS = 1e-08
ADAM_WD = 0.01
ADAM_STEP = 10

VMEM_LIMIT = 56 * 1024 * 1024

NN = (((1,), (0,)), ((), ()))
NT = (((1,), (1,)), ((), ()))
TN = (((0,), (0,)), ((), ()))

MESH = pl.DeviceIdType.MESH
ANY = pl.BlockSpec(memory_space=pl.ANY)


def _dot(a, b, dims):
    return lax.dot_general(a, b, dims, preferred_element_type=F32)


def _params(sem):
    return pltpu.CompilerParams(dimension_semantics=sem, vmem_limit_bytes=VMEM_LIMIT)


def _matmul(name, a, b, out_shape, out_dtype, grid, a_spec, b_spec, o_spec, dims, acc_shape):
    nk = grid[-1]

    def body(a_ref, b_ref, o_ref, *scratch):
        part = _dot(a_ref[...].astype(BF16), b_ref[...].astype(BF16), dims)
        if nk == 1:
            o_ref[...] = part.astype(o_ref.dtype)
            return
        acc_ref, = scratch
        k = pl.program_id(len(grid) - 1)

        @pl.when(k == 0)
        def _():
            acc_ref[...] = part

        @pl.when(k > 0)
        def _():
            acc_ref[...] += part

        @pl.when(k == nk - 1)
        def _():
            o_ref[...] = acc_ref[...].astype(o_ref.dtype)

    scratch = [] if nk == 1 else [pltpu.VMEM(acc_shape, F32)]
    sem = ("parallel",) * (len(grid) - 1) + ("arbitrary",)
    return pl.pallas_call(
        body, name=name, grid=grid, in_specs=[a_spec, b_spec], out_specs=o_spec,
        out_shape=jax.ShapeDtypeStruct(out_shape, out_dtype), scratch_shapes=scratch,
        compiler_params=_params(sem))(a, b)


FULL_K = 8192


def _mm_tn_sharded(name, a, b, row_sharded, tm=512, tn=512, tk=FULL_K):
    k, m = a.shape
    n = b.shape[1]
    m4, n4 = (m // N_CHIPS, n) if row_sharded else (m, n // N_CHIPS)
    tm, tn, tk = min(tm, m4), min(tn, n4), min(tk, k)
    per_m, per_n = m4 // tm, n4 // tn
    if row_sharded:
        o_map = lambda i, j, l: (i // per_m, i % per_m, j)
    else:
        o_map = lambda i, j, l: (j // per_n, i, j % per_n)
    return _matmul(name, a, b, (N_CHIPS, m4, n4), BF16, (m // tm, n // tn, k // tk),
                   pl.BlockSpec((tk, tm), lambda i, j, l: (l, i)),
                   pl.BlockSpec((tk, tn), lambda i, j, l: (l, j)),
                   pl.BlockSpec((None, tm, tn), o_map), TN, (tm, tn))


def _mm_nn(name, a, b, out_dtype, tm=512, tn=512, tk=FULL_K):
    m, k = a.shape
    n = b.shape[1]
    tm, tn, tk = min(tm, m), min(tn, n), min(tk, k)
    return _matmul(name, a, b, (m, n), out_dtype, (m // tm, n // tn, k // tk),
                   pl.BlockSpec((tm, tk), lambda i, j, l: (i, l)),
                   pl.BlockSpec((tk, tn), lambda i, j, l: (l, j)),
                   pl.BlockSpec((tm, tn), lambda i, j, l: (i, j)), NN, (tm, tn))


def _mm_nt(name, a, b, out_dtype, tm=512, tn=512, tk=FULL_K):
    m, k = a.shape
    n = b.shape[0]
    tm, tn, tk = min(tm, m), min(tn, n), min(tk, k)
    return _matmul(name, a, b, (m, n), out_dtype, (m // tm, n // tn, k // tk),
                   pl.BlockSpec((tm, tk), lambda i, j, l: (i, l)),
                   pl.BlockSpec((tn, tk), lambda i, j, l: (j, l)),
                   pl.BlockSpec((tm, tn), lambda i, j, l: (i, j)), NT, (tm, tn))


def _mm_tn(name, a, b, out_dtype, tm=512, tn=512, tk=FULL_K):
    k, m = a.shape
    n = b.shape[1]
    tm, tn, tk = min(tm, m), min(tn, n), min(tk, k)
    return _matmul(name, a, b, (m, n), out_dtype, (m // tm, n // tn, k // tk),
                   pl.BlockSpec((tk, tm), lambda i, j, l: (l, i)),
                   pl.BlockSpec((tk, tn), lambda i, j, l: (l, j)),
                   pl.BlockSpec((tm, tn), lambda i, j, l: (i, j)), TN, (tm, tn))


TR = 256


def _row_spec(width=D):
    return pl.BlockSpec((TR, width), lambda i: (i, 0))


def _vec_spec(width=D):
    return pl.BlockSpec((1, width), lambda i: (0, 0))


def _rms(x, g):
    r = lax.rsqrt(jnp.mean(x * x, axis=-1, keepdims=True) + EPS)
    return x * r * g


def _rms_fwd(name, x, g):
    def body(x_ref, g_ref, h_ref):
        h_ref[...] = _rms(x_ref[...], g_ref[...]).astype(BF16)

    return pl.pallas_call(
        body, name=name, grid=(S // TR,), in_specs=[_row_spec(), _vec_spec()], out_specs=_row_spec(),
        out_shape=jax.ShapeDtypeStruct((S, D), BF16), compiler_params=_params(("parallel",)))(x, g)


def _post_pre_fwd(name, x, y, g_post, g_pre):
    has_pre = g_pre is not None

    def body(*refs):
        if has_pre:
            x_ref, y_ref, gp_ref, gn_ref, xn_ref, h_ref = refs
        else:
            x_ref, y_ref, gp_ref, xn_ref = refs
        xn = x_ref[...] + _rms(y_ref[...], gp_ref[...])
        xn_ref[...] = xn
        if has_pre:
            h_ref[...] = _rms(xn, gn_ref[...]).astype(BF16)

    ins = [x, y, g_post] + ([g_pre] if has_pre else [])
    in_specs = [_row_spec(), _row_spec(), _vec_spec()] + ([_vec_spec()] if has_pre else [])
    out_shape = [jax.ShapeDtypeStruct((S, D), F32)] + ([jax.ShapeDtypeStruct((S, D), BF16)] if has_pre else [])
    out_specs = [_row_spec()] + ([_row_spec()] if has_pre else [])
    out = pl.pallas_call(
        body, name=name, grid=(S // TR,), in_specs=in_specs, out_specs=out_specs, out_shape=out_shape,
        compiler_params=_params(("parallel",)))(*ins)
    return out if has_pre else (out[0], None)


def _rms_bwd_math(x, g, dy):
    r = lax.rsqrt(jnp.mean(x * x, axis=-1, keepdims=True) + EPS)
    n = x * r
    dn = dy * g
    dx = r * (dn - n * jnp.mean(dn * n, axis=-1, keepdims=True))
    return dx, jnp.sum(dy * n, axis=0, keepdims=True)


def _norm_bwd(name, dres, pre=None, post=None):
    has_pre, has_post = pre is not None, post is not None

    def body(*refs):
        refs = list(refs)
        dres_ref = refs.pop(0)
        if has_pre:
            xn_ref, gn_ref, dh_ref = refs[:3]
            refs = refs[3:]
        if has_post:
            y_ref, gp_ref = refs[:2]
            refs = refs[2:]
        dxn_ref = refs.pop(0)
        dy_ref = refs.pop(0) if has_post else None
        dgn_ref = refs.pop(0) if has_pre else None
        dgp_ref = refs.pop(0) if has_post else None
        first = pl.program_id(0) == 0
        dxn = dres_ref[...]
        if has_pre:
            dx, dg = _rms_bwd_math(xn_ref[...], gn_ref[...], dh_ref[...])
            dxn = dxn + dx

            @pl.when(first)
            def _():
                dgn_ref[...] = dg

            @pl.when(jnp.logical_not(first))
            def _():
                dgn_ref[...] += dg
        dxn_ref[...] = dxn
        if has_post:
            dy, dg = _rms_bwd_math(y_ref[...], gp_ref[...], dxn)
            dy_ref[...] = dy.astype(BF16)

            @pl.when(first)
            def _():
                dgp_ref[...] = dg

            @pl.when(jnp.logical_not(first))
            def _():
                dgp_ref[...] += dg

    ins, in_specs = [dres], [_row_spec()]
    if has_pre:
        ins += list(pre)
        in_specs += [_row_spec(), _vec_spec(), _row_spec()]
    if has_post:
        ins += list(post)
        in_specs += [_row_spec(), _vec_spec()]
    out_shape, out_specs = [jax.ShapeDtypeStruct((S, D), F32)], [_row_spec()]
    if has_post:
        out_shape.append(jax.ShapeDtypeStruct((S, D), BF16))
        out_specs.append(_row_spec())
    for _ in range(int(has_pre) + int(has_post)):
        out_shape.append(jax.ShapeDtypeStruct((1, D), F32))
        out_specs.append(_vec_spec())
    out = list(pl.pallas_call(
        body, name=name, grid=(S // TR,), in_specs=in_specs, out_specs=out_specs, out_shape=out_shape,
        compiler_params=_params(("arbitrary",)))(*ins))
    dxn = out.pop(0)
    dy = out.pop(0) if has_post else None
    dgn = out.pop(0) if has_pre else None
    dgp = out.pop(0) if has_post else None
    return dxn, dy, dgn, dgp


def _loss_kernel(y, target):
    def body(y_ref, t_ref, loss_ref, dy_ref):
        e = y_ref[...] - t_ref[...]
        dy_ref[...] = e * (1.0 / D)
        part = jnp.zeros((1, 128), F32) + 0.5 * jnp.sum(jnp.mean(e * e, axis=-1, keepdims=True))

        @pl.when(pl.program_id(0) == 0)
        def _():
            loss_ref[...] = part

        @pl.when(pl.program_id(0) > 0)
        def _():
            loss_ref[...] += part

    return pl.pallas_call(
        body, name="loss", grid=(S // TR,), in_specs=[_row_spec(), _row_spec()],
        out_specs=[_vec_spec(128), _row_spec()],
        out_shape=[jax.ShapeDtypeStruct((1, 128), F32), jax.ShapeDtypeStruct((S, D), F32)],
        compiler_params=_params(("arbitrary",)))(y, target)


def _t5_bucket_np(dist):
    max_exact = NUM_BUCKETS // 2
    nf = np.maximum(dist, 1).astype(np.float32)
    large = max_exact + (np.log(nf / max_exact) / np.float32(math.log(MAX_DISTANCE / max_exact))
                         * (NUM_BUCKETS - max_exact)).astype(np.int32)
    large = np.minimum(large, NUM_BUCKETS - 1)
    return np.where(dist < max_exact, dist, large).astype(np.int32)


def _bucket_maps():
    a = np.arange(BLK)[:, None]
    b = np.arange(2 * BLK)[None, :]
    dist = np.maximum(a + BLK - b, 0)
    maps = [_t5_bucket_np(dist * d) for _, d in A_GROUPS] + [_t5_bucket_np(dist)]
    return np.stack(maps).astype(np.int32)


def _classes(arr, d):
    return arr.reshape(S // d, d * arr.shape[1])


def _class_spec(arr, col0, d):
    ncol = arr.shape[1] // 128
    return pl.BlockSpec((S // d, 128), lambda p, r: (0, r * ncol + col0 + p))


def _band_rows(b):
    return (pl.ds(pl.multiple_of(b * BLK, BLK), BLK), pl.ds(pl.multiple_of(jnp.maximum(b - 1, 0) * BLK, BLK), BLK))


def _band_bias(tab_ref, bidx_ref, h):
    bi = bidx_ref[...]
    bias = jnp.zeros((BLK, 2 * BLK), F32)
    for kk in range(NUM_BUCKETS):
        bias = jnp.where(bi == kk, tab_ref[kk, h], bias)
    return bias


def _lane_lo(rows=BLK):
    return lax.broadcasted_iota(jnp.int32, (rows, 128), 1) < HD


def _per_head(x, lo):
    return (jnp.sum(jnp.where(lo, x, 0.0), axis=1, keepdims=True) * (1.0 / HD),
            jnp.sum(jnp.where(lo, 0.0, x), axis=1, keepdims=True) * (1.0 / HD))


def _band_fill(bias_ref, tab_ref, bidx_ref, head, maxd):
    a = lax.broadcasted_iota(jnp.int32, (BLK, 2 * BLK), 0)
    c = lax.broadcasted_iota(jnp.int32, (BLK, 2 * BLK), 1)
    dist = a + BLK - c
    in_band = jnp.logical_and(dist >= 0, dist <= maxd)
    for h in range(2):
        bias = jnp.where(in_band, _band_bias(tab_ref, bidx_ref, head + h), NEG)
        bias_ref[1, h * BLK:(h + 1) * BLK, :] = bias
        bias_ref[0, h * BLK:(h + 1) * BLK, :] = jnp.where(c >= BLK, bias, NEG)


def _stack_heads(x, lo, dtype=BF16):
    return jnp.concatenate([jnp.where(lo, x, 0.0), jnp.where(lo, 0.0, x)], axis=0).astype(dtype)


def _unstack_heads(x, lo):
    n = x.shape[0] // 2
    return jnp.where(lo, x[:n], x[n:])


def _stack_rows(ref, prev, cur):
    return jnp.concatenate([ref[prev, :], ref[cur, :]], axis=0).astype(BF16)


def _band_fwd(name, d, n_pairs, maxd, head0, srcs, bidx_g, tab, sinks):
    nb = S // d // BLK
    (qa, qc), (ka, kc), (va, vc) = srcs
    out_spec = pl.BlockSpec((S // d, 128), lambda p, r: (0, r * n_pairs + p))
    smem = pl.BlockSpec(memory_space=pltpu.SMEM)
    full = pl.BlockSpec((BLK, 2 * BLK), lambda p, r: (0, 0))

    def body(tab_ref, sink_ref, q_ref, k_ref, v_ref, bidx_ref, o_ref, lse_ref, bias_ref):
        p, r = pl.program_id(0), pl.program_id(1)

        @pl.when(r == 0)
        def _():
            _band_fill(bias_ref, tab_ref, bidx_ref, head0 + 2 * p, maxd)

        lo = _lane_lo()
        sink = jnp.where(lax.broadcasted_iota(jnp.int32, (2 * BLK, 1), 0) < BLK, sink_ref[2 * p], sink_ref[2 * p + 1])

        def block(b, carry):
            cur, prev = _band_rows(b)
            qs = _stack_heads(q_ref[cur, :] * SCALE, lo)
            ks, vs = _stack_rows(k_ref, prev, cur), _stack_rows(v_ref, prev, cur)
            s = _dot(qs, ks, NT) + bias_ref[jnp.minimum(b, 1)]
            m = jnp.max(s, axis=1, keepdims=True)
            pr = jnp.exp(s - m)
            l = jnp.sum(pr, axis=1, keepdims=True)
            num = _dot(pr.astype(BF16), vs, NN)
            lse = m + jnp.log(l)
            sig = 1.0 / (1.0 + jnp.exp(sink - lse))
            o_ref[cur, :] = _unstack_heads(num * (sig / l), lo)
            lse_ref[cur, :] = _unstack_heads(lse + jnp.zeros((2 * BLK, 128), F32), lo)
            return carry

        lax.fori_loop(0, nb, block, 0, unroll=min(nb, 2))

    shape = jax.ShapeDtypeStruct((S // d, d * n_pairs * 128), F32)
    o, lse = pl.pallas_call(
        body, name=name, grid=(n_pairs, d),
        in_specs=[smem, smem, _class_spec(qa, qc, d), _class_spec(ka, kc, d), _class_spec(va, vc, d), full],
        out_specs=[out_spec, out_spec], out_shape=[shape, shape],
        scratch_shapes=[pltpu.VMEM((2, 2 * BLK, 2 * BLK), F32)],
        compiler_params=_params(("parallel", "arbitrary")))(
            tab, sinks, _classes(qa, d), _classes(ka, d), _classes(va, d), bidx_g)
    return o.reshape(S, n_pairs * 128), lse.reshape(S, n_pairs * 128)


def _band_bwd(name, d, n_pairs, maxd, head0, srcs, bidx_g, tab, sinks, o, lse, do, stats_in):
    nb = S // d // BLK
    rows = S // d
    (qa, qc), (ka, kc), (va, vc) = srcs
    cls_spec = pl.BlockSpec((rows, 128), lambda p, r: (0, r * n_pairs + p))
    smem = pl.BlockSpec(memory_space=pltpu.SMEM)
    full = pl.BlockSpec((BLK, 2 * BLK), lambda p, r: (0, 0))
    stat_spec = pl.BlockSpec((2, 8, 128), lambda p, r: (p, 0, 0))

    def body(tab_ref, sink_ref, q_ref, k_ref, v_ref, bidx_ref, o_ref, lse_ref, do_ref, sin_ref,
             dq_ref, dk_ref, dv_ref, stat_ref, bias_ref, dsacc_ref, sk_ref):
        p, r = pl.program_id(0), pl.program_id(1)

        @pl.when(r == 0)
        def _():
            _band_fill(bias_ref, tab_ref, bidx_ref, head0 + 2 * p, maxd)
            dsacc_ref[...] = jnp.zeros_like(dsacc_ref)
            sk_ref[...] = jnp.zeros_like(sk_ref)

        dk_ref[...] = jnp.zeros_like(dk_ref)
        dv_ref[...] = jnp.zeros_like(dv_ref)
        lo = _lane_lo()
        head1 = lax.broadcasted_iota(jnp.int32, (2 * BLK, 1), 0) >= BLK
        sink = jnp.where(head1, sink_ref[2 * p + 1], sink_ref[2 * p])

        def block(b, carry):
            cur, prev = _band_rows(b)
            qs = _stack_heads(q_ref[cur, :] * SCALE, lo)
            ks, vs = _stack_rows(k_ref, prev, cur), _stack_rows(v_ref, prev, cur)
            do = do_ref[cur, :]
            dos = _stack_heads(do, lo, F32)
            lse = jnp.concatenate(_per_head(lse_ref[cur, :], lo), axis=0)
            prod = do * o_ref[cur, :]
            delta = jnp.concatenate([jnp.sum(jnp.where(lo, prod, 0.0), axis=1, keepdims=True),
                                     jnp.sum(jnp.where(lo, 0.0, prod), axis=1, keepdims=True)], axis=0)
            sig = 1.0 / (1.0 + jnp.exp(sink - lse))
            pr = jnp.exp(_dot(qs, ks, NT) + bias_ref[jnp.minimum(b, 1)] - lse)
            ds = pr * (sig * (_dot(dos.astype(BF16), vs, NT) - delta))
            dsb = ds.astype(BF16)
            dq_ref[cur, :] = SCALE * _unstack_heads(_dot(dsb, ks, NN), lo)
            dk = _dot(dsb, qs, TN)
            dv = _dot(pr.astype(BF16), (sig * dos).astype(BF16), TN)
            dk_ref[prev, :] += dk[:BLK]
            dk_ref[cur, :] += dk[BLK:]
            dv_ref[prev, :] += dv[:BLK]
            dv_ref[cur, :] += dv[BLK:]
            dsacc_ref[...] += ds
            sink_grad = -delta * (1.0 - sig)
            for h in range(2):
                sk_ref[h] += jnp.zeros((8, 128), F32) + jnp.sum(sink_grad[h * BLK:(h + 1) * BLK])
            return carry

        lax.fori_loop(0, nb, block, 0)

        @pl.when(r == d - 1)
        def _():
            bi = bidx_ref[...]
            lane = lax.broadcasted_iota(jnp.int32, (8, 128), 1)
            sub = lax.broadcasted_iota(jnp.int32, (8, 128), 0)
            for h in range(2):
                acc = dsacc_ref[h * BLK:(h + 1) * BLK, :]
                row = jnp.where(jnp.logical_and(sub == 1, lane == 0), sk_ref[h], 0.0)
                for kk in range(NUM_BUCKETS):
                    tot = jnp.sum(jnp.where(bi == kk, acc, 0.0))
                    row = jnp.where(jnp.logical_and(sub == 0, lane == kk), tot, row)
                stat_ref[h] = row + jnp.where(sub == 0, sin_ref[h], 0.0)

    shape = jax.ShapeDtypeStruct((rows, d * n_pairs * 128), F32)
    dq, dk, dv, stats = pl.pallas_call(
        body, name=name, grid=(n_pairs, d),
        in_specs=[smem, smem, _class_spec(qa, qc, d), _class_spec(ka, kc, d), _class_spec(va, vc, d), full,
                  cls_spec, cls_spec, cls_spec, stat_spec],
        out_specs=[cls_spec, cls_spec, cls_spec, stat_spec],
        out_shape=[shape, shape, shape, jax.ShapeDtypeStruct((2 * n_pairs, 8, 128), F32)],
        scratch_shapes=[pltpu.VMEM((2, 2 * BLK, 2 * BLK), F32), pltpu.VMEM((2 * BLK, 2 * BLK), F32),
                        pltpu.VMEM((2, 8, 128), F32)],
        compiler_params=_params(("arbitrary", "arbitrary")))(
            tab, sinks, _classes(qa, d), _classes(ka, d), _classes(va, d), bidx_g,
            _classes(o, d), _classes(lse, d), _classes(do, d), stats_in)
    width = n_pairs * 128
    return dq.reshape(S, width), dk.reshape(S, width), dv.reshape(S, width), stats


def _comb_fwd(o_g, lse_g):
    def body(o0, o1, o2, l0, l1, l2, out_ref, outb_ref, lse_ref):
        a0, a1, a2 = l0[...], l1[...], l2[...]
        m = jnp.maximum(jnp.maximum(a0, a1), a2)
        e0, e1, e2 = jnp.exp(a0 - m), jnp.exp(a1 - m), jnp.exp(a2 - m)
        tot = e0 + e1 + e2
        out = (e0 * o0[...] + e1 * o1[...] + e2 * o2[...]) / tot
        out_ref[...] = out
        outb_ref[...] = out.astype(BF16)
        lse_ref[...] = m + jnp.log(tot)

    spec = _row_spec(4 * HD)
    f32 = jax.ShapeDtypeStruct((S, 4 * HD), F32)
    return pl.pallas_call(
        body, name="comb_fwd", grid=(S // TR,), in_specs=[spec] * 6, out_specs=[spec] * 3,
        out_shape=[f32, jax.ShapeDtypeStruct((S, 4 * HD), BF16), f32],
        compiler_params=_params(("parallel",)))(*o_g, *lse_g)


def _split2(x):
    hi = x.astype(BF16)
    return hi, (x - hi.astype(F32)).astype(BF16)


KB = 2 * BLK
SBQ = 2 * BLK


def _tri_sum(x, tri):
    hi, lo = _split2(x)
    both = _dot(jnp.concatenate([hi, lo], axis=0), tri, NN)
    return both[:x.shape[0]] + both[x.shape[0]:]


def _tri(strict_upper):
    r = lax.broadcasted_iota(jnp.int32, (KB, KB), 0)
    c = lax.broadcasted_iota(jnp.int32, (KB, KB), 1)
    return jnp.where(r > c if strict_upper else r < c, 1.0, 0.0).astype(BF16)


def _sb_terms(qs, kj, before):
    z = _dot(qs, kj, NT)
    lsp = jnp.minimum(z, 0.0) - jnp.log(1.0 + jnp.exp(-jnp.abs(z)))
    return lsp, jnp.where(before, lsp - z, 0.0)


def _sb_before(i, m):
    t = (lax.broadcasted_iota(jnp.int32, (2 * SBQ, KB), 0) & (SBQ - 1)) + i * SBQ
    s = lax.broadcasted_iota(jnp.int32, (2 * SBQ, KB), 1) + m * KB
    return s < t


C_COL = 3072 // 128


def _sb_fwd(proj):
    blk = lambda off: pl.BlockSpec((SBQ, 128), lambda p, i: (i, off + p))
    col = lambda off: pl.BlockSpec((S, 128), lambda p, i: (0, off + p))
    out = pl.BlockSpec((SBQ, 128), lambda p, i: (i, p))

    def body(q_ref, k_ref, v_ref, o_ref, ob_ref, tot_ref):
        i = pl.program_id(1)
        lo = _lane_lo(SBQ)
        qs = _stack_heads(q_ref[...] * SCALE, lo)
        suffix = _tri(True)

        def step(n, carry):
            acc, rest = carry
            m = i - n
            rows = pl.ds(pl.multiple_of(m * KB, KB), KB)
            kj, vj = k_ref[rows, :].astype(BF16), v_ref[rows, :].astype(BF16)
            before = _sb_before(i, m)
            lsp, lk = _sb_terms(qs, kj, before)
            w = jnp.where(before, jnp.exp(lsp + _tri_sum(lk, suffix) + rest), 0.0)
            return acc + _dot(w.astype(BF16), vj, NN), rest + jnp.sum(lk, axis=1, keepdims=True)

        acc, rest = lax.fori_loop(0, i + 1, step, (jnp.zeros((2 * SBQ, 128), F32), jnp.zeros((2 * SBQ, 1), F32)))
        o = _unstack_heads(acc, lo)
        o_ref[...] = o
        ob_ref[...] = o.astype(BF16)
        tot_ref[...] = _unstack_heads(rest + jnp.zeros((2 * SBQ, 128), F32), lo)

    f32 = jax.ShapeDtypeStruct((S, 4 * HD), F32)
    return pl.pallas_call(
        body, name="sb_fwd", grid=(2, S // SBQ), in_specs=[blk(C_COL), col(C_COL + 2), col(C_COL + 4)],
        out_specs=[out, out, out], out_shape=[f32, jax.ShapeDtypeStruct((S, 4 * HD), BF16), f32],
        compiler_params=_params(("parallel", "arbitrary")))(proj, proj, proj)


def _sb_bwd(proj, tot, do):
    blk = lambda off: pl.BlockSpec((SBQ, 128), lambda p, i: (i, off + p))
    col = lambda off: pl.BlockSpec((S, 128), lambda p, i: (0, off + p))

    def body(q_ref, k_ref, v_ref, tot_ref, do_ref, dq_ref, dk_ref, dv_ref):
        i = pl.program_id(1)

        @pl.when(i == 0)
        def _():
            dk_ref[...] = jnp.zeros_like(dk_ref)
            dv_ref[...] = jnp.zeros_like(dv_ref)

        lo = _lane_lo(SBQ)
        qs = _stack_heads(q_ref[...] * SCALE, lo)
        dos = _stack_heads(do_ref[...], lo)
        tots = jnp.concatenate(_per_head(tot_ref[...], lo), axis=0)
        prefix = _tri(False)

        def step(m, carry):
            dq, keep_left, g_left = carry
            rows = pl.ds(pl.multiple_of(m * KB, KB), KB)
            kj, vj = k_ref[rows, :].astype(BF16), v_ref[rows, :].astype(BF16)
            before = _sb_before(i, m)
            lsp, lk = _sb_terms(qs, kj, before)
            log_rest = tots - keep_left - lk - _tri_sum(lk, prefix)
            w = jnp.where(before, jnp.exp(lsp + log_rest), 0.0)
            g = w * _dot(dos, vj, NT)
            g_before = g_left + _dot(g.astype(BF16), prefix, NN)
            beta = jnp.exp(lsp)
            dz = jnp.where(before, g * (1.0 - beta) - g_before * beta, 0.0).astype(BF16)
            dk_ref[rows, :] += _dot(dz, qs, TN)
            dv_ref[rows, :] += _dot(w.astype(BF16), dos, TN)
            return (dq + _dot(dz, kj, NN), keep_left + jnp.sum(lk, axis=1, keepdims=True),
                    g_left + jnp.sum(g, axis=1, keepdims=True))

        zero = (jnp.zeros((2 * SBQ, 128), F32), jnp.zeros((2 * SBQ, 1), F32), jnp.zeros((2 * SBQ, 1), F32))
        dq, _, _ = lax.fori_loop(0, i + 1, step, zero)
        dq_ref[...] = SCALE * _unstack_heads(dq, lo)

    out_blk = pl.BlockSpec((SBQ, 128), lambda p, i: (i, p))
    out_col = pl.BlockSpec((S, 128), lambda p, i: (0, p))
    f32 = jax.ShapeDtypeStruct((S, 4 * HD), F32)
    return pl.pallas_call(
        body, name="sb_bwd", grid=(2, S // SBQ),
        in_specs=[blk(C_COL), col(C_COL + 2), col(C_COL + 4), out_blk, out_blk],
        out_specs=[out_blk, out_col, out_col], out_shape=[f32, f32, f32],
        compiler_params=_params(("arbitrary", "arbitrary")))(proj, proj, proj, tot, do)


TG = 256
GATE_BLK0 = OFF_GATE // TG


def _gate_specs():
    grid = (D // TG, S // TG)
    p_specs = [pl.BlockSpec((TG, TG), functools.partial(lambda c, r, br: (r, GATE_BLK0 + br * (D // TG) + c), br=br))
               for br in range(3)]
    b_spec = pl.BlockSpec((3, TG), lambda c, r: (0, c))
    t_spec = pl.BlockSpec((TG, TG), lambda c, r: (r, c))
    return grid, p_specs, b_spec, t_spec


def _sigmoid(x):
    return 1.0 / (1.0 + jnp.exp(-x))


def _three_rows(rows):
    sub = lax.broadcasted_iota(jnp.int32, (3, rows[0].shape[1]), 0)
    return jnp.where(sub == 0, rows[0], jnp.where(sub == 1, rows[1], rows[2]))


def _gate_fwd(proj, b_gate, br):
    grid, p_specs, b_spec, t_spec = _gate_specs()

    def body(p0, p1, p2, b_ref, r0, r1, r2, out_ref):
        acc = jnp.zeros((TG, TG), F32)
        for n, (p, r) in enumerate(((p0, r0), (p1, r1), (p2, r2))):
            acc += _sigmoid(p[...] + b_ref[n:n + 1, :]) * r[...]
        out_ref[...] = acc.astype(BF16)

    return pl.pallas_call(
        body, name="gate_fwd", grid=grid, in_specs=p_specs + [b_spec] + [t_spec] * 3, out_specs=t_spec,
        out_shape=jax.ShapeDtypeStruct((S, D), BF16),
        compiler_params=_params(("parallel", "parallel")))(proj, proj, proj, b_gate, *br)


def _gate_bwd(proj, b_gate, br, dmerged):
    grid, p_specs, b_spec, t_spec = _gate_specs()

    def body(p0, p1, p2, b_ref, r0, r1, r2, dm_ref, e0, e1, e2, g0, g1, g2, db_ref):
        dm = dm_ref[...]
        rows = []
        for n, (p, r, e_ref, dg_ref) in enumerate(((p0, r0, e0, g0), (p1, r1, e1, g1), (p2, r2, e2, g2))):
            g = _sigmoid(p[...] + b_ref[n:n + 1, :])
            e_ref[...] = (dm * g).astype(BF16)
            dpre = dm * r[...] * g * (1.0 - g)
            dg_ref[...] = dpre.astype(BF16)
            rows.append(jnp.sum(dpre, axis=0, keepdims=True))
        db = _three_rows(rows)

        @pl.when(pl.program_id(1) == 0)
        def _():
            db_ref[...] = db

        @pl.when(pl.program_id(1) > 0)
        def _():
            db_ref[...] += db

    bf = jax.ShapeDtypeStruct((S, D), BF16)
    out = pl.pallas_call(
        body, name="gate_bwd", grid=grid, in_specs=p_specs + [b_spec] + [t_spec] * 4,
        out_specs=[t_spec] * 6 + [b_spec], out_shape=[bf] * 6 + [jax.ShapeDtypeStruct((3, D), F32)],
        compiler_params=_params(("parallel", "arbitrary")))(proj, proj, proj, b_gate, *br, dmerged)
    return out[:3], out[3:6], out[6]


TC = 256
N_FF_BLK = D_FF // TC
GELU_C = math.sqrt(2.0 / math.pi)


def _shift_down(x, n):
    rows = lax.broadcasted_iota(jnp.int32, x.shape, 0)
    return jnp.where(rows >= n, pltpu.roll(x, n, axis=0), 0.0)


def _shift_up(x, n):
    rows = lax.broadcasted_iota(jnp.int32, x.shape, 0)
    return jnp.where(rows < x.shape[0] - n, pltpu.roll(x, x.shape[0] - n, axis=0), 0.0)


def _conv(u, w, b):
    s1, s2 = _shift_down(u, 1), _shift_down(u, 2)
    return w[2:3, :] * u + w[1:2, :] * s1 + w[0:1, :] * s2 + b, s1, s2


def _gelu_parts(x):
    inner = GELU_C * (x + 0.044715 * x * x * x)
    t = jnp.tanh(inner)
    gelu = 0.5 * x * (1.0 + t)
    dgelu = 0.5 * (1.0 + t) + 0.5 * x * (1.0 - t * t) * GELU_C * (1.0 + 3 * 0.044715 * x * x)
    return gelu, dgelu


def _conv_specs():
    ug = pl.BlockSpec((S, TC), lambda c: (0, c))
    uv = pl.BlockSpec((S, TC), lambda c: (0, N_FF_BLK + c))
    wg = pl.BlockSpec((3, TC), lambda c: (0, c))
    wv = pl.BlockSpec((3, TC), lambda c: (0, N_FF_BLK + c))
    bg = pl.BlockSpec((1, TC), lambda c: (0, c))
    bv = pl.BlockSpec((1, TC), lambda c: (0, N_FF_BLK + c))
    return ug, uv, wg, wv, bg, bv


def _conv_fwd(u, conv_w, conv_b):
    ug, uv, wg, wv, bg, bv = _conv_specs()

    def body(ug_ref, uv_ref, wg_ref, wv_ref, bg_ref, bv_ref, a_ref):
        gc = _conv(ug_ref[...], wg_ref[...], bg_ref[...])[0]
        vc = _conv(uv_ref[...], wv_ref[...], bv_ref[...])[0]
        a_ref[...] = (_gelu_parts(gc)[0] * vc).astype(BF16)

    return pl.pallas_call(
        body, name="conv_fwd", grid=(N_FF_BLK,), in_specs=[ug, uv, wg, wv, bg, bv], out_specs=ug,
        out_shape=jax.ShapeDtypeStruct((S, D_FF), BF16),
        compiler_params=_params(("parallel",)))(u, u, conv_w, conv_w, conv_b, conv_b)


def _conv_bwd(u, conv_w, conv_b, da):
    ug, uv, wg, wv, bg, bv = _conv_specs()

    def back(duc, u, s1, s2, w):
        du = w[2:3, :] * duc + w[1:2, :] * _shift_up(duc, 1) + w[0:1, :] * _shift_up(duc, 2)
        dw = _three_rows([jnp.sum(duc * s2, axis=0, keepdims=True), jnp.sum(duc * s1, axis=0, keepdims=True),
                          jnp.sum(duc * u, axis=0, keepdims=True)])
        return du, dw, jnp.sum(duc, axis=0, keepdims=True)

    def body(ug_ref, uv_ref, wg_ref, wv_ref, bg_ref, bv_ref, da_ref, dug_ref, duv_ref, dwg_ref, dwv_ref, dbg_ref, dbv_ref):
        u_g, u_v = ug_ref[...], uv_ref[...]
        gc, g1, g2 = _conv(u_g, wg_ref[...], bg_ref[...])
        vc, v1, v2 = _conv(u_v, wv_ref[...], bv_ref[...])
        gelu, dgelu = _gelu_parts(gc)
        da = da_ref[...]
        du, dw, db = back(da * vc * dgelu, u_g, g1, g2, wg_ref[...])
        dug_ref[...] = du.astype(BF16)
        dwg_ref[...] = dw
        dbg_ref[...] = db
        du, dw, db = back(da * gelu, u_v, v1, v2, wv_ref[...])
        duv_ref[...] = du.astype(BF16)
        dwv_ref[...] = dw
        dbv_ref[...] = db

    return pl.pallas_call(
        body, name="conv_bwd", grid=(N_FF_BLK,), in_specs=[ug, uv, wg, wv, bg, bv, ug],
        out_specs=[ug, ug, wg, wg, bg, bg],
        out_shape=[jax.ShapeDtypeStruct((S, D_FF), BF16), jax.ShapeDtypeStruct((S, D_FF), BF16),
                   jax.ShapeDtypeStruct((3, D_FF), F32), jax.ShapeDtypeStruct((3, D_FF), F32),
                   jax.ShapeDtypeStruct((1, D_FF), F32), jax.ShapeDtypeStruct((1, D_FF), F32)],
        compiler_params=_params(("parallel",)))(u, u, conv_w, conv_w, conv_b, conv_b, da)


def _adamw(name, w, g, m, v):
    shape = w.shape
    cols = shape[-1]
    flat = [t.reshape(-1, cols) for t in (w, g, m, v)]
    r = flat[0].shape[0]
    tr = min(128, r)

    def body(w_ref, g_ref, m_ref, v_ref, d_ref, mo_ref, vo_ref):
        g = g_ref[...]
        m = ADAM_B1 * m_ref[...] + (1.0 - ADAM_B1) * g
        v = ADAM_B2 * v_ref[...] + (1.0 - ADAM_B2) * (g * g)
        m_hat = m / (1.0 - ADAM_B1 ** ADAM_STEP)
        v_hat = v / (1.0 - ADAM_B2 ** ADAM_STEP)
        d_ref[...] = -ADAM_LR * (m_hat / (jnp.sqrt(v_hat) + ADAM_EPS) + ADAM_WD * w_ref[...])
        mo_ref[...] = m
        vo_ref[...] = v

    spec = pl.BlockSpec((tr, cols), lambda i: (i, 0))
    outs = pl.pallas_call(
        body, name=name, grid=(pl.cdiv(r, tr),), in_specs=[spec] * 4, out_specs=[spec] * 3,
        out_shape=[jax.ShapeDtypeStruct((r, cols), F32)] * 3, compiler_params=_params(("parallel",)))(*flat)
    return [t.reshape(shape) for t in outs]


def _place():
    x, y, c = lax.axis_index("x"), lax.axis_index("y"), lax.axis_index("c")
    chips = [(1 - x, y), (x, 1 - y), (1 - x, 1 - y)]
    return x, y, c, chips


def _scalars(*vals):
    return jnp.stack([jnp.asarray(v, jnp.int32) for v in vals])


HBM = pl.BlockSpec(memory_space=pltpu.HBM)
SEM = pl.BlockSpec(memory_space=pltpu.SEMAPHORE)
SPLIT_COPY = pltpu.CompilerParams(has_side_effects=pltpu.SideEffectType.DATAFLOW_SIDE_EFFECTING)


def _in_hbm(x):
    return pltpu.with_memory_space_constraint(x, pltpu.HBM)


def _cast_into_slot(name, w, layer, chip):
    _, k, n4 = w.shape
    tr = max(t for t in range(16, 257, 16) if k % t == 0)

    def body(chip_ref, w_ref, o_ref):
        o_ref[...] = w_ref[...].astype(BF16)

    return pl.pallas_call(
        body, name=name,
        grid_spec=pltpu.PrefetchScalarGridSpec(
            num_scalar_prefetch=1, grid=(k // tr,),
            in_specs=[pl.BlockSpec((None, tr, n4), lambda i, s: (layer, i, 0))],
            out_specs=pl.BlockSpec((None, tr, n4), lambda i, s: (s[0], i, 0))),
        out_shape=jax.ShapeDtypeStruct((N_CHIPS, k, n4), BF16),
        compiler_params=_params(("parallel",)))(_scalars(chip), w)


def _gather_copy(buf_ref, k, from_chip, send_sem, recv_sem, chips, c):
    rows = buf_ref.at[from_chip]
    return pltpu.make_async_remote_copy(src_ref=rows, dst_ref=rows, send_sem=send_sem, recv_sem=recv_sem,
                                        device_id=(*chips[k], c), device_id_type=MESH)


def _gather_start(name, bufs, groups):
    n, ng = len(bufs), len(groups)
    where = {a: (gi, e) for gi, g in enumerate(groups) for e, a in enumerate(g)}

    def body(*refs):
        ins, sems, token = refs[:n], refs[n:n + 2 * ng], refs[-1]
        x, y, c, chips = _place()
        for a in range(n):
            gi, e = where[a]
            for k in range(3):
                _gather_copy(ins[a], k, 2 * x + y, sems[2 * gi].at[3 * e + k], sems[2 * gi + 1].at[3 * e + k],
                             chips, c).start()
        token[...] = jnp.zeros_like(token)

    out_shape = [pltpu.SemaphoreType.DMA((3 * len(g),)) for g in groups for _ in range(2)]
    out_shape += [pltpu.HBM(b.shape, b.dtype) for b in bufs] + [jax.ShapeDtypeStruct((8, 128), F32)]
    out = pl.pallas_call(
        body, name=name, in_specs=[HBM] * n,
        out_specs=[SEM] * (2 * ng) + [HBM] * n + [pl.BlockSpec(memory_space=pltpu.VMEM)], out_shape=out_shape,
        input_output_aliases={a: 2 * ng + a for a in range(n)}, compiler_params=SPLIT_COPY)(*[_in_hbm(b) for b in bufs])
    sems = [(out[2 * gi], out[2 * gi + 1]) for gi in range(ng)]
    return sems, list(out[2 * ng:2 * ng + n]), out[-1]


def _gather_wait(name, bufs, send, recv, after):
    n = len(bufs)

    def body(*refs):
        ins, send_sem, recv_sem = refs[:n], refs[n], refs[n + 1]
        x, y, c, chips = _place()
        for e in range(n):
            for k in range(3):
                sems = (send_sem.at[3 * e + k], recv_sem.at[3 * e + k])
                _gather_copy(ins[e], k, 2 * x + y, *sems, chips, c).wait_send()
                _gather_copy(ins[e], k, 2 * chips[k][0] + chips[k][1], *sems, chips, c).wait_recv()

    return pl.pallas_call(
        body, name=name, in_specs=[HBM] * n + [SEM, SEM, ANY], out_specs=[HBM] * n,
        out_shape=[pltpu.HBM(b.shape, b.dtype) for b in bufs],
        input_output_aliases={a: a for a in range(n)}, compiler_params=SPLIT_COPY)(*bufs, send, recv, after)


def _reduce_copy(g_ref, land_ref, mask, send_sem, recv_sem, x, y, c, sending):
    px, py, pc = x ^ ((mask >> 2) & 1), y ^ ((mask >> 1) & 1), c ^ (mask & 1)
    half = g_ref.shape[1] // 2
    src = g_ref.at[2 * px + py, pl.ds(pl.multiple_of(pc * half, half), half)]
    dst = land_ref.at[4 * x + 2 * y + c] if sending else land_ref.at[4 * px + 2 * py + pc]
    return pltpu.make_async_remote_copy(src_ref=src, dst_ref=dst, send_sem=send_sem, recv_sem=recv_sem,
                                        device_id=(px, py, pc), device_id_type=MESH)


def _reduce_start(name, grads):
    n = len(grads)
    lands = [lax.empty((N_DEV, g.shape[1] // 2, g.shape[2]), g.dtype) for g in grads]

    def body(*refs):
        gs, ls, send_sem, recv_sem = refs[:n], refs[n:2 * n], refs[2 * n], refs[2 * n + 1]
        x, y, c, _ = _place()
        for a in range(n):
            for mask in range(1, N_DEV):
                s = (N_DEV - 1) * a + mask - 1
                _reduce_copy(gs[a], ls[a], mask, send_sem.at[s], recv_sem.at[s], x, y, c, True).start()
        refs[-1][...] = jnp.zeros_like(refs[-1])

    sem = pltpu.SemaphoreType.DMA((n * (N_DEV - 1),))
    out = pl.pallas_call(
        body, name=name, in_specs=[HBM] * (2 * n),
        out_specs=[SEM, SEM] + [HBM] * (2 * n) + [pl.BlockSpec(memory_space=pltpu.VMEM)],
        out_shape=[sem, sem] + [pltpu.HBM(t.shape, t.dtype) for t in grads + lands] + [jax.ShapeDtypeStruct((8, 128), F32)],
        input_output_aliases={a: 2 + a for a in range(2 * n)}, compiler_params=SPLIT_COPY)(
            *[_in_hbm(t) for t in grads + lands])
    return out[0], out[1], list(out[2:2 + n]), list(out[2 + n:2 + 2 * n]), out[-1]


def _reduce_wait(name, send, recv, grads, lands, after):
    n = len(grads)

    def body(*refs):
        gs, ls, send_sem, recv_sem = refs[:n], refs[n:2 * n], refs[2 * n], refs[2 * n + 1]
        x, y, c, _ = _place()
        for a in range(n):
            for mask in range(1, N_DEV):
                s = (N_DEV - 1) * a + mask - 1
                sems = (send_sem.at[s], recv_sem.at[s])
                _reduce_copy(gs[a], ls[a], mask, *sems, x, y, c, True).wait_send()
                _reduce_copy(gs[a], ls[a], mask, *sems, x, y, c, False).wait_recv()

    out = pl.pallas_call(
        body, name=name, in_specs=[HBM] * (2 * n) + [SEM, SEM, ANY], out_specs=[HBM] * (2 * n),
        out_shape=[pltpu.HBM(t.shape, t.dtype) for t in grads + lands],
        input_output_aliases={a: a for a in range(2 * n)}, compiler_params=SPLIT_COPY)(*grads, *lands, send, recv, after)
    return list(out[:n]), list(out[n:])


def _reduce_sum(name, g, land, layer, into, chip, c):
    _, k4, n4 = g.shape
    half = k4 // 2
    tr = max(t for t in range(16, 513, 16) if half % t == 0)
    per = half // tr
    me = 2 * chip + c

    def body(s_ref, own_ref, *refs):
        total = own_ref[...].astype(F32)
        for ref in refs[:N_DEV - 1]:
            total = total + ref[...].astype(F32)
        refs[-1][...] = total

    in_specs = [pl.BlockSpec((None, tr, n4), lambda i, s: (s[0], s[1] * per + i, 0))]
    in_specs += [pl.BlockSpec((None, tr, n4), functools.partial(lambda i, s, m: (s[1 + m], i, 0), m=m))
                 for m in range(1, N_DEV)]
    ins = [g] + [land] * (N_DEV - 1)
    aliases = {}
    if into is not None:
        in_specs, ins, aliases = in_specs + [ANY], ins + [into], {1 + N_DEV: 0}
    return pl.pallas_call(
        body, name=name,
        grid_spec=pltpu.PrefetchScalarGridSpec(
            num_scalar_prefetch=1, grid=(per,), in_specs=in_specs,
            out_specs=pl.BlockSpec((None, tr, n4), lambda i, s: (layer, s[1] * per + i, 0))),
        out_shape=jax.ShapeDtypeStruct((DEPTH, k4, n4), F32), input_output_aliases=aliases,
        compiler_params=_params(("parallel",)))(_scalars(chip, c, *[me ^ m for m in range(1, N_DEV)]), *ins)


def _join_halves(name, bufs):
    n = len(bufs)

    def body(*refs):
        ins, outs = refs[:n], refs[n:2 * n]
        send_sem, recv_sem = refs[2 * n:]
        x, y, c, _ = _place()

        def rows(ref, which):
            half = ref.shape[1] // 2
            return ref.at[:, pl.ds(pl.multiple_of(which * half, half), half)]

        sends = [pltpu.make_async_remote_copy(
            src_ref=rows(ins[a], c), dst_ref=rows(outs[a], c), send_sem=send_sem.at[a], recv_sem=recv_sem.at[a],
            device_id=(x, y, 1 - c), device_id_type=MESH) for a in range(n)]
        for cp in sends:
            cp.start()
        for a in range(n):
            sends[a].wait_send()
            pltpu.make_async_remote_copy(
                src_ref=rows(ins[a], c), dst_ref=rows(outs[a], 1 - c), send_sem=send_sem.at[a], recv_sem=recv_sem.at[a],
                device_id=(x, y, 1 - c), device_id_type=MESH).wait_recv()

    return pl.pallas_call(
        body, name=name, in_specs=[ANY] * n, out_specs=[ANY] * n,
        out_shape=[jax.ShapeDtypeStruct(b.shape, b.dtype) for b in bufs],
        input_output_aliases={a: a for a in range(n)},
        scratch_shapes=[pltpu.SemaphoreType.DMA((n,)), pltpu.SemaphoreType.DMA((n,))],
    )(*bufs)


def _all_reduce_small(block):
    r = block.shape[0]

    def body(x_ref, out_ref, slots, send_sem, recv_sem):
        x, y, c, _ = _place()
        me = 4 * x + 2 * y + c
        slots[me] = x_ref[...]
        sends = []
        for mask in range(1, N_DEV):
            fx, fy, fc = (mask >> 2) & 1, (mask >> 1) & 1, mask & 1
            peer = (x ^ fx, y ^ fy, c ^ fc)
            cp = pltpu.make_async_remote_copy(
                src_ref=x_ref, dst_ref=slots.at[me], send_sem=send_sem.at[mask - 1], recv_sem=recv_sem.at[mask - 1],
                device_id=peer, device_id_type=MESH)
            cp.start()
            sends.append(cp)
        for mask in range(1, N_DEV):
            src = me ^ mask
            pltpu.make_async_remote_copy(
                src_ref=x_ref, dst_ref=slots.at[src], send_sem=send_sem.at[mask - 1], recv_sem=recv_sem.at[mask - 1],
                device_id=(x, y, c), device_id_type=MESH).wait_recv()
        for cp in sends:
            cp.wait_send()
        total = slots[0]
        for d in range(1, N_DEV):
            total = total + slots[d]
        out_ref[...] = total

    vmem = pl.BlockSpec(memory_space=pltpu.VMEM)
    return pl.pallas_call(
        body, name="all_reduce_small", in_specs=[vmem], out_specs=vmem,
        out_shape=jax.ShapeDtypeStruct((r, 128), F32),
        scratch_shapes=[pltpu.VMEM((N_DEV, r, 128), F32), pltpu.SemaphoreType.DMA((N_DEV - 1,)),
                        pltpu.SemaphoreType.DMA((N_DEV - 1,))],
        compiler_params=pltpu.CompilerParams(vmem_limit_bytes=VMEM_LIMIT))(block)


B_Q_COL = 2304 // 128
B_K0, B_V0, B_END = 2816, 2944, 3072


def _full_cols(w_g):
    return w_g.transpose(1, 0, 2).reshape(w_g.shape[1], -1)


def _group_src(proj, g):
    if A_GROUPS[g][1] == 1:
        return ((proj, 2 * g), (proj, 6 + 2 * g), (proj, 12 + 2 * g))
    packed = jnp.concatenate([proj[:, t * 768 + g * 256:t * 768 + (g + 1) * 256] for t in range(3)], axis=1)
    return ((packed, 0), (packed, 2), (packed, 4))


def _kv_expand(kv):
    return jnp.broadcast_to(kv.reshape(S, 2, 1, HD), (S, 2, 4, HD)).reshape(S, 8 * HD)


def _kv_reduce(dkv):
    return dkv.reshape(S, 2, 4, HD).sum(axis=2).reshape(S, 2 * HD)


def _mixer_fwd(h1, wget, rel_bias, sinks_l, bidx):
    w = dict(wget(0, h1))
    proj = _mm_nt("proj_in", h1, w["w_in"], F32, tn=1152)
    no_sinks = jnp.full((4,), NEG, F32)
    srcs = [_group_src(proj, g) for g in range(3)]
    o_g, lse_g = [], []
    for g, (_, d) in enumerate(A_GROUPS):
        o, lse = _band_fwd("band_fwd_g%d" % g, d, 2, BLK, 4 * g, srcs[g], bidx[g], rel_bias, no_sinks)
        o_g.append(o)
        lse_g.append(lse)
    o_a32, o_a, lse_a = _comb_fwd(o_g, lse_g)
    src_b = ((proj, B_Q_COL), (_kv_expand(proj[:, B_K0:B_V0]), 0), (_kv_expand(proj[:, B_V0:B_END]), 0))
    o_b32, lse_b = _band_fwd("band_fwd_b", 1, 4, BLK - 1, N_A, src_b, bidx[3], rel_bias, sinks_l)
    o_b = o_b32.astype(BF16)
    o_c32, o_c, tot_c = _sb_fwd(proj)
    w.update(wget(1, o_c32))
    br = [_mm_nn("branch_a", o_a, w["w_br_a"], F32), _mm_nn("branch_b", o_b, w["w_br_b"], F32),
          _mm_nn("branch_c", o_c, w["w_br_c"], F32)]
    merged = _gate_fwd(proj, w["b_gate"], br)
    mo = _mm_nn("out_proj", merged, w["w_out"], F32)
    saved = dict(proj=proj, srcs=srcs, src_b=src_b, o_a32=o_a32, lse_a=lse_a, o_b32=o_b32, lse_b=lse_b, tot_c=tot_c,
                 o_a=o_a, o_b=o_b, o_c=o_c, br=br, merged=merged)
    return mo, saved, w


def _mixer_bwd(d_mo, h1, w, sv, rel_bias, sinks_l, bidx, stats_in, emit):
    grads = {}
    dmerged = _mm_nt("out_proj_dx", d_mo, w["w_out"], F32)
    grads["w_out"] = _mm_tn_sharded("out_proj_dw", sv["merged"], d_mo, True)
    e, dgate, db_gate = _gate_bwd(sv["proj"], w["b_gate"], sv["br"], dmerged)
    grads["b_gate"] = db_gate
    d_o = {}
    for n, name in enumerate("abc"):
        d_o[name] = _mm_nt("branch_%s_dx" % name, e[n], w["w_br_" + name], F32)
        grads["w_br_" + name] = _mm_tn_sharded("branch_%s_dw" % name, sv["o_" + name], e[n], False)
    zero = emit(1, grads)
    no_sinks = jnp.full((4,), NEG, F32) + zero[0]
    dqs, dks, dvs, stats = [], [], [], []
    for g, (_, d) in enumerate(A_GROUPS):
        dq, dk, dv, st = _band_bwd("band_bwd_g%d" % g, d, 2, BLK, 4 * g, sv["srcs"][g], bidx[g], rel_bias, no_sinks,
                                   sv["o_a32"], sv["lse_a"], d_o["a"], stats_in[4 * g:4 * g + 4])
        dqs.append(dq)
        dks.append(dk)
        dvs.append(dv)
        stats.append(st)
    dq_b, dk_x, dv_x, st = _band_bwd("band_bwd_b", 1, 4, BLK - 1, N_A, sv["src_b"], bidx[3], rel_bias, sinks_l,
                                     sv["o_b32"], sv["lse_b"], d_o["b"], stats_in[N_A:])
    stats = jnp.concatenate(stats + [st], axis=0)
    dcq, dck, dcv = _sb_bwd(sv["proj"], sv["tot_c"], d_o["c"])
    cols = dqs + dks + dvs + [dq_b, _kv_reduce(dk_x), _kv_reduce(dv_x), dcq, dck, dcv]
    dproj = jnp.concatenate([t.astype(BF16) for t in cols] + list(dgate), axis=1)
    grads["w_in"] = _mm_tn("proj_in_dw", dproj, h1, BF16, tm=768).reshape(N_CHIPS, IN_SHARD, D)
    zero = emit(2, grads)
    dh1 = _mm_nn("proj_in_dx", dproj, w["w_in"], F32)
    return dh1, grads, stats, zero


def _ffn_fwd(h2, w):
    u = _mm_nn("ffn_up", h2, w["w_up"], F32, tn=1024)
    a = _conv_fwd(u, w["conv_w"], w["conv_b"])
    dn = _mm_nn("ffn_down", a, w["w_down"], F32)
    return dn, dict(u=u, a=a)


def _ffn_bwd(d_dn, h2, w, sv):
    grads = {}
    da = _mm_nt("ffn_down_dx", d_dn, w["w_down"], F32, tn=1024)
    grads["w_down"] = _mm_tn_sharded("ffn_down_dw", sv["a"], d_dn, True)
    dug, duv, dwg, dwv, dbg, dbv = _conv_bwd(sv["u"], w["conv_w"], w["conv_b"], da)
    du = jnp.concatenate([dug, duv], axis=1)
    grads["conv_w"] = jnp.concatenate([dwg, dwv], axis=1)
    grads["conv_b"] = jnp.concatenate([dbg, dbv], axis=1)
    dh2 = _mm_nt("ffn_up_dx", du, w["w_up"], F32)
    grads["w_up"] = _mm_tn_sharded("ffn_up_dw", h2, du, False, tn=1024)
    return dh2, grads


BIG = ("w_in", "w_br_a", "w_br_b", "w_br_c", "w_out", "w_up", "w_down")


def _shard_view(name, w):
    return jnp.swapaxes(w, 1, 2) if name == "w_in" else w
WEIGHT_GROUPS = (("w_in", "b_gate"), ("w_br_a", "w_br_b", "w_br_c", "w_out"), ("w_up", "conv_w", "w_down"))
GRAD_GROUPS = (("w_down", "w_up"), ("w_out", "w_br_a", "w_br_b", "w_br_c"), ("w_in",))
SMALL_ROWS = (("rel_bias", 8), ("attn_pre_norm", 16), ("attn_post_norm", 16), ("ffn_pre_norm", 16), ("ffn_post_norm", 16),
              ("sinks", 8), ("conv_b", 128), ("b_gate", 48), ("conv_w", 384), ("loss", 8))


def _pack_small(vals):
    rows = []
    for name, n in SMALL_ROWS:
        flat = vals[name].reshape(-1).astype(F32)
        rows.append(jnp.pad(flat, (0, n * 128 - flat.shape[0])).reshape(n, 128))
    return jnp.concatenate(rows, axis=0)


def _unpack_small(block, shapes):
    out, row = {}, 0
    for name, n in SMALL_ROWS:
        size = int(np.prod(shapes[name]))
        out[name] = block[row:row + n].reshape(-1)[:size].reshape(shapes[name])
        row += n
    return out


def kernel(x, rel_bias, attn_pre_norm, w_in, b_gate, sinks, w_br_a, w_br_b, w_br_c, w_out, attn_post_norm, ffn_pre_norm, w_up, conv_w, conv_b, w_down, ffn_post_norm, loss_target, m_rel_bias, m_attn_pre_norm, m_w_in, m_b_gate, m_sinks, m_w_br_a, m_w_br_b, m_w_br_c, m_w_out, m_attn_post_norm, m_ffn_pre_norm, m_w_up, m_conv_w, m_conv_b, m_w_down, m_ffn_post_norm, v_rel_bias, v_attn_pre_norm, v_w_in, v_b_gate, v_sinks, v_w_br_a, v_w_br_b, v_w_br_c, v_w_out, v_attn_post_norm, v_ffn_pre_norm, v_w_up, v_conv_w, v_conv_b, v_w_down, v_ffn_post_norm):
    names = ("rel_bias", "attn_pre_norm", "w_in", "b_gate", "sinks", "w_br_a", "w_br_b", "w_br_c", "w_out",
             "attn_post_norm", "ffn_pre_norm", "w_up", "conv_w", "conv_b", "w_down", "ffn_post_norm")
    weights = dict(zip(names, (rel_bias, attn_pre_norm, w_in, b_gate, sinks, w_br_a, w_br_b, w_br_c, w_out,
                               attn_post_norm, ffn_pre_norm, w_up, conv_w, conv_b, w_down, ffn_post_norm)))
    mom1 = dict(zip(names, (m_rel_bias, m_attn_pre_norm, m_w_in, m_b_gate, m_sinks, m_w_br_a, m_w_br_b, m_w_br_c,
                            m_w_out, m_attn_post_norm, m_ffn_pre_norm, m_w_up, m_conv_w, m_conv_b, m_w_down,
                            m_ffn_post_norm)))
    mom2 = dict(zip(names, (v_rel_bias, v_attn_pre_norm, v_w_in, v_b_gate, v_sinks, v_w_br_a, v_w_br_b, v_w_br_c,
                            v_w_out, v_attn_post_norm, v_ffn_pre_norm, v_w_up, v_conv_w, v_conv_b, v_w_down,
                            v_ffn_post_norm)))

    chip = 2 * lax.axis_index("x") + lax.axis_index("y")
    core = lax.axis_index("c")

    keys = [(n, l) for l in range(DEPTH) for group in WEIGHT_GROUPS for n in group]
    groups = [[keys.index((n, l)) for n in group] for l in range(DEPTH) for group in WEIGHT_GROUPS]

    def slot_buffer(n, l):
        if n in BIG:
            return _cast_into_slot("cast_" + n, _shard_view(n, weights[n]), l, chip)
        shard = weights[n][l]
        return lax.dynamic_update_slice(jnp.zeros((N_CHIPS,) + shard.shape, F32), shard[None],
                                        (chip, jnp.int32(0), jnp.int32(0)))

    sems, in_flight, _ = _gather_start("gather_start", [slot_buffer(*k) for k in keys], groups)

    def wget(l, gi, after):
        g = l * len(WEIGHT_GROUPS) + gi
        got = _gather_wait("gather_wait_%d_%d" % (l, gi), [in_flight[a] for a in groups[g]], *sems[g], after)
        out = {}
        for n, buf in zip(WEIGHT_GROUPS[gi], got):
            out[n] = buf.reshape(-1, buf.shape[-1]) if n in ("w_in", "w_out", "w_down") else _full_cols(buf)
        if gi == len(WEIGHT_GROUPS) - 1:
            out["conv_b"] = conv_b[l:l + 1]
        return out

    pending = []

    def emit(l, gi, grads):
        group = GRAD_GROUPS[gi]
        *started, token = _reduce_start("reduce_start_%d_%d" % (l, gi), [grads[n] for n in group])
        pending.append((l, group) + tuple(started))
        return token[:1, :1]

    local = _local_step(x.reshape(S, D), loss_target.reshape(S, D), wget, emit, rel_bias, sinks, attn_pre_norm,
                        attn_post_norm, ffn_pre_norm, ffn_post_norm)
    return _reduce_and_update(x.shape, names, weights, mom1, mom2, chip, core, pending, *local)


def _local_step(xs, target, wget, emit, rel_bias, sinks, attn_pre_norm, attn_post_norm, ffn_pre_norm, ffn_post_norm):
    bidx = jnp.asarray(_bucket_maps())

    saved, layers = [], []
    h1 = _rms_fwd("pre_norm_first", xs, attn_pre_norm[0:1])
    x_in = xs
    for l in range(DEPTH):
        mo, sv_mix, w = _mixer_fwd(h1, functools.partial(wget, l), rel_bias, sinks[l], bidx)
        x_mid, h2 = _post_pre_fwd("post_attn_norm", x_in, mo, attn_post_norm[l:l + 1], ffn_pre_norm[l:l + 1])
        w.update(wget(l, 2, h2))
        dn, sv_ffn = _ffn_fwd(h2, w)
        g_next = attn_pre_norm[l + 1:l + 2] if l + 1 < DEPTH else None
        x_out, h1_next = _post_pre_fwd("post_ffn_norm" if l + 1 < DEPTH else "post_ffn_norm_last", x_mid, dn,
                                       ffn_post_norm[l:l + 1], g_next)
        saved.append(dict(x_in=x_in, h1=h1, mo=mo, x_mid=x_mid, h2=h2, dn=dn, mix=sv_mix, ffn=sv_ffn))
        layers.append(w)
        x_in, h1 = x_out, h1_next

    loss_row, dres = _loss_kernel(x_in, target)

    small = [None] * DEPTH
    stats = jnp.zeros((N_BAND_Q, 8, 128), F32)
    dh_next = None
    for l in reversed(range(DEPTH)):
        w, sv = layers[l], saved[l]
        if l + 1 < DEPTH:
            pre = (saved[l + 1]["x_in"], attn_pre_norm[l + 1:l + 2] + zero, dh_next)
            dres, d_dn, dg_pre_next, dg_fpost = _norm_bwd("post_ffn_norm_bwd", dres, pre,
                                                          (sv["dn"], ffn_post_norm[l:l + 1]))
            small[l + 1]["attn_pre_norm"] = dg_pre_next
        else:
            dres, d_dn, _, dg_fpost = _norm_bwd("post_ffn_norm_last_bwd", dres, None, (sv["dn"], ffn_post_norm[l:l + 1]))
        dh2, g_ffn = _ffn_bwd(d_dn, sv["h2"], w, sv["ffn"])
        zero = emit(l, 0, g_ffn)
        dres, d_mo, dg_fpre, dg_apost = _norm_bwd("post_attn_norm_bwd", dres,
                                                  (sv["x_mid"], ffn_pre_norm[l:l + 1] + zero, dh2),
                                                  (sv["mo"], attn_post_norm[l:l + 1]))
        dh_next, g_mix, stats, zero = _mixer_bwd(d_mo, sv["h1"], w, sv["mix"], rel_bias, sinks[l], bidx, stats,
                                                 functools.partial(emit, l))
        small[l] = dict(ffn_post_norm=dg_fpost, ffn_pre_norm=dg_fpre, attn_post_norm=dg_apost,
                        sinks=stats[N_A:, 1, 0], conv_b=g_ffn["conv_b"], b_gate=g_mix["b_gate"], conv_w=g_ffn["conv_w"])
    grad_x, _, dg_pre0, _ = _norm_bwd("pre_norm_first_bwd", dres, (saved[0]["x_in"], attn_pre_norm[0:1] + zero, dh_next),
                                      None)
    small[0]["attn_pre_norm"] = dg_pre0
    return loss_row, grad_x, small, stats


def _reduce_and_update(x_shape, names, weights, mom1, mom2, chip, core, pending, loss_row, grad_x, small, stats):
    delta, new_m, new_v, grads = {}, {}, {}, {}

    def update(n, g):
        grads[n] = g
        delta[n], new_m[n], new_v[n] = _adamw("adamw_" + n, _shard_view(n, weights[n]), g,
                                              _shard_view(n, mom1[n]), _shard_view(n, mom2[n]))

    summed = {}

    def finish(which, after):
        for l, group, send, recv, gs, lands in pending:
            if (group == ("w_in",)) == which:
                gs, lands = _reduce_wait("reduce_wait_%d_%s" % (l, group[0]), send, recv, gs, lands, after)
                for n, g, land in zip(group, gs, lands):
                    summed[n] = _reduce_sum("reduce_sum_%d_%s" % (l, n), g, land, l, summed.get(n), chip, core)

    finish(False, grad_x)
    early = [n for n in BIG if n != "w_in"]
    for n, g in zip(early, _join_halves("join_halves", [summed[n] for n in early])):
        update(n, g)
    finish(True, delta[early[-1]])
    update("w_in", _join_halves("join_halves_w_in", [summed["w_in"]])[0])
    for out in (grads, delta, new_m, new_v):
        out["w_in"] = _shard_view("w_in", out["w_in"])

    small_vals = {n: jnp.stack([small[l][n].reshape(weights[n].shape[1:]) for l in range(DEPTH)])
                  for n in ("attn_pre_norm", "attn_post_norm", "ffn_pre_norm", "ffn_post_norm", "conv_b", "sinks")}
    small_vals["b_gate"] = jnp.stack([small[l]["b_gate"] for l in range(DEPTH)])
    small_vals["conv_w"] = jnp.stack([small[l]["conv_w"] for l in range(DEPTH)])
    small_vals["rel_bias"] = stats[:, 0, :NUM_BUCKETS].T
    small_vals["loss"] = loss_row[0, :1]
    shapes = {n: v.shape for n, v in small_vals.items()}
    packed, delta["w_in"] = lax.optimization_barrier((_pack_small(small_vals), delta["w_in"]))
    reduced = _unpack_small(_all_reduce_small(packed), shapes)
    reduced["b_gate"] = lax.dynamic_slice_in_dim(reduced["b_gate"], chip * (D // N_CHIPS), D // N_CHIPS, axis=2)
    reduced["conv_w"] = lax.dynamic_slice_in_dim(reduced["conv_w"], chip * (2 * D_FF // N_CHIPS), 2 * D_FF // N_CHIPS, axis=2)
    for n in names:
        if n not in grads:
            update(n, reduced[n].reshape(weights[n].shape))

    loss = reduced["loss"].reshape(())
    return (loss, grad_x.reshape(x_shape), *[grads[n] for n in names], *[delta[n] for n in names],
            *[new_m[n] for n in names], *[new_v[n] for n in names])
```

```python
import functools
import math

import numpy as np
import jax
import jax.numpy as jnp
from jax import lax
from jax.experimental import pallas as pl
from jax.experimental.pallas import tpu as pltpu

F32 = jnp.float32
BF16 = jnp.bfloat16

S = 2048
D = 1024
DEPTH = 2
HD = 64
BLK = 128
NQB = S // BLK
A_GROUPS = ((128, 1), (512, 4), (2048, 16))
N_BAND_Q = 20
N_A = 12
NUM_BUCKETS = 32
MAX_DISTANCE = 2048
D_FF = 4096
IN_COLS = 6912
IN_SHARD = IN_COLS // 4
OFF_GATE = 3840
EPS = 1e-6
SCALE = HD ** -0.5
NEG = -1e30
N_CHIPS = 4
N_DEV = 8

ADAM_LR = 0.001
ADAM_B1 = 0.9
ADAM_B2 = 0.999
ADAM_EPS = 1e-08
ADAM_WD = 0.01
ADAM_STEP = 10

VMEM_LIMIT = 56 * 1024 * 1024

NN = (((1,), (0,)), ((), ()))
NT = (((1,), (1,)), ((), ()))
TN = (((0,), (0,)), ((), ()))

MESH = pl.DeviceIdType.MESH
ANY = pl.BlockSpec(memory_space=pl.ANY)


def _dot(a, b, dims):
    return lax.dot_general(a, b, dims, preferred_element_type=F32)


def _params(sem):
    return pltpu.CompilerParams(dimension_semantics=sem, vmem_limit_bytes=VMEM_LIMIT)


def _matmul(name, a, b, out_shape, out_dtype, grid, a_spec, b_spec, o_spec, dims, acc_shape):
    nk = grid[-1]

    def body(a_ref, b_ref, o_ref, *scratch):
        part = _dot(a_ref[...].astype(BF16), b_ref[...].astype(BF16), dims)
        if nk == 1:
            o_ref[...] = part.astype(o_ref.dtype)
            return
        acc_ref, = scratch
        k = pl.program_id(len(grid) - 1)

        @pl.when(k == 0)
        def _():
            acc_ref[...] = part

        @pl.when(k > 0)
        def _():
            acc_ref[...] += part

        @pl.when(k == nk - 1)
        def _():
            o_ref[...] = acc_ref[...].astype(o_ref.dtype)

    scratch = [] if nk == 1 else [pltpu.VMEM(acc_shape, F32)]
    sem = ("parallel",) * (len(grid) - 1) + ("arbitrary",)
    return pl.pallas_call(
        body, name=name, grid=grid, in_specs=[a_spec, b_spec], out_specs=o_spec,
        out_shape=jax.ShapeDtypeStruct(out_shape, out_dtype), scratch_shapes=scratch,
        compiler_params=_params(sem))(a, b)


FULL_K = 8192


def _mm_tn_sharded(name, a, b, row_sharded, tm=512, tn=512, tk=FULL_K):
    k, m = a.shape
    n = b.shape[1]
    m4, n4 = (m // N_CHIPS, n) if row_sharded else (m, n // N_CHIPS)
    tm, tn, tk = min(tm, m4), min(tn, n4), min(tk, k)
    per_m, per_n = m4 // tm, n4 // tn
    if row_sharded:
        o_map = lambda i, j, l: (i // per_m, i % per_m, j)
    else:
        o_map = lambda i, j, l: (j // per_n, i, j % per_n)
    return _matmul(name, a, b, (N_CHIPS, m4, n4), BF16, (m // tm, n // tn, k // tk),
                   pl.BlockSpec((tk, tm), lambda i, j, l: (l, i)),
                   pl.BlockSpec((tk, tn), lambda i, j, l: (l, j)),
                   pl.BlockSpec((None, tm, tn), o_map), TN, (tm, tn))


def _mm_nn(name, a, b, out_dtype, tm=512, tn=512, tk=FULL_K):
    m, k = a.shape
    n = b.shape[1]
    tm, tn, tk = min(tm, m), min(tn, n), min(tk, k)
    return _matmul(name, a, b, (m, n), out_dtype, (m // tm, n // tn, k // tk),
                   pl.BlockSpec((tm, tk), lambda i, j, l: (i, l)),
                   pl.BlockSpec((tk, tn), lambda i, j, l: (l, j)),
                   pl.BlockSpec((tm, tn), lambda i, j, l: (i, j)), NN, (tm, tn))


def _mm_nt(name, a, b, out_dtype, tm=512, tn=512, tk=FULL_K):
    m, k = a.shape
    n = b.shape[0]
    tm, tn, tk = min(tm, m), min(tn, n), min(tk, k)
    return _matmul(name, a, b, (m, n), out_dtype, (m // tm, n // tn, k // tk),
                   pl.BlockSpec((tm, tk), lambda i, j, l: (i, l)),
                   pl.BlockSpec((tn, tk), lambda i, j, l: (j, l)),
                   pl.BlockSpec((tm, tn), lambda i, j, l: (i, j)), NT, (tm, tn))


def _mm_tn(name, a, b, out_dtype, tm=512, tn=512, tk=FULL_K):
    k, m = a.shape
    n = b.shape[1]
    tm, tn, tk = min(tm, m), min(tn, n), min(tk, k)
    return _matmul(name, a, b, (m, n), out_dtype, (m // tm, n // tn, k // tk),
                   pl.BlockSpec((tk, tm), lambda i, j, l: (l, i)),
                   pl.BlockSpec((tk, tn), lambda i, j, l: (l, j)),
                   pl.BlockSpec((tm, tn), lambda i, j, l: (i, j)), TN, (tm, tn))


TR = 256


def _row_spec(width=D):
    return pl.BlockSpec((TR, width), lambda i: (i, 0))


def _vec_spec(width=D):
    return pl.BlockSpec((1, width), lambda i: (0, 0))


def _rms(x, g):
    r = lax.rsqrt(jnp.mean(x * x, axis=-1, keepdims=True) + EPS)
    return x * r * g


def _rms_fwd(name, x, g):
    def body(x_ref, g_ref, h_ref):
        h_ref[...] = _rms(x_ref[...], g_ref[...]).astype(BF16)

    return pl.pallas_call(
        body, name=name, grid=(S // TR,), in_specs=[_row_spec(), _vec_spec()], out_specs=_row_spec(),
        out_shape=jax.ShapeDtypeStruct((S, D), BF16), compiler_params=_params(("parallel",)))(x, g)


def _post_pre_fwd(name, x, y, g_post, g_pre):
    has_pre = g_pre is not None

    def body(*refs):
        if has_pre:
            x_ref, y_ref, gp_ref, gn_ref, xn_ref, h_ref = refs
        else:
            x_ref, y_ref, gp_ref, xn_ref = refs
        xn = x_ref[...] + _rms(y_ref[...], gp_ref[...])
        xn_ref[...] = xn
        if has_pre:
            h_ref[...] = _rms(xn, gn_ref[...]).astype(BF16)

    ins = [x, y, g_post] + ([g_pre] if has_pre else [])
    in_specs = [_row_spec(), _row_spec(), _vec_spec()] + ([_vec_spec()] if has_pre else [])
    out_shape = [jax.ShapeDtypeStruct((S, D), F32)] + ([jax.ShapeDtypeStruct((S, D), BF16)] if has_pre else [])
    out_specs = [_row_spec()] + ([_row_spec()] if has_pre else [])
    out = pl.pallas_call(
        body, name=name, grid=(S // TR,), in_specs=in_specs, out_specs=out_specs, out_shape=out_shape,
        compiler_params=_params(("parallel",)))(*ins)
    return out if has_pre else (out[0], None)


def _rms_bwd_math(x, g, dy):
    r = lax.rsqrt(jnp.mean(x * x, axis=-1, keepdims=True) + EPS)
    n = x * r
    dn = dy * g
    dx = r * (dn - n * jnp.mean(dn * n, axis=-1, keepdims=True))
    return dx, jnp.sum(dy * n, axis=0, keepdims=True)


def _norm_bwd(name, dres, pre=None, post=None):
    has_pre, has_post = pre is not None, post is not None

    def body(*refs):
        refs = list(refs)
        dres_ref = refs.pop(0)
        if has_pre:
            xn_ref, gn_ref, dh_ref = refs[:3]
            refs = refs[3:]
        if has_post:
            y_ref, gp_ref = refs[:2]
            refs = refs[2:]
        dxn_ref = refs.pop(0)
        dy_ref = refs.pop(0) if has_post else None
        dgn_ref = refs.pop(0) if has_pre else None
        dgp_ref = refs.pop(0) if has_post else None
        first = pl.program_id(0) == 0
        dxn = dres_ref[...]
        if has_pre:
            dx, dg = _rms_bwd_math(xn_ref[...], gn_ref[...], dh_ref[...])
            dxn = dxn + dx

            @pl.when(first)
            def _():
                dgn_ref[...] = dg

            @pl.when(jnp.logical_not(first))
            def _():
                dgn_ref[...] += dg
        dxn_ref[...] = dxn
        if has_post:
            dy, dg = _rms_bwd_math(y_ref[...], gp_ref[...], dxn)
            dy_ref[...] = dy.astype(BF16)

            @pl.when(first)
            def _():
                dgp_ref[...] = dg

            @pl.when(jnp.logical_not(first))
            def _():
                dgp_ref[...] += dg

    ins, in_specs = [dres], [_row_spec()]
    if has_pre:
        ins += list(pre)
        in_specs += [_row_spec(), _vec_spec(), _row_spec()]
    if has_post:
        ins += list(post)
        in_specs += [_row_spec(), _vec_spec()]
    out_shape, out_specs = [jax.ShapeDtypeStruct((S, D), F32)], [_row_spec()]
    if has_post:
        out_shape.append(jax.ShapeDtypeStruct((S, D), BF16))
        out_specs.append(_row_spec())
    for _ in range(int(has_pre) + int(has_post)):
        out_shape.append(jax.ShapeDtypeStruct((1, D), F32))
        out_specs.append(_vec_spec())
    out = list(pl.pallas_call(
        body, name=name, grid=(S // TR,), in_specs=in_specs, out_specs=out_specs, out_shape=out_shape,
        compiler_params=_params(("arbitrary",)))(*ins))
    dxn = out.pop(0)
    dy = out.pop(0) if has_post else None
    dgn = out.pop(0) if has_pre else None
    dgp = out.pop(0) if has_post else None
    return dxn, dy, dgn, dgp


def _loss_kernel(y, target):
    def body(y_ref, t_ref, loss_ref, dy_ref):
        e = y_ref[...] - t_ref[...]
        dy_ref[...] = e * (1.0 / D)
        part = jnp.zeros((1, 128), F32) + 0.5 * jnp.sum(jnp.mean(e * e, axis=-1, keepdims=True))

        @pl.when(pl.program_id(0) == 0)
        def _():
            loss_ref[...] = part

        @pl.when(pl.program_id(0) > 0)
        def _():
            loss_ref[...] += part

    return pl.pallas_call(
        body, name="loss", grid=(S // TR,), in_specs=[_row_spec(), _row_spec()],
        out_specs=[_vec_spec(128), _row_spec()],
        out_shape=[jax.ShapeDtypeStruct((1, 128), F32), jax.ShapeDtypeStruct((S, D), F32)],
        compiler_params=_params(("arbitrary",)))(y, target)


def _t5_bucket_np(dist):
    max_exact = NUM_BUCKETS // 2
    nf = np.maximum(dist, 1).astype(np.float32)
    large = max_exact + (np.log(nf / max_exact) / np.float32(math.log(MAX_DISTANCE / max_exact))
                         * (NUM_BUCKETS - max_exact)).astype(np.int32)
    large = np.minimum(large, NUM_BUCKETS - 1)
    return np.where(dist < max_exact, dist, large).astype(np.int32)


def _bucket_maps():
    a = np.arange(BLK)[:, None]
    b = np.arange(2 * BLK)[None, :]
    dist = np.maximum(a + BLK - b, 0)
    maps = [_t5_bucket_np(dist * d) for _, d in A_GROUPS] + [_t5_bucket_np(dist)]
    return np.stack(maps).astype(np.int32)


def _classes(arr, d):
    return arr.reshape(S // d, d * arr.shape[1])


def _class_spec(arr, col0, d):
    ncol = arr.shape[1] // 128
    return pl.BlockSpec((S // d, 128), lambda p, r: (0, r * ncol + col0 + p))


def _band_rows(b):
    return (pl.ds(pl.multiple_of(b * BLK, BLK), BLK), pl.ds(pl.multiple_of(jnp.maximum(b - 1, 0) * BLK, BLK), BLK))


def _band_bias(tab_ref, bidx_ref, h):
    bi = bidx_ref[...]
    bias = jnp.zeros((BLK, 2 * BLK), F32)
    for kk in range(NUM_BUCKETS):
        bias = jnp.where(bi == kk, tab_ref[kk, h], bias)
    return bias


def _lane_lo(rows=BLK):
    return lax.broadcasted_iota(jnp.int32, (rows, 128), 1) < HD


def _per_head(x, lo):
    return (jnp.sum(jnp.where(lo, x, 0.0), axis=1, keepdims=True) * (1.0 / HD),
            jnp.sum(jnp.where(lo, 0.0, x), axis=1, keepdims=True) * (1.0 / HD))


def _band_fill(bias_ref, tab_ref, bidx_ref, head, maxd):
    a = lax.broadcasted_iota(jnp.int32, (BLK, 2 * BLK), 0)
    c = lax.broadcasted_iota(jnp.int32, (BLK, 2 * BLK), 1)
    dist = a + BLK - c
    in_band = jnp.logical_and(dist >= 0, dist <= maxd)
    for h in range(2):
        bias = jnp.where(in_band, _band_bias(tab_ref, bidx_ref, head + h), NEG)
        bias_ref[1, h * BLK:(h + 1) * BLK, :] = bias
        bias_ref[0, h * BLK:(h + 1) * BLK, :] = jnp.where(c >= BLK, bias, NEG)


def _stack_heads(x, lo, dtype=BF16):
    return jnp.concatenate([jnp.where(lo, x, 0.0), jnp.where(lo, 0.0, x)], axis=0).astype(dtype)


def _unstack_heads(x, lo):
    n = x.shape[0] // 2
    return jnp.where(lo, x[:n], x[n:])


def _stack_rows(ref, prev, cur):
    return jnp.concatenate([ref[prev, :], ref[cur, :]], axis=0).astype(BF16)


def _band_fwd(name, d, n_pairs, maxd, head0, srcs, bidx_g, tab, sinks):
    nb = S // d // BLK
    (qa, qc), (ka, kc), (va, vc) = srcs
    out_spec = pl.BlockSpec((S // d, 128), lambda p, r: (0, r * n_pairs + p))
    smem = pl.BlockSpec(memory_space=pltpu.SMEM)
    full = pl.BlockSpec((BLK, 2 * BLK), lambda p, r: (0, 0))

    def body(tab_ref, sink_ref, q_ref, k_ref, v_ref, bidx_ref, o_ref, lse_ref, bias_ref):
        p, r = pl.program_id(0), pl.program_id(1)

        @pl.when(r == 0)
        def _():
            _band_fill(bias_ref, tab_ref, bidx_ref, head0 + 2 * p, maxd)

        lo = _lane_lo()
        sink = jnp.where(lax.broadcasted_iota(jnp.int32, (2 * BLK, 1), 0) < BLK, sink_ref[2 * p], sink_ref[2 * p + 1])

        def block(b, carry):
            cur, prev = _band_rows(b)
            qs = _stack_heads(q_ref[cur, :] * SCALE, lo)
            ks, vs = _stack_rows(k_ref, prev, cur), _stack_rows(v_ref, prev, cur)
            s = _dot(qs, ks, NT) + bias_ref[jnp.minimum(b, 1)]
            m = jnp.max(s, axis=1, keepdims=True)
            pr = jnp.exp(s - m)
            l = jnp.sum(pr, axis=1, keepdims=True)
            num = _dot(pr.astype(BF16), vs, NN)
            lse = m + jnp.log(l)
            sig = 1.0 / (1.0 + jnp.exp(sink - lse))
            o_ref[cur, :] = _unstack_heads(num * (sig / l), lo)
            lse_ref[cur, :] = _unstack_heads(lse + jnp.zeros((2 * BLK, 128), F32), lo)
            return carry

        lax.fori_loop(0, nb, block, 0, unroll=min(nb, 2))

    shape = jax.ShapeDtypeStruct((S // d, d * n_pairs * 128), F32)
    o, lse = pl.pallas_call(
        body, name=name, grid=(n_pairs, d),
        in_specs=[smem, smem, _class_spec(qa, qc, d), _class_spec(ka, kc, d), _class_spec(va, vc, d), full],
        out_specs=[out_spec, out_spec], out_shape=[shape, shape],
        scratch_shapes=[pltpu.VMEM((2, 2 * BLK, 2 * BLK), F32)],
        compiler_params=_params(("parallel", "arbitrary")))(
            tab, sinks, _classes(qa, d), _classes(ka, d), _classes(va, d), bidx_g)
    return o.reshape(S, n_pairs * 128), lse.reshape(S, n_pairs * 128)


def _band_bwd(name, d, n_pairs, maxd, head0, srcs, bidx_g, tab, sinks, o, lse, do, stats_in):
    nb = S // d // BLK
    rows = S // d
    (qa, qc), (ka, kc), (va, vc) = srcs
    cls_spec = pl.BlockSpec((rows, 128), lambda p, r: (0, r * n_pairs + p))
    smem = pl.BlockSpec(memory_space=pltpu.SMEM)
    full = pl.BlockSpec((BLK, 2 * BLK), lambda p, r: (0, 0))
    stat_spec = pl.BlockSpec((2, 8, 128), lambda p, r: (p, 0, 0))

    def body(tab_ref, sink_ref, q_ref, k_ref, v_ref, bidx_ref, o_ref, lse_ref, do_ref, sin_ref,
             dq_ref, dk_ref, dv_ref, stat_ref, bias_ref, dsacc_ref, sk_ref):
        p, r = pl.program_id(0), pl.program_id(1)

        @pl.when(r == 0)
        def _():
            _band_fill(bias_ref, tab_ref, bidx_ref, head0 + 2 * p, maxd)
            dsacc_ref[...] = jnp.zeros_like(dsacc_ref)
            sk_ref[...] = jnp.zeros_like(sk_ref)

        dk_ref[...] = jnp.zeros_like(dk_ref)
        dv_ref[...] = jnp.zeros_like(dv_ref)
        lo = _lane_lo()
        head1 = lax.broadcasted_iota(jnp.int32, (2 * BLK, 1), 0) >= BLK
        sink = jnp.where(head1, sink_ref[2 * p + 1], sink_ref[2 * p])

        def block(b, carry):
            cur, prev = _band_rows(b)
            qs = _stack_heads(q_ref[cur, :] * SCALE, lo)
            ks, vs = _stack_rows(k_ref, prev, cur), _stack_rows(v_ref, prev, cur)
            do = do_ref[cur, :]
            dos = _stack_heads(do, lo, F32)
            lse = jnp.concatenate(_per_head(lse_ref[cur, :], lo), axis=0)
            prod = do * o_ref[cur, :]
            delta = jnp.concatenate([jnp.sum(jnp.where(lo, prod, 0.0), axis=1, keepdims=True),
                                     jnp.sum(jnp.where(lo, 0.0, prod), axis=1, keepdims=True)], axis=0)
            sig = 1.0 / (1.0 + jnp.exp(sink - lse))
            pr = jnp.exp(_dot(qs, ks, NT) + bias_ref[jnp.minimum(b, 1)] - lse)
            ds = pr * (sig * (_dot(dos.astype(BF16), vs, NT) - delta))
            dsb = ds.astype(BF16)
            dq_ref[cur, :] = SCALE * _unstack_heads(_dot(dsb, ks, NN), lo)
            dk = _dot(dsb, qs, TN)
            dv = _dot(pr.astype(BF16), (sig * dos).astype(BF16), TN)
            dk_ref[prev, :] += dk[:BLK]
            dk_ref[cur, :] += dk[BLK:]
            dv_ref[prev, :] += dv[:BLK]
            dv_ref[cur, :] += dv[BLK:]
            dsacc_ref[...] += ds
            sink_grad = -delta * (1.0 - sig)
            for h in range(2):
                sk_ref[h] += jnp.zeros((8, 128), F32) + jnp.sum(sink_grad[h * BLK:(h + 1) * BLK])
            return carry

        lax.fori_loop(0, nb, block, 0, unroll=min(nb, 2))

        @pl.when(r == d - 1)
        def _():
            bi = bidx_ref[...]
            lane = lax.broadcasted_iota(jnp.int32, (8, 128), 1)
            sub = lax.broadcasted_iota(jnp.int32, (8, 128), 0)
            for h in range(2):
                acc = dsacc_ref[h * BLK:(h + 1) * BLK, :]
                row = jnp.where(jnp.logical_and(sub == 1, lane == 0), sk_ref[h], 0.0)
                for kk in range(NUM_BUCKETS):
                    tot = jnp.sum(jnp.where(bi == kk, acc, 0.0))
                    row = jnp.where(jnp.logical_and(sub == 0, lane == kk), tot, row)
                stat_ref[h] = row + jnp.where(sub == 0, sin_ref[h], 0.0)

    shape = jax.ShapeDtypeStruct((rows, d * n_pairs * 128), F32)
    dq, dk, dv, stats = pl.pallas_call(
        body, name=name, grid=(n_pairs, d),
        in_specs=[smem, smem, _class_spec(qa, qc, d), _class_spec(ka, kc, d), _class_spec(va, vc, d), full,
                  cls_spec, cls_spec, cls_spec, stat_spec],
        out_specs=[cls_spec, cls_spec, cls_spec, stat_spec],
        out_shape=[shape, shape, shape, jax.ShapeDtypeStruct((2 * n_pairs, 8, 128), F32)],
        scratch_shapes=[pltpu.VMEM((2, 2 * BLK, 2 * BLK), F32), pltpu.VMEM((2 * BLK, 2 * BLK), F32),
                        pltpu.VMEM((2, 8, 128), F32)],
        compiler_params=_params(("arbitrary", "arbitrary")))(
            tab, sinks, _classes(qa, d), _classes(ka, d), _classes(va, d), bidx_g,
            _classes(o, d), _classes(lse, d), _classes(do, d), stats_in)
    width = n_pairs * 128
    return dq.reshape(S, width), dk.reshape(S, width), dv.reshape(S, width), stats


def _comb_fwd(o_g, lse_g):
    def body(o0, o1, o2, l0, l1, l2, out_ref, outb_ref, lse_ref):
        a0, a1, a2 = l0[...], l1[...], l2[...]
        m = jnp.maximum(jnp.maximum(a0, a1), a2)
        e0, e1, e2 = jnp.exp(a0 - m), jnp.exp(a1 - m), jnp.exp(a2 - m)
        tot = e0 + e1 + e2
        out = (e0 * o0[...] + e1 * o1[...] + e2 * o2[...]) / tot
        out_ref[...] = out
        outb_ref[...] = out.astype(BF16)
        lse_ref[...] = m + jnp.log(tot)

    spec = _row_spec(4 * HD)
    f32 = jax.ShapeDtypeStruct((S, 4 * HD), F32)
    return pl.pallas_call(
        body, name="comb_fwd", grid=(S // TR,), in_specs=[spec] * 6, out_specs=[spec] * 3,
        out_shape=[f32, jax.ShapeDtypeStruct((S, 4 * HD), BF16), f32],
        compiler_params=_params(("parallel",)))(*o_g, *lse_g)


def _split2(x):
    hi = x.astype(BF16)
    return hi, (x - hi.astype(F32)).astype(BF16)


KB = 2 * BLK
SBQ = 2 * BLK


def _tri_sum(x, tri):
    hi, lo = _split2(x)
    both = _dot(jnp.concatenate([hi, lo], axis=0), tri, NN)
    return both[:x.shape[0]] + both[x.shape[0]:]


def _tri(strict_upper):
    r = lax.broadcasted_iota(jnp.int32, (KB, KB), 0)
    c = lax.broadcasted_iota(jnp.int32, (KB, KB), 1)
    return jnp.where(r > c if strict_upper else r < c, 1.0, 0.0).astype(BF16)


def _sb_terms(qs, kj, before):
    z = _dot(qs, kj, NT)
    lsp = jnp.minimum(z, 0.0) - jnp.log(1.0 + jnp.exp(-jnp.abs(z)))
    return lsp, jnp.where(before, lsp - z, 0.0)


def _sb_before(i, m):
    t = (lax.broadcasted_iota(jnp.int32, (2 * SBQ, KB), 0) & (SBQ - 1)) + i * SBQ
    s = lax.broadcasted_iota(jnp.int32, (2 * SBQ, KB), 1) + m * KB
    return s < t


C_COL = 3072 // 128


def _sb_fwd(proj):
    blk = lambda off: pl.BlockSpec((SBQ, 128), lambda p, i: (i, off + p))
    col = lambda off: pl.BlockSpec((S, 128), lambda p, i: (0, off + p))
    out = pl.BlockSpec((SBQ, 128), lambda p, i: (i, p))

    def body(q_ref, k_ref, v_ref, o_ref, ob_ref, tot_ref):
        i = pl.program_id(1)
        lo = _lane_lo(SBQ)
        qs = _stack_heads(q_ref[...] * SCALE, lo)
        suffix = _tri(True)

        def step(n, carry):
            acc, rest = carry
            m = i - n
            rows = pl.ds(pl.multiple_of(m * KB, KB), KB)
            kj, vj = k_ref[rows, :].astype(BF16), v_ref[rows, :].astype(BF16)
            before = _sb_before(i, m)
            lsp, lk = _sb_terms(qs, kj, before)
            w = jnp.where(before, jnp.exp(lsp + _tri_sum(lk, suffix) + rest), 0.0)
            return acc + _dot(w.astype(BF16), vj, NN), rest + jnp.sum(lk, axis=1, keepdims=True)

        acc, rest = lax.fori_loop(0, i + 1, step, (jnp.zeros((2 * SBQ, 128), F32), jnp.zeros((2 * SBQ, 1), F32)))
        o = _unstack_heads(acc, lo)
        o_ref[...] = o
        ob_ref[...] = o.astype(BF16)
        tot_ref[...] = _unstack_heads(rest + jnp.zeros((2 * SBQ, 128), F32), lo)

    f32 = jax.ShapeDtypeStruct((S, 4 * HD), F32)
    return pl.pallas_call(
        body, name="sb_fwd", grid=(2, S // SBQ), in_specs=[blk(C_COL), col(C_COL + 2), col(C_COL + 4)],
        out_specs=[out, out, out], out_shape=[f32, jax.ShapeDtypeStruct((S, 4 * HD), BF16), f32],
        compiler_params=_params(("parallel", "arbitrary")))(proj, proj, proj)


def _sb_bwd(proj, tot, do):
    blk = lambda off: pl.BlockSpec((SBQ, 128), lambda p, i: (i, off + p))
    col = lambda off: pl.BlockSpec((S, 128), lambda p, i: (0, off + p))

    def body(q_ref, k_ref, v_ref, tot_ref, do_ref, dq_ref, dk_ref, dv_ref):
        i = pl.program_id(1)

        @pl.when(i == 0)
        def _():
            dk_ref[...] = jnp.zeros_like(dk_ref)
            dv_ref[...] = jnp.zeros_like(dv_ref)

        lo = _lane_lo(SBQ)
        qs = _stack_heads(q_ref[...] * SCALE, lo)
        dos = _stack_heads(do_ref[...], lo)
        tots = jnp.concatenate(_per_head(tot_ref[...], lo), axis=0)
        prefix = _tri(False)

        def step(m, carry):
            dq, keep_left, g_left = carry
            rows = pl.ds(pl.multiple_of(m * KB, KB), KB)
            kj, vj = k_ref[rows, :].astype(BF16), v_ref[rows, :].astype(BF16)
            before = _sb_before(i, m)
            lsp, lk = _sb_terms(qs, kj, before)
            log_rest = tots - keep_left - lk - _tri_sum(lk, prefix)
            w = jnp.where(before, jnp.exp(lsp + log_rest), 0.0)
            g = w * _dot(dos, vj, NT)
            g_before = g_left + _dot(g.astype(BF16), prefix, NN)
            beta = jnp.exp(lsp)
            dz = jnp.where(before, g * (1.0 - beta) - g_before * beta, 0.0).astype(BF16)
            dk_ref[rows, :] += _dot(dz, qs, TN)
            dv_ref[rows, :] += _dot(w.astype(BF16), dos, TN)
            return (dq + _dot(dz, kj, NN), keep_left + jnp.sum(lk, axis=1, keepdims=True),
                    g_left + jnp.sum(g, axis=1, keepdims=True))

        zero = (jnp.zeros((2 * SBQ, 128), F32), jnp.zeros((2 * SBQ, 1), F32), jnp.zeros((2 * SBQ, 1), F32))
        dq, _, _ = lax.fori_loop(0, i + 1, step, zero)
        dq_ref[...] = SCALE * _unstack_heads(dq, lo)

    out_blk = pl.BlockSpec((SBQ, 128), lambda p, i: (i, p))
    out_col = pl.BlockSpec((S, 128), lambda p, i: (0, p))
    f32 = jax.ShapeDtypeStruct((S, 4 * HD), F32)
    return pl.pallas_call(
        body, name="sb_bwd", grid=(2, S // SBQ),
        in_specs=[blk(C_COL), col(C_COL + 2), col(C_COL + 4), out_blk, out_blk],
        out_specs=[out_blk, out_col, out_col], out_shape=[f32, f32, f32],
        compiler_params=_params(("arbitrary", "arbitrary")))(proj, proj, proj, tot, do)


TG = 256
GATE_BLK0 = OFF_GATE // TG


def _gate_specs():
    grid = (D // TG, S // TG)
    p_specs = [pl.BlockSpec((TG, TG), functools.partial(lambda c, r, br: (r, GATE_BLK0 + br * (D // TG) + c), br=br))
               for br in range(3)]
    b_spec = pl.BlockSpec((3, TG), lambda c, r: (0, c))
    t_spec = pl.BlockSpec((TG, TG), lambda c, r: (r, c))
    return grid, p_specs, b_spec, t_spec


def _sigmoid(x):
    return 1.0 / (1.0 + jnp.exp(-x))


def _three_rows(rows):
    sub = lax.broadcasted_iota(jnp.int32, (3, rows[0].shape[1]), 0)
    return jnp.where(sub == 0, rows[0], jnp.where(sub == 1, rows[1], rows[2]))


def _gate_fwd(proj, b_gate, br):
    grid, p_specs, b_spec, t_spec = _gate_specs()

    def body(p0, p1, p2, b_ref, r0, r1, r2, out_ref):
        acc = jnp.zeros((TG, TG), F32)
        for n, (p, r) in enumerate(((p0, r0), (p1, r1), (p2, r2))):
            acc += _sigmoid(p[...] + b_ref[n:n + 1, :]) * r[...]
        out_ref[...] = acc.astype(BF16)

    return pl.pallas_call(
        body, name="gate_fwd", grid=grid, in_specs=p_specs + [b_spec] + [t_spec] * 3, out_specs=t_spec,
        out_shape=jax.ShapeDtypeStruct((S, D), BF16),
        compiler_params=_params(("parallel", "parallel")))(proj, proj, proj, b_gate, *br)


def _gate_bwd(proj, b_gate, br, dmerged):
    grid, p_specs, b_spec, t_spec = _gate_specs()

    def body(p0, p1, p2, b_ref, r0, r1, r2, dm_ref, e0, e1, e2, g0, g1, g2, db_ref):
        dm = dm_ref[...]
        rows = []
        for n, (p, r, e_ref, dg_ref) in enumerate(((p0, r0, e0, g0), (p1, r1, e1, g1), (p2, r2, e2, g2))):
            g = _sigmoid(p[...] + b_ref[n:n + 1, :])
            e_ref[...] = (dm * g).astype(BF16)
            dpre = dm * r[...] * g * (1.0 - g)
            dg_ref[...] = dpre.astype(BF16)
            rows.append(jnp.sum(dpre, axis=0, keepdims=True))
        db = _three_rows(rows)

        @pl.when(pl.program_id(1) == 0)
        def _():
            db_ref[...] = db

        @pl.when(pl.program_id(1) > 0)
        def _():
            db_ref[...] += db

    bf = jax.ShapeDtypeStruct((S, D), BF16)
    out = pl.pallas_call(
        body, name="gate_bwd", grid=grid, in_specs=p_specs + [b_spec] + [t_spec] * 4,
        out_specs=[t_spec] * 6 + [b_spec], out_shape=[bf] * 6 + [jax.ShapeDtypeStruct((3, D), F32)],
        compiler_params=_params(("parallel", "arbitrary")))(proj, proj, proj, b_gate, *br, dmerged)
    return out[:3], out[3:6], out[6]


TC = 256
N_FF_BLK = D_FF // TC
GELU_C = math.sqrt(2.0 / math.pi)


def _shift_down(x, n):
    rows = lax.broadcasted_iota(jnp.int32, x.shape, 0)
    return jnp.where(rows >= n, pltpu.roll(x, n, axis=0), 0.0)


def _shift_up(x, n):
    rows = lax.broadcasted_iota(jnp.int32, x.shape, 0)
    return jnp.where(rows < x.shape[0] - n, pltpu.roll(x, x.shape[0] - n, axis=0), 0.0)


def _conv(u, w, b):
    s1, s2 = _shift_down(u, 1), _shift_down(u, 2)
    return w[2:3, :] * u + w[1:2, :] * s1 + w[0:1, :] * s2 + b, s1, s2


def _gelu_parts(x):
    inner = GELU_C * (x + 0.044715 * x * x * x)
    t = jnp.tanh(inner)
    gelu = 0.5 * x * (1.0 + t)
    dgelu = 0.5 * (1.0 + t) + 0.5 * x * (1.0 - t * t) * GELU_C * (1.0 + 3 * 0.044715 * x * x)
    return gelu, dgelu


def _conv_specs():
    ug = pl.BlockSpec((S, TC), lambda c: (0, c))
    uv = pl.BlockSpec((S, TC), lambda c: (0, N_FF_BLK + c))
    wg = pl.BlockSpec((3, TC), lambda c: (0, c))
    wv = pl.BlockSpec((3, TC), lambda c: (0, N_FF_BLK + c))
    bg = pl.BlockSpec((1, TC), lambda c: (0, c))
    bv = pl.BlockSpec((1, TC), lambda c: (0, N_FF_BLK + c))
    return ug, uv, wg, wv, bg, bv


def _conv_fwd(u, conv_w, conv_b):
    ug, uv, wg, wv, bg, bv = _conv_specs()

    def body(ug_ref, uv_ref, wg_ref, wv_ref, bg_ref, bv_ref, a_ref):
        gc = _conv(ug_ref[...], wg_ref[...], bg_ref[...])[0]
        vc = _conv(uv_ref[...], wv_ref[...], bv_ref[...])[0]
        a_ref[...] = (_gelu_parts(gc)[0] * vc).astype(BF16)

    return pl.pallas_call(
        body, name="conv_fwd", grid=(N_FF_BLK,), in_specs=[ug, uv, wg, wv, bg, bv], out_specs=ug,
        out_shape=jax.ShapeDtypeStruct((S, D_FF), BF16),
        compiler_params=_params(("parallel",)))(u, u, conv_w, conv_w, conv_b, conv_b)


def _conv_bwd(u, conv_w, conv_b, da):
    ug, uv, wg, wv, bg, bv = _conv_specs()

    def back(duc, u, s1, s2, w):
        du = w[2:3, :] * duc + w[1:2, :] * _shift_up(duc, 1) + w[0:1, :] * _shift_up(duc, 2)
        dw = _three_rows([jnp.sum(duc * s2, axis=0, keepdims=True), jnp.sum(duc * s1, axis=0, keepdims=True),
                          jnp.sum(duc * u, axis=0, keepdims=True)])
        return du, dw, jnp.sum(duc, axis=0, keepdims=True)

    def body(ug_ref, uv_ref, wg_ref, wv_ref, bg_ref, bv_ref, da_ref, dug_ref, duv_ref, dwg_ref, dwv_ref, dbg_ref, dbv_ref):
        u_g, u_v = ug_ref[...], uv_ref[...]
        gc, g1, g2 = _conv(u_g, wg_ref[...], bg_ref[...])
        vc, v1, v2 = _conv(u_v, wv_ref[...], bv_ref[...])
        gelu, dgelu = _gelu_parts(gc)
        da = da_ref[...]
        du, dw, db = back(da * vc * dgelu, u_g, g1, g2, wg_ref[...])
        dug_ref[...] = du.astype(BF16)
        dwg_ref[...] = dw
        dbg_ref[...] = db
        du, dw, db = back(da * gelu, u_v, v1, v2, wv_ref[...])
        duv_ref[...] = du.astype(BF16)
        dwv_ref[...] = dw
        dbv_ref[...] = db

    return pl.pallas_call(
        body, name="conv_bwd", grid=(N_FF_BLK,), in_specs=[ug, uv, wg, wv, bg, bv, ug],
        out_specs=[ug, ug, wg, wg, bg, bg],
        out_shape=[jax.ShapeDtypeStruct((S, D_FF), BF16), jax.ShapeDtypeStruct((S, D_FF), BF16),
                   jax.ShapeDtypeStruct((3, D_FF), F32), jax.ShapeDtypeStruct((3, D_FF), F32),
                   jax.ShapeDtypeStruct((1, D_FF), F32), jax.ShapeDtypeStruct((1, D_FF), F32)],
        compiler_params=_params(("parallel",)))(u, u, conv_w, conv_w, conv_b, conv_b, da)


def _adamw(name, w, g, m, v):
    shape = w.shape
    cols = shape[-1]
    flat = [t.reshape(-1, cols) for t in (w, g, m, v)]
    r = flat[0].shape[0]
    tr = min(128, r)

    def body(w_ref, g_ref, m_ref, v_ref, d_ref, mo_ref, vo_ref):
        g = g_ref[...]
        m = ADAM_B1 * m_ref[...] + (1.0 - ADAM_B1) * g
        v = ADAM_B2 * v_ref[...] + (1.0 - ADAM_B2) * (g * g)
        m_hat = m / (1.0 - ADAM_B1 ** ADAM_STEP)
        v_hat = v / (1.0 - ADAM_B2 ** ADAM_STEP)
        d_ref[...] = -ADAM_LR * (m_hat / (jnp.sqrt(v_hat) + ADAM_EPS) + ADAM_WD * w_ref[...])
        mo_ref[...] = m
        vo_ref[...] = v

    spec = pl.BlockSpec((tr, cols), lambda i: (i, 0))
    outs = pl.pallas_call(
        body, name=name, grid=(pl.cdiv(r, tr),), in_specs=[spec] * 4, out_specs=[spec] * 3,
        out_shape=[jax.ShapeDtypeStruct((r, cols), F32)] * 3, compiler_params=_params(("parallel",)))(*flat)
    return [t.reshape(shape) for t in outs]


def _place():
    x, y, c = lax.axis_index("x"), lax.axis_index("y"), lax.axis_index("c")
    chips = [(1 - x, y), (x, 1 - y), (1 - x, 1 - y)]
    return x, y, c, chips


def _scalars(*vals):
    return jnp.stack([jnp.asarray(v, jnp.int32) for v in vals])


HBM = pl.BlockSpec(memory_space=pltpu.HBM)
SEM = pl.BlockSpec(memory_space=pltpu.SEMAPHORE)
SPLIT_COPY = pltpu.CompilerParams(has_side_effects=pltpu.SideEffectType.DATAFLOW_SIDE_EFFECTING)


def _in_hbm(x):
    return pltpu.with_memory_space_constraint(x, pltpu.HBM)


def _cast_into_slot(name, w, layer, chip):
    _, k, n4 = w.shape
    tr = max(t for t in range(16, 257, 16) if k % t == 0)

    def body(chip_ref, w_ref, o_ref):
        o_ref[...] = w_ref[...].astype(BF16)

    return pl.pallas_call(
        body, name=name,
        grid_spec=pltpu.PrefetchScalarGridSpec(
            num_scalar_prefetch=1, grid=(k // tr,),
            in_specs=[pl.BlockSpec((None, tr, n4), lambda i, s: (layer, i, 0))],
            out_specs=pl.BlockSpec((None, tr, n4), lambda i, s: (s[0], i, 0))),
        out_shape=jax.ShapeDtypeStruct((N_CHIPS, k, n4), BF16),
        compiler_params=_params(("parallel",)))(_scalars(chip), w)


def _gather_copy(buf_ref, k, from_chip, send_sem, recv_sem, chips, c, half=False):
    rows = buf_ref.at[from_chip]
    if half:
        h = buf_ref.shape[1] // 2
        rows = buf_ref.at[from_chip, pl.ds(pl.multiple_of(c * h, h), h)]
    return pltpu.make_async_remote_copy(src_ref=rows, dst_ref=rows, send_sem=send_sem, recv_sem=recv_sem,
                                        device_id=(*chips[k], c), device_id_type=MESH)


def _gather_start(name, bufs, groups, halved=()):
    n, ng = len(bufs), len(groups)
    where = {a: (gi, e) for gi, g in enumerate(groups) for e, a in enumerate(g)}

    def body(*refs):
        ins, sems, token = refs[:n], refs[n:n + 2 * ng], refs[-1]
        x, y, c, chips = _place()
        for a in range(n):
            gi, e = where[a]
            for k in range(3):
                _gather_copy(ins[a], k, 2 * x + y, sems[2 * gi].at[3 * e + k], sems[2 * gi + 1].at[3 * e + k],
                             chips, c, a in halved).start()
        token[...] = jnp.zeros_like(token)

    out_shape = [pltpu.SemaphoreType.DMA((3 * len(g),)) for g in groups for _ in range(2)]
    out_shape += [pltpu.HBM(b.shape, b.dtype) for b in bufs] + [jax.ShapeDtypeStruct((8, 128), F32)]
    out = pl.pallas_call(
        body, name=name, in_specs=[HBM] * n,
        out_specs=[SEM] * (2 * ng) + [HBM] * n + [pl.BlockSpec(memory_space=pltpu.VMEM)], out_shape=out_shape,
        input_output_aliases={a: 2 * ng + a for a in range(n)}, compiler_params=SPLIT_COPY)(*[_in_hbm(b) for b in bufs])
    sems = [(out[2 * gi], out[2 * gi + 1]) for gi in range(ng)]
    return sems, list(out[2 * ng:2 * ng + n]), out[-1]


def _gather_wait(name, bufs, send, recv, after, halved=()):
    n = len(bufs)

    def body(*refs):
        ins, send_sem, recv_sem = refs[:n], refs[n], refs[n + 1]
        x, y, c, chips = _place()
        for e in range(n):
            for k in range(3):
                sems = (send_sem.at[3 * e + k], recv_sem.at[3 * e + k])
                _gather_copy(ins[e], k, 2 * x + y, *sems, chips, c, e in halved).wait_send()
                _gather_copy(ins[e], k, 2 * chips[k][0] + chips[k][1], *sems, chips, c, e in halved).wait_recv()

    return pl.pallas_call(
        body, name=name, in_specs=[HBM] * n + [SEM, SEM, ANY], out_specs=[HBM] * n,
        out_shape=[pltpu.HBM(b.shape, b.dtype) for b in bufs],
        input_output_aliases={a: a for a in range(n)}, compiler_params=SPLIT_COPY)(*bufs, send, recv, after)


def _swap_halves(name, bufs):
    n = len(bufs)

    def body(*refs):
        ins, outs = refs[:n], refs[n:2 * n]
        send_sem, recv_sem = refs[2 * n:]
        x, y, c, chips = _place()

        def piece(ref, k, which):
            h = ref.shape[1] // 2
            return ref.at[2 * chips[k][0] + chips[k][1], pl.ds(pl.multiple_of(which * h, h), h)]

        def copy(a, k, which):
            return pltpu.make_async_remote_copy(
                src_ref=piece(ins[a], k, c), dst_ref=piece(outs[a], k, which), send_sem=send_sem.at[3 * a + k],
                recv_sem=recv_sem.at[3 * a + k], device_id=(x, y, 1 - c), device_id_type=MESH)

        for a in range(n):
            for k in range(3):
                copy(a, k, c).start()
        for a in range(n):
            for k in range(3):
                copy(a, k, c).wait_send()
                copy(a, k, 1 - c).wait_recv()

    return pl.pallas_call(
        body, name=name, in_specs=[ANY] * n, out_specs=[ANY] * n,
        out_shape=[jax.ShapeDtypeStruct(b.shape, b.dtype) for b in bufs],
        input_output_aliases={a: a for a in range(n)},
        scratch_shapes=[pltpu.SemaphoreType.DMA((3 * n,)), pltpu.SemaphoreType.DMA((3 * n,))],
    )(*bufs)


def _reduce_copy(g_ref, land_ref, mask, send_sem, recv_sem, x, y, c, sending):
    px, py, pc = x ^ ((mask >> 2) & 1), y ^ ((mask >> 1) & 1), c ^ (mask & 1)
    half = g_ref.shape[1] // 2
    src = g_ref.at[2 * px + py, pl.ds(pl.multiple_of(pc * half, half), half)]
    dst = land_ref.at[4 * x + 2 * y + c] if sending else land_ref.at[4 * px + 2 * py + pc]
    return pltpu.make_async_remote_copy(src_ref=src, dst_ref=dst, send_sem=send_sem, recv_sem=recv_sem,
                                        device_id=(px, py, pc), device_id_type=MESH)


def _reduce_start(name, grads):
    n = len(grads)
    lands = [lax.empty((N_DEV, g.shape[1] // 2, g.shape[2]), g.dtype) for g in grads]

    def body(*refs):
        gs, ls, send_sem, recv_sem = refs[:n], refs[n:2 * n], refs[2 * n], refs[2 * n + 1]
        x, y, c, _ = _place()
        for a in range(n):
            for mask in range(1, N_DEV):
                s = (N_DEV - 1) * a + mask - 1
                _reduce_copy(gs[a], ls[a], mask, send_sem.at[s], recv_sem.at[s], x, y, c, True).start()
        refs[-1][...] = jnp.zeros_like(refs[-1])

    sem = pltpu.SemaphoreType.DMA((n * (N_DEV - 1),))
    out = pl.pallas_call(
        body, name=name, in_specs=[HBM] * (2 * n),
        out_specs=[SEM, SEM] + [HBM] * (2 * n) + [pl.BlockSpec(memory_space=pltpu.VMEM)],
        out_shape=[sem, sem] + [pltpu.HBM(t.shape, t.dtype) for t in grads + lands] + [jax.ShapeDtypeStruct((8, 128), F32)],
        input_output_aliases={a: 2 + a for a in range(2 * n)}, compiler_params=SPLIT_COPY)(
            *[_in_hbm(t) for t in grads + lands])
    return out[0], out[1], list(out[2:2 + n]), list(out[2 + n:2 + 2 * n]), out[-1]


def _reduce_wait(name, send, recv, grads, lands, after):
    n = len(grads)

    def body(*refs):
        gs, ls, send_sem, recv_sem = refs[:n], refs[n:2 * n], refs[2 * n], refs[2 * n + 1]
        x, y, c, _ = _place()
        for a in range(n):
            for mask in range(1, N_DEV):
                s = (N_DEV - 1) * a + mask - 1
                sems = (send_sem.at[s], recv_sem.at[s])
                _reduce_copy(gs[a], ls[a], mask, *sems, x, y, c, True).wait_send()
                _reduce_copy(gs[a], ls[a], mask, *sems, x, y, c, False).wait_recv()

    out = pl.pallas_call(
        body, name=name, in_specs=[HBM] * (2 * n) + [SEM, SEM, ANY], out_specs=[HBM] * (2 * n),
        out_shape=[pltpu.HBM(t.shape, t.dtype) for t in grads + lands],
        input_output_aliases={a: a for a in range(2 * n)}, compiler_params=SPLIT_COPY)(*grads, *lands, send, recv, after)
    return list(out[:n]), list(out[n:])


def _reduce_sum(name, g, land, layer, into, chip, c):
    _, k4, n4 = g.shape
    half = k4 // 2
    tr = max(t for t in range(16, 513, 16) if half % t == 0)
    per = half // tr
    me = 2 * chip + c

    def body(s_ref, own_ref, *refs):
        total = own_ref[...].astype(F32)
        for ref in refs[:N_DEV - 1]:
            total = total + ref[...].astype(F32)
        refs[-1][...] = total

    in_specs = [pl.BlockSpec((None, tr, n4), lambda i, s: (s[0], s[1] * per + i, 0))]
    in_specs += [pl.BlockSpec((None, tr, n4), functools.partial(lambda i, s, m: (s[1 + m], i, 0), m=m))
                 for m in range(1, N_DEV)]
    ins = [g] + [land] * (N_DEV - 1)
    aliases = {}
    if into is not None:
        in_specs, ins, aliases = in_specs + [ANY], ins + [into], {1 + N_DEV: 0}
    return pl.pallas_call(
        body, name=name,
        grid_spec=pltpu.PrefetchScalarGridSpec(
            num_scalar_prefetch=1, grid=(per,), in_specs=in_specs,
            out_specs=pl.BlockSpec((None, tr, n4), lambda i, s: (layer, s[1] * per + i, 0))),
        out_shape=jax.ShapeDtypeStruct((DEPTH, k4, n4), F32), input_output_aliases=aliases,
        compiler_params=_params(("parallel",)))(_scalars(chip, c, *[me ^ m for m in range(1, N_DEV)]), *ins)


def _join_halves(name, bufs):
    n = len(bufs)

    def body(*refs):
        ins, outs = refs[:n], refs[n:2 * n]
        send_sem, recv_sem = refs[2 * n:]
        x, y, c, _ = _place()

        def rows(ref, which):
            half = ref.shape[1] // 2
            return ref.at[:, pl.ds(pl.multiple_of(which * half, half), half)]

        sends = [pltpu.make_async_remote_copy(
            src_ref=rows(ins[a], c), dst_ref=rows(outs[a], c), send_sem=send_sem.at[a], recv_sem=recv_sem.at[a],
            device_id=(x, y, 1 - c), device_id_type=MESH) for a in range(n)]
        for cp in sends:
            cp.start()
        for a in range(n):
            sends[a].wait_send()
            pltpu.make_async_remote_copy(
                src_ref=rows(ins[a], c), dst_ref=rows(outs[a], 1 - c), send_sem=send_sem.at[a], recv_sem=recv_sem.at[a],
                device_id=(x, y, 1 - c), device_id_type=MESH).wait_recv()

    return pl.pallas_call(
        body, name=name, in_specs=[ANY] * n, out_specs=[ANY] * n,
        out_shape=[jax.ShapeDtypeStruct(b.shape, b.dtype) for b in bufs],
        input_output_aliases={a: a for a in range(n)},
        scratch_shapes=[pltpu.SemaphoreType.DMA((n,)), pltpu.SemaphoreType.DMA((n,))],
    )(*bufs)


def _all_reduce_small(block):
    r = block.shape[0]

    def body(x_ref, out_ref, slots, send_sem, recv_sem):
        x, y, c, _ = _place()
        me = 4 * x + 2 * y + c
        slots[me] = x_ref[...]
        sends = []
        for mask in range(1, N_DEV):
            fx, fy, fc = (mask >> 2) & 1, (mask >> 1) & 1, mask & 1
            peer = (x ^ fx, y ^ fy, c ^ fc)
            cp = pltpu.make_async_remote_copy(
                src_ref=x_ref, dst_ref=slots.at[me], send_sem=send_sem.at[mask - 1], recv_sem=recv_sem.at[mask - 1],
                device_id=peer, device_id_type=MESH)
            cp.start()
            sends.append(cp)
        for mask in range(1, N_DEV):
            src = me ^ mask
            pltpu.make_async_remote_copy(
                src_ref=x_ref, dst_ref=slots.at[src], send_sem=send_sem.at[mask - 1], recv_sem=recv_sem.at[mask - 1],
                device_id=(x, y, c), device_id_type=MESH).wait_recv()
        for cp in sends:
            cp.wait_send()
        total = slots[0]
        for d in range(1, N_DEV):
            total = total + slots[d]
        out_ref[...] = total

    vmem = pl.BlockSpec(memory_space=pltpu.VMEM)
    return pl.pallas_call(
        body, name="all_reduce_small", in_specs=[vmem], out_specs=vmem,
        out_shape=jax.ShapeDtypeStruct((r, 128), F32),
        scratch_shapes=[pltpu.VMEM((N_DEV, r, 128), F32), pltpu.SemaphoreType.DMA((N_DEV - 1,)),
                        pltpu.SemaphoreType.DMA((N_DEV - 1,))],
        compiler_params=pltpu.CompilerParams(vmem_limit_bytes=VMEM_LIMIT))(block)


B_Q_COL = 2304 // 128
B_K0, B_V0, B_END = 2816, 2944, 3072


def _full_cols(w_g):
    return w_g.transpose(1, 0, 2).reshape(w_g.shape[1], -1)


def _group_src(proj, g):
    if A_GROUPS[g][1] == 1:
        return ((proj, 2 * g), (proj, 6 + 2 * g), (proj, 12 + 2 * g))
    packed = jnp.concatenate([proj[:, t * 768 + g * 256:t * 768 + (g + 1) * 256] for t in range(3)], axis=1)
    return ((packed, 0), (packed, 2), (packed, 4))


def _kv_expand(kv):
    return jnp.broadcast_to(kv.reshape(S, 2, 1, HD), (S, 2, 4, HD)).reshape(S, 8 * HD)


def _kv_reduce(dkv):
    return dkv.reshape(S, 2, 4, HD).sum(axis=2).reshape(S, 2 * HD)


def _mixer_fwd(h1, wget, rel_bias, sinks_l, bidx):
    w = dict(wget(0, h1))
    proj = _mm_nt("proj_in", h1, w["w_in"], F32, tn=1152)
    no_sinks = jnp.full((4,), NEG, F32)
    srcs = [_group_src(proj, g) for g in range(3)]
    o_g, lse_g = [], []
    for g, (_, d) in enumerate(A_GROUPS):
        o, lse = _band_fwd("band_fwd_g%d" % g, d, 2, BLK, 4 * g, srcs[g], bidx[g], rel_bias, no_sinks)
        o_g.append(o)
        lse_g.append(lse)
    o_a32, o_a, lse_a = _comb_fwd(o_g, lse_g)
    src_b = ((proj, B_Q_COL), (_kv_expand(proj[:, B_K0:B_V0]), 0), (_kv_expand(proj[:, B_V0:B_END]), 0))
    o_b32, lse_b = _band_fwd("band_fwd_b", 1, 4, BLK - 1, N_A, src_b, bidx[3], rel_bias, sinks_l)
    o_b = o_b32.astype(BF16)
    o_c32, o_c, tot_c = _sb_fwd(proj)
    w.update(wget(1, o_c32))
    br = [_mm_nn("branch_a", o_a, w["w_br_a"], F32), _mm_nn("branch_b", o_b, w["w_br_b"], F32),
          _mm_nn("branch_c", o_c, w["w_br_c"], F32)]
    merged = _gate_fwd(proj, w["b_gate"], br)
    mo = _mm_nn("out_proj", merged, w["w_out"], F32)
    saved = dict(proj=proj, srcs=srcs, src_b=src_b, o_a32=o_a32, lse_a=lse_a, o_b32=o_b32, lse_b=lse_b, tot_c=tot_c,
                 o_a=o_a, o_b=o_b, o_c=o_c, br=br, merged=merged)
    return mo, saved, w


def _mixer_bwd(d_mo, h1, w, sv, rel_bias, sinks_l, bidx, stats_in, emit):
    grads = {}
    dmerged = _mm_nt("out_proj_dx", d_mo, w["w_out"], F32)
    grads["w_out"] = _mm_tn_sharded("out_proj_dw", sv["merged"], d_mo, True)
    e, dgate, db_gate = _gate_bwd(sv["proj"], w["b_gate"], sv["br"], dmerged)
    grads["b_gate"] = db_gate
    d_o = {}
    for n, name in enumerate("abc"):
        d_o[name] = _mm_nt("branch_%s_dx" % name, e[n], w["w_br_" + name], F32)
        grads["w_br_" + name] = _mm_tn_sharded("branch_%s_dw" % name, sv["o_" + name], e[n], False)
    zero = emit(1, grads)
    no_sinks = jnp.full((4,), NEG, F32) + zero[0]
    dqs, dks, dvs, stats = [], [], [], []
    for g, (_, d) in enumerate(A_GROUPS):
        dq, dk, dv, st = _band_bwd("band_bwd_g%d" % g, d, 2, BLK, 4 * g, sv["srcs"][g], bidx[g], rel_bias, no_sinks,
                                   sv["o_a32"], sv["lse_a"], d_o["a"], stats_in[4 * g:4 * g + 4])
        dqs.append(dq)
        dks.append(dk)
        dvs.append(dv)
        stats.append(st)
    dq_b, dk_x, dv_x, st = _band_bwd("band_bwd_b", 1, 4, BLK - 1, N_A, sv["src_b"], bidx[3], rel_bias, sinks_l,
                                     sv["o_b32"], sv["lse_b"], d_o["b"], stats_in[N_A:])
    stats = jnp.concatenate(stats + [st], axis=0)
    dcq, dck, dcv = _sb_bwd(sv["proj"], sv["tot_c"], d_o["c"])
    cols = dqs + dks + dvs + [dq_b, _kv_reduce(dk_x), _kv_reduce(dv_x), dcq, dck, dcv]
    dproj = jnp.concatenate([t.astype(BF16) for t in cols] + list(dgate), axis=1)
    grads["w_in"] = _mm_tn("proj_in_dw", dproj, h1, BF16, tm=768).reshape(N_CHIPS, IN_SHARD, D)
    zero = emit(2, grads)
    dh1 = _mm_nn("proj_in_dx", dproj, w["w_in"], F32)
    return dh1, grads, stats, zero


def _ffn_fwd(h2, w):
    u = _mm_nn("ffn_up", h2, w["w_up"], F32, tn=1024)
    a = _conv_fwd(u, w["conv_w"], w["conv_b"])
    dn = _mm_nn("ffn_down", a, w["w_down"], F32)
    return dn, dict(u=u, a=a)


def _ffn_bwd(d_dn, h2, w, sv):
    grads = {}
    da = _mm_nt("ffn_down_dx", d_dn, w["w_down"], F32, tn=1024)
    grads["w_down"] = _mm_tn_sharded("ffn_down_dw", sv["a"], d_dn, True)
    dug, duv, dwg, dwv, dbg, dbv = _conv_bwd(sv["u"], w["conv_w"], w["conv_b"], da)
    du = jnp.concatenate([dug, duv], axis=1)
    grads["conv_w"] = jnp.concatenate([dwg, dwv], axis=1)
    grads["conv_b"] = jnp.concatenate([dbg, dbv], axis=1)
    dh2 = _mm_nt("ffn_up_dx", du, w["w_up"], F32)
    grads["w_up"] = _mm_tn_sharded("ffn_up_dw", h2, du, False, tn=1024)
    return dh2, grads


BIG = ("w_in", "w_br_a", "w_br_b", "w_br_c", "w_out", "w_up", "w_down")


def _shard_view(name, w):
    return jnp.swapaxes(w, 1, 2) if name == "w_in" else w
WEIGHT_GROUPS = (("w_in", "b_gate"), ("w_br_a", "w_br_b", "w_br_c", "w_out"), ("w_up", "conv_w", "w_down"))
GRAD_GROUPS = (("w_down", "w_up"), ("w_out", "w_br_a", "w_br_b", "w_br_c"), ("w_in",))
SMALL_ROWS = (("rel_bias", 8), ("attn_pre_norm", 16), ("attn_post_norm", 16), ("ffn_pre_norm", 16), ("ffn_post_norm", 16),
              ("sinks", 8), ("conv_b", 128), ("b_gate", 48), ("conv_w", 384), ("loss", 8))


def _pack_small(vals):
    rows = []
    for name, n in SMALL_ROWS:
        flat = vals[name].reshape(-1).astype(F32)
        rows.append(jnp.pad(flat, (0, n * 128 - flat.shape[0])).reshape(n, 128))
    return jnp.concatenate(rows, axis=0)


def _unpack_small(block, shapes):
    out, row = {}, 0
    for name, n in SMALL_ROWS:
        size = int(np.prod(shapes[name]))
        out[name] = block[row:row + n].reshape(-1)[:size].reshape(shapes[name])
        row += n
    return out


def kernel(x, rel_bias, attn_pre_norm, w_in, b_gate, sinks, w_br_a, w_br_b, w_br_c, w_out, attn_post_norm, ffn_pre_norm, w_up, conv_w, conv_b, w_down, ffn_post_norm, loss_target, m_rel_bias, m_attn_pre_norm, m_w_in, m_b_gate, m_sinks, m_w_br_a, m_w_br_b, m_w_br_c, m_w_out, m_attn_post_norm, m_ffn_pre_norm, m_w_up, m_conv_w, m_conv_b, m_w_down, m_ffn_post_norm, v_rel_bias, v_attn_pre_norm, v_w_in, v_b_gate, v_sinks, v_w_br_a, v_w_br_b, v_w_br_c, v_w_out, v_attn_post_norm, v_ffn_pre_norm, v_w_up, v_conv_w, v_conv_b, v_w_down, v_ffn_post_norm):
    names = ("rel_bias", "attn_pre_norm", "w_in", "b_gate", "sinks", "w_br_a", "w_br_b", "w_br_c", "w_out",
             "attn_post_norm", "ffn_pre_norm", "w_up", "conv_w", "conv_b", "w_down", "ffn_post_norm")
    weights = dict(zip(names, (rel_bias, attn_pre_norm, w_in, b_gate, sinks, w_br_a, w_br_b, w_br_c, w_out,
                               attn_post_norm, ffn_pre_norm, w_up, conv_w, conv_b, w_down, ffn_post_norm)))
    mom1 = dict(zip(names, (m_rel_bias, m_attn_pre_norm, m_w_in, m_b_gate, m_sinks, m_w_br_a, m_w_br_b, m_w_br_c,
                            m_w_out, m_attn_post_norm, m_ffn_pre_norm, m_w_up, m_conv_w, m_conv_b, m_w_down,
                            m_ffn_post_norm)))
    mom2 = dict(zip(names, (v_rel_bias, v_attn_pre_norm, v_w_in, v_b_gate, v_sinks, v_w_br_a, v_w_br_b, v_w_br_c,
                            v_w_out, v_attn_post_norm, v_ffn_pre_norm, v_w_up, v_conv_w, v_conv_b, v_w_down,
                            v_ffn_post_norm)))

    chip = 2 * lax.axis_index("x") + lax.axis_index("y")
    core = lax.axis_index("c")

    keys = [(n, l) for l in range(DEPTH) for group in WEIGHT_GROUPS for n in group]
    groups = [[keys.index((n, l)) for n in group] for l in range(DEPTH) for group in WEIGHT_GROUPS]

    def slot_buffer(n, l):
        if n in BIG:
            return _cast_into_slot("cast_" + n, _shard_view(n, weights[n]), l, chip)
        shard = weights[n][l]
        return lax.dynamic_update_slice(jnp.zeros((N_CHIPS,) + shard.shape, F32), shard[None],
                                        (chip, jnp.int32(0), jnp.int32(0)))

    first = keys.index(("w_in", 0))
    sems, in_flight, _ = _gather_start("gather_start", [slot_buffer(*k) for k in keys], groups, (first,))

    def wget(l, gi, after):
        g = l * len(WEIGHT_GROUPS) + gi
        halved = tuple(e for e, a in enumerate(groups[g]) if a == first)
        got = list(_gather_wait("gather_wait_%d_%d" % (l, gi), [in_flight[a] for a in groups[g]], *sems[g], after,
                                halved))
        for e in halved:
            got[e] = _swap_halves("swap_halves", [got[e]])[0]
        out = {}
        for n, buf in zip(WEIGHT_GROUPS[gi], got):
            out[n] = buf.reshape(-1, buf.shape[-1]) if n in ("w_in", "w_out", "w_down") else _full_cols(buf)
        if gi == len(WEIGHT_GROUPS) - 1:
            out["conv_b"] = conv_b[l:l + 1]
        return out

    pending = []

    def emit(l, gi, grads):
        group = GRAD_GROUPS[gi]
        *started, token = _reduce_start("reduce_start_%d_%d" % (l, gi), [grads[n] for n in group])
        pending.append((l, group) + tuple(started))
        return token[:1, :1]

    local = _local_step(x.reshape(S, D), loss_target.reshape(S, D), wget, emit, rel_bias, sinks, attn_pre_norm,
                        attn_post_norm, ffn_pre_norm, ffn_post_norm)
    return _reduce_and_update(x.shape, names, weights, mom1, mom2, chip, core, pending, *local)


def _local_step(xs, target, wget, emit, rel_bias, sinks, attn_pre_norm, attn_post_norm, ffn_pre_norm, ffn_post_norm):
    bidx = jnp.asarray(_bucket_maps())

    saved, layers = [], []
    h1 = _rms_fwd("pre_norm_first", xs, attn_pre_norm[0:1])
    x_in = xs
    for l in range(DEPTH):
        mo, sv_mix, w = _mixer_fwd(h1, functools.partial(wget, l), rel_bias, sinks[l], bidx)
        x_mid, h2 = _post_pre_fwd("post_attn_norm", x_in, mo, attn_post_norm[l:l + 1], ffn_pre_norm[l:l + 1])
        w.update(wget(l, 2, h2))
        dn, sv_ffn = _ffn_fwd(h2, w)
        g_next = attn_pre_norm[l + 1:l + 2] if l + 1 < DEPTH else None
        x_out, h1_next = _post_pre_fwd("post_ffn_norm" if l + 1 < DEPTH else "post_ffn_norm_last", x_mid, dn,
                                       ffn_post_norm[l:l + 1], g_next)
        saved.append(dict(x_in=x_in, h1=h1, mo=mo, x_mid=x_mid, h2=h2, dn=dn, mix=sv_mix, ffn=sv_ffn))
        layers.append(w)
        x_in, h1 = x_out, h1_next

    loss_row, dres = _loss_kernel(x_in, target)

    small = [None] * DEPTH
    stats = jnp.zeros((N_BAND_Q, 8, 128), F32)
    dh_next = None
    for l in reversed(range(DEPTH)):
        w, sv = layers[l], saved[l]
        if l + 1 < DEPTH:
            pre = (saved[l + 1]["x_in"], attn_pre_norm[l + 1:l + 2] + zero, dh_next)
            dres, d_dn, dg_pre_next, dg_fpost = _norm_bwd("post_ffn_norm_bwd", dres, pre,
                                                          (sv["dn"], ffn_post_norm[l:l + 1]))
            small[l + 1]["attn_pre_norm"] = dg_pre_next
        else:
            dres, d_dn, _, dg_fpost = _norm_bwd("post_ffn_norm_last_bwd", dres, None, (sv["dn"], ffn_post_norm[l:l + 1]))
        dh2, g_ffn = _ffn_bwd(d_dn, sv["h2"], w, sv["ffn"])
        zero = emit(l, 0, g_ffn)
        dres, d_mo, dg_fpre, dg_apost = _norm_bwd("post_attn_norm_bwd", dres,
                                                  (sv["x_mid"], ffn_pre_norm[l:l + 1] + zero, dh2),
                                                  (sv["mo"], attn_post_norm[l:l + 1]))
        dh_next, g_mix, stats, zero = _mixer_bwd(d_mo, sv["h1"], w, sv["mix"], rel_bias, sinks[l], bidx, stats,
                                                 functools.partial(emit, l))
        small[l] = dict(ffn_post_norm=dg_fpost, ffn_pre_norm=dg_fpre, attn_post_norm=dg_apost,
                        sinks=stats[N_A:, 1, 0], conv_b=g_ffn["conv_b"], b_gate=g_mix["b_gate"], conv_w=g_ffn["conv_w"])
    grad_x, _, dg_pre0, _ = _norm_bwd("pre_norm_first_bwd", dres, (saved[0]["x_in"], attn_pre_norm[0:1] + zero, dh_next),
                                      None)
    small[0]["attn_pre_norm"] = dg_pre0
    return loss_row, grad_x, small, stats


def _reduce_and_update(x_shape, names, weights, mom1, mom2, chip, core, pending, loss_row, grad_x, small, stats):
    delta, new_m, new_v, grads = {}, {}, {}, {}

    def update(n, g):
        grads[n] = g
        delta[n], new_m[n], new_v[n] = _adamw("adamw_" + n, _shard_view(n, weights[n]), g,
                                              _shard_view(n, mom1[n]), _shard_view(n, mom2[n]))

    summed = {}

    def finish(which, after):
        for l, group, send, recv, gs, lands in pending:
            if (group == ("w_in",)) == which:
                gs, lands = _reduce_wait("reduce_wait_%d_%s" % (l, group[0]), send, recv, gs, lands, after)
                for n, g, land in zip(group, gs, lands):
                    summed[n] = _reduce_sum("reduce_sum_%d_%s" % (l, n), g, land, l, summed.get(n), chip, core)

    finish(False, grad_x)
    early = [n for n in BIG if n != "w_in"]
    for n, g in zip(early, _join_halves("join_halves", [summed[n] for n in early])):
        update(n, g)
    finish(True, delta[early[-1]])
    update("w_in", _join_halves("join_halves_w_in", [summed["w_in"]])[0])
    for out in (grads, delta, new_m, new_v):
        out["w_in"] = _shard_view("w_in", out["w_in"])

    small_vals = {n: jnp.stack([small[l][n].reshape(weights[n].shape[1:]) for l in range(DEPTH)])
                  for n in ("attn_pre_norm", "attn_post_norm", "ffn_pre_norm", "ffn_post_norm", "conv_b", "sinks")}
    small_vals["b_gate"] = jnp.stack([small[l]["b_gate"] for l in range(DEPTH)])
    small_vals["conv_w"] = jnp.stack([small[l]["conv_w"] for l in range(DEPTH)])
    small_vals["rel_bias"] = stats[:, 0, :NUM_BUCKETS].T
    small_vals["loss"] = loss_row[0, :1]
    shapes = {n: v.shape for n, v in small_vals.items()}
    packed, delta["w_in"] = lax.optimization_barrier((_pack_small(small_vals), delta["w_in"]))
    reduced = _unpack_small(_all_reduce_small(packed), shapes)
    reduced["b_gate"] = lax.dynamic_slice_in_dim(reduced["b_gate"], chip * (D // N_CHIPS), D // N_CHIPS, axis=2)
    reduced["conv_w"] = lax.dynamic_slice_in_dim(reduced["conv_w"], chip * (2 * D_FF // N_CHIPS), 2 * D_FF // N_CHIPS, axis=2)
    for n in names:
        if n not in grads:
            update(n, reduced[n].reshape(weights[n].shape))

    loss = reduced["loss"].reshape(())
    return (loss, grad_x.reshape(x_shape), *[grads[n] for n in names], *[delta[n] for n in names],
            *[new_m[n] for n in names], *[new_v[n] for n in names])
```

```python
import functools
import math

import numpy as np
import jax
import jax.numpy as jnp
from jax import lax
from jax.experimental import pallas as pl
from jax.experimental.pallas import tpu as pltpu

F32 = jnp.float32
BF16 = jnp.bfloat16

S = 2048
D = 1024
DEPTH = 2
HD = 64
BLK = 128
NQB = S // BLK
A_GROUPS = ((128, 1), (512, 4), (2048, 16))
N_BAND_Q = 20
N_A = 12
NUM_BUCKETS = 32
MAX_DISTANCE = 2048
D_FF = 4096
IN_COLS = 6912
IN_SHARD = IN_COLS // 4
OFF_GATE = 3840
EPS = 1e-6
SCALE = HD ** -0.5
NEG = -1e30
N_CHIPS = 4
N_DEV = 8

ADAM_LR = 0.001
ADAM_B1 = 0.9
ADAM_B2 = 0.999
ADAM_EPS = 1e-08
ADAM_WD = 0.01
ADAM_STEP = 10

VMEM_LIMIT = 56 * 1024 * 1024

NN = (((1,), (0,)), ((), ()))
NT = (((1,), (1,)), ((), ()))
TN = (((0,), (0,)), ((), ()))

MESH = pl.DeviceIdType.MESH
ANY = pl.BlockSpec(memory_space=pl.ANY)


def _dot(a, b, dims):
    return lax.dot_general(a, b, dims, preferred_element_type=F32)


def _params(sem):
    return pltpu.CompilerParams(dimension_semantics=sem, vmem_limit_bytes=VMEM_LIMIT)


def _matmul(name, a, b, out_shape, out_dtype, grid, a_spec, b_spec, o_spec, dims, acc_shape):
    nk = grid[-1]

    def body(a_ref, b_ref, o_ref, *scratch):
        part = _dot(a_ref[...].astype(BF16), b_ref[...].astype(BF16), dims)
        if nk == 1:
            o_ref[...] = part.astype(o_ref.dtype)
            return
        acc_ref, = scratch
        k = pl.program_id(len(grid) - 1)

        @pl.when(k == 0)
        def _():
            acc_ref[...] = part

        @pl.when(k > 0)
        def _():
            acc_ref[...] += part

        @pl.when(k == nk - 1)
        def _():
            o_ref[...] = acc_ref[...].astype(o_ref.dtype)

    scratch = [] if nk == 1 else [pltpu.VMEM(acc_shape, F32)]
    sem = ("parallel",) * (len(grid) - 1) + ("arbitrary",)
    return pl.pallas_call(
        body, name=name, grid=grid, in_specs=[a_spec, b_spec], out_specs=o_spec,
        out_shape=jax.ShapeDtypeStruct(out_shape, out_dtype), scratch_shapes=scratch,
        compiler_params=_params(sem))(a, b)


FULL_K = 8192


def _mm_tn_sharded(name, a, b, row_sharded, tm=512, tn=512, tk=FULL_K):
    k, m = a.shape
    n = b.shape[1]
    m4, n4 = (m // N_CHIPS, n) if row_sharded else (m, n // N_CHIPS)
    tm, tn, tk = min(tm, m4), min(tn, n4), min(tk, k)
    per_m, per_n = m4 // tm, n4 // tn
    if row_sharded:
        o_map = lambda i, j, l: (i // per_m, i % per_m, j)
    else:
        o_map = lambda i, j, l: (j // per_n, i, j % per_n)
    return _matmul(name, a, b, (N_CHIPS, m4, n4), BF16, (m // tm, n // tn, k // tk),
                   pl.BlockSpec((tk, tm), lambda i, j, l: (l, i)),
                   pl.BlockSpec((tk, tn), lambda i, j, l: (l, j)),
                   pl.BlockSpec((None, tm, tn), o_map), TN, (tm, tn))


def _mm_nn(name, a, b, out_dtype, tm=512, tn=512, tk=FULL_K):
    m, k = a.shape
    n = b.shape[1]
    tm, tn, tk = min(tm, m), min(tn, n), min(tk, k)
    return _matmul(name, a, b, (m, n), out_dtype, (m // tm, n // tn, k // tk),
                   pl.BlockSpec((tm, tk), lambda i, j, l: (i, l)),
                   pl.BlockSpec((tk, tn), lambda i, j, l: (l, j)),
                   pl.BlockSpec((tm, tn), lambda i, j, l: (i, j)), NN, (tm, tn))


def _mm_nt(name, a, b, out_dtype, tm=512, tn=512, tk=FULL_K):
    m, k = a.shape
    n = b.shape[0]
    tm, tn, tk = min(tm, m), min(tn, n), min(tk, k)
    return _matmul(name, a, b, (m, n), out_dtype, (m // tm, n // tn, k // tk),
                   pl.BlockSpec((tm, tk), lambda i, j, l: (i, l)),
                   pl.BlockSpec((tn, tk), lambda i, j, l: (j, l)),
                   pl.BlockSpec((tm, tn), lambda i, j, l: (i, j)), NT, (tm, tn))


def _mm_tn(name, a, b, out_dtype, tm=512, tn=512, tk=FULL_K):
    k, m = a.shape
    n = b.shape[1]
    tm, tn, tk = min(tm, m), min(tn, n), min(tk, k)
    return _matmul(name, a, b, (m, n), out_dtype, (m // tm, n // tn, k // tk),
                   pl.BlockSpec((tk, tm), lambda i, j, l: (l, i)),
                   pl.BlockSpec((tk, tn), lambda i, j, l: (l, j)),
                   pl.BlockSpec((tm, tn), lambda i, j, l: (i, j)), TN, (tm, tn))


TR = 256


def _row_spec(width=D):
    return pl.BlockSpec((TR, width), lambda i: (i, 0))


def _vec_spec(width=D):
    return pl.BlockSpec((1, width), lambda i: (0, 0))


def _rms(x, g):
    r = lax.rsqrt(jnp.mean(x * x, axis=-1, keepdims=True) + EPS)
    return x * r * g


def _rms_fwd(name, x, g):
    def body(x_ref, g_ref, h_ref):
        h_ref[...] = _rms(x_ref[...], g_ref[...]).astype(BF16)

    return pl.pallas_call(
        body, name=name, grid=(S // TR,), in_specs=[_row_spec(), _vec_spec()], out_specs=_row_spec(),
        out_shape=jax.ShapeDtypeStruct((S, D), BF16), compiler_params=_params(("parallel",)))(x, g)


def _post_pre_fwd(name, x, y, g_post, g_pre):
    has_pre = g_pre is not None

    def body(*refs):
        if has_pre:
            x_ref, y_ref, gp_ref, gn_ref, xn_ref, h_ref = refs
        else:
            x_ref, y_ref, gp_ref, xn_ref = refs
        xn = x_ref[...] + _rms(y_ref[...], gp_ref[...])
        xn_ref[...] = xn
        if has_pre:
            h_ref[...] = _rms(xn, gn_ref[...]).astype(BF16)

    ins = [x, y, g_post] + ([g_pre] if has_pre else [])
    in_specs = [_row_spec(), _row_spec(), _vec_spec()] + ([_vec_spec()] if has_pre else [])
    out_shape = [jax.ShapeDtypeStruct((S, D), F32)] + ([jax.ShapeDtypeStruct((S, D), BF16)] if has_pre else [])
    out_specs = [_row_spec()] + ([_row_spec()] if has_pre else [])
    out = pl.pallas_call(
        body, name=name, grid=(S // TR,), in_specs=in_specs, out_specs=out_specs, out_shape=out_shape,
        compiler_params=_params(("parallel",)))(*ins)
    return out if has_pre else (out[0], None)


def _rms_bwd_math(x, g, dy):
    r = lax.rsqrt(jnp.mean(x * x, axis=-1, keepdims=True) + EPS)
    n = x * r
    dn = dy * g
    dx = r * (dn - n * jnp.mean(dn * n, axis=-1, keepdims=True))
    return dx, jnp.sum(dy * n, axis=0, keepdims=True)


def _norm_bwd(name, dres, pre=None, post=None):
    has_pre, has_post = pre is not None, post is not None

    def body(*refs):
        refs = list(refs)
        dres_ref = refs.pop(0)
        if has_pre:
            xn_ref, gn_ref, dh_ref = refs[:3]
            refs = refs[3:]
        if has_post:
            y_ref, gp_ref = refs[:2]
            refs = refs[2:]
        dxn_ref = refs.pop(0)
        dy_ref = refs.pop(0) if has_post else None
        dgn_ref = refs.pop(0) if has_pre else None
        dgp_ref = refs.pop(0) if has_post else None
        first = pl.program_id(0) == 0
        dxn = dres_ref[...]
        if has_pre:
            dx, dg = _rms_bwd_math(xn_ref[...], gn_ref[...], dh_ref[...])
            dxn = dxn + dx

            @pl.when(first)
            def _():
                dgn_ref[...] = dg

            @pl.when(jnp.logical_not(first))
            def _():
                dgn_ref[...] += dg
        dxn_ref[...] = dxn
        if has_post:
            dy, dg = _rms_bwd_math(y_ref[...], gp_ref[...], dxn)
            dy_ref[...] = dy.astype(BF16)

            @pl.when(first)
            def _():
                dgp_ref[...] = dg

            @pl.when(jnp.logical_not(first))
            def _():
                dgp_ref[...] += dg

    ins, in_specs = [dres], [_row_spec()]
    if has_pre:
        ins += list(pre)
        in_specs += [_row_spec(), _vec_spec(), _row_spec()]
    if has_post:
        ins += list(post)
        in_specs += [_row_spec(), _vec_spec()]
    out_shape, out_specs = [jax.ShapeDtypeStruct((S, D), F32)], [_row_spec()]
    if has_post:
        out_shape.append(jax.ShapeDtypeStruct((S, D), BF16))
        out_specs.append(_row_spec())
    for _ in range(int(has_pre) + int(has_post)):
        out_shape.append(jax.ShapeDtypeStruct((1, D), F32))
        out_specs.append(_vec_spec())
    out = list(pl.pallas_call(
        body, name=name, grid=(S // TR,), in_specs=in_specs, out_specs=out_specs, out_shape=out_shape,
        compiler_params=_params(("arbitrary",)))(*ins))
    dxn = out.pop(0)
    dy = out.pop(0) if has_post else None
    dgn = out.pop(0) if has_pre else None
    dgp = out.pop(0) if has_post else None
    return dxn, dy, dgn, dgp


def _loss_kernel(y, target):
    def body(y_ref, t_ref, loss_ref, dy_ref):
        e = y_ref[...] - t_ref[...]
        dy_ref[...] = e * (1.0 / D)
        part = jnp.zeros((1, 128), F32) + 0.5 * jnp.sum(jnp.mean(e * e, axis=-1, keepdims=True))

        @pl.when(pl.program_id(0) == 0)
        def _():
            loss_ref[...] = part

        @pl.when(pl.program_id(0) > 0)
        def _():
            loss_ref[...] += part

    return pl.pallas_call(
        body, name="loss", grid=(S // TR,), in_specs=[_row_spec(), _row_spec()],
        out_specs=[_vec_spec(128), _row_spec()],
        out_shape=[jax.ShapeDtypeStruct((1, 128), F32), jax.ShapeDtypeStruct((S, D), F32)],
        compiler_params=_params(("arbitrary",)))(y, target)


def _t5_bucket_np(dist):
    max_exact = NUM_BUCKETS // 2
    nf = np.maximum(dist, 1).astype(np.float32)
    large = max_exact + (np.log(nf / max_exact) / np.float32(math.log(MAX_DISTANCE / max_exact))
                         * (NUM_BUCKETS - max_exact)).astype(np.int32)
    large = np.minimum(large, NUM_BUCKETS - 1)
    return np.where(dist < max_exact, dist, large).astype(np.int32)


def _bucket_maps():
    a = np.arange(BLK)[:, None]
    b = np.arange(2 * BLK)[None, :]
    dist = np.maximum(a + BLK - b, 0)
    maps = [_t5_bucket_np(dist * d) for _, d in A_GROUPS] + [_t5_bucket_np(dist)]
    return np.stack(maps).astype(np.int32)


def _classes(arr, d):
    return arr.reshape(S // d, d * arr.shape[1])


def _class_spec(arr, col0, d):
    ncol = arr.shape[1] // 128
    return pl.BlockSpec((S // d, 128), lambda p, r: (0, r * ncol + col0 + p))


def _band_rows(b):
    return (pl.ds(pl.multiple_of(b * BLK, BLK), BLK), pl.ds(pl.multiple_of(jnp.maximum(b - 1, 0) * BLK, BLK), BLK))


def _band_bias(tab_ref, bidx_ref, h):
    bi = bidx_ref[...]
    bias = jnp.zeros((BLK, 2 * BLK), F32)
    for kk in range(NUM_BUCKETS):
        bias = jnp.where(bi == kk, tab_ref[kk, h], bias)
    return bias


def _lane_lo(rows=BLK):
    return lax.broadcasted_iota(jnp.int32, (rows, 128), 1) < HD


def _per_head(x, lo):
    return (jnp.sum(jnp.where(lo, x, 0.0), axis=1, keepdims=True) * (1.0 / HD),
            jnp.sum(jnp.where(lo, 0.0, x), axis=1, keepdims=True) * (1.0 / HD))


def _band_fill(bias_ref, tab_ref, bidx_ref, head, maxd):
    a = lax.broadcasted_iota(jnp.int32, (BLK, 2 * BLK), 0)
    c = lax.broadcasted_iota(jnp.int32, (BLK, 2 * BLK), 1)
    dist = a + BLK - c
    in_band = jnp.logical_and(dist >= 0, dist <= maxd)
    for h in range(2):
        bias = jnp.where(in_band, _band_bias(tab_ref, bidx_ref, head + h), NEG)
        bias_ref[1, h * BLK:(h + 1) * BLK, :] = bias
        bias_ref[0, h * BLK:(h + 1) * BLK, :] = jnp.where(c >= BLK, bias, NEG)


def _stack_heads(x, lo, dtype=BF16):
    return jnp.concatenate([jnp.where(lo, x, 0.0), jnp.where(lo, 0.0, x)], axis=0).astype(dtype)


def _unstack_heads(x, lo):
    n = x.shape[0] // 2
    return jnp.where(lo, x[:n], x[n:])


def _stack_rows(ref, prev, cur):
    return jnp.concatenate([ref[prev, :], ref[cur, :]], axis=0).astype(BF16)


def _band_fwd(name, d, n_pairs, maxd, head0, srcs, bidx_g, tab, sinks):
    nb = S // d // BLK
    (qa, qc), (ka, kc), (va, vc) = srcs
    out_spec = pl.BlockSpec((S // d, 128), lambda p, r: (0, r * n_pairs + p))
    smem = pl.BlockSpec(memory_space=pltpu.SMEM)
    full = pl.BlockSpec((BLK, 2 * BLK), lambda p, r: (0, 0))

    def body(tab_ref, sink_ref, q_ref, k_ref, v_ref, bidx_ref, o_ref, lse_ref, bias_ref):
        p, r = pl.program_id(0), pl.program_id(1)

        @pl.when(r == 0)
        def _():
            _band_fill(bias_ref, tab_ref, bidx_ref, head0 + 2 * p, maxd)

        lo = _lane_lo()
        sink = jnp.where(lax.broadcasted_iota(jnp.int32, (2 * BLK, 1), 0) < BLK, sink_ref[2 * p], sink_ref[2 * p + 1])

        def block(b, carry):
            cur, prev = _band_rows(b)
            qs = _stack_heads(q_ref[cur, :] * SCALE, lo)
            ks, vs = _stack_rows(k_ref, prev, cur), _stack_rows(v_ref, prev, cur)
            s = _dot(qs, ks, NT) + bias_ref[jnp.minimum(b, 1)]
            m = jnp.max(s, axis=1, keepdims=True)
            pr = jnp.exp(s - m)
            l = jnp.sum(pr, axis=1, keepdims=True)
            num = _dot(pr.astype(BF16), vs, NN)
            lse = m + jnp.log(l)
            sig = 1.0 / (1.0 + jnp.exp(sink - lse))
            o_ref[cur, :] = _unstack_heads(num * (sig / l), lo)
            lse_ref[cur, :] = _unstack_heads(lse + jnp.zeros((2 * BLK, 128), F32), lo)
            return carry

        lax.fori_loop(0, nb, block, 0, unroll=min(nb, 2))

    shape = jax.ShapeDtypeStruct((S // d, d * n_pairs * 128), F32)
    o, lse = pl.pallas_call(
        body, name=name, grid=(n_pairs, d),
        in_specs=[smem, smem, _class_spec(qa, qc, d), _class_spec(ka, kc, d), _class_spec(va, vc, d), full],
        out_specs=[out_spec, out_spec], out_shape=[shape, shape],
        scratch_shapes=[pltpu.VMEM((2, 2 * BLK, 2 * BLK), F32)],
        compiler_params=_params(("parallel", "arbitrary")))(
            tab, sinks, _classes(qa, d), _classes(ka, d), _classes(va, d), bidx_g)
    return o.reshape(S, n_pairs * 128), lse.reshape(S, n_pairs * 128)


def _band_bwd(name, d, n_pairs, maxd, head0, srcs, bidx_g, tab, sinks, o, lse, do, stats_in):
    nb = S // d // BLK
    rows = S // d
    (qa, qc), (ka, kc), (va, vc) = srcs
    cls_spec = pl.BlockSpec((rows, 128), lambda p, r: (0, r * n_pairs + p))
    smem = pl.BlockSpec(memory_space=pltpu.SMEM)
    full = pl.BlockSpec((BLK, 2 * BLK), lambda p, r: (0, 0))
    stat_spec = pl.BlockSpec((2, 8, 128), lambda p, r: (p, 0, 0))

    def body(tab_ref, sink_ref, q_ref, k_ref, v_ref, bidx_ref, o_ref, lse_ref, do_ref, sin_ref,
             dq_ref, dk_ref, dv_ref, stat_ref, bias_ref, dsacc_ref, sk_ref):
        p, r = pl.program_id(0), pl.program_id(1)

        @pl.when(r == 0)
        def _():
            _band_fill(bias_ref, tab_ref, bidx_ref, head0 + 2 * p, maxd)
            dsacc_ref[...] = jnp.zeros_like(dsacc_ref)
            sk_ref[...] = jnp.zeros_like(sk_ref)

        dk_ref[...] = jnp.zeros_like(dk_ref)
        dv_ref[...] = jnp.zeros_like(dv_ref)
        lo = _lane_lo()
        head1 = lax.broadcasted_iota(jnp.int32, (2 * BLK, 1), 0) >= BLK
        sink = jnp.where(head1, sink_ref[2 * p + 1], sink_ref[2 * p])

        def block(b, carry):
            cur, prev = _band_rows(b)
            qs = _stack_heads(q_ref[cur, :] * SCALE, lo)
            ks, vs = _stack_rows(k_ref, prev, cur), _stack_rows(v_ref, prev, cur)
            do = do_ref[cur, :]
            dos = _stack_heads(do, lo, F32)
            lse = jnp.concatenate(_per_head(lse_ref[cur, :], lo), axis=0)
            prod = do * o_ref[cur, :]
            delta = jnp.concatenate([jnp.sum(jnp.where(lo, prod, 0.0), axis=1, keepdims=True),
                                     jnp.sum(jnp.where(lo, 0.0, prod), axis=1, keepdims=True)], axis=0)
            sig = 1.0 / (1.0 + jnp.exp(sink - lse))
            pr = jnp.exp(_dot(qs, ks, NT) + bias_ref[jnp.minimum(b, 1)] - lse)
            ds = pr * (sig * (_dot(dos.astype(BF16), vs, NT) - delta))
            dsb = ds.astype(BF16)
            dq_ref[cur, :] = SCALE * _unstack_heads(_dot(dsb, ks, NN), lo)
            dk = _dot(dsb, qs, TN)
            dv = _dot(pr.astype(BF16), (sig * dos).astype(BF16), TN)
            dk_ref[prev, :] += dk[:BLK]
            dk_ref[cur, :] += dk[BLK:]
            dv_ref[prev, :] += dv[:BLK]
            dv_ref[cur, :] += dv[BLK:]
            dsacc_ref[...] += ds
            sink_grad = -delta * (1.0 - sig)
            for h in range(2):
                sk_ref[h] += jnp.zeros((8, 128), F32) + jnp.sum(sink_grad[h * BLK:(h + 1) * BLK])
            return carry

        lax.fori_loop(0, nb, block, 0, unroll=min(nb, 2))

        @pl.when(r == d - 1)
        def _():
            bi = bidx_ref[...]
            lane = lax.broadcasted_iota(jnp.int32, (8, 128), 1)
            sub = lax.broadcasted_iota(jnp.int32, (8, 128), 0)
            for h in range(2):
                acc = dsacc_ref[h * BLK:(h + 1) * BLK, :]
                row = jnp.where(jnp.logical_and(sub == 1, lane == 0), sk_ref[h], 0.0)
                for kk in range(NUM_BUCKETS):
                    tot = jnp.sum(jnp.where(bi == kk, acc, 0.0))
                    row = jnp.where(jnp.logical_and(sub == 0, lane == kk), tot, row)
                stat_ref[h] = row + jnp.where(sub == 0, sin_ref[h], 0.0)

    shape = jax.ShapeDtypeStruct((rows, d * n_pairs * 128), F32)
    dq, dk, dv, stats = pl.pallas_call(
        body, name=name, grid=(n_pairs, d),
        in_specs=[smem, smem, _class_spec(qa, qc, d), _class_spec(ka, kc, d), _class_spec(va, vc, d), full,
                  cls_spec, cls_spec, cls_spec, stat_spec],
        out_specs=[cls_spec, cls_spec, cls_spec, stat_spec],
        out_shape=[shape, shape, shape, jax.ShapeDtypeStruct((2 * n_pairs, 8, 128), F32)],
        scratch_shapes=[pltpu.VMEM((2, 2 * BLK, 2 * BLK), F32), pltpu.VMEM((2 * BLK, 2 * BLK), F32),
                        pltpu.VMEM((2, 8, 128), F32)],
        compiler_params=_params(("arbitrary", "arbitrary")))(
            tab, sinks, _classes(qa, d), _classes(ka, d), _classes(va, d), bidx_g,
            _classes(o, d), _classes(lse, d), _classes(do, d), stats_in)
    width = n_pairs * 128
    return dq.reshape(S, width), dk.reshape(S, width), dv.reshape(S, width), stats


def _comb_fwd(o_g, lse_g):
    def body(o0, o1, o2, l0, l1, l2, out_ref, outb_ref, lse_ref):
        a0, a1, a2 = l0[...], l1[...], l2[...]
        m = jnp.maximum(jnp.maximum(a0, a1), a2)
        e0, e1, e2 = jnp.exp(a0 - m), jnp.exp(a1 - m), jnp.exp(a2 - m)
        tot = e0 + e1 + e2
        out = (e0 * o0[...] + e1 * o1[...] + e2 * o2[...]) / tot
        out_ref[...] = out
        outb_ref[...] = out.astype(BF16)
        lse_ref[...] = m + jnp.log(tot)

    spec = _row_spec(4 * HD)
    f32 = jax.ShapeDtypeStruct((S, 4 * HD), F32)
    return pl.pallas_call(
        body, name="comb_fwd", grid=(S // TR,), in_specs=[spec] * 6, out_specs=[spec] * 3,
        out_shape=[f32, jax.ShapeDtypeStruct((S, 4 * HD), BF16), f32],
        compiler_params=_params(("parallel",)))(*o_g, *lse_g)


def _split2(x):
    hi = x.astype(BF16)
    return hi, (x - hi.astype(F32)).astype(BF16)


KB = 2 * BLK
SBQ = 2 * BLK


def _tri_sum(x, tri):
    hi, lo = _split2(x)
    both = _dot(jnp.concatenate([hi, lo], axis=0), tri, NN)
    return both[:x.shape[0]] + both[x.shape[0]:]


def _tri(strict_upper):
    r = lax.broadcasted_iota(jnp.int32, (KB, KB), 0)
    c = lax.broadcasted_iota(jnp.int32, (KB, KB), 1)
    return jnp.where(r > c if strict_upper else r < c, 1.0, 0.0).astype(BF16)


def _sb_terms(qs, kj, before):
    z = _dot(qs, kj, NT)
    lsp = jnp.minimum(z, 0.0) - jnp.log(1.0 + jnp.exp(-jnp.abs(z)))
    return lsp, jnp.where(before, lsp - z, 0.0)


def _sb_before(i, m):
    t = (lax.broadcasted_iota(jnp.int32, (2 * SBQ, KB), 0) & (SBQ - 1)) + i * SBQ
    s = lax.broadcasted_iota(jnp.int32, (2 * SBQ, KB), 1) + m * KB
    return s < t


C_COL = 3072 // 128


def _sb_fwd(proj):
    blk = lambda off: pl.BlockSpec((SBQ, 128), lambda p, i: (i, off + p))
    col = lambda off: pl.BlockSpec((S, 128), lambda p, i: (0, off + p))
    out = pl.BlockSpec((SBQ, 128), lambda p, i: (i, p))

    def body(q_ref, k_ref, v_ref, o_ref, ob_ref, tot_ref):
        i = pl.program_id(1)
        lo = _lane_lo(SBQ)
        qs = _stack_heads(q_ref[...] * SCALE, lo)
        suffix = _tri(True)

        def step(n, carry):
            acc, rest = carry
            m = i - n
            rows = pl.ds(pl.multiple_of(m * KB, KB), KB)
            kj, vj = k_ref[rows, :].astype(BF16), v_ref[rows, :].astype(BF16)
            before = _sb_before(i, m)
            lsp, lk = _sb_terms(qs, kj, before)
            w = jnp.where(before, jnp.exp(lsp + _tri_sum(lk, suffix) + rest), 0.0)
            return acc + _dot(w.astype(BF16), vj, NN), rest + jnp.sum(lk, axis=1, keepdims=True)

        acc, rest = lax.fori_loop(0, i + 1, step, (jnp.zeros((2 * SBQ, 128), F32), jnp.zeros((2 * SBQ, 1), F32)))
        o = _unstack_heads(acc, lo)
        o_ref[...] = o
        ob_ref[...] = o.astype(BF16)
        tot_ref[...] = _unstack_heads(rest + jnp.zeros((2 * SBQ, 128), F32), lo)

    f32 = jax.ShapeDtypeStruct((S, 4 * HD), F32)
    return pl.pallas_call(
        body, name="sb_fwd", grid=(2, S // SBQ), in_specs=[blk(C_COL), col(C_COL + 2), col(C_COL + 4)],
        out_specs=[out, out, out], out_shape=[f32, jax.ShapeDtypeStruct((S, 4 * HD), BF16), f32],
        compiler_params=_params(("parallel", "arbitrary")))(proj, proj, proj)


def _sb_bwd(proj, tot, do):
    blk = lambda off: pl.BlockSpec((SBQ, 128), lambda p, i: (i, off + p))
    col = lambda off: pl.BlockSpec((S, 128), lambda p, i: (0, off + p))

    def body(q_ref, k_ref, v_ref, tot_ref, do_ref, dq_ref, dk_ref, dv_ref):
        i = pl.program_id(1)

        @pl.when(i == 0)
        def _():
            dk_ref[...] = jnp.zeros_like(dk_ref)
            dv_ref[...] = jnp.zeros_like(dv_ref)

        lo = _lane_lo(SBQ)
        qs = _stack_heads(q_ref[...] * SCALE, lo)
        dos = _stack_heads(do_ref[...], lo)
        tots = jnp.concatenate(_per_head(tot_ref[...], lo), axis=0)
        prefix = _tri(False)

        def step(m, carry):
            dq, keep_left, g_left = carry
            rows = pl.ds(pl.multiple_of(m * KB, KB), KB)
            kj, vj = k_ref[rows, :].astype(BF16), v_ref[rows, :].astype(BF16)
            before = _sb_before(i, m)
            lsp, lk = _sb_terms(qs, kj, before)
            log_rest = tots - keep_left - lk - _tri_sum(lk, prefix)
            w = jnp.where(before, jnp.exp(lsp + log_rest), 0.0)
            g = w * _dot(dos, vj, NT)
            g_before = g_left + _dot(g.astype(BF16), prefix, NN)
            beta = jnp.exp(lsp)
            dz = jnp.where(before, g * (1.0 - beta) - g_before * beta, 0.0).astype(BF16)
            dk_ref[rows, :] += _dot(dz, qs, TN)
            dv_ref[rows, :] += _dot(w.astype(BF16), dos, TN)
            return (dq + _dot(dz, kj, NN), keep_left + jnp.sum(lk, axis=1, keepdims=True),
                    g_left + jnp.sum(g, axis=1, keepdims=True))

        zero = (jnp.zeros((2 * SBQ, 128), F32), jnp.zeros((2 * SBQ, 1), F32), jnp.zeros((2 * SBQ, 1), F32))
        dq, _, _ = lax.fori_loop(0, i + 1, step, zero)
        dq_ref[...] = SCALE * _unstack_heads(dq, lo)

    out_blk = pl.BlockSpec((SBQ, 128), lambda p, i: (i, p))
    out_col = pl.BlockSpec((S, 128), lambda p, i: (0, p))
    f32 = jax.ShapeDtypeStruct((S, 4 * HD), F32)
    return pl.pallas_call(
        body, name="sb_bwd", grid=(2, S // SBQ),
        in_specs=[blk(C_COL), col(C_COL + 2), col(C_COL + 4), out_blk, out_blk],
        out_specs=[out_blk, out_col, out_col], out_shape=[f32, f32, f32],
        compiler_params=_params(("arbitrary", "arbitrary")))(proj, proj, proj, tot, do)


TG = 256
GATE_BLK0 = OFF_GATE // TG


def _gate_specs():
    grid = (D // TG, S // TG)
    p_specs = [pl.BlockSpec((TG, TG), functools.partial(lambda c, r, br: (r, GATE_BLK0 + br * (D // TG) + c), br=br))
               for br in range(3)]
    b_spec = pl.BlockSpec((3, TG), lambda c, r: (0, c))
    t_spec = pl.BlockSpec((TG, TG), lambda c, r: (r, c))
    return grid, p_specs, b_spec, t_spec


def _sigmoid(x):
    return 1.0 / (1.0 + jnp.exp(-x))


def _three_rows(rows):
    sub = lax.broadcasted_iota(jnp.int32, (3, rows[0].shape[1]), 0)
    return jnp.where(sub == 0, rows[0], jnp.where(sub == 1, rows[1], rows[2]))


def _gate_fwd(proj, b_gate, br):
    grid, p_specs, b_spec, t_spec = _gate_specs()

    def body(p0, p1, p2, b_ref, r0, r1, r2, out_ref):
        acc = jnp.zeros((TG, TG), F32)
        for n, (p, r) in enumerate(((p0, r0), (p1, r1), (p2, r2))):
            acc += _sigmoid(p[...] + b_ref[n:n + 1, :]) * r[...]
        out_ref[...] = acc.astype(BF16)

    return pl.pallas_call(
        body, name="gate_fwd", grid=grid, in_specs=p_specs + [b_spec] + [t_spec] * 3, out_specs=t_spec,
        out_shape=jax.ShapeDtypeStruct((S, D), BF16),
        compiler_params=_params(("parallel", "parallel")))(proj, proj, proj, b_gate, *br)


def _gate_bwd(proj, b_gate, br, dmerged):
    grid, p_specs, b_spec, t_spec = _gate_specs()

    def body(p0, p1, p2, b_ref, r0, r1, r2, dm_ref, e0, e1, e2, g0, g1, g2, db_ref):
        dm = dm_ref[...]
        rows = []
        for n, (p, r, e_ref, dg_ref) in enumerate(((p0, r0, e0, g0), (p1, r1, e1, g1), (p2, r2, e2, g2))):
            g = _sigmoid(p[...] + b_ref[n:n + 1, :])
            e_ref[...] = (dm * g).astype(BF16)
            dpre = dm * r[...] * g * (1.0 - g)
            dg_ref[...] = dpre.astype(BF16)
            rows.append(jnp.sum(dpre, axis=0, keepdims=True))
        db = _three_rows(rows)

        @pl.when(pl.program_id(1) == 0)
        def _():
            db_ref[...] = db

        @pl.when(pl.program_id(1) > 0)
        def _():
            db_ref[...] += db

    bf = jax.ShapeDtypeStruct((S, D), BF16)
    out = pl.pallas_call(
        body, name="gate_bwd", grid=grid, in_specs=p_specs + [b_spec] + [t_spec] * 4,
        out_specs=[t_spec] * 6 + [b_spec], out_shape=[bf] * 6 + [jax.ShapeDtypeStruct((3, D), F32)],
        compiler_params=_params(("parallel", "arbitrary")))(proj, proj, proj, b_gate, *br, dmerged)
    return out[:3], out[3:6], out[6]


TC = 256
N_FF_BLK = D_FF // TC
GELU_C = math.sqrt(2.0 / math.pi)


def _shift_down(x, n):
    rows = lax.broadcasted_iota(jnp.int32, x.shape, 0)
    return jnp.where(rows >= n, pltpu.roll(x, n, axis=0), 0.0)


def _shift_up(x, n):
    rows = lax.broadcasted_iota(jnp.int32, x.shape, 0)
    return jnp.where(rows < x.shape[0] - n, pltpu.roll(x, x.shape[0] - n, axis=0), 0.0)


def _conv(u, w, b):
    s1, s2 = _shift_down(u, 1), _shift_down(u, 2)
    return w[2:3, :] * u + w[1:2, :] * s1 + w[0:1, :] * s2 + b, s1, s2


def _gelu_parts(x):
    inner = GELU_C * (x + 0.044715 * x * x * x)
    t = jnp.tanh(inner)
    gelu = 0.5 * x * (1.0 + t)
    dgelu = 0.5 * (1.0 + t) + 0.5 * x * (1.0 - t * t) * GELU_C * (1.0 + 3 * 0.044715 * x * x)
    return gelu, dgelu


def _conv_specs():
    ug = pl.BlockSpec((S, TC), lambda c: (0, c))
    uv = pl.BlockSpec((S, TC), lambda c: (0, N_FF_BLK + c))
    wg = pl.BlockSpec((3, TC), lambda c: (0, c))
    wv = pl.BlockSpec((3, TC), lambda c: (0, N_FF_BLK + c))
    bg = pl.BlockSpec((1, TC), lambda c: (0, c))
    bv = pl.BlockSpec((1, TC), lambda c: (0, N_FF_BLK + c))
    return ug, uv, wg, wv, bg, bv


def _conv_fwd(u, conv_w, conv_b):
    ug, uv, wg, wv, bg, bv = _conv_specs()

    def body(ug_ref, uv_ref, wg_ref, wv_ref, bg_ref, bv_ref, a_ref):
        gc = _conv(ug_ref[...], wg_ref[...], bg_ref[...])[0]
        vc = _conv(uv_ref[...], wv_ref[...], bv_ref[...])[0]
        a_ref[...] = (_gelu_parts(gc)[0] * vc).astype(BF16)

    return pl.pallas_call(
        body, name="conv_fwd", grid=(N_FF_BLK,), in_specs=[ug, uv, wg, wv, bg, bv], out_specs=ug,
        out_shape=jax.ShapeDtypeStruct((S, D_FF), BF16),
        compiler_params=_params(("parallel",)))(u, u, conv_w, conv_w, conv_b, conv_b)


def _conv_bwd(u, conv_w, conv_b, da):
    ug, uv, wg, wv, bg, bv = _conv_specs()

    def back(duc, u, s1, s2, w):
        du = w[2:3, :] * duc + w[1:2, :] * _shift_up(duc, 1) + w[0:1, :] * _shift_up(duc, 2)
        dw = _three_rows([jnp.sum(duc * s2, axis=0, keepdims=True), jnp.sum(duc * s1, axis=0, keepdims=True),
                          jnp.sum(duc * u, axis=0, keepdims=True)])
        return du, dw, jnp.sum(duc, axis=0, keepdims=True)

    def body(ug_ref, uv_ref, wg_ref, wv_ref, bg_ref, bv_ref, da_ref, dug_ref, duv_ref, dwg_ref, dwv_ref, dbg_ref, dbv_ref):
        u_g, u_v = ug_ref[...], uv_ref[...]
        gc, g1, g2 = _conv(u_g, wg_ref[...], bg_ref[...])
        vc, v1, v2 = _conv(u_v, wv_ref[...], bv_ref[...])
        gelu, dgelu = _gelu_parts(gc)
        da = da_ref[...]
        du, dw, db = back(da * vc * dgelu, u_g, g1, g2, wg_ref[...])
        dug_ref[...] = du.astype(BF16)
        dwg_ref[...] = dw
        dbg_ref[...] = db
        du, dw, db = back(da * gelu, u_v, v1, v2, wv_ref[...])
        duv_ref[...] = du.astype(BF16)
        dwv_ref[...] = dw
        dbv_ref[...] = db

    return pl.pallas_call(
        body, name="conv_bwd", grid=(N_FF_BLK,), in_specs=[ug, uv, wg, wv, bg, bv, ug],
        out_specs=[ug, ug, wg, wg, bg, bg],
        out_shape=[jax.ShapeDtypeStruct((S, D_FF), BF16), jax.ShapeDtypeStruct((S, D_FF), BF16),
                   jax.ShapeDtypeStruct((3, D_FF), F32), jax.ShapeDtypeStruct((3, D_FF), F32),
                   jax.ShapeDtypeStruct((1, D_FF), F32), jax.ShapeDtypeStruct((1, D_FF), F32)],
        compiler_params=_params(("parallel",)))(u, u, conv_w, conv_w, conv_b, conv_b, da)


def _adamw(name, w, g, m, v):
    shape = w.shape
    cols = shape[-1]
    flat = [t.reshape(-1, cols) for t in (w, g, m, v)]
    r = flat[0].shape[0]
    tr = min(128, r)

    def body(w_ref, g_ref, m_ref, v_ref, d_ref, mo_ref, vo_ref):
        g = g_ref[...]
        m = ADAM_B1 * m_ref[...] + (1.0 - ADAM_B1) * g
        v = ADAM_B2 * v_ref[...] + (1.0 - ADAM_B2) * (g * g)
        m_hat = m / (1.0 - ADAM_B1 ** ADAM_STEP)
        v_hat = v / (1.0 - ADAM_B2 ** ADAM_STEP)
        d_ref[...] = -ADAM_LR * (m_hat / (jnp.sqrt(v_hat) + ADAM_EPS) + ADAM_WD * w_ref[...])
        mo_ref[...] = m
        vo_ref[...] = v

    spec = pl.BlockSpec((tr, cols), lambda i: (i, 0))
    outs = pl.pallas_call(
        body, name=name, grid=(pl.cdiv(r, tr),), in_specs=[spec] * 4, out_specs=[spec] * 3,
        out_shape=[jax.ShapeDtypeStruct((r, cols), F32)] * 3, compiler_params=_params(("parallel",)))(*flat)
    return [t.reshape(shape) for t in outs]


def _place():
    x, y, c = lax.axis_index("x"), lax.axis_index("y"), lax.axis_index("c")
    chips = [(1 - x, y), (x, 1 - y), (1 - x, 1 - y)]
    return x, y, c, chips


def _scalars(*vals):
    return jnp.stack([jnp.asarray(v, jnp.int32) for v in vals])


HBM = pl.BlockSpec(memory_space=pltpu.HBM)
SEM = pl.BlockSpec(memory_space=pltpu.SEMAPHORE)
SPLIT_COPY = pltpu.CompilerParams(has_side_effects=pltpu.SideEffectType.DATAFLOW_SIDE_EFFECTING)


def _in_hbm(x):
    return pltpu.with_memory_space_constraint(x, pltpu.HBM)


def _cast_into_slot(name, w, layer, chip):
    _, k, n4 = w.shape
    tr = max(t for t in range(16, 257, 16) if k % t == 0)

    def body(chip_ref, w_ref, o_ref):
        o_ref[...] = w_ref[...].astype(BF16)

    return pl.pallas_call(
        body, name=name,
        grid_spec=pltpu.PrefetchScalarGridSpec(
            num_scalar_prefetch=1, grid=(k // tr,),
            in_specs=[pl.BlockSpec((None, tr, n4), lambda i, s: (layer, i, 0))],
            out_specs=pl.BlockSpec((None, tr, n4), lambda i, s: (s[0], i, 0))),
        out_shape=jax.ShapeDtypeStruct((N_CHIPS, k, n4), BF16),
        compiler_params=_params(("parallel",)))(_scalars(chip), w)


def _gather_copy(buf_ref, k, from_chip, send_sem, recv_sem, chips, c, half=False):
    rows = buf_ref.at[from_chip]
    if half:
        h = buf_ref.shape[1] // 2
        rows = buf_ref.at[from_chip, pl.ds(pl.multiple_of(c * h, h), h)]
    return pltpu.make_async_remote_copy(src_ref=rows, dst_ref=rows, send_sem=send_sem, recv_sem=recv_sem,
                                        device_id=(*chips[k], c), device_id_type=MESH)


def _gather_start(name, bufs, groups, halved=()):
    n, ng = len(bufs), len(groups)
    where = {a: (gi, e) for gi, g in enumerate(groups) for e, a in enumerate(g)}

    def body(*refs):
        ins, sems, token = refs[:n], refs[n:n + 2 * ng], refs[-1]
        x, y, c, chips = _place()
        for a in range(n):
            gi, e = where[a]
            for k in range(3):
                _gather_copy(ins[a], k, 2 * x + y, sems[2 * gi].at[3 * e + k], sems[2 * gi + 1].at[3 * e + k],
                             chips, c, a in halved).start()
        token[...] = jnp.zeros_like(token)

    out_shape = [pltpu.SemaphoreType.DMA((3 * len(g),)) for g in groups for _ in range(2)]
    out_shape += [pltpu.HBM(b.shape, b.dtype) for b in bufs] + [jax.ShapeDtypeStruct((8, 128), F32)]
    out = pl.pallas_call(
        body, name=name, in_specs=[HBM] * n,
        out_specs=[SEM] * (2 * ng) + [HBM] * n + [pl.BlockSpec(memory_space=pltpu.VMEM)], out_shape=out_shape,
        input_output_aliases={a: 2 * ng + a for a in range(n)}, compiler_params=SPLIT_COPY)(*[_in_hbm(b) for b in bufs])
    sems = [(out[2 * gi], out[2 * gi + 1]) for gi in range(ng)]
    return sems, list(out[2 * ng:2 * ng + n]), out[-1]


def _gather_wait(name, bufs, send, recv, after, halved=()):
    n = len(bufs)

    def body(*refs):
        ins, send_sem, recv_sem = refs[:n], refs[n], refs[n + 1]
        x, y, c, chips = _place()
        for e in range(n):
            for k in range(3):
                sems = (send_sem.at[3 * e + k], recv_sem.at[3 * e + k])
                _gather_copy(ins[e], k, 2 * x + y, *sems, chips, c, e in halved).wait_send()
                _gather_copy(ins[e], k, 2 * chips[k][0] + chips[k][1], *sems, chips, c, e in halved).wait_recv()

    return pl.pallas_call(
        body, name=name, in_specs=[HBM] * n + [SEM, SEM, ANY], out_specs=[HBM] * n,
        out_shape=[pltpu.HBM(b.shape, b.dtype) for b in bufs],
        input_output_aliases={a: a for a in range(n)}, compiler_params=SPLIT_COPY)(*bufs, send, recv, after)


def _swap_halves(name, bufs):
    n = len(bufs)

    def body(*refs):
        ins, outs = refs[:n], refs[n:2 * n]
        send_sem, recv_sem = refs[2 * n:]
        x, y, c, chips = _place()

        def piece(ref, k, which):
            h = ref.shape[1] // 2
            return ref.at[2 * chips[k][0] + chips[k][1], pl.ds(pl.multiple_of(which * h, h), h)]

        def copy(a, k, which):
            return pltpu.make_async_remote_copy(
                src_ref=piece(ins[a], k, c), dst_ref=piece(outs[a], k, which), send_sem=send_sem.at[3 * a + k],
                recv_sem=recv_sem.at[3 * a + k], device_id=(x, y, 1 - c), device_id_type=MESH)

        for a in range(n):
            for k in range(3):
                copy(a, k, c).start()
        for a in range(n):
            for k in range(3):
                copy(a, k, c).wait_send()
                copy(a, k, 1 - c).wait_recv()

    return pl.pallas_call(
        body, name=name, in_specs=[ANY] * n, out_specs=[ANY] * n,
        out_shape=[jax.ShapeDtypeStruct(b.shape, b.dtype) for b in bufs],
        input_output_aliases={a: a for a in range(n)},
        scratch_shapes=[pltpu.SemaphoreType.DMA((3 * n,)), pltpu.SemaphoreType.DMA((3 * n,))],
    )(*bufs)


def _reduce_copy(g_ref, land_ref, mask, send_sem, recv_sem, x, y, c, sending):
    px, py, pc = x ^ ((mask >> 2) & 1), y ^ ((mask >> 1) & 1), c ^ (mask & 1)
    half = g_ref.shape[1] // 2
    src = g_ref.at[2 * px + py, pl.ds(pl.multiple_of(pc * half, half), half)]
    dst = land_ref.at[4 * x + 2 * y + c] if sending else land_ref.at[4 * px + 2 * py + pc]
    return pltpu.make_async_remote_copy(src_ref=src, dst_ref=dst, send_sem=send_sem, recv_sem=recv_sem,
                                        device_id=(px, py, pc), device_id_type=MESH)


def _reduce_start(name, grads):
    n = len(grads)
    lands = [lax.empty((N_DEV, g.shape[1] // 2, g.shape[2]), g.dtype) for g in grads]

    def body(*refs):
        gs, ls, send_sem, recv_sem = refs[:n], refs[n:2 * n], refs[2 * n], refs[2 * n + 1]
        x, y, c, _ = _place()
        for a in range(n):
            for mask in range(1, N_DEV):
                s = (N_DEV - 1) * a + mask - 1
                _reduce_copy(gs[a], ls[a], mask, send_sem.at[s], recv_sem.at[s], x, y, c, True).start()
        refs[-1][...] = jnp.zeros_like(refs[-1])

    sem = pltpu.SemaphoreType.DMA((n * (N_DEV - 1),))
    out = pl.pallas_call(
        body, name=name, in_specs=[HBM] * (2 * n),
        out_specs=[SEM, SEM] + [HBM] * (2 * n) + [pl.BlockSpec(memory_space=pltpu.VMEM)],
        out_shape=[sem, sem] + [pltpu.HBM(t.shape, t.dtype) for t in grads + lands] + [jax.ShapeDtypeStruct((8, 128), F32)],
        input_output_aliases={a: 2 + a for a in range(2 * n)}, compiler_params=SPLIT_COPY)(
            *[_in_hbm(t) for t in grads + lands])
    return out[0], out[1], list(out[2:2 + n]), list(out[2 + n:2 + 2 * n]), out[-1]


def _reduce_wait(name, send, recv, grads, lands, after):
    n = len(grads)

    def body(*refs):
        gs, ls, send_sem, recv_sem = refs[:n], refs[n:2 * n], refs[2 * n], refs[2 * n + 1]
        x, y, c, _ = _place()
        for a in range(n):
            for mask in range(1, N_DEV):
                s = (N_DEV - 1) * a + mask - 1
                sems = (send_sem.at[s], recv_sem.at[s])
                _reduce_copy(gs[a], ls[a], mask, *sems, x, y, c, True).wait_send()
                _reduce_copy(gs[a], ls[a], mask, *sems, x, y, c, False).wait_recv()

    out = pl.pallas_call(
        body, name=name, in_specs=[HBM] * (2 * n) + [SEM, SEM, ANY], out_specs=[HBM] * (2 * n),
        out_shape=[pltpu.HBM(t.shape, t.dtype) for t in grads + lands],
        input_output_aliases={a: a for a in range(2 * n)}, compiler_params=SPLIT_COPY)(*grads, *lands, send, recv, after)
    return list(out[:n]), list(out[n:])


def _reduce_sum(name, g, land, layer, into, chip, c):
    _, k4, n4 = g.shape
    half = k4 // 2
    tr = max(t for t in range(16, 513, 16) if half % t == 0)
    per = half // tr
    me = 2 * chip + c

    def body(s_ref, own_ref, *refs):
        total = own_ref[...].astype(F32)
        for ref in refs[:N_DEV - 1]:
            total = total + ref[...].astype(F32)
        refs[-1][...] = total

    in_specs = [pl.BlockSpec((None, tr, n4), lambda i, s: (s[0], s[1] * per + i, 0))]
    in_specs += [pl.BlockSpec((None, tr, n4), functools.partial(lambda i, s, m: (s[1 + m], i, 0), m=m))
                 for m in range(1, N_DEV)]
    ins = [g] + [land] * (N_DEV - 1)
    aliases = {}
    if into is not None:
        in_specs, ins, aliases = in_specs + [ANY], ins + [into], {1 + N_DEV: 0}
    return pl.pallas_call(
        body, name=name,
        grid_spec=pltpu.PrefetchScalarGridSpec(
            num_scalar_prefetch=1, grid=(per,), in_specs=in_specs,
            out_specs=pl.BlockSpec((None, tr, n4), lambda i, s: (layer, s[1] * per + i, 0))),
        out_shape=jax.ShapeDtypeStruct((DEPTH, k4, n4), F32), input_output_aliases=aliases,
        compiler_params=_params(("parallel",)))(_scalars(chip, c, *[me ^ m for m in range(1, N_DEV)]), *ins)


def _join_halves(name, bufs):
    n = len(bufs)

    def body(*refs):
        ins, outs = refs[:n], refs[n:2 * n]
        send_sem, recv_sem = refs[2 * n:]
        x, y, c, _ = _place()

        def rows(ref, which):
            half = ref.shape[1] // 2
            return ref.at[:, pl.ds(pl.multiple_of(which * half, half), half)]

        sends = [pltpu.make_async_remote_copy(
            src_ref=rows(ins[a], c), dst_ref=rows(outs[a], c), send_sem=send_sem.at[a], recv_sem=recv_sem.at[a],
            device_id=(x, y, 1 - c), device_id_type=MESH) for a in range(n)]
        for cp in sends:
            cp.start()
        for a in range(n):
            sends[a].wait_send()
            pltpu.make_async_remote_copy(
                src_ref=rows(ins[a], c), dst_ref=rows(outs[a], 1 - c), send_sem=send_sem.at[a], recv_sem=recv_sem.at[a],
                device_id=(x, y, 1 - c), device_id_type=MESH).wait_recv()

    return pl.pallas_call(
        body, name=name, in_specs=[ANY] * n, out_specs=[ANY] * n,
        out_shape=[jax.ShapeDtypeStruct(b.shape, b.dtype) for b in bufs],
        input_output_aliases={a: a for a in range(n)},
        scratch_shapes=[pltpu.SemaphoreType.DMA((n,)), pltpu.SemaphoreType.DMA((n,))],
    )(*bufs)


def _all_reduce_small(block):
    r = block.shape[0]

    def body(x_ref, out_ref, slots, send_sem, recv_sem):
        x, y, c, _ = _place()
        me = 4 * x + 2 * y + c
        slots[me] = x_ref[...]
        sends = []
        for mask in range(1, N_DEV):
            fx, fy, fc = (mask >> 2) & 1, (mask >> 1) & 1, mask & 1
            peer = (x ^ fx, y ^ fy, c ^ fc)
            cp = pltpu.make_async_remote_copy(
                src_ref=x_ref, dst_ref=slots.at[me], send_sem=send_sem.at[mask - 1], recv_sem=recv_sem.at[mask - 1],
                device_id=peer, device_id_type=MESH)
            cp.start()
            sends.append(cp)
        for mask in range(1, N_DEV):
            src = me ^ mask
            pltpu.make_async_remote_copy(
                src_ref=x_ref, dst_ref=slots.at[src], send_sem=send_sem.at[mask - 1], recv_sem=recv_sem.at[mask - 1],
                device_id=(x, y, c), device_id_type=MESH).wait_recv()
        for cp in sends:
            cp.wait_send()
        total = slots[0]
        for d in range(1, N_DEV):
            total = total + slots[d]
        out_ref[...] = total

    vmem = pl.BlockSpec(memory_space=pltpu.VMEM)
    return pl.pallas_call(
        body, name="all_reduce_small", in_specs=[vmem], out_specs=vmem,
        out_shape=jax.ShapeDtypeStruct((r, 128), F32),
        scratch_shapes=[pltpu.VMEM((N_DEV, r, 128), F32), pltpu.SemaphoreType.DMA((N_DEV - 1,)),
                        pltpu.SemaphoreType.DMA((N_DEV - 1,))],
        compiler_params=pltpu.CompilerParams(vmem_limit_bytes=VMEM_LIMIT))(block)


B_Q_COL = 2304 // 128
B_K0, B_V0, B_END = 2816, 2944, 3072


def _full_cols(w_g):
    return w_g.transpose(1, 0, 2).reshape(w_g.shape[1], -1)


def _group_src(proj, g):
    if A_GROUPS[g][1] == 1:
        return ((proj, 2 * g), (proj, 6 + 2 * g), (proj, 12 + 2 * g))
    packed = jnp.concatenate([proj[:, t * 768 + g * 256:t * 768 + (g + 1) * 256] for t in range(3)], axis=1)
    return ((packed, 0), (packed, 2), (packed, 4))


def _kv_expand(kv):
    return jnp.broadcast_to(kv.reshape(S, 2, 1, HD), (S, 2, 4, HD)).reshape(S, 8 * HD)


def _kv_reduce(dkv):
    return dkv.reshape(S, 2, 4, HD).sum(axis=2).reshape(S, 2 * HD)


def _mixer_fwd(h1, wget, rel_bias, sinks_l, bidx):
    w = dict(wget(0, h1))
    proj = _mm_nt("proj_in", h1, w["w_in"], F32, tm=S, tn=1152)
    no_sinks = jnp.full((4,), NEG, F32)
    srcs = [_group_src(proj, g) for g in range(3)]
    o_g, lse_g = [], []
    for g, (_, d) in enumerate(A_GROUPS):
        o, lse = _band_fwd("band_fwd_g%d" % g, d, 2, BLK, 4 * g, srcs[g], bidx[g], rel_bias, no_sinks)
        o_g.append(o)
        lse_g.append(lse)
    o_a32, o_a, lse_a = _comb_fwd(o_g, lse_g)
    src_b = ((proj, B_Q_COL), (_kv_expand(proj[:, B_K0:B_V0]), 0), (_kv_expand(proj[:, B_V0:B_END]), 0))
    o_b32, lse_b = _band_fwd("band_fwd_b", 1, 4, BLK - 1, N_A, src_b, bidx[3], rel_bias, sinks_l)
    o_b = o_b32.astype(BF16)
    o_c32, o_c, tot_c = _sb_fwd(proj)
    w.update(wget(1, o_c32))
    br = [_mm_nn("branch_a", o_a, w["w_br_a"], F32), _mm_nn("branch_b", o_b, w["w_br_b"], F32),
          _mm_nn("branch_c", o_c, w["w_br_c"], F32)]
    merged = _gate_fwd(proj, w["b_gate"], br)
    mo = _mm_nn("out_proj", merged, w["w_out"], F32, tm=S)
    saved = dict(proj=proj, srcs=srcs, src_b=src_b, o_a32=o_a32, lse_a=lse_a, o_b32=o_b32, lse_b=lse_b, tot_c=tot_c,
                 o_a=o_a, o_b=o_b, o_c=o_c, br=br, merged=merged)
    return mo, saved, w


def _mixer_bwd(d_mo, h1, w, sv, rel_bias, sinks_l, bidx, stats_in, emit):
    grads = {}
    dmerged = _mm_nt("out_proj_dx", d_mo, w["w_out"], F32, tm=S)
    grads["w_out"] = _mm_tn_sharded("out_proj_dw", sv["merged"], d_mo, True)
    e, dgate, db_gate = _gate_bwd(sv["proj"], w["b_gate"], sv["br"], dmerged)
    grads["b_gate"] = db_gate
    d_o = {}
    for n, name in enumerate("abc"):
        d_o[name] = _mm_nt("branch_%s_dx" % name, e[n], w["w_br_" + name], F32)
        grads["w_br_" + name] = _mm_tn_sharded("branch_%s_dw" % name, sv["o_" + name], e[n], False)
    zero = emit(1, grads)
    no_sinks = jnp.full((4,), NEG, F32) + zero[0]
    dqs, dks, dvs, stats = [], [], [], []
    for g, (_, d) in enumerate(A_GROUPS):
        dq, dk, dv, st = _band_bwd("band_bwd_g%d" % g, d, 2, BLK, 4 * g, sv["srcs"][g], bidx[g], rel_bias, no_sinks,
                                   sv["o_a32"], sv["lse_a"], d_o["a"], stats_in[4 * g:4 * g + 4])
        dqs.append(dq)
        dks.append(dk)
        dvs.append(dv)
        stats.append(st)
    dq_b, dk_x, dv_x, st = _band_bwd("band_bwd_b", 1, 4, BLK - 1, N_A, sv["src_b"], bidx[3], rel_bias, sinks_l,
                                     sv["o_b32"], sv["lse_b"], d_o["b"], stats_in[N_A:])
    stats = jnp.concatenate(stats + [st], axis=0)
    dcq, dck, dcv = _sb_bwd(sv["proj"], sv["tot_c"], d_o["c"])
    cols = dqs + dks + dvs + [dq_b, _kv_reduce(dk_x), _kv_reduce(dv_x), dcq, dck, dcv]
    dproj = jnp.concatenate([t.astype(BF16) for t in cols] + list(dgate), axis=1)
    grads["w_in"] = _mm_tn("proj_in_dw", dproj, h1, BF16, tm=1152, tn=1024).reshape(N_CHIPS, IN_SHARD, D)
    zero = emit(2, grads)
    dh1 = _mm_nn("proj_in_dx", dproj, w["w_in"], F32)
    return dh1, grads, stats, zero


def _ffn_fwd(h2, w):
    u = _mm_nn("ffn_up", h2, w["w_up"], F32, tm=S, tn=1024)
    a = _conv_fwd(u, w["conv_w"], w["conv_b"])
    dn = _mm_nn("ffn_down", a, w["w_down"], F32, tm=1024)
    return dn, dict(u=u, a=a)


def _ffn_bwd(d_dn, h2, w, sv):
    grads = {}
    da = _mm_nt("ffn_down_dx", d_dn, w["w_down"], F32, tm=S, tn=1024)
    grads["w_down"] = _mm_tn_sharded("ffn_down_dw", sv["a"], d_dn, True, tm=1024, tn=1024)
    dug, duv, dwg, dwv, dbg, dbv = _conv_bwd(sv["u"], w["conv_w"], w["conv_b"], da)
    du = jnp.concatenate([dug, duv], axis=1)
    grads["conv_w"] = jnp.concatenate([dwg, dwv], axis=1)
    grads["conv_b"] = jnp.concatenate([dbg, dbv], axis=1)
    dh2 = _mm_nt("ffn_up_dx", du, w["w_up"], F32)
    grads["w_up"] = _mm_tn_sharded("ffn_up_dw", h2, du, False, tm=1024, tn=1024)
    return dh2, grads


BIG = ("w_in", "w_br_a", "w_br_b", "w_br_c", "w_out", "w_up", "w_down")


def _shard_view(name, w):
    return jnp.swapaxes(w, 1, 2) if name == "w_in" else w
WEIGHT_GROUPS = (("w_in", "b_gate"), ("w_br_a", "w_br_b", "w_br_c", "w_out"), ("w_up", "conv_w", "w_down"))
GRAD_GROUPS = (("w_down", "w_up"), ("w_out", "w_br_a", "w_br_b", "w_br_c"), ("w_in",))
SMALL_ROWS = (("rel_bias", 8), ("attn_pre_norm", 16), ("attn_post_norm", 16), ("ffn_pre_norm", 16), ("ffn_post_norm", 16),
              ("sinks", 8), ("conv_b", 128), ("b_gate", 48), ("conv_w", 384), ("loss", 8))


def _pack_small(vals):
    rows = []
    for name, n in SMALL_ROWS:
        flat = vals[name].reshape(-1).astype(F32)
        rows.append(jnp.pad(flat, (0, n * 128 - flat.shape[0])).reshape(n, 128))
    return jnp.concatenate(rows, axis=0)


def _unpack_small(block, shapes):
    out, row = {}, 0
    for name, n in SMALL_ROWS:
        size = int(np.prod(shapes[name]))
        out[name] = block[row:row + n].reshape(-1)[:size].reshape(shapes[name])
        row += n
    return out


def kernel(x, rel_bias, attn_pre_norm, w_in, b_gate, sinks, w_br_a, w_br_b, w_br_c, w_out, attn_post_norm, ffn_pre_norm, w_up, conv_w, conv_b, w_down, ffn_post_norm, loss_target, m_rel_bias, m_attn_pre_norm, m_w_in, m_b_gate, m_sinks, m_w_br_a, m_w_br_b, m_w_br_c, m_w_out, m_attn_post_norm, m_ffn_pre_norm, m_w_up, m_conv_w, m_conv_b, m_w_down, m_ffn_post_norm, v_rel_bias, v_attn_pre_norm, v_w_in, v_b_gate, v_sinks, v_w_br_a, v_w_br_b, v_w_br_c, v_w_out, v_attn_post_norm, v_ffn_pre_norm, v_w_up, v_conv_w, v_conv_b, v_w_down, v_ffn_post_norm):
    names = ("rel_bias", "attn_pre_norm", "w_in", "b_gate", "sinks", "w_br_a", "w_br_b", "w_br_c", "w_out",
             "attn_post_norm", "ffn_pre_norm", "w_up", "conv_w", "conv_b", "w_down", "ffn_post_norm")
    weights = dict(zip(names, (rel_bias, attn_pre_norm, w_in, b_gate, sinks, w_br_a, w_br_b, w_br_c, w_out,
                               attn_post_norm, ffn_pre_norm, w_up, conv_w, conv_b, w_down, ffn_post_norm)))
    mom1 = dict(zip(names, (m_rel_bias, m_attn_pre_norm, m_w_in, m_b_gate, m_sinks, m_w_br_a, m_w_br_b, m_w_br_c,
                            m_w_out, m_attn_post_norm, m_ffn_pre_norm, m_w_up, m_conv_w, m_conv_b, m_w_down,
                            m_ffn_post_norm)))
    mom2 = dict(zip(names, (v_rel_bias, v_attn_pre_norm, v_w_in, v_b_gate, v_sinks, v_w_br_a, v_w_br_b, v_w_br_c,
                            v_w_out, v_attn_post_norm, v_ffn_pre_norm, v_w_up, v_conv_w, v_conv_b, v_w_down,
                            v_ffn_post_norm)))

    chip = 2 * lax.axis_index("x") + lax.axis_index("y")
    core = lax.axis_index("c")

    keys = [(n, l) for l in range(DEPTH) for group in WEIGHT_GROUPS for n in group]
    groups = [[keys.index((n, l)) for n in group] for l in range(DEPTH) for group in WEIGHT_GROUPS]

    def slot_buffer(n, l):
        if n in BIG:
            return _cast_into_slot("cast_" + n, _shard_view(n, weights[n]), l, chip)
        shard = weights[n][l]
        return lax.dynamic_update_slice(jnp.zeros((N_CHIPS,) + shard.shape, F32), shard[None],
                                        (chip, jnp.int32(0), jnp.int32(0)))

    first = keys.index(("w_in", 0))
    sems, in_flight, _ = _gather_start("gather_start", [slot_buffer(*k) for k in keys], groups, (first,))

    def wget(l, gi, after):
        g = l * len(WEIGHT_GROUPS) + gi
        halved = tuple(e for e, a in enumerate(groups[g]) if a == first)
        got = list(_gather_wait("gather_wait_%d_%d" % (l, gi), [in_flight[a] for a in groups[g]], *sems[g], after,
                                halved))
        for e in halved:
            got[e] = _swap_halves("swap_halves", [got[e]])[0]
        out = {}
        for n, buf in zip(WEIGHT_GROUPS[gi], got):
            out[n] = buf.reshape(-1, buf.shape[-1]) if n in ("w_in", "w_out", "w_down") else _full_cols(buf)
        if gi == len(WEIGHT_GROUPS) - 1:
            out["conv_b"] = conv_b[l:l + 1]
        return out

    pending = []

    def emit(l, gi, grads):
        group = GRAD_GROUPS[gi]
        *started, token = _reduce_start("reduce_start_%d_%d" % (l, gi), [grads[n] for n in group])
        pending.append((l, group) + tuple(started))
        return token[:1, :1]

    local = _local_step(x.reshape(S, D), loss_target.reshape(S, D), wget, emit, rel_bias, sinks, attn_pre_norm,
                        attn_post_norm, ffn_pre_norm, ffn_post_norm)
    return _reduce_and_update(x.shape, names, weights, mom1, mom2, chip, core, pending, *local)


def _local_step(xs, target, wget, emit, rel_bias, sinks, attn_pre_norm, attn_post_norm, ffn_pre_norm, ffn_post_norm):
    bidx = jnp.asarray(_bucket_maps())

    saved, layers = [], []
    h1 = _rms_fwd("pre_norm_first", xs, attn_pre_norm[0:1])
    x_in = xs
    for l in range(DEPTH):
        mo, sv_mix, w = _mixer_fwd(h1, functools.partial(wget, l), rel_bias, sinks[l], bidx)
        x_mid, h2 = _post_pre_fwd("post_attn_norm", x_in, mo, attn_post_norm[l:l + 1], ffn_pre_norm[l:l + 1])
        w.update(wget(l, 2, h2))
        dn, sv_ffn = _ffn_fwd(h2, w)
        g_next = attn_pre_norm[l + 1:l + 2] if l + 1 < DEPTH else None
        x_out, h1_next = _post_pre_fwd("post_ffn_norm" if l + 1 < DEPTH else "post_ffn_norm_last", x_mid, dn,
                                       ffn_post_norm[l:l + 1], g_next)
        saved.append(dict(x_in=x_in, h1=h1, mo=mo, x_mid=x_mid, h2=h2, dn=dn, mix=sv_mix, ffn=sv_ffn))
        layers.append(w)
        x_in, h1 = x_out, h1_next

    loss_row, dres = _loss_kernel(x_in, target)

    small = [None] * DEPTH
    stats = jnp.zeros((N_BAND_Q, 8, 128), F32)
    dh_next = None
    for l in reversed(range(DEPTH)):
        w, sv = layers[l], saved[l]
        if l + 1 < DEPTH:
            pre = (saved[l + 1]["x_in"], attn_pre_norm[l + 1:l + 2] + zero, dh_next)
            dres, d_dn, dg_pre_next, dg_fpost = _norm_bwd("post_ffn_norm_bwd", dres, pre,
                                                          (sv["dn"], ffn_post_norm[l:l + 1]))
            small[l + 1]["attn_pre_norm"] = dg_pre_next
        else:
            dres, d_dn, _, dg_fpost = _norm_bwd("post_ffn_norm_last_bwd", dres, None, (sv["dn"], ffn_post_norm[l:l + 1]))
        dh2, g_ffn = _ffn_bwd(d_dn, sv["h2"], w, sv["ffn"])
        zero = emit(l, 0, g_ffn)
        dres, d_mo, dg_fpre, dg_apost = _norm_bwd("post_attn_norm_bwd", dres,
                                                  (sv["x_mid"], ffn_pre_norm[l:l + 1] + zero, dh2),
                                                  (sv["mo"], attn_post_norm[l:l + 1]))
        dh_next, g_mix, stats, zero = _mixer_bwd(d_mo, sv["h1"], w, sv["mix"], rel_bias, sinks[l], bidx, stats,
                                                 functools.partial(emit, l))
        small[l] = dict(ffn_post_norm=dg_fpost, ffn_pre_norm=dg_fpre, attn_post_norm=dg_apost,
                        sinks=stats[N_A:, 1, 0], conv_b=g_ffn["conv_b"], b_gate=g_mix["b_gate"], conv_w=g_ffn["conv_w"])
    grad_x, _, dg_pre0, _ = _norm_bwd("pre_norm_first_bwd", dres, (saved[0]["x_in"], attn_pre_norm[0:1] + zero, dh_next),
                                      None)
    small[0]["attn_pre_norm"] = dg_pre0
    return loss_row, grad_x, small, stats


def _reduce_and_update(x_shape, names, weights, mom1, mom2, chip, core, pending, loss_row, grad_x, small, stats):
    delta, new_m, new_v, grads = {}, {}, {}, {}

    def update(n, g):
        grads[n] = g
        delta[n], new_m[n], new_v[n] = _adamw("adamw_" + n, _shard_view(n, weights[n]), g,
                                              _shard_view(n, mom1[n]), _shard_view(n, mom2[n]))

    summed = {}

    def finish(which, after):
        for l, group, send, recv, gs, lands in pending:
            if (group == ("w_in",)) == which:
                gs, lands = _reduce_wait("reduce_wait_%d_%s" % (l, group[0]), send, recv, gs, lands, after)
                for n, g, land in zip(group, gs, lands):
                    summed[n] = _reduce_sum("reduce_sum_%d_%s" % (l, n), g, land, l, summed.get(n), chip, core)

    finish(False, grad_x)
    early = [n for n in BIG if n != "w_in"]
    for n, g in zip(early, _join_halves("join_halves", [summed[n] for n in early])):
        update(n, g)
    finish(True, delta[early[-1]])
    update("w_in", _join_halves("join_halves_w_in", [summed["w_in"]])[0])
    for out in (grads, delta, new_m, new_v):
        out["w_in"] = _shard_view("w_in", out["w_in"])

    small_vals = {n: jnp.stack([small[l][n].reshape(weights[n].shape[1:]) for l in range(DEPTH)])
                  for n in ("attn_pre_norm", "attn_post_norm", "ffn_pre_norm", "ffn_post_norm", "conv_b", "sinks")}
    small_vals["b_gate"] = jnp.stack([small[l]["b_gate"] for l in range(DEPTH)])
    small_vals["conv_w"] = jnp.stack([small[l]["conv_w"] for l in range(DEPTH)])
    small_vals["rel_bias"] = stats[:, 0, :NUM_BUCKETS].T
    small_vals["loss"] = loss_row[0, :1]
    shapes = {n: v.shape for n, v in small_vals.items()}
    packed, delta["w_in"] = lax.optimization_barrier((_pack_small(small_vals), delta["w_in"]))
    reduced = _unpack_small(_all_reduce_small(packed), shapes)
    reduced["b_gate"] = lax.dynamic_slice_in_dim(reduced["b_gate"], chip * (D // N_CHIPS), D // N_CHIPS, axis=2)
    reduced["conv_w"] = lax.dynamic_slice_in_dim(reduced["conv_w"], chip * (2 * D_FF // N_CHIPS), 2 * D_FF // N_CHIPS, axis=2)
    for n in names:
        if n not in grads:
            update(n, reduced[n].reshape(weights[n].shape))

    loss = reduced["loss"].reshape(())
    return (loss, grad_x.reshape(x_shape), *[grads[n] for n in names], *[delta[n] for n in names],
            *[new_m[n] for n in names], *[new_v[n] for n in names])
```

```python
import functools
import math

import numpy as np
import jax
import jax.numpy as jnp
from jax import lax
from jax.experimental import pallas as pl
from jax.experimental.pallas import tpu as pltpu

F32 = jnp.float32
BF16 = jnp.bfloat16

S = 2048
D = 1024
DEPTH = 2
HD = 64
BLK = 128
NQB = S // BLK
A_GROUPS = ((128, 1), (512, 4), (2048, 16))
N_BAND_Q = 20
N_A = 12
NUM_BUCKETS = 32
MAX_DISTANCE = 2048
D_FF = 4096
IN_COLS = 6912
IN_SHARD = IN_COLS // 4
OFF_GATE = 3840
EPS = 1e-6
SCALE = HD ** -0.5
NEG = -1e30
N_CHIPS = 4
N_DEV = 8

ADAM_LR = 0.001
ADAM_B1 = 0.9
ADAM_B2 = 0.999
ADAM_EPS = 1e-08
ADAM_WD = 0.01
ADAM_STEP = 10

VMEM_LIMIT = 56 * 1024 * 1024

NN = (((1,), (0,)), ((), ()))
NT = (((1,), (1,)), ((), ()))
TN = (((0,), (0,)), ((), ()))

MESH = pl.DeviceIdType.MESH
ANY = pl.BlockSpec(memory_space=pl.ANY)


def _dot(a, b, dims):
    return lax.dot_general(a, b, dims, preferred_element_type=F32)


def _params(sem):
    return pltpu.CompilerParams(dimension_semantics=sem, vmem_limit_bytes=VMEM_LIMIT)


def _matmul(name, a, b, out_shape, out_dtype, grid, a_spec, b_spec, o_spec, dims, acc_shape):
    nk = grid[-1]

    def body(a_ref, b_ref, o_ref, *scratch):
        part = _dot(a_ref[...].astype(BF16), b_ref[...].astype(BF16), dims)
        if nk == 1:
            o_ref[...] = part.astype(o_ref.dtype)
            return
        acc_ref, = scratch
        k = pl.program_id(len(grid) - 1)

        @pl.when(k == 0)
        def _():
            acc_ref[...] = part

        @pl.when(k > 0)
        def _():
            acc_ref[...] += part

        @pl.when(k == nk - 1)
        def _():
            o_ref[...] = acc_ref[...].astype(o_ref.dtype)

    scratch = [] if nk == 1 else [pltpu.VMEM(acc_shape, F32)]
    sem = ("parallel",) * (len(grid) - 1) + ("arbitrary",)
    return pl.pallas_call(
        body, name=name, grid=grid, in_specs=[a_spec, b_spec], out_specs=o_spec,
        out_shape=jax.ShapeDtypeStruct(out_shape, out_dtype), scratch_shapes=scratch,
        compiler_params=_params(sem))(a, b)


FULL_K = 8192


def _mm_tn_sharded(name, a, b, row_sharded, tm=512, tn=512, tk=FULL_K):
    k, m = a.shape
    n = b.shape[1]
    m4, n4 = (m // N_CHIPS, n) if row_sharded else (m, n // N_CHIPS)
    tm, tn, tk = min(tm, m4), min(tn, n4), min(tk, k)
    per_m, per_n = m4 // tm, n4 // tn
    if row_sharded:
        o_map = lambda i, j, l: (i // per_m, i % per_m, j)
    else:
        o_map = lambda i, j, l: (j // per_n, i, j % per_n)
    return _matmul(name, a, b, (N_CHIPS, m4, n4), BF16, (m // tm, n // tn, k // tk),
                   pl.BlockSpec((tk, tm), lambda i, j, l: (l, i)),
                   pl.BlockSpec((tk, tn), lambda i, j, l: (l, j)),
                   pl.BlockSpec((None, tm, tn), o_map), TN, (tm, tn))


def _mm_nn(name, a, b, out_dtype, tm=512, tn=512, tk=FULL_K):
    m, k = a.shape
    n = b.shape[1]
    tm, tn, tk = min(tm, m), min(tn, n), min(tk, k)
    return _matmul(name, a, b, (m, n), out_dtype, (m // tm, n // tn, k // tk),
                   pl.BlockSpec((tm, tk), lambda i, j, l: (i, l)),
                   pl.BlockSpec((tk, tn), lambda i, j, l: (l, j)),
                   pl.BlockSpec((tm, tn), lambda i, j, l: (i, j)), NN, (tm, tn))


def _mm_nt(name, a, b, out_dtype, tm=512, tn=512, tk=FULL_K):
    m, k = a.shape
    n = b.shape[0]
    tm, tn, tk = min(tm, m), min(tn, n), min(tk, k)
    return _matmul(name, a, b, (m, n), out_dtype, (m // tm, n // tn, k // tk),
                   pl.BlockSpec((tm, tk), lambda i, j, l: (i, l)),
                   pl.BlockSpec((tn, tk), lambda i, j, l: (j, l)),
                   pl.BlockSpec((tm, tn), lambda i, j, l: (i, j)), NT, (tm, tn))


def _mm_tn(name, a, b, out_dtype, tm=512, tn=512, tk=FULL_K):
    k, m = a.shape
    n = b.shape[1]
    tm, tn, tk = min(tm, m), min(tn, n), min(tk, k)
    return _matmul(name, a, b, (m, n), out_dtype, (m // tm, n // tn, k // tk),
                   pl.BlockSpec((tk, tm), lambda i, j, l: (l, i)),
                   pl.BlockSpec((tk, tn), lambda i, j, l: (l, j)),
                   pl.BlockSpec((tm, tn), lambda i, j, l: (i, j)), TN, (tm, tn))


TR = 512


def _row_spec(width=D):
    return pl.BlockSpec((TR, width), lambda i: (i, 0))


def _vec_spec(width=D):
    return pl.BlockSpec((1, width), lambda i: (0, 0))


def _rms(x, g):
    r = lax.rsqrt(jnp.mean(x * x, axis=-1, keepdims=True) + EPS)
    return x * r * g


def _rms_fwd(name, x, g):
    def body(x_ref, g_ref, h_ref):
        h_ref[...] = _rms(x_ref[...], g_ref[...]).astype(BF16)

    return pl.pallas_call(
        body, name=name, grid=(S // TR,), in_specs=[_row_spec(), _vec_spec()], out_specs=_row_spec(),
        out_shape=jax.ShapeDtypeStruct((S, D), BF16), compiler_params=_params(("parallel",)))(x, g)


def _post_pre_fwd(name, x, y, g_post, g_pre):
    has_pre = g_pre is not None

    def body(*refs):
        if has_pre:
            x_ref, y_ref, gp_ref, gn_ref, xn_ref, h_ref = refs
        else:
            x_ref, y_ref, gp_ref, xn_ref = refs
        xn = x_ref[...] + _rms(y_ref[...], gp_ref[...])
        xn_ref[...] = xn
        if has_pre:
            h_ref[...] = _rms(xn, gn_ref[...]).astype(BF16)

    ins = [x, y, g_post] + ([g_pre] if has_pre else [])
    in_specs = [_row_spec(), _row_spec(), _vec_spec()] + ([_vec_spec()] if has_pre else [])
    out_shape = [jax.ShapeDtypeStruct((S, D), F32)] + ([jax.ShapeDtypeStruct((S, D), BF16)] if has_pre else [])
    out_specs = [_row_spec()] + ([_row_spec()] if has_pre else [])
    out = pl.pallas_call(
        body, name=name, grid=(S // TR,), in_specs=in_specs, out_specs=out_specs, out_shape=out_shape,
        compiler_params=_params(("parallel",)))(*ins)
    return out if has_pre else (out[0], None)


def _rms_bwd_math(x, g, dy):
    r = lax.rsqrt(jnp.mean(x * x, axis=-1, keepdims=True) + EPS)
    n = x * r
    dn = dy * g
    dx = r * (dn - n * jnp.mean(dn * n, axis=-1, keepdims=True))
    return dx, jnp.sum(dy * n, axis=0, keepdims=True)


def _norm_bwd(name, dres, pre=None, post=None):
    has_pre, has_post = pre is not None, post is not None

    def body(*refs):
        refs = list(refs)
        dres_ref = refs.pop(0)
        if has_pre:
            xn_ref, gn_ref, dh_ref = refs[:3]
            refs = refs[3:]
        if has_post:
            y_ref, gp_ref = refs[:2]
            refs = refs[2:]
        dxn_ref = refs.pop(0)
        dy_ref = refs.pop(0) if has_post else None
        dgn_ref = refs.pop(0) if has_pre else None
        dgp_ref = refs.pop(0) if has_post else None
        first = pl.program_id(0) == 0
        dxn = dres_ref[...]
        if has_pre:
            dx, dg = _rms_bwd_math(xn_ref[...], gn_ref[...], dh_ref[...])
            dxn = dxn + dx

            @pl.when(first)
            def _():
                dgn_ref[...] = dg

            @pl.when(jnp.logical_not(first))
            def _():
                dgn_ref[...] += dg
        dxn_ref[...] = dxn
        if has_post:
            dy, dg = _rms_bwd_math(y_ref[...], gp_ref[...], dxn)
            dy_ref[...] = dy.astype(BF16)

            @pl.when(first)
            def _():
                dgp_ref[...] = dg

            @pl.when(jnp.logical_not(first))
            def _():
                dgp_ref[...] += dg

    ins, in_specs = [dres], [_row_spec()]
    if has_pre:
        ins += list(pre)
        in_specs += [_row_spec(), _vec_spec(), _row_spec()]
    if has_post:
        ins += list(post)
        in_specs += [_row_spec(), _vec_spec()]
    out_shape, out_specs = [jax.ShapeDtypeStruct((S, D), F32)], [_row_spec()]
    if has_post:
        out_shape.append(jax.ShapeDtypeStruct((S, D), BF16))
        out_specs.append(_row_spec())
    for _ in range(int(has_pre) + int(has_post)):
        out_shape.append(jax.ShapeDtypeStruct((1, D), F32))
        out_specs.append(_vec_spec())
    out = list(pl.pallas_call(
        body, name=name, grid=(S // TR,), in_specs=in_specs, out_specs=out_specs, out_shape=out_shape,
        compiler_params=_params(("arbitrary",)))(*ins))
    dxn = out.pop(0)
    dy = out.pop(0) if has_post else None
    dgn = out.pop(0) if has_pre else None
    dgp = out.pop(0) if has_post else None
    return dxn, dy, dgn, dgp


def _loss_kernel(y, target):
    def body(y_ref, t_ref, loss_ref, dy_ref):
        e = y_ref[...] - t_ref[...]
        dy_ref[...] = e * (1.0 / D)
        part = jnp.zeros((1, 128), F32) + 0.5 * jnp.sum(jnp.mean(e * e, axis=-1, keepdims=True))

        @pl.when(pl.program_id(0) == 0)
        def _():
            loss_ref[...] = part

        @pl.when(pl.program_id(0) > 0)
        def _():
            loss_ref[...] += part

    return pl.pallas_call(
        body, name="loss", grid=(S // TR,), in_specs=[_row_spec(), _row_spec()],
        out_specs=[_vec_spec(128), _row_spec()],
        out_shape=[jax.ShapeDtypeStruct((1, 128), F32), jax.ShapeDtypeStruct((S, D), F32)],
        compiler_params=_params(("arbitrary",)))(y, target)


def _t5_bucket_np(dist):
    max_exact = NUM_BUCKETS // 2
    nf = np.maximum(dist, 1).astype(np.float32)
    large = max_exact + (np.log(nf / max_exact) / np.float32(math.log(MAX_DISTANCE / max_exact))
                         * (NUM_BUCKETS - max_exact)).astype(np.int32)
    large = np.minimum(large, NUM_BUCKETS - 1)
    return np.where(dist < max_exact, dist, large).astype(np.int32)


def _bucket_maps():
    a = np.arange(BLK)[:, None]
    b = np.arange(2 * BLK)[None, :]
    dist = np.maximum(a + BLK - b, 0)
    maps = [_t5_bucket_np(dist * d) for _, d in A_GROUPS] + [_t5_bucket_np(dist)]
    return np.stack(maps).astype(np.int32)


def _classes(arr, d):
    return arr.reshape(S // d, d * arr.shape[1])


def _class_spec(arr, col0, d):
    ncol = arr.shape[1] // 128
    return pl.BlockSpec((S // d, 128), lambda p, r: (0, r * ncol + col0 + p))


def _band_rows(b):
    return (pl.ds(pl.multiple_of(b * BLK, BLK), BLK), pl.ds(pl.multiple_of(jnp.maximum(b - 1, 0) * BLK, BLK), BLK))


def _band_bias(tab_ref, bidx_ref, h):
    bi = bidx_ref[...]
    bias = jnp.zeros((BLK, 2 * BLK), F32)
    for kk in range(NUM_BUCKETS):
        bias = jnp.where(bi == kk, tab_ref[kk, h], bias)
    return bias


def _lane_lo(rows=BLK):
    return lax.broadcasted_iota(jnp.int32, (rows, 128), 1) < HD


def _per_head(x, lo):
    return (jnp.sum(jnp.where(lo, x, 0.0), axis=1, keepdims=True) * (1.0 / HD),
            jnp.sum(jnp.where(lo, 0.0, x), axis=1, keepdims=True) * (1.0 / HD))


def _band_fill(bias_ref, tab_ref, bidx_ref, head, maxd):
    a = lax.broadcasted_iota(jnp.int32, (BLK, 2 * BLK), 0)
    c = lax.broadcasted_iota(jnp.int32, (BLK, 2 * BLK), 1)
    dist = a + BLK - c
    in_band = jnp.logical_and(dist >= 0, dist <= maxd)
    for h in range(2):
        bias = jnp.where(in_band, _band_bias(tab_ref, bidx_ref, head + h), NEG)
        bias_ref[1, h * BLK:(h + 1) * BLK, :] = bias
        bias_ref[0, h * BLK:(h + 1) * BLK, :] = jnp.where(c >= BLK, bias, NEG)


def _stack_heads(x, lo, dtype=BF16):
    return jnp.concatenate([jnp.where(lo, x, 0.0), jnp.where(lo, 0.0, x)], axis=0).astype(dtype)


def _unstack_heads(x, lo):
    n = x.shape[0] // 2
    return jnp.where(lo, x[:n], x[n:])


def _stack_rows(ref, prev, cur):
    return jnp.concatenate([ref[prev, :], ref[cur, :]], axis=0).astype(BF16)


def _band_fwd(name, d, n_pairs, maxd, head0, srcs, bidx_g, tab, sinks):
    nb = S // d // BLK
    (qa, qc), (ka, kc), (va, vc) = srcs
    out_spec = pl.BlockSpec((S // d, 128), lambda p, r: (0, r * n_pairs + p))
    smem = pl.BlockSpec(memory_space=pltpu.SMEM)
    full = pl.BlockSpec((BLK, 2 * BLK), lambda p, r: (0, 0))

    def body(tab_ref, sink_ref, q_ref, k_ref, v_ref, bidx_ref, o_ref, lse_ref, bias_ref):
        p, r = pl.program_id(0), pl.program_id(1)

        @pl.when(r == 0)
        def _():
            _band_fill(bias_ref, tab_ref, bidx_ref, head0 + 2 * p, maxd)

        lo = _lane_lo()
        sink = jnp.where(lax.broadcasted_iota(jnp.int32, (2 * BLK, 1), 0) < BLK, sink_ref[2 * p], sink_ref[2 * p + 1])

        def block(b, carry):
            cur, prev = _band_rows(b)
            qs = _stack_heads(q_ref[cur, :] * SCALE, lo)
            ks, vs = _stack_rows(k_ref, prev, cur), _stack_rows(v_ref, prev, cur)
            s = _dot(qs, ks, NT) + bias_ref[jnp.minimum(b, 1)]
            m = jnp.max(s, axis=1, keepdims=True)
            pr = jnp.exp(s - m)
            l = jnp.sum(pr, axis=1, keepdims=True)
            num = _dot(pr.astype(BF16), vs, NN)
            lse = m + jnp.log(l)
            sig = 1.0 / (1.0 + jnp.exp(sink - lse))
            o_ref[cur, :] = _unstack_heads(num * (sig / l), lo)
            lse_ref[cur, :] = _unstack_heads(lse + jnp.zeros((2 * BLK, 128), F32), lo)
            return carry

        lax.fori_loop(0, nb, block, 0, unroll=min(nb, 2))

    shape = jax.ShapeDtypeStruct((S // d, d * n_pairs * 128), F32)
    o, lse = pl.pallas_call(
        body, name=name, grid=(n_pairs, d),
        in_specs=[smem, smem, _class_spec(qa, qc, d), _class_spec(ka, kc, d), _class_spec(va, vc, d), full],
        out_specs=[out_spec, out_spec], out_shape=[shape, shape],
        scratch_shapes=[pltpu.VMEM((2, 2 * BLK, 2 * BLK), F32)],
        compiler_params=_params(("parallel", "arbitrary")))(
            tab, sinks, _classes(qa, d), _classes(ka, d), _classes(va, d), bidx_g)
    return o.reshape(S, n_pairs * 128), lse.reshape(S, n_pairs * 128)


def _band_bwd(name, d, n_pairs, maxd, head0, srcs, bidx_g, tab, sinks, o, lse, do, stats_in):
    nb = S // d // BLK
    rows = S // d
    (qa, qc), (ka, kc), (va, vc) = srcs
    cls_spec = pl.BlockSpec((rows, 128), lambda p, r: (0, r * n_pairs + p))
    smem = pl.BlockSpec(memory_space=pltpu.SMEM)
    full = pl.BlockSpec((BLK, 2 * BLK), lambda p, r: (0, 0))
    stat_spec = pl.BlockSpec((2, 8, 128), lambda p, r: (p, 0, 0))

    def body(tab_ref, sink_ref, q_ref, k_ref, v_ref, bidx_ref, o_ref, lse_ref, do_ref, sin_ref,
             dq_ref, dk_ref, dv_ref, stat_ref, bias_ref, dsacc_ref, sk_ref):
        p, r = pl.program_id(0), pl.program_id(1)

        @pl.when(r == 0)
        def _():
            _band_fill(bias_ref, tab_ref, bidx_ref, head0 + 2 * p, maxd)
            dsacc_ref[...] = jnp.zeros_like(dsacc_ref)
            sk_ref[...] = jnp.zeros_like(sk_ref)

        dk_ref[...] = jnp.zeros_like(dk_ref)
        dv_ref[...] = jnp.zeros_like(dv_ref)
        lo = _lane_lo()
        head1 = lax.broadcasted_iota(jnp.int32, (2 * BLK, 1), 0) >= BLK
        sink = jnp.where(head1, sink_ref[2 * p + 1], sink_ref[2 * p])

        def block(b, carry):
            cur, prev = _band_rows(b)
            qs = _stack_heads(q_ref[cur, :] * SCALE, lo)
            ks, vs = _stack_rows(k_ref, prev, cur), _stack_rows(v_ref, prev, cur)
            do = do_ref[cur, :]
            dos = _stack_heads(do, lo, F32)
            lse = jnp.concatenate(_per_head(lse_ref[cur, :], lo), axis=0)
            prod = do * o_ref[cur, :]
            delta = jnp.concatenate([jnp.sum(jnp.where(lo, prod, 0.0), axis=1, keepdims=True),
                                     jnp.sum(jnp.where(lo, 0.0, prod), axis=1, keepdims=True)], axis=0)
            sig = 1.0 / (1.0 + jnp.exp(sink - lse))
            pr = jnp.exp(_dot(qs, ks, NT) + bias_ref[jnp.minimum(b, 1)] - lse)
            ds = pr * (sig * (_dot(dos.astype(BF16), vs, NT) - delta))
            dsb = ds.astype(BF16)
            dq_ref[cur, :] = SCALE * _unstack_heads(_dot(dsb, ks, NN), lo)
            dk = _dot(dsb, qs, TN)
            dv = _dot(pr.astype(BF16), (sig * dos).astype(BF16), TN)
            dk_ref[prev, :] += dk[:BLK]
            dk_ref[cur, :] += dk[BLK:]
            dv_ref[prev, :] += dv[:BLK]
            dv_ref[cur, :] += dv[BLK:]
            dsacc_ref[...] += ds
            sink_grad = -delta * (1.0 - sig)
            for h in range(2):
                sk_ref[h] += jnp.zeros((8, 128), F32) + jnp.sum(sink_grad[h * BLK:(h + 1) * BLK])
            return carry

        lax.fori_loop(0, nb, block, 0, unroll=min(nb, 2))

        @pl.when(r == d - 1)
        def _():
            bi = bidx_ref[...]
            lane = lax.broadcasted_iota(jnp.int32, (8, 128), 1)
            sub = lax.broadcasted_iota(jnp.int32, (8, 128), 0)
            for h in range(2):
                acc = dsacc_ref[h * BLK:(h + 1) * BLK, :]
                row = jnp.where(jnp.logical_and(sub == 1, lane == 0), sk_ref[h], 0.0)
                for kk in range(NUM_BUCKETS):
                    tot = jnp.sum(jnp.where(bi == kk, acc, 0.0))
                    row = jnp.where(jnp.logical_and(sub == 0, lane == kk), tot, row)
                stat_ref[h] = row + jnp.where(sub == 0, sin_ref[h], 0.0)

    shape = jax.ShapeDtypeStruct((rows, d * n_pairs * 128), F32)
    dq, dk, dv, stats = pl.pallas_call(
        body, name=name, grid=(n_pairs, d),
        in_specs=[smem, smem, _class_spec(qa, qc, d), _class_spec(ka, kc, d), _class_spec(va, vc, d), full,
                  cls_spec, cls_spec, cls_spec, stat_spec],
        out_specs=[cls_spec, cls_spec, cls_spec, stat_spec],
        out_shape=[shape, shape, shape, jax.ShapeDtypeStruct((2 * n_pairs, 8, 128), F32)],
        scratch_shapes=[pltpu.VMEM((2, 2 * BLK, 2 * BLK), F32), pltpu.VMEM((2 * BLK, 2 * BLK), F32),
                        pltpu.VMEM((2, 8, 128), F32)],
        compiler_params=_params(("arbitrary", "arbitrary")))(
            tab, sinks, _classes(qa, d), _classes(ka, d), _classes(va, d), bidx_g,
            _classes(o, d), _classes(lse, d), _classes(do, d), stats_in)
    width = n_pairs * 128
    return dq.reshape(S, width), dk.reshape(S, width), dv.reshape(S, width), stats


def _comb_fwd(o_g, lse_g):
    def body(o0, o1, o2, l0, l1, l2, out_ref, outb_ref, lse_ref):
        a0, a1, a2 = l0[...], l1[...], l2[...]
        m = jnp.maximum(jnp.maximum(a0, a1), a2)
        e0, e1, e2 = jnp.exp(a0 - m), jnp.exp(a1 - m), jnp.exp(a2 - m)
        tot = e0 + e1 + e2
        out = (e0 * o0[...] + e1 * o1[...] + e2 * o2[...]) / tot
        out_ref[...] = out
        outb_ref[...] = out.astype(BF16)
        lse_ref[...] = m + jnp.log(tot)

    spec = _row_spec(4 * HD)
    f32 = jax.ShapeDtypeStruct((S, 4 * HD), F32)
    return pl.pallas_call(
        body, name="comb_fwd", grid=(S // TR,), in_specs=[spec] * 6, out_specs=[spec] * 3,
        out_shape=[f32, jax.ShapeDtypeStruct((S, 4 * HD), BF16), f32],
        compiler_params=_params(("parallel",)))(*o_g, *lse_g)


def _split2(x):
    hi = x.astype(BF16)
    return hi, (x - hi.astype(F32)).astype(BF16)


KB = 2 * BLK
SBQ = 2 * BLK


def _tri_sum(x, tri):
    hi, lo = _split2(x)
    both = _dot(jnp.concatenate([hi, lo], axis=0), tri, NN)
    return both[:x.shape[0]] + both[x.shape[0]:]


def _tri(strict_upper):
    r = lax.broadcasted_iota(jnp.int32, (KB, KB), 0)
    c = lax.broadcasted_iota(jnp.int32, (KB, KB), 1)
    return jnp.where(r > c if strict_upper else r < c, 1.0, 0.0).astype(BF16)


def _sb_terms(qs, kj, before):
    z = _dot(qs, kj, NT)
    lsp = jnp.minimum(z, 0.0) - jnp.log(1.0 + jnp.exp(-jnp.abs(z)))
    return lsp, jnp.where(before, lsp - z, 0.0)


def _sb_before(i, m):
    t = (lax.broadcasted_iota(jnp.int32, (2 * SBQ, KB), 0) & (SBQ - 1)) + i * SBQ
    s = lax.broadcasted_iota(jnp.int32, (2 * SBQ, KB), 1) + m * KB
    return s < t


C_COL = 3072 // 128


def _sb_fwd(proj):
    blk = lambda off: pl.BlockSpec((SBQ, 128), lambda p, i: (i, off + p))
    col = lambda off: pl.BlockSpec((S, 128), lambda p, i: (0, off + p))
    out = pl.BlockSpec((SBQ, 128), lambda p, i: (i, p))

    def body(q_ref, k_ref, v_ref, o_ref, ob_ref, tot_ref):
        i = pl.program_id(1)
        lo = _lane_lo(SBQ)
        qs = _stack_heads(q_ref[...] * SCALE, lo)
        suffix = _tri(True)

        def step(n, carry):
            acc, rest = carry
            m = i - n
            rows = pl.ds(pl.multiple_of(m * KB, KB), KB)
            kj, vj = k_ref[rows, :].astype(BF16), v_ref[rows, :].astype(BF16)
            before = _sb_before(i, m)
            lsp, lk = _sb_terms(qs, kj, before)
            w = jnp.where(before, jnp.exp(lsp + _tri_sum(lk, suffix) + rest), 0.0)
            return acc + _dot(w.astype(BF16), vj, NN), rest + jnp.sum(lk, axis=1, keepdims=True)

        acc, rest = lax.fori_loop(0, i + 1, step, (jnp.zeros((2 * SBQ, 128), F32), jnp.zeros((2 * SBQ, 1), F32)))
        o = _unstack_heads(acc, lo)
        o_ref[...] = o
        ob_ref[...] = o.astype(BF16)
        tot_ref[...] = _unstack_heads(rest + jnp.zeros((2 * SBQ, 128), F32), lo)

    f32 = jax.ShapeDtypeStruct((S, 4 * HD), F32)
    return pl.pallas_call(
        body, name="sb_fwd", grid=(2, S // SBQ), in_specs=[blk(C_COL), col(C_COL + 2), col(C_COL + 4)],
        out_specs=[out, out, out], out_shape=[f32, jax.ShapeDtypeStruct((S, 4 * HD), BF16), f32],
        compiler_params=_params(("parallel", "arbitrary")))(proj, proj, proj)


def _sb_bwd(proj, tot, do):
    blk = lambda off: pl.BlockSpec((SBQ, 128), lambda p, i: (i, off + p))
    col = lambda off: pl.BlockSpec((S, 128), lambda p, i: (0, off + p))

    def body(q_ref, k_ref, v_ref, tot_ref, do_ref, dq_ref, dk_ref, dv_ref):
        i = pl.program_id(1)

        @pl.when(i == 0)
        def _():
            dk_ref[...] = jnp.zeros_like(dk_ref)
            dv_ref[...] = jnp.zeros_like(dv_ref)

        lo = _lane_lo(SBQ)
        qs = _stack_heads(q_ref[...] * SCALE, lo)
        dos = _stack_heads(do_ref[...], lo)
        tots = jnp.concatenate(_per_head(tot_ref[...], lo), axis=0)
        prefix = _tri(False)

        def step(m, carry):
            dq, keep_left, g_left = carry
            rows = pl.ds(pl.multiple_of(m * KB, KB), KB)
            kj, vj = k_ref[rows, :].astype(BF16), v_ref[rows, :].astype(BF16)
            before = _sb_before(i, m)
            lsp, lk = _sb_terms(qs, kj, before)
            log_rest = tots - keep_left - lk - _tri_sum(lk, prefix)
            w = jnp.where(before, jnp.exp(lsp + log_rest), 0.0)
            g = w * _dot(dos, vj, NT)
            g_before = g_left + _dot(g.astype(BF16), prefix, NN)
            beta = jnp.exp(lsp)
            dz = jnp.where(before, g * (1.0 - beta) - g_before * beta, 0.0).astype(BF16)
            dk_ref[rows, :] += _dot(dz, qs, TN)
            dv_ref[rows, :] += _dot(w.astype(BF16), dos, TN)
            return (dq + _dot(dz, kj, NN), keep_left + jnp.sum(lk, axis=1, keepdims=True),
                    g_left + jnp.sum(g, axis=1, keepdims=True))

        zero = (jnp.zeros((2 * SBQ, 128), F32), jnp.zeros((2 * SBQ, 1), F32), jnp.zeros((2 * SBQ, 1), F32))
        dq, _, _ = lax.fori_loop(0, i + 1, step, zero)
        dq_ref[...] = SCALE * _unstack_heads(dq, lo)

    out_blk = pl.BlockSpec((SBQ, 128), lambda p, i: (i, p))
    out_col = pl.BlockSpec((S, 128), lambda p, i: (0, p))
    f32 = jax.ShapeDtypeStruct((S, 4 * HD), F32)
    return pl.pallas_call(
        body, name="sb_bwd", grid=(2, S // SBQ),
        in_specs=[blk(C_COL), col(C_COL + 2), col(C_COL + 4), out_blk, out_blk],
        out_specs=[out_blk, out_col, out_col], out_shape=[f32, f32, f32],
        compiler_params=_params(("arbitrary", "arbitrary")))(proj, proj, proj, tot, do)


TG = 256
TGR = 1024
GATE_BLK0 = OFF_GATE // TG


def _gate_specs():
    grid = (D // TG, S // TGR)
    p_specs = [pl.BlockSpec((TGR, TG), functools.partial(lambda c, r, br: (r, GATE_BLK0 + br * (D // TG) + c), br=br))
               for br in range(3)]
    b_spec = pl.BlockSpec((3, TG), lambda c, r: (0, c))
    t_spec = pl.BlockSpec((TGR, TG), lambda c, r: (r, c))
    return grid, p_specs, b_spec, t_spec


def _sigmoid(x):
    return 1.0 / (1.0 + jnp.exp(-x))


def _three_rows(rows):
    sub = lax.broadcasted_iota(jnp.int32, (3, rows[0].shape[1]), 0)
    return jnp.where(sub == 0, rows[0], jnp.where(sub == 1, rows[1], rows[2]))


def _gate_fwd(proj, b_gate, br):
    grid, p_specs, b_spec, t_spec = _gate_specs()

    def body(p0, p1, p2, b_ref, r0, r1, r2, out_ref):
        acc = jnp.zeros((TGR, TG), F32)
        for n, (p, r) in enumerate(((p0, r0), (p1, r1), (p2, r2))):
            acc += _sigmoid(p[...] + b_ref[n:n + 1, :]) * r[...]
        out_ref[...] = acc.astype(BF16)

    return pl.pallas_call(
        body, name="gate_fwd", grid=grid, in_specs=p_specs + [b_spec] + [t_spec] * 3, out_specs=t_spec,
        out_shape=jax.ShapeDtypeStruct((S, D), BF16),
        compiler_params=_params(("parallel", "parallel")))(proj, proj, proj, b_gate, *br)


def _gate_bwd(proj, b_gate, br, dmerged):
    grid, p_specs, b_spec, t_spec = _gate_specs()

    def body(p0, p1, p2, b_ref, r0, r1, r2, dm_ref, e0, e1, e2, g0, g1, g2, db_ref):
        dm = dm_ref[...]
        rows = []
        for n, (p, r, e_ref, dg_ref) in enumerate(((p0, r0, e0, g0), (p1, r1, e1, g1), (p2, r2, e2, g2))):
            g = _sigmoid(p[...] + b_ref[n:n + 1, :])
            e_ref[...] = (dm * g).astype(BF16)
            dpre = dm * r[...] * g * (1.0 - g)
            dg_ref[...] = dpre.astype(BF16)
            rows.append(jnp.sum(dpre, axis=0, keepdims=True))
        db = _three_rows(rows)

        @pl.when(pl.program_id(1) == 0)
        def _():
            db_ref[...] = db

        @pl.when(pl.program_id(1) > 0)
        def _():
            db_ref[...] += db

    bf = jax.ShapeDtypeStruct((S, D), BF16)
    out = pl.pallas_call(
        body, name="gate_bwd", grid=grid, in_specs=p_specs + [b_spec] + [t_spec] * 4,
        out_specs=[t_spec] * 6 + [b_spec], out_shape=[bf] * 6 + [jax.ShapeDtypeStruct((3, D), F32)],
        compiler_params=_params(("parallel", "arbitrary")))(proj, proj, proj, b_gate, *br, dmerged)
    return out[:3], out[3:6], out[6]


TC = 256
N_FF_BLK = D_FF // TC
GELU_C = math.sqrt(2.0 / math.pi)


def _shift_down(x, n):
    rows = lax.broadcasted_iota(jnp.int32, x.shape, 0)
    return jnp.where(rows >= n, pltpu.roll(x, n, axis=0), 0.0)


def _shift_up(x, n):
    rows = lax.broadcasted_iota(jnp.int32, x.shape, 0)
    return jnp.where(rows < x.shape[0] - n, pltpu.roll(x, x.shape[0] - n, axis=0), 0.0)


def _conv(u, w, b):
    s1, s2 = _shift_down(u, 1), _shift_down(u, 2)
    return w[2:3, :] * u + w[1:2, :] * s1 + w[0:1, :] * s2 + b, s1, s2


def _gelu_parts(x):
    inner = GELU_C * (x + 0.044715 * x * x * x)
    t = jnp.tanh(inner)
    gelu = 0.5 * x * (1.0 + t)
    dgelu = 0.5 * (1.0 + t) + 0.5 * x * (1.0 - t * t) * GELU_C * (1.0 + 3 * 0.044715 * x * x)
    return gelu, dgelu


def _conv_specs():
    ug = pl.BlockSpec((S, TC), lambda c: (0, c))
    uv = pl.BlockSpec((S, TC), lambda c: (0, N_FF_BLK + c))
    wg = pl.BlockSpec((3, TC), lambda c: (0, c))
    wv = pl.BlockSpec((3, TC), lambda c: (0, N_FF_BLK + c))
    bg = pl.BlockSpec((1, TC), lambda c: (0, c))
    bv = pl.BlockSpec((1, TC), lambda c: (0, N_FF_BLK + c))
    return ug, uv, wg, wv, bg, bv


def _conv_fwd(u, conv_w, conv_b):
    ug, uv, wg, wv, bg, bv = _conv_specs()

    def body(ug_ref, uv_ref, wg_ref, wv_ref, bg_ref, bv_ref, a_ref):
        gc = _conv(ug_ref[...], wg_ref[...], bg_ref[...])[0]
        vc = _conv(uv_ref[...], wv_ref[...], bv_ref[...])[0]
        a_ref[...] = (_gelu_parts(gc)[0] * vc).astype(BF16)

    return pl.pallas_call(
        body, name="conv_fwd", grid=(N_FF_BLK,), in_specs=[ug, uv, wg, wv, bg, bv], out_specs=ug,
        out_shape=jax.ShapeDtypeStruct((S, D_FF), BF16),
        compiler_params=_params(("parallel",)))(u, u, conv_w, conv_w, conv_b, conv_b)


def _conv_bwd(u, conv_w, conv_b, da):
    ug, uv, wg, wv, bg, bv = _conv_specs()

    def back(duc, u, s1, s2, w):
        du = w[2:3, :] * duc + w[1:2, :] * _shift_up(duc, 1) + w[0:1, :] * _shift_up(duc, 2)
        dw = _three_rows([jnp.sum(duc * s2, axis=0, keepdims=True), jnp.sum(duc * s1, axis=0, keepdims=True),
                          jnp.sum(duc * u, axis=0, keepdims=True)])
        return du, dw, jnp.sum(duc, axis=0, keepdims=True)

    def body(ug_ref, uv_ref, wg_ref, wv_ref, bg_ref, bv_ref, da_ref, dug_ref, duv_ref, dwg_ref, dwv_ref, dbg_ref, dbv_ref):
        u_g, u_v = ug_ref[...], uv_ref[...]
        gc, g1, g2 = _conv(u_g, wg_ref[...], bg_ref[...])
        vc, v1, v2 = _conv(u_v, wv_ref[...], bv_ref[...])
        gelu, dgelu = _gelu_parts(gc)
        da = da_ref[...]
        du, dw, db = back(da * vc * dgelu, u_g, g1, g2, wg_ref[...])
        dug_ref[...] = du.astype(BF16)
        dwg_ref[...] = dw
        dbg_ref[...] = db
        du, dw, db = back(da * gelu, u_v, v1, v2, wv_ref[...])
        duv_ref[...] = du.astype(BF16)
        dwv_ref[...] = dw
        dbv_ref[...] = db

    return pl.pallas_call(
        body, name="conv_bwd", grid=(N_FF_BLK,), in_specs=[ug, uv, wg, wv, bg, bv, ug],
        out_specs=[ug, ug, wg, wg, bg, bg],
        out_shape=[jax.ShapeDtypeStruct((S, D_FF), BF16), jax.ShapeDtypeStruct((S, D_FF), BF16),
                   jax.ShapeDtypeStruct((3, D_FF), F32), jax.ShapeDtypeStruct((3, D_FF), F32),
                   jax.ShapeDtypeStruct((1, D_FF), F32), jax.ShapeDtypeStruct((1, D_FF), F32)],
        compiler_params=_params(("parallel",)))(u, u, conv_w, conv_w, conv_b, conv_b, da)


def _adamw(name, w, g, m, v):
    shape = w.shape
    cols = shape[-1]
    flat = [t.reshape(-1, cols) for t in (w, g, m, v)]
    r = flat[0].shape[0]
    tr = min(128, r)

    def body(w_ref, g_ref, m_ref, v_ref, d_ref, mo_ref, vo_ref):
        g = g_ref[...]
        m = ADAM_B1 * m_ref[...] + (1.0 - ADAM_B1) * g
        v = ADAM_B2 * v_ref[...] + (1.0 - ADAM_B2) * (g * g)
        m_hat = m / (1.0 - ADAM_B1 ** ADAM_STEP)
        v_hat = v / (1.0 - ADAM_B2 ** ADAM_STEP)
        d_ref[...] = -ADAM_LR * (m_hat / (jnp.sqrt(v_hat) + ADAM_EPS) + ADAM_WD * w_ref[...])
        mo_ref[...] = m
        vo_ref[...] = v

    spec = pl.BlockSpec((tr, cols), lambda i: (i, 0))
    outs = pl.pallas_call(
        body, name=name, grid=(pl.cdiv(r, tr),), in_specs=[spec] * 4, out_specs=[spec] * 3,
        out_shape=[jax.ShapeDtypeStruct((r, cols), F32)] * 3, compiler_params=_params(("parallel",)))(*flat)
    return [t.reshape(shape) for t in outs]


def _place():
    x, y, c = lax.axis_index("x"), lax.axis_index("y"), lax.axis_index("c")
    chips = [(1 - x, y), (x, 1 - y), (1 - x, 1 - y)]
    return x, y, c, chips


def _scalars(*vals):
    return jnp.stack([jnp.asarray(v, jnp.int32) for v in vals])


HBM = pl.BlockSpec(memory_space=pltpu.HBM)
SEM = pl.BlockSpec(memory_space=pltpu.SEMAPHORE)
SPLIT_COPY = pltpu.CompilerParams(has_side_effects=pltpu.SideEffectType.DATAFLOW_SIDE_EFFECTING)


def _in_hbm(x):
    return pltpu.with_memory_space_constraint(x, pltpu.HBM)


def _cast_into_slot(name, w, layer, chip):
    _, k, n4 = w.shape
    tr = max(t for t in range(16, 257, 16) if k % t == 0)

    def body(chip_ref, w_ref, o_ref):
        o_ref[...] = w_ref[...].astype(BF16)

    return pl.pallas_call(
        body, name=name,
        grid_spec=pltpu.PrefetchScalarGridSpec(
            num_scalar_prefetch=1, grid=(k // tr,),
            in_specs=[pl.BlockSpec((None, tr, n4), lambda i, s: (layer, i, 0))],
            out_specs=pl.BlockSpec((None, tr, n4), lambda i, s: (s[0], i, 0))),
        out_shape=jax.ShapeDtypeStruct((N_CHIPS, k, n4), BF16),
        compiler_params=_params(("parallel",)))(_scalars(chip), w)


def _gather_copy(buf_ref, k, from_chip, send_sem, recv_sem, chips, c, half=False):
    rows = buf_ref.at[from_chip]
    if half:
        h = buf_ref.shape[1] // 2
        rows = buf_ref.at[from_chip, pl.ds(pl.multiple_of(c * h, h), h)]
    return pltpu.make_async_remote_copy(src_ref=rows, dst_ref=rows, send_sem=send_sem, recv_sem=recv_sem,
                                        device_id=(*chips[k], c), device_id_type=MESH)


def _gather_start(name, bufs, groups, halved=()):
    n, ng = len(bufs), len(groups)
    where = {a: (gi, e) for gi, g in enumerate(groups) for e, a in enumerate(g)}

    def body(*refs):
        ins, sems, token = refs[:n], refs[n:n + 2 * ng], refs[-1]
        x, y, c, chips = _place()
        for a in range(n):
            gi, e = where[a]
            for k in range(3):
                _gather_copy(ins[a], k, 2 * x + y, sems[2 * gi].at[3 * e + k], sems[2 * gi + 1].at[3 * e + k],
                             chips, c, a in halved).start()
        token[...] = jnp.zeros_like(token)

    out_shape = [pltpu.SemaphoreType.DMA((3 * len(g),)) for g in groups for _ in range(2)]
    out_shape += [pltpu.HBM(b.shape, b.dtype) for b in bufs] + [jax.ShapeDtypeStruct((8, 128), F32)]
    out = pl.pallas_call(
        body, name=name, in_specs=[HBM] * n,
        out_specs=[SEM] * (2 * ng) + [HBM] * n + [pl.BlockSpec(memory_space=pltpu.VMEM)], out_shape=out_shape,
        input_output_aliases={a: 2 * ng + a for a in range(n)}, compiler_params=SPLIT_COPY)(*[_in_hbm(b) for b in bufs])
    sems = [(out[2 * gi], out[2 * gi + 1]) for gi in range(ng)]
    return sems, list(out[2 * ng:2 * ng + n]), out[-1]


def _gather_wait(name, bufs, send, recv, after, halved=()):
    n = len(bufs)

    def body(*refs):
        ins, send_sem, recv_sem = refs[:n], refs[n], refs[n + 1]
        x, y, c, chips = _place()
        for e in range(n):
            for k in range(3):
                sems = (send_sem.at[3 * e + k], recv_sem.at[3 * e + k])
                _gather_copy(ins[e], k, 2 * x + y, *sems, chips, c, e in halved).wait_send()
                _gather_copy(ins[e], k, 2 * chips[k][0] + chips[k][1], *sems, chips, c, e in halved).wait_recv()

    return pl.pallas_call(
        body, name=name, in_specs=[HBM] * n + [SEM, SEM, ANY], out_specs=[HBM] * n,
        out_shape=[pltpu.HBM(b.shape, b.dtype) for b in bufs],
        input_output_aliases={a: a for a in range(n)}, compiler_params=SPLIT_COPY)(*bufs, send, recv, after)


def _swap_halves(name, bufs):
    n = len(bufs)

    def body(*refs):
        ins, outs = refs[:n], refs[n:2 * n]
        send_sem, recv_sem = refs[2 * n:]
        x, y, c, chips = _place()

        def piece(ref, k, which):
            h = ref.shape[1] // 2
            return ref.at[2 * chips[k][0] + chips[k][1], pl.ds(pl.multiple_of(which * h, h), h)]

        def copy(a, k, which):
            return pltpu.make_async_remote_copy(
                src_ref=piece(ins[a], k, c), dst_ref=piece(outs[a], k, which), send_sem=send_sem.at[3 * a + k],
                recv_sem=recv_sem.at[3 * a + k], device_id=(x, y, 1 - c), device_id_type=MESH)

        for a in range(n):
            for k in range(3):
                copy(a, k, c).start()
        for a in range(n):
            for k in range(3):
                copy(a, k, c).wait_send()
                copy(a, k, 1 - c).wait_recv()

    return pl.pallas_call(
        body, name=name, in_specs=[ANY] * n, out_specs=[ANY] * n,
        out_shape=[jax.ShapeDtypeStruct(b.shape, b.dtype) for b in bufs],
        input_output_aliases={a: a for a in range(n)},
        scratch_shapes=[pltpu.SemaphoreType.DMA((3 * n,)), pltpu.SemaphoreType.DMA((3 * n,))],
    )(*bufs)


def _reduce_copy(g_ref, land_ref, mask, send_sem, recv_sem, x, y, c, sending):
    px, py, pc = x ^ ((mask >> 2) & 1), y ^ ((mask >> 1) & 1), c ^ (mask & 1)
    half = g_ref.shape[1] // 2
    src = g_ref.at[2 * px + py, pl.ds(pl.multiple_of(pc * half, half), half)]
    dst = land_ref.at[4 * x + 2 * y + c] if sending else land_ref.at[4 * px + 2 * py + pc]
    return pltpu.make_async_remote_copy(src_ref=src, dst_ref=dst, send_sem=send_sem, recv_sem=recv_sem,
                                        device_id=(px, py, pc), device_id_type=MESH)


def _reduce_start(name, grads):
    n = len(grads)
    lands = [lax.empty((N_DEV, g.shape[1] // 2, g.shape[2]), g.dtype) for g in grads]

    def body(*refs):
        gs, ls, send_sem, recv_sem = refs[:n], refs[n:2 * n], refs[2 * n], refs[2 * n + 1]
        x, y, c, _ = _place()
        for a in range(n):
            for mask in range(1, N_DEV):
                s = (N_DEV - 1) * a + mask - 1
                _reduce_copy(gs[a], ls[a], mask, send_sem.at[s], recv_sem.at[s], x, y, c, True).start()
        refs[-1][...] = jnp.zeros_like(refs[-1])

    sem = pltpu.SemaphoreType.DMA((n * (N_DEV - 1),))
    out = pl.pallas_call(
        body, name=name, in_specs=[HBM] * (2 * n),
        out_specs=[SEM, SEM] + [HBM] * (2 * n) + [pl.BlockSpec(memory_space=pltpu.VMEM)],
        out_shape=[sem, sem] + [pltpu.HBM(t.shape, t.dtype) for t in grads + lands] + [jax.ShapeDtypeStruct((8, 128), F32)],
        input_output_aliases={a: 2 + a for a in range(2 * n)}, compiler_params=SPLIT_COPY)(
            *[_in_hbm(t) for t in grads + lands])
    return out[0], out[1], list(out[2:2 + n]), list(out[2 + n:2 + 2 * n]), out[-1]


def _reduce_wait(name, send, recv, grads, lands, after):
    n = len(grads)

    def body(*refs):
        gs, ls, send_sem, recv_sem = refs[:n], refs[n:2 * n], refs[2 * n], refs[2 * n + 1]
        x, y, c, _ = _place()
        for a in range(n):
            for mask in range(1, N_DEV):
                s = (N_DEV - 1) * a + mask - 1
                sems = (send_sem.at[s], recv_sem.at[s])
                _reduce_copy(gs[a], ls[a], mask, *sems, x, y, c, True).wait_send()
                _reduce_copy(gs[a], ls[a], mask, *sems, x, y, c, False).wait_recv()

    out = pl.pallas_call(
        body, name=name, in_specs=[HBM] * (2 * n) + [SEM, SEM, ANY], out_specs=[HBM] * (2 * n),
        out_shape=[pltpu.HBM(t.shape, t.dtype) for t in grads + lands],
        input_output_aliases={a: a for a in range(2 * n)}, compiler_params=SPLIT_COPY)(*grads, *lands, send, recv, after)
    return list(out[:n]), list(out[n:])


def _reduce_sum(name, g, land, layer, into, chip, c):
    _, k4, n4 = g.shape
    half = k4 // 2
    tr = max(t for t in range(16, 513, 16) if half % t == 0)
    per = half // tr
    me = 2 * chip + c

    def body(s_ref, own_ref, *refs):
        total = own_ref[...].astype(F32)
        for ref in refs[:N_DEV - 1]:
            total = total + ref[...].astype(F32)
        refs[-1][...] = total

    in_specs = [pl.BlockSpec((None, tr, n4), lambda i, s: (s[0], s[1] * per + i, 0))]
    in_specs += [pl.BlockSpec((None, tr, n4), functools.partial(lambda i, s, m: (s[1 + m], i, 0), m=m))
                 for m in range(1, N_DEV)]
    ins = [g] + [land] * (N_DEV - 1)
    aliases = {}
    if into is not None:
        in_specs, ins, aliases = in_specs + [ANY], ins + [into], {1 + N_DEV: 0}
    return pl.pallas_call(
        body, name=name,
        grid_spec=pltpu.PrefetchScalarGridSpec(
            num_scalar_prefetch=1, grid=(per,), in_specs=in_specs,
            out_specs=pl.BlockSpec((None, tr, n4), lambda i, s: (layer, s[1] * per + i, 0))),
        out_shape=jax.ShapeDtypeStruct((DEPTH, k4, n4), F32), input_output_aliases=aliases,
        compiler_params=_params(("parallel",)))(_scalars(chip, c, *[me ^ m for m in range(1, N_DEV)]), *ins)


def _join_halves(name, bufs):
    n = len(bufs)

    def body(*refs):
        ins, outs = refs[:n], refs[n:2 * n]
        send_sem, recv_sem = refs[2 * n:]
        x, y, c, _ = _place()

        def rows(ref, which):
            half = ref.shape[1] // 2
            return ref.at[:, pl.ds(pl.multiple_of(which * half, half), half)]

        sends = [pltpu.make_async_remote_copy(
            src_ref=rows(ins[a], c), dst_ref=rows(outs[a], c), send_sem=send_sem.at[a], recv_sem=recv_sem.at[a],
            device_id=(x, y, 1 - c), device_id_type=MESH) for a in range(n)]
        for cp in sends:
            cp.start()
        for a in range(n):
            sends[a].wait_send()
            pltpu.make_async_remote_copy(
                src_ref=rows(ins[a], c), dst_ref=rows(outs[a], 1 - c), send_sem=send_sem.at[a], recv_sem=recv_sem.at[a],
                device_id=(x, y, 1 - c), device_id_type=MESH).wait_recv()

    return pl.pallas_call(
        body, name=name, in_specs=[ANY] * n, out_specs=[ANY] * n,
        out_shape=[jax.ShapeDtypeStruct(b.shape, b.dtype) for b in bufs],
        input_output_aliases={a: a for a in range(n)},
        scratch_shapes=[pltpu.SemaphoreType.DMA((n,)), pltpu.SemaphoreType.DMA((n,))],
    )(*bufs)


def _all_reduce_small(block):
    r = block.shape[0]

    def body(x_ref, out_ref, slots, send_sem, recv_sem):
        x, y, c, _ = _place()
        me = 4 * x + 2 * y + c
        slots[me] = x_ref[...]
        sends = []
        for mask in range(1, N_DEV):
            fx, fy, fc = (mask >> 2) & 1, (mask >> 1) & 1, mask & 1
            peer = (x ^ fx, y ^ fy, c ^ fc)
            cp = pltpu.make_async_remote_copy(
                src_ref=x_ref, dst_ref=slots.at[me], send_sem=send_sem.at[mask - 1], recv_sem=recv_sem.at[mask - 1],
                device_id=peer, device_id_type=MESH)
            cp.start()
            sends.append(cp)
        for mask in range(1, N_DEV):
            src = me ^ mask
            pltpu.make_async_remote_copy(
                src_ref=x_ref, dst_ref=slots.at[src], send_sem=send_sem.at[mask - 1], recv_sem=recv_sem.at[mask - 1],
                device_id=(x, y, c), device_id_type=MESH).wait_recv()
        for cp in sends:
            cp.wait_send()
        total = slots[0]
        for d in range(1, N_DEV):
            total = total + slots[d]
        out_ref[...] = total

    vmem = pl.BlockSpec(memory_space=pltpu.VMEM)
    return pl.pallas_call(
        body, name="all_reduce_small", in_specs=[vmem], out_specs=vmem,
        out_shape=jax.ShapeDtypeStruct((r, 128), F32),
        scratch_shapes=[pltpu.VMEM((N_DEV, r, 128), F32), pltpu.SemaphoreType.DMA((N_DEV - 1,)),
                        pltpu.SemaphoreType.DMA((N_DEV - 1,))],
        compiler_params=pltpu.CompilerParams(vmem_limit_bytes=VMEM_LIMIT))(block)


B_Q_COL = 2304 // 128
B_K0, B_V0, B_END = 2816, 2944, 3072


def _full_cols(w_g):
    return w_g.transpose(1, 0, 2).reshape(w_g.shape[1], -1)


def _group_src(proj, g):
    if A_GROUPS[g][1] == 1:
        return ((proj, 2 * g), (proj, 6 + 2 * g), (proj, 12 + 2 * g))
    packed = jnp.concatenate([proj[:, t * 768 + g * 256:t * 768 + (g + 1) * 256] for t in range(3)], axis=1)
    return ((packed, 0), (packed, 2), (packed, 4))


def _kv_expand(kv):
    return jnp.broadcast_to(kv.reshape(S, 2, 1, HD), (S, 2, 4, HD)).reshape(S, 8 * HD)


def _kv_reduce(dkv):
    return dkv.reshape(S, 2, 4, HD).sum(axis=2).reshape(S, 2 * HD)


def _mixer_fwd(h1, wget, rel_bias, sinks_l, bidx):
    w = dict(wget(0, h1))
    proj = _mm_nt("proj_in", h1, w["w_in"], F32, tm=S, tn=1152)
    no_sinks = jnp.full((4,), NEG, F32)
    srcs = [_group_src(proj, g) for g in range(3)]
    o_g, lse_g = [], []
    for g, (_, d) in enumerate(A_GROUPS):
        o, lse = _band_fwd("band_fwd_g%d" % g, d, 2, BLK, 4 * g, srcs[g], bidx[g], rel_bias, no_sinks)
        o_g.append(o)
        lse_g.append(lse)
    o_a32, o_a, lse_a = _comb_fwd(o_g, lse_g)
    src_b = ((proj, B_Q_COL), (_kv_expand(proj[:, B_K0:B_V0]), 0), (_kv_expand(proj[:, B_V0:B_END]), 0))
    o_b32, lse_b = _band_fwd("band_fwd_b", 1, 4, BLK - 1, N_A, src_b, bidx[3], rel_bias, sinks_l)
    o_b = o_b32.astype(BF16)
    o_c32, o_c, tot_c = _sb_fwd(proj)
    w.update(wget(1, o_c32))
    br = [_mm_nn("branch_a", o_a, w["w_br_a"], F32, tm=S), _mm_nn("branch_b", o_b, w["w_br_b"], F32, tm=S),
          _mm_nn("branch_c", o_c, w["w_br_c"], F32, tm=S)]
    merged = _gate_fwd(proj, w["b_gate"], br)
    mo = _mm_nn("out_proj", merged, w["w_out"], F32, tm=S)
    saved = dict(proj=proj, srcs=srcs, src_b=src_b, o_a32=o_a32, lse_a=lse_a, o_b32=o_b32, lse_b=lse_b, tot_c=tot_c,
                 o_a=o_a, o_b=o_b, o_c=o_c, br=br, merged=merged)
    return mo, saved, w


def _mixer_bwd(d_mo, h1, w, sv, rel_bias, sinks_l, bidx, stats_in, emit):
    grads = {}
    dmerged = _mm_nt("out_proj_dx", d_mo, w["w_out"], F32, tm=S)
    grads["w_out"] = _mm_tn_sharded("out_proj_dw", sv["merged"], d_mo, True)
    e, dgate, db_gate = _gate_bwd(sv["proj"], w["b_gate"], sv["br"], dmerged)
    grads["b_gate"] = db_gate
    d_o = {}
    for n, name in enumerate("abc"):
        d_o[name] = _mm_nt("branch_%s_dx" % name, e[n], w["w_br_" + name], F32, tm=S)
        grads["w_br_" + name] = _mm_tn_sharded("branch_%s_dw" % name, sv["o_" + name], e[n], False)
    zero = emit(1, grads)
    no_sinks = jnp.full((4,), NEG, F32) + zero[0]
    dqs, dks, dvs, stats = [], [], [], []
    for g, (_, d) in enumerate(A_GROUPS):
        dq, dk, dv, st = _band_bwd("band_bwd_g%d" % g, d, 2, BLK, 4 * g, sv["srcs"][g], bidx[g], rel_bias, no_sinks,
                                   sv["o_a32"], sv["lse_a"], d_o["a"], stats_in[4 * g:4 * g + 4])
        dqs.append(dq)
        dks.append(dk)
        dvs.append(dv)
        stats.append(st)
    dq_b, dk_x, dv_x, st = _band_bwd("band_bwd_b", 1, 4, BLK - 1, N_A, sv["src_b"], bidx[3], rel_bias, sinks_l,
                                     sv["o_b32"], sv["lse_b"], d_o["b"], stats_in[N_A:])
    stats = jnp.concatenate(stats + [st], axis=0)
    dcq, dck, dcv = _sb_bwd(sv["proj"], sv["tot_c"], d_o["c"])
    cols = dqs + dks + dvs + [dq_b, _kv_reduce(dk_x), _kv_reduce(dv_x), dcq, dck, dcv]
    dproj = jnp.concatenate([t.astype(BF16) for t in cols] + list(dgate), axis=1)
    grads["w_in"] = _mm_tn("proj_in_dw", dproj, h1, BF16, tm=1152, tn=1024).reshape(N_CHIPS, IN_SHARD, D)
    zero = emit(2, grads)
    dh1 = _mm_nn("proj_in_dx", dproj, w["w_in"], F32, tm=S, tk=2304)
    return dh1, grads, stats, zero


def _ffn_fwd(h2, w):
    u = _mm_nn("ffn_up", h2, w["w_up"], F32, tm=S, tn=1024)
    a = _conv_fwd(u, w["conv_w"], w["conv_b"])
    dn = _mm_nn("ffn_down", a, w["w_down"], F32, tm=1024)
    return dn, dict(u=u, a=a)


def _ffn_bwd(d_dn, h2, w, sv):
    grads = {}
    da = _mm_nt("ffn_down_dx", d_dn, w["w_down"], F32, tm=S, tn=1024)
    grads["w_down"] = _mm_tn_sharded("ffn_down_dw", sv["a"], d_dn, True, tm=1024, tn=1024)
    dug, duv, dwg, dwv, dbg, dbv = _conv_bwd(sv["u"], w["conv_w"], w["conv_b"], da)
    du = jnp.concatenate([dug, duv], axis=1)
    grads["conv_w"] = jnp.concatenate([dwg, dwv], axis=1)
    grads["conv_b"] = jnp.concatenate([dbg, dbv], axis=1)
    dh2 = _mm_nt("ffn_up_dx", du, w["w_up"], F32, tm=S, tk=2048)
    grads["w_up"] = _mm_tn_sharded("ffn_up_dw", h2, du, False, tm=1024, tn=1024)
    return dh2, grads


BIG = ("w_in", "w_br_a", "w_br_b", "w_br_c", "w_out", "w_up", "w_down")


def _shard_view(name, w):
    return jnp.swapaxes(w, 1, 2) if name == "w_in" else w
WEIGHT_GROUPS = (("w_in", "b_gate"), ("w_br_a", "w_br_b", "w_br_c", "w_out"), ("w_up", "conv_w", "w_down"))
GRAD_GROUPS = (("w_down", "w_up"), ("w_out", "w_br_a", "w_br_b", "w_br_c"), ("w_in",))
SMALL_ROWS = (("rel_bias", 8), ("attn_pre_norm", 16), ("attn_post_norm", 16), ("ffn_pre_norm", 16), ("ffn_post_norm", 16),
              ("sinks", 8), ("conv_b", 128), ("b_gate", 48), ("conv_w", 384), ("loss", 8))


def _pack_small(vals):
    rows = []
    for name, n in SMALL_ROWS:
        flat = vals[name].reshape(-1).astype(F32)
        rows.append(jnp.pad(flat, (0, n * 128 - flat.shape[0])).reshape(n, 128))
    return jnp.concatenate(rows, axis=0)


def _unpack_small(block, shapes):
    out, row = {}, 0
    for name, n in SMALL_ROWS:
        size = int(np.prod(shapes[name]))
        out[name] = block[row:row + n].reshape(-1)[:size].reshape(shapes[name])
        row += n
    return out


def kernel(x, rel_bias, attn_pre_norm, w_in, b_gate, sinks, w_br_a, w_br_b, w_br_c, w_out, attn_post_norm, ffn_pre_norm, w_up, conv_w, conv_b, w_down, ffn_post_norm, loss_target, m_rel_bias, m_attn_pre_norm, m_w_in, m_b_gate, m_sinks, m_w_br_a, m_w_br_b, m_w_br_c, m_w_out, m_attn_post_norm, m_ffn_pre_norm, m_w_up, m_conv_w, m_conv_b, m_w_down, m_ffn_post_norm, v_rel_bias, v_attn_pre_norm, v_w_in, v_b_gate, v_sinks, v_w_br_a, v_w_br_b, v_w_br_c, v_w_out, v_attn_post_norm, v_ffn_pre_norm, v_w_up, v_conv_w, v_conv_b, v_w_down, v_ffn_post_norm):
    names = ("rel_bias", "attn_pre_norm", "w_in", "b_gate", "sinks", "w_br_a", "w_br_b", "w_br_c", "w_out",
             "attn_post_norm", "ffn_pre_norm", "w_up", "conv_w", "conv_b", "w_down", "ffn_post_norm")
    weights = dict(zip(names, (rel_bias, attn_pre_norm, w_in, b_gate, sinks, w_br_a, w_br_b, w_br_c, w_out,
                               attn_post_norm, ffn_pre_norm, w_up, conv_w, conv_b, w_down, ffn_post_norm)))
    mom1 = dict(zip(names, (m_rel_bias, m_attn_pre_norm, m_w_in, m_b_gate, m_sinks, m_w_br_a, m_w_br_b, m_w_br_c,
                            m_w_out, m_attn_post_norm, m_ffn_pre_norm, m_w_up, m_conv_w, m_conv_b, m_w_down,
                            m_ffn_post_norm)))
    mom2 = dict(zip(names, (v_rel_bias, v_attn_pre_norm, v_w_in, v_b_gate, v_sinks, v_w_br_a, v_w_br_b, v_w_br_c,
                            v_w_out, v_attn_post_norm, v_ffn_pre_norm, v_w_up, v_conv_w, v_conv_b, v_w_down,
                            v_ffn_post_norm)))

    chip = 2 * lax.axis_index("x") + lax.axis_index("y")
    core = lax.axis_index("c")

    keys = [(n, l) for l in range(DEPTH) for group in WEIGHT_GROUPS for n in group]
    groups = [[keys.index((n, l)) for n in group] for l in range(DEPTH) for group in WEIGHT_GROUPS]

    def slot_buffer(n, l):
        if n in BIG:
            return _cast_into_slot("cast_" + n, _shard_view(n, weights[n]), l, chip)
        shard = weights[n][l]
        return lax.dynamic_update_slice(jnp.zeros((N_CHIPS,) + shard.shape, F32), shard[None],
                                        (chip, jnp.int32(0), jnp.int32(0)))

    first = keys.index(("w_in", 0))
    sems, in_flight, _ = _gather_start("gather_start", [slot_buffer(*k) for k in keys], groups, (first,))

    def wget(l, gi, after):
        g = l * len(WEIGHT_GROUPS) + gi
        halved = tuple(e for e, a in enumerate(groups[g]) if a == first)
        got = list(_gather_wait("gather_wait_%d_%d" % (l, gi), [in_flight[a] for a in groups[g]], *sems[g], after,
                                halved))
        for e in halved:
            got[e] = _swap_halves("swap_halves", [got[e]])[0]
        out = {}
        for n, buf in zip(WEIGHT_GROUPS[gi], got):
            out[n] = buf.reshape(-1, buf.shape[-1]) if n in ("w_in", "w_out", "w_down") else _full_cols(buf)
        if gi == len(WEIGHT_GROUPS) - 1:
            out["conv_b"] = conv_b[l:l + 1]
        return out

    pending = []

    def emit(l, gi, grads):
        group = GRAD_GROUPS[gi]
        *started, token = _reduce_start("reduce_start_%d_%d" % (l, gi), [grads[n] for n in group])
        pending.append((l, group) + tuple(started))
        return token[:1, :1]

    local = _local_step(x.reshape(S, D), loss_target.reshape(S, D), wget, emit, rel_bias, sinks, attn_pre_norm,
                        attn_post_norm, ffn_pre_norm, ffn_post_norm)
    return _reduce_and_update(x.shape, names, weights, mom1, mom2, chip, core, pending, *local)


def _local_step(xs, target, wget, emit, rel_bias, sinks, attn_pre_norm, attn_post_norm, ffn_pre_norm, ffn_post_norm):
    bidx = jnp.asarray(_bucket_maps())

    saved, layers = [], []
    h1 = _rms_fwd("pre_norm_first", xs, attn_pre_norm[0:1])
    x_in = xs
    for l in range(DEPTH):
        mo, sv_mix, w = _mixer_fwd(h1, functools.partial(wget, l), rel_bias, sinks[l], bidx)
        x_mid, h2 = _post_pre_fwd("post_attn_norm", x_in, mo, attn_post_norm[l:l + 1], ffn_pre_norm[l:l + 1])
        w.update(wget(l, 2, h2))
        dn, sv_ffn = _ffn_fwd(h2, w)
        g_next = attn_pre_norm[l + 1:l + 2] if l + 1 < DEPTH else None
        x_out, h1_next = _post_pre_fwd("post_ffn_norm" if l + 1 < DEPTH else "post_ffn_norm_last", x_mid, dn,
                                       ffn_post_norm[l:l + 1], g_next)
        saved.append(dict(x_in=x_in, h1=h1, mo=mo, x_mid=x_mid, h2=h2, dn=dn, mix=sv_mix, ffn=sv_ffn))
        layers.append(w)
        x_in, h1 = x_out, h1_next

    loss_row, dres = _loss_kernel(x_in, target)

    small = [None] * DEPTH
    stats = jnp.zeros((N_BAND_Q, 8, 128), F32)
    dh_next = None
    for l in reversed(range(DEPTH)):
        w, sv = layers[l], saved[l]
        if l + 1 < DEPTH:
            pre = (saved[l + 1]["x_in"], attn_pre_norm[l + 1:l + 2] + zero, dh_next)
            dres, d_dn, dg_pre_next, dg_fpost = _norm_bwd("post_ffn_norm_bwd", dres, pre,
                                                          (sv["dn"], ffn_post_norm[l:l + 1]))
            small[l + 1]["attn_pre_norm"] = dg_pre_next
        else:
            dres, d_dn, _, dg_fpost = _norm_bwd("post_ffn_norm_last_bwd", dres, None, (sv["dn"], ffn_post_norm[l:l + 1]))
        dh2, g_ffn = _ffn_bwd(d_dn, sv["h2"], w, sv["ffn"])
        zero = emit(l, 0, g_ffn)
        dres, d_mo, dg_fpre, dg_apost = _norm_bwd("post_attn_norm_bwd", dres,
                                                  (sv["x_mid"], ffn_pre_norm[l:l + 1] + zero, dh2),
                                                  (sv["mo"], attn_post_norm[l:l + 1]))
        dh_next, g_mix, stats, zero = _mixer_bwd(d_mo, sv["h1"], w, sv["mix"], rel_bias, sinks[l], bidx, stats,
                                                 functools.partial(emit, l))
        small[l] = dict(ffn_post_norm=dg_fpost, ffn_pre_norm=dg_fpre, attn_post_norm=dg_apost,
                        sinks=stats[N_A:, 1, 0], conv_b=g_ffn["conv_b"], b_gate=g_mix["b_gate"], conv_w=g_ffn["conv_w"])
    grad_x, _, dg_pre0, _ = _norm_bwd("pre_norm_first_bwd", dres, (saved[0]["x_in"], attn_pre_norm[0:1] + zero, dh_next),
                                      None)
    small[0]["attn_pre_norm"] = dg_pre0
    return loss_row, grad_x, small, stats


def _reduce_and_update(x_shape, names, weights, mom1, mom2, chip, core, pending, loss_row, grad_x, small, stats):
    delta, new_m, new_v, grads = {}, {}, {}, {}

    def update(n, g):
        grads[n] = g
        delta[n], new_m[n], new_v[n] = _adamw("adamw_" + n, _shard_view(n, weights[n]), g,
                                              _shard_view(n, mom1[n]), _shard_view(n, mom2[n]))

    summed = {}

    def finish(which, after):
        for l, group, send, recv, gs, lands in pending:
            if (group == ("w_in",)) == which:
                gs, lands = _reduce_wait("reduce_wait_%d_%s" % (l, group[0]), send, recv, gs, lands, after)
                for n, g, land in zip(group, gs, lands):
                    summed[n] = _reduce_sum("reduce_sum_%d_%s" % (l, n), g, land, l, summed.get(n), chip, core)

    finish(False, grad_x)
    early = [n for n in BIG if n != "w_in"]
    for n, g in zip(early, _join_halves("join_halves", [summed[n] for n in early])):
        update(n, g)
    finish(True, delta[early[-1]])
    update("w_in", _join_halves("join_halves_w_in", [summed["w_in"]])[0])
    for out in (grads, delta, new_m, new_v):
        out["w_in"] = _shard_view("w_in", out["w_in"])

    small_vals = {n: jnp.stack([small[l][n].reshape(weights[n].shape[1:]) for l in range(DEPTH)])
                  for n in ("attn_pre_norm", "attn_post_norm", "ffn_pre_norm", "ffn_post_norm", "conv_b", "sinks")}
    small_vals["b_gate"] = jnp.stack([small[l]["b_gate"] for l in range(DEPTH)])
    small_vals["conv_w"] = jnp.stack([small[l]["conv_w"] for l in range(DEPTH)])
    small_vals["rel_bias"] = stats[:, 0, :NUM_BUCKETS].T
    small_vals["loss"] = loss_row[0, :1]
    shapes = {n: v.shape for n, v in small_vals.items()}
    packed, delta["w_in"] = lax.optimization_barrier((_pack_small(small_vals), delta["w_in"]))
    reduced = _unpack_small(_all_reduce_small(packed), shapes)
    reduced["b_gate"] = lax.dynamic_slice_in_dim(reduced["b_gate"], chip * (D // N_CHIPS), D // N_CHIPS, axis=2)
    reduced["conv_w"] = lax.dynamic_slice_in_dim(reduced["conv_w"], chip * (2 * D_FF // N_CHIPS), 2 * D_FF // N_CHIPS, axis=2)
    for n in names:
        if n not in grads:
            update(n, reduced[n].reshape(weights[n].shape))

    loss = reduced["loss"].reshape(())
    return (loss, grad_x.reshape(x_shape), *[grads[n] for n in names], *[delta[n] for n in names],
            *[new_m[n] for n in names], *[new_v[n] for n in names])
```

```python
import functools
import math

import numpy as np
import jax
import jax.numpy as jnp
from jax import lax
from jax.experimental import pallas as pl
from jax.experimental.pallas import tpu as pltpu

F32 = jnp.float32
BF16 = jnp.bfloat16

S = 2048
D = 1024
DEPTH = 2
HD = 64
BLK = 128
NQB = S // BLK
A_GROUPS = ((128, 1), (512, 4), (2048, 16))
N_BAND_Q = 20
N_A = 12
NUM_BUCKETS = 32
MAX_DISTANCE = 2048
D_FF = 4096
IN_COLS = 6912
IN_SHARD = IN_COLS // 4
OFF_GATE = 3840
EPS = 1e-6
SCALE = HD ** -0.5
NEG = -1e30
N_CHIPS = 4
N_DEV = 8

ADAM_LR = 0.001
ADAM_B1 = 0.9
ADAM_B2 = 0.999
ADAM_EPS = 1e-08
ADAM_WD = 0.01
ADAM_STEP = 10

VMEM_LIMIT = 56 * 1024 * 1024

NN = (((1,), (0,)), ((), ()))
NT = (((1,), (1,)), ((), ()))
TN = (((0,), (0,)), ((), ()))

MESH = pl.DeviceIdType.MESH
ANY = pl.BlockSpec(memory_space=pl.ANY)


def _dot(a, b, dims):
    return lax.dot_general(a, b, dims, preferred_element_type=F32)


def _params(sem):
    return pltpu.CompilerParams(dimension_semantics=sem, vmem_limit_bytes=VMEM_LIMIT)


def _matmul(name, a, b, out_shape, out_dtype, grid, a_spec, b_spec, o_spec, dims, acc_shape):
    nk = grid[-1]

    def body(a_ref, b_ref, o_ref, *scratch):
        part = _dot(a_ref[...].astype(BF16), b_ref[...].astype(BF16), dims)
        if nk == 1:
            o_ref[...] = part.astype(o_ref.dtype)
            return
        acc_ref, = scratch
        k = pl.program_id(len(grid) - 1)

        @pl.when(k == 0)
        def _():
            acc_ref[...] = part

        @pl.when(k > 0)
        def _():
            acc_ref[...] += part

        @pl.when(k == nk - 1)
        def _():
            o_ref[...] = acc_ref[...].astype(o_ref.dtype)

    scratch = [] if nk == 1 else [pltpu.VMEM(acc_shape, F32)]
    sem = ("parallel",) * (len(grid) - 1) + ("arbitrary",)
    return pl.pallas_call(
        body, name=name, grid=grid, in_specs=[a_spec, b_spec], out_specs=o_spec,
        out_shape=jax.ShapeDtypeStruct(out_shape, out_dtype), scratch_shapes=scratch,
        compiler_params=_params(sem))(a, b)


FULL_K = 8192


def _mm_tn_sharded(name, a, b, row_sharded, tm=512, tn=512, tk=FULL_K):
    k, m = a.shape
    n = b.shape[1]
    m4, n4 = (m // N_CHIPS, n) if row_sharded else (m, n // N_CHIPS)
    tm, tn, tk = min(tm, m4), min(tn, n4), min(tk, k)
    per_m, per_n = m4 // tm, n4 // tn
    if row_sharded:
        o_map = lambda i, j, l: (i // per_m, i % per_m, j)
    else:
        o_map = lambda i, j, l: (j // per_n, i, j % per_n)
    return _matmul(name, a, b, (N_CHIPS, m4, n4), BF16, (m // tm, n // tn, k // tk),
                   pl.BlockSpec((tk, tm), lambda i, j, l: (l, i)),
                   pl.BlockSpec((tk, tn), lambda i, j, l: (l, j)),
                   pl.BlockSpec((None, tm, tn), o_map), TN, (tm, tn))


def _mm_nn(name, a, b, out_dtype, tm=512, tn=512, tk=FULL_K):
    m, k = a.shape
    n = b.shape[1]
    tm, tn, tk = min(tm, m), min(tn, n), min(tk, k)
    return _matmul(name, a, b, (m, n), out_dtype, (m // tm, n // tn, k // tk),
                   pl.BlockSpec((tm, tk), lambda i, j, l: (i, l)),
                   pl.BlockSpec((tk, tn), lambda i, j, l: (l, j)),
                   pl.BlockSpec((tm, tn), lambda i, j, l: (i, j)), NN, (tm, tn))


def _mm_nt(name, a, b, out_dtype, tm=512, tn=512, tk=FULL_K):
    m, k = a.shape
    n = b.shape[0]
    tm, tn, tk = min(tm, m), min(tn, n), min(tk, k)
    return _matmul(name, a, b, (m, n), out_dtype, (m // tm, n // tn, k // tk),
                   pl.BlockSpec((tm, tk), lambda i, j, l: (i, l)),
                   pl.BlockSpec((tn, tk), lambda i, j, l: (j, l)),
                   pl.BlockSpec((tm, tn), lambda i, j, l: (i, j)), NT, (tm, tn))


def _mm_tn(name, a, b, out_dtype, tm=512, tn=512, tk=FULL_K):
    k, m = a.shape
    n = b.shape[1]
    tm, tn, tk = min(tm, m), min(tn, n), min(tk, k)
    return _matmul(name, a, b, (m, n), out_dtype, (m // tm, n // tn, k // tk),
                   pl.BlockSpec((tk, tm), lambda i, j, l: (l, i)),
                   pl.BlockSpec((tk, tn), lambda i, j, l: (l, j)),
                   pl.BlockSpec((tm, tn), lambda i, j, l: (i, j)), TN, (tm, tn))


TR = 512


def _row_spec(width=D):
    return pl.BlockSpec((TR, width), lambda i: (i, 0))


def _vec_spec(width=D):
    return pl.BlockSpec((1, width), lambda i: (0, 0))


def _rms(x, g):
    r = lax.rsqrt(jnp.mean(x * x, axis=-1, keepdims=True) + EPS)
    return x * r * g


def _rms_fwd(name, x, g):
    def body(x_ref, g_ref, h_ref):
        h_ref[...] = _rms(x_ref[...], g_ref[...]).astype(BF16)

    return pl.pallas_call(
        body, name=name, grid=(S // TR,), in_specs=[_row_spec(), _vec_spec()], out_specs=_row_spec(),
        out_shape=jax.ShapeDtypeStruct((S, D), BF16), compiler_params=_params(("parallel",)))(x, g)


def _post_pre_fwd(name, x, y, g_post, g_pre):
    has_pre = g_pre is not None

    def body(*refs):
        if has_pre:
            x_ref, y_ref, gp_ref, gn_ref, xn_ref, h_ref = refs
        else:
            x_ref, y_ref, gp_ref, xn_ref = refs
        xn = x_ref[...] + _rms(y_ref[...], gp_ref[...])
        xn_ref[...] = xn
        if has_pre:
            h_ref[...] = _rms(xn, gn_ref[...]).astype(BF16)

    ins = [x, y, g_post] + ([g_pre] if has_pre else [])
    in_specs = [_row_spec(), _row_spec(), _vec_spec()] + ([_vec_spec()] if has_pre else [])
    out_shape = [jax.ShapeDtypeStruct((S, D), F32)] + ([jax.ShapeDtypeStruct((S, D), BF16)] if has_pre else [])
    out_specs = [_row_spec()] + ([_row_spec()] if has_pre else [])
    out = pl.pallas_call(
        body, name=name, grid=(S // TR,), in_specs=in_specs, out_specs=out_specs, out_shape=out_shape,
        compiler_params=_params(("parallel",)))(*ins)
    return out if has_pre else (out[0], None)


def _rms_bwd_math(x, g, dy):
    r = lax.rsqrt(jnp.mean(x * x, axis=-1, keepdims=True) + EPS)
    n = x * r
    dn = dy * g
    dx = r * (dn - n * jnp.mean(dn * n, axis=-1, keepdims=True))
    return dx, jnp.sum(dy * n, axis=0, keepdims=True)


def _norm_bwd(name, dres, pre=None, post=None):
    has_pre, has_post = pre is not None, post is not None

    def body(*refs):
        refs = list(refs)
        dres_ref = refs.pop(0)
        if has_pre:
            xn_ref, gn_ref, dh_ref = refs[:3]
            refs = refs[3:]
        if has_post:
            y_ref, gp_ref = refs[:2]
            refs = refs[2:]
        dxn_ref = refs.pop(0)
        dy_ref = refs.pop(0) if has_post else None
        dgn_ref = refs.pop(0) if has_pre else None
        dgp_ref = refs.pop(0) if has_post else None
        first = pl.program_id(0) == 0
        dxn = dres_ref[...]
        if has_pre:
            dx, dg = _rms_bwd_math(xn_ref[...], gn_ref[...], dh_ref[...])
            dxn = dxn + dx

            @pl.when(first)
            def _():
                dgn_ref[...] = dg

            @pl.when(jnp.logical_not(first))
            def _():
                dgn_ref[...] += dg
        dxn_ref[...] = dxn
        if has_post:
            dy, dg = _rms_bwd_math(y_ref[...], gp_ref[...], dxn)
            dy_ref[...] = dy.astype(BF16)

            @pl.when(first)
            def _():
                dgp_ref[...] = dg

            @pl.when(jnp.logical_not(first))
            def _():
                dgp_ref[...] += dg

    ins, in_specs = [dres], [_row_spec()]
    if has_pre:
        ins += list(pre)
        in_specs += [_row_spec(), _vec_spec(), _row_spec()]
    if has_post:
        ins += list(post)
        in_specs += [_row_spec(), _vec_spec()]
    out_shape, out_specs = [jax.ShapeDtypeStruct((S, D), F32)], [_row_spec()]
    if has_post:
        out_shape.append(jax.ShapeDtypeStruct((S, D), BF16))
        out_specs.append(_row_spec())
    for _ in range(int(has_pre) + int(has_post)):
        out_shape.append(jax.ShapeDtypeStruct((1, D), F32))
        out_specs.append(_vec_spec())
    out = list(pl.pallas_call(
        body, name=name, grid=(S // TR,), in_specs=in_specs, out_specs=out_specs, out_shape=out_shape,
        compiler_params=_params(("arbitrary",)))(*ins))
    dxn = out.pop(0)
    dy = out.pop(0) if has_post else None
    dgn = out.pop(0) if has_pre else None
    dgp = out.pop(0) if has_post else None
    return dxn, dy, dgn, dgp


def _loss_kernel(y, target):
    def body(y_ref, t_ref, loss_ref, dy_ref):
        e = y_ref[...] - t_ref[...]
        dy_ref[...] = e * (1.0 / D)
        part = jnp.zeros((1, 128), F32) + 0.5 * jnp.sum(jnp.mean(e * e, axis=-1, keepdims=True))

        @pl.when(pl.program_id(0) == 0)
        def _():
            loss_ref[...] = part

        @pl.when(pl.program_id(0) > 0)
        def _():
            loss_ref[...] += part

    return pl.pallas_call(
        body, name="loss", grid=(S // TR,), in_specs=[_row_spec(), _row_spec()],
        out_specs=[_vec_spec(128), _row_spec()],
        out_shape=[jax.ShapeDtypeStruct((1, 128), F32), jax.ShapeDtypeStruct((S, D), F32)],
        compiler_params=_params(("arbitrary",)))(y, target)


def _t5_bucket_np(dist):
    max_exact = NUM_BUCKETS // 2
    nf = np.maximum(dist, 1).astype(np.float32)
    large = max_exact + (np.log(nf / max_exact) / np.float32(math.log(MAX_DISTANCE / max_exact))
                         * (NUM_BUCKETS - max_exact)).astype(np.int32)
    large = np.minimum(large, NUM_BUCKETS - 1)
    return np.where(dist < max_exact, dist, large).astype(np.int32)


def _bucket_maps():
    a = np.arange(BLK)[:, None]
    b = np.arange(2 * BLK)[None, :]
    dist = np.maximum(a + BLK - b, 0)
    maps = [_t5_bucket_np(dist * d) for _, d in A_GROUPS] + [_t5_bucket_np(dist)]
    return np.stack(maps).astype(np.int32)


def _pair_spec(col0):
    return pl.BlockSpec((S, 128), lambda p: (0, col0 + p))


def _band_rows(i, d):
    nb = S // d // BLK
    r, b = i // nb, i % nb
    cur = pl.ds(b * BLK * d + r, BLK, stride=d)
    prev = pl.ds(jnp.maximum(b - 1, 0) * BLK * d + r, BLK, stride=d)
    return cur, prev, jnp.minimum(b, 1)


def _band_bias(tab_ref, bidx_ref, h):
    bi = bidx_ref[...]
    bias = jnp.zeros((BLK, 2 * BLK), F32)
    for kk in range(NUM_BUCKETS):
        bias = jnp.where(bi == kk, tab_ref[kk, h], bias)
    return bias


def _lane_lo(rows=BLK):
    return lax.broadcasted_iota(jnp.int32, (rows, 128), 1) < HD


def _per_head(x, lo):
    return (jnp.sum(jnp.where(lo, x, 0.0), axis=1, keepdims=True) * (1.0 / HD),
            jnp.sum(jnp.where(lo, 0.0, x), axis=1, keepdims=True) * (1.0 / HD))


def _band_fill(bias_ref, tab_ref, bidx_ref, head, maxd):
    a = lax.broadcasted_iota(jnp.int32, (BLK, 2 * BLK), 0)
    c = lax.broadcasted_iota(jnp.int32, (BLK, 2 * BLK), 1)
    dist = a + BLK - c
    in_band = jnp.logical_and(dist >= 0, dist <= maxd)
    for h in range(2):
        bias = jnp.where(in_band, _band_bias(tab_ref, bidx_ref, head + h), NEG)
        bias_ref[1, h * BLK:(h + 1) * BLK, :] = bias
        bias_ref[0, h * BLK:(h + 1) * BLK, :] = jnp.where(c >= BLK, bias, NEG)


def _stack_heads(x, lo, dtype=BF16):
    return jnp.concatenate([jnp.where(lo, x, 0.0), jnp.where(lo, 0.0, x)], axis=0).astype(dtype)


def _unstack_heads(x, lo):
    n = x.shape[0] // 2
    return jnp.where(lo, x[:n], x[n:])


def _stack_rows(ref, prev, cur):
    return jnp.concatenate([ref[prev, :], ref[cur, :]], axis=0).astype(BF16)


def _band_fwd(name, d, n_pairs, maxd, head0, srcs, bidx_g, tab, sinks):
    (qa, qc), (ka, kc), (va, vc) = srcs
    out_spec = _pair_spec(0)
    smem = pl.BlockSpec(memory_space=pltpu.SMEM)
    full = pl.BlockSpec((BLK, 2 * BLK), lambda p: (0, 0))

    def body(tab_ref, sink_ref, q_ref, k_ref, v_ref, bidx_ref, o_ref, lse_ref, bias_ref):
        p = pl.program_id(0)
        _band_fill(bias_ref, tab_ref, bidx_ref, head0 + 2 * p, maxd)
        lo = _lane_lo()
        sink = jnp.where(lax.broadcasted_iota(jnp.int32, (2 * BLK, 1), 0) < BLK, sink_ref[2 * p], sink_ref[2 * p + 1])

        def block(i, carry):
            cur, prev, has_prev = _band_rows(i, d)
            qs = _stack_heads(q_ref[cur, :] * SCALE, lo)
            ks, vs = _stack_rows(k_ref, prev, cur), _stack_rows(v_ref, prev, cur)
            s = _dot(qs, ks, NT) + bias_ref[has_prev]
            m = jnp.max(s, axis=1, keepdims=True)
            pr = jnp.exp(s - m)
            l = jnp.sum(pr, axis=1, keepdims=True)
            num = _dot(pr.astype(BF16), vs, NN)
            lse = m + jnp.log(l)
            sig = 1.0 / (1.0 + jnp.exp(sink - lse))
            o_ref[cur, :] = _unstack_heads(num * (sig / l), lo)
            lse_ref[cur, :] = _unstack_heads(lse + jnp.zeros((2 * BLK, 128), F32), lo)
            return carry

        lax.fori_loop(0, NQB, block, 0, unroll=2)

    shape = jax.ShapeDtypeStruct((S, n_pairs * 128), F32)
    return pl.pallas_call(
        body, name=name, grid=(n_pairs,),
        in_specs=[smem, smem, _pair_spec(qc), _pair_spec(kc), _pair_spec(vc), full],
        out_specs=[out_spec, out_spec], out_shape=[shape, shape],
        scratch_shapes=[pltpu.VMEM((2, 2 * BLK, 2 * BLK), F32)],
        compiler_params=_params(("parallel",)))(tab, sinks, qa, ka, va, bidx_g)


def _band_bwd(name, d, n_pairs, maxd, head0, srcs, bidx_g, tab, sinks, o, lse, do, stats_in):
    (qa, qc), (ka, kc), (va, vc) = srcs
    pair = _pair_spec(0)
    smem = pl.BlockSpec(memory_space=pltpu.SMEM)
    full = pl.BlockSpec((BLK, 2 * BLK), lambda p: (0, 0))
    stat_spec = pl.BlockSpec((2, 8, 128), lambda p: (p, 0, 0))

    def body(tab_ref, sink_ref, q_ref, k_ref, v_ref, bidx_ref, o_ref, lse_ref, do_ref, sin_ref,
             dq_ref, dk_ref, dv_ref, stat_ref, bias_ref, dsacc_ref, sk_ref):
        p = pl.program_id(0)
        _band_fill(bias_ref, tab_ref, bidx_ref, head0 + 2 * p, maxd)
        dsacc_ref[...] = jnp.zeros_like(dsacc_ref)
        sk_ref[...] = jnp.zeros_like(sk_ref)
        dk_ref[...] = jnp.zeros_like(dk_ref)
        dv_ref[...] = jnp.zeros_like(dv_ref)
        lo = _lane_lo()
        head1 = lax.broadcasted_iota(jnp.int32, (2 * BLK, 1), 0) >= BLK
        sink = jnp.where(head1, sink_ref[2 * p + 1], sink_ref[2 * p])

        def block(i, carry):
            cur, prev, has_prev = _band_rows(i, d)
            qs = _stack_heads(q_ref[cur, :] * SCALE, lo)
            ks, vs = _stack_rows(k_ref, prev, cur), _stack_rows(v_ref, prev, cur)
            do = do_ref[cur, :]
            dos = _stack_heads(do, lo, F32)
            lse = jnp.concatenate(_per_head(lse_ref[cur, :], lo), axis=0)
            prod = do * o_ref[cur, :]
            delta = jnp.concatenate([jnp.sum(jnp.where(lo, prod, 0.0), axis=1, keepdims=True),
                                     jnp.sum(jnp.where(lo, 0.0, prod), axis=1, keepdims=True)], axis=0)
            sig = 1.0 / (1.0 + jnp.exp(sink - lse))
            pr = jnp.exp(_dot(qs, ks, NT) + bias_ref[has_prev] - lse)
            ds = pr * (sig * (_dot(dos.astype(BF16), vs, NT) - delta))
            dsb = ds.astype(BF16)
            dq_ref[cur, :] = SCALE * _unstack_heads(_dot(dsb, ks, NN), lo)
            dk = _dot(dsb, qs, TN)
            dv = _dot(pr.astype(BF16), (sig * dos).astype(BF16), TN)
            dk_ref[prev, :] += dk[:BLK]
            dk_ref[cur, :] += dk[BLK:]
            dv_ref[prev, :] += dv[:BLK]
            dv_ref[cur, :] += dv[BLK:]
            dsacc_ref[...] += ds
            sink_grad = -delta * (1.0 - sig)
            for h in range(2):
                sk_ref[h] += jnp.zeros((8, 128), F32) + jnp.sum(sink_grad[h * BLK:(h + 1) * BLK])
            return carry

        lax.fori_loop(0, NQB, block, 0, unroll=2)

        bi = bidx_ref[...]
        lane = lax.broadcasted_iota(jnp.int32, (8, 128), 1)
        sub = lax.broadcasted_iota(jnp.int32, (8, 128), 0)
        for h in range(2):
            acc = dsacc_ref[h * BLK:(h + 1) * BLK, :]
            row = jnp.where(jnp.logical_and(sub == 1, lane == 0), sk_ref[h], 0.0)
            for kk in range(NUM_BUCKETS):
                tot = jnp.sum(jnp.where(bi == kk, acc, 0.0))
                row = jnp.where(jnp.logical_and(sub == 0, lane == kk), tot, row)
            stat_ref[h] = row + jnp.where(sub == 0, sin_ref[h], 0.0)

    shape = jax.ShapeDtypeStruct((S, n_pairs * 128), F32)
    return pl.pallas_call(
        body, name=name, grid=(n_pairs,),
        in_specs=[smem, smem, _pair_spec(qc), _pair_spec(kc), _pair_spec(vc), full, pair, pair, pair, stat_spec],
        out_specs=[pair, pair, pair, stat_spec],
        out_shape=[shape, shape, shape, jax.ShapeDtypeStruct((2 * n_pairs, 8, 128), F32)],
        scratch_shapes=[pltpu.VMEM((2, 2 * BLK, 2 * BLK), F32), pltpu.VMEM((2 * BLK, 2 * BLK), F32),
                        pltpu.VMEM((2, 8, 128), F32)],
        compiler_params=_params(("parallel",)))(tab, sinks, qa, ka, va, bidx_g, o, lse, do, stats_in)


def _comb_fwd(o_g, lse_g):
    def body(o0, o1, o2, l0, l1, l2, out_ref, outb_ref, lse_ref):
        a0, a1, a2 = l0[...], l1[...], l2[...]
        m = jnp.maximum(jnp.maximum(a0, a1), a2)
        e0, e1, e2 = jnp.exp(a0 - m), jnp.exp(a1 - m), jnp.exp(a2 - m)
        tot = e0 + e1 + e2
        out = (e0 * o0[...] + e1 * o1[...] + e2 * o2[...]) / tot
        out_ref[...] = out
        outb_ref[...] = out.astype(BF16)
        lse_ref[...] = m + jnp.log(tot)

    spec = _row_spec(4 * HD)
    f32 = jax.ShapeDtypeStruct((S, 4 * HD), F32)
    return pl.pallas_call(
        body, name="comb_fwd", grid=(S // TR,), in_specs=[spec] * 6, out_specs=[spec] * 3,
        out_shape=[f32, jax.ShapeDtypeStruct((S, 4 * HD), BF16), f32],
        compiler_params=_params(("parallel",)))(*o_g, *lse_g)


def _split2(x):
    hi = x.astype(BF16)
    return hi, (x - hi.astype(F32)).astype(BF16)


KB = 2 * BLK
SBQ = 2 * BLK


def _tri_sum(x, tri):
    hi, lo = _split2(x)
    both = _dot(jnp.concatenate([hi, lo], axis=0), tri, NN)
    return both[:x.shape[0]] + both[x.shape[0]:]


def _tri(strict_upper):
    r = lax.broadcasted_iota(jnp.int32, (KB, KB), 0)
    c = lax.broadcasted_iota(jnp.int32, (KB, KB), 1)
    return jnp.where(r > c if strict_upper else r < c, 1.0, 0.0).astype(BF16)


def _sb_terms(qs, kj, before):
    z = _dot(qs, kj, NT)
    lsp = jnp.minimum(z, 0.0) - jnp.log(1.0 + jnp.exp(-jnp.abs(z)))
    return lsp, jnp.where(before, lsp - z, 0.0)


def _sb_before(i, m):
    t = (lax.broadcasted_iota(jnp.int32, (2 * SBQ, KB), 0) & (SBQ - 1)) + i * SBQ
    s = lax.broadcasted_iota(jnp.int32, (2 * SBQ, KB), 1) + m * KB
    return s < t


C_COL = 3072 // 128


def _sb_fwd(proj):
    blk = lambda off: pl.BlockSpec((SBQ, 128), lambda p, i: (i, off + p))
    col = lambda off: pl.BlockSpec((S, 128), lambda p, i: (0, off + p))
    out = pl.BlockSpec((SBQ, 128), lambda p, i: (i, p))

    def body(q_ref, k_ref, v_ref, o_ref, ob_ref, tot_ref):
        i = pl.program_id(1)
        lo = _lane_lo(SBQ)
        qs = _stack_heads(q_ref[...] * SCALE, lo)
        suffix = _tri(True)

        def step(n, carry):
            acc, rest = carry
            m = i - n
            rows = pl.ds(pl.multiple_of(m * KB, KB), KB)
            kj, vj = k_ref[rows, :].astype(BF16), v_ref[rows, :].astype(BF16)
            before = _sb_before(i, m)
            lsp, lk = _sb_terms(qs, kj, before)
            w = jnp.where(before, jnp.exp(lsp + _tri_sum(lk, suffix) + rest), 0.0)
            return acc + _dot(w.astype(BF16), vj, NN), rest + jnp.sum(lk, axis=1, keepdims=True)

        acc, rest = lax.fori_loop(0, i + 1, step, (jnp.zeros((2 * SBQ, 128), F32), jnp.zeros((2 * SBQ, 1), F32)))
        o = _unstack_heads(acc, lo)
        o_ref[...] = o
        ob_ref[...] = o.astype(BF16)
        tot_ref[...] = _unstack_heads(rest + jnp.zeros((2 * SBQ, 128), F32), lo)

    f32 = jax.ShapeDtypeStruct((S, 4 * HD), F32)
    return pl.pallas_call(
        body, name="sb_fwd", grid=(2, S // SBQ), in_specs=[blk(C_COL), col(C_COL + 2), col(C_COL + 4)],
        out_specs=[out, out, out], out_shape=[f32, jax.ShapeDtypeStruct((S, 4 * HD), BF16), f32],
        compiler_params=_params(("parallel", "arbitrary")))(proj, proj, proj)


def _sb_bwd(proj, tot, do):
    blk = lambda off: pl.BlockSpec((SBQ, 128), lambda p, i: (i, off + p))
    col = lambda off: pl.BlockSpec((S, 128), lambda p, i: (0, off + p))

    def body(q_ref, k_ref, v_ref, tot_ref, do_ref, dq_ref, dk_ref, dv_ref):
        i = pl.program_id(1)

        @pl.when(i == 0)
        def _():
            dk_ref[...] = jnp.zeros_like(dk_ref)
            dv_ref[...] = jnp.zeros_like(dv_ref)

        lo = _lane_lo(SBQ)
        qs = _stack_heads(q_ref[...] * SCALE, lo)
        dos = _stack_heads(do_ref[...], lo)
        tots = jnp.concatenate(_per_head(tot_ref[...], lo), axis=0)
        prefix = _tri(False)

        def step(m, carry):
            dq, keep_left, g_left = carry
            rows = pl.ds(pl.multiple_of(m * KB, KB), KB)
            kj, vj = k_ref[rows, :].astype(BF16), v_ref[rows, :].astype(BF16)
            before = _sb_before(i, m)
            lsp, lk = _sb_terms(qs, kj, before)
            log_rest = tots - keep_left - lk - _tri_sum(lk, prefix)
            w = jnp.where(before, jnp.exp(lsp + log_rest), 0.0)
            g = w * _dot(dos, vj, NT)
            g_before = g_left + _dot(g.astype(BF16), prefix, NN)
            beta = jnp.exp(lsp)
            dz = jnp.where(before, g * (1.0 - beta) - g_before * beta, 0.0).astype(BF16)
            dk_ref[rows, :] += _dot(dz, qs, TN)
            dv_ref[rows, :] += _dot(w.astype(BF16), dos, TN)
            return (dq + _dot(dz, kj, NN), keep_left + jnp.sum(lk, axis=1, keepdims=True),
                    g_left + jnp.sum(g, axis=1, keepdims=True))

        zero = (jnp.zeros((2 * SBQ, 128), F32), jnp.zeros((2 * SBQ, 1), F32), jnp.zeros((2 * SBQ, 1), F32))
        dq, _, _ = lax.fori_loop(0, i + 1, step, zero)
        dq_ref[...] = SCALE * _unstack_heads(dq, lo)

    out_blk = pl.BlockSpec((SBQ, 128), lambda p, i: (i, p))
    out_col = pl.BlockSpec((S, 128), lambda p, i: (0, p))
    f32 = jax.ShapeDtypeStruct((S, 4 * HD), F32)
    return pl.pallas_call(
        body, name="sb_bwd", grid=(2, S // SBQ),
        in_specs=[blk(C_COL), col(C_COL + 2), col(C_COL + 4), out_blk, out_blk],
        out_specs=[out_blk, out_col, out_col], out_shape=[f32, f32, f32],
        compiler_params=_params(("arbitrary", "arbitrary")))(proj, proj, proj, tot, do)


TG = 256
TGR = 1024
GATE_BLK0 = OFF_GATE // TG


def _gate_specs():
    grid = (D // TG, S // TGR)
    p_specs = [pl.BlockSpec((TGR, TG), functools.partial(lambda c, r, br: (r, GATE_BLK0 + br * (D // TG) + c), br=br))
               for br in range(3)]
    b_spec = pl.BlockSpec((3, TG), lambda c, r: (0, c))
    t_spec = pl.BlockSpec((TGR, TG), lambda c, r: (r, c))
    return grid, p_specs, b_spec, t_spec


def _sigmoid(x):
    return 1.0 / (1.0 + jnp.exp(-x))


def _three_rows(rows):
    sub = lax.broadcasted_iota(jnp.int32, (3, rows[0].shape[1]), 0)
    return jnp.where(sub == 0, rows[0], jnp.where(sub == 1, rows[1], rows[2]))


def _gate_fwd(proj, b_gate, br):
    grid, p_specs, b_spec, t_spec = _gate_specs()

    def body(p0, p1, p2, b_ref, r0, r1, r2, out_ref):
        acc = jnp.zeros((TGR, TG), F32)
        for n, (p, r) in enumerate(((p0, r0), (p1, r1), (p2, r2))):
            acc += _sigmoid(p[...] + b_ref[n:n + 1, :]) * r[...]
        out_ref[...] = acc.astype(BF16)

    return pl.pallas_call(
        body, name="gate_fwd", grid=grid, in_specs=p_specs + [b_spec] + [t_spec] * 3, out_specs=t_spec,
        out_shape=jax.ShapeDtypeStruct((S, D), BF16),
        compiler_params=_params(("parallel", "parallel")))(proj, proj, proj, b_gate, *br)


def _gate_bwd(proj, b_gate, br, dmerged):
    grid, p_specs, b_spec, t_spec = _gate_specs()

    def body(p0, p1, p2, b_ref, r0, r1, r2, dm_ref, e0, e1, e2, g0, g1, g2, db_ref):
        dm = dm_ref[...]
        rows = []
        for n, (p, r, e_ref, dg_ref) in enumerate(((p0, r0, e0, g0), (p1, r1, e1, g1), (p2, r2, e2, g2))):
            g = _sigmoid(p[...] + b_ref[n:n + 1, :])
            e_ref[...] = (dm * g).astype(BF16)
            dpre = dm * r[...] * g * (1.0 - g)
            dg_ref[...] = dpre.astype(BF16)
            rows.append(jnp.sum(dpre, axis=0, keepdims=True))
        db = _three_rows(rows)

        @pl.when(pl.program_id(1) == 0)
        def _():
            db_ref[...] = db

        @pl.when(pl.program_id(1) > 0)
        def _():
            db_ref[...] += db

    bf = jax.ShapeDtypeStruct((S, D), BF16)
    out = pl.pallas_call(
        body, name="gate_bwd", grid=grid, in_specs=p_specs + [b_spec] + [t_spec] * 4,
        out_specs=[t_spec] * 6 + [b_spec], out_shape=[bf] * 6 + [jax.ShapeDtypeStruct((3, D), F32)],
        compiler_params=_params(("parallel", "arbitrary")))(proj, proj, proj, b_gate, *br, dmerged)
    return out[:3], out[3:6], out[6]


TC = 256
N_FF_BLK = D_FF // TC
GELU_C = math.sqrt(2.0 / math.pi)


def _shift_down(x, n):
    rows = lax.broadcasted_iota(jnp.int32, x.shape, 0)
    return jnp.where(rows >= n, pltpu.roll(x, n, axis=0), 0.0)


def _shift_up(x, n):
    rows = lax.broadcasted_iota(jnp.int32, x.shape, 0)
    return jnp.where(rows < x.shape[0] - n, pltpu.roll(x, x.shape[0] - n, axis=0), 0.0)


def _conv(u, w, b):
    s1, s2 = _shift_down(u, 1), _shift_down(u, 2)
    return w[2:3, :] * u + w[1:2, :] * s1 + w[0:1, :] * s2 + b, s1, s2


def _gelu_parts(x):
    inner = GELU_C * (x + 0.044715 * x * x * x)
    t = jnp.tanh(inner)
    gelu = 0.5 * x * (1.0 + t)
    dgelu = 0.5 * (1.0 + t) + 0.5 * x * (1.0 - t * t) * GELU_C * (1.0 + 3 * 0.044715 * x * x)
    return gelu, dgelu


def _conv_specs():
    ug = pl.BlockSpec((S, TC), lambda c: (0, c))
    uv = pl.BlockSpec((S, TC), lambda c: (0, N_FF_BLK + c))
    wg = pl.BlockSpec((3, TC), lambda c: (0, c))
    wv = pl.BlockSpec((3, TC), lambda c: (0, N_FF_BLK + c))
    bg = pl.BlockSpec((1, TC), lambda c: (0, c))
    bv = pl.BlockSpec((1, TC), lambda c: (0, N_FF_BLK + c))
    return ug, uv, wg, wv, bg, bv


def _conv_fwd(u, conv_w, conv_b):
    ug, uv, wg, wv, bg, bv = _conv_specs()

    def body(ug_ref, uv_ref, wg_ref, wv_ref, bg_ref, bv_ref, a_ref):
        gc = _conv(ug_ref[...], wg_ref[...], bg_ref[...])[0]
        vc = _conv(uv_ref[...], wv_ref[...], bv_ref[...])[0]
        a_ref[...] = (_gelu_parts(gc)[0] * vc).astype(BF16)

    return pl.pallas_call(
        body, name="conv_fwd", grid=(N_FF_BLK,), in_specs=[ug, uv, wg, wv, bg, bv], out_specs=ug,
        out_shape=jax.ShapeDtypeStruct((S, D_FF), BF16),
        compiler_params=_params(("parallel",)))(u, u, conv_w, conv_w, conv_b, conv_b)


def _conv_bwd(u, conv_w, conv_b, da):
    ug, uv, wg, wv, bg, bv = _conv_specs()

    def back(duc, u, s1, s2, w):
        du = w[2:3, :] * duc + w[1:2, :] * _shift_up(duc, 1) + w[0:1, :] * _shift_up(duc, 2)
        dw = _three_rows([jnp.sum(duc * s2, axis=0, keepdims=True), jnp.sum(duc * s1, axis=0, keepdims=True),
                          jnp.sum(duc * u, axis=0, keepdims=True)])
        return du, dw, jnp.sum(duc, axis=0, keepdims=True)

    def body(ug_ref, uv_ref, wg_ref, wv_ref, bg_ref, bv_ref, da_ref, dug_ref, duv_ref, dwg_ref, dwv_ref, dbg_ref, dbv_ref):
        u_g, u_v = ug_ref[...], uv_ref[...]
        gc, g1, g2 = _conv(u_g, wg_ref[...], bg_ref[...])
        vc, v1, v2 = _conv(u_v, wv_ref[...], bv_ref[...])
        gelu, dgelu = _gelu_parts(gc)
        da = da_ref[...]
        du, dw, db = back(da * vc * dgelu, u_g, g1, g2, wg_ref[...])
        dug_ref[...] = du.astype(BF16)
        dwg_ref[...] = dw
        dbg_ref[...] = db
        du, dw, db = back(da * gelu, u_v, v1, v2, wv_ref[...])
        duv_ref[...] = du.astype(BF16)
        dwv_ref[...] = dw
        dbv_ref[...] = db

    return pl.pallas_call(
        body, name="conv_bwd", grid=(N_FF_BLK,), in_specs=[ug, uv, wg, wv, bg, bv, ug],
        out_specs=[ug, ug, wg, wg, bg, bg],
        out_shape=[jax.ShapeDtypeStruct((S, D_FF), BF16), jax.ShapeDtypeStruct((S, D_FF), BF16),
                   jax.ShapeDtypeStruct((3, D_FF), F32), jax.ShapeDtypeStruct((3, D_FF), F32),
                   jax.ShapeDtypeStruct((1, D_FF), F32), jax.ShapeDtypeStruct((1, D_FF), F32)],
        compiler_params=_params(("parallel",)))(u, u, conv_w, conv_w, conv_b, conv_b, da)


def _adamw(name, w, g, m, v):
    shape = w.shape
    cols = shape[-1]
    flat = [t.reshape(-1, cols) for t in (w, g, m, v)]
    r = flat[0].shape[0]
    tr = min(128, r)

    def body(w_ref, g_ref, m_ref, v_ref, d_ref, mo_ref, vo_ref):
        g = g_ref[...]
        m = ADAM_B1 * m_ref[...] + (1.0 - ADAM_B1) * g
        v = ADAM_B2 * v_ref[...] + (1.0 - ADAM_B2) * (g * g)
        m_hat = m / (1.0 - ADAM_B1 ** ADAM_STEP)
        v_hat = v / (1.0 - ADAM_B2 ** ADAM_STEP)
        d_ref[...] = -ADAM_LR * (m_hat / (jnp.sqrt(v_hat) + ADAM_EPS) + ADAM_WD * w_ref[...])
        mo_ref[...] = m
        vo_ref[...] = v

    spec = pl.BlockSpec((tr, cols), lambda i: (i, 0))
    outs = pl.pallas_call(
        body, name=name, grid=(pl.cdiv(r, tr),), in_specs=[spec] * 4, out_specs=[spec] * 3,
        out_shape=[jax.ShapeDtypeStruct((r, cols), F32)] * 3, compiler_params=_params(("parallel",)))(*flat)
    return [t.reshape(shape) for t in outs]


def _place():
    x, y, c = lax.axis_index("x"), lax.axis_index("y"), lax.axis_index("c")
    chips = [(1 - x, y), (x, 1 - y), (1 - x, 1 - y)]
    return x, y, c, chips


def _scalars(*vals):
    return jnp.stack([jnp.asarray(v, jnp.int32) for v in vals])


HBM = pl.BlockSpec(memory_space=pltpu.HBM)
SEM = pl.BlockSpec(memory_space=pltpu.SEMAPHORE)
SPLIT_COPY = pltpu.CompilerParams(has_side_effects=pltpu.SideEffectType.DATAFLOW_SIDE_EFFECTING)


def _in_hbm(x):
    return pltpu.with_memory_space_constraint(x, pltpu.HBM)


def _cast_into_slot(name, w, layer, chip):
    _, k, n4 = w.shape
    tr = max(t for t in range(16, 257, 16) if k % t == 0)

    def body(chip_ref, w_ref, o_ref):
        o_ref[...] = w_ref[...].astype(BF16)

    return pl.pallas_call(
        body, name=name,
        grid_spec=pltpu.PrefetchScalarGridSpec(
            num_scalar_prefetch=1, grid=(k // tr,),
            in_specs=[pl.BlockSpec((None, tr, n4), lambda i, s: (layer, i, 0))],
            out_specs=pl.BlockSpec((None, tr, n4), lambda i, s: (s[0], i, 0))),
        out_shape=jax.ShapeDtypeStruct((N_CHIPS, k, n4), BF16),
        compiler_params=_params(("parallel",)))(_scalars(chip), w)


def _gather_copy(buf_ref, k, from_chip, send_sem, recv_sem, chips, c, half=False):
    rows = buf_ref.at[from_chip]
    if half:
        h = buf_ref.shape[1] // 2
        rows = buf_ref.at[from_chip, pl.ds(pl.multiple_of(c * h, h), h)]
    return pltpu.make_async_remote_copy(src_ref=rows, dst_ref=rows, send_sem=send_sem, recv_sem=recv_sem,
                                        device_id=(*chips[k], c), device_id_type=MESH)


def _gather_start(name, bufs, groups, halved=()):
    n, ng = len(bufs), len(groups)
    where = {a: (gi, e) for gi, g in enumerate(groups) for e, a in enumerate(g)}

    def body(*refs):
        ins, sems, token = refs[:n], refs[n:n + 2 * ng], refs[-1]
        x, y, c, chips = _place()
        for a in range(n):
            gi, e = where[a]
            for k in range(3):
                _gather_copy(ins[a], k, 2 * x + y, sems[2 * gi].at[3 * e + k], sems[2 * gi + 1].at[3 * e + k],
                             chips, c, a in halved).start()
        token[...] = jnp.zeros_like(token)

    out_shape = [pltpu.SemaphoreType.DMA((3 * len(g),)) for g in groups for _ in range(2)]
    out_shape += [pltpu.HBM(b.shape, b.dtype) for b in bufs] + [jax.ShapeDtypeStruct((8, 128), F32)]
    out = pl.pallas_call(
        body, name=name, in_specs=[HBM] * n,
        out_specs=[SEM] * (2 * ng) + [HBM] * n + [pl.BlockSpec(memory_space=pltpu.VMEM)], out_shape=out_shape,
        input_output_aliases={a: 2 * ng + a for a in range(n)}, compiler_params=SPLIT_COPY)(*[_in_hbm(b) for b in bufs])
    sems = [(out[2 * gi], out[2 * gi + 1]) for gi in range(ng)]
    return sems, list(out[2 * ng:2 * ng + n]), out[-1]


def _gather_wait(name, bufs, send, recv, after, halved=()):
    n = len(bufs)

    def body(*refs):
        ins, send_sem, recv_sem = refs[:n], refs[n], refs[n + 1]
        x, y, c, chips = _place()
        for e in range(n):
            for k in range(3):
                sems = (send_sem.at[3 * e + k], recv_sem.at[3 * e + k])
                _gather_copy(ins[e], k, 2 * x + y, *sems, chips, c, e in halved).wait_send()
                _gather_copy(ins[e], k, 2 * chips[k][0] + chips[k][1], *sems, chips, c, e in halved).wait_recv()

    return pl.pallas_call(
        body, name=name, in_specs=[HBM] * n + [SEM, SEM, ANY], out_specs=[HBM] * n,
        out_shape=[pltpu.HBM(b.shape, b.dtype) for b in bufs],
        input_output_aliases={a: a for a in range(n)}, compiler_params=SPLIT_COPY)(*bufs, send, recv, after)


def _swap_halves(name, bufs):
    n = len(bufs)

    def body(*refs):
        ins, outs = refs[:n], refs[n:2 * n]
        send_sem, recv_sem = refs[2 * n:]
        x, y, c, chips = _place()

        def piece(ref, k, which):
            h = ref.shape[1] // 2
            return ref.at[2 * chips[k][0] + chips[k][1], pl.ds(pl.multiple_of(which * h, h), h)]

        def copy(a, k, which):
            return pltpu.make_async_remote_copy(
                src_ref=piece(ins[a], k, c), dst_ref=piece(outs[a], k, which), send_sem=send_sem.at[3 * a + k],
                recv_sem=recv_sem.at[3 * a + k], device_id=(x, y, 1 - c), device_id_type=MESH)

        for a in range(n):
            for k in range(3):
                copy(a, k, c).start()
        for a in range(n):
            for k in range(3):
                copy(a, k, c).wait_send()
                copy(a, k, 1 - c).wait_recv()

    return pl.pallas_call(
        body, name=name, in_specs=[ANY] * n, out_specs=[ANY] * n,
        out_shape=[jax.ShapeDtypeStruct(b.shape, b.dtype) for b in bufs],
        input_output_aliases={a: a for a in range(n)},
        scratch_shapes=[pltpu.SemaphoreType.DMA((3 * n,)), pltpu.SemaphoreType.DMA((3 * n,))],
    )(*bufs)


def _reduce_copy(g_ref, land_ref, mask, send_sem, recv_sem, x, y, c, sending):
    px, py, pc = x ^ ((mask >> 2) & 1), y ^ ((mask >> 1) & 1), c ^ (mask & 1)
    half = g_ref.shape[1] // 2
    src = g_ref.at[2 * px + py, pl.ds(pl.multiple_of(pc * half, half), half)]
    dst = land_ref.at[4 * x + 2 * y + c] if sending else land_ref.at[4 * px + 2 * py + pc]
    return pltpu.make_async_remote_copy(src_ref=src, dst_ref=dst, send_sem=send_sem, recv_sem=recv_sem,
                                        device_id=(px, py, pc), device_id_type=MESH)


def _reduce_start(name, grads):
    n = len(grads)
    lands = [lax.empty((N_DEV, g.shape[1] // 2, g.shape[2]), g.dtype) for g in grads]

    def body(*refs):
        gs, ls, send_sem, recv_sem = refs[:n], refs[n:2 * n], refs[2 * n], refs[2 * n + 1]
        x, y, c, _ = _place()
        for a in range(n):
            for mask in range(1, N_DEV):
                s = (N_DEV - 1) * a + mask - 1
                _reduce_copy(gs[a], ls[a], mask, send_sem.at[s], recv_sem.at[s], x, y, c, True).start()
        refs[-1][...] = jnp.zeros_like(refs[-1])

    sem = pltpu.SemaphoreType.DMA((n * (N_DEV - 1),))
    out = pl.pallas_call(
        body, name=name, in_specs=[HBM] * (2 * n),
        out_specs=[SEM, SEM] + [HBM] * (2 * n) + [pl.BlockSpec(memory_space=pltpu.VMEM)],
        out_shape=[sem, sem] + [pltpu.HBM(t.shape, t.dtype) for t in grads + lands] + [jax.ShapeDtypeStruct((8, 128), F32)],
        input_output_aliases={a: 2 + a for a in range(2 * n)}, compiler_params=SPLIT_COPY)(
            *[_in_hbm(t) for t in grads + lands])
    return out[0], out[1], list(out[2:2 + n]), list(out[2 + n:2 + 2 * n]), out[-1]


def _reduce_wait(name, send, recv, grads, lands, after):
    n = len(grads)

    def body(*refs):
        gs, ls, send_sem, recv_sem = refs[:n], refs[n:2 * n], refs[2 * n], refs[2 * n + 1]
        x, y, c, _ = _place()
        for a in range(n):
            for mask in range(1, N_DEV):
                s = (N_DEV - 1) * a + mask - 1
                sems = (send_sem.at[s], recv_sem.at[s])
                _reduce_copy(gs[a], ls[a], mask, *sems, x, y, c, True).wait_send()
                _reduce_copy(gs[a], ls[a], mask, *sems, x, y, c, False).wait_recv()

    out = pl.pallas_call(
        body, name=name, in_specs=[HBM] * (2 * n) + [SEM, SEM, ANY], out_specs=[HBM] * (2 * n),
        out_shape=[pltpu.HBM(t.shape, t.dtype) for t in grads + lands],
        input_output_aliases={a: a for a in range(2 * n)}, compiler_params=SPLIT_COPY)(*grads, *lands, send, recv, after)
    return list(out[:n]), list(out[n:])


def _reduce_sum(name, g, land, layer, into, chip, c):
    _, k4, n4 = g.shape
    half = k4 // 2
    tr = max(t for t in range(16, 513, 16) if half % t == 0)
    per = half // tr
    me = 2 * chip + c

    def body(s_ref, own_ref, *refs):
        total = own_ref[...].astype(F32)
        for ref in refs[:N_DEV - 1]:
            total = total + ref[...].astype(F32)
        refs[-1][...] = total

    in_specs = [pl.BlockSpec((None, tr, n4), lambda i, s: (s[0], s[1] * per + i, 0))]
    in_specs += [pl.BlockSpec((None, tr, n4), functools.partial(lambda i, s, m: (s[1 + m], i, 0), m=m))
                 for m in range(1, N_DEV)]
    ins = [g] + [land] * (N_DEV - 1)
    aliases = {}
    if into is not None:
        in_specs, ins, aliases = in_specs + [ANY], ins + [into], {1 + N_DEV: 0}
    return pl.pallas_call(
        body, name=name,
        grid_spec=pltpu.PrefetchScalarGridSpec(
            num_scalar_prefetch=1, grid=(per,), in_specs=in_specs,
            out_specs=pl.BlockSpec((None, tr, n4), lambda i, s: (layer, s[1] * per + i, 0))),
        out_shape=jax.ShapeDtypeStruct((DEPTH, k4, n4), F32), input_output_aliases=aliases,
        compiler_params=_params(("parallel",)))(_scalars(chip, c, *[me ^ m for m in range(1, N_DEV)]), *ins)


def _join_halves(name, bufs):
    n = len(bufs)

    def body(*refs):
        ins, outs = refs[:n], refs[n:2 * n]
        send_sem, recv_sem = refs[2 * n:]
        x, y, c, _ = _place()

        def rows(ref, which):
            half = ref.shape[1] // 2
            return ref.at[:, pl.ds(pl.multiple_of(which * half, half), half)]

        sends = [pltpu.make_async_remote_copy(
            src_ref=rows(ins[a], c), dst_ref=rows(outs[a], c), send_sem=send_sem.at[a], recv_sem=recv_sem.at[a],
            device_id=(x, y, 1 - c), device_id_type=MESH) for a in range(n)]
        for cp in sends:
            cp.start()
        for a in range(n):
            sends[a].wait_send()
            pltpu.make_async_remote_copy(
                src_ref=rows(ins[a], c), dst_ref=rows(outs[a], 1 - c), send_sem=send_sem.at[a], recv_sem=recv_sem.at[a],
                device_id=(x, y, 1 - c), device_id_type=MESH).wait_recv()

    return pl.pallas_call(
        body, name=name, in_specs=[ANY] * n, out_specs=[ANY] * n,
        out_shape=[jax.ShapeDtypeStruct(b.shape, b.dtype) for b in bufs],
        input_output_aliases={a: a for a in range(n)},
        scratch_shapes=[pltpu.SemaphoreType.DMA((n,)), pltpu.SemaphoreType.DMA((n,))],
    )(*bufs)


def _all_reduce_small(block):
    r = block.shape[0]

    def body(x_ref, out_ref, slots, send_sem, recv_sem):
        x, y, c, _ = _place()
        me = 4 * x + 2 * y + c
        slots[me] = x_ref[...]
        sends = []
        for mask in range(1, N_DEV):
            fx, fy, fc = (mask >> 2) & 1, (mask >> 1) & 1, mask & 1
            peer = (x ^ fx, y ^ fy, c ^ fc)
            cp = pltpu.make_async_remote_copy(
                src_ref=x_ref, dst_ref=slots.at[me], send_sem=send_sem.at[mask - 1], recv_sem=recv_sem.at[mask - 1],
                device_id=peer, device_id_type=MESH)
            cp.start()
            sends.append(cp)
        for mask in range(1, N_DEV):
            src = me ^ mask
            pltpu.make_async_remote_copy(
                src_ref=x_ref, dst_ref=slots.at[src], send_sem=send_sem.at[mask - 1], recv_sem=recv_sem.at[mask - 1],
                device_id=(x, y, c), device_id_type=MESH).wait_recv()
        for cp in sends:
            cp.wait_send()
        total = slots[0]
        for d in range(1, N_DEV):
            total = total + slots[d]
        out_ref[...] = total

    vmem = pl.BlockSpec(memory_space=pltpu.VMEM)
    return pl.pallas_call(
        body, name="all_reduce_small", in_specs=[vmem], out_specs=vmem,
        out_shape=jax.ShapeDtypeStruct((r, 128), F32),
        scratch_shapes=[pltpu.VMEM((N_DEV, r, 128), F32), pltpu.SemaphoreType.DMA((N_DEV - 1,)),
                        pltpu.SemaphoreType.DMA((N_DEV - 1,))],
        compiler_params=pltpu.CompilerParams(vmem_limit_bytes=VMEM_LIMIT))(block)


B_Q_COL = 2304 // 128
B_K0, B_V0, B_END = 2816, 2944, 3072


def _full_cols(w_g):
    return w_g.transpose(1, 0, 2).reshape(w_g.shape[1], -1)


def _group_src(proj, g):
    return ((proj, 2 * g), (proj, 6 + 2 * g), (proj, 12 + 2 * g))


def _kv_expand(kv):
    return jnp.broadcast_to(kv.reshape(S, 2, 1, HD), (S, 2, 4, HD)).reshape(S, 8 * HD)


def _kv_reduce(dkv):
    return dkv.reshape(S, 2, 4, HD).sum(axis=2).reshape(S, 2 * HD)


def _mixer_fwd(h1, wget, rel_bias, sinks_l, bidx):
    w = dict(wget(0, h1))
    proj = _mm_nt("proj_in", h1, w["w_in"], F32, tm=S, tn=1152)
    no_sinks = jnp.full((4,), NEG, F32)
    srcs = [_group_src(proj, g) for g in range(3)]
    o_g, lse_g = [], []
    for g, (_, d) in enumerate(A_GROUPS):
        o, lse = _band_fwd("band_fwd_g%d" % g, d, 2, BLK, 4 * g, srcs[g], bidx[g], rel_bias, no_sinks)
        o_g.append(o)
        lse_g.append(lse)
    o_a32, o_a, lse_a = _comb_fwd(o_g, lse_g)
    src_b = ((proj, B_Q_COL), (_kv_expand(proj[:, B_K0:B_V0]), 0), (_kv_expand(proj[:, B_V0:B_END]), 0))
    o_b32, lse_b = _band_fwd("band_fwd_b", 1, 4, BLK - 1, N_A, src_b, bidx[3], rel_bias, sinks_l)
    o_b = o_b32.astype(BF16)
    o_c32, o_c, tot_c = _sb_fwd(proj)
    w.update(wget(1, o_c32))
    br = [_mm_nn("branch_a", o_a, w["w_br_a"], F32, tm=S), _mm_nn("branch_b", o_b, w["w_br_b"], F32, tm=S),
          _mm_nn("branch_c", o_c, w["w_br_c"], F32, tm=S)]
    merged = _gate_fwd(proj, w["b_gate"], br)
    mo = _mm_nn("out_proj", merged, w["w_out"], F32, tm=S)
    saved = dict(proj=proj, srcs=srcs, src_b=src_b, o_a32=o_a32, lse_a=lse_a, o_b32=o_b32, lse_b=lse_b, tot_c=tot_c,
                 o_a=o_a, o_b=o_b, o_c=o_c, br=br, merged=merged)
    return mo, saved, w


def _mixer_bwd(d_mo, h1, w, sv, rel_bias, sinks_l, bidx, stats_in, emit):
    grads = {}
    dmerged = _mm_nt("out_proj_dx", d_mo, w["w_out"], F32, tm=S)
    grads["w_out"] = _mm_tn_sharded("out_proj_dw", sv["merged"], d_mo, True)
    e, dgate, db_gate = _gate_bwd(sv["proj"], w["b_gate"], sv["br"], dmerged)
    grads["b_gate"] = db_gate
    d_o = {}
    for n, name in enumerate("abc"):
        d_o[name] = _mm_nt("branch_%s_dx" % name, e[n], w["w_br_" + name], F32, tm=S)
        grads["w_br_" + name] = _mm_tn_sharded("branch_%s_dw" % name, sv["o_" + name], e[n], False)
    zero = emit(1, grads)
    no_sinks = jnp.full((4,), NEG, F32) + zero[0]
    dqs, dks, dvs, stats = [], [], [], []
    for g, (_, d) in enumerate(A_GROUPS):
        dq, dk, dv, st = _band_bwd("band_bwd_g%d" % g, d, 2, BLK, 4 * g, sv["srcs"][g], bidx[g], rel_bias, no_sinks,
                                   sv["o_a32"], sv["lse_a"], d_o["a"], stats_in[4 * g:4 * g + 4])
        dqs.append(dq)
        dks.append(dk)
        dvs.append(dv)
        stats.append(st)
    dq_b, dk_x, dv_x, st = _band_bwd("band_bwd_b", 1, 4, BLK - 1, N_A, sv["src_b"], bidx[3], rel_bias, sinks_l,
                                     sv["o_b32"], sv["lse_b"], d_o["b"], stats_in[N_A:])
    stats = jnp.concatenate(stats + [st], axis=0)
    dcq, dck, dcv = _sb_bwd(sv["proj"], sv["tot_c"], d_o["c"])
    cols = dqs + dks + dvs + [dq_b, _kv_reduce(dk_x), _kv_reduce(dv_x), dcq, dck, dcv]
    dproj = jnp.concatenate([t.astype(BF16) for t in cols] + list(dgate), axis=1)
    grads["w_in"] = _mm_tn("proj_in_dw", dproj, h1, BF16, tm=1152, tn=1024).reshape(N_CHIPS, IN_SHARD, D)
    zero = emit(2, grads)
    dh1 = _mm_nn("proj_in_dx", dproj, w["w_in"], F32, tm=S, tk=2304)
    return dh1, grads, stats, zero


def _ffn_fwd(h2, w):
    u = _mm_nn("ffn_up", h2, w["w_up"], F32, tm=S, tn=1024)
    a = _conv_fwd(u, w["conv_w"], w["conv_b"])
    dn = _mm_nn("ffn_down", a, w["w_down"], F32, tm=1024)
    return dn, dict(u=u, a=a)


def _ffn_bwd(d_dn, h2, w, sv):
    grads = {}
    da = _mm_nt("ffn_down_dx", d_dn, w["w_down"], F32, tm=S, tn=1024)
    grads["w_down"] = _mm_tn_sharded("ffn_down_dw", sv["a"], d_dn, True, tm=1024, tn=1024)
    dug, duv, dwg, dwv, dbg, dbv = _conv_bwd(sv["u"], w["conv_w"], w["conv_b"], da)
    du = jnp.concatenate([dug, duv], axis=1)
    grads["conv_w"] = jnp.concatenate([dwg, dwv], axis=1)
    grads["conv_b"] = jnp.concatenate([dbg, dbv], axis=1)
    dh2 = _mm_nt("ffn_up_dx", du, w["w_up"], F32, tm=S, tk=2048)
    grads["w_up"] = _mm_tn_sharded("ffn_up_dw", h2, du, False, tm=1024, tn=1024)
    return dh2, grads


BIG = ("w_in", "w_br_a", "w_br_b", "w_br_c", "w_out", "w_up", "w_down")


def _shard_view(name, w):
    return jnp.swapaxes(w, 1, 2) if name == "w_in" else w
WEIGHT_GROUPS = (("w_in", "b_gate"), ("w_br_a", "w_br_b", "w_br_c", "w_out"), ("w_up", "conv_w", "w_down"))
GRAD_GROUPS = (("w_down", "w_up"), ("w_out", "w_br_a", "w_br_b", "w_br_c"), ("w_in",))
SMALL_ROWS = (("rel_bias", 8), ("attn_pre_norm", 16), ("attn_post_norm", 16), ("ffn_pre_norm", 16), ("ffn_post_norm", 16),
              ("sinks", 8), ("conv_b", 128), ("b_gate", 48), ("conv_w", 384), ("loss", 8))


def _pack_small(vals):
    rows = []
    for name, n in SMALL_ROWS:
        flat = vals[name].reshape(-1).astype(F32)
        rows.append(jnp.pad(flat, (0, n * 128 - flat.shape[0])).reshape(n, 128))
    return jnp.concatenate(rows, axis=0)


def _unpack_small(block, shapes):
    out, row = {}, 0
    for name, n in SMALL_ROWS:
        size = int(np.prod(shapes[name]))
        out[name] = block[row:row + n].reshape(-1)[:size].reshape(shapes[name])
        row += n
    return out


def kernel(x, rel_bias, attn_pre_norm, w_in, b_gate, sinks, w_br_a, w_br_b, w_br_c, w_out, attn_post_norm, ffn_pre_norm, w_up, conv_w, conv_b, w_down, ffn_post_norm, loss_target, m_rel_bias, m_attn_pre_norm, m_w_in, m_b_gate, m_sinks, m_w_br_a, m_w_br_b, m_w_br_c, m_w_out, m_attn_post_norm, m_ffn_pre_norm, m_w_up, m_conv_w, m_conv_b, m_w_down, m_ffn_post_norm, v_rel_bias, v_attn_pre_norm, v_w_in, v_b_gate, v_sinks, v_w_br_a, v_w_br_b, v_w_br_c, v_w_out, v_attn_post_norm, v_ffn_pre_norm, v_w_up, v_conv_w, v_conv_b, v_w_down, v_ffn_post_norm):
    names = ("rel_bias", "attn_pre_norm", "w_in", "b_gate", "sinks", "w_br_a", "w_br_b", "w_br_c", "w_out",
             "attn_post_norm", "ffn_pre_norm", "w_up", "conv_w", "conv_b", "w_down", "ffn_post_norm")
    weights = dict(zip(names, (rel_bias, attn_pre_norm, w_in, b_gate, sinks, w_br_a, w_br_b, w_br_c, w_out,
                               attn_post_norm, ffn_pre_norm, w_up, conv_w, conv_b, w_down, ffn_post_norm)))
    mom1 = dict(zip(names, (m_rel_bias, m_attn_pre_norm, m_w_in, m_b_gate, m_sinks, m_w_br_a, m_w_br_b, m_w_br_c,
                            m_w_out, m_attn_post_norm, m_ffn_pre_norm, m_w_up, m_conv_w, m_conv_b, m_w_down,
                            m_ffn_post_norm)))
    mom2 = dict(zip(names, (v_rel_bias, v_attn_pre_norm, v_w_in, v_b_gate, v_sinks, v_w_br_a, v_w_br_b, v_w_br_c,
                            v_w_out, v_attn_post_norm, v_ffn_pre_norm, v_w_up, v_conv_w, v_conv_b, v_w_down,
                            v_ffn_post_norm)))

    chip = 2 * lax.axis_index("x") + lax.axis_index("y")
    core = lax.axis_index("c")

    keys = [(n, l) for l in range(DEPTH) for group in WEIGHT_GROUPS for n in group]
    groups = [[keys.index((n, l)) for n in group] for l in range(DEPTH) for group in WEIGHT_GROUPS]

    def slot_buffer(n, l):
        if n in BIG:
            return _cast_into_slot("cast_" + n, _shard_view(n, weights[n]), l, chip)
        shard = weights[n][l]
        return lax.dynamic_update_slice(jnp.zeros((N_CHIPS,) + shard.shape, F32), shard[None],
                                        (chip, jnp.int32(0), jnp.int32(0)))

    first = keys.index(("w_in", 0))
    sems, in_flight, _ = _gather_start("gather_start", [slot_buffer(*k) for k in keys], groups, (first,))

    def wget(l, gi, after):
        g = l * len(WEIGHT_GROUPS) + gi
        halved = tuple(e for e, a in enumerate(groups[g]) if a == first)
        got = list(_gather_wait("gather_wait_%d_%d" % (l, gi), [in_flight[a] for a in groups[g]], *sems[g], after,
                                halved))
        for e in halved:
            got[e] = _swap_halves("swap_halves", [got[e]])[0]
        out = {}
        for n, buf in zip(WEIGHT_GROUPS[gi], got):
            out[n] = buf.reshape(-1, buf.shape[-1]) if n in ("w_in", "w_out", "w_down") else _full_cols(buf)
        if gi == len(WEIGHT_GROUPS) - 1:
            out["conv_b"] = conv_b[l:l + 1]
        return out

    pending = []

    def emit(l, gi, grads):
        group = GRAD_GROUPS[gi]
        *started, token = _reduce_start("reduce_start_%d_%d" % (l, gi), [grads[n] for n in group])
        pending.append((l, group) + tuple(started))
        return token[:1, :1]

    local = _local_step(x.reshape(S, D), loss_target.reshape(S, D), wget, emit, rel_bias, sinks, attn_pre_norm,
                        attn_post_norm, ffn_pre_norm, ffn_post_norm)
    return _reduce_and_update(x.shape, names, weights, mom1, mom2, chip, core, pending, *local)


def _local_step(xs, target, wget, emit, rel_bias, sinks, attn_pre_norm, attn_post_norm, ffn_pre_norm, ffn_post_norm):
    bidx = jnp.asarray(_bucket_maps())

    saved, layers = [], []
    h1 = _rms_fwd("pre_norm_first", xs, attn_pre_norm[0:1])
    x_in = xs
    for l in range(DEPTH):
        mo, sv_mix, w = _mixer_fwd(h1, functools.partial(wget, l), rel_bias, sinks[l], bidx)
        x_mid, h2 = _post_pre_fwd("post_attn_norm", x_in, mo, attn_post_norm[l:l + 1], ffn_pre_norm[l:l + 1])
        w.update(wget(l, 2, h2))
        dn, sv_ffn = _ffn_fwd(h2, w)
        g_next = attn_pre_norm[l + 1:l + 2] if l + 1 < DEPTH else None
        x_out, h1_next = _post_pre_fwd("post_ffn_norm" if l + 1 < DEPTH else "post_ffn_norm_last", x_mid, dn,
                                       ffn_post_norm[l:l + 1], g_next)
        saved.append(dict(x_in=x_in, h1=h1, mo=mo, x_mid=x_mid, h2=h2, dn=dn, mix=sv_mix, ffn=sv_ffn))
        layers.append(w)
        x_in, h1 = x_out, h1_next

    loss_row, dres = _loss_kernel(x_in, target)

    small = [None] * DEPTH
    stats = jnp.zeros((N_BAND_Q, 8, 128), F32)
    dh_next = None
    for l in reversed(range(DEPTH)):
        w, sv = layers[l], saved[l]
        if l + 1 < DEPTH:
            pre = (saved[l + 1]["x_in"], attn_pre_norm[l + 1:l + 2] + zero, dh_next)
            dres, d_dn, dg_pre_next, dg_fpost = _norm_bwd("post_ffn_norm_bwd", dres, pre,
                                                          (sv["dn"], ffn_post_norm[l:l + 1]))
            small[l + 1]["attn_pre_norm"] = dg_pre_next
        else:
            dres, d_dn, _, dg_fpost = _norm_bwd("post_ffn_norm_last_bwd", dres, None, (sv["dn"], ffn_post_norm[l:l + 1]))
        dh2, g_ffn = _ffn_bwd(d_dn, sv["h2"], w, sv["ffn"])
        zero = emit(l, 0, g_ffn)
        dres, d_mo, dg_fpre, dg_apost = _norm_bwd("post_attn_norm_bwd", dres,
                                                  (sv["x_mid"], ffn_pre_norm[l:l + 1] + zero, dh2),
                                                  (sv["mo"], attn_post_norm[l:l + 1]))
        dh_next, g_mix, stats, zero = _mixer_bwd(d_mo, sv["h1"], w, sv["mix"], rel_bias, sinks[l], bidx, stats,
                                                 functools.partial(emit, l))
        small[l] = dict(ffn_post_norm=dg_fpost, ffn_pre_norm=dg_fpre, attn_post_norm=dg_apost,
                        sinks=stats[N_A:, 1, 0], conv_b=g_ffn["conv_b"], b_gate=g_mix["b_gate"], conv_w=g_ffn["conv_w"])
    grad_x, _, dg_pre0, _ = _norm_bwd("pre_norm_first_bwd", dres, (saved[0]["x_in"], attn_pre_norm[0:1] + zero, dh_next),
                                      None)
    small[0]["attn_pre_norm"] = dg_pre0
    return loss_row, grad_x, small, stats


def _reduce_and_update(x_shape, names, weights, mom1, mom2, chip, core, pending, loss_row, grad_x, small, stats):
    delta, new_m, new_v, grads = {}, {}, {}, {}

    def update(n, g):
        grads[n] = g
        delta[n], new_m[n], new_v[n] = _adamw("adamw_" + n, _shard_view(n, weights[n]), g,
                                              _shard_view(n, mom1[n]), _shard_view(n, mom2[n]))

    summed = {}

    def finish(which, after):
        for l, group, send, recv, gs, lands in pending:
            if (group == ("w_in",)) == which:
                gs, lands = _reduce_wait("reduce_wait_%d_%s" % (l, group[0]), send, recv, gs, lands, after)
                for n, g, land in zip(group, gs, lands):
                    summed[n] = _reduce_sum("reduce_sum_%d_%s" % (l, n), g, land, l, summed.get(n), chip, core)

    finish(False, grad_x)
    early = [n for n in BIG if n != "w_in"]
    for n, g in zip(early, _join_halves("join_halves", [summed[n] for n in early])):
        update(n, g)
    finish(True, delta[early[-1]])
    update("w_in", _join_halves("join_halves_w_in", [summed["w_in"]])[0])
    for out in (grads, delta, new_m, new_v):
        out["w_in"] = _shard_view("w_in", out["w_in"])

    small_vals = {n: jnp.stack([small[l][n].reshape(weights[n].shape[1:]) for l in range(DEPTH)])
                  for n in ("attn_pre_norm", "attn_post_norm", "ffn_pre_norm", "ffn_post_norm", "conv_b", "sinks")}
    small_vals["b_gate"] = jnp.stack([small[l]["b_gate"] for l in range(DEPTH)])
    small_vals["conv_w"] = jnp.stack([small[l]["conv_w"] for l in range(DEPTH)])
    small_vals["rel_bias"] = stats[:, 0, :NUM_BUCKETS].T
    small_vals["loss"] = loss_row[0, :1]
    shapes = {n: v.shape for n, v in small_vals.items()}
    packed, delta["w_in"] = lax.optimization_barrier((_pack_small(small_vals), delta["w_in"]))
    reduced = _unpack_small(_all_reduce_small(packed), shapes)
    reduced["b_gate"] = lax.dynamic_slice_in_dim(reduced["b_gate"], chip * (D // N_CHIPS), D // N_CHIPS, axis=2)
    reduced["conv_w"] = lax.dynamic_slice_in_dim(reduced["conv_w"], chip * (2 * D_FF // N_CHIPS), 2 * D_FF // N_CHIPS, axis=2)
    for n in names:
        if n not in grads:
            update(n, reduced[n].reshape(weights[n].shape))

    loss = reduced["loss"].reshape(())
    return (loss, grad_x.reshape(x_shape), *[grads[n] for n in names], *[delta[n] for n in names],
            *[new_m[n] for n in names], *[new_v[n] for n in names])
```

```python
import functools
import math

import numpy as np
import jax
import jax.numpy as jnp
from jax import lax
from jax.experimental import pallas as pl
from jax.experimental.pallas import tpu as pltpu

F32 = jnp.float32
BF16 = jnp.bfloat16

S = 2048
D = 1024
DEPTH = 2
HD = 64
BLK = 128
NQB = S // BLK
A_GROUPS = ((128, 1), (512, 4), (2048, 16))
N_BAND_Q = 20
N_A = 12
NUM_BUCKETS = 32
MAX_DISTANCE = 2048
D_FF = 4096
IN_COLS = 6912
IN_SHARD = IN_COLS // 4
OFF_GATE = 3840
EPS = 1e-6
SCALE = HD ** -0.5
NEG = -1e30
N_CHIPS = 4
N_DEV = 8

ADAM_LR = 0.001
ADAM_B1 = 0.9
ADAM_B2 = 0.999
ADAM_EPS = 1e-08
ADAM_WD = 0.01
ADAM_STEP = 10

VMEM_LIMIT = 56 * 1024 * 1024

NN = (((1,), (0,)), ((), ()))
NT = (((1,), (1,)), ((), ()))
TN = (((0,), (0,)), ((), ()))

MESH = pl.DeviceIdType.MESH
ANY = pl.BlockSpec(memory_space=pl.ANY)


def _dot(a, b, dims):
    return lax.dot_general(a, b, dims, preferred_element_type=F32)


def _params(sem):
    return pltpu.CompilerParams(dimension_semantics=sem, vmem_limit_bytes=VMEM_LIMIT)


def _matmul(name, a, b, out_shape, out_dtype, grid, a_spec, b_spec, o_spec, dims, acc_shape):
    nk = grid[-1]

    def body(a_ref, b_ref, o_ref, *scratch):
        part = _dot(a_ref[...].astype(BF16), b_ref[...].astype(BF16), dims)
        if nk == 1:
            o_ref[...] = part.astype(o_ref.dtype)
            return
        acc_ref, = scratch
        k = pl.program_id(len(grid) - 1)

        @pl.when(k == 0)
        def _():
            acc_ref[...] = part

        @pl.when(k > 0)
        def _():
            acc_ref[...] += part

        @pl.when(k == nk - 1)
        def _():
            o_ref[...] = acc_ref[...].astype(o_ref.dtype)

    scratch = [] if nk == 1 else [pltpu.VMEM(acc_shape, F32)]
    sem = ("parallel",) * (len(grid) - 1) + ("arbitrary",)
    return pl.pallas_call(
        body, name=name, grid=grid, in_specs=[a_spec, b_spec], out_specs=o_spec,
        out_shape=jax.ShapeDtypeStruct(out_shape, out_dtype), scratch_shapes=scratch,
        compiler_params=_params(sem))(a, b)


FULL_K = 8192


def _mm_tn_sharded(name, a, b, row_sharded, tm=512, tn=512, tk=FULL_K):
    k, m = a.shape
    n = b.shape[1]
    m4, n4 = (m // N_CHIPS, n) if row_sharded else (m, n // N_CHIPS)
    tm, tn, tk = min(tm, m4), min(tn, n4), min(tk, k)
    per_m, per_n = m4 // tm, n4 // tn
    if row_sharded:
        o_map = lambda i, j, l: (i // per_m, i % per_m, j)
    else:
        o_map = lambda i, j, l: (j // per_n, i, j % per_n)
    return _matmul(name, a, b, (N_CHIPS, m4, n4), BF16, (m // tm, n // tn, k // tk),
                   pl.BlockSpec((tk, tm), lambda i, j, l: (l, i)),
                   pl.BlockSpec((tk, tn), lambda i, j, l: (l, j)),
                   pl.BlockSpec((None, tm, tn), o_map), TN, (tm, tn))


def _mm_nn(name, a, b, out_dtype, tm=512, tn=512, tk=FULL_K):
    m, k = a.shape
    n = b.shape[1]
    tm, tn, tk = min(tm, m), min(tn, n), min(tk, k)
    return _matmul(name, a, b, (m, n), out_dtype, (m // tm, n // tn, k // tk),
                   pl.BlockSpec((tm, tk), lambda i, j, l: (i, l)),
                   pl.BlockSpec((tk, tn), lambda i, j, l: (l, j)),
                   pl.BlockSpec((tm, tn), lambda i, j, l: (i, j)), NN, (tm, tn))


def _mm_nt(name, a, b, out_dtype, tm=512, tn=512, tk=FULL_K):
    m, k = a.shape
    n = b.shape[0]
    tm, tn, tk = min(tm, m), min(tn, n), min(tk, k)
    return _matmul(name, a, b, (m, n), out_dtype, (m // tm, n // tn, k // tk),
                   pl.BlockSpec((tm, tk), lambda i, j, l: (i, l)),
                   pl.BlockSpec((tn, tk), lambda i, j, l: (j, l)),
                   pl.BlockSpec((tm, tn), lambda i, j, l: (i, j)), NT, (tm, tn))


def _mm_tn(name, a, b, out_dtype, tm=512, tn=512, tk=FULL_K):
    k, m = a.shape
    n = b.shape[1]
    tm, tn, tk = min(tm, m), min(tn, n), min(tk, k)
    return _matmul(name, a, b, (m, n), out_dtype, (m // tm, n // tn, k // tk),
                   pl.BlockSpec((tk, tm), lambda i, j, l: (l, i)),
                   pl.BlockSpec((tk, tn), lambda i, j, l: (l, j)),
                   pl.BlockSpec((tm, tn), lambda i, j, l: (i, j)), TN, (tm, tn))


TR = 512


def _row_spec(width=D):
    return pl.BlockSpec((TR, width), lambda i: (i, 0))


def _vec_spec(width=D):
    return pl.BlockSpec((1, width), lambda i: (0, 0))


def _rms(x, g):
    r = lax.rsqrt(jnp.mean(x * x, axis=-1, keepdims=True) + EPS)
    return x * r * g


def _rms_fwd(name, x, g):
    def body(x_ref, g_ref, h_ref):
        h_ref[...] = _rms(x_ref[...], g_ref[...]).astype(BF16)

    return pl.pallas_call(
        body, name=name, grid=(S // TR,), in_specs=[_row_spec(), _vec_spec()], out_specs=_row_spec(),
        out_shape=jax.ShapeDtypeStruct((S, D), BF16), compiler_params=_params(("parallel",)))(x, g)


def _post_pre_fwd(name, x, y, g_post, g_pre):
    has_pre = g_pre is not None

    def body(*refs):
        if has_pre:
            x_ref, y_ref, gp_ref, gn_ref, xn_ref, h_ref = refs
        else:
            x_ref, y_ref, gp_ref, xn_ref = refs
        xn = x_ref[...] + _rms(y_ref[...], gp_ref[...])
        xn_ref[...] = xn
        if has_pre:
            h_ref[...] = _rms(xn, gn_ref[...]).astype(BF16)

    ins = [x, y, g_post] + ([g_pre] if has_pre else [])
    in_specs = [_row_spec(), _row_spec(), _vec_spec()] + ([_vec_spec()] if has_pre else [])
    out_shape = [jax.ShapeDtypeStruct((S, D), F32)] + ([jax.ShapeDtypeStruct((S, D), BF16)] if has_pre else [])
    out_specs = [_row_spec()] + ([_row_spec()] if has_pre else [])
    out = pl.pallas_call(
        body, name=name, grid=(S // TR,), in_specs=in_specs, out_specs=out_specs, out_shape=out_shape,
        compiler_params=_params(("parallel",)))(*ins)
    return out if has_pre else (out[0], None)


def _rms_bwd_math(x, g, dy):
    r = lax.rsqrt(jnp.mean(x * x, axis=-1, keepdims=True) + EPS)
    n = x * r
    dn = dy * g
    dx = r * (dn - n * jnp.mean(dn * n, axis=-1, keepdims=True))
    return dx, jnp.sum(dy * n, axis=0, keepdims=True)


def _norm_bwd(name, dres, pre=None, post=None):
    has_pre, has_post = pre is not None, post is not None

    def body(*refs):
        refs = list(refs)
        dres_ref = refs.pop(0)
        if has_pre:
            xn_ref, gn_ref, dh_ref = refs[:3]
            refs = refs[3:]
        if has_post:
            y_ref, gp_ref = refs[:2]
            refs = refs[2:]
        dxn_ref = refs.pop(0)
        dy_ref = refs.pop(0) if has_post else None
        dgn_ref = refs.pop(0) if has_pre else None
        dgp_ref = refs.pop(0) if has_post else None
        first = pl.program_id(0) == 0
        dxn = dres_ref[...]
        if has_pre:
            dx, dg = _rms_bwd_math(xn_ref[...], gn_ref[...], dh_ref[...])
            dxn = dxn + dx

            @pl.when(first)
            def _():
                dgn_ref[...] = dg

            @pl.when(jnp.logical_not(first))
            def _():
                dgn_ref[...] += dg
        dxn_ref[...] = dxn
        if has_post:
            dy, dg = _rms_bwd_math(y_ref[...], gp_ref[...], dxn)
            dy_ref[...] = dy.astype(BF16)

            @pl.when(first)
            def _():
                dgp_ref[...] = dg

            @pl.when(jnp.logical_not(first))
            def _():
                dgp_ref[...] += dg

    ins, in_specs = [dres], [_row_spec()]
    if has_pre:
        ins += list(pre)
        in_specs += [_row_spec(), _vec_spec(), _row_spec()]
    if has_post:
        ins += list(post)
        in_specs += [_row_spec(), _vec_spec()]
    out_shape, out_specs = [jax.ShapeDtypeStruct((S, D), F32)], [_row_spec()]
    if has_post:
        out_shape.append(jax.ShapeDtypeStruct((S, D), BF16))
        out_specs.append(_row_spec())
    for _ in range(int(has_pre) + int(has_post)):
        out_shape.append(jax.ShapeDtypeStruct((1, D), F32))
        out_specs.append(_vec_spec())
    out = list(pl.pallas_call(
        body, name=name, grid=(S // TR,), in_specs=in_specs, out_specs=out_specs, out_shape=out_shape,
        compiler_params=_params(("arbitrary",)))(*ins))
    dxn = out.pop(0)
    dy = out.pop(0) if has_post else None
    dgn = out.pop(0) if has_pre else None
    dgp = out.pop(0) if has_post else None
    return dxn, dy, dgn, dgp


def _loss_kernel(y, target):
    def body(y_ref, t_ref, loss_ref, dy_ref):
        e = y_ref[...] - t_ref[...]
        dy_ref[...] = e * (1.0 / D)
        part = jnp.zeros((1, 128), F32) + 0.5 * jnp.sum(jnp.mean(e * e, axis=-1, keepdims=True))

        @pl.when(pl.program_id(0) == 0)
        def _():
            loss_ref[...] = part

        @pl.when(pl.program_id(0) > 0)
        def _():
            loss_ref[...] += part

    return pl.pallas_call(
        body, name="loss", grid=(S // TR,), in_specs=[_row_spec(), _row_spec()],
        out_specs=[_vec_spec(128), _row_spec()],
        out_shape=[jax.ShapeDtypeStruct((1, 128), F32), jax.ShapeDtypeStruct((S, D), F32)],
        compiler_params=_params(("arbitrary",)))(y, target)


def _t5_bucket_np(dist):
    max_exact = NUM_BUCKETS // 2
    nf = np.maximum(dist, 1).astype(np.float32)
    large = max_exact + (np.log(nf / max_exact) / np.float32(math.log(MAX_DISTANCE / max_exact))
                         * (NUM_BUCKETS - max_exact)).astype(np.int32)
    large = np.minimum(large, NUM_BUCKETS - 1)
    return np.where(dist < max_exact, dist, large).astype(np.int32)


def _bucket_maps():
    a = np.arange(BLK)[:, None]
    b = np.arange(2 * BLK)[None, :]
    dist = np.maximum(a + BLK - b, 0)
    maps = [_t5_bucket_np(dist * d) for _, d in A_GROUPS] + [_t5_bucket_np(dist)]
    return np.stack(maps).astype(np.int32)


def _pair_spec(col0):
    return pl.BlockSpec((S, 128), lambda p: (0, col0 + p))


def _band_rows(i, d):
    nb = S // d // BLK
    r, b = i // nb, i % nb
    cur = pl.ds(b * BLK * d + r, BLK, stride=d)
    prev = pl.ds(jnp.maximum(b - 1, 0) * BLK * d + r, BLK, stride=d)
    return cur, prev, jnp.minimum(b, 1)


def _band_bias(tab_ref, bidx_ref, h):
    bi = bidx_ref[...]
    bias = jnp.zeros((BLK, 2 * BLK), F32)
    for kk in range(NUM_BUCKETS):
        bias = jnp.where(bi == kk, tab_ref[kk, h], bias)
    return bias


def _lane_lo(rows=BLK):
    return lax.broadcasted_iota(jnp.int32, (rows, 128), 1) < HD


def _per_head(x, lo):
    return (jnp.sum(jnp.where(lo, x, 0.0), axis=1, keepdims=True) * (1.0 / HD),
            jnp.sum(jnp.where(lo, 0.0, x), axis=1, keepdims=True) * (1.0 / HD))


def _band_fill(bias_ref, tab_ref, bidx_ref, head, maxd):
    a = lax.broadcasted_iota(jnp.int32, (BLK, 2 * BLK), 0)
    c = lax.broadcasted_iota(jnp.int32, (BLK, 2 * BLK), 1)
    dist = a + BLK - c
    in_band = jnp.logical_and(dist >= 0, dist <= maxd)
    for h in range(2):
        bias = jnp.where(in_band, _band_bias(tab_ref, bidx_ref, head + h), NEG)
        bias_ref[1, h * BLK:(h + 1) * BLK, :] = bias
        bias_ref[0, h * BLK:(h + 1) * BLK, :] = jnp.where(c >= BLK, bias, NEG)


def _stack_heads(x, lo, dtype=BF16):
    return jnp.concatenate([jnp.where(lo, x, 0.0), jnp.where(lo, 0.0, x)], axis=0).astype(dtype)


def _unstack_heads(x, lo):
    n = x.shape[0] // 2
    return jnp.where(lo, x[:n], x[n:])


def _stack_rows(ref, prev, cur):
    return jnp.concatenate([ref[prev, :], ref[cur, :]], axis=0).astype(BF16)


def _band_fwd(name, d, n_pairs, maxd, head0, srcs, bidx_g, tab, sinks):
    (qa, qc), (ka, kc), (va, vc) = srcs
    out_spec = _pair_spec(0)
    smem = pl.BlockSpec(memory_space=pltpu.SMEM)
    full = pl.BlockSpec((BLK, 2 * BLK), lambda p: (0, 0))

    def body(tab_ref, sink_ref, q_ref, k_ref, v_ref, bidx_ref, o_ref, lse_ref, bias_ref):
        p = pl.program_id(0)
        _band_fill(bias_ref, tab_ref, bidx_ref, head0 + 2 * p, maxd)
        lo = _lane_lo()
        sink = jnp.where(lax.broadcasted_iota(jnp.int32, (2 * BLK, 1), 0) < BLK, sink_ref[2 * p], sink_ref[2 * p + 1])

        def block(i, carry):
            cur, prev, has_prev = _band_rows(i, d)
            qs = _stack_heads(q_ref[cur, :] * SCALE, lo)
            ks, vs = _stack_rows(k_ref, prev, cur), _stack_rows(v_ref, prev, cur)
            s = _dot(qs, ks, NT) + bias_ref[has_prev]
            m = jnp.max(s, axis=1, keepdims=True)
            pr = jnp.exp(s - m)
            l = jnp.sum(pr, axis=1, keepdims=True)
            num = _dot(pr.astype(BF16), vs, NN)
            lse = m + jnp.log(l)
            sig = 1.0 / (1.0 + jnp.exp(sink - lse))
            o_ref[cur, :] = _unstack_heads(num * (sig / l), lo)
            lse_ref[cur, :] = _unstack_heads(lse + jnp.zeros((2 * BLK, 128), F32), lo)
            return carry

        lax.fori_loop(0, NQB, block, 0, unroll=2)

    shape = jax.ShapeDtypeStruct((S, n_pairs * 128), F32)
    return pl.pallas_call(
        body, name=name, grid=(n_pairs,),
        in_specs=[smem, smem, _pair_spec(qc), _pair_spec(kc), _pair_spec(vc), full],
        out_specs=[out_spec, out_spec], out_shape=[shape, shape],
        scratch_shapes=[pltpu.VMEM((2, 2 * BLK, 2 * BLK), F32)],
        compiler_params=_params(("parallel",)))(tab, sinks, qa, ka, va, bidx_g)


def _band_bwd(name, d, n_pairs, maxd, head0, srcs, bidx_g, tab, sinks, o, lse, do, stats_in):
    (qa, qc), (ka, kc), (va, vc) = srcs
    pair = _pair_spec(0)
    smem = pl.BlockSpec(memory_space=pltpu.SMEM)
    full = pl.BlockSpec((BLK, 2 * BLK), lambda p: (0, 0))
    stat_spec = pl.BlockSpec((2, 8, 128), lambda p: (p, 0, 0))

    def body(tab_ref, sink_ref, q_ref, k_ref, v_ref, bidx_ref, o_ref, lse_ref, do_ref, sin_ref,
             dq_ref, dk_ref, dv_ref, stat_ref, bias_ref, dsacc_ref, sk_ref):
        p = pl.program_id(0)
        _band_fill(bias_ref, tab_ref, bidx_ref, head0 + 2 * p, maxd)
        dsacc_ref[...] = jnp.zeros_like(dsacc_ref)
        sk_ref[...] = jnp.zeros_like(sk_ref)
        dk_ref[...] = jnp.zeros_like(dk_ref)
        dv_ref[...] = jnp.zeros_like(dv_ref)
        lo = _lane_lo()
        head1 = lax.broadcasted_iota(jnp.int32, (2 * BLK, 1), 0) >= BLK
        sink = jnp.where(head1, sink_ref[2 * p + 1], sink_ref[2 * p])

        def block(i, carry):
            cur, prev, has_prev = _band_rows(i, d)
            qs = _stack_heads(q_ref[cur, :] * SCALE, lo)
            ks, vs = _stack_rows(k_ref, prev, cur), _stack_rows(v_ref, prev, cur)
            do = do_ref[cur, :]
            dos = _stack_heads(do, lo, F32)
            lse = jnp.concatenate(_per_head(lse_ref[cur, :], lo), axis=0)
            prod = do * o_ref[cur, :]
            delta = jnp.concatenate([jnp.sum(jnp.where(lo, prod, 0.0), axis=1, keepdims=True),
                                     jnp.sum(jnp.where(lo, 0.0, prod), axis=1, keepdims=True)], axis=0)
            sig = 1.0 / (1.0 + jnp.exp(sink - lse))
            pr = jnp.exp(_dot(qs, ks, NT) + bias_ref[has_prev] - lse)
            ds = pr * (sig * (_dot(dos.astype(BF16), vs, NT) - delta))
            dsb = ds.astype(BF16)
            dq_ref[cur, :] = SCALE * _unstack_heads(_dot(dsb, ks, NN), lo)
            dk = _dot(dsb, qs, TN)
            dv = _dot(pr.astype(BF16), (sig * dos).astype(BF16), TN)
            dk_ref[prev, :] += dk[:BLK]
            dk_ref[cur, :] += dk[BLK:]
            dv_ref[prev, :] += dv[:BLK]
            dv_ref[cur, :] += dv[BLK:]
            dsacc_ref[...] += ds
            sink_grad = -delta * (1.0 - sig)
            for h in range(2):
                sk_ref[h] += jnp.zeros((8, 128), F32) + jnp.sum(sink_grad[h * BLK:(h + 1) * BLK])
            return carry

        lax.fori_loop(0, NQB, block, 0, unroll=2)

        bi = bidx_ref[...]
        lane = lax.broadcasted_iota(jnp.int32, (8, 128), 1)
        sub = lax.broadcasted_iota(jnp.int32, (8, 128), 0)
        for h in range(2):
            acc = dsacc_ref[h * BLK:(h + 1) * BLK, :]
            row = jnp.where(jnp.logical_and(sub == 1, lane == 0), sk_ref[h], 0.0)
            for kk in range(NUM_BUCKETS):
                tot = jnp.sum(jnp.where(bi == kk, acc, 0.0))
                row = jnp.where(jnp.logical_and(sub == 0, lane == kk), tot, row)
            stat_ref[h] = row + jnp.where(sub == 0, sin_ref[h], 0.0)

    shape = jax.ShapeDtypeStruct((S, n_pairs * 128), F32)
    return pl.pallas_call(
        body, name=name, grid=(n_pairs,),
        in_specs=[smem, smem, _pair_spec(qc), _pair_spec(kc), _pair_spec(vc), full, pair, pair, pair, stat_spec],
        out_specs=[pair, pair, pair, stat_spec],
        out_shape=[shape, shape, shape, jax.ShapeDtypeStruct((2 * n_pairs, 8, 128), F32)],
        scratch_shapes=[pltpu.VMEM((2, 2 * BLK, 2 * BLK), F32), pltpu.VMEM((2 * BLK, 2 * BLK), F32),
                        pltpu.VMEM((2, 8, 128), F32)],
        compiler_params=_params(("parallel",)))(tab, sinks, qa, ka, va, bidx_g, o, lse, do, stats_in)


def _comb_fwd(o_g, lse_g):
    def body(o0, o1, o2, l0, l1, l2, out_ref, outb_ref, lse_ref):
        a0, a1, a2 = l0[...], l1[...], l2[...]
        m = jnp.maximum(jnp.maximum(a0, a1), a2)
        e0, e1, e2 = jnp.exp(a0 - m), jnp.exp(a1 - m), jnp.exp(a2 - m)
        tot = e0 + e1 + e2
        out = (e0 * o0[...] + e1 * o1[...] + e2 * o2[...]) / tot
        out_ref[...] = out
        outb_ref[...] = out.astype(BF16)
        lse_ref[...] = m + jnp.log(tot)

    spec = _row_spec(4 * HD)
    f32 = jax.ShapeDtypeStruct((S, 4 * HD), F32)
    return pl.pallas_call(
        body, name="comb_fwd", grid=(S // TR,), in_specs=[spec] * 6, out_specs=[spec] * 3,
        out_shape=[f32, jax.ShapeDtypeStruct((S, 4 * HD), BF16), f32],
        compiler_params=_params(("parallel",)))(*o_g, *lse_g)


def _split2(x):
    hi = x.astype(BF16)
    return hi, (x - hi.astype(F32)).astype(BF16)


KB = 2 * BLK
SBQ = 2 * BLK


def _tri_sum(x, tri):
    hi, lo = _split2(x)
    both = _dot(jnp.concatenate([hi, lo], axis=0), tri, NN)
    return both[:x.shape[0]] + both[x.shape[0]:]


def _tri(strict_upper):
    r = lax.broadcasted_iota(jnp.int32, (KB, KB), 0)
    c = lax.broadcasted_iota(jnp.int32, (KB, KB), 1)
    return jnp.where(r > c if strict_upper else r < c, 1.0, 0.0).astype(BF16)


def _sb_terms(qs, kj, before):
    z = _dot(qs, kj, NT)
    lsp = jnp.minimum(z, 0.0) - jnp.log(1.0 + jnp.exp(-jnp.abs(z)))
    return lsp, jnp.where(before, lsp - z, 0.0)


def _sb_before(i, m):
    t = (lax.broadcasted_iota(jnp.int32, (2 * SBQ, KB), 0) & (SBQ - 1)) + i * SBQ
    s = lax.broadcasted_iota(jnp.int32, (2 * SBQ, KB), 1) + m * KB
    return s < t


C_COL = 3072 // 128


def _sb_fwd(proj):
    blk = lambda off: pl.BlockSpec((SBQ, 128), lambda p, i: (i, off + p))
    col = lambda off: pl.BlockSpec((S, 128), lambda p, i: (0, off + p))
    out = pl.BlockSpec((SBQ, 128), lambda p, i: (i, p))

    def body(q_ref, k_ref, v_ref, o_ref, ob_ref, tot_ref):
        i = pl.program_id(1)
        lo = _lane_lo(SBQ)
        qs = _stack_heads(q_ref[...] * SCALE, lo)
        suffix = _tri(True)

        def step(n, carry):
            acc, rest = carry
            m = i - n
            rows = pl.ds(pl.multiple_of(m * KB, KB), KB)
            kj, vj = k_ref[rows, :].astype(BF16), v_ref[rows, :].astype(BF16)
            before = _sb_before(i, m)
            lsp, lk = _sb_terms(qs, kj, before)
            w = jnp.where(before, jnp.exp(lsp + _tri_sum(lk, suffix) + rest), 0.0)
            return acc + _dot(w.astype(BF16), vj, NN), rest + jnp.sum(lk, axis=1, keepdims=True)

        acc, rest = lax.fori_loop(0, i + 1, step, (jnp.zeros((2 * SBQ, 128), F32), jnp.zeros((2 * SBQ, 1), F32)))
        o = _unstack_heads(acc, lo)
        o_ref[...] = o
        ob_ref[...] = o.astype(BF16)
        tot_ref[...] = _unstack_heads(rest + jnp.zeros((2 * SBQ, 128), F32), lo)

    f32 = jax.ShapeDtypeStruct((S, 4 * HD), F32)
    return pl.pallas_call(
        body, name="sb_fwd", grid=(2, S // SBQ), in_specs=[blk(C_COL), col(C_COL + 2), col(C_COL + 4)],
        out_specs=[out, out, out], out_shape=[f32, jax.ShapeDtypeStruct((S, 4 * HD), BF16), f32],
        compiler_params=_params(("parallel", "arbitrary")))(proj, proj, proj)


def _sb_bwd(proj, tot, do):
    blk = lambda off: pl.BlockSpec((SBQ, 128), lambda p, i: (i, off + p))
    col = lambda off: pl.BlockSpec((S, 128), lambda p, i: (0, off + p))

    def body(q_ref, k_ref, v_ref, tot_ref, do_ref, dq_ref, dk_ref, dv_ref):
        i = pl.program_id(1)

        @pl.when(i == 0)
        def _():
            dk_ref[...] = jnp.zeros_like(dk_ref)
            dv_ref[...] = jnp.zeros_like(dv_ref)

        lo = _lane_lo(SBQ)
        qs = _stack_heads(q_ref[...] * SCALE, lo)
        dos = _stack_heads(do_ref[...], lo)
        tots = jnp.concatenate(_per_head(tot_ref[...], lo), axis=0)
        prefix = _tri(False)

        def step(m, carry):
            dq, keep_left, g_left = carry
            rows = pl.ds(pl.multiple_of(m * KB, KB), KB)
            kj, vj = k_ref[rows, :].astype(BF16), v_ref[rows, :].astype(BF16)
            before = _sb_before(i, m)
            lsp, lk = _sb_terms(qs, kj, before)
            log_rest = tots - keep_left - lk - _tri_sum(lk, prefix)
            w = jnp.where(before, jnp.exp(lsp + log_rest), 0.0)
            g = w * _dot(dos, vj, NT)
            g_before = g_left + _dot(g.astype(BF16), prefix, NN)
            beta = jnp.exp(lsp)
            dz = jnp.where(before, g * (1.0 - beta) - g_before * beta, 0.0).astype(BF16)
            dk_ref[rows, :] += _dot(dz, qs, TN)
            dv_ref[rows, :] += _dot(w.astype(BF16), dos, TN)
            return (dq + _dot(dz, kj, NN), keep_left + jnp.sum(lk, axis=1, keepdims=True),
                    g_left + jnp.sum(g, axis=1, keepdims=True))

        zero = (jnp.zeros((2 * SBQ, 128), F32), jnp.zeros((2 * SBQ, 1), F32), jnp.zeros((2 * SBQ, 1), F32))
        dq, _, _ = lax.fori_loop(0, i + 1, step, zero)
        dq_ref[...] = SCALE * _unstack_heads(dq, lo)

    out_blk = pl.BlockSpec((SBQ, 128), lambda p, i: (i, p))
    out_col = pl.BlockSpec((S, 128), lambda p, i: (0, p))
    f32 = jax.ShapeDtypeStruct((S, 4 * HD), F32)
    return pl.pallas_call(
        body, name="sb_bwd", grid=(2, S // SBQ),
        in_specs=[blk(C_COL), col(C_COL + 2), col(C_COL + 4), out_blk, out_blk],
        out_specs=[out_blk, out_col, out_col], out_shape=[f32, f32, f32],
        compiler_params=_params(("arbitrary", "arbitrary")))(proj, proj, proj, tot, do)


TG = 256
TGR = 1024
GATE_BLK0 = OFF_GATE // TG


def _gate_specs():
    grid = (D // TG, S // TGR)
    p_specs = [pl.BlockSpec((TGR, TG), functools.partial(lambda c, r, br: (r, GATE_BLK0 + br * (D // TG) + c), br=br))
               for br in range(3)]
    b_spec = pl.BlockSpec((3, TG), lambda c, r: (0, c))
    t_spec = pl.BlockSpec((TGR, TG), lambda c, r: (r, c))
    return grid, p_specs, b_spec, t_spec


def _sigmoid(x):
    return 1.0 / (1.0 + jnp.exp(-x))


def _three_rows(rows):
    sub = lax.broadcasted_iota(jnp.int32, (3, rows[0].shape[1]), 0)
    return jnp.where(sub == 0, rows[0], jnp.where(sub == 1, rows[1], rows[2]))


def _gate_fwd(proj, b_gate, br):
    grid, p_specs, b_spec, t_spec = _gate_specs()

    def body(p0, p1, p2, b_ref, r0, r1, r2, out_ref):
        acc = jnp.zeros((TGR, TG), F32)
        for n, (p, r) in enumerate(((p0, r0), (p1, r1), (p2, r2))):
            acc += _sigmoid(p[...] + b_ref[n:n + 1, :]) * r[...]
        out_ref[...] = acc.astype(BF16)

    return pl.pallas_call(
        body, name="gate_fwd", grid=grid, in_specs=p_specs + [b_spec] + [t_spec] * 3, out_specs=t_spec,
        out_shape=jax.ShapeDtypeStruct((S, D), BF16),
        compiler_params=_params(("parallel", "parallel")))(proj, proj, proj, b_gate, *br)


def _gate_bwd(proj, b_gate, br, dmerged):
    grid, p_specs, b_spec, t_spec = _gate_specs()

    def body(p0, p1, p2, b_ref, r0, r1, r2, dm_ref, e0, e1, e2, g0, g1, g2, db_ref):
        dm = dm_ref[...]
        rows = []
        for n, (p, r, e_ref, dg_ref) in enumerate(((p0, r0, e0, g0), (p1, r1, e1, g1), (p2, r2, e2, g2))):
            g = _sigmoid(p[...] + b_ref[n:n + 1, :])
            e_ref[...] = (dm * g).astype(BF16)
            dpre = dm * r[...] * g * (1.0 - g)
            dg_ref[...] = dpre.astype(BF16)
            rows.append(jnp.sum(dpre, axis=0, keepdims=True))
        db = _three_rows(rows)

        @pl.when(pl.program_id(1) == 0)
        def _():
            db_ref[...] = db

        @pl.when(pl.program_id(1) > 0)
        def _():
            db_ref[...] += db

    bf = jax.ShapeDtypeStruct((S, D), BF16)
    out = pl.pallas_call(
        body, name="gate_bwd", grid=grid, in_specs=p_specs + [b_spec] + [t_spec] * 4,
        out_specs=[t_spec] * 6 + [b_spec], out_shape=[bf] * 6 + [jax.ShapeDtypeStruct((3, D), F32)],
        compiler_params=_params(("parallel", "arbitrary")))(proj, proj, proj, b_gate, *br, dmerged)
    return out[:3], out[3:6], out[6]


TC = 256
N_FF_BLK = D_FF // TC
GELU_C = math.sqrt(2.0 / math.pi)


def _shift_down(x, n):
    rows = lax.broadcasted_iota(jnp.int32, x.shape, 0)
    return jnp.where(rows >= n, pltpu.roll(x, n, axis=0), 0.0)


def _shift_up(x, n):
    rows = lax.broadcasted_iota(jnp.int32, x.shape, 0)
    return jnp.where(rows < x.shape[0] - n, pltpu.roll(x, x.shape[0] - n, axis=0), 0.0)


def _conv(u, w, b):
    s1, s2 = _shift_down(u, 1), _shift_down(u, 2)
    return w[2:3, :] * u + w[1:2, :] * s1 + w[0:1, :] * s2 + b, s1, s2


def _gelu_parts(x):
    inner = GELU_C * (x + 0.044715 * x * x * x)
    t = jnp.tanh(inner)
    gelu = 0.5 * x * (1.0 + t)
    dgelu = 0.5 * (1.0 + t) + 0.5 * x * (1.0 - t * t) * GELU_C * (1.0 + 3 * 0.044715 * x * x)
    return gelu, dgelu


def _conv_specs():
    ug = pl.BlockSpec((S, TC), lambda c: (0, c))
    uv = pl.BlockSpec((S, TC), lambda c: (0, N_FF_BLK + c))
    wg = pl.BlockSpec((3, TC), lambda c: (0, c))
    wv = pl.BlockSpec((3, TC), lambda c: (0, N_FF_BLK + c))
    bg = pl.BlockSpec((1, TC), lambda c: (0, c))
    bv = pl.BlockSpec((1, TC), lambda c: (0, N_FF_BLK + c))
    return ug, uv, wg, wv, bg, bv


def _conv_fwd(u, conv_w, conv_b):
    ug, uv, wg, wv, bg, bv = _conv_specs()

    def body(ug_ref, uv_ref, wg_ref, wv_ref, bg_ref, bv_ref, a_ref):
        gc = _conv(ug_ref[...], wg_ref[...], bg_ref[...])[0]
        vc = _conv(uv_ref[...], wv_ref[...], bv_ref[...])[0]
        a_ref[...] = (_gelu_parts(gc)[0] * vc).astype(BF16)

    return pl.pallas_call(
        body, name="conv_fwd", grid=(N_FF_BLK,), in_specs=[ug, uv, wg, wv, bg, bv], out_specs=ug,
        out_shape=jax.ShapeDtypeStruct((S, D_FF), BF16),
        compiler_params=_params(("parallel",)))(u, u, conv_w, conv_w, conv_b, conv_b)


def _conv_bwd(u, conv_w, conv_b, da):
    ug, uv, wg, wv, bg, bv = _conv_specs()

    def back(duc, u, s1, s2, w):
        du = w[2:3, :] * duc + w[1:2, :] * _shift_up(duc, 1) + w[0:1, :] * _shift_up(duc, 2)
        dw = _three_rows([jnp.sum(duc * s2, axis=0, keepdims=True), jnp.sum(duc * s1, axis=0, keepdims=True),
                          jnp.sum(duc * u, axis=0, keepdims=True)])
        return du, dw, jnp.sum(duc, axis=0, keepdims=True)

    def body(ug_ref, uv_ref, wg_ref, wv_ref, bg_ref, bv_ref, da_ref, dug_ref, duv_ref, dwg_ref, dwv_ref, dbg_ref, dbv_ref):
        u_g, u_v = ug_ref[...], uv_ref[...]
        gc, g1, g2 = _conv(u_g, wg_ref[...], bg_ref[...])
        vc, v1, v2 = _conv(u_v, wv_ref[...], bv_ref[...])
        gelu, dgelu = _gelu_parts(gc)
        da = da_ref[...]
        du, dw, db = back(da * vc * dgelu, u_g, g1, g2, wg_ref[...])
        dug_ref[...] = du.astype(BF16)
        dwg_ref[...] = dw
        dbg_ref[...] = db
        du, dw, db = back(da * gelu, u_v, v1, v2, wv_ref[...])
        duv_ref[...] = du.astype(BF16)
        dwv_ref[...] = dw
        dbv_ref[...] = db

    return pl.pallas_call(
        body, name="conv_bwd", grid=(N_FF_BLK,), in_specs=[ug, uv, wg, wv, bg, bv, ug],
        out_specs=[ug, ug, wg, wg, bg, bg],
        out_shape=[jax.ShapeDtypeStruct((S, D_FF), BF16), jax.ShapeDtypeStruct((S, D_FF), BF16),
                   jax.ShapeDtypeStruct((3, D_FF), F32), jax.ShapeDtypeStruct((3, D_FF), F32),
                   jax.ShapeDtypeStruct((1, D_FF), F32), jax.ShapeDtypeStruct((1, D_FF), F32)],
        compiler_params=_params(("parallel",)))(u, u, conv_w, conv_w, conv_b, conv_b, da)


def _adamw(name, w, g, m, v):
    shape = w.shape
    cols = shape[-1]
    flat = [t.reshape(-1, cols) for t in (w, g, m, v)]
    r = flat[0].shape[0]
    tr = min(256, r)

    def body(w_ref, g_ref, m_ref, v_ref, d_ref, mo_ref, vo_ref):
        g = g_ref[...]
        m = ADAM_B1 * m_ref[...] + (1.0 - ADAM_B1) * g
        v = ADAM_B2 * v_ref[...] + (1.0 - ADAM_B2) * (g * g)
        m_hat = m / (1.0 - ADAM_B1 ** ADAM_STEP)
        v_hat = v / (1.0 - ADAM_B2 ** ADAM_STEP)
        d_ref[...] = -ADAM_LR * (m_hat / (jnp.sqrt(v_hat) + ADAM_EPS) + ADAM_WD * w_ref[...])
        mo_ref[...] = m
        vo_ref[...] = v

    spec = pl.BlockSpec((tr, cols), lambda i: (i, 0))
    outs = pl.pallas_call(
        body, name=name, grid=(pl.cdiv(r, tr),), in_specs=[spec] * 4, out_specs=[spec] * 3,
        out_shape=[jax.ShapeDtypeStruct((r, cols), F32)] * 3, compiler_params=_params(("parallel",)))(*flat)
    return [t.reshape(shape) for t in outs]


def _place():
    x, y, c = lax.axis_index("x"), lax.axis_index("y"), lax.axis_index("c")
    chips = [(1 - x, y), (x, 1 - y), (1 - x, 1 - y)]
    return x, y, c, chips


def _scalars(*vals):
    return jnp.stack([jnp.asarray(v, jnp.int32) for v in vals])


HBM = pl.BlockSpec(memory_space=pltpu.HBM)
SEM = pl.BlockSpec(memory_space=pltpu.SEMAPHORE)
SPLIT_COPY = pltpu.CompilerParams(has_side_effects=pltpu.SideEffectType.DATAFLOW_SIDE_EFFECTING)


def _in_hbm(x):
    return pltpu.with_memory_space_constraint(x, pltpu.HBM)


def _cast_into_slot(name, w, layer, chip):
    _, k, n4 = w.shape
    tr = max(t for t in range(16, 257, 16) if k % t == 0)

    def body(chip_ref, w_ref, o_ref):
        o_ref[...] = w_ref[...].astype(BF16)

    return pl.pallas_call(
        body, name=name,
        grid_spec=pltpu.PrefetchScalarGridSpec(
            num_scalar_prefetch=1, grid=(k // tr,),
            in_specs=[pl.BlockSpec((None, tr, n4), lambda i, s: (layer, i, 0))],
            out_specs=pl.BlockSpec((None, tr, n4), lambda i, s: (s[0], i, 0))),
        out_shape=jax.ShapeDtypeStruct((N_CHIPS, k, n4), BF16),
        compiler_params=_params(("parallel",)))(_scalars(chip), w)


def _gather_copy(buf_ref, k, from_chip, send_sem, recv_sem, chips, c, half=False):
    rows = buf_ref.at[from_chip]
    if half:
        h = buf_ref.shape[1] // 2
        rows = buf_ref.at[from_chip, pl.ds(pl.multiple_of(c * h, h), h)]
    return pltpu.make_async_remote_copy(src_ref=rows, dst_ref=rows, send_sem=send_sem, recv_sem=recv_sem,
                                        device_id=(*chips[k], c), device_id_type=MESH)


def _gather_start(name, bufs, groups, halved=()):
    n, ng = len(bufs), len(groups)
    where = {a: (gi, e) for gi, g in enumerate(groups) for e, a in enumerate(g)}

    def body(*refs):
        ins, sems, token = refs[:n], refs[n:n + 2 * ng], refs[-1]
        x, y, c, chips = _place()
        for a in range(n):
            gi, e = where[a]
            for k in range(3):
                _gather_copy(ins[a], k, 2 * x + y, sems[2 * gi].at[3 * e + k], sems[2 * gi + 1].at[3 * e + k],
                             chips, c, a in halved).start()
        token[...] = jnp.zeros_like(token)

    out_shape = [pltpu.SemaphoreType.DMA((3 * len(g),)) for g in groups for _ in range(2)]
    out_shape += [pltpu.HBM(b.shape, b.dtype) for b in bufs] + [jax.ShapeDtypeStruct((8, 128), F32)]
    out = pl.pallas_call(
        body, name=name, in_specs=[HBM] * n,
        out_specs=[SEM] * (2 * ng) + [HBM] * n + [pl.BlockSpec(memory_space=pltpu.VMEM)], out_shape=out_shape,
        input_output_aliases={a: 2 * ng + a for a in range(n)}, compiler_params=SPLIT_COPY)(*[_in_hbm(b) for b in bufs])
    sems = [(out[2 * gi], out[2 * gi + 1]) for gi in range(ng)]
    return sems, list(out[2 * ng:2 * ng + n]), out[-1]


def _gather_wait(name, bufs, send, recv, after, halved=()):
    n = len(bufs)

    def body(*refs):
        ins, send_sem, recv_sem = refs[:n], refs[n], refs[n + 1]
        x, y, c, chips = _place()
        for e in range(n):
            for k in range(3):
                sems = (send_sem.at[3 * e + k], recv_sem.at[3 * e + k])
                _gather_copy(ins[e], k, 2 * x + y, *sems, chips, c, e in halved).wait_send()
                _gather_copy(ins[e], k, 2 * chips[k][0] + chips[k][1], *sems, chips, c, e in halved).wait_recv()

    return pl.pallas_call(
        body, name=name, in_specs=[HBM] * n + [SEM, SEM, ANY], out_specs=[HBM] * n,
        out_shape=[pltpu.HBM(b.shape, b.dtype) for b in bufs],
        input_output_aliases={a: a for a in range(n)}, compiler_params=SPLIT_COPY)(*bufs, send, recv, after)


def _swap_halves(name, bufs):
    n = len(bufs)

    def body(*refs):
        ins, outs = refs[:n], refs[n:2 * n]
        send_sem, recv_sem = refs[2 * n:]
        x, y, c, chips = _place()

        def piece(ref, k, which):
            h = ref.shape[1] // 2
            return ref.at[2 * chips[k][0] + chips[k][1], pl.ds(pl.multiple_of(which * h, h), h)]

        def copy(a, k, which):
            return pltpu.make_async_remote_copy(
                src_ref=piece(ins[a], k, c), dst_ref=piece(outs[a], k, which), send_sem=send_sem.at[3 * a + k],
                recv_sem=recv_sem.at[3 * a + k], device_id=(x, y, 1 - c), device_id_type=MESH)

        for a in range(n):
            for k in range(3):
                copy(a, k, c).start()
        for a in range(n):
            for k in range(3):
                copy(a, k, c).wait_send()
                copy(a, k, 1 - c).wait_recv()

    return pl.pallas_call(
        body, name=name, in_specs=[ANY] * n, out_specs=[ANY] * n,
        out_shape=[jax.ShapeDtypeStruct(b.shape, b.dtype) for b in bufs],
        input_output_aliases={a: a for a in range(n)},
        scratch_shapes=[pltpu.SemaphoreType.DMA((3 * n,)), pltpu.SemaphoreType.DMA((3 * n,))],
    )(*bufs)


def _reduce_copy(g_ref, land_ref, mask, send_sem, recv_sem, x, y, c, sending):
    px, py, pc = x ^ ((mask >> 2) & 1), y ^ ((mask >> 1) & 1), c ^ (mask & 1)
    half = g_ref.shape[1] // 2
    src = g_ref.at[2 * px + py, pl.ds(pl.multiple_of(pc * half, half), half)]
    dst = land_ref.at[4 * x + 2 * y + c] if sending else land_ref.at[4 * px + 2 * py + pc]
    return pltpu.make_async_remote_copy(src_ref=src, dst_ref=dst, send_sem=send_sem, recv_sem=recv_sem,
                                        device_id=(px, py, pc), device_id_type=MESH)


def _reduce_start(name, grads):
    n = len(grads)
    lands = [lax.empty((N_DEV, g.shape[1] // 2, g.shape[2]), g.dtype) for g in grads]

    def body(*refs):
        gs, ls, send_sem, recv_sem = refs[:n], refs[n:2 * n], refs[2 * n], refs[2 * n + 1]
        x, y, c, _ = _place()
        for a in range(n):
            for mask in range(1, N_DEV):
                s = (N_DEV - 1) * a + mask - 1
                _reduce_copy(gs[a], ls[a], mask, send_sem.at[s], recv_sem.at[s], x, y, c, True).start()
        refs[-1][...] = jnp.zeros_like(refs[-1])

    sem = pltpu.SemaphoreType.DMA((n * (N_DEV - 1),))
    out = pl.pallas_call(
        body, name=name, in_specs=[HBM] * (2 * n),
        out_specs=[SEM, SEM] + [HBM] * (2 * n) + [pl.BlockSpec(memory_space=pltpu.VMEM)],
        out_shape=[sem, sem] + [pltpu.HBM(t.shape, t.dtype) for t in grads + lands] + [jax.ShapeDtypeStruct((8, 128), F32)],
        input_output_aliases={a: 2 + a for a in range(2 * n)}, compiler_params=SPLIT_COPY)(
            *[_in_hbm(t) for t in grads + lands])
    return out[0], out[1], list(out[2:2 + n]), list(out[2 + n:2 + 2 * n]), out[-1]


def _reduce_wait(name, send, recv, grads, lands, after):
    n = len(grads)

    def body(*refs):
        gs, ls, send_sem, recv_sem = refs[:n], refs[n:2 * n], refs[2 * n], refs[2 * n + 1]
        x, y, c, _ = _place()
        for a in range(n):
            for mask in range(1, N_DEV):
                s = (N_DEV - 1) * a + mask - 1
                sems = (send_sem.at[s], recv_sem.at[s])
                _reduce_copy(gs[a], ls[a], mask, *sems, x, y, c, True).wait_send()
                _reduce_copy(gs[a], ls[a], mask, *sems, x, y, c, False).wait_recv()

    out = pl.pallas_call(
        body, name=name, in_specs=[HBM] * (2 * n) + [SEM, SEM, ANY], out_specs=[HBM] * (2 * n),
        out_shape=[pltpu.HBM(t.shape, t.dtype) for t in grads + lands],
        input_output_aliases={a: a for a in range(2 * n)}, compiler_params=SPLIT_COPY)(*grads, *lands, send, recv, after)
    return list(out[:n]), list(out[n:])


def _reduce_sum(name, g, land, layer, into, chip, c):
    _, k4, n4 = g.shape
    half = k4 // 2
    tr = max(t for t in range(16, 513, 16) if half % t == 0)
    per = half // tr
    me = 2 * chip + c

    def body(s_ref, own_ref, *refs):
        total = own_ref[...].astype(F32)
        for ref in refs[:N_DEV - 1]:
            total = total + ref[...].astype(F32)
        refs[-1][...] = total

    in_specs = [pl.BlockSpec((None, tr, n4), lambda i, s: (s[0], s[1] * per + i, 0))]
    in_specs += [pl.BlockSpec((None, tr, n4), functools.partial(lambda i, s, m: (s[1 + m], i, 0), m=m))
                 for m in range(1, N_DEV)]
    ins = [g] + [land] * (N_DEV - 1)
    aliases = {}
    if into is not None:
        in_specs, ins, aliases = in_specs + [ANY], ins + [into], {1 + N_DEV: 0}
    return pl.pallas_call(
        body, name=name,
        grid_spec=pltpu.PrefetchScalarGridSpec(
            num_scalar_prefetch=1, grid=(per,), in_specs=in_specs,
            out_specs=pl.BlockSpec((None, tr, n4), lambda i, s: (layer, s[1] * per + i, 0))),
        out_shape=jax.ShapeDtypeStruct((DEPTH, k4, n4), F32), input_output_aliases=aliases,
        compiler_params=_params(("parallel",)))(_scalars(chip, c, *[me ^ m for m in range(1, N_DEV)]), *ins)


def _join_halves(name, bufs):
    n = len(bufs)

    def body(*refs):
        ins, outs = refs[:n], refs[n:2 * n]
        send_sem, recv_sem = refs[2 * n:]
        x, y, c, _ = _place()

        def rows(ref, which):
            half = ref.shape[1] // 2
            return ref.at[:, pl.ds(pl.multiple_of(which * half, half), half)]

        sends = [pltpu.make_async_remote_copy(
            src_ref=rows(ins[a], c), dst_ref=rows(outs[a], c), send_sem=send_sem.at[a], recv_sem=recv_sem.at[a],
            device_id=(x, y, 1 - c), device_id_type=MESH) for a in range(n)]
        for cp in sends:
            cp.start()
        for a in range(n):
            sends[a].wait_send()
            pltpu.make_async_remote_copy(
                src_ref=rows(ins[a], c), dst_ref=rows(outs[a], 1 - c), send_sem=send_sem.at[a], recv_sem=recv_sem.at[a],
                device_id=(x, y, 1 - c), device_id_type=MESH).wait_recv()

    return pl.pallas_call(
        body, name=name, in_specs=[ANY] * n, out_specs=[ANY] * n,
        out_shape=[jax.ShapeDtypeStruct(b.shape, b.dtype) for b in bufs],
        input_output_aliases={a: a for a in range(n)},
        scratch_shapes=[pltpu.SemaphoreType.DMA((n,)), pltpu.SemaphoreType.DMA((n,))],
    )(*bufs)


def _all_reduce_small(block):
    r = block.shape[0]

    def body(x_ref, out_ref, slots, send_sem, recv_sem):
        x, y, c, _ = _place()
        me = 4 * x + 2 * y + c
        slots[me] = x_ref[...]
        sends = []
        for mask in range(1, N_DEV):
            fx, fy, fc = (mask >> 2) & 1, (mask >> 1) & 1, mask & 1
            peer = (x ^ fx, y ^ fy, c ^ fc)
            cp = pltpu.make_async_remote_copy(
                src_ref=x_ref, dst_ref=slots.at[me], send_sem=send_sem.at[mask - 1], recv_sem=recv_sem.at[mask - 1],
                device_id=peer, device_id_type=MESH)
            cp.start()
            sends.append(cp)
        for mask in range(1, N_DEV):
            src = me ^ mask
            pltpu.make_async_remote_copy(
                src_ref=x_ref, dst_ref=slots.at[src], send_sem=send_sem.at[mask - 1], recv_sem=recv_sem.at[mask - 1],
                device_id=(x, y, c), device_id_type=MESH).wait_recv()
        for cp in sends:
            cp.wait_send()
        total = slots[0]
        for d in range(1, N_DEV):
            total = total + slots[d]
        out_ref[...] = total

    vmem = pl.BlockSpec(memory_space=pltpu.VMEM)
    return pl.pallas_call(
        body, name="all_reduce_small", in_specs=[vmem], out_specs=vmem,
        out_shape=jax.ShapeDtypeStruct((r, 128), F32),
        scratch_shapes=[pltpu.VMEM((N_DEV, r, 128), F32), pltpu.SemaphoreType.DMA((N_DEV - 1,)),
                        pltpu.SemaphoreType.DMA((N_DEV - 1,))],
        compiler_params=pltpu.CompilerParams(vmem_limit_bytes=VMEM_LIMIT))(block)


B_Q_COL = 2304 // 128
B_K0, B_V0, B_END = 2816, 2944, 3072


def _full_cols(w_g):
    return w_g.transpose(1, 0, 2).reshape(w_g.shape[1], -1)


def _group_src(proj, g):
    return ((proj, 2 * g), (proj, 6 + 2 * g), (proj, 12 + 2 * g))


def _kv_expand(kv):
    return jnp.broadcast_to(kv.reshape(S, 2, 1, HD), (S, 2, 4, HD)).reshape(S, 8 * HD)


def _kv_reduce(dkv):
    return dkv.reshape(S, 2, 4, HD).sum(axis=2).reshape(S, 2 * HD)


def _mixer_fwd(h1, wget, rel_bias, sinks_l, bidx):
    w = dict(wget(0, h1))
    proj = _mm_nt("proj_in", h1, w["w_in"], F32, tm=S, tn=1152)
    no_sinks = jnp.full((4,), NEG, F32)
    srcs = [_group_src(proj, g) for g in range(3)]
    o_g, lse_g = [], []
    for g, (_, d) in enumerate(A_GROUPS):
        o, lse = _band_fwd("band_fwd_g%d" % g, d, 2, BLK, 4 * g, srcs[g], bidx[g], rel_bias, no_sinks)
        o_g.append(o)
        lse_g.append(lse)
    o_a32, o_a, lse_a = _comb_fwd(o_g, lse_g)
    src_b = ((proj, B_Q_COL), (_kv_expand(proj[:, B_K0:B_V0]), 0), (_kv_expand(proj[:, B_V0:B_END]), 0))
    o_b32, lse_b = _band_fwd("band_fwd_b", 1, 4, BLK - 1, N_A, src_b, bidx[3], rel_bias, sinks_l)
    o_b = o_b32.astype(BF16)
    o_c32, o_c, tot_c = _sb_fwd(proj)
    w.update(wget(1, o_c32))
    br = [_mm_nn("branch_a", o_a, w["w_br_a"], F32, tm=S), _mm_nn("branch_b", o_b, w["w_br_b"], F32, tm=S),
          _mm_nn("branch_c", o_c, w["w_br_c"], F32, tm=S)]
    merged = _gate_fwd(proj, w["b_gate"], br)
    mo = _mm_nn("out_proj", merged, w["w_out"], F32, tm=S)
    saved = dict(proj=proj, srcs=srcs, src_b=src_b, o_a32=o_a32, lse_a=lse_a, o_b32=o_b32, lse_b=lse_b, tot_c=tot_c,
                 o_a=o_a, o_b=o_b, o_c=o_c, br=br, merged=merged)
    return mo, saved, w


def _mixer_bwd(d_mo, h1, w, sv, rel_bias, sinks_l, bidx, stats_in, emit):
    grads = {}
    dmerged = _mm_nt("out_proj_dx", d_mo, w["w_out"], F32, tm=S)
    grads["w_out"] = _mm_tn_sharded("out_proj_dw", sv["merged"], d_mo, True)
    e, dgate, db_gate = _gate_bwd(sv["proj"], w["b_gate"], sv["br"], dmerged)
    grads["b_gate"] = db_gate
    d_o = {}
    for n, name in enumerate("abc"):
        d_o[name] = _mm_nt("branch_%s_dx" % name, e[n], w["w_br_" + name], F32, tm=S)
        grads["w_br_" + name] = _mm_tn_sharded("branch_%s_dw" % name, sv["o_" + name], e[n], False)
    zero = emit(1, grads)
    no_sinks = jnp.full((4,), NEG, F32) + zero[0]
    dqs, dks, dvs, stats = [], [], [], []
    for g, (_, d) in enumerate(A_GROUPS):
        dq, dk, dv, st = _band_bwd("band_bwd_g%d" % g, d, 2, BLK, 4 * g, sv["srcs"][g], bidx[g], rel_bias, no_sinks,
                                   sv["o_a32"], sv["lse_a"], d_o["a"], stats_in[4 * g:4 * g + 4])
        dqs.append(dq)
        dks.append(dk)
        dvs.append(dv)
        stats.append(st)
    dq_b, dk_x, dv_x, st = _band_bwd("band_bwd_b", 1, 4, BLK - 1, N_A, sv["src_b"], bidx[3], rel_bias, sinks_l,
                                     sv["o_b32"], sv["lse_b"], d_o["b"], stats_in[N_A:])
    stats = jnp.concatenate(stats + [st], axis=0)
    dcq, dck, dcv = _sb_bwd(sv["proj"], sv["tot_c"], d_o["c"])
    cols = dqs + dks + dvs + [dq_b, _kv_reduce(dk_x), _kv_reduce(dv_x), dcq, dck, dcv]
    dproj = jnp.concatenate([t.astype(BF16) for t in cols] + list(dgate), axis=1)
    grads["w_in"] = _mm_tn("proj_in_dw", dproj, h1, BF16, tm=1152, tn=1024).reshape(N_CHIPS, IN_SHARD, D)
    zero = emit(2, grads)
    dh1 = _mm_nn("proj_in_dx", dproj, w["w_in"], F32, tm=S, tk=2304)
    return dh1, grads, stats, zero


def _ffn_fwd(h2, w):
    u = _mm_nn("ffn_up", h2, w["w_up"], F32, tm=S, tn=1024)
    a = _conv_fwd(u, w["conv_w"], w["conv_b"])
    dn = _mm_nn("ffn_down", a, w["w_down"], F32, tm=1024)
    return dn, dict(u=u, a=a)


def _ffn_bwd(d_dn, h2, w, sv):
    grads = {}
    da = _mm_nt("ffn_down_dx", d_dn, w["w_down"], F32, tm=S, tn=1024)
    grads["w_down"] = _mm_tn_sharded("ffn_down_dw", sv["a"], d_dn, True, tm=1024, tn=1024)
    dug, duv, dwg, dwv, dbg, dbv = _conv_bwd(sv["u"], w["conv_w"], w["conv_b"], da)
    du = jnp.concatenate([dug, duv], axis=1)
    grads["conv_w"] = jnp.concatenate([dwg, dwv], axis=1)
    grads["conv_b"] = jnp.concatenate([dbg, dbv], axis=1)
    dh2 = _mm_nt("ffn_up_dx", du, w["w_up"], F32, tm=S, tk=2048)
    grads["w_up"] = _mm_tn_sharded("ffn_up_dw", h2, du, False, tm=1024, tn=1024)
    return dh2, grads


BIG = ("w_in", "w_br_a", "w_br_b", "w_br_c", "w_out", "w_up", "w_down")


def _shard_view(name, w):
    return jnp.swapaxes(w, 1, 2) if name == "w_in" else w
WEIGHT_GROUPS = (("w_in", "b_gate"), ("w_br_a", "w_br_b", "w_br_c", "w_out"), ("w_up", "conv_w", "w_down"))
GRAD_GROUPS = (("w_down", "w_up"), ("w_out", "w_br_a", "w_br_b", "w_br_c"), ("w_in",))
SMALL_ROWS = (("rel_bias", 8), ("attn_pre_norm", 16), ("attn_post_norm", 16), ("ffn_pre_norm", 16), ("ffn_post_norm", 16),
              ("sinks", 8), ("conv_b", 128), ("b_gate", 48), ("conv_w", 384), ("loss", 8))


def _pack_small(vals):
    rows = []
    for name, n in SMALL_ROWS:
        flat = vals[name].reshape(-1).astype(F32)
        rows.append(jnp.pad(flat, (0, n * 128 - flat.shape[0])).reshape(n, 128))
    return jnp.concatenate(rows, axis=0)


def _unpack_small(block, shapes):
    out, row = {}, 0
    for name, n in SMALL_ROWS:
        size = int(np.prod(shapes[name]))
        out[name] = block[row:row + n].reshape(-1)[:size].reshape(shapes[name])
        row += n
    return out


def kernel(x, rel_bias, attn_pre_norm, w_in, b_gate, sinks, w_br_a, w_br_b, w_br_c, w_out, attn_post_norm, ffn_pre_norm, w_up, conv_w, conv_b, w_down, ffn_post_norm, loss_target, m_rel_bias, m_attn_pre_norm, m_w_in, m_b_gate, m_sinks, m_w_br_a, m_w_br_b, m_w_br_c, m_w_out, m_attn_post_norm, m_ffn_pre_norm, m_w_up, m_conv_w, m_conv_b, m_w_down, m_ffn_post_norm, v_rel_bias, v_attn_pre_norm, v_w_in, v_b_gate, v_sinks, v_w_br_a, v_w_br_b, v_w_br_c, v_w_out, v_attn_post_norm, v_ffn_pre_norm, v_w_up, v_conv_w, v_conv_b, v_w_down, v_ffn_post_norm):
    names = ("rel_bias", "attn_pre_norm", "w_in", "b_gate", "sinks", "w_br_a", "w_br_b", "w_br_c", "w_out",
             "attn_post_norm", "ffn_pre_norm", "w_up", "conv_w", "conv_b", "w_down", "ffn_post_norm")
    weights = dict(zip(names, (rel_bias, attn_pre_norm, w_in, b_gate, sinks, w_br_a, w_br_b, w_br_c, w_out,
                               attn_post_norm, ffn_pre_norm, w_up, conv_w, conv_b, w_down, ffn_post_norm)))
    mom1 = dict(zip(names, (m_rel_bias, m_attn_pre_norm, m_w_in, m_b_gate, m_sinks, m_w_br_a, m_w_br_b, m_w_br_c,
                            m_w_out, m_attn_post_norm, m_ffn_pre_norm, m_w_up, m_conv_w, m_conv_b, m_w_down,
                            m_ffn_post_norm)))
    mom2 = dict(zip(names, (v_rel_bias, v_attn_pre_norm, v_w_in, v_b_gate, v_sinks, v_w_br_a, v_w_br_b, v_w_br_c,
                            v_w_out, v_attn_post_norm, v_ffn_pre_norm, v_w_up, v_conv_w, v_conv_b, v_w_down,
                            v_ffn_post_norm)))

    chip = 2 * lax.axis_index("x") + lax.axis_index("y")
    core = lax.axis_index("c")

    keys = [(n, l) for l in range(DEPTH) for group in WEIGHT_GROUPS for n in group]
    groups = [[keys.index((n, l)) for n in group] for l in range(DEPTH) for group in WEIGHT_GROUPS]

    def slot_buffer(n, l):
        if n in BIG:
            return _cast_into_slot("cast_" + n, _shard_view(n, weights[n]), l, chip)
        shard = weights[n][l]
        return lax.dynamic_update_slice(jnp.zeros((N_CHIPS,) + shard.shape, F32), shard[None],
                                        (chip, jnp.int32(0), jnp.int32(0)))

    first = keys.index(("w_in", 0))
    n_first = len(groups[0])
    sems, in_flight, _ = _gather_start("gather_start_first", [slot_buffer(*k) for k in keys[:n_first]], groups[:1],
                                       (first,))
    more = _gather_start("gather_start", [slot_buffer(*k) for k in keys[n_first:]],
                         [[a - n_first for a in g] for g in groups[1:]])
    sems, in_flight, started = sems + more[0], in_flight + more[1], more[2]

    def wget(l, gi, after):
        g = l * len(WEIGHT_GROUPS) + gi
        after = started if g == 0 else after
        halved = tuple(e for e, a in enumerate(groups[g]) if a == first)
        got = list(_gather_wait("gather_wait_%d_%d" % (l, gi), [in_flight[a] for a in groups[g]], *sems[g], after,
                                halved))
        for e in halved:
            got[e] = _swap_halves("swap_halves", [got[e]])[0]
        out = {}
        for n, buf in zip(WEIGHT_GROUPS[gi], got):
            out[n] = buf.reshape(-1, buf.shape[-1]) if n in ("w_in", "w_out", "w_down") else _full_cols(buf)
        if gi == len(WEIGHT_GROUPS) - 1:
            out["conv_b"] = conv_b[l:l + 1]
        return out

    pending = []

    def emit(l, gi, grads):
        group = GRAD_GROUPS[gi]
        *started, token = _reduce_start("reduce_start_%d_%d" % (l, gi), [grads[n] for n in group])
        pending.append((l, group) + tuple(started))
        return token[:1, :1]

    local = _local_step(x.reshape(S, D), loss_target.reshape(S, D), wget, emit, rel_bias, sinks, attn_pre_norm,
                        attn_post_norm, ffn_pre_norm, ffn_post_norm)
    return _reduce_and_update(x.shape, names, weights, mom1, mom2, chip, core, pending, *local)


def _local_step(xs, target, wget, emit, rel_bias, sinks, attn_pre_norm, attn_post_norm, ffn_pre_norm, ffn_post_norm):
    bidx = jnp.asarray(_bucket_maps())

    saved, layers = [], []
    h1 = _rms_fwd("pre_norm_first", xs, attn_pre_norm[0:1])
    x_in = xs
    for l in range(DEPTH):
        mo, sv_mix, w = _mixer_fwd(h1, functools.partial(wget, l), rel_bias, sinks[l], bidx)
        x_mid, h2 = _post_pre_fwd("post_attn_norm", x_in, mo, attn_post_norm[l:l + 1], ffn_pre_norm[l:l + 1])
        w.update(wget(l, 2, h2))
        dn, sv_ffn = _ffn_fwd(h2, w)
        g_next = attn_pre_norm[l + 1:l + 2] if l + 1 < DEPTH else None
        x_out, h1_next = _post_pre_fwd("post_ffn_norm" if l + 1 < DEPTH else "post_ffn_norm_last", x_mid, dn,
                                       ffn_post_norm[l:l + 1], g_next)
        saved.append(dict(x_in=x_in, h1=h1, mo=mo, x_mid=x_mid, h2=h2, dn=dn, mix=sv_mix, ffn=sv_ffn))
        layers.append(w)
        x_in, h1 = x_out, h1_next

    loss_row, dres = _loss_kernel(x_in, target)

    small = [None] * DEPTH
    stats = jnp.zeros((N_BAND_Q, 8, 128), F32)
    dh_next = None
    for l in reversed(range(DEPTH)):
        w, sv = layers[l], saved[l]
        if l + 1 < DEPTH:
            pre = (saved[l + 1]["x_in"], attn_pre_norm[l + 1:l + 2] + zero, dh_next)
            dres, d_dn, dg_pre_next, dg_fpost = _norm_bwd("post_ffn_norm_bwd", dres, pre,
                                                          (sv["dn"], ffn_post_norm[l:l + 1]))
            small[l + 1]["attn_pre_norm"] = dg_pre_next
        else:
            dres, d_dn, _, dg_fpost = _norm_bwd("post_ffn_norm_last_bwd", dres, None, (sv["dn"], ffn_post_norm[l:l + 1]))
        dh2, g_ffn = _ffn_bwd(d_dn, sv["h2"], w, sv["ffn"])
        zero = emit(l, 0, g_ffn)
        dres, d_mo, dg_fpre, dg_apost = _norm_bwd("post_attn_norm_bwd", dres,
                                                  (sv["x_mid"], ffn_pre_norm[l:l + 1] + zero, dh2),
                                                  (sv["mo"], attn_post_norm[l:l + 1]))
        dh_next, g_mix, stats, zero = _mixer_bwd(d_mo, sv["h1"], w, sv["mix"], rel_bias, sinks[l], bidx, stats,
                                                 functools.partial(emit, l))
        small[l] = dict(ffn_post_norm=dg_fpost, ffn_pre_norm=dg_fpre, attn_post_norm=dg_apost,
                        sinks=stats[N_A:, 1, 0], conv_b=g_ffn["conv_b"], b_gate=g_mix["b_gate"], conv_w=g_ffn["conv_w"])
    grad_x, _, dg_pre0, _ = _norm_bwd("pre_norm_first_bwd", dres, (saved[0]["x_in"], attn_pre_norm[0:1] + zero, dh_next),
                                      None)
    small[0]["attn_pre_norm"] = dg_pre0
    return loss_row, grad_x, small, stats


def _reduce_and_update(x_shape, names, weights, mom1, mom2, chip, core, pending, loss_row, grad_x, small, stats):
    delta, new_m, new_v, grads = {}, {}, {}, {}

    def update(n, g):
        grads[n] = g
        delta[n], new_m[n], new_v[n] = _adamw("adamw_" + n, _shard_view(n, weights[n]), g,
                                              _shard_view(n, mom1[n]), _shard_view(n, mom2[n]))

    summed = {}

    def finish(which, after):
        for l, group, send, recv, gs, lands in pending:
            if (group == ("w_in",)) == which:
                gs, lands = _reduce_wait("reduce_wait_%d_%s" % (l, group[0]), send, recv, gs, lands, after)
                for n, g, land in zip(group, gs, lands):
                    summed[n] = _reduce_sum("reduce_sum_%d_%s" % (l, n), g, land, l, summed.get(n), chip, core)

    finish(False, grad_x)
    early = [n for n in BIG if n != "w_in"]
    for n, g in zip(early, _join_halves("join_halves", [summed[n] for n in early])):
        update(n, g)
    finish(True, delta[early[-1]])
    update("w_in", _join_halves("join_halves_w_in", [summed["w_in"]])[0])
    for out in (grads, delta, new_m, new_v):
        out["w_in"] = _shard_view("w_in", out["w_in"])

    small_vals = {n: jnp.stack([small[l][n].reshape(weights[n].shape[1:]) for l in range(DEPTH)])
                  for n in ("attn_pre_norm", "attn_post_norm", "ffn_pre_norm", "ffn_post_norm", "conv_b", "sinks")}
    small_vals["b_gate"] = jnp.stack([small[l]["b_gate"] for l in range(DEPTH)])
    small_vals["conv_w"] = jnp.stack([small[l]["conv_w"] for l in range(DEPTH)])
    small_vals["rel_bias"] = stats[:, 0, :NUM_BUCKETS].T
    small_vals["loss"] = loss_row[0, :1]
    shapes = {n: v.shape for n, v in small_vals.items()}
    packed, delta["w_in"] = lax.optimization_barrier((_pack_small(small_vals), delta["w_in"]))
    reduced = _unpack_small(_all_reduce_small(packed), shapes)
    reduced["b_gate"] = lax.dynamic_slice_in_dim(reduced["b_gate"], chip * (D // N_CHIPS), D // N_CHIPS, axis=2)
    reduced["conv_w"] = lax.dynamic_slice_in_dim(reduced["conv_w"], chip * (2 * D_FF // N_CHIPS), 2 * D_FF // N_CHIPS, axis=2)
    for n in names:
        if n not in grads:
            update(n, reduced[n].reshape(weights[n].shape))

    loss = reduced["loss"].reshape(())
    return (loss, grad_x.reshape(x_shape), *[grads[n] for n in names], *[delta[n] for n in names],
            *[new_m[n] for n in names], *[new_v[n] for n in names])
```

```python
import functools
import math

import numpy as np
import jax
import jax.numpy as jnp
from jax import lax
from jax.experimental import pallas as pl
from jax.experimental.pallas import tpu as pltpu

F32 = jnp.float32
BF16 = jnp.bfloat16

S = 2048
D = 1024
DEPTH = 2
HD = 64
BLK = 128
NQB = S // BLK
A_GROUPS = ((128, 1), (512, 4), (2048, 16))
N_BAND_Q = 20
N_A = 12
NUM_BUCKETS = 32
MAX_DISTANCE = 2048
D_FF = 4096
IN_COLS = 6912
IN_SHARD = IN_COLS // 4
OFF_GATE = 3840
EPS = 1e-6
SCALE = HD ** -0.5
NEG = -1e30
N_CHIPS = 4
N_DEV = 8

ADAM_LR = 0.001
ADAM_B1 = 0.9
ADAM_B2 = 0.999
ADAM_EPS = 1e-08
ADAM_WD = 0.01
ADAM_STEP = 10

VMEM_LIMIT = 56 * 1024 * 1024

NN = (((1,), (0,)), ((), ()))
NT = (((1,), (1,)), ((), ()))
TN = (((0,), (0,)), ((), ()))

MESH = pl.DeviceIdType.MESH
ANY = pl.BlockSpec(memory_space=pl.ANY)


def _dot(a, b, dims):
    return lax.dot_general(a, b, dims, preferred_element_type=F32)


def _params(sem):
    return pltpu.CompilerParams(dimension_semantics=sem, vmem_limit_bytes=VMEM_LIMIT)


def _matmul(name, a, b, out_shape, out_dtype, grid, a_spec, b_spec, o_spec, dims, acc_shape):
    nk = grid[-1]

    def body(a_ref, b_ref, o_ref, *scratch):
        part = _dot(a_ref[...].astype(BF16), b_ref[...].astype(BF16), dims)
        if nk == 1:
            o_ref[...] = part.astype(o_ref.dtype)
            return
        acc_ref, = scratch
        k = pl.program_id(len(grid) - 1)

        @pl.when(k == 0)
        def _():
            acc_ref[...] = part

        @pl.when(k > 0)
        def _():
            acc_ref[...] += part

        @pl.when(k == nk - 1)
        def _():
            o_ref[...] = acc_ref[...].astype(o_ref.dtype)

    scratch = [] if nk == 1 else [pltpu.VMEM(acc_shape, F32)]
    sem = ("parallel",) * (len(grid) - 1) + ("arbitrary",)
    return pl.pallas_call(
        body, name=name, grid=grid, in_specs=[a_spec, b_spec], out_specs=o_spec,
        out_shape=jax.ShapeDtypeStruct(out_shape, out_dtype), scratch_shapes=scratch,
        compiler_params=_params(sem))(a, b)


FULL_K = 8192


def _mm_tn_sharded(name, a, b, row_sharded, tm=512, tn=512, tk=FULL_K):
    k, m = a.shape
    n = b.shape[1]
    m4, n4 = (m // N_CHIPS, n) if row_sharded else (m, n // N_CHIPS)
    tm, tn, tk = min(tm, m4), min(tn, n4), min(tk, k)
    per_m, per_n = m4 // tm, n4 // tn
    if row_sharded:
        o_map = lambda i, j, l: (i // per_m, i % per_m, j)
    else:
        o_map = lambda i, j, l: (j // per_n, i, j % per_n)
    return _matmul(name, a, b, (N_CHIPS, m4, n4), BF16, (m // tm, n // tn, k // tk),
                   pl.BlockSpec((tk, tm), lambda i, j, l: (l, i)),
                   pl.BlockSpec((tk, tn), lambda i, j, l: (l, j)),
                   pl.BlockSpec((None, tm, tn), o_map), TN, (tm, tn))


def _mm_nn(name, a, b, out_dtype, tm=512, tn=512, tk=FULL_K):
    m, k = a.shape
    n = b.shape[1]
    tm, tn, tk = min(tm, m), min(tn, n), min(tk, k)
    return _matmul(name, a, b, (m, n), out_dtype, (m // tm, n // tn, k // tk),
                   pl.BlockSpec((tm, tk), lambda i, j, l: (i, l)),
                   pl.BlockSpec((tk, tn), lambda i, j, l: (l, j)),
                   pl.BlockSpec((tm, tn), lambda i, j, l: (i, j)), NN, (tm, tn))


def _mm_nt(name, a, b, out_dtype, tm=512, tn=512, tk=FULL_K):
    m, k = a.shape
    n = b.shape[0]
    tm, tn, tk = min(tm, m), min(tn, n), min(tk, k)
    return _matmul(name, a, b, (m, n), out_dtype, (m // tm, n // tn, k // tk),
                   pl.BlockSpec((tm, tk), lambda i, j, l: (i, l)),
                   pl.BlockSpec((tn, tk), lambda i, j, l: (j, l)),
                   pl.BlockSpec((tm, tn), lambda i, j, l: (i, j)), NT, (tm, tn))


def _mm_tn(name, a, b, out_dtype, tm=512, tn=512, tk=FULL_K):
    k, m = a.shape
    n = b.shape[1]
    tm, tn, tk = min(tm, m), min(tn, n), min(tk, k)
    return _matmul(name, a, b, (m, n), out_dtype, (m // tm, n // tn, k // tk),
                   pl.BlockSpec((tk, tm), lambda i, j, l: (l, i)),
                   pl.BlockSpec((tk, tn), lambda i, j, l: (l, j)),
                   pl.BlockSpec((tm, tn), lambda i, j, l: (i, j)), TN, (tm, tn))


TR = 512


def _row_spec(width=D):
    return pl.BlockSpec((TR, width), lambda i: (i, 0))


def _vec_spec(width=D):
    return pl.BlockSpec((1, width), lambda i: (0, 0))


def _rms(x, g):
    r = lax.rsqrt(jnp.mean(x * x, axis=-1, keepdims=True) + EPS)
    return x * r * g


def _rms_fwd(name, x, g):
    def body(x_ref, g_ref, h_ref):
        h_ref[...] = _rms(x_ref[...], g_ref[...]).astype(BF16)

    return pl.pallas_call(
        body, name=name, grid=(S // TR,), in_specs=[_row_spec(), _vec_spec()], out_specs=_row_spec(),
        out_shape=jax.ShapeDtypeStruct((S, D), BF16), compiler_params=_params(("parallel",)))(x, g)


def _post_pre_fwd(name, x, y, g_post, g_pre):
    has_pre = g_pre is not None

    def body(*refs):
        if has_pre:
            x_ref, y_ref, gp_ref, gn_ref, xn_ref, h_ref = refs
        else:
            x_ref, y_ref, gp_ref, xn_ref = refs
        xn = x_ref[...] + _rms(y_ref[...], gp_ref[...])
        xn_ref[...] = xn
        if has_pre:
            h_ref[...] = _rms(xn, gn_ref[...]).astype(BF16)

    ins = [x, y, g_post] + ([g_pre] if has_pre else [])
    in_specs = [_row_spec(), _row_spec(), _vec_spec()] + ([_vec_spec()] if has_pre else [])
    out_shape = [jax.ShapeDtypeStruct((S, D), F32)] + ([jax.ShapeDtypeStruct((S, D), BF16)] if has_pre else [])
    out_specs = [_row_spec()] + ([_row_spec()] if has_pre else [])
    out = pl.pallas_call(
        body, name=name, grid=(S // TR,), in_specs=in_specs, out_specs=out_specs, out_shape=out_shape,
        compiler_params=_params(("parallel",)))(*ins)
    return out if has_pre else (out[0], None)


def _rms_bwd_math(x, g, dy):
    r = lax.rsqrt(jnp.mean(x * x, axis=-1, keepdims=True) + EPS)
    n = x * r
    dn = dy * g
    dx = r * (dn - n * jnp.mean(dn * n, axis=-1, keepdims=True))
    return dx, jnp.sum(dy * n, axis=0, keepdims=True)


def _norm_bwd(name, dres, pre=None, post=None):
    has_pre, has_post = pre is not None, post is not None

    def body(*refs):
        refs = list(refs)
        dres_ref = refs.pop(0)
        if has_pre:
            xn_ref, gn_ref, dh_ref = refs[:3]
            refs = refs[3:]
        if has_post:
            y_ref, gp_ref = refs[:2]
            refs = refs[2:]
        dxn_ref = refs.pop(0)
        dy_ref = refs.pop(0) if has_post else None
        dgn_ref = refs.pop(0) if has_pre else None
        dgp_ref = refs.pop(0) if has_post else None
        first = pl.program_id(0) == 0
        dxn = dres_ref[...]
        if has_pre:
            dx, dg = _rms_bwd_math(xn_ref[...], gn_ref[...], dh_ref[...])
            dxn = dxn + dx

            @pl.when(first)
            def _():
                dgn_ref[...] = dg

            @pl.when(jnp.logical_not(first))
            def _():
                dgn_ref[...] += dg
        dxn_ref[...] = dxn
        if has_post:
            dy, dg = _rms_bwd_math(y_ref[...], gp_ref[...], dxn)
            dy_ref[...] = dy.astype(BF16)

            @pl.when(first)
            def _():
                dgp_ref[...] = dg

            @pl.when(jnp.logical_not(first))
            def _():
                dgp_ref[...] += dg

    ins, in_specs = [dres], [_row_spec()]
    if has_pre:
        ins += list(pre)
        in_specs += [_row_spec(), _vec_spec(), _row_spec()]
    if has_post:
        ins += list(post)
        in_specs += [_row_spec(), _vec_spec()]
    out_shape, out_specs = [jax.ShapeDtypeStruct((S, D), F32)], [_row_spec()]
    if has_post:
        out_shape.append(jax.ShapeDtypeStruct((S, D), BF16))
        out_specs.append(_row_spec())
    for _ in range(int(has_pre) + int(has_post)):
        out_shape.append(jax.ShapeDtypeStruct((1, D), F32))
        out_specs.append(_vec_spec())
    out = list(pl.pallas_call(
        body, name=name, grid=(S // TR,), in_specs=in_specs, out_specs=out_specs, out_shape=out_shape,
        compiler_params=_params(("arbitrary",)))(*ins))
    dxn = out.pop(0)
    dy = out.pop(0) if has_post else None
    dgn = out.pop(0) if has_pre else None
    dgp = out.pop(0) if has_post else None
    return dxn, dy, dgn, dgp


def _loss_kernel(y, target):
    def body(y_ref, t_ref, loss_ref, dy_ref):
        e = y_ref[...] - t_ref[...]
        dy_ref[...] = e * (1.0 / D)
        part = jnp.zeros((1, 128), F32) + 0.5 * jnp.sum(jnp.mean(e * e, axis=-1, keepdims=True))

        @pl.when(pl.program_id(0) == 0)
        def _():
            loss_ref[...] = part

        @pl.when(pl.program_id(0) > 0)
        def _():
            loss_ref[...] += part

    return pl.pallas_call(
        body, name="loss", grid=(S // TR,), in_specs=[_row_spec(), _row_spec()],
        out_specs=[_vec_spec(128), _row_spec()],
        out_shape=[jax.ShapeDtypeStruct((1, 128), F32), jax.ShapeDtypeStruct((S, D), F32)],
        compiler_params=_params(("arbitrary",)))(y, target)


def _t5_bucket_np(dist):
    max_exact = NUM_BUCKETS // 2
    nf = np.maximum(dist, 1).astype(np.float32)
    large = max_exact + (np.log(nf / max_exact) / np.float32(math.log(MAX_DISTANCE / max_exact))
                         * (NUM_BUCKETS - max_exact)).astype(np.int32)
    large = np.minimum(large, NUM_BUCKETS - 1)
    return np.where(dist < max_exact, dist, large).astype(np.int32)


def _bucket_maps():
    a = np.arange(BLK)[:, None]
    b = np.arange(2 * BLK)[None, :]
    dist = np.maximum(a + BLK - b, 0)
    maps = [_t5_bucket_np(dist * d) for _, d in A_GROUPS] + [_t5_bucket_np(dist)]
    return np.stack(maps).astype(np.int32)


def _pair_spec(col0):
    return pl.BlockSpec((S, 128), lambda p: (0, col0 + p))


def _band_rows(i, d):
    nb = S // d // BLK
    r, b = i // nb, i % nb
    cur = pl.ds(b * BLK * d + r, BLK, stride=d)
    prev = pl.ds(jnp.maximum(b - 1, 0) * BLK * d + r, BLK, stride=d)
    return cur, prev, jnp.minimum(b, 1)


def _band_bias(tab_ref, bidx_ref, h):
    bi = bidx_ref[...]
    bias = jnp.zeros((BLK, 2 * BLK), F32)
    for kk in range(NUM_BUCKETS):
        bias = jnp.where(bi == kk, tab_ref[kk, h], bias)
    return bias


def _lane_lo(rows=BLK):
    return lax.broadcasted_iota(jnp.int32, (rows, 128), 1) < HD


def _per_head(x, lo):
    return (jnp.sum(jnp.where(lo, x, 0.0), axis=1, keepdims=True) * (1.0 / HD),
            jnp.sum(jnp.where(lo, 0.0, x), axis=1, keepdims=True) * (1.0 / HD))


def _band_fill(bias_ref, tab_ref, bidx_ref, head, maxd):
    a = lax.broadcasted_iota(jnp.int32, (BLK, 2 * BLK), 0)
    c = lax.broadcasted_iota(jnp.int32, (BLK, 2 * BLK), 1)
    dist = a + BLK - c
    in_band = jnp.logical_and(dist >= 0, dist <= maxd)
    for h in range(2):
        bias = jnp.where(in_band, _band_bias(tab_ref, bidx_ref, head + h), NEG)
        bias_ref[1, h * BLK:(h + 1) * BLK, :] = bias
        bias_ref[0, h * BLK:(h + 1) * BLK, :] = jnp.where(c >= BLK, bias, NEG)


def _stack_heads(x, lo, dtype=BF16):
    return jnp.concatenate([jnp.where(lo, x, 0.0), jnp.where(lo, 0.0, x)], axis=0).astype(dtype)


def _unstack_heads(x, lo):
    n = x.shape[0] // 2
    return jnp.where(lo, x[:n], x[n:])


def _stack_rows(ref, prev, cur):
    return jnp.concatenate([ref[prev, :], ref[cur, :]], axis=0).astype(BF16)


def _band_fwd(name, d, n_pairs, maxd, head0, srcs, bidx_g, tab, sinks):
    (qa, qc), (ka, kc), (va, vc) = srcs
    out_spec = _pair_spec(0)
    smem = pl.BlockSpec(memory_space=pltpu.SMEM)
    full = pl.BlockSpec((BLK, 2 * BLK), lambda p: (0, 0))

    def body(tab_ref, sink_ref, q_ref, k_ref, v_ref, bidx_ref, o_ref, lse_ref, bias_ref):
        p = pl.program_id(0)
        _band_fill(bias_ref, tab_ref, bidx_ref, head0 + 2 * p, maxd)
        lo = _lane_lo()
        sink = jnp.where(lax.broadcasted_iota(jnp.int32, (2 * BLK, 1), 0) < BLK, sink_ref[2 * p], sink_ref[2 * p + 1])

        def block(i, carry):
            cur, prev, has_prev = _band_rows(i, d)
            qs = _stack_heads(q_ref[cur, :] * SCALE, lo)
            ks, vs = _stack_rows(k_ref, prev, cur), _stack_rows(v_ref, prev, cur)
            s = _dot(qs, ks, NT) + bias_ref[has_prev]
            m = jnp.max(s, axis=1, keepdims=True)
            pr = jnp.exp(s - m)
            l = jnp.sum(pr, axis=1, keepdims=True)
            num = _dot(pr.astype(BF16), vs, NN)
            lse = m + jnp.log(l)
            sig = 1.0 / (1.0 + jnp.exp(sink - lse))
            o_ref[cur, :] = _unstack_heads(num * (sig / l), lo)
            lse_ref[cur, :] = _unstack_heads(lse + jnp.zeros((2 * BLK, 128), F32), lo)
            return carry

        lax.fori_loop(0, NQB, block, 0, unroll=2)

    shape = jax.ShapeDtypeStruct((S, n_pairs * 128), F32)
    return pl.pallas_call(
        body, name=name, grid=(n_pairs,),
        in_specs=[smem, smem, _pair_spec(qc), _pair_spec(kc), _pair_spec(vc), full],
        out_specs=[out_spec, out_spec], out_shape=[shape, shape],
        scratch_shapes=[pltpu.VMEM((2, 2 * BLK, 2 * BLK), F32)],
        compiler_params=_params(("parallel",)))(tab, sinks, qa, ka, va, bidx_g)


def _band_bwd(name, d, n_pairs, maxd, head0, srcs, bidx_g, tab, sinks, o, lse, do, stats_in):
    (qa, qc), (ka, kc), (va, vc) = srcs
    pair = _pair_spec(0)
    smem = pl.BlockSpec(memory_space=pltpu.SMEM)
    full = pl.BlockSpec((BLK, 2 * BLK), lambda p: (0, 0))
    stat_spec = pl.BlockSpec((2, 8, 128), lambda p: (p, 0, 0))

    def body(tab_ref, sink_ref, q_ref, k_ref, v_ref, bidx_ref, o_ref, lse_ref, do_ref, sin_ref,
             dq_ref, dk_ref, dv_ref, stat_ref, bias_ref, dsacc_ref, sk_ref):
        p = pl.program_id(0)
        _band_fill(bias_ref, tab_ref, bidx_ref, head0 + 2 * p, maxd)
        dsacc_ref[...] = jnp.zeros_like(dsacc_ref)
        sk_ref[...] = jnp.zeros_like(sk_ref)
        dk_ref[...] = jnp.zeros_like(dk_ref)
        dv_ref[...] = jnp.zeros_like(dv_ref)
        lo = _lane_lo()
        head1 = lax.broadcasted_iota(jnp.int32, (2 * BLK, 1), 0) >= BLK
        sink = jnp.where(head1, sink_ref[2 * p + 1], sink_ref[2 * p])

        def block(i, carry):
            cur, prev, has_prev = _band_rows(i, d)
            qs = _stack_heads(q_ref[cur, :] * SCALE, lo)
            ks, vs = _stack_rows(k_ref, prev, cur), _stack_rows(v_ref, prev, cur)
            do = do_ref[cur, :]
            dos = _stack_heads(do, lo, F32)
            lse = jnp.concatenate(_per_head(lse_ref[cur, :], lo), axis=0)
            prod = do * o_ref[cur, :]
            delta = jnp.concatenate([jnp.sum(jnp.where(lo, prod, 0.0), axis=1, keepdims=True),
                                     jnp.sum(jnp.where(lo, 0.0, prod), axis=1, keepdims=True)], axis=0)
            sig = 1.0 / (1.0 + jnp.exp(sink - lse))
            pr = jnp.exp(_dot(qs, ks, NT) + bias_ref[has_prev] - lse)
            ds = pr * (sig * (_dot(dos.astype(BF16), vs, NT) - delta))
            dsb = ds.astype(BF16)
            dq_ref[cur, :] = SCALE * _unstack_heads(_dot(dsb, ks, NN), lo)
            dk = _dot(dsb, qs, TN)
            dv = _dot(pr.astype(BF16), (sig * dos).astype(BF16), TN)
            dk_ref[prev, :] += dk[:BLK]
            dk_ref[cur, :] += dk[BLK:]
            dv_ref[prev, :] += dv[:BLK]
            dv_ref[cur, :] += dv[BLK:]
            dsacc_ref[...] += ds
            sink_grad = -delta * (1.0 - sig)
            for h in range(2):
                sk_ref[h] += jnp.zeros((8, 128), F32) + jnp.sum(sink_grad[h * BLK:(h + 1) * BLK])
            return carry

        lax.fori_loop(0, NQB, block, 0, unroll=2)

        bi = bidx_ref[...]
        lane = lax.broadcasted_iota(jnp.int32, (8, 128), 1)
        sub = lax.broadcasted_iota(jnp.int32, (8, 128), 0)
        for h in range(2):
            acc = dsacc_ref[h * BLK:(h + 1) * BLK, :]
            row = jnp.where(jnp.logical_and(sub == 1, lane == 0), sk_ref[h], 0.0)
            for kk in range(NUM_BUCKETS):
                tot = jnp.sum(jnp.where(bi == kk, acc, 0.0))
                row = jnp.where(jnp.logical_and(sub == 0, lane == kk), tot, row)
            stat_ref[h] = row + jnp.where(sub == 0, sin_ref[h], 0.0)

    shape = jax.ShapeDtypeStruct((S, n_pairs * 128), F32)
    return pl.pallas_call(
        body, name=name, grid=(n_pairs,),
        in_specs=[smem, smem, _pair_spec(qc), _pair_spec(kc), _pair_spec(vc), full, pair, pair, pair, stat_spec],
        out_specs=[pair, pair, pair, stat_spec],
        out_shape=[shape, shape, shape, jax.ShapeDtypeStruct((2 * n_pairs, 8, 128), F32)],
        scratch_shapes=[pltpu.VMEM((2, 2 * BLK, 2 * BLK), F32), pltpu.VMEM((2 * BLK, 2 * BLK), F32),
                        pltpu.VMEM((2, 8, 128), F32)],
        compiler_params=_params(("parallel",)))(tab, sinks, qa, ka, va, bidx_g, o, lse, do, stats_in)


def _comb_fwd(o_g, lse_g):
    def body(o0, o1, o2, l0, l1, l2, out_ref, outb_ref, lse_ref):
        a0, a1, a2 = l0[...], l1[...], l2[...]
        m = jnp.maximum(jnp.maximum(a0, a1), a2)
        e0, e1, e2 = jnp.exp(a0 - m), jnp.exp(a1 - m), jnp.exp(a2 - m)
        tot = e0 + e1 + e2
        out = (e0 * o0[...] + e1 * o1[...] + e2 * o2[...]) / tot
        out_ref[...] = out
        outb_ref[...] = out.astype(BF16)
        lse_ref[...] = m + jnp.log(tot)

    spec = _row_spec(4 * HD)
    f32 = jax.ShapeDtypeStruct((S, 4 * HD), F32)
    return pl.pallas_call(
        body, name="comb_fwd", grid=(S // TR,), in_specs=[spec] * 6, out_specs=[spec] * 3,
        out_shape=[f32, jax.ShapeDtypeStruct((S, 4 * HD), BF16), f32],
        compiler_params=_params(("parallel",)))(*o_g, *lse_g)


def _split2(x):
    hi = x.astype(BF16)
    return hi, (x - hi.astype(F32)).astype(BF16)


KB = 2 * BLK
SBQ = 2 * BLK


def _tri_sum(x, tri):
    hi, lo = _split2(x)
    both = _dot(jnp.concatenate([hi, lo], axis=0), tri, NN)
    return both[:x.shape[0]] + both[x.shape[0]:]


def _tri(strict_upper):
    r = lax.broadcasted_iota(jnp.int32, (KB, KB), 0)
    c = lax.broadcasted_iota(jnp.int32, (KB, KB), 1)
    return jnp.where(r > c if strict_upper else r < c, 1.0, 0.0).astype(BF16)


def _sb_terms(qs, kj, before):
    z = _dot(qs, kj, NT)
    lsp = jnp.minimum(z, 0.0) - jnp.log(1.0 + jnp.exp(-jnp.abs(z)))
    return lsp, _sb_keep(before, lsp - z)


def _sb_keep(before, x):
    return x if before is None else jnp.where(before, x, 0.0)


def _sb_before(i, m):
    t = (lax.broadcasted_iota(jnp.int32, (2 * SBQ, KB), 0) & (SBQ - 1)) + i * SBQ
    s = lax.broadcasted_iota(jnp.int32, (2 * SBQ, KB), 1) + m * KB
    return s < t


C_COL = 3072 // 128


def _sb_fwd(proj):
    blk = lambda off: pl.BlockSpec((SBQ, 128), lambda p, i: (i, off + p))
    col = lambda off: pl.BlockSpec((S, 128), lambda p, i: (0, off + p))
    out = pl.BlockSpec((SBQ, 128), lambda p, i: (i, p))

    def body(q_ref, k_ref, v_ref, o_ref, ob_ref, tot_ref):
        i = pl.program_id(1)
        lo = _lane_lo(SBQ)
        qs = _stack_heads(q_ref[...] * SCALE, lo)
        suffix = _tri(True)

        def step(n, carry, diagonal=False):
            acc, rest = carry
            m = i - n
            rows = pl.ds(pl.multiple_of(m * KB, KB), KB)
            kj, vj = k_ref[rows, :].astype(BF16), v_ref[rows, :].astype(BF16)
            before = _sb_before(i, m) if diagonal else None
            lsp, lk = _sb_terms(qs, kj, before)
            w = _sb_keep(before, jnp.exp(lsp + _tri_sum(lk, suffix) + rest))
            return acc + _dot(w.astype(BF16), vj, NN), rest + jnp.sum(lk, axis=1, keepdims=True)

        first = step(0, (jnp.zeros((2 * SBQ, 128), F32), jnp.zeros((2 * SBQ, 1), F32)), diagonal=True)
        acc, rest = lax.fori_loop(1, i + 1, step, first)
        o = _unstack_heads(acc, lo)
        o_ref[...] = o
        ob_ref[...] = o.astype(BF16)
        tot_ref[...] = _unstack_heads(rest + jnp.zeros((2 * SBQ, 128), F32), lo)

    f32 = jax.ShapeDtypeStruct((S, 4 * HD), F32)
    return pl.pallas_call(
        body, name="sb_fwd", grid=(2, S // SBQ), in_specs=[blk(C_COL), col(C_COL + 2), col(C_COL + 4)],
        out_specs=[out, out, out], out_shape=[f32, jax.ShapeDtypeStruct((S, 4 * HD), BF16), f32],
        compiler_params=_params(("parallel", "arbitrary")))(proj, proj, proj)


def _sb_bwd(proj, tot, do):
    blk = lambda off: pl.BlockSpec((SBQ, 128), lambda p, i: (i, off + p))
    col = lambda off: pl.BlockSpec((S, 128), lambda p, i: (0, off + p))

    def body(q_ref, k_ref, v_ref, tot_ref, do_ref, dq_ref, dk_ref, dv_ref):
        i = pl.program_id(1)

        @pl.when(i == 0)
        def _():
            dk_ref[...] = jnp.zeros_like(dk_ref)
            dv_ref[...] = jnp.zeros_like(dv_ref)

        lo = _lane_lo(SBQ)
        qs = _stack_heads(q_ref[...] * SCALE, lo)
        dos = _stack_heads(do_ref[...], lo)
        tots = jnp.concatenate(_per_head(tot_ref[...], lo), axis=0)
        prefix = _tri(False)

        def step(m, carry, diagonal=False):
            dq, keep_left, g_left = carry
            rows = pl.ds(pl.multiple_of(m * KB, KB), KB)
            kj, vj = k_ref[rows, :].astype(BF16), v_ref[rows, :].astype(BF16)
            before = _sb_before(i, m) if diagonal else None
            lsp, lk = _sb_terms(qs, kj, before)
            log_rest = tots - keep_left - lk - _tri_sum(lk, prefix)
            w = _sb_keep(before, jnp.exp(lsp + log_rest))
            g = w * _dot(dos, vj, NT)
            g_before = g_left + _dot(g.astype(BF16), prefix, NN)
            beta = jnp.exp(lsp)
            dz = _sb_keep(before, g * (1.0 - beta) - g_before * beta).astype(BF16)
            dk_ref[rows, :] += _dot(dz, qs, TN)
            dv_ref[rows, :] += _dot(w.astype(BF16), dos, TN)
            return (dq + _dot(dz, kj, NN), keep_left + jnp.sum(lk, axis=1, keepdims=True),
                    g_left + jnp.sum(g, axis=1, keepdims=True))

        zero = (jnp.zeros((2 * SBQ, 128), F32), jnp.zeros((2 * SBQ, 1), F32), jnp.zeros((2 * SBQ, 1), F32))
        dq, _, _ = step(i, lax.fori_loop(0, i, step, zero), diagonal=True)
        dq_ref[...] = SCALE * _unstack_heads(dq, lo)

    out_blk = pl.BlockSpec((SBQ, 128), lambda p, i: (i, p))
    out_col = pl.BlockSpec((S, 128), lambda p, i: (0, p))
    f32 = jax.ShapeDtypeStruct((S, 4 * HD), F32)
    return pl.pallas_call(
        body, name="sb_bwd", grid=(2, S // SBQ),
        in_specs=[blk(C_COL), col(C_COL + 2), col(C_COL + 4), out_blk, out_blk],
        out_specs=[out_blk, out_col, out_col], out_shape=[f32, f32, f32],
        compiler_params=_params(("arbitrary", "arbitrary")))(proj, proj, proj, tot, do)


TG = 256
TGR = 1024
GATE_BLK0 = OFF_GATE // TG


def _gate_specs():
    grid = (D // TG, S // TGR)
    p_specs = [pl.BlockSpec((TGR, TG), functools.partial(lambda c, r, br: (r, GATE_BLK0 + br * (D // TG) + c), br=br))
               for br in range(3)]
    b_spec = pl.BlockSpec((3, TG), lambda c, r: (0, c))
    t_spec = pl.BlockSpec((TGR, TG), lambda c, r: (r, c))
    return grid, p_specs, b_spec, t_spec


def _sigmoid(x):
    return 1.0 / (1.0 + jnp.exp(-x))


def _three_rows(rows):
    sub = lax.broadcasted_iota(jnp.int32, (3, rows[0].shape[1]), 0)
    return jnp.where(sub == 0, rows[0], jnp.where(sub == 1, rows[1], rows[2]))


def _gate_fwd(proj, b_gate, br):
    grid, p_specs, b_spec, t_spec = _gate_specs()

    def body(p0, p1, p2, b_ref, r0, r1, r2, out_ref):
        acc = jnp.zeros((TGR, TG), F32)
        for n, (p, r) in enumerate(((p0, r0), (p1, r1), (p2, r2))):
            acc += _sigmoid(p[...] + b_ref[n:n + 1, :]) * r[...]
        out_ref[...] = acc.astype(BF16)

    return pl.pallas_call(
        body, name="gate_fwd", grid=grid, in_specs=p_specs + [b_spec] + [t_spec] * 3, out_specs=t_spec,
        out_shape=jax.ShapeDtypeStruct((S, D), BF16),
        compiler_params=_params(("parallel", "parallel")))(proj, proj, proj, b_gate, *br)


def _gate_bwd(proj, b_gate, br, dmerged):
    grid, p_specs, b_spec, t_spec = _gate_specs()

    def body(p0, p1, p2, b_ref, r0, r1, r2, dm_ref, e0, e1, e2, g0, g1, g2, db_ref):
        dm = dm_ref[...]
        rows = []
        for n, (p, r, e_ref, dg_ref) in enumerate(((p0, r0, e0, g0), (p1, r1, e1, g1), (p2, r2, e2, g2))):
            g = _sigmoid(p[...] + b_ref[n:n + 1, :])
            e_ref[...] = (dm * g).astype(BF16)
            dpre = dm * r[...] * g * (1.0 - g)
            dg_ref[...] = dpre.astype(BF16)
            rows.append(jnp.sum(dpre, axis=0, keepdims=True))
        db = _three_rows(rows)

        @pl.when(pl.program_id(1) == 0)
        def _():
            db_ref[...] = db

        @pl.when(pl.program_id(1) > 0)
        def _():
            db_ref[...] += db

    bf = jax.ShapeDtypeStruct((S, D), BF16)
    out = pl.pallas_call(
        body, name="gate_bwd", grid=grid, in_specs=p_specs + [b_spec] + [t_spec] * 4,
        out_specs=[t_spec] * 6 + [b_spec], out_shape=[bf] * 6 + [jax.ShapeDtypeStruct((3, D), F32)],
        compiler_params=_params(("parallel", "arbitrary")))(proj, proj, proj, b_gate, *br, dmerged)
    return out[:3], out[3:6], out[6]


TC = 256
N_FF_BLK = D_FF // TC
GELU_C = math.sqrt(2.0 / math.pi)


def _shift_down(x, n):
    rows = lax.broadcasted_iota(jnp.int32, x.shape, 0)
    return jnp.where(rows >= n, pltpu.roll(x, n, axis=0), 0.0)


def _shift_up(x, n):
    rows = lax.broadcasted_iota(jnp.int32, x.shape, 0)
    return jnp.where(rows < x.shape[0] - n, pltpu.roll(x, x.shape[0] - n, axis=0), 0.0)


def _conv(u, w, b):
    s1, s2 = _shift_down(u, 1), _shift_down(u, 2)
    return w[2:3, :] * u + w[1:2, :] * s1 + w[0:1, :] * s2 + b, s1, s2


def _gelu_parts(x):
    inner = GELU_C * (x + 0.044715 * x * x * x)
    t = jnp.tanh(inner)
    gelu = 0.5 * x * (1.0 + t)
    dgelu = 0.5 * (1.0 + t) + 0.5 * x * (1.0 - t * t) * GELU_C * (1.0 + 3 * 0.044715 * x * x)
    return gelu, dgelu


def _conv_specs():
    ug = pl.BlockSpec((S, TC), lambda c: (0, c))
    uv = pl.BlockSpec((S, TC), lambda c: (0, N_FF_BLK + c))
    wg = pl.BlockSpec((3, TC), lambda c: (0, c))
    wv = pl.BlockSpec((3, TC), lambda c: (0, N_FF_BLK + c))
    bg = pl.BlockSpec((1, TC), lambda c: (0, c))
    bv = pl.BlockSpec((1, TC), lambda c: (0, N_FF_BLK + c))
    return ug, uv, wg, wv, bg, bv


def _conv_fwd(u, conv_w, conv_b):
    ug, uv, wg, wv, bg, bv = _conv_specs()

    def body(ug_ref, uv_ref, wg_ref, wv_ref, bg_ref, bv_ref, a_ref):
        gc = _conv(ug_ref[...], wg_ref[...], bg_ref[...])[0]
        vc = _conv(uv_ref[...], wv_ref[...], bv_ref[...])[0]
        a_ref[...] = (_gelu_parts(gc)[0] * vc).astype(BF16)

    return pl.pallas_call(
        body, name="conv_fwd", grid=(N_FF_BLK,), in_specs=[ug, uv, wg, wv, bg, bv], out_specs=ug,
        out_shape=jax.ShapeDtypeStruct((S, D_FF), BF16),
        compiler_params=_params(("parallel",)))(u, u, conv_w, conv_w, conv_b, conv_b)


def _conv_bwd(u, conv_w, conv_b, da):
    ug, uv, wg, wv, bg, bv = _conv_specs()

    def back(duc, u, s1, s2, w):
        du = w[2:3, :] * duc + w[1:2, :] * _shift_up(duc, 1) + w[0:1, :] * _shift_up(duc, 2)
        dw = _three_rows([jnp.sum(duc * s2, axis=0, keepdims=True), jnp.sum(duc * s1, axis=0, keepdims=True),
                          jnp.sum(duc * u, axis=0, keepdims=True)])
        return du, dw, jnp.sum(duc, axis=0, keepdims=True)

    def body(ug_ref, uv_ref, wg_ref, wv_ref, bg_ref, bv_ref, da_ref, dug_ref, duv_ref, dwg_ref, dwv_ref, dbg_ref, dbv_ref):
        u_g, u_v = ug_ref[...], uv_ref[...]
        gc, g1, g2 = _conv(u_g, wg_ref[...], bg_ref[...])
        vc, v1, v2 = _conv(u_v, wv_ref[...], bv_ref[...])
        gelu, dgelu = _gelu_parts(gc)
        da = da_ref[...]
        du, dw, db = back(da * vc * dgelu, u_g, g1, g2, wg_ref[...])
        dug_ref[...] = du.astype(BF16)
        dwg_ref[...] = dw
        dbg_ref[...] = db
        du, dw, db = back(da * gelu, u_v, v1, v2, wv_ref[...])
        duv_ref[...] = du.astype(BF16)
        dwv_ref[...] = dw
        dbv_ref[...] = db

    return pl.pallas_call(
        body, name="conv_bwd", grid=(N_FF_BLK,), in_specs=[ug, uv, wg, wv, bg, bv, ug],
        out_specs=[ug, ug, wg, wg, bg, bg],
        out_shape=[jax.ShapeDtypeStruct((S, D_FF), BF16), jax.ShapeDtypeStruct((S, D_FF), BF16),
                   jax.ShapeDtypeStruct((3, D_FF), F32), jax.ShapeDtypeStruct((3, D_FF), F32),
                   jax.ShapeDtypeStruct((1, D_FF), F32), jax.ShapeDtypeStruct((1, D_FF), F32)],
        compiler_params=_params(("parallel",)))(u, u, conv_w, conv_w, conv_b, conv_b, da)


def _adamw(name, w, g, m, v):
    shape = w.shape
    cols = shape[-1]
    flat = [t.reshape(-1, cols) for t in (w, g, m, v)]
    r = flat[0].shape[0]
    tr = min(256, r)

    def body(w_ref, g_ref, m_ref, v_ref, d_ref, mo_ref, vo_ref):
        g = g_ref[...]
        m = ADAM_B1 * m_ref[...] + (1.0 - ADAM_B1) * g
        v = ADAM_B2 * v_ref[...] + (1.0 - ADAM_B2) * (g * g)
        m_hat = m / (1.0 - ADAM_B1 ** ADAM_STEP)
        v_hat = v / (1.0 - ADAM_B2 ** ADAM_STEP)
        d_ref[...] = -ADAM_LR * (m_hat / (jnp.sqrt(v_hat) + ADAM_EPS) + ADAM_WD * w_ref[...])
        mo_ref[...] = m
        vo_ref[...] = v

    spec = pl.BlockSpec((tr, cols), lambda i: (i, 0))
    outs = pl.pallas_call(
        body, name=name, grid=(pl.cdiv(r, tr),), in_specs=[spec] * 4, out_specs=[spec] * 3,
        out_shape=[jax.ShapeDtypeStruct((r, cols), F32)] * 3, compiler_params=_params(("parallel",)))(*flat)
    return [t.reshape(shape) for t in outs]


def _place():
    x, y, c = lax.axis_index("x"), lax.axis_index("y"), lax.axis_index("c")
    chips = [(1 - x, y), (x, 1 - y), (1 - x, 1 - y)]
    return x, y, c, chips


def _scalars(*vals):
    return jnp.stack([jnp.asarray(v, jnp.int32) for v in vals])


HBM = pl.BlockSpec(memory_space=pltpu.HBM)
SEM = pl.BlockSpec(memory_space=pltpu.SEMAPHORE)
SPLIT_COPY = pltpu.CompilerParams(has_side_effects=pltpu.SideEffectType.DATAFLOW_SIDE_EFFECTING)


def _in_hbm(x):
    return pltpu.with_memory_space_constraint(x, pltpu.HBM)


def _cast_into_slot(name, w, layer, chip):
    _, k, n4 = w.shape
    tr = max(t for t in range(16, 257, 16) if k % t == 0)

    def body(chip_ref, w_ref, o_ref):
        o_ref[...] = w_ref[...].astype(BF16)

    return pl.pallas_call(
        body, name=name,
        grid_spec=pltpu.PrefetchScalarGridSpec(
            num_scalar_prefetch=1, grid=(k // tr,),
            in_specs=[pl.BlockSpec((None, tr, n4), lambda i, s: (layer, i, 0))],
            out_specs=pl.BlockSpec((None, tr, n4), lambda i, s: (s[0], i, 0))),
        out_shape=jax.ShapeDtypeStruct((N_CHIPS, k, n4), BF16),
        compiler_params=_params(("parallel",)))(_scalars(chip), w)


def _gather_copy(buf_ref, k, from_chip, send_sem, recv_sem, chips, c, half=False):
    rows = buf_ref.at[from_chip]
    if half:
        h = buf_ref.shape[1] // 2
        rows = buf_ref.at[from_chip, pl.ds(pl.multiple_of(c * h, h), h)]
    return pltpu.make_async_remote_copy(src_ref=rows, dst_ref=rows, send_sem=send_sem, recv_sem=recv_sem,
                                        device_id=(*chips[k], c), device_id_type=MESH)


def _gather_start(name, bufs, groups, halved=()):
    n, ng = len(bufs), len(groups)
    where = {a: (gi, e) for gi, g in enumerate(groups) for e, a in enumerate(g)}

    def body(*refs):
        ins, sems, token = refs[:n], refs[n:n + 2 * ng], refs[-1]
        x, y, c, chips = _place()
        for a in range(n):
            gi, e = where[a]
            for k in range(3):
                _gather_copy(ins[a], k, 2 * x + y, sems[2 * gi].at[3 * e + k], sems[2 * gi + 1].at[3 * e + k],
                             chips, c, a in halved).start()
        token[...] = jnp.zeros_like(token)

    out_shape = [pltpu.SemaphoreType.DMA((3 * len(g),)) for g in groups for _ in range(2)]
    out_shape += [pltpu.HBM(b.shape, b.dtype) for b in bufs] + [jax.ShapeDtypeStruct((8, 128), F32)]
    out = pl.pallas_call(
        body, name=name, in_specs=[HBM] * n,
        out_specs=[SEM] * (2 * ng) + [HBM] * n + [pl.BlockSpec(memory_space=pltpu.VMEM)], out_shape=out_shape,
        input_output_aliases={a: 2 * ng + a for a in range(n)}, compiler_params=SPLIT_COPY)(*[_in_hbm(b) for b in bufs])
    sems = [(out[2 * gi], out[2 * gi + 1]) for gi in range(ng)]
    return sems, list(out[2 * ng:2 * ng + n]), out[-1]


def _gather_wait(name, bufs, send, recv, after, halved=()):
    n = len(bufs)

    def body(*refs):
        ins, send_sem, recv_sem = refs[:n], refs[n], refs[n + 1]
        x, y, c, chips = _place()
        for e in range(n):
            for k in range(3):
                sems = (send_sem.at[3 * e + k], recv_sem.at[3 * e + k])
                _gather_copy(ins[e], k, 2 * x + y, *sems, chips, c, e in halved).wait_send()
                _gather_copy(ins[e], k, 2 * chips[k][0] + chips[k][1], *sems, chips, c, e in halved).wait_recv()

    return pl.pallas_call(
        body, name=name, in_specs=[HBM] * n + [SEM, SEM, ANY], out_specs=[HBM] * n,
        out_shape=[pltpu.HBM(b.shape, b.dtype) for b in bufs],
        input_output_aliases={a: a for a in range(n)}, compiler_params=SPLIT_COPY)(*bufs, send, recv, after)


def _swap_halves(name, bufs):
    n = len(bufs)

    def body(*refs):
        ins, outs = refs[:n], refs[n:2 * n]
        send_sem, recv_sem = refs[2 * n:]
        x, y, c, chips = _place()

        def piece(ref, k, which):
            h = ref.shape[1] // 2
            return ref.at[2 * chips[k][0] + chips[k][1], pl.ds(pl.multiple_of(which * h, h), h)]

        def copy(a, k, which):
            return pltpu.make_async_remote_copy(
                src_ref=piece(ins[a], k, c), dst_ref=piece(outs[a], k, which), send_sem=send_sem.at[3 * a + k],
                recv_sem=recv_sem.at[3 * a + k], device_id=(x, y, 1 - c), device_id_type=MESH)

        for a in range(n):
            for k in range(3):
                copy(a, k, c).start()
        for a in range(n):
            for k in range(3):
                copy(a, k, c).wait_send()
                copy(a, k, 1 - c).wait_recv()

    return pl.pallas_call(
        body, name=name, in_specs=[ANY] * n, out_specs=[ANY] * n,
        out_shape=[jax.ShapeDtypeStruct(b.shape, b.dtype) for b in bufs],
        input_output_aliases={a: a for a in range(n)},
        scratch_shapes=[pltpu.SemaphoreType.DMA((3 * n,)), pltpu.SemaphoreType.DMA((3 * n,))],
    )(*bufs)


def _reduce_copy(g_ref, land_ref, mask, send_sem, recv_sem, x, y, c, sending):
    px, py, pc = x ^ ((mask >> 2) & 1), y ^ ((mask >> 1) & 1), c ^ (mask & 1)
    half = g_ref.shape[1] // 2
    src = g_ref.at[2 * px + py, pl.ds(pl.multiple_of(pc * half, half), half)]
    dst = land_ref.at[4 * x + 2 * y + c] if sending else land_ref.at[4 * px + 2 * py + pc]
    return pltpu.make_async_remote_copy(src_ref=src, dst_ref=dst, send_sem=send_sem, recv_sem=recv_sem,
                                        device_id=(px, py, pc), device_id_type=MESH)


def _reduce_start(name, grads):
    n = len(grads)
    lands = [lax.empty((N_DEV, g.shape[1] // 2, g.shape[2]), g.dtype) for g in grads]

    def body(*refs):
        gs, ls, send_sem, recv_sem = refs[:n], refs[n:2 * n], refs[2 * n], refs[2 * n + 1]
        x, y, c, _ = _place()
        for a in range(n):
            for mask in range(1, N_DEV):
                s = (N_DEV - 1) * a + mask - 1
                _reduce_copy(gs[a], ls[a], mask, send_sem.at[s], recv_sem.at[s], x, y, c, True).start()
        refs[-1][...] = jnp.zeros_like(refs[-1])

    sem = pltpu.SemaphoreType.DMA((n * (N_DEV - 1),))
    out = pl.pallas_call(
        body, name=name, in_specs=[HBM] * (2 * n),
        out_specs=[SEM, SEM] + [HBM] * (2 * n) + [pl.BlockSpec(memory_space=pltpu.VMEM)],
        out_shape=[sem, sem] + [pltpu.HBM(t.shape, t.dtype) for t in grads + lands] + [jax.ShapeDtypeStruct((8, 128), F32)],
        input_output_aliases={a: 2 + a for a in range(2 * n)}, compiler_params=SPLIT_COPY)(
            *[_in_hbm(t) for t in grads + lands])
    return out[0], out[1], list(out[2:2 + n]), list(out[2 + n:2 + 2 * n]), out[-1]


def _reduce_wait(name, send, recv, grads, lands, after):
    n = len(grads)

    def body(*refs):
        gs, ls, send_sem, recv_sem = refs[:n], refs[n:2 * n], refs[2 * n], refs[2 * n + 1]
        x, y, c, _ = _place()
        for a in range(n):
            for mask in range(1, N_DEV):
                s = (N_DEV - 1) * a + mask - 1
                sems = (send_sem.at[s], recv_sem.at[s])
                _reduce_copy(gs[a], ls[a], mask, *sems, x, y, c, True).wait_send()
                _reduce_copy(gs[a], ls[a], mask, *sems, x, y, c, False).wait_recv()

    out = pl.pallas_call(
        body, name=name, in_specs=[HBM] * (2 * n) + [SEM, SEM, ANY], out_specs=[HBM] * (2 * n),
        out_shape=[pltpu.HBM(t.shape, t.dtype) for t in grads + lands],
        input_output_aliases={a: a for a in range(2 * n)}, compiler_params=SPLIT_COPY)(*grads, *lands, send, recv, after)
    return list(out[:n]), list(out[n:])


def _reduce_sum(name, g, land, layer, into, chip, c):
    _, k4, n4 = g.shape
    half = k4 // 2
    tr = max(t for t in range(16, 513, 16) if half % t == 0)
    per = half // tr
    me = 2 * chip + c

    def body(s_ref, own_ref, *refs):
        total = own_ref[...].astype(F32)
        for ref in refs[:N_DEV - 1]:
            total = total + ref[...].astype(F32)
        refs[-1][...] = total

    in_specs = [pl.BlockSpec((None, tr, n4), lambda i, s: (s[0], s[1] * per + i, 0))]
    in_specs += [pl.BlockSpec((None, tr, n4), functools.partial(lambda i, s, m: (s[1 + m], i, 0), m=m))
                 for m in range(1, N_DEV)]
    ins = [g] + [land] * (N_DEV - 1)
    aliases = {}
    if into is not None:
        in_specs, ins, aliases = in_specs + [ANY], ins + [into], {1 + N_DEV: 0}
    return pl.pallas_call(
        body, name=name,
        grid_spec=pltpu.PrefetchScalarGridSpec(
            num_scalar_prefetch=1, grid=(per,), in_specs=in_specs,
            out_specs=pl.BlockSpec((None, tr, n4), lambda i, s: (layer, s[1] * per + i, 0))),
        out_shape=jax.ShapeDtypeStruct((DEPTH, k4, n4), F32), input_output_aliases=aliases,
        compiler_params=_params(("parallel",)))(_scalars(chip, c, *[me ^ m for m in range(1, N_DEV)]), *ins)


def _join_halves(name, bufs):
    n = len(bufs)

    def body(*refs):
        ins, outs = refs[:n], refs[n:2 * n]
        send_sem, recv_sem = refs[2 * n:]
        x, y, c, _ = _place()

        def rows(ref, which):
            half = ref.shape[1] // 2
            return ref.at[:, pl.ds(pl.multiple_of(which * half, half), half)]

        sends = [pltpu.make_async_remote_copy(
            src_ref=rows(ins[a], c), dst_ref=rows(outs[a], c), send_sem=send_sem.at[a], recv_sem=recv_sem.at[a],
            device_id=(x, y, 1 - c), device_id_type=MESH) for a in range(n)]
        for cp in sends:
            cp.start()
        for a in range(n):
            sends[a].wait_send()
            pltpu.make_async_remote_copy(
                src_ref=rows(ins[a], c), dst_ref=rows(outs[a], 1 - c), send_sem=send_sem.at[a], recv_sem=recv_sem.at[a],
                device_id=(x, y, 1 - c), device_id_type=MESH).wait_recv()

    return pl.pallas_call(
        body, name=name, in_specs=[ANY] * n, out_specs=[ANY] * n,
        out_shape=[jax.ShapeDtypeStruct(b.shape, b.dtype) for b in bufs],
        input_output_aliases={a: a for a in range(n)},
        scratch_shapes=[pltpu.SemaphoreType.DMA((n,)), pltpu.SemaphoreType.DMA((n,))],
    )(*bufs)


def _all_reduce_small(block):
    r = block.shape[0]

    def body(x_ref, out_ref, slots, send_sem, recv_sem):
        x, y, c, _ = _place()
        me = 4 * x + 2 * y + c
        slots[me] = x_ref[...]
        sends = []
        for mask in range(1, N_DEV):
            fx, fy, fc = (mask >> 2) & 1, (mask >> 1) & 1, mask & 1
            peer = (x ^ fx, y ^ fy, c ^ fc)
            cp = pltpu.make_async_remote_copy(
                src_ref=x_ref, dst_ref=slots.at[me], send_sem=send_sem.at[mask - 1], recv_sem=recv_sem.at[mask - 1],
                device_id=peer, device_id_type=MESH)
            cp.start()
            sends.append(cp)
        for mask in range(1, N_DEV):
            src = me ^ mask
            pltpu.make_async_remote_copy(
                src_ref=x_ref, dst_ref=slots.at[src], send_sem=send_sem.at[mask - 1], recv_sem=recv_sem.at[mask - 1],
                device_id=(x, y, c), device_id_type=MESH).wait_recv()
        for cp in sends:
            cp.wait_send()
        total = slots[0]
        for d in range(1, N_DEV):
            total = total + slots[d]
        out_ref[...] = total

    vmem = pl.BlockSpec(memory_space=pltpu.VMEM)
    return pl.pallas_call(
        body, name="all_reduce_small", in_specs=[vmem], out_specs=vmem,
        out_shape=jax.ShapeDtypeStruct((r, 128), F32),
        scratch_shapes=[pltpu.VMEM((N_DEV, r, 128), F32), pltpu.SemaphoreType.DMA((N_DEV - 1,)),
                        pltpu.SemaphoreType.DMA((N_DEV - 1,))],
        compiler_params=pltpu.CompilerParams(vmem_limit_bytes=VMEM_LIMIT))(block)


B_Q_COL = 2304 // 128
B_K0, B_V0, B_END = 2816, 2944, 3072


def _full_cols(w_g):
    return w_g.transpose(1, 0, 2).reshape(w_g.shape[1], -1)


def _group_src(proj, g):
    return ((proj, 2 * g), (proj, 6 + 2 * g), (proj, 12 + 2 * g))


def _kv_expand(kv):
    return jnp.broadcast_to(kv.reshape(S, 2, 1, HD), (S, 2, 4, HD)).reshape(S, 8 * HD)


def _kv_reduce(dkv):
    return dkv.reshape(S, 2, 4, HD).sum(axis=2).reshape(S, 2 * HD)


def _mixer_fwd(h1, wget, rel_bias, sinks_l, bidx):
    w = dict(wget(0, h1))
    proj = _mm_nt("proj_in", h1, w["w_in"], F32, tm=S, tn=1152)
    no_sinks = jnp.full((4,), NEG, F32)
    srcs = [_group_src(proj, g) for g in range(3)]
    o_g, lse_g = [], []
    for g, (_, d) in enumerate(A_GROUPS):
        o, lse = _band_fwd("band_fwd_g%d" % g, d, 2, BLK, 4 * g, srcs[g], bidx[g], rel_bias, no_sinks)
        o_g.append(o)
        lse_g.append(lse)
    o_a32, o_a, lse_a = _comb_fwd(o_g, lse_g)
    src_b = ((proj, B_Q_COL), (_kv_expand(proj[:, B_K0:B_V0]), 0), (_kv_expand(proj[:, B_V0:B_END]), 0))
    o_b32, lse_b = _band_fwd("band_fwd_b", 1, 4, BLK - 1, N_A, src_b, bidx[3], rel_bias, sinks_l)
    o_b = o_b32.astype(BF16)
    o_c32, o_c, tot_c = _sb_fwd(proj)
    w.update(wget(1, o_c32))
    br = [_mm_nn("branch_a", o_a, w["w_br_a"], F32, tm=S), _mm_nn("branch_b", o_b, w["w_br_b"], F32, tm=S),
          _mm_nn("branch_c", o_c, w["w_br_c"], F32, tm=S)]
    merged = _gate_fwd(proj, w["b_gate"], br)
    mo = _mm_nn("out_proj", merged, w["w_out"], F32, tm=S)
    saved = dict(proj=proj, srcs=srcs, src_b=src_b, o_a32=o_a32, lse_a=lse_a, o_b32=o_b32, lse_b=lse_b, tot_c=tot_c,
                 o_a=o_a, o_b=o_b, o_c=o_c, br=br, merged=merged)
    return mo, saved, w


def _mixer_bwd(d_mo, h1, w, sv, rel_bias, sinks_l, bidx, stats_in, emit):
    grads = {}
    dmerged = _mm_nt("out_proj_dx", d_mo, w["w_out"], F32, tm=S)
    grads["w_out"] = _mm_tn_sharded("out_proj_dw", sv["merged"], d_mo, True)
    e, dgate, db_gate = _gate_bwd(sv["proj"], w["b_gate"], sv["br"], dmerged)
    grads["b_gate"] = db_gate
    d_o = {}
    for n, name in enumerate("abc"):
        d_o[name] = _mm_nt("branch_%s_dx" % name, e[n], w["w_br_" + name], F32, tm=S)
        grads["w_br_" + name] = _mm_tn_sharded("branch_%s_dw" % name, sv["o_" + name], e[n], False)
    zero = emit(1, grads)
    no_sinks = jnp.full((4,), NEG, F32) + zero[0]
    dqs, dks, dvs, stats = [], [], [], []
    for g, (_, d) in enumerate(A_GROUPS):
        dq, dk, dv, st = _band_bwd("band_bwd_g%d" % g, d, 2, BLK, 4 * g, sv["srcs"][g], bidx[g], rel_bias, no_sinks,
                                   sv["o_a32"], sv["lse_a"], d_o["a"], stats_in[4 * g:4 * g + 4])
        dqs.append(dq)
        dks.append(dk)
        dvs.append(dv)
        stats.append(st)
    dq_b, dk_x, dv_x, st = _band_bwd("band_bwd_b", 1, 4, BLK - 1, N_A, sv["src_b"], bidx[3], rel_bias, sinks_l,
                                     sv["o_b32"], sv["lse_b"], d_o["b"], stats_in[N_A:])
    stats = jnp.concatenate(stats + [st], axis=0)
    dcq, dck, dcv = _sb_bwd(sv["proj"], sv["tot_c"], d_o["c"])
    cols = dqs + dks + dvs + [dq_b, _kv_reduce(dk_x), _kv_reduce(dv_x), dcq, dck, dcv]
    dproj = jnp.concatenate([t.astype(BF16) for t in cols] + list(dgate), axis=1)
    grads["w_in"] = _mm_tn("proj_in_dw", dproj, h1, BF16, tm=1152, tn=1024).reshape(N_CHIPS, IN_SHARD, D)
    zero = emit(2, grads)
    dh1 = _mm_nn("proj_in_dx", dproj, w["w_in"], F32, tm=S, tk=2304)
    return dh1, grads, stats, zero


def _ffn_fwd(h2, w):
    u = _mm_nn("ffn_up", h2, w["w_up"], F32, tm=S, tn=1024)
    a = _conv_fwd(u, w["conv_w"], w["conv_b"])
    dn = _mm_nn("ffn_down", a, w["w_down"], F32, tm=1024)
    return dn, dict(u=u, a=a)


def _ffn_bwd(d_dn, h2, w, sv):
    grads = {}
    da = _mm_nt("ffn_down_dx", d_dn, w["w_down"], F32, tm=S, tn=1024)
    grads["w_down"] = _mm_tn_sharded("ffn_down_dw", sv["a"], d_dn, True, tm=1024, tn=1024)
    dug, duv, dwg, dwv, dbg, dbv = _conv_bwd(sv["u"], w["conv_w"], w["conv_b"], da)
    du = jnp.concatenate([dug, duv], axis=1)
    grads["conv_w"] = jnp.concatenate([dwg, dwv], axis=1)
    grads["conv_b"] = jnp.concatenate([dbg, dbv], axis=1)
    dh2 = _mm_nt("ffn_up_dx", du, w["w_up"], F32, tm=S, tk=2048)
    grads["w_up"] = _mm_tn_sharded("ffn_up_dw", h2, du, False, tm=1024, tn=1024)
    return dh2, grads


BIG = ("w_in", "w_br_a", "w_br_b", "w_br_c", "w_out", "w_up", "w_down")


def _shard_view(name, w):
    return jnp.swapaxes(w, 1, 2) if name == "w_in" else w
WEIGHT_GROUPS = (("w_in", "b_gate"), ("w_br_a", "w_br_b", "w_br_c", "w_out"), ("w_up", "conv_w", "w_down"))
GRAD_GROUPS = (("w_down", "w_up"), ("w_out", "w_br_a", "w_br_b", "w_br_c"), ("w_in",))
SMALL_ROWS = (("rel_bias", 8), ("attn_pre_norm", 16), ("attn_post_norm", 16), ("ffn_pre_norm", 16), ("ffn_post_norm", 16),
              ("sinks", 8), ("conv_b", 128), ("b_gate", 48), ("conv_w", 384), ("loss", 8))


def _pack_small(vals):
    rows = []
    for name, n in SMALL_ROWS:
        flat = vals[name].reshape(-1).astype(F32)
        rows.append(jnp.pad(flat, (0, n * 128 - flat.shape[0])).reshape(n, 128))
    return jnp.concatenate(rows, axis=0)


def _unpack_small(block, shapes):
    out, row = {}, 0
    for name, n in SMALL_ROWS:
        size = int(np.prod(shapes[name]))
        out[name] = block[row:row + n].reshape(-1)[:size].reshape(shapes[name])
        row += n
    return out


def kernel(x, rel_bias, attn_pre_norm, w_in, b_gate, sinks, w_br_a, w_br_b, w_br_c, w_out, attn_post_norm, ffn_pre_norm, w_up, conv_w, conv_b, w_down, ffn_post_norm, loss_target, m_rel_bias, m_attn_pre_norm, m_w_in, m_b_gate, m_sinks, m_w_br_a, m_w_br_b, m_w_br_c, m_w_out, m_attn_post_norm, m_ffn_pre_norm, m_w_up, m_conv_w, m_conv_b, m_w_down, m_ffn_post_norm, v_rel_bias, v_attn_pre_norm, v_w_in, v_b_gate, v_sinks, v_w_br_a, v_w_br_b, v_w_br_c, v_w_out, v_attn_post_norm, v_ffn_pre_norm, v_w_up, v_conv_w, v_conv_b, v_w_down, v_ffn_post_norm):
    names = ("rel_bias", "attn_pre_norm", "w_in", "b_gate", "sinks", "w_br_a", "w_br_b", "w_br_c", "w_out",
             "attn_post_norm", "ffn_pre_norm", "w_up", "conv_w", "conv_b", "w_down", "ffn_post_norm")
    weights = dict(zip(names, (rel_bias, attn_pre_norm, w_in, b_gate, sinks, w_br_a, w_br_b, w_br_c, w_out,
                               attn_post_norm, ffn_pre_norm, w_up, conv_w, conv_b, w_down, ffn_post_norm)))
    mom1 = dict(zip(names, (m_rel_bias, m_attn_pre_norm, m_w_in, m_b_gate, m_sinks, m_w_br_a, m_w_br_b, m_w_br_c,
                            m_w_out, m_attn_post_norm, m_ffn_pre_norm, m_w_up, m_conv_w, m_conv_b, m_w_down,
                            m_ffn_post_norm)))
    mom2 = dict(zip(names, (v_rel_bias, v_attn_pre_norm, v_w_in, v_b_gate, v_sinks, v_w_br_a, v_w_br_b, v_w_br_c,
                            v_w_out, v_attn_post_norm, v_ffn_pre_norm, v_w_up, v_conv_w, v_conv_b, v_w_down,
                            v_ffn_post_norm)))

    chip = 2 * lax.axis_index("x") + lax.axis_index("y")
    core = lax.axis_index("c")

    keys = [(n, l) for l in range(DEPTH) for group in WEIGHT_GROUPS for n in group]
    groups = [[keys.index((n, l)) for n in group] for l in range(DEPTH) for group in WEIGHT_GROUPS]

    def slot_buffer(n, l):
        if n in BIG:
            return _cast_into_slot("cast_" + n, _shard_view(n, weights[n]), l, chip)
        shard = weights[n][l]
        return lax.dynamic_update_slice(jnp.zeros((N_CHIPS,) + shard.shape, F32), shard[None],
                                        (chip, jnp.int32(0), jnp.int32(0)))

    first = keys.index(("w_in", 0))
    n_first = len(groups[0])
    sems, in_flight, _ = _gather_start("gather_start_first", [slot_buffer(*k) for k in keys[:n_first]], groups[:1],
                                       (first,))
    more = _gather_start("gather_start", [slot_buffer(*k) for k in keys[n_first:]],
                         [[a - n_first for a in g] for g in groups[1:]])
    sems, in_flight, started = sems + more[0], in_flight + more[1], more[2]

    def wget(l, gi, after):
        g = l * len(WEIGHT_GROUPS) + gi
        after = started if g == 0 else after
        halved = tuple(e for e, a in enumerate(groups[g]) if a == first)
        got = list(_gather_wait("gather_wait_%d_%d" % (l, gi), [in_flight[a] for a in groups[g]], *sems[g], after,
                                halved))
        for e in halved:
            got[e] = _swap_halves("swap_halves", [got[e]])[0]
        out = {}
        for n, buf in zip(WEIGHT_GROUPS[gi], got):
            out[n] = buf.reshape(-1, buf.shape[-1]) if n in ("w_in", "w_out", "w_down") else _full_cols(buf)
        if gi == len(WEIGHT_GROUPS) - 1:
            out["conv_b"] = conv_b[l:l + 1]
        return out

    pending = []

    def emit(l, gi, grads):
        group = GRAD_GROUPS[gi]
        *started, token = _reduce_start("reduce_start_%d_%d" % (l, gi), [grads[n] for n in group])
        pending.append((l, group) + tuple(started))
        return token[:1, :1]

    local = _local_step(x.reshape(S, D), loss_target.reshape(S, D), wget, emit, rel_bias, sinks, attn_pre_norm,
                        attn_post_norm, ffn_pre_norm, ffn_post_norm)
    return _reduce_and_update(x.shape, names, weights, mom1, mom2, chip, core, pending, *local)


def _local_step(xs, target, wget, emit, rel_bias, sinks, attn_pre_norm, attn_post_norm, ffn_pre_norm, ffn_post_norm):
    bidx = jnp.asarray(_bucket_maps())

    saved, layers = [], []
    h1 = _rms_fwd("pre_norm_first", xs, attn_pre_norm[0:1])
    x_in = xs
    for l in range(DEPTH):
        mo, sv_mix, w = _mixer_fwd(h1, functools.partial(wget, l), rel_bias, sinks[l], bidx)
        x_mid, h2 = _post_pre_fwd("post_attn_norm", x_in, mo, attn_post_norm[l:l + 1], ffn_pre_norm[l:l + 1])
        w.update(wget(l, 2, h2))
        dn, sv_ffn = _ffn_fwd(h2, w)
        g_next = attn_pre_norm[l + 1:l + 2] if l + 1 < DEPTH else None
        x_out, h1_next = _post_pre_fwd("post_ffn_norm" if l + 1 < DEPTH else "post_ffn_norm_last", x_mid, dn,
                                       ffn_post_norm[l:l + 1], g_next)
        saved.append(dict(x_in=x_in, h1=h1, mo=mo, x_mid=x_mid, h2=h2, dn=dn, mix=sv_mix, ffn=sv_ffn))
        layers.append(w)
        x_in, h1 = x_out, h1_next

    loss_row, dres = _loss_kernel(x_in, target)

    small = [None] * DEPTH
    stats = jnp.zeros((N_BAND_Q, 8, 128), F32)
    dh_next = None
    for l in reversed(range(DEPTH)):
        w, sv = layers[l], saved[l]
        if l + 1 < DEPTH:
            pre = (saved[l + 1]["x_in"], attn_pre_norm[l + 1:l + 2] + zero, dh_next)
            dres, d_dn, dg_pre_next, dg_fpost = _norm_bwd("post_ffn_norm_bwd", dres, pre,
                                                          (sv["dn"], ffn_post_norm[l:l + 1]))
            small[l + 1]["attn_pre_norm"] = dg_pre_next
        else:
            dres, d_dn, _, dg_fpost = _norm_bwd("post_ffn_norm_last_bwd", dres, None, (sv["dn"], ffn_post_norm[l:l + 1]))
        dh2, g_ffn = _ffn_bwd(d_dn, sv["h2"], w, sv["ffn"])
        zero = emit(l, 0, g_ffn)
        dres, d_mo, dg_fpre, dg_apost = _norm_bwd("post_attn_norm_bwd", dres,
                                                  (sv["x_mid"], ffn_pre_norm[l:l + 1] + zero, dh2),
                                                  (sv["mo"], attn_post_norm[l:l + 1]))
        dh_next, g_mix, stats, zero = _mixer_bwd(d_mo, sv["h1"], w, sv["mix"], rel_bias, sinks[l], bidx, stats,
                                                 functools.partial(emit, l))
        small[l] = dict(ffn_post_norm=dg_fpost, ffn_pre_norm=dg_fpre, attn_post_norm=dg_apost,
                        sinks=stats[N_A:, 1, 0], conv_b=g_ffn["conv_b"], b_gate=g_mix["b_gate"], conv_w=g_ffn["conv_w"])
    grad_x, _, dg_pre0, _ = _norm_bwd("pre_norm_first_bwd", dres, (saved[0]["x_in"], attn_pre_norm[0:1] + zero, dh_next),
                                      None)
    small[0]["attn_pre_norm"] = dg_pre0
    return loss_row, grad_x, small, stats


def _reduce_and_update(x_shape, names, weights, mom1, mom2, chip, core, pending, loss_row, grad_x, small, stats):
    delta, new_m, new_v, grads = {}, {}, {}, {}

    def update(n, g):
        grads[n] = g
        delta[n], new_m[n], new_v[n] = _adamw("adamw_" + n, _shard_view(n, weights[n]), g,
                                              _shard_view(n, mom1[n]), _shard_view(n, mom2[n]))

    summed = {}

    def finish(which, after):
        for l, group, send, recv, gs, lands in pending:
            if (group == ("w_in",)) == which:
                gs, lands = _reduce_wait("reduce_wait_%d_%s" % (l, group[0]), send, recv, gs, lands, after)
                for n, g, land in zip(group, gs, lands):
                    summed[n] = _reduce_sum("reduce_sum_%d_%s" % (l, n), g, land, l, summed.get(n), chip, core)

    finish(False, grad_x)
    early = [n for n in BIG if n != "w_in"]
    for n, g in zip(early, _join_halves("join_halves", [summed[n] for n in early])):
        update(n, g)
    finish(True, delta[early[-1]])
    update("w_in", _join_halves("join_halves_w_in", [summed["w_in"]])[0])
    for out in (grads, delta, new_m, new_v):
        out["w_in"] = _shard_view("w_in", out["w_in"])

    small_vals = {n: jnp.stack([small[l][n].reshape(weights[n].shape[1:]) for l in range(DEPTH)])
                  for n in ("attn_pre_norm", "attn_post_norm", "ffn_pre_norm", "ffn_post_norm", "conv_b", "sinks")}
    small_vals["b_gate"] = jnp.stack([small[l]["b_gate"] for l in range(DEPTH)])
    small_vals["conv_w"] = jnp.stack([small[l]["conv_w"] for l in range(DEPTH)])
    small_vals["rel_bias"] = stats[:, 0, :NUM_BUCKETS].T
    small_vals["loss"] = loss_row[0, :1]
    shapes = {n: v.shape for n, v in small_vals.items()}
    packed, delta["w_in"] = lax.optimization_barrier((_pack_small(small_vals), delta["w_in"]))
    reduced = _unpack_small(_all_reduce_small(packed), shapes)
    reduced["b_gate"] = lax.dynamic_slice_in_dim(reduced["b_gate"], chip * (D // N_CHIPS), D // N_CHIPS, axis=2)
    reduced["conv_w"] = lax.dynamic_slice_in_dim(reduced["conv_w"], chip * (2 * D_FF // N_CHIPS), 2 * D_FF // N_CHIPS, axis=2)
    for n in names:
        if n not in grads:
            update(n, reduced[n].reshape(weights[n].shape))

    loss = reduced["loss"].reshape(())
    return (loss, grad_x.reshape(x_shape), *[grads[n] for n in names], *[delta[n] for n in names],
            *[new_m[n] for n in names], *[new_v[n] for n in names])
```

```python
import functools
import math

import numpy as np
import jax
import jax.numpy as jnp
from jax import lax
from jax.experimental import pallas as pl
from jax.experimental.pallas import tpu as pltpu

F32 = jnp.float32
BF16 = jnp.bfloat16

S = 2048
D = 1024
DEPTH = 2
HD = 64
BLK = 128
NQB = S // BLK
A_GROUPS = ((128, 1), (512, 4), (2048, 16))
N_BAND_Q = 20
N_A = 12
NUM_BUCKETS = 32
MAX_DISTANCE = 2048
D_FF = 4096
IN_COLS = 6912
IN_SHARD = IN_COLS // 4
OFF_GATE = 3840
EPS = 1e-6
SCALE = HD ** -0.5
NEG = -1e30
N_CHIPS = 4
N_DEV = 8

ADAM_LR = 0.001
ADAM_B1 = 0.9
ADAM_B2 = 0.999
ADAM_EPS = 1e-08
ADAM_WD = 0.01
ADAM_STEP = 10

VMEM_LIMIT = 56 * 1024 * 1024

NN = (((1,), (0,)), ((), ()))
NT = (((1,), (1,)), ((), ()))
TN = (((0,), (0,)), ((), ()))

MESH = pl.DeviceIdType.MESH
ANY = pl.BlockSpec(memory_space=pl.ANY)


def _dot(a, b, dims):
    return lax.dot_general(a, b, dims, preferred_element_type=F32)


def _params(sem):
    return pltpu.CompilerParams(dimension_semantics=sem, vmem_limit_bytes=VMEM_LIMIT)


def _matmul(name, a, b, out_shape, out_dtype, grid, a_spec, b_spec, o_spec, dims, acc_shape):
    nk = grid[-1]

    def body(a_ref, b_ref, o_ref, *scratch):
        part = _dot(a_ref[...].astype(BF16), b_ref[...].astype(BF16), dims)
        if nk == 1:
            o_ref[...] = part.astype(o_ref.dtype)
            return
        acc_ref, = scratch
        k = pl.program_id(len(grid) - 1)

        @pl.when(k == 0)
        def _():
            acc_ref[...] = part

        @pl.when(k > 0)
        def _():
            acc_ref[...] += part

        @pl.when(k == nk - 1)
        def _():
            o_ref[...] = acc_ref[...].astype(o_ref.dtype)

    scratch = [] if nk == 1 else [pltpu.VMEM(acc_shape, F32)]
    sem = ("parallel",) * (len(grid) - 1) + ("arbitrary",)
    return pl.pallas_call(
        body, name=name, grid=grid, in_specs=[a_spec, b_spec], out_specs=o_spec,
        out_shape=jax.ShapeDtypeStruct(out_shape, out_dtype), scratch_shapes=scratch,
        compiler_params=_params(sem))(a, b)


FULL_K = 8192


def _mm_tn_sharded(name, a, b, row_sharded, tm=512, tn=512, tk=FULL_K):
    k, m = a.shape
    n = b.shape[1]
    m4, n4 = (m // N_CHIPS, n) if row_sharded else (m, n // N_CHIPS)
    tm, tn, tk = min(tm, m4), min(tn, n4), min(tk, k)
    per_m, per_n = m4 // tm, n4 // tn
    if row_sharded:
        o_map = lambda i, j, l: (i // per_m, i % per_m, j)
    else:
        o_map = lambda i, j, l: (j // per_n, i, j % per_n)
    return _matmul(name, a, b, (N_CHIPS, m4, n4), BF16, (m // tm, n // tn, k // tk),
                   pl.BlockSpec((tk, tm), lambda i, j, l: (l, i)),
                   pl.BlockSpec((tk, tn), lambda i, j, l: (l, j)),
                   pl.BlockSpec((None, tm, tn), o_map), TN, (tm, tn))


def _mm_nn(name, a, b, out_dtype, tm=512, tn=512, tk=FULL_K):
    m, k = a.shape
    n = b.shape[1]
    tm, tn, tk = min(tm, m), min(tn, n), min(tk, k)
    return _matmul(name, a, b, (m, n), out_dtype, (m // tm, n // tn, k // tk),
                   pl.BlockSpec((tm, tk), lambda i, j, l: (i, l)),
                   pl.BlockSpec((tk, tn), lambda i, j, l: (l, j)),
                   pl.BlockSpec((tm, tn), lambda i, j, l: (i, j)), NN, (tm, tn))


def _mm_nt(name, a, b, out_dtype, tm=512, tn=512, tk=FULL_K):
    m, k = a.shape
    n = b.shape[0]
    tm, tn, tk = min(tm, m), min(tn, n), min(tk, k)
    return _matmul(name, a, b, (m, n), out_dtype, (m // tm, n // tn, k // tk),
                   pl.BlockSpec((tm, tk), lambda i, j, l: (i, l)),
                   pl.BlockSpec((tn, tk), lambda i, j, l: (j, l)),
                   pl.BlockSpec((tm, tn), lambda i, j, l: (i, j)), NT, (tm, tn))


def _mm_tn(name, a, b, out_dtype, tm=512, tn=512, tk=FULL_K):
    k, m = a.shape
    n = b.shape[1]
    tm, tn, tk = min(tm, m), min(tn, n), min(tk, k)
    return _matmul(name, a, b, (m, n), out_dtype, (m // tm, n // tn, k // tk),
                   pl.BlockSpec((tk, tm), lambda i, j, l: (l, i)),
                   pl.BlockSpec((tk, tn), lambda i, j, l: (l, j)),
                   pl.BlockSpec((tm, tn), lambda i, j, l: (i, j)), TN, (tm, tn))


TR = 512


def _row_spec(width=D):
    return pl.BlockSpec((TR, width), lambda i: (i, 0))


def _vec_spec(width=D):
    return pl.BlockSpec((1, width), lambda i: (0, 0))


def _rms(x, g):
    r = lax.rsqrt(jnp.mean(x * x, axis=-1, keepdims=True) + EPS)
    return x * r * g


def _rms_fwd(name, x, g):
    def body(x_ref, g_ref, h_ref):
        h_ref[...] = _rms(x_ref[...], g_ref[...]).astype(BF16)

    return pl.pallas_call(
        body, name=name, grid=(S // TR,), in_specs=[_row_spec(), _vec_spec()], out_specs=_row_spec(),
        out_shape=jax.ShapeDtypeStruct((S, D), BF16), compiler_params=_params(("parallel",)))(x, g)


def _post_pre_fwd(name, x, y, g_post, g_pre):
    has_pre = g_pre is not None

    def body(*refs):
        if has_pre:
            x_ref, y_ref, gp_ref, gn_ref, xn_ref, h_ref = refs
        else:
            x_ref, y_ref, gp_ref, xn_ref = refs
        xn = x_ref[...] + _rms(y_ref[...], gp_ref[...])
        xn_ref[...] = xn
        if has_pre:
            h_ref[...] = _rms(xn, gn_ref[...]).astype(BF16)

    ins = [x, y, g_post] + ([g_pre] if has_pre else [])
    in_specs = [_row_spec(), _row_spec(), _vec_spec()] + ([_vec_spec()] if has_pre else [])
    out_shape = [jax.ShapeDtypeStruct((S, D), F32)] + ([jax.ShapeDtypeStruct((S, D), BF16)] if has_pre else [])
    out_specs = [_row_spec()] + ([_row_spec()] if has_pre else [])
    out = pl.pallas_call(
        body, name=name, grid=(S // TR,), in_specs=in_specs, out_specs=out_specs, out_shape=out_shape,
        compiler_params=_params(("parallel",)))(*ins)
    return out if has_pre else (out[0], None)


def _rms_bwd_math(x, g, dy):
    r = lax.rsqrt(jnp.mean(x * x, axis=-1, keepdims=True) + EPS)
    n = x * r
    dn = dy * g
    dx = r * (dn - n * jnp.mean(dn * n, axis=-1, keepdims=True))
    return dx, jnp.sum(dy * n, axis=0, keepdims=True)


def _norm_bwd(name, dres, pre=None, post=None):
    has_pre, has_post = pre is not None, post is not None

    def body(*refs):
        refs = list(refs)
        dres_ref = refs.pop(0)
        if has_pre:
            xn_ref, gn_ref, dh_ref = refs[:3]
            refs = refs[3:]
        if has_post:
            y_ref, gp_ref = refs[:2]
            refs = refs[2:]
        dxn_ref = refs.pop(0)
        dy_ref = refs.pop(0) if has_post else None
        dgn_ref = refs.pop(0) if has_pre else None
        dgp_ref = refs.pop(0) if has_post else None
        first = pl.program_id(0) == 0
        dxn = dres_ref[...]
        if has_pre:
            dx, dg = _rms_bwd_math(xn_ref[...], gn_ref[...], dh_ref[...])
            dxn = dxn + dx

            @pl.when(first)
            def _():
                dgn_ref[...] = dg

            @pl.when(jnp.logical_not(first))
            def _():
                dgn_ref[...] += dg
        dxn_ref[...] = dxn
        if has_post:
            dy, dg = _rms_bwd_math(y_ref[...], gp_ref[...], dxn)
            dy_ref[...] = dy.astype(BF16)

            @pl.when(first)
            def _():
                dgp_ref[...] = dg

            @pl.when(jnp.logical_not(first))
            def _():
                dgp_ref[...] += dg

    ins, in_specs = [dres], [_row_spec()]
    if has_pre:
        ins += list(pre)
        in_specs += [_row_spec(), _vec_spec(), _row_spec()]
    if has_post:
        ins += list(post)
        in_specs += [_row_spec(), _vec_spec()]
    out_shape, out_specs = [jax.ShapeDtypeStruct((S, D), F32)], [_row_spec()]
    if has_post:
        out_shape.append(jax.ShapeDtypeStruct((S, D), BF16))
        out_specs.append(_row_spec())
    for _ in range(int(has_pre) + int(has_post)):
        out_shape.append(jax.ShapeDtypeStruct((1, D), F32))
        out_specs.append(_vec_spec())
    out = list(pl.pallas_call(
        body, name=name, grid=(S // TR,), in_specs=in_specs, out_specs=out_specs, out_shape=out_shape,
        compiler_params=_params(("arbitrary",)))(*ins))
    dxn = out.pop(0)
    dy = out.pop(0) if has_post else None
    dgn = out.pop(0) if has_pre else None
    dgp = out.pop(0) if has_post else None
    return dxn, dy, dgn, dgp


def _loss_kernel(y, target):
    def body(y_ref, t_ref, loss_ref, dy_ref):
        e = y_ref[...] - t_ref[...]
        dy_ref[...] = e * (1.0 / D)
        part = jnp.zeros((1, 128), F32) + 0.5 * jnp.sum(jnp.mean(e * e, axis=-1, keepdims=True))

        @pl.when(pl.program_id(0) == 0)
        def _():
            loss_ref[...] = part

        @pl.when(pl.program_id(0) > 0)
        def _():
            loss_ref[...] += part

    return pl.pallas_call(
        body, name="loss", grid=(S // TR,), in_specs=[_row_spec(), _row_spec()],
        out_specs=[_vec_spec(128), _row_spec()],
        out_shape=[jax.ShapeDtypeStruct((1, 128), F32), jax.ShapeDtypeStruct((S, D), F32)],
        compiler_params=_params(("arbitrary",)))(y, target)


def _t5_bucket_np(dist):
    max_exact = NUM_BUCKETS // 2
    nf = np.maximum(dist, 1).astype(np.float32)
    large = max_exact + (np.log(nf / max_exact) / np.float32(math.log(MAX_DISTANCE / max_exact))
                         * (NUM_BUCKETS - max_exact)).astype(np.int32)
    large = np.minimum(large, NUM_BUCKETS - 1)
    return np.where(dist < max_exact, dist, large).astype(np.int32)


def _bucket_maps():
    a = np.arange(BLK)[:, None]
    b = np.arange(2 * BLK)[None, :]
    dist = np.maximum(a + BLK - b, 0)
    maps = [_t5_bucket_np(dist * d) for _, d in A_GROUPS] + [_t5_bucket_np(dist)]
    return np.stack(maps).astype(np.int32)


def _pair_spec(col0):
    return pl.BlockSpec((S, 128), lambda p: (0, col0 + p))


def _band_rows(i, d):
    nb = S // d // BLK
    r, b = i // nb, i % nb
    cur = pl.ds(b * BLK * d + r, BLK, stride=d)
    prev = pl.ds(jnp.maximum(b - 1, 0) * BLK * d + r, BLK, stride=d)
    return cur, prev, jnp.minimum(b, 1)


def _band_bias(tab_ref, bidx_ref, h):
    bi = bidx_ref[...]
    bias = jnp.zeros((BLK, 2 * BLK), F32)
    for kk in range(NUM_BUCKETS):
        bias = jnp.where(bi == kk, tab_ref[kk, h], bias)
    return bias


def _lane_lo(rows=BLK):
    return lax.broadcasted_iota(jnp.int32, (rows, 128), 1) < HD


def _per_head(x, lo):
    return (jnp.sum(jnp.where(lo, x, 0.0), axis=1, keepdims=True) * (1.0 / HD),
            jnp.sum(jnp.where(lo, 0.0, x), axis=1, keepdims=True) * (1.0 / HD))


def _band_fill(bias_ref, tab_ref, bidx_ref, head, maxd):
    a = lax.broadcasted_iota(jnp.int32, (BLK, 2 * BLK), 0)
    c = lax.broadcasted_iota(jnp.int32, (BLK, 2 * BLK), 1)
    dist = a + BLK - c
    in_band = jnp.logical_and(dist >= 0, dist <= maxd)
    for h in range(2):
        bias = jnp.where(in_band, _band_bias(tab_ref, bidx_ref, head + h), NEG)
        bias_ref[1, h * BLK:(h + 1) * BLK, :] = bias
        bias_ref[0, h * BLK:(h + 1) * BLK, :] = jnp.where(c >= BLK, bias, NEG)


def _stack_heads(x, lo, dtype=BF16):
    return jnp.concatenate([jnp.where(lo, x, 0.0), jnp.where(lo, 0.0, x)], axis=0).astype(dtype)


def _unstack_heads(x, lo):
    n = x.shape[0] // 2
    return jnp.where(lo, x[:n], x[n:])


def _stack_rows(ref, prev, cur):
    return jnp.concatenate([ref[prev, :], ref[cur, :]], axis=0).astype(BF16)


def _band_fwd(name, d, n_pairs, maxd, head0, srcs, bidx_g, tab, sinks):
    (qa, qc), (ka, kc), (va, vc) = srcs
    out_spec = _pair_spec(0)
    smem = pl.BlockSpec(memory_space=pltpu.SMEM)
    full = pl.BlockSpec((BLK, 2 * BLK), lambda p: (0, 0))

    def body(tab_ref, sink_ref, q_ref, k_ref, v_ref, bidx_ref, o_ref, lse_ref, bias_ref):
        p = pl.program_id(0)
        _band_fill(bias_ref, tab_ref, bidx_ref, head0 + 2 * p, maxd)
        lo = _lane_lo()
        sink = jnp.where(lax.broadcasted_iota(jnp.int32, (2 * BLK, 1), 0) < BLK, sink_ref[2 * p], sink_ref[2 * p + 1])

        def block(i, carry):
            cur, prev, has_prev = _band_rows(i, d)
            qs = _stack_heads(q_ref[cur, :] * SCALE, lo)
            ks, vs = _stack_rows(k_ref, prev, cur), _stack_rows(v_ref, prev, cur)
            s = _dot(qs, ks, NT) + bias_ref[has_prev]
            m = jnp.max(s, axis=1, keepdims=True)
            pr = jnp.exp(s - m)
            l = jnp.sum(pr, axis=1, keepdims=True)
            num = _dot(pr.astype(BF16), vs, NN)
            lse = m + jnp.log(l)
            sig = 1.0 / (1.0 + jnp.exp(sink - lse))
            o_ref[cur, :] = _unstack_heads(num * (sig / l), lo)
            lse_ref[cur, :] = _unstack_heads(lse + jnp.zeros((2 * BLK, 128), F32), lo)
            return carry

        lax.fori_loop(0, NQB, block, 0, unroll=2)

    shape = jax.ShapeDtypeStruct((S, n_pairs * 128), F32)
    return pl.pallas_call(
        body, name=name, grid=(n_pairs,),
        in_specs=[smem, smem, _pair_spec(qc), _pair_spec(kc), _pair_spec(vc), full],
        out_specs=[out_spec, out_spec], out_shape=[shape, shape],
        scratch_shapes=[pltpu.VMEM((2, 2 * BLK, 2 * BLK), F32)],
        compiler_params=_params(("parallel",)))(tab, sinks, qa, ka, va, bidx_g)


def _band_bwd(name, d, n_pairs, maxd, head0, srcs, bidx_g, tab, sinks, o, lse, do, stats_in):
    (qa, qc), (ka, kc), (va, vc) = srcs
    pair = _pair_spec(0)
    smem = pl.BlockSpec(memory_space=pltpu.SMEM)
    full = pl.BlockSpec((BLK, 2 * BLK), lambda p: (0, 0))
    stat_spec = pl.BlockSpec((2, 8, 128), lambda p: (p, 0, 0))

    def body(tab_ref, sink_ref, q_ref, k_ref, v_ref, bidx_ref, o_ref, lse_ref, do_ref, sin_ref,
             dq_ref, dk_ref, dv_ref, stat_ref, bias_ref, dsacc_ref, sk_ref):
        p = pl.program_id(0)
        _band_fill(bias_ref, tab_ref, bidx_ref, head0 + 2 * p, maxd)
        dsacc_ref[...] = jnp.zeros_like(dsacc_ref)
        sk_ref[...] = jnp.zeros_like(sk_ref)
        dk_ref[...] = jnp.zeros_like(dk_ref)
        dv_ref[...] = jnp.zeros_like(dv_ref)
        lo = _lane_lo()
        head1 = lax.broadcasted_iota(jnp.int32, (2 * BLK, 1), 0) >= BLK
        sink = jnp.where(head1, sink_ref[2 * p + 1], sink_ref[2 * p])

        def block(i, carry):
            cur, prev, has_prev = _band_rows(i, d)
            qs = _stack_heads(q_ref[cur, :] * SCALE, lo)
            ks, vs = _stack_rows(k_ref, prev, cur), _stack_rows(v_ref, prev, cur)
            do = do_ref[cur, :]
            dos = _stack_heads(do, lo, F32)
            lse = jnp.concatenate(_per_head(lse_ref[cur, :], lo), axis=0)
            prod = do * o_ref[cur, :]
            delta = jnp.concatenate([jnp.sum(jnp.where(lo, prod, 0.0), axis=1, keepdims=True),
                                     jnp.sum(jnp.where(lo, 0.0, prod), axis=1, keepdims=True)], axis=0)
            sig = 1.0 / (1.0 + jnp.exp(sink - lse))
            pr = jnp.exp(_dot(qs, ks, NT) + bias_ref[has_prev] - lse)
            ds = pr * (sig * (_dot(dos.astype(BF16), vs, NT) - delta))
            dsb = ds.astype(BF16)
            dq_ref[cur, :] = SCALE * _unstack_heads(_dot(dsb, ks, NN), lo)
            dk = _dot(dsb, qs, TN)
            dv = _dot(pr.astype(BF16), (sig * dos).astype(BF16), TN)
            dk_ref[prev, :] += dk[:BLK]
            dk_ref[cur, :] += dk[BLK:]
            dv_ref[prev, :] += dv[:BLK]
            dv_ref[cur, :] += dv[BLK:]
            dsacc_ref[...] += ds
            sink_grad = -delta * (1.0 - sig)
            for h in range(2):
                sk_ref[h] += jnp.zeros((8, 128), F32) + jnp.sum(sink_grad[h * BLK:(h + 1) * BLK])
            return carry

        lax.fori_loop(0, NQB, block, 0, unroll=2)

        bi = bidx_ref[...]
        lane = lax.broadcasted_iota(jnp.int32, (8, 128), 1)
        sub = lax.broadcasted_iota(jnp.int32, (8, 128), 0)
        for h in range(2):
            acc = dsacc_ref[h * BLK:(h + 1) * BLK, :]
            row = jnp.where(jnp.logical_and(sub == 1, lane == 0), sk_ref[h], 0.0)
            for kk in range(NUM_BUCKETS):
                tot = jnp.sum(jnp.where(bi == kk, acc, 0.0))
                row = jnp.where(jnp.logical_and(sub == 0, lane == kk), tot, row)
            stat_ref[h] = row + jnp.where(sub == 0, sin_ref[h], 0.0)

    shape = jax.ShapeDtypeStruct((S, n_pairs * 128), F32)
    return pl.pallas_call(
        body, name=name, grid=(n_pairs,),
        in_specs=[smem, smem, _pair_spec(qc), _pair_spec(kc), _pair_spec(vc), full, pair, pair, pair, stat_spec],
        out_specs=[pair, pair, pair, stat_spec],
        out_shape=[shape, shape, shape, jax.ShapeDtypeStruct((2 * n_pairs, 8, 128), F32)],
        scratch_shapes=[pltpu.VMEM((2, 2 * BLK, 2 * BLK), F32), pltpu.VMEM((2 * BLK, 2 * BLK), F32),
                        pltpu.VMEM((2, 8, 128), F32)],
        compiler_params=_params(("parallel",)))(tab, sinks, qa, ka, va, bidx_g, o, lse, do, stats_in)


def _comb_fwd(o_g, lse_g):
    def body(o0, o1, o2, l0, l1, l2, out_ref, outb_ref, lse_ref):
        a0, a1, a2 = l0[...], l1[...], l2[...]
        m = jnp.maximum(jnp.maximum(a0, a1), a2)
        e0, e1, e2 = jnp.exp(a0 - m), jnp.exp(a1 - m), jnp.exp(a2 - m)
        tot = e0 + e1 + e2
        out = (e0 * o0[...] + e1 * o1[...] + e2 * o2[...]) / tot
        out_ref[...] = out
        outb_ref[...] = out.astype(BF16)
        lse_ref[...] = m + jnp.log(tot)

    spec = _row_spec(4 * HD)
    f32 = jax.ShapeDtypeStruct((S, 4 * HD), F32)
    return pl.pallas_call(
        body, name="comb_fwd", grid=(S // TR,), in_specs=[spec] * 6, out_specs=[spec] * 3,
        out_shape=[f32, jax.ShapeDtypeStruct((S, 4 * HD), BF16), f32],
        compiler_params=_params(("parallel",)))(*o_g, *lse_g)


def _split2(x):
    hi = x.astype(BF16)
    return hi, (x - hi.astype(F32)).astype(BF16)


KB = 2 * BLK
SBQ = 2 * BLK


def _tri_sum(x, tri):
    hi, lo = _split2(x)
    both = _dot(jnp.concatenate([hi, lo], axis=0), tri, NN)
    return both[:x.shape[0]] + both[x.shape[0]:]


def _tri(strict_upper):
    r = lax.broadcasted_iota(jnp.int32, (KB, KB), 0)
    c = lax.broadcasted_iota(jnp.int32, (KB, KB), 1)
    return jnp.where(r > c if strict_upper else r < c, 1.0, 0.0).astype(BF16)


def _sb_terms(qs, kj, before):
    z = _dot(qs, kj, NT)
    lsp = jnp.minimum(z, 0.0) - jnp.log(1.0 + jnp.exp(-jnp.abs(z)))
    return lsp, _sb_keep(before, lsp - z)


def _sb_keep(before, x):
    return x if before is None else jnp.where(before, x, 0.0)


def _sb_before(i, m):
    t = (lax.broadcasted_iota(jnp.int32, (2 * SBQ, KB), 0) & (SBQ - 1)) + i * SBQ
    s = lax.broadcasted_iota(jnp.int32, (2 * SBQ, KB), 1) + m * KB
    return s < t


C_COL = 3072 // 128


def _sb_fwd(proj):
    blk = lambda off: pl.BlockSpec((SBQ, 128), lambda p, i: (i, off + p))
    col = lambda off: pl.BlockSpec((S, 128), lambda p, i: (0, off + p))
    out = pl.BlockSpec((SBQ, 128), lambda p, i: (i, p))

    def body(q_ref, k_ref, v_ref, o_ref, ob_ref, tot_ref):
        i = pl.program_id(1)
        lo = _lane_lo(SBQ)
        qs = _stack_heads(q_ref[...] * SCALE, lo)
        suffix = _tri(True)

        def step(n, carry, diagonal=False):
            acc, rest = carry
            m = i - n
            rows = pl.ds(pl.multiple_of(m * KB, KB), KB)
            kj, vj = k_ref[rows, :].astype(BF16), v_ref[rows, :].astype(BF16)
            before = _sb_before(i, m) if diagonal else None
            lsp, lk = _sb_terms(qs, kj, before)
            w = _sb_keep(before, jnp.exp(lsp + _tri_sum(lk, suffix) + rest))
            return acc + _dot(w.astype(BF16), vj, NN), rest + jnp.sum(lk, axis=1, keepdims=True)

        first = step(0, (jnp.zeros((2 * SBQ, 128), F32), jnp.zeros((2 * SBQ, 1), F32)), diagonal=True)
        acc, rest = lax.fori_loop(1, i + 1, step, first)
        o = _unstack_heads(acc, lo)
        o_ref[...] = o
        ob_ref[...] = o.astype(BF16)
        tot_ref[...] = _unstack_heads(rest + jnp.zeros((2 * SBQ, 128), F32), lo)

    f32 = jax.ShapeDtypeStruct((S, 4 * HD), F32)
    return pl.pallas_call(
        body, name="sb_fwd", grid=(2, S // SBQ), in_specs=[blk(C_COL), col(C_COL + 2), col(C_COL + 4)],
        out_specs=[out, out, out], out_shape=[f32, jax.ShapeDtypeStruct((S, 4 * HD), BF16), f32],
        compiler_params=_params(("parallel", "arbitrary")))(proj, proj, proj)


def _sb_bwd(proj, tot, do):
    blk = lambda off: pl.BlockSpec((SBQ, 128), lambda p, i: (i, off + p))
    col = lambda off: pl.BlockSpec((S, 128), lambda p, i: (0, off + p))

    def body(q_ref, k_ref, v_ref, tot_ref, do_ref, dq_ref, dk_ref, dv_ref):
        i = pl.program_id(1)

        @pl.when(i == 0)
        def _():
            dk_ref[...] = jnp.zeros_like(dk_ref)
            dv_ref[...] = jnp.zeros_like(dv_ref)

        lo = _lane_lo(SBQ)
        qs = _stack_heads(q_ref[...] * SCALE, lo)
        dos = _stack_heads(do_ref[...], lo)
        tots = jnp.concatenate(_per_head(tot_ref[...], lo), axis=0)
        prefix = _tri(False)

        def step(m, carry, diagonal=False):
            dq, keep_left, g_left = carry
            rows = pl.ds(pl.multiple_of(m * KB, KB), KB)
            kj, vj = k_ref[rows, :].astype(BF16), v_ref[rows, :].astype(BF16)
            before = _sb_before(i, m) if diagonal else None
            lsp, lk = _sb_terms(qs, kj, before)
            log_rest = tots - keep_left - lk - _tri_sum(lk, prefix)
            w = _sb_keep(before, jnp.exp(lsp + log_rest))
            g = w * _dot(dos, vj, NT)
            g_before = g_left + _dot(g.astype(BF16), prefix, NN)
            beta = jnp.exp(lsp)
            dz = _sb_keep(before, g * (1.0 - beta) - g_before * beta).astype(BF16)
            dk_ref[rows, :] += _dot(dz, qs, TN)
            dv_ref[rows, :] += _dot(w.astype(BF16), dos, TN)
            return (dq + _dot(dz, kj, NN), keep_left + jnp.sum(lk, axis=1, keepdims=True),
                    g_left + jnp.sum(g, axis=1, keepdims=True))

        zero = (jnp.zeros((2 * SBQ, 128), F32), jnp.zeros((2 * SBQ, 1), F32), jnp.zeros((2 * SBQ, 1), F32))
        dq, _, _ = step(i, lax.fori_loop(0, i, step, zero), diagonal=True)
        dq_ref[...] = SCALE * _unstack_heads(dq, lo)

    out_blk = pl.BlockSpec((SBQ, 128), lambda p, i: (i, p))
    out_col = pl.BlockSpec((S, 128), lambda p, i: (0, p))
    f32 = jax.ShapeDtypeStruct((S, 4 * HD), F32)
    return pl.pallas_call(
        body, name="sb_bwd", grid=(2, S // SBQ),
        in_specs=[blk(C_COL), col(C_COL + 2), col(C_COL + 4), out_blk, out_blk],
        out_specs=[out_blk, out_col, out_col], out_shape=[f32, f32, f32],
        compiler_params=_params(("arbitrary", "arbitrary")))(proj, proj, proj, tot, do)


TG = 256
TGR = 1024
GATE_BLK0 = OFF_GATE // TG


def _gate_specs():
    grid = (D // TG, S // TGR)
    p_specs = [pl.BlockSpec((TGR, TG), functools.partial(lambda c, r, br: (r, GATE_BLK0 + br * (D // TG) + c), br=br))
               for br in range(3)]
    b_spec = pl.BlockSpec((3, TG), lambda c, r: (0, c))
    t_spec = pl.BlockSpec((TGR, TG), lambda c, r: (r, c))
    return grid, p_specs, b_spec, t_spec


def _sigmoid(x):
    return 1.0 / (1.0 + jnp.exp(-x))


def _three_rows(rows):
    sub = lax.broadcasted_iota(jnp.int32, (3, rows[0].shape[1]), 0)
    return jnp.where(sub == 0, rows[0], jnp.where(sub == 1, rows[1], rows[2]))


def _gate_fwd(proj, b_gate, br):
    grid, p_specs, b_spec, t_spec = _gate_specs()

    def body(p0, p1, p2, b_ref, r0, r1, r2, out_ref):
        acc = jnp.zeros((TGR, TG), F32)
        for n, (p, r) in enumerate(((p0, r0), (p1, r1), (p2, r2))):
            acc += _sigmoid(p[...] + b_ref[n:n + 1, :]) * r[...]
        out_ref[...] = acc.astype(BF16)

    return pl.pallas_call(
        body, name="gate_fwd", grid=grid, in_specs=p_specs + [b_spec] + [t_spec] * 3, out_specs=t_spec,
        out_shape=jax.ShapeDtypeStruct((S, D), BF16),
        compiler_params=_params(("parallel", "parallel")))(proj, proj, proj, b_gate, *br)


def _gate_bwd(proj, b_gate, br, dmerged):
    grid, p_specs, b_spec, t_spec = _gate_specs()

    def body(p0, p1, p2, b_ref, r0, r1, r2, dm_ref, e0, e1, e2, g0, g1, g2, db_ref):
        dm = dm_ref[...]
        rows = []
        for n, (p, r, e_ref, dg_ref) in enumerate(((p0, r0, e0, g0), (p1, r1, e1, g1), (p2, r2, e2, g2))):
            g = _sigmoid(p[...] + b_ref[n:n + 1, :])
            e_ref[...] = (dm * g).astype(BF16)
            dpre = dm * r[...] * g * (1.0 - g)
            dg_ref[...] = dpre.astype(BF16)
            rows.append(jnp.sum(dpre, axis=0, keepdims=True))
        db = _three_rows(rows)

        @pl.when(pl.program_id(1) == 0)
        def _():
            db_ref[...] = db

        @pl.when(pl.program_id(1) > 0)
        def _():
            db_ref[...] += db

    bf = jax.ShapeDtypeStruct((S, D), BF16)
    out = pl.pallas_call(
        body, name="gate_bwd", grid=grid, in_specs=p_specs + [b_spec] + [t_spec] * 4,
        out_specs=[t_spec] * 6 + [b_spec], out_shape=[bf] * 6 + [jax.ShapeDtypeStruct((3, D), F32)],
        compiler_params=_params(("parallel", "arbitrary")))(proj, proj, proj, b_gate, *br, dmerged)
    return out[:3], out[3:6], out[6]


TC = 256
N_FF_BLK = D_FF // TC
GELU_C = math.sqrt(2.0 / math.pi)


def _shift_down(x, n):
    rows = lax.broadcasted_iota(jnp.int32, x.shape, 0)
    return jnp.where(rows >= n, pltpu.roll(x, n, axis=0), 0.0)


def _shift_up(x, n):
    rows = lax.broadcasted_iota(jnp.int32, x.shape, 0)
    return jnp.where(rows < x.shape[0] - n, pltpu.roll(x, x.shape[0] - n, axis=0), 0.0)


def _conv(u, w, b):
    s1, s2 = _shift_down(u, 1), _shift_down(u, 2)
    return w[2:3, :] * u + w[1:2, :] * s1 + w[0:1, :] * s2 + b, s1, s2


def _gelu_parts(x):
    inner = GELU_C * (x + 0.044715 * x * x * x)
    t = jnp.tanh(inner)
    gelu = 0.5 * x * (1.0 + t)
    dgelu = 0.5 * (1.0 + t) + 0.5 * x * (1.0 - t * t) * GELU_C * (1.0 + 3 * 0.044715 * x * x)
    return gelu, dgelu


def _conv_specs():
    ug = pl.BlockSpec((S, TC), lambda c: (0, c))
    uv = pl.BlockSpec((S, TC), lambda c: (0, N_FF_BLK + c))
    wg = pl.BlockSpec((3, TC), lambda c: (0, c))
    wv = pl.BlockSpec((3, TC), lambda c: (0, N_FF_BLK + c))
    bg = pl.BlockSpec((1, TC), lambda c: (0, c))
    bv = pl.BlockSpec((1, TC), lambda c: (0, N_FF_BLK + c))
    return ug, uv, wg, wv, bg, bv


def _conv_fwd(u, conv_w, conv_b):
    ug, uv, wg, wv, bg, bv = _conv_specs()

    def body(ug_ref, uv_ref, wg_ref, wv_ref, bg_ref, bv_ref, a_ref):
        gc = _conv(ug_ref[...], wg_ref[...], bg_ref[...])[0]
        vc = _conv(uv_ref[...], wv_ref[...], bv_ref[...])[0]
        a_ref[...] = (_gelu_parts(gc)[0] * vc).astype(BF16)

    return pl.pallas_call(
        body, name="conv_fwd", grid=(N_FF_BLK,), in_specs=[ug, uv, wg, wv, bg, bv], out_specs=ug,
        out_shape=jax.ShapeDtypeStruct((S, D_FF), BF16),
        compiler_params=_params(("parallel",)))(u, u, conv_w, conv_w, conv_b, conv_b)


def _conv_bwd(u, conv_w, conv_b, da):
    ug, uv, wg, wv, bg, bv = _conv_specs()

    def back(duc, u, s1, s2, w):
        du = w[2:3, :] * duc + w[1:2, :] * _shift_up(duc, 1) + w[0:1, :] * _shift_up(duc, 2)
        dw = _three_rows([jnp.sum(duc * s2, axis=0, keepdims=True), jnp.sum(duc * s1, axis=0, keepdims=True),
                          jnp.sum(duc * u, axis=0, keepdims=True)])
        return du, dw, jnp.sum(duc, axis=0, keepdims=True)

    def body(ug_ref, uv_ref, wg_ref, wv_ref, bg_ref, bv_ref, da_ref, dug_ref, duv_ref, dwg_ref, dwv_ref, dbg_ref, dbv_ref):
        u_g, u_v = ug_ref[...], uv_ref[...]
        gc, g1, g2 = _conv(u_g, wg_ref[...], bg_ref[...])
        vc, v1, v2 = _conv(u_v, wv_ref[...], bv_ref[...])
        gelu, dgelu = _gelu_parts(gc)
        da = da_ref[...]
        du, dw, db = back(da * vc * dgelu, u_g, g1, g2, wg_ref[...])
        dug_ref[...] = du.astype(BF16)
        dwg_ref[...] = dw
        dbg_ref[...] = db
        du, dw, db = back(da * gelu, u_v, v1, v2, wv_ref[...])
        duv_ref[...] = du.astype(BF16)
        dwv_ref[...] = dw
        dbv_ref[...] = db

    return pl.pallas_call(
        body, name="conv_bwd", grid=(N_FF_BLK,), in_specs=[ug, uv, wg, wv, bg, bv, ug],
        out_specs=[ug, ug, wg, wg, bg, bg],
        out_shape=[jax.ShapeDtypeStruct((S, D_FF), BF16), jax.ShapeDtypeStruct((S, D_FF), BF16),
                   jax.ShapeDtypeStruct((3, D_FF), F32), jax.ShapeDtypeStruct((3, D_FF), F32),
                   jax.ShapeDtypeStruct((1, D_FF), F32), jax.ShapeDtypeStruct((1, D_FF), F32)],
        compiler_params=_params(("parallel",)))(u, u, conv_w, conv_w, conv_b, conv_b, da)


def _adamw(name, w, g, m, v):
    shape = w.shape
    cols = shape[-1]
    flat = [t.reshape(-1, cols) for t in (w, g, m, v)]
    r = flat[0].shape[0]
    tr = min(r, max(8, 2 * 1024 * 1024 // (4 * cols)))

    def body(w_ref, g_ref, m_ref, v_ref, d_ref, mo_ref, vo_ref):
        g = g_ref[...]
        m = ADAM_B1 * m_ref[...] + (1.0 - ADAM_B1) * g
        v = ADAM_B2 * v_ref[...] + (1.0 - ADAM_B2) * (g * g)
        m_hat = m / (1.0 - ADAM_B1 ** ADAM_STEP)
        v_hat = v / (1.0 - ADAM_B2 ** ADAM_STEP)
        d_ref[...] = -ADAM_LR * (m_hat / (jnp.sqrt(v_hat) + ADAM_EPS) + ADAM_WD * w_ref[...])
        mo_ref[...] = m
        vo_ref[...] = v

    spec = pl.BlockSpec((tr, cols), lambda i: (i, 0))
    outs = pl.pallas_call(
        body, name=name, grid=(pl.cdiv(r, tr),), in_specs=[spec] * 4, out_specs=[spec] * 3,
        out_shape=[jax.ShapeDtypeStruct((r, cols), F32)] * 3, compiler_params=_params(("parallel",)))(*flat)
    return [t.reshape(shape) for t in outs]


def _place():
    x, y, c = lax.axis_index("x"), lax.axis_index("y"), lax.axis_index("c")
    chips = [(1 - x, y), (x, 1 - y), (1 - x, 1 - y)]
    return x, y, c, chips


def _scalars(*vals):
    return jnp.stack([jnp.asarray(v, jnp.int32) for v in vals])


HBM = pl.BlockSpec(memory_space=pltpu.HBM)
SEM = pl.BlockSpec(memory_space=pltpu.SEMAPHORE)
SPLIT_COPY = pltpu.CompilerParams(has_side_effects=pltpu.SideEffectType.DATAFLOW_SIDE_EFFECTING)


def _in_hbm(x):
    return pltpu.with_memory_space_constraint(x, pltpu.HBM)


def _cast_into_slot(name, w, layer, chip):
    _, k, n4 = w.shape
    tr = max(t for t in range(16, 513, 16) if k % t == 0)

    def body(chip_ref, w_ref, o_ref):
        o_ref[...] = w_ref[...].astype(BF16)

    return pl.pallas_call(
        body, name=name,
        grid_spec=pltpu.PrefetchScalarGridSpec(
            num_scalar_prefetch=1, grid=(k // tr,),
            in_specs=[pl.BlockSpec((None, tr, n4), lambda i, s: (layer, i, 0))],
            out_specs=pl.BlockSpec((None, tr, n4), lambda i, s: (s[0], i, 0))),
        out_shape=jax.ShapeDtypeStruct((N_CHIPS, k, n4), BF16),
        compiler_params=_params(("parallel",)))(_scalars(chip), w)


def _gather_copy(buf_ref, k, from_chip, send_sem, recv_sem, chips, c, half=False):
    rows = buf_ref.at[from_chip]
    if half:
        h = buf_ref.shape[1] // 2
        rows = buf_ref.at[from_chip, pl.ds(pl.multiple_of(c * h, h), h)]
    return pltpu.make_async_remote_copy(src_ref=rows, dst_ref=rows, send_sem=send_sem, recv_sem=recv_sem,
                                        device_id=(*chips[k], c), device_id_type=MESH)


def _gather_start(name, bufs, groups, halved=()):
    n, ng = len(bufs), len(groups)
    where = {a: (gi, e) for gi, g in enumerate(groups) for e, a in enumerate(g)}

    def body(*refs):
        ins, sems, token = refs[:n], refs[n:n + 2 * ng], refs[-1]
        x, y, c, chips = _place()
        for a in range(n):
            gi, e = where[a]
            for k in range(3):
                _gather_copy(ins[a], k, 2 * x + y, sems[2 * gi].at[3 * e + k], sems[2 * gi + 1].at[3 * e + k],
                             chips, c, a in halved).start()
        token[...] = jnp.zeros_like(token)

    out_shape = [pltpu.SemaphoreType.DMA((3 * len(g),)) for g in groups for _ in range(2)]
    out_shape += [pltpu.HBM(b.shape, b.dtype) for b in bufs] + [jax.ShapeDtypeStruct((8, 128), F32)]
    out = pl.pallas_call(
        body, name=name, in_specs=[HBM] * n,
        out_specs=[SEM] * (2 * ng) + [HBM] * n + [pl.BlockSpec(memory_space=pltpu.VMEM)], out_shape=out_shape,
        input_output_aliases={a: 2 * ng + a for a in range(n)}, compiler_params=SPLIT_COPY)(*[_in_hbm(b) for b in bufs])
    sems = [(out[2 * gi], out[2 * gi + 1]) for gi in range(ng)]
    return sems, list(out[2 * ng:2 * ng + n]), out[-1]


def _gather_wait(name, bufs, send, recv, after, halved=()):
    n = len(bufs)

    def body(*refs):
        ins, send_sem, recv_sem = refs[:n], refs[n], refs[n + 1]
        x, y, c, chips = _place()
        for e in range(n):
            for k in range(3):
                sems = (send_sem.at[3 * e + k], recv_sem.at[3 * e + k])
                _gather_copy(ins[e], k, 2 * x + y, *sems, chips, c, e in halved).wait_send()
                _gather_copy(ins[e], k, 2 * chips[k][0] + chips[k][1], *sems, chips, c, e in halved).wait_recv()

    return pl.pallas_call(
        body, name=name, in_specs=[HBM] * n + [SEM, SEM, ANY], out_specs=[HBM] * n,
        out_shape=[pltpu.HBM(b.shape, b.dtype) for b in bufs],
        input_output_aliases={a: a for a in range(n)}, compiler_params=SPLIT_COPY)(*bufs, send, recv, after)


def _swap_halves(name, bufs):
    n = len(bufs)

    def body(*refs):
        ins, outs = refs[:n], refs[n:2 * n]
        send_sem, recv_sem = refs[2 * n:]
        x, y, c, chips = _place()

        def piece(ref, k, which):
            h = ref.shape[1] // 2
            return ref.at[2 * chips[k][0] + chips[k][1], pl.ds(pl.multiple_of(which * h, h), h)]

        def copy(a, k, which):
            return pltpu.make_async_remote_copy(
                src_ref=piece(ins[a], k, c), dst_ref=piece(outs[a], k, which), send_sem=send_sem.at[3 * a + k],
                recv_sem=recv_sem.at[3 * a + k], device_id=(x, y, 1 - c), device_id_type=MESH)

        for a in range(n):
            for k in range(3):
                copy(a, k, c).start()
        for a in range(n):
            for k in range(3):
                copy(a, k, c).wait_send()
                copy(a, k, 1 - c).wait_recv()

    return pl.pallas_call(
        body, name=name, in_specs=[ANY] * n, out_specs=[ANY] * n,
        out_shape=[jax.ShapeDtypeStruct(b.shape, b.dtype) for b in bufs],
        input_output_aliases={a: a for a in range(n)},
        scratch_shapes=[pltpu.SemaphoreType.DMA((3 * n,)), pltpu.SemaphoreType.DMA((3 * n,))],
    )(*bufs)


def _reduce_copy(g_ref, land_ref, mask, send_sem, recv_sem, x, y, c, sending):
    px, py, pc = x ^ ((mask >> 2) & 1), y ^ ((mask >> 1) & 1), c ^ (mask & 1)
    half = g_ref.shape[1] // 2
    src = g_ref.at[2 * px + py, pl.ds(pl.multiple_of(pc * half, half), half)]
    dst = land_ref.at[4 * x + 2 * y + c] if sending else land_ref.at[4 * px + 2 * py + pc]
    return pltpu.make_async_remote_copy(src_ref=src, dst_ref=dst, send_sem=send_sem, recv_sem=recv_sem,
                                        device_id=(px, py, pc), device_id_type=MESH)


def _reduce_start(name, grads):
    n = len(grads)
    lands = [lax.empty((N_DEV, g.shape[1] // 2, g.shape[2]), g.dtype) for g in grads]

    def body(*refs):
        gs, ls, send_sem, recv_sem = refs[:n], refs[n:2 * n], refs[2 * n], refs[2 * n + 1]
        x, y, c, _ = _place()
        for a in range(n):
            for mask in range(1, N_DEV):
                s = (N_DEV - 1) * a + mask - 1
                _reduce_copy(gs[a], ls[a], mask, send_sem.at[s], recv_sem.at[s], x, y, c, True).start()
        refs[-1][...] = jnp.zeros_like(refs[-1])

    sem = pltpu.SemaphoreType.DMA((n * (N_DEV - 1),))
    out = pl.pallas_call(
        body, name=name, in_specs=[HBM] * (2 * n),
        out_specs=[SEM, SEM] + [HBM] * (2 * n) + [pl.BlockSpec(memory_space=pltpu.VMEM)],
        out_shape=[sem, sem] + [pltpu.HBM(t.shape, t.dtype) for t in grads + lands] + [jax.ShapeDtypeStruct((8, 128), F32)],
        input_output_aliases={a: 2 + a for a in range(2 * n)}, compiler_params=SPLIT_COPY)(
            *[_in_hbm(t) for t in grads + lands])
    return out[0], out[1], list(out[2:2 + n]), list(out[2 + n:2 + 2 * n]), out[-1]


def _reduce_wait(name, send, recv, grads, lands, after):
    n = len(grads)

    def body(*refs):
        gs, ls, send_sem, recv_sem = refs[:n], refs[n:2 * n], refs[2 * n], refs[2 * n + 1]
        x, y, c, _ = _place()
        for a in range(n):
            for mask in range(1, N_DEV):
                s = (N_DEV - 1) * a + mask - 1
                sems = (send_sem.at[s], recv_sem.at[s])
                _reduce_copy(gs[a], ls[a], mask, *sems, x, y, c, True).wait_send()
                _reduce_copy(gs[a], ls[a], mask, *sems, x, y, c, False).wait_recv()

    out = pl.pallas_call(
        body, name=name, in_specs=[HBM] * (2 * n) + [SEM, SEM, ANY], out_specs=[HBM] * (2 * n),
        out_shape=[pltpu.HBM(t.shape, t.dtype) for t in grads + lands],
        input_output_aliases={a: a for a in range(2 * n)}, compiler_params=SPLIT_COPY)(*grads, *lands, send, recv, after)
    return list(out[:n]), list(out[n:])


def _reduce_sum(name, g, land, layer, into, chip, c):
    _, k4, n4 = g.shape
    half = k4 // 2
    tr = max(t for t in range(16, 513, 16) if half % t == 0)
    per = half // tr
    me = 2 * chip + c

    def body(s_ref, own_ref, *refs):
        total = own_ref[...].astype(F32)
        for ref in refs[:N_DEV - 1]:
            total = total + ref[...].astype(F32)
        refs[-1][...] = total

    in_specs = [pl.BlockSpec((None, tr, n4), lambda i, s: (s[0], s[1] * per + i, 0))]
    in_specs += [pl.BlockSpec((None, tr, n4), functools.partial(lambda i, s, m: (s[1 + m], i, 0), m=m))
                 for m in range(1, N_DEV)]
    ins = [g] + [land] * (N_DEV - 1)
    aliases = {}
    if into is not None:
        in_specs, ins, aliases = in_specs + [ANY], ins + [into], {1 + N_DEV: 0}
    return pl.pallas_call(
        body, name=name,
        grid_spec=pltpu.PrefetchScalarGridSpec(
            num_scalar_prefetch=1, grid=(per,), in_specs=in_specs,
            out_specs=pl.BlockSpec((None, tr, n4), lambda i, s: (layer, s[1] * per + i, 0))),
        out_shape=jax.ShapeDtypeStruct((DEPTH, k4, n4), F32), input_output_aliases=aliases,
        compiler_params=_params(("parallel",)))(_scalars(chip, c, *[me ^ m for m in range(1, N_DEV)]), *ins)


def _join_halves(name, bufs):
    n = len(bufs)

    def body(*refs):
        ins, outs = refs[:n], refs[n:2 * n]
        send_sem, recv_sem = refs[2 * n:]
        x, y, c, _ = _place()

        def rows(ref, which):
            half = ref.shape[1] // 2
            return ref.at[:, pl.ds(pl.multiple_of(which * half, half), half)]

        sends = [pltpu.make_async_remote_copy(
            src_ref=rows(ins[a], c), dst_ref=rows(outs[a], c), send_sem=send_sem.at[a], recv_sem=recv_sem.at[a],
            device_id=(x, y, 1 - c), device_id_type=MESH) for a in range(n)]
        for cp in sends:
            cp.start()
        for a in range(n):
            sends[a].wait_send()
            pltpu.make_async_remote_copy(
                src_ref=rows(ins[a], c), dst_ref=rows(outs[a], 1 - c), send_sem=send_sem.at[a], recv_sem=recv_sem.at[a],
                device_id=(x, y, 1 - c), device_id_type=MESH).wait_recv()

    return pl.pallas_call(
        body, name=name, in_specs=[ANY] * n, out_specs=[ANY] * n,
        out_shape=[jax.ShapeDtypeStruct(b.shape, b.dtype) for b in bufs],
        input_output_aliases={a: a for a in range(n)},
        scratch_shapes=[pltpu.SemaphoreType.DMA((n,)), pltpu.SemaphoreType.DMA((n,))],
    )(*bufs)


def _all_reduce_small(block):
    r = block.shape[0]

    def body(x_ref, out_ref, slots, send_sem, recv_sem):
        x, y, c, _ = _place()
        me = 4 * x + 2 * y + c
        slots[me] = x_ref[...]
        sends = []
        for mask in range(1, N_DEV):
            fx, fy, fc = (mask >> 2) & 1, (mask >> 1) & 1, mask & 1
            peer = (x ^ fx, y ^ fy, c ^ fc)
            cp = pltpu.make_async_remote_copy(
                src_ref=x_ref, dst_ref=slots.at[me], send_sem=send_sem.at[mask - 1], recv_sem=recv_sem.at[mask - 1],
                device_id=peer, device_id_type=MESH)
            cp.start()
            sends.append(cp)
        for mask in range(1, N_DEV):
            src = me ^ mask
            pltpu.make_async_remote_copy(
                src_ref=x_ref, dst_ref=slots.at[src], send_sem=send_sem.at[mask - 1], recv_sem=recv_sem.at[mask - 1],
                device_id=(x, y, c), device_id_type=MESH).wait_recv()
        for cp in sends:
            cp.wait_send()
        total = slots[0]
        for d in range(1, N_DEV):
            total = total + slots[d]
        out_ref[...] = total

    vmem = pl.BlockSpec(memory_space=pltpu.VMEM)
    return pl.pallas_call(
        body, name="all_reduce_small", in_specs=[vmem], out_specs=vmem,
        out_shape=jax.ShapeDtypeStruct((r, 128), F32),
        scratch_shapes=[pltpu.VMEM((N_DEV, r, 128), F32), pltpu.SemaphoreType.DMA((N_DEV - 1,)),
                        pltpu.SemaphoreType.DMA((N_DEV - 1,))],
        compiler_params=pltpu.CompilerParams(vmem_limit_bytes=VMEM_LIMIT))(block)


B_Q_COL = 2304 // 128
B_K0, B_V0, B_END = 2816, 2944, 3072


def _full_cols(w_g):
    return w_g.transpose(1, 0, 2).reshape(w_g.shape[1], -1)


def _group_src(proj, g):
    return ((proj, 2 * g), (proj, 6 + 2 * g), (proj, 12 + 2 * g))


def _kv_expand(kv):
    return jnp.broadcast_to(kv.reshape(S, 2, 1, HD), (S, 2, 4, HD)).reshape(S, 8 * HD)


def _kv_reduce(dkv):
    return dkv.reshape(S, 2, 4, HD).sum(axis=2).reshape(S, 2 * HD)


def _mixer_fwd(h1, wget, rel_bias, sinks_l, bidx):
    w = dict(wget(0, h1))
    proj = _mm_nt("proj_in", h1, w["w_in"], F32, tm=S, tn=1152)
    no_sinks = jnp.full((4,), NEG, F32)
    srcs = [_group_src(proj, g) for g in range(3)]
    o_g, lse_g = [], []
    for g, (_, d) in enumerate(A_GROUPS):
        o, lse = _band_fwd("band_fwd_g%d" % g, d, 2, BLK, 4 * g, srcs[g], bidx[g], rel_bias, no_sinks)
        o_g.append(o)
        lse_g.append(lse)
    o_a32, o_a, lse_a = _comb_fwd(o_g, lse_g)
    src_b = ((proj, B_Q_COL), (_kv_expand(proj[:, B_K0:B_V0]), 0), (_kv_expand(proj[:, B_V0:B_END]), 0))
    o_b32, lse_b = _band_fwd("band_fwd_b", 1, 4, BLK - 1, N_A, src_b, bidx[3], rel_bias, sinks_l)
    o_b = o_b32.astype(BF16)
    o_c32, o_c, tot_c = _sb_fwd(proj)
    w.update(wget(1, o_c32))
    br = [_mm_nn("branch_a", o_a, w["w_br_a"], F32, tm=S), _mm_nn("branch_b", o_b, w["w_br_b"], F32, tm=S),
          _mm_nn("branch_c", o_c, w["w_br_c"], F32, tm=S)]
    merged = _gate_fwd(proj, w["b_gate"], br)
    mo = _mm_nn("out_proj", merged, w["w_out"], F32, tm=S)
    saved = dict(proj=proj, srcs=srcs, src_b=src_b, o_a32=o_a32, lse_a=lse_a, o_b32=o_b32, lse_b=lse_b, tot_c=tot_c,
                 o_a=o_a, o_b=o_b, o_c=o_c, br=br, merged=merged)
    return mo, saved, w


def _mixer_bwd(d_mo, h1, w, sv, rel_bias, sinks_l, bidx, stats_in, emit):
    grads = {}
    dmerged = _mm_nt("out_proj_dx", d_mo, w["w_out"], F32, tm=S)
    grads["w_out"] = _mm_tn_sharded("out_proj_dw", sv["merged"], d_mo, True)
    e, dgate, db_gate = _gate_bwd(sv["proj"], w["b_gate"], sv["br"], dmerged)
    grads["b_gate"] = db_gate
    d_o = {}
    for n, name in enumerate("abc"):
        d_o[name] = _mm_nt("branch_%s_dx" % name, e[n], w["w_br_" + name], F32, tm=S)
        grads["w_br_" + name] = _mm_tn_sharded("branch_%s_dw" % name, sv["o_" + name], e[n], False)
    zero = emit(1, grads)
    no_sinks = jnp.full((4,), NEG, F32) + zero[0]
    dqs, dks, dvs, stats = [], [], [], []
    for g, (_, d) in enumerate(A_GROUPS):
        dq, dk, dv, st = _band_bwd("band_bwd_g%d" % g, d, 2, BLK, 4 * g, sv["srcs"][g], bidx[g], rel_bias, no_sinks,
                                   sv["o_a32"], sv["lse_a"], d_o["a"], stats_in[4 * g:4 * g + 4])
        dqs.append(dq)
        dks.append(dk)
        dvs.append(dv)
        stats.append(st)
    dq_b, dk_x, dv_x, st = _band_bwd("band_bwd_b", 1, 4, BLK - 1, N_A, sv["src_b"], bidx[3], rel_bias, sinks_l,
                                     sv["o_b32"], sv["lse_b"], d_o["b"], stats_in[N_A:])
    stats = jnp.concatenate(stats + [st], axis=0)
    dcq, dck, dcv = _sb_bwd(sv["proj"], sv["tot_c"], d_o["c"])
    cols = dqs + dks + dvs + [dq_b, _kv_reduce(dk_x), _kv_reduce(dv_x), dcq, dck, dcv]
    dproj = jnp.concatenate([t.astype(BF16) for t in cols] + list(dgate), axis=1)
    grads["w_in"] = _mm_tn("proj_in_dw", dproj, h1, BF16, tm=1152, tn=1024).reshape(N_CHIPS, IN_SHARD, D)
    zero = emit(2, grads)
    dh1 = _mm_nn("proj_in_dx", dproj, w["w_in"], F32, tm=S, tk=2304)
    return dh1, grads, stats, zero


def _ffn_fwd(h2, w):
    u = _mm_nn("ffn_up", h2, w["w_up"], F32, tm=S, tn=1024)
    a = _conv_fwd(u, w["conv_w"], w["conv_b"])
    dn = _mm_nn("ffn_down", a, w["w_down"], F32, tm=1024)
    return dn, dict(u=u, a=a)


def _ffn_bwd(d_dn, h2, w, sv):
    grads = {}
    da = _mm_nt("ffn_down_dx", d_dn, w["w_down"], F32, tm=S, tn=1024)
    grads["w_down"] = _mm_tn_sharded("ffn_down_dw", sv["a"], d_dn, True, tm=1024, tn=1024)
    dug, duv, dwg, dwv, dbg, dbv = _conv_bwd(sv["u"], w["conv_w"], w["conv_b"], da)
    du = jnp.concatenate([dug, duv], axis=1)
    grads["conv_w"] = jnp.concatenate([dwg, dwv], axis=1)
    grads["conv_b"] = jnp.concatenate([dbg, dbv], axis=1)
    dh2 = _mm_nt("ffn_up_dx", du, w["w_up"], F32, tm=S, tk=2048)
    grads["w_up"] = _mm_tn_sharded("ffn_up_dw", h2, du, False, tm=1024, tn=1024)
    return dh2, grads


BIG = ("w_in", "w_br_a", "w_br_b", "w_br_c", "w_out", "w_up", "w_down")


def _shard_view(name, w):
    return jnp.swapaxes(w, 1, 2) if name == "w_in" else w
WEIGHT_GROUPS = (("w_in", "b_gate"), ("w_br_a", "w_br_b", "w_br_c", "w_out"), ("w_up", "conv_w", "w_down"))
GRAD_GROUPS = (("w_down", "w_up"), ("w_out", "w_br_a", "w_br_b", "w_br_c"), ("w_in",))
SMALL_ROWS = (("rel_bias", 8), ("attn_pre_norm", 16), ("attn_post_norm", 16), ("ffn_pre_norm", 16), ("ffn_post_norm", 16),
              ("sinks", 8), ("conv_b", 128), ("b_gate", 48), ("conv_w", 384), ("loss", 8))


def _pack_small(vals):
    rows = []
    for name, n in SMALL_ROWS:
        flat = vals[name].reshape(-1).astype(F32)
        rows.append(jnp.pad(flat, (0, n * 128 - flat.shape[0])).reshape(n, 128))
    return jnp.concatenate(rows, axis=0)


def _unpack_small(block, shapes):
    out, row = {}, 0
    for name, n in SMALL_ROWS:
        size = int(np.prod(shapes[name]))
        out[name] = block[row:row + n].reshape(-1)[:size].reshape(shapes[name])
        row += n
    return out


def kernel(x, rel_bias, attn_pre_norm, w_in, b_gate, sinks, w_br_a, w_br_b, w_br_c, w_out, attn_post_norm, ffn_pre_norm, w_up, conv_w, conv_b, w_down, ffn_post_norm, loss_target, m_rel_bias, m_attn_pre_norm, m_w_in, m_b_gate, m_sinks, m_w_br_a, m_w_br_b, m_w_br_c, m_w_out, m_attn_post_norm, m_ffn_pre_norm, m_w_up, m_conv_w, m_conv_b, m_w_down, m_ffn_post_norm, v_rel_bias, v_attn_pre_norm, v_w_in, v_b_gate, v_sinks, v_w_br_a, v_w_br_b, v_w_br_c, v_w_out, v_attn_post_norm, v_ffn_pre_norm, v_w_up, v_conv_w, v_conv_b, v_w_down, v_ffn_post_norm):
    names = ("rel_bias", "attn_pre_norm", "w_in", "b_gate", "sinks", "w_br_a", "w_br_b", "w_br_c", "w_out",
             "attn_post_norm", "ffn_pre_norm", "w_up", "conv_w", "conv_b", "w_down", "ffn_post_norm")
    weights = dict(zip(names, (rel_bias, attn_pre_norm, w_in, b_gate, sinks, w_br_a, w_br_b, w_br_c, w_out,
                               attn_post_norm, ffn_pre_norm, w_up, conv_w, conv_b, w_down, ffn_post_norm)))
    mom1 = dict(zip(names, (m_rel_bias, m_attn_pre_norm, m_w_in, m_b_gate, m_sinks, m_w_br_a, m_w_br_b, m_w_br_c,
                            m_w_out, m_attn_post_norm, m_ffn_pre_norm, m_w_up, m_conv_w, m_conv_b, m_w_down,
                            m_ffn_post_norm)))
    mom2 = dict(zip(names, (v_rel_bias, v_attn_pre_norm, v_w_in, v_b_gate, v_sinks, v_w_br_a, v_w_br_b, v_w_br_c,
                            v_w_out, v_attn_post_norm, v_ffn_pre_norm, v_w_up, v_conv_w, v_conv_b, v_w_down,
                            v_ffn_post_norm)))

    chip = 2 * lax.axis_index("x") + lax.axis_index("y")
    core = lax.axis_index("c")

    keys = [(n, l) for l in range(DEPTH) for group in WEIGHT_GROUPS for n in group]
    groups = [[keys.index((n, l)) for n in group] for l in range(DEPTH) for group in WEIGHT_GROUPS]

    def slot_buffer(n, l):
        if n in BIG:
            return _cast_into_slot("cast_" + n, _shard_view(n, weights[n]), l, chip)
        shard = weights[n][l]
        return lax.dynamic_update_slice(jnp.zeros((N_CHIPS,) + shard.shape, F32), shard[None],
                                        (chip, jnp.int32(0), jnp.int32(0)))

    by_halves = [keys.index(k) for k in (("w_in", 0), ("w_up", DEPTH - 1), ("w_down", DEPTH - 1))]
    n_first = len(groups[0])
    sems, in_flight, _ = _gather_start("gather_start_first", [slot_buffer(*k) for k in keys[:n_first]], groups[:1],
                                       tuple(a for a in by_halves if a < n_first))
    more = _gather_start("gather_start", [slot_buffer(*k) for k in keys[n_first:]],
                         [[a - n_first for a in g] for g in groups[1:]],
                         tuple(a - n_first for a in by_halves if a >= n_first))
    sems, in_flight, started = sems + more[0], in_flight + more[1], more[2]

    def wget(l, gi, after):
        g = l * len(WEIGHT_GROUPS) + gi
        after = started if g == 0 else after
        halved = tuple(e for e, a in enumerate(groups[g]) if a in by_halves)
        got = list(_gather_wait("gather_wait_%d_%d" % (l, gi), [in_flight[a] for a in groups[g]], *sems[g], after,
                                halved))
        if halved:
            for e, buf in zip(halved, _swap_halves("swap_halves_%d_%d" % (l, gi), [got[e] for e in halved])):
                got[e] = buf
        out = {}
        for n, buf in zip(WEIGHT_GROUPS[gi], got):
            out[n] = buf.reshape(-1, buf.shape[-1]) if n in ("w_in", "w_out", "w_down") else _full_cols(buf)
        if gi == len(WEIGHT_GROUPS) - 1:
            out["conv_b"] = conv_b[l:l + 1]
        return out

    pending = []

    def emit(l, gi, grads):
        group = GRAD_GROUPS[gi]
        *started, token = _reduce_start("reduce_start_%d_%d" % (l, gi), [grads[n] for n in group])
        pending.append((l, group) + tuple(started))
        return token[:1, :1]

    local = _local_step(x.reshape(S, D), loss_target.reshape(S, D), wget, emit, rel_bias, sinks, attn_pre_norm,
                        attn_post_norm, ffn_pre_norm, ffn_post_norm)
    return _reduce_and_update(x.shape, names, weights, mom1, mom2, chip, core, pending, *local)


def _local_step(xs, target, wget, emit, rel_bias, sinks, attn_pre_norm, attn_post_norm, ffn_pre_norm, ffn_post_norm):
    bidx = jnp.asarray(_bucket_maps())

    saved, layers = [], []
    h1 = _rms_fwd("pre_norm_first", xs, attn_pre_norm[0:1])
    x_in = xs
    for l in range(DEPTH):
        mo, sv_mix, w = _mixer_fwd(h1, functools.partial(wget, l), rel_bias, sinks[l], bidx)
        x_mid, h2 = _post_pre_fwd("post_attn_norm", x_in, mo, attn_post_norm[l:l + 1], ffn_pre_norm[l:l + 1])
        w.update(wget(l, 2, h2))
        dn, sv_ffn = _ffn_fwd(h2, w)
        g_next = attn_pre_norm[l + 1:l + 2] if l + 1 < DEPTH else None
        x_out, h1_next = _post_pre_fwd("post_ffn_norm" if l + 1 < DEPTH else "post_ffn_norm_last", x_mid, dn,
                                       ffn_post_norm[l:l + 1], g_next)
        saved.append(dict(x_in=x_in, h1=h1, mo=mo, x_mid=x_mid, h2=h2, dn=dn, mix=sv_mix, ffn=sv_ffn))
        layers.append(w)
        x_in, h1 = x_out, h1_next

    loss_row, dres = _loss_kernel(x_in, target)

    small = [None] * DEPTH
    stats = jnp.zeros((N_BAND_Q, 8, 128), F32)
    dh_next = None
    for l in reversed(range(DEPTH)):
        w, sv = layers[l], saved[l]
        if l + 1 < DEPTH:
            pre = (saved[l + 1]["x_in"], attn_pre_norm[l + 1:l + 2] + zero, dh_next)
            dres, d_dn, dg_pre_next, dg_fpost = _norm_bwd("post_ffn_norm_bwd", dres, pre,
                                                          (sv["dn"], ffn_post_norm[l:l + 1]))
            small[l + 1]["attn_pre_norm"] = dg_pre_next
        else:
            dres, d_dn, _, dg_fpost = _norm_bwd("post_ffn_norm_last_bwd", dres, None, (sv["dn"], ffn_post_norm[l:l + 1]))
        dh2, g_ffn = _ffn_bwd(d_dn, sv["h2"], w, sv["ffn"])
        zero = emit(l, 0, g_ffn)
        dres, d_mo, dg_fpre, dg_apost = _norm_bwd("post_attn_norm_bwd", dres,
                                                  (sv["x_mid"], ffn_pre_norm[l:l + 1] + zero, dh2),
                                                  (sv["mo"], attn_post_norm[l:l + 1]))
        dh_next, g_mix, stats, zero = _mixer_bwd(d_mo, sv["h1"], w, sv["mix"], rel_bias, sinks[l], bidx, stats,
                                                 functools.partial(emit, l))
        small[l] = dict(ffn_post_norm=dg_fpost, ffn_pre_norm=dg_fpre, attn_post_norm=dg_apost,
                        sinks=stats[N_A:, 1, 0], conv_b=g_ffn["conv_b"], b_gate=g_mix["b_gate"], conv_w=g_ffn["conv_w"])
    grad_x, _, dg_pre0, _ = _norm_bwd("pre_norm_first_bwd", dres, (saved[0]["x_in"], attn_pre_norm[0:1] + zero, dh_next),
                                      None)
    small[0]["attn_pre_norm"] = dg_pre0
    return loss_row, grad_x, small, stats


def _reduce_and_update(x_shape, names, weights, mom1, mom2, chip, core, pending, loss_row, grad_x, small, stats):
    delta, new_m, new_v, grads = {}, {}, {}, {}

    def update(n, g):
        grads[n] = g
        delta[n], new_m[n], new_v[n] = _adamw("adamw_" + n, _shard_view(n, weights[n]), g,
                                              _shard_view(n, mom1[n]), _shard_view(n, mom2[n]))

    summed = {}

    def finish(which, after):
        for l, group, send, recv, gs, lands in pending:
            if (group == ("w_in",)) == which:
                gs, lands = _reduce_wait("reduce_wait_%d_%s" % (l, group[0]), send, recv, gs, lands, after)
                for n, g, land in zip(group, gs, lands):
                    summed[n] = _reduce_sum("reduce_sum_%d_%s" % (l, n), g, land, l, summed.get(n), chip, core)

    finish(False, grad_x)
    early = [n for n in BIG if n != "w_in"]
    for n, g in zip(early, _join_halves("join_halves", [summed[n] for n in early])):
        update(n, g)
    finish(True, delta[early[-1]])
    update("w_in", _join_halves("join_halves_w_in", [summed["w_in"]])[0])
    for out in (grads, delta, new_m, new_v):
        out["w_in"] = _shard_view("w_in", out["w_in"])

    small_vals = {n: jnp.stack([small[l][n].reshape(weights[n].shape[1:]) for l in range(DEPTH)])
                  for n in ("attn_pre_norm", "attn_post_norm", "ffn_pre_norm", "ffn_post_norm", "conv_b", "sinks")}
    small_vals["b_gate"] = jnp.stack([small[l]["b_gate"] for l in range(DEPTH)])
    small_vals["conv_w"] = jnp.stack([small[l]["conv_w"] for l in range(DEPTH)])
    small_vals["rel_bias"] = stats[:, 0, :NUM_BUCKETS].T
    small_vals["loss"] = loss_row[0, :1]
    shapes = {n: v.shape for n, v in small_vals.items()}
    packed, delta["w_in"] = lax.optimization_barrier((_pack_small(small_vals), delta["w_in"]))
    reduced = _unpack_small(_all_reduce_small(packed), shapes)
    reduced["b_gate"] = lax.dynamic_slice_in_dim(reduced["b_gate"], chip * (D // N_CHIPS), D // N_CHIPS, axis=2)
    reduced["conv_w"] = lax.dynamic_slice_in_dim(reduced["conv_w"], chip * (2 * D_FF // N_CHIPS), 2 * D_FF // N_CHIPS, axis=2)
    for n in names:
        if n not in grads:
            update(n, reduced[n].reshape(weights[n].shape))

    loss = reduced["loss"].reshape(())
    return (loss, grad_x.reshape(x_shape), *[grads[n] for n in names], *[delta[n] for n in names],
            *[new_m[n] for n in names], *[new_v[n] for n in names])
```

```python
import functools
import math

import numpy as np
import jax
import jax.numpy as jnp
from jax import lax
from jax.experimental import pallas as pl
from jax.experimental.pallas import tpu as pltpu

F32 = jnp.float32
BF16 = jnp.bfloat16

S = 2048
D = 1024
DEPTH = 2
HD = 64
BLK = 128
NQB = S // BLK
A_GROUPS = ((128, 1), (512, 4), (2048, 16))
N_BAND_Q = 20
N_A = 12
NUM_BUCKETS = 32
MAX_DISTANCE = 2048
D_FF = 4096
IN_COLS = 6912
IN_SHARD = IN_COLS // 4
OFF_GATE = 3840
EPS = 1e-6
SCALE = HD ** -0.5
NEG = -1e30
N_CHIPS = 4
N_DEV = 8

ADAM_LR = 0.001
ADAM_B1 = 0.9
ADAM_B2 = 0.999
ADAM_EPS = 1e-08
ADAM_WD = 0.01
ADAM_STEP = 10

VMEM_LIMIT = 56 * 1024 * 1024

NN = (((1,), (0,)), ((), ()))
NT = (((1,), (1,)), ((), ()))
TN = (((0,), (0,)), ((), ()))

MESH = pl.DeviceIdType.MESH
ANY = pl.BlockSpec(memory_space=pl.ANY)


def _dot(a, b, dims):
    return lax.dot_general(a, b, dims, preferred_element_type=F32)


def _params(sem):
    return pltpu.CompilerParams(dimension_semantics=sem, vmem_limit_bytes=VMEM_LIMIT)


def _matmul(name, a, b, out_shape, out_dtype, grid, a_spec, b_spec, o_spec, dims, acc_shape):
    nk = grid[-1]

    def body(a_ref, b_ref, o_ref, *scratch):
        part = _dot(a_ref[...].astype(BF16), b_ref[...].astype(BF16), dims)
        if nk == 1:
            o_ref[...] = part.astype(o_ref.dtype)
            return
        acc_ref, = scratch
        k = pl.program_id(len(grid) - 1)

        @pl.when(k == 0)
        def _():
            acc_ref[...] = part

        @pl.when(k > 0)
        def _():
            acc_ref[...] += part

        @pl.when(k == nk - 1)
        def _():
            o_ref[...] = acc_ref[...].astype(o_ref.dtype)

    scratch = [] if nk == 1 else [pltpu.VMEM(acc_shape, F32)]
    sem = ("parallel",) * (len(grid) - 1) + ("arbitrary",)
    return pl.pallas_call(
        body, name=name, grid=grid, in_specs=[a_spec, b_spec], out_specs=o_spec,
        out_shape=jax.ShapeDtypeStruct(out_shape, out_dtype), scratch_shapes=scratch,
        compiler_params=_params(sem))(a, b)


FULL_K = 8192


def _mm_tn_sharded(name, a, b, row_sharded, tm=512, tn=512, tk=FULL_K):
    k, m = a.shape
    n = b.shape[1]
    m4, n4 = (m // N_CHIPS, n) if row_sharded else (m, n // N_CHIPS)
    tm, tn, tk = min(tm, m4), min(tn, n4), min(tk, k)
    per_m, per_n = m4 // tm, n4 // tn
    if row_sharded:
        o_map = lambda i, j, l: (i // per_m, i % per_m, j)
    else:
        o_map = lambda i, j, l: (j // per_n, i, j % per_n)
    return _matmul(name, a, b, (N_CHIPS, m4, n4), BF16, (m // tm, n // tn, k // tk),
                   pl.BlockSpec((tk, tm), lambda i, j, l: (l, i)),
                   pl.BlockSpec((tk, tn), lambda i, j, l: (l, j)),
                   pl.BlockSpec((None, tm, tn), o_map), TN, (tm, tn))


def _mm_nn(name, a, b, out_dtype, tm=512, tn=512, tk=FULL_K):
    m, k = a.shape
    n = b.shape[1]
    tm, tn, tk = min(tm, m), min(tn, n), min(tk, k)
    return _matmul(name, a, b, (m, n), out_dtype, (m // tm, n // tn, k // tk),
                   pl.BlockSpec((tm, tk), lambda i, j, l: (i, l)),
                   pl.BlockSpec((tk, tn), lambda i, j, l: (l, j)),
                   pl.BlockSpec((tm, tn), lambda i, j, l: (i, j)), NN, (tm, tn))


def _mm_nt(name, a, b, out_dtype, tm=512, tn=512, tk=FULL_K):
    m, k = a.shape
    n = b.shape[0]
    tm, tn, tk = min(tm, m), min(tn, n), min(tk, k)
    return _matmul(name, a, b, (m, n), out_dtype, (m // tm, n // tn, k // tk),
                   pl.BlockSpec((tm, tk), lambda i, j, l: (i, l)),
                   pl.BlockSpec((tn, tk), lambda i, j, l: (j, l)),
                   pl.BlockSpec((tm, tn), lambda i, j, l: (i, j)), NT, (tm, tn))


def _mm_tn(name, a, b, out_dtype, tm=512, tn=512, tk=FULL_K):
    k, m = a.shape
    n = b.shape[1]
    tm, tn, tk = min(tm, m), min(tn, n), min(tk, k)
    return _matmul(name, a, b, (m, n), out_dtype, (m // tm, n // tn, k // tk),
                   pl.BlockSpec((tk, tm), lambda i, j, l: (l, i)),
                   pl.BlockSpec((tk, tn), lambda i, j, l: (l, j)),
                   pl.BlockSpec((tm, tn), lambda i, j, l: (i, j)), TN, (tm, tn))


TR = 512


def _row_spec(width=D):
    return pl.BlockSpec((TR, width), lambda i: (i, 0))


def _vec_spec(width=D):
    return pl.BlockSpec((1, width), lambda i: (0, 0))


def _rms(x, g):
    r = lax.rsqrt(jnp.mean(x * x, axis=-1, keepdims=True) + EPS)
    return x * r * g


def _rms_fwd(name, x, g):
    def body(x_ref, g_ref, h_ref):
        h_ref[...] = _rms(x_ref[...], g_ref[...]).astype(BF16)

    return pl.pallas_call(
        body, name=name, grid=(S // TR,), in_specs=[_row_spec(), _vec_spec()], out_specs=_row_spec(),
        out_shape=jax.ShapeDtypeStruct((S, D), BF16), compiler_params=_params(("parallel",)))(x, g)


def _post_pre_fwd(name, x, y, g_post, g_pre):
    has_pre = g_pre is not None

    def body(*refs):
        if has_pre:
            x_ref, y_ref, gp_ref, gn_ref, xn_ref, h_ref = refs
        else:
            x_ref, y_ref, gp_ref, xn_ref = refs
        xn = x_ref[...] + _rms(y_ref[...], gp_ref[...])
        xn_ref[...] = xn
        if has_pre:
            h_ref[...] = _rms(xn, gn_ref[...]).astype(BF16)

    ins = [x, y, g_post] + ([g_pre] if has_pre else [])
    in_specs = [_row_spec(), _row_spec(), _vec_spec()] + ([_vec_spec()] if has_pre else [])
    out_shape = [jax.ShapeDtypeStruct((S, D), F32)] + ([jax.ShapeDtypeStruct((S, D), BF16)] if has_pre else [])
    out_specs = [_row_spec()] + ([_row_spec()] if has_pre else [])
    out = pl.pallas_call(
        body, name=name, grid=(S // TR,), in_specs=in_specs, out_specs=out_specs, out_shape=out_shape,
        compiler_params=_params(("parallel",)))(*ins)
    return out if has_pre else (out[0], None)


def _rms_bwd_math(x, g, dy):
    r = lax.rsqrt(jnp.mean(x * x, axis=-1, keepdims=True) + EPS)
    n = x * r
    dn = dy * g
    dx = r * (dn - n * jnp.mean(dn * n, axis=-1, keepdims=True))
    return dx, jnp.sum(dy * n, axis=0, keepdims=True)


def _norm_bwd(name, dres, pre=None, post=None):
    has_pre, has_post = pre is not None, post is not None

    def body(*refs):
        refs = list(refs)
        dres_ref = refs.pop(0)
        if has_pre:
            xn_ref, gn_ref, dh_ref = refs[:3]
            refs = refs[3:]
        if has_post:
            y_ref, gp_ref = refs[:2]
            refs = refs[2:]
        dxn_ref = refs.pop(0)
        dy_ref = refs.pop(0) if has_post else None
        dgn_ref = refs.pop(0) if has_pre else None
        dgp_ref = refs.pop(0) if has_post else None
        first = pl.program_id(0) == 0
        dxn = dres_ref[...]
        if has_pre:
            dx, dg = _rms_bwd_math(xn_ref[...], gn_ref[...], dh_ref[...])
            dxn = dxn + dx

            @pl.when(first)
            def _():
                dgn_ref[...] = dg

            @pl.when(jnp.logical_not(first))
            def _():
                dgn_ref[...] += dg
        dxn_ref[...] = dxn
        if has_post:
            dy, dg = _rms_bwd_math(y_ref[...], gp_ref[...], dxn)
            dy_ref[...] = dy.astype(BF16)

            @pl.when(first)
            def _():
                dgp_ref[...] = dg

            @pl.when(jnp.logical_not(first))
            def _():
                dgp_ref[...] += dg

    ins, in_specs = [dres], [_row_spec()]
    if has_pre:
        ins += list(pre)
        in_specs += [_row_spec(), _vec_spec(), _row_spec()]
    if has_post:
        ins += list(post)
        in_specs += [_row_spec(), _vec_spec()]
    out_shape, out_specs = [jax.ShapeDtypeStruct((S, D), F32)], [_row_spec()]
    if has_post:
        out_shape.append(jax.ShapeDtypeStruct((S, D), BF16))
        out_specs.append(_row_spec())
    for _ in range(int(has_pre) + int(has_post)):
        out_shape.append(jax.ShapeDtypeStruct((1, D), F32))
        out_specs.append(_vec_spec())
    out = list(pl.pallas_call(
        body, name=name, grid=(S // TR,), in_specs=in_specs, out_specs=out_specs, out_shape=out_shape,
        compiler_params=_params(("arbitrary",)))(*ins))
    dxn = out.pop(0)
    dy = out.pop(0) if has_post else None
    dgn = out.pop(0) if has_pre else None
    dgp = out.pop(0) if has_post else None
    return dxn, dy, dgn, dgp


def _loss_kernel(y, target):
    def body(y_ref, t_ref, loss_ref, dy_ref):
        e = y_ref[...] - t_ref[...]
        dy_ref[...] = e * (1.0 / D)
        part = jnp.zeros((1, 128), F32) + 0.5 * jnp.sum(jnp.mean(e * e, axis=-1, keepdims=True))

        @pl.when(pl.program_id(0) == 0)
        def _():
            loss_ref[...] = part

        @pl.when(pl.program_id(0) > 0)
        def _():
            loss_ref[...] += part

    return pl.pallas_call(
        body, name="loss", grid=(S // TR,), in_specs=[_row_spec(), _row_spec()],
        out_specs=[_vec_spec(128), _row_spec()],
        out_shape=[jax.ShapeDtypeStruct((1, 128), F32), jax.ShapeDtypeStruct((S, D), F32)],
        compiler_params=_params(("arbitrary",)))(y, target)


def _t5_bucket_np(dist):
    max_exact = NUM_BUCKETS // 2
    nf = np.maximum(dist, 1).astype(np.float32)
    large = max_exact + (np.log(nf / max_exact) / np.float32(math.log(MAX_DISTANCE / max_exact))
                         * (NUM_BUCKETS - max_exact)).astype(np.int32)
    large = np.minimum(large, NUM_BUCKETS - 1)
    return np.where(dist < max_exact, dist, large).astype(np.int32)


def _bucket_maps():
    a = np.arange(BLK)[:, None]
    b = np.arange(2 * BLK)[None, :]
    dist = np.maximum(a + BLK - b, 0)
    maps = [_t5_bucket_np(dist * d) for _, d in A_GROUPS] + [_t5_bucket_np(dist)]
    return np.stack(maps).astype(np.int32)


def _pair_spec(col0):
    return pl.BlockSpec((S, 128), lambda p: (0, col0 + p))


def _band_rows(i, d):
    nb = S // d // BLK
    r, b = i // nb, i % nb
    cur = pl.ds(b * BLK * d + r, BLK, stride=d)
    prev = pl.ds(jnp.maximum(b - 1, 0) * BLK * d + r, BLK, stride=d)
    return cur, prev, jnp.minimum(b, 1)


def _band_bias(tab_ref, bidx_ref, h):
    bi = bidx_ref[...]
    bias = jnp.zeros((BLK, 2 * BLK), F32)
    for kk in range(NUM_BUCKETS):
        bias = jnp.where(bi == kk, tab_ref[kk, h], bias)
    return bias


def _lane_lo(rows=BLK):
    return lax.broadcasted_iota(jnp.int32, (rows, 128), 1) < HD


def _per_head(x, lo):
    return (jnp.sum(jnp.where(lo, x, 0.0), axis=1, keepdims=True) * (1.0 / HD),
            jnp.sum(jnp.where(lo, 0.0, x), axis=1, keepdims=True) * (1.0 / HD))


def _band_fill(bias_ref, tab_ref, bidx_ref, head, maxd):
    a = lax.broadcasted_iota(jnp.int32, (BLK, 2 * BLK), 0)
    c = lax.broadcasted_iota(jnp.int32, (BLK, 2 * BLK), 1)
    dist = a + BLK - c
    in_band = jnp.logical_and(dist >= 0, dist <= maxd)
    for h in range(2):
        bias = jnp.where(in_band, _band_bias(tab_ref, bidx_ref, head + h), NEG)
        bias_ref[1, h * BLK:(h + 1) * BLK, :] = bias
        bias_ref[0, h * BLK:(h + 1) * BLK, :] = jnp.where(c >= BLK, bias, NEG)


def _stack_heads(x, lo, dtype=BF16):
    return jnp.concatenate([jnp.where(lo, x, 0.0), jnp.where(lo, 0.0, x)], axis=0).astype(dtype)


def _unstack_heads(x, lo):
    n = x.shape[0] // 2
    return jnp.where(lo, x[:n], x[n:])


def _stack_rows(ref, prev, cur):
    return jnp.concatenate([ref[prev, :], ref[cur, :]], axis=0).astype(BF16)


def _band_fwd(name, d, n_pairs, maxd, head0, srcs, bidx_g, tab, sinks):
    (qa, qc), (ka, kc), (va, vc) = srcs
    out_spec = _pair_spec(0)
    smem = pl.BlockSpec(memory_space=pltpu.SMEM)
    full = pl.BlockSpec((BLK, 2 * BLK), lambda p: (0, 0))

    def body(tab_ref, sink_ref, q_ref, k_ref, v_ref, bidx_ref, o_ref, lse_ref, bias_ref):
        p = pl.program_id(0)
        _band_fill(bias_ref, tab_ref, bidx_ref, head0 + 2 * p, maxd)
        lo = _lane_lo()
        sink = jnp.where(lax.broadcasted_iota(jnp.int32, (2 * BLK, 1), 0) < BLK, sink_ref[2 * p], sink_ref[2 * p + 1])

        def block(i, carry):
            cur, prev, has_prev = _band_rows(i, d)
            qs = _stack_heads(q_ref[cur, :] * SCALE, lo)
            ks, vs = _stack_rows(k_ref, prev, cur), _stack_rows(v_ref, prev, cur)
            s = _dot(qs, ks, NT) + bias_ref[has_prev]
            m = jnp.max(s, axis=1, keepdims=True)
            pr = jnp.exp(s - m)
            l = jnp.sum(pr, axis=1, keepdims=True)
            num = _dot(pr.astype(BF16), vs, NN)
            lse = m + jnp.log(l)
            sig = 1.0 / (1.0 + jnp.exp(sink - lse))
            o_ref[cur, :] = _unstack_heads(num * (sig / l), lo)
            lse_ref[cur, :] = _unstack_heads(lse + jnp.zeros((2 * BLK, 128), F32), lo)
            return carry

        lax.fori_loop(0, NQB, block, 0, unroll=2)

    shape = jax.ShapeDtypeStruct((S, n_pairs * 128), F32)
    return pl.pallas_call(
        body, name=name, grid=(n_pairs,),
        in_specs=[smem, smem, _pair_spec(qc), _pair_spec(kc), _pair_spec(vc), full],
        out_specs=[out_spec, out_spec], out_shape=[shape, shape],
        scratch_shapes=[pltpu.VMEM((2, 2 * BLK, 2 * BLK), F32)],
        compiler_params=_params(("parallel",)))(tab, sinks, qa, ka, va, bidx_g)


def _band_bwd(name, d, n_pairs, maxd, head0, srcs, bidx_g, tab, sinks, o, lse, do, stats_in):
    (qa, qc), (ka, kc), (va, vc) = srcs
    pair = _pair_spec(0)
    smem = pl.BlockSpec(memory_space=pltpu.SMEM)
    full = pl.BlockSpec((BLK, 2 * BLK), lambda p: (0, 0))
    stat_spec = pl.BlockSpec((2, 8, 128), lambda p: (p, 0, 0))

    def body(tab_ref, sink_ref, q_ref, k_ref, v_ref, bidx_ref, o_ref, lse_ref, do_ref, sin_ref,
             dq_ref, dk_ref, dv_ref, stat_ref, bias_ref, dsacc_ref, sk_ref):
        p = pl.program_id(0)
        _band_fill(bias_ref, tab_ref, bidx_ref, head0 + 2 * p, maxd)
        dsacc_ref[...] = jnp.zeros_like(dsacc_ref)
        sk_ref[...] = jnp.zeros_like(sk_ref)
        dk_ref[...] = jnp.zeros_like(dk_ref)
        dv_ref[...] = jnp.zeros_like(dv_ref)
        lo = _lane_lo()
        head1 = lax.broadcasted_iota(jnp.int32, (2 * BLK, 1), 0) >= BLK
        sink = jnp.where(head1, sink_ref[2 * p + 1], sink_ref[2 * p])

        def block(i, carry):
            cur, prev, has_prev = _band_rows(i, d)
            qs = _stack_heads(q_ref[cur, :] * SCALE, lo)
            ks, vs = _stack_rows(k_ref, prev, cur), _stack_rows(v_ref, prev, cur)
            do = do_ref[cur, :]
            dos = _stack_heads(do, lo, F32)
            lse = jnp.concatenate(_per_head(lse_ref[cur, :], lo), axis=0)
            prod = do * o_ref[cur, :]
            delta = jnp.concatenate([jnp.sum(jnp.where(lo, prod, 0.0), axis=1, keepdims=True),
                                     jnp.sum(jnp.where(lo, 0.0, prod), axis=1, keepdims=True)], axis=0)
            sig = 1.0 / (1.0 + jnp.exp(sink - lse))
            pr = jnp.exp(_dot(qs, ks, NT) + bias_ref[has_prev] - lse)
            ds = pr * (sig * (_dot(dos.astype(BF16), vs, NT) - delta))
            dsb = ds.astype(BF16)
            dq_ref[cur, :] = SCALE * _unstack_heads(_dot(dsb, ks, NN), lo)
            dk = _dot(dsb, qs, TN)
            dv = _dot(pr.astype(BF16), (sig * dos).astype(BF16), TN)
            dk_ref[prev, :] += dk[:BLK]
            dk_ref[cur, :] += dk[BLK:]
            dv_ref[prev, :] += dv[:BLK]
            dv_ref[cur, :] += dv[BLK:]
            dsacc_ref[...] += ds
            sink_grad = -delta * (1.0 - sig)
            for h in range(2):
                sk_ref[h] += jnp.zeros((8, 128), F32) + jnp.sum(sink_grad[h * BLK:(h + 1) * BLK])
            return carry

        lax.fori_loop(0, NQB, block, 0, unroll=2)

        bi = bidx_ref[...]
        lane = lax.broadcasted_iota(jnp.int32, (8, 128), 1)
        sub = lax.broadcasted_iota(jnp.int32, (8, 128), 0)
        for h in range(2):
            acc = dsacc_ref[h * BLK:(h + 1) * BLK, :]
            row = jnp.where(jnp.logical_and(sub == 1, lane == 0), sk_ref[h], 0.0)
            for kk in range(NUM_BUCKETS):
                tot = jnp.sum(jnp.where(bi == kk, acc, 0.0))
                row = jnp.where(jnp.logical_and(sub == 0, lane == kk), tot, row)
            stat_ref[h] = row + jnp.where(sub == 0, sin_ref[h], 0.0)

    shape = jax.ShapeDtypeStruct((S, n_pairs * 128), F32)
    return pl.pallas_call(
        body, name=name, grid=(n_pairs,),
        in_specs=[smem, smem, _pair_spec(qc), _pair_spec(kc), _pair_spec(vc), full, pair, pair, pair, stat_spec],
        out_specs=[pair, pair, pair, stat_spec],
        out_shape=[shape, shape, shape, jax.ShapeDtypeStruct((2 * n_pairs, 8, 128), F32)],
        scratch_shapes=[pltpu.VMEM((2, 2 * BLK, 2 * BLK), F32), pltpu.VMEM((2 * BLK, 2 * BLK), F32),
                        pltpu.VMEM((2, 8, 128), F32)],
        compiler_params=_params(("parallel",)))(tab, sinks, qa, ka, va, bidx_g, o, lse, do, stats_in)


def _comb_fwd(o_g, lse_g):
    def body(o0, o1, o2, l0, l1, l2, out_ref, outb_ref, lse_ref):
        a0, a1, a2 = l0[...], l1[...], l2[...]
        m = jnp.maximum(jnp.maximum(a0, a1), a2)
        e0, e1, e2 = jnp.exp(a0 - m), jnp.exp(a1 - m), jnp.exp(a2 - m)
        tot = e0 + e1 + e2
        out = (e0 * o0[...] + e1 * o1[...] + e2 * o2[...]) / tot
        out_ref[...] = out
        outb_ref[...] = out.astype(BF16)
        lse_ref[...] = m + jnp.log(tot)

    spec = _row_spec(4 * HD)
    f32 = jax.ShapeDtypeStruct((S, 4 * HD), F32)
    return pl.pallas_call(
        body, name="comb_fwd", grid=(S // TR,), in_specs=[spec] * 6, out_specs=[spec] * 3,
        out_shape=[f32, jax.ShapeDtypeStruct((S, 4 * HD), BF16), f32],
        compiler_params=_params(("parallel",)))(*o_g, *lse_g)


def _split2(x):
    hi = x.astype(BF16)
    return hi, (x - hi.astype(F32)).astype(BF16)


KB = 2 * BLK
SBQ = 2 * BLK


def _tri_sum(x, tri):
    hi, lo = _split2(x)
    both = _dot(jnp.concatenate([hi, lo], axis=0), tri, NN)
    return both[:x.shape[0]] + both[x.shape[0]:]


def _tri(strict_upper):
    r = lax.broadcasted_iota(jnp.int32, (KB, KB), 0)
    c = lax.broadcasted_iota(jnp.int32, (KB, KB), 1)
    return jnp.where(r > c if strict_upper else r < c, 1.0, 0.0).astype(BF16)


def _sb_terms(qs, kj, before):
    z = _dot(qs, kj, NT)
    lsp = jnp.minimum(z, 0.0) - jnp.log(1.0 + jnp.exp(-jnp.abs(z)))
    return lsp, _sb_keep(before, lsp - z)


def _sb_keep(before, x):
    return x if before is None else jnp.where(before, x, 0.0)


def _sb_before(i, m):
    t = (lax.broadcasted_iota(jnp.int32, (2 * SBQ, KB), 0) & (SBQ - 1)) + i * SBQ
    s = lax.broadcasted_iota(jnp.int32, (2 * SBQ, KB), 1) + m * KB
    return s < t


C_COL = 3072 // 128


def _sb_fwd(proj):
    blk = lambda off: pl.BlockSpec((SBQ, 128), lambda p, i: (i, off + p))
    col = lambda off: pl.BlockSpec((S, 128), lambda p, i: (0, off + p))
    out = pl.BlockSpec((SBQ, 128), lambda p, i: (i, p))

    def body(q_ref, k_ref, v_ref, o_ref, ob_ref, tot_ref):
        i = pl.program_id(1)
        lo = _lane_lo(SBQ)
        qs = _stack_heads(q_ref[...] * SCALE, lo)
        suffix = _tri(True)

        def step(n, carry, diagonal=False):
            acc, rest = carry
            m = i - n
            rows = pl.ds(pl.multiple_of(m * KB, KB), KB)
            kj, vj = k_ref[rows, :].astype(BF16), v_ref[rows, :].astype(BF16)
            before = _sb_before(i, m) if diagonal else None
            lsp, lk = _sb_terms(qs, kj, before)
            w = _sb_keep(before, jnp.exp(lsp + _tri_sum(lk, suffix) + rest))
            return acc + _dot(w.astype(BF16), vj, NN), rest + jnp.sum(lk, axis=1, keepdims=True)

        first = step(0, (jnp.zeros((2 * SBQ, 128), F32), jnp.zeros((2 * SBQ, 1), F32)), diagonal=True)
        acc, rest = lax.fori_loop(1, i + 1, step, first)
        o = _unstack_heads(acc, lo)
        o_ref[...] = o
        ob_ref[...] = o.astype(BF16)
        tot_ref[...] = _unstack_heads(rest + jnp.zeros((2 * SBQ, 128), F32), lo)

    f32 = jax.ShapeDtypeStruct((S, 4 * HD), F32)
    return pl.pallas_call(
        body, name="sb_fwd", grid=(2, S // SBQ), in_specs=[blk(C_COL), col(C_COL + 2), col(C_COL + 4)],
        out_specs=[out, out, out], out_shape=[f32, jax.ShapeDtypeStruct((S, 4 * HD), BF16), f32],
        compiler_params=_params(("parallel", "arbitrary")))(proj, proj, proj)


def _sb_bwd(proj, tot, do):
    blk = lambda off: pl.BlockSpec((SBQ, 128), lambda p, i: (i, off + p))
    col = lambda off: pl.BlockSpec((S, 128), lambda p, i: (0, off + p))

    def body(q_ref, k_ref, v_ref, tot_ref, do_ref, dq_ref, dk_ref, dv_ref):
        i = pl.program_id(1)

        @pl.when(i == 0)
        def _():
            dk_ref[...] = jnp.zeros_like(dk_ref)
            dv_ref[...] = jnp.zeros_like(dv_ref)

        lo = _lane_lo(SBQ)
        qs = _stack_heads(q_ref[...] * SCALE, lo)
        dos = _stack_heads(do_ref[...], lo)
        tots = jnp.concatenate(_per_head(tot_ref[...], lo), axis=0)
        prefix = _tri(False)

        def step(m, carry, diagonal=False):
            dq, keep_left, g_left = carry
            rows = pl.ds(pl.multiple_of(m * KB, KB), KB)
            kj, vj = k_ref[rows, :].astype(BF16), v_ref[rows, :].astype(BF16)
            before = _sb_before(i, m) if diagonal else None
            lsp, lk = _sb_terms(qs, kj, before)
            log_rest = tots - keep_left - lk - _tri_sum(lk, prefix)
            w = _sb_keep(before, jnp.exp(lsp + log_rest))
            g = w * _dot(dos, vj, NT)
            g_before = g_left + _dot(g.astype(BF16), prefix, NN)
            beta = jnp.exp(lsp)
            dz = _sb_keep(before, g * (1.0 - beta) - g_before * beta).astype(BF16)
            dk_ref[rows, :] += _dot(dz, qs, TN)
            dv_ref[rows, :] += _dot(w.astype(BF16), dos, TN)
            return (dq + _dot(dz, kj, NN), keep_left + jnp.sum(lk, axis=1, keepdims=True),
                    g_left + jnp.sum(g, axis=1, keepdims=True))

        zero = (jnp.zeros((2 * SBQ, 128), F32), jnp.zeros((2 * SBQ, 1), F32), jnp.zeros((2 * SBQ, 1), F32))
        dq, _, _ = step(i, lax.fori_loop(0, i, step, zero), diagonal=True)
        dq_ref[...] = SCALE * _unstack_heads(dq, lo)

    out_blk = pl.BlockSpec((SBQ, 128), lambda p, i: (i, p))
    out_col = pl.BlockSpec((S, 128), lambda p, i: (0, p))
    f32 = jax.ShapeDtypeStruct((S, 4 * HD), F32)
    return pl.pallas_call(
        body, name="sb_bwd", grid=(2, S // SBQ),
        in_specs=[blk(C_COL), col(C_COL + 2), col(C_COL + 4), out_blk, out_blk],
        out_specs=[out_blk, out_col, out_col], out_shape=[f32, f32, f32],
        compiler_params=_params(("arbitrary", "arbitrary")))(proj, proj, proj, tot, do)


TG = 256
TGR = 1024
GATE_BLK0 = OFF_GATE // TG


def _gate_specs():
    grid = (D // TG, S // TGR)
    p_specs = [pl.BlockSpec((TGR, TG), functools.partial(lambda c, r, br: (r, GATE_BLK0 + br * (D // TG) + c), br=br))
               for br in range(3)]
    b_spec = pl.BlockSpec((3, TG), lambda c, r: (0, c))
    t_spec = pl.BlockSpec((TGR, TG), lambda c, r: (r, c))
    return grid, p_specs, b_spec, t_spec


def _sigmoid(x):
    return 1.0 / (1.0 + jnp.exp(-x))


def _three_rows(rows):
    sub = lax.broadcasted_iota(jnp.int32, (3, rows[0].shape[1]), 0)
    return jnp.where(sub == 0, rows[0], jnp.where(sub == 1, rows[1], rows[2]))


def _gate_fwd(proj, b_gate, br):
    grid, p_specs, b_spec, t_spec = _gate_specs()

    def body(p0, p1, p2, b_ref, r0, r1, r2, out_ref):
        acc = jnp.zeros((TGR, TG), F32)
        for n, (p, r) in enumerate(((p0, r0), (p1, r1), (p2, r2))):
            acc += _sigmoid(p[...] + b_ref[n:n + 1, :]) * r[...]
        out_ref[...] = acc.astype(BF16)

    return pl.pallas_call(
        body, name="gate_fwd", grid=grid, in_specs=p_specs + [b_spec] + [t_spec] * 3, out_specs=t_spec,
        out_shape=jax.ShapeDtypeStruct((S, D), BF16),
        compiler_params=_params(("parallel", "parallel")))(proj, proj, proj, b_gate, *br)


def _gate_bwd(proj, b_gate, br, dmerged):
    grid, p_specs, b_spec, t_spec = _gate_specs()

    def body(p0, p1, p2, b_ref, r0, r1, r2, dm_ref, e0, e1, e2, g0, g1, g2, db_ref):
        dm = dm_ref[...]
        rows = []
        for n, (p, r, e_ref, dg_ref) in enumerate(((p0, r0, e0, g0), (p1, r1, e1, g1), (p2, r2, e2, g2))):
            g = _sigmoid(p[...] + b_ref[n:n + 1, :])
            e_ref[...] = (dm * g).astype(BF16)
            dpre = dm * r[...] * g * (1.0 - g)
            dg_ref[...] = dpre.astype(BF16)
            rows.append(jnp.sum(dpre, axis=0, keepdims=True))
        db = _three_rows(rows)

        @pl.when(pl.program_id(1) == 0)
        def _():
            db_ref[...] = db

        @pl.when(pl.program_id(1) > 0)
        def _():
            db_ref[...] += db

    bf = jax.ShapeDtypeStruct((S, D), BF16)
    out = pl.pallas_call(
        body, name="gate_bwd", grid=grid, in_specs=p_specs + [b_spec] + [t_spec] * 4,
        out_specs=[t_spec] * 6 + [b_spec], out_shape=[bf] * 6 + [jax.ShapeDtypeStruct((3, D), F32)],
        compiler_params=_params(("parallel", "arbitrary")))(proj, proj, proj, b_gate, *br, dmerged)
    return out[:3], out[3:6], out[6]


TC = 256
N_FF_BLK = D_FF // TC
GELU_C = math.sqrt(2.0 / math.pi)


def _shift_down(x, n):
    rows = lax.broadcasted_iota(jnp.int32, x.shape, 0)
    return jnp.where(rows >= n, pltpu.roll(x, n, axis=0), 0.0)


def _shift_up(x, n):
    rows = lax.broadcasted_iota(jnp.int32, x.shape, 0)
    return jnp.where(rows < x.shape[0] - n, pltpu.roll(x, x.shape[0] - n, axis=0), 0.0)


def _conv(u, w, b):
    s1, s2 = _shift_down(u, 1), _shift_down(u, 2)
    return w[2:3, :] * u + w[1:2, :] * s1 + w[0:1, :] * s2 + b, s1, s2


def _gelu_parts(x):
    inner = GELU_C * (x + 0.044715 * x * x * x)
    t = jnp.tanh(inner)
    gelu = 0.5 * x * (1.0 + t)
    dgelu = 0.5 * (1.0 + t) + 0.5 * x * (1.0 - t * t) * GELU_C * (1.0 + 3 * 0.044715 * x * x)
    return gelu, dgelu


def _conv_specs():
    ug = pl.BlockSpec((S, TC), lambda c: (0, c))
    uv = pl.BlockSpec((S, TC), lambda c: (0, N_FF_BLK + c))
    wg = pl.BlockSpec((3, TC), lambda c: (0, c))
    wv = pl.BlockSpec((3, TC), lambda c: (0, N_FF_BLK + c))
    bg = pl.BlockSpec((1, TC), lambda c: (0, c))
    bv = pl.BlockSpec((1, TC), lambda c: (0, N_FF_BLK + c))
    return ug, uv, wg, wv, bg, bv


def _conv_fwd(u, conv_w, conv_b):
    ug, uv, wg, wv, bg, bv = _conv_specs()

    def body(ug_ref, uv_ref, wg_ref, wv_ref, bg_ref, bv_ref, a_ref):
        gc = _conv(ug_ref[...], wg_ref[...], bg_ref[...])[0]
        vc = _conv(uv_ref[...], wv_ref[...], bv_ref[...])[0]
        a_ref[...] = (_gelu_parts(gc)[0] * vc).astype(BF16)

    return pl.pallas_call(
        body, name="conv_fwd", grid=(N_FF_BLK,), in_specs=[ug, uv, wg, wv, bg, bv], out_specs=ug,
        out_shape=jax.ShapeDtypeStruct((S, D_FF), BF16),
        compiler_params=_params(("parallel",)))(u, u, conv_w, conv_w, conv_b, conv_b)


def _conv_bwd(u, conv_w, conv_b, da):
    ug, uv, wg, wv, bg, bv = _conv_specs()

    def back(duc, u, s1, s2, w):
        du = w[2:3, :] * duc + w[1:2, :] * _shift_up(duc, 1) + w[0:1, :] * _shift_up(duc, 2)
        dw = _three_rows([jnp.sum(duc * s2, axis=0, keepdims=True), jnp.sum(duc * s1, axis=0, keepdims=True),
                          jnp.sum(duc * u, axis=0, keepdims=True)])
        return du, dw, jnp.sum(duc, axis=0, keepdims=True)

    def body(ug_ref, uv_ref, wg_ref, wv_ref, bg_ref, bv_ref, da_ref, dug_ref, duv_ref, dwg_ref, dwv_ref, dbg_ref, dbv_ref):
        u_g, u_v = ug_ref[...], uv_ref[...]
        gc, g1, g2 = _conv(u_g, wg_ref[...], bg_ref[...])
        vc, v1, v2 = _conv(u_v, wv_ref[...], bv_ref[...])
        gelu, dgelu = _gelu_parts(gc)
        da = da_ref[...]
        du, dw, db = back(da * vc * dgelu, u_g, g1, g2, wg_ref[...])
        dug_ref[...] = du.astype(BF16)
        dwg_ref[...] = dw
        dbg_ref[...] = db
        du, dw, db = back(da * gelu, u_v, v1, v2, wv_ref[...])
        duv_ref[...] = du.astype(BF16)
        dwv_ref[...] = dw
        dbv_ref[...] = db

    return pl.pallas_call(
        body, name="conv_bwd", grid=(N_FF_BLK,), in_specs=[ug, uv, wg, wv, bg, bv, ug],
        out_specs=[ug, ug, wg, wg, bg, bg],
        out_shape=[jax.ShapeDtypeStruct((S, D_FF), BF16), jax.ShapeDtypeStruct((S, D_FF), BF16),
                   jax.ShapeDtypeStruct((3, D_FF), F32), jax.ShapeDtypeStruct((3, D_FF), F32),
                   jax.ShapeDtypeStruct((1, D_FF), F32), jax.ShapeDtypeStruct((1, D_FF), F32)],
        compiler_params=_params(("parallel",)))(u, u, conv_w, conv_w, conv_b, conv_b, da)


def _adamw(name, w, g, m, v):
    shape = w.shape
    cols = shape[-1]
    flat = [t.reshape(-1, cols) for t in (w, g, m, v)]
    r = flat[0].shape[0]
    tr = min(r, max(8, 2 * 1024 * 1024 // (4 * cols)))

    def body(w_ref, g_ref, m_ref, v_ref, go_ref, d_ref, mo_ref, vo_ref):
        g = g_ref[...]
        go_ref[...] = g
        m = ADAM_B1 * m_ref[...] + (1.0 - ADAM_B1) * g
        v = ADAM_B2 * v_ref[...] + (1.0 - ADAM_B2) * (g * g)
        m_hat = m / (1.0 - ADAM_B1 ** ADAM_STEP)
        v_hat = v / (1.0 - ADAM_B2 ** ADAM_STEP)
        d_ref[...] = -ADAM_LR * (m_hat / (jnp.sqrt(v_hat) + ADAM_EPS) + ADAM_WD * w_ref[...])
        mo_ref[...] = m
        vo_ref[...] = v

    spec = pl.BlockSpec((tr, cols), lambda i: (i, 0))
    outs = pl.pallas_call(
        body, name=name, grid=(pl.cdiv(r, tr),), in_specs=[spec] * 4, out_specs=[spec] * 4,
        out_shape=[jax.ShapeDtypeStruct((r, cols), F32)] * 4, compiler_params=_params(("parallel",)))(*flat)
    return [t.reshape(shape) for t in outs]


def _place():
    x, y, c = lax.axis_index("x"), lax.axis_index("y"), lax.axis_index("c")
    chips = [(1 - x, y), (x, 1 - y), (1 - x, 1 - y)]
    return x, y, c, chips


def _scalars(*vals):
    return jnp.stack([jnp.asarray(v, jnp.int32) for v in vals])


HBM = pl.BlockSpec(memory_space=pltpu.HBM)
SEM = pl.BlockSpec(memory_space=pltpu.SEMAPHORE)
SPLIT_COPY = pltpu.CompilerParams(has_side_effects=pltpu.SideEffectType.DATAFLOW_SIDE_EFFECTING)


def _in_hbm(x):
    return pltpu.with_memory_space_constraint(x, pltpu.HBM)


def _cast_into_slot(name, w, layer, chip):
    _, k, n4 = w.shape
    tr = max(t for t in range(16, 513, 16) if k % t == 0)

    def body(chip_ref, w_ref, o_ref):
        o_ref[...] = w_ref[...].astype(BF16)

    return pl.pallas_call(
        body, name=name,
        grid_spec=pltpu.PrefetchScalarGridSpec(
            num_scalar_prefetch=1, grid=(k // tr,),
            in_specs=[pl.BlockSpec((None, tr, n4), lambda i, s: (layer, i, 0))],
            out_specs=pl.BlockSpec((None, tr, n4), lambda i, s: (s[0], i, 0))),
        out_shape=jax.ShapeDtypeStruct((N_CHIPS, k, n4), BF16),
        compiler_params=_params(("parallel",)))(_scalars(chip), w)


def _gather_copy(buf_ref, k, from_chip, send_sem, recv_sem, chips, c, half=False):
    rows = buf_ref.at[from_chip]
    if half:
        h = buf_ref.shape[1] // 2
        rows = buf_ref.at[from_chip, pl.ds(pl.multiple_of(c * h, h), h)]
    return pltpu.make_async_remote_copy(src_ref=rows, dst_ref=rows, send_sem=send_sem, recv_sem=recv_sem,
                                        device_id=(*chips[k], c), device_id_type=MESH)


def _gather_start(name, bufs, groups, halved=()):
    n, ng = len(bufs), len(groups)
    where = {a: (gi, e) for gi, g in enumerate(groups) for e, a in enumerate(g)}

    def body(*refs):
        ins, sems, token = refs[:n], refs[n:n + 2 * ng], refs[-1]
        x, y, c, chips = _place()
        for a in range(n):
            gi, e = where[a]
            for k in range(3):
                _gather_copy(ins[a], k, 2 * x + y, sems[2 * gi].at[3 * e + k], sems[2 * gi + 1].at[3 * e + k],
                             chips, c, a in halved).start()
        token[...] = jnp.zeros_like(token)

    out_shape = [pltpu.SemaphoreType.DMA((3 * len(g),)) for g in groups for _ in range(2)]
    out_shape += [pltpu.HBM(b.shape, b.dtype) for b in bufs] + [jax.ShapeDtypeStruct((8, 128), F32)]
    out = pl.pallas_call(
        body, name=name, in_specs=[HBM] * n,
        out_specs=[SEM] * (2 * ng) + [HBM] * n + [pl.BlockSpec(memory_space=pltpu.VMEM)], out_shape=out_shape,
        input_output_aliases={a: 2 * ng + a for a in range(n)}, compiler_params=SPLIT_COPY)(*[_in_hbm(b) for b in bufs])
    sems = [(out[2 * gi], out[2 * gi + 1]) for gi in range(ng)]
    return sems, list(out[2 * ng:2 * ng + n]), out[-1]


def _gather_wait(name, bufs, send, recv, after, halved=()):
    n = len(bufs)

    def body(*refs):
        ins, send_sem, recv_sem = refs[:n], refs[n], refs[n + 1]
        x, y, c, chips = _place()
        for e in range(n):
            for k in range(3):
                sems = (send_sem.at[3 * e + k], recv_sem.at[3 * e + k])
                _gather_copy(ins[e], k, 2 * x + y, *sems, chips, c, e in halved).wait_send()
                _gather_copy(ins[e], k, 2 * chips[k][0] + chips[k][1], *sems, chips, c, e in halved).wait_recv()

    return pl.pallas_call(
        body, name=name, in_specs=[HBM] * n + [SEM, SEM, ANY], out_specs=[HBM] * n,
        out_shape=[pltpu.HBM(b.shape, b.dtype) for b in bufs],
        input_output_aliases={a: a for a in range(n)}, compiler_params=SPLIT_COPY)(*bufs, send, recv, after)


def _swap_halves(name, bufs):
    n = len(bufs)

    def body(*refs):
        ins, outs = refs[:n], refs[n:2 * n]
        send_sem, recv_sem = refs[2 * n:]
        x, y, c, chips = _place()

        def piece(ref, k, which):
            h = ref.shape[1] // 2
            return ref.at[2 * chips[k][0] + chips[k][1], pl.ds(pl.multiple_of(which * h, h), h)]

        def copy(a, k, which):
            return pltpu.make_async_remote_copy(
                src_ref=piece(ins[a], k, c), dst_ref=piece(outs[a], k, which), send_sem=send_sem.at[3 * a + k],
                recv_sem=recv_sem.at[3 * a + k], device_id=(x, y, 1 - c), device_id_type=MESH)

        for a in range(n):
            for k in range(3):
                copy(a, k, c).start()
        for a in range(n):
            for k in range(3):
                copy(a, k, c).wait_send()
                copy(a, k, 1 - c).wait_recv()

    return pl.pallas_call(
        body, name=name, in_specs=[ANY] * n, out_specs=[ANY] * n,
        out_shape=[jax.ShapeDtypeStruct(b.shape, b.dtype) for b in bufs],
        input_output_aliases={a: a for a in range(n)},
        scratch_shapes=[pltpu.SemaphoreType.DMA((3 * n,)), pltpu.SemaphoreType.DMA((3 * n,))],
    )(*bufs)


def _reduce_copy(g_ref, land_ref, mask, send_sem, recv_sem, x, y, c, sending):
    px, py, pc = x ^ ((mask >> 2) & 1), y ^ ((mask >> 1) & 1), c ^ (mask & 1)
    half = g_ref.shape[1] // 2
    src = g_ref.at[2 * px + py, pl.ds(pl.multiple_of(pc * half, half), half)]
    dst = land_ref.at[4 * x + 2 * y + c] if sending else land_ref.at[4 * px + 2 * py + pc]
    return pltpu.make_async_remote_copy(src_ref=src, dst_ref=dst, send_sem=send_sem, recv_sem=recv_sem,
                                        device_id=(px, py, pc), device_id_type=MESH)


def _reduce_start(name, grads):
    n = len(grads)
    lands = [lax.empty((N_DEV, g.shape[1] // 2, g.shape[2]), g.dtype) for g in grads]

    def body(*refs):
        gs, ls, send_sem, recv_sem = refs[:n], refs[n:2 * n], refs[2 * n], refs[2 * n + 1]
        x, y, c, _ = _place()
        for a in range(n):
            for mask in range(1, N_DEV):
                s = (N_DEV - 1) * a + mask - 1
                _reduce_copy(gs[a], ls[a], mask, send_sem.at[s], recv_sem.at[s], x, y, c, True).start()
        refs[-1][...] = jnp.zeros_like(refs[-1])

    sem = pltpu.SemaphoreType.DMA((n * (N_DEV - 1),))
    out = pl.pallas_call(
        body, name=name, in_specs=[HBM] * (2 * n),
        out_specs=[SEM, SEM] + [HBM] * (2 * n) + [pl.BlockSpec(memory_space=pltpu.VMEM)],
        out_shape=[sem, sem] + [pltpu.HBM(t.shape, t.dtype) for t in grads + lands] + [jax.ShapeDtypeStruct((8, 128), F32)],
        input_output_aliases={a: 2 + a for a in range(2 * n)}, compiler_params=SPLIT_COPY)(
            *[_in_hbm(t) for t in grads + lands])
    return out[0], out[1], list(out[2:2 + n]), list(out[2 + n:2 + 2 * n]), out[-1]


def _reduce_wait(name, send, recv, grads, lands, after):
    n = len(grads)

    def body(*refs):
        gs, ls, send_sem, recv_sem = refs[:n], refs[n:2 * n], refs[2 * n], refs[2 * n + 1]
        x, y, c, _ = _place()
        for a in range(n):
            for mask in range(1, N_DEV):
                s = (N_DEV - 1) * a + mask - 1
                sems = (send_sem.at[s], recv_sem.at[s])
                _reduce_copy(gs[a], ls[a], mask, *sems, x, y, c, True).wait_send()
                _reduce_copy(gs[a], ls[a], mask, *sems, x, y, c, False).wait_recv()

    out = pl.pallas_call(
        body, name=name, in_specs=[HBM] * (2 * n) + [SEM, SEM, ANY], out_specs=[HBM] * (2 * n),
        out_shape=[pltpu.HBM(t.shape, t.dtype) for t in grads + lands],
        input_output_aliases={a: a for a in range(2 * n)}, compiler_params=SPLIT_COPY)(*grads, *lands, send, recv, after)
    return list(out[:n]), list(out[n:])


def _reduce_sum(name, g, land, layer, into, chip, c):
    _, k4, n4 = g.shape
    half = k4 // 2
    tr = max(t for t in range(16, 513, 16) if half % t == 0)
    per = half // tr
    me = 2 * chip + c

    def body(s_ref, own_ref, *refs):
        total = own_ref[...].astype(F32)
        for ref in refs[:N_DEV - 1]:
            total = total + ref[...].astype(F32)
        refs[-1][...] = total

    in_specs = [pl.BlockSpec((None, tr, n4), lambda i, s: (s[0], s[1] * per + i, 0))]
    in_specs += [pl.BlockSpec((None, tr, n4), functools.partial(lambda i, s, m: (s[1 + m], i, 0), m=m))
                 for m in range(1, N_DEV)]
    ins = [g] + [land] * (N_DEV - 1)
    aliases = {}
    if into is not None:
        in_specs, ins, aliases = in_specs + [ANY], ins + [into], {1 + N_DEV: 0}
    return pl.pallas_call(
        body, name=name,
        grid_spec=pltpu.PrefetchScalarGridSpec(
            num_scalar_prefetch=1, grid=(per,), in_specs=in_specs,
            out_specs=pl.BlockSpec((None, tr, n4), lambda i, s: (layer, s[1] * per + i, 0))),
        out_shape=jax.ShapeDtypeStruct((DEPTH, k4, n4), F32), input_output_aliases=aliases,
        compiler_params=_params(("parallel",)))(_scalars(chip, c, *[me ^ m for m in range(1, N_DEV)]), *ins)


def _join_halves(name, bufs):
    n = len(bufs)

    def body(*refs):
        ins, outs = refs[:n], refs[n:2 * n]
        send_sem, recv_sem = refs[2 * n:]
        x, y, c, _ = _place()

        def rows(ref, which):
            half = ref.shape[1] // 2
            return ref.at[:, pl.ds(pl.multiple_of(which * half, half), half)]

        sends = [pltpu.make_async_remote_copy(
            src_ref=rows(ins[a], c), dst_ref=rows(outs[a], c), send_sem=send_sem.at[a], recv_sem=recv_sem.at[a],
            device_id=(x, y, 1 - c), device_id_type=MESH) for a in range(n)]
        for cp in sends:
            cp.start()
        for a in range(n):
            sends[a].wait_send()
            pltpu.make_async_remote_copy(
                src_ref=rows(ins[a], c), dst_ref=rows(outs[a], 1 - c), send_sem=send_sem.at[a], recv_sem=recv_sem.at[a],
                device_id=(x, y, 1 - c), device_id_type=MESH).wait_recv()

    return pl.pallas_call(
        body, name=name, in_specs=[ANY] * n, out_specs=[ANY] * n,
        out_shape=[jax.ShapeDtypeStruct(b.shape, b.dtype) for b in bufs],
        input_output_aliases={a: a for a in range(n)},
        scratch_shapes=[pltpu.SemaphoreType.DMA((n,)), pltpu.SemaphoreType.DMA((n,))],
    )(*bufs)


def _small_copy(b_ref, l_ref, mask, send_sem, recv_sem, x, y, c, sending):
    px, py, pc = x ^ ((mask >> 2) & 1), y ^ ((mask >> 1) & 1), c ^ (mask & 1)
    dst = l_ref.at[4 * x + 2 * y + c] if sending else l_ref.at[4 * px + 2 * py + pc]
    return pltpu.make_async_remote_copy(src_ref=b_ref, dst_ref=dst, send_sem=send_sem.at[mask - 1],
                                        recv_sem=recv_sem.at[mask - 1], device_id=(px, py, pc), device_id_type=MESH)


def _small_start(block):
    land = lax.empty((N_DEV,) + block.shape, block.dtype)

    def body(b_ref, l_ref, send_sem, recv_sem, b_thru, l_thru, token):
        x, y, c, _ = _place()
        for mask in range(1, N_DEV):
            _small_copy(b_ref, l_ref, mask, send_sem, recv_sem, x, y, c, True).start()
        token[...] = jnp.zeros_like(token)

    sem = pltpu.SemaphoreType.DMA((N_DEV - 1,))
    return pl.pallas_call(
        body, name="small_start", in_specs=[HBM, HBM],
        out_specs=[SEM, SEM, HBM, HBM, pl.BlockSpec(memory_space=pltpu.VMEM)],
        out_shape=[sem, sem, pltpu.HBM(block.shape, block.dtype), pltpu.HBM(land.shape, land.dtype),
                   jax.ShapeDtypeStruct((8, 128), F32)],
        input_output_aliases={0: 2, 1: 3}, compiler_params=SPLIT_COPY)(_in_hbm(block), _in_hbm(land))


def _small_wait(send, recv, block, land, after):
    def body(b_ref, l_ref, send_sem, recv_sem, after_ref, b_out, l_out):
        x, y, c, _ = _place()
        for mask in range(1, N_DEV):
            _small_copy(b_ref, l_ref, mask, send_sem, recv_sem, x, y, c, True).wait_send()
            _small_copy(b_ref, l_ref, mask, send_sem, recv_sem, x, y, c, False).wait_recv()

    return pl.pallas_call(
        body, name="small_wait", in_specs=[HBM, HBM, SEM, SEM, ANY], out_specs=[HBM, HBM],
        out_shape=[pltpu.HBM(block.shape, block.dtype), pltpu.HBM(land.shape, land.dtype)],
        input_output_aliases={0: 0, 1: 1}, compiler_params=SPLIT_COPY)(block, land, send, recv, after)


def _small_sum(land):
    def body(l_ref, out_ref):
        total = l_ref[0]
        for d in range(1, N_DEV):
            total = total + l_ref[d]
        out_ref[...] = total

    vmem = pl.BlockSpec(memory_space=pltpu.VMEM)
    return pl.pallas_call(
        body, name="small_sum", in_specs=[vmem], out_specs=vmem,
        out_shape=jax.ShapeDtypeStruct(land.shape[1:], F32),
        compiler_params=pltpu.CompilerParams(vmem_limit_bytes=VMEM_LIMIT))(land)


B_Q_COL = 2304 // 128
B_K0, B_V0, B_END = 2816, 2944, 3072


def _full_cols(w_g):
    return w_g.transpose(1, 0, 2).reshape(w_g.shape[1], -1)


def _group_src(proj, g):
    return ((proj, 2 * g), (proj, 6 + 2 * g), (proj, 12 + 2 * g))


def _kv_expand(kv):
    return jnp.broadcast_to(kv.reshape(S, 2, 1, HD), (S, 2, 4, HD)).reshape(S, 8 * HD)


def _kv_reduce(dkv):
    return dkv.reshape(S, 2, 4, HD).sum(axis=2).reshape(S, 2 * HD)


def _mixer_fwd(h1, wget, rel_bias, sinks_l, bidx):
    w = dict(wget(0, h1))
    proj = _mm_nt("proj_in", h1, w["w_in"], F32, tm=S, tn=1152)
    no_sinks = jnp.full((4,), NEG, F32)
    srcs = [_group_src(proj, g) for g in range(3)]
    o_g, lse_g = [], []
    for g, (_, d) in enumerate(A_GROUPS):
        o, lse = _band_fwd("band_fwd_g%d" % g, d, 2, BLK, 4 * g, srcs[g], bidx[g], rel_bias, no_sinks)
        o_g.append(o)
        lse_g.append(lse)
    o_a32, o_a, lse_a = _comb_fwd(o_g, lse_g)
    src_b = ((proj, B_Q_COL), (_kv_expand(proj[:, B_K0:B_V0]), 0), (_kv_expand(proj[:, B_V0:B_END]), 0))
    o_b32, lse_b = _band_fwd("band_fwd_b", 1, 4, BLK - 1, N_A, src_b, bidx[3], rel_bias, sinks_l)
    o_b = o_b32.astype(BF16)
    o_c32, o_c, tot_c = _sb_fwd(proj)
    w.update(wget(1, o_c32))
    br = [_mm_nn("branch_a", o_a, w["w_br_a"], F32, tm=S), _mm_nn("branch_b", o_b, w["w_br_b"], F32, tm=S),
          _mm_nn("branch_c", o_c, w["w_br_c"], F32, tm=S)]
    merged = _gate_fwd(proj, w["b_gate"], br)
    mo = _mm_nn("out_proj", merged, w["w_out"], F32, tm=S)
    saved = dict(proj=proj, srcs=srcs, src_b=src_b, o_a32=o_a32, lse_a=lse_a, o_b32=o_b32, lse_b=lse_b, tot_c=tot_c,
                 o_a=o_a, o_b=o_b, o_c=o_c, br=br, merged=merged)
    return mo, saved, w


def _mixer_bwd(d_mo, h1, w, sv, rel_bias, sinks_l, bidx, stats_in, emit):
    grads = {}
    dmerged = _mm_nt("out_proj_dx", d_mo, w["w_out"], F32, tm=S)
    grads["w_out"] = _mm_tn_sharded("out_proj_dw", sv["merged"], d_mo, True)
    e, dgate, db_gate = _gate_bwd(sv["proj"], w["b_gate"], sv["br"], dmerged)
    grads["b_gate"] = db_gate
    d_o = {}
    for n, name in enumerate("abc"):
        d_o[name] = _mm_nt("branch_%s_dx" % name, e[n], w["w_br_" + name], F32, tm=S)
        grads["w_br_" + name] = _mm_tn_sharded("branch_%s_dw" % name, sv["o_" + name], e[n], False)
    zero = emit(1, grads)
    no_sinks = jnp.full((4,), NEG, F32) + zero[0]
    dqs, dks, dvs, stats = [], [], [], []
    for g, (_, d) in enumerate(A_GROUPS):
        dq, dk, dv, st = _band_bwd("band_bwd_g%d" % g, d, 2, BLK, 4 * g, sv["srcs"][g], bidx[g], rel_bias, no_sinks,
                                   sv["o_a32"], sv["lse_a"], d_o["a"], stats_in[4 * g:4 * g + 4])
        dqs.append(dq)
        dks.append(dk)
        dvs.append(dv)
        stats.append(st)
    dq_b, dk_x, dv_x, st = _band_bwd("band_bwd_b", 1, 4, BLK - 1, N_A, sv["src_b"], bidx[3], rel_bias, sinks_l,
                                     sv["o_b32"], sv["lse_b"], d_o["b"], stats_in[N_A:])
    stats = jnp.concatenate(stats + [st], axis=0)
    dcq, dck, dcv = _sb_bwd(sv["proj"], sv["tot_c"], d_o["c"])
    cols = dqs + dks + dvs + [dq_b, _kv_reduce(dk_x), _kv_reduce(dv_x), dcq, dck, dcv]
    dproj = jnp.concatenate([t.astype(BF16) for t in cols] + list(dgate), axis=1)
    grads["w_in"] = _mm_tn("proj_in_dw", dproj, h1, BF16, tm=1152, tn=1024).reshape(N_CHIPS, IN_SHARD, D)
    zero = emit(2, grads)
    dh1 = _mm_nn("proj_in_dx", dproj, w["w_in"], F32, tm=S, tk=2304)
    return dh1, grads, stats, zero


def _ffn_fwd(h2, w):
    u = _mm_nn("ffn_up", h2, w["w_up"], F32, tm=S, tn=1024)
    a = _conv_fwd(u, w["conv_w"], w["conv_b"])
    dn = _mm_nn("ffn_down", a, w["w_down"], F32, tm=1024)
    return dn, dict(u=u, a=a)


def _ffn_bwd(d_dn, h2, w, sv):
    grads = {}
    da = _mm_nt("ffn_down_dx", d_dn, w["w_down"], F32, tm=S, tn=1024)
    grads["w_down"] = _mm_tn_sharded("ffn_down_dw", sv["a"], d_dn, True, tm=1024, tn=1024)
    dug, duv, dwg, dwv, dbg, dbv = _conv_bwd(sv["u"], w["conv_w"], w["conv_b"], da)
    du = jnp.concatenate([dug, duv], axis=1)
    grads["conv_w"] = jnp.concatenate([dwg, dwv], axis=1)
    grads["conv_b"] = jnp.concatenate([dbg, dbv], axis=1)
    dh2 = _mm_nt("ffn_up_dx", du, w["w_up"], F32, tm=S, tk=2048)
    grads["w_up"] = _mm_tn_sharded("ffn_up_dw", h2, du, False, tm=1024, tn=1024)
    return dh2, grads


BIG = ("w_in", "w_br_a", "w_br_b", "w_br_c", "w_out", "w_up", "w_down")


def _shard_view(name, w):
    return jnp.swapaxes(w, 1, 2) if name == "w_in" else w
WEIGHT_GROUPS = (("w_in", "b_gate"), ("w_br_a", "w_br_b", "w_br_c", "w_out"), ("w_up", "conv_w", "w_down"))
GRAD_GROUPS = (("w_down", "w_up"), ("w_out", "w_br_a", "w_br_b", "w_br_c"), ("w_in",))
SMALL_ROWS = (("rel_bias", 8), ("attn_pre_norm", 16), ("attn_post_norm", 16), ("ffn_pre_norm", 16), ("ffn_post_norm", 16),
              ("sinks", 8), ("conv_b", 128), ("b_gate", 48), ("conv_w", 384), ("loss", 8))


def _pack_small(vals):
    rows = []
    for name, n in SMALL_ROWS:
        flat = vals[name].reshape(-1).astype(F32)
        rows.append(jnp.pad(flat, (0, n * 128 - flat.shape[0])).reshape(n, 128))
    return jnp.concatenate(rows, axis=0)


def _unpack_small(block, shapes):
    out, row = {}, 0
    for name, n in SMALL_ROWS:
        size = int(np.prod(shapes[name]))
        out[name] = block[row:row + n].reshape(-1)[:size].reshape(shapes[name])
        row += n
    return out


def kernel(x, rel_bias, attn_pre_norm, w_in, b_gate, sinks, w_br_a, w_br_b, w_br_c, w_out, attn_post_norm, ffn_pre_norm, w_up, conv_w, conv_b, w_down, ffn_post_norm, loss_target, m_rel_bias, m_attn_pre_norm, m_w_in, m_b_gate, m_sinks, m_w_br_a, m_w_br_b, m_w_br_c, m_w_out, m_attn_post_norm, m_ffn_pre_norm, m_w_up, m_conv_w, m_conv_b, m_w_down, m_ffn_post_norm, v_rel_bias, v_attn_pre_norm, v_w_in, v_b_gate, v_sinks, v_w_br_a, v_w_br_b, v_w_br_c, v_w_out, v_attn_post_norm, v_ffn_pre_norm, v_w_up, v_conv_w, v_conv_b, v_w_down, v_ffn_post_norm):
    names = ("rel_bias", "attn_pre_norm", "w_in", "b_gate", "sinks", "w_br_a", "w_br_b", "w_br_c", "w_out",
             "attn_post_norm", "ffn_pre_norm", "w_up", "conv_w", "conv_b", "w_down", "ffn_post_norm")
    weights = dict(zip(names, (rel_bias, attn_pre_norm, w_in, b_gate, sinks, w_br_a, w_br_b, w_br_c, w_out,
                               attn_post_norm, ffn_pre_norm, w_up, conv_w, conv_b, w_down, ffn_post_norm)))
    mom1 = dict(zip(names, (m_rel_bias, m_attn_pre_norm, m_w_in, m_b_gate, m_sinks, m_w_br_a, m_w_br_b, m_w_br_c,
                            m_w_out, m_attn_post_norm, m_ffn_pre_norm, m_w_up, m_conv_w, m_conv_b, m_w_down,
                            m_ffn_post_norm)))
    mom2 = dict(zip(names, (v_rel_bias, v_attn_pre_norm, v_w_in, v_b_gate, v_sinks, v_w_br_a, v_w_br_b, v_w_br_c,
                            v_w_out, v_attn_post_norm, v_ffn_pre_norm, v_w_up, v_conv_w, v_conv_b, v_w_down,
                            v_ffn_post_norm)))

    chip = 2 * lax.axis_index("x") + lax.axis_index("y")
    core = lax.axis_index("c")

    keys = [(n, l) for l in range(DEPTH) for group in WEIGHT_GROUPS for n in group]
    groups = [[keys.index((n, l)) for n in group] for l in range(DEPTH) for group in WEIGHT_GROUPS]

    def slot_buffer(n, l):
        if n in BIG:
            return _cast_into_slot("cast_" + n, _shard_view(n, weights[n]), l, chip)
        shard = weights[n][l]
        return lax.dynamic_update_slice(jnp.zeros((N_CHIPS,) + shard.shape, F32), shard[None],
                                        (chip, jnp.int32(0), jnp.int32(0)))

    by_halves = [keys.index(k) for k in (("w_in", 0), ("w_up", DEPTH - 1), ("w_down", DEPTH - 1))]
    n_first = len(groups[0])
    sems, in_flight, _ = _gather_start("gather_start_first", [slot_buffer(*k) for k in keys[:n_first]], groups[:1],
                                       tuple(a for a in by_halves if a < n_first))
    more = _gather_start("gather_start", [slot_buffer(*k) for k in keys[n_first:]],
                         [[a - n_first for a in g] for g in groups[1:]],
                         tuple(a - n_first for a in by_halves if a >= n_first))
    sems, in_flight, started = sems + more[0], in_flight + more[1], more[2]

    def wget(l, gi, after):
        g = l * len(WEIGHT_GROUPS) + gi
        after = started if g == 0 else after
        halved = tuple(e for e, a in enumerate(groups[g]) if a in by_halves)
        got = list(_gather_wait("gather_wait_%d_%d" % (l, gi), [in_flight[a] for a in groups[g]], *sems[g], after,
                                halved))
        if halved:
            for e, buf in zip(halved, _swap_halves("swap_halves_%d_%d" % (l, gi), [got[e] for e in halved])):
                got[e] = buf
        out = {}
        for n, buf in zip(WEIGHT_GROUPS[gi], got):
            out[n] = buf.reshape(-1, buf.shape[-1]) if n in ("w_in", "w_out", "w_down") else _full_cols(buf)
        if gi == len(WEIGHT_GROUPS) - 1:
            out["conv_b"] = conv_b[l:l + 1]
        return out

    pending = []

    def emit(l, gi, grads):
        group = GRAD_GROUPS[gi]
        *started, token = _reduce_start("reduce_start_%d_%d" % (l, gi), [grads[n] for n in group])
        pending.append((l, group) + tuple(started))
        return token[:1, :1]

    local = _local_step(x.reshape(S, D), loss_target.reshape(S, D), wget, emit, rel_bias, sinks, attn_pre_norm,
                        attn_post_norm, ffn_pre_norm, ffn_post_norm)
    return _reduce_and_update(x.shape, names, weights, mom1, mom2, chip, core, pending, *local)


def _local_step(xs, target, wget, emit, rel_bias, sinks, attn_pre_norm, attn_post_norm, ffn_pre_norm, ffn_post_norm):
    bidx = jnp.asarray(_bucket_maps())

    saved, layers = [], []
    h1 = _rms_fwd("pre_norm_first", xs, attn_pre_norm[0:1])
    x_in = xs
    for l in range(DEPTH):
        mo, sv_mix, w = _mixer_fwd(h1, functools.partial(wget, l), rel_bias, sinks[l], bidx)
        x_mid, h2 = _post_pre_fwd("post_attn_norm", x_in, mo, attn_post_norm[l:l + 1], ffn_pre_norm[l:l + 1])
        w.update(wget(l, 2, h2))
        dn, sv_ffn = _ffn_fwd(h2, w)
        g_next = attn_pre_norm[l + 1:l + 2] if l + 1 < DEPTH else None
        x_out, h1_next = _post_pre_fwd("post_ffn_norm" if l + 1 < DEPTH else "post_ffn_norm_last", x_mid, dn,
                                       ffn_post_norm[l:l + 1], g_next)
        saved.append(dict(x_in=x_in, h1=h1, mo=mo, x_mid=x_mid, h2=h2, dn=dn, mix=sv_mix, ffn=sv_ffn))
        layers.append(w)
        x_in, h1 = x_out, h1_next

    loss_row, dres = _loss_kernel(x_in, target)

    small = [None] * DEPTH
    stats = jnp.zeros((N_BAND_Q, 8, 128), F32)
    dh_next = None
    for l in reversed(range(DEPTH)):
        w, sv = layers[l], saved[l]
        if l + 1 < DEPTH:
            pre = (saved[l + 1]["x_in"], attn_pre_norm[l + 1:l + 2] + zero, dh_next)
            dres, d_dn, dg_pre_next, dg_fpost = _norm_bwd("post_ffn_norm_bwd", dres, pre,
                                                          (sv["dn"], ffn_post_norm[l:l + 1]))
            small[l + 1]["attn_pre_norm"] = dg_pre_next
        else:
            dres, d_dn, _, dg_fpost = _norm_bwd("post_ffn_norm_last_bwd", dres, None, (sv["dn"], ffn_post_norm[l:l + 1]))
        dh2, g_ffn = _ffn_bwd(d_dn, sv["h2"], w, sv["ffn"])
        zero = emit(l, 0, g_ffn)
        dres, d_mo, dg_fpre, dg_apost = _norm_bwd("post_attn_norm_bwd", dres,
                                                  (sv["x_mid"], ffn_pre_norm[l:l + 1] + zero, dh2),
                                                  (sv["mo"], attn_post_norm[l:l + 1]))
        dh_next, g_mix, stats, zero = _mixer_bwd(d_mo, sv["h1"], w, sv["mix"], rel_bias, sinks[l], bidx, stats,
                                                 functools.partial(emit, l))
        small[l] = dict(ffn_post_norm=dg_fpost, ffn_pre_norm=dg_fpre, attn_post_norm=dg_apost,
                        sinks=stats[N_A:, 1, 0], conv_b=g_ffn["conv_b"], b_gate=g_mix["b_gate"], conv_w=g_ffn["conv_w"])
    grad_x, _, dg_pre0, _ = _norm_bwd("pre_norm_first_bwd", dres, (saved[0]["x_in"], attn_pre_norm[0:1] + zero, dh_next),
                                      None)
    small[0]["attn_pre_norm"] = dg_pre0
    return loss_row, grad_x, small, stats


def _reduce_and_update(x_shape, names, weights, mom1, mom2, chip, core, pending, loss_row, grad_x, small, stats):
    delta, new_m, new_v, grads = {}, {}, {}, {}

    def update(n, g):
        grads[n], delta[n], new_m[n], new_v[n] = _adamw("adamw_" + n, _shard_view(n, weights[n]), g,
                                                        _shard_view(n, mom1[n]), _shard_view(n, mom2[n]))

    small_vals = {n: jnp.stack([small[l][n].reshape(weights[n].shape[1:]) for l in range(DEPTH)])
                  for n in ("attn_pre_norm", "attn_post_norm", "ffn_pre_norm", "ffn_post_norm", "conv_b", "sinks")}
    small_vals["b_gate"] = jnp.stack([small[l]["b_gate"] for l in range(DEPTH)])
    small_vals["conv_w"] = jnp.stack([small[l]["conv_w"] for l in range(DEPTH)])
    small_vals["rel_bias"] = stats[:, 0, :NUM_BUCKETS].T
    small_vals["loss"] = loss_row[0, :1]
    shapes = {n: v.shape for n, v in small_vals.items()}
    small_send, small_recv, packed, small_land, started = _small_start(_pack_small(small_vals))

    summed = {}

    def finish(which, after):
        for l, group, send, recv, gs, lands in pending:
            if (group == ("w_in",)) == which:
                gs, lands = _reduce_wait("reduce_wait_%d_%s" % (l, group[0]), send, recv, gs, lands, after)
                for n, g, land in zip(group, gs, lands):
                    summed[n] = _reduce_sum("reduce_sum_%d_%s" % (l, n), g, land, l, summed.get(n), chip, core)

    finish(False, started)
    early = [n for n in BIG if n != "w_in"]
    for n, g in zip(early, _join_halves("join_halves", [summed[n] for n in early])):
        update(n, g)
    finish(True, delta[early[-1]])
    update("w_in", _join_halves("join_halves_w_in", [summed["w_in"]])[0])

    packed, small_land = _small_wait(small_send, small_recv, packed, small_land, delta["w_in"])
    small_land = lax.dynamic_update_slice(small_land, packed[None], (2 * chip + core, jnp.int32(0), jnp.int32(0)))
    reduced = _unpack_small(_small_sum(small_land), shapes)
    reduced["b_gate"] = lax.dynamic_slice_in_dim(reduced["b_gate"], chip * (D // N_CHIPS), D // N_CHIPS, axis=2)
    reduced["conv_w"] = lax.dynamic_slice_in_dim(reduced["conv_w"], chip * (2 * D_FF // N_CHIPS), 2 * D_FF // N_CHIPS, axis=2)
    for n in names:
        if n not in grads:
            update(n, reduced[n].reshape(weights[n].shape))
    for out in (grads, delta, new_m, new_v):
        out["w_in"] = _shard_view("w_in", out["w_in"])

    loss = reduced["loss"].reshape(())
    return (loss, grad_x.reshape(x_shape), *[grads[n] for n in names], *[delta[n] for n in names],
            *[new_m[n] for n in names], *[new_v[n] for n in names])
```

```python
import functools
import math

import numpy as np
import jax
import jax.numpy as jnp
from jax import lax
from jax.experimental import pallas as pl
from jax.experimental.pallas import tpu as pltpu

F32 = jnp.float32
BF16 = jnp.bfloat16

S = 2048
D = 1024
DEPTH = 2
HD = 64
BLK = 128
NQB = S // BLK
A_GROUPS = ((128, 1), (512, 4), (2048, 16))
N_BAND_Q = 20
N_A = 12
NUM_BUCKETS = 32
MAX_DISTANCE = 2048
D_FF = 4096
IN_COLS = 6912
IN_SHARD = IN_COLS // 4
OFF_GATE = 3840
EPS = 1e-6
SCALE = HD ** -0.5
NEG = -1e30
N_CHIPS = 4
N_DEV = 8

ADAM_LR = 0.001
ADAM_B1 = 0.9
ADAM_B2 = 0.999
ADAM_EPS = 1e-08
ADAM_WD = 0.01
ADAM_STEP = 10

VMEM_LIMIT = 56 * 1024 * 1024

NN = (((1,), (0,)), ((), ()))
NT = (((1,), (1,)), ((), ()))
TN = (((0,), (0,)), ((), ()))

MESH = pl.DeviceIdType.MESH
ANY = pl.BlockSpec(memory_space=pl.ANY)


def _dot(a, b, dims):
    return lax.dot_general(a, b, dims, preferred_element_type=F32)


def _params(sem):
    return pltpu.CompilerParams(dimension_semantics=sem, vmem_limit_bytes=VMEM_LIMIT)


def _matmul(name, a, b, out_shape, out_dtype, grid, a_spec, b_spec, o_spec, dims, acc_shape):
    nk = grid[-1]

    def body(a_ref, b_ref, o_ref, *scratch):
        part = _dot(a_ref[...].astype(BF16), b_ref[...].astype(BF16), dims)
        if nk == 1:
            o_ref[...] = part.astype(o_ref.dtype)
            return
        acc_ref, = scratch
        k = pl.program_id(len(grid) - 1)

        @pl.when(k == 0)
        def _():
            acc_ref[...] = part

        @pl.when(k > 0)
        def _():
            acc_ref[...] += part

        @pl.when(k == nk - 1)
        def _():
            o_ref[...] = acc_ref[...].astype(o_ref.dtype)

    scratch = [] if nk == 1 else [pltpu.VMEM(acc_shape, F32)]
    sem = ("parallel",) * (len(grid) - 1) + ("arbitrary",)
    return pl.pallas_call(
        body, name=name, grid=grid, in_specs=[a_spec, b_spec], out_specs=o_spec,
        out_shape=jax.ShapeDtypeStruct(out_shape, out_dtype), scratch_shapes=scratch,
        compiler_params=_params(sem))(a, b)


FULL_K = 8192


def _mm_tn_sharded(name, a, b, row_sharded, tm=512, tn=512, tk=FULL_K):
    k, m = a.shape
    n = b.shape[1]
    m4, n4 = (m // N_CHIPS, n) if row_sharded else (m, n // N_CHIPS)
    tm, tn, tk = min(tm, m4), min(tn, n4), min(tk, k)
    per_m, per_n = m4 // tm, n4 // tn
    if row_sharded:
        o_map = lambda i, j, l: (i // per_m, i % per_m, j)
    else:
        o_map = lambda i, j, l: (j // per_n, i, j % per_n)
    return _matmul(name, a, b, (N_CHIPS, m4, n4), BF16, (m // tm, n // tn, k // tk),
                   pl.BlockSpec((tk, tm), lambda i, j, l: (l, i)),
                   pl.BlockSpec((tk, tn), lambda i, j, l: (l, j)),
                   pl.BlockSpec((None, tm, tn), o_map), TN, (tm, tn))


def _mm_nn(name, a, b, out_dtype, tm=512, tn=512, tk=FULL_K):
    m, k = a.shape
    n = b.shape[1]
    tm, tn, tk = min(tm, m), min(tn, n), min(tk, k)
    return _matmul(name, a, b, (m, n), out_dtype, (m // tm, n // tn, k // tk),
                   pl.BlockSpec((tm, tk), lambda i, j, l: (i, l)),
                   pl.BlockSpec((tk, tn), lambda i, j, l: (l, j)),
                   pl.BlockSpec((tm, tn), lambda i, j, l: (i, j)), NN, (tm, tn))


def _mm_nt(name, a, b, out_dtype, tm=512, tn=512, tk=FULL_K):
    m, k = a.shape
    n = b.shape[0]
    tm, tn, tk = min(tm, m), min(tn, n), min(tk, k)
    return _matmul(name, a, b, (m, n), out_dtype, (m // tm, n // tn, k // tk),
                   pl.BlockSpec((tm, tk), lambda i, j, l: (i, l)),
                   pl.BlockSpec((tn, tk), lambda i, j, l: (j, l)),
                   pl.BlockSpec((tm, tn), lambda i, j, l: (i, j)), NT, (tm, tn))


def _mm_tn(name, a, b, out_dtype, tm=512, tn=512, tk=FULL_K):
    k, m = a.shape
    n = b.shape[1]
    tm, tn, tk = min(tm, m), min(tn, n), min(tk, k)
    return _matmul(name, a, b, (m, n), out_dtype, (m // tm, n // tn, k // tk),
                   pl.BlockSpec((tk, tm), lambda i, j, l: (l, i)),
                   pl.BlockSpec((tk, tn), lambda i, j, l: (l, j)),
                   pl.BlockSpec((tm, tn), lambda i, j, l: (i, j)), TN, (tm, tn))


TR = 512


def _row_spec(width=D):
    return pl.BlockSpec((TR, width), lambda i: (i, 0))


def _vec_spec(width=D):
    return pl.BlockSpec((1, width), lambda i: (0, 0))


def _rms(x, g):
    r = lax.rsqrt(jnp.mean(x * x, axis=-1, keepdims=True) + EPS)
    return x * r * g


def _rms_fwd(name, x, g):
    def body(x_ref, g_ref, h_ref):
        h_ref[...] = _rms(x_ref[...], g_ref[...]).astype(BF16)

    return pl.pallas_call(
        body, name=name, grid=(S // TR,), in_specs=[_row_spec(), _vec_spec()], out_specs=_row_spec(),
        out_shape=jax.ShapeDtypeStruct((S, D), BF16), compiler_params=_params(("parallel",)))(x, g)


def _post_pre_fwd(name, x, y, g_post, g_pre):
    has_pre = g_pre is not None

    def body(*refs):
        if has_pre:
            x_ref, y_ref, gp_ref, gn_ref, xn_ref, h_ref = refs
        else:
            x_ref, y_ref, gp_ref, xn_ref = refs
        xn = x_ref[...] + _rms(y_ref[...], gp_ref[...])
        xn_ref[...] = xn
        if has_pre:
            h_ref[...] = _rms(xn, gn_ref[...]).astype(BF16)

    ins = [x, y, g_post] + ([g_pre] if has_pre else [])
    in_specs = [_row_spec(), _row_spec(), _vec_spec()] + ([_vec_spec()] if has_pre else [])
    out_shape = [jax.ShapeDtypeStruct((S, D), F32)] + ([jax.ShapeDtypeStruct((S, D), BF16)] if has_pre else [])
    out_specs = [_row_spec()] + ([_row_spec()] if has_pre else [])
    out = pl.pallas_call(
        body, name=name, grid=(S // TR,), in_specs=in_specs, out_specs=out_specs, out_shape=out_shape,
        compiler_params=_params(("parallel",)))(*ins)
    return out if has_pre else (out[0], None)


def _rms_bwd_math(x, g, dy):
    r = lax.rsqrt(jnp.mean(x * x, axis=-1, keepdims=True) + EPS)
    n = x * r
    dn = dy * g
    dx = r * (dn - n * jnp.mean(dn * n, axis=-1, keepdims=True))
    return dx, jnp.sum(dy * n, axis=0, keepdims=True)


def _norm_bwd(name, dres, pre=None, post=None):
    has_pre, has_post = pre is not None, post is not None

    def body(*refs):
        refs = list(refs)
        dres_ref = refs.pop(0)
        if has_pre:
            xn_ref, gn_ref, dh_ref = refs[:3]
            refs = refs[3:]
        if has_post:
            y_ref, gp_ref = refs[:2]
            refs = refs[2:]
        dxn_ref = refs.pop(0)
        dy_ref = refs.pop(0) if has_post else None
        dgn_ref = refs.pop(0) if has_pre else None
        dgp_ref = refs.pop(0) if has_post else None
        first = pl.program_id(0) == 0
        dxn = dres_ref[...]
        if has_pre:
            dx, dg = _rms_bwd_math(xn_ref[...], gn_ref[...], dh_ref[...])
            dxn = dxn + dx

            @pl.when(first)
            def _():
                dgn_ref[...] = dg

            @pl.when(jnp.logical_not(first))
            def _():
                dgn_ref[...] += dg
        dxn_ref[...] = dxn
        if has_post:
            dy, dg = _rms_bwd_math(y_ref[...], gp_ref[...], dxn)
            dy_ref[...] = dy.astype(BF16)

            @pl.when(first)
            def _():
                dgp_ref[...] = dg

            @pl.when(jnp.logical_not(first))
            def _():
                dgp_ref[...] += dg

    ins, in_specs = [dres], [_row_spec()]
    if has_pre:
        ins += list(pre)
        in_specs += [_row_spec(), _vec_spec(), _row_spec()]
    if has_post:
        ins += list(post)
        in_specs += [_row_spec(), _vec_spec()]
    out_shape, out_specs = [jax.ShapeDtypeStruct((S, D), F32)], [_row_spec()]
    if has_post:
        out_shape.append(jax.ShapeDtypeStruct((S, D), BF16))
        out_specs.append(_row_spec())
    for _ in range(int(has_pre) + int(has_post)):
        out_shape.append(jax.ShapeDtypeStruct((1, D), F32))
        out_specs.append(_vec_spec())
    out = list(pl.pallas_call(
        body, name=name, grid=(S // TR,), in_specs=in_specs, out_specs=out_specs, out_shape=out_shape,
        compiler_params=_params(("arbitrary",)))(*ins))
    dxn = out.pop(0)
    dy = out.pop(0) if has_post else None
    dgn = out.pop(0) if has_pre else None
    dgp = out.pop(0) if has_post else None
    return dxn, dy, dgn, dgp


def _loss_kernel(y, target):
    def body(y_ref, t_ref, loss_ref, dy_ref):
        e = y_ref[...] - t_ref[...]
        dy_ref[...] = e * (1.0 / D)
        part = jnp.zeros((1, 128), F32) + 0.5 * jnp.sum(jnp.mean(e * e, axis=-1, keepdims=True))

        @pl.when(pl.program_id(0) == 0)
        def _():
            loss_ref[...] = part

        @pl.when(pl.program_id(0) > 0)
        def _():
            loss_ref[...] += part

    return pl.pallas_call(
        body, name="loss", grid=(S // TR,), in_specs=[_row_spec(), _row_spec()],
        out_specs=[_vec_spec(128), _row_spec()],
        out_shape=[jax.ShapeDtypeStruct((1, 128), F32), jax.ShapeDtypeStruct((S, D), F32)],
        compiler_params=_params(("arbitrary",)))(y, target)


def _t5_bucket_np(dist):
    max_exact = NUM_BUCKETS // 2
    nf = np.maximum(dist, 1).astype(np.float32)
    large = max_exact + (np.log(nf / max_exact) / np.float32(math.log(MAX_DISTANCE / max_exact))
                         * (NUM_BUCKETS - max_exact)).astype(np.int32)
    large = np.minimum(large, NUM_BUCKETS - 1)
    return np.where(dist < max_exact, dist, large).astype(np.int32)


def _bucket_maps():
    a = np.arange(BLK)[:, None]
    b = np.arange(2 * BLK)[None, :]
    dist = np.maximum(a + BLK - b, 0)
    maps = [_t5_bucket_np(dist * d) for _, d in A_GROUPS] + [_t5_bucket_np(dist)]
    return np.stack(maps).astype(np.int32)


def _pair_spec(col0):
    return pl.BlockSpec((S, 128), lambda p: (0, col0 + p))


def _band_rows(i, d):
    nb = S // d // BLK
    r, b = i // nb, i % nb
    cur = pl.ds(b * BLK * d + r, BLK, stride=d)
    prev = pl.ds(jnp.maximum(b - 1, 0) * BLK * d + r, BLK, stride=d)
    return cur, prev, jnp.minimum(b, 1)


def _band_bias(tab_ref, bi, h):
    bias = jnp.zeros((BLK, 2 * BLK), F32)
    for kk in range(NUM_BUCKETS):
        bias = jnp.where(bi == kk, tab_ref[kk, h], bias)
    return bias


def _lane_lo(rows=BLK):
    return lax.broadcasted_iota(jnp.int32, (rows, 128), 1) < HD


def _per_head(x, lo):
    return (jnp.sum(jnp.where(lo, x, 0.0), axis=1, keepdims=True) * (1.0 / HD),
            jnp.sum(jnp.where(lo, 0.0, x), axis=1, keepdims=True) * (1.0 / HD))


def _band_fill(bias_ref, tab_ref, bi, head, maxd):
    a = lax.broadcasted_iota(jnp.int32, (BLK, 2 * BLK), 0)
    c = lax.broadcasted_iota(jnp.int32, (BLK, 2 * BLK), 1)
    dist = a + BLK - c
    in_band = jnp.logical_and(dist >= 0, dist <= maxd)
    for h in range(2):
        bias = jnp.where(in_band, _band_bias(tab_ref, bi, head + h), NEG)
        bias_ref[1, h * BLK:(h + 1) * BLK, :] = bias
        bias_ref[0, h * BLK:(h + 1) * BLK, :] = jnp.where(c >= BLK, bias, NEG)


def _stack_heads(x, lo, dtype=BF16):
    return jnp.concatenate([jnp.where(lo, x, 0.0), jnp.where(lo, 0.0, x)], axis=0).astype(dtype)


def _unstack_heads(x, lo):
    n = x.shape[0] // 2
    return jnp.where(lo, x[:n], x[n:])


def _stack_rows(ref, prev, cur):
    return jnp.concatenate([ref[prev, :], ref[cur, :]], axis=0).astype(BF16)


def _blocks_of_group(group, dils, block):
    def run(d):
        lax.fori_loop(0, NQB, functools.partial(block, d), 0, unroll=2)

    if len(dils) == 1:
        run(dils[0])
        return
    for g, d in enumerate(dils):
        pl.when(group == g)(functools.partial(run, d))


def _band_fwd(name, dils, n_pairs, maxd, head0, srcs, bidx_g, tab, sinks):
    (qa, qc), (ka, kc), (va, vc) = srcs
    per_group = n_pairs // len(dils)
    out_spec = _pair_spec(0)
    smem = pl.BlockSpec(memory_space=pltpu.SMEM)
    full = pl.BlockSpec((len(dils), BLK, 2 * BLK), lambda p: (0, 0, 0))

    def body(tab_ref, sink_ref, q_ref, k_ref, v_ref, bidx_ref, o_ref, lse_ref, bias_ref):
        p = pl.program_id(0)
        _band_fill(bias_ref, tab_ref, bidx_ref[p // per_group], head0 + 2 * p, maxd)
        lo = _lane_lo()
        sink = jnp.where(lax.broadcasted_iota(jnp.int32, (2 * BLK, 1), 0) < BLK, sink_ref[2 * p], sink_ref[2 * p + 1])

        def block(d, i, carry):
            cur, prev, has_prev = _band_rows(i, d)
            qs = _stack_heads(q_ref[cur, :] * SCALE, lo)
            ks, vs = _stack_rows(k_ref, prev, cur), _stack_rows(v_ref, prev, cur)
            s = _dot(qs, ks, NT) + bias_ref[has_prev]
            m = jnp.max(s, axis=1, keepdims=True)
            pr = jnp.exp(s - m)
            l = jnp.sum(pr, axis=1, keepdims=True)
            num = _dot(pr.astype(BF16), vs, NN)
            lse = m + jnp.log(l)
            sig = 1.0 / (1.0 + jnp.exp(sink - lse))
            o_ref[cur, :] = _unstack_heads(num * (sig / l), lo)
            lse_ref[cur, :] = _unstack_heads(lse + jnp.zeros((2 * BLK, 128), F32), lo)
            return carry

        _blocks_of_group(p // per_group, dils, block)

    shape = jax.ShapeDtypeStruct((S, n_pairs * 128), F32)
    return pl.pallas_call(
        body, name=name, grid=(n_pairs,),
        in_specs=[smem, smem, _pair_spec(qc), _pair_spec(kc), _pair_spec(vc), full],
        out_specs=[out_spec, out_spec], out_shape=[shape, shape],
        scratch_shapes=[pltpu.VMEM((2, 2 * BLK, 2 * BLK), F32)],
        compiler_params=_params(("parallel",)))(tab, sinks, qa, ka, va, bidx_g)


def _band_bwd(name, dils, n_pairs, maxd, head0, srcs, bidx_g, tab, sinks, o, lse, do, stats_in):
    (qa, qc), (ka, kc), (va, vc) = srcs
    per_group = n_pairs // len(dils)
    pair = _pair_spec(0)
    shared = pl.BlockSpec((S, 128), lambda p: (0, p % per_group))
    smem = pl.BlockSpec(memory_space=pltpu.SMEM)
    full = pl.BlockSpec((len(dils), BLK, 2 * BLK), lambda p: (0, 0, 0))
    stat_spec = pl.BlockSpec((2, 8, 128), lambda p: (p, 0, 0))

    def body(tab_ref, sink_ref, q_ref, k_ref, v_ref, bidx_ref, o_ref, lse_ref, do_ref, sin_ref,
             dq_ref, dk_ref, dv_ref, stat_ref, bias_ref, dsacc_ref, sk_ref):
        p = pl.program_id(0)
        _band_fill(bias_ref, tab_ref, bidx_ref[p // per_group], head0 + 2 * p, maxd)
        dsacc_ref[...] = jnp.zeros_like(dsacc_ref)
        sk_ref[...] = jnp.zeros_like(sk_ref)
        dk_ref[...] = jnp.zeros_like(dk_ref)
        dv_ref[...] = jnp.zeros_like(dv_ref)
        lo = _lane_lo()
        head1 = lax.broadcasted_iota(jnp.int32, (2 * BLK, 1), 0) >= BLK
        sink = jnp.where(head1, sink_ref[2 * p + 1], sink_ref[2 * p])

        def block(d, i, carry):
            cur, prev, has_prev = _band_rows(i, d)
            qs = _stack_heads(q_ref[cur, :] * SCALE, lo)
            ks, vs = _stack_rows(k_ref, prev, cur), _stack_rows(v_ref, prev, cur)
            do = do_ref[cur, :]
            dos = _stack_heads(do, lo, F32)
            lse = jnp.concatenate(_per_head(lse_ref[cur, :], lo), axis=0)
            prod = do * o_ref[cur, :]
            delta = jnp.concatenate([jnp.sum(jnp.where(lo, prod, 0.0), axis=1, keepdims=True),
                                     jnp.sum(jnp.where(lo, 0.0, prod), axis=1, keepdims=True)], axis=0)
            sig = 1.0 / (1.0 + jnp.exp(sink - lse))
            pr = jnp.exp(_dot(qs, ks, NT) + bias_ref[has_prev] - lse)
            ds = pr * (sig * (_dot(dos.astype(BF16), vs, NT) - delta))
            dsb = ds.astype(BF16)
            dq_ref[cur, :] = SCALE * _unstack_heads(_dot(dsb, ks, NN), lo)
            dk = _dot(dsb, qs, TN)
            dv = _dot(pr.astype(BF16), (sig * dos).astype(BF16), TN)
            dk_ref[prev, :] += dk[:BLK]
            dk_ref[cur, :] += dk[BLK:]
            dv_ref[prev, :] += dv[:BLK]
            dv_ref[cur, :] += dv[BLK:]
            dsacc_ref[...] += ds
            sink_grad = -delta * (1.0 - sig)
            for h in range(2):
                sk_ref[h] += jnp.zeros((8, 128), F32) + jnp.sum(sink_grad[h * BLK:(h + 1) * BLK])
            return carry

        _blocks_of_group(p // per_group, dils, block)

        bi = bidx_ref[p // per_group]
        lane = lax.broadcasted_iota(jnp.int32, (8, 128), 1)
        sub = lax.broadcasted_iota(jnp.int32, (8, 128), 0)
        for h in range(2):
            acc = dsacc_ref[h * BLK:(h + 1) * BLK, :]
            row = jnp.where(jnp.logical_and(sub == 1, lane == 0), sk_ref[h], 0.0)
            for kk in range(NUM_BUCKETS):
                tot = jnp.sum(jnp.where(bi == kk, acc, 0.0))
                row = jnp.where(jnp.logical_and(sub == 0, lane == kk), tot, row)
            stat_ref[h] = row + jnp.where(sub == 0, sin_ref[h], 0.0)

    shape = jax.ShapeDtypeStruct((S, n_pairs * 128), F32)
    return pl.pallas_call(
        body, name=name, grid=(n_pairs,),
        in_specs=[smem, smem, _pair_spec(qc), _pair_spec(kc), _pair_spec(vc), full, shared, shared, shared, stat_spec],
        out_specs=[pair, pair, pair, stat_spec],
        out_shape=[shape, shape, shape, jax.ShapeDtypeStruct((2 * n_pairs, 8, 128), F32)],
        scratch_shapes=[pltpu.VMEM((2, 2 * BLK, 2 * BLK), F32), pltpu.VMEM((2 * BLK, 2 * BLK), F32),
                        pltpu.VMEM((2, 8, 128), F32)],
        compiler_params=_params(("parallel",)))(tab, sinks, qa, ka, va, bidx_g, o, lse, do, stats_in)


def _comb_fwd(o_g, lse_g):
    def body(o0, o1, o2, l0, l1, l2, out_ref, outb_ref, lse_ref):
        a0, a1, a2 = l0[...], l1[...], l2[...]
        m = jnp.maximum(jnp.maximum(a0, a1), a2)
        e0, e1, e2 = jnp.exp(a0 - m), jnp.exp(a1 - m), jnp.exp(a2 - m)
        tot = e0 + e1 + e2
        out = (e0 * o0[...] + e1 * o1[...] + e2 * o2[...]) / tot
        out_ref[...] = out
        outb_ref[...] = out.astype(BF16)
        lse_ref[...] = m + jnp.log(tot)

    spec = _row_spec(4 * HD)
    groups = [pl.BlockSpec((TR, 4 * HD), functools.partial(lambda i, g: (i, g), g=g)) for g in range(len(A_GROUPS))]
    f32 = jax.ShapeDtypeStruct((S, 4 * HD), F32)
    return pl.pallas_call(
        body, name="comb_fwd", grid=(S // TR,), in_specs=groups + groups, out_specs=[spec] * 3,
        out_shape=[f32, jax.ShapeDtypeStruct((S, 4 * HD), BF16), f32],
        compiler_params=_params(("parallel",)))(o_g, o_g, o_g, lse_g, lse_g, lse_g)


def _split2(x):
    hi = x.astype(BF16)
    return hi, (x - hi.astype(F32)).astype(BF16)


KB = 2 * BLK
SBQ = 2 * BLK


def _tri_sum(x, tri):
    hi, lo = _split2(x)
    both = _dot(jnp.concatenate([hi, lo], axis=0), tri, NN)
    return both[:x.shape[0]] + both[x.shape[0]:]


def _tri(strict_upper):
    r = lax.broadcasted_iota(jnp.int32, (KB, KB), 0)
    c = lax.broadcasted_iota(jnp.int32, (KB, KB), 1)
    return jnp.where(r > c if strict_upper else r < c, 1.0, 0.0).astype(BF16)


def _sb_terms(qs, kj, before):
    z = _dot(qs, kj, NT)
    lsp = jnp.minimum(z, 0.0) - jnp.log(1.0 + jnp.exp(-jnp.abs(z)))
    return lsp, _sb_keep(before, lsp - z)


def _sb_keep(before, x):
    return x if before is None else jnp.where(before, x, 0.0)


def _sb_before(i, m):
    t = (lax.broadcasted_iota(jnp.int32, (2 * SBQ, KB), 0) & (SBQ - 1)) + i * SBQ
    s = lax.broadcasted_iota(jnp.int32, (2 * SBQ, KB), 1) + m * KB
    return s < t


C_COL = 3072 // 128


def _sb_fwd(proj):
    blk = lambda off: pl.BlockSpec((SBQ, 128), lambda p, i: (i, off + p))
    col = lambda off: pl.BlockSpec((S, 128), lambda p, i: (0, off + p))
    out = pl.BlockSpec((SBQ, 128), lambda p, i: (i, p))

    def body(q_ref, k_ref, v_ref, o_ref, ob_ref, tot_ref):
        i = pl.program_id(1)
        lo = _lane_lo(SBQ)
        qs = _stack_heads(q_ref[...] * SCALE, lo)
        suffix = _tri(True)

        def step(n, carry, diagonal=False):
            acc, rest = carry
            m = i - n
            rows = pl.ds(pl.multiple_of(m * KB, KB), KB)
            kj, vj = k_ref[rows, :].astype(BF16), v_ref[rows, :].astype(BF16)
            before = _sb_before(i, m) if diagonal else None
            lsp, lk = _sb_terms(qs, kj, before)
            w = _sb_keep(before, jnp.exp(lsp + _tri_sum(lk, suffix) + rest))
            return acc + _dot(w.astype(BF16), vj, NN), rest + jnp.sum(lk, axis=1, keepdims=True)

        first = step(0, (jnp.zeros((2 * SBQ, 128), F32), jnp.zeros((2 * SBQ, 1), F32)), diagonal=True)
        acc, rest = lax.fori_loop(1, i + 1, step, first)
        o = _unstack_heads(acc, lo)
        o_ref[...] = o
        ob_ref[...] = o.astype(BF16)
        tot_ref[...] = _unstack_heads(rest + jnp.zeros((2 * SBQ, 128), F32), lo)

    f32 = jax.ShapeDtypeStruct((S, 4 * HD), F32)
    return pl.pallas_call(
        body, name="sb_fwd", grid=(2, S // SBQ), in_specs=[blk(C_COL), col(C_COL + 2), col(C_COL + 4)],
        out_specs=[out, out, out], out_shape=[f32, jax.ShapeDtypeStruct((S, 4 * HD), BF16), f32],
        compiler_params=_params(("parallel", "arbitrary")))(proj, proj, proj)


def _sb_bwd(proj, tot, do):
    blk = lambda off: pl.BlockSpec((SBQ, 128), lambda p, i: (i, off + p))
    col = lambda off: pl.BlockSpec((S, 128), lambda p, i: (0, off + p))

    def body(q_ref, k_ref, v_ref, tot_ref, do_ref, dq_ref, dk_ref, dv_ref):
        i = pl.program_id(1)

        @pl.when(i == 0)
        def _():
            dk_ref[...] = jnp.zeros_like(dk_ref)
            dv_ref[...] = jnp.zeros_like(dv_ref)

        lo = _lane_lo(SBQ)
        qs = _stack_heads(q_ref[...] * SCALE, lo)
        dos = _stack_heads(do_ref[...], lo)
        tots = jnp.concatenate(_per_head(tot_ref[...], lo), axis=0)
        prefix = _tri(False)

        def step(m, carry, diagonal=False):
            dq, keep_left, g_left = carry
            rows = pl.ds(pl.multiple_of(m * KB, KB), KB)
            kj, vj = k_ref[rows, :].astype(BF16), v_ref[rows, :].astype(BF16)
            before = _sb_before(i, m) if diagonal else None
            lsp, lk = _sb_terms(qs, kj, before)
            log_rest = tots - keep_left - lk - _tri_sum(lk, prefix)
            w = _sb_keep(before, jnp.exp(lsp + log_rest))
            g = w * _dot(dos, vj, NT)
            g_before = g_left + _dot(g.astype(BF16), prefix, NN)
            beta = jnp.exp(lsp)
            dz = _sb_keep(before, g * (1.0 - beta) - g_before * beta).astype(BF16)
            dk_ref[rows, :] += _dot(dz, qs, TN)
            dv_ref[rows, :] += _dot(w.astype(BF16), dos, TN)
            return (dq + _dot(dz, kj, NN), keep_left + jnp.sum(lk, axis=1, keepdims=True),
                    g_left + jnp.sum(g, axis=1, keepdims=True))

        zero = (jnp.zeros((2 * SBQ, 128), F32), jnp.zeros((2 * SBQ, 1), F32), jnp.zeros((2 * SBQ, 1), F32))
        dq, _, _ = step(i, lax.fori_loop(0, i, step, zero), diagonal=True)
        dq_ref[...] = SCALE * _unstack_heads(dq, lo)

    out_blk = pl.BlockSpec((SBQ, 128), lambda p, i: (i, p))
    out_col = pl.BlockSpec((S, 128), lambda p, i: (0, p))
    f32 = jax.ShapeDtypeStruct((S, 4 * HD), F32)
    return pl.pallas_call(
        body, name="sb_bwd", grid=(2, S // SBQ),
        in_specs=[blk(C_COL), col(C_COL + 2), col(C_COL + 4), out_blk, out_blk],
        out_specs=[out_blk, out_col, out_col], out_shape=[f32, f32, f32],
        compiler_params=_params(("arbitrary", "arbitrary")))(proj, proj, proj, tot, do)


TG = 256
TGR = 1024
GATE_BLK0 = OFF_GATE // TG


def _gate_specs():
    grid = (D // TG, S // TGR)
    p_specs = [pl.BlockSpec((TGR, TG), functools.partial(lambda c, r, br: (r, GATE_BLK0 + br * (D // TG) + c), br=br))
               for br in range(3)]
    b_spec = pl.BlockSpec((3, TG), lambda c, r: (0, c))
    t_spec = pl.BlockSpec((TGR, TG), lambda c, r: (r, c))
    return grid, p_specs, b_spec, t_spec


def _sigmoid(x):
    return 1.0 / (1.0 + jnp.exp(-x))


def _three_rows(rows):
    sub = lax.broadcasted_iota(jnp.int32, (3, rows[0].shape[1]), 0)
    return jnp.where(sub == 0, rows[0], jnp.where(sub == 1, rows[1], rows[2]))


def _gate_fwd(proj, b_gate, br):
    grid, p_specs, b_spec, t_spec = _gate_specs()

    def body(p0, p1, p2, b_ref, r0, r1, r2, out_ref):
        acc = jnp.zeros((TGR, TG), F32)
        for n, (p, r) in enumerate(((p0, r0), (p1, r1), (p2, r2))):
            acc += _sigmoid(p[...] + b_ref[n:n + 1, :]) * r[...]
        out_ref[...] = acc.astype(BF16)

    return pl.pallas_call(
        body, name="gate_fwd", grid=grid, in_specs=p_specs + [b_spec] + [t_spec] * 3, out_specs=t_spec,
        out_shape=jax.ShapeDtypeStruct((S, D), BF16),
        compiler_params=_params(("parallel", "parallel")))(proj, proj, proj, b_gate, *br)


def _gate_bwd(proj, b_gate, br, dmerged):
    grid, p_specs, b_spec, t_spec = _gate_specs()

    def body(p0, p1, p2, b_ref, r0, r1, r2, dm_ref, e0, e1, e2, g0, g1, g2, db_ref):
        dm = dm_ref[...]
        rows = []
        for n, (p, r, e_ref, dg_ref) in enumerate(((p0, r0, e0, g0), (p1, r1, e1, g1), (p2, r2, e2, g2))):
            g = _sigmoid(p[...] + b_ref[n:n + 1, :])
            e_ref[...] = (dm * g).astype(BF16)
            dpre = dm * r[...] * g * (1.0 - g)
            dg_ref[...] = dpre.astype(BF16)
            rows.append(jnp.sum(dpre, axis=0, keepdims=True))
        db = _three_rows(rows)

        @pl.when(pl.program_id(1) == 0)
        def _():
            db_ref[...] = db

        @pl.when(pl.program_id(1) > 0)
        def _():
            db_ref[...] += db

    bf = jax.ShapeDtypeStruct((S, D), BF16)
    out = pl.pallas_call(
        body, name="gate_bwd", grid=grid, in_specs=p_specs + [b_spec] + [t_spec] * 4,
        out_specs=[t_spec] * 6 + [b_spec], out_shape=[bf] * 6 + [jax.ShapeDtypeStruct((3, D), F32)],
        compiler_params=_params(("parallel", "arbitrary")))(proj, proj, proj, b_gate, *br, dmerged)
    return out[:3], out[3:6], out[6]


TC = 256
N_FF_BLK = D_FF // TC
GELU_C = math.sqrt(2.0 / math.pi)


def _shift_down(x, n):
    rows = lax.broadcasted_iota(jnp.int32, x.shape, 0)
    return jnp.where(rows >= n, pltpu.roll(x, n, axis=0), 0.0)


def _shift_up(x, n):
    rows = lax.broadcasted_iota(jnp.int32, x.shape, 0)
    return jnp.where(rows < x.shape[0] - n, pltpu.roll(x, x.shape[0] - n, axis=0), 0.0)


def _conv(u, w, b):
    s1, s2 = _shift_down(u, 1), _shift_down(u, 2)
    return w[2:3, :] * u + w[1:2, :] * s1 + w[0:1, :] * s2 + b, s1, s2


def _gelu_parts(x):
    inner = GELU_C * (x + 0.044715 * x * x * x)
    t = jnp.tanh(inner)
    gelu = 0.5 * x * (1.0 + t)
    dgelu = 0.5 * (1.0 + t) + 0.5 * x * (1.0 - t * t) * GELU_C * (1.0 + 3 * 0.044715 * x * x)
    return gelu, dgelu


def _conv_specs():
    ug = pl.BlockSpec((S, TC), lambda c: (0, c))
    uv = pl.BlockSpec((S, TC), lambda c: (0, N_FF_BLK + c))
    wg = pl.BlockSpec((3, TC), lambda c: (0, c))
    wv = pl.BlockSpec((3, TC), lambda c: (0, N_FF_BLK + c))
    bg = pl.BlockSpec((1, TC), lambda c: (0, c))
    bv = pl.BlockSpec((1, TC), lambda c: (0, N_FF_BLK + c))
    return ug, uv, wg, wv, bg, bv


def _conv_fwd(u, conv_w, conv_b):
    ug, uv, wg, wv, bg, bv = _conv_specs()

    def body(ug_ref, uv_ref, wg_ref, wv_ref, bg_ref, bv_ref, a_ref):
        gc = _conv(ug_ref[...], wg_ref[...], bg_ref[...])[0]
        vc = _conv(uv_ref[...], wv_ref[...], bv_ref[...])[0]
        a_ref[...] = (_gelu_parts(gc)[0] * vc).astype(BF16)

    return pl.pallas_call(
        body, name="conv_fwd", grid=(N_FF_BLK,), in_specs=[ug, uv, wg, wv, bg, bv], out_specs=ug,
        out_shape=jax.ShapeDtypeStruct((S, D_FF), BF16),
        compiler_params=_params(("parallel",)))(u, u, conv_w, conv_w, conv_b, conv_b)


def _conv_bwd(u, conv_w, conv_b, da):
    ug, uv, wg, wv, bg, bv = _conv_specs()

    def back(duc, u, s1, s2, w):
        du = w[2:3, :] * duc + w[1:2, :] * _shift_up(duc, 1) + w[0:1, :] * _shift_up(duc, 2)
        dw = _three_rows([jnp.sum(duc * s2, axis=0, keepdims=True), jnp.sum(duc * s1, axis=0, keepdims=True),
                          jnp.sum(duc * u, axis=0, keepdims=True)])
        return du, dw, jnp.sum(duc, axis=0, keepdims=True)

    def body(ug_ref, uv_ref, wg_ref, wv_ref, bg_ref, bv_ref, da_ref, dug_ref, duv_ref, dwg_ref, dwv_ref, dbg_ref, dbv_ref):
        u_g, u_v = ug_ref[...], uv_ref[...]
        gc, g1, g2 = _conv(u_g, wg_ref[...], bg_ref[...])
        vc, v1, v2 = _conv(u_v, wv_ref[...], bv_ref[...])
        gelu, dgelu = _gelu_parts(gc)
        da = da_ref[...]
        du, dw, db = back(da * vc * dgelu, u_g, g1, g2, wg_ref[...])
        dug_ref[...] = du.astype(BF16)
        dwg_ref[...] = dw
        dbg_ref[...] = db
        du, dw, db = back(da * gelu, u_v, v1, v2, wv_ref[...])
        duv_ref[...] = du.astype(BF16)
        dwv_ref[...] = dw
        dbv_ref[...] = db

    return pl.pallas_call(
        body, name="conv_bwd", grid=(N_FF_BLK,), in_specs=[ug, uv, wg, wv, bg, bv, ug],
        out_specs=[ug, ug, wg, wg, bg, bg],
        out_shape=[jax.ShapeDtypeStruct((S, D_FF), BF16), jax.ShapeDtypeStruct((S, D_FF), BF16),
                   jax.ShapeDtypeStruct((3, D_FF), F32), jax.ShapeDtypeStruct((3, D_FF), F32),
                   jax.ShapeDtypeStruct((1, D_FF), F32), jax.ShapeDtypeStruct((1, D_FF), F32)],
        compiler_params=_params(("parallel",)))(u, u, conv_w, conv_w, conv_b, conv_b, da)


def _adamw(name, w, g, m, v):
    shape = w.shape
    cols = shape[-1]
    flat = [t.reshape(-1, cols) for t in (w, g, m, v)]
    r = flat[0].shape[0]
    tr = min(r, max(8, 2 * 1024 * 1024 // (4 * cols)))

    def body(w_ref, g_ref, m_ref, v_ref, go_ref, d_ref, mo_ref, vo_ref):
        g = g_ref[...]
        go_ref[...] = g
        m = ADAM_B1 * m_ref[...] + (1.0 - ADAM_B1) * g
        v = ADAM_B2 * v_ref[...] + (1.0 - ADAM_B2) * (g * g)
        m_hat = m / (1.0 - ADAM_B1 ** ADAM_STEP)
        v_hat = v / (1.0 - ADAM_B2 ** ADAM_STEP)
        d_ref[...] = -ADAM_LR * (m_hat / (jnp.sqrt(v_hat) + ADAM_EPS) + ADAM_WD * w_ref[...])
        mo_ref[...] = m
        vo_ref[...] = v

    spec = pl.BlockSpec((tr, cols), lambda i: (i, 0))
    outs = pl.pallas_call(
        body, name=name, grid=(pl.cdiv(r, tr),), in_specs=[spec] * 4, out_specs=[spec] * 4,
        out_shape=[jax.ShapeDtypeStruct((r, cols), F32)] * 4, compiler_params=_params(("parallel",)))(*flat)
    return [t.reshape(shape) for t in outs]


def _place():
    x, y, c = lax.axis_index("x"), lax.axis_index("y"), lax.axis_index("c")
    chips = [(1 - x, y), (x, 1 - y), (1 - x, 1 - y)]
    return x, y, c, chips


def _scalars(*vals):
    return jnp.stack([jnp.asarray(v, jnp.int32) for v in vals])


HBM = pl.BlockSpec(memory_space=pltpu.HBM)
SEM = pl.BlockSpec(memory_space=pltpu.SEMAPHORE)
SPLIT_COPY = pltpu.CompilerParams(has_side_effects=pltpu.SideEffectType.DATAFLOW_SIDE_EFFECTING)


def _in_hbm(x):
    return pltpu.with_memory_space_constraint(x, pltpu.HBM)


def _cast_into_slot(name, w, layer, chip):
    _, k, n4 = w.shape
    tr = max(t for t in range(16, 513, 16) if k % t == 0)

    def body(chip_ref, w_ref, o_ref):
        o_ref[...] = w_ref[...].astype(BF16)

    return pl.pallas_call(
        body, name=name,
        grid_spec=pltpu.PrefetchScalarGridSpec(
            num_scalar_prefetch=1, grid=(k // tr,),
            in_specs=[pl.BlockSpec((None, tr, n4), lambda i, s: (layer, i, 0))],
            out_specs=pl.BlockSpec((None, tr, n4), lambda i, s: (s[0], i, 0))),
        out_shape=jax.ShapeDtypeStruct((N_CHIPS, k, n4), BF16),
        compiler_params=_params(("parallel",)))(_scalars(chip), w)


def _gather_copy(buf_ref, k, from_chip, send_sem, recv_sem, chips, c, half=False):
    rows = buf_ref.at[from_chip]
    if half:
        h = buf_ref.shape[1] // 2
        rows = buf_ref.at[from_chip, pl.ds(pl.multiple_of(c * h, h), h)]
    return pltpu.make_async_remote_copy(src_ref=rows, dst_ref=rows, send_sem=send_sem, recv_sem=recv_sem,
                                        device_id=(*chips[k], c), device_id_type=MESH)


def _gather_start(name, bufs, groups, halved=()):
    n, ng = len(bufs), len(groups)
    where = {a: (gi, e) for gi, g in enumerate(groups) for e, a in enumerate(g)}

    def body(*refs):
        ins, sems, token = refs[:n], refs[n:n + 2 * ng], refs[-1]
        x, y, c, chips = _place()
        for a in range(n):
            gi, e = where[a]
            for k in range(3):
                _gather_copy(ins[a], k, 2 * x + y, sems[2 * gi].at[3 * e + k], sems[2 * gi + 1].at[3 * e + k],
                             chips, c, a in halved).start()
        token[...] = jnp.zeros_like(token)

    out_shape = [pltpu.SemaphoreType.DMA((3 * len(g),)) for g in groups for _ in range(2)]
    out_shape += [pltpu.HBM(b.shape, b.dtype) for b in bufs] + [jax.ShapeDtypeStruct((8, 128), F32)]
    out = pl.pallas_call(
        body, name=name, in_specs=[HBM] * n,
        out_specs=[SEM] * (2 * ng) + [HBM] * n + [pl.BlockSpec(memory_space=pltpu.VMEM)], out_shape=out_shape,
        input_output_aliases={a: 2 * ng + a for a in range(n)}, compiler_params=SPLIT_COPY)(*[_in_hbm(b) for b in bufs])
    sems = [(out[2 * gi], out[2 * gi + 1]) for gi in range(ng)]
    return sems, list(out[2 * ng:2 * ng + n]), out[-1]


def _gather_wait(name, bufs, send, recv, after, halved=()):
    n = len(bufs)

    def body(*refs):
        ins, send_sem, recv_sem = refs[:n], refs[n], refs[n + 1]
        x, y, c, chips = _place()
        for e in range(n):
            for k in range(3):
                sems = (send_sem.at[3 * e + k], recv_sem.at[3 * e + k])
                _gather_copy(ins[e], k, 2 * x + y, *sems, chips, c, e in halved).wait_send()
                _gather_copy(ins[e], k, 2 * chips[k][0] + chips[k][1], *sems, chips, c, e in halved).wait_recv()

    return pl.pallas_call(
        body, name=name, in_specs=[HBM] * n + [SEM, SEM, ANY], out_specs=[HBM] * n,
        out_shape=[pltpu.HBM(b.shape, b.dtype) for b in bufs],
        input_output_aliases={a: a for a in range(n)}, compiler_params=SPLIT_COPY)(*bufs, send, recv, after)


def _swap_halves(name, bufs):
    n = len(bufs)

    def body(*refs):
        ins, outs = refs[:n], refs[n:2 * n]
        send_sem, recv_sem = refs[2 * n:]
        x, y, c, chips = _place()

        def piece(ref, k, which):
            h = ref.shape[1] // 2
            return ref.at[2 * chips[k][0] + chips[k][1], pl.ds(pl.multiple_of(which * h, h), h)]

        def copy(a, k, which):
            return pltpu.make_async_remote_copy(
                src_ref=piece(ins[a], k, c), dst_ref=piece(outs[a], k, which), send_sem=send_sem.at[3 * a + k],
                recv_sem=recv_sem.at[3 * a + k], device_id=(x, y, 1 - c), device_id_type=MESH)

        for a in range(n):
            for k in range(3):
                copy(a, k, c).start()
        for a in range(n):
            for k in range(3):
                copy(a, k, c).wait_send()
                copy(a, k, 1 - c).wait_recv()

    return pl.pallas_call(
        body, name=name, in_specs=[ANY] * n, out_specs=[ANY] * n,
        out_shape=[jax.ShapeDtypeStruct(b.shape, b.dtype) for b in bufs],
        input_output_aliases={a: a for a in range(n)},
        scratch_shapes=[pltpu.SemaphoreType.DMA((3 * n,)), pltpu.SemaphoreType.DMA((3 * n,))],
    )(*bufs)


def _reduce_copy(g_ref, land_ref, mask, send_sem, recv_sem, x, y, c, sending):
    px, py, pc = x ^ ((mask >> 2) & 1), y ^ ((mask >> 1) & 1), c ^ (mask & 1)
    half = g_ref.shape[1] // 2
    src = g_ref.at[2 * px + py, pl.ds(pl.multiple_of(pc * half, half), half)]
    dst = land_ref.at[4 * x + 2 * y + c] if sending else land_ref.at[4 * px + 2 * py + pc]
    return pltpu.make_async_remote_copy(src_ref=src, dst_ref=dst, send_sem=send_sem, recv_sem=recv_sem,
                                        device_id=(px, py, pc), device_id_type=MESH)


def _reduce_start(name, grads):
    n = len(grads)
    lands = [lax.empty((N_DEV, g.shape[1] // 2, g.shape[2]), g.dtype) for g in grads]

    def body(*refs):
        gs, ls, send_sem, recv_sem = refs[:n], refs[n:2 * n], refs[2 * n], refs[2 * n + 1]
        x, y, c, _ = _place()
        for a in range(n):
            for mask in range(1, N_DEV):
                s = (N_DEV - 1) * a + mask - 1
                _reduce_copy(gs[a], ls[a], mask, send_sem.at[s], recv_sem.at[s], x, y, c, True).start()
        refs[-1][...] = jnp.zeros_like(refs[-1])

    sem = pltpu.SemaphoreType.DMA((n * (N_DEV - 1),))
    out = pl.pallas_call(
        body, name=name, in_specs=[HBM] * (2 * n),
        out_specs=[SEM, SEM] + [HBM] * (2 * n) + [pl.BlockSpec(memory_space=pltpu.VMEM)],
        out_shape=[sem, sem] + [pltpu.HBM(t.shape, t.dtype) for t in grads + lands] + [jax.ShapeDtypeStruct((8, 128), F32)],
        input_output_aliases={a: 2 + a for a in range(2 * n)}, compiler_params=SPLIT_COPY)(
            *[_in_hbm(t) for t in grads + lands])
    return out[0], out[1], list(out[2:2 + n]), list(out[2 + n:2 + 2 * n]), out[-1]


def _reduce_wait(name, send, recv, grads, lands, after):
    n = len(grads)

    def body(*refs):
        gs, ls, send_sem, recv_sem = refs[:n], refs[n:2 * n], refs[2 * n], refs[2 * n + 1]
        x, y, c, _ = _place()
        for a in range(n):
            for mask in range(1, N_DEV):
                s = (N_DEV - 1) * a + mask - 1
                sems = (send_sem.at[s], recv_sem.at[s])
                _reduce_copy(gs[a], ls[a], mask, *sems, x, y, c, True).wait_send()
                _reduce_copy(gs[a], ls[a], mask, *sems, x, y, c, False).wait_recv()

    out = pl.pallas_call(
        body, name=name, in_specs=[HBM] * (2 * n) + [SEM, SEM, ANY], out_specs=[HBM] * (2 * n),
        out_shape=[pltpu.HBM(t.shape, t.dtype) for t in grads + lands],
        input_output_aliases={a: a for a in range(2 * n)}, compiler_params=SPLIT_COPY)(*grads, *lands, send, recv, after)
    return list(out[:n]), list(out[n:])


def _reduce_sum(name, g, land, layer, into, chip, c):
    _, k4, n4 = g.shape
    half = k4 // 2
    tr = max(t for t in range(16, 513, 16) if half % t == 0)
    per = half // tr
    me = 2 * chip + c

    def body(s_ref, own_ref, *refs):
        total = own_ref[...].astype(F32)
        for ref in refs[:N_DEV - 1]:
            total = total + ref[...].astype(F32)
        refs[-1][...] = total

    in_specs = [pl.BlockSpec((None, tr, n4), lambda i, s: (s[0], s[1] * per + i, 0))]
    in_specs += [pl.BlockSpec((None, tr, n4), functools.partial(lambda i, s, m: (s[1 + m], i, 0), m=m))
                 for m in range(1, N_DEV)]
    ins = [g] + [land] * (N_DEV - 1)
    aliases = {}
    if into is not None:
        in_specs, ins, aliases = in_specs + [ANY], ins + [into], {1 + N_DEV: 0}
    return pl.pallas_call(
        body, name=name,
        grid_spec=pltpu.PrefetchScalarGridSpec(
            num_scalar_prefetch=1, grid=(per,), in_specs=in_specs,
            out_specs=pl.BlockSpec((None, tr, n4), lambda i, s: (layer, s[1] * per + i, 0))),
        out_shape=jax.ShapeDtypeStruct((DEPTH, k4, n4), F32), input_output_aliases=aliases,
        compiler_params=_params(("parallel",)))(_scalars(chip, c, *[me ^ m for m in range(1, N_DEV)]), *ins)


def _join_halves(name, bufs):
    n = len(bufs)

    def body(*refs):
        ins, outs = refs[:n], refs[n:2 * n]
        send_sem, recv_sem = refs[2 * n:]
        x, y, c, _ = _place()

        def rows(ref, which):
            half = ref.shape[1] // 2
            return ref.at[:, pl.ds(pl.multiple_of(which * half, half), half)]

        sends = [pltpu.make_async_remote_copy(
            src_ref=rows(ins[a], c), dst_ref=rows(outs[a], c), send_sem=send_sem.at[a], recv_sem=recv_sem.at[a],
            device_id=(x, y, 1 - c), device_id_type=MESH) for a in range(n)]
        for cp in sends:
            cp.start()
        for a in range(n):
            sends[a].wait_send()
            pltpu.make_async_remote_copy(
                src_ref=rows(ins[a], c), dst_ref=rows(outs[a], 1 - c), send_sem=send_sem.at[a], recv_sem=recv_sem.at[a],
                device_id=(x, y, 1 - c), device_id_type=MESH).wait_recv()

    return pl.pallas_call(
        body, name=name, in_specs=[ANY] * n, out_specs=[ANY] * n,
        out_shape=[jax.ShapeDtypeStruct(b.shape, b.dtype) for b in bufs],
        input_output_aliases={a: a for a in range(n)},
        scratch_shapes=[pltpu.SemaphoreType.DMA((n,)), pltpu.SemaphoreType.DMA((n,))],
    )(*bufs)


def _small_copy(b_ref, l_ref, mask, send_sem, recv_sem, x, y, c, sending):
    px, py, pc = x ^ ((mask >> 2) & 1), y ^ ((mask >> 1) & 1), c ^ (mask & 1)
    dst = l_ref.at[4 * x + 2 * y + c] if sending else l_ref.at[4 * px + 2 * py + pc]
    return pltpu.make_async_remote_copy(src_ref=b_ref, dst_ref=dst, send_sem=send_sem.at[mask - 1],
                                        recv_sem=recv_sem.at[mask - 1], device_id=(px, py, pc), device_id_type=MESH)


def _small_start(block):
    land = lax.empty((N_DEV,) + block.shape, block.dtype)

    def body(b_ref, l_ref, send_sem, recv_sem, b_thru, l_thru, token):
        x, y, c, _ = _place()
        for mask in range(1, N_DEV):
            _small_copy(b_ref, l_ref, mask, send_sem, recv_sem, x, y, c, True).start()
        token[...] = jnp.zeros_like(token)

    sem = pltpu.SemaphoreType.DMA((N_DEV - 1,))
    return pl.pallas_call(
        body, name="small_start", in_specs=[HBM, HBM],
        out_specs=[SEM, SEM, HBM, HBM, pl.BlockSpec(memory_space=pltpu.VMEM)],
        out_shape=[sem, sem, pltpu.HBM(block.shape, block.dtype), pltpu.HBM(land.shape, land.dtype),
                   jax.ShapeDtypeStruct((8, 128), F32)],
        input_output_aliases={0: 2, 1: 3}, compiler_params=SPLIT_COPY)(_in_hbm(block), _in_hbm(land))


def _small_wait(send, recv, block, land, after):
    def body(b_ref, l_ref, send_sem, recv_sem, after_ref, b_out, l_out):
        x, y, c, _ = _place()
        for mask in range(1, N_DEV):
            _small_copy(b_ref, l_ref, mask, send_sem, recv_sem, x, y, c, True).wait_send()
            _small_copy(b_ref, l_ref, mask, send_sem, recv_sem, x, y, c, False).wait_recv()

    return pl.pallas_call(
        body, name="small_wait", in_specs=[HBM, HBM, SEM, SEM, ANY], out_specs=[HBM, HBM],
        out_shape=[pltpu.HBM(block.shape, block.dtype), pltpu.HBM(land.shape, land.dtype)],
        input_output_aliases={0: 0, 1: 1}, compiler_params=SPLIT_COPY)(block, land, send, recv, after)


def _small_sum(land):
    def body(l_ref, out_ref):
        total = l_ref[0]
        for d in range(1, N_DEV):
            total = total + l_ref[d]
        out_ref[...] = total

    vmem = pl.BlockSpec(memory_space=pltpu.VMEM)
    return pl.pallas_call(
        body, name="small_sum", in_specs=[vmem], out_specs=vmem,
        out_shape=jax.ShapeDtypeStruct(land.shape[1:], F32),
        compiler_params=pltpu.CompilerParams(vmem_limit_bytes=VMEM_LIMIT))(land)


B_Q_COL = 2304 // 128
B_K0, B_V0, B_END = 2816, 2944, 3072


def _full_cols(w_g):
    return w_g.transpose(1, 0, 2).reshape(w_g.shape[1], -1)


A_DILS = tuple(d for _, d in A_GROUPS)
A_PAIRS = N_A // 2


def _src_a(proj):
    return ((proj, 0), (proj, A_PAIRS), (proj, 2 * A_PAIRS))


def _kv_expand(kv):
    return jnp.broadcast_to(kv.reshape(S, 2, 1, HD), (S, 2, 4, HD)).reshape(S, 8 * HD)


def _kv_reduce(dkv):
    return dkv.reshape(S, 2, 4, HD).sum(axis=2).reshape(S, 2 * HD)


def _mixer_fwd(h1, wget, rel_bias, sinks_l, bidx):
    w = dict(wget(0, h1))
    proj = _mm_nt("proj_in", h1, w["w_in"], F32, tm=S, tn=1152)
    no_sinks = jnp.full((N_A,), NEG, F32)
    o_g, lse_g = _band_fwd("band_fwd_a", A_DILS, A_PAIRS, BLK, 0, _src_a(proj), bidx[:3], rel_bias, no_sinks)
    o_a32, o_a, lse_a = _comb_fwd(o_g, lse_g)
    src_b = ((proj, B_Q_COL), (_kv_expand(proj[:, B_K0:B_V0]), 0), (_kv_expand(proj[:, B_V0:B_END]), 0))
    o_b32, lse_b = _band_fwd("band_fwd_b", (1,), 4, BLK - 1, N_A, src_b, bidx[3:], rel_bias, sinks_l)
    o_b = o_b32.astype(BF16)
    o_c32, o_c, tot_c = _sb_fwd(proj)
    w.update(wget(1, o_c32))
    br = [_mm_nn("branch_a", o_a, w["w_br_a"], F32, tm=S), _mm_nn("branch_b", o_b, w["w_br_b"], F32, tm=S),
          _mm_nn("branch_c", o_c, w["w_br_c"], F32, tm=S)]
    merged = _gate_fwd(proj, w["b_gate"], br)
    mo = _mm_nn("out_proj", merged, w["w_out"], F32, tm=S)
    saved = dict(proj=proj, src_b=src_b, o_a32=o_a32, lse_a=lse_a, o_b32=o_b32, lse_b=lse_b, tot_c=tot_c,
                 o_a=o_a, o_b=o_b, o_c=o_c, br=br, merged=merged)
    return mo, saved, w


def _mixer_bwd(d_mo, h1, w, sv, rel_bias, sinks_l, bidx, stats_in, emit):
    grads = {}
    dmerged = _mm_nt("out_proj_dx", d_mo, w["w_out"], F32, tm=S)
    grads["w_out"] = _mm_tn_sharded("out_proj_dw", sv["merged"], d_mo, True)
    e, dgate, db_gate = _gate_bwd(sv["proj"], w["b_gate"], sv["br"], dmerged)
    grads["b_gate"] = db_gate
    d_o = {}
    for n, name in enumerate("abc"):
        d_o[name] = _mm_nt("branch_%s_dx" % name, e[n], w["w_br_" + name], F32, tm=S)
        grads["w_br_" + name] = _mm_tn_sharded("branch_%s_dw" % name, sv["o_" + name], e[n], False)
    zero = emit(1, grads)
    no_sinks = jnp.full((N_A,), NEG, F32) + zero[0]
    dq_a, dk_a, dv_a, st_a = _band_bwd("band_bwd_a", A_DILS, A_PAIRS, BLK, 0, _src_a(sv["proj"]), bidx[:3], rel_bias,
                                       no_sinks, sv["o_a32"], sv["lse_a"], d_o["a"], stats_in[:N_A])
    dq_b, dk_x, dv_x, st_b = _band_bwd("band_bwd_b", (1,), 4, BLK - 1, N_A, sv["src_b"], bidx[3:], rel_bias, sinks_l,
                                       sv["o_b32"], sv["lse_b"], d_o["b"], stats_in[N_A:])
    stats = jnp.concatenate([st_a, st_b], axis=0)
    dcq, dck, dcv = _sb_bwd(sv["proj"], sv["tot_c"], d_o["c"])
    cols = [dq_a, dk_a, dv_a, dq_b, _kv_reduce(dk_x), _kv_reduce(dv_x), dcq, dck, dcv]
    dproj = jnp.concatenate([t.astype(BF16) for t in cols] + list(dgate), axis=1)
    grads["w_in"] = _mm_tn("proj_in_dw", dproj, h1, BF16, tm=1152, tn=1024).reshape(N_CHIPS, IN_SHARD, D)
    zero = emit(2, grads)
    dh1 = _mm_nn("proj_in_dx", dproj, w["w_in"], F32, tm=S, tk=2304)
    return dh1, grads, stats, zero


def _ffn_fwd(h2, w):
    u = _mm_nn("ffn_up", h2, w["w_up"], F32, tm=S, tn=1024)
    a = _conv_fwd(u, w["conv_w"], w["conv_b"])
    dn = _mm_nn("ffn_down", a, w["w_down"], F32, tm=1024)
    return dn, dict(u=u, a=a)


def _ffn_bwd(d_dn, h2, w, sv):
    grads = {}
    da = _mm_nt("ffn_down_dx", d_dn, w["w_down"], F32, tm=S, tn=1024)
    grads["w_down"] = _mm_tn_sharded("ffn_down_dw", sv["a"], d_dn, True, tm=1024, tn=1024)
    dug, duv, dwg, dwv, dbg, dbv = _conv_bwd(sv["u"], w["conv_w"], w["conv_b"], da)
    du = jnp.concatenate([dug, duv], axis=1)
    grads["conv_w"] = jnp.concatenate([dwg, dwv], axis=1)
    grads["conv_b"] = jnp.concatenate([dbg, dbv], axis=1)
    dh2 = _mm_nt("ffn_up_dx", du, w["w_up"], F32, tm=S, tk=2048)
    grads["w_up"] = _mm_tn_sharded("ffn_up_dw", h2, du, False, tm=1024, tn=1024)
    return dh2, grads


BIG = ("w_in", "w_br_a", "w_br_b", "w_br_c", "w_out", "w_up", "w_down")


def _shard_view(name, w):
    return jnp.swapaxes(w, 1, 2) if name == "w_in" else w
WEIGHT_GROUPS = (("w_in", "b_gate"), ("w_br_a", "w_br_b", "w_br_c", "w_out"), ("w_up", "conv_w", "w_down"))
GRAD_GROUPS = (("w_down", "w_up"), ("w_out", "w_br_a", "w_br_b", "w_br_c"), ("w_in",))
SMALL_ROWS = (("rel_bias", 8), ("attn_pre_norm", 16), ("attn_post_norm", 16), ("ffn_pre_norm", 16), ("ffn_post_norm", 16),
              ("sinks", 8), ("conv_b", 128), ("b_gate", 48), ("conv_w", 384), ("loss", 8))


def _pack_small(vals):
    rows = []
    for name, n in SMALL_ROWS:
        flat = vals[name].reshape(-1).astype(F32)
        rows.append(jnp.pad(flat, (0, n * 128 - flat.shape[0])).reshape(n, 128))
    return jnp.concatenate(rows, axis=0)


def _unpack_small(block, shapes):
    out, row = {}, 0
    for name, n in SMALL_ROWS:
        size = int(np.prod(shapes[name]))
        out[name] = block[row:row + n].reshape(-1)[:size].reshape(shapes[name])
        row += n
    return out


def kernel(x, rel_bias, attn_pre_norm, w_in, b_gate, sinks, w_br_a, w_br_b, w_br_c, w_out, attn_post_norm, ffn_pre_norm, w_up, conv_w, conv_b, w_down, ffn_post_norm, loss_target, m_rel_bias, m_attn_pre_norm, m_w_in, m_b_gate, m_sinks, m_w_br_a, m_w_br_b, m_w_br_c, m_w_out, m_attn_post_norm, m_ffn_pre_norm, m_w_up, m_conv_w, m_conv_b, m_w_down, m_ffn_post_norm, v_rel_bias, v_attn_pre_norm, v_w_in, v_b_gate, v_sinks, v_w_br_a, v_w_br_b, v_w_br_c, v_w_out, v_attn_post_norm, v_ffn_pre_norm, v_w_up, v_conv_w, v_conv_b, v_w_down, v_ffn_post_norm):
    names = ("rel_bias", "attn_pre_norm", "w_in", "b_gate", "sinks", "w_br_a", "w_br_b", "w_br_c", "w_out",
             "attn_post_norm", "ffn_pre_norm", "w_up", "conv_w", "conv_b", "w_down", "ffn_post_norm")
    weights = dict(zip(names, (rel_bias, attn_pre_norm, w_in, b_gate, sinks, w_br_a, w_br_b, w_br_c, w_out,
                               attn_post_norm, ffn_pre_norm, w_up, conv_w, conv_b, w_down, ffn_post_norm)))
    mom1 = dict(zip(names, (m_rel_bias, m_attn_pre_norm, m_w_in, m_b_gate, m_sinks, m_w_br_a, m_w_br_b, m_w_br_c,
                            m_w_out, m_attn_post_norm, m_ffn_pre_norm, m_w_up, m_conv_w, m_conv_b, m_w_down,
                            m_ffn_post_norm)))
    mom2 = dict(zip(names, (v_rel_bias, v_attn_pre_norm, v_w_in, v_b_gate, v_sinks, v_w_br_a, v_w_br_b, v_w_br_c,
                            v_w_out, v_attn_post_norm, v_ffn_pre_norm, v_w_up, v_conv_w, v_conv_b, v_w_down,
                            v_ffn_post_norm)))

    chip = 2 * lax.axis_index("x") + lax.axis_index("y")
    core = lax.axis_index("c")

    keys = [(n, l) for l in range(DEPTH) for group in WEIGHT_GROUPS for n in group]
    groups = [[keys.index((n, l)) for n in group] for l in range(DEPTH) for group in WEIGHT_GROUPS]

    def slot_buffer(n, l):
        if n in BIG:
            return _cast_into_slot("cast_" + n, _shard_view(n, weights[n]), l, chip)
        shard = weights[n][l]
        return lax.dynamic_update_slice(jnp.zeros((N_CHIPS,) + shard.shape, F32), shard[None],
                                        (chip, jnp.int32(0), jnp.int32(0)))

    by_halves = [keys.index(k) for k in (("w_in", 0), ("w_up", DEPTH - 1), ("w_down", DEPTH - 1))]
    n_first = len(groups[0])
    sems, in_flight, _ = _gather_start("gather_start_first", [slot_buffer(*k) for k in keys[:n_first]], groups[:1],
                                       tuple(a for a in by_halves if a < n_first))
    more = _gather_start("gather_start", [slot_buffer(*k) for k in keys[n_first:]],
                         [[a - n_first for a in g] for g in groups[1:]],
                         tuple(a - n_first for a in by_halves if a >= n_first))
    sems, in_flight, started = sems + more[0], in_flight + more[1], more[2]

    def wget(l, gi, after):
        g = l * len(WEIGHT_GROUPS) + gi
        after = started if g == 0 else after
        halved = tuple(e for e, a in enumerate(groups[g]) if a in by_halves)
        got = list(_gather_wait("gather_wait_%d_%d" % (l, gi), [in_flight[a] for a in groups[g]], *sems[g], after,
                                halved))
        if halved:
            for e, buf in zip(halved, _swap_halves("swap_halves_%d_%d" % (l, gi), [got[e] for e in halved])):
                got[e] = buf
        out = {}
        for n, buf in zip(WEIGHT_GROUPS[gi], got):
            out[n] = buf.reshape(-1, buf.shape[-1]) if n in ("w_in", "w_out", "w_down") else _full_cols(buf)
        if gi == len(WEIGHT_GROUPS) - 1:
            out["conv_b"] = conv_b[l:l + 1]
        return out

    pending = []

    def emit(l, gi, grads):
        group = GRAD_GROUPS[gi]
        *started, token = _reduce_start("reduce_start_%d_%d" % (l, gi), [grads[n] for n in group])
        pending.append((l, group) + tuple(started))
        return token[:1, :1]

    local = _local_step(x.reshape(S, D), loss_target.reshape(S, D), wget, emit, rel_bias, sinks, attn_pre_norm,
                        attn_post_norm, ffn_pre_norm, ffn_post_norm)
    return _reduce_and_update(x.shape, names, weights, mom1, mom2, chip, core, pending, *local)


def _local_step(xs, target, wget, emit, rel_bias, sinks, attn_pre_norm, attn_post_norm, ffn_pre_norm, ffn_post_norm):
    bidx = jnp.asarray(_bucket_maps())

    saved, layers = [], []
    h1 = _rms_fwd("pre_norm_first", xs, attn_pre_norm[0:1])
    x_in = xs
    for l in range(DEPTH):
        mo, sv_mix, w = _mixer_fwd(h1, functools.partial(wget, l), rel_bias, sinks[l], bidx)
        x_mid, h2 = _post_pre_fwd("post_attn_norm", x_in, mo, attn_post_norm[l:l + 1], ffn_pre_norm[l:l + 1])
        w.update(wget(l, 2, h2))
        dn, sv_ffn = _ffn_fwd(h2, w)
        g_next = attn_pre_norm[l + 1:l + 2] if l + 1 < DEPTH else None
        x_out, h1_next = _post_pre_fwd("post_ffn_norm" if l + 1 < DEPTH else "post_ffn_norm_last", x_mid, dn,
                                       ffn_post_norm[l:l + 1], g_next)
        saved.append(dict(x_in=x_in, h1=h1, mo=mo, x_mid=x_mid, h2=h2, dn=dn, mix=sv_mix, ffn=sv_ffn))
        layers.append(w)
        x_in, h1 = x_out, h1_next

    loss_row, dres = _loss_kernel(x_in, target)

    small = [None] * DEPTH
    stats = jnp.zeros((N_BAND_Q, 8, 128), F32)
    dh_next = None
    for l in reversed(range(DEPTH)):
        w, sv = layers[l], saved[l]
        if l + 1 < DEPTH:
            pre = (saved[l + 1]["x_in"], attn_pre_norm[l + 1:l + 2] + zero, dh_next)
            dres, d_dn, dg_pre_next, dg_fpost = _norm_bwd("post_ffn_norm_bwd", dres, pre,
                                                          (sv["dn"], ffn_post_norm[l:l + 1]))
            small[l + 1]["attn_pre_norm"] = dg_pre_next
        else:
            dres, d_dn, _, dg_fpost = _norm_bwd("post_ffn_norm_last_bwd", dres, None, (sv["dn"], ffn_post_norm[l:l + 1]))
        dh2, g_ffn = _ffn_bwd(d_dn, sv["h2"], w, sv["ffn"])
        zero = emit(l, 0, g_ffn)
        dres, d_mo, dg_fpre, dg_apost = _norm_bwd("post_attn_norm_bwd", dres,
                                                  (sv["x_mid"], ffn_pre_norm[l:l + 1] + zero, dh2),
                                                  (sv["mo"], attn_post_norm[l:l + 1]))
        dh_next, g_mix, stats, zero = _mixer_bwd(d_mo, sv["h1"], w, sv["mix"], rel_bias, sinks[l], bidx, stats,
                                                 functools.partial(emit, l))
        small[l] = dict(ffn_post_norm=dg_fpost, ffn_pre_norm=dg_fpre, attn_post_norm=dg_apost,
                        sinks=stats[N_A:, 1, 0], conv_b=g_ffn["conv_b"], b_gate=g_mix["b_gate"], conv_w=g_ffn["conv_w"])
    grad_x, _, dg_pre0, _ = _norm_bwd("pre_norm_first_bwd", dres, (saved[0]["x_in"], attn_pre_norm[0:1] + zero, dh_next),
                                      None)
    small[0]["attn_pre_norm"] = dg_pre0
    return loss_row, grad_x, small, stats


def _reduce_and_update(x_shape, names, weights, mom1, mom2, chip, core, pending, loss_row, grad_x, small, stats):
    delta, new_m, new_v, grads = {}, {}, {}, {}

    def update(n, g):
        grads[n], delta[n], new_m[n], new_v[n] = _adamw("adamw_" + n, _shard_view(n, weights[n]), g,
                                                        _shard_view(n, mom1[n]), _shard_view(n, mom2[n]))

    small_vals = {n: jnp.stack([small[l][n].reshape(weights[n].shape[1:]) for l in range(DEPTH)])
                  for n in ("attn_pre_norm", "attn_post_norm", "ffn_pre_norm", "ffn_post_norm", "conv_b", "sinks")}
    small_vals["b_gate"] = jnp.stack([small[l]["b_gate"] for l in range(DEPTH)])
    small_vals["conv_w"] = jnp.stack([small[l]["conv_w"] for l in range(DEPTH)])
    small_vals["rel_bias"] = stats[:, 0, :NUM_BUCKETS].T
    small_vals["loss"] = loss_row[0, :1]
    shapes = {n: v.shape for n, v in small_vals.items()}
    small_send, small_recv, packed, small_land, started = _small_start(_pack_small(small_vals))

    summed = {}

    def finish(which, after):
        for l, group, send, recv, gs, lands in pending:
            if (group == ("w_in",)) == which:
                gs, lands = _reduce_wait("reduce_wait_%d_%s" % (l, group[0]), send, recv, gs, lands, after)
                for n, g, land in zip(group, gs, lands):
                    summed[n] = _reduce_sum("reduce_sum_%d_%s" % (l, n), g, land, l, summed.get(n), chip, core)

    finish(False, started)
    early = [n for n in BIG if n != "w_in"]
    for n, g in zip(early, _join_halves("join_halves", [summed[n] for n in early])):
        update(n, g)
    finish(True, delta[early[-1]])
    update("w_in", _join_halves("join_halves_w_in", [summed["w_in"]])[0])

    packed, small_land = _small_wait(small_send, small_recv, packed, small_land, delta["w_in"])
    small_land = lax.dynamic_update_slice(small_land, packed[None], (2 * chip + core, jnp.int32(0), jnp.int32(0)))
    reduced = _unpack_small(_small_sum(small_land), shapes)
    reduced["b_gate"] = lax.dynamic_slice_in_dim(reduced["b_gate"], chip * (D // N_CHIPS), D // N_CHIPS, axis=2)
    reduced["conv_w"] = lax.dynamic_slice_in_dim(reduced["conv_w"], chip * (2 * D_FF // N_CHIPS), 2 * D_FF // N_CHIPS, axis=2)
    for n in names:
        if n not in grads:
            update(n, reduced[n].reshape(weights[n].shape))
    for out in (grads, delta, new_m, new_v):
        out["w_in"] = _shard_view("w_in", out["w_in"])

    loss = reduced["loss"].reshape(())
    return (loss, grad_x.reshape(x_shape), *[grads[n] for n in names], *[delta[n] for n in names],
            *[new_m[n] for n in names], *[new_v[n] for n in names])
```

```python
import functools
import math

import numpy as np
import jax
import jax.numpy as jnp
from jax import lax
from jax.experimental import pallas as pl
from jax.experimental.pallas import tpu as pltpu

F32 = jnp.float32
BF16 = jnp.bfloat16

S = 2048
D = 1024
DEPTH = 2
HD = 64
BLK = 128
NQB = S // BLK
A_GROUPS = ((128, 1), (512, 4), (2048, 16))
N_BAND_Q = 20
N_A = 12
NUM_BUCKETS = 32
MAX_DISTANCE = 2048
D_FF = 4096
IN_COLS = 6912
IN_SHARD = IN_COLS // 4
OFF_GATE = 3840
EPS = 1e-6
SCALE = HD ** -0.5
NEG = -1e30
N_CHIPS = 4
N_DEV = 8

ADAM_LR = 0.001
ADAM_B1 = 0.9
ADAM_B2 = 0.999
ADAM_EPS = 1e-08
ADAM_WD = 0.01
ADAM_STEP = 10

VMEM_LIMIT = 56 * 1024 * 1024

NN = (((1,), (0,)), ((), ()))
NT = (((1,), (1,)), ((), ()))
TN = (((0,), (0,)), ((), ()))

MESH = pl.DeviceIdType.MESH
ANY = pl.BlockSpec(memory_space=pl.ANY)


def _dot(a, b, dims):
    return lax.dot_general(a, b, dims, preferred_element_type=F32)


def _params(sem):
    return pltpu.CompilerParams(dimension_semantics=sem, vmem_limit_bytes=VMEM_LIMIT)


def _matmul(name, a, b, out_shape, out_dtype, grid, a_spec, b_spec, o_spec, dims, acc_shape):
    nk = grid[-1]

    def body(a_ref, b_ref, o_ref, *scratch):
        part = _dot(a_ref[...].astype(BF16), b_ref[...].astype(BF16), dims)
        if nk == 1:
            o_ref[...] = part.astype(o_ref.dtype)
            return
        acc_ref, = scratch
        k = pl.program_id(len(grid) - 1)

        @pl.when(k == 0)
        def _():
            acc_ref[...] = part

        @pl.when(k > 0)
        def _():
            acc_ref[...] += part

        @pl.when(k == nk - 1)
        def _():
            o_ref[...] = acc_ref[...].astype(o_ref.dtype)

    scratch = [] if nk == 1 else [pltpu.VMEM(acc_shape, F32)]
    sem = ("parallel",) * (len(grid) - 1) + ("arbitrary",)
    return pl.pallas_call(
        body, name=name, grid=grid, in_specs=[a_spec, b_spec], out_specs=o_spec,
        out_shape=jax.ShapeDtypeStruct(out_shape, out_dtype), scratch_shapes=scratch,
        compiler_params=_params(sem))(a, b)


FULL_K = 8192


def _mm_tn_sharded(name, a, b, row_sharded, tm=512, tn=512, tk=FULL_K):
    k, m = a.shape
    n = b.shape[-1] * (b.size // (k * b.shape[-1]))
    m4, n4 = (m // N_CHIPS, n) if row_sharded else (m, n // N_CHIPS)
    tm, tn, tk = min(tm, m4), min(tn, n4), min(tk, k)
    per_m, per_n, per_half = m4 // tm, n4 // tn, b.shape[-1] // tn
    if row_sharded:
        o_map = lambda i, j, l: (i // per_m, i % per_m, j)
    else:
        o_map = lambda i, j, l: (j // per_n, i, j % per_n)
    if b.ndim == 2:
        b_spec = pl.BlockSpec((tk, tn), lambda i, j, l: (l, j))
    else:
        b_spec = pl.BlockSpec((None, tk, tn), lambda i, j, l: (j // per_half, l, j % per_half))
    return _matmul(name, a, b, (N_CHIPS, m4, n4), BF16, (m // tm, n // tn, k // tk),
                   pl.BlockSpec((tk, tm), lambda i, j, l: (l, i)), b_spec,
                   pl.BlockSpec((None, tm, tn), o_map), TN, (tm, tn))


def _mm_nn(name, a, b, out_dtype, tm=512, tn=512, tk=FULL_K):
    m, k = a.shape
    n = b.shape[1]
    tm, tn, tk = min(tm, m), min(tn, n), min(tk, k)
    return _matmul(name, a, b, (m, n), out_dtype, (m // tm, n // tn, k // tk),
                   pl.BlockSpec((tm, tk), lambda i, j, l: (i, l)),
                   pl.BlockSpec((tk, tn), lambda i, j, l: (l, j)),
                   pl.BlockSpec((tm, tn), lambda i, j, l: (i, j)), NN, (tm, tn))


def _mm_nt(name, a, b, out_dtype, tm=512, tn=512, tk=FULL_K):
    m = a.shape[-2]
    n, k = b.shape
    tm, tn, tk = min(tm, m), min(tn, n), min(tk, a.shape[-1])
    per_half = a.shape[-1] // tk
    if a.ndim == 2:
        a_spec = pl.BlockSpec((tm, tk), lambda i, j, l: (i, l))
    else:
        a_spec = pl.BlockSpec((None, tm, tk), lambda i, j, l: (l // per_half, i, l % per_half))
    return _matmul(name, a, b, (m, n), out_dtype, (m // tm, n // tn, k // tk), a_spec,
                   pl.BlockSpec((tn, tk), lambda i, j, l: (j, l)),
                   pl.BlockSpec((tm, tn), lambda i, j, l: (i, j)), NT, (tm, tn))


def _mm_tn(name, a, b, out_dtype, tm=512, tn=512, tk=FULL_K):
    k, m = a.shape
    n = b.shape[1]
    tm, tn, tk = min(tm, m), min(tn, n), min(tk, k)
    return _matmul(name, a, b, (m, n), out_dtype, (m // tm, n // tn, k // tk),
                   pl.BlockSpec((tk, tm), lambda i, j, l: (l, i)),
                   pl.BlockSpec((tk, tn), lambda i, j, l: (l, j)),
                   pl.BlockSpec((tm, tn), lambda i, j, l: (i, j)), TN, (tm, tn))


TR = 512


def _row_spec(width=D):
    return pl.BlockSpec((TR, width), lambda i: (i, 0))


def _vec_spec(width=D):
    return pl.BlockSpec((1, width), lambda i: (0, 0))


def _rms(x, g):
    r = lax.rsqrt(jnp.mean(x * x, axis=-1, keepdims=True) + EPS)
    return x * r * g


def _rms_fwd(name, x, g):
    def body(x_ref, g_ref, h_ref):
        h_ref[...] = _rms(x_ref[...], g_ref[...]).astype(BF16)

    return pl.pallas_call(
        body, name=name, grid=(S // TR,), in_specs=[_row_spec(), _vec_spec()], out_specs=_row_spec(),
        out_shape=jax.ShapeDtypeStruct((S, D), BF16), compiler_params=_params(("parallel",)))(x, g)


def _post_pre_fwd(name, x, y, g_post, g_pre):
    has_pre = g_pre is not None

    def body(*refs):
        if has_pre:
            x_ref, y_ref, gp_ref, gn_ref, xn_ref, h_ref = refs
        else:
            x_ref, y_ref, gp_ref, xn_ref = refs
        xn = x_ref[...] + _rms(y_ref[...], gp_ref[...])
        xn_ref[...] = xn
        if has_pre:
            h_ref[...] = _rms(xn, gn_ref[...]).astype(BF16)

    ins = [x, y, g_post] + ([g_pre] if has_pre else [])
    in_specs = [_row_spec(), _row_spec(), _vec_spec()] + ([_vec_spec()] if has_pre else [])
    out_shape = [jax.ShapeDtypeStruct((S, D), F32)] + ([jax.ShapeDtypeStruct((S, D), BF16)] if has_pre else [])
    out_specs = [_row_spec()] + ([_row_spec()] if has_pre else [])
    out = pl.pallas_call(
        body, name=name, grid=(S // TR,), in_specs=in_specs, out_specs=out_specs, out_shape=out_shape,
        compiler_params=_params(("parallel",)))(*ins)
    return out if has_pre else (out[0], None)


def _rms_bwd_math(x, g, dy):
    r = lax.rsqrt(jnp.mean(x * x, axis=-1, keepdims=True) + EPS)
    n = x * r
    dn = dy * g
    dx = r * (dn - n * jnp.mean(dn * n, axis=-1, keepdims=True))
    return dx, jnp.sum(dy * n, axis=0, keepdims=True)


def _norm_bwd(name, dres, pre=None, post=None):
    has_pre, has_post = pre is not None, post is not None

    def body(*refs):
        refs = list(refs)
        dres_ref = refs.pop(0)
        if has_pre:
            xn_ref, gn_ref, dh_ref = refs[:3]
            refs = refs[3:]
        if has_post:
            y_ref, gp_ref = refs[:2]
            refs = refs[2:]
        dxn_ref = refs.pop(0)
        dy_ref = refs.pop(0) if has_post else None
        dgn_ref = refs.pop(0) if has_pre else None
        dgp_ref = refs.pop(0) if has_post else None
        first = pl.program_id(0) == 0
        dxn = dres_ref[...]
        if has_pre:
            dx, dg = _rms_bwd_math(xn_ref[...], gn_ref[...], dh_ref[...])
            dxn = dxn + dx

            @pl.when(first)
            def _():
                dgn_ref[...] = dg

            @pl.when(jnp.logical_not(first))
            def _():
                dgn_ref[...] += dg
        dxn_ref[...] = dxn
        if has_post:
            dy, dg = _rms_bwd_math(y_ref[...], gp_ref[...], dxn)
            dy_ref[...] = dy.astype(BF16)

            @pl.when(first)
            def _():
                dgp_ref[...] = dg

            @pl.when(jnp.logical_not(first))
            def _():
                dgp_ref[...] += dg

    ins, in_specs = [dres], [_row_spec()]
    if has_pre:
        ins += list(pre)
        in_specs += [_row_spec(), _vec_spec(), _row_spec()]
    if has_post:
        ins += list(post)
        in_specs += [_row_spec(), _vec_spec()]
    out_shape, out_specs = [jax.ShapeDtypeStruct((S, D), F32)], [_row_spec()]
    if has_post:
        out_shape.append(jax.ShapeDtypeStruct((S, D), BF16))
        out_specs.append(_row_spec())
    for _ in range(int(has_pre) + int(has_post)):
        out_shape.append(jax.ShapeDtypeStruct((1, D), F32))
        out_specs.append(_vec_spec())
    out = list(pl.pallas_call(
        body, name=name, grid=(S // TR,), in_specs=in_specs, out_specs=out_specs, out_shape=out_shape,
        compiler_params=_params(("arbitrary",)))(*ins))
    dxn = out.pop(0)
    dy = out.pop(0) if has_post else None
    dgn = out.pop(0) if has_pre else None
    dgp = out.pop(0) if has_post else None
    return dxn, dy, dgn, dgp


def _loss_kernel(y, target):
    def body(y_ref, t_ref, loss_ref, dy_ref):
        e = y_ref[...] - t_ref[...]
        dy_ref[...] = e * (1.0 / D)
        part = jnp.zeros((1, 128), F32) + 0.5 * jnp.sum(jnp.mean(e * e, axis=-1, keepdims=True))

        @pl.when(pl.program_id(0) == 0)
        def _():
            loss_ref[...] = part

        @pl.when(pl.program_id(0) > 0)
        def _():
            loss_ref[...] += part

    return pl.pallas_call(
        body, name="loss", grid=(S // TR,), in_specs=[_row_spec(), _row_spec()],
        out_specs=[_vec_spec(128), _row_spec()],
        out_shape=[jax.ShapeDtypeStruct((1, 128), F32), jax.ShapeDtypeStruct((S, D), F32)],
        compiler_params=_params(("arbitrary",)))(y, target)


def _t5_bucket_np(dist):
    max_exact = NUM_BUCKETS // 2
    nf = np.maximum(dist, 1).astype(np.float32)
    large = max_exact + (np.log(nf / max_exact) / np.float32(math.log(MAX_DISTANCE / max_exact))
                         * (NUM_BUCKETS - max_exact)).astype(np.int32)
    large = np.minimum(large, NUM_BUCKETS - 1)
    return np.where(dist < max_exact, dist, large).astype(np.int32)


def _bucket_maps():
    a = np.arange(BLK)[:, None]
    b = np.arange(2 * BLK)[None, :]
    dist = np.maximum(a + BLK - b, 0)
    maps = [_t5_bucket_np(dist * d) for _, d in A_GROUPS] + [_t5_bucket_np(dist)]
    return np.stack(maps).astype(np.int32)


def _pair_spec(col0):
    return pl.BlockSpec((S, 128), lambda p: (0, col0 + p))


def _band_rows(i, d):
    nb = S // d // BLK
    r, b = i // nb, i % nb
    cur = pl.ds(b * BLK * d + r, BLK, stride=d)
    prev = pl.ds(jnp.maximum(b - 1, 0) * BLK * d + r, BLK, stride=d)
    return cur, prev, jnp.minimum(b, 1)


def _band_bias(tab_ref, bi, h):
    bias = jnp.zeros((BLK, 2 * BLK), F32)
    for kk in range(NUM_BUCKETS):
        bias = jnp.where(bi == kk, tab_ref[kk, h], bias)
    return bias


def _lane_lo(rows=BLK):
    return lax.broadcasted_iota(jnp.int32, (rows, 128), 1) < HD


def _per_head(x, lo):
    return (jnp.sum(jnp.where(lo, x, 0.0), axis=1, keepdims=True) * (1.0 / HD),
            jnp.sum(jnp.where(lo, 0.0, x), axis=1, keepdims=True) * (1.0 / HD))


def _band_fill(bias_ref, tab_ref, bi, head, maxd):
    a = lax.broadcasted_iota(jnp.int32, (BLK, 2 * BLK), 0)
    c = lax.broadcasted_iota(jnp.int32, (BLK, 2 * BLK), 1)
    dist = a + BLK - c
    in_band = jnp.logical_and(dist >= 0, dist <= maxd)
    for h in range(2):
        bias = jnp.where(in_band, _band_bias(tab_ref, bi, head + h), NEG)
        bias_ref[1, h * BLK:(h + 1) * BLK, :] = bias
        bias_ref[0, h * BLK:(h + 1) * BLK, :] = jnp.where(c >= BLK, bias, NEG)


def _stack_heads(x, lo, dtype=BF16):
    return jnp.concatenate([jnp.where(lo, x, 0.0), jnp.where(lo, 0.0, x)], axis=0).astype(dtype)


def _unstack_heads(x, lo):
    n = x.shape[0] // 2
    return jnp.where(lo, x[:n], x[n:])


def _stack_rows(ref, prev, cur):
    return jnp.concatenate([ref[prev, :], ref[cur, :]], axis=0).astype(BF16)


def _blocks_of_group(group, dils, block):
    def run(d):
        lax.fori_loop(0, NQB, functools.partial(block, d), 0, unroll=2)

    if len(dils) == 1:
        run(dils[0])
        return
    for g, d in enumerate(dils):
        pl.when(group == g)(functools.partial(run, d))


def _band_fwd(name, dils, n_pairs, maxd, head0, srcs, bidx_g, tab, sinks):
    (qa, qc), (ka, kc), (va, vc) = srcs
    per_group = n_pairs // len(dils)
    out_spec = _pair_spec(0)
    smem = pl.BlockSpec(memory_space=pltpu.SMEM)
    full = pl.BlockSpec((len(dils), BLK, 2 * BLK), lambda p: (0, 0, 0))

    def body(tab_ref, sink_ref, q_ref, k_ref, v_ref, bidx_ref, o_ref, lse_ref, bias_ref):
        p = pl.program_id(0)
        _band_fill(bias_ref, tab_ref, bidx_ref[p // per_group], head0 + 2 * p, maxd)
        lo = _lane_lo()
        sink = jnp.where(lax.broadcasted_iota(jnp.int32, (2 * BLK, 1), 0) < BLK, sink_ref[2 * p], sink_ref[2 * p + 1])

        def block(d, i, carry):
            cur, prev, has_prev = _band_rows(i, d)
            qs = _stack_heads(q_ref[cur, :] * SCALE, lo)
            ks, vs = _stack_rows(k_ref, prev, cur), _stack_rows(v_ref, prev, cur)
            s = _dot(qs, ks, NT) + bias_ref[has_prev]
            m = jnp.max(s, axis=1, keepdims=True)
            pr = jnp.exp(s - m)
            l = jnp.sum(pr, axis=1, keepdims=True)
            num = _dot(pr.astype(BF16), vs, NN)
            lse = m + jnp.log(l)
            sig = 1.0 / (1.0 + jnp.exp(sink - lse))
            o_ref[cur, :] = _unstack_heads(num * (sig / l), lo)
            lse_ref[cur, :] = _unstack_heads(lse + jnp.zeros((2 * BLK, 128), F32), lo)
            return carry

        _blocks_of_group(p // per_group, dils, block)

    shape = jax.ShapeDtypeStruct((S, n_pairs * 128), F32)
    return pl.pallas_call(
        body, name=name, grid=(n_pairs,),
        in_specs=[smem, smem, _pair_spec(qc), _pair_spec(kc), _pair_spec(vc), full],
        out_specs=[out_spec, out_spec], out_shape=[shape, shape],
        scratch_shapes=[pltpu.VMEM((2, 2 * BLK, 2 * BLK), F32)],
        compiler_params=_params(("parallel",)))(tab, sinks, qa, ka, va, bidx_g)


def _band_bwd(name, dils, n_pairs, maxd, head0, srcs, bidx_g, tab, sinks, o, lse, do, stats_in):
    (qa, qc), (ka, kc), (va, vc) = srcs
    per_group = n_pairs // len(dils)
    pair = _pair_spec(0)
    shared = pl.BlockSpec((S, 128), lambda p: (0, p % per_group))
    smem = pl.BlockSpec(memory_space=pltpu.SMEM)
    full = pl.BlockSpec((len(dils), BLK, 2 * BLK), lambda p: (0, 0, 0))
    stat_spec = pl.BlockSpec((2, 8, 128), lambda p: (p, 0, 0))

    def body(tab_ref, sink_ref, q_ref, k_ref, v_ref, bidx_ref, o_ref, lse_ref, do_ref, sin_ref,
             dq_ref, dk_ref, dv_ref, stat_ref, bias_ref, dsacc_ref, sk_ref):
        p = pl.program_id(0)
        _band_fill(bias_ref, tab_ref, bidx_ref[p // per_group], head0 + 2 * p, maxd)
        dsacc_ref[...] = jnp.zeros_like(dsacc_ref)
        sk_ref[...] = jnp.zeros_like(sk_ref)
        dk_ref[...] = jnp.zeros_like(dk_ref)
        dv_ref[...] = jnp.zeros_like(dv_ref)
        lo = _lane_lo()
        head1 = lax.broadcasted_iota(jnp.int32, (2 * BLK, 1), 0) >= BLK
        sink = jnp.where(head1, sink_ref[2 * p + 1], sink_ref[2 * p])

        def block(d, i, carry):
            cur, prev, has_prev = _band_rows(i, d)
            qs = _stack_heads(q_ref[cur, :] * SCALE, lo)
            ks, vs = _stack_rows(k_ref, prev, cur), _stack_rows(v_ref, prev, cur)
            do = do_ref[cur, :]
            dos = _stack_heads(do, lo, F32)
            lse = jnp.concatenate(_per_head(lse_ref[cur, :], lo), axis=0)
            prod = do * o_ref[cur, :]
            delta = jnp.concatenate([jnp.sum(jnp.where(lo, prod, 0.0), axis=1, keepdims=True),
                                     jnp.sum(jnp.where(lo, 0.0, prod), axis=1, keepdims=True)], axis=0)
            sig = 1.0 / (1.0 + jnp.exp(sink - lse))
            pr = jnp.exp(_dot(qs, ks, NT) + bias_ref[has_prev] - lse)
            ds = pr * (sig * (_dot(dos.astype(BF16), vs, NT) - delta))
            dsb = ds.astype(BF16)
            dq_ref[cur, :] = SCALE * _unstack_heads(_dot(dsb, ks, NN), lo)
            dk = _dot(dsb, qs, TN)
            dv = _dot(pr.astype(BF16), (sig * dos).astype(BF16), TN)
            dk_ref[prev, :] += dk[:BLK]
            dk_ref[cur, :] += dk[BLK:]
            dv_ref[prev, :] += dv[:BLK]
            dv_ref[cur, :] += dv[BLK:]
            dsacc_ref[...] += ds
            sink_grad = -delta * (1.0 - sig)
            for h in range(2):
                sk_ref[h] += jnp.zeros((8, 128), F32) + jnp.sum(sink_grad[h * BLK:(h + 1) * BLK])
            return carry

        _blocks_of_group(p // per_group, dils, block)

        bi = bidx_ref[p // per_group]
        lane = lax.broadcasted_iota(jnp.int32, (8, 128), 1)
        sub = lax.broadcasted_iota(jnp.int32, (8, 128), 0)
        for h in range(2):
            acc = dsacc_ref[h * BLK:(h + 1) * BLK, :]
            row = jnp.where(jnp.logical_and(sub == 1, lane == 0), sk_ref[h], 0.0)
            for kk in range(NUM_BUCKETS):
                tot = jnp.sum(jnp.where(bi == kk, acc, 0.0))
                row = jnp.where(jnp.logical_and(sub == 0, lane == kk), tot, row)
            stat_ref[h] = row + jnp.where(sub == 0, sin_ref[h], 0.0)

    shape = jax.ShapeDtypeStruct((S, n_pairs * 128), F32)
    return pl.pallas_call(
        body, name=name, grid=(n_pairs,),
        in_specs=[smem, smem, _pair_spec(qc), _pair_spec(kc), _pair_spec(vc), full, shared, shared, shared, stat_spec],
        out_specs=[pair, pair, pair, stat_spec],
        out_shape=[shape, shape, shape, jax.ShapeDtypeStruct((2 * n_pairs, 8, 128), F32)],
        scratch_shapes=[pltpu.VMEM((2, 2 * BLK, 2 * BLK), F32), pltpu.VMEM((2 * BLK, 2 * BLK), F32),
                        pltpu.VMEM((2, 8, 128), F32)],
        compiler_params=_params(("parallel",)))(tab, sinks, qa, ka, va, bidx_g, o, lse, do, stats_in)


def _comb_fwd(o_g, lse_g):
    def body(o0, o1, o2, l0, l1, l2, out_ref, outb_ref, lse_ref):
        a0, a1, a2 = l0[...], l1[...], l2[...]
        m = jnp.maximum(jnp.maximum(a0, a1), a2)
        e0, e1, e2 = jnp.exp(a0 - m), jnp.exp(a1 - m), jnp.exp(a2 - m)
        tot = e0 + e1 + e2
        out = (e0 * o0[...] + e1 * o1[...] + e2 * o2[...]) / tot
        out_ref[...] = out
        outb_ref[...] = out.astype(BF16)
        lse_ref[...] = m + jnp.log(tot)

    spec = _row_spec(4 * HD)
    groups = [pl.BlockSpec((TR, 4 * HD), functools.partial(lambda i, g: (i, g), g=g)) for g in range(len(A_GROUPS))]
    f32 = jax.ShapeDtypeStruct((S, 4 * HD), F32)
    return pl.pallas_call(
        body, name="comb_fwd", grid=(S // TR,), in_specs=groups + groups, out_specs=[spec] * 3,
        out_shape=[f32, jax.ShapeDtypeStruct((S, 4 * HD), BF16), f32],
        compiler_params=_params(("parallel",)))(o_g, o_g, o_g, lse_g, lse_g, lse_g)


def _split2(x):
    hi = x.astype(BF16)
    return hi, (x - hi.astype(F32)).astype(BF16)


KB = 2 * BLK
SBQ = 2 * BLK


def _tri_sum(x, tri):
    hi, lo = _split2(x)
    both = _dot(jnp.concatenate([hi, lo], axis=0), tri, NN)
    return both[:x.shape[0]] + both[x.shape[0]:]


def _tri(strict_upper):
    r = lax.broadcasted_iota(jnp.int32, (KB, KB), 0)
    c = lax.broadcasted_iota(jnp.int32, (KB, KB), 1)
    return jnp.where(r > c if strict_upper else r < c, 1.0, 0.0).astype(BF16)


def _sb_terms(qs, kj, before):
    z = _dot(qs, kj, NT)
    lsp = jnp.minimum(z, 0.0) - jnp.log(1.0 + jnp.exp(-jnp.abs(z)))
    return lsp, _sb_keep(before, lsp - z)


def _sb_keep(before, x):
    return x if before is None else jnp.where(before, x, 0.0)


def _sb_before(i, m):
    t = (lax.broadcasted_iota(jnp.int32, (2 * SBQ, KB), 0) & (SBQ - 1)) + i * SBQ
    s = lax.broadcasted_iota(jnp.int32, (2 * SBQ, KB), 1) + m * KB
    return s < t


C_COL = 3072 // 128


def _sb_fwd(proj):
    blk = lambda off: pl.BlockSpec((SBQ, 128), lambda p, i: (i, off + p))
    col = lambda off: pl.BlockSpec((S, 128), lambda p, i: (0, off + p))
    out = pl.BlockSpec((SBQ, 128), lambda p, i: (i, p))

    def body(q_ref, k_ref, v_ref, o_ref, ob_ref, tot_ref):
        i = pl.program_id(1)
        lo = _lane_lo(SBQ)
        qs = _stack_heads(q_ref[...] * SCALE, lo)
        suffix = _tri(True)

        def step(n, carry, diagonal=False):
            acc, rest = carry
            m = i - n
            rows = pl.ds(pl.multiple_of(m * KB, KB), KB)
            kj, vj = k_ref[rows, :].astype(BF16), v_ref[rows, :].astype(BF16)
            before = _sb_before(i, m) if diagonal else None
            lsp, lk = _sb_terms(qs, kj, before)
            w = _sb_keep(before, jnp.exp(lsp + _tri_sum(lk, suffix) + rest))
            return acc + _dot(w.astype(BF16), vj, NN), rest + jnp.sum(lk, axis=1, keepdims=True)

        first = step(0, (jnp.zeros((2 * SBQ, 128), F32), jnp.zeros((2 * SBQ, 1), F32)), diagonal=True)
        acc, rest = lax.fori_loop(1, i + 1, step, first)
        o = _unstack_heads(acc, lo)
        o_ref[...] = o
        ob_ref[...] = o.astype(BF16)
        tot_ref[...] = _unstack_heads(rest + jnp.zeros((2 * SBQ, 128), F32), lo)

    f32 = jax.ShapeDtypeStruct((S, 4 * HD), F32)
    return pl.pallas_call(
        body, name="sb_fwd", grid=(2, S // SBQ), in_specs=[blk(C_COL), col(C_COL + 2), col(C_COL + 4)],
        out_specs=[out, out, out], out_shape=[f32, jax.ShapeDtypeStruct((S, 4 * HD), BF16), f32],
        compiler_params=_params(("parallel", "arbitrary")))(proj, proj, proj)


def _sb_bwd(proj, tot, do):
    blk = lambda off: pl.BlockSpec((SBQ, 128), lambda p, i: (i, off + p))
    col = lambda off: pl.BlockSpec((S, 128), lambda p, i: (0, off + p))

    def body(q_ref, k_ref, v_ref, tot_ref, do_ref, dq_ref, dk_ref, dv_ref):
        i = pl.program_id(1)

        @pl.when(i == 0)
        def _():
            dk_ref[...] = jnp.zeros_like(dk_ref)
            dv_ref[...] = jnp.zeros_like(dv_ref)

        lo = _lane_lo(SBQ)
        qs = _stack_heads(q_ref[...] * SCALE, lo)
        dos = _stack_heads(do_ref[...], lo)
        tots = jnp.concatenate(_per_head(tot_ref[...], lo), axis=0)
        prefix = _tri(False)

        def step(m, carry, diagonal=False):
            dq, keep_left, g_left = carry
            rows = pl.ds(pl.multiple_of(m * KB, KB), KB)
            kj, vj = k_ref[rows, :].astype(BF16), v_ref[rows, :].astype(BF16)
            before = _sb_before(i, m) if diagonal else None
            lsp, lk = _sb_terms(qs, kj, before)
            log_rest = tots - keep_left - lk - _tri_sum(lk, prefix)
            w = _sb_keep(before, jnp.exp(lsp + log_rest))
            g = w * _dot(dos, vj, NT)
            g_before = g_left + _dot(g.astype(BF16), prefix, NN)
            beta = jnp.exp(lsp)
            dz = _sb_keep(before, g * (1.0 - beta) - g_before * beta).astype(BF16)
            dk_ref[rows, :] += _dot(dz, qs, TN)
            dv_ref[rows, :] += _dot(w.astype(BF16), dos, TN)
            return (dq + _dot(dz, kj, NN), keep_left + jnp.sum(lk, axis=1, keepdims=True),
                    g_left + jnp.sum(g, axis=1, keepdims=True))

        zero = (jnp.zeros((2 * SBQ, 128), F32), jnp.zeros((2 * SBQ, 1), F32), jnp.zeros((2 * SBQ, 1), F32))
        dq, _, _ = step(i, lax.fori_loop(0, i, step, zero), diagonal=True)
        dq_ref[...] = SCALE * _unstack_heads(dq, lo)

    out_blk = pl.BlockSpec((SBQ, 128), lambda p, i: (i, p))
    out_col = pl.BlockSpec((S, 128), lambda p, i: (0, p))
    f32 = jax.ShapeDtypeStruct((S, 4 * HD), F32)
    return pl.pallas_call(
        body, name="sb_bwd", grid=(2, S // SBQ),
        in_specs=[blk(C_COL), col(C_COL + 2), col(C_COL + 4), out_blk, out_blk],
        out_specs=[out_blk, out_col, out_col], out_shape=[f32, f32, f32],
        compiler_params=_params(("arbitrary", "arbitrary")))(proj, proj, proj, tot, do)


TG = 256
TGR = 1024
GATE_BLK0 = OFF_GATE // TG


def _gate_specs():
    grid = (D // TG, S // TGR)
    p_specs = [pl.BlockSpec((TGR, TG), functools.partial(lambda c, r, br: (r, GATE_BLK0 + br * (D // TG) + c), br=br))
               for br in range(3)]
    b_spec = pl.BlockSpec((3, TG), lambda c, r: (0, c))
    t_spec = pl.BlockSpec((TGR, TG), lambda c, r: (r, c))
    return grid, p_specs, b_spec, t_spec


def _sigmoid(x):
    return 1.0 / (1.0 + jnp.exp(-x))


def _three_rows(rows):
    sub = lax.broadcasted_iota(jnp.int32, (3, rows[0].shape[1]), 0)
    return jnp.where(sub == 0, rows[0], jnp.where(sub == 1, rows[1], rows[2]))


def _gate_fwd(proj, b_gate, br):
    grid, p_specs, b_spec, t_spec = _gate_specs()

    def body(p0, p1, p2, b_ref, r0, r1, r2, out_ref):
        acc = jnp.zeros((TGR, TG), F32)
        for n, (p, r) in enumerate(((p0, r0), (p1, r1), (p2, r2))):
            acc += _sigmoid(p[...] + b_ref[n:n + 1, :]) * r[...]
        out_ref[...] = acc.astype(BF16)

    return pl.pallas_call(
        body, name="gate_fwd", grid=grid, in_specs=p_specs + [b_spec] + [t_spec] * 3, out_specs=t_spec,
        out_shape=jax.ShapeDtypeStruct((S, D), BF16),
        compiler_params=_params(("parallel", "parallel")))(proj, proj, proj, b_gate, *br)


def _gate_bwd(proj, b_gate, br, dmerged):
    grid, p_specs, b_spec, t_spec = _gate_specs()

    def body(p0, p1, p2, b_ref, r0, r1, r2, dm_ref, e0, e1, e2, g0, g1, g2, db_ref):
        dm = dm_ref[...]
        rows = []
        for n, (p, r, e_ref, dg_ref) in enumerate(((p0, r0, e0, g0), (p1, r1, e1, g1), (p2, r2, e2, g2))):
            g = _sigmoid(p[...] + b_ref[n:n + 1, :])
            e_ref[...] = (dm * g).astype(BF16)
            dpre = dm * r[...] * g * (1.0 - g)
            dg_ref[...] = dpre.astype(BF16)
            rows.append(jnp.sum(dpre, axis=0, keepdims=True))
        db = _three_rows(rows)

        @pl.when(pl.program_id(1) == 0)
        def _():
            db_ref[...] = db

        @pl.when(pl.program_id(1) > 0)
        def _():
            db_ref[...] += db

    bf = jax.ShapeDtypeStruct((S, D), BF16)
    out = pl.pallas_call(
        body, name="gate_bwd", grid=grid, in_specs=p_specs + [b_spec] + [t_spec] * 4,
        out_specs=[t_spec] * 6 + [b_spec], out_shape=[bf] * 6 + [jax.ShapeDtypeStruct((3, D), F32)],
        compiler_params=_params(("parallel", "arbitrary")))(proj, proj, proj, b_gate, *br, dmerged)
    return out[:3], out[3:6], out[6]


TC = 256
N_FF_BLK = D_FF // TC
GELU_C = math.sqrt(2.0 / math.pi)


def _shift_down(x, n):
    rows = lax.broadcasted_iota(jnp.int32, x.shape, 0)
    return jnp.where(rows >= n, pltpu.roll(x, n, axis=0), 0.0)


def _shift_up(x, n):
    rows = lax.broadcasted_iota(jnp.int32, x.shape, 0)
    return jnp.where(rows < x.shape[0] - n, pltpu.roll(x, x.shape[0] - n, axis=0), 0.0)


def _conv(u, w, b):
    s1, s2 = _shift_down(u, 1), _shift_down(u, 2)
    return w[2:3, :] * u + w[1:2, :] * s1 + w[0:1, :] * s2 + b, s1, s2


def _gelu_parts(x):
    inner = GELU_C * (x + 0.044715 * x * x * x)
    t = jnp.tanh(inner)
    gelu = 0.5 * x * (1.0 + t)
    dgelu = 0.5 * (1.0 + t) + 0.5 * x * (1.0 - t * t) * GELU_C * (1.0 + 3 * 0.044715 * x * x)
    return gelu, dgelu


def _conv_specs():
    ug = pl.BlockSpec((S, TC), lambda c: (0, c))
    uv = pl.BlockSpec((S, TC), lambda c: (0, N_FF_BLK + c))
    per_shard = 2 * N_FF_BLK // N_CHIPS
    wg = pl.BlockSpec((None, 3, TC), lambda c: (c // per_shard, 0, c % per_shard))
    wv = pl.BlockSpec((None, 3, TC), lambda c: ((N_FF_BLK + c) // per_shard, 0, (N_FF_BLK + c) % per_shard))
    bg = pl.BlockSpec((1, TC), lambda c: (0, c))
    bv = pl.BlockSpec((1, TC), lambda c: (0, N_FF_BLK + c))
    return ug, uv, wg, wv, bg, bv


def _conv_fwd(u, conv_w, conv_b):
    ug, uv, wg, wv, bg, bv = _conv_specs()

    def body(ug_ref, uv_ref, wg_ref, wv_ref, bg_ref, bv_ref, a_ref):
        gc = _conv(ug_ref[...], wg_ref[...], bg_ref[...])[0]
        vc = _conv(uv_ref[...], wv_ref[...], bv_ref[...])[0]
        a_ref[...] = (_gelu_parts(gc)[0] * vc).astype(BF16)

    return pl.pallas_call(
        body, name="conv_fwd", grid=(N_FF_BLK,), in_specs=[ug, uv, wg, wv, bg, bv], out_specs=ug,
        out_shape=jax.ShapeDtypeStruct((S, D_FF), BF16),
        compiler_params=_params(("parallel",)))(u, u, conv_w, conv_w, conv_b, conv_b)


def _conv_bwd(u, conv_w, conv_b, da):
    ug, uv, wg, wv, bg, bv = _conv_specs()
    dw_spec = pl.BlockSpec((3, TC), lambda c: (0, c))

    def back(duc, u, s1, s2, w):
        du = w[2:3, :] * duc + w[1:2, :] * _shift_up(duc, 1) + w[0:1, :] * _shift_up(duc, 2)
        dw = _three_rows([jnp.sum(duc * s2, axis=0, keepdims=True), jnp.sum(duc * s1, axis=0, keepdims=True),
                          jnp.sum(duc * u, axis=0, keepdims=True)])
        return du, dw, jnp.sum(duc, axis=0, keepdims=True)

    def body(ug_ref, uv_ref, wg_ref, wv_ref, bg_ref, bv_ref, da_ref, du_ref, dwg_ref, dwv_ref, dbg_ref, dbv_ref):
        u_g, u_v = ug_ref[...], uv_ref[...]
        gc, g1, g2 = _conv(u_g, wg_ref[...], bg_ref[...])
        vc, v1, v2 = _conv(u_v, wv_ref[...], bv_ref[...])
        gelu, dgelu = _gelu_parts(gc)
        da = da_ref[...]
        du, dw, db = back(da * vc * dgelu, u_g, g1, g2, wg_ref[...])
        du_ref[0] = du.astype(BF16)
        dwg_ref[...] = dw
        dbg_ref[...] = db
        du, dw, db = back(da * gelu, u_v, v1, v2, wv_ref[...])
        du_ref[1] = du.astype(BF16)
        dwv_ref[...] = dw
        dbv_ref[...] = db

    return pl.pallas_call(
        body, name="conv_bwd", grid=(N_FF_BLK,), in_specs=[ug, uv, wg, wv, bg, bv, ug],
        out_specs=[pl.BlockSpec((2, S, TC), lambda c: (0, 0, c)), dw_spec, dw_spec, bg, bg],
        out_shape=[jax.ShapeDtypeStruct((2, S, D_FF), BF16),
                   jax.ShapeDtypeStruct((3, D_FF), F32), jax.ShapeDtypeStruct((3, D_FF), F32),
                   jax.ShapeDtypeStruct((1, D_FF), F32), jax.ShapeDtypeStruct((1, D_FF), F32)],
        compiler_params=_params(("parallel",)))(u, u, conv_w, conv_w, conv_b, conv_b, da)


def _adamw(name, w, g, m, v):
    shape = w.shape
    cols = shape[-1]
    flat = [t.reshape(-1, cols) for t in (w, g, m, v)]
    r = flat[0].shape[0]
    tr = min(r, max(8, 2 * 1024 * 1024 // (4 * cols)))

    def body(w_ref, g_ref, m_ref, v_ref, go_ref, d_ref, mo_ref, vo_ref):
        g = g_ref[...]
        go_ref[...] = g
        m = ADAM_B1 * m_ref[...] + (1.0 - ADAM_B1) * g
        v = ADAM_B2 * v_ref[...] + (1.0 - ADAM_B2) * (g * g)
        m_hat = m / (1.0 - ADAM_B1 ** ADAM_STEP)
        v_hat = v / (1.0 - ADAM_B2 ** ADAM_STEP)
        d_ref[...] = -ADAM_LR * (m_hat / (jnp.sqrt(v_hat) + ADAM_EPS) + ADAM_WD * w_ref[...])
        mo_ref[...] = m
        vo_ref[...] = v

    spec = pl.BlockSpec((tr, cols), lambda i: (i, 0))
    outs = pl.pallas_call(
        body, name=name, grid=(pl.cdiv(r, tr),), in_specs=[spec] * 4, out_specs=[spec] * 4,
        out_shape=[jax.ShapeDtypeStruct((r, cols), F32)] * 4, compiler_params=_params(("parallel",)))(*flat)
    return [t.reshape(shape) for t in outs]


def _place():
    x, y, c = lax.axis_index("x"), lax.axis_index("y"), lax.axis_index("c")
    chips = [(1 - x, y), (x, 1 - y), (1 - x, 1 - y)]
    return x, y, c, chips


def _scalars(*vals):
    return jnp.stack([jnp.asarray(v, jnp.int32) for v in vals])


HBM = pl.BlockSpec(memory_space=pltpu.HBM)
SEM = pl.BlockSpec(memory_space=pltpu.SEMAPHORE)
SPLIT_COPY = pltpu.CompilerParams(has_side_effects=pltpu.SideEffectType.DATAFLOW_SIDE_EFFECTING)


def _in_hbm(x):
    return pltpu.with_memory_space_constraint(x, pltpu.HBM)


def _cast_into_slot(name, w, layer, chip):
    _, k, n4 = w.shape
    tr = max(t for t in range(16, 513, 16) if k % t == 0)

    def body(chip_ref, w_ref, o_ref):
        o_ref[...] = w_ref[...].astype(BF16)

    return pl.pallas_call(
        body, name=name,
        grid_spec=pltpu.PrefetchScalarGridSpec(
            num_scalar_prefetch=1, grid=(k // tr,),
            in_specs=[pl.BlockSpec((None, tr, n4), lambda i, s: (layer, i, 0))],
            out_specs=pl.BlockSpec((None, tr, n4), lambda i, s: (s[0], i, 0))),
        out_shape=jax.ShapeDtypeStruct((N_CHIPS, k, n4), BF16),
        compiler_params=_params(("parallel",)))(_scalars(chip), w)


def _gather_copy(buf_ref, k, from_chip, send_sem, recv_sem, chips, c, half=False):
    rows = buf_ref.at[from_chip]
    if half:
        h = buf_ref.shape[1] // 2
        rows = buf_ref.at[from_chip, pl.ds(pl.multiple_of(c * h, h), h)]
    return pltpu.make_async_remote_copy(src_ref=rows, dst_ref=rows, send_sem=send_sem, recv_sem=recv_sem,
                                        device_id=(*chips[k], c), device_id_type=MESH)


def _gather_start(name, bufs, groups, halved=()):
    n, ng = len(bufs), len(groups)
    where = {a: (gi, e) for gi, g in enumerate(groups) for e, a in enumerate(g)}

    def body(*refs):
        ins, sems, token = refs[:n], refs[n:n + 2 * ng], refs[-1]
        x, y, c, chips = _place()
        for a in range(n):
            gi, e = where[a]
            for k in range(3):
                _gather_copy(ins[a], k, 2 * x + y, sems[2 * gi].at[3 * e + k], sems[2 * gi + 1].at[3 * e + k],
                             chips, c, a in halved).start()
        token[...] = jnp.zeros_like(token)

    out_shape = [pltpu.SemaphoreType.DMA((3 * len(g),)) for g in groups for _ in range(2)]
    out_shape += [pltpu.HBM(b.shape, b.dtype) for b in bufs] + [jax.ShapeDtypeStruct((8, 128), F32)]
    out = pl.pallas_call(
        body, name=name, in_specs=[HBM] * n,
        out_specs=[SEM] * (2 * ng) + [HBM] * n + [pl.BlockSpec(memory_space=pltpu.VMEM)], out_shape=out_shape,
        input_output_aliases={a: 2 * ng + a for a in range(n)}, compiler_params=SPLIT_COPY)(*[_in_hbm(b) for b in bufs])
    sems = [(out[2 * gi], out[2 * gi + 1]) for gi in range(ng)]
    return sems, list(out[2 * ng:2 * ng + n]), out[-1]


def _gather_wait(name, bufs, send, recv, after, halved=()):
    n = len(bufs)

    def body(*refs):
        ins, send_sem, recv_sem = refs[:n], refs[n], refs[n + 1]
        x, y, c, chips = _place()
        for e in range(n):
            for k in range(3):
                sems = (send_sem.at[3 * e + k], recv_sem.at[3 * e + k])
                _gather_copy(ins[e], k, 2 * x + y, *sems, chips, c, e in halved).wait_send()
                _gather_copy(ins[e], k, 2 * chips[k][0] + chips[k][1], *sems, chips, c, e in halved).wait_recv()

    return pl.pallas_call(
        body, name=name, in_specs=[HBM] * n + [SEM, SEM, ANY], out_specs=[HBM] * n,
        out_shape=[pltpu.HBM(b.shape, b.dtype) for b in bufs],
        input_output_aliases={a: a for a in range(n)}, compiler_params=SPLIT_COPY)(*bufs, send, recv, after)


def _swap_halves(name, bufs):
    n = len(bufs)

    def body(*refs):
        ins, outs = refs[:n], refs[n:2 * n]
        send_sem, recv_sem = refs[2 * n:]
        x, y, c, chips = _place()

        def piece(ref, k, which):
            h = ref.shape[1] // 2
            return ref.at[2 * chips[k][0] + chips[k][1], pl.ds(pl.multiple_of(which * h, h), h)]

        def copy(a, k, which):
            return pltpu.make_async_remote_copy(
                src_ref=piece(ins[a], k, c), dst_ref=piece(outs[a], k, which), send_sem=send_sem.at[3 * a + k],
                recv_sem=recv_sem.at[3 * a + k], device_id=(x, y, 1 - c), device_id_type=MESH)

        for a in range(n):
            for k in range(3):
                copy(a, k, c).start()
        for a in range(n):
            for k in range(3):
                copy(a, k, c).wait_send()
                copy(a, k, 1 - c).wait_recv()

    return pl.pallas_call(
        body, name=name, in_specs=[ANY] * n, out_specs=[ANY] * n,
        out_shape=[jax.ShapeDtypeStruct(b.shape, b.dtype) for b in bufs],
        input_output_aliases={a: a for a in range(n)},
        scratch_shapes=[pltpu.SemaphoreType.DMA((3 * n,)), pltpu.SemaphoreType.DMA((3 * n,))],
    )(*bufs)


def _reduce_copy(g_ref, land_ref, mask, send_sem, recv_sem, x, y, c, sending):
    px, py, pc = x ^ ((mask >> 2) & 1), y ^ ((mask >> 1) & 1), c ^ (mask & 1)
    half = g_ref.shape[1] // 2
    src = g_ref.at[2 * px + py, pl.ds(pl.multiple_of(pc * half, half), half)]
    dst = land_ref.at[4 * x + 2 * y + c] if sending else land_ref.at[4 * px + 2 * py + pc]
    return pltpu.make_async_remote_copy(src_ref=src, dst_ref=dst, send_sem=send_sem, recv_sem=recv_sem,
                                        device_id=(px, py, pc), device_id_type=MESH)


def _reduce_start(name, grads):
    n = len(grads)
    lands = [lax.empty((N_DEV, g.shape[1] // 2, g.shape[2]), g.dtype) for g in grads]

    def body(*refs):
        gs, ls, send_sem, recv_sem = refs[:n], refs[n:2 * n], refs[2 * n], refs[2 * n + 1]
        x, y, c, _ = _place()
        for a in range(n):
            for mask in range(1, N_DEV):
                s = (N_DEV - 1) * a + mask - 1
                _reduce_copy(gs[a], ls[a], mask, send_sem.at[s], recv_sem.at[s], x, y, c, True).start()
        refs[-1][...] = jnp.zeros_like(refs[-1])

    sem = pltpu.SemaphoreType.DMA((n * (N_DEV - 1),))
    out = pl.pallas_call(
        body, name=name, in_specs=[HBM] * (2 * n),
        out_specs=[SEM, SEM] + [HBM] * (2 * n) + [pl.BlockSpec(memory_space=pltpu.VMEM)],
        out_shape=[sem, sem] + [pltpu.HBM(t.shape, t.dtype) for t in grads + lands] + [jax.ShapeDtypeStruct((8, 128), F32)],
        input_output_aliases={a: 2 + a for a in range(2 * n)}, compiler_params=SPLIT_COPY)(
            *[_in_hbm(t) for t in grads + lands])
    return out[0], out[1], list(out[2:2 + n]), list(out[2 + n:2 + 2 * n]), out[-1]


def _reduce_wait(name, send, recv, grads, lands, after):
    n = len(grads)

    def body(*refs):
        gs, ls, send_sem, recv_sem = refs[:n], refs[n:2 * n], refs[2 * n], refs[2 * n + 1]
        x, y, c, _ = _place()
        for a in range(n):
            for mask in range(1, N_DEV):
                s = (N_DEV - 1) * a + mask - 1
                sems = (send_sem.at[s], recv_sem.at[s])
                _reduce_copy(gs[a], ls[a], mask, *sems, x, y, c, True).wait_send()
                _reduce_copy(gs[a], ls[a], mask, *sems, x, y, c, False).wait_recv()

    out = pl.pallas_call(
        body, name=name, in_specs=[HBM] * (2 * n) + [SEM, SEM, ANY], out_specs=[HBM] * (2 * n),
        out_shape=[pltpu.HBM(t.shape, t.dtype) for t in grads + lands],
        input_output_aliases={a: a for a in range(2 * n)}, compiler_params=SPLIT_COPY)(*grads, *lands, send, recv, after)
    return list(out[:n]), list(out[n:])


def _reduce_sum(name, g, land, layer, into, chip, c):
    _, k4, n4 = g.shape
    half = k4 // 2
    tr = max(t for t in range(16, 513, 16) if half % t == 0)
    per = half // tr
    me = 2 * chip + c

    def body(s_ref, own_ref, *refs):
        total = own_ref[...].astype(F32)
        for ref in refs[:N_DEV - 1]:
            total = total + ref[...].astype(F32)
        refs[-1][...] = total

    in_specs = [pl.BlockSpec((None, tr, n4), lambda i, s: (s[0], s[1] * per + i, 0))]
    in_specs += [pl.BlockSpec((None, tr, n4), functools.partial(lambda i, s, m: (s[1 + m], i, 0), m=m))
                 for m in range(1, N_DEV)]
    ins = [g] + [land] * (N_DEV - 1)
    aliases = {}
    if into is not None:
        in_specs, ins, aliases = in_specs + [ANY], ins + [into], {1 + N_DEV: 0}
    return pl.pallas_call(
        body, name=name,
        grid_spec=pltpu.PrefetchScalarGridSpec(
            num_scalar_prefetch=1, grid=(per,), in_specs=in_specs,
            out_specs=pl.BlockSpec((None, tr, n4), lambda i, s: (layer, s[1] * per + i, 0))),
        out_shape=jax.ShapeDtypeStruct((DEPTH, k4, n4), F32), input_output_aliases=aliases,
        compiler_params=_params(("parallel",)))(_scalars(chip, c, *[me ^ m for m in range(1, N_DEV)]), *ins)


def _join_halves(name, bufs):
    n = len(bufs)

    def body(*refs):
        ins, outs = refs[:n], refs[n:2 * n]
        send_sem, recv_sem = refs[2 * n:]
        x, y, c, _ = _place()

        def rows(ref, which):
            half = ref.shape[1] // 2
            return ref.at[:, pl.ds(pl.multiple_of(which * half, half), half)]

        sends = [pltpu.make_async_remote_copy(
            src_ref=rows(ins[a], c), dst_ref=rows(outs[a], c), send_sem=send_sem.at[a], recv_sem=recv_sem.at[a],
            device_id=(x, y, 1 - c), device_id_type=MESH) for a in range(n)]
        for cp in sends:
            cp.start()
        for a in range(n):
            sends[a].wait_send()
            pltpu.make_async_remote_copy(
                src_ref=rows(ins[a], c), dst_ref=rows(outs[a], 1 - c), send_sem=send_sem.at[a], recv_sem=recv_sem.at[a],
                device_id=(x, y, 1 - c), device_id_type=MESH).wait_recv()

    return pl.pallas_call(
        body, name=name, in_specs=[ANY] * n, out_specs=[ANY] * n,
        out_shape=[jax.ShapeDtypeStruct(b.shape, b.dtype) for b in bufs],
        input_output_aliases={a: a for a in range(n)},
        scratch_shapes=[pltpu.SemaphoreType.DMA((n,)), pltpu.SemaphoreType.DMA((n,))],
    )(*bufs)


def _small_copy(b_ref, l_ref, mask, send_sem, recv_sem, x, y, c, sending):
    px, py, pc = x ^ ((mask >> 2) & 1), y ^ ((mask >> 1) & 1), c ^ (mask & 1)
    dst = l_ref.at[4 * x + 2 * y + c] if sending else l_ref.at[4 * px + 2 * py + pc]
    return pltpu.make_async_remote_copy(src_ref=b_ref, dst_ref=dst, send_sem=send_sem.at[mask - 1],
                                        recv_sem=recv_sem.at[mask - 1], device_id=(px, py, pc), device_id_type=MESH)


def _small_start(block):
    land = lax.empty((N_DEV,) + block.shape, block.dtype)

    def body(b_ref, l_ref, send_sem, recv_sem, b_thru, l_thru, token):
        x, y, c, _ = _place()
        for mask in range(1, N_DEV):
            _small_copy(b_ref, l_ref, mask, send_sem, recv_sem, x, y, c, True).start()
        token[...] = jnp.zeros_like(token)

    sem = pltpu.SemaphoreType.DMA((N_DEV - 1,))
    return pl.pallas_call(
        body, name="small_start", in_specs=[HBM, HBM],
        out_specs=[SEM, SEM, HBM, HBM, pl.BlockSpec(memory_space=pltpu.VMEM)],
        out_shape=[sem, sem, pltpu.HBM(block.shape, block.dtype), pltpu.HBM(land.shape, land.dtype),
                   jax.ShapeDtypeStruct((8, 128), F32)],
        input_output_aliases={0: 2, 1: 3}, compiler_params=SPLIT_COPY)(_in_hbm(block), _in_hbm(land))


def _small_wait(send, recv, block, land, after):
    def body(b_ref, l_ref, send_sem, recv_sem, after_ref, b_out, l_out):
        x, y, c, _ = _place()
        for mask in range(1, N_DEV):
            _small_copy(b_ref, l_ref, mask, send_sem, recv_sem, x, y, c, True).wait_send()
            _small_copy(b_ref, l_ref, mask, send_sem, recv_sem, x, y, c, False).wait_recv()

    return pl.pallas_call(
        body, name="small_wait", in_specs=[HBM, HBM, SEM, SEM, ANY], out_specs=[HBM, HBM],
        out_shape=[pltpu.HBM(block.shape, block.dtype), pltpu.HBM(land.shape, land.dtype)],
        input_output_aliases={0: 0, 1: 1}, compiler_params=SPLIT_COPY)(block, land, send, recv, after)


def _small_sum(land):
    def body(l_ref, out_ref):
        total = l_ref[0]
        for d in range(1, N_DEV):
            total = total + l_ref[d]
        out_ref[...] = total

    vmem = pl.BlockSpec(memory_space=pltpu.VMEM)
    return pl.pallas_call(
        body, name="small_sum", in_specs=[vmem], out_specs=vmem,
        out_shape=jax.ShapeDtypeStruct(land.shape[1:], F32),
        compiler_params=pltpu.CompilerParams(vmem_limit_bytes=VMEM_LIMIT))(land)


B_Q_COL = 2304 // 128
B_K0, B_V0, B_END = 2816, 2944, 3072


def _full_cols(w_g):
    return w_g.transpose(1, 0, 2).reshape(w_g.shape[1], -1)


A_DILS = tuple(d for _, d in A_GROUPS)
A_PAIRS = N_A // 2


def _src_a(proj):
    return ((proj, 0), (proj, A_PAIRS), (proj, 2 * A_PAIRS))


def _kv_expand(kv):
    return jnp.broadcast_to(kv.reshape(S, 2, 1, HD), (S, 2, 4, HD)).reshape(S, 8 * HD)


def _kv_reduce(dkv):
    return dkv.reshape(S, 2, 4, HD).sum(axis=2).reshape(S, 2 * HD)


def _mixer_fwd(h1, wget, rel_bias, sinks_l, bidx):
    w = dict(wget(0, h1))
    proj = _mm_nt("proj_in", h1, w["w_in"], F32, tm=S, tn=1152)
    no_sinks = jnp.full((N_A,), NEG, F32)
    o_g, lse_g = _band_fwd("band_fwd_a", A_DILS, A_PAIRS, BLK, 0, _src_a(proj), bidx[:3], rel_bias, no_sinks)
    o_a32, o_a, lse_a = _comb_fwd(o_g, lse_g)
    src_b = ((proj, B_Q_COL), (_kv_expand(proj[:, B_K0:B_V0]), 0), (_kv_expand(proj[:, B_V0:B_END]), 0))
    o_b32, lse_b = _band_fwd("band_fwd_b", (1,), 4, BLK - 1, N_A, src_b, bidx[3:], rel_bias, sinks_l)
    o_b = o_b32.astype(BF16)
    o_c32, o_c, tot_c = _sb_fwd(proj)
    w.update(wget(1, o_c32))
    br = [_mm_nn("branch_a", o_a, w["w_br_a"], F32, tm=S), _mm_nn("branch_b", o_b, w["w_br_b"], F32, tm=S),
          _mm_nn("branch_c", o_c, w["w_br_c"], F32, tm=S)]
    merged = _gate_fwd(proj, w["b_gate"], br)
    mo = _mm_nn("out_proj", merged, w["w_out"], F32, tm=S)
    saved = dict(proj=proj, src_b=src_b, o_a32=o_a32, lse_a=lse_a, o_b32=o_b32, lse_b=lse_b, tot_c=tot_c,
                 o_a=o_a, o_b=o_b, o_c=o_c, br=br, merged=merged)
    return mo, saved, w


def _mixer_bwd(d_mo, h1, w, sv, rel_bias, sinks_l, bidx, stats_in, emit):
    grads = {}
    dmerged = _mm_nt("out_proj_dx", d_mo, w["w_out"], F32, tm=S)
    grads["w_out"] = _mm_tn_sharded("out_proj_dw", sv["merged"], d_mo, True)
    e, dgate, db_gate = _gate_bwd(sv["proj"], w["b_gate"], sv["br"], dmerged)
    grads["b_gate"] = db_gate
    d_o = {}
    for n, name in enumerate("abc"):
        d_o[name] = _mm_nt("branch_%s_dx" % name, e[n], w["w_br_" + name], F32, tm=S)
        grads["w_br_" + name] = _mm_tn_sharded("branch_%s_dw" % name, sv["o_" + name], e[n], False)
    zero = emit(1, grads)
    no_sinks = jnp.full((N_A,), NEG, F32) + zero[0]
    dq_a, dk_a, dv_a, st_a = _band_bwd("band_bwd_a", A_DILS, A_PAIRS, BLK, 0, _src_a(sv["proj"]), bidx[:3], rel_bias,
                                       no_sinks, sv["o_a32"], sv["lse_a"], d_o["a"], stats_in[:N_A])
    dq_b, dk_x, dv_x, st_b = _band_bwd("band_bwd_b", (1,), 4, BLK - 1, N_A, sv["src_b"], bidx[3:], rel_bias, sinks_l,
                                       sv["o_b32"], sv["lse_b"], d_o["b"], stats_in[N_A:])
    stats = jnp.concatenate([st_a, st_b], axis=0)
    dcq, dck, dcv = _sb_bwd(sv["proj"], sv["tot_c"], d_o["c"])
    cols = [dq_a, dk_a, dv_a, dq_b, _kv_reduce(dk_x), _kv_reduce(dv_x), dcq, dck, dcv]
    dproj = jnp.concatenate([t.astype(BF16) for t in cols] + list(dgate), axis=1)
    grads["w_in"] = _mm_tn("proj_in_dw", dproj, h1, BF16, tm=1152, tn=1024).reshape(N_CHIPS, IN_SHARD, D)
    zero = emit(2, grads)
    dh1 = _mm_nn("proj_in_dx", dproj, w["w_in"], F32, tm=S, tk=2304)
    return dh1, grads, stats, zero


def _ffn_fwd(h2, w):
    u = _mm_nn("ffn_up", h2, w["w_up"], F32, tm=S, tn=1024)
    a = _conv_fwd(u, w["conv_w"], w["conv_b"])
    dn = _mm_nn("ffn_down", a, w["w_down"], F32, tm=1024)
    return dn, dict(u=u, a=a)


def _ffn_bwd(d_dn, h2, w, sv):
    grads = {}
    da = _mm_nt("ffn_down_dx", d_dn, w["w_down"], F32, tm=S, tn=1024)
    grads["w_down"] = _mm_tn_sharded("ffn_down_dw", sv["a"], d_dn, True, tm=1024, tn=1024)
    du, dwg, dwv, dbg, dbv = _conv_bwd(sv["u"], w["conv_w"], w["conv_b"], da)
    grads["conv_w"] = jnp.concatenate([dwg, dwv], axis=1)
    grads["conv_b"] = jnp.concatenate([dbg, dbv], axis=1)
    dh2 = _mm_nt("ffn_up_dx", du, w["w_up"], F32, tm=S, tk=2048)
    grads["w_up"] = _mm_tn_sharded("ffn_up_dw", h2, du, False, tm=1024, tn=1024)
    return dh2, grads


BIG = ("w_in", "w_br_a", "w_br_b", "w_br_c", "w_out", "w_up", "w_down")


def _shard_view(name, w):
    return jnp.swapaxes(w, 1, 2) if name == "w_in" else w
WEIGHT_GROUPS = (("w_in", "b_gate"), ("w_br_a", "w_br_b", "w_br_c", "w_out"), ("w_up", "conv_w", "w_down"))
GRAD_GROUPS = (("w_down", "w_up"), ("w_out", "w_br_a", "w_br_b", "w_br_c"), ("w_in",))
SMALL_ROWS = (("rel_bias", 8), ("attn_pre_norm", 16), ("attn_post_norm", 16), ("ffn_pre_norm", 16), ("ffn_post_norm", 16),
              ("sinks", 8), ("conv_b", 128), ("b_gate", 48), ("conv_w", 384), ("loss", 8))


def _pack_small(vals):
    rows = []
    for name, n in SMALL_ROWS:
        flat = vals[name].reshape(-1).astype(F32)
        rows.append(jnp.pad(flat, (0, n * 128 - flat.shape[0])).reshape(n, 128))
    return jnp.concatenate(rows, axis=0)


def _unpack_small(block, shapes):
    out, row = {}, 0
    for name, n in SMALL_ROWS:
        size = int(np.prod(shapes[name]))
        out[name] = block[row:row + n].reshape(-1)[:size].reshape(shapes[name])
        row += n
    return out


def kernel(x, rel_bias, attn_pre_norm, w_in, b_gate, sinks, w_br_a, w_br_b, w_br_c, w_out, attn_post_norm, ffn_pre_norm, w_up, conv_w, conv_b, w_down, ffn_post_norm, loss_target, m_rel_bias, m_attn_pre_norm, m_w_in, m_b_gate, m_sinks, m_w_br_a, m_w_br_b, m_w_br_c, m_w_out, m_attn_post_norm, m_ffn_pre_norm, m_w_up, m_conv_w, m_conv_b, m_w_down, m_ffn_post_norm, v_rel_bias, v_attn_pre_norm, v_w_in, v_b_gate, v_sinks, v_w_br_a, v_w_br_b, v_w_br_c, v_w_out, v_attn_post_norm, v_ffn_pre_norm, v_w_up, v_conv_w, v_conv_b, v_w_down, v_ffn_post_norm):
    names = ("rel_bias", "attn_pre_norm", "w_in", "b_gate", "sinks", "w_br_a", "w_br_b", "w_br_c", "w_out",
             "attn_post_norm", "ffn_pre_norm", "w_up", "conv_w", "conv_b", "w_down", "ffn_post_norm")
    weights = dict(zip(names, (rel_bias, attn_pre_norm, w_in, b_gate, sinks, w_br_a, w_br_b, w_br_c, w_out,
                               attn_post_norm, ffn_pre_norm, w_up, conv_w, conv_b, w_down, ffn_post_norm)))
    mom1 = dict(zip(names, (m_rel_bias, m_attn_pre_norm, m_w_in, m_b_gate, m_sinks, m_w_br_a, m_w_br_b, m_w_br_c,
                            m_w_out, m_attn_post_norm, m_ffn_pre_norm, m_w_up, m_conv_w, m_conv_b, m_w_down,
                            m_ffn_post_norm)))
    mom2 = dict(zip(names, (v_rel_bias, v_attn_pre_norm, v_w_in, v_b_gate, v_sinks, v_w_br_a, v_w_br_b, v_w_br_c,
                            v_w_out, v_attn_post_norm, v_ffn_pre_norm, v_w_up, v_conv_w, v_conv_b, v_w_down,
                            v_ffn_post_norm)))

    chip = 2 * lax.axis_index("x") + lax.axis_index("y")
    core = lax.axis_index("c")

    keys = [(n, l) for l in range(DEPTH) for group in WEIGHT_GROUPS for n in group]
    groups = [[keys.index((n, l)) for n in group] for l in range(DEPTH) for group in WEIGHT_GROUPS]

    def slot_buffer(n, l):
        if n in BIG:
            return _cast_into_slot("cast_" + n, _shard_view(n, weights[n]), l, chip)
        shard = weights[n][l]
        return lax.dynamic_update_slice(jnp.zeros((N_CHIPS,) + shard.shape, F32), shard[None],
                                        (chip, jnp.int32(0), jnp.int32(0)))

    by_halves = [keys.index(k) for k in (("w_in", 0), ("w_up", DEPTH - 1), ("w_down", DEPTH - 1))]
    n_first = len(groups[0])
    sems, in_flight, _ = _gather_start("gather_start_first", [slot_buffer(*k) for k in keys[:n_first]], groups[:1],
                                       tuple(a for a in by_halves if a < n_first))
    more = _gather_start("gather_start", [slot_buffer(*k) for k in keys[n_first:]],
                         [[a - n_first for a in g] for g in groups[1:]],
                         tuple(a - n_first for a in by_halves if a >= n_first))
    sems, in_flight, started = sems + more[0], in_flight + more[1], more[2]

    def wget(l, gi, after):
        g = l * len(WEIGHT_GROUPS) + gi
        after = started if g == 0 else after
        halved = tuple(e for e, a in enumerate(groups[g]) if a in by_halves)
        got = list(_gather_wait("gather_wait_%d_%d" % (l, gi), [in_flight[a] for a in groups[g]], *sems[g], after,
                                halved))
        if halved:
            for e, buf in zip(halved, _swap_halves("swap_halves_%d_%d" % (l, gi), [got[e] for e in halved])):
                got[e] = buf
        out = {}
        for n, buf in zip(WEIGHT_GROUPS[gi], got):
            if n in ("w_in", "w_out", "w_down"):
                out[n] = buf.reshape(-1, buf.shape[-1])
            else:
                out[n] = buf if n == "conv_w" else _full_cols(buf)
        if gi == len(WEIGHT_GROUPS) - 1:
            out["conv_b"] = conv_b[l:l + 1]
        return out

    pending = []

    def emit(l, gi, grads):
        group = GRAD_GROUPS[gi]
        *started, token = _reduce_start("reduce_start_%d_%d" % (l, gi), [grads[n] for n in group])
        pending.append((l, group) + tuple(started))
        return token[:1, :1]

    local = _local_step(x.reshape(S, D), loss_target.reshape(S, D), wget, emit, rel_bias, sinks, attn_pre_norm,
                        attn_post_norm, ffn_pre_norm, ffn_post_norm)
    return _reduce_and_update(x.shape, names, weights, mom1, mom2, chip, core, pending, *local)


def _local_step(xs, target, wget, emit, rel_bias, sinks, attn_pre_norm, attn_post_norm, ffn_pre_norm, ffn_post_norm):
    bidx = jnp.asarray(_bucket_maps())

    saved, layers = [], []
    h1 = _rms_fwd("pre_norm_first", xs, attn_pre_norm[0:1])
    x_in = xs
    for l in range(DEPTH):
        mo, sv_mix, w = _mixer_fwd(h1, functools.partial(wget, l), rel_bias, sinks[l], bidx)
        x_mid, h2 = _post_pre_fwd("post_attn_norm", x_in, mo, attn_post_norm[l:l + 1], ffn_pre_norm[l:l + 1])
        w.update(wget(l, 2, h2))
        dn, sv_ffn = _ffn_fwd(h2, w)
        g_next = attn_pre_norm[l + 1:l + 2] if l + 1 < DEPTH else None
        x_out, h1_next = _post_pre_fwd("post_ffn_norm" if l + 1 < DEPTH else "post_ffn_norm_last", x_mid, dn,
                                       ffn_post_norm[l:l + 1], g_next)
        saved.append(dict(x_in=x_in, h1=h1, mo=mo, x_mid=x_mid, h2=h2, dn=dn, mix=sv_mix, ffn=sv_ffn))
        layers.append(w)
        x_in, h1 = x_out, h1_next

    loss_row, dres = _loss_kernel(x_in, target)

    small = [None] * DEPTH
    stats = jnp.zeros((N_BAND_Q, 8, 128), F32)
    dh_next = None
    for l in reversed(range(DEPTH)):
        w, sv = layers[l], saved[l]
        if l + 1 < DEPTH:
            pre = (saved[l + 1]["x_in"], attn_pre_norm[l + 1:l + 2] + zero, dh_next)
            dres, d_dn, dg_pre_next, dg_fpost = _norm_bwd("post_ffn_norm_bwd", dres, pre,
                                                          (sv["dn"], ffn_post_norm[l:l + 1]))
            small[l + 1]["attn_pre_norm"] = dg_pre_next
        else:
            dres, d_dn, _, dg_fpost = _norm_bwd("post_ffn_norm_last_bwd", dres, None, (sv["dn"], ffn_post_norm[l:l + 1]))
        dh2, g_ffn = _ffn_bwd(d_dn, sv["h2"], w, sv["ffn"])
        zero = emit(l, 0, g_ffn)
        dres, d_mo, dg_fpre, dg_apost = _norm_bwd("post_attn_norm_bwd", dres,
                                                  (sv["x_mid"], ffn_pre_norm[l:l + 1] + zero, dh2),
                                                  (sv["mo"], attn_post_norm[l:l + 1]))
        dh_next, g_mix, stats, zero = _mixer_bwd(d_mo, sv["h1"], w, sv["mix"], rel_bias, sinks[l], bidx, stats,
                                                 functools.partial(emit, l))
        small[l] = dict(ffn_post_norm=dg_fpost, ffn_pre_norm=dg_fpre, attn_post_norm=dg_apost,
                        sinks=stats[N_A:, 1, 0], conv_b=g_ffn["conv_b"], b_gate=g_mix["b_gate"], conv_w=g_ffn["conv_w"])
    grad_x, _, dg_pre0, _ = _norm_bwd("pre_norm_first_bwd", dres, (saved[0]["x_in"], attn_pre_norm[0:1] + zero, dh_next),
                                      None)
    small[0]["attn_pre_norm"] = dg_pre0
    return loss_row, grad_x, small, stats


def _reduce_and_update(x_shape, names, weights, mom1, mom2, chip, core, pending, loss_row, grad_x, small, stats):
    delta, new_m, new_v, grads = {}, {}, {}, {}

    def update(n, g):
        grads[n], delta[n], new_m[n], new_v[n] = _adamw("adamw_" + n, _shard_view(n, weights[n]), g,
                                                        _shard_view(n, mom1[n]), _shard_view(n, mom2[n]))

    small_vals = {n: jnp.stack([small[l][n].reshape(weights[n].shape[1:]) for l in range(DEPTH)])
                  for n in ("attn_pre_norm", "attn_post_norm", "ffn_pre_norm", "ffn_post_norm", "conv_b", "sinks")}
    small_vals["b_gate"] = jnp.stack([small[l]["b_gate"] for l in range(DEPTH)])
    small_vals["conv_w"] = jnp.stack([small[l]["conv_w"] for l in range(DEPTH)])
    small_vals["rel_bias"] = stats[:, 0, :NUM_BUCKETS].T
    small_vals["loss"] = loss_row[0, :1]
    shapes = {n: v.shape for n, v in small_vals.items()}
    small_send, small_recv, packed, small_land, started = _small_start(_pack_small(small_vals))

    summed = {}

    def finish(which, after):
        for l, group, send, recv, gs, lands in pending:
            if (group == ("w_in",)) == which:
                gs, lands = _reduce_wait("reduce_wait_%d_%s" % (l, group[0]), send, recv, gs, lands, after)
                for n, g, land in zip(group, gs, lands):
                    summed[n] = _reduce_sum("reduce_sum_%d_%s" % (l, n), g, land, l, summed.get(n), chip, core)

    finish(False, started)
    early = [n for n in BIG if n != "w_in"]
    for n, g in zip(early, _join_halves("join_halves", [summed[n] for n in early])):
        update(n, g)
    finish(True, delta[early[-1]])
    update("w_in", _join_halves("join_halves_w_in", [summed["w_in"]])[0])

    packed, small_land = _small_wait(small_send, small_recv, packed, small_land, delta["w_in"])
    small_land = lax.dynamic_update_slice(small_land, packed[None], (2 * chip + core, jnp.int32(0), jnp.int32(0)))
    reduced = _unpack_small(_small_sum(small_land), shapes)
    reduced["b_gate"] = lax.dynamic_slice_in_dim(reduced["b_gate"], chip * (D // N_CHIPS), D // N_CHIPS, axis=2)
    reduced["conv_w"] = lax.dynamic_slice_in_dim(reduced["conv_w"], chip * (2 * D_FF // N_CHIPS), 2 * D_FF // N_CHIPS, axis=2)
    for n in names:
        if n not in grads:
            update(n, reduced[n].reshape(weights[n].shape))
    for out in (grads, delta, new_m, new_v):
        out["w_in"] = _shard_view("w_in", out["w_in"])

    loss = reduced["loss"].reshape(())
    return (loss, grad_x.reshape(x_shape), *[grads[n] for n in names], *[delta[n] for n in names],
            *[new_m[n] for n in names], *[new_v[n] for n in names])
```

```python
import functools
import math

import numpy as np
import jax
import jax.numpy as jnp
from jax import lax
from jax.experimental import pallas as pl
from jax.experimental.pallas import tpu as pltpu

F32 = jnp.float32
BF16 = jnp.bfloat16

S = 2048
D = 1024
DEPTH = 2
HD = 64
BLK = 128
NQB = S // BLK
A_GROUPS = ((128, 1), (512, 4), (2048, 16))
N_BAND_Q = 20
N_A = 12
NUM_BUCKETS = 32
MAX_DISTANCE = 2048
D_FF = 4096
IN_COLS = 6912
IN_SHARD = IN_COLS // 4
OFF_GATE = 3840
EPS = 1e-6
SCALE = HD ** -0.5
NEG = -1e30
N_CHIPS = 4
N_DEV = 8

ADAM_LR = 0.001
ADAM_B1 = 0.9
ADAM_B2 = 0.999
ADAM_EPS = 1e-08
ADAM_WD = 0.01
ADAM_STEP = 10

VMEM_LIMIT = 56 * 1024 * 1024

NN = (((1,), (0,)), ((), ()))
NT = (((1,), (1,)), ((), ()))
TN = (((0,), (0,)), ((), ()))

MESH = pl.DeviceIdType.MESH
ANY = pl.BlockSpec(memory_space=pl.ANY)


def _dot(a, b, dims):
    return lax.dot_general(a, b, dims, preferred_element_type=F32)


def _params(sem):
    return pltpu.CompilerParams(dimension_semantics=sem, vmem_limit_bytes=VMEM_LIMIT)


def _matmul(name, a, b, out_shape, out_dtype, grid, a_spec, b_spec, o_spec, dims, acc_shape):
    nk = grid[-1]

    def body(a_ref, b_ref, o_ref, *scratch):
        part = _dot(a_ref[...].astype(BF16), b_ref[...].astype(BF16), dims)
        if nk == 1:
            o_ref[...] = part.astype(o_ref.dtype)
            return
        acc_ref, = scratch
        k = pl.program_id(len(grid) - 1)

        @pl.when(k == 0)
        def _():
            acc_ref[...] = part

        @pl.when(k > 0)
        def _():
            acc_ref[...] += part

        @pl.when(k == nk - 1)
        def _():
            o_ref[...] = acc_ref[...].astype(o_ref.dtype)

    scratch = [] if nk == 1 else [pltpu.VMEM(acc_shape, F32)]
    sem = ("parallel",) * (len(grid) - 1) + ("arbitrary",)
    return pl.pallas_call(
        body, name=name, grid=grid, in_specs=[a_spec, b_spec], out_specs=o_spec,
        out_shape=jax.ShapeDtypeStruct(out_shape, out_dtype), scratch_shapes=scratch,
        compiler_params=_params(sem))(a, b)


FULL_K = 8192


def _mm_tn_sharded(name, a, b, row_sharded, tm=512, tn=512, tk=FULL_K):
    k, m = a.shape
    n = b.shape[1]
    m4, n4 = (m // N_CHIPS, n) if row_sharded else (m, n // N_CHIPS)
    tm, tn, tk = min(tm, m4), min(tn, n4), min(tk, k)
    per_m, per_n = m4 // tm, n4 // tn
    if row_sharded:
        o_map = lambda i, j, l: (i // per_m, i % per_m, j)
    else:
        o_map = lambda i, j, l: (j // per_n, i, j % per_n)
    return _matmul(name, a, b, (N_CHIPS, m4, n4), BF16, (m // tm, n // tn, k // tk),
                   pl.BlockSpec((tk, tm), lambda i, j, l: (l, i)),
                   pl.BlockSpec((tk, tn), lambda i, j, l: (l, j)),
                   pl.BlockSpec((None, tm, tn), o_map), TN, (tm, tn))


def _mm_nn(name, a, b, out_dtype, tm=512, tn=512, tk=FULL_K):
    m, k = a.shape
    n = b.size // k
    tm, tn, tk = min(tm, m), min(tn, b.shape[-1]), min(tk, k)
    per_shard = b.shape[-1] // tn
    if b.ndim == 2:
        b_spec = pl.BlockSpec((tk, tn), lambda i, j, l: (l, j))
    else:
        b_spec = pl.BlockSpec((None, tk, tn), lambda i, j, l: (j // per_shard, l, j % per_shard))
    return _matmul(name, a, b, (m, n), out_dtype, (m // tm, n // tn, k // tk),
                   pl.BlockSpec((tm, tk), lambda i, j, l: (i, l)), b_spec,
                   pl.BlockSpec((tm, tn), lambda i, j, l: (i, j)), NN, (tm, tn))


def _mm_nt(name, a, b, out_dtype, tm=512, tn=512, tk=FULL_K):
    m, k = a.shape
    n = b.shape[-2]
    tm, tn, tk = min(tm, m), min(tn, n), min(tk, b.shape[-1])
    per_shard = b.shape[-1] // tk
    if b.ndim == 2:
        b_spec = pl.BlockSpec((tn, tk), lambda i, j, l: (j, l))
    else:
        b_spec = pl.BlockSpec((None, tn, tk), lambda i, j, l: (l // per_shard, j, l % per_shard))
    return _matmul(name, a, b, (m, n), out_dtype, (m // tm, n // tn, k // tk),
                   pl.BlockSpec((tm, tk), lambda i, j, l: (i, l)), b_spec,
                   pl.BlockSpec((tm, tn), lambda i, j, l: (i, j)), NT, (tm, tn))


def _mm_tn(name, a, b, out_dtype, tm=512, tn=512, tk=FULL_K):
    k, m = a.shape
    n = b.shape[1]
    tm, tn, tk = min(tm, m), min(tn, n), min(tk, k)
    return _matmul(name, a, b, (m, n), out_dtype, (m // tm, n // tn, k // tk),
                   pl.BlockSpec((tk, tm), lambda i, j, l: (l, i)),
                   pl.BlockSpec((tk, tn), lambda i, j, l: (l, j)),
                   pl.BlockSpec((tm, tn), lambda i, j, l: (i, j)), TN, (tm, tn))


TR = 512


def _row_spec(width=D):
    return pl.BlockSpec((TR, width), lambda i: (i, 0))


def _vec_spec(width=D):
    return pl.BlockSpec((1, width), lambda i: (0, 0))


def _rms(x, g):
    r = lax.rsqrt(jnp.mean(x * x, axis=-1, keepdims=True) + EPS)
    return x * r * g


def _rms_fwd(name, x, g):
    def body(x_ref, g_ref, h_ref):
        h_ref[...] = _rms(x_ref[...], g_ref[...]).astype(BF16)

    return pl.pallas_call(
        body, name=name, grid=(S // TR,), in_specs=[_row_spec(), _vec_spec()], out_specs=_row_spec(),
        out_shape=jax.ShapeDtypeStruct((S, D), BF16), compiler_params=_params(("parallel",)))(x, g)


def _post_pre_fwd(name, x, y, g_post, g_pre):
    has_pre = g_pre is not None

    def body(*refs):
        if has_pre:
            x_ref, y_ref, gp_ref, gn_ref, xn_ref, h_ref = refs
        else:
            x_ref, y_ref, gp_ref, xn_ref = refs
        xn = x_ref[...] + _rms(y_ref[...], gp_ref[...])
        xn_ref[...] = xn
        if has_pre:
            h_ref[...] = _rms(xn, gn_ref[...]).astype(BF16)

    ins = [x, y, g_post] + ([g_pre] if has_pre else [])
    in_specs = [_row_spec(), _row_spec(), _vec_spec()] + ([_vec_spec()] if has_pre else [])
    out_shape = [jax.ShapeDtypeStruct((S, D), F32)] + ([jax.ShapeDtypeStruct((S, D), BF16)] if has_pre else [])
    out_specs = [_row_spec()] + ([_row_spec()] if has_pre else [])
    out = pl.pallas_call(
        body, name=name, grid=(S // TR,), in_specs=in_specs, out_specs=out_specs, out_shape=out_shape,
        compiler_params=_params(("parallel",)))(*ins)
    return out if has_pre else (out[0], None)


def _rms_bwd_math(x, g, dy):
    r = lax.rsqrt(jnp.mean(x * x, axis=-1, keepdims=True) + EPS)
    n = x * r
    dn = dy * g
    dx = r * (dn - n * jnp.mean(dn * n, axis=-1, keepdims=True))
    return dx, jnp.sum(dy * n, axis=0, keepdims=True)


def _norm_bwd(name, dres, pre=None, post=None):
    has_pre, has_post = pre is not None, post is not None

    def body(*refs):
        refs = list(refs)
        dres_ref = refs.pop(0)
        if has_pre:
            xn_ref, gn_ref, dh_ref = refs[:3]
            refs = refs[3:]
        if has_post:
            y_ref, gp_ref = refs[:2]
            refs = refs[2:]
        dxn_ref = refs.pop(0)
        dy_ref = refs.pop(0) if has_post else None
        dgn_ref = refs.pop(0) if has_pre else None
        dgp_ref = refs.pop(0) if has_post else None
        first = pl.program_id(0) == 0
        dxn = dres_ref[...]
        if has_pre:
            dx, dg = _rms_bwd_math(xn_ref[...], gn_ref[...], dh_ref[...])
            dxn = dxn + dx

            @pl.when(first)
            def _():
                dgn_ref[...] = dg

            @pl.when(jnp.logical_not(first))
            def _():
                dgn_ref[...] += dg
        dxn_ref[...] = dxn
        if has_post:
            dy, dg = _rms_bwd_math(y_ref[...], gp_ref[...], dxn)
            dy_ref[...] = dy.astype(BF16)

            @pl.when(first)
            def _():
                dgp_ref[...] = dg

            @pl.when(jnp.logical_not(first))
            def _():
                dgp_ref[...] += dg

    ins, in_specs = [dres], [_row_spec()]
    if has_pre:
        ins += list(pre)
        in_specs += [_row_spec(), _vec_spec(), _row_spec()]
    if has_post:
        ins += list(post)
        in_specs += [_row_spec(), _vec_spec()]
    out_shape, out_specs = [jax.ShapeDtypeStruct((S, D), F32)], [_row_spec()]
    if has_post:
        out_shape.append(jax.ShapeDtypeStruct((S, D), BF16))
        out_specs.append(_row_spec())
    for _ in range(int(has_pre) + int(has_post)):
        out_shape.append(jax.ShapeDtypeStruct((1, D), F32))
        out_specs.append(_vec_spec())
    out = list(pl.pallas_call(
        body, name=name, grid=(S // TR,), in_specs=in_specs, out_specs=out_specs, out_shape=out_shape,
        compiler_params=_params(("arbitrary",)))(*ins))
    dxn = out.pop(0)
    dy = out.pop(0) if has_post else None
    dgn = out.pop(0) if has_pre else None
    dgp = out.pop(0) if has_post else None
    return dxn, dy, dgn, dgp


def _loss_kernel(y, target):
    def body(y_ref, t_ref, loss_ref, dy_ref):
        e = y_ref[...] - t_ref[...]
        dy_ref[...] = e * (1.0 / D)
        part = jnp.zeros((1, 128), F32) + 0.5 * jnp.sum(jnp.mean(e * e, axis=-1, keepdims=True))

        @pl.when(pl.program_id(0) == 0)
        def _():
            loss_ref[...] = part

        @pl.when(pl.program_id(0) > 0)
        def _():
            loss_ref[...] += part

    return pl.pallas_call(
        body, name="loss", grid=(S // TR,), in_specs=[_row_spec(), _row_spec()],
        out_specs=[_vec_spec(128), _row_spec()],
        out_shape=[jax.ShapeDtypeStruct((1, 128), F32), jax.ShapeDtypeStruct((S, D), F32)],
        compiler_params=_params(("arbitrary",)))(y, target)


def _t5_bucket_np(dist):
    max_exact = NUM_BUCKETS // 2
    nf = np.maximum(dist, 1).astype(np.float32)
    large = max_exact + (np.log(nf / max_exact) / np.float32(math.log(MAX_DISTANCE / max_exact))
                         * (NUM_BUCKETS - max_exact)).astype(np.int32)
    large = np.minimum(large, NUM_BUCKETS - 1)
    return np.where(dist < max_exact, dist, large).astype(np.int32)


def _bucket_maps():
    a = np.arange(BLK)[:, None]
    b = np.arange(2 * BLK)[None, :]
    dist = np.maximum(a + BLK - b, 0)
    maps = [_t5_bucket_np(dist * d) for _, d in A_GROUPS] + [_t5_bucket_np(dist)]
    return np.stack(maps).astype(np.int32)


def _pair_spec(col0):
    return pl.BlockSpec((S, 128), lambda p: (0, col0 + p))


def _band_rows(i, d):
    nb = S // d // BLK
    r, b = i // nb, i % nb
    cur = pl.ds(b * BLK * d + r, BLK, stride=d)
    prev = pl.ds(jnp.maximum(b - 1, 0) * BLK * d + r, BLK, stride=d)
    return cur, prev, jnp.minimum(b, 1)


def _band_bias(tab_ref, bi, h):
    bias = jnp.zeros((BLK, 2 * BLK), F32)
    for kk in range(NUM_BUCKETS):
        bias = jnp.where(bi == kk, tab_ref[kk, h], bias)
    return bias


def _lane_lo(rows=BLK):
    return lax.broadcasted_iota(jnp.int32, (rows, 128), 1) < HD


def _per_head(x, lo):
    return (jnp.sum(jnp.where(lo, x, 0.0), axis=1, keepdims=True) * (1.0 / HD),
            jnp.sum(jnp.where(lo, 0.0, x), axis=1, keepdims=True) * (1.0 / HD))


def _band_fill(bias_ref, tab_ref, bi, head, maxd):
    a = lax.broadcasted_iota(jnp.int32, (BLK, 2 * BLK), 0)
    c = lax.broadcasted_iota(jnp.int32, (BLK, 2 * BLK), 1)
    dist = a + BLK - c
    in_band = jnp.logical_and(dist >= 0, dist <= maxd)
    for h in range(2):
        bias = jnp.where(in_band, _band_bias(tab_ref, bi, head + h), NEG)
        bias_ref[1, h * BLK:(h + 1) * BLK, :] = bias
        bias_ref[0, h * BLK:(h + 1) * BLK, :] = jnp.where(c >= BLK, bias, NEG)


def _stack_heads(x, lo, dtype=BF16):
    return jnp.concatenate([jnp.where(lo, x, 0.0), jnp.where(lo, 0.0, x)], axis=0).astype(dtype)


def _unstack_heads(x, lo):
    n = x.shape[0] // 2
    return jnp.where(lo, x[:n], x[n:])


def _stack_rows(ref, prev, cur):
    return jnp.concatenate([ref[prev, :], ref[cur, :]], axis=0).astype(BF16)


def _blocks_of_group(group, dils, block):
    def run(d):
        lax.fori_loop(0, NQB, functools.partial(block, d), 0, unroll=2)

    if len(dils) == 1:
        run(dils[0])
        return
    for g, d in enumerate(dils):
        pl.when(group == g)(functools.partial(run, d))


def _band_fwd(name, dils, n_pairs, maxd, head0, srcs, bidx_g, tab, sinks):
    (qa, qc), (ka, kc), (va, vc) = srcs
    per_group = n_pairs // len(dils)
    out_spec = _pair_spec(0)
    smem = pl.BlockSpec(memory_space=pltpu.SMEM)
    full = pl.BlockSpec((len(dils), BLK, 2 * BLK), lambda p: (0, 0, 0))

    def body(tab_ref, sink_ref, q_ref, k_ref, v_ref, bidx_ref, o_ref, lse_ref, bias_ref):
        p = pl.program_id(0)
        _band_fill(bias_ref, tab_ref, bidx_ref[p // per_group], head0 + 2 * p, maxd)
        lo = _lane_lo()
        sink = jnp.where(lax.broadcasted_iota(jnp.int32, (2 * BLK, 1), 0) < BLK, sink_ref[2 * p], sink_ref[2 * p + 1])

        def block(d, i, carry):
            cur, prev, has_prev = _band_rows(i, d)
            qs = _stack_heads(q_ref[cur, :] * SCALE, lo)
            ks, vs = _stack_rows(k_ref, prev, cur), _stack_rows(v_ref, prev, cur)
            s = _dot(qs, ks, NT) + bias_ref[has_prev]
            m = jnp.max(s, axis=1, keepdims=True)
            pr = jnp.exp(s - m)
            l = jnp.sum(pr, axis=1, keepdims=True)
            num = _dot(pr.astype(BF16), vs, NN)
            lse = m + jnp.log(l)
            sig = 1.0 / (1.0 + jnp.exp(sink - lse))
            o_ref[cur, :] = _unstack_heads(num * (sig / l), lo)
            lse_ref[cur, :] = _unstack_heads(lse + jnp.zeros((2 * BLK, 128), F32), lo)
            return carry

        _blocks_of_group(p // per_group, dils, block)

    shape = jax.ShapeDtypeStruct((S, n_pairs * 128), F32)
    return pl.pallas_call(
        body, name=name, grid=(n_pairs,),
        in_specs=[smem, smem, _pair_spec(qc), _pair_spec(kc), _pair_spec(vc), full],
        out_specs=[out_spec, out_spec], out_shape=[shape, shape],
        scratch_shapes=[pltpu.VMEM((2, 2 * BLK, 2 * BLK), F32)],
        compiler_params=_params(("parallel",)))(tab, sinks, qa, ka, va, bidx_g)


def _band_bwd(name, dils, n_pairs, maxd, head0, srcs, bidx_g, tab, sinks, o, lse, do, stats_in):
    (qa, qc), (ka, kc), (va, vc) = srcs
    per_group = n_pairs // len(dils)
    pair = _pair_spec(0)
    shared = pl.BlockSpec((S, 128), lambda p: (0, p % per_group))
    smem = pl.BlockSpec(memory_space=pltpu.SMEM)
    full = pl.BlockSpec((len(dils), BLK, 2 * BLK), lambda p: (0, 0, 0))
    stat_spec = pl.BlockSpec((2, 8, 128), lambda p: (p, 0, 0))

    def body(tab_ref, sink_ref, q_ref, k_ref, v_ref, bidx_ref, o_ref, lse_ref, do_ref, sin_ref,
             dq_ref, dk_ref, dv_ref, stat_ref, bias_ref, dsacc_ref, sk_ref):
        p = pl.program_id(0)
        _band_fill(bias_ref, tab_ref, bidx_ref[p // per_group], head0 + 2 * p, maxd)
        dsacc_ref[...] = jnp.zeros_like(dsacc_ref)
        sk_ref[...] = jnp.zeros_like(sk_ref)
        dk_ref[...] = jnp.zeros_like(dk_ref)
        dv_ref[...] = jnp.zeros_like(dv_ref)
        lo = _lane_lo()
        head1 = lax.broadcasted_iota(jnp.int32, (2 * BLK, 1), 0) >= BLK
        sink = jnp.where(head1, sink_ref[2 * p + 1], sink_ref[2 * p])

        def block(d, i, carry):
            cur, prev, has_prev = _band_rows(i, d)
            qs = _stack_heads(q_ref[cur, :] * SCALE, lo)
            ks, vs = _stack_rows(k_ref, prev, cur), _stack_rows(v_ref, prev, cur)
            do = do_ref[cur, :]
            dos = _stack_heads(do, lo, F32)
            lse = jnp.concatenate(_per_head(lse_ref[cur, :], lo), axis=0)
            prod = do * o_ref[cur, :]
            delta = jnp.concatenate([jnp.sum(jnp.where(lo, prod, 0.0), axis=1, keepdims=True),
                                     jnp.sum(jnp.where(lo, 0.0, prod), axis=1, keepdims=True)], axis=0)
            sig = 1.0 / (1.0 + jnp.exp(sink - lse))
            pr = jnp.exp(_dot(qs, ks, NT) + bias_ref[has_prev] - lse)
            ds = pr * (sig * (_dot(dos.astype(BF16), vs, NT) - delta))
            dsb = ds.astype(BF16)
            dq_ref[cur, :] = SCALE * _unstack_heads(_dot(dsb, ks, NN), lo)
            dk = _dot(dsb, qs, TN)
            dv = _dot(pr.astype(BF16), (sig * dos).astype(BF16), TN)
            dk_ref[prev, :] += dk[:BLK]
            dk_ref[cur, :] += dk[BLK:]
            dv_ref[prev, :] += dv[:BLK]
            dv_ref[cur, :] += dv[BLK:]
            dsacc_ref[...] += ds
            sink_grad = -delta * (1.0 - sig)
            for h in range(2):
                sk_ref[h] += jnp.zeros((8, 128), F32) + jnp.sum(sink_grad[h * BLK:(h + 1) * BLK])
            return carry

        _blocks_of_group(p // per_group, dils, block)

        bi = bidx_ref[p // per_group]
        lane = lax.broadcasted_iota(jnp.int32, (8, 128), 1)
        sub = lax.broadcasted_iota(jnp.int32, (8, 128), 0)
        for h in range(2):
            acc = dsacc_ref[h * BLK:(h + 1) * BLK, :]
            row = jnp.where(jnp.logical_and(sub == 1, lane == 0), sk_ref[h], 0.0)
            for kk in range(NUM_BUCKETS):
                tot = jnp.sum(jnp.where(bi == kk, acc, 0.0))
                row = jnp.where(jnp.logical_and(sub == 0, lane == kk), tot, row)
            stat_ref[h] = row + jnp.where(sub == 0, sin_ref[h], 0.0)

    shape = jax.ShapeDtypeStruct((S, n_pairs * 128), F32)
    return pl.pallas_call(
        body, name=name, grid=(n_pairs,),
        in_specs=[smem, smem, _pair_spec(qc), _pair_spec(kc), _pair_spec(vc), full, shared, shared, shared, stat_spec],
        out_specs=[pair, pair, pair, stat_spec],
        out_shape=[shape, shape, shape, jax.ShapeDtypeStruct((2 * n_pairs, 8, 128), F32)],
        scratch_shapes=[pltpu.VMEM((2, 2 * BLK, 2 * BLK), F32), pltpu.VMEM((2 * BLK, 2 * BLK), F32),
                        pltpu.VMEM((2, 8, 128), F32)],
        compiler_params=_params(("parallel",)))(tab, sinks, qa, ka, va, bidx_g, o, lse, do, stats_in)


def _comb_fwd(o_g, lse_g):
    def body(o0, o1, o2, l0, l1, l2, out_ref, outb_ref, lse_ref):
        a0, a1, a2 = l0[...], l1[...], l2[...]
        m = jnp.maximum(jnp.maximum(a0, a1), a2)
        e0, e1, e2 = jnp.exp(a0 - m), jnp.exp(a1 - m), jnp.exp(a2 - m)
        tot = e0 + e1 + e2
        out = (e0 * o0[...] + e1 * o1[...] + e2 * o2[...]) / tot
        out_ref[...] = out
        outb_ref[...] = out.astype(BF16)
        lse_ref[...] = m + jnp.log(tot)

    spec = _row_spec(4 * HD)
    groups = [pl.BlockSpec((TR, 4 * HD), functools.partial(lambda i, g: (i, g), g=g)) for g in range(len(A_GROUPS))]
    f32 = jax.ShapeDtypeStruct((S, 4 * HD), F32)
    return pl.pallas_call(
        body, name="comb_fwd", grid=(S // TR,), in_specs=groups + groups, out_specs=[spec] * 3,
        out_shape=[f32, jax.ShapeDtypeStruct((S, 4 * HD), BF16), f32],
        compiler_params=_params(("parallel",)))(o_g, o_g, o_g, lse_g, lse_g, lse_g)


def _split2(x):
    hi = x.astype(BF16)
    return hi, (x - hi.astype(F32)).astype(BF16)


KB = 2 * BLK
SBQ = 2 * BLK


def _tri_sum(x, tri):
    hi, lo = _split2(x)
    both = _dot(jnp.concatenate([hi, lo], axis=0), tri, NN)
    return both[:x.shape[0]] + both[x.shape[0]:]


def _tri(strict_upper):
    r = lax.broadcasted_iota(jnp.int32, (KB, KB), 0)
    c = lax.broadcasted_iota(jnp.int32, (KB, KB), 1)
    return jnp.where(r > c if strict_upper else r < c, 1.0, 0.0).astype(BF16)


def _sb_terms(qs, kj, before):
    z = _dot(qs, kj, NT)
    lsp = jnp.minimum(z, 0.0) - jnp.log(1.0 + jnp.exp(-jnp.abs(z)))
    return lsp, _sb_keep(before, lsp - z)


def _sb_keep(before, x):
    return x if before is None else jnp.where(before, x, 0.0)


def _sb_before(i, m):
    t = (lax.broadcasted_iota(jnp.int32, (2 * SBQ, KB), 0) & (SBQ - 1)) + i * SBQ
    s = lax.broadcasted_iota(jnp.int32, (2 * SBQ, KB), 1) + m * KB
    return s < t


C_COL = 3072 // 128


def _sb_fwd(proj):
    blk = lambda off: pl.BlockSpec((SBQ, 128), lambda p, i: (i, off + p))
    col = lambda off: pl.BlockSpec((S, 128), lambda p, i: (0, off + p))
    out = pl.BlockSpec((SBQ, 128), lambda p, i: (i, p))

    def body(q_ref, k_ref, v_ref, o_ref, ob_ref, tot_ref):
        i = pl.program_id(1)
        lo = _lane_lo(SBQ)
        qs = _stack_heads(q_ref[...] * SCALE, lo)
        suffix = _tri(True)

        def step(n, carry, diagonal=False):
            acc, rest = carry
            m = i - n
            rows = pl.ds(pl.multiple_of(m * KB, KB), KB)
            kj, vj = k_ref[rows, :].astype(BF16), v_ref[rows, :].astype(BF16)
            before = _sb_before(i, m) if diagonal else None
            lsp, lk = _sb_terms(qs, kj, before)
            w = _sb_keep(before, jnp.exp(lsp + _tri_sum(lk, suffix) + rest))
            return acc + _dot(w.astype(BF16), vj, NN), rest + jnp.sum(lk, axis=1, keepdims=True)

        first = step(0, (jnp.zeros((2 * SBQ, 128), F32), jnp.zeros((2 * SBQ, 1), F32)), diagonal=True)
        acc, rest = lax.fori_loop(1, i + 1, step, first)
        o = _unstack_heads(acc, lo)
        o_ref[...] = o
        ob_ref[...] = o.astype(BF16)
        tot_ref[...] = _unstack_heads(rest + jnp.zeros((2 * SBQ, 128), F32), lo)

    f32 = jax.ShapeDtypeStruct((S, 4 * HD), F32)
    return pl.pallas_call(
        body, name="sb_fwd", grid=(2, S // SBQ), in_specs=[blk(C_COL), col(C_COL + 2), col(C_COL + 4)],
        out_specs=[out, out, out], out_shape=[f32, jax.ShapeDtypeStruct((S, 4 * HD), BF16), f32],
        compiler_params=_params(("parallel", "arbitrary")))(proj, proj, proj)


def _sb_bwd(proj, tot, do):
    blk = lambda off: pl.BlockSpec((SBQ, 128), lambda p, i: (i, off + p))
    col = lambda off: pl.BlockSpec((S, 128), lambda p, i: (0, off + p))

    def body(q_ref, k_ref, v_ref, tot_ref, do_ref, dq_ref, dk_ref, dv_ref):
        i = pl.program_id(1)

        @pl.when(i == 0)
        def _():
            dk_ref[...] = jnp.zeros_like(dk_ref)
            dv_ref[...] = jnp.zeros_like(dv_ref)

        lo = _lane_lo(SBQ)
        qs = _stack_heads(q_ref[...] * SCALE, lo)
        dos = _stack_heads(do_ref[...], lo)
        tots = jnp.concatenate(_per_head(tot_ref[...], lo), axis=0)
        prefix = _tri(False)

        def step(m, carry, diagonal=False):
            dq, keep_left, g_left = carry
            rows = pl.ds(pl.multiple_of(m * KB, KB), KB)
            kj, vj = k_ref[rows, :].astype(BF16), v_ref[rows, :].astype(BF16)
            before = _sb_before(i, m) if diagonal else None
            lsp, lk = _sb_terms(qs, kj, before)
            log_rest = tots - keep_left - lk - _tri_sum(lk, prefix)
            w = _sb_keep(before, jnp.exp(lsp + log_rest))
            g = w * _dot(dos, vj, NT)
            g_before = g_left + _dot(g.astype(BF16), prefix, NN)
            beta = jnp.exp(lsp)
            dz = _sb_keep(before, g * (1.0 - beta) - g_before * beta).astype(BF16)
            dk_ref[rows, :] += _dot(dz, qs, TN)
            dv_ref[rows, :] += _dot(w.astype(BF16), dos, TN)
            return (dq + _dot(dz, kj, NN), keep_left + jnp.sum(lk, axis=1, keepdims=True),
                    g_left + jnp.sum(g, axis=1, keepdims=True))

        zero = (jnp.zeros((2 * SBQ, 128), F32), jnp.zeros((2 * SBQ, 1), F32), jnp.zeros((2 * SBQ, 1), F32))
        dq, _, _ = step(i, lax.fori_loop(0, i, step, zero), diagonal=True)
        dq_ref[...] = SCALE * _unstack_heads(dq, lo)

    out_blk = pl.BlockSpec((SBQ, 128), lambda p, i: (i, p))
    out_col = pl.BlockSpec((S, 128), lambda p, i: (0, p))
    f32 = jax.ShapeDtypeStruct((S, 4 * HD), F32)
    return pl.pallas_call(
        body, name="sb_bwd", grid=(2, S // SBQ),
        in_specs=[blk(C_COL), col(C_COL + 2), col(C_COL + 4), out_blk, out_blk],
        out_specs=[out_blk, out_col, out_col], out_shape=[f32, f32, f32],
        compiler_params=_params(("arbitrary", "arbitrary")))(proj, proj, proj, tot, do)


TG = 256
TGR = 1024
GATE_BLK0 = OFF_GATE // TG


def _gate_specs():
    grid = (D // TG, S // TGR)
    p_specs = [pl.BlockSpec((TGR, TG), functools.partial(lambda c, r, br: (r, GATE_BLK0 + br * (D // TG) + c), br=br))
               for br in range(3)]
    b_spec = pl.BlockSpec((3, TG), lambda c, r: (0, c))
    t_spec = pl.BlockSpec((TGR, TG), lambda c, r: (r, c))
    return grid, p_specs, b_spec, t_spec


def _sigmoid(x):
    return 1.0 / (1.0 + jnp.exp(-x))


def _three_rows(rows):
    sub = lax.broadcasted_iota(jnp.int32, (3, rows[0].shape[1]), 0)
    return jnp.where(sub == 0, rows[0], jnp.where(sub == 1, rows[1], rows[2]))


def _gate_fwd(proj, b_gate, br):
    grid, p_specs, b_spec, t_spec = _gate_specs()

    def body(p0, p1, p2, b_ref, r0, r1, r2, out_ref):
        acc = jnp.zeros((TGR, TG), F32)
        for n, (p, r) in enumerate(((p0, r0), (p1, r1), (p2, r2))):
            acc += _sigmoid(p[...] + b_ref[n:n + 1, :]) * r[...]
        out_ref[...] = acc.astype(BF16)

    return pl.pallas_call(
        body, name="gate_fwd", grid=grid, in_specs=p_specs + [b_spec] + [t_spec] * 3, out_specs=t_spec,
        out_shape=jax.ShapeDtypeStruct((S, D), BF16),
        compiler_params=_params(("parallel", "parallel")))(proj, proj, proj, b_gate, *br)


def _gate_bwd(proj, b_gate, br, dmerged):
    grid, p_specs, b_spec, t_spec = _gate_specs()

    def body(p0, p1, p2, b_ref, r0, r1, r2, dm_ref, e0, e1, e2, g0, g1, g2, db_ref):
        dm = dm_ref[...]
        rows = []
        for n, (p, r, e_ref, dg_ref) in enumerate(((p0, r0, e0, g0), (p1, r1, e1, g1), (p2, r2, e2, g2))):
            g = _sigmoid(p[...] + b_ref[n:n + 1, :])
            e_ref[...] = (dm * g).astype(BF16)
            dpre = dm * r[...] * g * (1.0 - g)
            dg_ref[...] = dpre.astype(BF16)
            rows.append(jnp.sum(dpre, axis=0, keepdims=True))
        db = _three_rows(rows)

        @pl.when(pl.program_id(1) == 0)
        def _():
            db_ref[...] = db

        @pl.when(pl.program_id(1) > 0)
        def _():
            db_ref[...] += db

    bf = jax.ShapeDtypeStruct((S, D), BF16)
    out = pl.pallas_call(
        body, name="gate_bwd", grid=grid, in_specs=p_specs + [b_spec] + [t_spec] * 4,
        out_specs=[t_spec] * 6 + [b_spec], out_shape=[bf] * 6 + [jax.ShapeDtypeStruct((3, D), F32)],
        compiler_params=_params(("parallel", "arbitrary")))(proj, proj, proj, b_gate, *br, dmerged)
    return out[:3], out[3:6], out[6]


TC = 256
N_FF_BLK = D_FF // TC
GELU_C = math.sqrt(2.0 / math.pi)


def _shift_down(x, n):
    rows = lax.broadcasted_iota(jnp.int32, x.shape, 0)
    return jnp.where(rows >= n, pltpu.roll(x, n, axis=0), 0.0)


def _shift_up(x, n):
    rows = lax.broadcasted_iota(jnp.int32, x.shape, 0)
    return jnp.where(rows < x.shape[0] - n, pltpu.roll(x, x.shape[0] - n, axis=0), 0.0)


def _conv(u, w, b):
    s1, s2 = _shift_down(u, 1), _shift_down(u, 2)
    return w[2:3, :] * u + w[1:2, :] * s1 + w[0:1, :] * s2 + b, s1, s2


def _gelu_parts(x):
    inner = GELU_C * (x + 0.044715 * x * x * x)
    t = jnp.tanh(inner)
    gelu = 0.5 * x * (1.0 + t)
    dgelu = 0.5 * (1.0 + t) + 0.5 * x * (1.0 - t * t) * GELU_C * (1.0 + 3 * 0.044715 * x * x)
    return gelu, dgelu


def _conv_specs():
    ug = pl.BlockSpec((S, TC), lambda c: (0, c))
    uv = pl.BlockSpec((S, TC), lambda c: (0, N_FF_BLK + c))
    wg = pl.BlockSpec((3, TC), lambda c: (0, c))
    wv = pl.BlockSpec((3, TC), lambda c: (0, N_FF_BLK + c))
    bg = pl.BlockSpec((1, TC), lambda c: (0, c))
    bv = pl.BlockSpec((1, TC), lambda c: (0, N_FF_BLK + c))
    return ug, uv, wg, wv, bg, bv


def _conv_fwd(u, conv_w, conv_b):
    ug, uv, wg, wv, bg, bv = _conv_specs()

    def body(ug_ref, uv_ref, wg_ref, wv_ref, bg_ref, bv_ref, a_ref):
        gc = _conv(ug_ref[...], wg_ref[...], bg_ref[...])[0]
        vc = _conv(uv_ref[...], wv_ref[...], bv_ref[...])[0]
        a_ref[...] = (_gelu_parts(gc)[0] * vc).astype(BF16)

    return pl.pallas_call(
        body, name="conv_fwd", grid=(N_FF_BLK,), in_specs=[ug, uv, wg, wv, bg, bv], out_specs=ug,
        out_shape=jax.ShapeDtypeStruct((S, D_FF), BF16),
        compiler_params=_params(("parallel",)))(u, u, conv_w, conv_w, conv_b, conv_b)


def _conv_bwd(u, conv_w, conv_b, da):
    ug, uv, wg, wv, bg, bv = _conv_specs()

    def back(duc, u, s1, s2, w):
        du = w[2:3, :] * duc + w[1:2, :] * _shift_up(duc, 1) + w[0:1, :] * _shift_up(duc, 2)
        dw = _three_rows([jnp.sum(duc * s2, axis=0, keepdims=True), jnp.sum(duc * s1, axis=0, keepdims=True),
                          jnp.sum(duc * u, axis=0, keepdims=True)])
        return du, dw, jnp.sum(duc, axis=0, keepdims=True)

    def body(ug_ref, uv_ref, wg_ref, wv_ref, bg_ref, bv_ref, da_ref, dug_ref, duv_ref, dwg_ref, dwv_ref, dbg_ref, dbv_ref):
        u_g, u_v = ug_ref[...], uv_ref[...]
        gc, g1, g2 = _conv(u_g, wg_ref[...], bg_ref[...])
        vc, v1, v2 = _conv(u_v, wv_ref[...], bv_ref[...])
        gelu, dgelu = _gelu_parts(gc)
        da = da_ref[...]
        du, dw, db = back(da * vc * dgelu, u_g, g1, g2, wg_ref[...])
        dug_ref[...] = du.astype(BF16)
        dwg_ref[...] = dw
        dbg_ref[...] = db
        du, dw, db = back(da * gelu, u_v, v1, v2, wv_ref[...])
        duv_ref[...] = du.astype(BF16)
        dwv_ref[...] = dw
        dbv_ref[...] = db

    return pl.pallas_call(
        body, name="conv_bwd", grid=(N_FF_BLK,), in_specs=[ug, uv, wg, wv, bg, bv, ug],
        out_specs=[ug, ug, wg, wg, bg, bg],
        out_shape=[jax.ShapeDtypeStruct((S, D_FF), BF16), jax.ShapeDtypeStruct((S, D_FF), BF16),
                   jax.ShapeDtypeStruct((3, D_FF), F32), jax.ShapeDtypeStruct((3, D_FF), F32),
                   jax.ShapeDtypeStruct((1, D_FF), F32), jax.ShapeDtypeStruct((1, D_FF), F32)],
        compiler_params=_params(("parallel",)))(u, u, conv_w, conv_w, conv_b, conv_b, da)


def _adamw(name, w, g, m, v):
    shape = w.shape
    cols = shape[-1]
    flat = [t.reshape(-1, cols) for t in (w, g, m, v)]
    r = flat[0].shape[0]
    tr = min(r, max(8, 2 * 1024 * 1024 // (4 * cols)))

    def body(w_ref, g_ref, m_ref, v_ref, go_ref, d_ref, mo_ref, vo_ref):
        g = g_ref[...]
        go_ref[...] = g
        m = ADAM_B1 * m_ref[...] + (1.0 - ADAM_B1) * g
        v = ADAM_B2 * v_ref[...] + (1.0 - ADAM_B2) * (g * g)
        m_hat = m / (1.0 - ADAM_B1 ** ADAM_STEP)
        v_hat = v / (1.0 - ADAM_B2 ** ADAM_STEP)
        d_ref[...] = -ADAM_LR * (m_hat / (jnp.sqrt(v_hat) + ADAM_EPS) + ADAM_WD * w_ref[...])
        mo_ref[...] = m
        vo_ref[...] = v

    spec = pl.BlockSpec((tr, cols), lambda i: (i, 0))
    outs = pl.pallas_call(
        body, name=name, grid=(pl.cdiv(r, tr),), in_specs=[spec] * 4, out_specs=[spec] * 4,
        out_shape=[jax.ShapeDtypeStruct((r, cols), F32)] * 4, compiler_params=_params(("parallel",)))(*flat)
    return [t.reshape(shape) for t in outs]


def _place():
    x, y, c = lax.axis_index("x"), lax.axis_index("y"), lax.axis_index("c")
    chips = [(1 - x, y), (x, 1 - y), (1 - x, 1 - y)]
    return x, y, c, chips


def _scalars(*vals):
    return jnp.stack([jnp.asarray(v, jnp.int32) for v in vals])


HBM = pl.BlockSpec(memory_space=pltpu.HBM)
SEM = pl.BlockSpec(memory_space=pltpu.SEMAPHORE)
SPLIT_COPY = pltpu.CompilerParams(has_side_effects=pltpu.SideEffectType.DATAFLOW_SIDE_EFFECTING)


def _in_hbm(x):
    return pltpu.with_memory_space_constraint(x, pltpu.HBM)


def _cast_into_slot(name, w, layer, chip):
    _, k, n4 = w.shape
    tr = max(t for t in range(16, 513, 16) if k % t == 0)

    def body(chip_ref, w_ref, o_ref):
        o_ref[...] = w_ref[...].astype(BF16)

    return pl.pallas_call(
        body, name=name,
        grid_spec=pltpu.PrefetchScalarGridSpec(
            num_scalar_prefetch=1, grid=(k // tr,),
            in_specs=[pl.BlockSpec((None, tr, n4), lambda i, s: (layer, i, 0))],
            out_specs=pl.BlockSpec((None, tr, n4), lambda i, s: (s[0], i, 0))),
        out_shape=jax.ShapeDtypeStruct((N_CHIPS, k, n4), BF16),
        compiler_params=_params(("parallel",)))(_scalars(chip), w)


def _gather_copy(buf_ref, k, from_chip, send_sem, recv_sem, chips, c, half=False):
    rows = buf_ref.at[from_chip]
    if half:
        h = buf_ref.shape[1] // 2
        rows = buf_ref.at[from_chip, pl.ds(pl.multiple_of(c * h, h), h)]
    return pltpu.make_async_remote_copy(src_ref=rows, dst_ref=rows, send_sem=send_sem, recv_sem=recv_sem,
                                        device_id=(*chips[k], c), device_id_type=MESH)


def _gather_start(name, bufs, groups, halved=()):
    n, ng = len(bufs), len(groups)
    where = {a: (gi, e) for gi, g in enumerate(groups) for e, a in enumerate(g)}

    def body(*refs):
        ins, sems, token = refs[:n], refs[n:n + 2 * ng], refs[-1]
        x, y, c, chips = _place()
        for a in range(n):
            gi, e = where[a]
            for k in range(3):
                _gather_copy(ins[a], k, 2 * x + y, sems[2 * gi].at[3 * e + k], sems[2 * gi + 1].at[3 * e + k],
                             chips, c, a in halved).start()
        token[...] = jnp.zeros_like(token)

    out_shape = [pltpu.SemaphoreType.DMA((3 * len(g),)) for g in groups for _ in range(2)]
    out_shape += [pltpu.HBM(b.shape, b.dtype) for b in bufs] + [jax.ShapeDtypeStruct((8, 128), F32)]
    out = pl.pallas_call(
        body, name=name, in_specs=[HBM] * n,
        out_specs=[SEM] * (2 * ng) + [HBM] * n + [pl.BlockSpec(memory_space=pltpu.VMEM)], out_shape=out_shape,
        input_output_aliases={a: 2 * ng + a for a in range(n)}, compiler_params=SPLIT_COPY)(*[_in_hbm(b) for b in bufs])
    sems = [(out[2 * gi], out[2 * gi + 1]) for gi in range(ng)]
    return sems, list(out[2 * ng:2 * ng + n]), out[-1]


def _gather_wait(name, bufs, send, recv, after, halved=()):
    n = len(bufs)

    def body(*refs):
        ins, send_sem, recv_sem = refs[:n], refs[n], refs[n + 1]
        x, y, c, chips = _place()
        for e in range(n):
            for k in range(3):
                sems = (send_sem.at[3 * e + k], recv_sem.at[3 * e + k])
                _gather_copy(ins[e], k, 2 * x + y, *sems, chips, c, e in halved).wait_send()
                _gather_copy(ins[e], k, 2 * chips[k][0] + chips[k][1], *sems, chips, c, e in halved).wait_recv()

    return pl.pallas_call(
        body, name=name, in_specs=[HBM] * n + [SEM, SEM, ANY], out_specs=[HBM] * n,
        out_shape=[pltpu.HBM(b.shape, b.dtype) for b in bufs],
        input_output_aliases={a: a for a in range(n)}, compiler_params=SPLIT_COPY)(*bufs, send, recv, after)


def _swap_halves(name, bufs):
    n = len(bufs)

    def body(*refs):
        ins, outs = refs[:n], refs[n:2 * n]
        send_sem, recv_sem = refs[2 * n:]
        x, y, c, chips = _place()

        def piece(ref, k, which):
            h = ref.shape[1] // 2
            return ref.at[2 * chips[k][0] + chips[k][1], pl.ds(pl.multiple_of(which * h, h), h)]

        def copy(a, k, which):
            return pltpu.make_async_remote_copy(
                src_ref=piece(ins[a], k, c), dst_ref=piece(outs[a], k, which), send_sem=send_sem.at[3 * a + k],
                recv_sem=recv_sem.at[3 * a + k], device_id=(x, y, 1 - c), device_id_type=MESH)

        for a in range(n):
            for k in range(3):
                copy(a, k, c).start()
        for a in range(n):
            for k in range(3):
                copy(a, k, c).wait_send()
                copy(a, k, 1 - c).wait_recv()

    return pl.pallas_call(
        body, name=name, in_specs=[ANY] * n, out_specs=[ANY] * n,
        out_shape=[jax.ShapeDtypeStruct(b.shape, b.dtype) for b in bufs],
        input_output_aliases={a: a for a in range(n)},
        scratch_shapes=[pltpu.SemaphoreType.DMA((3 * n,)), pltpu.SemaphoreType.DMA((3 * n,))],
    )(*bufs)


def _reduce_copy(g_ref, land_ref, mask, send_sem, recv_sem, x, y, c, sending):
    px, py, pc = x ^ ((mask >> 2) & 1), y ^ ((mask >> 1) & 1), c ^ (mask & 1)
    half = g_ref.shape[1] // 2
    src = g_ref.at[2 * px + py, pl.ds(pl.multiple_of(pc * half, half), half)]
    dst = land_ref.at[4 * x + 2 * y + c] if sending else land_ref.at[4 * px + 2 * py + pc]
    return pltpu.make_async_remote_copy(src_ref=src, dst_ref=dst, send_sem=send_sem, recv_sem=recv_sem,
                                        device_id=(px, py, pc), device_id_type=MESH)


def _reduce_start(name, grads):
    n = len(grads)
    lands = [lax.empty((N_DEV, g.shape[1] // 2, g.shape[2]), g.dtype) for g in grads]

    def body(*refs):
        gs, ls, send_sem, recv_sem = refs[:n], refs[n:2 * n], refs[2 * n], refs[2 * n + 1]
        x, y, c, _ = _place()
        for a in range(n):
            for mask in range(1, N_DEV):
                s = (N_DEV - 1) * a + mask - 1
                _reduce_copy(gs[a], ls[a], mask, send_sem.at[s], recv_sem.at[s], x, y, c, True).start()
        refs[-1][...] = jnp.zeros_like(refs[-1])

    sem = pltpu.SemaphoreType.DMA((n * (N_DEV - 1),))
    out = pl.pallas_call(
        body, name=name, in_specs=[HBM] * (2 * n),
        out_specs=[SEM, SEM] + [HBM] * (2 * n) + [pl.BlockSpec(memory_space=pltpu.VMEM)],
        out_shape=[sem, sem] + [pltpu.HBM(t.shape, t.dtype) for t in grads + lands] + [jax.ShapeDtypeStruct((8, 128), F32)],
        input_output_aliases={a: 2 + a for a in range(2 * n)}, compiler_params=SPLIT_COPY)(
            *[_in_hbm(t) for t in grads + lands])
    return out[0], out[1], list(out[2:2 + n]), list(out[2 + n:2 + 2 * n]), out[-1]


def _reduce_wait(name, send, recv, grads, lands, after):
    n = len(grads)

    def body(*refs):
        gs, ls, send_sem, recv_sem = refs[:n], refs[n:2 * n], refs[2 * n], refs[2 * n + 1]
        x, y, c, _ = _place()
        for a in range(n):
            for mask in range(1, N_DEV):
                s = (N_DEV - 1) * a + mask - 1
                sems = (send_sem.at[s], recv_sem.at[s])
                _reduce_copy(gs[a], ls[a], mask, *sems, x, y, c, True).wait_send()
                _reduce_copy(gs[a], ls[a], mask, *sems, x, y, c, False).wait_recv()

    out = pl.pallas_call(
        body, name=name, in_specs=[HBM] * (2 * n) + [SEM, SEM, ANY], out_specs=[HBM] * (2 * n),
        out_shape=[pltpu.HBM(t.shape, t.dtype) for t in grads + lands],
        input_output_aliases={a: a for a in range(2 * n)}, compiler_params=SPLIT_COPY)(*grads, *lands, send, recv, after)
    return list(out[:n]), list(out[n:])


def _reduce_sum(name, g, land, layer, into, chip, c):
    _, k4, n4 = g.shape
    half = k4 // 2
    tr = max(t for t in range(16, 513, 16) if half % t == 0)
    per = half // tr
    me = 2 * chip + c

    def body(s_ref, own_ref, *refs):
        total = own_ref[...].astype(F32)
        for ref in refs[:N_DEV - 1]:
            total = total + ref[...].astype(F32)
        refs[-1][...] = total

    in_specs = [pl.BlockSpec((None, tr, n4), lambda i, s: (s[0], s[1] * per + i, 0))]
    in_specs += [pl.BlockSpec((None, tr, n4), functools.partial(lambda i, s, m: (s[1 + m], i, 0), m=m))
                 for m in range(1, N_DEV)]
    ins = [g] + [land] * (N_DEV - 1)
    aliases = {}
    if into is not None:
        in_specs, ins, aliases = in_specs + [ANY], ins + [into], {1 + N_DEV: 0}
    return pl.pallas_call(
        body, name=name,
        grid_spec=pltpu.PrefetchScalarGridSpec(
            num_scalar_prefetch=1, grid=(per,), in_specs=in_specs,
            out_specs=pl.BlockSpec((None, tr, n4), lambda i, s: (layer, s[1] * per + i, 0))),
        out_shape=jax.ShapeDtypeStruct((DEPTH, k4, n4), F32), input_output_aliases=aliases,
        compiler_params=_params(("parallel",)))(_scalars(chip, c, *[me ^ m for m in range(1, N_DEV)]), *ins)


def _join_halves(name, bufs):
    n = len(bufs)

    def body(*refs):
        ins, outs = refs[:n], refs[n:2 * n]
        send_sem, recv_sem = refs[2 * n:]
        x, y, c, _ = _place()

        def rows(ref, which):
            half = ref.shape[1] // 2
            return ref.at[:, pl.ds(pl.multiple_of(which * half, half), half)]

        sends = [pltpu.make_async_remote_copy(
            src_ref=rows(ins[a], c), dst_ref=rows(outs[a], c), send_sem=send_sem.at[a], recv_sem=recv_sem.at[a],
            device_id=(x, y, 1 - c), device_id_type=MESH) for a in range(n)]
        for cp in sends:
            cp.start()
        for a in range(n):
            sends[a].wait_send()
            pltpu.make_async_remote_copy(
                src_ref=rows(ins[a], c), dst_ref=rows(outs[a], 1 - c), send_sem=send_sem.at[a], recv_sem=recv_sem.at[a],
                device_id=(x, y, 1 - c), device_id_type=MESH).wait_recv()

    return pl.pallas_call(
        body, name=name, in_specs=[ANY] * n, out_specs=[ANY] * n,
        out_shape=[jax.ShapeDtypeStruct(b.shape, b.dtype) for b in bufs],
        input_output_aliases={a: a for a in range(n)},
        scratch_shapes=[pltpu.SemaphoreType.DMA((n,)), pltpu.SemaphoreType.DMA((n,))],
    )(*bufs)


def _small_copy(b_ref, l_ref, mask, send_sem, recv_sem, x, y, c, sending):
    px, py, pc = x ^ ((mask >> 2) & 1), y ^ ((mask >> 1) & 1), c ^ (mask & 1)
    dst = l_ref.at[4 * x + 2 * y + c] if sending else l_ref.at[4 * px + 2 * py + pc]
    return pltpu.make_async_remote_copy(src_ref=b_ref, dst_ref=dst, send_sem=send_sem.at[mask - 1],
                                        recv_sem=recv_sem.at[mask - 1], device_id=(px, py, pc), device_id_type=MESH)


def _small_start(block):
    land = lax.empty((N_DEV,) + block.shape, block.dtype)

    def body(b_ref, l_ref, send_sem, recv_sem, b_thru, l_thru, token):
        x, y, c, _ = _place()
        for mask in range(1, N_DEV):
            _small_copy(b_ref, l_ref, mask, send_sem, recv_sem, x, y, c, True).start()
        token[...] = jnp.zeros_like(token)

    sem = pltpu.SemaphoreType.DMA((N_DEV - 1,))
    return pl.pallas_call(
        body, name="small_start", in_specs=[HBM, HBM],
        out_specs=[SEM, SEM, HBM, HBM, pl.BlockSpec(memory_space=pltpu.VMEM)],
        out_shape=[sem, sem, pltpu.HBM(block.shape, block.dtype), pltpu.HBM(land.shape, land.dtype),
                   jax.ShapeDtypeStruct((8, 128), F32)],
        input_output_aliases={0: 2, 1: 3}, compiler_params=SPLIT_COPY)(_in_hbm(block), _in_hbm(land))


def _small_wait(send, recv, block, land, after):
    def body(b_ref, l_ref, send_sem, recv_sem, after_ref, b_out, l_out):
        x, y, c, _ = _place()
        for mask in range(1, N_DEV):
            _small_copy(b_ref, l_ref, mask, send_sem, recv_sem, x, y, c, True).wait_send()
            _small_copy(b_ref, l_ref, mask, send_sem, recv_sem, x, y, c, False).wait_recv()

    return pl.pallas_call(
        body, name="small_wait", in_specs=[HBM, HBM, SEM, SEM, ANY], out_specs=[HBM, HBM],
        out_shape=[pltpu.HBM(block.shape, block.dtype), pltpu.HBM(land.shape, land.dtype)],
        input_output_aliases={0: 0, 1: 1}, compiler_params=SPLIT_COPY)(block, land, send, recv, after)


def _small_sum(land):
    def body(l_ref, out_ref):
        total = l_ref[0]
        for d in range(1, N_DEV):
            total = total + l_ref[d]
        out_ref[...] = total

    vmem = pl.BlockSpec(memory_space=pltpu.VMEM)
    return pl.pallas_call(
        body, name="small_sum", in_specs=[vmem], out_specs=vmem,
        out_shape=jax.ShapeDtypeStruct(land.shape[1:], F32),
        compiler_params=pltpu.CompilerParams(vmem_limit_bytes=VMEM_LIMIT))(land)


B_Q_COL = 2304 // 128
B_K0, B_V0, B_END = 2816, 2944, 3072


def _full_cols(w_g):
    return w_g.transpose(1, 0, 2).reshape(w_g.shape[1], -1)


A_DILS = tuple(d for _, d in A_GROUPS)
A_PAIRS = N_A // 2


def _src_a(proj):
    return ((proj, 0), (proj, A_PAIRS), (proj, 2 * A_PAIRS))


def _kv_expand(kv):
    return jnp.broadcast_to(kv.reshape(S, 2, 1, HD), (S, 2, 4, HD)).reshape(S, 8 * HD)


def _kv_reduce(dkv):
    return dkv.reshape(S, 2, 4, HD).sum(axis=2).reshape(S, 2 * HD)


def _mixer_fwd(h1, wget, rel_bias, sinks_l, bidx):
    w = dict(wget(0, h1))
    proj = _mm_nt("proj_in", h1, w["w_in"], F32, tm=S, tn=1152)
    no_sinks = jnp.full((N_A,), NEG, F32)
    o_g, lse_g = _band_fwd("band_fwd_a", A_DILS, A_PAIRS, BLK, 0, _src_a(proj), bidx[:3], rel_bias, no_sinks)
    o_a32, o_a, lse_a = _comb_fwd(o_g, lse_g)
    src_b = ((proj, B_Q_COL), (_kv_expand(proj[:, B_K0:B_V0]), 0), (_kv_expand(proj[:, B_V0:B_END]), 0))
    o_b32, lse_b = _band_fwd("band_fwd_b", (1,), 4, BLK - 1, N_A, src_b, bidx[3:], rel_bias, sinks_l)
    o_b = o_b32.astype(BF16)
    o_c32, o_c, tot_c = _sb_fwd(proj)
    w.update(wget(1, o_c32))
    br = [_mm_nn("branch_a", o_a, w["w_br_a"], F32, tm=S), _mm_nn("branch_b", o_b, w["w_br_b"], F32, tm=S),
          _mm_nn("branch_c", o_c, w["w_br_c"], F32, tm=S)]
    merged = _gate_fwd(proj, w["b_gate"], br)
    mo = _mm_nn("out_proj", merged, w["w_out"], F32, tm=S)
    saved = dict(proj=proj, src_b=src_b, o_a32=o_a32, lse_a=lse_a, o_b32=o_b32, lse_b=lse_b, tot_c=tot_c,
                 o_a=o_a, o_b=o_b, o_c=o_c, br=br, merged=merged)
    return mo, saved, w


def _mixer_bwd(d_mo, h1, w, sv, rel_bias, sinks_l, bidx, stats_in, emit):
    grads = {}
    dmerged = _mm_nt("out_proj_dx", d_mo, w["w_out"], F32, tm=S)
    grads["w_out"] = _mm_tn_sharded("out_proj_dw", sv["merged"], d_mo, True)
    e, dgate, db_gate = _gate_bwd(sv["proj"], w["b_gate"], sv["br"], dmerged)
    grads["b_gate"] = db_gate
    d_o = {}
    for n, name in enumerate("abc"):
        d_o[name] = _mm_nt("branch_%s_dx" % name, e[n], w["w_br_" + name], F32, tm=S)
        grads["w_br_" + name] = _mm_tn_sharded("branch_%s_dw" % name, sv["o_" + name], e[n], False)
    zero = emit(1, grads)
    no_sinks = jnp.full((N_A,), NEG, F32) + zero[0]
    dq_a, dk_a, dv_a, st_a = _band_bwd("band_bwd_a", A_DILS, A_PAIRS, BLK, 0, _src_a(sv["proj"]), bidx[:3], rel_bias,
                                       no_sinks, sv["o_a32"], sv["lse_a"], d_o["a"], stats_in[:N_A])
    dq_b, dk_x, dv_x, st_b = _band_bwd("band_bwd_b", (1,), 4, BLK - 1, N_A, sv["src_b"], bidx[3:], rel_bias, sinks_l,
                                       sv["o_b32"], sv["lse_b"], d_o["b"], stats_in[N_A:])
    stats = jnp.concatenate([st_a, st_b], axis=0)
    dcq, dck, dcv = _sb_bwd(sv["proj"], sv["tot_c"], d_o["c"])
    cols = [dq_a, dk_a, dv_a, dq_b, _kv_reduce(dk_x), _kv_reduce(dv_x), dcq, dck, dcv]
    dproj = jnp.concatenate([t.astype(BF16) for t in cols] + list(dgate), axis=1)
    grads["w_in"] = _mm_tn("proj_in_dw", dproj, h1, BF16, tm=1152, tn=1024).reshape(N_CHIPS, IN_SHARD, D)
    zero = emit(2, grads)
    dh1 = _mm_nn("proj_in_dx", dproj, w["w_in"], F32, tm=S, tk=2304)
    return dh1, grads, stats, zero


def _ffn_fwd(h2, w):
    u = _mm_nn("ffn_up", h2, w["w_up"], F32, tm=S, tn=1024)
    a = _conv_fwd(u, w["conv_w"], w["conv_b"])
    dn = _mm_nn("ffn_down", a, w["w_down"], F32, tm=1024)
    return dn, dict(u=u, a=a)


def _ffn_bwd(d_dn, h2, w, sv):
    grads = {}
    da = _mm_nt("ffn_down_dx", d_dn, w["w_down"], F32, tm=S, tn=1024)
    grads["w_down"] = _mm_tn_sharded("ffn_down_dw", sv["a"], d_dn, True, tm=1024, tn=1024)
    dug, duv, dwg, dwv, dbg, dbv = _conv_bwd(sv["u"], w["conv_w"], w["conv_b"], da)
    du = jnp.concatenate([dug, duv], axis=1)
    grads["conv_w"] = jnp.concatenate([dwg, dwv], axis=1)
    grads["conv_b"] = jnp.concatenate([dbg, dbv], axis=1)
    dh2 = _mm_nt("ffn_up_dx", du, w["w_up"], F32, tm=S, tk=2048)
    grads["w_up"] = _mm_tn_sharded("ffn_up_dw", h2, du, False, tm=1024, tn=1024)
    return dh2, grads


BIG = ("w_in", "w_br_a", "w_br_b", "w_br_c", "w_out", "w_up", "w_down")


def _shard_view(name, w):
    return jnp.swapaxes(w, 1, 2) if name == "w_in" else w
WEIGHT_GROUPS = (("w_in", "b_gate"), ("w_br_a", "w_br_b", "w_br_c", "w_out"), ("w_up", "conv_w", "w_down"))
GRAD_GROUPS = (("w_down", "w_up"), ("w_out", "w_br_a", "w_br_b", "w_br_c"), ("w_in",))
SMALL_ROWS = (("rel_bias", 8), ("attn_pre_norm", 16), ("attn_post_norm", 16), ("ffn_pre_norm", 16), ("ffn_post_norm", 16),
              ("sinks", 8), ("conv_b", 128), ("b_gate", 48), ("conv_w", 384), ("loss", 8))


def _pack_small(vals):
    rows = []
    for name, n in SMALL_ROWS:
        flat = vals[name].reshape(-1).astype(F32)
        rows.append(jnp.pad(flat, (0, n * 128 - flat.shape[0])).reshape(n, 128))
    return jnp.concatenate(rows, axis=0)


def _unpack_small(block, shapes):
    out, row = {}, 0
    for name, n in SMALL_ROWS:
        size = int(np.prod(shapes[name]))
        out[name] = block[row:row + n].reshape(-1)[:size].reshape(shapes[name])
        row += n
    return out


def kernel(x, rel_bias, attn_pre_norm, w_in, b_gate, sinks, w_br_a, w_br_b, w_br_c, w_out, attn_post_norm, ffn_pre_norm, w_up, conv_w, conv_b, w_down, ffn_post_norm, loss_target, m_rel_bias, m_attn_pre_norm, m_w_in, m_b_gate, m_sinks, m_w_br_a, m_w_br_b, m_w_br_c, m_w_out, m_attn_post_norm, m_ffn_pre_norm, m_w_up, m_conv_w, m_conv_b, m_w_down, m_ffn_post_norm, v_rel_bias, v_attn_pre_norm, v_w_in, v_b_gate, v_sinks, v_w_br_a, v_w_br_b, v_w_br_c, v_w_out, v_attn_post_norm, v_ffn_pre_norm, v_w_up, v_conv_w, v_conv_b, v_w_down, v_ffn_post_norm):
    names = ("rel_bias", "attn_pre_norm", "w_in", "b_gate", "sinks", "w_br_a", "w_br_b", "w_br_c", "w_out",
             "attn_post_norm", "ffn_pre_norm", "w_up", "conv_w", "conv_b", "w_down", "ffn_post_norm")
    weights = dict(zip(names, (rel_bias, attn_pre_norm, w_in, b_gate, sinks, w_br_a, w_br_b, w_br_c, w_out,
                               attn_post_norm, ffn_pre_norm, w_up, conv_w, conv_b, w_down, ffn_post_norm)))
    mom1 = dict(zip(names, (m_rel_bias, m_attn_pre_norm, m_w_in, m_b_gate, m_sinks, m_w_br_a, m_w_br_b, m_w_br_c,
                            m_w_out, m_attn_post_norm, m_ffn_pre_norm, m_w_up, m_conv_w, m_conv_b, m_w_down,
                            m_ffn_post_norm)))
    mom2 = dict(zip(names, (v_rel_bias, v_attn_pre_norm, v_w_in, v_b_gate, v_sinks, v_w_br_a, v_w_br_b, v_w_br_c,
                            v_w_out, v_attn_post_norm, v_ffn_pre_norm, v_w_up, v_conv_w, v_conv_b, v_w_down,
                            v_ffn_post_norm)))

    chip = 2 * lax.axis_index("x") + lax.axis_index("y")
    core = lax.axis_index("c")

    keys = [(n, l) for l in range(DEPTH) for group in WEIGHT_GROUPS for n in group]
    groups = [[keys.index((n, l)) for n in group] for l in range(DEPTH) for group in WEIGHT_GROUPS]

    def slot_buffer(n, l):
        if n in BIG:
            return _cast_into_slot("cast_" + n, _shard_view(n, weights[n]), l, chip)
        shard = weights[n][l]
        return lax.dynamic_update_slice(jnp.zeros((N_CHIPS,) + shard.shape, F32), shard[None],
                                        (chip, jnp.int32(0), jnp.int32(0)))

    by_halves = [keys.index(k) for k in (("w_in", 0), ("w_up", DEPTH - 1), ("w_down", DEPTH - 1))]
    n_first = len(groups[0])
    sems, in_flight, _ = _gather_start("gather_start_first", [slot_buffer(*k) for k in keys[:n_first]], groups[:1],
                                       tuple(a for a in by_halves if a < n_first))
    more = _gather_start("gather_start", [slot_buffer(*k) for k in keys[n_first:]],
                         [[a - n_first for a in g] for g in groups[1:]],
                         tuple(a - n_first for a in by_halves if a >= n_first))
    sems, in_flight, started = sems + more[0], in_flight + more[1], more[2]

    def wget(l, gi, after):
        g = l * len(WEIGHT_GROUPS) + gi
        after = started if g == 0 else after
        halved = tuple(e for e, a in enumerate(groups[g]) if a in by_halves)
        got = list(_gather_wait("gather_wait_%d_%d" % (l, gi), [in_flight[a] for a in groups[g]], *sems[g], after,
                                halved))
        if halved:
            for e, buf in zip(halved, _swap_halves("swap_halves_%d_%d" % (l, gi), [got[e] for e in halved])):
                got[e] = buf
        out = {}
        for n, buf in zip(WEIGHT_GROUPS[gi], got):
            if n in ("w_in", "w_out", "w_down"):
                out[n] = buf.reshape(-1, buf.shape[-1])
            else:
                out[n] = buf if n == "w_up" else _full_cols(buf)
        if gi == len(WEIGHT_GROUPS) - 1:
            out["conv_b"] = conv_b[l:l + 1]
        return out

    pending = []

    def emit(l, gi, grads):
        group = GRAD_GROUPS[gi]
        *started, token = _reduce_start("reduce_start_%d_%d" % (l, gi), [grads[n] for n in group])
        pending.append((l, group) + tuple(started))
        return token[:1, :1]

    local = _local_step(x.reshape(S, D), loss_target.reshape(S, D), wget, emit, rel_bias, sinks, attn_pre_norm,
                        attn_post_norm, ffn_pre_norm, ffn_post_norm)
    return _reduce_and_update(x.shape, names, weights, mom1, mom2, chip, core, pending, *local)


def _local_step(xs, target, wget, emit, rel_bias, sinks, attn_pre_norm, attn_post_norm, ffn_pre_norm, ffn_post_norm):
    bidx = jnp.asarray(_bucket_maps())

    saved, layers = [], []
    h1 = _rms_fwd("pre_norm_first", xs, attn_pre_norm[0:1])
    x_in = xs
    for l in range(DEPTH):
        mo, sv_mix, w = _mixer_fwd(h1, functools.partial(wget, l), rel_bias, sinks[l], bidx)
        x_mid, h2 = _post_pre_fwd("post_attn_norm", x_in, mo, attn_post_norm[l:l + 1], ffn_pre_norm[l:l + 1])
        w.update(wget(l, 2, h2))
        dn, sv_ffn = _ffn_fwd(h2, w)
        g_next = attn_pre_norm[l + 1:l + 2] if l + 1 < DEPTH else None
        x_out, h1_next = _post_pre_fwd("post_ffn_norm" if l + 1 < DEPTH else "post_ffn_norm_last", x_mid, dn,
                                       ffn_post_norm[l:l + 1], g_next)
        saved.append(dict(x_in=x_in, h1=h1, mo=mo, x_mid=x_mid, h2=h2, dn=dn, mix=sv_mix, ffn=sv_ffn))
        layers.append(w)
        x_in, h1 = x_out, h1_next

    loss_row, dres = _loss_kernel(x_in, target)

    small = [None] * DEPTH
    stats = jnp.zeros((N_BAND_Q, 8, 128), F32)
    dh_next = None
    for l in reversed(range(DEPTH)):
        w, sv = layers[l], saved[l]
        if l + 1 < DEPTH:
            pre = (saved[l + 1]["x_in"], attn_pre_norm[l + 1:l + 2] + zero, dh_next)
            dres, d_dn, dg_pre_next, dg_fpost = _norm_bwd("post_ffn_norm_bwd", dres, pre,
                                                          (sv["dn"], ffn_post_norm[l:l + 1]))
            small[l + 1]["attn_pre_norm"] = dg_pre_next
        else:
            dres, d_dn, _, dg_fpost = _norm_bwd("post_ffn_norm_last_bwd", dres, None, (sv["dn"], ffn_post_norm[l:l + 1]))
        dh2, g_ffn = _ffn_bwd(d_dn, sv["h2"], w, sv["ffn"])
        zero = emit(l, 0, g_ffn)
        dres, d_mo, dg_fpre, dg_apost = _norm_bwd("post_attn_norm_bwd", dres,
                                                  (sv["x_mid"], ffn_pre_norm[l:l + 1] + zero, dh2),
                                                  (sv["mo"], attn_post_norm[l:l + 1]))
        dh_next, g_mix, stats, zero = _mixer_bwd(d_mo, sv["h1"], w, sv["mix"], rel_bias, sinks[l], bidx, stats,
                                                 functools.partial(emit, l))
        small[l] = dict(ffn_post_norm=dg_fpost, ffn_pre_norm=dg_fpre, attn_post_norm=dg_apost,
                        sinks=stats[N_A:, 1, 0], conv_b=g_ffn["conv_b"], b_gate=g_mix["b_gate"], conv_w=g_ffn["conv_w"])
    grad_x, _, dg_pre0, _ = _norm_bwd("pre_norm_first_bwd", dres, (saved[0]["x_in"], attn_pre_norm[0:1] + zero, dh_next),
                                      None)
    small[0]["attn_pre_norm"] = dg_pre0
    return loss_row, grad_x, small, stats


def _reduce_and_update(x_shape, names, weights, mom1, mom2, chip, core, pending, loss_row, grad_x, small, stats):
    delta, new_m, new_v, grads = {}, {}, {}, {}

    def update(n, g):
        grads[n], delta[n], new_m[n], new_v[n] = _adamw("adamw_" + n, _shard_view(n, weights[n]), g,
                                                        _shard_view(n, mom1[n]), _shard_view(n, mom2[n]))

    small_vals = {n: jnp.stack([small[l][n].reshape(weights[n].shape[1:]) for l in range(DEPTH)])
                  for n in ("attn_pre_norm", "attn_post_norm", "ffn_pre_norm", "ffn_post_norm", "conv_b", "sinks")}
    small_vals["b_gate"] = jnp.stack([small[l]["b_gate"] for l in range(DEPTH)])
    small_vals["conv_w"] = jnp.stack([small[l]["conv_w"] for l in range(DEPTH)])
    small_vals["rel_bias"] = stats[:, 0, :NUM_BUCKETS].T
    small_vals["loss"] = loss_row[0, :1]
    shapes = {n: v.shape for n, v in small_vals.items()}
    small_send, small_recv, packed, small_land, started = _small_start(_pack_small(small_vals))

    summed = {}

    def finish(which, after):
        for l, group, send, recv, gs, lands in pending:
            if (group == ("w_in",)) == which:
                gs, lands = _reduce_wait("reduce_wait_%d_%s" % (l, group[0]), send, recv, gs, lands, after)
                for n, g, land in zip(group, gs, lands):
                    summed[n] = _reduce_sum("reduce_sum_%d_%s" % (l, n), g, land, l, summed.get(n), chip, core)

    finish(False, started)
    early = [n for n in BIG if n != "w_in"]
    for n, g in zip(early, _join_halves("join_halves", [summed[n] for n in early])):
        update(n, g)
    finish(True, delta[early[-1]])
    update("w_in", _join_halves("join_halves_w_in", [summed["w_in"]])[0])

    packed, small_land = _small_wait(small_send, small_recv, packed, small_land, delta["w_in"])
    small_land = lax.dynamic_update_slice(small_land, packed[None], (2 * chip + core, jnp.int32(0), jnp.int32(0)))
    reduced = _unpack_small(_small_sum(small_land), shapes)
    reduced["b_gate"] = lax.dynamic_slice_in_dim(reduced["b_gate"], chip * (D // N_CHIPS), D // N_CHIPS, axis=2)
    reduced["conv_w"] = lax.dynamic_slice_in_dim(reduced["conv_w"], chip * (2 * D_FF // N_CHIPS), 2 * D_FF // N_CHIPS, axis=2)
    for n in names:
        if n not in grads:
            update(n, reduced[n].reshape(weights[n].shape))
    for out in (grads, delta, new_m, new_v):
        out["w_in"] = _shard_view("w_in", out["w_in"])

    loss = reduced["loss"].reshape(())
    return (loss, grad_x.reshape(x_shape), *[grads[n] for n in names], *[delta[n] for n in names],
            *[new_m[n] for n in names], *[new_v[n] for n in names])
```

```python
import functools
import math

import numpy as np
import jax
import jax.numpy as jnp
from jax import lax
from jax.experimental import pallas as pl
from jax.experimental.pallas import tpu as pltpu

F32 = jnp.float32
BF16 = jnp.bfloat16

S = 2048
D = 1024
DEPTH = 2
HD = 64
BLK = 128
NQB = S // BLK
A_GROUPS = ((128, 1), (512, 4), (2048, 16))
N_BAND_Q = 20
N_A = 12
NUM_BUCKETS = 32
MAX_DISTANCE = 2048
D_FF = 4096
IN_COLS = 6912
IN_SHARD = IN_COLS // 4
OFF_GATE = 3840
EPS = 1e-6
SCALE = HD ** -0.5
NEG = -1e30
N_CHIPS = 4
N_DEV = 8

ADAM_LR = 0.001
ADAM_B1 = 0.9
ADAM_B2 = 0.999
ADAM_EPS = 1e-08
ADAM_WD = 0.01
ADAM_STEP = 10

VMEM_LIMIT = 56 * 1024 * 1024

NN = (((1,), (0,)), ((), ()))
NT = (((1,), (1,)), ((), ()))
TN = (((0,), (0,)), ((), ()))

MESH = pl.DeviceIdType.MESH
ANY = pl.BlockSpec(memory_space=pl.ANY)


def _dot(a, b, dims):
    return lax.dot_general(a, b, dims, preferred_element_type=F32)


def _params(sem):
    return pltpu.CompilerParams(dimension_semantics=sem, vmem_limit_bytes=VMEM_LIMIT)


def _matmul(name, a, b, out_shape, out_dtype, grid, a_spec, b_spec, o_spec, dims, acc_shape):
    nk = grid[-1]

    def body(a_ref, b_ref, o_ref, *scratch):
        part = _dot(a_ref[...].astype(BF16), b_ref[...].astype(BF16), dims)
        if nk == 1:
            o_ref[...] = part.astype(o_ref.dtype)
            return
        acc_ref, = scratch
        k = pl.program_id(len(grid) - 1)

        @pl.when(k == 0)
        def _():
            acc_ref[...] = part

        @pl.when(k > 0)
        def _():
            acc_ref[...] += part

        @pl.when(k == nk - 1)
        def _():
            o_ref[...] = acc_ref[...].astype(o_ref.dtype)

    scratch = [] if nk == 1 else [pltpu.VMEM(acc_shape, F32)]
    sem = ("parallel",) * (len(grid) - 1) + ("arbitrary",)
    return pl.pallas_call(
        body, name=name, grid=grid, in_specs=[a_spec, b_spec], out_specs=o_spec,
        out_shape=jax.ShapeDtypeStruct(out_shape, out_dtype), scratch_shapes=scratch,
        compiler_params=_params(sem))(a, b)


FULL_K = 8192


def _mm_tn_sharded(name, a, b, row_sharded, tm=512, tn=512, tk=FULL_K):
    k, m = a.shape
    n = b.shape[1]
    m4, n4 = (m // N_CHIPS, n) if row_sharded else (m, n // N_CHIPS)
    tm, tn, tk = min(tm, m4), min(tn, n4), min(tk, k)
    per_m, per_n = m4 // tm, n4 // tn
    if row_sharded:
        o_map = lambda i, j, l: (i // per_m, i % per_m, j)
    else:
        o_map = lambda i, j, l: (j // per_n, i, j % per_n)
    return _matmul(name, a, b, (N_CHIPS, m4, n4), BF16, (m // tm, n // tn, k // tk),
                   pl.BlockSpec((tk, tm), lambda i, j, l: (l, i)),
                   pl.BlockSpec((tk, tn), lambda i, j, l: (l, j)),
                   pl.BlockSpec((None, tm, tn), o_map), TN, (tm, tn))


def _mm_nn(name, a, b, out_dtype, tm=512, tn=512, tk=FULL_K):
    m, k = a.shape
    n = b.size // k
    tm, tn, tk = min(tm, m), min(tn, b.shape[-1]), min(tk, k)
    per_shard = b.shape[-1] // tn
    if b.ndim == 2:
        b_spec = pl.BlockSpec((tk, tn), lambda i, j, l: (l, j))
    else:
        b_spec = pl.BlockSpec((None, tk, tn), lambda i, j, l: (j // per_shard, l, j % per_shard))
    return _matmul(name, a, b, (m, n), out_dtype, (m // tm, n // tn, k // tk),
                   pl.BlockSpec((tm, tk), lambda i, j, l: (i, l)), b_spec,
                   pl.BlockSpec((tm, tn), lambda i, j, l: (i, j)), NN, (tm, tn))


def _mm_nt(name, a, b, out_dtype, tm=512, tn=512, tk=FULL_K):
    m, k = a.shape
    n = b.shape[-2]
    tm, tn, tk = min(tm, m), min(tn, n), min(tk, b.shape[-1])
    per_shard = b.shape[-1] // tk
    if b.ndim == 2:
        b_spec = pl.BlockSpec((tn, tk), lambda i, j, l: (j, l))
    else:
        b_spec = pl.BlockSpec((None, tn, tk), lambda i, j, l: (l // per_shard, j, l % per_shard))
    return _matmul(name, a, b, (m, n), out_dtype, (m // tm, n // tn, k // tk),
                   pl.BlockSpec((tm, tk), lambda i, j, l: (i, l)), b_spec,
                   pl.BlockSpec((tm, tn), lambda i, j, l: (i, j)), NT, (tm, tn))


def _mm_tn(name, a, b, out_dtype, tm=512, tn=512, tk=FULL_K):
    k, m = a.shape
    n = b.shape[1]
    tm, tn, tk = min(tm, m), min(tn, n), min(tk, k)
    return _matmul(name, a, b, (m, n), out_dtype, (m // tm, n // tn, k // tk),
                   pl.BlockSpec((tk, tm), lambda i, j, l: (l, i)),
                   pl.BlockSpec((tk, tn), lambda i, j, l: (l, j)),
                   pl.BlockSpec((tm, tn), lambda i, j, l: (i, j)), TN, (tm, tn))


TR = 512


def _row_spec(width=D):
    return pl.BlockSpec((TR, width), lambda i: (i, 0))


def _vec_spec(width=D):
    return pl.BlockSpec((1, width), lambda i: (0, 0))


def _rms(x, g):
    r = lax.rsqrt(jnp.mean(x * x, axis=-1, keepdims=True) + EPS)
    return x * r * g


def _rms_fwd(name, x, g):
    def body(x_ref, g_ref, h_ref):
        h_ref[...] = _rms(x_ref[...], g_ref[...]).astype(BF16)

    return pl.pallas_call(
        body, name=name, grid=(S // TR,), in_specs=[_row_spec(), _vec_spec()], out_specs=_row_spec(),
        out_shape=jax.ShapeDtypeStruct((S, D), BF16), compiler_params=_params(("parallel",)))(x, g)


def _post_pre_fwd(name, x, y, g_post, g_pre):
    has_pre = g_pre is not None

    def body(*refs):
        if has_pre:
            x_ref, y_ref, gp_ref, gn_ref, xn_ref, h_ref = refs
        else:
            x_ref, y_ref, gp_ref, xn_ref = refs
        xn = x_ref[...] + _rms(y_ref[...], gp_ref[...])
        xn_ref[...] = xn
        if has_pre:
            h_ref[...] = _rms(xn, gn_ref[...]).astype(BF16)

    ins = [x, y, g_post] + ([g_pre] if has_pre else [])
    in_specs = [_row_spec(), _row_spec(), _vec_spec()] + ([_vec_spec()] if has_pre else [])
    out_shape = [jax.ShapeDtypeStruct((S, D), F32)] + ([jax.ShapeDtypeStruct((S, D), BF16)] if has_pre else [])
    out_specs = [_row_spec()] + ([_row_spec()] if has_pre else [])
    out = pl.pallas_call(
        body, name=name, grid=(S // TR,), in_specs=in_specs, out_specs=out_specs, out_shape=out_shape,
        compiler_params=_params(("parallel",)))(*ins)
    return out if has_pre else (out[0], None)


def _rms_bwd_math(x, g, dy):
    r = lax.rsqrt(jnp.mean(x * x, axis=-1, keepdims=True) + EPS)
    n = x * r
    dn = dy * g
    dx = r * (dn - n * jnp.mean(dn * n, axis=-1, keepdims=True))
    return dx, jnp.sum(dy * n, axis=0, keepdims=True)


def _norm_bwd(name, dres, pre=None, post=None):
    has_pre, has_post = pre is not None, post is not None

    def body(*refs):
        refs = list(refs)
        dres_ref = refs.pop(0)
        if has_pre:
            xn_ref, gn_ref, dh_ref = refs[:3]
            refs = refs[3:]
        if has_post:
            y_ref, gp_ref = refs[:2]
            refs = refs[2:]
        dxn_ref = refs.pop(0)
        dy_ref = refs.pop(0) if has_post else None
        dgn_ref = refs.pop(0) if has_pre else None
        dgp_ref = refs.pop(0) if has_post else None
        first = pl.program_id(0) == 0
        dxn = dres_ref[...]
        if has_pre:
            dx, dg = _rms_bwd_math(xn_ref[...], gn_ref[...], dh_ref[...])
            dxn = dxn + dx

            @pl.when(first)
            def _():
                dgn_ref[...] = dg

            @pl.when(jnp.logical_not(first))
            def _():
                dgn_ref[...] += dg
        dxn_ref[...] = dxn
        if has_post:
            dy, dg = _rms_bwd_math(y_ref[...], gp_ref[...], dxn)
            dy_ref[...] = dy.astype(BF16)

            @pl.when(first)
            def _():
                dgp_ref[...] = dg

            @pl.when(jnp.logical_not(first))
            def _():
                dgp_ref[...] += dg

    ins, in_specs = [dres], [_row_spec()]
    if has_pre:
        ins += list(pre)
        in_specs += [_row_spec(), _vec_spec(), _row_spec()]
    if has_post:
        ins += list(post)
        in_specs += [_row_spec(), _vec_spec()]
    out_shape, out_specs = [jax.ShapeDtypeStruct((S, D), F32)], [_row_spec()]
    if has_post:
        out_shape.append(jax.ShapeDtypeStruct((S, D), BF16))
        out_specs.append(_row_spec())
    for _ in range(int(has_pre) + int(has_post)):
        out_shape.append(jax.ShapeDtypeStruct((1, D), F32))
        out_specs.append(_vec_spec())
    out = list(pl.pallas_call(
        body, name=name, grid=(S // TR,), in_specs=in_specs, out_specs=out_specs, out_shape=out_shape,
        compiler_params=_params(("arbitrary",)))(*ins))
    dxn = out.pop(0)
    dy = out.pop(0) if has_post else None
    dgn = out.pop(0) if has_pre else None
    dgp = out.pop(0) if has_post else None
    return dxn, dy, dgn, dgp


def _loss_kernel(y, target):
    def body(y_ref, t_ref, loss_ref, dy_ref):
        e = y_ref[...] - t_ref[...]
        dy_ref[...] = e * (1.0 / D)
        part = jnp.zeros((1, 128), F32) + 0.5 * jnp.sum(jnp.mean(e * e, axis=-1, keepdims=True))

        @pl.when(pl.program_id(0) == 0)
        def _():
            loss_ref[...] = part

        @pl.when(pl.program_id(0) > 0)
        def _():
            loss_ref[...] += part

    return pl.pallas_call(
        body, name="loss", grid=(S // TR,), in_specs=[_row_spec(), _row_spec()],
        out_specs=[_vec_spec(128), _row_spec()],
        out_shape=[jax.ShapeDtypeStruct((1, 128), F32), jax.ShapeDtypeStruct((S, D), F32)],
        compiler_params=_params(("arbitrary",)))(y, target)


def _t5_bucket_np(dist):
    max_exact = NUM_BUCKETS // 2
    nf = np.maximum(dist, 1).astype(np.float32)
    large = max_exact + (np.log(nf / max_exact) / np.float32(math.log(MAX_DISTANCE / max_exact))
                         * (NUM_BUCKETS - max_exact)).astype(np.int32)
    large = np.minimum(large, NUM_BUCKETS - 1)
    return np.where(dist < max_exact, dist, large).astype(np.int32)


def _bucket_maps():
    a = np.arange(BLK)[:, None]
    b = np.arange(2 * BLK)[None, :]
    dist = np.maximum(a + BLK - b, 0)
    maps = [_t5_bucket_np(dist * d) for _, d in A_GROUPS] + [_t5_bucket_np(dist)]
    return np.stack(maps).astype(np.int32)


def _pair_spec(col0):
    return pl.BlockSpec((S, 128), lambda p: (0, col0 + p))


def _band_rows(i, d):
    nb = S // d // BLK
    r, b = i // nb, i % nb
    cur = pl.ds(b * BLK * d + r, BLK, stride=d)
    prev = pl.ds(jnp.maximum(b - 1, 0) * BLK * d + r, BLK, stride=d)
    return cur, prev, jnp.minimum(b, 1)


def _band_bias(tab_ref, bi, h):
    bias = jnp.zeros((BLK, 2 * BLK), F32)
    for kk in range(NUM_BUCKETS):
        bias = jnp.where(bi == kk, tab_ref[kk, h], bias)
    return bias


def _lane_lo(rows=BLK):
    return lax.broadcasted_iota(jnp.int32, (rows, 128), 1) < HD


def _per_head(x, lo):
    return (jnp.sum(jnp.where(lo, x, 0.0), axis=1, keepdims=True) * (1.0 / HD),
            jnp.sum(jnp.where(lo, 0.0, x), axis=1, keepdims=True) * (1.0 / HD))


def _band_fill(bias_ref, tab_ref, bi, head, maxd):
    a = lax.broadcasted_iota(jnp.int32, (BLK, 2 * BLK), 0)
    c = lax.broadcasted_iota(jnp.int32, (BLK, 2 * BLK), 1)
    dist = a + BLK - c
    in_band = jnp.logical_and(dist >= 0, dist <= maxd)
    for h in range(2):
        bias = jnp.where(in_band, _band_bias(tab_ref, bi, head + h), NEG)
        bias_ref[1, h * BLK:(h + 1) * BLK, :] = bias
        bias_ref[0, h * BLK:(h + 1) * BLK, :] = jnp.where(c >= BLK, bias, NEG)


def _stack_heads(x, lo, dtype=BF16):
    return jnp.concatenate([jnp.where(lo, x, 0.0), jnp.where(lo, 0.0, x)], axis=0).astype(dtype)


def _unstack_heads(x, lo):
    n = x.shape[0] // 2
    return jnp.where(lo, x[:n], x[n:])


def _stack_rows(ref, prev, cur):
    return jnp.concatenate([ref[prev, :], ref[cur, :]], axis=0).astype(BF16)


PAIRS_PER_KV = 2


def _kv_specs(kc, vc, kv_shared):
    if not kv_shared:
        return [_pair_spec(kc), _pair_spec(vc)], []
    shared = [pl.BlockSpec((S, 128), functools.partial(lambda p, c: (0, c), c=c)) for c in (kc, vc)]
    return shared, [pltpu.VMEM((S, 128), F32)] * 2


def _expand_kv(dst_ref, src_ref, pair):
    x = src_ref[...]
    own = lax.broadcasted_iota(jnp.int32, (S, 128), 1) // HD == pair // PAIRS_PER_KV
    dst_ref[...] = jnp.where(own, x, pltpu.roll(x, HD, 1))


def _blocks_of_group(group, dils, block):
    def run(d):
        lax.fori_loop(0, NQB, functools.partial(block, d), 0, unroll=2)

    if len(dils) == 1:
        run(dils[0])
        return
    for g, d in enumerate(dils):
        pl.when(group == g)(functools.partial(run, d))


def _band_fwd(name, dils, n_pairs, maxd, head0, srcs, bidx_g, tab, sinks, kv_shared=False):
    (qa, qc), (ka, kc), (va, vc) = srcs
    per_group = n_pairs // len(dils)
    out_spec = _pair_spec(0)
    smem = pl.BlockSpec(memory_space=pltpu.SMEM)
    full = pl.BlockSpec((len(dils), BLK, 2 * BLK), lambda p: (0, 0, 0))

    kv_specs, kv_scratch = _kv_specs(kc, vc, kv_shared)

    def body(tab_ref, sink_ref, q_ref, k_ref, v_ref, bidx_ref, o_ref, lse_ref, bias_ref, *expanded):
        p = pl.program_id(0)
        if kv_shared:
            _expand_kv(expanded[0], k_ref, p)
            _expand_kv(expanded[1], v_ref, p)
            k_ref, v_ref = expanded
        _band_fill(bias_ref, tab_ref, bidx_ref[p // per_group], head0 + 2 * p, maxd)
        lo = _lane_lo()
        sink = jnp.where(lax.broadcasted_iota(jnp.int32, (2 * BLK, 1), 0) < BLK, sink_ref[2 * p], sink_ref[2 * p + 1])

        def block(d, i, carry):
            cur, prev, has_prev = _band_rows(i, d)
            qs = _stack_heads(q_ref[cur, :] * SCALE, lo)
            ks, vs = _stack_rows(k_ref, prev, cur), _stack_rows(v_ref, prev, cur)
            s = _dot(qs, ks, NT) + bias_ref[has_prev]
            m = jnp.max(s, axis=1, keepdims=True)
            pr = jnp.exp(s - m)
            l = jnp.sum(pr, axis=1, keepdims=True)
            num = _dot(pr.astype(BF16), vs, NN)
            lse = m + jnp.log(l)
            sig = 1.0 / (1.0 + jnp.exp(sink - lse))
            o_ref[cur, :] = _unstack_heads(num * (sig / l), lo)
            lse_ref[cur, :] = _unstack_heads(lse + jnp.zeros((2 * BLK, 128), F32), lo)
            return carry

        _blocks_of_group(p // per_group, dils, block)

    shape = jax.ShapeDtypeStruct((S, n_pairs * 128), F32)
    return pl.pallas_call(
        body, name=name, grid=(n_pairs,),
        in_specs=[smem, smem, _pair_spec(qc)] + kv_specs + [full],
        out_specs=[out_spec, out_spec], out_shape=[shape, shape],
        scratch_shapes=[pltpu.VMEM((2, 2 * BLK, 2 * BLK), F32)] + kv_scratch,
        compiler_params=_params(("parallel",)))(tab, sinks, qa, ka, va, bidx_g)


def _band_bwd(name, dils, n_pairs, maxd, head0, srcs, bidx_g, tab, sinks, o, lse, do, stats_in, kv_shared=False):
    (qa, qc), (ka, kc), (va, vc) = srcs
    per_group = n_pairs // len(dils)
    pair = _pair_spec(0)
    shared = pl.BlockSpec((S, 128), lambda p: (0, p % per_group))
    smem = pl.BlockSpec(memory_space=pltpu.SMEM)
    full = pl.BlockSpec((len(dils), BLK, 2 * BLK), lambda p: (0, 0, 0))
    stat_spec = pl.BlockSpec((2, 8, 128), lambda p: (p, 0, 0))
    kv_specs, kv_scratch = _kv_specs(kc, vc, kv_shared)

    def body(tab_ref, sink_ref, q_ref, k_ref, v_ref, bidx_ref, o_ref, lse_ref, do_ref, sin_ref,
             dq_ref, dk_ref, dv_ref, stat_ref, bias_ref, dsacc_ref, sk_ref, *expanded):
        p = pl.program_id(0)
        if kv_shared:
            _expand_kv(expanded[0], k_ref, p)
            _expand_kv(expanded[1], v_ref, p)
            k_ref, v_ref = expanded
        _band_fill(bias_ref, tab_ref, bidx_ref[p // per_group], head0 + 2 * p, maxd)
        dsacc_ref[...] = jnp.zeros_like(dsacc_ref)
        sk_ref[...] = jnp.zeros_like(sk_ref)
        dk_ref[...] = jnp.zeros_like(dk_ref)
        dv_ref[...] = jnp.zeros_like(dv_ref)
        lo = _lane_lo()
        head1 = lax.broadcasted_iota(jnp.int32, (2 * BLK, 1), 0) >= BLK
        sink = jnp.where(head1, sink_ref[2 * p + 1], sink_ref[2 * p])

        def block(d, i, carry):
            cur, prev, has_prev = _band_rows(i, d)
            qs = _stack_heads(q_ref[cur, :] * SCALE, lo)
            ks, vs = _stack_rows(k_ref, prev, cur), _stack_rows(v_ref, prev, cur)
            do = do_ref[cur, :]
            dos = _stack_heads(do, lo, F32)
            lse = jnp.concatenate(_per_head(lse_ref[cur, :], lo), axis=0)
            prod = do * o_ref[cur, :]
            delta = jnp.concatenate([jnp.sum(jnp.where(lo, prod, 0.0), axis=1, keepdims=True),
                                     jnp.sum(jnp.where(lo, 0.0, prod), axis=1, keepdims=True)], axis=0)
            sig = 1.0 / (1.0 + jnp.exp(sink - lse))
            pr = jnp.exp(_dot(qs, ks, NT) + bias_ref[has_prev] - lse)
            ds = pr * (sig * (_dot(dos.astype(BF16), vs, NT) - delta))
            dsb = ds.astype(BF16)
            dq_ref[cur, :] = SCALE * _unstack_heads(_dot(dsb, ks, NN), lo)
            dk = _dot(dsb, qs, TN)
            dv = _dot(pr.astype(BF16), (sig * dos).astype(BF16), TN)
            dk_ref[prev, :] += dk[:BLK]
            dk_ref[cur, :] += dk[BLK:]
            dv_ref[prev, :] += dv[:BLK]
            dv_ref[cur, :] += dv[BLK:]
            dsacc_ref[...] += ds
            sink_grad = -delta * (1.0 - sig)
            for h in range(2):
                sk_ref[h] += jnp.zeros((8, 128), F32) + jnp.sum(sink_grad[h * BLK:(h + 1) * BLK])
            return carry

        _blocks_of_group(p // per_group, dils, block)

        bi = bidx_ref[p // per_group]
        lane = lax.broadcasted_iota(jnp.int32, (8, 128), 1)
        sub = lax.broadcasted_iota(jnp.int32, (8, 128), 0)
        for h in range(2):
            acc = dsacc_ref[h * BLK:(h + 1) * BLK, :]
            row = jnp.where(jnp.logical_and(sub == 1, lane == 0), sk_ref[h], 0.0)
            for kk in range(NUM_BUCKETS):
                tot = jnp.sum(jnp.where(bi == kk, acc, 0.0))
                row = jnp.where(jnp.logical_and(sub == 0, lane == kk), tot, row)
            stat_ref[h] = row + jnp.where(sub == 0, sin_ref[h], 0.0)

    shape = jax.ShapeDtypeStruct((S, n_pairs * 128), F32)
    return pl.pallas_call(
        body, name=name, grid=(n_pairs,),
        in_specs=[smem, smem, _pair_spec(qc)] + kv_specs + [full, shared, shared, shared, stat_spec],
        out_specs=[pair, pair, pair, stat_spec],
        out_shape=[shape, shape, shape, jax.ShapeDtypeStruct((2 * n_pairs, 8, 128), F32)],
        scratch_shapes=[pltpu.VMEM((2, 2 * BLK, 2 * BLK), F32), pltpu.VMEM((2 * BLK, 2 * BLK), F32),
                        pltpu.VMEM((2, 8, 128), F32)] + kv_scratch,
        compiler_params=_params(("parallel",)))(tab, sinks, qa, ka, va, bidx_g, o, lse, do, stats_in)


def _comb_fwd(o_g, lse_g):
    def body(o0, o1, o2, l0, l1, l2, out_ref, outb_ref, lse_ref):
        a0, a1, a2 = l0[...], l1[...], l2[...]
        m = jnp.maximum(jnp.maximum(a0, a1), a2)
        e0, e1, e2 = jnp.exp(a0 - m), jnp.exp(a1 - m), jnp.exp(a2 - m)
        tot = e0 + e1 + e2
        out = (e0 * o0[...] + e1 * o1[...] + e2 * o2[...]) / tot
        out_ref[...] = out
        outb_ref[...] = out.astype(BF16)
        lse_ref[...] = m + jnp.log(tot)

    spec = _row_spec(4 * HD)
    groups = [pl.BlockSpec((TR, 4 * HD), functools.partial(lambda i, g: (i, g), g=g)) for g in range(len(A_GROUPS))]
    f32 = jax.ShapeDtypeStruct((S, 4 * HD), F32)
    return pl.pallas_call(
        body, name="comb_fwd", grid=(S // TR,), in_specs=groups + groups, out_specs=[spec] * 3,
        out_shape=[f32, jax.ShapeDtypeStruct((S, 4 * HD), BF16), f32],
        compiler_params=_params(("parallel",)))(o_g, o_g, o_g, lse_g, lse_g, lse_g)


def _split2(x):
    hi = x.astype(BF16)
    return hi, (x - hi.astype(F32)).astype(BF16)


KB = 2 * BLK
SBQ = 2 * BLK


def _tri_sum(x, tri):
    hi, lo = _split2(x)
    both = _dot(jnp.concatenate([hi, lo], axis=0), tri, NN)
    return both[:x.shape[0]] + both[x.shape[0]:]


def _tri(strict_upper):
    r = lax.broadcasted_iota(jnp.int32, (KB, KB), 0)
    c = lax.broadcasted_iota(jnp.int32, (KB, KB), 1)
    return jnp.where(r > c if strict_upper else r < c, 1.0, 0.0).astype(BF16)


def _sb_terms(qs, kj, before):
    z = _dot(qs, kj, NT)
    lsp = jnp.minimum(z, 0.0) - jnp.log(1.0 + jnp.exp(-jnp.abs(z)))
    return lsp, _sb_keep(before, lsp - z)


def _sb_keep(before, x):
    return x if before is None else jnp.where(before, x, 0.0)


def _sb_before(i, m):
    t = (lax.broadcasted_iota(jnp.int32, (2 * SBQ, KB), 0) & (SBQ - 1)) + i * SBQ
    s = lax.broadcasted_iota(jnp.int32, (2 * SBQ, KB), 1) + m * KB
    return s < t


C_COL = 3072 // 128


def _sb_fwd(proj):
    blk = lambda off: pl.BlockSpec((SBQ, 128), lambda p, i: (i, off + p))
    col = lambda off: pl.BlockSpec((S, 128), lambda p, i: (0, off + p))
    out = pl.BlockSpec((SBQ, 128), lambda p, i: (i, p))

    def body(q_ref, k_ref, v_ref, o_ref, ob_ref, tot_ref):
        i = pl.program_id(1)
        lo = _lane_lo(SBQ)
        qs = _stack_heads(q_ref[...] * SCALE, lo)
        suffix = _tri(True)

        def step(n, carry, diagonal=False):
            acc, rest = carry
            m = i - n
            rows = pl.ds(pl.multiple_of(m * KB, KB), KB)
            kj, vj = k_ref[rows, :].astype(BF16), v_ref[rows, :].astype(BF16)
            before = _sb_before(i, m) if diagonal else None
            lsp, lk = _sb_terms(qs, kj, before)
            w = _sb_keep(before, jnp.exp(lsp + _tri_sum(lk, suffix) + rest))
            return acc + _dot(w.astype(BF16), vj, NN), rest + jnp.sum(lk, axis=1, keepdims=True)

        first = step(0, (jnp.zeros((2 * SBQ, 128), F32), jnp.zeros((2 * SBQ, 1), F32)), diagonal=True)
        acc, rest = lax.fori_loop(1, i + 1, step, first)
        o = _unstack_heads(acc, lo)
        o_ref[...] = o
        ob_ref[...] = o.astype(BF16)
        tot_ref[...] = _unstack_heads(rest + jnp.zeros((2 * SBQ, 128), F32), lo)

    f32 = jax.ShapeDtypeStruct((S, 4 * HD), F32)
    return pl.pallas_call(
        body, name="sb_fwd", grid=(2, S // SBQ), in_specs=[blk(C_COL), col(C_COL + 2), col(C_COL + 4)],
        out_specs=[out, out, out], out_shape=[f32, jax.ShapeDtypeStruct((S, 4 * HD), BF16), f32],
        compiler_params=_params(("parallel", "arbitrary")))(proj, proj, proj)


def _sb_bwd(proj, tot, do):
    blk = lambda off: pl.BlockSpec((SBQ, 128), lambda p, i: (i, off + p))
    col = lambda off: pl.BlockSpec((S, 128), lambda p, i: (0, off + p))

    def body(q_ref, k_ref, v_ref, tot_ref, do_ref, dq_ref, dk_ref, dv_ref):
        i = pl.program_id(1)

        @pl.when(i == 0)
        def _():
            dk_ref[...] = jnp.zeros_like(dk_ref)
            dv_ref[...] = jnp.zeros_like(dv_ref)

        lo = _lane_lo(SBQ)
        qs = _stack_heads(q_ref[...] * SCALE, lo)
        dos = _stack_heads(do_ref[...], lo)
        tots = jnp.concatenate(_per_head(tot_ref[...], lo), axis=0)
        prefix = _tri(False)

        def step(m, carry, diagonal=False):
            dq, keep_left, g_left = carry
            rows = pl.ds(pl.multiple_of(m * KB, KB), KB)
            kj, vj = k_ref[rows, :].astype(BF16), v_ref[rows, :].astype(BF16)
            before = _sb_before(i, m) if diagonal else None
            lsp, lk = _sb_terms(qs, kj, before)
            log_rest = tots - keep_left - lk - _tri_sum(lk, prefix)
            w = _sb_keep(before, jnp.exp(lsp + log_rest))
            g = w * _dot(dos, vj, NT)
            g_before = g_left + _dot(g.astype(BF16), prefix, NN)
            beta = jnp.exp(lsp)
            dz = _sb_keep(before, g * (1.0 - beta) - g_before * beta).astype(BF16)
            dk_ref[rows, :] += _dot(dz, qs, TN)
            dv_ref[rows, :] += _dot(w.astype(BF16), dos, TN)
            return (dq + _dot(dz, kj, NN), keep_left + jnp.sum(lk, axis=1, keepdims=True),
                    g_left + jnp.sum(g, axis=1, keepdims=True))

        zero = (jnp.zeros((2 * SBQ, 128), F32), jnp.zeros((2 * SBQ, 1), F32), jnp.zeros((2 * SBQ, 1), F32))
        dq, _, _ = step(i, lax.fori_loop(0, i, step, zero), diagonal=True)
        dq_ref[...] = SCALE * _unstack_heads(dq, lo)

    out_blk = pl.BlockSpec((SBQ, 128), lambda p, i: (i, p))
    out_col = pl.BlockSpec((S, 128), lambda p, i: (0, p))
    f32 = jax.ShapeDtypeStruct((S, 4 * HD), F32)
    return pl.pallas_call(
        body, name="sb_bwd", grid=(2, S // SBQ),
        in_specs=[blk(C_COL), col(C_COL + 2), col(C_COL + 4), out_blk, out_blk],
        out_specs=[out_blk, out_col, out_col], out_shape=[f32, f32, f32],
        compiler_params=_params(("arbitrary", "arbitrary")))(proj, proj, proj, tot, do)


TG = 256
TGR = 1024
GATE_BLK0 = OFF_GATE // TG


def _gate_specs():
    grid = (D // TG, S // TGR)
    p_specs = [pl.BlockSpec((TGR, TG), functools.partial(lambda c, r, br: (r, GATE_BLK0 + br * (D // TG) + c), br=br))
               for br in range(3)]
    b_spec = pl.BlockSpec((3, TG), lambda c, r: (0, c))
    t_spec = pl.BlockSpec((TGR, TG), lambda c, r: (r, c))
    return grid, p_specs, b_spec, t_spec


def _sigmoid(x):
    return 1.0 / (1.0 + jnp.exp(-x))


def _three_rows(rows):
    sub = lax.broadcasted_iota(jnp.int32, (3, rows[0].shape[1]), 0)
    return jnp.where(sub == 0, rows[0], jnp.where(sub == 1, rows[1], rows[2]))


def _gate_fwd(proj, b_gate, br):
    grid, p_specs, b_spec, t_spec = _gate_specs()

    def body(p0, p1, p2, b_ref, r0, r1, r2, out_ref):
        acc = jnp.zeros((TGR, TG), F32)
        for n, (p, r) in enumerate(((p0, r0), (p1, r1), (p2, r2))):
            acc += _sigmoid(p[...] + b_ref[n:n + 1, :]) * r[...]
        out_ref[...] = acc.astype(BF16)

    return pl.pallas_call(
        body, name="gate_fwd", grid=grid, in_specs=p_specs + [b_spec] + [t_spec] * 3, out_specs=t_spec,
        out_shape=jax.ShapeDtypeStruct((S, D), BF16),
        compiler_params=_params(("parallel", "parallel")))(proj, proj, proj, b_gate, *br)


def _gate_bwd(proj, b_gate, br, dmerged):
    grid, p_specs, b_spec, t_spec = _gate_specs()

    def body(p0, p1, p2, b_ref, r0, r1, r2, dm_ref, e0, e1, e2, g0, g1, g2, db_ref):
        dm = dm_ref[...]
        rows = []
        for n, (p, r, e_ref, dg_ref) in enumerate(((p0, r0, e0, g0), (p1, r1, e1, g1), (p2, r2, e2, g2))):
            g = _sigmoid(p[...] + b_ref[n:n + 1, :])
            e_ref[...] = (dm * g).astype(BF16)
            dpre = dm * r[...] * g * (1.0 - g)
            dg_ref[...] = dpre.astype(BF16)
            rows.append(jnp.sum(dpre, axis=0, keepdims=True))
        db = _three_rows(rows)

        @pl.when(pl.program_id(1) == 0)
        def _():
            db_ref[...] = db

        @pl.when(pl.program_id(1) > 0)
        def _():
            db_ref[...] += db

    bf = jax.ShapeDtypeStruct((S, D), BF16)
    out = pl.pallas_call(
        body, name="gate_bwd", grid=grid, in_specs=p_specs + [b_spec] + [t_spec] * 4,
        out_specs=[t_spec] * 6 + [b_spec], out_shape=[bf] * 6 + [jax.ShapeDtypeStruct((3, D), F32)],
        compiler_params=_params(("parallel", "arbitrary")))(proj, proj, proj, b_gate, *br, dmerged)
    return out[:3], out[3:6], out[6]


TC = 256
N_FF_BLK = D_FF // TC
GELU_C = math.sqrt(2.0 / math.pi)


def _shift_down(x, n):
    rows = lax.broadcasted_iota(jnp.int32, x.shape, 0)
    return jnp.where(rows >= n, pltpu.roll(x, n, axis=0), 0.0)


def _shift_up(x, n):
    rows = lax.broadcasted_iota(jnp.int32, x.shape, 0)
    return jnp.where(rows < x.shape[0] - n, pltpu.roll(x, x.shape[0] - n, axis=0), 0.0)


def _conv(u, w, b):
    s1, s2 = _shift_down(u, 1), _shift_down(u, 2)
    return w[2:3, :] * u + w[1:2, :] * s1 + w[0:1, :] * s2 + b, s1, s2


def _gelu_parts(x):
    inner = GELU_C * (x + 0.044715 * x * x * x)
    t = jnp.tanh(inner)
    gelu = 0.5 * x * (1.0 + t)
    dgelu = 0.5 * (1.0 + t) + 0.5 * x * (1.0 - t * t) * GELU_C * (1.0 + 3 * 0.044715 * x * x)
    return gelu, dgelu


def _conv_specs():
    ug = pl.BlockSpec((S, TC), lambda c: (0, c))
    uv = pl.BlockSpec((S, TC), lambda c: (0, N_FF_BLK + c))
    wg = pl.BlockSpec((3, TC), lambda c: (0, c))
    wv = pl.BlockSpec((3, TC), lambda c: (0, N_FF_BLK + c))
    bg = pl.BlockSpec((1, TC), lambda c: (0, c))
    bv = pl.BlockSpec((1, TC), lambda c: (0, N_FF_BLK + c))
    return ug, uv, wg, wv, bg, bv


def _conv_fwd(u, conv_w, conv_b):
    ug, uv, wg, wv, bg, bv = _conv_specs()

    def body(ug_ref, uv_ref, wg_ref, wv_ref, bg_ref, bv_ref, a_ref):
        gc = _conv(ug_ref[...], wg_ref[...], bg_ref[...])[0]
        vc = _conv(uv_ref[...], wv_ref[...], bv_ref[...])[0]
        a_ref[...] = (_gelu_parts(gc)[0] * vc).astype(BF16)

    return pl.pallas_call(
        body, name="conv_fwd", grid=(N_FF_BLK,), in_specs=[ug, uv, wg, wv, bg, bv], out_specs=ug,
        out_shape=jax.ShapeDtypeStruct((S, D_FF), BF16),
        compiler_params=_params(("parallel",)))(u, u, conv_w, conv_w, conv_b, conv_b)


def _conv_bwd(u, conv_w, conv_b, da):
    ug, uv, wg, wv, bg, bv = _conv_specs()

    def back(duc, u, s1, s2, w):
        du = w[2:3, :] * duc + w[1:2, :] * _shift_up(duc, 1) + w[0:1, :] * _shift_up(duc, 2)
        dw = _three_rows([jnp.sum(duc * s2, axis=0, keepdims=True), jnp.sum(duc * s1, axis=0, keepdims=True),
                          jnp.sum(duc * u, axis=0, keepdims=True)])
        return du, dw, jnp.sum(duc, axis=0, keepdims=True)

    def body(ug_ref, uv_ref, wg_ref, wv_ref, bg_ref, bv_ref, da_ref, dug_ref, duv_ref, dwg_ref, dwv_ref, dbg_ref, dbv_ref):
        u_g, u_v = ug_ref[...], uv_ref[...]
        gc, g1, g2 = _conv(u_g, wg_ref[...], bg_ref[...])
        vc, v1, v2 = _conv(u_v, wv_ref[...], bv_ref[...])
        gelu, dgelu = _gelu_parts(gc)
        da = da_ref[...]
        du, dw, db = back(da * vc * dgelu, u_g, g1, g2, wg_ref[...])
        dug_ref[...] = du.astype(BF16)
        dwg_ref[...] = dw
        dbg_ref[...] = db
        du, dw, db = back(da * gelu, u_v, v1, v2, wv_ref[...])
        duv_ref[...] = du.astype(BF16)
        dwv_ref[...] = dw
        dbv_ref[...] = db

    return pl.pallas_call(
        body, name="conv_bwd", grid=(N_FF_BLK,), in_specs=[ug, uv, wg, wv, bg, bv, ug],
        out_specs=[ug, ug, wg, wg, bg, bg],
        out_shape=[jax.ShapeDtypeStruct((S, D_FF), BF16), jax.ShapeDtypeStruct((S, D_FF), BF16),
                   jax.ShapeDtypeStruct((3, D_FF), F32), jax.ShapeDtypeStruct((3, D_FF), F32),
                   jax.ShapeDtypeStruct((1, D_FF), F32), jax.ShapeDtypeStruct((1, D_FF), F32)],
        compiler_params=_params(("parallel",)))(u, u, conv_w, conv_w, conv_b, conv_b, da)


def _adamw(name, w, g, m, v):
    shape = w.shape
    cols = shape[-1]
    flat = [t.reshape(-1, cols) for t in (w, g, m, v)]
    r = flat[0].shape[0]
    tr = min(r, max(8, 2 * 1024 * 1024 // (4 * cols)))

    def body(w_ref, g_ref, m_ref, v_ref, go_ref, d_ref, mo_ref, vo_ref):
        g = g_ref[...]
        go_ref[...] = g
        m = ADAM_B1 * m_ref[...] + (1.0 - ADAM_B1) * g
        v = ADAM_B2 * v_ref[...] + (1.0 - ADAM_B2) * (g * g)
        m_hat = m / (1.0 - ADAM_B1 ** ADAM_STEP)
        v_hat = v / (1.0 - ADAM_B2 ** ADAM_STEP)
        d_ref[...] = -ADAM_LR * (m_hat / (jnp.sqrt(v_hat) + ADAM_EPS) + ADAM_WD * w_ref[...])
        mo_ref[...] = m
        vo_ref[...] = v

    spec = pl.BlockSpec((tr, cols), lambda i: (i, 0))
    outs = pl.pallas_call(
        body, name=name, grid=(pl.cdiv(r, tr),), in_specs=[spec] * 4, out_specs=[spec] * 4,
        out_shape=[jax.ShapeDtypeStruct((r, cols), F32)] * 4, compiler_params=_params(("parallel",)))(*flat)
    return [t.reshape(shape) for t in outs]


def _place():
    x, y, c = lax.axis_index("x"), lax.axis_index("y"), lax.axis_index("c")
    chips = [(1 - x, y), (x, 1 - y), (1 - x, 1 - y)]
    return x, y, c, chips


def _scalars(*vals):
    return jnp.stack([jnp.asarray(v, jnp.int32) for v in vals])


HBM = pl.BlockSpec(memory_space=pltpu.HBM)
SEM = pl.BlockSpec(memory_space=pltpu.SEMAPHORE)
SPLIT_COPY = pltpu.CompilerParams(has_side_effects=pltpu.SideEffectType.DATAFLOW_SIDE_EFFECTING)


def _in_hbm(x):
    return pltpu.with_memory_space_constraint(x, pltpu.HBM)


def _cast_into_slot(name, w, layer, chip):
    _, k, n4 = w.shape
    tr = max(t for t in range(16, 513, 16) if k % t == 0)

    def body(chip_ref, w_ref, o_ref):
        o_ref[...] = w_ref[...].astype(BF16)

    return pl.pallas_call(
        body, name=name,
        grid_spec=pltpu.PrefetchScalarGridSpec(
            num_scalar_prefetch=1, grid=(k // tr,),
            in_specs=[pl.BlockSpec((None, tr, n4), lambda i, s: (layer, i, 0))],
            out_specs=pl.BlockSpec((None, tr, n4), lambda i, s: (s[0], i, 0))),
        out_shape=jax.ShapeDtypeStruct((N_CHIPS, k, n4), BF16),
        compiler_params=_params(("parallel",)))(_scalars(chip), w)


def _gather_copy(buf_ref, k, from_chip, send_sem, recv_sem, chips, c, half=False):
    rows = buf_ref.at[from_chip]
    if half:
        h = buf_ref.shape[1] // 2
        rows = buf_ref.at[from_chip, pl.ds(pl.multiple_of(c * h, h), h)]
    return pltpu.make_async_remote_copy(src_ref=rows, dst_ref=rows, send_sem=send_sem, recv_sem=recv_sem,
                                        device_id=(*chips[k], c), device_id_type=MESH)


def _gather_start(name, bufs, groups, halved=()):
    n, ng = len(bufs), len(groups)
    where = {a: (gi, e) for gi, g in enumerate(groups) for e, a in enumerate(g)}

    def body(*refs):
        ins, sems, token = refs[:n], refs[n:n + 2 * ng], refs[-1]
        x, y, c, chips = _place()
        for a in range(n):
            gi, e = where[a]
            for k in range(3):
                _gather_copy(ins[a], k, 2 * x + y, sems[2 * gi].at[3 * e + k], sems[2 * gi + 1].at[3 * e + k],
                             chips, c, a in halved).start()
        token[...] = jnp.zeros_like(token)

    out_shape = [pltpu.SemaphoreType.DMA((3 * len(g),)) for g in groups for _ in range(2)]
    out_shape += [pltpu.HBM(b.shape, b.dtype) for b in bufs] + [jax.ShapeDtypeStruct((8, 128), F32)]
    out = pl.pallas_call(
        body, name=name, in_specs=[HBM] * n,
        out_specs=[SEM] * (2 * ng) + [HBM] * n + [pl.BlockSpec(memory_space=pltpu.VMEM)], out_shape=out_shape,
        input_output_aliases={a: 2 * ng + a for a in range(n)}, compiler_params=SPLIT_COPY)(*[_in_hbm(b) for b in bufs])
    sems = [(out[2 * gi], out[2 * gi + 1]) for gi in range(ng)]
    return sems, list(out[2 * ng:2 * ng + n]), out[-1]


def _gather_wait(name, bufs, send, recv, after, halved=()):
    n = len(bufs)

    def body(*refs):
        ins, send_sem, recv_sem = refs[:n], refs[n], refs[n + 1]
        x, y, c, chips = _place()
        for e in range(n):
            for k in range(3):
                sems = (send_sem.at[3 * e + k], recv_sem.at[3 * e + k])
                _gather_copy(ins[e], k, 2 * x + y, *sems, chips, c, e in halved).wait_send()
                _gather_copy(ins[e], k, 2 * chips[k][0] + chips[k][1], *sems, chips, c, e in halved).wait_recv()

    return pl.pallas_call(
        body, name=name, in_specs=[HBM] * n + [SEM, SEM, ANY], out_specs=[HBM] * n,
        out_shape=[pltpu.HBM(b.shape, b.dtype) for b in bufs],
        input_output_aliases={a: a for a in range(n)}, compiler_params=SPLIT_COPY)(*bufs, send, recv, after)


def _swap_halves(name, bufs):
    n = len(bufs)

    def body(*refs):
        ins, outs = refs[:n], refs[n:2 * n]
        send_sem, recv_sem = refs[2 * n:]
        x, y, c, chips = _place()

        def piece(ref, k, which):
            h = ref.shape[1] // 2
            return ref.at[2 * chips[k][0] + chips[k][1], pl.ds(pl.multiple_of(which * h, h), h)]

        def copy(a, k, which):
            return pltpu.make_async_remote_copy(
                src_ref=piece(ins[a], k, c), dst_ref=piece(outs[a], k, which), send_sem=send_sem.at[3 * a + k],
                recv_sem=recv_sem.at[3 * a + k], device_id=(x, y, 1 - c), device_id_type=MESH)

        for a in range(n):
            for k in range(3):
                copy(a, k, c).start()
        for a in range(n):
            for k in range(3):
                copy(a, k, c).wait_send()
                copy(a, k, 1 - c).wait_recv()

    return pl.pallas_call(
        body, name=name, in_specs=[ANY] * n, out_specs=[ANY] * n,
        out_shape=[jax.ShapeDtypeStruct(b.shape, b.dtype) for b in bufs],
        input_output_aliases={a: a for a in range(n)},
        scratch_shapes=[pltpu.SemaphoreType.DMA((3 * n,)), pltpu.SemaphoreType.DMA((3 * n,))],
    )(*bufs)


def _reduce_copy(g_ref, land_ref, mask, send_sem, recv_sem, x, y, c, sending):
    px, py, pc = x ^ ((mask >> 2) & 1), y ^ ((mask >> 1) & 1), c ^ (mask & 1)
    half = g_ref.shape[1] // 2
    src = g_ref.at[2 * px + py, pl.ds(pl.multiple_of(pc * half, half), half)]
    dst = land_ref.at[4 * x + 2 * y + c] if sending else land_ref.at[4 * px + 2 * py + pc]
    return pltpu.make_async_remote_copy(src_ref=src, dst_ref=dst, send_sem=send_sem, recv_sem=recv_sem,
                                        device_id=(px, py, pc), device_id_type=MESH)


def _reduce_start(name, grads):
    n = len(grads)
    lands = [lax.empty((N_DEV, g.shape[1] // 2, g.shape[2]), g.dtype) for g in grads]

    def body(*refs):
        gs, ls, send_sem, recv_sem = refs[:n], refs[n:2 * n], refs[2 * n], refs[2 * n + 1]
        x, y, c, _ = _place()
        for a in range(n):
            for mask in range(1, N_DEV):
                s = (N_DEV - 1) * a + mask - 1
                _reduce_copy(gs[a], ls[a], mask, send_sem.at[s], recv_sem.at[s], x, y, c, True).start()
        refs[-1][...] = jnp.zeros_like(refs[-1])

    sem = pltpu.SemaphoreType.DMA((n * (N_DEV - 1),))
    out = pl.pallas_call(
        body, name=name, in_specs=[HBM] * (2 * n),
        out_specs=[SEM, SEM] + [HBM] * (2 * n) + [pl.BlockSpec(memory_space=pltpu.VMEM)],
        out_shape=[sem, sem] + [pltpu.HBM(t.shape, t.dtype) for t in grads + lands] + [jax.ShapeDtypeStruct((8, 128), F32)],
        input_output_aliases={a: 2 + a for a in range(2 * n)}, compiler_params=SPLIT_COPY)(
            *[_in_hbm(t) for t in grads + lands])
    return out[0], out[1], list(out[2:2 + n]), list(out[2 + n:2 + 2 * n]), out[-1]


def _reduce_wait(name, send, recv, grads, lands, after):
    n = len(grads)

    def body(*refs):
        gs, ls, send_sem, recv_sem = refs[:n], refs[n:2 * n], refs[2 * n], refs[2 * n + 1]
        x, y, c, _ = _place()
        for a in range(n):
            for mask in range(1, N_DEV):
                s = (N_DEV - 1) * a + mask - 1
                sems = (send_sem.at[s], recv_sem.at[s])
                _reduce_copy(gs[a], ls[a], mask, *sems, x, y, c, True).wait_send()
                _reduce_copy(gs[a], ls[a], mask, *sems, x, y, c, False).wait_recv()

    out = pl.pallas_call(
        body, name=name, in_specs=[HBM] * (2 * n) + [SEM, SEM, ANY], out_specs=[HBM] * (2 * n),
        out_shape=[pltpu.HBM(t.shape, t.dtype) for t in grads + lands],
        input_output_aliases={a: a for a in range(2 * n)}, compiler_params=SPLIT_COPY)(*grads, *lands, send, recv, after)
    return list(out[:n]), list(out[n:])


def _reduce_sum(name, g, land, layer, into, chip, c):
    _, k4, n4 = g.shape
    half = k4 // 2
    tr = max(t for t in range(16, 513, 16) if half % t == 0)
    per = half // tr
    me = 2 * chip + c

    def body(s_ref, own_ref, *refs):
        total = own_ref[...].astype(F32)
        for ref in refs[:N_DEV - 1]:
            total = total + ref[...].astype(F32)
        refs[-1][...] = total

    in_specs = [pl.BlockSpec((None, tr, n4), lambda i, s: (s[0], s[1] * per + i, 0))]
    in_specs += [pl.BlockSpec((None, tr, n4), functools.partial(lambda i, s, m: (s[1 + m], i, 0), m=m))
                 for m in range(1, N_DEV)]
    ins = [g] + [land] * (N_DEV - 1)
    aliases = {}
    if into is not None:
        in_specs, ins, aliases = in_specs + [ANY], ins + [into], {1 + N_DEV: 0}
    return pl.pallas_call(
        body, name=name,
        grid_spec=pltpu.PrefetchScalarGridSpec(
            num_scalar_prefetch=1, grid=(per,), in_specs=in_specs,
            out_specs=pl.BlockSpec((None, tr, n4), lambda i, s: (layer, s[1] * per + i, 0))),
        out_shape=jax.ShapeDtypeStruct((DEPTH, k4, n4), F32), input_output_aliases=aliases,
        compiler_params=_params(("parallel",)))(_scalars(chip, c, *[me ^ m for m in range(1, N_DEV)]), *ins)


def _join_halves(name, bufs):
    n = len(bufs)

    def body(*refs):
        ins, outs = refs[:n], refs[n:2 * n]
        send_sem, recv_sem = refs[2 * n:]
        x, y, c, _ = _place()

        def rows(ref, which):
            half = ref.shape[1] // 2
            return ref.at[:, pl.ds(pl.multiple_of(which * half, half), half)]

        sends = [pltpu.make_async_remote_copy(
            src_ref=rows(ins[a], c), dst_ref=rows(outs[a], c), send_sem=send_sem.at[a], recv_sem=recv_sem.at[a],
            device_id=(x, y, 1 - c), device_id_type=MESH) for a in range(n)]
        for cp in sends:
            cp.start()
        for a in range(n):
            sends[a].wait_send()
            pltpu.make_async_remote_copy(
                src_ref=rows(ins[a], c), dst_ref=rows(outs[a], 1 - c), send_sem=send_sem.at[a], recv_sem=recv_sem.at[a],
                device_id=(x, y, 1 - c), device_id_type=MESH).wait_recv()

    return pl.pallas_call(
        body, name=name, in_specs=[ANY] * n, out_specs=[ANY] * n,
        out_shape=[jax.ShapeDtypeStruct(b.shape, b.dtype) for b in bufs],
        input_output_aliases={a: a for a in range(n)},
        scratch_shapes=[pltpu.SemaphoreType.DMA((n,)), pltpu.SemaphoreType.DMA((n,))],
    )(*bufs)


def _small_copy(b_ref, l_ref, mask, send_sem, recv_sem, x, y, c, sending):
    px, py, pc = x ^ ((mask >> 2) & 1), y ^ ((mask >> 1) & 1), c ^ (mask & 1)
    dst = l_ref.at[4 * x + 2 * y + c] if sending else l_ref.at[4 * px + 2 * py + pc]
    return pltpu.make_async_remote_copy(src_ref=b_ref, dst_ref=dst, send_sem=send_sem.at[mask - 1],
                                        recv_sem=recv_sem.at[mask - 1], device_id=(px, py, pc), device_id_type=MESH)


def _small_start(block):
    land = lax.empty((N_DEV,) + block.shape, block.dtype)

    def body(b_ref, l_ref, send_sem, recv_sem, b_thru, l_thru, token):
        x, y, c, _ = _place()
        for mask in range(1, N_DEV):
            _small_copy(b_ref, l_ref, mask, send_sem, recv_sem, x, y, c, True).start()
        token[...] = jnp.zeros_like(token)

    sem = pltpu.SemaphoreType.DMA((N_DEV - 1,))
    return pl.pallas_call(
        body, name="small_start", in_specs=[HBM, HBM],
        out_specs=[SEM, SEM, HBM, HBM, pl.BlockSpec(memory_space=pltpu.VMEM)],
        out_shape=[sem, sem, pltpu.HBM(block.shape, block.dtype), pltpu.HBM(land.shape, land.dtype),
                   jax.ShapeDtypeStruct((8, 128), F32)],
        input_output_aliases={0: 2, 1: 3}, compiler_params=SPLIT_COPY)(_in_hbm(block), _in_hbm(land))


def _small_wait(send, recv, block, land, after):
    def body(b_ref, l_ref, send_sem, recv_sem, after_ref, b_out, l_out):
        x, y, c, _ = _place()
        for mask in range(1, N_DEV):
            _small_copy(b_ref, l_ref, mask, send_sem, recv_sem, x, y, c, True).wait_send()
            _small_copy(b_ref, l_ref, mask, send_sem, recv_sem, x, y, c, False).wait_recv()

    return pl.pallas_call(
        body, name="small_wait", in_specs=[HBM, HBM, SEM, SEM, ANY], out_specs=[HBM, HBM],
        out_shape=[pltpu.HBM(block.shape, block.dtype), pltpu.HBM(land.shape, land.dtype)],
        input_output_aliases={0: 0, 1: 1}, compiler_params=SPLIT_COPY)(block, land, send, recv, after)


def _small_sum(land):
    def body(l_ref, out_ref):
        total = l_ref[0]
        for d in range(1, N_DEV):
            total = total + l_ref[d]
        out_ref[...] = total

    vmem = pl.BlockSpec(memory_space=pltpu.VMEM)
    return pl.pallas_call(
        body, name="small_sum", in_specs=[vmem], out_specs=vmem,
        out_shape=jax.ShapeDtypeStruct(land.shape[1:], F32),
        compiler_params=pltpu.CompilerParams(vmem_limit_bytes=VMEM_LIMIT))(land)


B_Q_COL = 2304 // 128
B_K0, B_V0 = 2816, 2944


def _full_cols(w_g):
    return w_g.transpose(1, 0, 2).reshape(w_g.shape[1], -1)


A_DILS = tuple(d for _, d in A_GROUPS)
A_PAIRS = N_A // 2


def _src_a(proj):
    return ((proj, 0), (proj, A_PAIRS), (proj, 2 * A_PAIRS))


def _kv_reduce(dkv):
    return dkv.reshape(S, 2, 2 * PAIRS_PER_KV, HD).sum(axis=2).reshape(S, 2 * HD)


def _mixer_fwd(h1, wget, rel_bias, sinks_l, bidx):
    w = dict(wget(0, h1))
    proj = _mm_nt("proj_in", h1, w["w_in"], F32, tm=S, tn=1152)
    no_sinks = jnp.full((N_A,), NEG, F32)
    o_g, lse_g = _band_fwd("band_fwd_a", A_DILS, A_PAIRS, BLK, 0, _src_a(proj), bidx[:3], rel_bias, no_sinks)
    o_a32, o_a, lse_a = _comb_fwd(o_g, lse_g)
    src_b = ((proj, B_Q_COL), (proj, B_K0 // 128), (proj, B_V0 // 128))
    o_b32, lse_b = _band_fwd("band_fwd_b", (1,), 4, BLK - 1, N_A, src_b, bidx[3:], rel_bias, sinks_l, kv_shared=True)
    o_b = o_b32.astype(BF16)
    o_c32, o_c, tot_c = _sb_fwd(proj)
    w.update(wget(1, o_c32))
    br = [_mm_nn("branch_a", o_a, w["w_br_a"], F32, tm=S), _mm_nn("branch_b", o_b, w["w_br_b"], F32, tm=S),
          _mm_nn("branch_c", o_c, w["w_br_c"], F32, tm=S)]
    merged = _gate_fwd(proj, w["b_gate"], br)
    mo = _mm_nn("out_proj", merged, w["w_out"], F32, tm=S)
    saved = dict(proj=proj, src_b=src_b, o_a32=o_a32, lse_a=lse_a, o_b32=o_b32, lse_b=lse_b, tot_c=tot_c,
                 o_a=o_a, o_b=o_b, o_c=o_c, br=br, merged=merged)
    return mo, saved, w


def _mixer_bwd(d_mo, h1, w, sv, rel_bias, sinks_l, bidx, stats_in, emit):
    grads = {}
    dmerged = _mm_nt("out_proj_dx", d_mo, w["w_out"], F32, tm=S)
    grads["w_out"] = _mm_tn_sharded("out_proj_dw", sv["merged"], d_mo, True)
    e, dgate, db_gate = _gate_bwd(sv["proj"], w["b_gate"], sv["br"], dmerged)
    grads["b_gate"] = db_gate
    d_o = {}
    for n, name in enumerate("abc"):
        d_o[name] = _mm_nt("branch_%s_dx" % name, e[n], w["w_br_" + name], F32, tm=S)
        grads["w_br_" + name] = _mm_tn_sharded("branch_%s_dw" % name, sv["o_" + name], e[n], False)
    zero = emit(1, grads)
    no_sinks = jnp.full((N_A,), NEG, F32) + zero[0]
    dq_a, dk_a, dv_a, st_a = _band_bwd("band_bwd_a", A_DILS, A_PAIRS, BLK, 0, _src_a(sv["proj"]), bidx[:3], rel_bias,
                                       no_sinks, sv["o_a32"], sv["lse_a"], d_o["a"], stats_in[:N_A])
    dq_b, dk_x, dv_x, st_b = _band_bwd("band_bwd_b", (1,), 4, BLK - 1, N_A, sv["src_b"], bidx[3:], rel_bias, sinks_l,
                                       sv["o_b32"], sv["lse_b"], d_o["b"], stats_in[N_A:], kv_shared=True)
    stats = jnp.concatenate([st_a, st_b], axis=0)
    dcq, dck, dcv = _sb_bwd(sv["proj"], sv["tot_c"], d_o["c"])
    cols = [dq_a, dk_a, dv_a, dq_b, _kv_reduce(dk_x), _kv_reduce(dv_x), dcq, dck, dcv]
    dproj = jnp.concatenate([t.astype(BF16) for t in cols] + list(dgate), axis=1)
    grads["w_in"] = _mm_tn("proj_in_dw", dproj, h1, BF16, tm=1152, tn=1024).reshape(N_CHIPS, IN_SHARD, D)
    zero = emit(2, grads)
    dh1 = _mm_nn("proj_in_dx", dproj, w["w_in"], F32, tm=S, tk=2304)
    return dh1, grads, stats, zero


def _ffn_fwd(h2, w):
    u = _mm_nn("ffn_up", h2, w["w_up"], F32, tm=S, tn=1024)
    a = _conv_fwd(u, w["conv_w"], w["conv_b"])
    dn = _mm_nn("ffn_down", a, w["w_down"], F32, tm=1024)
    return dn, dict(u=u, a=a)


def _ffn_bwd(d_dn, h2, w, sv):
    grads = {}
    da = _mm_nt("ffn_down_dx", d_dn, w["w_down"], F32, tm=S, tn=1024)
    grads["w_down"] = _mm_tn_sharded("ffn_down_dw", sv["a"], d_dn, True, tm=1024, tn=1024)
    dug, duv, dwg, dwv, dbg, dbv = _conv_bwd(sv["u"], w["conv_w"], w["conv_b"], da)
    du = jnp.concatenate([dug, duv], axis=1)
    grads["conv_w"] = jnp.concatenate([dwg, dwv], axis=1)
    grads["conv_b"] = jnp.concatenate([dbg, dbv], axis=1)
    dh2 = _mm_nt("ffn_up_dx", du, w["w_up"], F32, tm=S, tk=2048)
    grads["w_up"] = _mm_tn_sharded("ffn_up_dw", h2, du, False, tm=1024, tn=1024)
    return dh2, grads


BIG = ("w_in", "w_br_a", "w_br_b", "w_br_c", "w_out", "w_up", "w_down")


def _shard_view(name, w):
    return jnp.swapaxes(w, 1, 2) if name == "w_in" else w
WEIGHT_GROUPS = (("w_in", "b_gate"), ("w_br_a", "w_br_b", "w_br_c", "w_out"), ("w_up", "conv_w", "w_down"))
GRAD_GROUPS = (("w_down", "w_up"), ("w_out", "w_br_a", "w_br_b", "w_br_c"), ("w_in",))
SMALL_ROWS = (("rel_bias", 8), ("attn_pre_norm", 16), ("attn_post_norm", 16), ("ffn_pre_norm", 16), ("ffn_post_norm", 16),
              ("sinks", 8), ("conv_b", 128), ("b_gate", 48), ("conv_w", 384), ("loss", 8))


def _pack_small(vals):
    rows = []
    for name, n in SMALL_ROWS:
        flat = vals[name].reshape(-1).astype(F32)
        rows.append(jnp.pad(flat, (0, n * 128 - flat.shape[0])).reshape(n, 128))
    return jnp.concatenate(rows, axis=0)


def _unpack_small(block, shapes):
    out, row = {}, 0
    for name, n in SMALL_ROWS:
        size = int(np.prod(shapes[name]))
        out[name] = block[row:row + n].reshape(-1)[:size].reshape(shapes[name])
        row += n
    return out


def kernel(x, rel_bias, attn_pre_norm, w_in, b_gate, sinks, w_br_a, w_br_b, w_br_c, w_out, attn_post_norm, ffn_pre_norm, w_up, conv_w, conv_b, w_down, ffn_post_norm, loss_target, m_rel_bias, m_attn_pre_norm, m_w_in, m_b_gate, m_sinks, m_w_br_a, m_w_br_b, m_w_br_c, m_w_out, m_attn_post_norm, m_ffn_pre_norm, m_w_up, m_conv_w, m_conv_b, m_w_down, m_ffn_post_norm, v_rel_bias, v_attn_pre_norm, v_w_in, v_b_gate, v_sinks, v_w_br_a, v_w_br_b, v_w_br_c, v_w_out, v_attn_post_norm, v_ffn_pre_norm, v_w_up, v_conv_w, v_conv_b, v_w_down, v_ffn_post_norm):
    names = ("rel_bias", "attn_pre_norm", "w_in", "b_gate", "sinks", "w_br_a", "w_br_b", "w_br_c", "w_out",
             "attn_post_norm", "ffn_pre_norm", "w_up", "conv_w", "conv_b", "w_down", "ffn_post_norm")
    weights = dict(zip(names, (rel_bias, attn_pre_norm, w_in, b_gate, sinks, w_br_a, w_br_b, w_br_c, w_out,
                               attn_post_norm, ffn_pre_norm, w_up, conv_w, conv_b, w_down, ffn_post_norm)))
    mom1 = dict(zip(names, (m_rel_bias, m_attn_pre_norm, m_w_in, m_b_gate, m_sinks, m_w_br_a, m_w_br_b, m_w_br_c,
                            m_w_out, m_attn_post_norm, m_ffn_pre_norm, m_w_up, m_conv_w, m_conv_b, m_w_down,
                            m_ffn_post_norm)))
    mom2 = dict(zip(names, (v_rel_bias, v_attn_pre_norm, v_w_in, v_b_gate, v_sinks, v_w_br_a, v_w_br_b, v_w_br_c,
                            v_w_out, v_attn_post_norm, v_ffn_pre_norm, v_w_up, v_conv_w, v_conv_b, v_w_down,
                            v_ffn_post_norm)))

    chip = 2 * lax.axis_index("x") + lax.axis_index("y")
    core = lax.axis_index("c")

    keys = [(n, l) for l in range(DEPTH) for group in WEIGHT_GROUPS for n in group]
    groups = [[keys.index((n, l)) for n in group] for l in range(DEPTH) for group in WEIGHT_GROUPS]

    def slot_buffer(n, l):
        if n in BIG:
            return _cast_into_slot("cast_" + n, _shard_view(n, weights[n]), l, chip)
        shard = weights[n][l]
        return lax.dynamic_update_slice(jnp.zeros((N_CHIPS,) + shard.shape, F32), shard[None],
                                        (chip, jnp.int32(0), jnp.int32(0)))

    by_halves = [keys.index(k) for k in (("w_in", 0), ("w_up", DEPTH - 1), ("w_down", DEPTH - 1))]
    n_first = len(groups[0])
    sems, in_flight, _ = _gather_start("gather_start_first", [slot_buffer(*k) for k in keys[:n_first]], groups[:1],
                                       tuple(a for a in by_halves if a < n_first))
    more = _gather_start("gather_start", [slot_buffer(*k) for k in keys[n_first:]],
                         [[a - n_first for a in g] for g in groups[1:]],
                         tuple(a - n_first for a in by_halves if a >= n_first))
    sems, in_flight, started = sems + more[0], in_flight + more[1], more[2]

    def wget(l, gi, after):
        g = l * len(WEIGHT_GROUPS) + gi
        after = started if g == 0 else after
        halved = tuple(e for e, a in enumerate(groups[g]) if a in by_halves)
        got = list(_gather_wait("gather_wait_%d_%d" % (l, gi), [in_flight[a] for a in groups[g]], *sems[g], after,
                                halved))
        if halved:
            for e, buf in zip(halved, _swap_halves("swap_halves_%d_%d" % (l, gi), [got[e] for e in halved])):
                got[e] = buf
        out = {}
        for n, buf in zip(WEIGHT_GROUPS[gi], got):
            if n in ("w_in", "w_out", "w_down"):
                out[n] = buf.reshape(-1, buf.shape[-1])
            else:
                out[n] = buf if n == "w_up" else _full_cols(buf)
        if gi == len(WEIGHT_GROUPS) - 1:
            out["conv_b"] = conv_b[l:l + 1]
        return out

    pending = []

    def emit(l, gi, grads):
        group = GRAD_GROUPS[gi]
        *started, token = _reduce_start("reduce_start_%d_%d" % (l, gi), [grads[n] for n in group])
        pending.append((l, group) + tuple(started))
        return token[:1, :1]

    local = _local_step(x.reshape(S, D), loss_target.reshape(S, D), wget, emit, rel_bias, sinks, attn_pre_norm,
                        attn_post_norm, ffn_pre_norm, ffn_post_norm)
    return _reduce_and_update(x.shape, names, weights, mom1, mom2, chip, core, pending, *local)


def _local_step(xs, target, wget, emit, rel_bias, sinks, attn_pre_norm, attn_post_norm, ffn_pre_norm, ffn_post_norm):
    bidx = jnp.asarray(_bucket_maps())

    saved, layers = [], []
    h1 = _rms_fwd("pre_norm_first", xs, attn_pre_norm[0:1])
    x_in = xs
    for l in range(DEPTH):
        mo, sv_mix, w = _mixer_fwd(h1, functools.partial(wget, l), rel_bias, sinks[l], bidx)
        x_mid, h2 = _post_pre_fwd("post_attn_norm", x_in, mo, attn_post_norm[l:l + 1], ffn_pre_norm[l:l + 1])
        w.update(wget(l, 2, h2))
        dn, sv_ffn = _ffn_fwd(h2, w)
        g_next = attn_pre_norm[l + 1:l + 2] if l + 1 < DEPTH else None
        x_out, h1_next = _post_pre_fwd("post_ffn_norm" if l + 1 < DEPTH else "post_ffn_norm_last", x_mid, dn,
                                       ffn_post_norm[l:l + 1], g_next)
        saved.append(dict(x_in=x_in, h1=h1, mo=mo, x_mid=x_mid, h2=h2, dn=dn, mix=sv_mix, ffn=sv_ffn))
        layers.append(w)
        x_in, h1 = x_out, h1_next

    loss_row, dres = _loss_kernel(x_in, target)

    small = [None] * DEPTH
    stats = jnp.zeros((N_BAND_Q, 8, 128), F32)
    dh_next = None
    for l in reversed(range(DEPTH)):
        w, sv = layers[l], saved[l]
        if l + 1 < DEPTH:
            pre = (saved[l + 1]["x_in"], attn_pre_norm[l + 1:l + 2] + zero, dh_next)
            dres, d_dn, dg_pre_next, dg_fpost = _norm_bwd("post_ffn_norm_bwd", dres, pre,
                                                          (sv["dn"], ffn_post_norm[l:l + 1]))
            small[l + 1]["attn_pre_norm"] = dg_pre_next
        else:
            dres, d_dn, _, dg_fpost = _norm_bwd("post_ffn_norm_last_bwd", dres, None, (sv["dn"], ffn_post_norm[l:l + 1]))
        dh2, g_ffn = _ffn_bwd(d_dn, sv["h2"], w, sv["ffn"])
        zero = emit(l, 0, g_ffn)
        dres, d_mo, dg_fpre, dg_apost = _norm_bwd("post_attn_norm_bwd", dres,
                                                  (sv["x_mid"], ffn_pre_norm[l:l + 1] + zero, dh2),
                                                  (sv["mo"], attn_post_norm[l:l + 1]))
        dh_next, g_mix, stats, zero = _mixer_bwd(d_mo, sv["h1"], w, sv["mix"], rel_bias, sinks[l], bidx, stats,
                                                 functools.partial(emit, l))
        small[l] = dict(ffn_post_norm=dg_fpost, ffn_pre_norm=dg_fpre, attn_post_norm=dg_apost,
                        sinks=stats[N_A:, 1, 0], conv_b=g_ffn["conv_b"], b_gate=g_mix["b_gate"], conv_w=g_ffn["conv_w"])
    grad_x, _, dg_pre0, _ = _norm_bwd("pre_norm_first_bwd", dres, (saved[0]["x_in"], attn_pre_norm[0:1] + zero, dh_next),
                                      None)
    small[0]["attn_pre_norm"] = dg_pre0
    return loss_row, grad_x, small, stats


def _reduce_and_update(x_shape, names, weights, mom1, mom2, chip, core, pending, loss_row, grad_x, small, stats):
    delta, new_m, new_v, grads = {}, {}, {}, {}

    def update(n, g):
        grads[n], delta[n], new_m[n], new_v[n] = _adamw("adamw_" + n, _shard_view(n, weights[n]), g,
                                                        _shard_view(n, mom1[n]), _shard_view(n, mom2[n]))

    small_vals = {n: jnp.stack([small[l][n].reshape(weights[n].shape[1:]) for l in range(DEPTH)])
                  for n in ("attn_pre_norm", "attn_post_norm", "ffn_pre_norm", "ffn_post_norm", "conv_b", "sinks")}
    small_vals["b_gate"] = jnp.stack([small[l]["b_gate"] for l in range(DEPTH)])
    small_vals["conv_w"] = jnp.stack([small[l]["conv_w"] for l in range(DEPTH)])
    small_vals["rel_bias"] = stats[:, 0, :NUM_BUCKETS].T
    small_vals["loss"] = loss_row[0, :1]
    shapes = {n: v.shape for n, v in small_vals.items()}
    small_send, small_recv, packed, small_land, started = _small_start(_pack_small(small_vals))

    summed = {}

    def finish(which, after):
        for l, group, send, recv, gs, lands in pending:
            if (group == ("w_in",)) == which:
                gs, lands = _reduce_wait("reduce_wait_%d_%s" % (l, group[0]), send, recv, gs, lands, after)
                for n, g, land in zip(group, gs, lands):
                    summed[n] = _reduce_sum("reduce_sum_%d_%s" % (l, n), g, land, l, summed.get(n), chip, core)

    finish(False, started)
    early = [n for n in BIG if n != "w_in"]
    for n, g in zip(early, _join_halves("join_halves", [summed[n] for n in early])):
        update(n, g)
    finish(True, delta[early[-1]])
    update("w_in", _join_halves("join_halves_w_in", [summed["w_in"]])[0])

    packed, small_land = _small_wait(small_send, small_recv, packed, small_land, delta["w_in"])
    small_land = lax.dynamic_update_slice(small_land, packed[None], (2 * chip + core, jnp.int32(0), jnp.int32(0)))
    reduced = _unpack_small(_small_sum(small_land), shapes)
    reduced["b_gate"] = lax.dynamic_slice_in_dim(reduced["b_gate"], chip * (D // N_CHIPS), D // N_CHIPS, axis=2)
    reduced["conv_w"] = lax.dynamic_slice_in_dim(reduced["conv_w"], chip * (2 * D_FF // N_CHIPS), 2 * D_FF // N_CHIPS, axis=2)
    for n in names:
        if n not in grads:
            update(n, reduced[n].reshape(weights[n].shape))
    for out in (grads, delta, new_m, new_v):
        out["w_in"] = _shard_view("w_in", out["w_in"])

    loss = reduced["loss"].reshape(())
    return (loss, grad_x.reshape(x_shape), *[grads[n] for n in names], *[delta[n] for n in names],
            *[new_m[n] for n in names], *[new_v[n] for n in names])
```

```python
import functools
import math

import numpy as np
import jax
import jax.numpy as jnp
from jax import lax
from jax.experimental import pallas as pl
from jax.experimental.pallas import tpu as pltpu

F32 = jnp.float32
BF16 = jnp.bfloat16

S = 2048
D = 1024
DEPTH = 2
HD = 64
BLK = 128
NQB = S // BLK
A_GROUPS = ((128, 1), (512, 4), (2048, 16))
N_BAND_Q = 20
N_A = 12
NUM_BUCKETS = 32
MAX_DISTANCE = 2048
D_FF = 4096
IN_COLS = 6912
IN_SHARD = IN_COLS // 4
OFF_GATE = 3840
EPS = 1e-6
SCALE = HD ** -0.5
NEG = -1e30
N_CHIPS = 4
N_DEV = 8

ADAM_LR = 0.001
ADAM_B1 = 0.9
ADAM_B2 = 0.999
ADAM_EPS = 1e-08
ADAM_WD = 0.01
ADAM_STEP = 10

VMEM_LIMIT = 56 * 1024 * 1024

NN = (((1,), (0,)), ((), ()))
NT = (((1,), (1,)), ((), ()))
TN = (((0,), (0,)), ((), ()))

MESH = pl.DeviceIdType.MESH
ANY = pl.BlockSpec(memory_space=pl.ANY)


def _dot(a, b, dims):
    return lax.dot_general(a, b, dims, preferred_element_type=F32)


def _params(sem):
    return pltpu.CompilerParams(dimension_semantics=sem, vmem_limit_bytes=VMEM_LIMIT)


def _matmul(name, a, b, out_shape, out_dtype, grid, a_spec, b_spec, o_spec, dims, acc_shape):
    nk = grid[-1]

    def body(a_ref, b_ref, o_ref, *scratch):
        part = _dot(a_ref[...].astype(BF16), b_ref[...].astype(BF16), dims)
        if nk == 1:
            o_ref[...] = part.astype(o_ref.dtype)
            return
        acc_ref, = scratch
        k = pl.program_id(len(grid) - 1)

        @pl.when(k == 0)
        def _():
            acc_ref[...] = part

        @pl.when(k > 0)
        def _():
            acc_ref[...] += part

        @pl.when(k == nk - 1)
        def _():
            o_ref[...] = acc_ref[...].astype(o_ref.dtype)

    scratch = [] if nk == 1 else [pltpu.VMEM(acc_shape, F32)]
    sem = ("parallel",) * (len(grid) - 1) + ("arbitrary",)
    return pl.pallas_call(
        body, name=name, grid=grid, in_specs=[a_spec, b_spec], out_specs=o_spec,
        out_shape=jax.ShapeDtypeStruct(out_shape, out_dtype), scratch_shapes=scratch,
        compiler_params=_params(sem))(a, b)


FULL_K = 8192


def _mm_tn_sharded(name, a, b, row_sharded, tm=512, tn=512, tk=FULL_K):
    k, m = a.shape
    n = b.shape[1]
    m4, n4 = (m // N_CHIPS, n) if row_sharded else (m, n // N_CHIPS)
    tm, tn, tk = min(tm, m4), min(tn, n4), min(tk, k)
    per_m, per_n = m4 // tm, n4 // tn
    if row_sharded:
        o_map = lambda i, j, l: (i // per_m, i % per_m, j)
    else:
        o_map = lambda i, j, l: (j // per_n, i, j % per_n)
    return _matmul(name, a, b, (N_CHIPS, m4, n4), BF16, (m // tm, n // tn, k // tk),
                   pl.BlockSpec((tk, tm), lambda i, j, l: (l, i)),
                   pl.BlockSpec((tk, tn), lambda i, j, l: (l, j)),
                   pl.BlockSpec((None, tm, tn), o_map), TN, (tm, tn))


def _mm_nn(name, a, b, out_dtype, tm=512, tn=512, tk=FULL_K):
    m, k = a.shape
    n = b.size // k
    tm, tn, tk = min(tm, m), min(tn, b.shape[-1]), min(tk, k)
    per_shard = b.shape[-1] // tn
    if b.ndim == 2:
        b_spec = pl.BlockSpec((tk, tn), lambda i, j, l: (l, j))
    else:
        b_spec = pl.BlockSpec((None, tk, tn), lambda i, j, l: (j // per_shard, l, j % per_shard))
    return _matmul(name, a, b, (m, n), out_dtype, (m // tm, n // tn, k // tk),
                   pl.BlockSpec((tm, tk), lambda i, j, l: (i, l)), b_spec,
                   pl.BlockSpec((tm, tn), lambda i, j, l: (i, j)), NN, (tm, tn))


def _mm_nt(name, a, b, out_dtype, tm=512, tn=512, tk=FULL_K):
    m, k = a.shape
    n = b.shape[-2]
    tm, tn, tk = min(tm, m), min(tn, n), min(tk, b.shape[-1])
    per_shard = b.shape[-1] // tk
    if b.ndim == 2:
        b_spec = pl.BlockSpec((tn, tk), lambda i, j, l: (j, l))
    else:
        b_spec = pl.BlockSpec((None, tn, tk), lambda i, j, l: (l // per_shard, j, l % per_shard))
    return _matmul(name, a, b, (m, n), out_dtype, (m // tm, n // tn, k // tk),
                   pl.BlockSpec((tm, tk), lambda i, j, l: (i, l)), b_spec,
                   pl.BlockSpec((tm, tn), lambda i, j, l: (i, j)), NT, (tm, tn))


def _mm_tn(name, a, b, out_dtype, tm=512, tn=512, tk=FULL_K):
    k, m = a.shape
    n = b.shape[1]
    tm, tn, tk = min(tm, m), min(tn, n), min(tk, k)
    return _matmul(name, a, b, (m, n), out_dtype, (m // tm, n // tn, k // tk),
                   pl.BlockSpec((tk, tm), lambda i, j, l: (l, i)),
                   pl.BlockSpec((tk, tn), lambda i, j, l: (l, j)),
                   pl.BlockSpec((tm, tn), lambda i, j, l: (i, j)), TN, (tm, tn))


TR = 512


def _row_spec(width=D):
    return pl.BlockSpec((TR, width), lambda i: (i, 0))


def _vec_spec(width=D):
    return pl.BlockSpec((1, width), lambda i: (0, 0))


def _rms(x, g):
    r = lax.rsqrt(jnp.mean(x * x, axis=-1, keepdims=True) + EPS)
    return x * r * g


def _rms_fwd(name, x, g):
    def body(x_ref, g_ref, h_ref):
        h_ref[...] = _rms(x_ref[...], g_ref[...]).astype(BF16)

    return pl.pallas_call(
        body, name=name, grid=(S // TR,), in_specs=[_row_spec(), _vec_spec()], out_specs=_row_spec(),
        out_shape=jax.ShapeDtypeStruct((S, D), BF16), compiler_params=_params(("parallel",)))(x, g)


def _post_pre_fwd(name, x, y, g_post, g_pre):
    has_pre = g_pre is not None

    def body(*refs):
        if has_pre:
            x_ref, y_ref, gp_ref, gn_ref, xn_ref, h_ref = refs
        else:
            x_ref, y_ref, gp_ref, xn_ref = refs
        xn = x_ref[...] + _rms(y_ref[...], gp_ref[...])
        xn_ref[...] = xn
        if has_pre:
            h_ref[...] = _rms(xn, gn_ref[...]).astype(BF16)

    ins = [x, y, g_post] + ([g_pre] if has_pre else [])
    in_specs = [_row_spec(), _row_spec(), _vec_spec()] + ([_vec_spec()] if has_pre else [])
    out_shape = [jax.ShapeDtypeStruct((S, D), F32)] + ([jax.ShapeDtypeStruct((S, D), BF16)] if has_pre else [])
    out_specs = [_row_spec()] + ([_row_spec()] if has_pre else [])
    out = pl.pallas_call(
        body, name=name, grid=(S // TR,), in_specs=in_specs, out_specs=out_specs, out_shape=out_shape,
        compiler_params=_params(("parallel",)))(*ins)
    return out if has_pre else (out[0], None)


def _rms_bwd_math(x, g, dy):
    r = lax.rsqrt(jnp.mean(x * x, axis=-1, keepdims=True) + EPS)
    n = x * r
    dn = dy * g
    dx = r * (dn - n * jnp.mean(dn * n, axis=-1, keepdims=True))
    return dx, jnp.sum(dy * n, axis=0, keepdims=True)


def _norm_bwd(name, dres, pre=None, post=None):
    has_pre, has_post = pre is not None, post is not None

    def body(*refs):
        refs = list(refs)
        dres_ref = refs.pop(0)
        if has_pre:
            xn_ref, gn_ref, dh_ref = refs[:3]
            refs = refs[3:]
        if has_post:
            y_ref, gp_ref = refs[:2]
            refs = refs[2:]
        dxn_ref = refs.pop(0)
        dy_ref = refs.pop(0) if has_post else None
        dgn_ref = refs.pop(0) if has_pre else None
        dgp_ref = refs.pop(0) if has_post else None
        first = pl.program_id(0) == 0
        dxn = dres_ref[...]
        if has_pre:
            dx, dg = _rms_bwd_math(xn_ref[...], gn_ref[...], dh_ref[...])
            dxn = dxn + dx

            @pl.when(first)
            def _():
                dgn_ref[...] = dg

            @pl.when(jnp.logical_not(first))
            def _():
                dgn_ref[...] += dg
        dxn_ref[...] = dxn
        if has_post:
            dy, dg = _rms_bwd_math(y_ref[...], gp_ref[...], dxn)
            dy_ref[...] = dy.astype(BF16)

            @pl.when(first)
            def _():
                dgp_ref[...] = dg

            @pl.when(jnp.logical_not(first))
            def _():
                dgp_ref[...] += dg

    ins, in_specs = [dres], [_row_spec()]
    if has_pre:
        ins += list(pre)
        in_specs += [_row_spec(), _vec_spec(), _row_spec()]
    if has_post:
        ins += list(post)
        in_specs += [_row_spec(), _vec_spec()]
    out_shape, out_specs = [jax.ShapeDtypeStruct((S, D), F32)], [_row_spec()]
    if has_post:
        out_shape.append(jax.ShapeDtypeStruct((S, D), BF16))
        out_specs.append(_row_spec())
    for _ in range(int(has_pre) + int(has_post)):
        out_shape.append(jax.ShapeDtypeStruct((1, D), F32))
        out_specs.append(_vec_spec())
    out = list(pl.pallas_call(
        body, name=name, grid=(S // TR,), in_specs=in_specs, out_specs=out_specs, out_shape=out_shape,
        compiler_params=_params(("arbitrary",)))(*ins))
    dxn = out.pop(0)
    dy = out.pop(0) if has_post else None
    dgn = out.pop(0) if has_pre else None
    dgp = out.pop(0) if has_post else None
    return dxn, dy, dgn, dgp


def _loss_kernel(y, target):
    def body(y_ref, t_ref, loss_ref, dy_ref):
        e = y_ref[...] - t_ref[...]
        dy_ref[...] = e * (1.0 / D)
        part = jnp.zeros((1, 128), F32) + 0.5 * jnp.sum(jnp.mean(e * e, axis=-1, keepdims=True))

        @pl.when(pl.program_id(0) == 0)
        def _():
            loss_ref[...] = part

        @pl.when(pl.program_id(0) > 0)
        def _():
            loss_ref[...] += part

    return pl.pallas_call(
        body, name="loss", grid=(S // TR,), in_specs=[_row_spec(), _row_spec()],
        out_specs=[_vec_spec(128), _row_spec()],
        out_shape=[jax.ShapeDtypeStruct((1, 128), F32), jax.ShapeDtypeStruct((S, D), F32)],
        compiler_params=_params(("arbitrary",)))(y, target)


def _t5_bucket_np(dist):
    max_exact = NUM_BUCKETS // 2
    nf = np.maximum(dist, 1).astype(np.float32)
    large = max_exact + (np.log(nf / max_exact) / np.float32(math.log(MAX_DISTANCE / max_exact))
                         * (NUM_BUCKETS - max_exact)).astype(np.int32)
    large = np.minimum(large, NUM_BUCKETS - 1)
    return np.where(dist < max_exact, dist, large).astype(np.int32)


def _bucket_maps():
    a = np.arange(BLK)[:, None]
    b = np.arange(2 * BLK)[None, :]
    dist = np.maximum(a + BLK - b, 0)
    maps = [_t5_bucket_np(dist * d) for _, d in A_GROUPS] + [_t5_bucket_np(dist)]
    return np.stack(maps).astype(np.int32)


def _pair_spec(col0):
    return pl.BlockSpec((S, 128), lambda p: (0, col0 + p))


def _band_rows(i, d):
    nb = S // d // BLK
    r, b = i // nb, i % nb
    cur = pl.ds(b * BLK * d + r, BLK, stride=d)
    prev = pl.ds(jnp.maximum(b - 1, 0) * BLK * d + r, BLK, stride=d)
    return cur, prev, jnp.minimum(b, 1)


def _band_bias(tab_ref, bi, h):
    bias = jnp.zeros((BLK, 2 * BLK), F32)
    for kk in range(NUM_BUCKETS):
        bias = jnp.where(bi == kk, tab_ref[kk, h], bias)
    return bias


def _lane_lo(rows=BLK):
    return lax.broadcasted_iota(jnp.int32, (rows, 128), 1) < HD


def _per_head(x, lo):
    return (jnp.sum(jnp.where(lo, x, 0.0), axis=1, keepdims=True) * (1.0 / HD),
            jnp.sum(jnp.where(lo, 0.0, x), axis=1, keepdims=True) * (1.0 / HD))


def _band_fill(bias_ref, tab_ref, bi, head, maxd):
    a = lax.broadcasted_iota(jnp.int32, (BLK, 2 * BLK), 0)
    c = lax.broadcasted_iota(jnp.int32, (BLK, 2 * BLK), 1)
    dist = a + BLK - c
    in_band = jnp.logical_and(dist >= 0, dist <= maxd)
    for h in range(2):
        bias = jnp.where(in_band, _band_bias(tab_ref, bi, head + h), NEG)
        bias_ref[1, h * BLK:(h + 1) * BLK, :] = bias
        bias_ref[0, h * BLK:(h + 1) * BLK, :] = jnp.where(c >= BLK, bias, NEG)


def _stack_heads(x, lo, dtype=BF16):
    return jnp.concatenate([jnp.where(lo, x, 0.0), jnp.where(lo, 0.0, x)], axis=0).astype(dtype)


def _unstack_heads(x, lo):
    n = x.shape[0] // 2
    return jnp.where(lo, x[:n], x[n:])


def _stack_rows(ref, prev, cur):
    return jnp.concatenate([ref[prev, :], ref[cur, :]], axis=0).astype(BF16)


PAIRS_PER_KV = 2


def _kv_specs(kc, vc, kv_shared):
    if not kv_shared:
        return [_pair_spec(kc), _pair_spec(vc)], []
    shared = [pl.BlockSpec((S, 128), functools.partial(lambda p, c: (0, c), c=c)) for c in (kc, vc)]
    return shared, [pltpu.VMEM((S, 128), F32)] * 2


def _expand_kv(dst_ref, src_ref, pair):
    x = src_ref[...]
    own = lax.broadcasted_iota(jnp.int32, (S, 128), 1) // HD == pair // PAIRS_PER_KV
    dst_ref[...] = jnp.where(own, x, pltpu.roll(x, HD, 1))


def _fold_kv(out_ref, acc_ref, pair):
    x = acc_ref[...]
    own = lax.broadcasted_iota(jnp.int32, (S, 128), 1) // HD == pair // PAIRS_PER_KV
    part = jnp.where(own, x + pltpu.roll(x, HD, 1), 0.0)

    @pl.when(pair == 0)
    def _():
        out_ref[...] = part

    @pl.when(pair > 0)
    def _():
        out_ref[...] += part


def _blocks_of_group(group, dils, block):
    def run(d):
        lax.fori_loop(0, NQB, functools.partial(block, d), 0, unroll=2)

    if len(dils) == 1:
        run(dils[0])
        return
    for g, d in enumerate(dils):
        pl.when(group == g)(functools.partial(run, d))


def _band_fwd(name, dils, n_pairs, maxd, head0, srcs, bidx_g, tab, sinks, kv_shared=False):
    (qa, qc), (ka, kc), (va, vc) = srcs
    per_group = n_pairs // len(dils)
    out_spec = _pair_spec(0)
    smem = pl.BlockSpec(memory_space=pltpu.SMEM)
    full = pl.BlockSpec((len(dils), BLK, 2 * BLK), lambda p: (0, 0, 0))

    kv_specs, kv_scratch = _kv_specs(kc, vc, kv_shared)

    def body(tab_ref, sink_ref, q_ref, k_ref, v_ref, bidx_ref, o_ref, lse_ref, bias_ref, *expanded):
        p = pl.program_id(0)
        if kv_shared:
            _expand_kv(expanded[0], k_ref, p)
            _expand_kv(expanded[1], v_ref, p)
            k_ref, v_ref = expanded
        _band_fill(bias_ref, tab_ref, bidx_ref[p // per_group], head0 + 2 * p, maxd)
        lo = _lane_lo()
        sink = jnp.where(lax.broadcasted_iota(jnp.int32, (2 * BLK, 1), 0) < BLK, sink_ref[2 * p], sink_ref[2 * p + 1])

        def block(d, i, carry):
            cur, prev, has_prev = _band_rows(i, d)
            qs = _stack_heads(q_ref[cur, :] * SCALE, lo)
            ks, vs = _stack_rows(k_ref, prev, cur), _stack_rows(v_ref, prev, cur)
            s = _dot(qs, ks, NT) + bias_ref[has_prev]
            m = jnp.max(s, axis=1, keepdims=True)
            pr = jnp.exp(s - m)
            l = jnp.sum(pr, axis=1, keepdims=True)
            num = _dot(pr.astype(BF16), vs, NN)
            lse = m + jnp.log(l)
            sig = 1.0 / (1.0 + jnp.exp(sink - lse))
            o_ref[cur, :] = _unstack_heads(num * (sig / l), lo)
            lse_ref[cur, :] = _unstack_heads(lse + jnp.zeros((2 * BLK, 128), F32), lo)
            return carry

        _blocks_of_group(p // per_group, dils, block)

    shape = jax.ShapeDtypeStruct((S, n_pairs * 128), F32)
    return pl.pallas_call(
        body, name=name, grid=(n_pairs,),
        in_specs=[smem, smem, _pair_spec(qc)] + kv_specs + [full],
        out_specs=[out_spec, out_spec], out_shape=[shape, shape],
        scratch_shapes=[pltpu.VMEM((2, 2 * BLK, 2 * BLK), F32)] + kv_scratch,
        compiler_params=_params(("parallel",)))(tab, sinks, qa, ka, va, bidx_g)


def _band_bwd(name, dils, n_pairs, maxd, head0, srcs, bidx_g, tab, sinks, o, lse, do, stats_in, kv_shared=False):
    (qa, qc), (ka, kc), (va, vc) = srcs
    per_group = n_pairs // len(dils)
    pair = _pair_spec(0)
    shared = pl.BlockSpec((S, 128), lambda p: (0, p % per_group))
    smem = pl.BlockSpec(memory_space=pltpu.SMEM)
    full = pl.BlockSpec((len(dils), BLK, 2 * BLK), lambda p: (0, 0, 0))
    stat_spec = pl.BlockSpec((2, 8, 128), lambda p: (p, 0, 0))
    kv_specs, kv_scratch = _kv_specs(kc, vc, kv_shared)

    def body(tab_ref, sink_ref, q_ref, k_ref, v_ref, bidx_ref, o_ref, lse_ref, do_ref, sin_ref,
             dq_ref, dk_ref, dv_ref, stat_ref, bias_ref, dsacc_ref, sk_ref, *expanded):
        p = pl.program_id(0)
        if kv_shared:
            _expand_kv(expanded[0], k_ref, p)
            _expand_kv(expanded[1], v_ref, p)
            k_ref, v_ref = expanded[:2]
            dk_out, dv_out, dk_ref, dv_ref = dk_ref, dv_ref, expanded[2], expanded[3]
        _band_fill(bias_ref, tab_ref, bidx_ref[p // per_group], head0 + 2 * p, maxd)
        dsacc_ref[...] = jnp.zeros_like(dsacc_ref)
        sk_ref[...] = jnp.zeros_like(sk_ref)
        dk_ref[...] = jnp.zeros_like(dk_ref)
        dv_ref[...] = jnp.zeros_like(dv_ref)
        lo = _lane_lo()
        head1 = lax.broadcasted_iota(jnp.int32, (2 * BLK, 1), 0) >= BLK
        sink = jnp.where(head1, sink_ref[2 * p + 1], sink_ref[2 * p])

        def block(d, i, carry):
            cur, prev, has_prev = _band_rows(i, d)
            qs = _stack_heads(q_ref[cur, :] * SCALE, lo)
            ks, vs = _stack_rows(k_ref, prev, cur), _stack_rows(v_ref, prev, cur)
            do = do_ref[cur, :]
            dos = _stack_heads(do, lo, F32)
            lse = jnp.concatenate(_per_head(lse_ref[cur, :], lo), axis=0)
            prod = do * o_ref[cur, :]
            delta = jnp.concatenate([jnp.sum(jnp.where(lo, prod, 0.0), axis=1, keepdims=True),
                                     jnp.sum(jnp.where(lo, 0.0, prod), axis=1, keepdims=True)], axis=0)
            sig = 1.0 / (1.0 + jnp.exp(sink - lse))
            pr = jnp.exp(_dot(qs, ks, NT) + bias_ref[has_prev] - lse)
            ds = pr * (sig * (_dot(dos.astype(BF16), vs, NT) - delta))
            dsb = ds.astype(BF16)
            dq_ref[cur, :] = SCALE * _unstack_heads(_dot(dsb, ks, NN), lo)
            dk = _dot(dsb, qs, TN)
            dv = _dot(pr.astype(BF16), (sig * dos).astype(BF16), TN)
            dk_ref[prev, :] += dk[:BLK]
            dk_ref[cur, :] += dk[BLK:]
            dv_ref[prev, :] += dv[:BLK]
            dv_ref[cur, :] += dv[BLK:]
            dsacc_ref[...] += ds
            sink_grad = -delta * (1.0 - sig)
            for h in range(2):
                sk_ref[h] += jnp.zeros((8, 128), F32) + jnp.sum(sink_grad[h * BLK:(h + 1) * BLK])
            return carry

        _blocks_of_group(p // per_group, dils, block)
        if kv_shared:
            _fold_kv(dk_out, dk_ref, p)
            _fold_kv(dv_out, dv_ref, p)

        bi = bidx_ref[p // per_group]
        lane = lax.broadcasted_iota(jnp.int32, (8, 128), 1)
        sub = lax.broadcasted_iota(jnp.int32, (8, 128), 0)
        for h in range(2):
            acc = dsacc_ref[h * BLK:(h + 1) * BLK, :]
            row = jnp.where(jnp.logical_and(sub == 1, lane == 0), sk_ref[h], 0.0)
            for kk in range(NUM_BUCKETS):
                tot = jnp.sum(jnp.where(bi == kk, acc, 0.0))
                row = jnp.where(jnp.logical_and(sub == 0, lane == kk), tot, row)
            stat_ref[h] = row + jnp.where(sub == 0, sin_ref[h], 0.0)

    shape = jax.ShapeDtypeStruct((S, n_pairs * 128), F32)
    kv_spec = pl.BlockSpec((S, 128), lambda p: (0, 0)) if kv_shared else pair
    kv_shape = jax.ShapeDtypeStruct((S, 128), F32) if kv_shared else shape
    return pl.pallas_call(
        body, name=name, grid=(n_pairs,),
        in_specs=[smem, smem, _pair_spec(qc)] + kv_specs + [full, shared, shared, shared, stat_spec],
        out_specs=[pair, kv_spec, kv_spec, stat_spec],
        out_shape=[shape, kv_shape, kv_shape, jax.ShapeDtypeStruct((2 * n_pairs, 8, 128), F32)],
        scratch_shapes=[pltpu.VMEM((2, 2 * BLK, 2 * BLK), F32), pltpu.VMEM((2 * BLK, 2 * BLK), F32),
                        pltpu.VMEM((2, 8, 128), F32)] + kv_scratch * 2,
        compiler_params=_params(("arbitrary" if kv_shared else "parallel",)))(
            tab, sinks, qa, ka, va, bidx_g, o, lse, do, stats_in)


def _comb_fwd(o_g, lse_g):
    def body(o0, o1, o2, l0, l1, l2, out_ref, outb_ref, lse_ref):
        a0, a1, a2 = l0[...], l1[...], l2[...]
        m = jnp.maximum(jnp.maximum(a0, a1), a2)
        e0, e1, e2 = jnp.exp(a0 - m), jnp.exp(a1 - m), jnp.exp(a2 - m)
        tot = e0 + e1 + e2
        out = (e0 * o0[...] + e1 * o1[...] + e2 * o2[...]) / tot
        out_ref[...] = out
        outb_ref[...] = out.astype(BF16)
        lse_ref[...] = m + jnp.log(tot)

    spec = _row_spec(4 * HD)
    groups = [pl.BlockSpec((TR, 4 * HD), functools.partial(lambda i, g: (i, g), g=g)) for g in range(len(A_GROUPS))]
    f32 = jax.ShapeDtypeStruct((S, 4 * HD), F32)
    return pl.pallas_call(
        body, name="comb_fwd", grid=(S // TR,), in_specs=groups + groups, out_specs=[spec] * 3,
        out_shape=[f32, jax.ShapeDtypeStruct((S, 4 * HD), BF16), f32],
        compiler_params=_params(("parallel",)))(o_g, o_g, o_g, lse_g, lse_g, lse_g)


def _split2(x):
    hi = x.astype(BF16)
    return hi, (x - hi.astype(F32)).astype(BF16)


KB = 2 * BLK
SBQ = 2 * BLK


def _tri_sum(x, tri):
    hi, lo = _split2(x)
    both = _dot(jnp.concatenate([hi, lo], axis=0), tri, NN)
    return both[:x.shape[0]] + both[x.shape[0]:]


def _tri(strict_upper):
    r = lax.broadcasted_iota(jnp.int32, (KB, KB), 0)
    c = lax.broadcasted_iota(jnp.int32, (KB, KB), 1)
    return jnp.where(r > c if strict_upper else r < c, 1.0, 0.0).astype(BF16)


def _sb_terms(qs, kj, before):
    z = _dot(qs, kj, NT)
    lsp = jnp.minimum(z, 0.0) - jnp.log(1.0 + jnp.exp(-jnp.abs(z)))
    return lsp, _sb_keep(before, lsp - z)


def _sb_keep(before, x):
    return x if before is None else jnp.where(before, x, 0.0)


def _sb_before(i, m):
    t = (lax.broadcasted_iota(jnp.int32, (2 * SBQ, KB), 0) & (SBQ - 1)) + i * SBQ
    s = lax.broadcasted_iota(jnp.int32, (2 * SBQ, KB), 1) + m * KB
    return s < t


C_COL = 3072 // 128


def _sb_fwd(proj):
    blk = lambda off: pl.BlockSpec((SBQ, 128), lambda p, i: (i, off + p))
    col = lambda off: pl.BlockSpec((S, 128), lambda p, i: (0, off + p))
    out = pl.BlockSpec((SBQ, 128), lambda p, i: (i, p))

    def body(q_ref, k_ref, v_ref, o_ref, ob_ref, tot_ref):
        i = pl.program_id(1)
        lo = _lane_lo(SBQ)
        qs = _stack_heads(q_ref[...] * SCALE, lo)
        suffix = _tri(True)

        def step(n, carry, diagonal=False):
            acc, rest = carry
            m = i - n
            rows = pl.ds(pl.multiple_of(m * KB, KB), KB)
            kj, vj = k_ref[rows, :].astype(BF16), v_ref[rows, :].astype(BF16)
            before = _sb_before(i, m) if diagonal else None
            lsp, lk = _sb_terms(qs, kj, before)
            w = _sb_keep(before, jnp.exp(lsp + _tri_sum(lk, suffix) + rest))
            return acc + _dot(w.astype(BF16), vj, NN), rest + jnp.sum(lk, axis=1, keepdims=True)

        first = step(0, (jnp.zeros((2 * SBQ, 128), F32), jnp.zeros((2 * SBQ, 1), F32)), diagonal=True)
        acc, rest = lax.fori_loop(1, i + 1, step, first)
        o = _unstack_heads(acc, lo)
        o_ref[...] = o
        ob_ref[...] = o.astype(BF16)
        tot_ref[...] = _unstack_heads(rest + jnp.zeros((2 * SBQ, 128), F32), lo)

    f32 = jax.ShapeDtypeStruct((S, 4 * HD), F32)
    return pl.pallas_call(
        body, name="sb_fwd", grid=(2, S // SBQ), in_specs=[blk(C_COL), col(C_COL + 2), col(C_COL + 4)],
        out_specs=[out, out, out], out_shape=[f32, jax.ShapeDtypeStruct((S, 4 * HD), BF16), f32],
        compiler_params=_params(("parallel", "arbitrary")))(proj, proj, proj)


def _sb_bwd(proj, tot, do):
    blk = lambda off: pl.BlockSpec((SBQ, 128), lambda p, i: (i, off + p))
    col = lambda off: pl.BlockSpec((S, 128), lambda p, i: (0, off + p))

    def body(q_ref, k_ref, v_ref, tot_ref, do_ref, dq_ref, dk_ref, dv_ref):
        i = pl.program_id(1)

        @pl.when(i == 0)
        def _():
            dk_ref[...] = jnp.zeros_like(dk_ref)
            dv_ref[...] = jnp.zeros_like(dv_ref)

        lo = _lane_lo(SBQ)
        qs = _stack_heads(q_ref[...] * SCALE, lo)
        dos = _stack_heads(do_ref[...], lo)
        tots = jnp.concatenate(_per_head(tot_ref[...], lo), axis=0)
        prefix = _tri(False)

        def step(m, carry, diagonal=False):
            dq, keep_left, g_left = carry
            rows = pl.ds(pl.multiple_of(m * KB, KB), KB)
            kj, vj = k_ref[rows, :].astype(BF16), v_ref[rows, :].astype(BF16)
            before = _sb_before(i, m) if diagonal else None
            lsp, lk = _sb_terms(qs, kj, before)
            log_rest = tots - keep_left - lk - _tri_sum(lk, prefix)
            w = _sb_keep(before, jnp.exp(lsp + log_rest))
            g = w * _dot(dos, vj, NT)
            g_before = g_left + _dot(g.astype(BF16), prefix, NN)
            beta = jnp.exp(lsp)
            dz = _sb_keep(before, g * (1.0 - beta) - g_before * beta).astype(BF16)
            dk_ref[rows, :] += _dot(dz, qs, TN)
            dv_ref[rows, :] += _dot(w.astype(BF16), dos, TN)
            return (dq + _dot(dz, kj, NN), keep_left + jnp.sum(lk, axis=1, keepdims=True),
                    g_left + jnp.sum(g, axis=1, keepdims=True))

        zero = (jnp.zeros((2 * SBQ, 128), F32), jnp.zeros((2 * SBQ, 1), F32), jnp.zeros((2 * SBQ, 1), F32))
        dq, _, _ = step(i, lax.fori_loop(0, i, step, zero), diagonal=True)
        dq_ref[...] = SCALE * _unstack_heads(dq, lo)

    out_blk = pl.BlockSpec((SBQ, 128), lambda p, i: (i, p))
    out_col = pl.BlockSpec((S, 128), lambda p, i: (0, p))
    f32 = jax.ShapeDtypeStruct((S, 4 * HD), F32)
    return pl.pallas_call(
        body, name="sb_bwd", grid=(2, S // SBQ),
        in_specs=[blk(C_COL), col(C_COL + 2), col(C_COL + 4), out_blk, out_blk],
        out_specs=[out_blk, out_col, out_col], out_shape=[f32, f32, f32],
        compiler_params=_params(("arbitrary", "arbitrary")))(proj, proj, proj, tot, do)


TG = 256
TGR = 1024
GATE_BLK0 = OFF_GATE // TG


def _gate_specs():
    grid = (D // TG, S // TGR)
    p_specs = [pl.BlockSpec((TGR, TG), functools.partial(lambda c, r, br: (r, GATE_BLK0 + br * (D // TG) + c), br=br))
               for br in range(3)]
    b_spec = pl.BlockSpec((3, TG), lambda c, r: (0, c))
    t_spec = pl.BlockSpec((TGR, TG), lambda c, r: (r, c))
    return grid, p_specs, b_spec, t_spec


def _sigmoid(x):
    return 1.0 / (1.0 + jnp.exp(-x))


def _three_rows(rows):
    sub = lax.broadcasted_iota(jnp.int32, (3, rows[0].shape[1]), 0)
    return jnp.where(sub == 0, rows[0], jnp.where(sub == 1, rows[1], rows[2]))


def _gate_fwd(proj, b_gate, br):
    grid, p_specs, b_spec, t_spec = _gate_specs()

    def body(p0, p1, p2, b_ref, r0, r1, r2, out_ref):
        acc = jnp.zeros((TGR, TG), F32)
        for n, (p, r) in enumerate(((p0, r0), (p1, r1), (p2, r2))):
            acc += _sigmoid(p[...] + b_ref[n:n + 1, :]) * r[...]
        out_ref[...] = acc.astype(BF16)

    return pl.pallas_call(
        body, name="gate_fwd", grid=grid, in_specs=p_specs + [b_spec] + [t_spec] * 3, out_specs=t_spec,
        out_shape=jax.ShapeDtypeStruct((S, D), BF16),
        compiler_params=_params(("parallel", "parallel")))(proj, proj, proj, b_gate, *br)


def _gate_bwd(proj, b_gate, br, dmerged):
    grid, p_specs, b_spec, t_spec = _gate_specs()

    def body(p0, p1, p2, b_ref, r0, r1, r2, dm_ref, e0, e1, e2, g0, g1, g2, db_ref):
        dm = dm_ref[...]
        rows = []
        for n, (p, r, e_ref, dg_ref) in enumerate(((p0, r0, e0, g0), (p1, r1, e1, g1), (p2, r2, e2, g2))):
            g = _sigmoid(p[...] + b_ref[n:n + 1, :])
            e_ref[...] = (dm * g).astype(BF16)
            dpre = dm * r[...] * g * (1.0 - g)
            dg_ref[...] = dpre.astype(BF16)
            rows.append(jnp.sum(dpre, axis=0, keepdims=True))
        db = _three_rows(rows)

        @pl.when(pl.program_id(1) == 0)
        def _():
            db_ref[...] = db

        @pl.when(pl.program_id(1) > 0)
        def _():
            db_ref[...] += db

    bf = jax.ShapeDtypeStruct((S, D), BF16)
    out = pl.pallas_call(
        body, name="gate_bwd", grid=grid, in_specs=p_specs + [b_spec] + [t_spec] * 4,
        out_specs=[t_spec] * 6 + [b_spec], out_shape=[bf] * 6 + [jax.ShapeDtypeStruct((3, D), F32)],
        compiler_params=_params(("parallel", "arbitrary")))(proj, proj, proj, b_gate, *br, dmerged)
    return out[:3], out[3:6], out[6]


TC = 256
N_FF_BLK = D_FF // TC
GELU_C = math.sqrt(2.0 / math.pi)


def _shift_down(x, n):
    rows = lax.broadcasted_iota(jnp.int32, x.shape, 0)
    return jnp.where(rows >= n, pltpu.roll(x, n, axis=0), 0.0)


def _shift_up(x, n):
    rows = lax.broadcasted_iota(jnp.int32, x.shape, 0)
    return jnp.where(rows < x.shape[0] - n, pltpu.roll(x, x.shape[0] - n, axis=0), 0.0)


def _conv(u, w, b):
    s1, s2 = _shift_down(u, 1), _shift_down(u, 2)
    return w[2:3, :] * u + w[1:2, :] * s1 + w[0:1, :] * s2 + b, s1, s2


def _gelu_parts(x):
    inner = GELU_C * (x + 0.044715 * x * x * x)
    t = jnp.tanh(inner)
    gelu = 0.5 * x * (1.0 + t)
    dgelu = 0.5 * (1.0 + t) + 0.5 * x * (1.0 - t * t) * GELU_C * (1.0 + 3 * 0.044715 * x * x)
    return gelu, dgelu


def _conv_specs():
    ug = pl.BlockSpec((S, TC), lambda c: (0, c))
    uv = pl.BlockSpec((S, TC), lambda c: (0, N_FF_BLK + c))
    wg = pl.BlockSpec((3, TC), lambda c: (0, c))
    wv = pl.BlockSpec((3, TC), lambda c: (0, N_FF_BLK + c))
    bg = pl.BlockSpec((1, TC), lambda c: (0, c))
    bv = pl.BlockSpec((1, TC), lambda c: (0, N_FF_BLK + c))
    return ug, uv, wg, wv, bg, bv


def _conv_fwd(u, conv_w, conv_b):
    ug, uv, wg, wv, bg, bv = _conv_specs()

    def body(ug_ref, uv_ref, wg_ref, wv_ref, bg_ref, bv_ref, a_ref):
        gc = _conv(ug_ref[...], wg_ref[...], bg_ref[...])[0]
        vc = _conv(uv_ref[...], wv_ref[...], bv_ref[...])[0]
        a_ref[...] = (_gelu_parts(gc)[0] * vc).astype(BF16)

    return pl.pallas_call(
        body, name="conv_fwd", grid=(N_FF_BLK,), in_specs=[ug, uv, wg, wv, bg, bv], out_specs=ug,
        out_shape=jax.ShapeDtypeStruct((S, D_FF), BF16),
        compiler_params=_params(("parallel",)))(u, u, conv_w, conv_w, conv_b, conv_b)


def _conv_bwd(u, conv_w, conv_b, da):
    ug, uv, wg, wv, bg, bv = _conv_specs()

    def back(duc, u, s1, s2, w):
        du = w[2:3, :] * duc + w[1:2, :] * _shift_up(duc, 1) + w[0:1, :] * _shift_up(duc, 2)
        dw = _three_rows([jnp.sum(duc * s2, axis=0, keepdims=True), jnp.sum(duc * s1, axis=0, keepdims=True),
                          jnp.sum(duc * u, axis=0, keepdims=True)])
        return du, dw, jnp.sum(duc, axis=0, keepdims=True)

    def body(ug_ref, uv_ref, wg_ref, wv_ref, bg_ref, bv_ref, da_ref, dug_ref, duv_ref, dwg_ref, dwv_ref, dbg_ref, dbv_ref):
        u_g, u_v = ug_ref[...], uv_ref[...]
        gc, g1, g2 = _conv(u_g, wg_ref[...], bg_ref[...])
        vc, v1, v2 = _conv(u_v, wv_ref[...], bv_ref[...])
        gelu, dgelu = _gelu_parts(gc)
        da = da_ref[...]
        du, dw, db = back(da * vc * dgelu, u_g, g1, g2, wg_ref[...])
        dug_ref[...] = du.astype(BF16)
        dwg_ref[...] = dw
        dbg_ref[...] = db
        du, dw, db = back(da * gelu, u_v, v1, v2, wv_ref[...])
        duv_ref[...] = du.astype(BF16)
        dwv_ref[...] = dw
        dbv_ref[...] = db

    return pl.pallas_call(
        body, name="conv_bwd", grid=(N_FF_BLK,), in_specs=[ug, uv, wg, wv, bg, bv, ug],
        out_specs=[ug, ug, wg, wg, bg, bg],
        out_shape=[jax.ShapeDtypeStruct((S, D_FF), BF16), jax.ShapeDtypeStruct((S, D_FF), BF16),
                   jax.ShapeDtypeStruct((3, D_FF), F32), jax.ShapeDtypeStruct((3, D_FF), F32),
                   jax.ShapeDtypeStruct((1, D_FF), F32), jax.ShapeDtypeStruct((1, D_FF), F32)],
        compiler_params=_params(("parallel",)))(u, u, conv_w, conv_w, conv_b, conv_b, da)


def _adamw(name, w, g, m, v):
    shape = w.shape
    cols = shape[-1]
    flat = [t.reshape(-1, cols) for t in (w, g, m, v)]
    r = flat[0].shape[0]
    tr = min(r, max(8, 2 * 1024 * 1024 // (4 * cols)))

    def body(w_ref, g_ref, m_ref, v_ref, go_ref, d_ref, mo_ref, vo_ref):
        g = g_ref[...]
        go_ref[...] = g
        m = ADAM_B1 * m_ref[...] + (1.0 - ADAM_B1) * g
        v = ADAM_B2 * v_ref[...] + (1.0 - ADAM_B2) * (g * g)
        m_hat = m / (1.0 - ADAM_B1 ** ADAM_STEP)
        v_hat = v / (1.0 - ADAM_B2 ** ADAM_STEP)
        d_ref[...] = -ADAM_LR * (m_hat / (jnp.sqrt(v_hat) + ADAM_EPS) + ADAM_WD * w_ref[...])
        mo_ref[...] = m
        vo_ref[...] = v

    spec = pl.BlockSpec((tr, cols), lambda i: (i, 0))
    outs = pl.pallas_call(
        body, name=name, grid=(pl.cdiv(r, tr),), in_specs=[spec] * 4, out_specs=[spec] * 4,
        out_shape=[jax.ShapeDtypeStruct((r, cols), F32)] * 4, compiler_params=_params(("parallel",)))(*flat)
    return [t.reshape(shape) for t in outs]


def _place():
    x, y, c = lax.axis_index("x"), lax.axis_index("y"), lax.axis_index("c")
    chips = [(1 - x, y), (x, 1 - y), (1 - x, 1 - y)]
    return x, y, c, chips


def _scalars(*vals):
    return jnp.stack([jnp.asarray(v, jnp.int32) for v in vals])


HBM = pl.BlockSpec(memory_space=pltpu.HBM)
SEM = pl.BlockSpec(memory_space=pltpu.SEMAPHORE)
SPLIT_COPY = pltpu.CompilerParams(has_side_effects=pltpu.SideEffectType.DATAFLOW_SIDE_EFFECTING)


def _in_hbm(x):
    return pltpu.with_memory_space_constraint(x, pltpu.HBM)


def _cast_into_slot(name, w, layer, chip):
    _, k, n4 = w.shape
    tr = max(t for t in range(16, 513, 16) if k % t == 0)

    def body(chip_ref, w_ref, o_ref):
        o_ref[...] = w_ref[...].astype(BF16)

    return pl.pallas_call(
        body, name=name,
        grid_spec=pltpu.PrefetchScalarGridSpec(
            num_scalar_prefetch=1, grid=(k // tr,),
            in_specs=[pl.BlockSpec((None, tr, n4), lambda i, s: (layer, i, 0))],
            out_specs=pl.BlockSpec((None, tr, n4), lambda i, s: (s[0], i, 0))),
        out_shape=jax.ShapeDtypeStruct((N_CHIPS, k, n4), BF16),
        compiler_params=_params(("parallel",)))(_scalars(chip), w)


def _gather_copy(buf_ref, k, from_chip, send_sem, recv_sem, chips, c, half=False):
    rows = buf_ref.at[from_chip]
    if half:
        h = buf_ref.shape[1] // 2
        rows = buf_ref.at[from_chip, pl.ds(pl.multiple_of(c * h, h), h)]
    return pltpu.make_async_remote_copy(src_ref=rows, dst_ref=rows, send_sem=send_sem, recv_sem=recv_sem,
                                        device_id=(*chips[k], c), device_id_type=MESH)


def _gather_start(name, bufs, groups, halved=()):
    n, ng = len(bufs), len(groups)
    where = {a: (gi, e) for gi, g in enumerate(groups) for e, a in enumerate(g)}

    def body(*refs):
        ins, sems, token = refs[:n], refs[n:n + 2 * ng], refs[-1]
        x, y, c, chips = _place()
        for a in range(n):
            gi, e = where[a]
            for k in range(3):
                _gather_copy(ins[a], k, 2 * x + y, sems[2 * gi].at[3 * e + k], sems[2 * gi + 1].at[3 * e + k],
                             chips, c, a in halved).start()
        token[...] = jnp.zeros_like(token)

    out_shape = [pltpu.SemaphoreType.DMA((3 * len(g),)) for g in groups for _ in range(2)]
    out_shape += [pltpu.HBM(b.shape, b.dtype) for b in bufs] + [jax.ShapeDtypeStruct((8, 128), F32)]
    out = pl.pallas_call(
        body, name=name, in_specs=[HBM] * n,
        out_specs=[SEM] * (2 * ng) + [HBM] * n + [pl.BlockSpec(memory_space=pltpu.VMEM)], out_shape=out_shape,
        input_output_aliases={a: 2 * ng + a for a in range(n)}, compiler_params=SPLIT_COPY)(*[_in_hbm(b) for b in bufs])
    sems = [(out[2 * gi], out[2 * gi + 1]) for gi in range(ng)]
    return sems, list(out[2 * ng:2 * ng + n]), out[-1]


def _gather_wait(name, bufs, send, recv, after, halved=()):
    n = len(bufs)

    def body(*refs):
        ins, send_sem, recv_sem = refs[:n], refs[n], refs[n + 1]
        x, y, c, chips = _place()
        for e in range(n):
            for k in range(3):
                sems = (send_sem.at[3 * e + k], recv_sem.at[3 * e + k])
                _gather_copy(ins[e], k, 2 * x + y, *sems, chips, c, e in halved).wait_send()
                _gather_copy(ins[e], k, 2 * chips[k][0] + chips[k][1], *sems, chips, c, e in halved).wait_recv()

    return pl.pallas_call(
        body, name=name, in_specs=[HBM] * n + [SEM, SEM, ANY], out_specs=[HBM] * n,
        out_shape=[pltpu.HBM(b.shape, b.dtype) for b in bufs],
        input_output_aliases={a: a for a in range(n)}, compiler_params=SPLIT_COPY)(*bufs, send, recv, after)


def _swap_halves(name, bufs):
    n = len(bufs)

    def body(*refs):
        ins, outs = refs[:n], refs[n:2 * n]
        send_sem, recv_sem = refs[2 * n:]
        x, y, c, chips = _place()

        def piece(ref, k, which):
            h = ref.shape[1] // 2
            return ref.at[2 * chips[k][0] + chips[k][1], pl.ds(pl.multiple_of(which * h, h), h)]

        def copy(a, k, which):
            return pltpu.make_async_remote_copy(
                src_ref=piece(ins[a], k, c), dst_ref=piece(outs[a], k, which), send_sem=send_sem.at[3 * a + k],
                recv_sem=recv_sem.at[3 * a + k], device_id=(x, y, 1 - c), device_id_type=MESH)

        for a in range(n):
            for k in range(3):
                copy(a, k, c).start()
        for a in range(n):
            for k in range(3):
                copy(a, k, c).wait_send()
                copy(a, k, 1 - c).wait_recv()

    return pl.pallas_call(
        body, name=name, in_specs=[ANY] * n, out_specs=[ANY] * n,
        out_shape=[jax.ShapeDtypeStruct(b.shape, b.dtype) for b in bufs],
        input_output_aliases={a: a for a in range(n)},
        scratch_shapes=[pltpu.SemaphoreType.DMA((3 * n,)), pltpu.SemaphoreType.DMA((3 * n,))],
    )(*bufs)


def _reduce_copy(g_ref, land_ref, mask, send_sem, recv_sem, x, y, c, sending):
    px, py, pc = x ^ ((mask >> 2) & 1), y ^ ((mask >> 1) & 1), c ^ (mask & 1)
    half = g_ref.shape[1] // 2
    src = g_ref.at[2 * px + py, pl.ds(pl.multiple_of(pc * half, half), half)]
    dst = land_ref.at[4 * x + 2 * y + c] if sending else land_ref.at[4 * px + 2 * py + pc]
    return pltpu.make_async_remote_copy(src_ref=src, dst_ref=dst, send_sem=send_sem, recv_sem=recv_sem,
                                        device_id=(px, py, pc), device_id_type=MESH)


def _reduce_start(name, grads):
    n = len(grads)
    lands = [lax.empty((N_DEV, g.shape[1] // 2, g.shape[2]), g.dtype) for g in grads]

    def body(*refs):
        gs, ls, send_sem, recv_sem = refs[:n], refs[n:2 * n], refs[2 * n], refs[2 * n + 1]
        x, y, c, _ = _place()
        for a in range(n):
            for mask in range(1, N_DEV):
                s = (N_DEV - 1) * a + mask - 1
                _reduce_copy(gs[a], ls[a], mask, send_sem.at[s], recv_sem.at[s], x, y, c, True).start()
        refs[-1][...] = jnp.zeros_like(refs[-1])

    sem = pltpu.SemaphoreType.DMA((n * (N_DEV - 1),))
    out = pl.pallas_call(
        body, name=name, in_specs=[HBM] * (2 * n),
        out_specs=[SEM, SEM] + [HBM] * (2 * n) + [pl.BlockSpec(memory_space=pltpu.VMEM)],
        out_shape=[sem, sem] + [pltpu.HBM(t.shape, t.dtype) for t in grads + lands] + [jax.ShapeDtypeStruct((8, 128), F32)],
        input_output_aliases={a: 2 + a for a in range(2 * n)}, compiler_params=SPLIT_COPY)(
            *[_in_hbm(t) for t in grads + lands])
    return out[0], out[1], list(out[2:2 + n]), list(out[2 + n:2 + 2 * n]), out[-1]


def _reduce_wait(name, send, recv, grads, lands, after):
    n = len(grads)

    def body(*refs):
        gs, ls, send_sem, recv_sem = refs[:n], refs[n:2 * n], refs[2 * n], refs[2 * n + 1]
        x, y, c, _ = _place()
        for a in range(n):
            for mask in range(1, N_DEV):
                s = (N_DEV - 1) * a + mask - 1
                sems = (send_sem.at[s], recv_sem.at[s])
                _reduce_copy(gs[a], ls[a], mask, *sems, x, y, c, True).wait_send()
                _reduce_copy(gs[a], ls[a], mask, *sems, x, y, c, False).wait_recv()

    out = pl.pallas_call(
        body, name=name, in_specs=[HBM] * (2 * n) + [SEM, SEM, ANY], out_specs=[HBM] * (2 * n),
        out_shape=[pltpu.HBM(t.shape, t.dtype) for t in grads + lands],
        input_output_aliases={a: a for a in range(2 * n)}, compiler_params=SPLIT_COPY)(*grads, *lands, send, recv, after)
    return list(out[:n]), list(out[n:])


def _reduce_sum(name, g, land, layer, into, chip, c):
    _, k4, n4 = g.shape
    half = k4 // 2
    tr = max(t for t in range(16, 513, 16) if half % t == 0)
    per = half // tr
    me = 2 * chip + c

    def body(s_ref, own_ref, *refs):
        total = own_ref[...].astype(F32)
        for ref in refs[:N_DEV - 1]:
            total = total + ref[...].astype(F32)
        refs[-1][...] = total

    in_specs = [pl.BlockSpec((None, tr, n4), lambda i, s: (s[0], s[1] * per + i, 0))]
    in_specs += [pl.BlockSpec((None, tr, n4), functools.partial(lambda i, s, m: (s[1 + m], i, 0), m=m))
                 for m in range(1, N_DEV)]
    ins = [g] + [land] * (N_DEV - 1)
    aliases = {}
    if into is not None:
        in_specs, ins, aliases = in_specs + [ANY], ins + [into], {1 + N_DEV: 0}
    return pl.pallas_call(
        body, name=name,
        grid_spec=pltpu.PrefetchScalarGridSpec(
            num_scalar_prefetch=1, grid=(per,), in_specs=in_specs,
            out_specs=pl.BlockSpec((None, tr, n4), lambda i, s: (layer, s[1] * per + i, 0))),
        out_shape=jax.ShapeDtypeStruct((DEPTH, k4, n4), F32), input_output_aliases=aliases,
        compiler_params=_params(("parallel",)))(_scalars(chip, c, *[me ^ m for m in range(1, N_DEV)]), *ins)


def _join_halves(name, bufs):
    n = len(bufs)

    def body(*refs):
        ins, outs = refs[:n], refs[n:2 * n]
        send_sem, recv_sem = refs[2 * n:]
        x, y, c, _ = _place()

        def rows(ref, which):
            half = ref.shape[1] // 2
            return ref.at[:, pl.ds(pl.multiple_of(which * half, half), half)]

        sends = [pltpu.make_async_remote_copy(
            src_ref=rows(ins[a], c), dst_ref=rows(outs[a], c), send_sem=send_sem.at[a], recv_sem=recv_sem.at[a],
            device_id=(x, y, 1 - c), device_id_type=MESH) for a in range(n)]
        for cp in sends:
            cp.start()
        for a in range(n):
            sends[a].wait_send()
            pltpu.make_async_remote_copy(
                src_ref=rows(ins[a], c), dst_ref=rows(outs[a], 1 - c), send_sem=send_sem.at[a], recv_sem=recv_sem.at[a],
                device_id=(x, y, 1 - c), device_id_type=MESH).wait_recv()

    return pl.pallas_call(
        body, name=name, in_specs=[ANY] * n, out_specs=[ANY] * n,
        out_shape=[jax.ShapeDtypeStruct(b.shape, b.dtype) for b in bufs],
        input_output_aliases={a: a for a in range(n)},
        scratch_shapes=[pltpu.SemaphoreType.DMA((n,)), pltpu.SemaphoreType.DMA((n,))],
    )(*bufs)


def _small_copy(b_ref, l_ref, mask, send_sem, recv_sem, x, y, c, sending):
    px, py, pc = x ^ ((mask >> 2) & 1), y ^ ((mask >> 1) & 1), c ^ (mask & 1)
    dst = l_ref.at[4 * x + 2 * y + c] if sending else l_ref.at[4 * px + 2 * py + pc]
    return pltpu.make_async_remote_copy(src_ref=b_ref, dst_ref=dst, send_sem=send_sem.at[mask - 1],
                                        recv_sem=recv_sem.at[mask - 1], device_id=(px, py, pc), device_id_type=MESH)


def _small_start(block):
    land = lax.empty((N_DEV,) + block.shape, block.dtype)

    def body(b_ref, l_ref, send_sem, recv_sem, b_thru, l_thru, token):
        x, y, c, _ = _place()
        for mask in range(1, N_DEV):
            _small_copy(b_ref, l_ref, mask, send_sem, recv_sem, x, y, c, True).start()
        token[...] = jnp.zeros_like(token)

    sem = pltpu.SemaphoreType.DMA((N_DEV - 1,))
    return pl.pallas_call(
        body, name="small_start", in_specs=[HBM, HBM],
        out_specs=[SEM, SEM, HBM, HBM, pl.BlockSpec(memory_space=pltpu.VMEM)],
        out_shape=[sem, sem, pltpu.HBM(block.shape, block.dtype), pltpu.HBM(land.shape, land.dtype),
                   jax.ShapeDtypeStruct((8, 128), F32)],
        input_output_aliases={0: 2, 1: 3}, compiler_params=SPLIT_COPY)(_in_hbm(block), _in_hbm(land))


def _small_wait(send, recv, block, land, after):
    def body(b_ref, l_ref, send_sem, recv_sem, after_ref, b_out, l_out):
        x, y, c, _ = _place()
        for mask in range(1, N_DEV):
            _small_copy(b_ref, l_ref, mask, send_sem, recv_sem, x, y, c, True).wait_send()
            _small_copy(b_ref, l_ref, mask, send_sem, recv_sem, x, y, c, False).wait_recv()

    return pl.pallas_call(
        body, name="small_wait", in_specs=[HBM, HBM, SEM, SEM, ANY], out_specs=[HBM, HBM],
        out_shape=[pltpu.HBM(block.shape, block.dtype), pltpu.HBM(land.shape, land.dtype)],
        input_output_aliases={0: 0, 1: 1}, compiler_params=SPLIT_COPY)(block, land, send, recv, after)


def _small_sum(land):
    def body(l_ref, out_ref):
        total = l_ref[0]
        for d in range(1, N_DEV):
            total = total + l_ref[d]
        out_ref[...] = total

    vmem = pl.BlockSpec(memory_space=pltpu.VMEM)
    return pl.pallas_call(
        body, name="small_sum", in_specs=[vmem], out_specs=vmem,
        out_shape=jax.ShapeDtypeStruct(land.shape[1:], F32),
        compiler_params=pltpu.CompilerParams(vmem_limit_bytes=VMEM_LIMIT))(land)


B_Q_COL = 2304 // 128
B_K0, B_V0 = 2816, 2944


def _full_cols(w_g):
    return w_g.transpose(1, 0, 2).reshape(w_g.shape[1], -1)


A_DILS = tuple(d for _, d in A_GROUPS)
A_PAIRS = N_A // 2


def _src_a(proj):
    return ((proj, 0), (proj, A_PAIRS), (proj, 2 * A_PAIRS))


def _mixer_fwd(h1, wget, rel_bias, sinks_l, bidx):
    w = dict(wget(0, h1))
    proj = _mm_nt("proj_in", h1, w["w_in"], F32, tm=S, tn=1152)
    no_sinks = jnp.full((N_A,), NEG, F32)
    o_g, lse_g = _band_fwd("band_fwd_a", A_DILS, A_PAIRS, BLK, 0, _src_a(proj), bidx[:3], rel_bias, no_sinks)
    o_a32, o_a, lse_a = _comb_fwd(o_g, lse_g)
    src_b = ((proj, B_Q_COL), (proj, B_K0 // 128), (proj, B_V0 // 128))
    o_b32, lse_b = _band_fwd("band_fwd_b", (1,), 4, BLK - 1, N_A, src_b, bidx[3:], rel_bias, sinks_l, kv_shared=True)
    o_b = o_b32.astype(BF16)
    o_c32, o_c, tot_c = _sb_fwd(proj)
    w.update(wget(1, o_c32))
    br = [_mm_nn("branch_a", o_a, w["w_br_a"], F32, tm=S), _mm_nn("branch_b", o_b, w["w_br_b"], F32, tm=S),
          _mm_nn("branch_c", o_c, w["w_br_c"], F32, tm=S)]
    merged = _gate_fwd(proj, w["b_gate"], br)
    mo = _mm_nn("out_proj", merged, w["w_out"], F32, tm=S)
    saved = dict(proj=proj, src_b=src_b, o_a32=o_a32, lse_a=lse_a, o_b32=o_b32, lse_b=lse_b, tot_c=tot_c,
                 o_a=o_a, o_b=o_b, o_c=o_c, br=br, merged=merged)
    return mo, saved, w


def _mixer_bwd(d_mo, h1, w, sv, rel_bias, sinks_l, bidx, stats_in, emit):
    grads = {}
    dmerged = _mm_nt("out_proj_dx", d_mo, w["w_out"], F32, tm=S)
    grads["w_out"] = _mm_tn_sharded("out_proj_dw", sv["merged"], d_mo, True)
    e, dgate, db_gate = _gate_bwd(sv["proj"], w["b_gate"], sv["br"], dmerged)
    grads["b_gate"] = db_gate
    d_o = {}
    for n, name in enumerate("abc"):
        d_o[name] = _mm_nt("branch_%s_dx" % name, e[n], w["w_br_" + name], F32, tm=S)
        grads["w_br_" + name] = _mm_tn_sharded("branch_%s_dw" % name, sv["o_" + name], e[n], False)
    zero = emit(1, grads)
    no_sinks = jnp.full((N_A,), NEG, F32) + zero[0]
    dq_a, dk_a, dv_a, st_a = _band_bwd("band_bwd_a", A_DILS, A_PAIRS, BLK, 0, _src_a(sv["proj"]), bidx[:3], rel_bias,
                                       no_sinks, sv["o_a32"], sv["lse_a"], d_o["a"], stats_in[:N_A])
    dq_b, dk_b, dv_b, st_b = _band_bwd("band_bwd_b", (1,), 4, BLK - 1, N_A, sv["src_b"], bidx[3:], rel_bias, sinks_l,
                                       sv["o_b32"], sv["lse_b"], d_o["b"], stats_in[N_A:], kv_shared=True)
    stats = jnp.concatenate([st_a, st_b], axis=0)
    dcq, dck, dcv = _sb_bwd(sv["proj"], sv["tot_c"], d_o["c"])
    cols = [dq_a, dk_a, dv_a, dq_b, dk_b, dv_b, dcq, dck, dcv]
    dproj = jnp.concatenate([t.astype(BF16) for t in cols] + list(dgate), axis=1)
    grads["w_in"] = _mm_tn("proj_in_dw", dproj, h1, BF16, tm=1152, tn=1024).reshape(N_CHIPS, IN_SHARD, D)
    zero = emit(2, grads)
    dh1 = _mm_nn("proj_in_dx", dproj, w["w_in"], F32, tm=S, tk=2304)
    return dh1, grads, stats, zero


def _ffn_fwd(h2, w):
    u = _mm_nn("ffn_up", h2, w["w_up"], F32, tm=S, tn=1024)
    a = _conv_fwd(u, w["conv_w"], w["conv_b"])
    dn = _mm_nn("ffn_down", a, w["w_down"], F32, tm=1024)
    return dn, dict(u=u, a=a)


def _ffn_bwd(d_dn, h2, w, sv):
    grads = {}
    da = _mm_nt("ffn_down_dx", d_dn, w["w_down"], F32, tm=S, tn=1024)
    grads["w_down"] = _mm_tn_sharded("ffn_down_dw", sv["a"], d_dn, True, tm=1024, tn=1024)
    dug, duv, dwg, dwv, dbg, dbv = _conv_bwd(sv["u"], w["conv_w"], w["conv_b"], da)
    du = jnp.concatenate([dug, duv], axis=1)
    grads["conv_w"] = jnp.concatenate([dwg, dwv], axis=1)
    grads["conv_b"] = jnp.concatenate([dbg, dbv], axis=1)
    dh2 = _mm_nt("ffn_up_dx", du, w["w_up"], F32, tm=S, tk=2048)
    grads["w_up"] = _mm_tn_sharded("ffn_up_dw", h2, du, False, tm=1024, tn=1024)
    return dh2, grads


BIG = ("w_in", "w_br_a", "w_br_b", "w_br_c", "w_out", "w_up", "w_down")


def _shard_view(name, w):
    return jnp.swapaxes(w, 1, 2) if name == "w_in" else w
WEIGHT_GROUPS = (("w_in", "b_gate"), ("w_br_a", "w_br_b", "w_br_c", "w_out"), ("w_up", "conv_w", "w_down"))
GRAD_GROUPS = (("w_down", "w_up"), ("w_out", "w_br_a", "w_br_b", "w_br_c"), ("w_in",))
SMALL_ROWS = (("rel_bias", 8), ("attn_pre_norm", 16), ("attn_post_norm", 16), ("ffn_pre_norm", 16), ("ffn_post_norm", 16),
              ("sinks", 8), ("conv_b", 128), ("b_gate", 48), ("conv_w", 384), ("loss", 8))


def _pack_small(vals):
    rows = []
    for name, n in SMALL_ROWS:
        flat = vals[name].reshape(-1).astype(F32)
        rows.append(jnp.pad(flat, (0, n * 128 - flat.shape[0])).reshape(n, 128))
    return jnp.concatenate(rows, axis=0)


def _unpack_small(block, shapes):
    out, row = {}, 0
    for name, n in SMALL_ROWS:
        size = int(np.prod(shapes[name]))
        out[name] = block[row:row + n].reshape(-1)[:size].reshape(shapes[name])
        row += n
    return out


def kernel(x, rel_bias, attn_pre_norm, w_in, b_gate, sinks, w_br_a, w_br_b, w_br_c, w_out, attn_post_norm, ffn_pre_norm, w_up, conv_w, conv_b, w_down, ffn_post_norm, loss_target, m_rel_bias, m_attn_pre_norm, m_w_in, m_b_gate, m_sinks, m_w_br_a, m_w_br_b, m_w_br_c, m_w_out, m_attn_post_norm, m_ffn_pre_norm, m_w_up, m_conv_w, m_conv_b, m_w_down, m_ffn_post_norm, v_rel_bias, v_attn_pre_norm, v_w_in, v_b_gate, v_sinks, v_w_br_a, v_w_br_b, v_w_br_c, v_w_out, v_attn_post_norm, v_ffn_pre_norm, v_w_up, v_conv_w, v_conv_b, v_w_down, v_ffn_post_norm):
    names = ("rel_bias", "attn_pre_norm", "w_in", "b_gate", "sinks", "w_br_a", "w_br_b", "w_br_c", "w_out",
             "attn_post_norm", "ffn_pre_norm", "w_up", "conv_w", "conv_b", "w_down", "ffn_post_norm")
    weights = dict(zip(names, (rel_bias, attn_pre_norm, w_in, b_gate, sinks, w_br_a, w_br_b, w_br_c, w_out,
                               attn_post_norm, ffn_pre_norm, w_up, conv_w, conv_b, w_down, ffn_post_norm)))
    mom1 = dict(zip(names, (m_rel_bias, m_attn_pre_norm, m_w_in, m_b_gate, m_sinks, m_w_br_a, m_w_br_b, m_w_br_c,
                            m_w_out, m_attn_post_norm, m_ffn_pre_norm, m_w_up, m_conv_w, m_conv_b, m_w_down,
                            m_ffn_post_norm)))
    mom2 = dict(zip(names, (v_rel_bias, v_attn_pre_norm, v_w_in, v_b_gate, v_sinks, v_w_br_a, v_w_br_b, v_w_br_c,
                            v_w_out, v_attn_post_norm, v_ffn_pre_norm, v_w_up, v_conv_w, v_conv_b, v_w_down,
                            v_ffn_post_norm)))

    chip = 2 * lax.axis_index("x") + lax.axis_index("y")
    core = lax.axis_index("c")

    keys = [(n, l) for l in range(DEPTH) for group in WEIGHT_GROUPS for n in group]
    groups = [[keys.index((n, l)) for n in group] for l in range(DEPTH) for group in WEIGHT_GROUPS]

    def slot_buffer(n, l):
        if n in BIG:
            return _cast_into_slot("cast_" + n, _shard_view(n, weights[n]), l, chip)
        shard = weights[n][l]
        return lax.dynamic_update_slice(jnp.zeros((N_CHIPS,) + shard.shape, F32), shard[None],
                                        (chip, jnp.int32(0), jnp.int32(0)))

    by_halves = [keys.index(k) for k in (("w_in", 0), ("w_up", DEPTH - 1), ("w_down", DEPTH - 1))]
    n_first = len(groups[0])
    sems, in_flight, _ = _gather_start("gather_start_first", [slot_buffer(*k) for k in keys[:n_first]], groups[:1],
                                       tuple(a for a in by_halves if a < n_first))
    more = _gather_start("gather_start", [slot_buffer(*k) for k in keys[n_first:]],
                         [[a - n_first for a in g] for g in groups[1:]],
                         tuple(a - n_first for a in by_halves if a >= n_first))
    sems, in_flight, started = sems + more[0], in_flight + more[1], more[2]

    def wget(l, gi, after):
        g = l * len(WEIGHT_GROUPS) + gi
        after = started if g == 0 else after
        halved = tuple(e for e, a in enumerate(groups[g]) if a in by_halves)
        got = list(_gather_wait("gather_wait_%d_%d" % (l, gi), [in_flight[a] for a in groups[g]], *sems[g], after,
                                halved))
        if halved:
            for e, buf in zip(halved, _swap_halves("swap_halves_%d_%d" % (l, gi), [got[e] for e in halved])):
                got[e] = buf
        out = {}
        for n, buf in zip(WEIGHT_GROUPS[gi], got):
            if n in ("w_in", "w_out", "w_down"):
                out[n] = buf.reshape(-1, buf.shape[-1])
            else:
                out[n] = buf if n == "w_up" else _full_cols(buf)
        if gi == len(WEIGHT_GROUPS) - 1:
            out["conv_b"] = conv_b[l:l + 1]
        return out

    pending = []

    def emit(l, gi, grads):
        group = GRAD_GROUPS[gi]
        *started, token = _reduce_start("reduce_start_%d_%d" % (l, gi), [grads[n] for n in group])
        pending.append((l, group) + tuple(started))
        return token[:1, :1]

    local = _local_step(x.reshape(S, D), loss_target.reshape(S, D), wget, emit, rel_bias, sinks, attn_pre_norm,
                        attn_post_norm, ffn_pre_norm, ffn_post_norm)
    return _reduce_and_update(x.shape, names, weights, mom1, mom2, chip, core, pending, *local)


def _local_step(xs, target, wget, emit, rel_bias, sinks, attn_pre_norm, attn_post_norm, ffn_pre_norm, ffn_post_norm):
    bidx = jnp.asarray(_bucket_maps())

    saved, layers = [], []
    h1 = _rms_fwd("pre_norm_first", xs, attn_pre_norm[0:1])
    x_in = xs
    for l in range(DEPTH):
        mo, sv_mix, w = _mixer_fwd(h1, functools.partial(wget, l), rel_bias, sinks[l], bidx)
        x_mid, h2 = _post_pre_fwd("post_attn_norm", x_in, mo, attn_post_norm[l:l + 1], ffn_pre_norm[l:l + 1])
        w.update(wget(l, 2, h2))
        dn, sv_ffn = _ffn_fwd(h2, w)
        g_next = attn_pre_norm[l + 1:l + 2] if l + 1 < DEPTH else None
        x_out, h1_next = _post_pre_fwd("post_ffn_norm" if l + 1 < DEPTH else "post_ffn_norm_last", x_mid, dn,
                                       ffn_post_norm[l:l + 1], g_next)
        saved.append(dict(x_in=x_in, h1=h1, mo=mo, x_mid=x_mid, h2=h2, dn=dn, mix=sv_mix, ffn=sv_ffn))
        layers.append(w)
        x_in, h1 = x_out, h1_next

    loss_row, dres = _loss_kernel(x_in, target)

    small = [None] * DEPTH
    stats = jnp.zeros((N_BAND_Q, 8, 128), F32)
    dh_next = None
    for l in reversed(range(DEPTH)):
        w, sv = layers[l], saved[l]
        if l + 1 < DEPTH:
            pre = (saved[l + 1]["x_in"], attn_pre_norm[l + 1:l + 2] + zero, dh_next)
            dres, d_dn, dg_pre_next, dg_fpost = _norm_bwd("post_ffn_norm_bwd", dres, pre,
                                                          (sv["dn"], ffn_post_norm[l:l + 1]))
            small[l + 1]["attn_pre_norm"] = dg_pre_next
        else:
            dres, d_dn, _, dg_fpost = _norm_bwd("post_ffn_norm_last_bwd", dres, None, (sv["dn"], ffn_post_norm[l:l + 1]))
        dh2, g_ffn = _ffn_bwd(d_dn, sv["h2"], w, sv["ffn"])
        zero = emit(l, 0, g_ffn)
        dres, d_mo, dg_fpre, dg_apost = _norm_bwd("post_attn_norm_bwd", dres,
                                                  (sv["x_mid"], ffn_pre_norm[l:l + 1] + zero, dh2),
                                                  (sv["mo"], attn_post_norm[l:l + 1]))
        dh_next, g_mix, stats, zero = _mixer_bwd(d_mo, sv["h1"], w, sv["mix"], rel_bias, sinks[l], bidx, stats,
                                                 functools.partial(emit, l))
        small[l] = dict(ffn_post_norm=dg_fpost, ffn_pre_norm=dg_fpre, attn_post_norm=dg_apost,
                        sinks=stats[N_A:, 1, 0], conv_b=g_ffn["conv_b"], b_gate=g_mix["b_gate"], conv_w=g_ffn["conv_w"])
    grad_x, _, dg_pre0, _ = _norm_bwd("pre_norm_first_bwd", dres, (saved[0]["x_in"], attn_pre_norm[0:1] + zero, dh_next),
                                      None)
    small[0]["attn_pre_norm"] = dg_pre0
    return loss_row, grad_x, small, stats


def _reduce_and_update(x_shape, names, weights, mom1, mom2, chip, core, pending, loss_row, grad_x, small, stats):
    delta, new_m, new_v, grads = {}, {}, {}, {}

    def update(n, g):
        grads[n], delta[n], new_m[n], new_v[n] = _adamw("adamw_" + n, _shard_view(n, weights[n]), g,
                                                        _shard_view(n, mom1[n]), _shard_view(n, mom2[n]))

    small_vals = {n: jnp.stack([small[l][n].reshape(weights[n].shape[1:]) for l in range(DEPTH)])
                  for n in ("attn_pre_norm", "attn_post_norm", "ffn_pre_norm", "ffn_post_norm", "conv_b", "sinks")}
    small_vals["b_gate"] = jnp.stack([small[l]["b_gate"] for l in range(DEPTH)])
    small_vals["conv_w"] = jnp.stack([small[l]["conv_w"] for l in range(DEPTH)])
    small_vals["rel_bias"] = stats[:, 0, :NUM_BUCKETS].T
    small_vals["loss"] = loss_row[0, :1]
    shapes = {n: v.shape for n, v in small_vals.items()}
    small_send, small_recv, packed, small_land, started = _small_start(_pack_small(small_vals))

    summed = {}

    def finish(which, after):
        for l, group, send, recv, gs, lands in pending:
            if (group == ("w_in",)) == which:
                gs, lands = _reduce_wait("reduce_wait_%d_%s" % (l, group[0]), send, recv, gs, lands, after)
                for n, g, land in zip(group, gs, lands):
                    summed[n] = _reduce_sum("reduce_sum_%d_%s" % (l, n), g, land, l, summed.get(n), chip, core)

    finish(False, started)
    early = [n for n in BIG if n != "w_in"]
    for n, g in zip(early, _join_halves("join_halves", [summed[n] for n in early])):
        update(n, g)
    finish(True, delta[early[-1]])
    update("w_in", _join_halves("join_halves_w_in", [summed["w_in"]])[0])

    packed, small_land = _small_wait(small_send, small_recv, packed, small_land, delta["w_in"])
    small_land = lax.dynamic_update_slice(small_land, packed[None], (2 * chip + core, jnp.int32(0), jnp.int32(0)))
    reduced = _unpack_small(_small_sum(small_land), shapes)
    reduced["b_gate"] = lax.dynamic_slice_in_dim(reduced["b_gate"], chip * (D // N_CHIPS), D // N_CHIPS, axis=2)
    reduced["conv_w"] = lax.dynamic_slice_in_dim(reduced["conv_w"], chip * (2 * D_FF // N_CHIPS), 2 * D_FF // N_CHIPS, axis=2)
    for n in names:
        if n not in grads:
            update(n, reduced[n].reshape(weights[n].shape))
    for out in (grads, delta, new_m, new_v):
        out["w_in"] = _shard_view("w_in", out["w_in"])

    loss = reduced["loss"].reshape(())
    return (loss, grad_x.reshape(x_shape), *[grads[n] for n in names], *[delta[n] for n in names],
            *[new_m[n] for n in names], *[new_v[n] for n in names])
```

```python
import functools
import math

import numpy as np
import jax
import jax.numpy as jnp
from jax import lax
from jax.experimental import pallas as pl
from jax.experimental.pallas import tpu as pltpu

F32 = jnp.float32
BF16 = jnp.bfloat16

S = 2048
D = 1024
DEPTH = 2
HD = 64
BLK = 128
NQB = S // BLK
A_GROUPS = ((128, 1), (512, 4), (2048, 16))
N_BAND_Q = 20
N_A = 12
NUM_BUCKETS = 32
MAX_DISTANCE = 2048
D_FF = 4096
IN_COLS = 6912
IN_SHARD = IN_COLS // 4
OFF_GATE = 3840
EPS = 1e-6
SCALE = HD ** -0.5
NEG = -1e30
N_CHIPS = 4
N_DEV = 8

ADAM_LR = 0.001
ADAM_B1 = 0.9
ADAM_B2 = 0.999
ADAM_EPS = 1e-08
ADAM_WD = 0.01
ADAM_STEP = 10

VMEM_LIMIT = 56 * 1024 * 1024

NN = (((1,), (0,)), ((), ()))
NT = (((1,), (1,)), ((), ()))
TN = (((0,), (0,)), ((), ()))

MESH = pl.DeviceIdType.MESH
ANY = pl.BlockSpec(memory_space=pl.ANY)


def _dot(a, b, dims):
    return lax.dot_general(a, b, dims, preferred_element_type=F32)


def _params(sem):
    return pltpu.CompilerParams(dimension_semantics=sem, vmem_limit_bytes=VMEM_LIMIT)


def _matmul(name, a, b, out_shape, out_dtype, grid, a_spec, b_spec, o_spec, dims, acc_shape):
    nk = grid[-1]

    def body(a_ref, b_ref, o_ref, *scratch):
        part = _dot(a_ref[...].astype(BF16), b_ref[...].astype(BF16), dims)
        if nk == 1:
            o_ref[...] = part.astype(o_ref.dtype)
            return
        acc_ref, = scratch
        k = pl.program_id(len(grid) - 1)

        @pl.when(k == 0)
        def _():
            acc_ref[...] = part

        @pl.when(k > 0)
        def _():
            acc_ref[...] += part

        @pl.when(k == nk - 1)
        def _():
            o_ref[...] = acc_ref[...].astype(o_ref.dtype)

    scratch = [] if nk == 1 else [pltpu.VMEM(acc_shape, F32)]
    sem = ("parallel",) * (len(grid) - 1) + ("arbitrary",)
    return pl.pallas_call(
        body, name=name, grid=grid, in_specs=[a_spec, b_spec], out_specs=o_spec,
        out_shape=jax.ShapeDtypeStruct(out_shape, out_dtype), scratch_shapes=scratch,
        compiler_params=_params(sem))(a, b)


FULL_K = 8192


def _mm_tn_sharded(name, a, b, row_sharded, tm=512, tn=512, tk=FULL_K):
    k, m = a.shape
    n = b.shape[1]
    m4, n4 = (m // N_CHIPS, n) if row_sharded else (m, n // N_CHIPS)
    tm, tn, tk = min(tm, m4), min(tn, n4), min(tk, k)
    per_m, per_n = m4 // tm, n4 // tn
    if row_sharded:
        o_map = lambda i, j, l: (i // per_m, i % per_m, j)
    else:
        o_map = lambda i, j, l: (j // per_n, i, j % per_n)
    return _matmul(name, a, b, (N_CHIPS, m4, n4), BF16, (m // tm, n // tn, k // tk),
                   pl.BlockSpec((tk, tm), lambda i, j, l: (l, i)),
                   pl.BlockSpec((tk, tn), lambda i, j, l: (l, j)),
                   pl.BlockSpec((None, tm, tn), o_map), TN, (tm, tn))


def _mm_nn(name, a, b, out_dtype, tm=512, tn=512, tk=FULL_K):
    m, k = a.shape
    n = b.size // k
    tm, tn, tk = min(tm, m), min(tn, b.shape[-1]), min(tk, k)
    per_shard = b.shape[-1] // tn
    if b.ndim == 2:
        b_spec = pl.BlockSpec((tk, tn), lambda i, j, l: (l, j))
    else:
        b_spec = pl.BlockSpec((None, tk, tn), lambda i, j, l: (j // per_shard, l, j % per_shard))
    return _matmul(name, a, b, (m, n), out_dtype, (m // tm, n // tn, k // tk),
                   pl.BlockSpec((tm, tk), lambda i, j, l: (i, l)), b_spec,
                   pl.BlockSpec((tm, tn), lambda i, j, l: (i, j)), NN, (tm, tn))


def _mm_nt(name, a, b, out_dtype, tm=512, tn=512, tk=FULL_K):
    m, k = a.shape
    n = b.shape[-2]
    tm, tn, tk = min(tm, m), min(tn, n), min(tk, b.shape[-1])
    per_shard = b.shape[-1] // tk
    if b.ndim == 2:
        b_spec = pl.BlockSpec((tn, tk), lambda i, j, l: (j, l))
    else:
        b_spec = pl.BlockSpec((None, tn, tk), lambda i, j, l: (l // per_shard, j, l % per_shard))
    return _matmul(name, a, b, (m, n), out_dtype, (m // tm, n // tn, k // tk),
                   pl.BlockSpec((tm, tk), lambda i, j, l: (i, l)), b_spec,
                   pl.BlockSpec((tm, tn), lambda i, j, l: (i, j)), NT, (tm, tn))


def _mm_tn(name, a, b, out_dtype, tm=512, tn=512, tk=FULL_K):
    k, m = a.shape
    n = b.shape[1]
    tm, tn, tk = min(tm, m), min(tn, n), min(tk, k)
    return _matmul(name, a, b, (m, n), out_dtype, (m // tm, n // tn, k // tk),
                   pl.BlockSpec((tk, tm), lambda i, j, l: (l, i)),
                   pl.BlockSpec((tk, tn), lambda i, j, l: (l, j)),
                   pl.BlockSpec((tm, tn), lambda i, j, l: (i, j)), TN, (tm, tn))


TR = 512


def _row_spec(width=D):
    return pl.BlockSpec((TR, width), lambda i: (i, 0))


def _vec_spec(width=D):
    return pl.BlockSpec((1, width), lambda i: (0, 0))


def _rms(x, g):
    r = lax.rsqrt(jnp.mean(x * x, axis=-1, keepdims=True) + EPS)
    return x * r * g


def _rms_fwd(name, x, g):
    def body(x_ref, g_ref, h_ref):
        h_ref[...] = _rms(x_ref[...], g_ref[...]).astype(BF16)

    return pl.pallas_call(
        body, name=name, grid=(S // TR,), in_specs=[_row_spec(), _vec_spec()], out_specs=_row_spec(),
        out_shape=jax.ShapeDtypeStruct((S, D), BF16), compiler_params=_params(("parallel",)))(x, g)


def _post_pre_fwd(name, x, y, g_post, g_pre):
    has_pre = g_pre is not None

    def body(*refs):
        if has_pre:
            x_ref, y_ref, gp_ref, gn_ref, xn_ref, h_ref = refs
        else:
            x_ref, y_ref, gp_ref, xn_ref = refs
        xn = x_ref[...] + _rms(y_ref[...], gp_ref[...])
        xn_ref[...] = xn
        if has_pre:
            h_ref[...] = _rms(xn, gn_ref[...]).astype(BF16)

    ins = [x, y, g_post] + ([g_pre] if has_pre else [])
    in_specs = [_row_spec(), _row_spec(), _vec_spec()] + ([_vec_spec()] if has_pre else [])
    out_shape = [jax.ShapeDtypeStruct((S, D), F32)] + ([jax.ShapeDtypeStruct((S, D), BF16)] if has_pre else [])
    out_specs = [_row_spec()] + ([_row_spec()] if has_pre else [])
    out = pl.pallas_call(
        body, name=name, grid=(S // TR,), in_specs=in_specs, out_specs=out_specs, out_shape=out_shape,
        compiler_params=_params(("parallel",)))(*ins)
    return out if has_pre else (out[0], None)


def _rms_bwd_math(x, g, dy):
    r = lax.rsqrt(jnp.mean(x * x, axis=-1, keepdims=True) + EPS)
    n = x * r
    dn = dy * g
    dx = r * (dn - n * jnp.mean(dn * n, axis=-1, keepdims=True))
    return dx, jnp.sum(dy * n, axis=0, keepdims=True)


def _norm_bwd(name, dres, pre=None, post=None):
    has_pre, has_post = pre is not None, post is not None

    def body(*refs):
        refs = list(refs)
        dres_ref = refs.pop(0)
        if has_pre:
            xn_ref, gn_ref, dh_ref = refs[:3]
            refs = refs[3:]
        if has_post:
            y_ref, gp_ref = refs[:2]
            refs = refs[2:]
        dxn_ref = refs.pop(0)
        dy_ref = refs.pop(0) if has_post else None
        dgn_ref = refs.pop(0) if has_pre else None
        dgp_ref = refs.pop(0) if has_post else None
        first = pl.program_id(0) == 0
        dxn = dres_ref[...]
        if has_pre:
            dx, dg = _rms_bwd_math(xn_ref[...], gn_ref[...], dh_ref[...])
            dxn = dxn + dx

            @pl.when(first)
            def _():
                dgn_ref[...] = dg

            @pl.when(jnp.logical_not(first))
            def _():
                dgn_ref[...] += dg
        dxn_ref[...] = dxn
        if has_post:
            dy, dg = _rms_bwd_math(y_ref[...], gp_ref[...], dxn)
            dy_ref[...] = dy.astype(BF16)

            @pl.when(first)
            def _():
                dgp_ref[...] = dg

            @pl.when(jnp.logical_not(first))
            def _():
                dgp_ref[...] += dg

    ins, in_specs = [dres], [_row_spec()]
    if has_pre:
        ins += list(pre)
        in_specs += [_row_spec(), _vec_spec(), _row_spec()]
    if has_post:
        ins += list(post)
        in_specs += [_row_spec(), _vec_spec()]
    out_shape, out_specs = [jax.ShapeDtypeStruct((S, D), F32)], [_row_spec()]
    if has_post:
        out_shape.append(jax.ShapeDtypeStruct((S, D), BF16))
        out_specs.append(_row_spec())
    for _ in range(int(has_pre) + int(has_post)):
        out_shape.append(jax.ShapeDtypeStruct((1, D), F32))
        out_specs.append(_vec_spec())
    out = list(pl.pallas_call(
        body, name=name, grid=(S // TR,), in_specs=in_specs, out_specs=out_specs, out_shape=out_shape,
        compiler_params=_params(("arbitrary",)))(*ins))
    dxn = out.pop(0)
    dy = out.pop(0) if has_post else None
    dgn = out.pop(0) if has_pre else None
    dgp = out.pop(0) if has_post else None
    return dxn, dy, dgn, dgp


def _loss_kernel(y, target):
    def body(y_ref, t_ref, loss_ref, dy_ref):
        e = y_ref[...] - t_ref[...]
        dy_ref[...] = e * (1.0 / D)
        part = jnp.zeros((1, 128), F32) + 0.5 * jnp.sum(jnp.mean(e * e, axis=-1, keepdims=True))

        @pl.when(pl.program_id(0) == 0)
        def _():
            loss_ref[...] = part

        @pl.when(pl.program_id(0) > 0)
        def _():
            loss_ref[...] += part

    return pl.pallas_call(
        body, name="loss", grid=(S // TR,), in_specs=[_row_spec(), _row_spec()],
        out_specs=[_vec_spec(128), _row_spec()],
        out_shape=[jax.ShapeDtypeStruct((1, 128), F32), jax.ShapeDtypeStruct((S, D), F32)],
        compiler_params=_params(("arbitrary",)))(y, target)


def _t5_bucket_np(dist):
    max_exact = NUM_BUCKETS // 2
    nf = np.maximum(dist, 1).astype(np.float32)
    large = max_exact + (np.log(nf / max_exact) / np.float32(math.log(MAX_DISTANCE / max_exact))
                         * (NUM_BUCKETS - max_exact)).astype(np.int32)
    large = np.minimum(large, NUM_BUCKETS - 1)
    return np.where(dist < max_exact, dist, large).astype(np.int32)


def _bucket_maps():
    a = np.arange(BLK)[:, None]
    b = np.arange(2 * BLK)[None, :]
    dist = np.maximum(a + BLK - b, 0)
    maps = [_t5_bucket_np(dist * d) for _, d in A_GROUPS] + [_t5_bucket_np(dist)]
    return np.stack(maps).astype(np.int32)


def _pair_spec(col0):
    return pl.BlockSpec((S, 128), lambda p: (0, col0 + p))


def _band_rows(i, d):
    nb = S // d // BLK
    r, b = i // nb, i % nb
    cur = pl.ds(b * BLK * d + r, BLK, stride=d)
    prev = pl.ds(jnp.maximum(b - 1, 0) * BLK * d + r, BLK, stride=d)
    return cur, prev, jnp.minimum(b, 1)


def _band_bias(tab_ref, bi, h):
    bias = jnp.zeros((BLK, 2 * BLK), F32)
    for kk in range(NUM_BUCKETS):
        bias = jnp.where(bi == kk, tab_ref[kk, h], bias)
    return bias


def _lane_lo(rows=BLK):
    return lax.broadcasted_iota(jnp.int32, (rows, 128), 1) < HD


def _per_head(x, lo):
    return (jnp.sum(jnp.where(lo, x, 0.0), axis=1, keepdims=True) * (1.0 / HD),
            jnp.sum(jnp.where(lo, 0.0, x), axis=1, keepdims=True) * (1.0 / HD))


def _band_fill(bias_ref, tab_ref, bi, head, maxd):
    a = lax.broadcasted_iota(jnp.int32, (BLK, 2 * BLK), 0)
    c = lax.broadcasted_iota(jnp.int32, (BLK, 2 * BLK), 1)
    dist = a + BLK - c
    in_band = jnp.logical_and(dist >= 0, dist <= maxd)
    for h in range(2):
        bias = jnp.where(in_band, _band_bias(tab_ref, bi, head + h), NEG)
        bias_ref[1, h * BLK:(h + 1) * BLK, :] = bias
        bias_ref[0, h * BLK:(h + 1) * BLK, :] = jnp.where(c >= BLK, bias, NEG)


def _stack_heads(x, lo, dtype=BF16):
    return jnp.concatenate([jnp.where(lo, x, 0.0), jnp.where(lo, 0.0, x)], axis=0).astype(dtype)


def _unstack_heads(x, lo):
    n = x.shape[0] // 2
    return jnp.where(lo, x[:n], x[n:])


def _stack_rows(ref, prev, cur):
    return jnp.concatenate([ref[prev, :], ref[cur, :]], axis=0).astype(BF16)


PAIRS_PER_KV = 2


def _kv_specs(kc, vc, kv_shared):
    if not kv_shared:
        return [_pair_spec(kc), _pair_spec(vc)], []
    shared = [pl.BlockSpec((S, 128), functools.partial(lambda p, c: (0, c), c=c)) for c in (kc, vc)]
    return shared, [pltpu.VMEM((S, 128), F32)] * 2


def _expand_kv(dst_ref, src_ref, pair):
    x = src_ref[...]
    own = lax.broadcasted_iota(jnp.int32, (S, 128), 1) // HD == pair // PAIRS_PER_KV
    dst_ref[...] = jnp.where(own, x, pltpu.roll(x, HD, 1))


def _fold_kv(out_ref, acc_ref, pair):
    x = acc_ref[...]
    own = lax.broadcasted_iota(jnp.int32, (S, 128), 1) // HD == pair // PAIRS_PER_KV
    part = jnp.where(own, x + pltpu.roll(x, HD, 1), 0.0)

    @pl.when(pair == 0)
    def _():
        out_ref[...] = part

    @pl.when(pair > 0)
    def _():
        out_ref[...] += part


def _blocks_of_group(group, dils, block):
    def run(d):
        lax.fori_loop(0, NQB, functools.partial(block, d), 0, unroll=2)

    if len(dils) == 1:
        run(dils[0])
        return
    for g, d in enumerate(dils):
        pl.when(group == g)(functools.partial(run, d))


def _band_fwd(name, dils, n_pairs, maxd, head0, srcs, bidx_g, tab, sinks, kv_shared=False):
    (qa, qc), (ka, kc), (va, vc) = srcs
    per_group = n_pairs // len(dils)
    out_spec = _pair_spec(0)
    smem = pl.BlockSpec(memory_space=pltpu.SMEM)
    full = pl.BlockSpec((len(dils), BLK, 2 * BLK), lambda p: (0, 0, 0))

    kv_specs, kv_scratch = _kv_specs(kc, vc, kv_shared)

    def body(tab_ref, sink_ref, q_ref, k_ref, v_ref, bidx_ref, o_ref, lse_ref, bias_ref, *expanded):
        p = pl.program_id(0)
        if kv_shared:
            _expand_kv(expanded[0], k_ref, p)
            _expand_kv(expanded[1], v_ref, p)
            k_ref, v_ref = expanded
        _band_fill(bias_ref, tab_ref, bidx_ref[p // per_group], head0 + 2 * p, maxd)
        lo = _lane_lo()
        sink = jnp.where(lax.broadcasted_iota(jnp.int32, (2 * BLK, 1), 0) < BLK, sink_ref[2 * p], sink_ref[2 * p + 1])

        def block(d, i, carry):
            cur, prev, has_prev = _band_rows(i, d)
            qs = _stack_heads(q_ref[cur, :] * SCALE, lo)
            ks, vs = _stack_rows(k_ref, prev, cur), _stack_rows(v_ref, prev, cur)
            s = _dot(qs, ks, NT) + bias_ref[has_prev]
            m = jnp.max(s, axis=1, keepdims=True)
            pr = jnp.exp(s - m)
            l = jnp.sum(pr, axis=1, keepdims=True)
            num = _dot(pr.astype(BF16), vs, NN)
            lse = m + jnp.log(l)
            sig = 1.0 / (1.0 + jnp.exp(sink - lse))
            o_ref[cur, :] = _unstack_heads(num * (sig / l), lo)
            lse_ref[cur, :] = _unstack_heads(lse + jnp.zeros((2 * BLK, 128), F32), lo)
            return carry

        _blocks_of_group(p // per_group, dils, block)

    shape = jax.ShapeDtypeStruct((S, n_pairs * 128), F32)
    return pl.pallas_call(
        body, name=name, grid=(n_pairs,),
        in_specs=[smem, smem, _pair_spec(qc)] + kv_specs + [full],
        out_specs=[out_spec, out_spec], out_shape=[shape, shape],
        scratch_shapes=[pltpu.VMEM((2, 2 * BLK, 2 * BLK), F32)] + kv_scratch,
        compiler_params=_params(("parallel",)))(tab, sinks, qa, ka, va, bidx_g)


def _band_bwd(name, dils, n_pairs, maxd, head0, srcs, bidx_g, tab, sinks, o, lse, do, stats_in, kv_shared=False):
    (qa, qc), (ka, kc), (va, vc) = srcs
    per_group = n_pairs // len(dils)
    pair = _pair_spec(0)
    shared = pl.BlockSpec((S, 128), lambda p: (0, p % per_group))
    smem = pl.BlockSpec(memory_space=pltpu.SMEM)
    full = pl.BlockSpec((len(dils), BLK, 2 * BLK), lambda p: (0, 0, 0))
    stat_spec = pl.BlockSpec((2, 8, 128), lambda p: (p, 0, 0))
    kv_specs, kv_scratch = _kv_specs(kc, vc, kv_shared)

    def body(tab_ref, sink_ref, q_ref, k_ref, v_ref, bidx_ref, o_ref, lse_ref, do_ref, sin_ref,
             dq_ref, dk_ref, dv_ref, stat_ref, bias_ref, dsacc_ref, sk_ref, *expanded):
        p = pl.program_id(0)
        if kv_shared:
            _expand_kv(expanded[0], k_ref, p)
            _expand_kv(expanded[1], v_ref, p)
            k_ref, v_ref = expanded[:2]
            dk_out, dv_out, dk_ref, dv_ref = dk_ref, dv_ref, expanded[2], expanded[3]
        _band_fill(bias_ref, tab_ref, bidx_ref[p // per_group], head0 + 2 * p, maxd)
        dsacc_ref[...] = jnp.zeros_like(dsacc_ref)
        sk_ref[...] = jnp.zeros_like(sk_ref)
        dk_ref[...] = jnp.zeros_like(dk_ref)
        dv_ref[...] = jnp.zeros_like(dv_ref)
        lo = _lane_lo()
        head1 = lax.broadcasted_iota(jnp.int32, (2 * BLK, 1), 0) >= BLK
        sink = jnp.where(head1, sink_ref[2 * p + 1], sink_ref[2 * p])

        def block(d, i, carry):
            cur, prev, has_prev = _band_rows(i, d)
            qs = _stack_heads(q_ref[cur, :] * SCALE, lo)
            ks, vs = _stack_rows(k_ref, prev, cur), _stack_rows(v_ref, prev, cur)
            do = do_ref[cur, :]
            dos = _stack_heads(do, lo, F32)
            lse = jnp.concatenate(_per_head(lse_ref[cur, :], lo), axis=0)
            prod = do * o_ref[cur, :]
            delta = jnp.concatenate([jnp.sum(jnp.where(lo, prod, 0.0), axis=1, keepdims=True),
                                     jnp.sum(jnp.where(lo, 0.0, prod), axis=1, keepdims=True)], axis=0)
            sig = 1.0 / (1.0 + jnp.exp(sink - lse))
            pr = jnp.exp(_dot(qs, ks, NT) + bias_ref[has_prev] - lse)
            ds = pr * (sig * (_dot(dos.astype(BF16), vs, NT) - delta))
            dsb = ds.astype(BF16)
            dq_ref[cur, :] = SCALE * _unstack_heads(_dot(dsb, ks, NN), lo)
            dk = _dot(dsb, qs, TN)
            dv = _dot(pr.astype(BF16), (sig * dos).astype(BF16), TN)
            dk_ref[prev, :] += dk[:BLK]
            dk_ref[cur, :] += dk[BLK:]
            dv_ref[prev, :] += dv[:BLK]
            dv_ref[cur, :] += dv[BLK:]
            dsacc_ref[...] += ds
            sink_grad = -delta * (1.0 - sig)
            for h in range(2):
                sk_ref[h] += jnp.zeros((8, 128), F32) + jnp.sum(sink_grad[h * BLK:(h + 1) * BLK])
            return carry

        _blocks_of_group(p // per_group, dils, block)
        if kv_shared:
            _fold_kv(dk_out, dk_ref, p)
            _fold_kv(dv_out, dv_ref, p)

        bi = bidx_ref[p // per_group]
        lane = lax.broadcasted_iota(jnp.int32, (8, 128), 1)
        sub = lax.broadcasted_iota(jnp.int32, (8, 128), 0)
        for h in range(2):
            acc = dsacc_ref[h * BLK:(h + 1) * BLK, :]
            row = jnp.where(jnp.logical_and(sub == 1, lane == 0), sk_ref[h], 0.0)
            for kk in range(NUM_BUCKETS):
                tot = jnp.sum(jnp.where(bi == kk, acc, 0.0))
                row = jnp.where(jnp.logical_and(sub == 0, lane == kk), tot, row)
            stat_ref[h] = row + jnp.where(sub == 0, sin_ref[h], 0.0)

    shape = jax.ShapeDtypeStruct((S, n_pairs * 128), F32)
    kv_spec = pl.BlockSpec((S, 128), lambda p: (0, 0)) if kv_shared else pair
    kv_shape = jax.ShapeDtypeStruct((S, 128), F32) if kv_shared else shape
    return pl.pallas_call(
        body, name=name, grid=(n_pairs,),
        in_specs=[smem, smem, _pair_spec(qc)] + kv_specs + [full, shared, shared, shared, stat_spec],
        out_specs=[pair, kv_spec, kv_spec, stat_spec],
        out_shape=[shape, kv_shape, kv_shape, jax.ShapeDtypeStruct((2 * n_pairs, 8, 128), F32)],
        scratch_shapes=[pltpu.VMEM((2, 2 * BLK, 2 * BLK), F32), pltpu.VMEM((2 * BLK, 2 * BLK), F32),
                        pltpu.VMEM((2, 8, 128), F32)] + kv_scratch * 2,
        compiler_params=_params(("arbitrary" if kv_shared else "parallel",)))(
            tab, sinks, qa, ka, va, bidx_g, o, lse, do, stats_in)


def _comb_fwd(o_g, lse_g):
    def body(o0, o1, o2, l0, l1, l2, out_ref, outb_ref, lse_ref):
        a0, a1, a2 = l0[...], l1[...], l2[...]
        m = jnp.maximum(jnp.maximum(a0, a1), a2)
        e0, e1, e2 = jnp.exp(a0 - m), jnp.exp(a1 - m), jnp.exp(a2 - m)
        tot = e0 + e1 + e2
        out = (e0 * o0[...] + e1 * o1[...] + e2 * o2[...]) / tot
        out_ref[...] = out
        outb_ref[...] = out.astype(BF16)
        lse_ref[...] = m + jnp.log(tot)

    spec = _row_spec(4 * HD)
    groups = [pl.BlockSpec((TR, 4 * HD), functools.partial(lambda i, g: (i, g), g=g)) for g in range(len(A_GROUPS))]
    f32 = jax.ShapeDtypeStruct((S, 4 * HD), F32)
    return pl.pallas_call(
        body, name="comb_fwd", grid=(S // TR,), in_specs=groups + groups, out_specs=[spec] * 3,
        out_shape=[f32, jax.ShapeDtypeStruct((S, 4 * HD), BF16), f32],
        compiler_params=_params(("parallel",)))(o_g, o_g, o_g, lse_g, lse_g, lse_g)


def _split2(x):
    hi = x.astype(BF16)
    return hi, (x - hi.astype(F32)).astype(BF16)


KB = 2 * BLK
SBQ = 2 * BLK


def _tri_sum(x, tri):
    hi, lo = _split2(x)
    both = _dot(jnp.concatenate([hi, lo], axis=0), tri, NN)
    return both[:x.shape[0]] + both[x.shape[0]:]


def _tri(strict_upper):
    r = lax.broadcasted_iota(jnp.int32, (KB, KB), 0)
    c = lax.broadcasted_iota(jnp.int32, (KB, KB), 1)
    return jnp.where(r > c if strict_upper else r < c, 1.0, 0.0).astype(BF16)


def _sb_terms(qs, kj, before):
    z = _dot(qs, kj, NT)
    lsp = jnp.minimum(z, 0.0) - jnp.log(1.0 + jnp.exp(-jnp.abs(z)))
    return lsp, _sb_keep(before, lsp - z)


def _sb_keep(before, x):
    return x if before is None else jnp.where(before, x, 0.0)


def _sb_before(i, m):
    t = (lax.broadcasted_iota(jnp.int32, (2 * SBQ, KB), 0) & (SBQ - 1)) + i * SBQ
    s = lax.broadcasted_iota(jnp.int32, (2 * SBQ, KB), 1) + m * KB
    return s < t


C_COL = 3072 // 128


def _sb_fwd(proj):
    blk = lambda off: pl.BlockSpec((SBQ, 128), lambda p, i: (i, off + p))
    col = lambda off: pl.BlockSpec((S, 128), lambda p, i: (0, off + p))
    out = pl.BlockSpec((SBQ, 128), lambda p, i: (i, p))

    def body(q_ref, k_ref, v_ref, o_ref, ob_ref, tot_ref):
        i = pl.program_id(1)
        lo = _lane_lo(SBQ)
        qs = _stack_heads(q_ref[...] * SCALE, lo)
        suffix = _tri(True)

        def step(n, carry, diagonal=False):
            acc, rest = carry
            m = i - n
            rows = pl.ds(pl.multiple_of(m * KB, KB), KB)
            kj, vj = k_ref[rows, :].astype(BF16), v_ref[rows, :].astype(BF16)
            before = _sb_before(i, m) if diagonal else None
            lsp, lk = _sb_terms(qs, kj, before)
            w = _sb_keep(before, jnp.exp(lsp + _tri_sum(lk, suffix) + rest))
            return acc + _dot(w.astype(BF16), vj, NN), rest + jnp.sum(lk, axis=1, keepdims=True)

        first = step(0, (jnp.zeros((2 * SBQ, 128), F32), jnp.zeros((2 * SBQ, 1), F32)), diagonal=True)
        acc, rest = lax.fori_loop(1, i + 1, step, first)
        o = _unstack_heads(acc, lo)
        o_ref[...] = o
        ob_ref[...] = o.astype(BF16)
        tot_ref[...] = _unstack_heads(rest + jnp.zeros((2 * SBQ, 128), F32), lo)

    f32 = jax.ShapeDtypeStruct((S, 4 * HD), F32)
    return pl.pallas_call(
        body, name="sb_fwd", grid=(2, S // SBQ), in_specs=[blk(C_COL), col(C_COL + 2), col(C_COL + 4)],
        out_specs=[out, out, out], out_shape=[f32, jax.ShapeDtypeStruct((S, 4 * HD), BF16), f32],
        compiler_params=_params(("parallel", "arbitrary")))(proj, proj, proj)


def _sb_bwd(proj, tot, do):
    blk = lambda off: pl.BlockSpec((SBQ, 128), lambda p, i: (i, off + p))
    col = lambda off: pl.BlockSpec((S, 128), lambda p, i: (0, off + p))

    def body(q_ref, k_ref, v_ref, tot_ref, do_ref, dq_ref, dk_ref, dv_ref):
        i = pl.program_id(1)

        @pl.when(i == 0)
        def _():
            dk_ref[...] = jnp.zeros_like(dk_ref)
            dv_ref[...] = jnp.zeros_like(dv_ref)

        lo = _lane_lo(SBQ)
        qs = _stack_heads(q_ref[...] * SCALE, lo)
        dos = _stack_heads(do_ref[...], lo)
        tots = jnp.concatenate(_per_head(tot_ref[...], lo), axis=0)
        prefix = _tri(False)

        def step(m, carry, diagonal=False):
            dq, keep_left, g_left = carry
            rows = pl.ds(pl.multiple_of(m * KB, KB), KB)
            kj, vj = k_ref[rows, :].astype(BF16), v_ref[rows, :].astype(BF16)
            before = _sb_before(i, m) if diagonal else None
            lsp, lk = _sb_terms(qs, kj, before)
            log_rest = tots - keep_left - lk - _tri_sum(lk, prefix)
            w = _sb_keep(before, jnp.exp(lsp + log_rest))
            g = w * _dot(dos, vj, NT)
            g_before = g_left + _dot(g.astype(BF16), prefix, NN)
            beta = jnp.exp(lsp)
            dz = _sb_keep(before, g * (1.0 - beta) - g_before * beta).astype(BF16)
            dk_ref[rows, :] += _dot(dz, qs, TN)
            dv_ref[rows, :] += _dot(w.astype(BF16), dos, TN)
            return (dq + _dot(dz, kj, NN), keep_left + jnp.sum(lk, axis=1, keepdims=True),
                    g_left + jnp.sum(g, axis=1, keepdims=True))

        zero = (jnp.zeros((2 * SBQ, 128), F32), jnp.zeros((2 * SBQ, 1), F32), jnp.zeros((2 * SBQ, 1), F32))
        dq, _, _ = step(i, lax.fori_loop(0, i, step, zero), diagonal=True)
        dq_ref[...] = SCALE * _unstack_heads(dq, lo)

    out_blk = pl.BlockSpec((SBQ, 128), lambda p, i: (i, p))
    out_col = pl.BlockSpec((S, 128), lambda p, i: (0, p))
    f32 = jax.ShapeDtypeStruct((S, 4 * HD), F32)
    return pl.pallas_call(
        body, name="sb_bwd", grid=(2, S // SBQ),
        in_specs=[blk(C_COL), col(C_COL + 2), col(C_COL + 4), out_blk, out_blk],
        out_specs=[out_blk, out_col, out_col], out_shape=[f32, f32, f32],
        compiler_params=_params(("arbitrary", "arbitrary")))(proj, proj, proj, tot, do)


TG = 256
TGR = 1024
GATE_BLK0 = OFF_GATE // TG


def _gate_specs():
    grid = (D // TG, S // TGR)
    p_specs = [pl.BlockSpec((TGR, TG), functools.partial(lambda c, r, br: (r, GATE_BLK0 + br * (D // TG) + c), br=br))
               for br in range(3)]
    b_spec = pl.BlockSpec((3, TG), lambda c, r: (0, c))
    t_spec = pl.BlockSpec((TGR, TG), lambda c, r: (r, c))
    return grid, p_specs, b_spec, t_spec


def _sigmoid(x):
    return 1.0 / (1.0 + jnp.exp(-x))


def _three_rows(rows):
    sub = lax.broadcasted_iota(jnp.int32, (3, rows[0].shape[1]), 0)
    return jnp.where(sub == 0, rows[0], jnp.where(sub == 1, rows[1], rows[2]))


def _gate_fwd(proj, b_gate, br):
    grid, p_specs, b_spec, t_spec = _gate_specs()

    def body(p0, p1, p2, b_ref, r0, r1, r2, out_ref):
        acc = jnp.zeros((TGR, TG), F32)
        for n, (p, r) in enumerate(((p0, r0), (p1, r1), (p2, r2))):
            acc += _sigmoid(p[...] + b_ref[n:n + 1, :]) * r[...]
        out_ref[...] = acc.astype(BF16)

    return pl.pallas_call(
        body, name="gate_fwd", grid=grid, in_specs=p_specs + [b_spec] + [t_spec] * 3, out_specs=t_spec,
        out_shape=jax.ShapeDtypeStruct((S, D), BF16),
        compiler_params=_params(("parallel", "parallel")))(proj, proj, proj, b_gate, *br)


def _gate_bwd(proj, b_gate, br, dmerged):
    grid, p_specs, b_spec, t_spec = _gate_specs()

    def body(p0, p1, p2, b_ref, r0, r1, r2, dm_ref, e0, e1, e2, g0, g1, g2, db_ref):
        dm = dm_ref[...]
        rows = []
        for n, (p, r, e_ref, dg_ref) in enumerate(((p0, r0, e0, g0), (p1, r1, e1, g1), (p2, r2, e2, g2))):
            g = _sigmoid(p[...] + b_ref[n:n + 1, :])
            e_ref[...] = (dm * g).astype(BF16)
            dpre = dm * r[...] * g * (1.0 - g)
            dg_ref[...] = dpre.astype(BF16)
            rows.append(jnp.sum(dpre, axis=0, keepdims=True))
        db = _three_rows(rows)

        @pl.when(pl.program_id(1) == 0)
        def _():
            db_ref[...] = db

        @pl.when(pl.program_id(1) > 0)
        def _():
            db_ref[...] += db

    bf = jax.ShapeDtypeStruct((S, D), BF16)
    out = pl.pallas_call(
        body, name="gate_bwd", grid=grid, in_specs=p_specs + [b_spec] + [t_spec] * 4,
        out_specs=[t_spec] * 6 + [b_spec], out_shape=[bf] * 6 + [jax.ShapeDtypeStruct((3, D), F32)],
        compiler_params=_params(("parallel", "arbitrary")))(proj, proj, proj, b_gate, *br, dmerged)
    return out[:3], out[3:6], out[6]


TC = 256
N_FF_BLK = D_FF // TC
GELU_C = math.sqrt(2.0 / math.pi)


def _shift_down(x, n):
    rows = lax.broadcasted_iota(jnp.int32, x.shape, 0)
    return jnp.where(rows >= n, pltpu.roll(x, n, axis=0), 0.0)


def _shift_up(x, n):
    rows = lax.broadcasted_iota(jnp.int32, x.shape, 0)
    return jnp.where(rows < x.shape[0] - n, pltpu.roll(x, x.shape[0] - n, axis=0), 0.0)


def _conv(u, w, b):
    s1, s2 = _shift_down(u, 1), _shift_down(u, 2)
    return w[2:3, :] * u + w[1:2, :] * s1 + w[0:1, :] * s2 + b, s1, s2


def _gelu_parts(x):
    inner = GELU_C * (x + 0.044715 * x * x * x)
    t = jnp.tanh(inner)
    gelu = 0.5 * x * (1.0 + t)
    dgelu = 0.5 * (1.0 + t) + 0.5 * x * (1.0 - t * t) * GELU_C * (1.0 + 3 * 0.044715 * x * x)
    return gelu, dgelu


def _conv_specs():
    ug = pl.BlockSpec((S, TC), lambda c: (0, c))
    uv = pl.BlockSpec((S, TC), lambda c: (0, N_FF_BLK + c))
    wg = pl.BlockSpec((3, TC), lambda c: (0, c))
    wv = pl.BlockSpec((3, TC), lambda c: (0, N_FF_BLK + c))
    bg = pl.BlockSpec((1, TC), lambda c: (0, c))
    bv = pl.BlockSpec((1, TC), lambda c: (0, N_FF_BLK + c))
    return ug, uv, wg, wv, bg, bv


def _conv_fwd(u, conv_w, conv_b):
    ug, uv, wg, wv, bg, bv = _conv_specs()

    def body(ug_ref, uv_ref, wg_ref, wv_ref, bg_ref, bv_ref, a_ref):
        gc = _conv(ug_ref[...], wg_ref[...], bg_ref[...])[0]
        vc = _conv(uv_ref[...], wv_ref[...], bv_ref[...])[0]
        a_ref[...] = (_gelu_parts(gc)[0] * vc).astype(BF16)

    return pl.pallas_call(
        body, name="conv_fwd", grid=(N_FF_BLK,), in_specs=[ug, uv, wg, wv, bg, bv], out_specs=ug,
        out_shape=jax.ShapeDtypeStruct((S, D_FF), BF16),
        compiler_params=_params(("parallel",)))(u, u, conv_w, conv_w, conv_b, conv_b)


def _conv_bwd(u, conv_w, conv_b, da):
    ug, uv, wg, wv, bg, bv = _conv_specs()

    def back(duc, u, s1, s2, w):
        du = w[2:3, :] * duc + w[1:2, :] * _shift_up(duc, 1) + w[0:1, :] * _shift_up(duc, 2)
        dw = _three_rows([jnp.sum(duc * s2, axis=0, keepdims=True), jnp.sum(duc * s1, axis=0, keepdims=True),
                          jnp.sum(duc * u, axis=0, keepdims=True)])
        return du, dw, jnp.sum(duc, axis=0, keepdims=True)

    def body(ug_ref, uv_ref, wg_ref, wv_ref, bg_ref, bv_ref, da_ref, dug_ref, duv_ref, dwg_ref, dwv_ref, dbg_ref, dbv_ref):
        u_g, u_v = ug_ref[...], uv_ref[...]
        gc, g1, g2 = _conv(u_g, wg_ref[...], bg_ref[...])
        vc, v1, v2 = _conv(u_v, wv_ref[...], bv_ref[...])
        gelu, dgelu = _gelu_parts(gc)
        da = da_ref[...]
        du, dw, db = back(da * vc * dgelu, u_g, g1, g2, wg_ref[...])
        dug_ref[...] = du.astype(BF16)
        dwg_ref[...] = dw
        dbg_ref[...] = db
        du, dw, db = back(da * gelu, u_v, v1, v2, wv_ref[...])
        duv_ref[...] = du.astype(BF16)
        dwv_ref[...] = dw
        dbv_ref[...] = db

    return pl.pallas_call(
        body, name="conv_bwd", grid=(N_FF_BLK,), in_specs=[ug, uv, wg, wv, bg, bv, ug],
        out_specs=[ug, ug, wg, wg, bg, bg],
        out_shape=[jax.ShapeDtypeStruct((S, D_FF), BF16), jax.ShapeDtypeStruct((S, D_FF), BF16),
                   jax.ShapeDtypeStruct((3, D_FF), F32), jax.ShapeDtypeStruct((3, D_FF), F32),
                   jax.ShapeDtypeStruct((1, D_FF), F32), jax.ShapeDtypeStruct((1, D_FF), F32)],
        compiler_params=_params(("parallel",)))(u, u, conv_w, conv_w, conv_b, conv_b, da)


def _adamw(name, w, g, m, v):
    shape = w.shape
    cols = shape[-1]
    flat = [t.reshape(-1, cols) for t in (w, g, m, v)]
    r = flat[0].shape[0]
    tr = min(r, max(8, 2 * 1024 * 1024 // (4 * cols)))

    def body(w_ref, g_ref, m_ref, v_ref, go_ref, d_ref, mo_ref, vo_ref):
        g = g_ref[...]
        go_ref[...] = g
        m = ADAM_B1 * m_ref[...] + (1.0 - ADAM_B1) * g
        v = ADAM_B2 * v_ref[...] + (1.0 - ADAM_B2) * (g * g)
        m_hat = m / (1.0 - ADAM_B1 ** ADAM_STEP)
        v_hat = v / (1.0 - ADAM_B2 ** ADAM_STEP)
        d_ref[...] = -ADAM_LR * (m_hat / (jnp.sqrt(v_hat) + ADAM_EPS) + ADAM_WD * w_ref[...])
        mo_ref[...] = m
        vo_ref[...] = v

    spec = pl.BlockSpec((tr, cols), lambda i: (i, 0))
    outs = pl.pallas_call(
        body, name=name, grid=(pl.cdiv(r, tr),), in_specs=[spec] * 4, out_specs=[spec] * 4,
        out_shape=[jax.ShapeDtypeStruct((r, cols), F32)] * 4, compiler_params=_params(("parallel",)))(*flat)
    return [t.reshape(shape) for t in outs]


def _place():
    x, y, c = lax.axis_index("x"), lax.axis_index("y"), lax.axis_index("c")
    chips = [(1 - x, y), (x, 1 - y), (1 - x, 1 - y)]
    return x, y, c, chips


def _scalars(*vals):
    return jnp.stack([jnp.asarray(v, jnp.int32) for v in vals])


HBM = pl.BlockSpec(memory_space=pltpu.HBM)
SEM = pl.BlockSpec(memory_space=pltpu.SEMAPHORE)
SPLIT_COPY = pltpu.CompilerParams(has_side_effects=pltpu.SideEffectType.DATAFLOW_SIDE_EFFECTING)


def _in_hbm(x):
    return pltpu.with_memory_space_constraint(x, pltpu.HBM)


def _cast_into_slot(name, w, layer, chip):
    _, k, n4 = w.shape
    tr = max(t for t in range(16, 513, 16) if k % t == 0)

    def body(chip_ref, w_ref, o_ref):
        o_ref[...] = w_ref[...].astype(BF16)

    return pl.pallas_call(
        body, name=name,
        grid_spec=pltpu.PrefetchScalarGridSpec(
            num_scalar_prefetch=1, grid=(k // tr,),
            in_specs=[pl.BlockSpec((None, tr, n4), lambda i, s: (layer, i, 0))],
            out_specs=pl.BlockSpec((None, tr, n4), lambda i, s: (s[0], i, 0))),
        out_shape=jax.ShapeDtypeStruct((N_CHIPS, k, n4), BF16),
        compiler_params=_params(("parallel",)))(_scalars(chip), w)


def _gather_copy(buf_ref, k, from_chip, send_sem, recv_sem, chips, c, half=False):
    rows = buf_ref.at[from_chip]
    if half:
        h = buf_ref.shape[1] // 2
        rows = buf_ref.at[from_chip, pl.ds(pl.multiple_of(c * h, h), h)]
    return pltpu.make_async_remote_copy(src_ref=rows, dst_ref=rows, send_sem=send_sem, recv_sem=recv_sem,
                                        device_id=(*chips[k], c), device_id_type=MESH)


def _gather_start(name, bufs, groups, halved=()):
    n, ng = len(bufs), len(groups)
    where = {a: (gi, e) for gi, g in enumerate(groups) for e, a in enumerate(g)}

    def body(*refs):
        ins, sems, token = refs[:n], refs[n:n + 2 * ng], refs[-1]
        x, y, c, chips = _place()
        for a in range(n):
            gi, e = where[a]
            for k in range(3):
                _gather_copy(ins[a], k, 2 * x + y, sems[2 * gi].at[3 * e + k], sems[2 * gi + 1].at[3 * e + k],
                             chips, c, a in halved).start()
        token[...] = jnp.zeros_like(token)

    out_shape = [pltpu.SemaphoreType.DMA((3 * len(g),)) for g in groups for _ in range(2)]
    out_shape += [pltpu.HBM(b.shape, b.dtype) for b in bufs] + [jax.ShapeDtypeStruct((8, 128), F32)]
    out = pl.pallas_call(
        body, name=name, in_specs=[HBM] * n,
        out_specs=[SEM] * (2 * ng) + [HBM] * n + [pl.BlockSpec(memory_space=pltpu.VMEM)], out_shape=out_shape,
        input_output_aliases={a: 2 * ng + a for a in range(n)}, compiler_params=SPLIT_COPY)(*[_in_hbm(b) for b in bufs])
    sems = [(out[2 * gi], out[2 * gi + 1]) for gi in range(ng)]
    return sems, list(out[2 * ng:2 * ng + n]), out[-1]


def _gather_wait(name, bufs, send, recv, after, halved=()):
    n = len(bufs)

    def body(*refs):
        ins, send_sem, recv_sem = refs[:n], refs[n], refs[n + 1]
        x, y, c, chips = _place()
        for e in range(n):
            for k in range(3):
                sems = (send_sem.at[3 * e + k], recv_sem.at[3 * e + k])
                _gather_copy(ins[e], k, 2 * x + y, *sems, chips, c, e in halved).wait_send()
                _gather_copy(ins[e], k, 2 * chips[k][0] + chips[k][1], *sems, chips, c, e in halved).wait_recv()

    return pl.pallas_call(
        body, name=name, in_specs=[HBM] * n + [SEM, SEM, ANY], out_specs=[HBM] * n,
        out_shape=[pltpu.HBM(b.shape, b.dtype) for b in bufs],
        input_output_aliases={a: a for a in range(n)}, compiler_params=SPLIT_COPY)(*bufs, send, recv, after)


def _swap_halves(name, bufs):
    n = len(bufs)

    def body(*refs):
        ins, outs = refs[:n], refs[n:2 * n]
        send_sem, recv_sem = refs[2 * n:]
        x, y, c, chips = _place()

        def piece(ref, k, which):
            h = ref.shape[1] // 2
            return ref.at[2 * chips[k][0] + chips[k][1], pl.ds(pl.multiple_of(which * h, h), h)]

        def copy(a, k, which):
            return pltpu.make_async_remote_copy(
                src_ref=piece(ins[a], k, c), dst_ref=piece(outs[a], k, which), send_sem=send_sem.at[3 * a + k],
                recv_sem=recv_sem.at[3 * a + k], device_id=(x, y, 1 - c), device_id_type=MESH)

        for a in range(n):
            for k in range(3):
                copy(a, k, c).start()
        for a in range(n):
            for k in range(3):
                copy(a, k, c).wait_send()
                copy(a, k, 1 - c).wait_recv()

    return pl.pallas_call(
        body, name=name, in_specs=[ANY] * n, out_specs=[ANY] * n,
        out_shape=[jax.ShapeDtypeStruct(b.shape, b.dtype) for b in bufs],
        input_output_aliases={a: a for a in range(n)},
        scratch_shapes=[pltpu.SemaphoreType.DMA((3 * n,)), pltpu.SemaphoreType.DMA((3 * n,))],
    )(*bufs)


def _reduce_copy(g_ref, land_ref, mask, send_sem, recv_sem, x, y, c, sending):
    px, py, pc = x ^ ((mask >> 2) & 1), y ^ ((mask >> 1) & 1), c ^ (mask & 1)
    half = g_ref.shape[1] // 2
    src = g_ref.at[2 * px + py, pl.ds(pl.multiple_of(pc * half, half), half)]
    dst = land_ref.at[4 * x + 2 * y + c] if sending else land_ref.at[4 * px + 2 * py + pc]
    return pltpu.make_async_remote_copy(src_ref=src, dst_ref=dst, send_sem=send_sem, recv_sem=recv_sem,
                                        device_id=(px, py, pc), device_id_type=MESH)


def _reduce_start(name, grads):
    n = len(grads)
    lands = [lax.empty((N_DEV, g.shape[1] // 2, g.shape[2]), g.dtype) for g in grads]

    def body(*refs):
        gs, ls, send_sem, recv_sem = refs[:n], refs[n:2 * n], refs[2 * n], refs[2 * n + 1]
        x, y, c, _ = _place()
        for a in range(n):
            for mask in range(1, N_DEV):
                s = (N_DEV - 1) * a + mask - 1
                _reduce_copy(gs[a], ls[a], mask, send_sem.at[s], recv_sem.at[s], x, y, c, True).start()
        refs[-1][...] = jnp.zeros_like(refs[-1])

    sem = pltpu.SemaphoreType.DMA((n * (N_DEV - 1),))
    out = pl.pallas_call(
        body, name=name, in_specs=[HBM] * (2 * n),
        out_specs=[SEM, SEM] + [HBM] * (2 * n) + [pl.BlockSpec(memory_space=pltpu.VMEM)],
        out_shape=[sem, sem] + [pltpu.HBM(t.shape, t.dtype) for t in grads + lands] + [jax.ShapeDtypeStruct((8, 128), F32)],
        input_output_aliases={a: 2 + a for a in range(2 * n)}, compiler_params=SPLIT_COPY)(
            *[_in_hbm(t) for t in grads + lands])
    return out[0], out[1], list(out[2:2 + n]), list(out[2 + n:2 + 2 * n]), out[-1]


def _reduce_wait(name, send, recv, grads, lands, after):
    n = len(grads)

    def body(*refs):
        gs, ls, send_sem, recv_sem = refs[:n], refs[n:2 * n], refs[2 * n], refs[2 * n + 1]
        x, y, c, _ = _place()
        for a in range(n):
            for mask in range(1, N_DEV):
                s = (N_DEV - 1) * a + mask - 1
                sems = (send_sem.at[s], recv_sem.at[s])
                _reduce_copy(gs[a], ls[a], mask, *sems, x, y, c, True).wait_send()
                _reduce_copy(gs[a], ls[a], mask, *sems, x, y, c, False).wait_recv()

    out = pl.pallas_call(
        body, name=name, in_specs=[HBM] * (2 * n) + [SEM, SEM, ANY], out_specs=[HBM] * (2 * n),
        out_shape=[pltpu.HBM(t.shape, t.dtype) for t in grads + lands],
        input_output_aliases={a: a for a in range(2 * n)}, compiler_params=SPLIT_COPY)(*grads, *lands, send, recv, after)
    return list(out[:n]), list(out[n:])


def _reduce_sum(name, g, land, layer, into, chip, c):
    _, k4, n4 = g.shape
    half = k4 // 2
    tr = max(t for t in range(16, 513, 16) if half % t == 0)
    per = half // tr
    me = 2 * chip + c

    def body(s_ref, own_ref, *refs):
        total = own_ref[...].astype(F32)
        for ref in refs[:N_DEV - 1]:
            total = total + ref[...].astype(F32)
        refs[-1][...] = total

    in_specs = [pl.BlockSpec((None, tr, n4), lambda i, s: (s[0], s[1] * per + i, 0))]
    in_specs += [pl.BlockSpec((None, tr, n4), functools.partial(lambda i, s, m: (s[1 + m], i, 0), m=m))
                 for m in range(1, N_DEV)]
    ins = [g] + [land] * (N_DEV - 1)
    aliases = {}
    if into is not None:
        in_specs, ins, aliases = in_specs + [ANY], ins + [into], {1 + N_DEV: 0}
    return pl.pallas_call(
        body, name=name,
        grid_spec=pltpu.PrefetchScalarGridSpec(
            num_scalar_prefetch=1, grid=(per,), in_specs=in_specs,
            out_specs=pl.BlockSpec((None, tr, n4), lambda i, s: (layer, s[1] * per + i, 0))),
        out_shape=jax.ShapeDtypeStruct((DEPTH, k4, n4), F32), input_output_aliases=aliases,
        compiler_params=_params(("parallel",)))(_scalars(chip, c, *[me ^ m for m in range(1, N_DEV)]), *ins)


def _join_halves(name, bufs):
    n = len(bufs)

    def body(*refs):
        ins, outs = refs[:n], refs[n:2 * n]
        send_sem, recv_sem = refs[2 * n:]
        x, y, c, _ = _place()

        def rows(ref, which):
            half = ref.shape[1] // 2
            return ref.at[:, pl.ds(pl.multiple_of(which * half, half), half)]

        sends = [pltpu.make_async_remote_copy(
            src_ref=rows(ins[a], c), dst_ref=rows(outs[a], c), send_sem=send_sem.at[a], recv_sem=recv_sem.at[a],
            device_id=(x, y, 1 - c), device_id_type=MESH) for a in range(n)]
        for cp in sends:
            cp.start()
        for a in range(n):
            sends[a].wait_send()
            pltpu.make_async_remote_copy(
                src_ref=rows(ins[a], c), dst_ref=rows(outs[a], 1 - c), send_sem=send_sem.at[a], recv_sem=recv_sem.at[a],
                device_id=(x, y, 1 - c), device_id_type=MESH).wait_recv()

    return pl.pallas_call(
        body, name=name, in_specs=[ANY] * n, out_specs=[ANY] * n,
        out_shape=[jax.ShapeDtypeStruct(b.shape, b.dtype) for b in bufs],
        input_output_aliases={a: a for a in range(n)},
        scratch_shapes=[pltpu.SemaphoreType.DMA((n,)), pltpu.SemaphoreType.DMA((n,))],
    )(*bufs)


def _small_copy(b_ref, l_ref, mask, send_sem, recv_sem, x, y, c, sending):
    px, py, pc = x ^ ((mask >> 2) & 1), y ^ ((mask >> 1) & 1), c ^ (mask & 1)
    dst = l_ref.at[4 * x + 2 * y + c] if sending else l_ref.at[4 * px + 2 * py + pc]
    return pltpu.make_async_remote_copy(src_ref=b_ref, dst_ref=dst, send_sem=send_sem.at[mask - 1],
                                        recv_sem=recv_sem.at[mask - 1], device_id=(px, py, pc), device_id_type=MESH)


def _small_start(block):
    land = lax.empty((N_DEV,) + block.shape, block.dtype)

    def body(b_ref, l_ref, send_sem, recv_sem, b_thru, l_thru, token):
        x, y, c, _ = _place()
        for mask in range(1, N_DEV):
            _small_copy(b_ref, l_ref, mask, send_sem, recv_sem, x, y, c, True).start()
        token[...] = jnp.zeros_like(token)

    sem = pltpu.SemaphoreType.DMA((N_DEV - 1,))
    return pl.pallas_call(
        body, name="small_start", in_specs=[HBM, HBM],
        out_specs=[SEM, SEM, HBM, HBM, pl.BlockSpec(memory_space=pltpu.VMEM)],
        out_shape=[sem, sem, pltpu.HBM(block.shape, block.dtype), pltpu.HBM(land.shape, land.dtype),
                   jax.ShapeDtypeStruct((8, 128), F32)],
        input_output_aliases={0: 2, 1: 3}, compiler_params=SPLIT_COPY)(_in_hbm(block), _in_hbm(land))


def _small_wait(send, recv, block, land, after):
    def body(b_ref, l_ref, send_sem, recv_sem, after_ref, b_out, l_out):
        x, y, c, _ = _place()
        for mask in range(1, N_DEV):
            _small_copy(b_ref, l_ref, mask, send_sem, recv_sem, x, y, c, True).wait_send()
            _small_copy(b_ref, l_ref, mask, send_sem, recv_sem, x, y, c, False).wait_recv()

    return pl.pallas_call(
        body, name="small_wait", in_specs=[HBM, HBM, SEM, SEM, ANY], out_specs=[HBM, HBM],
        out_shape=[pltpu.HBM(block.shape, block.dtype), pltpu.HBM(land.shape, land.dtype)],
        input_output_aliases={0: 0, 1: 1}, compiler_params=SPLIT_COPY)(block, land, send, recv, after)


def _small_sum(land):
    def body(l_ref, out_ref):
        total = l_ref[0]
        for d in range(1, N_DEV):
            total = total + l_ref[d]
        out_ref[...] = total

    vmem = pl.BlockSpec(memory_space=pltpu.VMEM)
    return pl.pallas_call(
        body, name="small_sum", in_specs=[vmem], out_specs=vmem,
        out_shape=jax.ShapeDtypeStruct(land.shape[1:], F32),
        compiler_params=pltpu.CompilerParams(vmem_limit_bytes=VMEM_LIMIT))(land)


B_Q_COL = 2304 // 128
B_K0, B_V0 = 2816, 2944


def _full_cols(w_g):
    return w_g.transpose(1, 0, 2).reshape(w_g.shape[1], -1)


A_DILS = tuple(d for _, d in A_GROUPS)
A_PAIRS = N_A // 2


def _src_a(proj):
    return ((proj, 0), (proj, A_PAIRS), (proj, 2 * A_PAIRS))


def _mixer_fwd(h1, wget, rel_bias, sinks_l, bidx):
    w = dict(wget(0, h1))
    proj = _mm_nt("proj_in", h1, w["w_in"], F32, tm=S, tn=1152)
    no_sinks = jnp.full((N_A,), NEG, F32)
    o_g, lse_g = _band_fwd("band_fwd_a", A_DILS, A_PAIRS, BLK, 0, _src_a(proj), bidx[:3], rel_bias, no_sinks)
    o_a32, o_a, lse_a = _comb_fwd(o_g, lse_g)
    src_b = ((proj, B_Q_COL), (proj, B_K0 // 128), (proj, B_V0 // 128))
    o_b32, lse_b = _band_fwd("band_fwd_b", (1,), 4, BLK - 1, N_A, src_b, bidx[3:], rel_bias, sinks_l, kv_shared=True)
    o_b = o_b32.astype(BF16)
    o_c32, o_c, tot_c = _sb_fwd(proj)
    w.update(wget(1, o_c32))
    br = [_mm_nn("branch_a", o_a, w["w_br_a"], F32, tm=S), _mm_nn("branch_b", o_b, w["w_br_b"], F32, tm=S),
          _mm_nn("branch_c", o_c, w["w_br_c"], F32, tm=S)]
    merged = _gate_fwd(proj, w["b_gate"], br)
    mo = _mm_nn("out_proj", merged, w["w_out"], F32, tm=S)
    saved = dict(proj=proj, src_b=src_b, o_a32=o_a32, lse_a=lse_a, o_b32=o_b32, lse_b=lse_b, tot_c=tot_c,
                 o_a=o_a, o_b=o_b, o_c=o_c, br=br, merged=merged)
    return mo, saved, w


def _mixer_bwd(d_mo, h1, w, sv, rel_bias, sinks_l, bidx, stats_in, emit):
    grads = {}
    dmerged = _mm_nt("out_proj_dx", d_mo, w["w_out"], F32, tm=S)
    grads["w_out"] = _mm_tn_sharded("out_proj_dw", sv["merged"], d_mo, True)
    e, dgate, db_gate = _gate_bwd(sv["proj"], w["b_gate"], sv["br"], dmerged)
    grads["b_gate"] = db_gate
    d_o = {}
    for n, name in enumerate("abc"):
        d_o[name] = _mm_nt("branch_%s_dx" % name, e[n], w["w_br_" + name], F32, tm=S)
        grads["w_br_" + name] = _mm_tn_sharded("branch_%s_dw" % name, sv["o_" + name], e[n], False)
    zero = emit(1, grads)
    no_sinks = jnp.full((N_A,), NEG, F32) + zero[0]
    dq_a, dk_a, dv_a, st_a = _band_bwd("band_bwd_a", A_DILS, A_PAIRS, BLK, 0, _src_a(sv["proj"]), bidx[:3], rel_bias,
                                       no_sinks, sv["o_a32"], sv["lse_a"], d_o["a"], stats_in[:N_A])
    dq_b, dk_b, dv_b, st_b = _band_bwd("band_bwd_b", (1,), 4, BLK - 1, N_A, sv["src_b"], bidx[3:], rel_bias, sinks_l,
                                       sv["o_b32"], sv["lse_b"], d_o["b"], stats_in[N_A:], kv_shared=True)
    stats = jnp.concatenate([st_a, st_b], axis=0)
    dcq, dck, dcv = _sb_bwd(sv["proj"], sv["tot_c"], d_o["c"])
    cols = [dq_a, dk_a, dv_a, dq_b, dk_b, dv_b, dcq, dck, dcv]
    dproj = jnp.concatenate([t.astype(BF16) for t in cols] + list(dgate), axis=1)
    grads["w_in"] = _mm_tn("proj_in_dw", dproj, h1, BF16, tm=1152, tn=1024).reshape(N_CHIPS, IN_SHARD, D)
    zero = emit(2, grads)
    dh1 = _mm_nn("proj_in_dx", dproj, w["w_in"], F32, tm=S, tk=2304)
    return dh1, grads, stats, zero


def _ffn_fwd(h2, w):
    u = _mm_nn("ffn_up", h2, w["w_up"], F32, tm=S, tn=1024)
    a = _conv_fwd(u, w["conv_w"], w["conv_b"])
    dn = _mm_nn("ffn_down", a, w["w_down"], F32, tm=1024)
    return dn, dict(u=u, a=a)


def _ffn_bwd(d_dn, h2, w, sv):
    grads = {}
    da = _mm_nt("ffn_down_dx", d_dn, w["w_down"], F32, tm=S, tn=1024)
    grads["w_down"] = _mm_tn_sharded("ffn_down_dw", sv["a"], d_dn, True, tm=1024, tn=1024)
    dug, duv, dwg, dwv, dbg, dbv = _conv_bwd(sv["u"], w["conv_w"], w["conv_b"], da)
    du = jnp.concatenate([dug, duv], axis=1)
    grads["conv_w"] = jnp.concatenate([dwg, dwv], axis=1)
    grads["conv_b"] = jnp.concatenate([dbg, dbv], axis=1)
    dh2 = _mm_nt("ffn_up_dx", du, w["w_up"], F32, tm=S, tk=2048)
    grads["w_up"] = _mm_tn_sharded("ffn_up_dw", h2, du, False, tm=1024, tn=1024)
    return dh2, grads


BIG = ("w_in", "w_br_a", "w_br_b", "w_br_c", "w_out", "w_up", "w_down")


def _shard_view(name, w):
    return jnp.swapaxes(w, 1, 2) if name == "w_in" else w
WEIGHT_GROUPS = (("w_in", "b_gate"), ("w_br_a", "w_br_b", "w_br_c", "w_out"), ("w_up", "conv_w", "w_down"))
GRAD_GROUPS = (("w_down", "w_up"), ("w_out", "w_br_a", "w_br_b", "w_br_c"), ("w_in",))
SMALL_ROWS = (("rel_bias", 8), ("attn_pre_norm", 16), ("attn_post_norm", 16), ("ffn_pre_norm", 16), ("ffn_post_norm", 16),
              ("sinks", 8), ("conv_b", 128), ("b_gate", 48), ("conv_w", 384), ("loss", 8))


def _pack_small(vals):
    rows = []
    for name, n in SMALL_ROWS:
        flat = vals[name].reshape(-1).astype(F32)
        rows.append(jnp.pad(flat, (0, n * 128 - flat.shape[0])).reshape(n, 128))
    return jnp.concatenate(rows, axis=0)


def _unpack_small(block, shapes):
    out, row = {}, 0
    for name, n in SMALL_ROWS:
        size = int(np.prod(shapes[name]))
        out[name] = block[row:row + n].reshape(-1)[:size].reshape(shapes[name])
        row += n
    return out


def kernel(x, rel_bias, attn_pre_norm, w_in, b_gate, sinks, w_br_a, w_br_b, w_br_c, w_out, attn_post_norm, ffn_pre_norm, w_up, conv_w, conv_b, w_down, ffn_post_norm, loss_target, m_rel_bias, m_attn_pre_norm, m_w_in, m_b_gate, m_sinks, m_w_br_a, m_w_br_b, m_w_br_c, m_w_out, m_attn_post_norm, m_ffn_pre_norm, m_w_up, m_conv_w, m_conv_b, m_w_down, m_ffn_post_norm, v_rel_bias, v_attn_pre_norm, v_w_in, v_b_gate, v_sinks, v_w_br_a, v_w_br_b, v_w_br_c, v_w_out, v_attn_post_norm, v_ffn_pre_norm, v_w_up, v_conv_w, v_conv_b, v_w_down, v_ffn_post_norm):
    names = ("rel_bias", "attn_pre_norm", "w_in", "b_gate", "sinks", "w_br_a", "w_br_b", "w_br_c", "w_out",
             "attn_post_norm", "ffn_pre_norm", "w_up", "conv_w", "conv_b", "w_down", "ffn_post_norm")
    weights = dict(zip(names, (rel_bias, attn_pre_norm, w_in, b_gate, sinks, w_br_a, w_br_b, w_br_c, w_out,
                               attn_post_norm, ffn_pre_norm, w_up, conv_w, conv_b, w_down, ffn_post_norm)))
    mom1 = dict(zip(names, (m_rel_bias, m_attn_pre_norm, m_w_in, m_b_gate, m_sinks, m_w_br_a, m_w_br_b, m_w_br_c,
                            m_w_out, m_attn_post_norm, m_ffn_pre_norm, m_w_up, m_conv_w, m_conv_b, m_w_down,
                            m_ffn_post_norm)))
    mom2 = dict(zip(names, (v_rel_bias, v_attn_pre_norm, v_w_in, v_b_gate, v_sinks, v_w_br_a, v_w_br_b, v_w_br_c,
                            v_w_out, v_attn_post_norm, v_ffn_pre_norm, v_w_up, v_conv_w, v_conv_b, v_w_down,
                            v_ffn_post_norm)))

    chip = 2 * lax.axis_index("x") + lax.axis_index("y")
    core = lax.axis_index("c")

    keys = [(n, l) for l in range(DEPTH) for group in WEIGHT_GROUPS for n in group]
    groups = [[keys.index((n, l)) for n in group] for l in range(DEPTH) for group in WEIGHT_GROUPS]

    def slot_buffer(n, l):
        if n in BIG:
            return _cast_into_slot("cast_" + n, _shard_view(n, weights[n]), l, chip)
        shard = weights[n][l]
        return lax.dynamic_update_slice(jnp.zeros((N_CHIPS,) + shard.shape, F32), shard[None],
                                        (chip, jnp.int32(0), jnp.int32(0)))

    by_halves = [keys.index(k) for k in (("w_in", 0), ("w_up", DEPTH - 1), ("w_down", DEPTH - 1))]
    n_first = len(groups[0])
    sems, in_flight, first = _gather_start("gather_start_first", [slot_buffer(*k) for k in keys[:n_first]], groups[:1],
                                           tuple(a for a in by_halves if a < n_first))
    more = _gather_start("gather_start", [slot_buffer(*k) for k in keys[n_first:]],
                         [[a - n_first for a in g] for g in groups[1:]],
                         tuple(a - n_first for a in by_halves if a >= n_first))
    sems, in_flight, started = sems + more[0], in_flight + more[1], more[2]

    def wget(l, gi, after):
        g = l * len(WEIGHT_GROUPS) + gi
        after = started if g == 0 else after
        halved = tuple(e for e, a in enumerate(groups[g]) if a in by_halves)
        got = list(_gather_wait("gather_wait_%d_%d" % (l, gi), [in_flight[a] for a in groups[g]], *sems[g], after,
                                halved))
        if halved:
            for e, buf in zip(halved, _swap_halves("swap_halves_%d_%d" % (l, gi), [got[e] for e in halved])):
                got[e] = buf
        out = {}
        for n, buf in zip(WEIGHT_GROUPS[gi], got):
            if n in ("w_in", "w_out", "w_down"):
                out[n] = buf.reshape(-1, buf.shape[-1])
            else:
                out[n] = buf if n == "w_up" else _full_cols(buf)
        if gi == len(WEIGHT_GROUPS) - 1:
            out["conv_b"] = conv_b[l:l + 1]
        return out

    pending = []

    def emit(l, gi, grads):
        group = GRAD_GROUPS[gi]
        *started, token = _reduce_start("reduce_start_%d_%d" % (l, gi), [grads[n] for n in group])
        pending.append((l, group) + tuple(started))
        return token[:1, :1]

    local = _local_step(x.reshape(S, D), loss_target.reshape(S, D), wget, emit, rel_bias, sinks,
                        attn_pre_norm + first[:1, :1], attn_post_norm, ffn_pre_norm, ffn_post_norm)
    return _reduce_and_update(x.shape, names, weights, mom1, mom2, chip, core, pending, *local)


def _local_step(xs, target, wget, emit, rel_bias, sinks, attn_pre_norm, attn_post_norm, ffn_pre_norm, ffn_post_norm):
    bidx = jnp.asarray(_bucket_maps())

    saved, layers = [], []
    h1 = _rms_fwd("pre_norm_first", xs, attn_pre_norm[0:1])
    x_in = xs
    for l in range(DEPTH):
        mo, sv_mix, w = _mixer_fwd(h1, functools.partial(wget, l), rel_bias, sinks[l], bidx)
        x_mid, h2 = _post_pre_fwd("post_attn_norm", x_in, mo, attn_post_norm[l:l + 1], ffn_pre_norm[l:l + 1])
        w.update(wget(l, 2, h2))
        dn, sv_ffn = _ffn_fwd(h2, w)
        g_next = attn_pre_norm[l + 1:l + 2] if l + 1 < DEPTH else None
        x_out, h1_next = _post_pre_fwd("post_ffn_norm" if l + 1 < DEPTH else "post_ffn_norm_last", x_mid, dn,
                                       ffn_post_norm[l:l + 1], g_next)
        saved.append(dict(x_in=x_in, h1=h1, mo=mo, x_mid=x_mid, h2=h2, dn=dn, mix=sv_mix, ffn=sv_ffn))
        layers.append(w)
        x_in, h1 = x_out, h1_next

    loss_row, dres = _loss_kernel(x_in, target)

    small = [None] * DEPTH
    stats = jnp.zeros((N_BAND_Q, 8, 128), F32)
    dh_next = None
    for l in reversed(range(DEPTH)):
        w, sv = layers[l], saved[l]
        if l + 1 < DEPTH:
            pre = (saved[l + 1]["x_in"], attn_pre_norm[l + 1:l + 2] + zero, dh_next)
            dres, d_dn, dg_pre_next, dg_fpost = _norm_bwd("post_ffn_norm_bwd", dres, pre,
                                                          (sv["dn"], ffn_post_norm[l:l + 1]))
            small[l + 1]["attn_pre_norm"] = dg_pre_next
        else:
            dres, d_dn, _, dg_fpost = _norm_bwd("post_ffn_norm_last_bwd", dres, None, (sv["dn"], ffn_post_norm[l:l + 1]))
        dh2, g_ffn = _ffn_bwd(d_dn, sv["h2"], w, sv["ffn"])
        zero = emit(l, 0, g_ffn)
        dres, d_mo, dg_fpre, dg_apost = _norm_bwd("post_attn_norm_bwd", dres,
                                                  (sv["x_mid"], ffn_pre_norm[l:l + 1] + zero, dh2),
                                                  (sv["mo"], attn_post_norm[l:l + 1]))
        dh_next, g_mix, stats, zero = _mixer_bwd(d_mo, sv["h1"], w, sv["mix"], rel_bias, sinks[l], bidx, stats,
                                                 functools.partial(emit, l))
        small[l] = dict(ffn_post_norm=dg_fpost, ffn_pre_norm=dg_fpre, attn_post_norm=dg_apost,
                        sinks=stats[N_A:, 1, 0], conv_b=g_ffn["conv_b"], b_gate=g_mix["b_gate"], conv_w=g_ffn["conv_w"])
    grad_x, _, dg_pre0, _ = _norm_bwd("pre_norm_first_bwd", dres, (saved[0]["x_in"], attn_pre_norm[0:1] + zero, dh_next),
                                      None)
    small[0]["attn_pre_norm"] = dg_pre0
    return loss_row, grad_x, small, stats


def _reduce_and_update(x_shape, names, weights, mom1, mom2, chip, core, pending, loss_row, grad_x, small, stats):
    delta, new_m, new_v, grads = {}, {}, {}, {}

    def update(n, g):
        grads[n], delta[n], new_m[n], new_v[n] = _adamw("adamw_" + n, _shard_view(n, weights[n]), g,
                                                        _shard_view(n, mom1[n]), _shard_view(n, mom2[n]))

    small_vals = {n: jnp.stack([small[l][n].reshape(weights[n].shape[1:]) for l in range(DEPTH)])
                  for n in ("attn_pre_norm", "attn_post_norm", "ffn_pre_norm", "ffn_post_norm", "conv_b", "sinks")}
    small_vals["b_gate"] = jnp.stack([small[l]["b_gate"] for l in range(DEPTH)])
    small_vals["conv_w"] = jnp.stack([small[l]["conv_w"] for l in range(DEPTH)])
    small_vals["rel_bias"] = stats[:, 0, :NUM_BUCKETS].T
    small_vals["loss"] = loss_row[0, :1]
    shapes = {n: v.shape for n, v in small_vals.items()}
    small_send, small_recv, packed, small_land, started = _small_start(_pack_small(small_vals))

    summed = {}

    def finish(which, after):
        for l, group, send, recv, gs, lands in pending:
            if (group == ("w_in",)) == which:
                gs, lands = _reduce_wait("reduce_wait_%d_%s" % (l, group[0]), send, recv, gs, lands, after)
                for n, g, land in zip(group, gs, lands):
                    summed[n] = _reduce_sum("reduce_sum_%d_%s" % (l, n), g, land, l, summed.get(n), chip, core)

    finish(False, started)
    early = [n for n in BIG if n != "w_in"]
    for n, g in zip(early, _join_halves("join_halves", [summed[n] for n in early])):
        update(n, g)
    finish(True, delta[early[-1]])
    update("w_in", _join_halves("join_halves_w_in", [summed["w_in"]])[0])

    packed, small_land = _small_wait(small_send, small_recv, packed, small_land, delta["w_in"])
    small_land = lax.dynamic_update_slice(small_land, packed[None], (2 * chip + core, jnp.int32(0), jnp.int32(0)))
    reduced = _unpack_small(_small_sum(small_land), shapes)
    reduced["b_gate"] = lax.dynamic_slice_in_dim(reduced["b_gate"], chip * (D // N_CHIPS), D // N_CHIPS, axis=2)
    reduced["conv_w"] = lax.dynamic_slice_in_dim(reduced["conv_w"], chip * (2 * D_FF // N_CHIPS), 2 * D_FF // N_CHIPS, axis=2)
    for n in names:
        if n not in grads:
            update(n, reduced[n].reshape(weights[n].shape))
    for out in (grads, delta, new_m, new_v):
        out["w_in"] = _shard_view("w_in", out["w_in"])

    loss = reduced["loss"].reshape(())
    return (loss, grad_x.reshape(x_shape), *[grads[n] for n in names], *[delta[n] for n in names],
            *[new_m[n] for n in names], *[new_v[n] for n in names])
```

```python
import functools
import math

import numpy as np
import jax
import jax.numpy as jnp
from jax import lax
from jax.experimental import pallas as pl
from jax.experimental.pallas import tpu as pltpu

F32 = jnp.float32
BF16 = jnp.bfloat16

S = 2048
D = 1024
DEPTH = 2
HD = 64
BLK = 128
NQB = S // BLK
A_GROUPS = ((128, 1), (512, 4), (2048, 16))
N_BAND_Q = 20
N_A = 12
NUM_BUCKETS = 32
MAX_DISTANCE = 2048
D_FF = 4096
IN_COLS = 6912
IN_SHARD = IN_COLS // 4
OFF_GATE = 3840
EPS = 1e-6
SCALE = HD ** -0.5
NEG = -1e30
N_CHIPS = 4
N_DEV = 8

ADAM_LR = 0.001
ADAM_B1 = 0.9
ADAM_B2 = 0.999
ADAM_EPS = 1e-08
ADAM_WD = 0.01
ADAM_STEP = 10

VMEM_LIMIT = 56 * 1024 * 1024

NN = (((1,), (0,)), ((), ()))
NT = (((1,), (1,)), ((), ()))
TN = (((0,), (0,)), ((), ()))

MESH = pl.DeviceIdType.MESH
ANY = pl.BlockSpec(memory_space=pl.ANY)


def _dot(a, b, dims):
    return lax.dot_general(a, b, dims, preferred_element_type=F32)


def _params(sem):
    return pltpu.CompilerParams(dimension_semantics=sem, vmem_limit_bytes=VMEM_LIMIT)


def _matmul(name, a, b, out_shape, out_dtype, grid, a_spec, b_spec, o_spec, dims, acc_shape):
    nk = grid[-1]

    def body(a_ref, b_ref, o_ref, *scratch):
        part = _dot(a_ref[...].astype(BF16), b_ref[...].astype(BF16), dims)
        if nk == 1:
            o_ref[...] = part.astype(o_ref.dtype)
            return
        acc_ref, = scratch
        k = pl.program_id(len(grid) - 1)

        @pl.when(k == 0)
        def _():
            acc_ref[...] = part

        @pl.when(k > 0)
        def _():
            acc_ref[...] += part

        @pl.when(k == nk - 1)
        def _():
            o_ref[...] = acc_ref[...].astype(o_ref.dtype)

    scratch = [] if nk == 1 else [pltpu.VMEM(acc_shape, F32)]
    sem = ("parallel",) * (len(grid) - 1) + ("arbitrary",)
    return pl.pallas_call(
        body, name=name, grid=grid, in_specs=[a_spec, b_spec], out_specs=o_spec,
        out_shape=jax.ShapeDtypeStruct(out_shape, out_dtype), scratch_shapes=scratch,
        compiler_params=_params(sem))(a, b)


FULL_K = 8192


def _mm_tn_sharded(name, a, b, row_sharded, tm=512, tn=512, tk=FULL_K):
    k, m = a.shape
    n = b.shape[1]
    m4, n4 = (m // N_CHIPS, n) if row_sharded else (m, n // N_CHIPS)
    tm, tn, tk = min(tm, m4), min(tn, n4), min(tk, k)
    per_m, per_n = m4 // tm, n4 // tn
    if row_sharded:
        o_map = lambda i, j, l: (i // per_m, i % per_m, j)
    else:
        o_map = lambda i, j, l: (j // per_n, i, j % per_n)
    return _matmul(name, a, b, (N_CHIPS, m4, n4), BF16, (m // tm, n // tn, k // tk),
                   pl.BlockSpec((tk, tm), lambda i, j, l: (l, i)),
                   pl.BlockSpec((tk, tn), lambda i, j, l: (l, j)),
                   pl.BlockSpec((None, tm, tn), o_map), TN, (tm, tn))


def _mm_nn(name, a, b, out_dtype, tm=512, tn=512, tk=FULL_K):
    m, k = a.shape
    n = b.size // k
    tm, tn, tk = min(tm, m), min(tn, b.shape[-1]), min(tk, k)
    per_shard = b.shape[-1] // tn
    if b.ndim == 2:
        b_spec = pl.BlockSpec((tk, tn), lambda i, j, l: (l, j))
    else:
        b_spec = pl.BlockSpec((None, tk, tn), lambda i, j, l: (j // per_shard, l, j % per_shard))
    return _matmul(name, a, b, (m, n), out_dtype, (m // tm, n // tn, k // tk),
                   pl.BlockSpec((tm, tk), lambda i, j, l: (i, l)), b_spec,
                   pl.BlockSpec((tm, tn), lambda i, j, l: (i, j)), NN, (tm, tn))


def _mm_nt(name, a, b, out_dtype, tm=512, tn=512, tk=FULL_K):
    m, k = a.shape
    n = b.shape[-2]
    tm, tn, tk = min(tm, m), min(tn, n), min(tk, b.shape[-1])
    per_shard = b.shape[-1] // tk
    if b.ndim == 2:
        b_spec = pl.BlockSpec((tn, tk), lambda i, j, l: (j, l))
    else:
        b_spec = pl.BlockSpec((None, tn, tk), lambda i, j, l: (l // per_shard, j, l % per_shard))
    return _matmul(name, a, b, (m, n), out_dtype, (m // tm, n // tn, k // tk),
                   pl.BlockSpec((tm, tk), lambda i, j, l: (i, l)), b_spec,
                   pl.BlockSpec((tm, tn), lambda i, j, l: (i, j)), NT, (tm, tn))


def _mm_tn(name, a, b, out_dtype, tm=512, tn=512, tk=FULL_K):
    k, m = a.shape
    n = b.shape[1]
    tm, tn, tk = min(tm, m), min(tn, n), min(tk, k)
    return _matmul(name, a, b, (m, n), out_dtype, (m // tm, n // tn, k // tk),
                   pl.BlockSpec((tk, tm), lambda i, j, l: (l, i)),
                   pl.BlockSpec((tk, tn), lambda i, j, l: (l, j)),
                   pl.BlockSpec((tm, tn), lambda i, j, l: (i, j)), TN, (tm, tn))


TR = 512


def _row_spec(width=D):
    return pl.BlockSpec((TR, width), lambda i: (i, 0))


def _vec_spec(width=D):
    return pl.BlockSpec((1, width), lambda i: (0, 0))


def _rms(x, g):
    r = lax.rsqrt(jnp.mean(x * x, axis=-1, keepdims=True) + EPS)
    return x * r * g


def _rms_fwd(name, x, g):
    def body(x_ref, g_ref, h_ref):
        h_ref[...] = _rms(x_ref[...], g_ref[...]).astype(BF16)

    return pl.pallas_call(
        body, name=name, grid=(S // TR,), in_specs=[_row_spec(), _vec_spec()], out_specs=_row_spec(),
        out_shape=jax.ShapeDtypeStruct((S, D), BF16), compiler_params=_params(("parallel",)))(x, g)


def _post_pre_fwd(name, x, y, g_post, g_pre):
    has_pre = g_pre is not None

    def body(*refs):
        if has_pre:
            x_ref, y_ref, gp_ref, gn_ref, xn_ref, h_ref = refs
        else:
            x_ref, y_ref, gp_ref, xn_ref = refs
        xn = x_ref[...] + _rms(y_ref[...], gp_ref[...])
        xn_ref[...] = xn
        if has_pre:
            h_ref[...] = _rms(xn, gn_ref[...]).astype(BF16)

    ins = [x, y, g_post] + ([g_pre] if has_pre else [])
    in_specs = [_row_spec(), _row_spec(), _vec_spec()] + ([_vec_spec()] if has_pre else [])
    out_shape = [jax.ShapeDtypeStruct((S, D), F32)] + ([jax.ShapeDtypeStruct((S, D), BF16)] if has_pre else [])
    out_specs = [_row_spec()] + ([_row_spec()] if has_pre else [])
    out = pl.pallas_call(
        body, name=name, grid=(S // TR,), in_specs=in_specs, out_specs=out_specs, out_shape=out_shape,
        compiler_params=_params(("parallel",)))(*ins)
    return out if has_pre else (out[0], None)


def _rms_bwd_math(x, g, dy):
    r = lax.rsqrt(jnp.mean(x * x, axis=-1, keepdims=True) + EPS)
    n = x * r
    dn = dy * g
    dx = r * (dn - n * jnp.mean(dn * n, axis=-1, keepdims=True))
    return dx, jnp.sum(dy * n, axis=0, keepdims=True)


def _norm_bwd(name, dres, pre=None, post=None):
    has_pre, has_post = pre is not None, post is not None

    def body(*refs):
        refs = list(refs)
        dres_ref = refs.pop(0)
        if has_pre:
            xn_ref, gn_ref, dh_ref = refs[:3]
            refs = refs[3:]
        if has_post:
            y_ref, gp_ref = refs[:2]
            refs = refs[2:]
        dxn_ref = refs.pop(0)
        dy_ref = refs.pop(0) if has_post else None
        dgn_ref = refs.pop(0) if has_pre else None
        dgp_ref = refs.pop(0) if has_post else None
        first = pl.program_id(0) == 0
        dxn = dres_ref[...]
        if has_pre:
            dx, dg = _rms_bwd_math(xn_ref[...], gn_ref[...], dh_ref[...])
            dxn = dxn + dx

            @pl.when(first)
            def _():
                dgn_ref[...] = dg

            @pl.when(jnp.logical_not(first))
            def _():
                dgn_ref[...] += dg
        dxn_ref[...] = dxn
        if has_post:
            dy, dg = _rms_bwd_math(y_ref[...], gp_ref[...], dxn)
            dy_ref[...] = dy.astype(BF16)

            @pl.when(first)
            def _():
                dgp_ref[...] = dg

            @pl.when(jnp.logical_not(first))
            def _():
                dgp_ref[...] += dg

    ins, in_specs = [dres], [_row_spec()]
    if has_pre:
        ins += list(pre)
        in_specs += [_row_spec(), _vec_spec(), _row_spec()]
    if has_post:
        ins += list(post)
        in_specs += [_row_spec(), _vec_spec()]
    out_shape, out_specs = [jax.ShapeDtypeStruct((S, D), F32)], [_row_spec()]
    if has_post:
        out_shape.append(jax.ShapeDtypeStruct((S, D), BF16))
        out_specs.append(_row_spec())
    for _ in range(int(has_pre) + int(has_post)):
        out_shape.append(jax.ShapeDtypeStruct((1, D), F32))
        out_specs.append(_vec_spec())
    out = list(pl.pallas_call(
        body, name=name, grid=(S // TR,), in_specs=in_specs, out_specs=out_specs, out_shape=out_shape,
        compiler_params=_params(("arbitrary",)))(*ins))
    dxn = out.pop(0)
    dy = out.pop(0) if has_post else None
    dgn = out.pop(0) if has_pre else None
    dgp = out.pop(0) if has_post else None
    return dxn, dy, dgn, dgp


def _loss_kernel(y, target):
    def body(y_ref, t_ref, loss_ref, dy_ref):
        e = y_ref[...] - t_ref[...]
        dy_ref[...] = e * (1.0 / D)
        part = jnp.zeros((1, 128), F32) + 0.5 * jnp.sum(jnp.mean(e * e, axis=-1, keepdims=True))

        @pl.when(pl.program_id(0) == 0)
        def _():
            loss_ref[...] = part

        @pl.when(pl.program_id(0) > 0)
        def _():
            loss_ref[...] += part

    return pl.pallas_call(
        body, name="loss", grid=(S // TR,), in_specs=[_row_spec(), _row_spec()],
        out_specs=[_vec_spec(128), _row_spec()],
        out_shape=[jax.ShapeDtypeStruct((1, 128), F32), jax.ShapeDtypeStruct((S, D), F32)],
        compiler_params=_params(("arbitrary",)))(y, target)


def _t5_bucket_np(dist):
    max_exact = NUM_BUCKETS // 2
    nf = np.maximum(dist, 1).astype(np.float32)
    large = max_exact + (np.log(nf / max_exact) / np.float32(math.log(MAX_DISTANCE / max_exact))
                         * (NUM_BUCKETS - max_exact)).astype(np.int32)
    large = np.minimum(large, NUM_BUCKETS - 1)
    return np.where(dist < max_exact, dist, large).astype(np.int32)


def _bucket_maps():
    a = np.arange(BLK)[:, None]
    b = np.arange(2 * BLK)[None, :]
    dist = np.maximum(a + BLK - b, 0)
    maps = [_t5_bucket_np(dist * d) for _, d in A_GROUPS] + [_t5_bucket_np(dist)]
    return np.stack(maps).astype(np.int32)


def _pair_spec(col0):
    return pl.BlockSpec((S, 128), lambda p: (0, col0 + p))


def _band_rows(i, d):
    nb = S // d // BLK
    r, b = i // nb, i % nb
    cur = pl.ds(b * BLK * d + r, BLK, stride=d)
    prev = pl.ds(jnp.maximum(b - 1, 0) * BLK * d + r, BLK, stride=d)
    return cur, prev, jnp.minimum(b, 1)


def _band_bias(tab_ref, bi, h):
    bias = jnp.zeros((BLK, 2 * BLK), F32)
    for kk in range(NUM_BUCKETS):
        bias = jnp.where(bi == kk, tab_ref[kk, h], bias)
    return bias


def _lane_lo(rows=BLK):
    return lax.broadcasted_iota(jnp.int32, (rows, 128), 1) < HD


def _per_head(x, lo):
    return (jnp.sum(jnp.where(lo, x, 0.0), axis=1, keepdims=True) * (1.0 / HD),
            jnp.sum(jnp.where(lo, 0.0, x), axis=1, keepdims=True) * (1.0 / HD))


def _band_fill(bias_ref, tab_ref, bi, head, maxd):
    a = lax.broadcasted_iota(jnp.int32, (BLK, 2 * BLK), 0)
    c = lax.broadcasted_iota(jnp.int32, (BLK, 2 * BLK), 1)
    dist = a + BLK - c
    in_band = jnp.logical_and(dist >= 0, dist <= maxd)
    for h in range(2):
        bias = jnp.where(in_band, _band_bias(tab_ref, bi, head + h), NEG)
        bias_ref[1, h * BLK:(h + 1) * BLK, :] = bias
        bias_ref[0, h * BLK:(h + 1) * BLK, :] = jnp.where(c >= BLK, bias, NEG)


def _stack_heads(x, lo, dtype=BF16):
    return jnp.concatenate([jnp.where(lo, x, 0.0), jnp.where(lo, 0.0, x)], axis=0).astype(dtype)


def _unstack_heads(x, lo):
    n = x.shape[0] // 2
    return jnp.where(lo, x[:n], x[n:])


def _stack_rows(ref, prev, cur):
    return jnp.concatenate([ref[prev, :], ref[cur, :]], axis=0).astype(BF16)


PAIRS_PER_KV = 2


def _kv_specs(kc, vc, kv_shared):
    if not kv_shared:
        return [_pair_spec(kc), _pair_spec(vc)], []
    shared = [pl.BlockSpec((S, 128), functools.partial(lambda p, c: (0, c), c=c)) for c in (kc, vc)]
    return shared, [pltpu.VMEM((S, 128), F32)] * 2


def _expand_kv(dst_ref, src_ref, pair):
    x = src_ref[...]
    own = lax.broadcasted_iota(jnp.int32, (S, 128), 1) // HD == pair // PAIRS_PER_KV
    dst_ref[...] = jnp.where(own, x, pltpu.roll(x, HD, 1))


def _fold_kv(out_ref, acc_ref, pair):
    x = acc_ref[...]
    own = lax.broadcasted_iota(jnp.int32, (S, 128), 1) // HD == pair // PAIRS_PER_KV
    part = jnp.where(own, x + pltpu.roll(x, HD, 1), 0.0)

    @pl.when(pair == 0)
    def _():
        out_ref[...] = part

    @pl.when(pair > 0)
    def _():
        out_ref[...] += part


def _blocks_of_group(group, dils, block):
    def run(d):
        lax.fori_loop(0, NQB, functools.partial(block, d), 0, unroll=2)

    if len(dils) == 1:
        run(dils[0])
        return
    for g, d in enumerate(dils):
        pl.when(group == g)(functools.partial(run, d))


def _band_fwd(name, dils, n_pairs, maxd, head0, srcs, bidx_g, tab, sinks, kv_shared=False):
    (qa, qc), (ka, kc), (va, vc) = srcs
    per_group = n_pairs // len(dils)
    out_spec = _pair_spec(0)
    smem = pl.BlockSpec(memory_space=pltpu.SMEM)
    full = pl.BlockSpec((len(dils), BLK, 2 * BLK), lambda p: (0, 0, 0))

    kv_specs, kv_scratch = _kv_specs(kc, vc, kv_shared)

    def body(tab_ref, sink_ref, q_ref, k_ref, v_ref, bidx_ref, o_ref, lse_ref, bias_ref, *expanded):
        p = pl.program_id(0)
        if kv_shared:
            _expand_kv(expanded[0], k_ref, p)
            _expand_kv(expanded[1], v_ref, p)
            k_ref, v_ref = expanded
        _band_fill(bias_ref, tab_ref, bidx_ref[p // per_group], head0 + 2 * p, maxd)
        lo = _lane_lo()
        sink = jnp.where(lax.broadcasted_iota(jnp.int32, (2 * BLK, 1), 0) < BLK, sink_ref[2 * p], sink_ref[2 * p + 1])

        def block(d, i, carry):
            cur, prev, has_prev = _band_rows(i, d)
            qs = _stack_heads(q_ref[cur, :] * SCALE, lo)
            ks, vs = _stack_rows(k_ref, prev, cur), _stack_rows(v_ref, prev, cur)
            s = _dot(qs, ks, NT) + bias_ref[has_prev]
            m = jnp.max(s, axis=1, keepdims=True)
            pr = jnp.exp(s - m)
            l = jnp.sum(pr, axis=1, keepdims=True)
            num = _dot(pr.astype(BF16), vs, NN)
            lse = m + jnp.log(l)
            sig = 1.0 / (1.0 + jnp.exp(sink - lse))
            o_ref[cur, :] = _unstack_heads(num * (sig / l), lo)
            lse_ref[cur, :] = _unstack_heads(lse + jnp.zeros((2 * BLK, 128), F32), lo)
            return carry

        _blocks_of_group(p // per_group, dils, block)

    shape = jax.ShapeDtypeStruct((S, n_pairs * 128), F32)
    return pl.pallas_call(
        body, name=name, grid=(n_pairs,),
        in_specs=[smem, smem, _pair_spec(qc)] + kv_specs + [full],
        out_specs=[out_spec, out_spec], out_shape=[shape, shape],
        scratch_shapes=[pltpu.VMEM((2, 2 * BLK, 2 * BLK), F32)] + kv_scratch,
        compiler_params=_params(("parallel",)))(tab, sinks, qa, ka, va, bidx_g)


def _band_bwd(name, dils, n_pairs, maxd, head0, srcs, bidx_g, tab, sinks, o, lse, do, stats_in, kv_shared=False):
    (qa, qc), (ka, kc), (va, vc) = srcs
    per_group = n_pairs // len(dils)
    pair = _pair_spec(0)
    shared = pl.BlockSpec((S, 128), lambda p: (0, p % per_group))
    smem = pl.BlockSpec(memory_space=pltpu.SMEM)
    full = pl.BlockSpec((len(dils), BLK, 2 * BLK), lambda p: (0, 0, 0))
    stat_spec = pl.BlockSpec((2, 8, 128), lambda p: (p, 0, 0))
    kv_specs, kv_scratch = _kv_specs(kc, vc, kv_shared)

    def body(tab_ref, sink_ref, q_ref, k_ref, v_ref, bidx_ref, o_ref, lse_ref, do_ref, sin_ref,
             dq_ref, dk_ref, dv_ref, stat_ref, bias_ref, dsacc_ref, sk_ref, *expanded):
        p = pl.program_id(0)
        if kv_shared:
            _expand_kv(expanded[0], k_ref, p)
            _expand_kv(expanded[1], v_ref, p)
            k_ref, v_ref = expanded[:2]
            dk_out, dv_out, dk_ref, dv_ref = dk_ref, dv_ref, expanded[2], expanded[3]
        _band_fill(bias_ref, tab_ref, bidx_ref[p // per_group], head0 + 2 * p, maxd)
        dsacc_ref[...] = jnp.zeros_like(dsacc_ref)
        sk_ref[...] = jnp.zeros_like(sk_ref)
        dk_ref[...] = jnp.zeros_like(dk_ref)
        dv_ref[...] = jnp.zeros_like(dv_ref)
        lo = _lane_lo()
        head1 = lax.broadcasted_iota(jnp.int32, (2 * BLK, 1), 0) >= BLK
        sink = jnp.where(head1, sink_ref[2 * p + 1], sink_ref[2 * p])

        def block(d, i, carry):
            cur, prev, has_prev = _band_rows(i, d)
            qs = _stack_heads(q_ref[cur, :] * SCALE, lo)
            ks, vs = _stack_rows(k_ref, prev, cur), _stack_rows(v_ref, prev, cur)
            do = do_ref[cur, :]
            dos = _stack_heads(do, lo, F32)
            lse = jnp.concatenate(_per_head(lse_ref[cur, :], lo), axis=0)
            prod = do * o_ref[cur, :]
            delta = jnp.concatenate([jnp.sum(jnp.where(lo, prod, 0.0), axis=1, keepdims=True),
                                     jnp.sum(jnp.where(lo, 0.0, prod), axis=1, keepdims=True)], axis=0)
            sig = 1.0 / (1.0 + jnp.exp(sink - lse))
            pr = jnp.exp(_dot(qs, ks, NT) + bias_ref[has_prev] - lse)
            ds = pr * (sig * (_dot(dos.astype(BF16), vs, NT) - delta))
            dsb = ds.astype(BF16)
            dq_ref[cur, :] = SCALE * _unstack_heads(_dot(dsb, ks, NN), lo)
            dk = _dot(dsb, qs, TN)
            dv = _dot(pr.astype(BF16), (sig * dos).astype(BF16), TN)
            dk_ref[prev, :] += dk[:BLK]
            dk_ref[cur, :] += dk[BLK:]
            dv_ref[prev, :] += dv[:BLK]
            dv_ref[cur, :] += dv[BLK:]
            dsacc_ref[...] += ds
            sink_grad = -delta * (1.0 - sig)
            for h in range(2):
                sk_ref[h] += jnp.zeros((8, 128), F32) + jnp.sum(sink_grad[h * BLK:(h + 1) * BLK])
            return carry

        _blocks_of_group(p // per_group, dils, block)
        if kv_shared:
            _fold_kv(dk_out, dk_ref, p)
            _fold_kv(dv_out, dv_ref, p)

        bi = bidx_ref[p // per_group]
        lane = lax.broadcasted_iota(jnp.int32, (8, 128), 1)
        sub = lax.broadcasted_iota(jnp.int32, (8, 128), 0)
        for h in range(2):
            acc = dsacc_ref[h * BLK:(h + 1) * BLK, :]
            row = jnp.where(jnp.logical_and(sub == 1, lane == 0), sk_ref[h], 0.0)
            for kk in range(NUM_BUCKETS):
                tot = jnp.sum(jnp.where(bi == kk, acc, 0.0))
                row = jnp.where(jnp.logical_and(sub == 0, lane == kk), tot, row)
            stat_ref[h] = row + jnp.where(sub == 0, sin_ref[h], 0.0)

    shape = jax.ShapeDtypeStruct((S, n_pairs * 128), F32)
    kv_spec = pl.BlockSpec((S, 128), lambda p: (0, 0)) if kv_shared else pair
    kv_shape = jax.ShapeDtypeStruct((S, 128), F32) if kv_shared else shape
    return pl.pallas_call(
        body, name=name, grid=(n_pairs,),
        in_specs=[smem, smem, _pair_spec(qc)] + kv_specs + [full, shared, shared, shared, stat_spec],
        out_specs=[pair, kv_spec, kv_spec, stat_spec],
        out_shape=[shape, kv_shape, kv_shape, jax.ShapeDtypeStruct((2 * n_pairs, 8, 128), F32)],
        scratch_shapes=[pltpu.VMEM((2, 2 * BLK, 2 * BLK), F32), pltpu.VMEM((2 * BLK, 2 * BLK), F32),
                        pltpu.VMEM((2, 8, 128), F32)] + kv_scratch * 2,
        compiler_params=_params(("arbitrary" if kv_shared else "parallel",)))(
            tab, sinks, qa, ka, va, bidx_g, o, lse, do, stats_in)


def _comb_fwd(o_g, lse_g):
    def body(o0, o1, o2, l0, l1, l2, out_ref, outb_ref, lse_ref):
        a0, a1, a2 = l0[...], l1[...], l2[...]
        m = jnp.maximum(jnp.maximum(a0, a1), a2)
        e0, e1, e2 = jnp.exp(a0 - m), jnp.exp(a1 - m), jnp.exp(a2 - m)
        tot = e0 + e1 + e2
        out = (e0 * o0[...] + e1 * o1[...] + e2 * o2[...]) / tot
        out_ref[...] = out
        outb_ref[...] = out.astype(BF16)
        lse_ref[...] = m + jnp.log(tot)

    spec = _row_spec(4 * HD)
    groups = [pl.BlockSpec((TR, 4 * HD), functools.partial(lambda i, g: (i, g), g=g)) for g in range(len(A_GROUPS))]
    f32 = jax.ShapeDtypeStruct((S, 4 * HD), F32)
    return pl.pallas_call(
        body, name="comb_fwd", grid=(S // TR,), in_specs=groups + groups, out_specs=[spec] * 3,
        out_shape=[f32, jax.ShapeDtypeStruct((S, 4 * HD), BF16), f32],
        compiler_params=_params(("parallel",)))(o_g, o_g, o_g, lse_g, lse_g, lse_g)


def _split2(x):
    hi = x.astype(BF16)
    return hi, (x - hi.astype(F32)).astype(BF16)


KB = 2 * BLK
SBQ = 2 * BLK


def _tri_sum(x, tri):
    hi, lo = _split2(x)
    both = _dot(jnp.concatenate([hi, lo], axis=0), tri, NN)
    return both[:x.shape[0]] + both[x.shape[0]:]


def _tri(strict_upper):
    r = lax.broadcasted_iota(jnp.int32, (KB, KB), 0)
    c = lax.broadcasted_iota(jnp.int32, (KB, KB), 1)
    return jnp.where(r > c if strict_upper else r < c, 1.0, 0.0).astype(BF16)


def _sb_terms(qs, kj, before):
    z = _dot(qs, kj, NT)
    lsp = jnp.minimum(z, 0.0) - jnp.log(1.0 + jnp.exp(-jnp.abs(z)))
    return lsp, _sb_keep(before, lsp - z)


def _sb_keep(before, x):
    return x if before is None else jnp.where(before, x, 0.0)


def _sb_before(i, m):
    t = (lax.broadcasted_iota(jnp.int32, (2 * SBQ, KB), 0) & (SBQ - 1)) + i * SBQ
    s = lax.broadcasted_iota(jnp.int32, (2 * SBQ, KB), 1) + m * KB
    return s < t


C_COL = 3072 // 128


def _sb_fwd(proj):
    blk = lambda off: pl.BlockSpec((SBQ, 128), lambda p, i: (i, off + p))
    col = lambda off: pl.BlockSpec((S, 128), lambda p, i: (0, off + p))
    out = pl.BlockSpec((SBQ, 128), lambda p, i: (i, p))

    def body(q_ref, k_ref, v_ref, o_ref, ob_ref, tot_ref):
        i = pl.program_id(1)
        lo = _lane_lo(SBQ)
        qs = _stack_heads(q_ref[...] * SCALE, lo)
        suffix = _tri(True)

        def step(n, carry, diagonal=False):
            acc, rest = carry
            m = i - n
            rows = pl.ds(pl.multiple_of(m * KB, KB), KB)
            kj, vj = k_ref[rows, :].astype(BF16), v_ref[rows, :].astype(BF16)
            before = _sb_before(i, m) if diagonal else None
            lsp, lk = _sb_terms(qs, kj, before)
            w = _sb_keep(before, jnp.exp(lsp + _tri_sum(lk, suffix) + rest))
            return acc + _dot(w.astype(BF16), vj, NN), rest + jnp.sum(lk, axis=1, keepdims=True)

        first = step(0, (jnp.zeros((2 * SBQ, 128), F32), jnp.zeros((2 * SBQ, 1), F32)), diagonal=True)
        acc, rest = lax.fori_loop(1, i + 1, step, first)
        o = _unstack_heads(acc, lo)
        o_ref[...] = o
        ob_ref[...] = o.astype(BF16)
        tot_ref[...] = _unstack_heads(rest + jnp.zeros((2 * SBQ, 128), F32), lo)

    f32 = jax.ShapeDtypeStruct((S, 4 * HD), F32)
    return pl.pallas_call(
        body, name="sb_fwd", grid=(2, S // SBQ), in_specs=[blk(C_COL), col(C_COL + 2), col(C_COL + 4)],
        out_specs=[out, out, out], out_shape=[f32, jax.ShapeDtypeStruct((S, 4 * HD), BF16), f32],
        compiler_params=_params(("parallel", "arbitrary")))(proj, proj, proj)


def _sb_bwd(proj, tot, do):
    blk = lambda off: pl.BlockSpec((SBQ, 128), lambda p, i: (i, off + p))
    col = lambda off: pl.BlockSpec((S, 128), lambda p, i: (0, off + p))

    def body(q_ref, k_ref, v_ref, tot_ref, do_ref, dq_ref, dk_ref, dv_ref):
        i = pl.program_id(1)

        @pl.when(i == 0)
        def _():
            dk_ref[...] = jnp.zeros_like(dk_ref)
            dv_ref[...] = jnp.zeros_like(dv_ref)

        lo = _lane_lo(SBQ)
        qs = _stack_heads(q_ref[...] * SCALE, lo)
        dos = _stack_heads(do_ref[...], lo)
        tots = jnp.concatenate(_per_head(tot_ref[...], lo), axis=0)
        prefix = _tri(False)

        def step(m, carry, diagonal=False):
            dq, keep_left, g_left = carry
            rows = pl.ds(pl.multiple_of(m * KB, KB), KB)
            kj, vj = k_ref[rows, :].astype(BF16), v_ref[rows, :].astype(BF16)
            before = _sb_before(i, m) if diagonal else None
            lsp, lk = _sb_terms(qs, kj, before)
            log_rest = tots - keep_left - lk - _tri_sum(lk, prefix)
            w = _sb_keep(before, jnp.exp(lsp + log_rest))
            g = w * _dot(dos, vj, NT)
            g_before = g_left + _dot(g.astype(BF16), prefix, NN)
            beta = jnp.exp(lsp)
            dz = _sb_keep(before, g * (1.0 - beta) - g_before * beta).astype(BF16)
            dk_ref[rows, :] += _dot(dz, qs, TN)
            dv_ref[rows, :] += _dot(w.astype(BF16), dos, TN)
            return (dq + _dot(dz, kj, NN), keep_left + jnp.sum(lk, axis=1, keepdims=True),
                    g_left + jnp.sum(g, axis=1, keepdims=True))

        zero = (jnp.zeros((2 * SBQ, 128), F32), jnp.zeros((2 * SBQ, 1), F32), jnp.zeros((2 * SBQ, 1), F32))
        dq, _, _ = step(i, lax.fori_loop(0, i, step, zero), diagonal=True)
        dq_ref[...] = SCALE * _unstack_heads(dq, lo)

    out_blk = pl.BlockSpec((SBQ, 128), lambda p, i: (i, p))
    out_col = pl.BlockSpec((S, 128), lambda p, i: (0, p))
    f32 = jax.ShapeDtypeStruct((S, 4 * HD), F32)
    return pl.pallas_call(
        body, name="sb_bwd", grid=(2, S // SBQ),
        in_specs=[blk(C_COL), col(C_COL + 2), col(C_COL + 4), out_blk, out_blk],
        out_specs=[out_blk, out_col, out_col], out_shape=[f32, f32, f32],
        compiler_params=_params(("arbitrary", "arbitrary")))(proj, proj, proj, tot, do)


TG = 256
TGR = 1024
GATE_BLK0 = OFF_GATE // TG


def _gate_specs():
    grid = (D // TG, S // TGR)
    p_specs = [pl.BlockSpec((TGR, TG), functools.partial(lambda c, r, br: (r, GATE_BLK0 + br * (D // TG) + c), br=br))
               for br in range(3)]
    b_spec = pl.BlockSpec((3, TG), lambda c, r: (0, c))
    t_spec = pl.BlockSpec((TGR, TG), lambda c, r: (r, c))
    return grid, p_specs, b_spec, t_spec


def _sigmoid(x):
    return 1.0 / (1.0 + jnp.exp(-x))


def _three_rows(rows):
    sub = lax.broadcasted_iota(jnp.int32, (3, rows[0].shape[1]), 0)
    return jnp.where(sub == 0, rows[0], jnp.where(sub == 1, rows[1], rows[2]))


def _gate_fwd(proj, b_gate, br):
    grid, p_specs, b_spec, t_spec = _gate_specs()

    def body(p0, p1, p2, b_ref, r0, r1, r2, out_ref):
        acc = jnp.zeros((TGR, TG), F32)
        for n, (p, r) in enumerate(((p0, r0), (p1, r1), (p2, r2))):
            acc += _sigmoid(p[...] + b_ref[n:n + 1, :]) * r[...]
        out_ref[...] = acc.astype(BF16)

    return pl.pallas_call(
        body, name="gate_fwd", grid=grid, in_specs=p_specs + [b_spec] + [t_spec] * 3, out_specs=t_spec,
        out_shape=jax.ShapeDtypeStruct((S, D), BF16),
        compiler_params=_params(("parallel", "parallel")))(proj, proj, proj, b_gate, *br)


def _gate_bwd(proj, b_gate, br, dmerged):
    grid, p_specs, b_spec, t_spec = _gate_specs()

    def body(p0, p1, p2, b_ref, r0, r1, r2, dm_ref, e0, e1, e2, g0, g1, g2, db_ref):
        dm = dm_ref[...]
        rows = []
        for n, (p, r, e_ref, dg_ref) in enumerate(((p0, r0, e0, g0), (p1, r1, e1, g1), (p2, r2, e2, g2))):
            g = _sigmoid(p[...] + b_ref[n:n + 1, :])
            e_ref[...] = (dm * g).astype(BF16)
            dpre = dm * r[...] * g * (1.0 - g)
            dg_ref[...] = dpre.astype(BF16)
            rows.append(jnp.sum(dpre, axis=0, keepdims=True))
        db = _three_rows(rows)

        @pl.when(pl.program_id(1) == 0)
        def _():
            db_ref[...] = db

        @pl.when(pl.program_id(1) > 0)
        def _():
            db_ref[...] += db

    bf = jax.ShapeDtypeStruct((S, D), BF16)
    out = pl.pallas_call(
        body, name="gate_bwd", grid=grid, in_specs=p_specs + [b_spec] + [t_spec] * 4,
        out_specs=[t_spec] * 6 + [b_spec], out_shape=[bf] * 6 + [jax.ShapeDtypeStruct((3, D), F32)],
        compiler_params=_params(("parallel", "arbitrary")))(proj, proj, proj, b_gate, *br, dmerged)
    return out[:3], out[3:6], out[6]


TC = 256
N_FF_BLK = D_FF // TC
GELU_C = math.sqrt(2.0 / math.pi)


def _shift_down(x, n):
    rows = lax.broadcasted_iota(jnp.int32, x.shape, 0)
    return jnp.where(rows >= n, pltpu.roll(x, n, axis=0), 0.0)


def _shift_up(x, n):
    rows = lax.broadcasted_iota(jnp.int32, x.shape, 0)
    return jnp.where(rows < x.shape[0] - n, pltpu.roll(x, x.shape[0] - n, axis=0), 0.0)


def _conv(u, w, b):
    s1, s2 = _shift_down(u, 1), _shift_down(u, 2)
    return w[2:3, :] * u + w[1:2, :] * s1 + w[0:1, :] * s2 + b, s1, s2


def _gelu_parts(x):
    inner = GELU_C * (x + 0.044715 * x * x * x)
    t = jnp.tanh(inner)
    gelu = 0.5 * x * (1.0 + t)
    dgelu = 0.5 * (1.0 + t) + 0.5 * x * (1.0 - t * t) * GELU_C * (1.0 + 3 * 0.044715 * x * x)
    return gelu, dgelu


def _conv_specs():
    ug = pl.BlockSpec((S, TC), lambda c: (0, c))
    uv = pl.BlockSpec((S, TC), lambda c: (0, N_FF_BLK + c))
    wg = pl.BlockSpec((3, TC), lambda c: (0, c))
    wv = pl.BlockSpec((3, TC), lambda c: (0, N_FF_BLK + c))
    bg = pl.BlockSpec((1, TC), lambda c: (0, c))
    bv = pl.BlockSpec((1, TC), lambda c: (0, N_FF_BLK + c))
    return ug, uv, wg, wv, bg, bv


def _conv_fwd(u, conv_w, conv_b):
    ug, uv, wg, wv, bg, bv = _conv_specs()

    def body(ug_ref, uv_ref, wg_ref, wv_ref, bg_ref, bv_ref, a_ref):
        gc = _conv(ug_ref[...], wg_ref[...], bg_ref[...])[0]
        vc = _conv(uv_ref[...], wv_ref[...], bv_ref[...])[0]
        a_ref[...] = (_gelu_parts(gc)[0] * vc).astype(BF16)

    return pl.pallas_call(
        body, name="conv_fwd", grid=(N_FF_BLK,), in_specs=[ug, uv, wg, wv, bg, bv], out_specs=ug,
        out_shape=jax.ShapeDtypeStruct((S, D_FF), BF16),
        compiler_params=_params(("parallel",)))(u, u, conv_w, conv_w, conv_b, conv_b)


def _conv_bwd(u, conv_w, conv_b, da):
    ug, uv, wg, wv, bg, bv = _conv_specs()

    def back(duc, u, s1, s2, w):
        du = w[2:3, :] * duc + w[1:2, :] * _shift_up(duc, 1) + w[0:1, :] * _shift_up(duc, 2)
        dw = _three_rows([jnp.sum(duc * s2, axis=0, keepdims=True), jnp.sum(duc * s1, axis=0, keepdims=True),
                          jnp.sum(duc * u, axis=0, keepdims=True)])
        return du, dw, jnp.sum(duc, axis=0, keepdims=True)

    def body(ug_ref, uv_ref, wg_ref, wv_ref, bg_ref, bv_ref, da_ref, dug_ref, duv_ref, dwg_ref, dwv_ref, dbg_ref, dbv_ref):
        u_g, u_v = ug_ref[...], uv_ref[...]
        gc, g1, g2 = _conv(u_g, wg_ref[...], bg_ref[...])
        vc, v1, v2 = _conv(u_v, wv_ref[...], bv_ref[...])
        gelu, dgelu = _gelu_parts(gc)
        da = da_ref[...]
        du, dw, db = back(da * vc * dgelu, u_g, g1, g2, wg_ref[...])
        dug_ref[...] = du.astype(BF16)
        dwg_ref[...] = dw
        dbg_ref[...] = db
        du, dw, db = back(da * gelu, u_v, v1, v2, wv_ref[...])
        duv_ref[...] = du.astype(BF16)
        dwv_ref[...] = dw
        dbv_ref[...] = db

    return pl.pallas_call(
        body, name="conv_bwd", grid=(N_FF_BLK,), in_specs=[ug, uv, wg, wv, bg, bv, ug],
        out_specs=[ug, ug, wg, wg, bg, bg],
        out_shape=[jax.ShapeDtypeStruct((S, D_FF), BF16), jax.ShapeDtypeStruct((S, D_FF), BF16),
                   jax.ShapeDtypeStruct((3, D_FF), F32), jax.ShapeDtypeStruct((3, D_FF), F32),
                   jax.ShapeDtypeStruct((1, D_FF), F32), jax.ShapeDtypeStruct((1, D_FF), F32)],
        compiler_params=_params(("parallel",)))(u, u, conv_w, conv_w, conv_b, conv_b, da)


def _adamw(name, w, g, m, v):
    shape = w.shape
    cols = shape[-1]
    flat = [t.reshape(-1, cols) for t in (w, g, m, v)]
    r = flat[0].shape[0]
    tr = min(r, max(8, 2 * 1024 * 1024 // (4 * cols)))

    def body(w_ref, g_ref, m_ref, v_ref, go_ref, d_ref, mo_ref, vo_ref):
        g = g_ref[...]
        go_ref[...] = g
        m = ADAM_B1 * m_ref[...] + (1.0 - ADAM_B1) * g
        v = ADAM_B2 * v_ref[...] + (1.0 - ADAM_B2) * (g * g)
        m_hat = m / (1.0 - ADAM_B1 ** ADAM_STEP)
        v_hat = v / (1.0 - ADAM_B2 ** ADAM_STEP)
        d_ref[...] = -ADAM_LR * (m_hat / (jnp.sqrt(v_hat) + ADAM_EPS) + ADAM_WD * w_ref[...])
        mo_ref[...] = m
        vo_ref[...] = v

    spec = pl.BlockSpec((tr, cols), lambda i: (i, 0))
    outs = pl.pallas_call(
        body, name=name, grid=(pl.cdiv(r, tr),), in_specs=[spec] * 4, out_specs=[spec] * 4,
        out_shape=[jax.ShapeDtypeStruct((r, cols), F32)] * 4, compiler_params=_params(("parallel",)))(*flat)
    return [t.reshape(shape) for t in outs]


def _place():
    x, y, c = lax.axis_index("x"), lax.axis_index("y"), lax.axis_index("c")
    chips = [(1 - x, y), (x, 1 - y), (1 - x, 1 - y)]
    return x, y, c, chips


def _scalars(*vals):
    return jnp.stack([jnp.asarray(v, jnp.int32) for v in vals])


HBM = pl.BlockSpec(memory_space=pltpu.HBM)
SEM = pl.BlockSpec(memory_space=pltpu.SEMAPHORE)
SPLIT_COPY = pltpu.CompilerParams(has_side_effects=pltpu.SideEffectType.DATAFLOW_SIDE_EFFECTING)


def _in_hbm(x):
    return pltpu.with_memory_space_constraint(x, pltpu.HBM)


def _cast_into_slot(name, w, layer, chip):
    _, k, n4 = w.shape
    tr = max(t for t in range(16, 513, 16) if k % t == 0)

    def body(chip_ref, w_ref, o_ref):
        o_ref[...] = w_ref[...].astype(BF16)

    return pl.pallas_call(
        body, name=name,
        grid_spec=pltpu.PrefetchScalarGridSpec(
            num_scalar_prefetch=1, grid=(k // tr,),
            in_specs=[pl.BlockSpec((None, tr, n4), lambda i, s: (layer, i, 0))],
            out_specs=pl.BlockSpec((None, tr, n4), lambda i, s: (s[0], i, 0))),
        out_shape=jax.ShapeDtypeStruct((N_CHIPS, k, n4), BF16),
        compiler_params=_params(("parallel",)))(_scalars(chip), w)


def _gather_copy(buf_ref, k, from_chip, send_sem, recv_sem, chips, c, half=False):
    rows = buf_ref.at[from_chip]
    if half:
        h = buf_ref.shape[1] // 2
        rows = buf_ref.at[from_chip, pl.ds(pl.multiple_of(c * h, h), h)]
    return pltpu.make_async_remote_copy(src_ref=rows, dst_ref=rows, send_sem=send_sem, recv_sem=recv_sem,
                                        device_id=(*chips[k], c), device_id_type=MESH)


def _gather_start(name, bufs, groups, halved=()):
    n, ng = len(bufs), len(groups)
    where = {a: (gi, e) for gi, g in enumerate(groups) for e, a in enumerate(g)}

    def body(*refs):
        ins, sems, token = refs[:n], refs[n:n + 2 * ng], refs[-1]
        x, y, c, chips = _place()
        for a in range(n):
            gi, e = where[a]
            for k in range(3):
                _gather_copy(ins[a], k, 2 * x + y, sems[2 * gi].at[3 * e + k], sems[2 * gi + 1].at[3 * e + k],
                             chips, c, a in halved).start()
        token[...] = jnp.zeros_like(token)

    out_shape = [pltpu.SemaphoreType.DMA((3 * len(g),)) for g in groups for _ in range(2)]
    out_shape += [pltpu.HBM(b.shape, b.dtype) for b in bufs] + [jax.ShapeDtypeStruct((8, 128), F32)]
    out = pl.pallas_call(
        body, name=name, in_specs=[HBM] * n,
        out_specs=[SEM] * (2 * ng) + [HBM] * n + [pl.BlockSpec(memory_space=pltpu.VMEM)], out_shape=out_shape,
        input_output_aliases={a: 2 * ng + a for a in range(n)}, compiler_params=SPLIT_COPY)(*[_in_hbm(b) for b in bufs])
    sems = [(out[2 * gi], out[2 * gi + 1]) for gi in range(ng)]
    return sems, list(out[2 * ng:2 * ng + n]), out[-1]


def _gather_wait(name, bufs, send, recv, after, halved=()):
    n = len(bufs)

    def body(*refs):
        ins, send_sem, recv_sem = refs[:n], refs[n], refs[n + 1]
        x, y, c, chips = _place()
        for e in range(n):
            for k in range(3):
                sems = (send_sem.at[3 * e + k], recv_sem.at[3 * e + k])
                _gather_copy(ins[e], k, 2 * x + y, *sems, chips, c, e in halved).wait_send()
                _gather_copy(ins[e], k, 2 * chips[k][0] + chips[k][1], *sems, chips, c, e in halved).wait_recv()

    return pl.pallas_call(
        body, name=name, in_specs=[HBM] * n + [SEM, SEM, ANY], out_specs=[HBM] * n,
        out_shape=[pltpu.HBM(b.shape, b.dtype) for b in bufs],
        input_output_aliases={a: a for a in range(n)}, compiler_params=SPLIT_COPY)(*bufs, send, recv, after)


def _swap_halves(name, bufs):
    n = len(bufs)

    def body(*refs):
        ins, outs = refs[:n], refs[n:2 * n]
        send_sem, recv_sem = refs[2 * n:]
        x, y, c, chips = _place()

        def piece(ref, k, which):
            h = ref.shape[1] // 2
            return ref.at[2 * chips[k][0] + chips[k][1], pl.ds(pl.multiple_of(which * h, h), h)]

        def copy(a, k, which):
            return pltpu.make_async_remote_copy(
                src_ref=piece(ins[a], k, c), dst_ref=piece(outs[a], k, which), send_sem=send_sem.at[3 * a + k],
                recv_sem=recv_sem.at[3 * a + k], device_id=(x, y, 1 - c), device_id_type=MESH)

        for a in range(n):
            for k in range(3):
                copy(a, k, c).start()
        for a in range(n):
            for k in range(3):
                copy(a, k, c).wait_send()
                copy(a, k, 1 - c).wait_recv()

    return pl.pallas_call(
        body, name=name, in_specs=[ANY] * n, out_specs=[ANY] * n,
        out_shape=[jax.ShapeDtypeStruct(b.shape, b.dtype) for b in bufs],
        input_output_aliases={a: a for a in range(n)},
        scratch_shapes=[pltpu.SemaphoreType.DMA((3 * n,)), pltpu.SemaphoreType.DMA((3 * n,))],
    )(*bufs)


def _reduce_copy(g_ref, land_ref, mask, send_sem, recv_sem, x, y, c, sending):
    px, py, pc = x ^ ((mask >> 2) & 1), y ^ ((mask >> 1) & 1), c ^ (mask & 1)
    half = g_ref.shape[1] // 2
    src = g_ref.at[2 * px + py, pl.ds(pl.multiple_of(pc * half, half), half)]
    dst = land_ref.at[4 * x + 2 * y + c] if sending else land_ref.at[4 * px + 2 * py + pc]
    return pltpu.make_async_remote_copy(src_ref=src, dst_ref=dst, send_sem=send_sem, recv_sem=recv_sem,
                                        device_id=(px, py, pc), device_id_type=MESH)


def _reduce_start(name, grads):
    n = len(grads)
    lands = [lax.empty((N_DEV, g.shape[1] // 2, g.shape[2]), g.dtype) for g in grads]

    def body(*refs):
        gs, ls, send_sem, recv_sem = refs[:n], refs[n:2 * n], refs[2 * n], refs[2 * n + 1]
        x, y, c, _ = _place()
        for a in range(n):
            for mask in range(1, N_DEV):
                s = (N_DEV - 1) * a + mask - 1
                _reduce_copy(gs[a], ls[a], mask, send_sem.at[s], recv_sem.at[s], x, y, c, True).start()
        refs[-1][...] = jnp.zeros_like(refs[-1])

    sem = pltpu.SemaphoreType.DMA((n * (N_DEV - 1),))
    out = pl.pallas_call(
        body, name=name, in_specs=[HBM] * (2 * n),
        out_specs=[SEM, SEM] + [HBM] * (2 * n) + [pl.BlockSpec(memory_space=pltpu.VMEM)],
        out_shape=[sem, sem] + [pltpu.HBM(t.shape, t.dtype) for t in grads + lands] + [jax.ShapeDtypeStruct((8, 128), F32)],
        input_output_aliases={a: 2 + a for a in range(2 * n)}, compiler_params=SPLIT_COPY)(
            *[_in_hbm(t) for t in grads + lands])
    return out[0], out[1], list(out[2:2 + n]), list(out[2 + n:2 + 2 * n]), out[-1]


def _reduce_wait(name, send, recv, grads, lands, after):
    n = len(grads)

    def body(*refs):
        gs, ls, send_sem, recv_sem = refs[:n], refs[n:2 * n], refs[2 * n], refs[2 * n + 1]
        x, y, c, _ = _place()
        for a in range(n):
            for mask in range(1, N_DEV):
                s = (N_DEV - 1) * a + mask - 1
                sems = (send_sem.at[s], recv_sem.at[s])
                _reduce_copy(gs[a], ls[a], mask, *sems, x, y, c, True).wait_send()
                _reduce_copy(gs[a], ls[a], mask, *sems, x, y, c, False).wait_recv()

    out = pl.pallas_call(
        body, name=name, in_specs=[HBM] * (2 * n) + [SEM, SEM, ANY], out_specs=[HBM] * (2 * n),
        out_shape=[pltpu.HBM(t.shape, t.dtype) for t in grads + lands],
        input_output_aliases={a: a for a in range(2 * n)}, compiler_params=SPLIT_COPY)(*grads, *lands, send, recv, after)
    return list(out[:n]), list(out[n:])


def _reduce_sum(name, g, land, layer, into, chip, c):
    _, k4, n4 = g.shape
    half = k4 // 2
    tr = max(t for t in range(16, 513, 16) if half % t == 0)
    per = half // tr
    me = 2 * chip + c

    def body(s_ref, own_ref, *refs):
        total = own_ref[...].astype(F32)
        for ref in refs[:N_DEV - 1]:
            total = total + ref[...].astype(F32)
        refs[-1][...] = total

    in_specs = [pl.BlockSpec((None, tr, n4), lambda i, s: (s[0], s[1] * per + i, 0))]
    in_specs += [pl.BlockSpec((None, tr, n4), functools.partial(lambda i, s, m: (s[1 + m], i, 0), m=m))
                 for m in range(1, N_DEV)]
    ins = [g] + [land] * (N_DEV - 1)
    aliases = {}
    if into is not None:
        in_specs, ins, aliases = in_specs + [ANY], ins + [into], {1 + N_DEV: 0}
    return pl.pallas_call(
        body, name=name,
        grid_spec=pltpu.PrefetchScalarGridSpec(
            num_scalar_prefetch=1, grid=(per,), in_specs=in_specs,
            out_specs=pl.BlockSpec((None, tr, n4), lambda i, s: (layer, s[1] * per + i, 0))),
        out_shape=jax.ShapeDtypeStruct((DEPTH, k4, n4), F32), input_output_aliases=aliases,
        compiler_params=_params(("parallel",)))(_scalars(chip, c, *[me ^ m for m in range(1, N_DEV)]), *ins)


def _join_halves(name, bufs):
    n = len(bufs)

    def body(*refs):
        ins, outs = refs[:n], refs[n:2 * n]
        send_sem, recv_sem = refs[2 * n:]
        x, y, c, _ = _place()

        def rows(ref, which):
            half = ref.shape[1] // 2
            return ref.at[:, pl.ds(pl.multiple_of(which * half, half), half)]

        sends = [pltpu.make_async_remote_copy(
            src_ref=rows(ins[a], c), dst_ref=rows(outs[a], c), send_sem=send_sem.at[a], recv_sem=recv_sem.at[a],
            device_id=(x, y, 1 - c), device_id_type=MESH) for a in range(n)]
        for cp in sends:
            cp.start()
        for a in range(n):
            sends[a].wait_send()
            pltpu.make_async_remote_copy(
                src_ref=rows(ins[a], c), dst_ref=rows(outs[a], 1 - c), send_sem=send_sem.at[a], recv_sem=recv_sem.at[a],
                device_id=(x, y, 1 - c), device_id_type=MESH).wait_recv()

    return pl.pallas_call(
        body, name=name, in_specs=[ANY] * n, out_specs=[ANY] * n,
        out_shape=[jax.ShapeDtypeStruct(b.shape, b.dtype) for b in bufs],
        input_output_aliases={a: a for a in range(n)},
        scratch_shapes=[pltpu.SemaphoreType.DMA((n,)), pltpu.SemaphoreType.DMA((n,))],
    )(*bufs)


def _small_copy(b_ref, l_ref, mask, send_sem, recv_sem, x, y, c, sending):
    px, py, pc = x ^ ((mask >> 2) & 1), y ^ ((mask >> 1) & 1), c ^ (mask & 1)
    dst = l_ref.at[4 * x + 2 * y + c] if sending else l_ref.at[4 * px + 2 * py + pc]
    return pltpu.make_async_remote_copy(src_ref=b_ref, dst_ref=dst, send_sem=send_sem.at[mask - 1],
                                        recv_sem=recv_sem.at[mask - 1], device_id=(px, py, pc), device_id_type=MESH)


def _small_start(block):
    land = lax.empty((N_DEV,) + block.shape, block.dtype)

    def body(b_ref, l_ref, send_sem, recv_sem, b_thru, l_thru, token):
        x, y, c, _ = _place()
        for mask in range(1, N_DEV):
            _small_copy(b_ref, l_ref, mask, send_sem, recv_sem, x, y, c, True).start()
        token[...] = jnp.zeros_like(token)

    sem = pltpu.SemaphoreType.DMA((N_DEV - 1,))
    return pl.pallas_call(
        body, name="small_start", in_specs=[HBM, HBM],
        out_specs=[SEM, SEM, HBM, HBM, pl.BlockSpec(memory_space=pltpu.VMEM)],
        out_shape=[sem, sem, pltpu.HBM(block.shape, block.dtype), pltpu.HBM(land.shape, land.dtype),
                   jax.ShapeDtypeStruct((8, 128), F32)],
        input_output_aliases={0: 2, 1: 3}, compiler_params=SPLIT_COPY)(_in_hbm(block), _in_hbm(land))


def _small_wait(send, recv, block, land, after):
    def body(b_ref, l_ref, send_sem, recv_sem, after_ref, b_out, l_out):
        x, y, c, _ = _place()
        for mask in range(1, N_DEV):
            _small_copy(b_ref, l_ref, mask, send_sem, recv_sem, x, y, c, True).wait_send()
            _small_copy(b_ref, l_ref, mask, send_sem, recv_sem, x, y, c, False).wait_recv()

    return pl.pallas_call(
        body, name="small_wait", in_specs=[HBM, HBM, SEM, SEM, ANY], out_specs=[HBM, HBM],
        out_shape=[pltpu.HBM(block.shape, block.dtype), pltpu.HBM(land.shape, land.dtype)],
        input_output_aliases={0: 0, 1: 1}, compiler_params=SPLIT_COPY)(block, land, send, recv, after)


def _small_sum(land):
    def body(l_ref, out_ref):
        total = l_ref[0]
        for d in range(1, N_DEV):
            total = total + l_ref[d]
        out_ref[...] = total

    vmem = pl.BlockSpec(memory_space=pltpu.VMEM)
    return pl.pallas_call(
        body, name="small_sum", in_specs=[vmem], out_specs=vmem,
        out_shape=jax.ShapeDtypeStruct(land.shape[1:], F32),
        compiler_params=pltpu.CompilerParams(vmem_limit_bytes=VMEM_LIMIT))(land)


B_Q_COL = 2304 // 128
B_K0, B_V0 = 2816, 2944


def _full_cols(w_g):
    return w_g.transpose(1, 0, 2).reshape(w_g.shape[1], -1)


A_DILS = tuple(d for _, d in A_GROUPS)
A_PAIRS = N_A // 2


def _src_a(proj):
    return ((proj, 0), (proj, A_PAIRS), (proj, 2 * A_PAIRS))


def _mixer_fwd(h1, wget, rel_bias, sinks_l, bidx):
    w = dict(wget(0, h1))
    proj = _mm_nt("proj_in", h1, w["w_in"], F32, tm=S, tn=1152)
    no_sinks = jnp.full((N_A,), NEG, F32)
    o_g, lse_g = _band_fwd("band_fwd_a", A_DILS, A_PAIRS, BLK, 0, _src_a(proj), bidx[:3], rel_bias, no_sinks)
    o_a32, o_a, lse_a = _comb_fwd(o_g, lse_g)
    src_b = ((proj, B_Q_COL), (proj, B_K0 // 128), (proj, B_V0 // 128))
    o_b32, lse_b = _band_fwd("band_fwd_b", (1,), 4, BLK - 1, N_A, src_b, bidx[3:], rel_bias, sinks_l, kv_shared=True)
    o_b = o_b32.astype(BF16)
    o_c32, o_c, tot_c = _sb_fwd(proj)
    w.update(wget(1, o_c32))
    br = [_mm_nn("branch_a", o_a, w["w_br_a"], F32, tm=S), _mm_nn("branch_b", o_b, w["w_br_b"], F32, tm=S),
          _mm_nn("branch_c", o_c, w["w_br_c"], F32, tm=S)]
    merged = _gate_fwd(proj, w["b_gate"], br)
    mo = _mm_nn("out_proj", merged, w["w_out"], F32, tm=S)
    saved = dict(proj=proj, src_b=src_b, o_a32=o_a32, lse_a=lse_a, o_b32=o_b32, lse_b=lse_b, tot_c=tot_c,
                 o_a=o_a, o_b=o_b, o_c=o_c, br=br, merged=merged)
    return mo, saved, w


def _mixer_bwd(d_mo, h1, w, sv, rel_bias, sinks_l, bidx, stats_in, emit):
    grads = {}
    dmerged = _mm_nt("out_proj_dx", d_mo, w["w_out"], F32, tm=S)
    grads["w_out"] = _mm_tn_sharded("out_proj_dw", sv["merged"], d_mo, True)
    e, dgate, db_gate = _gate_bwd(sv["proj"], w["b_gate"], sv["br"], dmerged)
    grads["b_gate"] = db_gate
    d_o = {}
    for n, name in enumerate("abc"):
        d_o[name] = _mm_nt("branch_%s_dx" % name, e[n], w["w_br_" + name], F32, tm=S)
        grads["w_br_" + name] = _mm_tn_sharded("branch_%s_dw" % name, sv["o_" + name], e[n], False)
    zero = emit(1, grads)
    no_sinks = jnp.full((N_A,), NEG, F32) + zero[0]
    dq_a, dk_a, dv_a, st_a = _band_bwd("band_bwd_a", A_DILS, A_PAIRS, BLK, 0, _src_a(sv["proj"]), bidx[:3], rel_bias,
                                       no_sinks, sv["o_a32"], sv["lse_a"], d_o["a"], stats_in[:N_A])
    dq_b, dk_b, dv_b, st_b = _band_bwd("band_bwd_b", (1,), 4, BLK - 1, N_A, sv["src_b"], bidx[3:], rel_bias, sinks_l,
                                       sv["o_b32"], sv["lse_b"], d_o["b"], stats_in[N_A:], kv_shared=True)
    stats = jnp.concatenate([st_a, st_b], axis=0)
    dcq, dck, dcv = _sb_bwd(sv["proj"], sv["tot_c"], d_o["c"])
    cols = [dq_a, dk_a, dv_a, dq_b, dk_b, dv_b, dcq, dck, dcv]
    dproj = jnp.concatenate([t.astype(BF16) for t in cols] + list(dgate), axis=1)
    grads["w_in"] = _mm_tn("proj_in_dw", dproj, h1, BF16, tm=1152, tn=1024).reshape(N_CHIPS, IN_SHARD, D)
    zero = emit(2, grads)
    dh1 = _mm_nn("proj_in_dx", dproj, w["w_in"], F32, tm=S, tk=2304)
    return dh1, grads, stats, zero


def _ffn_fwd(h2, w):
    u = _mm_nn("ffn_up", h2, w["w_up"], F32, tm=S, tn=1024)
    a = _conv_fwd(u, w["conv_w"], w["conv_b"])
    dn = _mm_nn("ffn_down", a, w["w_down"], F32, tm=1024)
    return dn, dict(u=u, a=a)


def _ffn_bwd(d_dn, h2, w, sv):
    grads = {}
    da = _mm_nt("ffn_down_dx", d_dn, w["w_down"], F32, tm=S, tn=1024)
    grads["w_down"] = _mm_tn_sharded("ffn_down_dw", sv["a"], d_dn, True, tm=1024, tn=1024)
    dug, duv, dwg, dwv, dbg, dbv = _conv_bwd(sv["u"], w["conv_w"], w["conv_b"], da)
    du = jnp.concatenate([dug, duv], axis=1)
    grads["conv_w"] = jnp.concatenate([dwg, dwv], axis=1)
    grads["conv_b"] = jnp.concatenate([dbg, dbv], axis=1)
    dh2 = _mm_nt("ffn_up_dx", du, w["w_up"], F32, tm=S, tk=2048)
    grads["w_up"] = _mm_tn_sharded("ffn_up_dw", h2, du, False, tm=1024, tn=1024)
    return dh2, grads


BIG = ("w_in", "w_br_a", "w_br_b", "w_br_c", "w_out", "w_up", "w_down")


def _shard_view(name, w):
    return jnp.swapaxes(w, 1, 2) if name == "w_in" else w
WEIGHT_GROUPS = (("w_in", "b_gate"), ("w_br_a", "w_br_b", "w_br_c", "w_out"), ("w_up", "conv_w", "w_down"))
GRAD_GROUPS = (("w_down", "w_up"), ("w_out", "w_br_a", "w_br_b", "w_br_c"), ("w_in",))
SMALL_ROWS = (("rel_bias", 8), ("attn_pre_norm", 16), ("attn_post_norm", 16), ("ffn_pre_norm", 16), ("ffn_post_norm", 16),
              ("sinks", 8), ("conv_b", 128), ("b_gate", 48), ("conv_w", 384), ("loss", 8))


def _pack_small(vals):
    rows = []
    for name, n in SMALL_ROWS:
        flat = vals[name].reshape(-1).astype(F32)
        rows.append(jnp.pad(flat, (0, n * 128 - flat.shape[0])).reshape(n, 128))
    return jnp.concatenate(rows, axis=0)


def _unpack_small(block, shapes):
    out, row = {}, 0
    for name, n in SMALL_ROWS:
        size = int(np.prod(shapes[name]))
        out[name] = block[row:row + n].reshape(-1)[:size].reshape(shapes[name])
        row += n
    return out


def kernel(x, rel_bias, attn_pre_norm, w_in, b_gate, sinks, w_br_a, w_br_b, w_br_c, w_out, attn_post_norm, ffn_pre_norm, w_up, conv_w, conv_b, w_down, ffn_post_norm, loss_target, m_rel_bias, m_attn_pre_norm, m_w_in, m_b_gate, m_sinks, m_w_br_a, m_w_br_b, m_w_br_c, m_w_out, m_attn_post_norm, m_ffn_pre_norm, m_w_up, m_conv_w, m_conv_b, m_w_down, m_ffn_post_norm, v_rel_bias, v_attn_pre_norm, v_w_in, v_b_gate, v_sinks, v_w_br_a, v_w_br_b, v_w_br_c, v_w_out, v_attn_post_norm, v_ffn_pre_norm, v_w_up, v_conv_w, v_conv_b, v_w_down, v_ffn_post_norm):
    names = ("rel_bias", "attn_pre_norm", "w_in", "b_gate", "sinks", "w_br_a", "w_br_b", "w_br_c", "w_out",
             "attn_post_norm", "ffn_pre_norm", "w_up", "conv_w", "conv_b", "w_down", "ffn_post_norm")
    weights = dict(zip(names, (rel_bias, attn_pre_norm, w_in, b_gate, sinks, w_br_a, w_br_b, w_br_c, w_out,
                               attn_post_norm, ffn_pre_norm, w_up, conv_w, conv_b, w_down, ffn_post_norm)))
    mom1 = dict(zip(names, (m_rel_bias, m_attn_pre_norm, m_w_in, m_b_gate, m_sinks, m_w_br_a, m_w_br_b, m_w_br_c,
                            m_w_out, m_attn_post_norm, m_ffn_pre_norm, m_w_up, m_conv_w, m_conv_b, m_w_down,
                            m_ffn_post_norm)))
    mom2 = dict(zip(names, (v_rel_bias, v_attn_pre_norm, v_w_in, v_b_gate, v_sinks, v_w_br_a, v_w_br_b, v_w_br_c,
                            v_w_out, v_attn_post_norm, v_ffn_pre_norm, v_w_up, v_conv_w, v_conv_b, v_w_down,
                            v_ffn_post_norm)))

    chip = 2 * lax.axis_index("x") + lax.axis_index("y")
    core = lax.axis_index("c")

    keys = [(n, l) for l in range(DEPTH) for group in WEIGHT_GROUPS for n in group]
    groups = [[keys.index((n, l)) for n in group] for l in range(DEPTH) for group in WEIGHT_GROUPS]

    def slot_buffer(n, l):
        if n in BIG:
            return _cast_into_slot("cast_" + n, _shard_view(n, weights[n]), l, chip)
        shard = weights[n][l]
        return lax.dynamic_update_slice(jnp.zeros((N_CHIPS,) + shard.shape, F32), shard[None],
                                        (chip, jnp.int32(0), jnp.int32(0)))

    by_halves = [a for a, (n, _) in enumerate(keys) if n in BIG]
    n_first = len(groups[0])
    sems, in_flight, first = _gather_start("gather_start_first", [slot_buffer(*k) for k in keys[:n_first]], groups[:1],
                                           tuple(a for a in by_halves if a < n_first))
    more = _gather_start("gather_start", [slot_buffer(*k) for k in keys[n_first:]],
                         [[a - n_first for a in g] for g in groups[1:]],
                         tuple(a - n_first for a in by_halves if a >= n_first))
    sems, in_flight, started = sems + more[0], in_flight + more[1], more[2]

    def wget(l, gi, after):
        g = l * len(WEIGHT_GROUPS) + gi
        after = started if g == 0 else after
        halved = tuple(e for e, a in enumerate(groups[g]) if a in by_halves)
        got = list(_gather_wait("gather_wait_%d_%d" % (l, gi), [in_flight[a] for a in groups[g]], *sems[g], after,
                                halved))
        if halved:
            for e, buf in zip(halved, _swap_halves("swap_halves_%d_%d" % (l, gi), [got[e] for e in halved])):
                got[e] = buf
        out = {}
        for n, buf in zip(WEIGHT_GROUPS[gi], got):
            if n in ("w_in", "w_out", "w_down"):
                out[n] = buf.reshape(-1, buf.shape[-1])
            else:
                out[n] = buf if n == "w_up" else _full_cols(buf)
        if gi == len(WEIGHT_GROUPS) - 1:
            out["conv_b"] = conv_b[l:l + 1]
        return out

    pending = []

    def emit(l, gi, grads):
        group = GRAD_GROUPS[gi]
        *started, token = _reduce_start("reduce_start_%d_%d" % (l, gi), [grads[n] for n in group])
        pending.append((l, group) + tuple(started))
        return token[:1, :1]

    local = _local_step(x.reshape(S, D), loss_target.reshape(S, D), wget, emit, rel_bias, sinks,
                        attn_pre_norm + first[:1, :1], attn_post_norm, ffn_pre_norm, ffn_post_norm)
    return _reduce_and_update(x.shape, names, weights, mom1, mom2, chip, core, pending, *local)


def _local_step(xs, target, wget, emit, rel_bias, sinks, attn_pre_norm, attn_post_norm, ffn_pre_norm, ffn_post_norm):
    bidx = jnp.asarray(_bucket_maps())

    saved, layers = [], []
    h1 = _rms_fwd("pre_norm_first", xs, attn_pre_norm[0:1])
    x_in = xs
    for l in range(DEPTH):
        mo, sv_mix, w = _mixer_fwd(h1, functools.partial(wget, l), rel_bias, sinks[l], bidx)
        x_mid, h2 = _post_pre_fwd("post_attn_norm", x_in, mo, attn_post_norm[l:l + 1], ffn_pre_norm[l:l + 1])
        w.update(wget(l, 2, h2))
        dn, sv_ffn = _ffn_fwd(h2, w)
        g_next = attn_pre_norm[l + 1:l + 2] if l + 1 < DEPTH else None
        x_out, h1_next = _post_pre_fwd("post_ffn_norm" if l + 1 < DEPTH else "post_ffn_norm_last", x_mid, dn,
                                       ffn_post_norm[l:l + 1], g_next)
        saved.append(dict(x_in=x_in, h1=h1, mo=mo, x_mid=x_mid, h2=h2, dn=dn, mix=sv_mix, ffn=sv_ffn))
        layers.append(w)
        x_in, h1 = x_out, h1_next

    loss_row, dres = _loss_kernel(x_in, target)

    small = [None] * DEPTH
    stats = jnp.zeros((N_BAND_Q, 8, 128), F32)
    dh_next = None
    for l in reversed(range(DEPTH)):
        w, sv = layers[l], saved[l]
        if l + 1 < DEPTH:
            pre = (saved[l + 1]["x_in"], attn_pre_norm[l + 1:l + 2] + zero, dh_next)
            dres, d_dn, dg_pre_next, dg_fpost = _norm_bwd("post_ffn_norm_bwd", dres, pre,
                                                          (sv["dn"], ffn_post_norm[l:l + 1]))
            small[l + 1]["attn_pre_norm"] = dg_pre_next
        else:
            dres, d_dn, _, dg_fpost = _norm_bwd("post_ffn_norm_last_bwd", dres, None, (sv["dn"], ffn_post_norm[l:l + 1]))
        dh2, g_ffn = _ffn_bwd(d_dn, sv["h2"], w, sv["ffn"])
        zero = emit(l, 0, g_ffn)
        dres, d_mo, dg_fpre, dg_apost = _norm_bwd("post_attn_norm_bwd", dres,
                                                  (sv["x_mid"], ffn_pre_norm[l:l + 1] + zero, dh2),
                                                  (sv["mo"], attn_post_norm[l:l + 1]))
        dh_next, g_mix, stats, zero = _mixer_bwd(d_mo, sv["h1"], w, sv["mix"], rel_bias, sinks[l], bidx, stats,
                                                 functools.partial(emit, l))
        small[l] = dict(ffn_post_norm=dg_fpost, ffn_pre_norm=dg_fpre, attn_post_norm=dg_apost,
                        sinks=stats[N_A:, 1, 0], conv_b=g_ffn["conv_b"], b_gate=g_mix["b_gate"], conv_w=g_ffn["conv_w"])
    grad_x, _, dg_pre0, _ = _norm_bwd("pre_norm_first_bwd", dres, (saved[0]["x_in"], attn_pre_norm[0:1] + zero, dh_next),
                                      None)
    small[0]["attn_pre_norm"] = dg_pre0
    return loss_row, grad_x, small, stats


def _reduce_and_update(x_shape, names, weights, mom1, mom2, chip, core, pending, loss_row, grad_x, small, stats):
    delta, new_m, new_v, grads = {}, {}, {}, {}

    def update(n, g):
        grads[n], delta[n], new_m[n], new_v[n] = _adamw("adamw_" + n, _shard_view(n, weights[n]), g,
                                                        _shard_view(n, mom1[n]), _shard_view(n, mom2[n]))

    small_vals = {n: jnp.stack([small[l][n].reshape(weights[n].shape[1:]) for l in range(DEPTH)])
                  for n in ("attn_pre_norm", "attn_post_norm", "ffn_pre_norm", "ffn_post_norm", "conv_b", "sinks")}
    small_vals["b_gate"] = jnp.stack([small[l]["b_gate"] for l in range(DEPTH)])
    small_vals["conv_w"] = jnp.stack([small[l]["conv_w"] for l in range(DEPTH)])
    small_vals["rel_bias"] = stats[:, 0, :NUM_BUCKETS].T
    small_vals["loss"] = loss_row[0, :1]
    shapes = {n: v.shape for n, v in small_vals.items()}
    small_send, small_recv, packed, small_land, started = _small_start(_pack_small(small_vals))

    summed = {}

    def finish(which, after):
        for l, group, send, recv, gs, lands in pending:
            if (group == ("w_in",)) == which:
                gs, lands = _reduce_wait("reduce_wait_%d_%s" % (l, group[0]), send, recv, gs, lands, after)
                for n, g, land in zip(group, gs, lands):
                    summed[n] = _reduce_sum("reduce_sum_%d_%s" % (l, n), g, land, l, summed.get(n), chip, core)

    finish(False, started)
    early = [n for n in BIG if n != "w_in"]
    for n, g in zip(early, _join_halves("join_halves", [summed[n] for n in early])):
        update(n, g)
    finish(True, delta[early[-1]])
    update("w_in", _join_halves("join_halves_w_in", [summed["w_in"]])[0])

    packed, small_land = _small_wait(small_send, small_recv, packed, small_land, delta["w_in"])
    small_land = lax.dynamic_update_slice(small_land, packed[None], (2 * chip + core, jnp.int32(0), jnp.int32(0)))
    reduced = _unpack_small(_small_sum(small_land), shapes)
    reduced["b_gate"] = lax.dynamic_slice_in_dim(reduced["b_gate"], chip * (D // N_CHIPS), D // N_CHIPS, axis=2)
    reduced["conv_w"] = lax.dynamic_slice_in_dim(reduced["conv_w"], chip * (2 * D_FF // N_CHIPS), 2 * D_FF // N_CHIPS, axis=2)
    for n in names:
        if n not in grads:
            update(n, reduced[n].reshape(weights[n].shape))
    for out in (grads, delta, new_m, new_v):
        out["w_in"] = _shard_view("w_in", out["w_in"])

    loss = reduced["loss"].reshape(())
    return (loss, grad_x.reshape(x_shape), *[grads[n] for n in names], *[delta[n] for n in names],
            *[new_m[n] for n in names], *[new_v[n] for n in names])
```

```python
import functools
import math

import numpy as np
import jax
import jax.numpy as jnp
from jax import lax
from jax.experimental import pallas as pl
from jax.experimental.pallas import tpu as pltpu

F32 = jnp.float32
BF16 = jnp.bfloat16

S = 2048
D = 1024
DEPTH = 2
HD = 64
BLK = 128
NQB = S // BLK
A_GROUPS = ((128, 1), (512, 4), (2048, 16))
N_BAND_Q = 20
N_A = 12
NUM_BUCKETS = 32
MAX_DISTANCE = 2048
D_FF = 4096
IN_COLS = 6912
IN_SHARD = IN_COLS // 4
OFF_GATE = 3840
EPS = 1e-6
SCALE = HD ** -0.5
NEG = -1e30
N_CHIPS = 4
N_DEV = 8

ADAM_LR = 0.001
ADAM_B1 = 0.9
ADAM_B2 = 0.999
ADAM_EPS = 1e-08
ADAM_WD = 0.01
ADAM_STEP = 10

VMEM_LIMIT = 56 * 1024 * 1024

NN = (((1,), (0,)), ((), ()))
NT = (((1,), (1,)), ((), ()))
TN = (((0,), (0,)), ((), ()))

MESH = pl.DeviceIdType.MESH
ANY = pl.BlockSpec(memory_space=pl.ANY)


def _dot(a, b, dims):
    return lax.dot_general(a, b, dims, preferred_element_type=F32)


def _params(sem):
    return pltpu.CompilerParams(dimension_semantics=sem, vmem_limit_bytes=VMEM_LIMIT)


def _matmul(name, a, b, out_shape, out_dtype, grid, a_spec, b_spec, o_spec, dims, acc_shape):
    nk = grid[-1]

    def body(a_ref, b_ref, o_ref, *scratch):
        part = _dot(a_ref[...].astype(BF16), b_ref[...].astype(BF16), dims)
        if nk == 1:
            o_ref[...] = part.astype(o_ref.dtype)
            return
        acc_ref, = scratch
        k = pl.program_id(len(grid) - 1)

        @pl.when(k == 0)
        def _():
            acc_ref[...] = part

        @pl.when(k > 0)
        def _():
            acc_ref[...] += part

        @pl.when(k == nk - 1)
        def _():
            o_ref[...] = acc_ref[...].astype(o_ref.dtype)

    scratch = [] if nk == 1 else [pltpu.VMEM(acc_shape, F32)]
    sem = ("parallel",) * (len(grid) - 1) + ("arbitrary",)
    return pl.pallas_call(
        body, name=name, grid=grid, in_specs=[a_spec, b_spec], out_specs=o_spec,
        out_shape=jax.ShapeDtypeStruct(out_shape, out_dtype), scratch_shapes=scratch,
        compiler_params=_params(sem))(a, b)


FULL_K = 8192


def _mm_tn_sharded(name, a, b, row_sharded, tm=512, tn=512, tk=FULL_K):
    k, m = a.shape
    n = b.shape[1]
    m4, n4 = (m // N_CHIPS, n) if row_sharded else (m, n // N_CHIPS)
    tm, tn, tk = min(tm, m4), min(tn, n4), min(tk, k)
    per_m, per_n = m4 // tm, n4 // tn
    if row_sharded:
        o_map = lambda i, j, l: (i // per_m, i % per_m, j)
    else:
        o_map = lambda i, j, l: (j // per_n, i, j % per_n)
    return _matmul(name, a, b, (N_CHIPS, m4, n4), BF16, (m // tm, n // tn, k // tk),
                   pl.BlockSpec((tk, tm), lambda i, j, l: (l, i)),
                   pl.BlockSpec((tk, tn), lambda i, j, l: (l, j)),
                   pl.BlockSpec((None, tm, tn), o_map), TN, (tm, tn))


def _mm_nn(name, a, b, out_dtype, tm=512, tn=512, tk=FULL_K):
    m, k = a.shape
    n = b.size // k
    tm, tn, tk = min(tm, m), min(tn, b.shape[-1]), min(tk, k)
    per_shard = b.shape[-1] // tn
    if b.ndim == 2:
        b_spec = pl.BlockSpec((tk, tn), lambda i, j, l: (l, j))
    else:
        b_spec = pl.BlockSpec((None, tk, tn), lambda i, j, l: (j // per_shard, l, j % per_shard))
    return _matmul(name, a, b, (m, n), out_dtype, (m // tm, n // tn, k // tk),
                   pl.BlockSpec((tm, tk), lambda i, j, l: (i, l)), b_spec,
                   pl.BlockSpec((tm, tn), lambda i, j, l: (i, j)), NN, (tm, tn))


def _mm_nt(name, a, b, out_dtype, tm=512, tn=512, tk=FULL_K):
    m, k = a.shape
    n = b.shape[-2]
    tm, tn, tk = min(tm, m), min(tn, n), min(tk, b.shape[-1])
    per_shard = b.shape[-1] // tk
    if b.ndim == 2:
        b_spec = pl.BlockSpec((tn, tk), lambda i, j, l: (j, l))
    else:
        b_spec = pl.BlockSpec((None, tn, tk), lambda i, j, l: (l // per_shard, j, l % per_shard))
    return _matmul(name, a, b, (m, n), out_dtype, (m // tm, n // tn, k // tk),
                   pl.BlockSpec((tm, tk), lambda i, j, l: (i, l)), b_spec,
                   pl.BlockSpec((tm, tn), lambda i, j, l: (i, j)), NT, (tm, tn))


def _mm_tn(name, a, b, out_dtype, tm=512, tn=512, tk=FULL_K):
    k, m = a.shape
    n = b.shape[1]
    tm, tn, tk = min(tm, m), min(tn, n), min(tk, k)
    return _matmul(name, a, b, (m, n), out_dtype, (m // tm, n // tn, k // tk),
                   pl.BlockSpec((tk, tm), lambda i, j, l: (l, i)),
                   pl.BlockSpec((tk, tn), lambda i, j, l: (l, j)),
                   pl.BlockSpec((tm, tn), lambda i, j, l: (i, j)), TN, (tm, tn))


TR = 512


def _row_spec(width=D):
    return pl.BlockSpec((TR, width), lambda i: (i, 0))


def _vec_spec(width=D):
    return pl.BlockSpec((1, width), lambda i: (0, 0))


def _rms(x, g):
    r = lax.rsqrt(jnp.mean(x * x, axis=-1, keepdims=True) + EPS)
    return x * r * g


def _rms_fwd(name, x, g):
    def body(x_ref, g_ref, h_ref):
        h_ref[...] = _rms(x_ref[...], g_ref[...]).astype(BF16)

    return pl.pallas_call(
        body, name=name, grid=(S // TR,), in_specs=[_row_spec(), _vec_spec()], out_specs=_row_spec(),
        out_shape=jax.ShapeDtypeStruct((S, D), BF16), compiler_params=_params(("parallel",)))(x, g)


def _post_pre_fwd(name, x, y, g_post, g_pre):
    has_pre = g_pre is not None

    def body(*refs):
        if has_pre:
            x_ref, y_ref, gp_ref, gn_ref, xn_ref, h_ref = refs
        else:
            x_ref, y_ref, gp_ref, xn_ref = refs
        xn = x_ref[...] + _rms(y_ref[...], gp_ref[...])
        xn_ref[...] = xn
        if has_pre:
            h_ref[...] = _rms(xn, gn_ref[...]).astype(BF16)

    ins = [x, y, g_post] + ([g_pre] if has_pre else [])
    in_specs = [_row_spec(), _row_spec(), _vec_spec()] + ([_vec_spec()] if has_pre else [])
    out_shape = [jax.ShapeDtypeStruct((S, D), F32)] + ([jax.ShapeDtypeStruct((S, D), BF16)] if has_pre else [])
    out_specs = [_row_spec()] + ([_row_spec()] if has_pre else [])
    out = pl.pallas_call(
        body, name=name, grid=(S // TR,), in_specs=in_specs, out_specs=out_specs, out_shape=out_shape,
        compiler_params=_params(("parallel",)))(*ins)
    return out if has_pre else (out[0], None)


def _rms_bwd_math(x, g, dy):
    r = lax.rsqrt(jnp.mean(x * x, axis=-1, keepdims=True) + EPS)
    n = x * r
    dn = dy * g
    dx = r * (dn - n * jnp.mean(dn * n, axis=-1, keepdims=True))
    return dx, jnp.sum(dy * n, axis=0, keepdims=True)


def _norm_bwd(name, dres, pre=None, post=None):
    has_pre, has_post = pre is not None, post is not None

    def body(*refs):
        refs = list(refs)
        dres_ref = refs.pop(0)
        if has_pre:
            xn_ref, gn_ref, dh_ref = refs[:3]
            refs = refs[3:]
        if has_post:
            y_ref, gp_ref = refs[:2]
            refs = refs[2:]
        dxn_ref = refs.pop(0)
        dy_ref = refs.pop(0) if has_post else None
        dgn_ref = refs.pop(0) if has_pre else None
        dgp_ref = refs.pop(0) if has_post else None
        first = pl.program_id(0) == 0
        dxn = dres_ref[...]
        if has_pre:
            dx, dg = _rms_bwd_math(xn_ref[...], gn_ref[...], dh_ref[...])
            dxn = dxn + dx

            @pl.when(first)
            def _():
                dgn_ref[...] = dg

            @pl.when(jnp.logical_not(first))
            def _():
                dgn_ref[...] += dg
        dxn_ref[...] = dxn
        if has_post:
            dy, dg = _rms_bwd_math(y_ref[...], gp_ref[...], dxn)
            dy_ref[...] = dy.astype(BF16)

            @pl.when(first)
            def _():
                dgp_ref[...] = dg

            @pl.when(jnp.logical_not(first))
            def _():
                dgp_ref[...] += dg

    ins, in_specs = [dres], [_row_spec()]
    if has_pre:
        ins += list(pre)
        in_specs += [_row_spec(), _vec_spec(), _row_spec()]
    if has_post:
        ins += list(post)
        in_specs += [_row_spec(), _vec_spec()]
    out_shape, out_specs = [jax.ShapeDtypeStruct((S, D), F32)], [_row_spec()]
    if has_post:
        out_shape.append(jax.ShapeDtypeStruct((S, D), BF16))
        out_specs.append(_row_spec())
    for _ in range(int(has_pre) + int(has_post)):
        out_shape.append(jax.ShapeDtypeStruct((1, D), F32))
        out_specs.append(_vec_spec())
    out = list(pl.pallas_call(
        body, name=name, grid=(S // TR,), in_specs=in_specs, out_specs=out_specs, out_shape=out_shape,
        compiler_params=_params(("arbitrary",)))(*ins))
    dxn = out.pop(0)
    dy = out.pop(0) if has_post else None
    dgn = out.pop(0) if has_pre else None
    dgp = out.pop(0) if has_post else None
    return dxn, dy, dgn, dgp


def _loss_kernel(y, target):
    def body(y_ref, t_ref, loss_ref, dy_ref):
        e = y_ref[...] - t_ref[...]
        dy_ref[...] = e * (1.0 / D)
        part = jnp.zeros((1, 128), F32) + 0.5 * jnp.sum(jnp.mean(e * e, axis=-1, keepdims=True))

        @pl.when(pl.program_id(0) == 0)
        def _():
            loss_ref[...] = part

        @pl.when(pl.program_id(0) > 0)
        def _():
            loss_ref[...] += part

    return pl.pallas_call(
        body, name="loss", grid=(S // TR,), in_specs=[_row_spec(), _row_spec()],
        out_specs=[_vec_spec(128), _row_spec()],
        out_shape=[jax.ShapeDtypeStruct((1, 128), F32), jax.ShapeDtypeStruct((S, D), F32)],
        compiler_params=_params(("arbitrary",)))(y, target)


def _t5_bucket_np(dist):
    max_exact = NUM_BUCKETS // 2
    nf = np.maximum(dist, 1).astype(np.float32)
    large = max_exact + (np.log(nf / max_exact) / np.float32(math.log(MAX_DISTANCE / max_exact))
                         * (NUM_BUCKETS - max_exact)).astype(np.int32)
    large = np.minimum(large, NUM_BUCKETS - 1)
    return np.where(dist < max_exact, dist, large).astype(np.int32)


def _bucket_maps():
    a = np.arange(BLK)[:, None]
    b = np.arange(2 * BLK)[None, :]
    dist = np.maximum(a + BLK - b, 0)
    maps = [_t5_bucket_np(dist * d) for _, d in A_GROUPS] + [_t5_bucket_np(dist)]
    return np.stack(maps).astype(np.int32)


def _pair_spec(col0):
    return pl.BlockSpec((S, 128), lambda p: (0, col0 + p))


def _band_rows(i, d):
    nb = S // d // BLK
    r, b = i // nb, i % nb
    cur = pl.ds(b * BLK * d + r, BLK, stride=d)
    prev = pl.ds(jnp.maximum(b - 1, 0) * BLK * d + r, BLK, stride=d)
    return cur, prev, jnp.minimum(b, 1)


def _band_bias(tab_ref, bi, h):
    bias = jnp.zeros((BLK, 2 * BLK), F32)
    for kk in range(NUM_BUCKETS):
        bias = jnp.where(bi == kk, tab_ref[kk, h], bias)
    return bias


def _lane_lo(rows=BLK):
    return lax.broadcasted_iota(jnp.int32, (rows, 128), 1) < HD


def _per_head(x, lo):
    return (jnp.sum(jnp.where(lo, x, 0.0), axis=1, keepdims=True) * (1.0 / HD),
            jnp.sum(jnp.where(lo, 0.0, x), axis=1, keepdims=True) * (1.0 / HD))


def _band_fill(bias_ref, tab_ref, bi, head, maxd):
    a = lax.broadcasted_iota(jnp.int32, (BLK, 2 * BLK), 0)
    c = lax.broadcasted_iota(jnp.int32, (BLK, 2 * BLK), 1)
    dist = a + BLK - c
    in_band = jnp.logical_and(dist >= 0, dist <= maxd)
    for h in range(2):
        bias = jnp.where(in_band, _band_bias(tab_ref, bi, head + h), NEG)
        bias_ref[1, h * BLK:(h + 1) * BLK, :] = bias
        bias_ref[0, h * BLK:(h + 1) * BLK, :] = jnp.where(c >= BLK, bias, NEG)


def _stack_heads(x, lo, dtype=BF16):
    return jnp.concatenate([jnp.where(lo, x, 0.0), jnp.where(lo, 0.0, x)], axis=0).astype(dtype)


def _unstack_heads(x, lo):
    n = x.shape[0] // 2
    return jnp.where(lo, x[:n], x[n:])


def _stack_rows(ref, prev, cur):
    return jnp.concatenate([ref[prev, :], ref[cur, :]], axis=0).astype(BF16)


PAIRS_PER_KV = 2


def _kv_specs(kc, vc, kv_shared):
    if not kv_shared:
        return [_pair_spec(kc), _pair_spec(vc)], []
    shared = [pl.BlockSpec((S, 128), functools.partial(lambda p, c: (0, c), c=c)) for c in (kc, vc)]
    return shared, [pltpu.VMEM((S, 128), F32)] * 2


def _expand_kv(dst_ref, src_ref, pair):
    x = src_ref[...]
    own = lax.broadcasted_iota(jnp.int32, (S, 128), 1) // HD == pair // PAIRS_PER_KV
    dst_ref[...] = jnp.where(own, x, pltpu.roll(x, HD, 1))


def _fold_kv(out_ref, acc_ref, pair):
    x = acc_ref[...]
    own = lax.broadcasted_iota(jnp.int32, (S, 128), 1) // HD == pair // PAIRS_PER_KV
    part = jnp.where(own, x + pltpu.roll(x, HD, 1), 0.0)

    @pl.when(pair == 0)
    def _():
        out_ref[...] = part

    @pl.when(pair > 0)
    def _():
        out_ref[...] += part


def _blocks_of_group(group, dils, block):
    def run(d):
        lax.fori_loop(0, NQB, functools.partial(block, d), 0, unroll=2)

    if len(dils) == 1:
        run(dils[0])
        return
    for g, d in enumerate(dils):
        pl.when(group == g)(functools.partial(run, d))


def _band_fwd(name, dils, n_pairs, maxd, head0, srcs, bidx_g, tab, sinks, kv_shared=False):
    (qa, qc), (ka, kc), (va, vc) = srcs
    per_group = n_pairs // len(dils)
    out_spec = _pair_spec(0)
    smem = pl.BlockSpec(memory_space=pltpu.SMEM)
    full = pl.BlockSpec((len(dils), BLK, 2 * BLK), lambda p: (0, 0, 0))

    kv_specs, kv_scratch = _kv_specs(kc, vc, kv_shared)

    def body(tab_ref, sink_ref, q_ref, k_ref, v_ref, bidx_ref, o_ref, lse_ref, bias_ref, *expanded):
        p = pl.program_id(0)
        if kv_shared:
            _expand_kv(expanded[0], k_ref, p)
            _expand_kv(expanded[1], v_ref, p)
            k_ref, v_ref = expanded
        _band_fill(bias_ref, tab_ref, bidx_ref[p // per_group], head0 + 2 * p, maxd)
        lo = _lane_lo()
        sink = jnp.where(lax.broadcasted_iota(jnp.int32, (2 * BLK, 1), 0) < BLK, sink_ref[2 * p], sink_ref[2 * p + 1])

        def block(d, i, carry):
            cur, prev, has_prev = _band_rows(i, d)
            qs = _stack_heads(q_ref[cur, :] * SCALE, lo)
            ks, vs = _stack_rows(k_ref, prev, cur), _stack_rows(v_ref, prev, cur)
            s = _dot(qs, ks, NT) + bias_ref[has_prev]
            m = jnp.max(s, axis=1, keepdims=True)
            pr = jnp.exp(s - m)
            l = jnp.sum(pr, axis=1, keepdims=True)
            num = _dot(pr.astype(BF16), vs, NN)
            lse = m + jnp.log(l)
            sig = 1.0 / (1.0 + jnp.exp(sink - lse))
            o_ref[cur, :] = _unstack_heads(num * (sig / l), lo)
            lse_ref[cur, :] = _unstack_heads(lse + jnp.zeros((2 * BLK, 128), F32), lo)
            return carry

        _blocks_of_group(p // per_group, dils, block)

    shape = jax.ShapeDtypeStruct((S, n_pairs * 128), F32)
    return pl.pallas_call(
        body, name=name, grid=(n_pairs,),
        in_specs=[smem, smem, _pair_spec(qc)] + kv_specs + [full],
        out_specs=[out_spec, out_spec], out_shape=[shape, shape],
        scratch_shapes=[pltpu.VMEM((2, 2 * BLK, 2 * BLK), F32)] + kv_scratch,
        compiler_params=_params(("parallel",)))(tab, sinks, qa, ka, va, bidx_g)


def _band_bwd(name, dils, n_pairs, maxd, head0, srcs, bidx_g, tab, sinks, o, lse, do, stats_in, kv_shared=False):
    (qa, qc), (ka, kc), (va, vc) = srcs
    per_group = n_pairs // len(dils)
    pair = _pair_spec(0)
    shared = pl.BlockSpec((S, 128), lambda p: (0, p % per_group))
    smem = pl.BlockSpec(memory_space=pltpu.SMEM)
    full = pl.BlockSpec((len(dils), BLK, 2 * BLK), lambda p: (0, 0, 0))
    stat_spec = pl.BlockSpec((2, 8, 128), lambda p: (p, 0, 0))
    kv_specs, kv_scratch = _kv_specs(kc, vc, kv_shared)

    def body(tab_ref, sink_ref, q_ref, k_ref, v_ref, bidx_ref, o_ref, lse_ref, do_ref, sin_ref,
             dq_ref, dk_ref, dv_ref, stat_ref, bias_ref, dsacc_ref, sk_ref, *expanded):
        p = pl.program_id(0)
        if kv_shared:
            _expand_kv(expanded[0], k_ref, p)
            _expand_kv(expanded[1], v_ref, p)
            k_ref, v_ref = expanded[:2]
            dk_out, dv_out, dk_ref, dv_ref = dk_ref, dv_ref, expanded[2], expanded[3]
        _band_fill(bias_ref, tab_ref, bidx_ref[p // per_group], head0 + 2 * p, maxd)
        dsacc_ref[...] = jnp.zeros_like(dsacc_ref)
        sk_ref[...] = jnp.zeros_like(sk_ref)
        dk_ref[...] = jnp.zeros_like(dk_ref)
        dv_ref[...] = jnp.zeros_like(dv_ref)
        lo = _lane_lo()
        head1 = lax.broadcasted_iota(jnp.int32, (2 * BLK, 1), 0) >= BLK
        sink = jnp.where(head1, sink_ref[2 * p + 1], sink_ref[2 * p])

        def block(d, i, carry):
            cur, prev, has_prev = _band_rows(i, d)
            qs = _stack_heads(q_ref[cur, :] * SCALE, lo)
            ks, vs = _stack_rows(k_ref, prev, cur), _stack_rows(v_ref, prev, cur)
            do = do_ref[cur, :]
            dos = _stack_heads(do, lo, F32)
            lse = jnp.concatenate(_per_head(lse_ref[cur, :], lo), axis=0)
            prod = do * o_ref[cur, :]
            delta = jnp.concatenate([jnp.sum(jnp.where(lo, prod, 0.0), axis=1, keepdims=True),
                                     jnp.sum(jnp.where(lo, 0.0, prod), axis=1, keepdims=True)], axis=0)
            sig = 1.0 / (1.0 + jnp.exp(sink - lse))
            pr = jnp.exp(_dot(qs, ks, NT) + bias_ref[has_prev] - lse)
            ds = pr * (sig * (_dot(dos.astype(BF16), vs, NT) - delta))
            dsb = ds.astype(BF16)
            dq_ref[cur, :] = SCALE * _unstack_heads(_dot(dsb, ks, NN), lo)
            dk = _dot(dsb, qs, TN)
            dv = _dot(pr.astype(BF16), (sig * dos).astype(BF16), TN)
            dk_ref[prev, :] += dk[:BLK]
            dk_ref[cur, :] += dk[BLK:]
            dv_ref[prev, :] += dv[:BLK]
            dv_ref[cur, :] += dv[BLK:]
            dsacc_ref[...] += ds
            sink_grad = -delta * (1.0 - sig)
            for h in range(2):
                sk_ref[h] += jnp.zeros((8, 128), F32) + jnp.sum(sink_grad[h * BLK:(h + 1) * BLK])
            return carry

        _blocks_of_group(p // per_group, dils, block)
        if kv_shared:
            _fold_kv(dk_out, dk_ref, p)
            _fold_kv(dv_out, dv_ref, p)

        bi = bidx_ref[p // per_group]
        lane = lax.broadcasted_iota(jnp.int32, (8, 128), 1)
        sub = lax.broadcasted_iota(jnp.int32, (8, 128), 0)
        for h in range(2):
            acc = dsacc_ref[h * BLK:(h + 1) * BLK, :]
            row = jnp.where(jnp.logical_and(sub == 1, lane == 0), sk_ref[h], 0.0)
            for kk in range(NUM_BUCKETS):
                tot = jnp.sum(jnp.where(bi == kk, acc, 0.0))
                row = jnp.where(jnp.logical_and(sub == 0, lane == kk), tot, row)
            stat_ref[h] = row + jnp.where(sub == 0, sin_ref[h], 0.0)

    shape = jax.ShapeDtypeStruct((S, n_pairs * 128), F32)
    kv_spec = pl.BlockSpec((S, 128), lambda p: (0, 0)) if kv_shared else pair
    kv_shape = jax.ShapeDtypeStruct((S, 128), F32) if kv_shared else shape
    return pl.pallas_call(
        body, name=name, grid=(n_pairs,),
        in_specs=[smem, smem, _pair_spec(qc)] + kv_specs + [full, shared, shared, shared, stat_spec],
        out_specs=[pair, kv_spec, kv_spec, stat_spec],
        out_shape=[shape, kv_shape, kv_shape, jax.ShapeDtypeStruct((2 * n_pairs, 8, 128), F32)],
        scratch_shapes=[pltpu.VMEM((2, 2 * BLK, 2 * BLK), F32), pltpu.VMEM((2 * BLK, 2 * BLK), F32),
                        pltpu.VMEM((2, 8, 128), F32)] + kv_scratch * 2,
        compiler_params=_params(("arbitrary" if kv_shared else "parallel",)))(
            tab, sinks, qa, ka, va, bidx_g, o, lse, do, stats_in)


def _comb_fwd(o_g, lse_g):
    def body(o0, o1, o2, l0, l1, l2, out_ref, outb_ref, lse_ref):
        a0, a1, a2 = l0[...], l1[...], l2[...]
        m = jnp.maximum(jnp.maximum(a0, a1), a2)
        e0, e1, e2 = jnp.exp(a0 - m), jnp.exp(a1 - m), jnp.exp(a2 - m)
        tot = e0 + e1 + e2
        out = (e0 * o0[...] + e1 * o1[...] + e2 * o2[...]) / tot
        out_ref[...] = out
        outb_ref[...] = out.astype(BF16)
        lse_ref[...] = m + jnp.log(tot)

    spec = _row_spec(4 * HD)
    groups = [pl.BlockSpec((TR, 4 * HD), functools.partial(lambda i, g: (i, g), g=g)) for g in range(len(A_GROUPS))]
    f32 = jax.ShapeDtypeStruct((S, 4 * HD), F32)
    return pl.pallas_call(
        body, name="comb_fwd", grid=(S // TR,), in_specs=groups + groups, out_specs=[spec] * 3,
        out_shape=[f32, jax.ShapeDtypeStruct((S, 4 * HD), BF16), f32],
        compiler_params=_params(("parallel",)))(o_g, o_g, o_g, lse_g, lse_g, lse_g)


def _split2(x):
    hi = x.astype(BF16)
    return hi, (x - hi.astype(F32)).astype(BF16)


KB = 2 * BLK
SBQ = 2 * BLK


def _tri_sum(x, tri):
    hi, lo = _split2(x)
    both = _dot(jnp.concatenate([hi, lo], axis=0), tri, NN)
    return both[:x.shape[0]] + both[x.shape[0]:]


def _tri(strict_upper):
    r = lax.broadcasted_iota(jnp.int32, (KB, KB), 0)
    c = lax.broadcasted_iota(jnp.int32, (KB, KB), 1)
    return jnp.where(r > c if strict_upper else r < c, 1.0, 0.0).astype(BF16)


def _sb_terms(qs, kj, before):
    z = _dot(qs, kj, NT)
    lsp = jnp.minimum(z, 0.0) - jnp.log(1.0 + jnp.exp(-jnp.abs(z)))
    return lsp, _sb_keep(before, lsp - z)


def _sb_keep(before, x):
    return x if before is None else jnp.where(before, x, 0.0)


def _sb_before(i, m):
    t = (lax.broadcasted_iota(jnp.int32, (2 * SBQ, KB), 0) & (SBQ - 1)) + i * SBQ
    s = lax.broadcasted_iota(jnp.int32, (2 * SBQ, KB), 1) + m * KB
    return s < t


C_COL = 3072 // 128


def _sb_fwd(proj):
    blk = lambda off: pl.BlockSpec((SBQ, 128), lambda p, i: (i, off + p))
    col = lambda off: pl.BlockSpec((S, 128), lambda p, i: (0, off + p))
    out = pl.BlockSpec((SBQ, 128), lambda p, i: (i, p))

    def body(q_ref, k_ref, v_ref, o_ref, ob_ref, tot_ref):
        i = pl.program_id(1)
        lo = _lane_lo(SBQ)
        qs = _stack_heads(q_ref[...] * SCALE, lo)
        suffix = _tri(True)

        def step(n, carry, diagonal=False):
            acc, rest = carry
            m = i - n
            rows = pl.ds(pl.multiple_of(m * KB, KB), KB)
            kj, vj = k_ref[rows, :].astype(BF16), v_ref[rows, :].astype(BF16)
            before = _sb_before(i, m) if diagonal else None
            lsp, lk = _sb_terms(qs, kj, before)
            w = _sb_keep(before, jnp.exp(lsp + _tri_sum(lk, suffix) + rest))
            return acc + _dot(w.astype(BF16), vj, NN), rest + jnp.sum(lk, axis=1, keepdims=True)

        first = step(0, (jnp.zeros((2 * SBQ, 128), F32), jnp.zeros((2 * SBQ, 1), F32)), diagonal=True)
        acc, rest = lax.fori_loop(1, i + 1, step, first)
        o = _unstack_heads(acc, lo)
        o_ref[...] = o
        ob_ref[...] = o.astype(BF16)
        tot_ref[...] = _unstack_heads(rest + jnp.zeros((2 * SBQ, 128), F32), lo)

    f32 = jax.ShapeDtypeStruct((S, 4 * HD), F32)
    return pl.pallas_call(
        body, name="sb_fwd", grid=(2, S // SBQ), in_specs=[blk(C_COL), col(C_COL + 2), col(C_COL + 4)],
        out_specs=[out, out, out], out_shape=[f32, jax.ShapeDtypeStruct((S, 4 * HD), BF16), f32],
        compiler_params=_params(("parallel", "arbitrary")))(proj, proj, proj)


def _sb_bwd(proj, tot, do):
    blk = lambda off: pl.BlockSpec((SBQ, 128), lambda p, i: (i, off + p))
    col = lambda off: pl.BlockSpec((S, 128), lambda p, i: (0, off + p))

    def body(q_ref, k_ref, v_ref, tot_ref, do_ref, dq_ref, dk_ref, dv_ref):
        i = pl.program_id(1)

        @pl.when(i == 0)
        def _():
            dk_ref[...] = jnp.zeros_like(dk_ref)
            dv_ref[...] = jnp.zeros_like(dv_ref)

        lo = _lane_lo(SBQ)
        qs = _stack_heads(q_ref[...] * SCALE, lo)
        dos = _stack_heads(do_ref[...], lo)
        tots = jnp.concatenate(_per_head(tot_ref[...], lo), axis=0)
        prefix = _tri(False)

        def step(m, carry, diagonal=False):
            dq, keep_left, g_left = carry
            rows = pl.ds(pl.multiple_of(m * KB, KB), KB)
            kj, vj = k_ref[rows, :].astype(BF16), v_ref[rows, :].astype(BF16)
            before = _sb_before(i, m) if diagonal else None
            lsp, lk = _sb_terms(qs, kj, before)
            log_rest = tots - keep_left - lk - _tri_sum(lk, prefix)
            w = _sb_keep(before, jnp.exp(lsp + log_rest))
            g = w * _dot(dos, vj, NT)
            g_before = g_left + _dot(g.astype(BF16), prefix, NN)
            beta = jnp.exp(lsp)
            dz = _sb_keep(before, g * (1.0 - beta) - g_before * beta).astype(BF16)
            dk_ref[rows, :] += _dot(dz, qs, TN)
            dv_ref[rows, :] += _dot(w.astype(BF16), dos, TN)
            return (dq + _dot(dz, kj, NN), keep_left + jnp.sum(lk, axis=1, keepdims=True),
                    g_left + jnp.sum(g, axis=1, keepdims=True))

        zero = (jnp.zeros((2 * SBQ, 128), F32), jnp.zeros((2 * SBQ, 1), F32), jnp.zeros((2 * SBQ, 1), F32))
        dq, _, _ = step(i, lax.fori_loop(0, i, step, zero), diagonal=True)
        dq_ref[...] = SCALE * _unstack_heads(dq, lo)

    out_blk = pl.BlockSpec((SBQ, 128), lambda p, i: (i, p))
    out_col = pl.BlockSpec((S, 128), lambda p, i: (0, p))
    f32 = jax.ShapeDtypeStruct((S, 4 * HD), F32)
    return pl.pallas_call(
        body, name="sb_bwd", grid=(2, S // SBQ),
        in_specs=[blk(C_COL), col(C_COL + 2), col(C_COL + 4), out_blk, out_blk],
        out_specs=[out_blk, out_col, out_col], out_shape=[f32, f32, f32],
        compiler_params=_params(("arbitrary", "arbitrary")))(proj, proj, proj, tot, do)


TG = 256
TGR = 1024
GATE_BLK0 = OFF_GATE // TG


def _gate_specs():
    grid = (D // TG, S // TGR)
    p_specs = [pl.BlockSpec((TGR, TG), functools.partial(lambda c, r, br: (r, GATE_BLK0 + br * (D // TG) + c), br=br))
               for br in range(3)]
    b_spec = pl.BlockSpec((3, TG), lambda c, r: (0, c))
    t_spec = pl.BlockSpec((TGR, TG), lambda c, r: (r, c))
    return grid, p_specs, b_spec, t_spec


def _sigmoid(x):
    return 1.0 / (1.0 + jnp.exp(-x))


def _three_rows(rows):
    sub = lax.broadcasted_iota(jnp.int32, (3, rows[0].shape[1]), 0)
    return jnp.where(sub == 0, rows[0], jnp.where(sub == 1, rows[1], rows[2]))


def _gate_fwd(proj, b_gate, br):
    grid, p_specs, b_spec, t_spec = _gate_specs()

    def body(p0, p1, p2, b_ref, r0, r1, r2, out_ref):
        acc = jnp.zeros((TGR, TG), F32)
        for n, (p, r) in enumerate(((p0, r0), (p1, r1), (p2, r2))):
            acc += _sigmoid(p[...] + b_ref[n:n + 1, :]) * r[...]
        out_ref[...] = acc.astype(BF16)

    return pl.pallas_call(
        body, name="gate_fwd", grid=grid, in_specs=p_specs + [b_spec] + [t_spec] * 3, out_specs=t_spec,
        out_shape=jax.ShapeDtypeStruct((S, D), BF16),
        compiler_params=_params(("parallel", "parallel")))(proj, proj, proj, b_gate, *br)


def _gate_bwd(proj, b_gate, br, dmerged):
    grid, p_specs, b_spec, t_spec = _gate_specs()

    def body(p0, p1, p2, b_ref, r0, r1, r2, dm_ref, e0, e1, e2, g0, g1, g2, db_ref):
        dm = dm_ref[...]
        rows = []
        for n, (p, r, e_ref, dg_ref) in enumerate(((p0, r0, e0, g0), (p1, r1, e1, g1), (p2, r2, e2, g2))):
            g = _sigmoid(p[...] + b_ref[n:n + 1, :])
            e_ref[...] = (dm * g).astype(BF16)
            dpre = dm * r[...] * g * (1.0 - g)
            dg_ref[...] = dpre.astype(BF16)
            rows.append(jnp.sum(dpre, axis=0, keepdims=True))
        db = _three_rows(rows)

        @pl.when(pl.program_id(1) == 0)
        def _():
            db_ref[...] = db

        @pl.when(pl.program_id(1) > 0)
        def _():
            db_ref[...] += db

    bf = jax.ShapeDtypeStruct((S, D), BF16)
    out = pl.pallas_call(
        body, name="gate_bwd", grid=grid, in_specs=p_specs + [b_spec] + [t_spec] * 4,
        out_specs=[t_spec] * 6 + [b_spec], out_shape=[bf] * 6 + [jax.ShapeDtypeStruct((3, D), F32)],
        compiler_params=_params(("parallel", "arbitrary")))(proj, proj, proj, b_gate, *br, dmerged)
    return out[:3], out[3:6], out[6]


TC = 256
N_FF_BLK = D_FF // TC
GELU_C = math.sqrt(2.0 / math.pi)


def _shift_down(x, n):
    rows = lax.broadcasted_iota(jnp.int32, x.shape, 0)
    return jnp.where(rows >= n, pltpu.roll(x, n, axis=0), 0.0)


def _shift_up(x, n):
    rows = lax.broadcasted_iota(jnp.int32, x.shape, 0)
    return jnp.where(rows < x.shape[0] - n, pltpu.roll(x, x.shape[0] - n, axis=0), 0.0)


def _conv(u, w, b):
    s1, s2 = _shift_down(u, 1), _shift_down(u, 2)
    return w[2:3, :] * u + w[1:2, :] * s1 + w[0:1, :] * s2 + b, s1, s2


def _gelu_parts(x):
    inner = GELU_C * (x + 0.044715 * x * x * x)
    t = jnp.tanh(inner)
    gelu = 0.5 * x * (1.0 + t)
    dgelu = 0.5 * (1.0 + t) + 0.5 * x * (1.0 - t * t) * GELU_C * (1.0 + 3 * 0.044715 * x * x)
    return gelu, dgelu


def _conv_specs():
    ug = pl.BlockSpec((S, TC), lambda c: (0, c))
    uv = pl.BlockSpec((S, TC), lambda c: (0, N_FF_BLK + c))
    wg = pl.BlockSpec((3, TC), lambda c: (0, c))
    wv = pl.BlockSpec((3, TC), lambda c: (0, N_FF_BLK + c))
    bg = pl.BlockSpec((1, TC), lambda c: (0, c))
    bv = pl.BlockSpec((1, TC), lambda c: (0, N_FF_BLK + c))
    return ug, uv, wg, wv, bg, bv


def _conv_fwd(u, conv_w, conv_b):
    ug, uv, wg, wv, bg, bv = _conv_specs()

    def body(ug_ref, uv_ref, wg_ref, wv_ref, bg_ref, bv_ref, a_ref):
        gc = _conv(ug_ref[...], wg_ref[...], bg_ref[...])[0]
        vc = _conv(uv_ref[...], wv_ref[...], bv_ref[...])[0]
        a_ref[...] = (_gelu_parts(gc)[0] * vc).astype(BF16)

    return pl.pallas_call(
        body, name="conv_fwd", grid=(N_FF_BLK,), in_specs=[ug, uv, wg, wv, bg, bv], out_specs=ug,
        out_shape=jax.ShapeDtypeStruct((S, D_FF), BF16),
        compiler_params=_params(("parallel",)))(u, u, conv_w, conv_w, conv_b, conv_b)


def _conv_bwd(u, conv_w, conv_b, da):
    ug, uv, wg, wv, bg, bv = _conv_specs()

    def back(duc, u, s1, s2, w):
        du = w[2:3, :] * duc + w[1:2, :] * _shift_up(duc, 1) + w[0:1, :] * _shift_up(duc, 2)
        dw = _three_rows([jnp.sum(duc * s2, axis=0, keepdims=True), jnp.sum(duc * s1, axis=0, keepdims=True),
                          jnp.sum(duc * u, axis=0, keepdims=True)])
        return du, dw, jnp.sum(duc, axis=0, keepdims=True)

    def body(ug_ref, uv_ref, wg_ref, wv_ref, bg_ref, bv_ref, da_ref, dug_ref, duv_ref, dwg_ref, dwv_ref, dbg_ref, dbv_ref):
        u_g, u_v = ug_ref[...], uv_ref[...]
        gc, g1, g2 = _conv(u_g, wg_ref[...], bg_ref[...])
        vc, v1, v2 = _conv(u_v, wv_ref[...], bv_ref[...])
        gelu, dgelu = _gelu_parts(gc)
        da = da_ref[...]
        du, dw, db = back(da * vc * dgelu, u_g, g1, g2, wg_ref[...])
        dug_ref[...] = du.astype(BF16)
        dwg_ref[...] = dw
        dbg_ref[...] = db
        du, dw, db = back(da * gelu, u_v, v1, v2, wv_ref[...])
        duv_ref[...] = du.astype(BF16)
        dwv_ref[...] = dw
        dbv_ref[...] = db

    return pl.pallas_call(
        body, name="conv_bwd", grid=(N_FF_BLK,), in_specs=[ug, uv, wg, wv, bg, bv, ug],
        out_specs=[ug, ug, wg, wg, bg, bg],
        out_shape=[jax.ShapeDtypeStruct((S, D_FF), BF16), jax.ShapeDtypeStruct((S, D_FF), BF16),
                   jax.ShapeDtypeStruct((3, D_FF), F32), jax.ShapeDtypeStruct((3, D_FF), F32),
                   jax.ShapeDtypeStruct((1, D_FF), F32), jax.ShapeDtypeStruct((1, D_FF), F32)],
        compiler_params=_params(("parallel",)))(u, u, conv_w, conv_w, conv_b, conv_b, da)


def _adamw(name, w, g, m, v):
    shape = w.shape
    cols = shape[-1]
    flat = [t.reshape(-1, cols) for t in (w, g, m, v)]
    r = flat[0].shape[0]
    tr = min(r, max(8, 2 * 1024 * 1024 // (4 * cols)))

    def body(w_ref, g_ref, m_ref, v_ref, go_ref, d_ref, mo_ref, vo_ref):
        g = g_ref[...]
        go_ref[...] = g
        m = ADAM_B1 * m_ref[...] + (1.0 - ADAM_B1) * g
        v = ADAM_B2 * v_ref[...] + (1.0 - ADAM_B2) * (g * g)
        m_hat = m / (1.0 - ADAM_B1 ** ADAM_STEP)
        v_hat = v / (1.0 - ADAM_B2 ** ADAM_STEP)
        d_ref[...] = -ADAM_LR * (m_hat / (jnp.sqrt(v_hat) + ADAM_EPS) + ADAM_WD * w_ref[...])
        mo_ref[...] = m
        vo_ref[...] = v

    spec = pl.BlockSpec((tr, cols), lambda i: (i, 0))
    outs = pl.pallas_call(
        body, name=name, grid=(pl.cdiv(r, tr),), in_specs=[spec] * 4, out_specs=[spec] * 4,
        out_shape=[jax.ShapeDtypeStruct((r, cols), F32)] * 4, compiler_params=_params(("parallel",)))(*flat)
    return [t.reshape(shape) for t in outs]


def _place():
    x, y, c = lax.axis_index("x"), lax.axis_index("y"), lax.axis_index("c")
    chips = [(1 - x, y), (x, 1 - y), (1 - x, 1 - y)]
    return x, y, c, chips


def _scalars(*vals):
    return jnp.stack([jnp.asarray(v, jnp.int32) for v in vals])


HBM = pl.BlockSpec(memory_space=pltpu.HBM)
SEM = pl.BlockSpec(memory_space=pltpu.SEMAPHORE)
SPLIT_COPY = pltpu.CompilerParams(has_side_effects=pltpu.SideEffectType.DATAFLOW_SIDE_EFFECTING)


def _in_hbm(x):
    return pltpu.with_memory_space_constraint(x, pltpu.HBM)


def _cast_into_slot(name, w, layer, chip):
    _, k, n4 = w.shape
    tr = max(t for t in range(16, 513, 16) if k % t == 0)

    def body(chip_ref, w_ref, o_ref):
        o_ref[...] = w_ref[...].astype(BF16)

    return pl.pallas_call(
        body, name=name,
        grid_spec=pltpu.PrefetchScalarGridSpec(
            num_scalar_prefetch=1, grid=(k // tr,),
            in_specs=[pl.BlockSpec((None, tr, n4), lambda i, s: (layer, i, 0))],
            out_specs=pl.BlockSpec((None, tr, n4), lambda i, s: (s[0], i, 0))),
        out_shape=jax.ShapeDtypeStruct((N_CHIPS, k, n4), BF16),
        compiler_params=_params(("parallel",)))(_scalars(chip), w)


def _gather_copy(buf_ref, k, from_chip, send_sem, recv_sem, chips, c, half=False):
    rows = buf_ref.at[from_chip]
    if half:
        h = buf_ref.shape[1] // 2
        rows = buf_ref.at[from_chip, pl.ds(pl.multiple_of(c * h, h), h)]
    return pltpu.make_async_remote_copy(src_ref=rows, dst_ref=rows, send_sem=send_sem, recv_sem=recv_sem,
                                        device_id=(*chips[k], c), device_id_type=MESH)


def _gather_start(name, bufs, groups, halved=()):
    n, ng = len(bufs), len(groups)
    where = {a: (gi, e) for gi, g in enumerate(groups) for e, a in enumerate(g)}

    def body(*refs):
        ins, sems, token = refs[:n], refs[n:n + 2 * ng], refs[-1]
        x, y, c, chips = _place()
        for a in range(n):
            gi, e = where[a]
            for k in range(3):
                _gather_copy(ins[a], k, 2 * x + y, sems[2 * gi].at[3 * e + k], sems[2 * gi + 1].at[3 * e + k],
                             chips, c, a in halved).start()
        token[...] = jnp.zeros_like(token)

    out_shape = [pltpu.SemaphoreType.DMA((3 * len(g),)) for g in groups for _ in range(2)]
    out_shape += [pltpu.HBM(b.shape, b.dtype) for b in bufs] + [jax.ShapeDtypeStruct((8, 128), F32)]
    out = pl.pallas_call(
        body, name=name, in_specs=[HBM] * n,
        out_specs=[SEM] * (2 * ng) + [HBM] * n + [pl.BlockSpec(memory_space=pltpu.VMEM)], out_shape=out_shape,
        input_output_aliases={a: 2 * ng + a for a in range(n)}, compiler_params=SPLIT_COPY)(*[_in_hbm(b) for b in bufs])
    sems = [(out[2 * gi], out[2 * gi + 1]) for gi in range(ng)]
    return sems, list(out[2 * ng:2 * ng + n]), out[-1]


def _gather_wait(name, bufs, send, recv, after, halved=()):
    n = len(bufs)

    def body(*refs):
        ins, send_sem, recv_sem = refs[:n], refs[n], refs[n + 1]
        x, y, c, chips = _place()
        for e in range(n):
            for k in range(3):
                sems = (send_sem.at[3 * e + k], recv_sem.at[3 * e + k])
                _gather_copy(ins[e], k, 2 * x + y, *sems, chips, c, e in halved).wait_send()
                _gather_copy(ins[e], k, 2 * chips[k][0] + chips[k][1], *sems, chips, c, e in halved).wait_recv()

    return pl.pallas_call(
        body, name=name, in_specs=[HBM] * n + [SEM, SEM, ANY], out_specs=[HBM] * n,
        out_shape=[pltpu.HBM(b.shape, b.dtype) for b in bufs],
        input_output_aliases={a: a for a in range(n)}, compiler_params=SPLIT_COPY)(*bufs, send, recv, after)


def _swap_halves(name, bufs):
    n = len(bufs)

    def body(*refs):
        ins, outs = refs[:n], refs[n:2 * n]
        send_sem, recv_sem = refs[2 * n:]
        x, y, c, chips = _place()

        def piece(ref, k, which):
            h = ref.shape[1] // 2
            return ref.at[2 * chips[k][0] + chips[k][1], pl.ds(pl.multiple_of(which * h, h), h)]

        def copy(a, k, which):
            return pltpu.make_async_remote_copy(
                src_ref=piece(ins[a], k, c), dst_ref=piece(outs[a], k, which), send_sem=send_sem.at[3 * a + k],
                recv_sem=recv_sem.at[3 * a + k], device_id=(x, y, 1 - c), device_id_type=MESH)

        for a in range(n):
            for k in range(3):
                copy(a, k, c).start()
        for a in range(n):
            for k in range(3):
                copy(a, k, c).wait_send()
                copy(a, k, 1 - c).wait_recv()

    return pl.pallas_call(
        body, name=name, in_specs=[ANY] * n, out_specs=[ANY] * n,
        out_shape=[jax.ShapeDtypeStruct(b.shape, b.dtype) for b in bufs],
        input_output_aliases={a: a for a in range(n)},
        scratch_shapes=[pltpu.SemaphoreType.DMA((3 * n,)), pltpu.SemaphoreType.DMA((3 * n,))],
    )(*bufs)


def _reduce_copy(g_ref, land_ref, mask, send_sem, recv_sem, x, y, c, sending):
    px, py, pc = x ^ ((mask >> 2) & 1), y ^ ((mask >> 1) & 1), c ^ (mask & 1)
    half = g_ref.shape[1] // 2
    src = g_ref.at[2 * px + py, pl.ds(pl.multiple_of(pc * half, half), half)]
    dst = land_ref.at[4 * x + 2 * y + c] if sending else land_ref.at[4 * px + 2 * py + pc]
    return pltpu.make_async_remote_copy(src_ref=src, dst_ref=dst, send_sem=send_sem, recv_sem=recv_sem,
                                        device_id=(px, py, pc), device_id_type=MESH)


def _reduce_start(name, grads):
    n = len(grads)
    lands = [lax.empty((N_DEV, g.shape[1] // 2, g.shape[2]), g.dtype) for g in grads]

    def body(*refs):
        gs, ls, send_sem, recv_sem = refs[:n], refs[n:2 * n], refs[2 * n], refs[2 * n + 1]
        x, y, c, _ = _place()
        for a in range(n):
            for mask in range(1, N_DEV):
                s = (N_DEV - 1) * a + mask - 1
                _reduce_copy(gs[a], ls[a], mask, send_sem.at[s], recv_sem.at[s], x, y, c, True).start()
        refs[-1][...] = jnp.zeros_like(refs[-1])

    sem = pltpu.SemaphoreType.DMA((n * (N_DEV - 1),))
    out = pl.pallas_call(
        body, name=name, in_specs=[HBM] * (2 * n),
        out_specs=[SEM, SEM] + [HBM] * (2 * n) + [pl.BlockSpec(memory_space=pltpu.VMEM)],
        out_shape=[sem, sem] + [pltpu.HBM(t.shape, t.dtype) for t in grads + lands] + [jax.ShapeDtypeStruct((8, 128), F32)],
        input_output_aliases={a: 2 + a for a in range(2 * n)}, compiler_params=SPLIT_COPY)(
            *[_in_hbm(t) for t in grads + lands])
    return out[0], out[1], list(out[2:2 + n]), list(out[2 + n:2 + 2 * n]), out[-1]


def _reduce_wait(name, send, recv, grads, lands, after):
    n = len(grads)

    def body(*refs):
        gs, ls, send_sem, recv_sem = refs[:n], refs[n:2 * n], refs[2 * n], refs[2 * n + 1]
        x, y, c, _ = _place()
        for a in range(n):
            for mask in range(1, N_DEV):
                s = (N_DEV - 1) * a + mask - 1
                sems = (send_sem.at[s], recv_sem.at[s])
                _reduce_copy(gs[a], ls[a], mask, *sems, x, y, c, True).wait_send()
                _reduce_copy(gs[a], ls[a], mask, *sems, x, y, c, False).wait_recv()

    out = pl.pallas_call(
        body, name=name, in_specs=[HBM] * (2 * n) + [SEM, SEM, ANY], out_specs=[HBM] * (2 * n),
        out_shape=[pltpu.HBM(t.shape, t.dtype) for t in grads + lands],
        input_output_aliases={a: a for a in range(2 * n)}, compiler_params=SPLIT_COPY)(*grads, *lands, send, recv, after)
    return list(out[:n]), list(out[n:])


def _reduce_sum(name, g, land, layer, into, chip, c):
    _, k4, n4 = g.shape
    half = k4 // 2
    tr = max(t for t in range(16, 513, 16) if half % t == 0)
    per = half // tr
    me = 2 * chip + c

    def body(s_ref, own_ref, *refs):
        total = own_ref[...].astype(F32)
        for ref in refs[:N_DEV - 1]:
            total = total + ref[...].astype(F32)
        refs[-1][...] = total

    in_specs = [pl.BlockSpec((None, tr, n4), lambda i, s: (s[0], s[1] * per + i, 0))]
    in_specs += [pl.BlockSpec((None, tr, n4), functools.partial(lambda i, s, m: (s[1 + m], i, 0), m=m))
                 for m in range(1, N_DEV)]
    ins = [g] + [land] * (N_DEV - 1)
    aliases = {}
    if into is not None:
        in_specs, ins, aliases = in_specs + [ANY], ins + [into], {1 + N_DEV: 0}
    return pl.pallas_call(
        body, name=name,
        grid_spec=pltpu.PrefetchScalarGridSpec(
            num_scalar_prefetch=1, grid=(per,), in_specs=in_specs,
            out_specs=pl.BlockSpec((None, tr, n4), lambda i, s: (layer, s[1] * per + i, 0))),
        out_shape=jax.ShapeDtypeStruct((DEPTH, k4, n4), F32), input_output_aliases=aliases,
        compiler_params=_params(("parallel",)))(_scalars(chip, c, *[me ^ m for m in range(1, N_DEV)]), *ins)


def _join_halves(name, bufs):
    n = len(bufs)

    def body(*refs):
        ins, outs = refs[:n], refs[n:2 * n]
        send_sem, recv_sem = refs[2 * n:]
        x, y, c, _ = _place()

        def rows(ref, which):
            half = ref.shape[1] // 2
            return ref.at[:, pl.ds(pl.multiple_of(which * half, half), half)]

        sends = [pltpu.make_async_remote_copy(
            src_ref=rows(ins[a], c), dst_ref=rows(outs[a], c), send_sem=send_sem.at[a], recv_sem=recv_sem.at[a],
            device_id=(x, y, 1 - c), device_id_type=MESH) for a in range(n)]
        for cp in sends:
            cp.start()
        for a in range(n):
            sends[a].wait_send()
            pltpu.make_async_remote_copy(
                src_ref=rows(ins[a], c), dst_ref=rows(outs[a], 1 - c), send_sem=send_sem.at[a], recv_sem=recv_sem.at[a],
                device_id=(x, y, 1 - c), device_id_type=MESH).wait_recv()

    return pl.pallas_call(
        body, name=name, in_specs=[ANY] * n, out_specs=[ANY] * n,
        out_shape=[jax.ShapeDtypeStruct(b.shape, b.dtype) for b in bufs],
        input_output_aliases={a: a for a in range(n)},
        scratch_shapes=[pltpu.SemaphoreType.DMA((n,)), pltpu.SemaphoreType.DMA((n,))],
    )(*bufs)


def _small_copy(b_ref, l_ref, mask, send_sem, recv_sem, x, y, c, sending):
    px, py, pc = x ^ ((mask >> 2) & 1), y ^ ((mask >> 1) & 1), c ^ (mask & 1)
    dst = l_ref.at[4 * x + 2 * y + c] if sending else l_ref.at[4 * px + 2 * py + pc]
    return pltpu.make_async_remote_copy(src_ref=b_ref, dst_ref=dst, send_sem=send_sem.at[mask - 1],
                                        recv_sem=recv_sem.at[mask - 1], device_id=(px, py, pc), device_id_type=MESH)


def _small_start(block):
    land = lax.empty((N_DEV,) + block.shape, block.dtype)

    def body(b_ref, l_ref, send_sem, recv_sem, b_thru, l_thru, token):
        x, y, c, _ = _place()
        for mask in range(1, N_DEV):
            _small_copy(b_ref, l_ref, mask, send_sem, recv_sem, x, y, c, True).start()
        token[...] = jnp.zeros_like(token)

    sem = pltpu.SemaphoreType.DMA((N_DEV - 1,))
    return pl.pallas_call(
        body, name="small_start", in_specs=[HBM, HBM],
        out_specs=[SEM, SEM, HBM, HBM, pl.BlockSpec(memory_space=pltpu.VMEM)],
        out_shape=[sem, sem, pltpu.HBM(block.shape, block.dtype), pltpu.HBM(land.shape, land.dtype),
                   jax.ShapeDtypeStruct((8, 128), F32)],
        input_output_aliases={0: 2, 1: 3}, compiler_params=SPLIT_COPY)(_in_hbm(block), _in_hbm(land))


def _small_wait(send, recv, block, land, after):
    def body(b_ref, l_ref, send_sem, recv_sem, after_ref, b_out, l_out):
        x, y, c, _ = _place()
        for mask in range(1, N_DEV):
            _small_copy(b_ref, l_ref, mask, send_sem, recv_sem, x, y, c, True).wait_send()
            _small_copy(b_ref, l_ref, mask, send_sem, recv_sem, x, y, c, False).wait_recv()

    return pl.pallas_call(
        body, name="small_wait", in_specs=[HBM, HBM, SEM, SEM, ANY], out_specs=[HBM, HBM],
        out_shape=[pltpu.HBM(block.shape, block.dtype), pltpu.HBM(land.shape, land.dtype)],
        input_output_aliases={0: 0, 1: 1}, compiler_params=SPLIT_COPY)(block, land, send, recv, after)


def _small_sum(land):
    def body(l_ref, out_ref):
        total = l_ref[0]
        for d in range(1, N_DEV):
            total = total + l_ref[d]
        out_ref[...] = total

    vmem = pl.BlockSpec(memory_space=pltpu.VMEM)
    return pl.pallas_call(
        body, name="small_sum", in_specs=[vmem], out_specs=vmem,
        out_shape=jax.ShapeDtypeStruct(land.shape[1:], F32),
        compiler_params=pltpu.CompilerParams(vmem_limit_bytes=VMEM_LIMIT))(land)


B_Q_COL = 2304 // 128
B_K0, B_V0 = 2816, 2944


def _full_cols(w_g):
    return w_g.transpose(1, 0, 2).reshape(w_g.shape[1], -1)


A_DILS = tuple(d for _, d in A_GROUPS)
A_PAIRS = N_A // 2


def _src_a(proj):
    return ((proj, 0), (proj, A_PAIRS), (proj, 2 * A_PAIRS))


def _mixer_fwd(h1, wget, rel_bias, sinks_l, bidx):
    w = dict(wget(0, h1))
    proj = _mm_nt("proj_in", h1, w["w_in"], F32, tm=S, tn=1152)
    no_sinks = jnp.full((N_A,), NEG, F32)
    o_g, lse_g = _band_fwd("band_fwd_a", A_DILS, A_PAIRS, BLK, 0, _src_a(proj), bidx[:3], rel_bias, no_sinks)
    o_a32, o_a, lse_a = _comb_fwd(o_g, lse_g)
    src_b = ((proj, B_Q_COL), (proj, B_K0 // 128), (proj, B_V0 // 128))
    o_b32, lse_b = _band_fwd("band_fwd_b", (1,), 4, BLK - 1, N_A, src_b, bidx[3:], rel_bias, sinks_l, kv_shared=True)
    o_b = o_b32.astype(BF16)
    o_c32, o_c, tot_c = _sb_fwd(proj)
    w.update(wget(1, o_c32))
    br = [_mm_nn("branch_a", o_a, w["w_br_a"], F32, tm=S), _mm_nn("branch_b", o_b, w["w_br_b"], F32, tm=S),
          _mm_nn("branch_c", o_c, w["w_br_c"], F32, tm=S)]
    merged = _gate_fwd(proj, w["b_gate"], br)
    mo = _mm_nn("out_proj", merged, w["w_out"], F32, tm=S)
    saved = dict(proj=proj, src_b=src_b, o_a32=o_a32, lse_a=lse_a, o_b32=o_b32, lse_b=lse_b, tot_c=tot_c,
                 o_a=o_a, o_b=o_b, o_c=o_c, br=br, merged=merged)
    return mo, saved, w


def _mixer_bwd(d_mo, h1, w, sv, rel_bias, sinks_l, bidx, stats_in, emit):
    grads = {}
    dmerged = _mm_nt("out_proj_dx", d_mo, w["w_out"], F32, tm=S)
    grads["w_out"] = _mm_tn_sharded("out_proj_dw", sv["merged"], d_mo, True)
    e, dgate, db_gate = _gate_bwd(sv["proj"], w["b_gate"], sv["br"], dmerged)
    grads["b_gate"] = db_gate
    d_o = {}
    for n, name in enumerate("abc"):
        d_o[name] = _mm_nt("branch_%s_dx" % name, e[n], w["w_br_" + name], F32, tm=S)
        grads["w_br_" + name] = _mm_tn_sharded("branch_%s_dw" % name, sv["o_" + name], e[n], False)
    zero = emit(1, grads)
    no_sinks = jnp.full((N_A,), NEG, F32) + zero[0]
    dq_a, dk_a, dv_a, st_a = _band_bwd("band_bwd_a", A_DILS, A_PAIRS, BLK, 0, _src_a(sv["proj"]), bidx[:3], rel_bias,
                                       no_sinks, sv["o_a32"], sv["lse_a"], d_o["a"], stats_in[:N_A])
    dq_b, dk_b, dv_b, st_b = _band_bwd("band_bwd_b", (1,), 4, BLK - 1, N_A, sv["src_b"], bidx[3:], rel_bias, sinks_l,
                                       sv["o_b32"], sv["lse_b"], d_o["b"], stats_in[N_A:], kv_shared=True)
    stats = jnp.concatenate([st_a, st_b], axis=0)
    dcq, dck, dcv = _sb_bwd(sv["proj"], sv["tot_c"], d_o["c"])
    cols = [dq_a, dk_a, dv_a, dq_b, dk_b, dv_b, dcq, dck, dcv]
    dproj = jnp.concatenate([t.astype(BF16) for t in cols] + list(dgate), axis=1)
    grads["w_in"] = _mm_tn("proj_in_dw", dproj, h1, BF16, tm=1152, tn=1024).reshape(N_CHIPS, IN_SHARD, D)
    zero = emit(2, grads)
    dh1 = _mm_nn("proj_in_dx", dproj, w["w_in"], F32, tm=S, tk=2304)
    return dh1, grads, stats, zero


def _ffn_fwd(h2, w):
    u = _mm_nn("ffn_up", h2, w["w_up"], F32, tm=S, tn=1024)
    a = _conv_fwd(u, w["conv_w"], w["conv_b"])
    dn = _mm_nn("ffn_down", a, w["w_down"], F32, tm=1024)
    return dn, dict(u=u, a=a)


def _ffn_bwd(d_dn, h2, w, sv):
    grads = {}
    da = _mm_nt("ffn_down_dx", d_dn, w["w_down"], F32, tm=S, tn=1024)
    grads["w_down"] = _mm_tn_sharded("ffn_down_dw", sv["a"], d_dn, True, tm=1024, tn=1024)
    dug, duv, dwg, dwv, dbg, dbv = _conv_bwd(sv["u"], w["conv_w"], w["conv_b"], da)
    du = jnp.concatenate([dug, duv], axis=1)
    grads["conv_w"] = jnp.concatenate([dwg, dwv], axis=1)
    grads["conv_b"] = jnp.concatenate([dbg, dbv], axis=1)
    dh2 = _mm_nt("ffn_up_dx", du, w["w_up"], F32, tm=S, tk=2048)
    grads["w_up"] = _mm_tn_sharded("ffn_up_dw", h2, du, False, tm=1024, tn=1024)
    return dh2, grads


BIG = ("w_in", "w_br_a", "w_br_b", "w_br_c", "w_out", "w_up", "w_down")


def _shard_view(name, w):
    return jnp.swapaxes(w, 1, 2) if name == "w_in" else w
WEIGHT_GROUPS = (("w_in", "b_gate"), ("w_br_a", "w_br_b", "w_br_c", "w_out"), ("w_up", "conv_w", "w_down"))
GRAD_GROUPS = (("w_down", "w_up"), ("w_out", "w_br_a", "w_br_b", "w_br_c"), ("w_in",))
SMALL_ROWS = (("rel_bias", 8), ("attn_pre_norm", 16), ("attn_post_norm", 16), ("ffn_pre_norm", 16), ("ffn_post_norm", 16),
              ("sinks", 8), ("conv_b", 128), ("b_gate", 48), ("conv_w", 384), ("loss", 8))


def _pack_small(vals):
    rows = []
    for name, n in SMALL_ROWS:
        flat = vals[name].reshape(-1).astype(F32)
        rows.append(jnp.pad(flat, (0, n * 128 - flat.shape[0])).reshape(n, 128))
    return jnp.concatenate(rows, axis=0)


def _unpack_small(block, shapes):
    out, row = {}, 0
    for name, n in SMALL_ROWS:
        size = int(np.prod(shapes[name]))
        out[name] = block[row:row + n].reshape(-1)[:size].reshape(shapes[name])
        row += n
    return out


def kernel(x, rel_bias, attn_pre_norm, w_in, b_gate, sinks, w_br_a, w_br_b, w_br_c, w_out, attn_post_norm, ffn_pre_norm, w_up, conv_w, conv_b, w_down, ffn_post_norm, loss_target, m_rel_bias, m_attn_pre_norm, m_w_in, m_b_gate, m_sinks, m_w_br_a, m_w_br_b, m_w_br_c, m_w_out, m_attn_post_norm, m_ffn_pre_norm, m_w_up, m_conv_w, m_conv_b, m_w_down, m_ffn_post_norm, v_rel_bias, v_attn_pre_norm, v_w_in, v_b_gate, v_sinks, v_w_br_a, v_w_br_b, v_w_br_c, v_w_out, v_attn_post_norm, v_ffn_pre_norm, v_w_up, v_conv_w, v_conv_b, v_w_down, v_ffn_post_norm):
    names = ("rel_bias", "attn_pre_norm", "w_in", "b_gate", "sinks", "w_br_a", "w_br_b", "w_br_c", "w_out",
             "attn_post_norm", "ffn_pre_norm", "w_up", "conv_w", "conv_b", "w_down", "ffn_post_norm")
    weights = dict(zip(names, (rel_bias, attn_pre_norm, w_in, b_gate, sinks, w_br_a, w_br_b, w_br_c, w_out,
                               attn_post_norm, ffn_pre_norm, w_up, conv_w, conv_b, w_down, ffn_post_norm)))
    mom1 = dict(zip(names, (m_rel_bias, m_attn_pre_norm, m_w_in, m_b_gate, m_sinks, m_w_br_a, m_w_br_b, m_w_br_c,
                            m_w_out, m_attn_post_norm, m_ffn_pre_norm, m_w_up, m_conv_w, m_conv_b, m_w_down,
                            m_ffn_post_norm)))
    mom2 = dict(zip(names, (v_rel_bias, v_attn_pre_norm, v_w_in, v_b_gate, v_sinks, v_w_br_a, v_w_br_b, v_w_br_c,
                            v_w_out, v_attn_post_norm, v_ffn_pre_norm, v_w_up, v_conv_w, v_conv_b, v_w_down,
                            v_ffn_post_norm)))

    chip = 2 * lax.axis_index("x") + lax.axis_index("y")
    core = lax.axis_index("c")

    keys = [(n, l) for l in range(DEPTH) for group in WEIGHT_GROUPS for n in group]
    groups = [[keys.index((n, l)) for n in group] for l in range(DEPTH) for group in WEIGHT_GROUPS]

    def slot_buffer(n, l):
        if n in BIG:
            return _cast_into_slot("cast_" + n, _shard_view(n, weights[n]), l, chip)
        shard = weights[n][l]
        return lax.dynamic_update_slice(jnp.zeros((N_CHIPS,) + shard.shape, F32), shard[None],
                                        (chip, jnp.int32(0), jnp.int32(0)))

    by_halves = [a for a, (n, l) in enumerate(keys) if n in BIG and l == 0]
    n_first = len(groups[0])
    sems, in_flight, first = _gather_start("gather_start_first", [slot_buffer(*k) for k in keys[:n_first]], groups[:1],
                                           tuple(a for a in by_halves if a < n_first))
    more = _gather_start("gather_start", [slot_buffer(*k) for k in keys[n_first:]],
                         [[a - n_first for a in g] for g in groups[1:]],
                         tuple(a - n_first for a in by_halves if a >= n_first))
    sems, in_flight, started = sems + more[0], in_flight + more[1], more[2]

    def wget(l, gi, after):
        g = l * len(WEIGHT_GROUPS) + gi
        after = started if g == 0 else after
        halved = tuple(e for e, a in enumerate(groups[g]) if a in by_halves)
        got = list(_gather_wait("gather_wait_%d_%d" % (l, gi), [in_flight[a] for a in groups[g]], *sems[g], after,
                                halved))
        if halved:
            for e, buf in zip(halved, _swap_halves("swap_halves_%d_%d" % (l, gi), [got[e] for e in halved])):
                got[e] = buf
        out = {}
        for n, buf in zip(WEIGHT_GROUPS[gi], got):
            if n in ("w_in", "w_out", "w_down"):
                out[n] = buf.reshape(-1, buf.shape[-1])
            else:
                out[n] = buf if n == "w_up" else _full_cols(buf)
        if gi == len(WEIGHT_GROUPS) - 1:
            out["conv_b"] = conv_b[l:l + 1]
        return out

    pending = []

    def emit(l, gi, grads):
        group = GRAD_GROUPS[gi]
        *started, token = _reduce_start("reduce_start_%d_%d" % (l, gi), [grads[n] for n in group])
        pending.append((l, group) + tuple(started))
        return token[:1, :1]

    local = _local_step(x.reshape(S, D), loss_target.reshape(S, D), wget, emit, rel_bias, sinks,
                        attn_pre_norm + first[:1, :1], attn_post_norm, ffn_pre_norm, ffn_post_norm)
    return _reduce_and_update(x.shape, names, weights, mom1, mom2, chip, core, pending, *local)


def _local_step(xs, target, wget, emit, rel_bias, sinks, attn_pre_norm, attn_post_norm, ffn_pre_norm, ffn_post_norm):
    bidx = jnp.asarray(_bucket_maps())

    saved, layers = [], []
    h1 = _rms_fwd("pre_norm_first", xs, attn_pre_norm[0:1])
    x_in = xs
    for l in range(DEPTH):
        mo, sv_mix, w = _mixer_fwd(h1, functools.partial(wget, l), rel_bias, sinks[l], bidx)
        x_mid, h2 = _post_pre_fwd("post_attn_norm", x_in, mo, attn_post_norm[l:l + 1], ffn_pre_norm[l:l + 1])
        w.update(wget(l, 2, h2))
        dn, sv_ffn = _ffn_fwd(h2, w)
        g_next = attn_pre_norm[l + 1:l + 2] if l + 1 < DEPTH else None
        x_out, h1_next = _post_pre_fwd("post_ffn_norm" if l + 1 < DEPTH else "post_ffn_norm_last", x_mid, dn,
                                       ffn_post_norm[l:l + 1], g_next)
        saved.append(dict(x_in=x_in, h1=h1, mo=mo, x_mid=x_mid, h2=h2, dn=dn, mix=sv_mix, ffn=sv_ffn))
        layers.append(w)
        x_in, h1 = x_out, h1_next

    loss_row, dres = _loss_kernel(x_in, target)

    small = [None] * DEPTH
    stats = jnp.zeros((N_BAND_Q, 8, 128), F32)
    dh_next = None
    for l in reversed(range(DEPTH)):
        w, sv = layers[l], saved[l]
        if l + 1 < DEPTH:
            pre = (saved[l + 1]["x_in"], attn_pre_norm[l + 1:l + 2] + zero, dh_next)
            dres, d_dn, dg_pre_next, dg_fpost = _norm_bwd("post_ffn_norm_bwd", dres, pre,
                                                          (sv["dn"], ffn_post_norm[l:l + 1]))
            small[l + 1]["attn_pre_norm"] = dg_pre_next
        else:
            dres, d_dn, _, dg_fpost = _norm_bwd("post_ffn_norm_last_bwd", dres, None, (sv["dn"], ffn_post_norm[l:l + 1]))
        dh2, g_ffn = _ffn_bwd(d_dn, sv["h2"], w, sv["ffn"])
        zero = emit(l, 0, g_ffn)
        dres, d_mo, dg_fpre, dg_apost = _norm_bwd("post_attn_norm_bwd", dres,
                                                  (sv["x_mid"], ffn_pre_norm[l:l + 1] + zero, dh2),
                                                  (sv["mo"], attn_post_norm[l:l + 1]))
        dh_next, g_mix, stats, zero = _mixer_bwd(d_mo, sv["h1"], w, sv["mix"], rel_bias, sinks[l], bidx, stats,
                                                 functools.partial(emit, l))
        small[l] = dict(ffn_post_norm=dg_fpost, ffn_pre_norm=dg_fpre, attn_post_norm=dg_apost,
                        sinks=stats[N_A:, 1, 0], conv_b=g_ffn["conv_b"], b_gate=g_mix["b_gate"], conv_w=g_ffn["conv_w"])
    grad_x, _, dg_pre0, _ = _norm_bwd("pre_norm_first_bwd", dres, (saved[0]["x_in"], attn_pre_norm[0:1] + zero, dh_next),
                                      None)
    small[0]["attn_pre_norm"] = dg_pre0
    return loss_row, grad_x, small, stats


def _reduce_and_update(x_shape, names, weights, mom1, mom2, chip, core, pending, loss_row, grad_x, small, stats):
    delta, new_m, new_v, grads = {}, {}, {}, {}

    def update(n, g):
        grads[n], delta[n], new_m[n], new_v[n] = _adamw("adamw_" + n, _shard_view(n, weights[n]), g,
                                                        _shard_view(n, mom1[n]), _shard_view(n, mom2[n]))

    small_vals = {n: jnp.stack([small[l][n].reshape(weights[n].shape[1:]) for l in range(DEPTH)])
                  for n in ("attn_pre_norm", "attn_post_norm", "ffn_pre_norm", "ffn_post_norm", "conv_b", "sinks")}
    small_vals["b_gate"] = jnp.stack([small[l]["b_gate"] for l in range(DEPTH)])
    small_vals["conv_w"] = jnp.stack([small[l]["conv_w"] for l in range(DEPTH)])
    small_vals["rel_bias"] = stats[:, 0, :NUM_BUCKETS].T
    small_vals["loss"] = loss_row[0, :1]
    shapes = {n: v.shape for n, v in small_vals.items()}
    small_send, small_recv, packed, small_land, started = _small_start(_pack_small(small_vals))

    summed = {}

    def finish(which, after):
        for l, group, send, recv, gs, lands in pending:
            if (group == ("w_in",)) == which:
                gs, lands = _reduce_wait("reduce_wait_%d_%s" % (l, group[0]), send, recv, gs, lands, after)
                for n, g, land in zip(group, gs, lands):
                    summed[n] = _reduce_sum("reduce_sum_%d_%s" % (l, n), g, land, l, summed.get(n), chip, core)

    finish(False, started)
    early = [n for n in BIG if n != "w_in"]
    for n, g in zip(early, _join_halves("join_halves", [summed[n] for n in early])):
        update(n, g)
    finish(True, delta[early[-1]])
    update("w_in", _join_halves("join_halves_w_in", [summed["w_in"]])[0])

    packed, small_land = _small_wait(small_send, small_recv, packed, small_land, delta["w_in"])
    small_land = lax.dynamic_update_slice(small_land, packed[None], (2 * chip + core, jnp.int32(0), jnp.int32(0)))
    reduced = _unpack_small(_small_sum(small_land), shapes)
    reduced["b_gate"] = lax.dynamic_slice_in_dim(reduced["b_gate"], chip * (D // N_CHIPS), D // N_CHIPS, axis=2)
    reduced["conv_w"] = lax.dynamic_slice_in_dim(reduced["conv_w"], chip * (2 * D_FF // N_CHIPS), 2 * D_FF // N_CHIPS, axis=2)
    for n in names:
        if n not in grads:
            update(n, reduced[n].reshape(weights[n].shape))
    for out in (grads, delta, new_m, new_v):
        out["w_in"] = _shard_view("w_in", out["w_in"])

    loss = reduced["loss"].reshape(())
    return (loss, grad_x.reshape(x_shape), *[grads[n] for n in names], *[delta[n] for n in names],
            *[new_m[n] for n in names], *[new_v[n] for n in names])
```
